```python
import jax, jax.numpy as jnp
from jax import lax
import numpy as np

D_MODEL = 1024
BATCH = 8
SEQ = 4096
DEPTH = 1

MLA_HEADS = 8
QK_NOPE_DIM = 64
QK_ROPE_DIM = 32
QK_HEAD_DIM = QK_NOPE_DIM + QK_ROPE_DIM
V_HEAD_DIM = 64
Q_LORA_RANK = 384
KV_LORA_RANK = 256
ROPE_BASE = 10000.0
Q_BLOCK = 128
HG_HEADS = 4
HG_KEY_DIM = 128
HG_VAL_DIM = 128
HG_WIDTH_K = HG_HEADS * HG_KEY_DIM
HG_WIDTH_V = HG_HEADS * HG_VAL_DIM
HG_CHUNK = 64
N_BRANCH = 2
BRANCH_WIDTH = MLA_HEADS * V_HEAD_DIM
FFN_HIDDEN = ((8 * D_MODEL // 3 + 255) // 256) * 256
PLE_DIM = 256
EPS = 1e-6

COL_SIZES = (Q_LORA_RANK, KV_LORA_RANK, QK_ROPE_DIM,
             HG_WIDTH_K, HG_WIDTH_K, HG_WIDTH_V, HG_WIDTH_V,
             N_BRANCH * D_MODEL)
IN_COLS = sum(COL_SIZES)

kernel_name = "hybrid_mla_hgrn2_gated_block"


def rms_norm(x, gain):
    xf = x.astype(jnp.float32)
    y = xf * lax.rsqrt(jnp.mean(xf * xf, axis=-1, keepdims=True) + EPS)
    return (y * gain.astype(jnp.float32)).astype(x.dtype)


def apply_rope(x, positions):
    r = x.shape[-1]
    half = r // 2
    inv_freq = jnp.exp(-jnp.log(ROPE_BASE) * jnp.arange(half, dtype=jnp.float32) * 2.0 / r)
    ang = positions.astype(jnp.float32)[..., None] * inv_freq
    cos = jnp.cos(ang)[:, :, None, :]
    sin = jnp.sin(ang)[:, :, None, :]
    xf = x.astype(jnp.float32)
    x1, x2 = xf[..., :half], xf[..., half:]
    out = jnp.concatenate([x1 * cos - x2 * sin, x2 * cos + x1 * sin], axis=-1)
    return out.astype(x.dtype)


def causal_block_attention(q, k, v):
    b, s, h, d = q.shape
    nb = s // Q_BLOCK
    qb = q.reshape(b, nb, Q_BLOCK, h, d).transpose(1, 0, 2, 3, 4)
    kpos = jnp.arange(s)
    scale = d ** -0.5

    def one_block(args):
        qi, bi = args
        sc = jnp.einsum('bqhd,bkhd->bhqk', qi, k, preferred_element_type=jnp.float32) * scale
        qpos = bi * Q_BLOCK + jnp.arange(Q_BLOCK)
        mask = kpos[None, :] <= qpos[:, None]
        sc = jnp.where(mask, sc, -jnp.inf)
        pr = jax.nn.softmax(sc, axis=-1).astype(v.dtype)
        return jnp.einsum('bhqk,bkhd->bqhd', pr, v)

    out = lax.map(one_block, (qb, jnp.arange(nb)))
    return out.transpose(1, 0, 2, 3, 4).reshape(b, s, h, v.shape[-1])


def hgrn2_chunked(q, k, v, log_f):
    b, s, h, kd = q.shape
    vd = v.shape[-1]
    nc = s // HG_CHUNK

    def to_chunks(t):
        return t.reshape(b, nc, HG_CHUNK, h, t.shape[-1]).transpose(1, 0, 3, 2, 4)

    qc, kc, vc, gc = to_chunks(q), to_chunks(k), to_chunks(v), to_chunks(log_f)
    causal = jnp.tril(jnp.ones((HG_CHUNK, HG_CHUNK), dtype=bool))[:, :, None]

    def step(state, inp):
        qi, ki, vi, gi = inp
        cum = jnp.cumsum(gi, axis=2)
        o_inter = jnp.einsum('bhck,bhkv->bhcv', qi * jnp.exp(cum), state)
        diff = cum[:, :, :, None, :] - cum[:, :, None, :, :]
        decay = jnp.exp(jnp.where(causal, diff, -jnp.inf))
        att = jnp.einsum('bhtk,bhsk,bhtsk->bhts', qi, ki, decay)
        o_intra = jnp.einsum('bhts,bhsv->bhtv', att, vi)
        last = cum[:, :, -1:, :]
        new_state = (jnp.exp(last[:, :, 0, :])[..., None] * state
                     + jnp.einsum('bhck,bhcv->bhkv', ki * jnp.exp(last - cum), vi))
        return new_state, o_inter + o_intra

    s0 = jnp.zeros((b, h, kd, vd), jnp.float32)
    _, o = lax.scan(step, s0, (qc, kc, vc, gc))
    return o.transpose(1, 0, 3, 2, 4).reshape(b, s, h, vd)


def _fwd_setup_inputs(seed: int = 0) -> dict:
    key = jax.random.key(seed)
    ks = jax.random.split(key, 24)

    def w(k, shape, fan_in):
        return jax.random.normal(k, shape, jnp.float32) * (fan_in ** -0.5)

    def gain(k, shape):
        return 1.0 + 0.02 * jax.random.normal(k, shape, jnp.float32)

    x = jax.random.normal(ks[0], (BATCH, SEQ, D_MODEL), jnp.float32)
    p = jax.random.normal(ks[1], (DEPTH, BATCH, SEQ, PLE_DIM), jnp.float32)
    offsets = jax.random.randint(ks[2], (BATCH, 1), 0, 1024, dtype=jnp.int32)
    positions = offsets + jnp.arange(SEQ, dtype=jnp.int32)[None, :]
    return {
        "x": x,
        "p": p,
        "positions": positions,
        "mix_norm_g": gain(ks[3], (DEPTH, D_MODEL)),
        "w_in": w(ks[4], (DEPTH, D_MODEL, IN_COLS), D_MODEL),
        "q_a_norm_g": gain(ks[5], (DEPTH, Q_LORA_RANK)),
        "w_uq": w(ks[6], (DEPTH, Q_LORA_RANK, MLA_HEADS * QK_HEAD_DIM), Q_LORA_RANK),
        "kv_a_norm_g": gain(ks[7], (DEPTH, KV_LORA_RANK)),
        "w_ukv": w(ks[8], (DEPTH, KV_LORA_RANK, MLA_HEADS * (QK_NOPE_DIM + V_HEAD_DIM)), KV_LORA_RANK),
        "q_norm_g": gain(ks[9], (DEPTH, QK_HEAD_DIM)),
        "k_norm_g": gain(ks[10], (DEPTH, QK_HEAD_DIM)),
        "hg_lb_logits": 0.5 * jax.random.normal(ks[11], (DEPTH + 1, HG_WIDTH_K), jnp.float32),
        "hg_out_norm_g": gain(ks[12], (DEPTH, HG_VAL_DIM)),
        "w_branch": w(ks[13], (DEPTH, N_BRANCH, BRANCH_WIDTH, D_MODEL), BRANCH_WIDTH),
        "w_out": w(ks[14], (DEPTH, D_MODEL, D_MODEL), D_MODEL),
        "ffn_norm_g": gain(ks[15], (DEPTH, D_MODEL)),
        "w_ffn_gate": w(ks[16], (DEPTH, D_MODEL, FFN_HIDDEN), D_MODEL),
        "w_ffn_up": w(ks[17], (DEPTH, D_MODEL, FFN_HIDDEN), D_MODEL),
        "w_ffn_down": w(ks[18], (DEPTH, FFN_HIDDEN, D_MODEL), FFN_HIDDEN),
        "ple_gate_norm_g": gain(ks[19], (DEPTH, D_MODEL)),
        "w_ple_gate": w(ks[20], (DEPTH, D_MODEL, D_MODEL), D_MODEL),
        "w_ple_proj": w(ks[21], (DEPTH, PLE_DIM, D_MODEL), PLE_DIM),
        "ple_post_norm_g": gain(ks[22], (DEPTH, D_MODEL)),
    }


def _fwd_reference(x, p, positions, mix_norm_g, w_in, q_a_norm_g, w_uq, kv_a_norm_g, w_ukv,
              q_norm_g, k_norm_g, hg_lb_logits, hg_out_norm_g, w_branch, w_out,
              ffn_norm_g, w_ffn_gate, w_ffn_up, w_ffn_down,
              ple_gate_norm_g, w_ple_gate, w_ple_proj, ple_post_norm_g):
    b, s, _ = x.shape
    split_points = np.cumsum(COL_SIZES)[:-1].tolist()
    lower_bounds = jnp.cumsum(jax.nn.softmax(hg_lb_logits.astype(jnp.float32), axis=0), axis=0)

    for layer in range(DEPTH):
        h = rms_norm(x, mix_norm_g[layer])
        proj = h @ w_in[layer]
        c_q, c_kv, k_rope_raw, hq, hf, hi, hg, br_gates = jnp.split(proj, split_points, axis=-1)

        q = (rms_norm(c_q, q_a_norm_g[layer]) @ w_uq[layer]).reshape(b, s, MLA_HEADS, QK_HEAD_DIM)
        kv = (rms_norm(c_kv, kv_a_norm_g[layer]) @ w_ukv[layer]).reshape(
            b, s, MLA_HEADS, QK_NOPE_DIM + V_HEAD_DIM)
        k_nope, v = kv[..., :QK_NOPE_DIM], kv[..., QK_NOPE_DIM:]
        k_rope = jnp.broadcast_to(k_rope_raw[:, :, None, :], (b, s, MLA_HEADS, QK_ROPE_DIM))
        k = jnp.concatenate([k_nope, k_rope], axis=-1)
        q = rms_norm(q, q_norm_g[layer])
        k = rms_norm(k, k_norm_g[layer])
        q = jnp.concatenate([q[..., :QK_NOPE_DIM], apply_rope(q[..., QK_NOPE_DIM:], positions)], axis=-1)
        k = jnp.concatenate([k[..., :QK_NOPE_DIM], apply_rope(k[..., QK_NOPE_DIM:], positions)], axis=-1)
        attn = causal_block_attention(q, k, v).reshape(b, s, BRANCH_WIDTH)

        lb = lower_bounds[layer]
        f = lb + (1.0 - lb) * jax.nn.sigmoid(hf.astype(jnp.float32))
        log_f = jnp.log(f)
        hk = 1.0 - f
        o = hgrn2_chunked(
            hq.astype(jnp.float32).reshape(b, s, HG_HEADS, HG_KEY_DIM),
            hk.reshape(b, s, HG_HEADS, HG_KEY_DIM),
            hi.astype(jnp.float32).reshape(b, s, HG_HEADS, HG_VAL_DIM),
            log_f.reshape(b, s, HG_HEADS, HG_KEY_DIM))
        o = rms_norm(o, hg_out_norm_g[layer]) * jax.nn.silu(
            hg.astype(jnp.float32).reshape(b, s, HG_HEADS, HG_VAL_DIM))
        rec = o.reshape(b, s, HG_WIDTH_V).astype(x.dtype)

        branches = jnp.stack([attn, rec], axis=2)
        y = jnp.einsum('bsgc,gcd->bsgd', branches, w_branch[layer])
        gates = jax.nn.sigmoid(br_gates.reshape(b, s, N_BRANCH, D_MODEL))
        x = x + jnp.sum(gates * y, axis=2) @ w_out[layer]

        h2 = rms_norm(x, ffn_norm_g[layer])
        x = x + (jax.nn.silu(h2 @ w_ffn_gate[layer]) * (h2 @ w_ffn_up[layer])) @ w_ffn_down[layer]

        e = rms_norm(p[layer] @ w_ple_proj[layer], ple_post_norm_g[layer])
        g = jax.nn.sigmoid(rms_norm(x, ple_gate_norm_g[layer]) @ w_ple_gate[layer])
        x = x + g * e
    return x


import jax as _jax
import jax.numpy as _jnp

TWIN_FORMAT = 'train_step'
FWD_PARAMS = ['x', 'p', 'positions', 'mix_norm_g', 'w_in', 'q_a_norm_g', 'w_uq', 'kv_a_norm_g', 'w_ukv', 'q_norm_g', 'k_norm_g', 'hg_lb_logits', 'hg_out_norm_g', 'w_branch', 'w_out', 'ffn_norm_g', 'w_ffn_gate', 'w_ffn_up', 'w_ffn_down', 'ple_gate_norm_g', 'w_ple_gate', 'w_ple_proj', 'ple_post_norm_g']
TWIN_WEIGHTS = ['mix_norm_g', 'w_in', 'q_a_norm_g', 'w_uq', 'kv_a_norm_g', 'w_ukv', 'q_norm_g', 'k_norm_g', 'hg_lb_logits', 'hg_out_norm_g', 'w_branch', 'w_out', 'ffn_norm_g', 'w_ffn_gate', 'w_ffn_up', 'w_ffn_down', 'ple_gate_norm_g', 'w_ple_gate', 'w_ple_proj', 'ple_post_norm_g']
TWIN_DIFF_INPUT = 'x'
TWIN_INPUTS = ['x', 'p', 'positions', 'mix_norm_g', 'w_in', 'q_a_norm_g', 'w_uq', 'kv_a_norm_g', 'w_ukv', 'q_norm_g', 'k_norm_g', 'hg_lb_logits', 'hg_out_norm_g', 'w_branch', 'w_out', 'ffn_norm_g', 'w_ffn_gate', 'w_ffn_up', 'w_ffn_down', 'ple_gate_norm_g', 'w_ple_gate', 'w_ple_proj', 'ple_post_norm_g', 'loss_target', 'm_mix_norm_g', 'm_w_in', 'm_q_a_norm_g', 'm_w_uq', 'm_kv_a_norm_g', 'm_w_ukv', 'm_q_norm_g', 'm_k_norm_g', 'm_hg_lb_logits', 'm_hg_out_norm_g', 'm_w_branch', 'm_w_out', 'm_ffn_norm_g', 'm_w_ffn_gate', 'm_w_ffn_up', 'm_w_ffn_down', 'm_ple_gate_norm_g', 'm_w_ple_gate', 'm_w_ple_proj', 'm_ple_post_norm_g', 'v_mix_norm_g', 'v_w_in', 'v_q_a_norm_g', 'v_w_uq', 'v_kv_a_norm_g', 'v_w_ukv', 'v_q_norm_g', 'v_k_norm_g', 'v_hg_lb_logits', 'v_hg_out_norm_g', 'v_w_branch', 'v_w_out', 'v_ffn_norm_g', 'v_w_ffn_gate', 'v_w_ffn_up', 'v_w_ffn_down', 'v_ple_gate_norm_g', 'v_w_ple_gate', 'v_w_ple_proj', 'v_ple_post_norm_g']
TWIN_OUTPUTS = ['loss', 'grad_x', 'grad_mix_norm_g', 'grad_w_in', 'grad_q_a_norm_g', 'grad_w_uq', 'grad_kv_a_norm_g', 'grad_w_ukv', 'grad_q_norm_g', 'grad_k_norm_g', 'grad_hg_lb_logits', 'grad_hg_out_norm_g', 'grad_w_branch', 'grad_w_out', 'grad_ffn_norm_g', 'grad_w_ffn_gate', 'grad_w_ffn_up', 'grad_w_ffn_down', 'grad_ple_gate_norm_g', 'grad_w_ple_gate', 'grad_w_ple_proj', 'grad_ple_post_norm_g', 'delta_mix_norm_g', 'delta_w_in', 'delta_q_a_norm_g', 'delta_w_uq', 'delta_kv_a_norm_g', 'delta_w_ukv', 'delta_q_norm_g', 'delta_k_norm_g', 'delta_hg_lb_logits', 'delta_hg_out_norm_g', 'delta_w_branch', 'delta_w_out', 'delta_ffn_norm_g', 'delta_w_ffn_gate', 'delta_w_ffn_up', 'delta_w_ffn_down', 'delta_ple_gate_norm_g', 'delta_w_ple_gate', 'delta_w_ple_proj', 'delta_ple_post_norm_g', 'new_m_mix_norm_g', 'new_m_w_in', 'new_m_q_a_norm_g', 'new_m_w_uq', 'new_m_kv_a_norm_g', 'new_m_w_ukv', 'new_m_q_norm_g', 'new_m_k_norm_g', 'new_m_hg_lb_logits', 'new_m_hg_out_norm_g', 'new_m_w_branch', 'new_m_w_out', 'new_m_ffn_norm_g', 'new_m_w_ffn_gate', 'new_m_w_ffn_up', 'new_m_w_ffn_down', 'new_m_ple_gate_norm_g', 'new_m_w_ple_gate', 'new_m_w_ple_proj', 'new_m_ple_post_norm_g', 'new_v_mix_norm_g', 'new_v_w_in', 'new_v_q_a_norm_g', 'new_v_w_uq', 'new_v_kv_a_norm_g', 'new_v_w_ukv', 'new_v_q_norm_g', 'new_v_k_norm_g', 'new_v_hg_lb_logits', 'new_v_hg_out_norm_g', 'new_v_w_branch', 'new_v_w_out', 'new_v_ffn_norm_g', 'new_v_w_ffn_gate', 'new_v_w_ffn_up', 'new_v_w_ffn_down', 'new_v_ple_gate_norm_g', 'new_v_w_ple_gate', 'new_v_w_ple_proj', 'new_v_ple_post_norm_g']
TWIN_LEAF_KINDS = {'loss': 'loss', 'grad_x': 'grad_x', 'grad_mix_norm_g': 'grad_w', 'grad_w_in': 'grad_w', 'grad_q_a_norm_g': 'grad_w', 'grad_w_uq': 'grad_w', 'grad_kv_a_norm_g': 'grad_w', 'grad_w_ukv': 'grad_w', 'grad_q_norm_g': 'grad_w', 'grad_k_norm_g': 'grad_w', 'grad_hg_lb_logits': 'grad_w', 'grad_hg_out_norm_g': 'grad_w', 'grad_w_branch': 'grad_w', 'grad_w_out': 'grad_w', 'grad_ffn_norm_g': 'grad_w', 'grad_w_ffn_gate': 'grad_w', 'grad_w_ffn_up': 'grad_w', 'grad_w_ffn_down': 'grad_w', 'grad_ple_gate_norm_g': 'grad_w', 'grad_w_ple_gate': 'grad_w', 'grad_w_ple_proj': 'grad_w', 'grad_ple_post_norm_g': 'grad_w', 'delta_mix_norm_g': 'delta_w', 'delta_w_in': 'delta_w', 'delta_q_a_norm_g': 'delta_w', 'delta_w_uq': 'delta_w', 'delta_kv_a_norm_g': 'delta_w', 'delta_w_ukv': 'delta_w', 'delta_q_norm_g': 'delta_w', 'delta_k_norm_g': 'delta_w', 'delta_hg_lb_logits': 'delta_w', 'delta_hg_out_norm_g': 'delta_w', 'delta_w_branch': 'delta_w', 'delta_w_out': 'delta_w', 'delta_ffn_norm_g': 'delta_w', 'delta_w_ffn_gate': 'delta_w', 'delta_w_ffn_up': 'delta_w', 'delta_w_ffn_down': 'delta_w', 'delta_ple_gate_norm_g': 'delta_w', 'delta_w_ple_gate': 'delta_w', 'delta_w_ple_proj': 'delta_w', 'delta_ple_post_norm_g': 'delta_w', 'new_m_mix_norm_g': 'new_m', 'new_m_w_in': 'new_m', 'new_m_q_a_norm_g': 'new_m', 'new_m_w_uq': 'new_m', 'new_m_kv_a_norm_g': 'new_m', 'new_m_w_ukv': 'new_m', 'new_m_q_norm_g': 'new_m', 'new_m_k_norm_g': 'new_m', 'new_m_hg_lb_logits': 'new_m', 'new_m_hg_out_norm_g': 'new_m', 'new_m_w_branch': 'new_m', 'new_m_w_out': 'new_m', 'new_m_ffn_norm_g': 'new_m', 'new_m_w_ffn_gate': 'new_m', 'new_m_w_ffn_up': 'new_m', 'new_m_w_ffn_down': 'new_m', 'new_m_ple_gate_norm_g': 'new_m', 'new_m_w_ple_gate': 'new_m', 'new_m_w_ple_proj': 'new_m', 'new_m_ple_post_norm_g': 'new_m', 'new_v_mix_norm_g': 'new_v', 'new_v_w_in': 'new_v', 'new_v_q_a_norm_g': 'new_v', 'new_v_w_uq': 'new_v', 'new_v_kv_a_norm_g': 'new_v', 'new_v_w_ukv': 'new_v', 'new_v_q_norm_g': 'new_v', 'new_v_k_norm_g': 'new_v', 'new_v_hg_lb_logits': 'new_v', 'new_v_hg_out_norm_g': 'new_v', 'new_v_w_branch': 'new_v', 'new_v_w_out': 'new_v', 'new_v_ffn_norm_g': 'new_v', 'new_v_w_ffn_gate': 'new_v', 'new_v_w_ffn_up': 'new_v', 'new_v_w_ffn_down': 'new_v', 'new_v_ple_gate_norm_g': 'new_v', 'new_v_w_ple_gate': 'new_v', 'new_v_w_ple_proj': 'new_v', 'new_v_ple_post_norm_g': 'new_v'}


def _forward(args):
    return _fwd_reference(*[args[k] for k in FWD_PARAMS])


def _output_shape():
    def fwd():
        inp = _fwd_setup_inputs(0)
        return _fwd_reference(*[inp[k] for k in FWD_PARAMS])
    out = _jax.eval_shape(fwd)
    return out.shape, out.dtype

N_MICROBATCH = 1
ADAM_LR = 0.001
ADAM_B1 = 0.9
ADAM_B2 = 0.999
ADAM_EPS = 1e-08
ADAM_WD = 0.01
ADAM_STEP = 10
PER_EXAMPLE_BATCH_AXIS = {'x': 0, 'p': 1, 'positions': 0, 'loss_target': 0}
SHARED_INPUTS = []
_WEIGHT_DTYPES = {'mix_norm_g': _jnp.float32, 'w_in': _jnp.float32, 'q_a_norm_g': _jnp.float32, 'w_uq': _jnp.float32, 'kv_a_norm_g': _jnp.float32, 'w_ukv': _jnp.float32, 'q_norm_g': _jnp.float32, 'k_norm_g': _jnp.float32, 'hg_lb_logits': _jnp.float32, 'hg_out_norm_g': _jnp.float32, 'w_branch': _jnp.float32, 'w_out': _jnp.float32, 'ffn_norm_g': _jnp.float32, 'w_ffn_gate': _jnp.float32, 'w_ffn_up': _jnp.float32, 'w_ffn_down': _jnp.float32, 'ple_gate_norm_g': _jnp.float32, 'w_ple_gate': _jnp.float32, 'w_ple_proj': _jnp.float32, 'ple_post_norm_g': _jnp.float32}
MOMENT_SCALE = {'mix_norm_g': 4.173200e+00, 'w_in': 1.613114e-01, 'q_a_norm_g': 7.075969e-02, 'w_uq': 4.972560e-02, 'kv_a_norm_g': 4.772755e-01, 'w_ukv': 7.116162e-02, 'q_norm_g': 6.742113e-01, 'k_norm_g': 6.759980e-01, 'hg_lb_logits': 1.264775e-01, 'hg_out_norm_g': 2.675229e+01, 'w_branch': 1.380458e-01, 'w_out': 1.828802e-01, 'ffn_norm_g': 2.473266e+01, 'w_ffn_gate': 1.699086e-01, 'w_ffn_up': 1.843927e-01, 'w_ffn_down': 2.900284e-01, 'ple_gate_norm_g': 9.697490e-01, 'w_ple_gate': 7.730322e-02, 'w_ple_proj': 1.559991e-01, 'ple_post_norm_g': 9.407409e+00}


def _to_microbatches(a, axis):
    t = _jnp.moveaxis(a, axis, 0)
    t = t.reshape((N_MICROBATCH, t.shape[0] // N_MICROBATCH) + t.shape[1:])
    return _jnp.moveaxis(t, 1, axis + 1)


def setup_inputs(seed: int = 0) -> dict:
    inp = _fwd_setup_inputs(seed)
    key = _jax.random.fold_in(_jax.random.key(seed), 7919)
    shape, _ = _output_shape()
    out = dict(inp)
    out["loss_target"] = _jax.random.normal(_jax.random.fold_in(key, 0), shape, _jnp.float32)
    for i, name in enumerate(TWIN_WEIGHTS):
        w = inp[name].astype(_jnp.float32)
        if MOMENT_SCALE is None:
            s = _jnp.sqrt(_jnp.mean(_jnp.square(w)) + 1e-30)
        else:
            s = MOMENT_SCALE[name]
        km, kv = _jax.random.split(_jax.random.fold_in(key, i + 1))
        out[name] = w
        out["m_" + name] = s * _jax.random.normal(km, w.shape, _jnp.float32)
        out["v_" + name] = (s * s) * _jax.random.uniform(kv, w.shape, _jnp.float32, 0.5, 1.5)
    if N_MICROBATCH > 1:
        for name, axis in PER_EXAMPLE_BATCH_AXIS.items():
            out[name] = _to_microbatches(out[name], axis)
    return {'x': out['x'], 'p': out['p'], 'positions': out['positions'], 'mix_norm_g': out['mix_norm_g'], 'w_in': out['w_in'], 'q_a_norm_g': out['q_a_norm_g'], 'w_uq': out['w_uq'], 'kv_a_norm_g': out['kv_a_norm_g'], 'w_ukv': out['w_ukv'], 'q_norm_g': out['q_norm_g'], 'k_norm_g': out['k_norm_g'], 'hg_lb_logits': out['hg_lb_logits'], 'hg_out_norm_g': out['hg_out_norm_g'], 'w_branch': out['w_branch'], 'w_out': out['w_out'], 'ffn_norm_g': out['ffn_norm_g'], 'w_ffn_gate': out['w_ffn_gate'], 'w_ffn_up': out['w_ffn_up'], 'w_ffn_down': out['w_ffn_down'], 'ple_gate_norm_g': out['ple_gate_norm_g'], 'w_ple_gate': out['w_ple_gate'], 'w_ple_proj': out['w_ple_proj'], 'ple_post_norm_g': out['ple_post_norm_g'], 'loss_target': out['loss_target'], 'm_mix_norm_g': out['m_mix_norm_g'], 'm_w_in': out['m_w_in'], 'm_q_a_norm_g': out['m_q_a_norm_g'], 'm_w_uq': out['m_w_uq'], 'm_kv_a_norm_g': out['m_kv_a_norm_g'], 'm_w_ukv': out['m_w_ukv'], 'm_q_norm_g': out['m_q_norm_g'], 'm_k_norm_g': out['m_k_norm_g'], 'm_hg_lb_logits': out['m_hg_lb_logits'], 'm_hg_out_norm_g': out['m_hg_out_norm_g'], 'm_w_branch': out['m_w_branch'], 'm_w_out': out['m_w_out'], 'm_ffn_norm_g': out['m_ffn_norm_g'], 'm_w_ffn_gate': out['m_w_ffn_gate'], 'm_w_ffn_up': out['m_w_ffn_up'], 'm_w_ffn_down': out['m_w_ffn_down'], 'm_ple_gate_norm_g': out['m_ple_gate_norm_g'], 'm_w_ple_gate': out['m_w_ple_gate'], 'm_w_ple_proj': out['m_w_ple_proj'], 'm_ple_post_norm_g': out['m_ple_post_norm_g'], 'v_mix_norm_g': out['v_mix_norm_g'], 'v_w_in': out['v_w_in'], 'v_q_a_norm_g': out['v_q_a_norm_g'], 'v_w_uq': out['v_w_uq'], 'v_kv_a_norm_g': out['v_kv_a_norm_g'], 'v_w_ukv': out['v_w_ukv'], 'v_q_norm_g': out['v_q_norm_g'], 'v_k_norm_g': out['v_k_norm_g'], 'v_hg_lb_logits': out['v_hg_lb_logits'], 'v_hg_out_norm_g': out['v_hg_out_norm_g'], 'v_w_branch': out['v_w_branch'], 'v_w_out': out['v_w_out'], 'v_ffn_norm_g': out['v_ffn_norm_g'], 'v_w_ffn_gate': out['v_w_ffn_gate'], 'v_w_ffn_up': out['v_w_ffn_up'], 'v_w_ffn_down': out['v_w_ffn_down'], 'v_ple_gate_norm_g': out['v_ple_gate_norm_g'], 'v_w_ple_gate': out['v_w_ple_gate'], 'v_w_ple_proj': out['v_w_ple_proj'], 'v_ple_post_norm_g': out['v_ple_post_norm_g']}


def _loss(weights, diff, rest, loss_target):
    with _jax.named_scope("forward"):
        args = {**rest, TWIN_DIFF_INPUT: diff, **{k: w.astype(_WEIGHT_DTYPES[k]) for k, w in weights.items()}}
        y = _forward(args)
    with _jax.named_scope("loss_head"):
        err = _jnp.square(y.astype(_jnp.float32) - loss_target)
        return 0.5 * _jnp.sum(_jnp.mean(err, axis=-1)) if err.ndim else 0.5 * err


def _adamw(w, g, m, v):
    m = ADAM_B1 * m + (1.0 - ADAM_B1) * g
    v = ADAM_B2 * v + (1.0 - ADAM_B2) * _jnp.square(g)
    m_hat = m / (1.0 - ADAM_B1 ** ADAM_STEP)
    v_hat = v / (1.0 - ADAM_B2 ** ADAM_STEP)
    delta = -ADAM_LR * (m_hat / (_jnp.sqrt(v_hat) + ADAM_EPS) + ADAM_WD * w)
    return delta, m, v


def reference(x, p, positions, mix_norm_g, w_in, q_a_norm_g, w_uq, kv_a_norm_g, w_ukv, q_norm_g, k_norm_g, hg_lb_logits, hg_out_norm_g, w_branch, w_out, ffn_norm_g, w_ffn_gate, w_ffn_up, w_ffn_down, ple_gate_norm_g, w_ple_gate, w_ple_proj, ple_post_norm_g, loss_target, m_mix_norm_g, m_w_in, m_q_a_norm_g, m_w_uq, m_kv_a_norm_g, m_w_ukv, m_q_norm_g, m_k_norm_g, m_hg_lb_logits, m_hg_out_norm_g, m_w_branch, m_w_out, m_ffn_norm_g, m_w_ffn_gate, m_w_ffn_up, m_w_ffn_down, m_ple_gate_norm_g, m_w_ple_gate, m_w_ple_proj, m_ple_post_norm_g, v_mix_norm_g, v_w_in, v_q_a_norm_g, v_w_uq, v_kv_a_norm_g, v_w_ukv, v_q_norm_g, v_k_norm_g, v_hg_lb_logits, v_hg_out_norm_g, v_w_branch, v_w_out, v_ffn_norm_g, v_w_ffn_gate, v_w_ffn_up, v_w_ffn_down, v_ple_gate_norm_g, v_w_ple_gate, v_w_ple_proj, v_ple_post_norm_g):
    given = dict(x=x, p=p, positions=positions, mix_norm_g=mix_norm_g, w_in=w_in, q_a_norm_g=q_a_norm_g, w_uq=w_uq, kv_a_norm_g=kv_a_norm_g, w_ukv=w_ukv, q_norm_g=q_norm_g, k_norm_g=k_norm_g, hg_lb_logits=hg_lb_logits, hg_out_norm_g=hg_out_norm_g, w_branch=w_branch, w_out=w_out, ffn_norm_g=ffn_norm_g, w_ffn_gate=w_ffn_gate, w_ffn_up=w_ffn_up, w_ffn_down=w_ffn_down, ple_gate_norm_g=ple_gate_norm_g, w_ple_gate=w_ple_gate, w_ple_proj=w_ple_proj, ple_post_norm_g=ple_post_norm_g, loss_target=loss_target, m_mix_norm_g=m_mix_norm_g, m_w_in=m_w_in, m_q_a_norm_g=m_q_a_norm_g, m_w_uq=m_w_uq, m_kv_a_norm_g=m_kv_a_norm_g, m_w_ukv=m_w_ukv, m_q_norm_g=m_q_norm_g, m_k_norm_g=m_k_norm_g, m_hg_lb_logits=m_hg_lb_logits, m_hg_out_norm_g=m_hg_out_norm_g, m_w_branch=m_w_branch, m_w_out=m_w_out, m_ffn_norm_g=m_ffn_norm_g, m_w_ffn_gate=m_w_ffn_gate, m_w_ffn_up=m_w_ffn_up, m_w_ffn_down=m_w_ffn_down, m_ple_gate_norm_g=m_ple_gate_norm_g, m_w_ple_gate=m_w_ple_gate, m_w_ple_proj=m_w_ple_proj, m_ple_post_norm_g=m_ple_post_norm_g, v_mix_norm_g=v_mix_norm_g, v_w_in=v_w_in, v_q_a_norm_g=v_q_a_norm_g, v_w_uq=v_w_uq, v_kv_a_norm_g=v_kv_a_norm_g, v_w_ukv=v_w_ukv, v_q_norm_g=v_q_norm_g, v_k_norm_g=v_k_norm_g, v_hg_lb_logits=v_hg_lb_logits, v_hg_out_norm_g=v_hg_out_norm_g, v_w_branch=v_w_branch, v_w_out=v_w_out, v_ffn_norm_g=v_ffn_norm_g, v_w_ffn_gate=v_w_ffn_gate, v_w_ffn_up=v_w_ffn_up, v_w_ffn_down=v_w_ffn_down, v_ple_gate_norm_g=v_ple_gate_norm_g, v_w_ple_gate=v_w_ple_gate, v_w_ple_proj=v_w_ple_proj, v_ple_post_norm_g=v_ple_post_norm_g)
    weights = {n: given[n] for n in TWIN_WEIGHTS}
    shared = {n: given[n] for n in SHARED_INPUTS}
    per_example = {n: given[n] for n in ['x', 'p', 'positions']}
    grad_fn = _jax.value_and_grad(_loss, argnums=(0, 1))

    def one_microbatch(ex, loss_target):
        ex = dict(ex)
        diff = ex.pop(TWIN_DIFF_INPUT)
        return grad_fn(weights, diff, {**shared, **ex}, loss_target)

    if N_MICROBATCH == 1:
        loss, (grad_w, grad_x) = one_microbatch(per_example, given["loss_target"])
    else:
        def body(carry, xs):
            loss_sum, grad_sum = carry
            l_k, (gw_k, gx_k) = one_microbatch(xs[0], xs[1])
            with _jax.named_scope("update"):
                return (loss_sum + l_k, _jax.tree.map(_jnp.add, grad_sum, gw_k)), gx_k

        init = (_jnp.zeros((), _jnp.float32), _jax.tree.map(_jnp.zeros_like, weights))
        (loss, grad_w), grad_x = _jax.lax.scan(body, init, (per_example, given["loss_target"]))
    with _jax.named_scope("update"):
        delta_w, new_m, new_v = {}, {}, {}
        for n in TWIN_WEIGHTS:
            delta_w[n], new_m[n], new_v[n] = _adamw(weights[n], grad_w[n], given["m_" + n], given["v_" + n])
    return (loss, grad_x, *[grad_w[n] for n in TWIN_WEIGHTS], *[delta_w[n] for n in TWIN_WEIGHTS],
            *[new_m[n] for n in TWIN_WEIGHTS], *[new_v[n] for n in TWIN_WEIGHTS])
```

```python
import functools

import jax
import jax.numpy as jnp
import numpy as np
from jax import lax
from jax.experimental import pallas as pl
from jax.experimental.pallas import tpu as pltpu

F32 = jnp.float32
MM = jnp.bfloat16
HI = lax.Precision.HIGHEST
MESH_ID = pl.DeviceIdType.MESH

D_MODEL = 1024
N_DEV = 8
MLA_HEADS = 8
QK_NOPE = 64
QK_ROPE = 32
QK_DIM = 96
V_DIM = 64
HEAD_PAD = 128
Q_RANK = 384
KV_RANK = 256
ROPE_BASE = 10000.0
HG_HEADS = 4
HG_DIM = 128
HG_W = 512
HG_CHUNK = 64
HG_SUB = 16
FFN = 2816
PLE = 256
EPS = 1e-6
ATT_SCALE = QK_DIM ** -0.5
NEG = -1e30

ADAM_LR = 0.001
ADAM_B1 = 0.9
ADAM_B2 = 0.999
ADAM_EPS = 1e-08
ADAM_WD = 0.01
ADAM_STEP = 10

SEC_CQ = (0, 384)
SEC_CKV = (384, 256)
SEC_KR = (640, 128)
SEC_HQ = (768, 512)
SEC_HF = (1280, 512)
SEC_HI = (1792, 512)
SEC_HG = (2304, 512)
SEC_BG = (2816, 2048)
IN_PAD = 4864
SECTIONS = (SEC_CQ, SEC_CKV, SEC_KR, SEC_HQ, SEC_HF, SEC_HI, SEC_HG, SEC_BG)

VMEM_LIMIT = 58 * 1024 * 1024
ROW_TILE = 256
ATT_TILE = 512
HG_BLOCK = 512


def _dot(a, b):
    return jnp.dot(a.astype(MM), b.astype(MM), preferred_element_type=F32)


def _dot_nt(a, b):
    return lax.dot_general(a.astype(MM), b.astype(MM), (((1,), (1,)), ((), ())), preferred_element_type=F32)


def _dot_tn(a, b):
    return lax.dot_general(a.astype(MM), b.astype(MM), (((0,), (0,)), ((), ())), preferred_element_type=F32)


def _dot_hi(a, b):
    return jnp.dot(a, b, preferred_element_type=F32, precision=HI)


def _sigmoid(x):
    return 1.0 / (1.0 + jnp.exp(-x))


def _rms(x, n=None):
    n = x.shape[-1] if n is None else n
    r = lax.rsqrt(jnp.sum(x * x, axis=-1, keepdims=True) * (1.0 / n) + EPS)
    return x * r, r


def _rms_bwd(dxh, xh, r, n=None):
    n = xh.shape[-1] if n is None else n
    return r * (dxh - xh * (jnp.sum(dxh * xh, axis=-1, keepdims=True) * (1.0 / n)))


def _rope_tables(pos, tm):
    lane = lax.broadcasted_iota(jnp.int32, (tm, HEAD_PAD), 1)
    idx = jnp.where(lane < QK_NOPE + QK_ROPE // 2, lane - QK_NOPE, lane - QK_NOPE - QK_ROPE // 2)
    inv = jnp.exp(idx.astype(F32) * (-np.log(ROPE_BASE) * 2.0 / QK_ROPE))
    ang = pos.astype(F32) * inv
    in_rope = (lane >= QK_NOPE) & (lane < QK_DIM)
    first = lane < QK_NOPE + QK_ROPE // 2
    cos_t = jnp.where(in_rope, jnp.cos(ang), 1.0)
    sin_t = jnp.where(in_rope, jnp.where(first, -jnp.sin(ang), jnp.sin(ang)), 0.0)
    return cos_t, sin_t, (first, in_rope)


def _rope_swap(x, halves):
    first, in_rope = halves
    half = QK_ROPE // 2
    return jnp.where(in_rope, jnp.where(first, pltpu.roll(x, HEAD_PAD - half, 1), pltpu.roll(x, half, 1)), 0.0)


def _cparams(sem, vmem=None):
    return pltpu.CompilerParams(dimension_semantics=sem, vmem_limit_bytes=vmem)


def _row_call(name, body, T, tm, row_ins, full_ins, row_outs, acc_outs, vmem=None):
    def kern(*refs):
        body(pl.program_id(0), *refs)

    in_specs = [pl.BlockSpec((tm, a.shape[1]), lambda i: (i, 0)) for a in row_ins]
    in_specs += [pl.BlockSpec(a.shape, lambda i, nd=a.ndim: (0,) * nd, pipeline_mode=pl.Buffered(1)) for a in full_ins]
    out_specs = [pl.BlockSpec((tm, n), lambda i: (i, 0)) for n, _ in row_outs]
    out_specs += [pl.BlockSpec(s, lambda i, nd=len(s): (0,) * nd) for s, _ in acc_outs]
    out_shape = [jax.ShapeDtypeStruct((T, n), dt) for n, dt in row_outs]
    out_shape += [jax.ShapeDtypeStruct(s, dt) for s, dt in acc_outs]
    return pl.pallas_call(
        kern, name=name, grid=(T // tm,), in_specs=in_specs, out_specs=out_specs, out_shape=out_shape,
        compiler_params=_cparams(("arbitrary",), vmem),
    )(*row_ins, *full_ins)


def _acc(ref, i, val):
    @pl.when(i == 0)
    def _():
        ref[...] = val

    @pl.when(i != 0)
    def _():
        ref[...] += val


def _in_proj_fwd(x, g_mix, w_in_pad, T, tm):
    def body(i, x_ref, g_ref, w_ref, h_ref, *outs):
        xh, _ = _rms(x_ref[...])
        h = (xh * g_ref[...]).astype(MM)
        h_ref[...] = h
        for (s, n), o_ref in zip(SECTIONS, outs):
            o_ref[...] = jnp.dot(h, w_ref[:, s:s + n], preferred_element_type=F32)

    row_outs = [(D_MODEL, MM)] + [(n, F32) for _, n in SECTIONS]
    return _row_call("in_proj_fwd", body, T, tm, [x], [g_mix, w_in_pad], row_outs, [], VMEM_LIMIT)


def _mla_heads_fwd(raw, g_pad, cos_t, sin_t, first):
    outs, saved = [], []
    for h in range(MLA_HEADS):
        xh, r = _rms(raw[:, h * HEAD_PAD:(h + 1) * HEAD_PAD], QK_DIM)
        y = xh * g_pad
        outs.append(y * cos_t + _rope_swap(y, first) * sin_t)
        saved.append((xh, r))
    return outs, saved


def _mla_prep_fwd(cq, ckv, kr, pos, g_qa, g_kva, g_qn, g_kn, w_uq, w_uk, w_uv, T, tm):
    def body(i, cq_ref, ckv_ref, kr_ref, pos_ref, gqa_ref, gkva_ref, gqn_ref, gkn_ref, wuq_ref, wuk_ref, wuv_ref,
             q_ref, k_ref, v_ref):
        cos_t, sin_t, first = _rope_tables(pos_ref[...], tm)
        cqn = _rms(cq_ref[...])[0] * gqa_ref[...]
        ckvn = _rms(ckv_ref[...])[0] * gkva_ref[...]
        q_raw = _dot(cqn, wuq_ref[...])
        k_raw = _dot(ckvn, wuk_ref[...]) + jnp.tile(kr_ref[...], (1, MLA_HEADS))
        qs, _ = _mla_heads_fwd(q_raw, gqn_ref[...], cos_t, sin_t, first)
        ks, _ = _mla_heads_fwd(k_raw, gkn_ref[...], cos_t, sin_t, first)
        q_ref[...] = jnp.concatenate(qs, axis=1).astype(MM)
        k_ref[...] = jnp.concatenate(ks, axis=1).astype(MM)
        v_ref[...] = _dot(ckvn, wuv_ref[...]).astype(MM)

    w = MLA_HEADS * HEAD_PAD
    return _row_call("mla_prep_fwd", body, T, tm, [cq, ckv, kr, pos], [g_qa, g_kva, g_qn, g_kn, w_uq, w_uk, w_uv],
                     [(w, MM), (w, MM), (w, MM)], [])


def _flash_fwd(qf, kf, vf, T):
    tq = min(ATT_TILE, T)
    nq = T // tq

    def body(q_ref, k_ref, v_ref, o_ref, lse_ref, m_s, l_s, acc_s):
        qi, ki = pl.program_id(1), pl.program_id(2)

        @pl.when(ki == 0)
        def _():
            m_s[...] = jnp.full_like(m_s, NEG)
            l_s[...] = jnp.zeros_like(l_s)
            acc_s[...] = jnp.zeros_like(acc_s)

        @pl.when(ki <= qi)
        def _():
            s = _dot_nt(q_ref[...], k_ref[...]) * ATT_SCALE
            row = qi * tq + lax.broadcasted_iota(jnp.int32, (tq, tq), 0)
            col = ki * tq + lax.broadcasted_iota(jnp.int32, (tq, tq), 1)
            s = jnp.where(col <= row, s, NEG)
            m_old = m_s[...]
            m_new = jnp.maximum(m_old, jnp.max(s, axis=1, keepdims=True))
            alpha = jnp.exp(m_old - m_new)
            p = jnp.exp(s - m_new)
            l_s[...] = alpha * l_s[...] + jnp.sum(p, axis=1, keepdims=True)
            acc_s[...] = alpha * acc_s[...] + _dot(p, v_ref[...])
            m_s[...] = m_new

        @pl.when(ki == qi)
        def _():
            l = l_s[...]
            o_ref[...] = acc_s[...] / l
            lse_ref[...] = jnp.broadcast_to(m_s[...] + jnp.log(l), (tq, HEAD_PAD))

    q_spec = pl.BlockSpec((tq, HEAD_PAD), lambda h, qi, ki: (qi, h))
    kv_spec = pl.BlockSpec((tq, HEAD_PAD), lambda h, qi, ki: (jnp.minimum(ki, qi), h))
    return pl.pallas_call(
        body, name="flash_fwd", grid=(MLA_HEADS, nq, nq),
        in_specs=[q_spec, kv_spec, kv_spec], out_specs=[q_spec, q_spec],
        out_shape=[jax.ShapeDtypeStruct((T, MLA_HEADS * HEAD_PAD), F32)] * 2,
        scratch_shapes=[pltpu.VMEM((tq, 1), F32), pltpu.VMEM((tq, 1), F32), pltpu.VMEM((tq, HEAD_PAD), F32)],
        compiler_params=_cparams(("parallel", "parallel", "arbitrary")),
    )(qf, kf, vf)


def _hg_gates(hf, lb):
    sg = _sigmoid(hf)
    f = lb + (1.0 - lb) * sg
    return sg, f, jnp.log(f), 1.0 - f


def _tri(n, lower):
    r = lax.broadcasted_iota(jnp.int32, (n, n), 0)
    c = lax.broadcasted_iota(jnp.int32, (n, n), 1)
    return jnp.where((c <= r) if lower else (c >= r), 1.0, 0.0).astype(F32)


def _hg_intra(q, k, b):
    C, S = HG_CHUNK, HG_SUB
    row_c = lax.broadcasted_iota(jnp.int32, (C, HG_DIM), 0)
    row_s = lax.broadcasted_iota(jnp.int32, (S, HG_DIM), 0)
    lane_c = lax.broadcasted_iota(jnp.int32, (S, C), 1)
    blocks, saved = [], []
    for blk in range(C // S):
        lo = blk * S
        q_b, k_b, b_b = q[lo:lo + S], k[lo:lo + S], b[lo:lo + S]
        a_b = jnp.zeros((S, C), F32)
        for j in range(S):
            w = jnp.exp(jnp.where(row_s >= j, b_b - b_b[j:j + 1], NEG))
            col = jnp.sum(q_b * (k_b[j:j + 1] * w), axis=1, keepdims=True)
            a_b = jnp.where(lane_c == lo + j, col, a_b)
        if blk > 0:
            ref = b[lo - 1:lo]
            q_e = jnp.exp(b_b - ref)
            q_t = q_b * q_e
            k_e = jnp.exp(jnp.where(row_c < lo, ref - b, NEG))
            a_b = a_b + _dot_nt(q_t, k * k_e)
            saved.append((q_t, k_e, q_e))
        else:
            saved.append(None)
        blocks.append(a_b)
    return jnp.concatenate(blocks, axis=0), saved


def _hgrn_fwd(hq, hf, hi, lb, T):
    rb = min(HG_BLOCK, T)
    ncb = rb // HG_CHUNK

    def body(hq_ref, hf_ref, hi_ref, lb_ref, o_ref, s0_ref, st_ref):
        @pl.when(pl.program_id(0) == 0)
        def _():
            st_ref[...] = jnp.zeros_like(st_ref)

        tril = _tri(HG_CHUNK, True)

        def chunk(c, carry):
            rows = pl.ds(pl.multiple_of(c * HG_CHUNK, HG_CHUNK), HG_CHUNK)
            _, _, logf, kk = _hg_gates(hf_ref[rows, :], lb_ref[...])
            b = _dot_hi(tril, logf)
            q_all, v_all = hq_ref[rows, :], hi_ref[rows, :]
            outs = []
            for h in range(HG_HEADS):
                ls = slice(h * HG_DIM, (h + 1) * HG_DIM)
                q, k, v, bh = q_all[:, ls], kk[:, ls], v_all[:, ls], b[:, ls]
                st = st_ref[h]
                s0_ref[c, h * HG_DIM:(h + 1) * HG_DIM, :] = st
                b_end = bh[HG_CHUNK - 1:HG_CHUNK]
                a, _ = _hg_intra(q, k, bh)
                outs.append(_dot_nt(q * jnp.exp(bh), st) + _dot(a, v))
                st_ref[h] = st * jnp.exp(b_end) + _dot_tn(v, k * jnp.exp(b_end - bh))
            o_ref[rows, :] = jnp.concatenate(outs, axis=1)
            return carry

        lax.fori_loop(0, ncb, chunk, 0)

    row = pl.BlockSpec((rb, HG_W), lambda i: (i, 0))
    return pl.pallas_call(
        body, name="hgrn_fwd", grid=(T // rb,),
        in_specs=[row, row, row, pl.BlockSpec((1, HG_W), lambda i: (0, 0))],
        out_specs=[row, pl.BlockSpec((ncb, HG_W, HG_DIM), lambda i: (i, 0, 0))],
        out_shape=[jax.ShapeDtypeStruct((T, HG_W), F32), jax.ShapeDtypeStruct((T // HG_CHUNK, HG_W, HG_DIM), F32)],
        scratch_shapes=[pltpu.VMEM((HG_HEADS, HG_DIM, HG_DIM), F32)],
        compiler_params=_cparams(("arbitrary",)),
    )(hq, hf, hi, lb)


def _hgrn_bwd(hq, hf, hi, do, s0, lb, T):
    rb = min(HG_BLOCK, T)
    ncb = rb // HG_CHUNK
    nb = T // rb
    C, S = HG_CHUNK, HG_SUB

    def body(hq_ref, hf_ref, hi_ref, do_ref, s0_ref, lb_ref, dq_ref, df_ref, dv_ref, dlb_ref, dst_ref):
        @pl.when(pl.program_id(0) == 0)
        def _():
            dst_ref[...] = jnp.zeros_like(dst_ref)
            dlb_ref[...] = jnp.zeros_like(dlb_ref)

        tril, triu = _tri(C, True), _tri(C, False)
        row_cc = lax.broadcasted_iota(jnp.int32, (C, C), 0)
        col_cc = lax.broadcasted_iota(jnp.int32, (C, C), 1)
        row_s = lax.broadcasted_iota(jnp.int32, (S, HG_DIM), 0)
        lane_sc = lax.broadcasted_iota(jnp.int32, (S, C), 1)
        last_row = lax.broadcasted_iota(jnp.int32, (C, HG_DIM), 0) == C - 1
        lb_v = lb_ref[...]

        def chunk(cc, carry):
            c = ncb - 1 - cc
            rows = pl.ds(pl.multiple_of(c * C, C), C)
            hf_c = hf_ref[rows, :]
            sg, f, logf, kk = _hg_gates(hf_c, lb_v)
            b = _dot_hi(tril, logf)
            q_all, v_all, do_all = hq_ref[rows, :], hi_ref[rows, :], do_ref[rows, :]
            dq_o, dk_o, dv_o, db_o = [], [], [], []
            for h in range(HG_HEADS):
                ls = slice(h * HG_DIM, (h + 1) * HG_DIM)
                q, k, v, bh, d_o = q_all[:, ls], kk[:, ls], v_all[:, ls], b[:, ls], do_all[:, ls]
                st0 = s0_ref[c, h * HG_DIM:(h + 1) * HG_DIM, :]
                dst = dst_ref[h]
                b_end = bh[C - 1:C]
                e_b, e_end = jnp.exp(bh), jnp.exp(b_end)
                e_rem = jnp.exp(b_end - bh)
                qe, kd = q * e_b, k * e_rem
                st_end = st0 * e_end + _dot_tn(v, kd)
                a, saved = _hg_intra(q, k, bh)
                d_a = jnp.where(col_cc <= row_cc, _dot_nt(d_o, v), 0.0)
                dv = _dot_tn(a, d_o) + _dot_nt(kd, dst)
                dq = e_b * _dot(d_o, st0)
                dk = e_rem * _dot(v, dst)
                dq_blocks, dk_diag = [], []
                for blk in range(C // S):
                    lo = blk * S
                    q_b, k_b, b_b = q[lo:lo + S], k[lo:lo + S], bh[lo:lo + S]
                    da_b = d_a[lo:lo + S]
                    dq_b = jnp.zeros((S, HG_DIM), F32)
                    dk_b = jnp.zeros((S, HG_DIM), F32)
                    for j in range(S):
                        w = jnp.exp(jnp.where(row_s >= j, b_b - b_b[j:j + 1], NEG))
                        col = jnp.sum(jnp.where(lane_sc == lo + j, da_b, 0.0), axis=1, keepdims=True)
                        dq_b = dq_b + col * (k_b[j:j + 1] * w)
                        dk_row = jnp.sum(col * (q_b * w), axis=0, keepdims=True)
                        dk_b = jnp.where(row_s == j, dk_row, dk_b)
                    if blk > 0:
                        q_t, k_e, q_e = saved[blk]
                        da_off = jnp.where(lane_sc < lo, da_b, 0.0)
                        dq_b = dq_b + _dot(da_off, k * k_e) * q_e
                        dk = dk + _dot_tn(da_off, q_t) * k_e
                    dq_blocks.append(dq_b)
                    dk_diag.append(dk_b)
                dq = dq + jnp.concatenate(dq_blocks, axis=0)
                dk = dk + jnp.concatenate(dk_diag, axis=0)
                extra = jnp.sum(dst * st_end, axis=0, keepdims=True)
                db_o.append(q * dq - k * dk + jnp.where(last_row, extra, 0.0))
                dst_ref[h] = dst * e_end + _dot_tn(d_o, qe)
                dq_o.append(dq)
                dk_o.append(dk)
                dv_o.append(dv)
            dlogf = _dot_hi(triu, jnp.concatenate(db_o, axis=1))
            d_f = dlogf / f - jnp.concatenate(dk_o, axis=1)
            dq_ref[rows, :] = jnp.concatenate(dq_o, axis=1)
            dv_ref[rows, :] = jnp.concatenate(dv_o, axis=1)
            df_ref[rows, :] = d_f * (1.0 - lb_v) * sg * (1.0 - sg)
            dlb_ref[...] += jnp.sum(d_f * (1.0 - sg), axis=0, keepdims=True)
            return carry

        lax.fori_loop(0, ncb, chunk, 0)

    row = pl.BlockSpec((rb, HG_W), lambda i: (nb - 1 - i, 0))
    one = pl.BlockSpec((1, HG_W), lambda i: (0, 0))
    return pl.pallas_call(
        body, name="hgrn_bwd", grid=(nb,),
        in_specs=[row, row, row, row, pl.BlockSpec((ncb, HG_W, HG_DIM), lambda i: (nb - 1 - i, 0, 0)), one],
        out_specs=[row, row, row, one],
        out_shape=[jax.ShapeDtypeStruct((T, HG_W), F32)] * 3 + [jax.ShapeDtypeStruct((1, HG_W), F32)],
        scratch_shapes=[pltpu.VMEM((HG_HEADS, HG_DIM, HG_DIM), F32)],
        compiler_params=_cparams(("arbitrary",)),
    )(hq, hf, hi, do, s0, lb)


def _silu_parts(x):
    sg = _sigmoid(x)
    return x * sg, sg * (1.0 + x * (1.0 - sg))


def _merge_fwd(attn, o, hg, bg, x, g_out, w_bra, w_brb, w_out, T, tm):
    def body(i, attn_ref, o_ref, hg_ref, bg_ref, x_ref, g_ref, wa_ref, wb_ref, wo_ref,
             x1_ref, ya_ref, yb_ref, m_ref, rec_ref):
        recs = []
        for h in range(HG_HEADS):
            ls = slice(h * HG_DIM, (h + 1) * HG_DIM)
            oh, _ = _rms(o_ref[:, ls])
            recs.append(oh * g_ref[...] * _silu_parts(hg_ref[:, ls])[0])
        rec = jnp.concatenate(recs, axis=1)
        ya = _dot(attn_ref[...], wa_ref[...])
        yb = _dot(rec, wb_ref[...])
        m = _sigmoid(bg_ref[:, :D_MODEL]) * ya + _sigmoid(bg_ref[:, D_MODEL:]) * yb
        x1_ref[...] = x_ref[...] + _dot(m, wo_ref[...])
        ya_ref[...] = ya
        yb_ref[...] = yb
        m_ref[...] = m.astype(MM)
        rec_ref[...] = rec.astype(MM)

    return _row_call("merge_fwd", body, T, tm, [attn, o, hg, bg, x], [g_out, w_bra, w_brb, w_out],
                     [(D_MODEL, F32), (D_MODEL, F32), (D_MODEL, F32), (D_MODEL, MM), (HG_W, MM)], [], VMEM_LIMIT)


def _ffn_fwd(x1, g_ffn, w_g, w_u, w_d, T, tm):
    def body(i, x1_ref, g_ref, wg_ref, wu_ref, wd_ref, x2_ref, gt_ref, up_ref, h2_ref):
        x1 = x1_ref[...]
        h2 = (_rms(x1)[0] * g_ref[...]).astype(MM)
        gt = jnp.dot(h2, wg_ref[...], preferred_element_type=F32)
        up = jnp.dot(h2, wu_ref[...], preferred_element_type=F32)
        a = _silu_parts(gt)[0] * up
        x2_ref[...] = x1 + _dot(a, wd_ref[...])
        gt_ref[...] = gt
        up_ref[...] = up
        h2_ref[...] = h2

    return _row_call("ffn_fwd", body, T, tm, [x1], [g_ffn, w_g, w_u, w_d],
                     [(D_MODEL, F32), (FFN, F32), (FFN, F32), (D_MODEL, MM)], [], VMEM_LIMIT)


def _ple_loss(x2, p, tgt, g_pg, g_post, w_pg, w_pp, T, tm):
    def body(i, x2_ref, p_ref, t_ref, gpg_ref, gpo_ref, wpg_ref, wpp_ref,
             dx2_ref, n3_ref, dz_ref, du_ref, loss_ref, dgpo_ref, dgpg_ref):
        x2 = x2_ref[...]
        uh, ru = _rms(_dot(p_ref[...], wpp_ref[...]))
        e = uh * gpo_ref[...]
        x2h, r3 = _rms(x2)
        n3 = x2h * gpg_ref[...]
        gate = _sigmoid(_dot(n3, wpg_ref[...]))
        diff = x2 + gate * e - t_ref[...]
        dy = diff * (1.0 / D_MODEL)
        de = dy * gate
        dz = dy * e * gate * (1.0 - gate)
        du = _rms_bwd(de * gpo_ref[...], uh, ru)
        dn3 = _dot_nt(dz, wpg_ref[...])
        dx2_ref[...] = dy + _rms_bwd(dn3 * gpg_ref[...], x2h, r3)
        n3_ref[...] = n3.astype(MM)
        dz_ref[...] = dz.astype(MM)
        du_ref[...] = du.astype(MM)
        _acc(loss_ref, i, jnp.sum(diff * diff, axis=0, keepdims=True) * (0.5 / D_MODEL))
        _acc(dgpo_ref, i, jnp.sum(de * uh, axis=0, keepdims=True))
        _acc(dgpg_ref, i, jnp.sum(dn3 * x2h, axis=0, keepdims=True))

    vec = ((1, D_MODEL), F32)
    return _row_call("ple_loss", body, T, tm, [x2, p, tgt], [g_pg, g_post, w_pg, w_pp],
                     [(D_MODEL, F32), (D_MODEL, MM), (D_MODEL, MM), (D_MODEL, MM)], [vec, vec, vec], VMEM_LIMIT)


def _ffn_bwd(dx2, x1, gt, up, g_ffn, w_g, w_u, w_d, T, tm):
    def body(i, dx2_ref, x1_ref, gt_ref, up_ref, g_ref, wg_ref, wu_ref, wd_ref,
             dx1_ref, a_ref, dgt_ref, dup_ref, dg_ref):
        dx2 = dx2_ref[...]
        x1h, r = _rms(x1_ref[...])
        up = up_ref[...]
        silu, dsilu = _silu_parts(gt_ref[...])
        da = _dot_nt(dx2, wd_ref[...])
        dgt = (da * up * dsilu).astype(MM)
        dup = (da * silu).astype(MM)
        dh2 = (lax.dot_general(dgt, wg_ref[...], (((1,), (1,)), ((), ())), preferred_element_type=F32)
               + lax.dot_general(dup, wu_ref[...], (((1,), (1,)), ((), ())), preferred_element_type=F32))
        dx1_ref[...] = dx2 + _rms_bwd(dh2 * g_ref[...], x1h, r)
        a_ref[...] = (silu * up).astype(MM)
        dgt_ref[...] = dgt
        dup_ref[...] = dup
        _acc(dg_ref, i, jnp.sum(dh2 * x1h, axis=0, keepdims=True))

    return _row_call("ffn_bwd", body, T, tm, [dx2, x1, gt, up], [g_ffn, w_g, w_u, w_d],
                     [(D_MODEL, F32), (FFN, MM), (FFN, MM), (FFN, MM)], [((1, D_MODEL), F32)], VMEM_LIMIT)


def _merge_bwd(dx1, ya, yb, bg, o, hg, g_out, w_bra, w_brb, w_out, T, tm):
    def body(i, dx1_ref, ya_ref, yb_ref, bg_ref, o_ref, hg_ref, g_ref, wa_ref, wb_ref, wo_ref,
             dattn_ref, do_ref, dhg_ref, dbg_ref, dya_ref, dyb_ref, dg_ref):
        dm = _dot_nt(dx1_ref[...], wo_ref[...])
        ga, gb = _sigmoid(bg_ref[:, :D_MODEL]), _sigmoid(bg_ref[:, D_MODEL:])
        dya, dyb = (dm * ga).astype(MM), (dm * gb).astype(MM)
        dbg_ref[:, :D_MODEL] = dm * ya_ref[...] * ga * (1.0 - ga)
        dbg_ref[:, D_MODEL:] = dm * yb_ref[...] * gb * (1.0 - gb)
        dya_ref[...] = dya
        dyb_ref[...] = dyb
        dattn_ref[...] = lax.dot_general(dya, wa_ref[...], (((1,), (1,)), ((), ())), preferred_element_type=F32)
        drec = lax.dot_general(dyb, wb_ref[...], (((1,), (1,)), ((), ())), preferred_element_type=F32)
        dg = jnp.zeros((1, HG_DIM), F32)
        for h in range(HG_HEADS):
            ls = slice(h * HG_DIM, (h + 1) * HG_DIM)
            oh, r = _rms(o_ref[:, ls])
            silu, dsilu = _silu_parts(hg_ref[:, ls])
            dr = drec[:, ls]
            dhg_ref[:, ls] = dr * oh * g_ref[...] * dsilu
            don = dr * silu
            dg = dg + jnp.sum(don * oh, axis=0, keepdims=True)
            do_ref[:, ls] = _rms_bwd(don * g_ref[...], oh, r)
        _acc(dg_ref, i, dg)

    return _row_call("merge_bwd", body, T, tm, [dx1, ya, yb, bg, o, hg], [g_out, w_bra, w_brb, w_out],
                     [(D_MODEL, F32), (HG_W, F32), (HG_W, F32), (2 * D_MODEL, F32), (D_MODEL, MM), (D_MODEL, MM)],
                     [((1, HG_DIM), F32)], VMEM_LIMIT)


def _flash_bwd(qf, kf, vf, o, do, lse, T):
    tq = min(ATT_TILE, T)
    nq = T // tq

    def body(q_ref, k_ref, v_ref, o_ref, do_ref, lse_ref, dq_ref, dk_ref, dv_ref):
        ki, qi = pl.program_id(1), pl.program_id(2)

        @pl.when((ki == 0) & (qi == 0))
        def _():
            dq_ref[...] = jnp.zeros_like(dq_ref)

        @pl.when(qi == 0)
        def _():
            dk_ref[...] = jnp.zeros_like(dk_ref)
            dv_ref[...] = jnp.zeros_like(dv_ref)

        @pl.when(qi >= ki)
        def _():
            q, k, d_o = q_ref[...], k_ref[...], do_ref[...]
            s = _dot_nt(q, k) * ATT_SCALE
            row = qi * tq + lax.broadcasted_iota(jnp.int32, (tq, tq), 0)
            col = ki * tq + lax.broadcasted_iota(jnp.int32, (tq, tq), 1)
            p = jnp.exp(jnp.where(col <= row, s, NEG) - lse_ref[:, :1])
            delta = jnp.sum(d_o * o_ref[...], axis=1, keepdims=True)
            ds = p * (_dot_nt(d_o, v_ref[...]) - delta) * ATT_SCALE
            dv_ref[...] += _dot_tn(p, d_o)
            dk_ref[...] += _dot_tn(ds, q)
            rows = pl.ds(pl.multiple_of(qi * tq, tq), tq)
            dq_ref[rows, :] += _dot(ds, k)

    q_spec = pl.BlockSpec((tq, HEAD_PAD), lambda h, ki, qi: (jnp.maximum(qi, ki), h))
    kv_spec = pl.BlockSpec((tq, HEAD_PAD), lambda h, ki, qi: (ki, h))
    w = MLA_HEADS * HEAD_PAD
    return pl.pallas_call(
        body, name="flash_bwd", grid=(MLA_HEADS, nq, nq),
        in_specs=[q_spec, kv_spec, kv_spec, q_spec, q_spec, q_spec],
        out_specs=[pl.BlockSpec((T, HEAD_PAD), lambda h, ki, qi: (0, h)), kv_spec, kv_spec],
        out_shape=[jax.ShapeDtypeStruct((T, w), F32)] * 3,
        compiler_params=_cparams(("parallel", "arbitrary", "arbitrary")),
    )(qf, kf, vf, o, do, lse)


def _mla_heads_bwd(d_out, saved, g_pad, cos_t, sin_t, first):
    d_raw, dg = [], jnp.zeros((1, HEAD_PAD), F32)
    for h in range(MLA_HEADS):
        xh, r = saved[h]
        dy = d_out[:, h * HEAD_PAD:(h + 1) * HEAD_PAD]
        dn = dy * cos_t + _rope_swap(dy * sin_t, first)
        dg = dg + jnp.sum(dn * xh, axis=0, keepdims=True)
        d_raw.append(_rms_bwd(dn * g_pad, xh, r, QK_DIM))
    return d_raw, dg


def _mla_prep_bwd(cq, ckv, kr, pos, dqf, dkf, dvf, g_qa, g_kva, g_qn, g_kn, w_uq, w_uk, w_uv, T, tm):
    def body(i, cq_ref, ckv_ref, kr_ref, pos_ref, dq_ref, dk_ref, dv_ref,
             gqa_ref, gkva_ref, gqn_ref, gkn_ref, wuq_ref, wuk_ref, wuv_ref,
             dcq_ref, dckv_ref, dkr_ref, dqraw_ref, dkraw_ref, cqn_ref, ckvn_ref,
             dgqa_ref, dgkva_ref, dgqn_ref, dgkn_ref):
        cos_t, sin_t, first = _rope_tables(pos_ref[...], tm)
        cqh, rq = _rms(cq_ref[...])
        ckvh, rkv = _rms(ckv_ref[...])
        cqn, ckvn = cqh * gqa_ref[...], ckvh * gkva_ref[...]
        q_raw = _dot(cqn, wuq_ref[...])
        k_raw = _dot(ckvn, wuk_ref[...]) + jnp.tile(kr_ref[...], (1, MLA_HEADS))
        _, q_saved = _mla_heads_fwd(q_raw, gqn_ref[...], cos_t, sin_t, first)
        _, k_saved = _mla_heads_fwd(k_raw, gkn_ref[...], cos_t, sin_t, first)
        dq_heads, dgqn = _mla_heads_bwd(dq_ref[...], q_saved, gqn_ref[...], cos_t, sin_t, first)
        dk_heads, dgkn = _mla_heads_bwd(dk_ref[...], k_saved, gkn_ref[...], cos_t, sin_t, first)
        dq_raw = jnp.concatenate(dq_heads, axis=1).astype(MM)
        dk_raw = jnp.concatenate(dk_heads, axis=1).astype(MM)
        lane = lax.broadcasted_iota(jnp.int32, (tm, HEAD_PAD), 1)
        dkr = dk_heads[0]
        for h in range(1, MLA_HEADS):
            dkr = dkr + dk_heads[h]
        dkr_ref[...] = jnp.where((lane >= QK_NOPE) & (lane < QK_DIM), dkr, 0.0)
        dcqn = lax.dot_general(dq_raw, wuq_ref[...], (((1,), (1,)), ((), ())), preferred_element_type=F32)
        dckvn = (lax.dot_general(dk_raw, wuk_ref[...], (((1,), (1,)), ((), ())), preferred_element_type=F32)
                 + _dot_nt(dv_ref[...], wuv_ref[...]))
        dcq_ref[...] = _rms_bwd(dcqn * gqa_ref[...], cqh, rq)
        dckv_ref[...] = _rms_bwd(dckvn * gkva_ref[...], ckvh, rkv)
        dqraw_ref[...] = dq_raw
        dkraw_ref[...] = dk_raw
        cqn_ref[...] = cqn.astype(MM)
        ckvn_ref[...] = ckvn.astype(MM)
        _acc(dgqa_ref, i, jnp.sum(dcqn * cqh, axis=0, keepdims=True))
        _acc(dgkva_ref, i, jnp.sum(dckvn * ckvh, axis=0, keepdims=True))
        _acc(dgqn_ref, i, dgqn)
        _acc(dgkn_ref, i, dgkn)

    w = MLA_HEADS * HEAD_PAD
    return _row_call(
        "mla_prep_bwd", body, T, tm, [cq, ckv, kr, pos, dqf, dkf, dvf], [g_qa, g_kva, g_qn, g_kn, w_uq, w_uk, w_uv],
        [(Q_RANK, F32), (KV_RANK, F32), (HEAD_PAD, F32), (w, MM), (w, MM), (Q_RANK, MM), (KV_RANK, MM)],
        [((1, Q_RANK), F32), ((1, KV_RANK), F32), ((1, HEAD_PAD), F32), ((1, HEAD_PAD), F32)], VMEM_LIMIT)


def _in_proj_bwd(x, dx1, dsecs, g_mix, w_in_pad, T, tm):
    def body(i, x_ref, dx1_ref, *rest):
        d_refs, (g_ref, w_ref, dx_ref, dp_ref, dg_ref) = rest[:len(SECTIONS)], rest[len(SECTIONS):]
        dh = jnp.zeros((tm, D_MODEL), F32)
        for (s, n), d_ref in zip(SECTIONS, d_refs):
            d = d_ref[...].astype(MM)
            dp_ref[:, s:s + n] = d
            dh = dh + lax.dot_general(d, w_ref[:, s:s + n], (((1,), (1,)), ((), ())), preferred_element_type=F32)
        xh, r = _rms(x_ref[...])
        dx_ref[...] = dx1_ref[...] + _rms_bwd(dh * g_ref[...], xh, r)
        _acc(dg_ref, i, jnp.sum(dh * xh, axis=0, keepdims=True))

    return _row_call("in_proj_bwd", body, T, tm, [x, dx1, *dsecs], [g_mix, w_in_pad],
                     [(D_MODEL, F32), (IN_PAD, MM)], [((1, D_MODEL), F32)], VMEM_LIMIT)


def _pick_block(n, cap):
    best = None
    for cand in range(128, min(n, cap) + 1, 128):
        if n % cand == 0:
            best = cand
    return n if best is None else best


def _pick_rows(n, cap):
    best = n
    for cand in range(8, min(n, cap) + 1, 8):
        if n % cand == 0:
            best = cand
    return best


def _matmul_tn(name, a, b):
    T, M = a.shape
    N = b.shape[1]
    bm, bn, bk = _pick_block(M, 512), _pick_block(N, 2560), min(512, T)

    def body(a_ref, b_ref, c_ref):
        @pl.when(pl.program_id(2) == 0)
        def _():
            c_ref[...] = jnp.zeros_like(c_ref)

        c_ref[...] += _dot_tn(a_ref[...], b_ref[...])

    return pl.pallas_call(
        body, name=name, grid=(M // bm, N // bn, T // bk),
        in_specs=[pl.BlockSpec((bk, bm), lambda i, j, k: (k, i)), pl.BlockSpec((bk, bn), lambda i, j, k: (k, j))],
        out_specs=pl.BlockSpec((bm, bn), lambda i, j, k: (i, j)),
        out_shape=jax.ShapeDtypeStruct((M, N), F32),
        compiler_params=_cparams(("parallel", "parallel", "arbitrary")),
    )(a, b)


def _pad_heads(w, real):
    r = w.shape[0]
    w = w.reshape(r, -1, real)
    return jnp.pad(w, ((0, 0), (0, 0), (0, HEAD_PAD - real))).reshape(r, -1)


def _unpad_heads(w, real):
    r = w.shape[0]
    return w.reshape(r, -1, HEAD_PAD)[:, :, :real].reshape(r, -1)


def _pad_gain(g, n):
    return jnp.pad(g.reshape(1, -1), ((0, 0), (0, n - g.shape[-1])))


def _local_step(x, p, pos, tgt, small, big):
    T = x.shape[0]
    tm = min(ROW_TILE, T)
    w_in = big["w_in"]
    rope_cols = w_in[:, 640:672]
    zeros = lambda n: jnp.zeros((D_MODEL, n), w_in.dtype)
    w_in_pad = jnp.concatenate([w_in[:, :640], zeros(QK_NOPE), rope_cols, zeros(HEAD_PAD - QK_DIM), w_in[:, 672:]], axis=1)
    w_uq = _pad_heads(big["w_uq"], QK_DIM)
    ukv = big["w_ukv"].reshape(KV_RANK, MLA_HEADS, QK_NOPE + V_DIM)
    w_uk = _pad_heads(ukv[:, :, :QK_NOPE].reshape(KV_RANK, -1), QK_NOPE)
    w_uv = _pad_heads(ukv[:, :, QK_NOPE:].reshape(KV_RANK, -1), V_DIM)
    w_bra = jnp.pad(big["w_branch"][0].reshape(MLA_HEADS, V_DIM, D_MODEL),
                    ((0, 0), (0, HEAD_PAD - V_DIM), (0, 0))).reshape(MLA_HEADS * HEAD_PAD, D_MODEL)
    w_brb = big["w_branch"][1]
    w_out, w_g, w_u, w_d = big["w_out"], big["w_ffn_gate"], big["w_ffn_up"], big["w_ffn_down"]
    w_pg, w_pp = big["w_ple_gate"], big["w_ple_proj"]

    g_mix, g_qa, g_kva = small["mix_norm_g"], small["q_a_norm_g"], small["kv_a_norm_g"]
    g_qn, g_kn = _pad_gain(small["q_norm_g"], HEAD_PAD), _pad_gain(small["k_norm_g"], HEAD_PAD)
    g_out, g_ffn = small["hg_out_norm_g"], small["ffn_norm_g"]
    g_pg, g_post = small["ple_gate_norm_g"], small["ple_post_norm_g"]
    logits = small["hg_lb_logits"]
    lb = _lower_bound(logits)

    h, cq, ckv, kr, hq, hf, hi, hg, bg = _in_proj_fwd(x, g_mix, w_in_pad, T, tm)
    qf, kf, vf = _mla_prep_fwd(cq, ckv, kr, pos, g_qa, g_kva, g_qn, g_kn, w_uq, w_uk, w_uv, T, tm)
    attn, lse = _flash_fwd(qf, kf, vf, T)
    o, s0 = _hgrn_fwd(hq, hf, hi, lb, T)
    x1, ya, yb, m, rec = _merge_fwd(attn, o, hg, bg, x, g_out, w_bra, w_brb, w_out, T, tm)
    x2, gt, up, h2 = _ffn_fwd(x1, g_ffn, w_g, w_u, w_d, T, tm)
    dx2, n3, dz, du, loss_p, dg_post, dg_pg = _ple_loss(x2, p, tgt, g_pg, g_post, w_pg, w_pp, T, tm)

    dx1, a, dgt, dup, dg_ffn = _ffn_bwd(dx2, x1, gt, up, g_ffn, w_g, w_u, w_d, T, tm)
    dattn, do, dhg, dbg, dya, dyb, dg_out = _merge_bwd(dx1, ya, yb, bg, o, hg, g_out, w_bra, w_brb, w_out, T, tm)
    dhq, dhf, dhi, dlb = _hgrn_bwd(hq, hf, hi, do, s0, lb, T)
    dqf, dkf, dvf = _flash_bwd(qf, kf, vf, attn, dattn, lse, T)
    (dcq, dckv, dkr, dq_raw, dk_raw, cqn, ckvn, dg_qa, dg_kva, dg_qn, dg_kn) = _mla_prep_bwd(
        cq, ckv, kr, pos, dqf, dkf, dvf, g_qa, g_kva, g_qn, g_kn, w_uq, w_uk, w_uv, T, tm)
    grad_x, dproj, dg_mix = _in_proj_bwd(x, dx1, [dcq, dckv, dkr, dhq, dhf, dhi, dhg, dbg], g_mix, w_in_pad, T, tm)

    d_in = _matmul_tn("dw_in", h, dproj)
    d_in = jnp.concatenate([d_in[:, :640], d_in[:, 640 + QK_NOPE:640 + QK_DIM], d_in[:, 768:]], axis=1)
    d_uq = _unpad_heads(_matmul_tn("dw_uq", cqn, dq_raw), QK_DIM)
    d_uk = _unpad_heads(_matmul_tn("dw_uk", ckvn, dk_raw), QK_NOPE).reshape(KV_RANK, MLA_HEADS, QK_NOPE)
    d_uv = _unpad_heads(_matmul_tn("dw_uv", ckvn, dvf), V_DIM).reshape(KV_RANK, MLA_HEADS, V_DIM)
    d_ukv = jnp.concatenate([d_uk, d_uv], axis=2).reshape(KV_RANK, -1)
    d_bra = _matmul_tn("dw_bra", attn, dya).reshape(MLA_HEADS, HEAD_PAD, D_MODEL)[:, :V_DIM].reshape(-1, D_MODEL)
    d_brb = _matmul_tn("dw_brb", rec, dyb)
    grads = {
        "w_in": d_in, "w_uq": d_uq, "w_ukv": d_ukv, "w_branch": jnp.stack([d_bra, d_brb]),
        "w_out": _matmul_tn("dw_out", m, dx1),
        "w_ffn_gate": _matmul_tn("dw_gate", h2, dgt), "w_ffn_up": _matmul_tn("dw_up", h2, dup),
        "w_ffn_down": _matmul_tn("dw_down", a, dx2),
        "w_ple_gate": _matmul_tn("dw_pg", n3, dz), "w_ple_proj": _matmul_tn("dw_pp", p, du),
    }
    dl0 = dlb * lb * (1.0 - lb)
    small_g = {
        "mix_norm_g": dg_mix, "q_a_norm_g": dg_qa, "kv_a_norm_g": dg_kva,
        "q_norm_g": dg_qn[:, :QK_DIM], "k_norm_g": dg_kn[:, :QK_DIM],
        "hg_lb_logits": jnp.concatenate([dl0, -dl0], axis=0), "hg_out_norm_g": dg_out,
        "ffn_norm_g": dg_ffn, "ple_gate_norm_g": dg_pg, "ple_post_norm_g": dg_post,
    }
    return loss_p, grad_x, small_g, grads


def _lower_bound(logits):
    def body(l_ref, lb_ref):
        l = l_ref[...]
        mx = jnp.max(l, axis=0, keepdims=True)
        e = jnp.exp(l - mx)
        lb_ref[...] = e[0:1] / jnp.sum(e, axis=0, keepdims=True)

    return pl.pallas_call(body, name="lower_bound", out_shape=jax.ShapeDtypeStruct((1, HG_W), F32))(logits)


def _my_place():
    return lax.axis_index("x"), lax.axis_index("y"), lax.axis_index("c")


def _all_gather(name, blk):
    rows, cols = blk.shape

    def body(x_ref, out_ref, send_sems, recv_sems, local_sem):
        x, y, c = _my_place()
        me, sibling = (x, y, c), (x, y, 1 - c)
        chips = [(1 - x, y), (x, 1 - y), (1 - x, 1 - y)]

        def slot(px, py, pc):
            return out_ref.at[4 * px + 2 * py + pc]

        def copy(k, block, to, src=None):
            return pltpu.make_async_remote_copy(
                src_ref=slot(*block) if src is None else src, dst_ref=slot(*block),
                send_sem=send_sems.at[k], recv_sem=recv_sems.at[k], device_id=to, device_id_type=MESH_ID)

        mine = pltpu.make_async_copy(x_ref, slot(*me), local_sem)
        mine.start()
        first = [copy(0, me, sibling, src=x_ref)]
        first += [copy(1 + j, me, (*chip, c), src=x_ref) for j, chip in enumerate(chips)]
        for cp in first:
            cp.start()
        passed = [copy(4 + j, (*chip, c), sibling) for j, chip in enumerate(chips)]
        for j, chip in enumerate(chips):
            copy(1 + j, (*chip, c), me).wait_recv()
            passed[j].start()
        copy(0, sibling, me).wait_recv()
        for j, chip in enumerate(chips):
            copy(4 + j, (*chip, 1 - c), me).wait_recv()
        for cp in first + passed:
            cp.wait_send()
        mine.wait()

    any_spec = pl.BlockSpec(memory_space=pl.ANY)
    return pl.pallas_call(
        body, name=name, out_shape=jax.ShapeDtypeStruct((N_DEV, rows, cols), blk.dtype),
        in_specs=[any_spec], out_specs=any_spec,
        scratch_shapes=[pltpu.SemaphoreType.DMA((7,)), pltpu.SemaphoreType.DMA((7,)), pltpu.SemaphoreType.DMA],
    )(blk)


def _exchange_sibling(g):
    _, rows, cols = g.shape

    def body(g_ref, out_ref, send_sems, recv_sems):
        x, y, c = _my_place()
        copies = [pltpu.make_async_remote_copy(
            src_ref=g_ref.at[2 * j + 1 - c], dst_ref=out_ref.at[j], send_sem=send_sems.at[j], recv_sem=recv_sems.at[j],
            device_id=(x, y, 1 - c), device_id_type=MESH_ID) for j in range(4)]
        for cp in copies:
            cp.start()
        for cp in copies:
            cp.wait()

    any_spec = pl.BlockSpec(memory_space=pl.ANY)
    return pl.pallas_call(
        body, name="rs_sibling", out_shape=jax.ShapeDtypeStruct((4, rows, cols), g.dtype),
        in_specs=[any_spec], out_specs=any_spec,
        scratch_shapes=[pltpu.SemaphoreType.DMA((4,)), pltpu.SemaphoreType.DMA((4,))],
    )(g)


def _chip_partial(g, got, c_idx):
    _, rows, cols = g.shape
    tr = _pick_rows(rows, 1024)

    def body(c_ref, g_ref, got_ref, out_ref):
        out_ref[...] = g_ref[...] + got_ref[...]

    grid_spec = pltpu.PrefetchScalarGridSpec(
        num_scalar_prefetch=1, grid=(4, rows // tr),
        in_specs=[pl.BlockSpec((1, tr, cols), lambda j, i, c_ref: (2 * j + c_ref[0], i, 0)),
                  pl.BlockSpec((1, tr, cols), lambda j, i, c_ref: (j, i, 0))],
        out_specs=pl.BlockSpec((1, tr, cols), lambda j, i, c_ref: (j, i, 0)))
    return pl.pallas_call(
        body, name="rs_chip_partial", grid_spec=grid_spec, out_shape=jax.ShapeDtypeStruct((4, rows, cols), g.dtype),
        compiler_params=_cparams(("parallel", "parallel")),
    )(c_idx, g, got)


def _exchange_chips(part):
    _, rows, cols = part.shape

    def body(p_ref, out_ref, send_sems, recv_sems):
        x, y, c = _my_place()
        chips = [(1 - x, y), (x, 1 - y), (1 - x, 1 - y)]
        copies = [pltpu.make_async_remote_copy(
            src_ref=p_ref.at[2 * px + py], dst_ref=out_ref.at[k], send_sem=send_sems.at[k], recv_sem=recv_sems.at[k],
            device_id=(px, py, c), device_id_type=MESH_ID) for k, (px, py) in enumerate(chips)]
        for cp in copies:
            cp.start()
        for cp in copies:
            cp.wait()

    any_spec = pl.BlockSpec(memory_space=pl.ANY)
    return pl.pallas_call(
        body, name="rs_chips", out_shape=jax.ShapeDtypeStruct((3, rows, cols), part.dtype),
        in_specs=[any_spec], out_specs=any_spec,
        scratch_shapes=[pltpu.SemaphoreType.DMA((3,)), pltpu.SemaphoreType.DMA((3,))],
    )(part)


def _final_sum(part, got, chip_idx):
    _, rows, cols = part.shape
    tr = _pick_rows(rows, 1024)

    def body(j_ref, p_ref, got_ref, out_ref):
        out_ref[...] = ((p_ref[0] + got_ref[0]) + got_ref[1]) + got_ref[2]

    grid_spec = pltpu.PrefetchScalarGridSpec(
        num_scalar_prefetch=1, grid=(rows // tr,),
        in_specs=[pl.BlockSpec((1, tr, cols), lambda i, j_ref: (j_ref[0], i, 0)),
                  pl.BlockSpec((3, tr, cols), lambda i, j_ref: (0, i, 0))],
        out_specs=pl.BlockSpec((tr, cols), lambda i, j_ref: (i, 0)))
    return pl.pallas_call(
        body, name="rs_final_sum", grid_spec=grid_spec, out_shape=jax.ShapeDtypeStruct((rows, cols), part.dtype),
        compiler_params=_cparams(("parallel",)),
    )(chip_idx, part, got)


def _reduce_scatter(g):
    x, y, c = _my_place()
    c_idx = jnp.reshape(c, (1,)).astype(jnp.int32)
    chip_idx = jnp.reshape(2 * x + y, (1,)).astype(jnp.int32)
    part = _chip_partial(g, _exchange_sibling(g), c_idx)
    return _final_sum(part, _exchange_chips(part), chip_idx)


def _adamw_math(w, g, m, v):
    m = ADAM_B1 * m + (1.0 - ADAM_B1) * g
    v = ADAM_B2 * v + (1.0 - ADAM_B2) * jnp.square(g)
    m_hat = m / (1.0 - ADAM_B1 ** ADAM_STEP)
    v_hat = v / (1.0 - ADAM_B2 ** ADAM_STEP)
    delta = -ADAM_LR * (m_hat / (jnp.sqrt(v_hat) + ADAM_EPS) + ADAM_WD * w)
    return delta, m, v


def _adamw(name, w, g, m, v):
    rows, cols = w.shape
    tr = 256 if rows % 256 == 0 else (128 if rows % 128 == 0 else rows)

    def body(w_ref, g_ref, m_ref, v_ref, d_ref, m2_ref, v2_ref):
        d_ref[...], m2_ref[...], v2_ref[...] = _adamw_math(w_ref[...], g_ref[...], m_ref[...], v_ref[...])

    spec = pl.BlockSpec((tr, cols), lambda i: (i, 0))
    return pl.pallas_call(
        body, name=name, grid=(rows // tr,), in_specs=[spec] * 4, out_specs=[spec] * 3,
        out_shape=[jax.ShapeDtypeStruct((rows, cols), F32)] * 3, compiler_params=_cparams(("parallel",)),
    )(w, g, m, v)


def _adamw_small(parts, w, m, v):
    rows = w.shape[0]

    def body(p_ref, w_ref, m_ref, v_ref, g_ref, d_ref, m2_ref, v2_ref):
        g = p_ref[0]
        for d in range(1, N_DEV):
            g = g + p_ref[d]
        g_ref[...] = g
        d_ref[...], m2_ref[...], v2_ref[...] = _adamw_math(w_ref[...], g, m_ref[...], v_ref[...])

    return pl.pallas_call(
        body, name="adamw_small", out_shape=[jax.ShapeDtypeStruct((rows, 128), F32)] * 4,
    )(parts, w, m, v)


BIG = (
    ("w_in", (D_MODEL, 4768), 1), ("w_uq", (Q_RANK, 768), 1), ("w_ukv", (KV_RANK, 1024), 1),
    ("w_branch", (2, 512, D_MODEL), 2), ("w_out", (D_MODEL, D_MODEL), 0),
    ("w_ffn_gate", (D_MODEL, FFN), 1), ("w_ffn_up", (D_MODEL, FFN), 1), ("w_ffn_down", (FFN, D_MODEL), 0),
    ("w_ple_gate", (D_MODEL, D_MODEL), 0), ("w_ple_proj", (PLE, D_MODEL), 1),
)
SMALL = (
    ("mix_norm_g", 1024), ("q_a_norm_g", 384), ("kv_a_norm_g", 256), ("q_norm_g", 96), ("k_norm_g", 96),
    ("hg_lb_logits", 1024), ("hg_out_norm_g", 128), ("ffn_norm_g", 1024), ("ple_gate_norm_g", 1024),
    ("ple_post_norm_g", 1024),
)
FLAT_COLS = 1024
SMALL_ROWS = 56


def _local_shape(shape, axis):
    s = list(shape)
    s[axis] //= N_DEV
    return tuple(s)


def _unshard(gathered, shape, axis):
    loc = _local_shape(shape, axis)
    t = gathered.reshape((N_DEV,) + loc)
    t = jnp.moveaxis(t, 0, axis)
    return t.reshape(shape)


def _shard_all(full, shape, axis):
    loc = _local_shape(shape, axis)
    t = full.reshape(shape[:axis] + (N_DEV, loc[axis]) + shape[axis + 1:])
    return jnp.moveaxis(t, axis, 0).reshape(N_DEV, -1)


def _pack_small(vals):
    rows = []
    for name, n in SMALL:
        v = vals[name].reshape(1, -1).astype(F32)
        rows.append(jnp.pad(v, ((0, 0), (0, (-n) % 128))).reshape(-1, 128))
    return jnp.concatenate(rows, axis=0)


def _unpack_small(packed, shapes):
    out, r = {}, 0
    for name, n in SMALL:
        k = (n + 127) // 128
        out[name] = packed[r:r + k].reshape(1, -1)[:, :n].reshape(shapes[name])
        r += k
    return out


_WEIGHTS = ["mix_norm_g", "w_in", "q_a_norm_g", "w_uq", "kv_a_norm_g", "w_ukv", "q_norm_g", "k_norm_g", "hg_lb_logits",
            "hg_out_norm_g", "w_branch", "w_out", "ffn_norm_g", "w_ffn_gate", "w_ffn_up", "w_ffn_down",
            "ple_gate_norm_g", "w_ple_gate", "w_ple_proj", "ple_post_norm_g"]


def _step(x, p, positions, tgt, w, m, v):
    big_names = [n for n, _, _ in BIG]
    small_names = [n for n, _ in SMALL]
    T = x.shape[1]

    flat = jnp.concatenate([w[n].reshape(-1).astype(MM) for n in big_names]).reshape(-1, FLAT_COLS)
    gathered = _all_gather("ag_weights", flat).reshape(N_DEV, -1)
    big, off = {}, 0
    for n, shape, axis in BIG:
        size = int(np.prod(shape)) // N_DEV
        big[n] = _unshard(gathered[:, off:off + size], shape, axis)
        off += size
    small = {n: w[n][0] if n != "hg_lb_logits" else w[n] for n in small_names}
    small = {n: (s.reshape(1, -1) if n != "hg_lb_logits" else s) for n, s in small.items()}

    loss_p, grad_x, small_g, grads = _local_step(
        x[0], p[0, 0], positions.reshape(T, 1), tgt[0], small, big)

    g_all = jnp.concatenate([_shard_all(grads[n], shape, axis) for n, shape, axis in BIG], axis=1)
    g_loc = _reduce_scatter(g_all.reshape(N_DEV, -1, FLAT_COLS)).reshape(-1)
    out_g, out_d, out_m, out_v = {}, {}, {}, {}
    off = 0
    for n, shape, axis in BIG:
        loc = _local_shape(shape, axis)
        size = int(np.prod(loc))
        two_d = (-1, loc[-1])
        g = g_loc[off:off + size].reshape(two_d)
        off += size
        d_, m_, v_ = _adamw("adamw_" + n, w[n].reshape(two_d), g, m[n].reshape(two_d), v[n].reshape(two_d))
        full = w[n].shape
        out_g[n], out_d[n], out_m[n], out_v[n] = g.reshape(full), d_.reshape(full), m_.reshape(full), v_.reshape(full)

    packed_g = _pack_small(small_g)
    loss_row = jnp.concatenate([jnp.pad(jnp.sum(loss_p).reshape(1, 1), ((0, 0), (0, 127))),
                                jnp.zeros((SMALL_ROWS - packed_g.shape[0] - 1, 128), F32)], axis=0)
    parts = _all_gather("ag_small", jnp.concatenate([packed_g, loss_row], axis=0))
    pad_rows = lambda t: jnp.pad(t, ((0, SMALL_ROWS - t.shape[0]), (0, 0)))
    sw = pad_rows(_pack_small({n: w[n] for n in small_names}))
    sm = pad_rows(_pack_small({n: m[n] for n in small_names}))
    sv = pad_rows(_pack_small({n: v[n] for n in small_names}))
    g_s, d_s, m_s, v_s = _adamw_small(parts, sw, sm, sv)
    shapes = {n: w[n].shape for n in small_names}
    n_packed = packed_g.shape[0]
    loss = g_s[n_packed, 0]
    for src, dst in ((g_s, out_g), (d_s, out_d), (m_s, out_m), (v_s, out_v)):
        dst.update(_unpack_small(src, shapes))

    outs = [loss, grad_x[None]]
    for table in (out_g, out_d, out_m, out_v):
        outs += [table[n] for n in _WEIGHTS]
    return tuple(outs)


def kernel(x, p, positions, mix_norm_g, w_in, q_a_norm_g, w_uq, kv_a_norm_g, w_ukv, q_norm_g, k_norm_g, hg_lb_logits, hg_out_norm_g, w_branch, w_out, ffn_norm_g, w_ffn_gate, w_ffn_up, w_ffn_down, ple_gate_norm_g, w_ple_gate, w_ple_proj, ple_post_norm_g, loss_target, m_mix_norm_g, m_w_in, m_q_a_norm_g, m_w_uq, m_kv_a_norm_g, m_w_ukv, m_q_norm_g, m_k_norm_g, m_hg_lb_logits, m_hg_out_norm_g, m_w_branch, m_w_out, m_ffn_norm_g, m_w_ffn_gate, m_w_ffn_up, m_w_ffn_down, m_ple_gate_norm_g, m_w_ple_gate, m_w_ple_proj, m_ple_post_norm_g, v_mix_norm_g, v_w_in, v_q_a_norm_g, v_w_uq, v_kv_a_norm_g, v_w_ukv, v_q_norm_g, v_k_norm_g, v_hg_lb_logits, v_hg_out_norm_g, v_w_branch, v_w_out, v_ffn_norm_g, v_w_ffn_gate, v_w_ffn_up, v_w_ffn_down, v_ple_gate_norm_g, v_w_ple_gate, v_w_ple_proj, v_ple_post_norm_g):
    w = dict(mix_norm_g=mix_norm_g, w_in=w_in, q_a_norm_g=q_a_norm_g, w_uq=w_uq, kv_a_norm_g=kv_a_norm_g, w_ukv=w_ukv,
             q_norm_g=q_norm_g, k_norm_g=k_norm_g, hg_lb_logits=hg_lb_logits, hg_out_norm_g=hg_out_norm_g,
             w_branch=w_branch, w_out=w_out, ffn_norm_g=ffn_norm_g, w_ffn_gate=w_ffn_gate, w_ffn_up=w_ffn_up,
             w_ffn_down=w_ffn_down, ple_gate_norm_g=ple_gate_norm_g, w_ple_gate=w_ple_gate, w_ple_proj=w_ple_proj,
             ple_post_norm_g=ple_post_norm_g)
    m = dict(mix_norm_g=m_mix_norm_g, w_in=m_w_in, q_a_norm_g=m_q_a_norm_g, w_uq=m_w_uq, kv_a_norm_g=m_kv_a_norm_g,
             w_ukv=m_w_ukv, q_norm_g=m_q_norm_g, k_norm_g=m_k_norm_g, hg_lb_logits=m_hg_lb_logits,
             hg_out_norm_g=m_hg_out_norm_g, w_branch=m_w_branch, w_out=m_w_out, ffn_norm_g=m_ffn_norm_g,
             w_ffn_gate=m_w_ffn_gate, w_ffn_up=m_w_ffn_up, w_ffn_down=m_w_ffn_down,
             ple_gate_norm_g=m_ple_gate_norm_g, w_ple_gate=m_w_ple_gate, w_ple_proj=m_w_ple_proj,
             ple_post_norm_g=m_ple_post_norm_g)
    v = dict(mix_norm_g=v_mix_norm_g, w_in=v_w_in, q_a_norm_g=v_q_a_norm_g, w_uq=v_w_uq, kv_a_norm_g=v_kv_a_norm_g,
             w_ukv=v_w_ukv, q_norm_g=v_q_norm_g, k_norm_g=v_k_norm_g, hg_lb_logits=v_hg_lb_logits,
             hg_out_norm_g=v_hg_out_norm_g, w_branch=v_w_branch, w_out=v_w_out, ffn_norm_g=v_ffn_norm_g,
             w_ffn_gate=v_w_ffn_gate, w_ffn_up=v_w_ffn_up, w_ffn_down=v_w_ffn_down,
             ple_gate_norm_g=v_ple_gate_norm_g, w_ple_gate=v_w_ple_gate, w_ple_proj=v_w_ple_proj,
             ple_post_norm_g=v_ple_post_norm_g)
    return _step(x, p, positions, loss_target, w, m, v)
```

```python
import functools

import jax
import jax.numpy as jnp
import numpy as np
from jax import lax
from jax.experimental import pallas as pl
from jax.experimental.pallas import tpu as pltpu

F32 = jnp.float32
MM = jnp.bfloat16
HI = lax.Precision.HIGHEST
MESH_ID = pl.DeviceIdType.MESH

D_MODEL = 1024
N_DEV = 8
MLA_HEADS = 8
QK_NOPE = 64
QK_ROPE = 32
QK_DIM = 96
V_DIM = 64
HEAD_PAD = 128
Q_RANK = 384
KV_RANK = 256
ROPE_BASE = 10000.0
HG_HEADS = 4
HG_DIM = 128
HG_W = 512
HG_CHUNK = 64
HG_SUB = 16
FFN = 2816
PLE = 256
EPS = 1e-6
ATT_SCALE = QK_DIM ** -0.5
NEG = -1e30

ADAM_LR = 0.001
ADAM_B1 = 0.9
ADAM_B2 = 0.999
ADAM_EPS = 1e-08
ADAM_WD = 0.01
ADAM_STEP = 10

SEC_CQ = (0, 384)
SEC_CKV = (384, 256)
SEC_KR = (640, 128)
SEC_HQ = (768, 512)
SEC_HF = (1280, 512)
SEC_HI = (1792, 512)
SEC_HG = (2304, 512)
SEC_BG = (2816, 2048)
IN_PAD = 4864
SECTIONS = (SEC_CQ, SEC_CKV, SEC_KR, SEC_HQ, SEC_HF, SEC_HI, SEC_HG, SEC_BG)

VMEM_LIMIT = 58 * 1024 * 1024
ROW_TILE = 256
ATT_TILE = 512
HG_BLOCK = 512


def _dot(a, b):
    return jnp.dot(a.astype(MM), b.astype(MM), preferred_element_type=F32)


def _dot_nt(a, b):
    return lax.dot_general(a.astype(MM), b.astype(MM), (((1,), (1,)), ((), ())), preferred_element_type=F32)


def _dot_tn(a, b):
    return lax.dot_general(a.astype(MM), b.astype(MM), (((0,), (0,)), ((), ())), preferred_element_type=F32)


def _dot_hi(a, b):
    return jnp.dot(a, b, preferred_element_type=F32, precision=HI)


def _sigmoid(x):
    return 1.0 / (1.0 + jnp.exp(-x))


def _rms(x, n=None):
    n = x.shape[-1] if n is None else n
    r = lax.rsqrt(jnp.sum(x * x, axis=-1, keepdims=True) * (1.0 / n) + EPS)
    return x * r, r


def _rms_bwd(dxh, xh, r, n=None):
    n = xh.shape[-1] if n is None else n
    return r * (dxh - xh * (jnp.sum(dxh * xh, axis=-1, keepdims=True) * (1.0 / n)))


def _rope_tables(pos, tm):
    lane = lax.broadcasted_iota(jnp.int32, (tm, HEAD_PAD), 1)
    idx = jnp.where(lane < QK_NOPE + QK_ROPE // 2, lane - QK_NOPE, lane - QK_NOPE - QK_ROPE // 2)
    inv = jnp.exp(idx.astype(F32) * (-np.log(ROPE_BASE) * 2.0 / QK_ROPE))
    ang = pos.astype(F32) * inv
    in_rope = (lane >= QK_NOPE) & (lane < QK_DIM)
    first = lane < QK_NOPE + QK_ROPE // 2
    cos_t = jnp.where(in_rope, jnp.cos(ang), 1.0)
    sin_t = jnp.where(in_rope, jnp.where(first, -jnp.sin(ang), jnp.sin(ang)), 0.0)
    return cos_t, sin_t, (first, in_rope)


def _rope_swap(x, halves):
    first, in_rope = halves
    half = QK_ROPE // 2
    return jnp.where(in_rope, jnp.where(first, pltpu.roll(x, HEAD_PAD - half, 1), pltpu.roll(x, half, 1)), 0.0)


def _cparams(sem, vmem=None):
    return pltpu.CompilerParams(dimension_semantics=sem, vmem_limit_bytes=vmem)


def _row_call(name, body, T, tm, row_ins, full_ins, row_outs, acc_outs, vmem=None):
    def kern(*refs):
        body(pl.program_id(0), *refs)

    in_specs = [pl.BlockSpec((tm, a.shape[1]), lambda i: (i, 0)) for a in row_ins]
    in_specs += [pl.BlockSpec(a.shape, lambda i, nd=a.ndim: (0,) * nd, pipeline_mode=pl.Buffered(1)) for a in full_ins]
    out_specs = [pl.BlockSpec((tm, n), lambda i: (i, 0)) for n, _ in row_outs]
    out_specs += [pl.BlockSpec(s, lambda i, nd=len(s): (0,) * nd) for s, _ in acc_outs]
    out_shape = [jax.ShapeDtypeStruct((T, n), dt) for n, dt in row_outs]
    out_shape += [jax.ShapeDtypeStruct(s, dt) for s, dt in acc_outs]
    return pl.pallas_call(
        kern, name=name, grid=(T // tm,), in_specs=in_specs, out_specs=out_specs, out_shape=out_shape,
        compiler_params=_cparams(("arbitrary",), vmem),
    )(*row_ins, *full_ins)


def _acc(ref, i, val):
    @pl.when(i == 0)
    def _():
        ref[...] = val

    @pl.when(i != 0)
    def _():
        ref[...] += val


def _in_proj_fwd(x, g_mix, w_in_pad, T, tm):
    def body(i, x_ref, g_ref, w_ref, h_ref, *outs):
        xh, _ = _rms(x_ref[...])
        h = (xh * g_ref[...]).astype(MM)
        h_ref[...] = h
        for (s, n), o_ref in zip(SECTIONS, outs):
            o_ref[...] = jnp.dot(h, w_ref[:, s:s + n], preferred_element_type=F32)

    row_outs = [(D_MODEL, MM)] + [(n, F32) for _, n in SECTIONS]
    return _row_call("in_proj_fwd", body, T, tm, [x], [g_mix, w_in_pad], row_outs, [], VMEM_LIMIT)


def _mla_heads_fwd(raw, g_pad, cos_t, sin_t, first):
    outs, saved = [], []
    for h in range(MLA_HEADS):
        xh, r = _rms(raw[:, h * HEAD_PAD:(h + 1) * HEAD_PAD], QK_DIM)
        y = xh * g_pad
        outs.append(y * cos_t + _rope_swap(y, first) * sin_t)
        saved.append((xh, r))
    return outs, saved


def _mla_raw_heads(cqn, ckvn, kr, wuq_ref, wukv_ref, tm):
    nope = lax.broadcasted_iota(jnp.int32, (tm, HEAD_PAD), 1) < QK_NOPE
    qs, ks, vs = [], [], []
    for h in range(MLA_HEADS):
        qs.append(_dot(cqn, wuq_ref[h]))
        kv = _dot(ckvn, wukv_ref[h])
        ks.append(jnp.where(nope, kv, kr))
        vs.append(jnp.where(nope, pltpu.roll(kv, V_DIM, 1), 0.0))
    return jnp.concatenate(qs, axis=1), jnp.concatenate(ks, axis=1), jnp.concatenate(vs, axis=1)


def _mla_prep_fwd(cq, ckv, kr, pos, g_qa, g_kva, g_qn, g_kn, w_uq, w_ukv, T, tm):
    def body(i, cq_ref, ckv_ref, kr_ref, pos_ref, gqa_ref, gkva_ref, gqn_ref, gkn_ref, wuq_ref, wukv_ref,
             q_ref, k_ref, v_ref):
        cos_t, sin_t, first = _rope_tables(pos_ref[...], tm)
        cqn = _rms(cq_ref[...])[0] * gqa_ref[...]
        ckvn = _rms(ckv_ref[...])[0] * gkva_ref[...]
        q_raw, k_raw, v = _mla_raw_heads(cqn, ckvn, kr_ref[...], wuq_ref, wukv_ref, tm)
        qs, _ = _mla_heads_fwd(q_raw, gqn_ref[...], cos_t, sin_t, first)
        ks, _ = _mla_heads_fwd(k_raw, gkn_ref[...], cos_t, sin_t, first)
        q_ref[...] = jnp.concatenate(qs, axis=1).astype(MM)
        k_ref[...] = jnp.concatenate(ks, axis=1).astype(MM)
        v_ref[...] = v.astype(MM)

    w = MLA_HEADS * HEAD_PAD
    return _row_call("mla_prep_fwd", body, T, tm, [cq, ckv, kr, pos], [g_qa, g_kva, g_qn, g_kn, w_uq, w_ukv],
                     [(w, MM), (w, MM), (w, MM)], [])


def _causal_pairs(n, by_query):
    if by_query:
        pairs = [(q, k) for q in range(n) for k in range(q + 1)]
    else:
        pairs = [(q, k) for k in range(n) for q in range(k, n)]
    return np.array([p[0] for p in pairs], np.int32), np.array([p[1] for p in pairs], np.int32)


def _flash_fwd(qf, kf, vf, T):
    tq = min(ATT_TILE, T)
    nq = T // tq

    qi_tab, ki_tab = _causal_pairs(nq, by_query=True)

    def body(qi_ref, ki_ref, q_ref, k_ref, v_ref, o_ref, lse_ref, m_s, l_s, acc_s):
        t = pl.program_id(1)
        qi, ki = qi_ref[t], ki_ref[t]

        @pl.when(ki == 0)
        def _():
            m_s[...] = jnp.full_like(m_s, NEG)
            l_s[...] = jnp.zeros_like(l_s)
            acc_s[...] = jnp.zeros_like(acc_s)

        def step(masked):
            s = _dot_nt(q_ref[...], k_ref[...]) * ATT_SCALE
            if masked:
                row = lax.broadcasted_iota(jnp.int32, (tq, tq), 0)
                col = lax.broadcasted_iota(jnp.int32, (tq, tq), 1)
                s = jnp.where(col <= row, s, NEG)
            m_old = m_s[...]
            m_new = jnp.maximum(m_old, jnp.max(s, axis=1, keepdims=True))
            alpha = jnp.exp(m_old - m_new)
            p = jnp.exp(s - m_new)
            l_s[...] = alpha * l_s[...] + jnp.sum(p, axis=1, keepdims=True)
            acc_s[...] = alpha * acc_s[...] + _dot(p, v_ref[...])
            m_s[...] = m_new

        @pl.when(ki < qi)
        def _():
            step(False)

        @pl.when(ki == qi)
        def _():
            step(True)
            l = l_s[...]
            o_ref[...] = acc_s[...] / l
            lse_ref[...] = jnp.broadcast_to(m_s[...] + jnp.log(l), (tq, HEAD_PAD))

    q_spec = pl.BlockSpec((tq, HEAD_PAD), lambda h, t, qi_ref, ki_ref: (qi_ref[t], h))
    kv_spec = pl.BlockSpec((tq, HEAD_PAD), lambda h, t, qi_ref, ki_ref: (ki_ref[t], h))
    grid_spec = pltpu.PrefetchScalarGridSpec(
        num_scalar_prefetch=2, grid=(MLA_HEADS, len(qi_tab)),
        in_specs=[q_spec, kv_spec, kv_spec], out_specs=[q_spec, q_spec],
        scratch_shapes=[pltpu.VMEM((tq, 1), F32), pltpu.VMEM((tq, 1), F32), pltpu.VMEM((tq, HEAD_PAD), F32)])
    return pl.pallas_call(
        body, name="flash_fwd", grid_spec=grid_spec,
        out_shape=[jax.ShapeDtypeStruct((T, MLA_HEADS * HEAD_PAD), F32)] * 2,
        compiler_params=_cparams(("parallel", "arbitrary")),
    )(jnp.asarray(qi_tab), jnp.asarray(ki_tab), qf, kf, vf)


def _hg_gates(hf, lb):
    sg = _sigmoid(hf)
    f = lb + (1.0 - lb) * sg
    return sg, f, jnp.log(f), 1.0 - f


def _tri(n, lower):
    r = lax.broadcasted_iota(jnp.int32, (n, n), 0)
    c = lax.broadcasted_iota(jnp.int32, (n, n), 1)
    return jnp.where((c <= r) if lower else (c >= r), 1.0, 0.0).astype(F32)


def _hg_intra(q, k, b):
    C, S = HG_CHUNK, HG_SUB
    row_c = lax.broadcasted_iota(jnp.int32, (C, HG_DIM), 0)
    row_s = lax.broadcasted_iota(jnp.int32, (S, HG_DIM), 0)
    lane_c = lax.broadcasted_iota(jnp.int32, (S, C), 1)
    blocks, saved = [], []
    for blk in range(C // S):
        lo = blk * S
        q_b, k_b, b_b = q[lo:lo + S], k[lo:lo + S], b[lo:lo + S]
        a_b = jnp.zeros((S, C), F32)
        for j in range(S):
            w = jnp.exp(jnp.where(row_s >= j, b_b - b_b[j:j + 1], NEG))
            col = jnp.sum(q_b * (k_b[j:j + 1] * w), axis=1, keepdims=True)
            a_b = jnp.where(lane_c == lo + j, col, a_b)
        if blk > 0:
            ref = b[lo - 1:lo]
            q_e = jnp.exp(b_b - ref)
            q_t = q_b * q_e
            k_e = jnp.exp(jnp.where(row_c < lo, ref - b, NEG))
            a_b = a_b + _dot_nt(q_t, k * k_e)
            saved.append((q_t, k_e, q_e))
        else:
            saved.append(None)
        blocks.append(a_b)
    return jnp.concatenate(blocks, axis=0), saved


def _hgrn_fwd(hq, hf, hi, lb, T):
    rb = min(HG_BLOCK, T)
    ncb = rb // HG_CHUNK

    def body(hq_ref, hf_ref, hi_ref, lb_ref, o_ref, s0_ref, st_ref):
        @pl.when(pl.program_id(0) == 0)
        def _():
            st_ref[...] = jnp.zeros_like(st_ref)

        tril = _tri(HG_CHUNK, True)

        def chunk(c, carry):
            rows = pl.ds(pl.multiple_of(c * HG_CHUNK, HG_CHUNK), HG_CHUNK)
            _, _, logf, kk = _hg_gates(hf_ref[rows, :], lb_ref[...])
            b = _dot_hi(tril, logf)
            q_all, v_all = hq_ref[rows, :], hi_ref[rows, :]
            outs = []
            for h in range(HG_HEADS):
                ls = slice(h * HG_DIM, (h + 1) * HG_DIM)
                q, k, v, bh = q_all[:, ls], kk[:, ls], v_all[:, ls], b[:, ls]
                st = st_ref[h]
                s0_ref[c, h * HG_DIM:(h + 1) * HG_DIM, :] = st
                b_end = bh[HG_CHUNK - 1:HG_CHUNK]
                a, _ = _hg_intra(q, k, bh)
                outs.append(_dot_nt(q * jnp.exp(bh), st) + _dot(a, v))
                st_ref[h] = st * jnp.exp(b_end) + _dot_tn(v, k * jnp.exp(b_end - bh))
            o_ref[rows, :] = jnp.concatenate(outs, axis=1)
            return carry

        lax.fori_loop(0, ncb, chunk, 0)

    row = pl.BlockSpec((rb, HG_W), lambda i: (i, 0))
    return pl.pallas_call(
        body, name="hgrn_fwd", grid=(T // rb,),
        in_specs=[row, row, row, pl.BlockSpec((1, HG_W), lambda i: (0, 0))],
        out_specs=[row, pl.BlockSpec((ncb, HG_W, HG_DIM), lambda i: (i, 0, 0))],
        out_shape=[jax.ShapeDtypeStruct((T, HG_W), F32), jax.ShapeDtypeStruct((T // HG_CHUNK, HG_W, HG_DIM), F32)],
        scratch_shapes=[pltpu.VMEM((HG_HEADS, HG_DIM, HG_DIM), F32)],
        compiler_params=_cparams(("arbitrary",)),
    )(hq, hf, hi, lb)


def _hgrn_bwd(hq, hf, hi, do, s0, lb, T):
    rb = min(HG_BLOCK, T)
    ncb = rb // HG_CHUNK
    nb = T // rb
    C, S = HG_CHUNK, HG_SUB

    def body(hq_ref, hf_ref, hi_ref, do_ref, s0_ref, lb_ref, dq_ref, df_ref, dv_ref, dlb_ref, dst_ref):
        @pl.when(pl.program_id(0) == 0)
        def _():
            dst_ref[...] = jnp.zeros_like(dst_ref)
            dlb_ref[...] = jnp.zeros_like(dlb_ref)

        tril, triu = _tri(C, True), _tri(C, False)
        row_cc = lax.broadcasted_iota(jnp.int32, (C, C), 0)
        col_cc = lax.broadcasted_iota(jnp.int32, (C, C), 1)
        row_s = lax.broadcasted_iota(jnp.int32, (S, HG_DIM), 0)
        lane_sc = lax.broadcasted_iota(jnp.int32, (S, C), 1)
        last_row = lax.broadcasted_iota(jnp.int32, (C, HG_DIM), 0) == C - 1
        lb_v = lb_ref[...]

        def chunk(cc, carry):
            c = ncb - 1 - cc
            rows = pl.ds(pl.multiple_of(c * C, C), C)
            hf_c = hf_ref[rows, :]
            sg, f, logf, kk = _hg_gates(hf_c, lb_v)
            b = _dot_hi(tril, logf)
            q_all, v_all, do_all = hq_ref[rows, :], hi_ref[rows, :], do_ref[rows, :]
            dq_o, dk_o, dv_o, db_o = [], [], [], []
            for h in range(HG_HEADS):
                ls = slice(h * HG_DIM, (h + 1) * HG_DIM)
                q, k, v, bh, d_o = q_all[:, ls], kk[:, ls], v_all[:, ls], b[:, ls], do_all[:, ls]
                st0 = s0_ref[c, h * HG_DIM:(h + 1) * HG_DIM, :]
                dst = dst_ref[h]
                b_end = bh[C - 1:C]
                e_b, e_end = jnp.exp(bh), jnp.exp(b_end)
                e_rem = jnp.exp(b_end - bh)
                qe, kd = q * e_b, k * e_rem
                st_end = st0 * e_end + _dot_tn(v, kd)
                a, saved = _hg_intra(q, k, bh)
                d_a = jnp.where(col_cc <= row_cc, _dot_nt(d_o, v), 0.0)
                dv = _dot_tn(a, d_o) + _dot_nt(kd, dst)
                dq = e_b * _dot(d_o, st0)
                dk = e_rem * _dot(v, dst)
                dq_blocks, dk_diag = [], []
                for blk in range(C // S):
                    lo = blk * S
                    q_b, k_b, b_b = q[lo:lo + S], k[lo:lo + S], bh[lo:lo + S]
                    da_b = d_a[lo:lo + S]
                    dq_b = jnp.zeros((S, HG_DIM), F32)
                    dk_b = jnp.zeros((S, HG_DIM), F32)
                    for j in range(S):
                        w = jnp.exp(jnp.where(row_s >= j, b_b - b_b[j:j + 1], NEG))
                        col = jnp.sum(jnp.where(lane_sc == lo + j, da_b, 0.0), axis=1, keepdims=True)
                        dq_b = dq_b + col * (k_b[j:j + 1] * w)
                        dk_row = jnp.sum(col * (q_b * w), axis=0, keepdims=True)
                        dk_b = jnp.where(row_s == j, dk_row, dk_b)
                    if blk > 0:
                        q_t, k_e, q_e = saved[blk]
                        da_off = jnp.where(lane_sc < lo, da_b, 0.0)
                        dq_b = dq_b + _dot(da_off, k * k_e) * q_e
                        dk = dk + _dot_tn(da_off, q_t) * k_e
                    dq_blocks.append(dq_b)
                    dk_diag.append(dk_b)
                dq = dq + jnp.concatenate(dq_blocks, axis=0)
                dk = dk + jnp.concatenate(dk_diag, axis=0)
                extra = jnp.sum(dst * st_end, axis=0, keepdims=True)
                db_o.append(q * dq - k * dk + jnp.where(last_row, extra, 0.0))
                dst_ref[h] = dst * e_end + _dot_tn(d_o, qe)
                dq_o.append(dq)
                dk_o.append(dk)
                dv_o.append(dv)
            dlogf = _dot_hi(triu, jnp.concatenate(db_o, axis=1))
            d_f = dlogf / f - jnp.concatenate(dk_o, axis=1)
            dq_ref[rows, :] = jnp.concatenate(dq_o, axis=1)
            dv_ref[rows, :] = jnp.concatenate(dv_o, axis=1)
            df_ref[rows, :] = d_f * (1.0 - lb_v) * sg * (1.0 - sg)
            dlb_ref[...] += jnp.sum(d_f * (1.0 - sg), axis=0, keepdims=True)
            return carry

        lax.fori_loop(0, ncb, chunk, 0)

    row = pl.BlockSpec((rb, HG_W), lambda i: (nb - 1 - i, 0))
    one = pl.BlockSpec((1, HG_W), lambda i: (0, 0))
    return pl.pallas_call(
        body, name="hgrn_bwd", grid=(nb,),
        in_specs=[row, row, row, row, pl.BlockSpec((ncb, HG_W, HG_DIM), lambda i: (nb - 1 - i, 0, 0)), one],
        out_specs=[row, row, row, one],
        out_shape=[jax.ShapeDtypeStruct((T, HG_W), F32)] * 3 + [jax.ShapeDtypeStruct((1, HG_W), F32)],
        scratch_shapes=[pltpu.VMEM((HG_HEADS, HG_DIM, HG_DIM), F32)],
        compiler_params=_cparams(("arbitrary",)),
    )(hq, hf, hi, do, s0, lb)


def _silu_parts(x):
    sg = _sigmoid(x)
    return x * sg, sg * (1.0 + x * (1.0 - sg))


def _merge_fwd(attn, o, hg, bg, x, g_out, w_bra, w_brb, w_out, T, tm):
    def body(i, attn_ref, o_ref, hg_ref, bg_ref, x_ref, g_ref, wa_ref, wb_ref, wo_ref,
             x1_ref, ya_ref, yb_ref, m_ref, rec_ref):
        recs = []
        for h in range(HG_HEADS):
            ls = slice(h * HG_DIM, (h + 1) * HG_DIM)
            oh, _ = _rms(o_ref[:, ls])
            recs.append(oh * g_ref[...] * _silu_parts(hg_ref[:, ls])[0])
        rec = jnp.concatenate(recs, axis=1)
        ya = _dot(attn_ref[...], wa_ref[...])
        yb = _dot(rec, wb_ref[...])
        m = _sigmoid(bg_ref[:, :D_MODEL]) * ya + _sigmoid(bg_ref[:, D_MODEL:]) * yb
        x1_ref[...] = x_ref[...] + _dot(m, wo_ref[...])
        ya_ref[...] = ya
        yb_ref[...] = yb
        m_ref[...] = m.astype(MM)
        rec_ref[...] = rec.astype(MM)

    return _row_call("merge_fwd", body, T, tm, [attn, o, hg, bg, x], [g_out, w_bra, w_brb, w_out],
                     [(D_MODEL, F32), (D_MODEL, F32), (D_MODEL, F32), (D_MODEL, MM), (HG_W, MM)], [], VMEM_LIMIT)


def _ffn_fwd(x1, g_ffn, w_g, w_u, w_d, T, tm):
    def body(i, x1_ref, g_ref, wg_ref, wu_ref, wd_ref, x2_ref, gt_ref, up_ref, h2_ref):
        x1 = x1_ref[...]
        h2 = (_rms(x1)[0] * g_ref[...]).astype(MM)
        gt = jnp.dot(h2, wg_ref[...], preferred_element_type=F32)
        up = jnp.dot(h2, wu_ref[...], preferred_element_type=F32)
        a = _silu_parts(gt)[0] * up
        x2_ref[...] = x1 + _dot(a, wd_ref[...])
        gt_ref[...] = gt
        up_ref[...] = up
        h2_ref[...] = h2

    return _row_call("ffn_fwd", body, T, tm, [x1], [g_ffn, w_g, w_u, w_d],
                     [(D_MODEL, F32), (FFN, F32), (FFN, F32), (D_MODEL, MM)], [], VMEM_LIMIT)


def _ple_loss(x2, p, tgt, g_pg, g_post, w_pg, w_pp, T, tm):
    def body(i, x2_ref, p_ref, t_ref, gpg_ref, gpo_ref, wpg_ref, wpp_ref,
             dx2_ref, n3_ref, dz_ref, du_ref, loss_ref, dgpo_ref, dgpg_ref):
        x2 = x2_ref[...]
        p_mm = p_ref[...].astype(MM)
        u = jnp.concatenate([jnp.dot(p_mm, wpp_ref[d], preferred_element_type=F32) for d in range(N_DEV)], axis=1)
        uh, ru = _rms(u)
        e = uh * gpo_ref[...]
        x2h, r3 = _rms(x2)
        n3 = x2h * gpg_ref[...]
        gate = _sigmoid(_dot(n3, wpg_ref[...]))
        diff = x2 + gate * e - t_ref[...]
        dy = diff * (1.0 / D_MODEL)
        de = dy * gate
        dz = dy * e * gate * (1.0 - gate)
        du = _rms_bwd(de * gpo_ref[...], uh, ru)
        dn3 = _dot_nt(dz, wpg_ref[...])
        dx2_ref[...] = dy + _rms_bwd(dn3 * gpg_ref[...], x2h, r3)
        n3_ref[...] = n3.astype(MM)
        dz_ref[...] = dz.astype(MM)
        du_ref[...] = du.astype(MM)
        _acc(loss_ref, i, jnp.sum(diff * diff, axis=0, keepdims=True) * (0.5 / D_MODEL))
        _acc(dgpo_ref, i, jnp.sum(de * uh, axis=0, keepdims=True))
        _acc(dgpg_ref, i, jnp.sum(dn3 * x2h, axis=0, keepdims=True))

    vec = ((1, D_MODEL), F32)
    return _row_call("ple_loss", body, T, tm, [x2, p, tgt], [g_pg, g_post, w_pg, w_pp],
                     [(D_MODEL, F32), (D_MODEL, MM), (D_MODEL, MM), (D_MODEL, MM)], [vec, vec, vec], VMEM_LIMIT)


def _ffn_bwd(dx2, x1, gt, up, g_ffn, w_g, w_u, w_d, T, tm):
    def body(i, dx2_ref, x1_ref, gt_ref, up_ref, g_ref, wg_ref, wu_ref, wd_ref,
             dx1_ref, a_ref, dgt_ref, dup_ref, dg_ref):
        dx2 = dx2_ref[...]
        x1h, r = _rms(x1_ref[...])
        up = up_ref[...]
        silu, dsilu = _silu_parts(gt_ref[...])
        da = _dot_nt(dx2, wd_ref[...])
        dgt = (da * up * dsilu).astype(MM)
        dup = (da * silu).astype(MM)
        dh2 = (lax.dot_general(dgt, wg_ref[...], (((1,), (1,)), ((), ())), preferred_element_type=F32)
               + lax.dot_general(dup, wu_ref[...], (((1,), (1,)), ((), ())), preferred_element_type=F32))
        dx1_ref[...] = dx2 + _rms_bwd(dh2 * g_ref[...], x1h, r)
        a_ref[...] = (silu * up).astype(MM)
        dgt_ref[...] = dgt
        dup_ref[...] = dup
        _acc(dg_ref, i, jnp.sum(dh2 * x1h, axis=0, keepdims=True))

    return _row_call("ffn_bwd", body, T, tm, [dx2, x1, gt, up], [g_ffn, w_g, w_u, w_d],
                     [(D_MODEL, F32), (FFN, MM), (FFN, MM), (FFN, MM)], [((1, D_MODEL), F32)], VMEM_LIMIT)


def _merge_bwd(dx1, ya, yb, bg, o, hg, g_out, w_bra, w_brb, w_out, T, tm):
    def body(i, dx1_ref, ya_ref, yb_ref, bg_ref, o_ref, hg_ref, g_ref, wa_ref, wb_ref, wo_ref,
             dattn_ref, do_ref, dhg_ref, dbg_ref, dya_ref, dyb_ref, dg_ref):
        dm = _dot_nt(dx1_ref[...], wo_ref[...])
        ga, gb = _sigmoid(bg_ref[:, :D_MODEL]), _sigmoid(bg_ref[:, D_MODEL:])
        dya, dyb = (dm * ga).astype(MM), (dm * gb).astype(MM)
        dbg_ref[:, :D_MODEL] = dm * ya_ref[...] * ga * (1.0 - ga)
        dbg_ref[:, D_MODEL:] = dm * yb_ref[...] * gb * (1.0 - gb)
        dya_ref[...] = dya
        dyb_ref[...] = dyb
        dattn_ref[...] = lax.dot_general(dya, wa_ref[...], (((1,), (1,)), ((), ())), preferred_element_type=F32)
        drec = lax.dot_general(dyb, wb_ref[...], (((1,), (1,)), ((), ())), preferred_element_type=F32)
        dg = jnp.zeros((1, HG_DIM), F32)
        for h in range(HG_HEADS):
            ls = slice(h * HG_DIM, (h + 1) * HG_DIM)
            oh, r = _rms(o_ref[:, ls])
            silu, dsilu = _silu_parts(hg_ref[:, ls])
            dr = drec[:, ls]
            dhg_ref[:, ls] = dr * oh * g_ref[...] * dsilu
            don = dr * silu
            dg = dg + jnp.sum(don * oh, axis=0, keepdims=True)
            do_ref[:, ls] = _rms_bwd(don * g_ref[...], oh, r)
        _acc(dg_ref, i, dg)

    return _row_call("merge_bwd", body, T, tm, [dx1, ya, yb, bg, o, hg], [g_out, w_bra, w_brb, w_out],
                     [(D_MODEL, F32), (HG_W, F32), (HG_W, F32), (2 * D_MODEL, F32), (D_MODEL, MM), (D_MODEL, MM)],
                     [((1, HG_DIM), F32)], VMEM_LIMIT)


def _flash_bwd(qf, kf, vf, o, do, lse, T):
    tq = min(ATT_TILE, T)
    nq = T // tq

    qi_tab, ki_tab = _causal_pairs(nq, by_query=False)

    def body(qi_ref, ki_ref, q_ref, k_ref, v_ref, o_ref, do_ref, lse_ref, dq_ref, dk_ref, dv_ref):
        t = pl.program_id(1)
        qi, ki = qi_ref[t], ki_ref[t]

        @pl.when(t == 0)
        def _():
            dq_ref[...] = jnp.zeros_like(dq_ref)

        def step(masked):
            q, k, d_o = q_ref[...], k_ref[...], do_ref[...]
            s = _dot_nt(q, k) * ATT_SCALE
            if masked:
                row = lax.broadcasted_iota(jnp.int32, (tq, tq), 0)
                col = lax.broadcasted_iota(jnp.int32, (tq, tq), 1)
                s = jnp.where(col <= row, s, NEG)
            p = jnp.exp(s - lse_ref[:, :1])
            delta = jnp.sum(d_o * o_ref[...], axis=1, keepdims=True)
            ds = p * (_dot_nt(d_o, v_ref[...]) - delta) * ATT_SCALE
            rows = pl.ds(pl.multiple_of(qi * tq, tq), tq)
            dq_ref[rows, :] += _dot(ds, k)
            return _dot_tn(p, d_o), _dot_tn(ds, q)

        @pl.when(qi == ki)
        def _():
            dv_ref[...], dk_ref[...] = step(True)

        @pl.when(qi > ki)
        def _():
            dv, dk = step(False)
            dv_ref[...] += dv
            dk_ref[...] += dk

    q_spec = pl.BlockSpec((tq, HEAD_PAD), lambda h, t, qi_ref, ki_ref: (qi_ref[t], h))
    kv_spec = pl.BlockSpec((tq, HEAD_PAD), lambda h, t, qi_ref, ki_ref: (ki_ref[t], h))
    w = MLA_HEADS * HEAD_PAD
    grid_spec = pltpu.PrefetchScalarGridSpec(
        num_scalar_prefetch=2, grid=(MLA_HEADS, len(qi_tab)),
        in_specs=[q_spec, kv_spec, kv_spec, q_spec, q_spec, q_spec],
        out_specs=[pl.BlockSpec((T, HEAD_PAD), lambda h, t, qi_ref, ki_ref: (0, h)), kv_spec, kv_spec])
    return pl.pallas_call(
        body, name="flash_bwd", grid_spec=grid_spec, out_shape=[jax.ShapeDtypeStruct((T, w), F32)] * 3,
        compiler_params=_cparams(("parallel", "arbitrary")),
    )(jnp.asarray(qi_tab), jnp.asarray(ki_tab), qf, kf, vf, o, do, lse)


def _mla_heads_bwd(d_out, saved, g_pad, cos_t, sin_t, first):
    d_raw, dg = [], jnp.zeros((1, HEAD_PAD), F32)
    for h in range(MLA_HEADS):
        xh, r = saved[h]
        dy = d_out[:, h * HEAD_PAD:(h + 1) * HEAD_PAD]
        dn = dy * cos_t + _rope_swap(dy * sin_t, first)
        dg = dg + jnp.sum(dn * xh, axis=0, keepdims=True)
        d_raw.append(_rms_bwd(dn * g_pad, xh, r, QK_DIM))
    return d_raw, dg


def _mla_prep_bwd(cq, ckv, kr, pos, dqf, dkf, dvf, g_qa, g_kva, g_qn, g_kn, w_uq, w_ukv, T, tm):
    def body(i, cq_ref, ckv_ref, kr_ref, pos_ref, dq_ref, dk_ref, dv_ref,
             gqa_ref, gkva_ref, gqn_ref, gkn_ref, wuq_ref, wukv_ref,
             dcq_ref, dckv_ref, dkr_ref, dqraw_ref, dkv_ref, cqn_ref, ckvn_ref,
             dgqa_ref, dgkva_ref, dgqn_ref, dgkn_ref):
        cos_t, sin_t, first = _rope_tables(pos_ref[...], tm)
        cqh, rq = _rms(cq_ref[...])
        ckvh, rkv = _rms(ckv_ref[...])
        cqn, ckvn = cqh * gqa_ref[...], ckvh * gkva_ref[...]
        q_raw, k_raw, _ = _mla_raw_heads(cqn, ckvn, kr_ref[...], wuq_ref, wukv_ref, tm)
        _, q_saved = _mla_heads_fwd(q_raw, gqn_ref[...], cos_t, sin_t, first)
        _, k_saved = _mla_heads_fwd(k_raw, gkn_ref[...], cos_t, sin_t, first)
        dq_heads, dgqn = _mla_heads_bwd(dq_ref[...], q_saved, gqn_ref[...], cos_t, sin_t, first)
        dk_heads, dgkn = _mla_heads_bwd(dk_ref[...], k_saved, gkn_ref[...], cos_t, sin_t, first)
        lane = lax.broadcasted_iota(jnp.int32, (tm, HEAD_PAD), 1)
        nope = lane < QK_NOPE
        dcqn = jnp.zeros((tm, Q_RANK), F32)
        dckvn = jnp.zeros((tm, KV_RANK), F32)
        dkr = jnp.zeros((tm, HEAD_PAD), F32)
        for h in range(MLA_HEADS):
            hs = slice(h * HEAD_PAD, (h + 1) * HEAD_PAD)
            dq_h = dq_heads[h].astype(MM)
            dkv_h = jnp.where(nope, dk_heads[h], pltpu.roll(dv_ref[:, hs], V_DIM, 1)).astype(MM)
            dqraw_ref[:, hs] = dq_h
            dkv_ref[:, hs] = dkv_h
            dcqn = dcqn + lax.dot_general(dq_h, wuq_ref[h], (((1,), (1,)), ((), ())), preferred_element_type=F32)
            dckvn = dckvn + lax.dot_general(dkv_h, wukv_ref[h], (((1,), (1,)), ((), ())), preferred_element_type=F32)
            dkr = dkr + dk_heads[h]
        dkr_ref[...] = jnp.where((lane >= QK_NOPE) & (lane < QK_DIM), dkr, 0.0)
        dcq_ref[...] = _rms_bwd(dcqn * gqa_ref[...], cqh, rq)
        dckv_ref[...] = _rms_bwd(dckvn * gkva_ref[...], ckvh, rkv)
        cqn_ref[...] = cqn.astype(MM)
        ckvn_ref[...] = ckvn.astype(MM)
        _acc(dgqa_ref, i, jnp.sum(dcqn * cqh, axis=0, keepdims=True))
        _acc(dgkva_ref, i, jnp.sum(dckvn * ckvh, axis=0, keepdims=True))
        _acc(dgqn_ref, i, dgqn)
        _acc(dgkn_ref, i, dgkn)

    w = MLA_HEADS * HEAD_PAD
    return _row_call(
        "mla_prep_bwd", body, T, tm, [cq, ckv, kr, pos, dqf, dkf, dvf], [g_qa, g_kva, g_qn, g_kn, w_uq, w_ukv],
        [(Q_RANK, F32), (KV_RANK, F32), (HEAD_PAD, F32), (w, MM), (w, MM), (Q_RANK, MM), (KV_RANK, MM)],
        [((1, Q_RANK), F32), ((1, KV_RANK), F32), ((1, HEAD_PAD), F32), ((1, HEAD_PAD), F32)], VMEM_LIMIT)


def _in_proj_bwd(x, dx1, dsecs, g_mix, w_in_pad, T, tm):
    def body(i, x_ref, dx1_ref, *rest):
        d_refs, (g_ref, w_ref, dx_ref, dp_ref, dg_ref) = rest[:len(SECTIONS)], rest[len(SECTIONS):]
        dh = jnp.zeros((tm, D_MODEL), F32)
        for (s, n), d_ref in zip(SECTIONS, d_refs):
            d = d_ref[...].astype(MM)
            dp_ref[:, s:s + n] = d
            dh = dh + lax.dot_general(d, w_ref[:, s:s + n], (((1,), (1,)), ((), ())), preferred_element_type=F32)
        xh, r = _rms(x_ref[...])
        dx_ref[...] = dx1_ref[...] + _rms_bwd(dh * g_ref[...], xh, r)
        _acc(dg_ref, i, jnp.sum(dh * xh, axis=0, keepdims=True))

    return _row_call("in_proj_bwd", body, T, tm, [x, dx1, *dsecs], [g_mix, w_in_pad],
                     [(D_MODEL, F32), (IN_PAD, MM)], [((1, D_MODEL), F32)], VMEM_LIMIT)


def _pick_block(n, cap):
    best = None
    for cand in range(128, min(n, cap) + 1, 128):
        if n % cand == 0:
            best = cand
    return n if best is None else best


def _pick_rows(n, cap):
    best = n
    for cand in range(8, min(n, cap) + 1, 8):
        if n % cand == 0:
            best = cand
    return best


def _matmul_tn(name, a, b, blocked=None):
    T, M = a.shape
    N = b.shape[1]
    bm, bk = _pick_block(M, 1408), min(512, T)
    bn = _pick_block(N, 2560) if blocked is None else blocked

    def body(a_ref, b_ref, c_ref):
        @pl.when(pl.program_id(2) == 0)
        def _():
            c_ref[...] = jnp.zeros_like(c_ref)

        c_ref[...] += _dot_tn(a_ref[...], b_ref[...])

    if blocked is None:
        out_spec = pl.BlockSpec((bm, bn), lambda i, j, k: (i, j))
        out_shape = jax.ShapeDtypeStruct((M, N), F32)
    else:
        out_spec = pl.BlockSpec((None, bm, bn), lambda i, j, k: (j, i, 0))
        out_shape = jax.ShapeDtypeStruct((N // bn, M, bn), F32)
    return pl.pallas_call(
        body, name=name, grid=(M // bm, N // bn, T // bk),
        in_specs=[pl.BlockSpec((bk, bm), lambda i, j, k: (k, i)), pl.BlockSpec((bk, bn), lambda i, j, k: (k, j))],
        out_specs=out_spec, out_shape=out_shape,
        compiler_params=_cparams(("parallel", "parallel", "arbitrary"), VMEM_LIMIT),
    )(a, b)


def _pad_gain(g, n):
    return jnp.pad(g.reshape(1, -1), ((0, 0), (0, n - g.shape[-1])))


def _cols_full(g):
    return jnp.swapaxes(g, 0, 1).reshape(g.shape[1], -1)


def _cols_blocked(full):
    r = full.shape[0]
    return jnp.swapaxes(full.reshape(r, N_DEV, -1), 0, 1)


def _local_step(x, p, pos, tgt, small, big):
    T = x.shape[0]
    tm = min(ROW_TILE, T)
    w_in = _cols_full(big["w_in"])
    zeros = lambda n: jnp.zeros((D_MODEL, n), w_in.dtype)
    w_in_pad = jnp.concatenate(
        [w_in[:, :640], zeros(QK_NOPE), w_in[:, 640:672], zeros(HEAD_PAD - QK_DIM), w_in[:, 672:]], axis=1)
    w_uq = jnp.pad(big["w_uq"], ((0, 0), (0, 0), (0, HEAD_PAD - QK_DIM)))
    w_ukv = big["w_ukv"]
    w_branch = jnp.moveaxis(big["w_branch"], 0, 2).reshape(2, HG_W, D_MODEL)
    w_bra = jnp.pad(w_branch[0].reshape(MLA_HEADS, V_DIM, D_MODEL),
                    ((0, 0), (0, HEAD_PAD - V_DIM), (0, 0))).reshape(MLA_HEADS * HEAD_PAD, D_MODEL)
    w_brb = w_branch[1]
    w_out = big["w_out"].reshape(D_MODEL, D_MODEL)
    w_g, w_u = _cols_full(big["w_ffn_gate"]), _cols_full(big["w_ffn_up"])
    w_d = big["w_ffn_down"].reshape(FFN, D_MODEL)
    w_pg, w_pp = big["w_ple_gate"].reshape(D_MODEL, D_MODEL), big["w_ple_proj"]

    g_mix, g_qa, g_kva = small["mix_norm_g"], small["q_a_norm_g"], small["kv_a_norm_g"]
    g_qn, g_kn = _pad_gain(small["q_norm_g"], HEAD_PAD), _pad_gain(small["k_norm_g"], HEAD_PAD)
    g_out, g_ffn = small["hg_out_norm_g"], small["ffn_norm_g"]
    g_pg, g_post = small["ple_gate_norm_g"], small["ple_post_norm_g"]
    logits = small["hg_lb_logits"]
    lb = _lower_bound(logits)

    h, cq, ckv, kr, hq, hf, hi, hg, bg = _in_proj_fwd(x, g_mix, w_in_pad, T, tm)
    qf, kf, vf = _mla_prep_fwd(cq, ckv, kr, pos, g_qa, g_kva, g_qn, g_kn, w_uq, w_ukv, T, tm)
    attn, lse = _flash_fwd(qf, kf, vf, T)
    o, s0 = _hgrn_fwd(hq, hf, hi, lb, T)
    x1, ya, yb, m, rec = _merge_fwd(attn, o, hg, bg, x, g_out, w_bra, w_brb, w_out, T, tm)
    x2, gt, up, h2 = _ffn_fwd(x1, g_ffn, w_g, w_u, w_d, T, tm)
    dx2, n3, dz, du, loss_p, dg_post, dg_pg = _ple_loss(x2, p, tgt, g_pg, g_post, w_pg, w_pp, T, tm)

    dx1, a, dgt, dup, dg_ffn = _ffn_bwd(dx2, x1, gt, up, g_ffn, w_g, w_u, w_d, T, tm)
    dattn, do, dhg, dbg, dya, dyb, dg_out = _merge_bwd(dx1, ya, yb, bg, o, hg, g_out, w_bra, w_brb, w_out, T, tm)
    dhq, dhf, dhi, dlb = _hgrn_bwd(hq, hf, hi, do, s0, lb, T)
    dqf, dkf, dvf = _flash_bwd(qf, kf, vf, attn, dattn, lse, T)
    (dcq, dckv, dkr, dq_raw, dkv, cqn, ckvn, dg_qa, dg_kva, dg_qn, dg_kn) = _mla_prep_bwd(
        cq, ckv, kr, pos, dqf, dkf, dvf, g_qa, g_kva, g_qn, g_kn, w_uq, w_ukv, T, tm)
    grad_x, dproj, dg_mix = _in_proj_bwd(x, dx1, [dcq, dckv, dkr, dhq, dhf, dhi, dhg, dbg], g_mix, w_in_pad, T, tm)

    d_in = _matmul_tn("dw_in", h, dproj)
    d_in = jnp.concatenate([d_in[:, :640], d_in[:, 640 + QK_NOPE:640 + QK_DIM], d_in[:, 768:]], axis=1)
    d_bra = _matmul_tn("dw_bra", attn, dya, blocked=HEAD_PAD)
    d_bra = d_bra.reshape(N_DEV, MLA_HEADS, HEAD_PAD, HEAD_PAD)[:, :, :V_DIM].reshape(N_DEV, HG_W, HEAD_PAD)
    d_brb = _matmul_tn("dw_brb", rec, dyb, blocked=HEAD_PAD)
    grads = {
        "w_in": _cols_blocked(d_in),
        "w_uq": _matmul_tn("dw_uq", cqn, dq_raw, blocked=HEAD_PAD)[:, :, :QK_DIM],
        "w_ukv": _matmul_tn("dw_ukv", ckvn, dkv, blocked=HEAD_PAD),
        "w_branch": jnp.concatenate([d_bra, d_brb], axis=1),
        "w_out": _matmul_tn("dw_out", m, dx1).reshape(N_DEV, -1, D_MODEL),
        "w_ffn_gate": _cols_blocked(_matmul_tn("dw_gate", h2, dgt)),
        "w_ffn_up": _cols_blocked(_matmul_tn("dw_up", h2, dup)),
        "w_ffn_down": _matmul_tn("dw_down", a, dx2).reshape(N_DEV, -1, D_MODEL),
        "w_ple_gate": _matmul_tn("dw_pg", n3, dz).reshape(N_DEV, -1, D_MODEL),
        "w_ple_proj": _matmul_tn("dw_pp", p, du, blocked=HEAD_PAD),
    }
    dl0 = dlb * lb * (1.0 - lb)
    small_g = {
        "mix_norm_g": dg_mix, "q_a_norm_g": dg_qa, "kv_a_norm_g": dg_kva,
        "q_norm_g": dg_qn[:, :QK_DIM], "k_norm_g": dg_kn[:, :QK_DIM],
        "hg_lb_logits": jnp.concatenate([dl0, -dl0], axis=0), "hg_out_norm_g": dg_out,
        "ffn_norm_g": dg_ffn, "ple_gate_norm_g": dg_pg, "ple_post_norm_g": dg_post,
    }
    return loss_p, grad_x, small_g, grads


def _lower_bound(logits):
    def body(l_ref, lb_ref):
        l = l_ref[...]
        mx = jnp.max(l, axis=0, keepdims=True)
        e = jnp.exp(l - mx)
        lb_ref[...] = e[0:1] / jnp.sum(e, axis=0, keepdims=True)

    return pl.pallas_call(body, name="lower_bound", out_shape=jax.ShapeDtypeStruct((1, HG_W), F32))(logits)


def _my_place():
    return lax.axis_index("x"), lax.axis_index("y"), lax.axis_index("c")


def _all_gather(name, blocks):
    n = len(blocks)

    def body(*refs):
        x_refs, out_refs = refs[:n], refs[n:2 * n]
        send_sems, recv_sems, local_sems = refs[2 * n:]
        x, y, c = _my_place()
        me, sibling = (x, y, c), (x, y, 1 - c)
        chips = [(1 - x, y), (x, 1 - y), (1 - x, 1 - y)]

        def copy(a, k, block, to, own=False):
            px, py, pc = block
            dst = out_refs[a].at[4 * px + 2 * py + pc]
            return pltpu.make_async_remote_copy(
                src_ref=x_refs[a] if own else dst, dst_ref=dst, send_sem=send_sems.at[7 * a + k],
                recv_sem=recv_sems.at[7 * a + k], device_id=to, device_id_type=MESH_ID)

        mine = [pltpu.make_async_copy(x_refs[a], out_refs[a].at[4 * x + 2 * y + c], local_sems.at[a]) for a in range(n)]
        for cp in mine:
            cp.start()
        first = []
        for a in range(n):
            first.append(copy(a, 0, me, sibling, own=True))
            first += [copy(a, 1 + j, me, (*chip, c), own=True) for j, chip in enumerate(chips)]
        for cp in first:
            cp.start()
        passed = []
        for j, chip in enumerate(chips):
            for a in range(n):
                copy(a, 1 + j, (*chip, c), me).wait_recv()
                passed.append(copy(a, 4 + j, (*chip, c), sibling))
                passed[-1].start()
        for a in range(n):
            copy(a, 0, sibling, me).wait_recv()
        for j, chip in enumerate(chips):
            for a in range(n):
                copy(a, 4 + j, (*chip, 1 - c), me).wait_recv()
        for cp in first + passed:
            cp.wait_send()
        for cp in mine:
            cp.wait()

    any_spec = pl.BlockSpec(memory_space=pl.ANY)
    return pl.pallas_call(
        body, name=name, out_shape=[jax.ShapeDtypeStruct((N_DEV,) + b.shape, b.dtype) for b in blocks],
        in_specs=[any_spec] * n, out_specs=[any_spec] * n,
        scratch_shapes=[pltpu.SemaphoreType.DMA((7 * n,)), pltpu.SemaphoreType.DMA((7 * n,)),
                        pltpu.SemaphoreType.DMA((n,))],
    )(*blocks)


def _exchange_sibling(gs):
    n = len(gs)

    def body(*refs):
        g_refs, out_refs, (send_sems, recv_sems) = refs[:n], refs[n:2 * n], refs[2 * n:]
        x, y, c = _my_place()
        copies = [pltpu.make_async_remote_copy(
            src_ref=g_refs[a].at[2 * j + 1 - c], dst_ref=out_refs[a].at[j], send_sem=send_sems.at[4 * a + j],
            recv_sem=recv_sems.at[4 * a + j], device_id=(x, y, 1 - c), device_id_type=MESH_ID)
            for a in range(n) for j in range(4)]
        for cp in copies:
            cp.start()
        for cp in copies:
            cp.wait()

    any_spec = pl.BlockSpec(memory_space=pl.ANY)
    return pl.pallas_call(
        body, name="rs_sibling", out_shape=[jax.ShapeDtypeStruct((4,) + g.shape[1:], g.dtype) for g in gs],
        in_specs=[any_spec] * n, out_specs=[any_spec] * n,
        scratch_shapes=[pltpu.SemaphoreType.DMA((4 * n,)), pltpu.SemaphoreType.DMA((4 * n,))],
    )(*gs)


def _chip_partial(name, g, got, c_idx):
    _, rows, cols = g.shape
    tr = _pick_rows(rows, 512)

    def body(c_ref, g_ref, got_ref, out_ref):
        out_ref[...] = (g_ref[...] + got_ref[...]).astype(MM)

    grid_spec = pltpu.PrefetchScalarGridSpec(
        num_scalar_prefetch=1, grid=(4, rows // tr),
        in_specs=[pl.BlockSpec((1, tr, cols), lambda j, i, c_ref: (2 * j + c_ref[0], i, 0)),
                  pl.BlockSpec((1, tr, cols), lambda j, i, c_ref: (j, i, 0))],
        out_specs=pl.BlockSpec((1, tr, cols), lambda j, i, c_ref: (j, i, 0)))
    return pl.pallas_call(
        body, name=name, grid_spec=grid_spec, out_shape=jax.ShapeDtypeStruct((4, rows, cols), MM),
        compiler_params=_cparams(("parallel", "parallel")),
    )(c_idx, g, got)


def _exchange_chips(parts):
    n = len(parts)

    def body(*refs):
        p_refs, out_refs, (send_sems, recv_sems) = refs[:n], refs[n:2 * n], refs[2 * n:]
        x, y, c = _my_place()
        chips = [(1 - x, y), (x, 1 - y), (1 - x, 1 - y)]
        copies = [pltpu.make_async_remote_copy(
            src_ref=p_refs[a].at[2 * px + py], dst_ref=out_refs[a].at[k], send_sem=send_sems.at[3 * a + k],
            recv_sem=recv_sems.at[3 * a + k], device_id=(px, py, c), device_id_type=MESH_ID)
            for a in range(n) for k, (px, py) in enumerate(chips)]
        for cp in copies:
            cp.start()
        for cp in copies:
            cp.wait()

    any_spec = pl.BlockSpec(memory_space=pl.ANY)
    return pl.pallas_call(
        body, name="rs_chips", out_shape=[jax.ShapeDtypeStruct((3,) + p.shape[1:], p.dtype) for p in parts],
        in_specs=[any_spec] * n, out_specs=[any_spec] * n,
        scratch_shapes=[pltpu.SemaphoreType.DMA((3 * n,)), pltpu.SemaphoreType.DMA((3 * n,))],
    )(*parts)


def _adamw_math(w, g, m, v):
    m = ADAM_B1 * m + (1.0 - ADAM_B1) * g
    v = ADAM_B2 * v + (1.0 - ADAM_B2) * jnp.square(g)
    m_hat = m / (1.0 - ADAM_B1 ** ADAM_STEP)
    v_hat = v / (1.0 - ADAM_B2 ** ADAM_STEP)
    delta = -ADAM_LR * (m_hat / (jnp.sqrt(v_hat) + ADAM_EPS) + ADAM_WD * w)
    return delta, m, v


def _sum_adamw(name, g, sib, got, w, m, v, slot_idx, chip_idx):
    _, rows, cols = g.shape
    tr = _pick_rows(rows, 256)

    def body(s_ref, j_ref, g_ref, sib_ref, got_ref, w_ref, m_ref, v_ref, go_ref, d_ref, m2_ref, v2_ref):
        grad = g_ref[0] + sib_ref[0]
        for k in range(3):
            grad = grad + got_ref[k].astype(F32)
        go_ref[...] = grad
        d_ref[...], m2_ref[...], v2_ref[...] = _adamw_math(w_ref[...], grad, m_ref[...], v_ref[...])

    flat = pl.BlockSpec((tr, cols), lambda i, s_ref, j_ref: (i, 0))
    grid_spec = pltpu.PrefetchScalarGridSpec(
        num_scalar_prefetch=2, grid=(rows // tr,),
        in_specs=[pl.BlockSpec((1, tr, cols), lambda i, s_ref, j_ref: (s_ref[0], i, 0)),
                  pl.BlockSpec((1, tr, cols), lambda i, s_ref, j_ref: (j_ref[0], i, 0)),
                  pl.BlockSpec((3, tr, cols), lambda i, s_ref, j_ref: (0, i, 0)), flat, flat, flat],
        out_specs=[flat] * 4)
    return pl.pallas_call(
        body, name=name, grid_spec=grid_spec, out_shape=[jax.ShapeDtypeStruct((rows, cols), F32)] * 4,
        compiler_params=_cparams(("parallel",)),
    )(slot_idx, chip_idx, g, sib, got, w, m, v)


def _adamw_small(parts, w, m, v):
    rows = w.shape[0]

    def body(p_ref, w_ref, m_ref, v_ref, g_ref, d_ref, m2_ref, v2_ref):
        g = p_ref[0]
        for d in range(1, N_DEV):
            g = g + p_ref[d]
        g_ref[...] = g
        d_ref[...], m2_ref[...], v2_ref[...] = _adamw_math(w_ref[...], g, m_ref[...], v_ref[...])

    return pl.pallas_call(
        body, name="adamw_small", out_shape=[jax.ShapeDtypeStruct((rows, 128), F32)] * 4,
    )(parts, w, m, v)


BIG = ("w_in", "w_uq", "w_ukv", "w_branch", "w_out", "w_ffn_gate", "w_ffn_up", "w_ffn_down", "w_ple_gate", "w_ple_proj")
SMALL = (
    ("mix_norm_g", 1024), ("q_a_norm_g", 384), ("kv_a_norm_g", 256), ("q_norm_g", 96), ("k_norm_g", 96),
    ("hg_lb_logits", 1024), ("hg_out_norm_g", 128), ("ffn_norm_g", 1024), ("ple_gate_norm_g", 1024),
    ("ple_post_norm_g", 1024),
)
SMALL_ROWS = 56


def _pack_small(vals):
    rows = []
    for name, n in SMALL:
        v = vals[name].reshape(1, -1).astype(F32)
        rows.append(jnp.pad(v, ((0, 0), (0, (-n) % 128))).reshape(-1, 128))
    return jnp.concatenate(rows, axis=0)


def _unpack_small(packed, shapes):
    out, r = {}, 0
    for name, n in SMALL:
        k = (n + 127) // 128
        out[name] = packed[r:r + k].reshape(1, -1)[:, :n].reshape(shapes[name])
        r += k
    return out


_WEIGHTS = ["mix_norm_g", "w_in", "q_a_norm_g", "w_uq", "kv_a_norm_g", "w_ukv", "q_norm_g", "k_norm_g", "hg_lb_logits",
            "hg_out_norm_g", "w_branch", "w_out", "ffn_norm_g", "w_ffn_gate", "w_ffn_up", "w_ffn_down",
            "ple_gate_norm_g", "w_ple_gate", "w_ple_proj", "ple_post_norm_g"]


def _step(x, p, positions, tgt, w, m, v):
    small_names = [n for n, _ in SMALL]
    T = x.shape[1]
    px, py, pc = _my_place()
    as_idx = lambda t: jnp.reshape(t, (1,)).astype(jnp.int32)
    two_d = lambda t: t.reshape(-1, t.shape[-1])

    gathered = _all_gather("ag_weights", [two_d(w[n]).astype(MM) for n in BIG])
    big = dict(zip(BIG, gathered))
    big["w_branch"] = big["w_branch"].reshape((N_DEV,) + w["w_branch"].shape[1:])
    small = {n: (w[n] if n == "hg_lb_logits" else w[n].reshape(1, -1)) for n in small_names}

    loss_p, grad_x, small_g, grads = _local_step(
        x[0], p[0, 0], positions.reshape(T, 1), tgt[0], small, big)

    gs = [grads[n] for n in BIG]
    sibs = _exchange_sibling(gs)
    parts = [_chip_partial("rs_partial_" + n, g, s, as_idx(pc)) for n, g, s in zip(BIG, gs, sibs)]
    gots = _exchange_chips(parts)
    out_g, out_d, out_m, out_v = {}, {}, {}, {}
    for n, g, s, got in zip(BIG, gs, sibs, gots):
        res = _sum_adamw("adamw_" + n, g, s, got, two_d(w[n]), two_d(m[n]), two_d(v[n]),
                         as_idx(4 * px + 2 * py + pc), as_idx(2 * px + py))
        out_g[n], out_d[n], out_m[n], out_v[n] = [r.reshape(w[n].shape) for r in res]

    packed_g = _pack_small(small_g)
    loss_row = jnp.concatenate([jnp.pad(jnp.sum(loss_p).reshape(1, 1), ((0, 0), (0, 127))),
                                jnp.zeros((SMALL_ROWS - packed_g.shape[0] - 1, 128), F32)], axis=0)
    parts = _all_gather("ag_small", [jnp.concatenate([packed_g, loss_row], axis=0)])[0]
    pad_rows = lambda t: jnp.pad(t, ((0, SMALL_ROWS - t.shape[0]), (0, 0)))
    sw = pad_rows(_pack_small({n: w[n] for n in small_names}))
    sm = pad_rows(_pack_small({n: m[n] for n in small_names}))
    sv = pad_rows(_pack_small({n: v[n] for n in small_names}))
    g_s, d_s, m_s, v_s = _adamw_small(parts, sw, sm, sv)
    shapes = {n: w[n].shape for n in small_names}
    n_packed = packed_g.shape[0]
    loss = g_s[n_packed, 0]
    for src, dst in ((g_s, out_g), (d_s, out_d), (m_s, out_m), (v_s, out_v)):
        dst.update(_unpack_small(src, shapes))

    outs = [loss, grad_x[None]]
    for table in (out_g, out_d, out_m, out_v):
        outs += [table[n] for n in _WEIGHTS]
    return tuple(outs)


def kernel(x, p, positions, mix_norm_g, w_in, q_a_norm_g, w_uq, kv_a_norm_g, w_ukv, q_norm_g, k_norm_g, hg_lb_logits, hg_out_norm_g, w_branch, w_out, ffn_norm_g, w_ffn_gate, w_ffn_up, w_ffn_down, ple_gate_norm_g, w_ple_gate, w_ple_proj, ple_post_norm_g, loss_target, m_mix_norm_g, m_w_in, m_q_a_norm_g, m_w_uq, m_kv_a_norm_g, m_w_ukv, m_q_norm_g, m_k_norm_g, m_hg_lb_logits, m_hg_out_norm_g, m_w_branch, m_w_out, m_ffn_norm_g, m_w_ffn_gate, m_w_ffn_up, m_w_ffn_down, m_ple_gate_norm_g, m_w_ple_gate, m_w_ple_proj, m_ple_post_norm_g, v_mix_norm_g, v_w_in, v_q_a_norm_g, v_w_uq, v_kv_a_norm_g, v_w_ukv, v_q_norm_g, v_k_norm_g, v_hg_lb_logits, v_hg_out_norm_g, v_w_branch, v_w_out, v_ffn_norm_g, v_w_ffn_gate, v_w_ffn_up, v_w_ffn_down, v_ple_gate_norm_g, v_w_ple_gate, v_w_ple_proj, v_ple_post_norm_g):
    w = dict(mix_norm_g=mix_norm_g, w_in=w_in, q_a_norm_g=q_a_norm_g, w_uq=w_uq, kv_a_norm_g=kv_a_norm_g, w_ukv=w_ukv,
             q_norm_g=q_norm_g, k_norm_g=k_norm_g, hg_lb_logits=hg_lb_logits, hg_out_norm_g=hg_out_norm_g,
             w_branch=w_branch, w_out=w_out, ffn_norm_g=ffn_norm_g, w_ffn_gate=w_ffn_gate, w_ffn_up=w_ffn_up,
             w_ffn_down=w_ffn_down, ple_gate_norm_g=ple_gate_norm_g, w_ple_gate=w_ple_gate, w_ple_proj=w_ple_proj,
             ple_post_norm_g=ple_post_norm_g)
    m = dict(mix_norm_g=m_mix_norm_g, w_in=m_w_in, q_a_norm_g=m_q_a_norm_g, w_uq=m_w_uq, kv_a_norm_g=m_kv_a_norm_g,
             w_ukv=m_w_ukv, q_norm_g=m_q_norm_g, k_norm_g=m_k_norm_g, hg_lb_logits=m_hg_lb_logits,
             hg_out_norm_g=m_hg_out_norm_g, w_branch=m_w_branch, w_out=m_w_out, ffn_norm_g=m_ffn_norm_g,
             w_ffn_gate=m_w_ffn_gate, w_ffn_up=m_w_ffn_up, w_ffn_down=m_w_ffn_down,
             ple_gate_norm_g=m_ple_gate_norm_g, w_ple_gate=m_w_ple_gate, w_ple_proj=m_w_ple_proj,
             ple_post_norm_g=m_ple_post_norm_g)
    v = dict(mix_norm_g=v_mix_norm_g, w_in=v_w_in, q_a_norm_g=v_q_a_norm_g, w_uq=v_w_uq, kv_a_norm_g=v_kv_a_norm_g,
             w_ukv=v_w_ukv, q_norm_g=v_q_norm_g, k_norm_g=v_k_norm_g, hg_lb_logits=v_hg_lb_logits,
             hg_out_norm_g=v_hg_out_norm_g, w_branch=v_w_branch, w_out=v_w_out, ffn_norm_g=v_ffn_norm_g,
             w_ffn_gate=v_w_ffn_gate, w_ffn_up=v_w_ffn_up, w_ffn_down=v_w_ffn_down,
             ple_gate_norm_g=v_ple_gate_norm_g, w_ple_gate=v_w_ple_gate, w_ple_proj=v_w_ple_proj,
             ple_post_norm_g=v_ple_post_norm_g)
    return _step(x, p, positions, loss_target, w, m, v)
```

```python
import functools

import jax
import jax.numpy as jnp
import numpy as np
from jax import lax
from jax.experimental import pallas as pl
from jax.experimental.pallas import tpu as pltpu

F32 = jnp.float32
MM = jnp.bfloat16
HI = lax.Precision.HIGHEST
MESH_ID = pl.DeviceIdType.MESH

D_MODEL = 1024
N_DEV = 8
MLA_HEADS = 8
QK_NOPE = 64
QK_ROPE = 32
QK_DIM = 96
V_DIM = 64
HEAD_PAD = 128
Q_RANK = 384
KV_RANK = 256
ROPE_BASE = 10000.0
HG_HEADS = 4
HG_DIM = 128
HG_W = 512
HG_CHUNK = 64
HG_SUB = 16
FFN = 2816
PLE = 256
EPS = 1e-6
ATT_SCALE = QK_DIM ** -0.5
NEG = -1e30

ADAM_LR = 0.001
ADAM_B1 = 0.9
ADAM_B2 = 0.999
ADAM_EPS = 1e-08
ADAM_WD = 0.01
ADAM_STEP = 10

SEC_CQ = (0, 384)
SEC_CKV = (384, 256)
SEC_KR = (640, 128)
SEC_HQ = (768, 512)
SEC_HF = (1280, 512)
SEC_HI = (1792, 512)
SEC_HG = (2304, 512)
SEC_BG = (2816, 2048)
IN_PAD = 4864
SECTIONS = (SEC_CQ, SEC_CKV, SEC_KR, SEC_HQ, SEC_HF, SEC_HI, SEC_HG, SEC_BG)

VMEM_LIMIT = 58 * 1024 * 1024
ROW_TILE = 256
ATT_TILE = 512
ATT_HEADS = 2
HG_BLOCK = 512


def _dot(a, b):
    return jnp.dot(a.astype(MM), b.astype(MM), preferred_element_type=F32)


def _dot_nt(a, b):
    return lax.dot_general(a.astype(MM), b.astype(MM), (((1,), (1,)), ((), ())), preferred_element_type=F32)


def _dot_tn(a, b):
    return lax.dot_general(a.astype(MM), b.astype(MM), (((0,), (0,)), ((), ())), preferred_element_type=F32)


def _dot_hi(a, b):
    return jnp.dot(a, b, preferred_element_type=F32, precision=HI)


def _sigmoid(x):
    return 1.0 / (1.0 + jnp.exp(-x))


def _rms(x, n=None):
    n = x.shape[-1] if n is None else n
    r = lax.rsqrt(jnp.sum(x * x, axis=-1, keepdims=True) * (1.0 / n) + EPS)
    return x * r, r


def _rms_bwd(dxh, xh, r, n=None):
    n = xh.shape[-1] if n is None else n
    return r * (dxh - xh * (jnp.sum(dxh * xh, axis=-1, keepdims=True) * (1.0 / n)))


def _rope_tables(pos, tm):
    lane = lax.broadcasted_iota(jnp.int32, (tm, HEAD_PAD), 1)
    idx = jnp.where(lane < QK_NOPE + QK_ROPE // 2, lane - QK_NOPE, lane - QK_NOPE - QK_ROPE // 2)
    inv = jnp.exp(idx.astype(F32) * (-np.log(ROPE_BASE) * 2.0 / QK_ROPE))
    ang = pos.astype(F32) * inv
    in_rope = (lane >= QK_NOPE) & (lane < QK_DIM)
    first = lane < QK_NOPE + QK_ROPE // 2
    cos_t = jnp.where(in_rope, jnp.cos(ang), 1.0)
    sin_t = jnp.where(in_rope, jnp.where(first, -jnp.sin(ang), jnp.sin(ang)), 0.0)
    return cos_t, sin_t, (first, in_rope)


def _rope_swap(x, halves):
    first, in_rope = halves
    half = QK_ROPE // 2
    return jnp.where(in_rope, jnp.where(first, pltpu.roll(x, HEAD_PAD - half, 1), pltpu.roll(x, half, 1)), 0.0)


def _cparams(sem, vmem=None):
    return pltpu.CompilerParams(dimension_semantics=sem, vmem_limit_bytes=vmem)


def _row_call(name, body, T, tm, row_ins, full_ins, row_outs, acc_outs, vmem=None):
    def kern(*refs):
        body(pl.program_id(0), *refs)

    in_specs = [pl.BlockSpec((tm, a.shape[1]), lambda i: (i, 0)) for a in row_ins]
    in_specs += [pl.BlockSpec(a.shape, lambda i, nd=a.ndim: (0,) * nd, pipeline_mode=pl.Buffered(1)) for a in full_ins]
    out_specs = [pl.BlockSpec((tm, n), lambda i: (i, 0)) for n, _ in row_outs]
    out_specs += [pl.BlockSpec(s, lambda i, nd=len(s): (0,) * nd) for s, _ in acc_outs]
    out_shape = [jax.ShapeDtypeStruct((T, n), dt) for n, dt in row_outs]
    out_shape += [jax.ShapeDtypeStruct(s, dt) for s, dt in acc_outs]
    return pl.pallas_call(
        kern, name=name, grid=(T // tm,), in_specs=in_specs, out_specs=out_specs, out_shape=out_shape,
        compiler_params=_cparams(("arbitrary",), vmem),
    )(*row_ins, *full_ins)


def _acc(ref, i, val):
    @pl.when(i == 0)
    def _():
        ref[...] = val

    @pl.when(i != 0)
    def _():
        ref[...] += val


def _in_proj_fwd(x, g_mix, w_in_pad, T, tm):
    def body(i, x_ref, g_ref, w_ref, h_ref, *outs):
        xh, _ = _rms(x_ref[...])
        h = (xh * g_ref[...]).astype(MM)
        h_ref[...] = h
        for (s, n), o_ref in zip(SECTIONS, outs):
            o_ref[...] = jnp.dot(h, w_ref[:, s:s + n], preferred_element_type=F32)

    row_outs = [(D_MODEL, MM)] + [(n, F32) for _, n in SECTIONS]
    return _row_call("in_proj_fwd", body, T, tm, [x], [g_mix, w_in_pad], row_outs, [], VMEM_LIMIT)


def _mla_heads_fwd(raw, g_pad, cos_t, sin_t, first):
    outs, saved = [], []
    for h in range(MLA_HEADS):
        xh, r = _rms(raw[:, h * HEAD_PAD:(h + 1) * HEAD_PAD], QK_DIM)
        y = xh * g_pad
        outs.append(y * cos_t + _rope_swap(y, first) * sin_t)
        saved.append((xh, r))
    return outs, saved


def _mla_raw_heads(cqn, ckvn, kr, wuq_ref, wukv_ref, tm):
    lane = lax.broadcasted_iota(jnp.int32, (tm, HEAD_PAD), 1)
    nope = lane < QK_NOPE
    one_lane = jnp.where(lane == V_DIM, 1.0, 0.0)
    qs, ks, vs = [], [], []
    for h in range(MLA_HEADS):
        qs.append(_dot(cqn, wuq_ref[h]))
        kv = _dot(ckvn, wukv_ref[h])
        ks.append(jnp.where(nope, kv, kr))
        vs.append(jnp.where(nope, pltpu.roll(kv, V_DIM, 1), one_lane))
    return jnp.concatenate(qs, axis=1), jnp.concatenate(ks, axis=1), jnp.concatenate(vs, axis=1)


def _mla_prep_fwd(cq, ckv, kr, pos, g_qa, g_kva, g_qn, g_kn, w_uq, w_ukv, T, tm):
    def body(i, cq_ref, ckv_ref, kr_ref, pos_ref, gqa_ref, gkva_ref, gqn_ref, gkn_ref, wuq_ref, wukv_ref,
             q_ref, k_ref, v_ref):
        cos_t, sin_t, first = _rope_tables(pos_ref[...], tm)
        cqn = _rms(cq_ref[...])[0] * gqa_ref[...]
        ckvn = _rms(ckv_ref[...])[0] * gkva_ref[...]
        q_raw, k_raw, v = _mla_raw_heads(cqn, ckvn, kr_ref[...], wuq_ref, wukv_ref, tm)
        qs, _ = _mla_heads_fwd(q_raw, gqn_ref[...], cos_t, sin_t, first)
        ks, _ = _mla_heads_fwd(k_raw, gkn_ref[...], cos_t, sin_t, first)
        q_ref[...] = (jnp.concatenate(qs, axis=1) * ATT_SCALE).astype(MM)
        k_ref[...] = jnp.concatenate(ks, axis=1).astype(MM)
        v_ref[...] = v.astype(MM)

    w = MLA_HEADS * HEAD_PAD
    return _row_call("mla_prep_fwd", body, T, tm, [cq, ckv, kr, pos], [g_qa, g_kva, g_qn, g_kn, w_uq, w_ukv],
                     [(w, MM), (w, MM), (w, MM)], [])


def _causal_pairs(n, by_query):
    if by_query:
        pairs = [(q, k) for q in range(n) for k in range(q + 1)]
    else:
        pairs = [(q, k) for k in range(n) for q in range(k, n)]
    return np.array([p[0] for p in pairs], np.int32), np.array([p[1] for p in pairs], np.int32)


def _flash_fwd(qf, kf, vf, T):
    tq = min(ATT_TILE, T)
    nq = T // tq

    qi_tab, ki_tab = _causal_pairs(nq, by_query=True)

    hp = ATT_HEADS

    def body(qi_ref, ki_ref, q_ref, k_ref, v_ref, o_ref, lse_ref, m_s, acc_s):
        t = pl.program_id(1)
        qi, ki = qi_ref[t], ki_ref[t]

        @pl.when(ki == 0)
        def _():
            m_s[...] = jnp.full_like(m_s, NEG)
            acc_s[...] = jnp.zeros_like(acc_s)

        def step(masked):
            for hh in range(hp):
                hs = slice(hh * HEAD_PAD, (hh + 1) * HEAD_PAD)
                s_t = _dot_nt(k_ref[:, hs], q_ref[:, hs])
                if masked:
                    key = lax.broadcasted_iota(jnp.int32, (tq, tq), 0)
                    qry = lax.broadcasted_iota(jnp.int32, (tq, tq), 1)
                    s_t = jnp.where(key <= qry, s_t, NEG)
                m_old = m_s[hh]
                m_new = jnp.maximum(m_old, jnp.max(s_t, axis=0, keepdims=True))
                p_t = jnp.exp(s_t - m_new)
                acc_s[hh] = jnp.exp(m_old - m_new) * acc_s[hh] + _dot_tn(v_ref[:, hs], p_t)
                m_s[hh] = m_new

        @pl.when(ki < qi)
        def _():
            step(False)

        @pl.when(ki == qi)
        def _():
            step(True)
            real = lax.broadcasted_iota(jnp.int32, (HEAD_PAD, tq), 0) < V_DIM
            for hh in range(hp):
                hs = slice(hh * HEAD_PAD, (hh + 1) * HEAD_PAD)
                acc = acc_s[hh]
                l = acc[V_DIM:V_DIM + 1]
                o_ref[:, hs] = jnp.where(real, acc / l, 0.0).T
                lse_ref[:, hs] = jnp.broadcast_to(m_s[hh] + jnp.log(l), (HEAD_PAD, tq)).T

    q_spec = pl.BlockSpec((tq, hp * HEAD_PAD), lambda h, t, qi_ref, ki_ref: (qi_ref[t], h))
    kv_spec = pl.BlockSpec((tq, hp * HEAD_PAD), lambda h, t, qi_ref, ki_ref: (ki_ref[t], h))
    grid_spec = pltpu.PrefetchScalarGridSpec(
        num_scalar_prefetch=2, grid=(MLA_HEADS // hp, len(qi_tab)),
        in_specs=[q_spec, kv_spec, kv_spec], out_specs=[q_spec, q_spec],
        scratch_shapes=[pltpu.VMEM((hp, 1, tq), F32), pltpu.VMEM((hp, HEAD_PAD, tq), F32)])
    return pl.pallas_call(
        body, name="flash_fwd", grid_spec=grid_spec,
        out_shape=[jax.ShapeDtypeStruct((T, MLA_HEADS * HEAD_PAD), F32)] * 2,
        compiler_params=_cparams(("parallel", "arbitrary")),
    )(jnp.asarray(qi_tab), jnp.asarray(ki_tab), qf, kf, vf)


def _hg_gates(hf, lb):
    sg = _sigmoid(hf)
    f = lb + (1.0 - lb) * sg
    return sg, f, jnp.log(f), 1.0 - f


def _tri(n, lower):
    r = lax.broadcasted_iota(jnp.int32, (n, n), 0)
    c = lax.broadcasted_iota(jnp.int32, (n, n), 1)
    return jnp.where((c <= r) if lower else (c >= r), 1.0, 0.0).astype(F32)


def _hg_intra(q, k, b):
    C, S = HG_CHUNK, HG_SUB
    row_c = lax.broadcasted_iota(jnp.int32, (C, HG_DIM), 0)
    row_s = lax.broadcasted_iota(jnp.int32, (S, HG_DIM), 0)
    lane_c = lax.broadcasted_iota(jnp.int32, (S, C), 1)
    blocks, saved = [], []
    for blk in range(C // S):
        lo = blk * S
        q_b, k_b, b_b = q[lo:lo + S], k[lo:lo + S], b[lo:lo + S]
        a_b = jnp.zeros((S, C), F32)
        for j in range(S):
            w = jnp.exp(jnp.where(row_s >= j, b_b - b_b[j:j + 1], NEG))
            col = jnp.sum(q_b * (k_b[j:j + 1] * w), axis=1, keepdims=True)
            a_b = jnp.where(lane_c == lo + j, col, a_b)
        if blk > 0:
            ref = b[lo - 1:lo]
            q_e = jnp.exp(b_b - ref)
            q_t = q_b * q_e
            k_e = jnp.exp(jnp.where(row_c < lo, ref - b, NEG))
            a_b = a_b + _dot_nt(q_t, k * k_e)
            saved.append((q_t, k_e, q_e))
        else:
            saved.append(None)
        blocks.append(a_b)
    return jnp.concatenate(blocks, axis=0), saved


def _hgrn_fwd(hq, hf, hi, lb, T):
    rb = min(HG_BLOCK, T)
    ncb = rb // HG_CHUNK

    def body(hq_ref, hf_ref, hi_ref, lb_ref, o_ref, s0_ref, st_ref):
        @pl.when(pl.program_id(0) == 0)
        def _():
            st_ref[...] = jnp.zeros_like(st_ref)

        tril = _tri(HG_CHUNK, True)

        def chunk(c, carry):
            rows = pl.ds(pl.multiple_of(c * HG_CHUNK, HG_CHUNK), HG_CHUNK)
            _, _, logf, kk = _hg_gates(hf_ref[rows, :], lb_ref[...])
            b = _dot_hi(tril, logf)
            q_all, v_all = hq_ref[rows, :], hi_ref[rows, :]
            outs = []
            for h in range(HG_HEADS):
                ls = slice(h * HG_DIM, (h + 1) * HG_DIM)
                q, k, v, bh = q_all[:, ls], kk[:, ls], v_all[:, ls], b[:, ls]
                st = st_ref[h]
                s0_ref[c, h * HG_DIM:(h + 1) * HG_DIM, :] = st
                b_end = bh[HG_CHUNK - 1:HG_CHUNK]
                a, _ = _hg_intra(q, k, bh)
                outs.append(_dot_nt(q * jnp.exp(bh), st) + _dot(a, v))
                st_ref[h] = st * jnp.exp(b_end) + _dot_tn(v, k * jnp.exp(b_end - bh))
            o_ref[rows, :] = jnp.concatenate(outs, axis=1)
            return carry

        lax.fori_loop(0, ncb, chunk, 0)

    row = pl.BlockSpec((rb, HG_W), lambda i: (i, 0))
    return pl.pallas_call(
        body, name="hgrn_fwd", grid=(T // rb,),
        in_specs=[row, row, row, pl.BlockSpec((1, HG_W), lambda i: (0, 0))],
        out_specs=[row, pl.BlockSpec((ncb, HG_W, HG_DIM), lambda i: (i, 0, 0))],
        out_shape=[jax.ShapeDtypeStruct((T, HG_W), F32), jax.ShapeDtypeStruct((T // HG_CHUNK, HG_W, HG_DIM), F32)],
        scratch_shapes=[pltpu.VMEM((HG_HEADS, HG_DIM, HG_DIM), F32)],
        compiler_params=_cparams(("arbitrary",)),
    )(hq, hf, hi, lb)


def _hgrn_bwd(hq, hf, hi, do, s0, lb, T):
    rb = min(HG_BLOCK, T)
    ncb = rb // HG_CHUNK
    nb = T // rb
    C, S = HG_CHUNK, HG_SUB

    def body(hq_ref, hf_ref, hi_ref, do_ref, s0_ref, lb_ref, dq_ref, df_ref, dv_ref, dlb_ref, dst_ref):
        @pl.when(pl.program_id(0) == 0)
        def _():
            dst_ref[...] = jnp.zeros_like(dst_ref)
            dlb_ref[...] = jnp.zeros_like(dlb_ref)

        tril, triu = _tri(C, True), _tri(C, False)
        row_cc = lax.broadcasted_iota(jnp.int32, (C, C), 0)
        col_cc = lax.broadcasted_iota(jnp.int32, (C, C), 1)
        row_s = lax.broadcasted_iota(jnp.int32, (S, HG_DIM), 0)
        lane_sc = lax.broadcasted_iota(jnp.int32, (S, C), 1)
        last_row = lax.broadcasted_iota(jnp.int32, (C, HG_DIM), 0) == C - 1
        lb_v = lb_ref[...]

        def chunk(cc, carry):
            c = ncb - 1 - cc
            rows = pl.ds(pl.multiple_of(c * C, C), C)
            hf_c = hf_ref[rows, :]
            sg, f, logf, kk = _hg_gates(hf_c, lb_v)
            b = _dot_hi(tril, logf)
            q_all, v_all, do_all = hq_ref[rows, :], hi_ref[rows, :], do_ref[rows, :]
            dq_o, dk_o, dv_o, db_o = [], [], [], []
            for h in range(HG_HEADS):
                ls = slice(h * HG_DIM, (h + 1) * HG_DIM)
                q, k, v, bh, d_o = q_all[:, ls], kk[:, ls], v_all[:, ls], b[:, ls], do_all[:, ls]
                st0 = s0_ref[c, h * HG_DIM:(h + 1) * HG_DIM, :]
                dst = dst_ref[h]
                b_end = bh[C - 1:C]
                e_b, e_end = jnp.exp(bh), jnp.exp(b_end)
                e_rem = jnp.exp(b_end - bh)
                qe, kd = q * e_b, k * e_rem
                st_end = st0 * e_end + _dot_tn(v, kd)
                a, saved = _hg_intra(q, k, bh)
                d_a = jnp.where(col_cc <= row_cc, _dot_nt(d_o, v), 0.0)
                dv = _dot_tn(a, d_o) + _dot_nt(kd, dst)
                dq = e_b * _dot(d_o, st0)
                dk = e_rem * _dot(v, dst)
                dq_blocks, dk_diag = [], []
                for blk in range(C // S):
                    lo = blk * S
                    q_b, k_b, b_b = q[lo:lo + S], k[lo:lo + S], bh[lo:lo + S]
                    da_b = d_a[lo:lo + S]
                    dq_b = jnp.zeros((S, HG_DIM), F32)
                    dk_b = jnp.zeros((S, HG_DIM), F32)
                    for j in range(S):
                        w = jnp.exp(jnp.where(row_s >= j, b_b - b_b[j:j + 1], NEG))
                        col = jnp.sum(jnp.where(lane_sc == lo + j, da_b, 0.0), axis=1, keepdims=True)
                        dq_b = dq_b + col * (k_b[j:j + 1] * w)
                        dk_row = jnp.sum(col * (q_b * w), axis=0, keepdims=True)
                        dk_b = jnp.where(row_s == j, dk_row, dk_b)
                    if blk > 0:
                        q_t, k_e, q_e = saved[blk]
                        da_off = jnp.where(lane_sc < lo, da_b, 0.0)
                        dq_b = dq_b + _dot(da_off, k * k_e) * q_e
                        dk = dk + _dot_tn(da_off, q_t) * k_e
                    dq_blocks.append(dq_b)
                    dk_diag.append(dk_b)
                dq = dq + jnp.concatenate(dq_blocks, axis=0)
                dk = dk + jnp.concatenate(dk_diag, axis=0)
                extra = jnp.sum(dst * st_end, axis=0, keepdims=True)
                db_o.append(q * dq - k * dk + jnp.where(last_row, extra, 0.0))
                dst_ref[h] = dst * e_end + _dot_tn(d_o, qe)
                dq_o.append(dq)
                dk_o.append(dk)
                dv_o.append(dv)
            dlogf = _dot_hi(triu, jnp.concatenate(db_o, axis=1))
            d_f = dlogf / f - jnp.concatenate(dk_o, axis=1)
            dq_ref[rows, :] = jnp.concatenate(dq_o, axis=1)
            dv_ref[rows, :] = jnp.concatenate(dv_o, axis=1)
            df_ref[rows, :] = d_f * (1.0 - lb_v) * sg * (1.0 - sg)
            dlb_ref[...] += jnp.sum(d_f * (1.0 - sg), axis=0, keepdims=True)
            return carry

        lax.fori_loop(0, ncb, chunk, 0)

    row = pl.BlockSpec((rb, HG_W), lambda i: (nb - 1 - i, 0))
    one = pl.BlockSpec((1, HG_W), lambda i: (0, 0))
    return pl.pallas_call(
        body, name="hgrn_bwd", grid=(nb,),
        in_specs=[row, row, row, row, pl.BlockSpec((ncb, HG_W, HG_DIM), lambda i: (nb - 1 - i, 0, 0)), one],
        out_specs=[row, row, row, one],
        out_shape=[jax.ShapeDtypeStruct((T, HG_W), F32)] * 3 + [jax.ShapeDtypeStruct((1, HG_W), F32)],
        scratch_shapes=[pltpu.VMEM((HG_HEADS, HG_DIM, HG_DIM), F32)],
        compiler_params=_cparams(("arbitrary",)),
    )(hq, hf, hi, do, s0, lb)


def _silu_parts(x):
    sg = _sigmoid(x)
    return x * sg, sg * (1.0 + x * (1.0 - sg))


def _merge_fwd(attn, o, hg, bg, x, g_out, w_bra, w_brb, w_out, T, tm):
    def body(i, attn_ref, o_ref, hg_ref, bg_ref, x_ref, g_ref, wa_ref, wb_ref, wo_ref,
             x1_ref, ya_ref, yb_ref, m_ref, rec_ref):
        recs = []
        for h in range(HG_HEADS):
            ls = slice(h * HG_DIM, (h + 1) * HG_DIM)
            oh, _ = _rms(o_ref[:, ls])
            recs.append(oh * g_ref[...] * _silu_parts(hg_ref[:, ls])[0])
        rec = jnp.concatenate(recs, axis=1)
        ya = _dot(attn_ref[...], wa_ref[...])
        yb = _dot(rec, wb_ref[...])
        m = _sigmoid(bg_ref[:, :D_MODEL]) * ya + _sigmoid(bg_ref[:, D_MODEL:]) * yb
        x1_ref[...] = x_ref[...] + _dot(m, wo_ref[...])
        ya_ref[...] = ya
        yb_ref[...] = yb
        m_ref[...] = m.astype(MM)
        rec_ref[...] = rec.astype(MM)

    return _row_call("merge_fwd", body, T, tm, [attn, o, hg, bg, x], [g_out, w_bra, w_brb, w_out],
                     [(D_MODEL, F32), (D_MODEL, F32), (D_MODEL, F32), (D_MODEL, MM), (HG_W, MM)], [], VMEM_LIMIT)


def _ffn_fwd(x1, g_ffn, w_g, w_u, w_d, T, tm):
    def body(i, x1_ref, g_ref, wg_ref, wu_ref, wd_ref, x2_ref, gt_ref, up_ref, h2_ref):
        x1 = x1_ref[...]
        h2 = (_rms(x1)[0] * g_ref[...]).astype(MM)
        gt = jnp.dot(h2, wg_ref[...], preferred_element_type=F32)
        up = jnp.dot(h2, wu_ref[...], preferred_element_type=F32)
        a = _silu_parts(gt)[0] * up
        x2_ref[...] = x1 + _dot(a, wd_ref[...])
        gt_ref[...] = gt
        up_ref[...] = up
        h2_ref[...] = h2

    return _row_call("ffn_fwd", body, T, tm, [x1], [g_ffn, w_g, w_u, w_d],
                     [(D_MODEL, F32), (FFN, F32), (FFN, F32), (D_MODEL, MM)], [], VMEM_LIMIT)


def _ple_loss(x2, p, tgt, g_pg, g_post, w_pg, w_pp, T, tm):
    def body(i, x2_ref, p_ref, t_ref, gpg_ref, gpo_ref, wpg_ref, wpp_ref,
             dx2_ref, n3_ref, dz_ref, du_ref, loss_ref, dgpo_ref, dgpg_ref):
        x2 = x2_ref[...]
        p_mm = p_ref[...].astype(MM)
        u = jnp.concatenate([jnp.dot(p_mm, wpp_ref[d], preferred_element_type=F32) for d in range(N_DEV)], axis=1)
        uh, ru = _rms(u)
        e = uh * gpo_ref[...]
        x2h, r3 = _rms(x2)
        n3 = x2h * gpg_ref[...]
        gate = _sigmoid(_dot(n3, wpg_ref[...]))
        diff = x2 + gate * e - t_ref[...]
        dy = diff * (1.0 / D_MODEL)
        de = dy * gate
        dz = dy * e * gate * (1.0 - gate)
        du = _rms_bwd(de * gpo_ref[...], uh, ru)
        dn3 = _dot_nt(dz, wpg_ref[...])
        dx2_ref[...] = dy + _rms_bwd(dn3 * gpg_ref[...], x2h, r3)
        n3_ref[...] = n3.astype(MM)
        dz_ref[...] = dz.astype(MM)
        du_ref[...] = du.astype(MM)
        _acc(loss_ref, i, jnp.sum(diff * diff, axis=0, keepdims=True) * (0.5 / D_MODEL))
        _acc(dgpo_ref, i, jnp.sum(de * uh, axis=0, keepdims=True))
        _acc(dgpg_ref, i, jnp.sum(dn3 * x2h, axis=0, keepdims=True))

    vec = ((1, D_MODEL), F32)
    return _row_call("ple_loss", body, T, tm, [x2, p, tgt], [g_pg, g_post, w_pg, w_pp],
                     [(D_MODEL, F32), (D_MODEL, MM), (D_MODEL, MM), (D_MODEL, MM)], [vec, vec, vec], VMEM_LIMIT)


def _ffn_bwd(dx2, x1, gt, up, g_ffn, w_g, w_u, w_d, T, tm):
    def body(i, dx2_ref, x1_ref, gt_ref, up_ref, g_ref, wg_ref, wu_ref, wd_ref,
             dx1_ref, a_ref, dgt_ref, dup_ref, dg_ref):
        dx2 = dx2_ref[...]
        x1h, r = _rms(x1_ref[...])
        up = up_ref[...]
        silu, dsilu = _silu_parts(gt_ref[...])
        da = _dot_nt(dx2, wd_ref[...])
        dgt = (da * up * dsilu).astype(MM)
        dup = (da * silu).astype(MM)
        dh2 = (lax.dot_general(dgt, wg_ref[...], (((1,), (1,)), ((), ())), preferred_element_type=F32)
               + lax.dot_general(dup, wu_ref[...], (((1,), (1,)), ((), ())), preferred_element_type=F32))
        dx1_ref[...] = dx2 + _rms_bwd(dh2 * g_ref[...], x1h, r)
        a_ref[...] = (silu * up).astype(MM)
        dgt_ref[...] = dgt
        dup_ref[...] = dup
        _acc(dg_ref, i, jnp.sum(dh2 * x1h, axis=0, keepdims=True))

    return _row_call("ffn_bwd", body, T, tm, [dx2, x1, gt, up], [g_ffn, w_g, w_u, w_d],
                     [(D_MODEL, F32), (FFN, MM), (FFN, MM), (FFN, MM)], [((1, D_MODEL), F32)], VMEM_LIMIT)


def _merge_bwd(dx1, ya, yb, bg, o, hg, g_out, w_bra, w_brb, w_out, T, tm):
    def body(i, dx1_ref, ya_ref, yb_ref, bg_ref, o_ref, hg_ref, g_ref, wa_ref, wb_ref, wo_ref,
             dattn_ref, do_ref, dhg_ref, dbg_ref, dya_ref, dyb_ref, dg_ref):
        dm = _dot_nt(dx1_ref[...], wo_ref[...])
        ga, gb = _sigmoid(bg_ref[:, :D_MODEL]), _sigmoid(bg_ref[:, D_MODEL:])
        dya, dyb = (dm * ga).astype(MM), (dm * gb).astype(MM)
        dbg_ref[:, :D_MODEL] = dm * ya_ref[...] * ga * (1.0 - ga)
        dbg_ref[:, D_MODEL:] = dm * yb_ref[...] * gb * (1.0 - gb)
        dya_ref[...] = dya
        dyb_ref[...] = dyb
        dattn_ref[...] = lax.dot_general(dya, wa_ref[...], (((1,), (1,)), ((), ())), preferred_element_type=F32)
        drec = lax.dot_general(dyb, wb_ref[...], (((1,), (1,)), ((), ())), preferred_element_type=F32)
        dg = jnp.zeros((1, HG_DIM), F32)
        for h in range(HG_HEADS):
            ls = slice(h * HG_DIM, (h + 1) * HG_DIM)
            oh, r = _rms(o_ref[:, ls])
            silu, dsilu = _silu_parts(hg_ref[:, ls])
            dr = drec[:, ls]
            dhg_ref[:, ls] = dr * oh * g_ref[...] * dsilu
            don = dr * silu
            dg = dg + jnp.sum(don * oh, axis=0, keepdims=True)
            do_ref[:, ls] = _rms_bwd(don * g_ref[...], oh, r)
        _acc(dg_ref, i, dg)

    return _row_call("merge_bwd", body, T, tm, [dx1, ya, yb, bg, o, hg], [g_out, w_bra, w_brb, w_out],
                     [(D_MODEL, F32), (HG_W, F32), (HG_W, F32), (2 * D_MODEL, F32), (D_MODEL, MM), (D_MODEL, MM)],
                     [((1, HG_DIM), F32)], VMEM_LIMIT)


def _flash_bwd(qf, kf, vf, o, do, lse, T):
    tq = min(ATT_TILE, T)
    nq = T // tq

    qi_tab, ki_tab = _causal_pairs(nq, by_query=False)

    def body(qi_ref, ki_ref, q_ref, k_ref, v_ref, o_ref, do_ref, lse_ref, dq_ref, dk_ref, dv_ref):
        t = pl.program_id(1)
        qi, ki = qi_ref[t], ki_ref[t]

        @pl.when(t == 0)
        def _():
            dq_ref[...] = jnp.zeros_like(dq_ref)

        def step(first):
            rows = pl.ds(pl.multiple_of(qi * tq, tq), tq)
            for hh in range(hp):
                hs = slice(hh * HEAD_PAD, (hh + 1) * HEAD_PAD)
                q, k, d_o = q_ref[:, hs], k_ref[:, hs], do_ref[:, hs]
                s = _dot_nt(q, k)
                if first:
                    row = lax.broadcasted_iota(jnp.int32, (tq, tq), 0)
                    col = lax.broadcasted_iota(jnp.int32, (tq, tq), 1)
                    s = jnp.where(col <= row, s, NEG)
                p = jnp.exp(s - lse_ref[:, hh * HEAD_PAD:hh * HEAD_PAD + 1])
                delta = jnp.sum(d_o * o_ref[:, hs], axis=1, keepdims=True)
                ds = p * (_dot_nt(d_o, v_ref[:, hs]) - delta)
                dq_ref[rows, hs] += _dot(ds, k)
                if first:
                    dv_ref[:, hs] = _dot_tn(p, d_o)
                    dk_ref[:, hs] = _dot_tn(ds, q)
                else:
                    dv_ref[:, hs] += _dot_tn(p, d_o)
                    dk_ref[:, hs] += _dot_tn(ds, q)

        @pl.when(qi == ki)
        def _():
            step(True)

        @pl.when(qi > ki)
        def _():
            step(False)

    hp = ATT_HEADS
    q_spec = pl.BlockSpec((tq, hp * HEAD_PAD), lambda h, t, qi_ref, ki_ref: (qi_ref[t], h))
    kv_spec = pl.BlockSpec((tq, hp * HEAD_PAD), lambda h, t, qi_ref, ki_ref: (ki_ref[t], h))
    w = MLA_HEADS * HEAD_PAD
    grid_spec = pltpu.PrefetchScalarGridSpec(
        num_scalar_prefetch=2, grid=(MLA_HEADS // hp, len(qi_tab)),
        in_specs=[q_spec, kv_spec, kv_spec, q_spec, q_spec, q_spec],
        out_specs=[pl.BlockSpec((T, hp * HEAD_PAD), lambda h, t, qi_ref, ki_ref: (0, h)), kv_spec, kv_spec])
    return pl.pallas_call(
        body, name="flash_bwd", grid_spec=grid_spec, out_shape=[jax.ShapeDtypeStruct((T, w), F32)] * 3,
        compiler_params=_cparams(("parallel", "arbitrary")),
    )(jnp.asarray(qi_tab), jnp.asarray(ki_tab), qf, kf, vf, o, do, lse)


def _mla_heads_bwd(d_out, saved, g_pad, cos_t, sin_t, first):
    d_raw, dg = [], jnp.zeros((1, HEAD_PAD), F32)
    for h in range(MLA_HEADS):
        xh, r = saved[h]
        dy = d_out[:, h * HEAD_PAD:(h + 1) * HEAD_PAD]
        dn = dy * cos_t + _rope_swap(dy * sin_t, first)
        dg = dg + jnp.sum(dn * xh, axis=0, keepdims=True)
        d_raw.append(_rms_bwd(dn * g_pad, xh, r, QK_DIM))
    return d_raw, dg


def _mla_prep_bwd(cq, ckv, kr, pos, dqf, dkf, dvf, g_qa, g_kva, g_qn, g_kn, w_uq, w_ukv, T, tm):
    def body(i, cq_ref, ckv_ref, kr_ref, pos_ref, dq_ref, dk_ref, dv_ref,
             gqa_ref, gkva_ref, gqn_ref, gkn_ref, wuq_ref, wukv_ref,
             dcq_ref, dckv_ref, dkr_ref, dqraw_ref, dkv_ref, cqn_ref, ckvn_ref,
             dgqa_ref, dgkva_ref, dgqn_ref, dgkn_ref):
        cos_t, sin_t, first = _rope_tables(pos_ref[...], tm)
        cqh, rq = _rms(cq_ref[...])
        ckvh, rkv = _rms(ckv_ref[...])
        cqn, ckvn = cqh * gqa_ref[...], ckvh * gkva_ref[...]
        q_raw, k_raw, _ = _mla_raw_heads(cqn, ckvn, kr_ref[...], wuq_ref, wukv_ref, tm)
        _, q_saved = _mla_heads_fwd(q_raw, gqn_ref[...], cos_t, sin_t, first)
        _, k_saved = _mla_heads_fwd(k_raw, gkn_ref[...], cos_t, sin_t, first)
        dq_heads, dgqn = _mla_heads_bwd(dq_ref[...] * ATT_SCALE, q_saved, gqn_ref[...], cos_t, sin_t, first)
        dk_heads, dgkn = _mla_heads_bwd(dk_ref[...], k_saved, gkn_ref[...], cos_t, sin_t, first)
        lane = lax.broadcasted_iota(jnp.int32, (tm, HEAD_PAD), 1)
        nope = lane < QK_NOPE
        dcqn = jnp.zeros((tm, Q_RANK), F32)
        dckvn = jnp.zeros((tm, KV_RANK), F32)
        dkr = jnp.zeros((tm, HEAD_PAD), F32)
        for h in range(MLA_HEADS):
            hs = slice(h * HEAD_PAD, (h + 1) * HEAD_PAD)
            dq_h = dq_heads[h].astype(MM)
            dkv_h = jnp.where(nope, dk_heads[h], pltpu.roll(dv_ref[:, hs], V_DIM, 1)).astype(MM)
            dqraw_ref[:, hs] = dq_h
            dkv_ref[:, hs] = dkv_h
            dcqn = dcqn + lax.dot_general(dq_h, wuq_ref[h], (((1,), (1,)), ((), ())), preferred_element_type=F32)
            dckvn = dckvn + lax.dot_general(dkv_h, wukv_ref[h], (((1,), (1,)), ((), ())), preferred_element_type=F32)
            dkr = dkr + dk_heads[h]
        dkr_ref[...] = jnp.where((lane >= QK_NOPE) & (lane < QK_DIM), dkr, 0.0)
        dcq_ref[...] = _rms_bwd(dcqn * gqa_ref[...], cqh, rq)
        dckv_ref[...] = _rms_bwd(dckvn * gkva_ref[...], ckvh, rkv)
        cqn_ref[...] = cqn.astype(MM)
        ckvn_ref[...] = ckvn.astype(MM)
        _acc(dgqa_ref, i, jnp.sum(dcqn * cqh, axis=0, keepdims=True))
        _acc(dgkva_ref, i, jnp.sum(dckvn * ckvh, axis=0, keepdims=True))
        _acc(dgqn_ref, i, dgqn)
        _acc(dgkn_ref, i, dgkn)

    w = MLA_HEADS * HEAD_PAD
    return _row_call(
        "mla_prep_bwd", body, T, tm, [cq, ckv, kr, pos, dqf, dkf, dvf], [g_qa, g_kva, g_qn, g_kn, w_uq, w_ukv],
        [(Q_RANK, F32), (KV_RANK, F32), (HEAD_PAD, F32), (w, MM), (w, MM), (Q_RANK, MM), (KV_RANK, MM)],
        [((1, Q_RANK), F32), ((1, KV_RANK), F32), ((1, HEAD_PAD), F32), ((1, HEAD_PAD), F32)], VMEM_LIMIT)


def _in_proj_bwd(x, dx1, dsecs, g_mix, w_in_pad, T, tm):
    def body(i, x_ref, dx1_ref, *rest):
        d_refs, (g_ref, w_ref, dx_ref, dp_ref, dg_ref) = rest[:len(SECTIONS)], rest[len(SECTIONS):]
        dh = jnp.zeros((tm, D_MODEL), F32)
        for (s, n), d_ref in zip(SECTIONS, d_refs):
            d = d_ref[...].astype(MM)
            dp_ref[:, s:s + n] = d
            dh = dh + lax.dot_general(d, w_ref[:, s:s + n], (((1,), (1,)), ((), ())), preferred_element_type=F32)
        xh, r = _rms(x_ref[...])
        dx_ref[...] = dx1_ref[...] + _rms_bwd(dh * g_ref[...], xh, r)
        _acc(dg_ref, i, jnp.sum(dh * xh, axis=0, keepdims=True))

    return _row_call("in_proj_bwd", body, T, tm, [x, dx1, *dsecs], [g_mix, w_in_pad],
                     [(D_MODEL, F32), (IN_PAD, MM)], [((1, D_MODEL), F32)], VMEM_LIMIT)


def _pick_block(n, cap):
    best = None
    for cand in range(128, min(n, cap) + 1, 128):
        if n % cand == 0:
            best = cand
    return n if best is None else best


def _pick_rows(n, cap):
    best = n
    for cand in range(8, min(n, cap) + 1, 8):
        if n % cand == 0:
            best = cand
    return best


def _matmul_tn(name, a, b, blocked=None):
    T, M = a.shape
    N = b.shape[1]
    bm, bk = _pick_block(M, 1408), min(512, T)
    bn = _pick_block(N, 2560)

    def body(a_ref, b_ref, c_ref):
        @pl.when(pl.program_id(2) == 0)
        def _():
            c_ref[...] = jnp.zeros_like(c_ref)

        if blocked is None:
            c_ref[...] += _dot_tn(a_ref[...], b_ref[...])
        else:
            a = a_ref[...].astype(MM)
            for d in range(bn // blocked):
                c_ref[d] += _dot_tn(a, b_ref[:, d * blocked:(d + 1) * blocked])

    if blocked is None:
        out_spec = pl.BlockSpec((bm, bn), lambda i, j, k: (i, j))
        out_shape = jax.ShapeDtypeStruct((M, N), F32)
    else:
        assert bn == N
        out_spec = pl.BlockSpec((N // blocked, bm, blocked), lambda i, j, k: (0, i, 0))
        out_shape = jax.ShapeDtypeStruct((N // blocked, M, blocked), F32)
    return pl.pallas_call(
        body, name=name, grid=(M // bm, N // bn, T // bk),
        in_specs=[pl.BlockSpec((bk, bm), lambda i, j, k: (k, i)), pl.BlockSpec((bk, bn), lambda i, j, k: (k, j))],
        out_specs=out_spec, out_shape=out_shape,
        compiler_params=_cparams(("parallel", "parallel", "arbitrary"), VMEM_LIMIT),
    )(a, b)


def _pad_gain(g, n):
    return jnp.pad(g.reshape(1, -1), ((0, 0), (0, n - g.shape[-1])))


def _cols_full(g):
    return jnp.swapaxes(g, 0, 1).reshape(g.shape[1], -1)


def _cols_blocked(full):
    r = full.shape[0]
    return jnp.swapaxes(full.reshape(r, N_DEV, -1), 0, 1)


def _local_step(x, p, pos, tgt, small, big):
    T = x.shape[0]
    tm = min(ROW_TILE, T)
    w_in = _cols_full(big["w_in"])
    zeros = lambda n: jnp.zeros((D_MODEL, n), w_in.dtype)
    w_in_pad = jnp.concatenate(
        [w_in[:, :640], zeros(QK_NOPE), w_in[:, 640:672], zeros(HEAD_PAD - QK_DIM), w_in[:, 672:]], axis=1)
    w_uq = jnp.pad(big["w_uq"], ((0, 0), (0, 0), (0, HEAD_PAD - QK_DIM)))
    w_ukv = big["w_ukv"]
    w_branch = jnp.moveaxis(big["w_branch"], 0, 2).reshape(2, HG_W, D_MODEL)
    w_bra = jnp.pad(w_branch[0].reshape(MLA_HEADS, V_DIM, D_MODEL),
                    ((0, 0), (0, HEAD_PAD - V_DIM), (0, 0))).reshape(MLA_HEADS * HEAD_PAD, D_MODEL)
    w_brb = w_branch[1]
    w_out = big["w_out"].reshape(D_MODEL, D_MODEL)
    w_g, w_u = _cols_full(big["w_ffn_gate"]), _cols_full(big["w_ffn_up"])
    w_d = big["w_ffn_down"].reshape(FFN, D_MODEL)
    w_pg, w_pp = big["w_ple_gate"].reshape(D_MODEL, D_MODEL), big["w_ple_proj"]

    g_mix, g_qa, g_kva = small["mix_norm_g"], small["q_a_norm_g"], small["kv_a_norm_g"]
    g_qn, g_kn = _pad_gain(small["q_norm_g"], HEAD_PAD), _pad_gain(small["k_norm_g"], HEAD_PAD)
    g_out, g_ffn = small["hg_out_norm_g"], small["ffn_norm_g"]
    g_pg, g_post = small["ple_gate_norm_g"], small["ple_post_norm_g"]
    logits = small["hg_lb_logits"]
    lb = _lower_bound(logits)

    h, cq, ckv, kr, hq, hf, hi, hg, bg = _in_proj_fwd(x, g_mix, w_in_pad, T, tm)
    qf, kf, vf = _mla_prep_fwd(cq, ckv, kr, pos, g_qa, g_kva, g_qn, g_kn, w_uq, w_ukv, T, tm)
    attn, lse = _flash_fwd(qf, kf, vf, T)
    o, s0 = _hgrn_fwd(hq, hf, hi, lb, T)
    x1, ya, yb, m, rec = _merge_fwd(attn, o, hg, bg, x, g_out, w_bra, w_brb, w_out, T, tm)
    x2, gt, up, h2 = _ffn_fwd(x1, g_ffn, w_g, w_u, w_d, T, tm)
    dx2, n3, dz, du, loss_p, dg_post, dg_pg = _ple_loss(x2, p, tgt, g_pg, g_post, w_pg, w_pp, T, tm)

    dx1, a, dgt, dup, dg_ffn = _ffn_bwd(dx2, x1, gt, up, g_ffn, w_g, w_u, w_d, T, tm)
    dattn, do, dhg, dbg, dya, dyb, dg_out = _merge_bwd(dx1, ya, yb, bg, o, hg, g_out, w_bra, w_brb, w_out, T, tm)
    dhq, dhf, dhi, dlb = _hgrn_bwd(hq, hf, hi, do, s0, lb, T)
    dqf, dkf, dvf = _flash_bwd(qf, kf, vf, attn, dattn, lse, T)
    (dcq, dckv, dkr, dq_raw, dkv, cqn, ckvn, dg_qa, dg_kva, dg_qn, dg_kn) = _mla_prep_bwd(
        cq, ckv, kr, pos, dqf, dkf, dvf, g_qa, g_kva, g_qn, g_kn, w_uq, w_ukv, T, tm)
    grad_x, dproj, dg_mix = _in_proj_bwd(x, dx1, [dcq, dckv, dkr, dhq, dhf, dhi, dhg, dbg], g_mix, w_in_pad, T, tm)

    d_in = _matmul_tn("dw_in", h, dproj)
    d_in = jnp.concatenate([d_in[:, :640], d_in[:, 640 + QK_NOPE:640 + QK_DIM], d_in[:, 768:]], axis=1)
    d_bra = _matmul_tn("dw_bra", attn, dya, blocked=HEAD_PAD)
    d_bra = d_bra.reshape(N_DEV, MLA_HEADS, HEAD_PAD, HEAD_PAD)[:, :, :V_DIM].reshape(N_DEV, HG_W, HEAD_PAD)
    d_brb = _matmul_tn("dw_brb", rec, dyb, blocked=HEAD_PAD)
    grads = {
        "w_in": _cols_blocked(d_in),
        "w_uq": _matmul_tn("dw_uq", cqn, dq_raw, blocked=HEAD_PAD)[:, :, :QK_DIM],
        "w_ukv": _matmul_tn("dw_ukv", ckvn, dkv, blocked=HEAD_PAD),
        "w_branch": jnp.concatenate([d_bra, d_brb], axis=1),
        "w_out": _matmul_tn("dw_out", m, dx1).reshape(N_DEV, -1, D_MODEL),
        "w_ffn_gate": _cols_blocked(_matmul_tn("dw_gate", h2, dgt)),
        "w_ffn_up": _cols_blocked(_matmul_tn("dw_up", h2, dup)),
        "w_ffn_down": _matmul_tn("dw_down", a, dx2).reshape(N_DEV, -1, D_MODEL),
        "w_ple_gate": _matmul_tn("dw_pg", n3, dz).reshape(N_DEV, -1, D_MODEL),
        "w_ple_proj": _matmul_tn("dw_pp", p, du, blocked=HEAD_PAD),
    }
    dl0 = dlb * lb * (1.0 - lb)
    small_g = {
        "mix_norm_g": dg_mix, "q_a_norm_g": dg_qa, "kv_a_norm_g": dg_kva,
        "q_norm_g": dg_qn[:, :QK_DIM], "k_norm_g": dg_kn[:, :QK_DIM],
        "hg_lb_logits": jnp.concatenate([dl0, -dl0], axis=0), "hg_out_norm_g": dg_out,
        "ffn_norm_g": dg_ffn, "ple_gate_norm_g": dg_pg, "ple_post_norm_g": dg_post,
    }
    return loss_p, grad_x, small_g, grads


def _lower_bound(logits):
    def body(l_ref, lb_ref):
        l = l_ref[...]
        mx = jnp.max(l, axis=0, keepdims=True)
        e = jnp.exp(l - mx)
        lb_ref[...] = e[0:1] / jnp.sum(e, axis=0, keepdims=True)

    return pl.pallas_call(body, name="lower_bound", out_shape=jax.ShapeDtypeStruct((1, HG_W), F32))(logits)


def _my_place():
    return lax.axis_index("x"), lax.axis_index("y"), lax.axis_index("c")


def _all_gather(name, blocks):
    n = len(blocks)

    def body(*refs):
        x_refs, out_refs = refs[:n], refs[n:2 * n]
        send_sems, recv_sems, local_sems = refs[2 * n:]
        x, y, c = _my_place()
        me, sibling = (x, y, c), (x, y, 1 - c)
        chips = [(1 - x, y), (x, 1 - y), (1 - x, 1 - y)]

        def copy(a, k, block, to, own=False):
            px, py, pc = block
            dst = out_refs[a].at[4 * px + 2 * py + pc]
            return pltpu.make_async_remote_copy(
                src_ref=x_refs[a] if own else dst, dst_ref=dst, send_sem=send_sems.at[7 * a + k],
                recv_sem=recv_sems.at[7 * a + k], device_id=to, device_id_type=MESH_ID)

        mine = [pltpu.make_async_copy(x_refs[a], out_refs[a].at[4 * x + 2 * y + c], local_sems.at[a]) for a in range(n)]
        for cp in mine:
            cp.start()
        first = []
        for a in range(n):
            first.append(copy(a, 0, me, sibling, own=True))
            first += [copy(a, 1 + j, me, (*chip, c), own=True) for j, chip in enumerate(chips)]
        for cp in first:
            cp.start()
        passed = []
        for j, chip in enumerate(chips):
            for a in range(n):
                copy(a, 1 + j, (*chip, c), me).wait_recv()
                passed.append(copy(a, 4 + j, (*chip, c), sibling))
                passed[-1].start()
        for a in range(n):
            copy(a, 0, sibling, me).wait_recv()
        for j, chip in enumerate(chips):
            for a in range(n):
                copy(a, 4 + j, (*chip, 1 - c), me).wait_recv()
        for cp in first + passed:
            cp.wait_send()
        for cp in mine:
            cp.wait()

    any_spec = pl.BlockSpec(memory_space=pl.ANY)
    return pl.pallas_call(
        body, name=name, out_shape=[jax.ShapeDtypeStruct((N_DEV,) + b.shape, b.dtype) for b in blocks],
        in_specs=[any_spec] * n, out_specs=[any_spec] * n,
        scratch_shapes=[pltpu.SemaphoreType.DMA((7 * n,)), pltpu.SemaphoreType.DMA((7 * n,)),
                        pltpu.SemaphoreType.DMA((n,))],
    )(*blocks)


def _exchange_sibling(gs):
    n = len(gs)

    def body(*refs):
        g_refs, out_refs, (send_sems, recv_sems) = refs[:n], refs[n:2 * n], refs[2 * n:]
        x, y, c = _my_place()
        copies = [pltpu.make_async_remote_copy(
            src_ref=g_refs[a].at[2 * j + 1 - c], dst_ref=out_refs[a].at[j], send_sem=send_sems.at[4 * a + j],
            recv_sem=recv_sems.at[4 * a + j], device_id=(x, y, 1 - c), device_id_type=MESH_ID)
            for a in range(n) for j in range(4)]
        for cp in copies:
            cp.start()
        for cp in copies:
            cp.wait()

    any_spec = pl.BlockSpec(memory_space=pl.ANY)
    return pl.pallas_call(
        body, name="rs_sibling", out_shape=[jax.ShapeDtypeStruct((4,) + g.shape[1:], g.dtype) for g in gs],
        in_specs=[any_spec] * n, out_specs=[any_spec] * n,
        scratch_shapes=[pltpu.SemaphoreType.DMA((4 * n,)), pltpu.SemaphoreType.DMA((4 * n,))],
    )(*gs)


def _chip_partial(name, g, got, c_idx):
    _, rows, cols = g.shape
    tr = _pick_rows(rows, 512)

    def body(c_ref, g_ref, got_ref, out_ref):
        out_ref[...] = (g_ref[...] + got_ref[...]).astype(MM)

    grid_spec = pltpu.PrefetchScalarGridSpec(
        num_scalar_prefetch=1, grid=(4, rows // tr),
        in_specs=[pl.BlockSpec((1, tr, cols), lambda j, i, c_ref: (2 * j + c_ref[0], i, 0)),
                  pl.BlockSpec((1, tr, cols), lambda j, i, c_ref: (j, i, 0))],
        out_specs=pl.BlockSpec((1, tr, cols), lambda j, i, c_ref: (j, i, 0)))
    return pl.pallas_call(
        body, name=name, grid_spec=grid_spec, out_shape=jax.ShapeDtypeStruct((4, rows, cols), MM),
        compiler_params=_cparams(("parallel", "parallel")),
    )(c_idx, g, got)


def _exchange_chips(parts):
    n = len(parts)

    def body(*refs):
        p_refs, out_refs, (send_sems, recv_sems) = refs[:n], refs[n:2 * n], refs[2 * n:]
        x, y, c = _my_place()
        chips = [(1 - x, y), (x, 1 - y), (1 - x, 1 - y)]
        copies = [pltpu.make_async_remote_copy(
            src_ref=p_refs[a].at[2 * px + py], dst_ref=out_refs[a].at[k], send_sem=send_sems.at[3 * a + k],
            recv_sem=recv_sems.at[3 * a + k], device_id=(px, py, c), device_id_type=MESH_ID)
            for a in range(n) for k, (px, py) in enumerate(chips)]
        for cp in copies:
            cp.start()
        for cp in copies:
            cp.wait()

    any_spec = pl.BlockSpec(memory_space=pl.ANY)
    return pl.pallas_call(
        body, name="rs_chips", out_shape=[jax.ShapeDtypeStruct((3,) + p.shape[1:], p.dtype) for p in parts],
        in_specs=[any_spec] * n, out_specs=[any_spec] * n,
        scratch_shapes=[pltpu.SemaphoreType.DMA((3 * n,)), pltpu.SemaphoreType.DMA((3 * n,))],
    )(*parts)


def _adamw_math(w, g, m, v):
    m = ADAM_B1 * m + (1.0 - ADAM_B1) * g
    v = ADAM_B2 * v + (1.0 - ADAM_B2) * jnp.square(g)
    m_hat = m / (1.0 - ADAM_B1 ** ADAM_STEP)
    v_hat = v / (1.0 - ADAM_B2 ** ADAM_STEP)
    delta = -ADAM_LR * (m_hat / (jnp.sqrt(v_hat) + ADAM_EPS) + ADAM_WD * w)
    return delta, m, v


def _sum_adamw(name, g, sib, got, w, m, v, slot_idx, chip_idx):
    _, rows, cols = g.shape
    tr = _pick_rows(rows, 256)

    def body(s_ref, j_ref, g_ref, sib_ref, got_ref, w_ref, m_ref, v_ref, go_ref, d_ref, m2_ref, v2_ref):
        grad = g_ref[0] + sib_ref[0]
        for k in range(3):
            grad = grad + got_ref[k].astype(F32)
        go_ref[...] = grad
        d_ref[...], m2_ref[...], v2_ref[...] = _adamw_math(w_ref[...], grad, m_ref[...], v_ref[...])

    flat = pl.BlockSpec((tr, cols), lambda i, s_ref, j_ref: (i, 0))
    grid_spec = pltpu.PrefetchScalarGridSpec(
        num_scalar_prefetch=2, grid=(rows // tr,),
        in_specs=[pl.BlockSpec((1, tr, cols), lambda i, s_ref, j_ref: (s_ref[0], i, 0)),
                  pl.BlockSpec((1, tr, cols), lambda i, s_ref, j_ref: (j_ref[0], i, 0)),
                  pl.BlockSpec((3, tr, cols), lambda i, s_ref, j_ref: (0, i, 0)), flat, flat, flat],
        out_specs=[flat] * 4)
    return pl.pallas_call(
        body, name=name, grid_spec=grid_spec, out_shape=[jax.ShapeDtypeStruct((rows, cols), F32)] * 4,
        compiler_params=_cparams(("parallel",)),
    )(slot_idx, chip_idx, g, sib, got, w, m, v)


def _adamw_small(parts, w, m, v):
    rows = w.shape[0]

    def body(p_ref, w_ref, m_ref, v_ref, g_ref, d_ref, m2_ref, v2_ref):
        g = p_ref[0]
        for d in range(1, N_DEV):
            g = g + p_ref[d]
        g_ref[...] = g
        d_ref[...], m2_ref[...], v2_ref[...] = _adamw_math(w_ref[...], g, m_ref[...], v_ref[...])

    return pl.pallas_call(
        body, name="adamw_small", out_shape=[jax.ShapeDtypeStruct((rows, 128), F32)] * 4,
    )(parts, w, m, v)


BIG = ("w_in", "w_uq", "w_ukv", "w_branch", "w_out", "w_ffn_gate", "w_ffn_up", "w_ffn_down", "w_ple_gate", "w_ple_proj")
SMALL = (
    ("mix_norm_g", 1024), ("q_a_norm_g", 384), ("kv_a_norm_g", 256), ("q_norm_g", 96), ("k_norm_g", 96),
    ("hg_lb_logits", 1024), ("hg_out_norm_g", 128), ("ffn_norm_g", 1024), ("ple_gate_norm_g", 1024),
    ("ple_post_norm_g", 1024),
)
SMALL_ROWS = 56


def _pack_small(vals):
    rows = []
    for name, n in SMALL:
        v = vals[name].reshape(1, -1).astype(F32)
        rows.append(jnp.pad(v, ((0, 0), (0, (-n) % 128))).reshape(-1, 128))
    return jnp.concatenate(rows, axis=0)


def _unpack_small(packed, shapes):
    out, r = {}, 0
    for name, n in SMALL:
        k = (n + 127) // 128
        out[name] = packed[r:r + k].reshape(1, -1)[:, :n].reshape(shapes[name])
        r += k
    return out


_WEIGHTS = ["mix_norm_g", "w_in", "q_a_norm_g", "w_uq", "kv_a_norm_g", "w_ukv", "q_norm_g", "k_norm_g", "hg_lb_logits",
            "hg_out_norm_g", "w_branch", "w_out", "ffn_norm_g", "w_ffn_gate", "w_ffn_up", "w_ffn_down",
            "ple_gate_norm_g", "w_ple_gate", "w_ple_proj", "ple_post_norm_g"]


def _step(x, p, positions, tgt, w, m, v):
    small_names = [n for n, _ in SMALL]
    T = x.shape[1]
    px, py, pc = _my_place()
    as_idx = lambda t: jnp.reshape(t, (1,)).astype(jnp.int32)
    two_d = lambda t: t.reshape(-1, t.shape[-1])

    gathered = _all_gather("ag_weights", [two_d(w[n]).astype(MM) for n in BIG])
    big = dict(zip(BIG, gathered))
    big["w_branch"] = big["w_branch"].reshape((N_DEV,) + w["w_branch"].shape[1:])
    small = {n: (w[n] if n == "hg_lb_logits" else w[n].reshape(1, -1)) for n in small_names}

    loss_p, grad_x, small_g, grads = _local_step(
        x[0], p[0, 0], positions.reshape(T, 1), tgt[0], small, big)

    gs = [grads[n] for n in BIG]
    sibs = _exchange_sibling(gs)
    parts = [_chip_partial("rs_partial_" + n, g, s, as_idx(pc)) for n, g, s in zip(BIG, gs, sibs)]
    gots = _exchange_chips(parts)
    out_g, out_d, out_m, out_v = {}, {}, {}, {}
    for n, g, s, got in zip(BIG, gs, sibs, gots):
        res = _sum_adamw("adamw_" + n, g, s, got, two_d(w[n]), two_d(m[n]), two_d(v[n]),
                         as_idx(4 * px + 2 * py + pc), as_idx(2 * px + py))
        out_g[n], out_d[n], out_m[n], out_v[n] = [r.reshape(w[n].shape) for r in res]

    packed_g = _pack_small(small_g)
    loss_row = jnp.concatenate([jnp.pad(jnp.sum(loss_p).reshape(1, 1), ((0, 0), (0, 127))),
                                jnp.zeros((SMALL_ROWS - packed_g.shape[0] - 1, 128), F32)], axis=0)
    parts = _all_gather("ag_small", [jnp.concatenate([packed_g, loss_row], axis=0)])[0]
    pad_rows = lambda t: jnp.pad(t, ((0, SMALL_ROWS - t.shape[0]), (0, 0)))
    sw = pad_rows(_pack_small({n: w[n] for n in small_names}))
    sm = pad_rows(_pack_small({n: m[n] for n in small_names}))
    sv = pad_rows(_pack_small({n: v[n] for n in small_names}))
    g_s, d_s, m_s, v_s = _adamw_small(parts, sw, sm, sv)
    shapes = {n: w[n].shape for n in small_names}
    n_packed = packed_g.shape[0]
    loss = g_s[n_packed, 0]
    for src, dst in ((g_s, out_g), (d_s, out_d), (m_s, out_m), (v_s, out_v)):
        dst.update(_unpack_small(src, shapes))

    outs = [loss, grad_x[None]]
    for table in (out_g, out_d, out_m, out_v):
        outs += [table[n] for n in _WEIGHTS]
    return tuple(outs)


def kernel(x, p, positions, mix_norm_g, w_in, q_a_norm_g, w_uq, kv_a_norm_g, w_ukv, q_norm_g, k_norm_g, hg_lb_logits, hg_out_norm_g, w_branch, w_out, ffn_norm_g, w_ffn_gate, w_ffn_up, w_ffn_down, ple_gate_norm_g, w_ple_gate, w_ple_proj, ple_post_norm_g, loss_target, m_mix_norm_g, m_w_in, m_q_a_norm_g, m_w_uq, m_kv_a_norm_g, m_w_ukv, m_q_norm_g, m_k_norm_g, m_hg_lb_logits, m_hg_out_norm_g, m_w_branch, m_w_out, m_ffn_norm_g, m_w_ffn_gate, m_w_ffn_up, m_w_ffn_down, m_ple_gate_norm_g, m_w_ple_gate, m_w_ple_proj, m_ple_post_norm_g, v_mix_norm_g, v_w_in, v_q_a_norm_g, v_w_uq, v_kv_a_norm_g, v_w_ukv, v_q_norm_g, v_k_norm_g, v_hg_lb_logits, v_hg_out_norm_g, v_w_branch, v_w_out, v_ffn_norm_g, v_w_ffn_gate, v_w_ffn_up, v_w_ffn_down, v_ple_gate_norm_g, v_w_ple_gate, v_w_ple_proj, v_ple_post_norm_g):
    w = dict(mix_norm_g=mix_norm_g, w_in=w_in, q_a_norm_g=q_a_norm_g, w_uq=w_uq, kv_a_norm_g=kv_a_norm_g, w_ukv=w_ukv,
             q_norm_g=q_norm_g, k_norm_g=k_norm_g, hg_lb_logits=hg_lb_logits, hg_out_norm_g=hg_out_norm_g,
             w_branch=w_branch, w_out=w_out, ffn_norm_g=ffn_norm_g, w_ffn_gate=w_ffn_gate, w_ffn_up=w_ffn_up,
             w_ffn_down=w_ffn_down, ple_gate_norm_g=ple_gate_norm_g, w_ple_gate=w_ple_gate, w_ple_proj=w_ple_proj,
             ple_post_norm_g=ple_post_norm_g)
    m = dict(mix_norm_g=m_mix_norm_g, w_in=m_w_in, q_a_norm_g=m_q_a_norm_g, w_uq=m_w_uq, kv_a_norm_g=m_kv_a_norm_g,
             w_ukv=m_w_ukv, q_norm_g=m_q_norm_g, k_norm_g=m_k_norm_g, hg_lb_logits=m_hg_lb_logits,
             hg_out_norm_g=m_hg_out_norm_g, w_branch=m_w_branch, w_out=m_w_out, ffn_norm_g=m_ffn_norm_g,
             w_ffn_gate=m_w_ffn_gate, w_ffn_up=m_w_ffn_up, w_ffn_down=m_w_ffn_down,
             ple_gate_norm_g=m_ple_gate_norm_g, w_ple_gate=m_w_ple_gate, w_ple_proj=m_w_ple_proj,
             ple_post_norm_g=m_ple_post_norm_g)
    v = dict(mix_norm_g=v_mix_norm_g, w_in=v_w_in, q_a_norm_g=v_q_a_norm_g, w_uq=v_w_uq, kv_a_norm_g=v_kv_a_norm_g,
             w_ukv=v_w_ukv, q_norm_g=v_q_norm_g, k_norm_g=v_k_norm_g, hg_lb_logits=v_hg_lb_logits,
             hg_out_norm_g=v_hg_out_norm_g, w_branch=v_w_branch, w_out=v_w_out, ffn_norm_g=v_ffn_norm_g,
             w_ffn_gate=v_w_ffn_gate, w_ffn_up=v_w_ffn_up, w_ffn_down=v_w_ffn_down,
             ple_gate_norm_g=v_ple_gate_norm_g, w_ple_gate=v_w_ple_gate, w_ple_proj=v_w_ple_proj,
             ple_post_norm_g=v_ple_post_norm_g)
    return _step(x, p, positions, loss_target, w, m, v)
```

```python
import functools

import jax
import jax.numpy as jnp
import numpy as np
from jax import lax
from jax.experimental import pallas as pl
from jax.experimental.pallas import tpu as pltpu

F32 = jnp.float32
MM = jnp.bfloat16
HI = lax.Precision.HIGHEST
MESH_ID = pl.DeviceIdType.MESH

D_MODEL = 1024
N_DEV = 8
MLA_HEADS = 8
QK_NOPE = 64
QK_ROPE = 32
QK_DIM = 96
V_DIM = 64
HEAD_PAD = 128
Q_RANK = 384
KV_RANK = 256
ROPE_BASE = 10000.0
HG_HEADS = 4
HG_DIM = 128
HG_W = 512
HG_CHUNK = 64
HG_SUB = 16
FFN = 2816
PLE = 256
EPS = 1e-6
ATT_SCALE = QK_DIM ** -0.5
NEG = -1e30

ADAM_LR = 0.001
ADAM_B1 = 0.9
ADAM_B2 = 0.999
ADAM_EPS = 1e-08
ADAM_WD = 0.01
ADAM_STEP = 10

SEC_CQ = (0, 384)
SEC_CKV = (384, 256)
SEC_KR = (640, 128)
SEC_HQ = (768, 512)
SEC_HF = (1280, 512)
SEC_HI = (1792, 512)
SEC_HG = (2304, 512)
SEC_BG = (2816, 2048)
IN_PAD = 4864
SECTIONS = (SEC_CQ, SEC_CKV, SEC_KR, SEC_HQ, SEC_HF, SEC_HI, SEC_HG, SEC_BG)

VMEM_LIMIT = 58 * 1024 * 1024
ROW_TILE = 256
ATT_TILE = 512
ATT_HEADS = 2
HG_BLOCK = 512


def _dot(a, b):
    return jnp.dot(a.astype(MM), b.astype(MM), preferred_element_type=F32)


def _dot_nt(a, b):
    return lax.dot_general(a.astype(MM), b.astype(MM), (((1,), (1,)), ((), ())), preferred_element_type=F32)


def _dot_tn(a, b):
    return lax.dot_general(a.astype(MM), b.astype(MM), (((0,), (0,)), ((), ())), preferred_element_type=F32)


def _dot_hi(a, b):
    return jnp.dot(a, b, preferred_element_type=F32, precision=HI)


def _sigmoid(x):
    return 1.0 / (1.0 + jnp.exp(-x))


def _rms(x, n=None):
    n = x.shape[-1] if n is None else n
    r = lax.rsqrt(jnp.sum(x * x, axis=-1, keepdims=True) * (1.0 / n) + EPS)
    return x * r, r


def _rms_bwd(dxh, xh, r, n=None):
    n = xh.shape[-1] if n is None else n
    return r * (dxh - xh * (jnp.sum(dxh * xh, axis=-1, keepdims=True) * (1.0 / n)))


def _rope_tables(pos, tm):
    lane = lax.broadcasted_iota(jnp.int32, (tm, HEAD_PAD), 1)
    idx = jnp.where(lane < QK_NOPE + QK_ROPE // 2, lane - QK_NOPE, lane - QK_NOPE - QK_ROPE // 2)
    inv = jnp.exp(idx.astype(F32) * (-np.log(ROPE_BASE) * 2.0 / QK_ROPE))
    ang = pos.astype(F32) * inv
    in_rope = (lane >= QK_NOPE) & (lane < QK_DIM)
    first = lane < QK_NOPE + QK_ROPE // 2
    cos_t = jnp.where(in_rope, jnp.cos(ang), 1.0)
    sin_t = jnp.where(in_rope, jnp.where(first, -jnp.sin(ang), jnp.sin(ang)), 0.0)
    return cos_t, sin_t, (first, in_rope)


def _rope_swap(x, halves):
    first, in_rope = halves
    half = QK_ROPE // 2
    return jnp.where(in_rope, jnp.where(first, pltpu.roll(x, HEAD_PAD - half, 1), pltpu.roll(x, half, 1)), 0.0)


def _cparams(sem, vmem=None):
    return pltpu.CompilerParams(dimension_semantics=sem, vmem_limit_bytes=vmem)


def _row_call(name, body, T, tm, row_ins, full_ins, row_outs, acc_outs, vmem=None):
    def kern(*refs):
        body(pl.program_id(0), *refs)

    in_specs = [pl.BlockSpec((tm, a.shape[1]), lambda i: (i, 0)) for a in row_ins]
    in_specs += [pl.BlockSpec(a.shape, lambda i, nd=a.ndim: (0,) * nd, pipeline_mode=pl.Buffered(1)) for a in full_ins]
    out_specs = [pl.BlockSpec((tm, n), lambda i: (i, 0)) for n, _ in row_outs]
    out_specs += [pl.BlockSpec(s, lambda i, nd=len(s): (0,) * nd) for s, _ in acc_outs]
    out_shape = [jax.ShapeDtypeStruct((T, n), dt) for n, dt in row_outs]
    out_shape += [jax.ShapeDtypeStruct(s, dt) for s, dt in acc_outs]
    return pl.pallas_call(
        kern, name=name, grid=(T // tm,), in_specs=in_specs, out_specs=out_specs, out_shape=out_shape,
        compiler_params=_cparams(("arbitrary",), vmem),
    )(*row_ins, *full_ins)


def _acc(ref, i, val):
    @pl.when(i == 0)
    def _():
        ref[...] = val

    @pl.when(i != 0)
    def _():
        ref[...] += val


def _in_proj_fwd(x, g_mix, w_in_pad, T, tm):
    def body(i, x_ref, g_ref, w_ref, h_ref, *outs):
        xh, _ = _rms(x_ref[...])
        h = (xh * g_ref[...]).astype(MM)
        h_ref[...] = h
        for (s, n), o_ref in zip(SECTIONS, outs):
            o_ref[...] = jnp.dot(h, w_ref[:, s:s + n], preferred_element_type=F32)

    row_outs = [(D_MODEL, MM)] + [(n, F32) for _, n in SECTIONS]
    return _row_call("in_proj_fwd", body, T, tm, [x], [g_mix, w_in_pad], row_outs, [], VMEM_LIMIT)


def _mla_heads_fwd(raw, g_pad, cos_t, sin_t, first):
    outs, saved = [], []
    for h in range(MLA_HEADS):
        xh, r = _rms(raw[:, h * HEAD_PAD:(h + 1) * HEAD_PAD], QK_DIM)
        y = xh * g_pad
        outs.append(y * cos_t + _rope_swap(y, first) * sin_t)
        saved.append((xh, r))
    return outs, saved


def _mla_raw_heads(cqn, ckvn, kr, wuq_ref, wukv_ref, tm):
    lane = lax.broadcasted_iota(jnp.int32, (tm, HEAD_PAD), 1)
    nope = lane < QK_NOPE
    one_lane = jnp.where(lane == V_DIM, 1.0, 0.0)
    qs, ks, vs = [], [], []
    for h in range(MLA_HEADS):
        qs.append(_dot(cqn, wuq_ref[h]))
        kv = _dot(ckvn, wukv_ref[h])
        ks.append(jnp.where(nope, kv, kr))
        vs.append(jnp.where(nope, pltpu.roll(kv, V_DIM, 1), one_lane))
    return jnp.concatenate(qs, axis=1), jnp.concatenate(ks, axis=1), jnp.concatenate(vs, axis=1)


def _mla_prep_fwd(cq, ckv, kr, pos, g_qa, g_kva, g_qn, g_kn, w_uq, w_ukv, T, tm):
    def body(i, cq_ref, ckv_ref, kr_ref, pos_ref, gqa_ref, gkva_ref, gqn_ref, gkn_ref, wuq_ref, wukv_ref,
             q_ref, k_ref, v_ref):
        cos_t, sin_t, first = _rope_tables(pos_ref[...], tm)
        cqn = _rms(cq_ref[...])[0] * gqa_ref[...]
        ckvn = _rms(ckv_ref[...])[0] * gkva_ref[...]
        q_raw, k_raw, v = _mla_raw_heads(cqn, ckvn, kr_ref[...], wuq_ref, wukv_ref, tm)
        qs, _ = _mla_heads_fwd(q_raw, gqn_ref[...], cos_t, sin_t, first)
        ks, _ = _mla_heads_fwd(k_raw, gkn_ref[...], cos_t, sin_t, first)
        q_ref[...] = (jnp.concatenate(qs, axis=1) * ATT_SCALE).astype(MM)
        k_ref[...] = jnp.concatenate(ks, axis=1).astype(MM)
        v_ref[...] = v.astype(MM)

    w = MLA_HEADS * HEAD_PAD
    return _row_call("mla_prep_fwd", body, T, tm, [cq, ckv, kr, pos], [g_qa, g_kva, g_qn, g_kn, w_uq, w_ukv],
                     [(w, MM), (w, MM), (w, MM)], [])


def _causal_pairs(n, by_query):
    if by_query:
        pairs = [(q, k) for q in range(n) for k in range(q + 1)]
    else:
        pairs = [(q, k) for k in range(n) for q in range(k, n)]
    return np.array([p[0] for p in pairs], np.int32), np.array([p[1] for p in pairs], np.int32)


def _flash_fwd(qf, kf, vf, T, ag_blocks=()):
    tq = min(ATT_TILE, T)
    nq = T // tq

    qi_tab, ki_tab = _causal_pairs(nq, by_query=True)

    hp = ATT_HEADS

    n_ag = len(ag_blocks)
    n_heads, n_pairs = MLA_HEADS // hp, len(qi_tab)

    def body(qi_ref, ki_ref, q_ref, k_ref, v_ref, *rest):
        ag_in, (o_ref, lse_ref), rest = rest[:n_ag], rest[n_ag:n_ag + 2], rest[n_ag + 2:]
        ag_out, (m_s, acc_s), ag_sems = rest[:n_ag], rest[n_ag:n_ag + 2], rest[n_ag + 2:]
        t = pl.program_id(1)
        qi, ki = qi_ref[t], ki_ref[t]
        if n_ag:
            @pl.when((pl.program_id(0) == 0) & (t == 0))
            def _():
                _ag_start(ag_in, ag_out, ag_sems)

        @pl.when(ki == 0)
        def _():
            m_s[...] = jnp.full_like(m_s, NEG)
            acc_s[...] = jnp.zeros_like(acc_s)

        def step(masked):
            for hh in range(hp):
                hs = slice(hh * HEAD_PAD, (hh + 1) * HEAD_PAD)
                s_t = _dot_nt(k_ref[:, hs], q_ref[:, hs])
                if masked:
                    key = lax.broadcasted_iota(jnp.int32, (tq, tq), 0)
                    qry = lax.broadcasted_iota(jnp.int32, (tq, tq), 1)
                    s_t = jnp.where(key <= qry, s_t, NEG)
                m_old = m_s[hh]
                m_new = jnp.maximum(m_old, jnp.max(s_t, axis=0, keepdims=True))
                p_t = jnp.exp(s_t - m_new)
                acc_s[hh] = jnp.exp(m_old - m_new) * acc_s[hh] + _dot_tn(v_ref[:, hs], p_t)
                m_s[hh] = m_new

        @pl.when(ki < qi)
        def _():
            step(False)

        @pl.when(ki == qi)
        def _():
            step(True)
            real = lax.broadcasted_iota(jnp.int32, (HEAD_PAD, tq), 0) < V_DIM
            for hh in range(hp):
                hs = slice(hh * HEAD_PAD, (hh + 1) * HEAD_PAD)
                acc = acc_s[hh]
                l = acc[V_DIM:V_DIM + 1]
                o_ref[:, hs] = jnp.where(real, acc / l, 0.0).T
                lse_ref[:, hs] = jnp.broadcast_to(m_s[hh] + jnp.log(l), (HEAD_PAD, tq)).T

        if n_ag:
            @pl.when((pl.program_id(0) == n_heads - 1) & (t == n_pairs - 1))
            def _():
                _ag_finish(ag_in, ag_out, ag_sems)

    q_spec = pl.BlockSpec((tq, hp * HEAD_PAD), lambda h, t, qi_ref, ki_ref: (qi_ref[t], h))
    kv_spec = pl.BlockSpec((tq, hp * HEAD_PAD), lambda h, t, qi_ref, ki_ref: (ki_ref[t], h))
    any_spec = pl.BlockSpec(memory_space=pl.ANY)
    grid_spec = pltpu.PrefetchScalarGridSpec(
        num_scalar_prefetch=2, grid=(n_heads, n_pairs),
        in_specs=[q_spec, kv_spec, kv_spec] + [any_spec] * n_ag, out_specs=[q_spec, q_spec] + [any_spec] * n_ag,
        scratch_shapes=[pltpu.VMEM((hp, 1, tq), F32), pltpu.VMEM((hp, HEAD_PAD, tq), F32)]
        + (_ag_sems(n_ag) if n_ag else []))
    return pl.pallas_call(
        body, name="flash_fwd", grid_spec=grid_spec,
        out_shape=[jax.ShapeDtypeStruct((T, MLA_HEADS * HEAD_PAD), F32)] * 2 + _ag_out_shapes(ag_blocks),
        compiler_params=_cparams(("arbitrary", "arbitrary")),
    )(jnp.asarray(qi_tab), jnp.asarray(ki_tab), qf, kf, vf, *ag_blocks)


def _hg_gates(hf, lb):
    sg = _sigmoid(hf)
    f = lb + (1.0 - lb) * sg
    return sg, f, jnp.log(f), 1.0 - f


def _tri(n, lower):
    r = lax.broadcasted_iota(jnp.int32, (n, n), 0)
    c = lax.broadcasted_iota(jnp.int32, (n, n), 1)
    return jnp.where((c <= r) if lower else (c >= r), 1.0, 0.0).astype(F32)


def _hg_intra(q, k, b):
    C, S = HG_CHUNK, HG_SUB
    row_c = lax.broadcasted_iota(jnp.int32, (C, HG_DIM), 0)
    row_s = lax.broadcasted_iota(jnp.int32, (S, HG_DIM), 0)
    lane_c = lax.broadcasted_iota(jnp.int32, (S, C), 1)
    blocks, saved = [], []
    for blk in range(C // S):
        lo = blk * S
        q_b, k_b, b_b = q[lo:lo + S], k[lo:lo + S], b[lo:lo + S]
        a_b = jnp.zeros((S, C), F32)
        for j in range(S):
            w = jnp.exp(jnp.where(row_s >= j, b_b - b_b[j:j + 1], NEG))
            col = jnp.sum(q_b * (k_b[j:j + 1] * w), axis=1, keepdims=True)
            a_b = jnp.where(lane_c == lo + j, col, a_b)
        if blk > 0:
            ref = b[lo - 1:lo]
            q_e = jnp.exp(b_b - ref)
            q_t = q_b * q_e
            k_e = jnp.exp(jnp.where(row_c < lo, ref - b, NEG))
            a_b = a_b + _dot_nt(q_t, k * k_e)
            saved.append((q_t, k_e, q_e))
        else:
            saved.append(None)
        blocks.append(a_b)
    return jnp.concatenate(blocks, axis=0), saved


def _hgrn_fwd(hq, hf, hi, lb, T):
    rb = min(HG_BLOCK, T)
    ncb = rb // HG_CHUNK

    def body(hq_ref, hf_ref, hi_ref, lb_ref, o_ref, s0_ref, st_ref):
        @pl.when(pl.program_id(0) == 0)
        def _():
            st_ref[...] = jnp.zeros_like(st_ref)

        tril = _tri(HG_CHUNK, True)

        def chunk(c, carry):
            rows = pl.ds(pl.multiple_of(c * HG_CHUNK, HG_CHUNK), HG_CHUNK)
            _, _, logf, kk = _hg_gates(hf_ref[rows, :], lb_ref[...])
            b = _dot_hi(tril, logf)
            q_all, v_all = hq_ref[rows, :], hi_ref[rows, :]
            outs = []
            for h in range(HG_HEADS):
                ls = slice(h * HG_DIM, (h + 1) * HG_DIM)
                q, k, v, bh = q_all[:, ls], kk[:, ls], v_all[:, ls], b[:, ls]
                st = st_ref[h]
                s0_ref[c, h * HG_DIM:(h + 1) * HG_DIM, :] = st
                b_end = bh[HG_CHUNK - 1:HG_CHUNK]
                a, _ = _hg_intra(q, k, bh)
                outs.append(_dot_nt(q * jnp.exp(bh), st) + _dot(a, v))
                st_ref[h] = st * jnp.exp(b_end) + _dot_tn(v, k * jnp.exp(b_end - bh))
            o_ref[rows, :] = jnp.concatenate(outs, axis=1)
            return carry

        lax.fori_loop(0, ncb, chunk, 0)

    row = pl.BlockSpec((rb, HG_W), lambda i: (i, 0))
    return pl.pallas_call(
        body, name="hgrn_fwd", grid=(T // rb,),
        in_specs=[row, row, row, pl.BlockSpec((1, HG_W), lambda i: (0, 0))],
        out_specs=[row, pl.BlockSpec((ncb, HG_W, HG_DIM), lambda i: (i, 0, 0))],
        out_shape=[jax.ShapeDtypeStruct((T, HG_W), F32), jax.ShapeDtypeStruct((T // HG_CHUNK, HG_W, HG_DIM), F32)],
        scratch_shapes=[pltpu.VMEM((HG_HEADS, HG_DIM, HG_DIM), F32)],
        compiler_params=_cparams(("arbitrary",)),
    )(hq, hf, hi, lb)


def _hgrn_bwd(hq, hf, hi, do, s0, lb, T, xchg=()):
    rb = min(HG_BLOCK, T)
    ncb = rb // HG_CHUNK
    nb = T // rb
    C, S = HG_CHUNK, HG_SUB
    n_x = len(xchg)

    def body(hq_ref, hf_ref, hi_ref, do_ref, s0_ref, lb_ref, *rest):
        x_in, (dq_ref, df_ref, dv_ref, dlb_ref), rest = rest[:n_x], rest[n_x:n_x + 4], rest[n_x + 4:]
        x_out, dst_ref, x_sems = rest[:n_x], rest[n_x], rest[n_x + 1:]

        @pl.when(pl.program_id(0) == 0)
        def _():
            dst_ref[...] = jnp.zeros_like(dst_ref)
            dlb_ref[...] = jnp.zeros_like(dlb_ref)
            for cp in _xchips_copies(x_in, x_out, x_sems) if n_x else ():
                cp.start()

        tril, triu = _tri(C, True), _tri(C, False)
        row_cc = lax.broadcasted_iota(jnp.int32, (C, C), 0)
        col_cc = lax.broadcasted_iota(jnp.int32, (C, C), 1)
        row_s = lax.broadcasted_iota(jnp.int32, (S, HG_DIM), 0)
        lane_sc = lax.broadcasted_iota(jnp.int32, (S, C), 1)
        last_row = lax.broadcasted_iota(jnp.int32, (C, HG_DIM), 0) == C - 1
        lb_v = lb_ref[...]

        def chunk(cc, carry):
            c = ncb - 1 - cc
            rows = pl.ds(pl.multiple_of(c * C, C), C)
            hf_c = hf_ref[rows, :]
            sg, f, logf, kk = _hg_gates(hf_c, lb_v)
            b = _dot_hi(tril, logf)
            q_all, v_all, do_all = hq_ref[rows, :], hi_ref[rows, :], do_ref[rows, :]
            dq_o, dk_o, dv_o, db_o = [], [], [], []
            for h in range(HG_HEADS):
                ls = slice(h * HG_DIM, (h + 1) * HG_DIM)
                q, k, v, bh, d_o = q_all[:, ls], kk[:, ls], v_all[:, ls], b[:, ls], do_all[:, ls]
                st0 = s0_ref[c, h * HG_DIM:(h + 1) * HG_DIM, :]
                dst = dst_ref[h]
                b_end = bh[C - 1:C]
                e_b, e_end = jnp.exp(bh), jnp.exp(b_end)
                e_rem = jnp.exp(b_end - bh)
                qe, kd = q * e_b, k * e_rem
                st_end = st0 * e_end + _dot_tn(v, kd)
                a, saved = _hg_intra(q, k, bh)
                d_a = jnp.where(col_cc <= row_cc, _dot_nt(d_o, v), 0.0)
                dv = _dot_tn(a, d_o) + _dot_nt(kd, dst)
                dq = e_b * _dot(d_o, st0)
                dk = e_rem * _dot(v, dst)
                dq_blocks, dk_diag = [], []
                for blk in range(C // S):
                    lo = blk * S
                    q_b, k_b, b_b = q[lo:lo + S], k[lo:lo + S], bh[lo:lo + S]
                    da_b = d_a[lo:lo + S]
                    dq_b = jnp.zeros((S, HG_DIM), F32)
                    dk_b = jnp.zeros((S, HG_DIM), F32)
                    for j in range(S):
                        w = jnp.exp(jnp.where(row_s >= j, b_b - b_b[j:j + 1], NEG))
                        col = jnp.sum(jnp.where(lane_sc == lo + j, da_b, 0.0), axis=1, keepdims=True)
                        dq_b = dq_b + col * (k_b[j:j + 1] * w)
                        dk_row = jnp.sum(col * (q_b * w), axis=0, keepdims=True)
                        dk_b = jnp.where(row_s == j, dk_row, dk_b)
                    if blk > 0:
                        q_t, k_e, q_e = saved[blk]
                        da_off = jnp.where(lane_sc < lo, da_b, 0.0)
                        dq_b = dq_b + _dot(da_off, k * k_e) * q_e
                        dk = dk + _dot_tn(da_off, q_t) * k_e
                    dq_blocks.append(dq_b)
                    dk_diag.append(dk_b)
                dq = dq + jnp.concatenate(dq_blocks, axis=0)
                dk = dk + jnp.concatenate(dk_diag, axis=0)
                extra = jnp.sum(dst * st_end, axis=0, keepdims=True)
                db_o.append(q * dq - k * dk + jnp.where(last_row, extra, 0.0))
                dst_ref[h] = dst * e_end + _dot_tn(d_o, qe)
                dq_o.append(dq)
                dk_o.append(dk)
                dv_o.append(dv)
            dlogf = _dot_hi(triu, jnp.concatenate(db_o, axis=1))
            d_f = dlogf / f - jnp.concatenate(dk_o, axis=1)
            dq_ref[rows, :] = jnp.concatenate(dq_o, axis=1)
            dv_ref[rows, :] = jnp.concatenate(dv_o, axis=1)
            df_ref[rows, :] = d_f * (1.0 - lb_v) * sg * (1.0 - sg)
            dlb_ref[...] += jnp.sum(d_f * (1.0 - sg), axis=0, keepdims=True)
            return carry

        lax.fori_loop(0, ncb, chunk, 0)

        if n_x:
            @pl.when(pl.program_id(0) == nb - 1)
            def _():
                for cp in _xchips_copies(x_in, x_out, x_sems):
                    cp.wait()

    row = pl.BlockSpec((rb, HG_W), lambda i: (nb - 1 - i, 0))
    one = pl.BlockSpec((1, HG_W), lambda i: (0, 0))
    any_spec = pl.BlockSpec(memory_space=pl.ANY)
    return pl.pallas_call(
        body, name="hgrn_bwd", grid=(nb,),
        in_specs=[row, row, row, row, pl.BlockSpec((ncb, HG_W, HG_DIM), lambda i: (nb - 1 - i, 0, 0)), one]
        + [any_spec] * n_x,
        out_specs=[row, row, row, one] + [any_spec] * n_x,
        out_shape=[jax.ShapeDtypeStruct((T, HG_W), F32)] * 3 + [jax.ShapeDtypeStruct((1, HG_W), F32)]
        + _xchips_out_shapes(xchg),
        scratch_shapes=[pltpu.VMEM((HG_HEADS, HG_DIM, HG_DIM), F32)] + (_xchips_sems(n_x) if n_x else []),
        compiler_params=_cparams(("arbitrary",)),
    )(hq, hf, hi, do, s0, lb, *xchg)


def _silu_parts(x):
    sg = _sigmoid(x)
    return x * sg, sg * (1.0 + x * (1.0 - sg))


def _merge_fwd(attn, o, hg, bg, x, g_out, w_bra, w_brb, w_out, T, tm):
    def body(i, attn_ref, o_ref, hg_ref, bg_ref, x_ref, g_ref, wa_ref, wb_ref, wo_ref,
             x1_ref, ya_ref, yb_ref, m_ref, rec_ref):
        recs = []
        for h in range(HG_HEADS):
            ls = slice(h * HG_DIM, (h + 1) * HG_DIM)
            oh, _ = _rms(o_ref[:, ls])
            recs.append(oh * g_ref[...] * _silu_parts(hg_ref[:, ls])[0])
        rec = jnp.concatenate(recs, axis=1)
        ya = _dot(attn_ref[...], wa_ref[...])
        yb = _dot(rec, wb_ref[...])
        m = _sigmoid(bg_ref[:, :D_MODEL]) * ya + _sigmoid(bg_ref[:, D_MODEL:]) * yb
        x1_ref[...] = x_ref[...] + _dot(m, wo_ref[...])
        ya_ref[...] = ya
        yb_ref[...] = yb
        m_ref[...] = m.astype(MM)
        rec_ref[...] = rec.astype(MM)

    return _row_call("merge_fwd", body, T, tm, [attn, o, hg, bg, x], [g_out, w_bra, w_brb, w_out],
                     [(D_MODEL, F32), (D_MODEL, F32), (D_MODEL, F32), (D_MODEL, MM), (HG_W, MM)], [], VMEM_LIMIT)


def _ffn_fwd(x1, g_ffn, w_g, w_u, w_d, T, tm):
    def body(i, x1_ref, g_ref, wg_ref, wu_ref, wd_ref, x2_ref, gt_ref, up_ref, h2_ref):
        x1 = x1_ref[...]
        h2 = (_rms(x1)[0] * g_ref[...]).astype(MM)
        gt = jnp.dot(h2, wg_ref[...], preferred_element_type=F32)
        up = jnp.dot(h2, wu_ref[...], preferred_element_type=F32)
        a = _silu_parts(gt)[0] * up
        x2_ref[...] = x1 + _dot(a, wd_ref[...])
        gt_ref[...] = gt
        up_ref[...] = up
        h2_ref[...] = h2

    return _row_call("ffn_fwd", body, T, tm, [x1], [g_ffn, w_g, w_u, w_d],
                     [(D_MODEL, F32), (FFN, F32), (FFN, F32), (D_MODEL, MM)], [], VMEM_LIMIT)


def _ple_loss(x2, p, tgt, g_pg, g_post, w_pg, w_pp, T, tm):
    def body(i, x2_ref, p_ref, t_ref, gpg_ref, gpo_ref, wpg_ref, wpp_ref,
             dx2_ref, n3_ref, dz_ref, du_ref, loss_ref, dgpo_ref, dgpg_ref):
        x2 = x2_ref[...]
        p_mm = p_ref[...].astype(MM)
        u = jnp.concatenate([jnp.dot(p_mm, wpp_ref[d], preferred_element_type=F32) for d in range(N_DEV)], axis=1)
        uh, ru = _rms(u)
        e = uh * gpo_ref[...]
        x2h, r3 = _rms(x2)
        n3 = x2h * gpg_ref[...]
        gate = _sigmoid(_dot(n3, wpg_ref[...]))
        diff = x2 + gate * e - t_ref[...]
        dy = diff * (1.0 / D_MODEL)
        de = dy * gate
        dz = dy * e * gate * (1.0 - gate)
        du = _rms_bwd(de * gpo_ref[...], uh, ru)
        dn3 = _dot_nt(dz, wpg_ref[...])
        dx2_ref[...] = dy + _rms_bwd(dn3 * gpg_ref[...], x2h, r3)
        n3_ref[...] = n3.astype(MM)
        dz_ref[...] = dz.astype(MM)
        du_ref[...] = du.astype(MM)
        _acc(loss_ref, i, jnp.sum(diff * diff, axis=0, keepdims=True) * (0.5 / D_MODEL))
        _acc(dgpo_ref, i, jnp.sum(de * uh, axis=0, keepdims=True))
        _acc(dgpg_ref, i, jnp.sum(dn3 * x2h, axis=0, keepdims=True))

    vec = ((1, D_MODEL), F32)
    return _row_call("ple_loss", body, T, tm, [x2, p, tgt], [g_pg, g_post, w_pg, w_pp],
                     [(D_MODEL, F32), (D_MODEL, MM), (D_MODEL, MM), (D_MODEL, MM)], [vec, vec, vec], VMEM_LIMIT)


def _ffn_bwd(dx2, x1, gt, up, g_ffn, w_g, w_u, w_d, T, tm):
    def body(i, dx2_ref, x1_ref, gt_ref, up_ref, g_ref, wg_ref, wu_ref, wd_ref,
             dx1_ref, a_ref, dgt_ref, dup_ref, dg_ref):
        dx2 = dx2_ref[...]
        x1h, r = _rms(x1_ref[...])
        up = up_ref[...]
        silu, dsilu = _silu_parts(gt_ref[...])
        da = _dot_nt(dx2, wd_ref[...])
        dgt = (da * up * dsilu).astype(MM)
        dup = (da * silu).astype(MM)
        dh2 = (lax.dot_general(dgt, wg_ref[...], (((1,), (1,)), ((), ())), preferred_element_type=F32)
               + lax.dot_general(dup, wu_ref[...], (((1,), (1,)), ((), ())), preferred_element_type=F32))
        dx1_ref[...] = dx2 + _rms_bwd(dh2 * g_ref[...], x1h, r)
        a_ref[...] = (silu * up).astype(MM)
        dgt_ref[...] = dgt
        dup_ref[...] = dup
        _acc(dg_ref, i, jnp.sum(dh2 * x1h, axis=0, keepdims=True))

    return _row_call("ffn_bwd", body, T, tm, [dx2, x1, gt, up], [g_ffn, w_g, w_u, w_d],
                     [(D_MODEL, F32), (FFN, MM), (FFN, MM), (FFN, MM)], [((1, D_MODEL), F32)], VMEM_LIMIT)


def _merge_bwd(dx1, ya, yb, bg, o, hg, g_out, w_bra, w_brb, w_out, T, tm):
    def body(i, dx1_ref, ya_ref, yb_ref, bg_ref, o_ref, hg_ref, g_ref, wa_ref, wb_ref, wo_ref,
             dattn_ref, do_ref, dhg_ref, dbg_ref, dya_ref, dyb_ref, dg_ref):
        dm = _dot_nt(dx1_ref[...], wo_ref[...])
        ga, gb = _sigmoid(bg_ref[:, :D_MODEL]), _sigmoid(bg_ref[:, D_MODEL:])
        dya, dyb = (dm * ga).astype(MM), (dm * gb).astype(MM)
        dbg_ref[:, :D_MODEL] = dm * ya_ref[...] * ga * (1.0 - ga)
        dbg_ref[:, D_MODEL:] = dm * yb_ref[...] * gb * (1.0 - gb)
        dya_ref[...] = dya
        dyb_ref[...] = dyb
        dattn_ref[...] = lax.dot_general(dya, wa_ref[...], (((1,), (1,)), ((), ())), preferred_element_type=F32)
        drec = lax.dot_general(dyb, wb_ref[...], (((1,), (1,)), ((), ())), preferred_element_type=F32)
        dg = jnp.zeros((1, HG_DIM), F32)
        for h in range(HG_HEADS):
            ls = slice(h * HG_DIM, (h + 1) * HG_DIM)
            oh, r = _rms(o_ref[:, ls])
            silu, dsilu = _silu_parts(hg_ref[:, ls])
            dr = drec[:, ls]
            dhg_ref[:, ls] = dr * oh * g_ref[...] * dsilu
            don = dr * silu
            dg = dg + jnp.sum(don * oh, axis=0, keepdims=True)
            do_ref[:, ls] = _rms_bwd(don * g_ref[...], oh, r)
        _acc(dg_ref, i, dg)

    return _row_call("merge_bwd", body, T, tm, [dx1, ya, yb, bg, o, hg], [g_out, w_bra, w_brb, w_out],
                     [(D_MODEL, F32), (HG_W, F32), (HG_W, F32), (2 * D_MODEL, F32), (D_MODEL, MM), (D_MODEL, MM)],
                     [((1, HG_DIM), F32)], VMEM_LIMIT)


def _flash_bwd(qf, kf, vf, o, do, lse, T, xchg=()):
    tq = min(ATT_TILE, T)
    nq = T // tq

    qi_tab, ki_tab = _causal_pairs(nq, by_query=False)

    n_x = len(xchg)
    hp = ATT_HEADS
    n_heads, n_pairs = MLA_HEADS // hp, len(qi_tab)

    def body(qi_ref, ki_ref, q_ref, k_ref, v_ref, o_ref, do_ref, lse_ref, *rest):
        x_in, (dq_ref, dk_ref, dv_ref), rest = rest[:n_x], rest[n_x:n_x + 3], rest[n_x + 3:]
        x_out, x_sems = rest[:n_x], rest[n_x:]
        t = pl.program_id(1)
        qi, ki = qi_ref[t], ki_ref[t]
        if n_x:
            @pl.when((pl.program_id(0) == 0) & (t == 0))
            def _():
                for cp in _xchips_copies(x_in, x_out, x_sems):
                    cp.start()

        @pl.when(t == 0)
        def _():
            dq_ref[...] = jnp.zeros_like(dq_ref)

        def step(first):
            rows = pl.ds(pl.multiple_of(qi * tq, tq), tq)
            for hh in range(hp):
                hs = slice(hh * HEAD_PAD, (hh + 1) * HEAD_PAD)
                q, k, d_o = q_ref[:, hs], k_ref[:, hs], do_ref[:, hs]
                s = _dot_nt(q, k)
                if first:
                    row = lax.broadcasted_iota(jnp.int32, (tq, tq), 0)
                    col = lax.broadcasted_iota(jnp.int32, (tq, tq), 1)
                    s = jnp.where(col <= row, s, NEG)
                p = jnp.exp(s - lse_ref[:, hh * HEAD_PAD:hh * HEAD_PAD + 1])
                delta = jnp.sum(d_o * o_ref[:, hs], axis=1, keepdims=True)
                ds = p * (_dot_nt(d_o, v_ref[:, hs]) - delta)
                dq_ref[rows, hs] += _dot(ds, k)
                if first:
                    dv_ref[:, hs] = _dot_tn(p, d_o)
                    dk_ref[:, hs] = _dot_tn(ds, q)
                else:
                    dv_ref[:, hs] += _dot_tn(p, d_o)
                    dk_ref[:, hs] += _dot_tn(ds, q)

        @pl.when(qi == ki)
        def _():
            step(True)

        @pl.when(qi > ki)
        def _():
            step(False)

        if n_x:
            @pl.when((pl.program_id(0) == n_heads - 1) & (t == n_pairs - 1))
            def _():
                for cp in _xchips_copies(x_in, x_out, x_sems):
                    cp.wait()

    q_spec = pl.BlockSpec((tq, hp * HEAD_PAD), lambda h, t, qi_ref, ki_ref: (qi_ref[t], h))
    kv_spec = pl.BlockSpec((tq, hp * HEAD_PAD), lambda h, t, qi_ref, ki_ref: (ki_ref[t], h))
    any_spec = pl.BlockSpec(memory_space=pl.ANY)
    w = MLA_HEADS * HEAD_PAD
    grid_spec = pltpu.PrefetchScalarGridSpec(
        num_scalar_prefetch=2, grid=(n_heads, n_pairs),
        in_specs=[q_spec, kv_spec, kv_spec, q_spec, q_spec, q_spec] + [any_spec] * n_x,
        out_specs=[pl.BlockSpec((T, hp * HEAD_PAD), lambda h, t, qi_ref, ki_ref: (0, h)), kv_spec, kv_spec]
        + [any_spec] * n_x,
        scratch_shapes=_xchips_sems(n_x) if n_x else [])
    return pl.pallas_call(
        body, name="flash_bwd", grid_spec=grid_spec,
        out_shape=[jax.ShapeDtypeStruct((T, w), F32)] * 3 + _xchips_out_shapes(xchg),
        compiler_params=_cparams(("arbitrary", "arbitrary")),
    )(jnp.asarray(qi_tab), jnp.asarray(ki_tab), qf, kf, vf, o, do, lse, *xchg)


def _mla_heads_bwd(d_out, saved, g_pad, cos_t, sin_t, first):
    d_raw, dg = [], jnp.zeros((1, HEAD_PAD), F32)
    for h in range(MLA_HEADS):
        xh, r = saved[h]
        dy = d_out[:, h * HEAD_PAD:(h + 1) * HEAD_PAD]
        dn = dy * cos_t + _rope_swap(dy * sin_t, first)
        dg = dg + jnp.sum(dn * xh, axis=0, keepdims=True)
        d_raw.append(_rms_bwd(dn * g_pad, xh, r, QK_DIM))
    return d_raw, dg


def _mla_prep_bwd(cq, ckv, kr, pos, dqf, dkf, dvf, g_qa, g_kva, g_qn, g_kn, w_uq, w_ukv, T, tm):
    def body(i, cq_ref, ckv_ref, kr_ref, pos_ref, dq_ref, dk_ref, dv_ref,
             gqa_ref, gkva_ref, gqn_ref, gkn_ref, wuq_ref, wukv_ref,
             dcq_ref, dckv_ref, dkr_ref, dqraw_ref, dkv_ref, cqn_ref, ckvn_ref,
             dgqa_ref, dgkva_ref, dgqn_ref, dgkn_ref):
        cos_t, sin_t, first = _rope_tables(pos_ref[...], tm)
        cqh, rq = _rms(cq_ref[...])
        ckvh, rkv = _rms(ckv_ref[...])
        cqn, ckvn = cqh * gqa_ref[...], ckvh * gkva_ref[...]
        q_raw, k_raw, _ = _mla_raw_heads(cqn, ckvn, kr_ref[...], wuq_ref, wukv_ref, tm)
        _, q_saved = _mla_heads_fwd(q_raw, gqn_ref[...], cos_t, sin_t, first)
        _, k_saved = _mla_heads_fwd(k_raw, gkn_ref[...], cos_t, sin_t, first)
        dq_heads, dgqn = _mla_heads_bwd(dq_ref[...] * ATT_SCALE, q_saved, gqn_ref[...], cos_t, sin_t, first)
        dk_heads, dgkn = _mla_heads_bwd(dk_ref[...], k_saved, gkn_ref[...], cos_t, sin_t, first)
        lane = lax.broadcasted_iota(jnp.int32, (tm, HEAD_PAD), 1)
        nope = lane < QK_NOPE
        dcqn = jnp.zeros((tm, Q_RANK), F32)
        dckvn = jnp.zeros((tm, KV_RANK), F32)
        dkr = jnp.zeros((tm, HEAD_PAD), F32)
        for h in range(MLA_HEADS):
            hs = slice(h * HEAD_PAD, (h + 1) * HEAD_PAD)
            dq_h = dq_heads[h].astype(MM)
            dkv_h = jnp.where(nope, dk_heads[h], pltpu.roll(dv_ref[:, hs], V_DIM, 1)).astype(MM)
            dqraw_ref[:, hs] = dq_h
            dkv_ref[:, hs] = dkv_h
            dcqn = dcqn + lax.dot_general(dq_h, wuq_ref[h], (((1,), (1,)), ((), ())), preferred_element_type=F32)
            dckvn = dckvn + lax.dot_general(dkv_h, wukv_ref[h], (((1,), (1,)), ((), ())), preferred_element_type=F32)
            dkr = dkr + dk_heads[h]
        dkr_ref[...] = jnp.where((lane >= QK_NOPE) & (lane < QK_DIM), dkr, 0.0)
        dcq_ref[...] = _rms_bwd(dcqn * gqa_ref[...], cqh, rq)
        dckv_ref[...] = _rms_bwd(dckvn * gkva_ref[...], ckvh, rkv)
        cqn_ref[...] = cqn.astype(MM)
        ckvn_ref[...] = ckvn.astype(MM)
        _acc(dgqa_ref, i, jnp.sum(dcqn * cqh, axis=0, keepdims=True))
        _acc(dgkva_ref, i, jnp.sum(dckvn * ckvh, axis=0, keepdims=True))
        _acc(dgqn_ref, i, dgqn)
        _acc(dgkn_ref, i, dgkn)

    w = MLA_HEADS * HEAD_PAD
    return _row_call(
        "mla_prep_bwd", body, T, tm, [cq, ckv, kr, pos, dqf, dkf, dvf], [g_qa, g_kva, g_qn, g_kn, w_uq, w_ukv],
        [(Q_RANK, F32), (KV_RANK, F32), (HEAD_PAD, F32), (w, MM), (w, MM), (Q_RANK, MM), (KV_RANK, MM)],
        [((1, Q_RANK), F32), ((1, KV_RANK), F32), ((1, HEAD_PAD), F32), ((1, HEAD_PAD), F32)], VMEM_LIMIT)


def _in_proj_bwd(x, dx1, dsecs, g_mix, w_in_pad, T, tm):
    def body(i, x_ref, dx1_ref, *rest):
        d_refs, (g_ref, w_ref, dx_ref, dp_ref, dg_ref) = rest[:len(SECTIONS)], rest[len(SECTIONS):]
        dh = jnp.zeros((tm, D_MODEL), F32)
        for (s, n), d_ref in zip(SECTIONS, d_refs):
            d = d_ref[...].astype(MM)
            dp_ref[:, s:s + n] = d
            dh = dh + lax.dot_general(d, w_ref[:, s:s + n], (((1,), (1,)), ((), ())), preferred_element_type=F32)
        xh, r = _rms(x_ref[...])
        dx_ref[...] = dx1_ref[...] + _rms_bwd(dh * g_ref[...], xh, r)
        _acc(dg_ref, i, jnp.sum(dh * xh, axis=0, keepdims=True))

    return _row_call("in_proj_bwd", body, T, tm, [x, dx1, *dsecs], [g_mix, w_in_pad],
                     [(D_MODEL, F32), (IN_PAD, MM)], [((1, D_MODEL), F32)], VMEM_LIMIT)


def _pick_block(n, cap):
    best = None
    for cand in range(128, min(n, cap) + 1, 128):
        if n % cand == 0:
            best = cand
    return n if best is None else best


def _pick_rows(n, cap):
    best = n
    for cand in range(8, min(n, cap) + 1, 8):
        if n % cand == 0:
            best = cand
    return best


def _matmul_tn(name, a, b, blocked=None):
    T, M = a.shape
    N = b.shape[1]
    bm, bk = _pick_block(M, 1408), min(512, T)
    bn = _pick_block(N, 2560)

    def body(a_ref, b_ref, c_ref):
        @pl.when(pl.program_id(2) == 0)
        def _():
            c_ref[...] = jnp.zeros_like(c_ref)

        if blocked is None:
            c_ref[...] += _dot_tn(a_ref[...], b_ref[...])
        else:
            a = a_ref[...].astype(MM)
            for d in range(bn // blocked):
                c_ref[d] += _dot_tn(a, b_ref[:, d * blocked:(d + 1) * blocked])

    if blocked is None:
        out_spec = pl.BlockSpec((bm, bn), lambda i, j, k: (i, j))
        out_shape = jax.ShapeDtypeStruct((M, N), F32)
    else:
        assert bn == N
        out_spec = pl.BlockSpec((N // blocked, bm, blocked), lambda i, j, k: (0, i, 0))
        out_shape = jax.ShapeDtypeStruct((N // blocked, M, blocked), F32)
    return pl.pallas_call(
        body, name=name, grid=(M // bm, N // bn, T // bk),
        in_specs=[pl.BlockSpec((bk, bm), lambda i, j, k: (k, i)), pl.BlockSpec((bk, bn), lambda i, j, k: (k, j))],
        out_specs=out_spec, out_shape=out_shape,
        compiler_params=_cparams(("parallel", "parallel", "arbitrary"), VMEM_LIMIT),
    )(a, b)


def _pad_gain(g, n):
    return jnp.pad(g.reshape(1, -1), ((0, 0), (0, n - g.shape[-1])))


def _cols_full(g):
    return jnp.swapaxes(g, 0, 1).reshape(g.shape[1], -1)


def _cols_blocked(full):
    r = full.shape[0]
    return jnp.swapaxes(full.reshape(r, N_DEV, -1), 0, 1)


GROUP_A = ("w_ffn_gate", "w_ffn_up", "w_ffn_down", "w_ple_gate", "w_ple_proj")
GROUP_B = ("w_branch", "w_out")
GROUP_C = ("w_in", "w_uq", "w_ukv")
EARLY = GROUP_C
LATE = GROUP_B + GROUP_A


def _local_step(x, p, pos, tgt, small, big, late_blocks=None, core=None):
    T = x.shape[0]
    tm = min(ROW_TILE, T)
    w_in = _cols_full(big["w_in"])
    zeros = lambda n: jnp.zeros((D_MODEL, n), w_in.dtype)
    w_in_pad = jnp.concatenate(
        [w_in[:, :640], zeros(QK_NOPE), w_in[:, 640:672], zeros(HEAD_PAD - QK_DIM), w_in[:, 672:]], axis=1)
    w_uq = jnp.pad(big["w_uq"], ((0, 0), (0, 0), (0, HEAD_PAD - QK_DIM)))
    w_ukv = big["w_ukv"]

    g_mix, g_qa, g_kva = small["mix_norm_g"], small["q_a_norm_g"], small["kv_a_norm_g"]
    g_qn, g_kn = _pad_gain(small["q_norm_g"], HEAD_PAD), _pad_gain(small["k_norm_g"], HEAD_PAD)
    g_out, g_ffn = small["hg_out_norm_g"], small["ffn_norm_g"]
    g_pg, g_post = small["ple_gate_norm_g"], small["ple_post_norm_g"]
    logits = small["hg_lb_logits"]
    lb = _lower_bound(logits)

    h, cq, ckv, kr, hq, hf, hi, hg, bg = _in_proj_fwd(x, g_mix, w_in_pad, T, tm)
    qf, kf, vf = _mla_prep_fwd(cq, ckv, kr, pos, g_qa, g_kva, g_qn, g_kn, w_uq, w_ukv, T, tm)
    if late_blocks is None:
        attn, lse = _flash_fwd(qf, kf, vf, T)
    else:
        attn, lse, *late = _flash_fwd(qf, kf, vf, T, ag_blocks=[late_blocks[n] for n in LATE])
        big = {**big, **dict(zip(LATE, late))}
    w_branch = jnp.moveaxis(big["w_branch"].reshape(N_DEV, 2, HG_W, HEAD_PAD), 0, 2).reshape(2, HG_W, D_MODEL)
    w_bra = jnp.pad(w_branch[0].reshape(MLA_HEADS, V_DIM, D_MODEL),
                    ((0, 0), (0, HEAD_PAD - V_DIM), (0, 0))).reshape(MLA_HEADS * HEAD_PAD, D_MODEL)
    w_brb = w_branch[1]
    w_out = big["w_out"].reshape(D_MODEL, D_MODEL)
    w_g, w_u = _cols_full(big["w_ffn_gate"]), _cols_full(big["w_ffn_up"])
    w_d = big["w_ffn_down"].reshape(FFN, D_MODEL)
    w_pg, w_pp = big["w_ple_gate"].reshape(D_MODEL, D_MODEL), big["w_ple_proj"]
    o, s0 = _hgrn_fwd(hq, hf, hi, lb, T)
    x1, ya, yb, m, rec = _merge_fwd(attn, o, hg, bg, x, g_out, w_bra, w_brb, w_out, T, tm)
    x2, gt, up, h2 = _ffn_fwd(x1, g_ffn, w_g, w_u, w_d, T, tm)
    dx2, n3, dz, du, loss_p, dg_post, dg_pg = _ple_loss(x2, p, tgt, g_pg, g_post, w_pg, w_pp, T, tm)

    grads, sibs, gots = {}, {}, {}

    def reduce_start(tag, names):
        if core is None:
            return ()
        got = _exchange_sibling("rs_sibling_" + tag, [grads[n] for n in names])
        sibs.update(zip(names, got))
        return [_chip_partial("rs_partial_" + n, grads[n], sibs[n], core) for n in names]

    dx1, a, dgt, dup, dg_ffn = _ffn_bwd(dx2, x1, gt, up, g_ffn, w_g, w_u, w_d, T, tm)
    grads["w_ffn_gate"] = _cols_blocked(_matmul_tn("dw_gate", h2, dgt))
    grads["w_ffn_up"] = _cols_blocked(_matmul_tn("dw_up", h2, dup))
    grads["w_ffn_down"] = _matmul_tn("dw_down", a, dx2).reshape(N_DEV, -1, D_MODEL)
    grads["w_ple_gate"] = _matmul_tn("dw_pg", n3, dz).reshape(N_DEV, -1, D_MODEL)
    grads["w_ple_proj"] = _matmul_tn("dw_pp", p, du, blocked=HEAD_PAD)
    parts_a = reduce_start("a", GROUP_A)

    dattn, do, dhg, dbg, dya, dyb, dg_out = _merge_bwd(dx1, ya, yb, bg, o, hg, g_out, w_bra, w_brb, w_out, T, tm)
    d_bra = _matmul_tn("dw_bra", attn, dya, blocked=HEAD_PAD)
    d_bra = d_bra.reshape(N_DEV, MLA_HEADS, HEAD_PAD, HEAD_PAD)[:, :, :V_DIM].reshape(N_DEV, HG_W, HEAD_PAD)
    grads["w_branch"] = jnp.concatenate([d_bra, _matmul_tn("dw_brb", rec, dyb, blocked=HEAD_PAD)], axis=1)
    grads["w_out"] = _matmul_tn("dw_out", m, dx1).reshape(N_DEV, -1, D_MODEL)
    parts_b = reduce_start("b", GROUP_B)

    dhq, dhf, dhi, dlb, *got_a = _hgrn_bwd(hq, hf, hi, do, s0, lb, T, xchg=parts_a)
    dqf, dkf, dvf, *got_b = _flash_bwd(qf, kf, vf, attn, dattn, lse, T, xchg=parts_b)
    (dcq, dckv, dkr, dq_raw, dkv, cqn, ckvn, dg_qa, dg_kva, dg_qn, dg_kn) = _mla_prep_bwd(
        cq, ckv, kr, pos, dqf, dkf, dvf, g_qa, g_kva, g_qn, g_kn, w_uq, w_ukv, T, tm)
    grad_x, dproj, dg_mix = _in_proj_bwd(x, dx1, [dcq, dckv, dkr, dhq, dhf, dhi, dhg, dbg], g_mix, w_in_pad, T, tm)
    d_in = _matmul_tn("dw_in", h, dproj)
    d_in = jnp.concatenate([d_in[:, :640], d_in[:, 640 + QK_NOPE:640 + QK_DIM], d_in[:, 768:]], axis=1)
    grads["w_in"] = _cols_blocked(d_in)
    grads["w_uq"] = _matmul_tn("dw_uq", cqn, dq_raw, blocked=HEAD_PAD)[:, :, :QK_DIM]
    grads["w_ukv"] = _matmul_tn("dw_ukv", ckvn, dkv, blocked=HEAD_PAD)
    parts_c = reduce_start("c", GROUP_C)
    if core is not None:
        gots.update(zip(GROUP_A, got_a))
        gots.update(zip(GROUP_B, got_b))
        gots.update(zip(GROUP_C, _exchange_chips(parts_c)))

    dl0 = dlb * lb * (1.0 - lb)
    small_g = {
        "mix_norm_g": dg_mix, "q_a_norm_g": dg_qa, "kv_a_norm_g": dg_kva,
        "q_norm_g": dg_qn[:, :QK_DIM], "k_norm_g": dg_kn[:, :QK_DIM],
        "hg_lb_logits": jnp.concatenate([dl0, -dl0], axis=0), "hg_out_norm_g": dg_out,
        "ffn_norm_g": dg_ffn, "ple_gate_norm_g": dg_pg, "ple_post_norm_g": dg_post,
    }
    return loss_p, grad_x, small_g, grads, sibs, gots


def _lower_bound(logits):
    def body(l_ref, lb_ref):
        l = l_ref[...]
        mx = jnp.max(l, axis=0, keepdims=True)
        e = jnp.exp(l - mx)
        lb_ref[...] = e[0:1] / jnp.sum(e, axis=0, keepdims=True)

    return pl.pallas_call(body, name="lower_bound", out_shape=jax.ShapeDtypeStruct((1, HG_W), F32))(logits)


def _my_place():
    return lax.axis_index("x"), lax.axis_index("y"), lax.axis_index("c")


def _all_gather(name, blocks):
    n = len(blocks)

    def body(*refs):
        x_refs, out_refs, sems = refs[:n], refs[n:2 * n], refs[2 * n:]
        _ag_start(x_refs, out_refs, sems)
        _ag_finish(x_refs, out_refs, sems)

    any_spec = pl.BlockSpec(memory_space=pl.ANY)
    return pl.pallas_call(
        body, name=name, out_shape=_ag_out_shapes(blocks),
        in_specs=[any_spec] * n, out_specs=[any_spec] * n, scratch_shapes=_ag_sems(n),
    )(*blocks)


def _ag_out_shapes(blocks):
    return [jax.ShapeDtypeStruct((N_DEV,) + b.shape, b.dtype) for b in blocks]


def _ag_sems(n):
    return [pltpu.SemaphoreType.DMA((7 * n,)), pltpu.SemaphoreType.DMA((7 * n,)), pltpu.SemaphoreType.DMA((n,))]


def _ag_parts(x_refs, out_refs, sems):
    send_sems, recv_sems, local_sems = sems
    x, y, c = _my_place()
    me, sibling = (x, y, c), (x, y, 1 - c)
    chips = [(1 - x, y), (x, 1 - y), (1 - x, 1 - y)]
    n = len(x_refs)

    def copy(a, k, block, to, own=False):
        px, py, pc = block
        dst = out_refs[a].at[4 * px + 2 * py + pc]
        return pltpu.make_async_remote_copy(
            src_ref=x_refs[a] if own else dst, dst_ref=dst, send_sem=send_sems.at[7 * a + k],
            recv_sem=recv_sems.at[7 * a + k], device_id=to, device_id_type=MESH_ID)

    mine = [pltpu.make_async_copy(x_refs[a], out_refs[a].at[4 * x + 2 * y + c], local_sems.at[a]) for a in range(n)]
    first = []
    for a in range(n):
        first.append(copy(a, 0, me, sibling, own=True))
        first += [copy(a, 1 + j, me, (*chip, c), own=True) for j, chip in enumerate(chips)]
    return copy, mine, first, me, sibling, chips, c, n


def _ag_start(x_refs, out_refs, sems):
    _, mine, first, *_ = _ag_parts(x_refs, out_refs, sems)
    for cp in mine + first:
        cp.start()


def _ag_finish(x_refs, out_refs, sems):
    copy, mine, first, me, sibling, chips, c, n = _ag_parts(x_refs, out_refs, sems)
    passed = []
    for j, chip in enumerate(chips):
        for a in range(n):
            copy(a, 1 + j, (*chip, c), me).wait_recv()
            passed.append(copy(a, 4 + j, (*chip, c), sibling))
            passed[-1].start()
    for a in range(n):
        copy(a, 0, sibling, me).wait_recv()
    for j, chip in enumerate(chips):
        for a in range(n):
            copy(a, 4 + j, (*chip, 1 - c), me).wait_recv()
    for cp in first + passed:
        cp.wait_send()
    for cp in mine:
        cp.wait()


def _exchange_sibling(name, gs):
    n = len(gs)

    def body(*refs):
        g_refs, out_refs, (send_sems, recv_sems) = refs[:n], refs[n:2 * n], refs[2 * n:]
        x, y, c = _my_place()
        copies = [pltpu.make_async_remote_copy(
            src_ref=g_refs[a].at[2 * j + 1 - c], dst_ref=out_refs[a].at[j], send_sem=send_sems.at[4 * a + j],
            recv_sem=recv_sems.at[4 * a + j], device_id=(x, y, 1 - c), device_id_type=MESH_ID)
            for a in range(n) for j in range(4)]
        for cp in copies:
            cp.start()
        for cp in copies:
            cp.wait()

    any_spec = pl.BlockSpec(memory_space=pl.ANY)
    return pl.pallas_call(
        body, name=name, out_shape=[jax.ShapeDtypeStruct((4,) + g.shape[1:], g.dtype) for g in gs],
        in_specs=[any_spec] * n, out_specs=[any_spec] * n,
        scratch_shapes=[pltpu.SemaphoreType.DMA((4 * n,)), pltpu.SemaphoreType.DMA((4 * n,))],
    )(*gs)


def _chip_partial(name, g, got, c_idx):
    _, rows, cols = g.shape
    tr = _pick_rows(rows, 512)

    def body(c_ref, g_ref, got_ref, out_ref):
        out_ref[...] = (g_ref[...] + got_ref[...]).astype(MM)

    grid_spec = pltpu.PrefetchScalarGridSpec(
        num_scalar_prefetch=1, grid=(4, rows // tr),
        in_specs=[pl.BlockSpec((1, tr, cols), lambda j, i, c_ref: (2 * j + c_ref[0], i, 0)),
                  pl.BlockSpec((1, tr, cols), lambda j, i, c_ref: (j, i, 0))],
        out_specs=pl.BlockSpec((1, tr, cols), lambda j, i, c_ref: (j, i, 0)))
    return pl.pallas_call(
        body, name=name, grid_spec=grid_spec, out_shape=jax.ShapeDtypeStruct((4, rows, cols), MM),
        compiler_params=_cparams(("parallel", "parallel")),
    )(c_idx, g, got)


def _exchange_chips(parts):
    n = len(parts)

    def body(*refs):
        p_refs, out_refs, sems = refs[:n], refs[n:2 * n], refs[2 * n:]
        for cp in _xchips_copies(p_refs, out_refs, sems):
            cp.start()
        for cp in _xchips_copies(p_refs, out_refs, sems):
            cp.wait()

    any_spec = pl.BlockSpec(memory_space=pl.ANY)
    return pl.pallas_call(
        body, name="rs_chips", out_shape=_xchips_out_shapes(parts),
        in_specs=[any_spec] * n, out_specs=[any_spec] * n, scratch_shapes=_xchips_sems(n),
    )(*parts)


def _xchips_out_shapes(parts):
    return [jax.ShapeDtypeStruct((3,) + p.shape[1:], p.dtype) for p in parts]


def _xchips_sems(n):
    return [pltpu.SemaphoreType.DMA((3 * n,)), pltpu.SemaphoreType.DMA((3 * n,))]


def _xchips_copies(p_refs, out_refs, sems):
    send_sems, recv_sems = sems
    x, y, c = _my_place()
    chips = [(1 - x, y), (x, 1 - y), (1 - x, 1 - y)]
    return [pltpu.make_async_remote_copy(
        src_ref=p_refs[a].at[2 * px + py], dst_ref=out_refs[a].at[k], send_sem=send_sems.at[3 * a + k],
        recv_sem=recv_sems.at[3 * a + k], device_id=(px, py, c), device_id_type=MESH_ID)
        for a in range(len(p_refs)) for k, (px, py) in enumerate(chips)]


def _adamw_math(w, g, m, v):
    m = ADAM_B1 * m + (1.0 - ADAM_B1) * g
    v = ADAM_B2 * v + (1.0 - ADAM_B2) * jnp.square(g)
    m_hat = m / (1.0 - ADAM_B1 ** ADAM_STEP)
    v_hat = v / (1.0 - ADAM_B2 ** ADAM_STEP)
    delta = -ADAM_LR * (m_hat / (jnp.sqrt(v_hat) + ADAM_EPS) + ADAM_WD * w)
    return delta, m, v


def _sum_adamw(name, g, sib, got, w, m, v, slot_idx, chip_idx):
    _, rows, cols = g.shape
    tr = _pick_rows(rows, 256)

    def body(s_ref, j_ref, g_ref, sib_ref, got_ref, w_ref, m_ref, v_ref, go_ref, d_ref, m2_ref, v2_ref):
        grad = g_ref[0] + sib_ref[0]
        for k in range(3):
            grad = grad + got_ref[k].astype(F32)
        go_ref[...] = grad
        d_ref[...], m2_ref[...], v2_ref[...] = _adamw_math(w_ref[...], grad, m_ref[...], v_ref[...])

    flat = pl.BlockSpec((tr, cols), lambda i, s_ref, j_ref: (i, 0))
    grid_spec = pltpu.PrefetchScalarGridSpec(
        num_scalar_prefetch=2, grid=(rows // tr,),
        in_specs=[pl.BlockSpec((1, tr, cols), lambda i, s_ref, j_ref: (s_ref[0], i, 0)),
                  pl.BlockSpec((1, tr, cols), lambda i, s_ref, j_ref: (j_ref[0], i, 0)),
                  pl.BlockSpec((3, tr, cols), lambda i, s_ref, j_ref: (0, i, 0)), flat, flat, flat],
        out_specs=[flat] * 4)
    return pl.pallas_call(
        body, name=name, grid_spec=grid_spec, out_shape=[jax.ShapeDtypeStruct((rows, cols), F32)] * 4,
        compiler_params=_cparams(("parallel",)),
    )(slot_idx, chip_idx, g, sib, got, w, m, v)


def _adamw_small(parts, w, m, v):
    rows = w.shape[0]

    def body(p_ref, w_ref, m_ref, v_ref, g_ref, d_ref, m2_ref, v2_ref):
        g = p_ref[0]
        for d in range(1, N_DEV):
            g = g + p_ref[d]
        g_ref[...] = g
        d_ref[...], m2_ref[...], v2_ref[...] = _adamw_math(w_ref[...], g, m_ref[...], v_ref[...])

    return pl.pallas_call(
        body, name="adamw_small", out_shape=[jax.ShapeDtypeStruct((rows, 128), F32)] * 4,
    )(parts, w, m, v)


BIG = ("w_in", "w_uq", "w_ukv", "w_branch", "w_out", "w_ffn_gate", "w_ffn_up", "w_ffn_down", "w_ple_gate", "w_ple_proj")
SMALL = (
    ("mix_norm_g", 1024), ("q_a_norm_g", 384), ("kv_a_norm_g", 256), ("q_norm_g", 96), ("k_norm_g", 96),
    ("hg_lb_logits", 1024), ("hg_out_norm_g", 128), ("ffn_norm_g", 1024), ("ple_gate_norm_g", 1024),
    ("ple_post_norm_g", 1024),
)
SMALL_ROWS = 56


def _pack_small(vals):
    rows = []
    for name, n in SMALL:
        v = vals[name].reshape(1, -1).astype(F32)
        rows.append(jnp.pad(v, ((0, 0), (0, (-n) % 128))).reshape(-1, 128))
    return jnp.concatenate(rows, axis=0)


def _unpack_small(packed, shapes):
    out, r = {}, 0
    for name, n in SMALL:
        k = (n + 127) // 128
        out[name] = packed[r:r + k].reshape(1, -1)[:, :n].reshape(shapes[name])
        r += k
    return out


_WEIGHTS = ["mix_norm_g", "w_in", "q_a_norm_g", "w_uq", "kv_a_norm_g", "w_ukv", "q_norm_g", "k_norm_g", "hg_lb_logits",
            "hg_out_norm_g", "w_branch", "w_out", "ffn_norm_g", "w_ffn_gate", "w_ffn_up", "w_ffn_down",
            "ple_gate_norm_g", "w_ple_gate", "w_ple_proj", "ple_post_norm_g"]


def _step(x, p, positions, tgt, w, m, v):
    small_names = [n for n, _ in SMALL]
    T = x.shape[1]
    px, py, pc = _my_place()
    as_idx = lambda t: jnp.reshape(t, (1,)).astype(jnp.int32)
    two_d = lambda t: t.reshape(-1, t.shape[-1])

    blocks = {n: two_d(w[n]).astype(MM) for n in BIG}
    big = dict(zip(EARLY, _all_gather("ag_weights", [blocks[n] for n in EARLY])))
    small = {n: (w[n] if n == "hg_lb_logits" else w[n].reshape(1, -1)) for n in small_names}

    loss_p, grad_x, small_g, grads, sibs, gots = _local_step(
        x[0], p[0, 0], positions.reshape(T, 1), tgt[0], small, big, late_blocks=blocks, core=as_idx(pc))

    out_g, out_d, out_m, out_v = {}, {}, {}, {}
    for n in BIG:
        res = _sum_adamw("adamw_" + n, grads[n], sibs[n], gots[n], two_d(w[n]), two_d(m[n]), two_d(v[n]),
                         as_idx(4 * px + 2 * py + pc), as_idx(2 * px + py))
        out_g[n], out_d[n], out_m[n], out_v[n] = [r.reshape(w[n].shape) for r in res]

    packed_g = _pack_small(small_g)
    loss_row = jnp.concatenate([jnp.pad(jnp.sum(loss_p).reshape(1, 1), ((0, 0), (0, 127))),
                                jnp.zeros((SMALL_ROWS - packed_g.shape[0] - 1, 128), F32)], axis=0)
    parts = _all_gather("ag_small", [jnp.concatenate([packed_g, loss_row], axis=0)])[0]
    pad_rows = lambda t: jnp.pad(t, ((0, SMALL_ROWS - t.shape[0]), (0, 0)))
    sw = pad_rows(_pack_small({n: w[n] for n in small_names}))
    sm = pad_rows(_pack_small({n: m[n] for n in small_names}))
    sv = pad_rows(_pack_small({n: v[n] for n in small_names}))
    g_s, d_s, m_s, v_s = _adamw_small(parts, sw, sm, sv)
    shapes = {n: w[n].shape for n in small_names}
    n_packed = packed_g.shape[0]
    loss = g_s[n_packed, 0]
    for src, dst in ((g_s, out_g), (d_s, out_d), (m_s, out_m), (v_s, out_v)):
        dst.update(_unpack_small(src, shapes))

    outs = [loss, grad_x[None]]
    for table in (out_g, out_d, out_m, out_v):
        outs += [table[n] for n in _WEIGHTS]
    return tuple(outs)


def kernel(x, p, positions, mix_norm_g, w_in, q_a_norm_g, w_uq, kv_a_norm_g, w_ukv, q_norm_g, k_norm_g, hg_lb_logits, hg_out_norm_g, w_branch, w_out, ffn_norm_g, w_ffn_gate, w_ffn_up, w_ffn_down, ple_gate_norm_g, w_ple_gate, w_ple_proj, ple_post_norm_g, loss_target, m_mix_norm_g, m_w_in, m_q_a_norm_g, m_w_uq, m_kv_a_norm_g, m_w_ukv, m_q_norm_g, m_k_norm_g, m_hg_lb_logits, m_hg_out_norm_g, m_w_branch, m_w_out, m_ffn_norm_g, m_w_ffn_gate, m_w_ffn_up, m_w_ffn_down, m_ple_gate_norm_g, m_w_ple_gate, m_w_ple_proj, m_ple_post_norm_g, v_mix_norm_g, v_w_in, v_q_a_norm_g, v_w_uq, v_kv_a_norm_g, v_w_ukv, v_q_norm_g, v_k_norm_g, v_hg_lb_logits, v_hg_out_norm_g, v_w_branch, v_w_out, v_ffn_norm_g, v_w_ffn_gate, v_w_ffn_up, v_w_ffn_down, v_ple_gate_norm_g, v_w_ple_gate, v_w_ple_proj, v_ple_post_norm_g):
    w = dict(mix_norm_g=mix_norm_g, w_in=w_in, q_a_norm_g=q_a_norm_g, w_uq=w_uq, kv_a_norm_g=kv_a_norm_g, w_ukv=w_ukv,
             q_norm_g=q_norm_g, k_norm_g=k_norm_g, hg_lb_logits=hg_lb_logits, hg_out_norm_g=hg_out_norm_g,
             w_branch=w_branch, w_out=w_out, ffn_norm_g=ffn_norm_g, w_ffn_gate=w_ffn_gate, w_ffn_up=w_ffn_up,
             w_ffn_down=w_ffn_down, ple_gate_norm_g=ple_gate_norm_g, w_ple_gate=w_ple_gate, w_ple_proj=w_ple_proj,
             ple_post_norm_g=ple_post_norm_g)
    m = dict(mix_norm_g=m_mix_norm_g, w_in=m_w_in, q_a_norm_g=m_q_a_norm_g, w_uq=m_w_uq, kv_a_norm_g=m_kv_a_norm_g,
             w_ukv=m_w_ukv, q_norm_g=m_q_norm_g, k_norm_g=m_k_norm_g, hg_lb_logits=m_hg_lb_logits,
             hg_out_norm_g=m_hg_out_norm_g, w_branch=m_w_branch, w_out=m_w_out, ffn_norm_g=m_ffn_norm_g,
             w_ffn_gate=m_w_ffn_gate, w_ffn_up=m_w_ffn_up, w_ffn_down=m_w_ffn_down,
             ple_gate_norm_g=m_ple_gate_norm_g, w_ple_gate=m_w_ple_gate, w_ple_proj=m_w_ple_proj,
             ple_post_norm_g=m_ple_post_norm_g)
    v = dict(mix_norm_g=v_mix_norm_g, w_in=v_w_in, q_a_norm_g=v_q_a_norm_g, w_uq=v_w_uq, kv_a_norm_g=v_kv_a_norm_g,
             w_ukv=v_w_ukv, q_norm_g=v_q_norm_g, k_norm_g=v_k_norm_g, hg_lb_logits=v_hg_lb_logits,
             hg_out_norm_g=v_hg_out_norm_g, w_branch=v_w_branch, w_out=v_w_out, ffn_norm_g=v_ffn_norm_g,
             w_ffn_gate=v_w_ffn_gate, w_ffn_up=v_w_ffn_up, w_ffn_down=v_w_ffn_down,
             ple_gate_norm_g=v_ple_gate_norm_g, w_ple_gate=v_w_ple_gate, w_ple_proj=v_w_ple_proj,
             ple_post_norm_g=v_ple_post_norm_g)
    return _step(x, p, positions, loss_target, w, m, v)
```

```python
import functools

import jax
import jax.numpy as jnp
import numpy as np
from jax import lax
from jax.experimental import pallas as pl
from jax.experimental.pallas import tpu as pltpu

F32 = jnp.float32
MM = jnp.bfloat16
HI = lax.Precision.HIGHEST
MESH_ID = pl.DeviceIdType.MESH

D_MODEL = 1024
N_DEV = 8
MLA_HEADS = 8
QK_NOPE = 64
QK_ROPE = 32
QK_DIM = 96
V_DIM = 64
HEAD_PAD = 128
Q_RANK = 384
KV_RANK = 256
ROPE_BASE = 10000.0
HG_HEADS = 4
HG_DIM = 128
HG_W = 512
HG_CHUNK = 64
HG_SUB = 16
FFN = 2816
PLE = 256
EPS = 1e-6
ATT_SCALE = QK_DIM ** -0.5
NEG = -1e30

ADAM_LR = 0.001
ADAM_B1 = 0.9
ADAM_B2 = 0.999
ADAM_EPS = 1e-08
ADAM_WD = 0.01
ADAM_STEP = 10

SEC_CQ = (0, 384)
SEC_CKV = (384, 256)
SEC_KR = (640, 128)
SEC_HQ = (768, 512)
SEC_HF = (1280, 512)
SEC_HI = (1792, 512)
SEC_HG = (2304, 512)
SEC_BG = (2816, 2048)
IN_PAD = 4864
SECTIONS = (SEC_CQ, SEC_CKV, SEC_KR, SEC_HQ, SEC_HF, SEC_HI, SEC_HG, SEC_BG)
COL_SECTIONS = ((0, 384), (384, 256), (640, 32), (672, 512), (1184, 512), (1696, 512), (2208, 512), (2720, 2048))
IN_COLS = 4768
IN_BLOCK = IN_COLS // 8

VMEM_LIMIT = 58 * 1024 * 1024
ROW_TILE = 256
ATT_TILE = 512
ATT_HEADS = 2
HG_BLOCK = 512


def _dot(a, b):
    return jnp.dot(a.astype(MM), b.astype(MM), preferred_element_type=F32)


def _dot_nt(a, b):
    return lax.dot_general(a.astype(MM), b.astype(MM), (((1,), (1,)), ((), ())), preferred_element_type=F32)


def _dot_tn(a, b):
    return lax.dot_general(a.astype(MM), b.astype(MM), (((0,), (0,)), ((), ())), preferred_element_type=F32)


def _dot_hi(a, b):
    return jnp.dot(a, b, preferred_element_type=F32, precision=HI)


def _sigmoid(x):
    return 1.0 / (1.0 + jnp.exp(-x))


def _rms(x, n=None):
    n = x.shape[-1] if n is None else n
    r = lax.rsqrt(jnp.sum(x * x, axis=-1, keepdims=True) * (1.0 / n) + EPS)
    return x * r, r


def _rms_bwd(dxh, xh, r, n=None):
    n = xh.shape[-1] if n is None else n
    return r * (dxh - xh * (jnp.sum(dxh * xh, axis=-1, keepdims=True) * (1.0 / n)))


def _rope_tables(pos, tm):
    lane = lax.broadcasted_iota(jnp.int32, (tm, HEAD_PAD), 1)
    idx = jnp.where(lane < QK_NOPE + QK_ROPE // 2, lane - QK_NOPE, lane - QK_NOPE - QK_ROPE // 2)
    inv = jnp.exp(idx.astype(F32) * (-np.log(ROPE_BASE) * 2.0 / QK_ROPE))
    ang = pos.astype(F32) * inv
    in_rope = (lane >= QK_NOPE) & (lane < QK_DIM)
    first = lane < QK_NOPE + QK_ROPE // 2
    cos_t = jnp.where(in_rope, jnp.cos(ang), 1.0)
    sin_t = jnp.where(in_rope, jnp.where(first, -jnp.sin(ang), jnp.sin(ang)), 0.0)
    return cos_t, sin_t, (first, in_rope)


def _rope_swap(x, halves):
    first, in_rope = halves
    half = QK_ROPE // 2
    return jnp.where(in_rope, jnp.where(first, pltpu.roll(x, HEAD_PAD - half, 1), pltpu.roll(x, half, 1)), 0.0)


def _cparams(sem, vmem=None):
    return pltpu.CompilerParams(dimension_semantics=sem, vmem_limit_bytes=vmem)


def _row_call(name, body, T, tm, row_ins, full_ins, row_outs, acc_outs, vmem=None):
    def kern(*refs):
        body(pl.program_id(0), *refs)

    in_specs = [pl.BlockSpec((tm, a.shape[1]), lambda i: (i, 0)) for a in row_ins]
    in_specs += [pl.BlockSpec(a.shape, lambda i, nd=a.ndim: (0,) * nd, pipeline_mode=pl.Buffered(1)) for a in full_ins]
    out_specs = [pl.BlockSpec((tm, n), lambda i: (i, 0)) for n, _ in row_outs]
    out_specs += [pl.BlockSpec(s, lambda i, nd=len(s): (0,) * nd) for s, _ in acc_outs]
    out_shape = [jax.ShapeDtypeStruct((T, n), dt) for n, dt in row_outs]
    out_shape += [jax.ShapeDtypeStruct(s, dt) for s, dt in acc_outs]
    return pl.pallas_call(
        kern, name=name, grid=(T // tm,), in_specs=in_specs, out_specs=out_specs, out_shape=out_shape,
        compiler_params=_cparams(("arbitrary",), vmem),
    )(*row_ins, *full_ins)


def _acc(ref, i, val):
    @pl.when(i == 0)
    def _():
        ref[...] = val

    @pl.when(i != 0)
    def _():
        ref[...] += val


def _in_proj_fwd(x, g_mix, w_in, T, tm):
    def body(i, x_ref, g_ref, w_ref, h_ref, *outs):
        xh, _ = _rms(x_ref[...])
        h = (xh * g_ref[...]).astype(MM)
        h_ref[...] = h
        proj = jnp.concatenate([_dot_nt(h, w_ref[d]) for d in range(N_DEV)], axis=1)
        for (s, n), o_ref in zip(COL_SECTIONS, outs):
            if n == QK_ROPE:
                o_ref[...] = jnp.concatenate(
                    [jnp.zeros((tm, QK_NOPE), F32), proj[:, s:s + n], jnp.zeros((tm, HEAD_PAD - QK_DIM), F32)], axis=1)
            else:
                o_ref[...] = proj[:, s:s + n]

    row_outs = [(D_MODEL, MM)] + [(n, F32) for _, n in SECTIONS]
    return _row_call("in_proj_fwd", body, T, tm, [x], [g_mix, w_in], row_outs, [], VMEM_LIMIT)


def _mla_heads_fwd(raw, g_pad, cos_t, sin_t, first):
    outs, saved = [], []
    for h in range(MLA_HEADS):
        xh, r = _rms(raw[:, h * HEAD_PAD:(h + 1) * HEAD_PAD], QK_DIM)
        y = xh * g_pad
        outs.append(y * cos_t + _rope_swap(y, first) * sin_t)
        saved.append((xh, r))
    return outs, saved


def _mla_raw_heads(cqn, ckvn, kr, wuq_ref, wukv_ref, tm):
    lane = lax.broadcasted_iota(jnp.int32, (tm, HEAD_PAD), 1)
    nope = lane < QK_NOPE
    one_lane = jnp.where(lane == V_DIM, 1.0, 0.0)
    qs, ks, vs = [], [], []
    for h in range(MLA_HEADS):
        qs.append(_dot_nt(cqn, wuq_ref[h]))
        kv = _dot(ckvn, wukv_ref[h])
        ks.append(jnp.where(nope, kv, kr))
        vs.append(jnp.where(nope, pltpu.roll(kv, V_DIM, 1), one_lane))
    return jnp.concatenate(qs, axis=1), jnp.concatenate(ks, axis=1), jnp.concatenate(vs, axis=1)


def _mla_prep_fwd(cq, ckv, kr, pos, g_qa, g_kva, g_qn, g_kn, w_uq, w_ukv, T, tm):
    def body(i, cq_ref, ckv_ref, kr_ref, pos_ref, gqa_ref, gkva_ref, gqn_ref, gkn_ref, wuq_ref, wukv_ref,
             q_ref, k_ref, v_ref):
        cos_t, sin_t, first = _rope_tables(pos_ref[...], tm)
        cqn = _rms(cq_ref[...])[0] * gqa_ref[...]
        ckvn = _rms(ckv_ref[...])[0] * gkva_ref[...]
        q_raw, k_raw, v = _mla_raw_heads(cqn, ckvn, kr_ref[...], wuq_ref, wukv_ref, tm)
        qs, _ = _mla_heads_fwd(q_raw, gqn_ref[...], cos_t, sin_t, first)
        ks, _ = _mla_heads_fwd(k_raw, gkn_ref[...], cos_t, sin_t, first)
        q_ref[...] = (jnp.concatenate(qs, axis=1) * ATT_SCALE).astype(MM)
        k_ref[...] = jnp.concatenate(ks, axis=1).astype(MM)
        v_ref[...] = v.astype(MM)

    w = MLA_HEADS * HEAD_PAD
    return _row_call("mla_prep_fwd", body, T, tm, [cq, ckv, kr, pos], [g_qa, g_kva, g_qn, g_kn, w_uq, w_ukv],
                     [(w, MM), (w, MM), (w, MM)], [])


def _causal_pairs(n, by_query):
    if by_query:
        pairs = [(q, k) for q in range(n) for k in range(q + 1)]
    else:
        pairs = [(q, k) for k in range(n) for q in range(k, n)]
    return np.array([p[0] for p in pairs], np.int32), np.array([p[1] for p in pairs], np.int32)


def _flash_fwd(qf, kf, vf, T, ag_blocks=()):
    tq = min(ATT_TILE, T)
    nq = T // tq

    qi_tab, ki_tab = _causal_pairs(nq, by_query=True)

    hp = ATT_HEADS

    n_ag = len(ag_blocks)
    n_heads, n_pairs = MLA_HEADS // hp, len(qi_tab)

    def body(qi_ref, ki_ref, q_ref, k_ref, v_ref, *rest):
        ag_in, (o_ref, lse_ref), rest = rest[:n_ag], rest[n_ag:n_ag + 2], rest[n_ag + 2:]
        ag_out, (m_s, acc_s), ag_sems = rest[:n_ag], rest[n_ag:n_ag + 2], rest[n_ag + 2:]
        t = pl.program_id(1)
        qi, ki = qi_ref[t], ki_ref[t]
        if n_ag:
            @pl.when((pl.program_id(0) == 0) & (t == 0))
            def _():
                _ag_start(ag_in, ag_out, ag_sems)

        @pl.when(ki == 0)
        def _():
            m_s[...] = jnp.full_like(m_s, NEG)
            acc_s[...] = jnp.zeros_like(acc_s)

        def step(masked):
            for hh in range(hp):
                hs = slice(hh * HEAD_PAD, (hh + 1) * HEAD_PAD)
                s_t = _dot_nt(k_ref[:, hs], q_ref[:, hs])
                if masked:
                    key = lax.broadcasted_iota(jnp.int32, (tq, tq), 0)
                    qry = lax.broadcasted_iota(jnp.int32, (tq, tq), 1)
                    s_t = jnp.where(key <= qry, s_t, NEG)
                m_old = m_s[hh]
                m_new = jnp.maximum(m_old, jnp.max(s_t, axis=0, keepdims=True))
                p_t = jnp.exp(s_t - m_new)
                acc_s[hh] = jnp.exp(m_old - m_new) * acc_s[hh] + _dot_tn(v_ref[:, hs], p_t)
                m_s[hh] = m_new

        @pl.when(ki < qi)
        def _():
            step(False)

        @pl.when(ki == qi)
        def _():
            step(True)
            real = lax.broadcasted_iota(jnp.int32, (HEAD_PAD, tq), 0) < V_DIM
            for hh in range(hp):
                hs = slice(hh * HEAD_PAD, (hh + 1) * HEAD_PAD)
                acc = acc_s[hh]
                l = acc[V_DIM:V_DIM + 1]
                o_ref[:, hs] = jnp.where(real, acc / l, 0.0).T
                lse_ref[:, hs] = jnp.broadcast_to(m_s[hh] + jnp.log(l), (HEAD_PAD, tq)).T

        if n_ag:
            @pl.when((pl.program_id(0) == n_heads - 1) & (t == n_pairs - 1))
            def _():
                _ag_finish(ag_in, ag_out, ag_sems)

    q_spec = pl.BlockSpec((tq, hp * HEAD_PAD), lambda h, t, qi_ref, ki_ref: (qi_ref[t], h))
    kv_spec = pl.BlockSpec((tq, hp * HEAD_PAD), lambda h, t, qi_ref, ki_ref: (ki_ref[t], h))
    any_spec = pl.BlockSpec(memory_space=pl.ANY)
    grid_spec = pltpu.PrefetchScalarGridSpec(
        num_scalar_prefetch=2, grid=(n_heads, n_pairs),
        in_specs=[q_spec, kv_spec, kv_spec] + [any_spec] * n_ag, out_specs=[q_spec, q_spec] + [any_spec] * n_ag,
        scratch_shapes=[pltpu.VMEM((hp, 1, tq), F32), pltpu.VMEM((hp, HEAD_PAD, tq), F32)]
        + (_ag_sems(n_ag) if n_ag else []))
    return pl.pallas_call(
        body, name="flash_fwd", grid_spec=grid_spec,
        out_shape=[jax.ShapeDtypeStruct((T, MLA_HEADS * HEAD_PAD), F32)] * 2 + _ag_out_shapes(ag_blocks),
        compiler_params=_cparams(("arbitrary", "arbitrary")),
    )(jnp.asarray(qi_tab), jnp.asarray(ki_tab), qf, kf, vf, *ag_blocks)


def _hg_gates(hf, lb):
    sg = _sigmoid(hf)
    f = lb + (1.0 - lb) * sg
    return sg, f, jnp.log(f), 1.0 - f


def _tri(n, lower):
    r = lax.broadcasted_iota(jnp.int32, (n, n), 0)
    c = lax.broadcasted_iota(jnp.int32, (n, n), 1)
    return jnp.where((c <= r) if lower else (c >= r), 1.0, 0.0).astype(F32)


def _hg_intra(q, k, b):
    C, S = HG_CHUNK, HG_SUB
    row_c = lax.broadcasted_iota(jnp.int32, (C, HG_DIM), 0)
    row_s = lax.broadcasted_iota(jnp.int32, (S, HG_DIM), 0)
    lane_c = lax.broadcasted_iota(jnp.int32, (S, C), 1)
    blocks, saved = [], []
    for blk in range(C // S):
        lo = blk * S
        q_b, k_b, b_b = q[lo:lo + S], k[lo:lo + S], b[lo:lo + S]
        a_b = jnp.zeros((S, C), F32)
        for j in range(S):
            w = jnp.exp(jnp.where(row_s >= j, b_b - b_b[j:j + 1], NEG))
            col = jnp.sum(q_b * (k_b[j:j + 1] * w), axis=1, keepdims=True)
            a_b = jnp.where(lane_c == lo + j, col, a_b)
        if blk > 0:
            ref = b[lo - 1:lo]
            q_e = jnp.exp(b_b - ref)
            q_t = q_b * q_e
            k_e = jnp.exp(jnp.where(row_c < lo, ref - b, NEG))
            a_b = a_b + _dot_nt(q_t, k * k_e)
            saved.append((q_t, k_e, q_e))
        else:
            saved.append(None)
        blocks.append(a_b)
    return jnp.concatenate(blocks, axis=0), saved


def _hgrn_fwd(hq, hf, hi, lb, T):
    rb = min(HG_BLOCK, T)
    ncb = rb // HG_CHUNK

    def body(hq_ref, hf_ref, hi_ref, lb_ref, o_ref, s0_ref, st_ref):
        @pl.when(pl.program_id(0) == 0)
        def _():
            st_ref[...] = jnp.zeros_like(st_ref)

        tril = _tri(HG_CHUNK, True)

        def chunk(c, carry):
            rows = pl.ds(pl.multiple_of(c * HG_CHUNK, HG_CHUNK), HG_CHUNK)
            _, _, logf, kk = _hg_gates(hf_ref[rows, :], lb_ref[...])
            b = _dot_hi(tril, logf)
            q_all, v_all = hq_ref[rows, :], hi_ref[rows, :]
            outs = []
            for h in range(HG_HEADS):
                ls = slice(h * HG_DIM, (h + 1) * HG_DIM)
                q, k, v, bh = q_all[:, ls], kk[:, ls], v_all[:, ls], b[:, ls]
                st = st_ref[h]
                s0_ref[c, h * HG_DIM:(h + 1) * HG_DIM, :] = st
                b_end = bh[HG_CHUNK - 1:HG_CHUNK]
                a, _ = _hg_intra(q, k, bh)
                outs.append(_dot_nt(q * jnp.exp(bh), st) + _dot(a, v))
                st_ref[h] = st * jnp.exp(b_end) + _dot_tn(v, k * jnp.exp(b_end - bh))
            o_ref[rows, :] = jnp.concatenate(outs, axis=1)
            return carry

        lax.fori_loop(0, ncb, chunk, 0)

    row = pl.BlockSpec((rb, HG_W), lambda i: (i, 0))
    return pl.pallas_call(
        body, name="hgrn_fwd", grid=(T // rb,),
        in_specs=[row, row, row, pl.BlockSpec((1, HG_W), lambda i: (0, 0))],
        out_specs=[row, pl.BlockSpec((ncb, HG_W, HG_DIM), lambda i: (i, 0, 0))],
        out_shape=[jax.ShapeDtypeStruct((T, HG_W), F32), jax.ShapeDtypeStruct((T // HG_CHUNK, HG_W, HG_DIM), F32)],
        scratch_shapes=[pltpu.VMEM((HG_HEADS, HG_DIM, HG_DIM), F32)],
        compiler_params=_cparams(("arbitrary",)),
    )(hq, hf, hi, lb)


def _hgrn_bwd(hq, hf, hi, do, s0, lb, T, xchg=()):
    rb = min(HG_BLOCK, T)
    ncb = rb // HG_CHUNK
    nb = T // rb
    C, S = HG_CHUNK, HG_SUB
    n_x = len(xchg)

    def body(hq_ref, hf_ref, hi_ref, do_ref, s0_ref, lb_ref, *rest):
        x_in, (dq_ref, df_ref, dv_ref, dlb_ref), rest = rest[:n_x], rest[n_x:n_x + 4], rest[n_x + 4:]
        x_out, dst_ref, x_sems = rest[:n_x], rest[n_x], rest[n_x + 1:]

        @pl.when(pl.program_id(0) == 0)
        def _():
            dst_ref[...] = jnp.zeros_like(dst_ref)
            dlb_ref[...] = jnp.zeros_like(dlb_ref)
            for cp in _xchips_copies(x_in, x_out, x_sems) if n_x else ():
                cp.start()

        tril, triu = _tri(C, True), _tri(C, False)
        row_cc = lax.broadcasted_iota(jnp.int32, (C, C), 0)
        col_cc = lax.broadcasted_iota(jnp.int32, (C, C), 1)
        row_s = lax.broadcasted_iota(jnp.int32, (S, HG_DIM), 0)
        lane_sc = lax.broadcasted_iota(jnp.int32, (S, C), 1)
        last_row = lax.broadcasted_iota(jnp.int32, (C, HG_DIM), 0) == C - 1
        lb_v = lb_ref[...]

        def chunk(cc, carry):
            c = ncb - 1 - cc
            rows = pl.ds(pl.multiple_of(c * C, C), C)
            hf_c = hf_ref[rows, :]
            sg, f, logf, kk = _hg_gates(hf_c, lb_v)
            b = _dot_hi(tril, logf)
            q_all, v_all, do_all = hq_ref[rows, :], hi_ref[rows, :], do_ref[rows, :]
            dq_o, dk_o, dv_o, db_o = [], [], [], []
            for h in range(HG_HEADS):
                ls = slice(h * HG_DIM, (h + 1) * HG_DIM)
                q, k, v, bh, d_o = q_all[:, ls], kk[:, ls], v_all[:, ls], b[:, ls], do_all[:, ls]
                st0 = s0_ref[c, h * HG_DIM:(h + 1) * HG_DIM, :]
                dst = dst_ref[h]
                b_end = bh[C - 1:C]
                e_b, e_end = jnp.exp(bh), jnp.exp(b_end)
                e_rem = jnp.exp(b_end - bh)
                qe, kd = q * e_b, k * e_rem
                st_end = st0 * e_end + _dot_tn(v, kd)
                a, saved = _hg_intra(q, k, bh)
                d_a = jnp.where(col_cc <= row_cc, _dot_nt(d_o, v), 0.0)
                dv = _dot_tn(a, d_o) + _dot_nt(kd, dst)
                dq = e_b * _dot(d_o, st0)
                dk = e_rem * _dot(v, dst)
                dq_blocks, dk_diag = [], []
                for blk in range(C // S):
                    lo = blk * S
                    q_b, k_b, b_b = q[lo:lo + S], k[lo:lo + S], bh[lo:lo + S]
                    da_b = d_a[lo:lo + S]
                    dq_b = jnp.zeros((S, HG_DIM), F32)
                    dk_b = jnp.zeros((S, HG_DIM), F32)
                    for j in range(S):
                        w = jnp.exp(jnp.where(row_s >= j, b_b - b_b[j:j + 1], NEG))
                        col = jnp.sum(jnp.where(lane_sc == lo + j, da_b, 0.0), axis=1, keepdims=True)
                        dq_b = dq_b + col * (k_b[j:j + 1] * w)
                        dk_row = jnp.sum(col * (q_b * w), axis=0, keepdims=True)
                        dk_b = jnp.where(row_s == j, dk_row, dk_b)
                    if blk > 0:
                        q_t, k_e, q_e = saved[blk]
                        da_off = jnp.where(lane_sc < lo, da_b, 0.0)
                        dq_b = dq_b + _dot(da_off, k * k_e) * q_e
                        dk = dk + _dot_tn(da_off, q_t) * k_e
                    dq_blocks.append(dq_b)
                    dk_diag.append(dk_b)
                dq = dq + jnp.concatenate(dq_blocks, axis=0)
                dk = dk + jnp.concatenate(dk_diag, axis=0)
                extra = jnp.sum(dst * st_end, axis=0, keepdims=True)
                db_o.append(q * dq - k * dk + jnp.where(last_row, extra, 0.0))
                dst_ref[h] = dst * e_end + _dot_tn(d_o, qe)
                dq_o.append(dq)
                dk_o.append(dk)
                dv_o.append(dv)
            dlogf = _dot_hi(triu, jnp.concatenate(db_o, axis=1))
            d_f = dlogf / f - jnp.concatenate(dk_o, axis=1)
            dq_ref[rows, :] = jnp.concatenate(dq_o, axis=1)
            dv_ref[rows, :] = jnp.concatenate(dv_o, axis=1)
            df_ref[rows, :] = d_f * (1.0 - lb_v) * sg * (1.0 - sg)
            dlb_ref[...] += jnp.sum(d_f * (1.0 - sg), axis=0, keepdims=True)
            return carry

        lax.fori_loop(0, ncb, chunk, 0)

        if n_x:
            @pl.when(pl.program_id(0) == nb - 1)
            def _():
                for cp in _xchips_copies(x_in, x_out, x_sems):
                    cp.wait()

    row = pl.BlockSpec((rb, HG_W), lambda i: (nb - 1 - i, 0))
    one = pl.BlockSpec((1, HG_W), lambda i: (0, 0))
    any_spec = pl.BlockSpec(memory_space=pl.ANY)
    return pl.pallas_call(
        body, name="hgrn_bwd", grid=(nb,),
        in_specs=[row, row, row, row, pl.BlockSpec((ncb, HG_W, HG_DIM), lambda i: (nb - 1 - i, 0, 0)), one]
        + [any_spec] * n_x,
        out_specs=[row, row, row, one] + [any_spec] * n_x,
        out_shape=[jax.ShapeDtypeStruct((T, HG_W), F32)] * 3 + [jax.ShapeDtypeStruct((1, HG_W), F32)]
        + _xchips_out_shapes(xchg),
        scratch_shapes=[pltpu.VMEM((HG_HEADS, HG_DIM, HG_DIM), F32)] + (_xchips_sems(n_x) if n_x else []),
        compiler_params=_cparams(("arbitrary",)),
    )(hq, hf, hi, do, s0, lb, *xchg)


def _silu_parts(x):
    sg = _sigmoid(x)
    return x * sg, sg * (1.0 + x * (1.0 - sg))


def _merge_fwd(attn, o, hg, bg, x, g_out, w_bra, w_brb, w_out, T, tm):
    def body(i, attn_ref, o_ref, hg_ref, bg_ref, x_ref, g_ref, wa_ref, wb_ref, wo_ref,
             x1_ref, ya_ref, yb_ref, m_ref, rec_ref):
        recs = []
        for h in range(HG_HEADS):
            ls = slice(h * HG_DIM, (h + 1) * HG_DIM)
            oh, _ = _rms(o_ref[:, ls])
            recs.append(oh * g_ref[...] * _silu_parts(hg_ref[:, ls])[0])
        rec = jnp.concatenate(recs, axis=1)
        ya = _dot(attn_ref[...], wa_ref[...])
        yb = _dot(rec, wb_ref[...])
        m = _sigmoid(bg_ref[:, :D_MODEL]) * ya + _sigmoid(bg_ref[:, D_MODEL:]) * yb
        x1_ref[...] = x_ref[...] + _dot(m, wo_ref[...])
        ya_ref[...] = ya
        yb_ref[...] = yb
        m_ref[...] = m.astype(MM)
        rec_ref[...] = rec.astype(MM)

    return _row_call("merge_fwd", body, T, tm, [attn, o, hg, bg, x], [g_out, w_bra, w_brb, w_out],
                     [(D_MODEL, F32), (D_MODEL, F32), (D_MODEL, F32), (D_MODEL, MM), (HG_W, MM)], [], VMEM_LIMIT)


def _ffn_fwd(x1, g_ffn, w_g, w_u, w_d, T, tm):
    def body(i, x1_ref, g_ref, wg_ref, wu_ref, wd_ref, x2_ref, gt_ref, up_ref, h2_ref):
        x1 = x1_ref[...]
        h2 = (_rms(x1)[0] * g_ref[...]).astype(MM)
        gt = _dot_nt(h2, wg_ref[...])
        up = _dot_nt(h2, wu_ref[...])
        a = _silu_parts(gt)[0] * up
        x2_ref[...] = x1 + _dot(a, wd_ref[...])
        gt_ref[...] = gt
        up_ref[...] = up
        h2_ref[...] = h2

    return _row_call("ffn_fwd", body, T, tm, [x1], [g_ffn, w_g, w_u, w_d],
                     [(D_MODEL, F32), (FFN, F32), (FFN, F32), (D_MODEL, MM)], [], VMEM_LIMIT)


def _ple_loss(x2, p, tgt, g_pg, g_post, w_pg, w_pp, T, tm):
    def body(i, x2_ref, p_ref, t_ref, gpg_ref, gpo_ref, wpg_ref, wpp_ref,
             dx2_ref, n3_ref, dz_ref, du_ref, loss_ref, dgpo_ref, dgpg_ref):
        x2 = x2_ref[...]
        p_mm = p_ref[...].astype(MM)
        u = jnp.concatenate([jnp.dot(p_mm, wpp_ref[d], preferred_element_type=F32) for d in range(N_DEV)], axis=1)
        uh, ru = _rms(u)
        e = uh * gpo_ref[...]
        x2h, r3 = _rms(x2)
        n3 = x2h * gpg_ref[...]
        gate = _sigmoid(_dot(n3, wpg_ref[...]))
        diff = x2 + gate * e - t_ref[...]
        dy = diff * (1.0 / D_MODEL)
        de = dy * gate
        dz = dy * e * gate * (1.0 - gate)
        du = _rms_bwd(de * gpo_ref[...], uh, ru)
        dn3 = _dot_nt(dz, wpg_ref[...])
        dx2_ref[...] = dy + _rms_bwd(dn3 * gpg_ref[...], x2h, r3)
        n3_ref[...] = n3.astype(MM)
        dz_ref[...] = dz.astype(MM)
        du_ref[...] = du.astype(MM)
        _acc(loss_ref, i, jnp.sum(diff * diff, axis=0, keepdims=True) * (0.5 / D_MODEL))
        _acc(dgpo_ref, i, jnp.sum(de * uh, axis=0, keepdims=True))
        _acc(dgpg_ref, i, jnp.sum(dn3 * x2h, axis=0, keepdims=True))

    vec = ((1, D_MODEL), F32)
    return _row_call("ple_loss", body, T, tm, [x2, p, tgt], [g_pg, g_post, w_pg, w_pp],
                     [(D_MODEL, F32), (D_MODEL, MM), (D_MODEL, MM), (D_MODEL, MM)], [vec, vec, vec], VMEM_LIMIT)


def _ffn_bwd(dx2, x1, gt, up, g_ffn, w_g, w_u, w_d, T, tm):
    def body(i, dx2_ref, x1_ref, gt_ref, up_ref, g_ref, wg_ref, wu_ref, wd_ref,
             dx1_ref, a_ref, dgt_ref, dup_ref, dg_ref):
        dx2 = dx2_ref[...]
        x1h, r = _rms(x1_ref[...])
        up = up_ref[...]
        silu, dsilu = _silu_parts(gt_ref[...])
        da = _dot_nt(dx2, wd_ref[...])
        dgt = (da * up * dsilu).astype(MM)
        dup = (da * silu).astype(MM)
        dh2 = (jnp.dot(dgt, wg_ref[...], preferred_element_type=F32)
               + jnp.dot(dup, wu_ref[...], preferred_element_type=F32))
        dx1_ref[...] = dx2 + _rms_bwd(dh2 * g_ref[...], x1h, r)
        a_ref[...] = (silu * up).astype(MM)
        dgt_ref[...] = dgt
        dup_ref[...] = dup
        _acc(dg_ref, i, jnp.sum(dh2 * x1h, axis=0, keepdims=True))

    return _row_call("ffn_bwd", body, T, tm, [dx2, x1, gt, up], [g_ffn, w_g, w_u, w_d],
                     [(D_MODEL, F32), (FFN, MM), (FFN, MM), (FFN, MM)], [((1, D_MODEL), F32)], VMEM_LIMIT)


def _merge_bwd(dx1, ya, yb, bg, o, hg, g_out, w_bra, w_brb, w_out, T, tm):
    def body(i, dx1_ref, ya_ref, yb_ref, bg_ref, o_ref, hg_ref, g_ref, wa_ref, wb_ref, wo_ref,
             dattn_ref, do_ref, dhg_ref, dbg_ref, dya_ref, dyb_ref, dg_ref):
        dm = _dot_nt(dx1_ref[...], wo_ref[...])
        ga, gb = _sigmoid(bg_ref[:, :D_MODEL]), _sigmoid(bg_ref[:, D_MODEL:])
        dya, dyb = (dm * ga).astype(MM), (dm * gb).astype(MM)
        dbg_ref[:, :D_MODEL] = dm * ya_ref[...] * ga * (1.0 - ga)
        dbg_ref[:, D_MODEL:] = dm * yb_ref[...] * gb * (1.0 - gb)
        dya_ref[...] = dya
        dyb_ref[...] = dyb
        dattn_ref[...] = lax.dot_general(dya, wa_ref[...], (((1,), (1,)), ((), ())), preferred_element_type=F32)
        drec = lax.dot_general(dyb, wb_ref[...], (((1,), (1,)), ((), ())), preferred_element_type=F32)
        dg = jnp.zeros((1, HG_DIM), F32)
        for h in range(HG_HEADS):
            ls = slice(h * HG_DIM, (h + 1) * HG_DIM)
            oh, r = _rms(o_ref[:, ls])
            silu, dsilu = _silu_parts(hg_ref[:, ls])
            dr = drec[:, ls]
            dhg_ref[:, ls] = dr * oh * g_ref[...] * dsilu
            don = dr * silu
            dg = dg + jnp.sum(don * oh, axis=0, keepdims=True)
            do_ref[:, ls] = _rms_bwd(don * g_ref[...], oh, r)
        _acc(dg_ref, i, dg)

    return _row_call("merge_bwd", body, T, tm, [dx1, ya, yb, bg, o, hg], [g_out, w_bra, w_brb, w_out],
                     [(D_MODEL, F32), (HG_W, F32), (HG_W, F32), (2 * D_MODEL, F32), (D_MODEL, MM), (D_MODEL, MM)],
                     [((1, HG_DIM), F32)], VMEM_LIMIT)


def _flash_bwd(qf, kf, vf, o, do, lse, T, xchg=()):
    tq = min(ATT_TILE, T)
    nq = T // tq

    qi_tab, ki_tab = _causal_pairs(nq, by_query=False)

    n_x = len(xchg)
    hp = ATT_HEADS
    n_heads, n_pairs = MLA_HEADS // hp, len(qi_tab)

    def body(qi_ref, ki_ref, q_ref, k_ref, v_ref, o_ref, do_ref, lse_ref, *rest):
        x_in, (dq_ref, dk_ref, dv_ref), rest = rest[:n_x], rest[n_x:n_x + 3], rest[n_x + 3:]
        x_out, x_sems = rest[:n_x], rest[n_x:]
        t = pl.program_id(1)
        qi, ki = qi_ref[t], ki_ref[t]
        if n_x:
            @pl.when((pl.program_id(0) == 0) & (t == 0))
            def _():
                for cp in _xchips_copies(x_in, x_out, x_sems):
                    cp.start()

        @pl.when(t == 0)
        def _():
            dq_ref[...] = jnp.zeros_like(dq_ref)

        def step(first):
            rows = pl.ds(pl.multiple_of(qi * tq, tq), tq)
            for hh in range(hp):
                hs = slice(hh * HEAD_PAD, (hh + 1) * HEAD_PAD)
                q, k, d_o = q_ref[:, hs], k_ref[:, hs], do_ref[:, hs]
                s = _dot_nt(q, k)
                if first:
                    row = lax.broadcasted_iota(jnp.int32, (tq, tq), 0)
                    col = lax.broadcasted_iota(jnp.int32, (tq, tq), 1)
                    s = jnp.where(col <= row, s, NEG)
                p = jnp.exp(s - lse_ref[:, hh * HEAD_PAD:hh * HEAD_PAD + 1])
                delta = jnp.sum(d_o * o_ref[:, hs], axis=1, keepdims=True)
                ds = p * (_dot_nt(d_o, v_ref[:, hs]) - delta)
                dq_ref[rows, hs] += _dot(ds, k)
                if first:
                    dv_ref[:, hs] = _dot_tn(p, d_o)
                    dk_ref[:, hs] = _dot_tn(ds, q)
                else:
                    dv_ref[:, hs] += _dot_tn(p, d_o)
                    dk_ref[:, hs] += _dot_tn(ds, q)

        @pl.when(qi == ki)
        def _():
            step(True)

        @pl.when(qi > ki)
        def _():
            step(False)

        if n_x:
            @pl.when((pl.program_id(0) == n_heads - 1) & (t == n_pairs - 1))
            def _():
                for cp in _xchips_copies(x_in, x_out, x_sems):
                    cp.wait()

    q_spec = pl.BlockSpec((tq, hp * HEAD_PAD), lambda h, t, qi_ref, ki_ref: (qi_ref[t], h))
    kv_spec = pl.BlockSpec((tq, hp * HEAD_PAD), lambda h, t, qi_ref, ki_ref: (ki_ref[t], h))
    any_spec = pl.BlockSpec(memory_space=pl.ANY)
    w = MLA_HEADS * HEAD_PAD
    grid_spec = pltpu.PrefetchScalarGridSpec(
        num_scalar_prefetch=2, grid=(n_heads, n_pairs),
        in_specs=[q_spec, kv_spec, kv_spec, q_spec, q_spec, q_spec] + [any_spec] * n_x,
        out_specs=[pl.BlockSpec((T, hp * HEAD_PAD), lambda h, t, qi_ref, ki_ref: (0, h)), kv_spec, kv_spec]
        + [any_spec] * n_x,
        scratch_shapes=_xchips_sems(n_x) if n_x else [])
    return pl.pallas_call(
        body, name="flash_bwd", grid_spec=grid_spec,
        out_shape=[jax.ShapeDtypeStruct((T, w), F32)] * 3 + _xchips_out_shapes(xchg),
        compiler_params=_cparams(("arbitrary", "arbitrary")),
    )(jnp.asarray(qi_tab), jnp.asarray(ki_tab), qf, kf, vf, o, do, lse, *xchg)


def _mla_heads_bwd(d_out, saved, g_pad, cos_t, sin_t, first):
    d_raw, dg = [], jnp.zeros((1, HEAD_PAD), F32)
    for h in range(MLA_HEADS):
        xh, r = saved[h]
        dy = d_out[:, h * HEAD_PAD:(h + 1) * HEAD_PAD]
        dn = dy * cos_t + _rope_swap(dy * sin_t, first)
        dg = dg + jnp.sum(dn * xh, axis=0, keepdims=True)
        d_raw.append(_rms_bwd(dn * g_pad, xh, r, QK_DIM))
    return d_raw, dg


def _mla_prep_bwd(cq, ckv, kr, pos, dqf, dkf, dvf, g_qa, g_kva, g_qn, g_kn, w_uq, w_ukv, T, tm):
    def body(i, cq_ref, ckv_ref, kr_ref, pos_ref, dq_ref, dk_ref, dv_ref,
             gqa_ref, gkva_ref, gqn_ref, gkn_ref, wuq_ref, wukv_ref,
             dcq_ref, dckv_ref, dkr_ref, dqraw_ref, dkv_ref, cqn_ref, ckvn_ref,
             dgqa_ref, dgkva_ref, dgqn_ref, dgkn_ref):
        cos_t, sin_t, first = _rope_tables(pos_ref[...], tm)
        cqh, rq = _rms(cq_ref[...])
        ckvh, rkv = _rms(ckv_ref[...])
        cqn, ckvn = cqh * gqa_ref[...], ckvh * gkva_ref[...]
        q_raw, k_raw, _ = _mla_raw_heads(cqn, ckvn, kr_ref[...], wuq_ref, wukv_ref, tm)
        _, q_saved = _mla_heads_fwd(q_raw, gqn_ref[...], cos_t, sin_t, first)
        _, k_saved = _mla_heads_fwd(k_raw, gkn_ref[...], cos_t, sin_t, first)
        dq_heads, dgqn = _mla_heads_bwd(dq_ref[...] * ATT_SCALE, q_saved, gqn_ref[...], cos_t, sin_t, first)
        dk_heads, dgkn = _mla_heads_bwd(dk_ref[...], k_saved, gkn_ref[...], cos_t, sin_t, first)
        lane = lax.broadcasted_iota(jnp.int32, (tm, HEAD_PAD), 1)
        nope = lane < QK_NOPE
        dcqn = jnp.zeros((tm, Q_RANK), F32)
        dckvn = jnp.zeros((tm, KV_RANK), F32)
        dkr = jnp.zeros((tm, HEAD_PAD), F32)
        for h in range(MLA_HEADS):
            hs = slice(h * HEAD_PAD, (h + 1) * HEAD_PAD)
            dq_h = dq_heads[h].astype(MM)
            dkv_h = jnp.where(nope, dk_heads[h], pltpu.roll(dv_ref[:, hs], V_DIM, 1)).astype(MM)
            dqraw_ref[:, hs] = dq_h
            dkv_ref[:, hs] = dkv_h
            dcqn = dcqn + jnp.dot(dq_h, wuq_ref[h], preferred_element_type=F32)
            dckvn = dckvn + lax.dot_general(dkv_h, wukv_ref[h], (((1,), (1,)), ((), ())), preferred_element_type=F32)
            dkr = dkr + dk_heads[h]
        dkr_ref[...] = jnp.where((lane >= QK_NOPE) & (lane < QK_DIM), dkr, 0.0)
        dcq_ref[...] = _rms_bwd(dcqn * gqa_ref[...], cqh, rq)
        dckv_ref[...] = _rms_bwd(dckvn * gkva_ref[...], ckvh, rkv)
        cqn_ref[...] = cqn.astype(MM)
        ckvn_ref[...] = ckvn.astype(MM)
        _acc(dgqa_ref, i, jnp.sum(dcqn * cqh, axis=0, keepdims=True))
        _acc(dgkva_ref, i, jnp.sum(dckvn * ckvh, axis=0, keepdims=True))
        _acc(dgqn_ref, i, dgqn)
        _acc(dgkn_ref, i, dgkn)

    w = MLA_HEADS * HEAD_PAD
    return _row_call(
        "mla_prep_bwd", body, T, tm, [cq, ckv, kr, pos, dqf, dkf, dvf], [g_qa, g_kva, g_qn, g_kn, w_uq, w_ukv],
        [(Q_RANK, F32), (KV_RANK, F32), (HEAD_PAD, F32), (w, MM), (w, MM), (Q_RANK, MM), (KV_RANK, MM)],
        [((1, Q_RANK), F32), ((1, KV_RANK), F32), ((1, HEAD_PAD), F32), ((1, HEAD_PAD), F32)], VMEM_LIMIT)


def _in_proj_bwd(x, dx1, dsecs, g_mix, w_in, T, tm):
    def body(i, x_ref, dx1_ref, *rest):
        d_refs, (g_ref, w_ref, dx_ref, dp_ref, dg_ref) = rest[:len(SECTIONS)], rest[len(SECTIONS):]
        pieces = [d_ref[:, QK_NOPE:QK_DIM] if n == QK_ROPE else d_ref[...] for (_, n), d_ref in zip(COL_SECTIONS, d_refs)]
        dproj = jnp.concatenate(pieces, axis=1)
        dh = jnp.zeros((tm, D_MODEL), F32)
        for d in range(N_DEV):
            blk = dproj[:, d * IN_BLOCK:(d + 1) * IN_BLOCK].astype(MM)
            dp_ref[d] = blk
            dh = dh + jnp.dot(blk, w_ref[d], preferred_element_type=F32)
        xh, r = _rms(x_ref[...])
        dx_ref[...] = dx1_ref[...] + _rms_bwd(dh * g_ref[...], xh, r)
        _acc(dg_ref, i, jnp.sum(dh * xh, axis=0, keepdims=True))

    in_specs = [pl.BlockSpec((tm, a.shape[1]), lambda i: (i, 0)) for a in [x, dx1, *dsecs]]
    in_specs += [pl.BlockSpec(g_mix.shape, lambda i: (0, 0)),
                 pl.BlockSpec(w_in.shape, lambda i: (0, 0, 0), pipeline_mode=pl.Buffered(1))]

    def kern(*refs):
        body(pl.program_id(0), *refs)

    return pl.pallas_call(
        kern, name="in_proj_bwd", grid=(T // tm,), in_specs=in_specs,
        out_specs=[pl.BlockSpec((tm, D_MODEL), lambda i: (i, 0)),
                   pl.BlockSpec((N_DEV, tm, IN_BLOCK), lambda i: (0, i, 0)),
                   pl.BlockSpec((1, D_MODEL), lambda i: (0, 0))],
        out_shape=[jax.ShapeDtypeStruct((T, D_MODEL), F32), jax.ShapeDtypeStruct((N_DEV, T, IN_BLOCK), MM),
                   jax.ShapeDtypeStruct((1, D_MODEL), F32)],
        compiler_params=_cparams(("arbitrary",), VMEM_LIMIT),
    )(x, dx1, *dsecs, g_mix, w_in)


def _pick_block(n, cap):
    best = None
    for cand in range(128, min(n, cap) + 1, 128):
        if n % cand == 0:
            best = cand
    return n if best is None else best


def _pick_rows(n, cap):
    best = n
    for cand in range(8, min(n, cap) + 1, 8):
        if n % cand == 0:
            best = cand
    return best


def _matmul_tn(name, a, b, blocked=None):
    T, M = a.shape
    N = b.shape[1]
    bm, bk = _pick_block(M, 1408), min(512, T)
    bn = _pick_block(N, 2560)

    def body(a_ref, b_ref, c_ref):
        @pl.when(pl.program_id(2) == 0)
        def _():
            c_ref[...] = jnp.zeros_like(c_ref)

        if blocked is None:
            c_ref[...] += _dot_tn(a_ref[...], b_ref[...])
        else:
            a = a_ref[...].astype(MM)
            for d in range(bn // blocked):
                c_ref[d] += _dot_tn(a, b_ref[:, d * blocked:(d + 1) * blocked])

    if blocked is None:
        out_spec = pl.BlockSpec((bm, bn), lambda i, j, k: (i, j))
        out_shape = jax.ShapeDtypeStruct((M, N), F32)
    else:
        assert bn == N
        out_spec = pl.BlockSpec((N // blocked, bm, blocked), lambda i, j, k: (0, i, 0))
        out_shape = jax.ShapeDtypeStruct((N // blocked, M, blocked), F32)
    return pl.pallas_call(
        body, name=name, grid=(M // bm, N // bn, T // bk),
        in_specs=[pl.BlockSpec((bk, bm), lambda i, j, k: (k, i)), pl.BlockSpec((bk, bn), lambda i, j, k: (k, j))],
        out_specs=out_spec, out_shape=out_shape,
        compiler_params=_cparams(("parallel", "parallel", "arbitrary"), VMEM_LIMIT),
    )(a, b)


def _matmul_tn_blocks(name, a, b):
    T, M = a.shape
    nd, _, c = b.shape
    bm, bk = _pick_block(M, 512), min(512, T)

    def body(a_ref, b_ref, c_ref):
        @pl.when(pl.program_id(1) == 0)
        def _():
            c_ref[...] = jnp.zeros_like(c_ref)

        a_blk = a_ref[...].astype(MM)
        for d in range(nd):
            c_ref[d] += _dot_tn(b_ref[d], a_blk)

    return pl.pallas_call(
        body, name=name, grid=(M // bm, T // bk),
        in_specs=[pl.BlockSpec((bk, bm), lambda i, k: (k, i)), pl.BlockSpec((nd, bk, c), lambda i, k: (0, k, 0))],
        out_specs=pl.BlockSpec((nd, c, bm), lambda i, k: (0, 0, i)),
        out_shape=jax.ShapeDtypeStruct((nd, c, M), F32),
        compiler_params=_cparams(("parallel", "arbitrary"), VMEM_LIMIT),
    )(a, b)


def _pad_gain(g, n):
    return jnp.pad(g.reshape(1, -1), ((0, 0), (0, n - g.shape[-1])))


GROUP_A = ("w_ffn_gate", "w_ffn_up", "w_ffn_down", "w_ple_gate", "w_ple_proj")
GROUP_B = ("w_branch", "w_out")
GROUP_C = ("w_in", "w_uq", "w_ukv")
EARLY = GROUP_C
LATE = GROUP_B + GROUP_A
TRANSPOSED = ("w_in", "w_uq", "w_ffn_gate", "w_ffn_up")


def _local_step(x, p, pos, tgt, small, big, late_blocks=None, core=None):
    T = x.shape[0]
    tm = min(ROW_TILE, T)
    w_in = big["w_in"]
    w_uq = jnp.pad(big["w_uq"], ((0, 0), (0, HEAD_PAD - QK_DIM), (0, 0)))
    w_ukv = big["w_ukv"]

    g_mix, g_qa, g_kva = small["mix_norm_g"], small["q_a_norm_g"], small["kv_a_norm_g"]
    g_qn, g_kn = _pad_gain(small["q_norm_g"], HEAD_PAD), _pad_gain(small["k_norm_g"], HEAD_PAD)
    g_out, g_ffn = small["hg_out_norm_g"], small["ffn_norm_g"]
    g_pg, g_post = small["ple_gate_norm_g"], small["ple_post_norm_g"]
    logits = small["hg_lb_logits"]
    lb = _lower_bound(logits)

    h, cq, ckv, kr, hq, hf, hi, hg, bg = _in_proj_fwd(x, g_mix, w_in, T, tm)
    qf, kf, vf = _mla_prep_fwd(cq, ckv, kr, pos, g_qa, g_kva, g_qn, g_kn, w_uq, w_ukv, T, tm)
    if late_blocks is None:
        attn, lse = _flash_fwd(qf, kf, vf, T)
    else:
        attn, lse, *late = _flash_fwd(qf, kf, vf, T, ag_blocks=[late_blocks[n] for n in LATE])
        big = {**big, **dict(zip(LATE, late))}
    w_branch = jnp.moveaxis(big["w_branch"].reshape(N_DEV, 2, HG_W, HEAD_PAD), 0, 2).reshape(2, HG_W, D_MODEL)
    w_bra = jnp.pad(w_branch[0].reshape(MLA_HEADS, V_DIM, D_MODEL),
                    ((0, 0), (0, HEAD_PAD - V_DIM), (0, 0))).reshape(MLA_HEADS * HEAD_PAD, D_MODEL)
    w_brb = w_branch[1]
    w_out = big["w_out"].reshape(D_MODEL, D_MODEL)
    w_g, w_u = big["w_ffn_gate"].reshape(FFN, D_MODEL), big["w_ffn_up"].reshape(FFN, D_MODEL)
    w_d = big["w_ffn_down"].reshape(FFN, D_MODEL)
    w_pg, w_pp = big["w_ple_gate"].reshape(D_MODEL, D_MODEL), big["w_ple_proj"]
    o, s0 = _hgrn_fwd(hq, hf, hi, lb, T)
    x1, ya, yb, m, rec = _merge_fwd(attn, o, hg, bg, x, g_out, w_bra, w_brb, w_out, T, tm)
    x2, gt, up, h2 = _ffn_fwd(x1, g_ffn, w_g, w_u, w_d, T, tm)
    dx2, n3, dz, du, loss_p, dg_post, dg_pg = _ple_loss(x2, p, tgt, g_pg, g_post, w_pg, w_pp, T, tm)

    grads, sibs, gots = {}, {}, {}

    def reduce_start(tag, names):
        if core is None:
            return ()
        got = _exchange_sibling("rs_sibling_" + tag, [grads[n] for n in names])
        sibs.update(zip(names, got))
        return [_chip_partial("rs_partial_" + n, grads[n], sibs[n], core) for n in names]

    dx1, a, dgt, dup, dg_ffn = _ffn_bwd(dx2, x1, gt, up, g_ffn, w_g, w_u, w_d, T, tm)
    grads["w_ffn_gate"] = _matmul_tn("dw_gate", dgt, h2).reshape(N_DEV, -1, D_MODEL)
    grads["w_ffn_up"] = _matmul_tn("dw_up", dup, h2).reshape(N_DEV, -1, D_MODEL)
    grads["w_ffn_down"] = _matmul_tn("dw_down", a, dx2).reshape(N_DEV, -1, D_MODEL)
    grads["w_ple_gate"] = _matmul_tn("dw_pg", n3, dz).reshape(N_DEV, -1, D_MODEL)
    grads["w_ple_proj"] = _matmul_tn("dw_pp", p, du, blocked=HEAD_PAD)
    parts_a = reduce_start("a", GROUP_A)

    dattn, do, dhg, dbg, dya, dyb, dg_out = _merge_bwd(dx1, ya, yb, bg, o, hg, g_out, w_bra, w_brb, w_out, T, tm)
    d_bra = _matmul_tn("dw_bra", attn, dya, blocked=HEAD_PAD)
    d_bra = d_bra.reshape(N_DEV, MLA_HEADS, HEAD_PAD, HEAD_PAD)[:, :, :V_DIM].reshape(N_DEV, HG_W, HEAD_PAD)
    grads["w_branch"] = jnp.concatenate([d_bra, _matmul_tn("dw_brb", rec, dyb, blocked=HEAD_PAD)], axis=1)
    grads["w_out"] = _matmul_tn("dw_out", m, dx1).reshape(N_DEV, -1, D_MODEL)
    parts_b = reduce_start("b", GROUP_B)

    dhq, dhf, dhi, dlb, *got_a = _hgrn_bwd(hq, hf, hi, do, s0, lb, T, xchg=parts_a)
    dqf, dkf, dvf, *got_b = _flash_bwd(qf, kf, vf, attn, dattn, lse, T, xchg=parts_b)
    (dcq, dckv, dkr, dq_raw, dkv, cqn, ckvn, dg_qa, dg_kva, dg_qn, dg_kn) = _mla_prep_bwd(
        cq, ckv, kr, pos, dqf, dkf, dvf, g_qa, g_kva, g_qn, g_kn, w_uq, w_ukv, T, tm)
    grad_x, dproj, dg_mix = _in_proj_bwd(x, dx1, [dcq, dckv, dkr, dhq, dhf, dhi, dhg, dbg], g_mix, w_in, T, tm)
    grads["w_in"] = _matmul_tn_blocks("dw_in", h, dproj)
    grads["w_uq"] = _matmul_tn("dw_uq", dq_raw, cqn).reshape(MLA_HEADS, HEAD_PAD, Q_RANK)[:, :QK_DIM]
    grads["w_ukv"] = _matmul_tn("dw_ukv", ckvn, dkv, blocked=HEAD_PAD)
    parts_c = reduce_start("c", GROUP_C)
    if core is not None:
        gots.update(zip(GROUP_A, got_a))
        gots.update(zip(GROUP_B, got_b))
        gots.update(zip(GROUP_C, _exchange_chips(parts_c)))

    dl0 = dlb * lb * (1.0 - lb)
    small_g = {
        "mix_norm_g": dg_mix, "q_a_norm_g": dg_qa, "kv_a_norm_g": dg_kva,
        "q_norm_g": dg_qn[:, :QK_DIM], "k_norm_g": dg_kn[:, :QK_DIM],
        "hg_lb_logits": jnp.concatenate([dl0, -dl0], axis=0), "hg_out_norm_g": dg_out,
        "ffn_norm_g": dg_ffn, "ple_gate_norm_g": dg_pg, "ple_post_norm_g": dg_post,
    }
    return loss_p, grad_x, small_g, grads, sibs, gots


def _lower_bound(logits):
    def body(l_ref, lb_ref):
        l = l_ref[...]
        mx = jnp.max(l, axis=0, keepdims=True)
        e = jnp.exp(l - mx)
        lb_ref[...] = e[0:1] / jnp.sum(e, axis=0, keepdims=True)

    return pl.pallas_call(body, name="lower_bound", out_shape=jax.ShapeDtypeStruct((1, HG_W), F32))(logits)


def _my_place():
    return lax.axis_index("x"), lax.axis_index("y"), lax.axis_index("c")


def _all_gather(name, blocks):
    n = len(blocks)

    def body(*refs):
        x_refs, out_refs, sems = refs[:n], refs[n:2 * n], refs[2 * n:]
        _ag_start(x_refs, out_refs, sems)
        _ag_finish(x_refs, out_refs, sems)

    any_spec = pl.BlockSpec(memory_space=pl.ANY)
    return pl.pallas_call(
        body, name=name, out_shape=_ag_out_shapes(blocks),
        in_specs=[any_spec] * n, out_specs=[any_spec] * n, scratch_shapes=_ag_sems(n),
    )(*blocks)


def _ag_out_shapes(blocks):
    return [jax.ShapeDtypeStruct((N_DEV,) + b.shape, b.dtype) for b in blocks]


def _ag_sems(n):
    return [pltpu.SemaphoreType.DMA((7 * n,)), pltpu.SemaphoreType.DMA((7 * n,)), pltpu.SemaphoreType.DMA((n,))]


def _ag_parts(x_refs, out_refs, sems):
    send_sems, recv_sems, local_sems = sems
    x, y, c = _my_place()
    me, sibling = (x, y, c), (x, y, 1 - c)
    chips = [(1 - x, y), (x, 1 - y), (1 - x, 1 - y)]
    n = len(x_refs)

    def copy(a, k, block, to, own=False):
        px, py, pc = block
        dst = out_refs[a].at[4 * px + 2 * py + pc]
        return pltpu.make_async_remote_copy(
            src_ref=x_refs[a] if own else dst, dst_ref=dst, send_sem=send_sems.at[7 * a + k],
            recv_sem=recv_sems.at[7 * a + k], device_id=to, device_id_type=MESH_ID)

    mine = [pltpu.make_async_copy(x_refs[a], out_refs[a].at[4 * x + 2 * y + c], local_sems.at[a]) for a in range(n)]
    first = []
    for a in range(n):
        first.append(copy(a, 0, me, sibling, own=True))
        first += [copy(a, 1 + j, me, (*chip, c), own=True) for j, chip in enumerate(chips)]
    return copy, mine, first, me, sibling, chips, c, n


def _ag_start(x_refs, out_refs, sems):
    _, mine, first, *_ = _ag_parts(x_refs, out_refs, sems)
    for cp in mine + first:
        cp.start()


def _ag_finish(x_refs, out_refs, sems):
    copy, mine, first, me, sibling, chips, c, n = _ag_parts(x_refs, out_refs, sems)
    passed = []
    for j, chip in enumerate(chips):
        for a in range(n):
            copy(a, 1 + j, (*chip, c), me).wait_recv()
            passed.append(copy(a, 4 + j, (*chip, c), sibling))
            passed[-1].start()
    for a in range(n):
        copy(a, 0, sibling, me).wait_recv()
    for j, chip in enumerate(chips):
        for a in range(n):
            copy(a, 4 + j, (*chip, 1 - c), me).wait_recv()
    for cp in first + passed:
        cp.wait_send()
    for cp in mine:
        cp.wait()


def _exchange_sibling(name, gs):
    n = len(gs)

    def body(*refs):
        g_refs, out_refs, (send_sems, recv_sems) = refs[:n], refs[n:2 * n], refs[2 * n:]
        x, y, c = _my_place()
        copies = [pltpu.make_async_remote_copy(
            src_ref=g_refs[a].at[2 * j + 1 - c], dst_ref=out_refs[a].at[j], send_sem=send_sems.at[4 * a + j],
            recv_sem=recv_sems.at[4 * a + j], device_id=(x, y, 1 - c), device_id_type=MESH_ID)
            for a in range(n) for j in range(4)]
        for cp in copies:
            cp.start()
        for cp in copies:
            cp.wait()

    any_spec = pl.BlockSpec(memory_space=pl.ANY)
    return pl.pallas_call(
        body, name=name, out_shape=[jax.ShapeDtypeStruct((4,) + g.shape[1:], g.dtype) for g in gs],
        in_specs=[any_spec] * n, out_specs=[any_spec] * n,
        scratch_shapes=[pltpu.SemaphoreType.DMA((4 * n,)), pltpu.SemaphoreType.DMA((4 * n,))],
    )(*gs)


def _chip_partial(name, g, got, c_idx):
    _, rows, cols = g.shape
    tr, tc = _tile_2d(rows, cols, 512)

    def body(c_ref, g_ref, got_ref, out_ref):
        out_ref[...] = (g_ref[...] + got_ref[...]).astype(MM)

    grid_spec = pltpu.PrefetchScalarGridSpec(
        num_scalar_prefetch=1, grid=(4, rows // tr, cols // tc),
        in_specs=[pl.BlockSpec((1, tr, tc), lambda j, i, k, c_ref: (2 * j + c_ref[0], i, k)),
                  pl.BlockSpec((1, tr, tc), lambda j, i, k, c_ref: (j, i, k))],
        out_specs=pl.BlockSpec((1, tr, tc), lambda j, i, k, c_ref: (j, i, k)))
    return pl.pallas_call(
        body, name=name, grid_spec=grid_spec, out_shape=jax.ShapeDtypeStruct((4, rows, cols), MM),
        compiler_params=_cparams(("parallel", "parallel", "parallel")),
    )(c_idx, g, got)


def _tile_2d(rows, cols, row_cap):
    if rows % 8 == 0:
        return _pick_rows(rows, row_cap), cols
    return rows, 256 if cols % 256 == 0 else cols


def _exchange_chips(parts):
    n = len(parts)

    def body(*refs):
        p_refs, out_refs, sems = refs[:n], refs[n:2 * n], refs[2 * n:]
        for cp in _xchips_copies(p_refs, out_refs, sems):
            cp.start()
        for cp in _xchips_copies(p_refs, out_refs, sems):
            cp.wait()

    any_spec = pl.BlockSpec(memory_space=pl.ANY)
    return pl.pallas_call(
        body, name="rs_chips", out_shape=_xchips_out_shapes(parts),
        in_specs=[any_spec] * n, out_specs=[any_spec] * n, scratch_shapes=_xchips_sems(n),
    )(*parts)


def _xchips_out_shapes(parts):
    return [jax.ShapeDtypeStruct((3,) + p.shape[1:], p.dtype) for p in parts]


def _xchips_sems(n):
    return [pltpu.SemaphoreType.DMA((3 * n,)), pltpu.SemaphoreType.DMA((3 * n,))]


def _xchips_copies(p_refs, out_refs, sems):
    send_sems, recv_sems = sems
    x, y, c = _my_place()
    chips = [(1 - x, y), (x, 1 - y), (1 - x, 1 - y)]
    return [pltpu.make_async_remote_copy(
        src_ref=p_refs[a].at[2 * px + py], dst_ref=out_refs[a].at[k], send_sem=send_sems.at[3 * a + k],
        recv_sem=recv_sems.at[3 * a + k], device_id=(px, py, c), device_id_type=MESH_ID)
        for a in range(len(p_refs)) for k, (px, py) in enumerate(chips)]


def _adamw_math(w, g, m, v):
    m = ADAM_B1 * m + (1.0 - ADAM_B1) * g
    v = ADAM_B2 * v + (1.0 - ADAM_B2) * jnp.square(g)
    m_hat = m / (1.0 - ADAM_B1 ** ADAM_STEP)
    v_hat = v / (1.0 - ADAM_B2 ** ADAM_STEP)
    delta = -ADAM_LR * (m_hat / (jnp.sqrt(v_hat) + ADAM_EPS) + ADAM_WD * w)
    return delta, m, v


def _sum_adamw(name, g, sib, got, w, m, v, slot_idx, chip_idx):
    _, rows, cols = g.shape
    tr, tc = _tile_2d(rows, cols, 256)

    def body(s_ref, j_ref, g_ref, sib_ref, got_ref, w_ref, m_ref, v_ref, go_ref, d_ref, m2_ref, v2_ref):
        grad = g_ref[0] + sib_ref[0]
        for k in range(3):
            grad = grad + got_ref[k].astype(F32)
        go_ref[...] = grad
        d_ref[...], m2_ref[...], v2_ref[...] = _adamw_math(w_ref[...], grad, m_ref[...], v_ref[...])

    flat = pl.BlockSpec((tr, tc), lambda i, k, s_ref, j_ref: (i, k))
    grid_spec = pltpu.PrefetchScalarGridSpec(
        num_scalar_prefetch=2, grid=(rows // tr, cols // tc),
        in_specs=[pl.BlockSpec((1, tr, tc), lambda i, k, s_ref, j_ref: (s_ref[0], i, k)),
                  pl.BlockSpec((1, tr, tc), lambda i, k, s_ref, j_ref: (j_ref[0], i, k)),
                  pl.BlockSpec((3, tr, tc), lambda i, k, s_ref, j_ref: (0, i, k)), flat, flat, flat],
        out_specs=[flat] * 4)
    return pl.pallas_call(
        body, name=name, grid_spec=grid_spec, out_shape=[jax.ShapeDtypeStruct((rows, cols), F32)] * 4,
        compiler_params=_cparams(("parallel", "parallel")),
    )(slot_idx, chip_idx, g, sib, got, w, m, v)


def _adamw_small(parts, w, m, v):
    rows = w.shape[0]

    def body(p_ref, w_ref, m_ref, v_ref, g_ref, d_ref, m2_ref, v2_ref):
        g = p_ref[0]
        for d in range(1, N_DEV):
            g = g + p_ref[d]
        g_ref[...] = g
        d_ref[...], m2_ref[...], v2_ref[...] = _adamw_math(w_ref[...], g, m_ref[...], v_ref[...])

    return pl.pallas_call(
        body, name="adamw_small", out_shape=[jax.ShapeDtypeStruct((rows, 128), F32)] * 4,
    )(parts, w, m, v)


BIG = ("w_in", "w_uq", "w_ukv", "w_branch", "w_out", "w_ffn_gate", "w_ffn_up", "w_ffn_down", "w_ple_gate", "w_ple_proj")
SMALL = (
    ("mix_norm_g", 1024), ("q_a_norm_g", 384), ("kv_a_norm_g", 256), ("q_norm_g", 96), ("k_norm_g", 96),
    ("hg_lb_logits", 1024), ("hg_out_norm_g", 128), ("ffn_norm_g", 1024), ("ple_gate_norm_g", 1024),
    ("ple_post_norm_g", 1024),
)
SMALL_ROWS = 56


def _pack_small(vals):
    rows = []
    for name, n in SMALL:
        v = vals[name].reshape(1, -1).astype(F32)
        rows.append(jnp.pad(v, ((0, 0), (0, (-n) % 128))).reshape(-1, 128))
    return jnp.concatenate(rows, axis=0)


def _unpack_small(packed, shapes):
    out, r = {}, 0
    for name, n in SMALL:
        k = (n + 127) // 128
        out[name] = packed[r:r + k].reshape(1, -1)[:, :n].reshape(shapes[name])
        r += k
    return out


_WEIGHTS = ["mix_norm_g", "w_in", "q_a_norm_g", "w_uq", "kv_a_norm_g", "w_ukv", "q_norm_g", "k_norm_g", "hg_lb_logits",
            "hg_out_norm_g", "w_branch", "w_out", "ffn_norm_g", "w_ffn_gate", "w_ffn_up", "w_ffn_down",
            "ple_gate_norm_g", "w_ple_gate", "w_ple_proj", "ple_post_norm_g"]


def _step(x, p, positions, tgt, w, m, v):
    small_names = [n for n, _ in SMALL]
    T = x.shape[1]
    px, py, pc = _my_place()
    as_idx = lambda t: jnp.reshape(t, (1,)).astype(jnp.int32)

    def two_d(n, t):
        t = t.reshape(-1, t.shape[-1])
        return t.T if n in TRANSPOSED else t

    def full_shape(n, t):
        return (t.T if n in TRANSPOSED else t).reshape(w[n].shape)

    blocks = {n: two_d(n, w[n]).astype(MM) for n in BIG}
    big = dict(zip(EARLY, _all_gather("ag_weights", [blocks[n] for n in EARLY])))
    small = {n: (w[n] if n == "hg_lb_logits" else w[n].reshape(1, -1)) for n in small_names}

    loss_p, grad_x, small_g, grads, sibs, gots = _local_step(
        x[0], p[0, 0], positions.reshape(T, 1), tgt[0], small, big, late_blocks=blocks, core=as_idx(pc))

    out_g, out_d, out_m, out_v = {}, {}, {}, {}
    for n in BIG:
        res = _sum_adamw("adamw_" + n, grads[n], sibs[n], gots[n], two_d(n, w[n]), two_d(n, m[n]), two_d(n, v[n]),
                         as_idx(4 * px + 2 * py + pc), as_idx(2 * px + py))
        out_g[n], out_d[n], out_m[n], out_v[n] = [full_shape(n, r) for r in res]

    packed_g = _pack_small(small_g)
    loss_row = jnp.concatenate([jnp.pad(jnp.sum(loss_p).reshape(1, 1), ((0, 0), (0, 127))),
                                jnp.zeros((SMALL_ROWS - packed_g.shape[0] - 1, 128), F32)], axis=0)
    parts = _all_gather("ag_small", [jnp.concatenate([packed_g, loss_row], axis=0)])[0]
    pad_rows = lambda t: jnp.pad(t, ((0, SMALL_ROWS - t.shape[0]), (0, 0)))
    sw = pad_rows(_pack_small({n: w[n] for n in small_names}))
    sm = pad_rows(_pack_small({n: m[n] for n in small_names}))
    sv = pad_rows(_pack_small({n: v[n] for n in small_names}))
    g_s, d_s, m_s, v_s = _adamw_small(parts, sw, sm, sv)
    shapes = {n: w[n].shape for n in small_names}
    n_packed = packed_g.shape[0]
    loss = g_s[n_packed, 0]
    for src, dst in ((g_s, out_g), (d_s, out_d), (m_s, out_m), (v_s, out_v)):
        dst.update(_unpack_small(src, shapes))

    outs = [loss, grad_x[None]]
    for table in (out_g, out_d, out_m, out_v):
        outs += [table[n] for n in _WEIGHTS]
    return tuple(outs)


def kernel(x, p, positions, mix_norm_g, w_in, q_a_norm_g, w_uq, kv_a_norm_g, w_ukv, q_norm_g, k_norm_g, hg_lb_logits, hg_out_norm_g, w_branch, w_out, ffn_norm_g, w_ffn_gate, w_ffn_up, w_ffn_down, ple_gate_norm_g, w_ple_gate, w_ple_proj, ple_post_norm_g, loss_target, m_mix_norm_g, m_w_in, m_q_a_norm_g, m_w_uq, m_kv_a_norm_g, m_w_ukv, m_q_norm_g, m_k_norm_g, m_hg_lb_logits, m_hg_out_norm_g, m_w_branch, m_w_out, m_ffn_norm_g, m_w_ffn_gate, m_w_ffn_up, m_w_ffn_down, m_ple_gate_norm_g, m_w_ple_gate, m_w_ple_proj, m_ple_post_norm_g, v_mix_norm_g, v_w_in, v_q_a_norm_g, v_w_uq, v_kv_a_norm_g, v_w_ukv, v_q_norm_g, v_k_norm_g, v_hg_lb_logits, v_hg_out_norm_g, v_w_branch, v_w_out, v_ffn_norm_g, v_w_ffn_gate, v_w_ffn_up, v_w_ffn_down, v_ple_gate_norm_g, v_w_ple_gate, v_w_ple_proj, v_ple_post_norm_g):
    w = dict(mix_norm_g=mix_norm_g, w_in=w_in, q_a_norm_g=q_a_norm_g, w_uq=w_uq, kv_a_norm_g=kv_a_norm_g, w_ukv=w_ukv,
             q_norm_g=q_norm_g, k_norm_g=k_norm_g, hg_lb_logits=hg_lb_logits, hg_out_norm_g=hg_out_norm_g,
             w_branch=w_branch, w_out=w_out, ffn_norm_g=ffn_norm_g, w_ffn_gate=w_ffn_gate, w_ffn_up=w_ffn_up,
             w_ffn_down=w_ffn_down, ple_gate_norm_g=ple_gate_norm_g, w_ple_gate=w_ple_gate, w_ple_proj=w_ple_proj,
             ple_post_norm_g=ple_post_norm_g)
    m = dict(mix_norm_g=m_mix_norm_g, w_in=m_w_in, q_a_norm_g=m_q_a_norm_g, w_uq=m_w_uq, kv_a_norm_g=m_kv_a_norm_g,
             w_ukv=m_w_ukv, q_norm_g=m_q_norm_g, k_norm_g=m_k_norm_g, hg_lb_logits=m_hg_lb_logits,
             hg_out_norm_g=m_hg_out_norm_g, w_branch=m_w_branch, w_out=m_w_out, ffn_norm_g=m_ffn_norm_g,
             w_ffn_gate=m_w_ffn_gate, w_ffn_up=m_w_ffn_up, w_ffn_down=m_w_ffn_down,
             ple_gate_norm_g=m_ple_gate_norm_g, w_ple_gate=m_w_ple_gate, w_ple_proj=m_w_ple_proj,
             ple_post_norm_g=m_ple_post_norm_g)
    v = dict(mix_norm_g=v_mix_norm_g, w_in=v_w_in, q_a_norm_g=v_q_a_norm_g, w_uq=v_w_uq, kv_a_norm_g=v_kv_a_norm_g,
             w_ukv=v_w_ukv, q_norm_g=v_q_norm_g, k_norm_g=v_k_norm_g, hg_lb_logits=v_hg_lb_logits,
             hg_out_norm_g=v_hg_out_norm_g, w_branch=v_w_branch, w_out=v_w_out, ffn_norm_g=v_ffn_norm_g,
             w_ffn_gate=v_w_ffn_gate, w_ffn_up=v_w_ffn_up, w_ffn_down=v_w_ffn_down,
             ple_gate_norm_g=v_ple_gate_norm_g, w_ple_gate=v_w_ple_gate, w_ple_proj=v_w_ple_proj,
             ple_post_norm_g=v_ple_post_norm_g)
    return _step(x, p, positions, loss_target, w, m, v)
```

```python
import functools

import jax
import jax.numpy as jnp
import numpy as np
from jax import lax
from jax.experimental import pallas as pl
from jax.experimental.pallas import tpu as pltpu

F32 = jnp.float32
MM = jnp.bfloat16
HI = lax.Precision.HIGHEST
MESH_ID = pl.DeviceIdType.MESH

D_MODEL = 1024
N_DEV = 8
MLA_HEADS = 8
QK_NOPE = 64
QK_ROPE = 32
QK_DIM = 96
V_DIM = 64
HEAD_PAD = 128
Q_RANK = 384
KV_RANK = 256
ROPE_BASE = 10000.0
HG_HEADS = 4
HG_DIM = 128
HG_W = 512
HG_CHUNK = 64
HG_SUB = 16
FFN = 2816
PLE = 256
EPS = 1e-6
ATT_SCALE = QK_DIM ** -0.5
NEG = -1e30

ADAM_LR = 0.001
ADAM_B1 = 0.9
ADAM_B2 = 0.999
ADAM_EPS = 1e-08
ADAM_WD = 0.01
ADAM_STEP = 10

SEC_CQ = (0, 384)
SEC_CKV = (384, 256)
SEC_KR = (640, 128)
SEC_HQ = (768, 512)
SEC_HF = (1280, 512)
SEC_HI = (1792, 512)
SEC_HG = (2304, 512)
SEC_BG = (2816, 2048)
IN_PAD = 4864
SECTIONS = (SEC_CQ, SEC_CKV, SEC_KR, SEC_HQ, SEC_HF, SEC_HI, SEC_HG, SEC_BG)
COL_SECTIONS = ((0, 384), (384, 256), (640, 32), (672, 512), (1184, 512), (1696, 512), (2208, 512), (2720, 2048))
IN_COLS = 4768
IN_BLOCK = IN_COLS // 8

VMEM_LIMIT = 58 * 1024 * 1024
ROW_TILE = 256
ATT_TILE = 512
ATT_HEADS = 2
HG_BLOCK = 512


def _dot(a, b):
    return jnp.dot(a.astype(MM), b.astype(MM), preferred_element_type=F32)


def _dot_nt(a, b):
    return lax.dot_general(a.astype(MM), b.astype(MM), (((1,), (1,)), ((), ())), preferred_element_type=F32)


def _dot_tn(a, b):
    return lax.dot_general(a.astype(MM), b.astype(MM), (((0,), (0,)), ((), ())), preferred_element_type=F32)


def _dot_hi(a, b):
    return jnp.dot(a, b, preferred_element_type=F32, precision=HI)


def _sigmoid(x):
    return 1.0 / (1.0 + jnp.exp(-x))


def _rms(x, n=None):
    n = x.shape[-1] if n is None else n
    r = lax.rsqrt(jnp.sum(x * x, axis=-1, keepdims=True) * (1.0 / n) + EPS)
    return x * r, r


def _rms_bwd(dxh, xh, r, n=None):
    n = xh.shape[-1] if n is None else n
    return r * (dxh - xh * (jnp.sum(dxh * xh, axis=-1, keepdims=True) * (1.0 / n)))


def _rope_tables(pos, tm):
    lane = lax.broadcasted_iota(jnp.int32, (tm, HEAD_PAD), 1)
    idx = jnp.where(lane < QK_NOPE + QK_ROPE // 2, lane - QK_NOPE, lane - QK_NOPE - QK_ROPE // 2)
    inv = jnp.exp(idx.astype(F32) * (-np.log(ROPE_BASE) * 2.0 / QK_ROPE))
    ang = pos.astype(F32) * inv
    in_rope = (lane >= QK_NOPE) & (lane < QK_DIM)
    first = lane < QK_NOPE + QK_ROPE // 2
    cos_t = jnp.where(in_rope, jnp.cos(ang), 1.0)
    sin_t = jnp.where(in_rope, jnp.where(first, -jnp.sin(ang), jnp.sin(ang)), 0.0)
    return cos_t, sin_t, (first, in_rope)


def _rope_swap(x, halves):
    first, in_rope = halves
    half = QK_ROPE // 2
    return jnp.where(in_rope, jnp.where(first, pltpu.roll(x, HEAD_PAD - half, 1), pltpu.roll(x, half, 1)), 0.0)


def _cparams(sem, vmem=None):
    return pltpu.CompilerParams(dimension_semantics=sem, vmem_limit_bytes=vmem)


def _row_call(name, body, T, tm, row_ins, full_ins, row_outs, acc_outs, vmem=None):
    def kern(*refs):
        body(pl.program_id(0), *refs)

    in_specs = [pl.BlockSpec((tm, a.shape[1]), lambda i: (i, 0)) for a in row_ins]
    in_specs += [pl.BlockSpec(a.shape, lambda i, nd=a.ndim: (0,) * nd, pipeline_mode=pl.Buffered(1)) for a in full_ins]
    out_specs = [pl.BlockSpec((tm, n), lambda i: (i, 0)) for n, _ in row_outs]
    out_specs += [pl.BlockSpec(s, lambda i, nd=len(s): (0,) * nd) for s, _ in acc_outs]
    out_shape = [jax.ShapeDtypeStruct((T, n), dt) for n, dt in row_outs]
    out_shape += [jax.ShapeDtypeStruct(s, dt) for s, dt in acc_outs]
    return pl.pallas_call(
        kern, name=name, grid=(T // tm,), in_specs=in_specs, out_specs=out_specs, out_shape=out_shape,
        compiler_params=_cparams(("arbitrary",), vmem),
    )(*row_ins, *full_ins)


def _acc(ref, i, val):
    @pl.when(i == 0)
    def _():
        ref[...] = val

    @pl.when(i != 0)
    def _():
        ref[...] += val


def _in_proj_fwd(x, g_mix, w_in, T, tm):
    def body(i, x_ref, g_ref, w_ref, h_ref, *outs):
        xh, _ = _rms(x_ref[...])
        h = (xh * g_ref[...]).astype(MM)
        h_ref[...] = h
        proj = jnp.concatenate([_dot_nt(h, w_ref[d]) for d in range(N_DEV)], axis=1)
        for (s, n), o_ref in zip(COL_SECTIONS, outs):
            if n == QK_ROPE:
                o_ref[...] = jnp.concatenate(
                    [jnp.zeros((tm, QK_NOPE), F32), proj[:, s:s + n], jnp.zeros((tm, HEAD_PAD - QK_DIM), F32)], axis=1)
            else:
                o_ref[...] = proj[:, s:s + n]

    row_outs = [(D_MODEL, MM)] + [(n, F32) for _, n in SECTIONS]
    return _row_call("in_proj_fwd", body, T, tm, [x], [g_mix, w_in], row_outs, [], VMEM_LIMIT)


def _mla_heads_fwd(raw, g_pad, cos_t, sin_t, first):
    outs, saved = [], []
    for h in range(MLA_HEADS):
        xh, r = _rms(raw[:, h * HEAD_PAD:(h + 1) * HEAD_PAD], QK_DIM)
        y = xh * g_pad
        outs.append(y * cos_t + _rope_swap(y, first) * sin_t)
        saved.append((xh, r))
    return outs, saved


def _mla_raw_heads(cqn, ckvn, kr, wuq_ref, wukv_ref, tm):
    lane = lax.broadcasted_iota(jnp.int32, (tm, HEAD_PAD), 1)
    nope = lane < QK_NOPE
    one_lane = jnp.where(lane == V_DIM, 1.0, 0.0)
    qs, ks, vs = [], [], []
    for h in range(MLA_HEADS):
        qs.append(_dot_nt(cqn, wuq_ref[h]))
        kv = _dot(ckvn, wukv_ref[h])
        ks.append(jnp.where(nope, kv, kr))
        vs.append(jnp.where(nope, pltpu.roll(kv, V_DIM, 1), one_lane))
    return jnp.concatenate(qs, axis=1), jnp.concatenate(ks, axis=1), jnp.concatenate(vs, axis=1)


def _mla_prep_fwd(cq, ckv, kr, pos, g_qa, g_kva, g_qn, g_kn, w_uq, w_ukv, T, tm):
    def body(i, cq_ref, ckv_ref, kr_ref, pos_ref, gqa_ref, gkva_ref, gqn_ref, gkn_ref, wuq_ref, wukv_ref,
             q_ref, k_ref, v_ref):
        cos_t, sin_t, first = _rope_tables(pos_ref[...], tm)
        cqn = _rms(cq_ref[...])[0] * gqa_ref[...]
        ckvn = _rms(ckv_ref[...])[0] * gkva_ref[...]
        q_raw, k_raw, v = _mla_raw_heads(cqn, ckvn, kr_ref[...], wuq_ref, wukv_ref, tm)
        qs, _ = _mla_heads_fwd(q_raw, gqn_ref[...], cos_t, sin_t, first)
        ks, _ = _mla_heads_fwd(k_raw, gkn_ref[...], cos_t, sin_t, first)
        q_ref[...] = (jnp.concatenate(qs, axis=1) * ATT_SCALE).astype(MM)
        k_ref[...] = jnp.concatenate(ks, axis=1).astype(MM)
        v_ref[...] = v.astype(MM)

    w = MLA_HEADS * HEAD_PAD
    return _row_call("mla_prep_fwd", body, T, tm, [cq, ckv, kr, pos], [g_qa, g_kva, g_qn, g_kn, w_uq, w_ukv],
                     [(w, MM), (w, MM), (w, MM)], [])


def _causal_pairs(n, by_query):
    if by_query:
        pairs = [(q, k) for q in range(n) for k in range(q + 1)]
    else:
        pairs = [(q, k) for k in range(n) for q in range(k, n)]
    return np.array([p[0] for p in pairs], np.int32), np.array([p[1] for p in pairs], np.int32)


def _flash_fwd(qf, kf, vf, T, ag_blocks=()):
    tq = min(ATT_TILE, T)
    nq = T // tq

    qi_tab, ki_tab = _causal_pairs(nq, by_query=True)

    hp = ATT_HEADS

    n_ag = len(ag_blocks)
    n_heads, n_pairs = MLA_HEADS // hp, len(qi_tab)

    def body(qi_ref, ki_ref, q_ref, k_ref, v_ref, *rest):
        ag_in, (o_ref, lse_ref), rest = rest[:n_ag], rest[n_ag:n_ag + 2], rest[n_ag + 2:]
        ag_out, (m_s, acc_s), ag_sems = rest[:n_ag], rest[n_ag:n_ag + 2], rest[n_ag + 2:]
        t = pl.program_id(1)
        qi, ki = qi_ref[t], ki_ref[t]
        if n_ag:
            @pl.when((pl.program_id(0) == 0) & (t == 0))
            def _():
                _ag_start(ag_in, ag_out, ag_sems)

        @pl.when(ki == 0)
        def _():
            m_s[...] = jnp.full_like(m_s, NEG)
            acc_s[...] = jnp.zeros_like(acc_s)

        def step(masked):
            for hh in range(hp):
                hs = slice(hh * HEAD_PAD, (hh + 1) * HEAD_PAD)
                s_t = _dot_nt(k_ref[:, hs], q_ref[:, hs])
                if masked:
                    key = lax.broadcasted_iota(jnp.int32, (tq, tq), 0)
                    qry = lax.broadcasted_iota(jnp.int32, (tq, tq), 1)
                    s_t = jnp.where(key <= qry, s_t, NEG)
                m_old = m_s[hh]
                m_new = jnp.maximum(m_old, jnp.max(s_t, axis=0, keepdims=True))
                p_t = jnp.exp(s_t - m_new)
                acc_s[hh] = jnp.exp(m_old - m_new) * acc_s[hh] + _dot_tn(v_ref[:, hs], p_t)
                m_s[hh] = m_new

        @pl.when(ki < qi)
        def _():
            step(False)

        @pl.when(ki == qi)
        def _():
            step(True)
            real = lax.broadcasted_iota(jnp.int32, (HEAD_PAD, tq), 0) < V_DIM
            for hh in range(hp):
                hs = slice(hh * HEAD_PAD, (hh + 1) * HEAD_PAD)
                acc = acc_s[hh]
                l = acc[V_DIM:V_DIM + 1]
                o_ref[:, hs] = jnp.where(real, acc / l, 0.0).T
                lse_ref[:, hs] = jnp.broadcast_to(m_s[hh] + jnp.log(l), (HEAD_PAD, tq)).T

        if n_ag:
            @pl.when((pl.program_id(0) == n_heads - 1) & (t == n_pairs - 1))
            def _():
                _ag_finish(ag_in, ag_out, ag_sems)

    q_spec = pl.BlockSpec((tq, hp * HEAD_PAD), lambda h, t, qi_ref, ki_ref: (qi_ref[t], h))
    kv_spec = pl.BlockSpec((tq, hp * HEAD_PAD), lambda h, t, qi_ref, ki_ref: (ki_ref[t], h))
    any_spec = pl.BlockSpec(memory_space=pl.ANY)
    grid_spec = pltpu.PrefetchScalarGridSpec(
        num_scalar_prefetch=2, grid=(n_heads, n_pairs),
        in_specs=[q_spec, kv_spec, kv_spec] + [any_spec] * n_ag, out_specs=[q_spec, q_spec] + [any_spec] * n_ag,
        scratch_shapes=[pltpu.VMEM((hp, 1, tq), F32), pltpu.VMEM((hp, HEAD_PAD, tq), F32)]
        + (_ag_sems(n_ag) if n_ag else []))
    return pl.pallas_call(
        body, name="flash_fwd", grid_spec=grid_spec,
        out_shape=[jax.ShapeDtypeStruct((T, MLA_HEADS * HEAD_PAD), F32)] * 2 + _ag_out_shapes(ag_blocks),
        compiler_params=_cparams(("arbitrary", "arbitrary")),
    )(jnp.asarray(qi_tab), jnp.asarray(ki_tab), qf, kf, vf, *ag_blocks)


def _hg_gates(hf, lb):
    sg = _sigmoid(hf)
    f = lb + (1.0 - lb) * sg
    return sg, f, jnp.log(f), 1.0 - f


def _tri(n, lower):
    r = lax.broadcasted_iota(jnp.int32, (n, n), 0)
    c = lax.broadcasted_iota(jnp.int32, (n, n), 1)
    return jnp.where((c <= r) if lower else (c >= r), 1.0, 0.0).astype(F32)


def _hg_intra(q, k, b):
    C, S = HG_CHUNK, HG_SUB
    row_c = lax.broadcasted_iota(jnp.int32, (C, HG_DIM), 0)
    row_s = lax.broadcasted_iota(jnp.int32, (S, HG_DIM), 0)
    lane_c = lax.broadcasted_iota(jnp.int32, (S, C), 1)
    blocks, saved = [], []
    for blk in range(C // S):
        lo = blk * S
        q_b, k_b, b_b = q[lo:lo + S], k[lo:lo + S], b[lo:lo + S]
        a_b = jnp.zeros((S, C), F32)
        for j in range(S):
            w = jnp.exp(jnp.where(row_s >= j, b_b - b_b[j:j + 1], NEG))
            col = jnp.sum(q_b * (k_b[j:j + 1] * w), axis=1, keepdims=True)
            a_b = jnp.where(lane_c == lo + j, col, a_b)
        if blk > 0:
            ref = b[lo - 1:lo]
            q_e = jnp.exp(b_b - ref)
            q_t = q_b * q_e
            k_e = jnp.exp(jnp.where(row_c < lo, ref - b, NEG))
            a_b = a_b + _dot_nt(q_t, k * k_e)
            saved.append((q_t, k_e, q_e))
        else:
            saved.append(None)
        blocks.append(a_b)
    return jnp.concatenate(blocks, axis=0), saved


def _hgrn_fwd(hq, hf, hi, lb, T):
    rb = min(HG_BLOCK, T)
    ncb = rb // HG_CHUNK

    def body(hq_ref, hf_ref, hi_ref, lb_ref, o_ref, s0_ref, st_ref):
        @pl.when(pl.program_id(0) == 0)
        def _():
            st_ref[...] = jnp.zeros_like(st_ref)

        tril = _tri(HG_CHUNK, True)

        def chunk(c, carry):
            rows = pl.ds(pl.multiple_of(c * HG_CHUNK, HG_CHUNK), HG_CHUNK)
            _, _, logf, kk = _hg_gates(hf_ref[rows, :], lb_ref[...])
            b = _dot_hi(tril, logf)
            q_all, v_all = hq_ref[rows, :], hi_ref[rows, :]
            outs = []
            for h in range(HG_HEADS):
                ls = slice(h * HG_DIM, (h + 1) * HG_DIM)
                q, k, v, bh = q_all[:, ls], kk[:, ls], v_all[:, ls], b[:, ls]
                st = st_ref[h]
                s0_ref[c, h * HG_DIM:(h + 1) * HG_DIM, :] = st
                b_end = bh[HG_CHUNK - 1:HG_CHUNK]
                a, _ = _hg_intra(q, k, bh)
                outs.append(_dot_nt(q * jnp.exp(bh), st) + _dot(a, v))
                st_ref[h] = st * jnp.exp(b_end) + _dot_tn(v, k * jnp.exp(b_end - bh))
            o_ref[rows, :] = jnp.concatenate(outs, axis=1)
            return carry

        lax.fori_loop(0, ncb, chunk, 0)

    row = pl.BlockSpec((rb, HG_W), lambda i: (i, 0))
    return pl.pallas_call(
        body, name="hgrn_fwd", grid=(T // rb,),
        in_specs=[row, row, row, pl.BlockSpec((1, HG_W), lambda i: (0, 0))],
        out_specs=[row, pl.BlockSpec((ncb, HG_W, HG_DIM), lambda i: (i, 0, 0))],
        out_shape=[jax.ShapeDtypeStruct((T, HG_W), F32), jax.ShapeDtypeStruct((T // HG_CHUNK, HG_W, HG_DIM), F32)],
        scratch_shapes=[pltpu.VMEM((HG_HEADS, HG_DIM, HG_DIM), F32)],
        compiler_params=_cparams(("arbitrary",)),
    )(hq, hf, hi, lb)


def _hgrn_bwd(hq, hf, hi, do, s0, lb, T, xchg=()):
    rb = min(HG_BLOCK, T)
    ncb = rb // HG_CHUNK
    nb = T // rb
    C, S = HG_CHUNK, HG_SUB
    n_x = len(xchg)

    def body(hq_ref, hf_ref, hi_ref, do_ref, s0_ref, lb_ref, *rest):
        x_in, (dq_ref, df_ref, dv_ref, dlb_ref), rest = rest[:n_x], rest[n_x:n_x + 4], rest[n_x + 4:]
        x_out, dst_ref, x_sems = rest[:n_x], rest[n_x], rest[n_x + 1:]

        @pl.when(pl.program_id(0) == 0)
        def _():
            dst_ref[...] = jnp.zeros_like(dst_ref)
            dlb_ref[...] = jnp.zeros_like(dlb_ref)
            for cp in _xchips_copies(x_in, x_out, x_sems) if n_x else ():
                cp.start()

        tril, triu = _tri(C, True), _tri(C, False)
        row_cc = lax.broadcasted_iota(jnp.int32, (C, C), 0)
        col_cc = lax.broadcasted_iota(jnp.int32, (C, C), 1)
        row_s = lax.broadcasted_iota(jnp.int32, (S, HG_DIM), 0)
        lane_sc = lax.broadcasted_iota(jnp.int32, (S, C), 1)
        last_row = lax.broadcasted_iota(jnp.int32, (C, HG_DIM), 0) == C - 1
        lb_v = lb_ref[...]

        def chunk(cc, carry):
            c = ncb - 1 - cc
            rows = pl.ds(pl.multiple_of(c * C, C), C)
            hf_c = hf_ref[rows, :]
            sg, f, logf, kk = _hg_gates(hf_c, lb_v)
            b = _dot_hi(tril, logf)
            q_all, v_all, do_all = hq_ref[rows, :], hi_ref[rows, :], do_ref[rows, :]
            dq_o, dk_o, dv_o, db_o = [], [], [], []
            for h in range(HG_HEADS):
                ls = slice(h * HG_DIM, (h + 1) * HG_DIM)
                q, k, v, bh, d_o = q_all[:, ls], kk[:, ls], v_all[:, ls], b[:, ls], do_all[:, ls]
                st0 = s0_ref[c, h * HG_DIM:(h + 1) * HG_DIM, :]
                dst = dst_ref[h]
                b_end = bh[C - 1:C]
                e_b, e_end = jnp.exp(bh), jnp.exp(b_end)
                e_rem = jnp.exp(b_end - bh)
                qe, kd = q * e_b, k * e_rem
                st_end = st0 * e_end + _dot_tn(v, kd)
                a, saved = _hg_intra(q, k, bh)
                d_a = jnp.where(col_cc <= row_cc, _dot_nt(d_o, v), 0.0)
                dv = _dot_tn(a, d_o) + _dot_nt(kd, dst)
                dq = e_b * _dot(d_o, st0)
                dk = e_rem * _dot(v, dst)
                dq_blocks, dk_diag = [], []
                for blk in range(C // S):
                    lo = blk * S
                    q_b, k_b, b_b = q[lo:lo + S], k[lo:lo + S], bh[lo:lo + S]
                    da_b = d_a[lo:lo + S]
                    dq_b = jnp.zeros((S, HG_DIM), F32)
                    dk_b = jnp.zeros((S, HG_DIM), F32)
                    for j in range(S):
                        w = jnp.exp(jnp.where(row_s >= j, b_b - b_b[j:j + 1], NEG))
                        col = jnp.sum(jnp.where(lane_sc == lo + j, da_b, 0.0), axis=1, keepdims=True)
                        dq_b = dq_b + col * (k_b[j:j + 1] * w)
                        dk_row = jnp.sum(col * (q_b * w), axis=0, keepdims=True)
                        dk_b = jnp.where(row_s == j, dk_row, dk_b)
                    if blk > 0:
                        q_t, k_e, q_e = saved[blk]
                        da_off = jnp.where(lane_sc < lo, da_b, 0.0)
                        dq_b = dq_b + _dot(da_off, k * k_e) * q_e
                        dk = dk + _dot_tn(da_off, q_t) * k_e
                    dq_blocks.append(dq_b)
                    dk_diag.append(dk_b)
                dq = dq + jnp.concatenate(dq_blocks, axis=0)
                dk = dk + jnp.concatenate(dk_diag, axis=0)
                extra = jnp.sum(dst * st_end, axis=0, keepdims=True)
                db_o.append(q * dq - k * dk + jnp.where(last_row, extra, 0.0))
                dst_ref[h] = dst * e_end + _dot_tn(d_o, qe)
                dq_o.append(dq)
                dk_o.append(dk)
                dv_o.append(dv)
            dlogf = _dot_hi(triu, jnp.concatenate(db_o, axis=1))
            d_f = dlogf / f - jnp.concatenate(dk_o, axis=1)
            dq_ref[rows, :] = jnp.concatenate(dq_o, axis=1)
            dv_ref[rows, :] = jnp.concatenate(dv_o, axis=1)
            df_ref[rows, :] = d_f * (1.0 - lb_v) * sg * (1.0 - sg)
            dlb_ref[...] += jnp.sum(d_f * (1.0 - sg), axis=0, keepdims=True)
            return carry

        lax.fori_loop(0, ncb, chunk, 0)

        if n_x:
            @pl.when(pl.program_id(0) == nb - 1)
            def _():
                for cp in _xchips_copies(x_in, x_out, x_sems):
                    cp.wait()

    row = pl.BlockSpec((rb, HG_W), lambda i: (nb - 1 - i, 0))
    one = pl.BlockSpec((1, HG_W), lambda i: (0, 0))
    any_spec = pl.BlockSpec(memory_space=pl.ANY)
    return pl.pallas_call(
        body, name="hgrn_bwd", grid=(nb,),
        in_specs=[row, row, row, row, pl.BlockSpec((ncb, HG_W, HG_DIM), lambda i: (nb - 1 - i, 0, 0)), one]
        + [any_spec] * n_x,
        out_specs=[row, row, row, one] + [any_spec] * n_x,
        out_shape=[jax.ShapeDtypeStruct((T, HG_W), F32)] * 3 + [jax.ShapeDtypeStruct((1, HG_W), F32)]
        + _xchips_out_shapes(xchg),
        scratch_shapes=[pltpu.VMEM((HG_HEADS, HG_DIM, HG_DIM), F32)] + (_xchips_sems(n_x) if n_x else []),
        compiler_params=_cparams(("arbitrary",)),
    )(hq, hf, hi, do, s0, lb, *xchg)


def _silu_parts(x):
    sg = _sigmoid(x)
    return x * sg, sg * (1.0 + x * (1.0 - sg))


def _merge_fwd(attn, o, hg, bg, x, g_out, w_bra, w_brb, w_out, T, tm):
    def body(i, attn_ref, o_ref, hg_ref, bg_ref, x_ref, g_ref, wa_ref, wb_ref, wo_ref,
             x1_ref, ya_ref, yb_ref, m_ref, rec_ref):
        recs = []
        for h in range(HG_HEADS):
            ls = slice(h * HG_DIM, (h + 1) * HG_DIM)
            oh, _ = _rms(o_ref[:, ls])
            recs.append(oh * g_ref[...] * _silu_parts(hg_ref[:, ls])[0])
        rec = jnp.concatenate(recs, axis=1)
        ya = _dot(attn_ref[...], wa_ref[...])
        yb = _dot(rec, wb_ref[...])
        m = _sigmoid(bg_ref[:, :D_MODEL]) * ya + _sigmoid(bg_ref[:, D_MODEL:]) * yb
        x1_ref[...] = x_ref[...] + _dot(m, wo_ref[...])
        ya_ref[...] = ya
        yb_ref[...] = yb
        m_ref[...] = m.astype(MM)
        rec_ref[...] = rec.astype(MM)

    return _row_call("merge_fwd", body, T, tm, [attn, o, hg, bg, x], [g_out, w_bra, w_brb, w_out],
                     [(D_MODEL, F32), (D_MODEL, F32), (D_MODEL, F32), (D_MODEL, MM), (HG_W, MM)], [], VMEM_LIMIT)


def _ffn_fwd(x1, g_ffn, w_g, w_u, w_d, T, tm):
    def body(i, x1_ref, g_ref, wg_ref, wu_ref, wd_ref, x2_ref, gt_ref, up_ref, h2_ref):
        x1 = x1_ref[...]
        h2 = (_rms(x1)[0] * g_ref[...]).astype(MM)
        gt = _dot_nt(h2, wg_ref[...])
        up = _dot_nt(h2, wu_ref[...])
        a = _silu_parts(gt)[0] * up
        x2_ref[...] = x1 + _dot(a, wd_ref[...])
        gt_ref[...] = gt
        up_ref[...] = up
        h2_ref[...] = h2

    return _row_call("ffn_fwd", body, T, tm, [x1], [g_ffn, w_g, w_u, w_d],
                     [(D_MODEL, F32), (FFN, F32), (FFN, F32), (D_MODEL, MM)], [], VMEM_LIMIT)


def _ple_loss(x2, p, tgt, g_pg, g_post, w_pg, w_pp, T, tm):
    def body(i, x2_ref, p_ref, t_ref, gpg_ref, gpo_ref, wpg_ref, wpp_ref,
             dx2_ref, loss_ref, dgpo_ref, dgpg_ref, dwpg_ref, dwpp_ref):
        x2 = x2_ref[...]
        p_mm = p_ref[...].astype(MM)
        u = jnp.concatenate([jnp.dot(p_mm, wpp_ref[d], preferred_element_type=F32) for d in range(N_DEV)], axis=1)
        uh, ru = _rms(u)
        e = uh * gpo_ref[...]
        x2h, r3 = _rms(x2)
        n3 = x2h * gpg_ref[...]
        gate = _sigmoid(_dot(n3, wpg_ref[...]))
        diff = x2 + gate * e - t_ref[...]
        dy = diff * (1.0 / D_MODEL)
        de = dy * gate
        dz = dy * e * gate * (1.0 - gate)
        du = _rms_bwd(de * gpo_ref[...], uh, ru)
        dn3 = _dot_nt(dz, wpg_ref[...])
        dx2_ref[...] = dy + _rms_bwd(dn3 * gpg_ref[...], x2h, r3)
        _acc(loss_ref, i, jnp.sum(diff * diff, axis=0, keepdims=True) * (0.5 / D_MODEL))
        _acc(dgpo_ref, i, jnp.sum(de * uh, axis=0, keepdims=True))
        _acc(dgpg_ref, i, jnp.sum(dn3 * x2h, axis=0, keepdims=True))
        _acc(dwpg_ref, i, _dot_tn(n3, dz))
        du_mm = du.astype(MM)
        for d in range(N_DEV):
            _acc(dwpp_ref.at[d], i, _dot_tn(p_mm, du_mm[:, d * HEAD_PAD:(d + 1) * HEAD_PAD]))

    vec = ((1, D_MODEL), F32)
    return _row_call("ple_loss", body, T, tm, [x2, p, tgt], [g_pg, g_post, w_pg, w_pp], [(D_MODEL, F32)],
                     [vec, vec, vec, ((D_MODEL, D_MODEL), F32), ((N_DEV, PLE, HEAD_PAD), F32)], VMEM_LIMIT)


def _ffn_bwd(dx2, x1, gt, up, g_ffn, w_g, w_u, w_d, T, tm):
    def body(i, dx2_ref, x1_ref, gt_ref, up_ref, g_ref, wg_ref, wu_ref, wd_ref,
             dx1_ref, a_ref, dgt_ref, dup_ref, dg_ref):
        dx2 = dx2_ref[...]
        x1h, r = _rms(x1_ref[...])
        up = up_ref[...]
        silu, dsilu = _silu_parts(gt_ref[...])
        da = _dot_nt(dx2, wd_ref[...])
        dgt = (da * up * dsilu).astype(MM)
        dup = (da * silu).astype(MM)
        dh2 = (jnp.dot(dgt, wg_ref[...], preferred_element_type=F32)
               + jnp.dot(dup, wu_ref[...], preferred_element_type=F32))
        dx1_ref[...] = dx2 + _rms_bwd(dh2 * g_ref[...], x1h, r)
        a_ref[...] = (silu * up).astype(MM)
        dgt_ref[...] = dgt
        dup_ref[...] = dup
        _acc(dg_ref, i, jnp.sum(dh2 * x1h, axis=0, keepdims=True))

    return _row_call("ffn_bwd", body, T, tm, [dx2, x1, gt, up], [g_ffn, w_g, w_u, w_d],
                     [(D_MODEL, F32), (FFN, MM), (FFN, MM), (FFN, MM)], [((1, D_MODEL), F32)], VMEM_LIMIT)


def _merge_bwd(dx1, ya, yb, bg, o, hg, attn, m, rec, g_out, w_bra, w_brb, w_out, T, tm):
    def body(i, dx1_ref, ya_ref, yb_ref, bg_ref, o_ref, hg_ref, attn_ref, m_ref, rec_ref, g_ref, wa_ref, wb_ref, wo_ref,
             dattn_ref, do_ref, dhg_ref, dbg_ref, dg_ref, dwo_ref, dwa_ref, dwb_ref):
        dx1 = dx1_ref[...].astype(MM)
        dm = _dot_nt(dx1, wo_ref[...])
        ga, gb = _sigmoid(bg_ref[:, :D_MODEL]), _sigmoid(bg_ref[:, D_MODEL:])
        dya, dyb = (dm * ga).astype(MM), (dm * gb).astype(MM)
        dbg_ref[:, :D_MODEL] = dm * ya_ref[...] * ga * (1.0 - ga)
        dbg_ref[:, D_MODEL:] = dm * yb_ref[...] * gb * (1.0 - gb)

        @pl.when(i == 0)
        def _():
            dwo_ref[...] = jnp.zeros_like(dwo_ref)
            dwa_ref[...] = jnp.zeros_like(dwa_ref)
            dwb_ref[...] = jnp.zeros_like(dwb_ref)

        dwo_ref[...] += _dot_tn(m_ref[...], dx1)
        attn_mm, rec_mm = attn_ref[...].astype(MM), rec_ref[...]
        for d in range(N_DEV):
            ds = slice(d * HEAD_PAD, (d + 1) * HEAD_PAD)
            dwa_ref[d] += _dot_tn(attn_mm, dya[:, ds])
            dwb_ref[d] += _dot_tn(rec_mm, dyb[:, ds])
        dattn_ref[...] = lax.dot_general(dya, wa_ref[...], (((1,), (1,)), ((), ())), preferred_element_type=F32)
        drec = lax.dot_general(dyb, wb_ref[...], (((1,), (1,)), ((), ())), preferred_element_type=F32)
        dg = jnp.zeros((1, HG_DIM), F32)
        for h in range(HG_HEADS):
            ls = slice(h * HG_DIM, (h + 1) * HG_DIM)
            oh, r = _rms(o_ref[:, ls])
            silu, dsilu = _silu_parts(hg_ref[:, ls])
            dr = drec[:, ls]
            dhg_ref[:, ls] = dr * oh * g_ref[...] * dsilu
            don = dr * silu
            dg = dg + jnp.sum(don * oh, axis=0, keepdims=True)
            do_ref[:, ls] = _rms_bwd(don * g_ref[...], oh, r)
        _acc(dg_ref, i, dg)

    return _row_call("merge_bwd", body, T, tm, [dx1, ya, yb, bg, o, hg, attn, m, rec], [g_out, w_bra, w_brb, w_out],
                     [(D_MODEL, F32), (HG_W, F32), (HG_W, F32), (2 * D_MODEL, F32)],
                     [((1, HG_DIM), F32), ((D_MODEL, D_MODEL), F32), ((N_DEV, MLA_HEADS * HEAD_PAD, HEAD_PAD), F32),
                      ((N_DEV, HG_W, HEAD_PAD), F32)], VMEM_LIMIT)


def _flash_bwd(qf, kf, vf, o, do, lse, T, xchg=()):
    tq = min(ATT_TILE, T)
    nq = T // tq

    qi_tab, ki_tab = _causal_pairs(nq, by_query=False)

    n_x = len(xchg)
    hp = ATT_HEADS
    n_heads, n_pairs = MLA_HEADS // hp, len(qi_tab)

    def body(qi_ref, ki_ref, q_ref, k_ref, v_ref, o_ref, do_ref, lse_ref, *rest):
        x_in, (dq_ref, dk_ref, dv_ref), rest = rest[:n_x], rest[n_x:n_x + 3], rest[n_x + 3:]
        x_out, x_sems = rest[:n_x], rest[n_x:]
        t = pl.program_id(1)
        qi, ki = qi_ref[t], ki_ref[t]
        if n_x:
            @pl.when((pl.program_id(0) == 0) & (t == 0))
            def _():
                for cp in _xchips_copies(x_in, x_out, x_sems):
                    cp.start()

        @pl.when(t == 0)
        def _():
            dq_ref[...] = jnp.zeros_like(dq_ref)

        def step(first):
            rows = pl.ds(pl.multiple_of(qi * tq, tq), tq)
            for hh in range(hp):
                hs = slice(hh * HEAD_PAD, (hh + 1) * HEAD_PAD)
                q, k, d_o = q_ref[:, hs], k_ref[:, hs], do_ref[:, hs]
                s = _dot_nt(q, k)
                if first:
                    row = lax.broadcasted_iota(jnp.int32, (tq, tq), 0)
                    col = lax.broadcasted_iota(jnp.int32, (tq, tq), 1)
                    s = jnp.where(col <= row, s, NEG)
                p = jnp.exp(s - lse_ref[:, hh * HEAD_PAD:hh * HEAD_PAD + 1])
                delta = jnp.sum(d_o * o_ref[:, hs], axis=1, keepdims=True)
                ds = p * (_dot_nt(d_o, v_ref[:, hs]) - delta)
                dq_ref[rows, hs] += _dot(ds, k)
                if first:
                    dv_ref[:, hs] = _dot_tn(p, d_o)
                    dk_ref[:, hs] = _dot_tn(ds, q)
                else:
                    dv_ref[:, hs] += _dot_tn(p, d_o)
                    dk_ref[:, hs] += _dot_tn(ds, q)

        @pl.when(qi == ki)
        def _():
            step(True)

        @pl.when(qi > ki)
        def _():
            step(False)

        if n_x:
            @pl.when((pl.program_id(0) == n_heads - 1) & (t == n_pairs - 1))
            def _():
                for cp in _xchips_copies(x_in, x_out, x_sems):
                    cp.wait()

    q_spec = pl.BlockSpec((tq, hp * HEAD_PAD), lambda h, t, qi_ref, ki_ref: (qi_ref[t], h))
    kv_spec = pl.BlockSpec((tq, hp * HEAD_PAD), lambda h, t, qi_ref, ki_ref: (ki_ref[t], h))
    any_spec = pl.BlockSpec(memory_space=pl.ANY)
    w = MLA_HEADS * HEAD_PAD
    grid_spec = pltpu.PrefetchScalarGridSpec(
        num_scalar_prefetch=2, grid=(n_heads, n_pairs),
        in_specs=[q_spec, kv_spec, kv_spec, q_spec, q_spec, q_spec] + [any_spec] * n_x,
        out_specs=[pl.BlockSpec((T, hp * HEAD_PAD), lambda h, t, qi_ref, ki_ref: (0, h)), kv_spec, kv_spec]
        + [any_spec] * n_x,
        scratch_shapes=_xchips_sems(n_x) if n_x else [])
    return pl.pallas_call(
        body, name="flash_bwd", grid_spec=grid_spec,
        out_shape=[jax.ShapeDtypeStruct((T, w), F32)] * 3 + _xchips_out_shapes(xchg),
        compiler_params=_cparams(("arbitrary", "arbitrary")),
    )(jnp.asarray(qi_tab), jnp.asarray(ki_tab), qf, kf, vf, o, do, lse, *xchg)


def _mla_heads_bwd(d_out, saved, g_pad, cos_t, sin_t, first):
    d_raw, dg = [], jnp.zeros((1, HEAD_PAD), F32)
    for h in range(MLA_HEADS):
        xh, r = saved[h]
        dy = d_out[:, h * HEAD_PAD:(h + 1) * HEAD_PAD]
        dn = dy * cos_t + _rope_swap(dy * sin_t, first)
        dg = dg + jnp.sum(dn * xh, axis=0, keepdims=True)
        d_raw.append(_rms_bwd(dn * g_pad, xh, r, QK_DIM))
    return d_raw, dg


def _mla_prep_bwd(cq, ckv, kr, pos, dqf, dkf, dvf, g_qa, g_kva, g_qn, g_kn, w_uq, w_ukv, T, tm):
    def body(i, cq_ref, ckv_ref, kr_ref, pos_ref, dq_ref, dk_ref, dv_ref,
             gqa_ref, gkva_ref, gqn_ref, gkn_ref, wuq_ref, wukv_ref,
             dcq_ref, dckv_ref, dkr_ref, dgqa_ref, dgkva_ref, dgqn_ref, dgkn_ref, dwuq_ref, dwukv_ref):
        cos_t, sin_t, first = _rope_tables(pos_ref[...], tm)
        cqh, rq = _rms(cq_ref[...])
        ckvh, rkv = _rms(ckv_ref[...])
        cqn, ckvn = cqh * gqa_ref[...], ckvh * gkva_ref[...]
        q_raw, k_raw, _ = _mla_raw_heads(cqn, ckvn, kr_ref[...], wuq_ref, wukv_ref, tm)
        _, q_saved = _mla_heads_fwd(q_raw, gqn_ref[...], cos_t, sin_t, first)
        _, k_saved = _mla_heads_fwd(k_raw, gkn_ref[...], cos_t, sin_t, first)
        dq_heads, dgqn = _mla_heads_bwd(dq_ref[...] * ATT_SCALE, q_saved, gqn_ref[...], cos_t, sin_t, first)
        dk_heads, dgkn = _mla_heads_bwd(dk_ref[...], k_saved, gkn_ref[...], cos_t, sin_t, first)
        lane = lax.broadcasted_iota(jnp.int32, (tm, HEAD_PAD), 1)
        nope = lane < QK_NOPE
        dcqn = jnp.zeros((tm, Q_RANK), F32)
        dckvn = jnp.zeros((tm, KV_RANK), F32)
        dkr = jnp.zeros((tm, HEAD_PAD), F32)
        cqn_mm, ckvn_mm = cqn.astype(MM), ckvn.astype(MM)
        for h in range(MLA_HEADS):
            hs = slice(h * HEAD_PAD, (h + 1) * HEAD_PAD)
            dq_h = dq_heads[h].astype(MM)
            dkv_h = jnp.where(nope, dk_heads[h], pltpu.roll(dv_ref[:, hs], V_DIM, 1)).astype(MM)
            _acc(dwuq_ref.at[h], i, _dot_tn(dq_h, cqn_mm))
            _acc(dwukv_ref.at[h], i, _dot_tn(ckvn_mm, dkv_h))
            dcqn = dcqn + jnp.dot(dq_h, wuq_ref[h], preferred_element_type=F32)
            dckvn = dckvn + lax.dot_general(dkv_h, wukv_ref[h], (((1,), (1,)), ((), ())), preferred_element_type=F32)
            dkr = dkr + dk_heads[h]
        dkr_ref[...] = jnp.where((lane >= QK_NOPE) & (lane < QK_DIM), dkr, 0.0)
        dcq_ref[...] = _rms_bwd(dcqn * gqa_ref[...], cqh, rq)
        dckv_ref[...] = _rms_bwd(dckvn * gkva_ref[...], ckvh, rkv)
        _acc(dgqa_ref, i, jnp.sum(dcqn * cqh, axis=0, keepdims=True))
        _acc(dgkva_ref, i, jnp.sum(dckvn * ckvh, axis=0, keepdims=True))
        _acc(dgqn_ref, i, dgqn)
        _acc(dgkn_ref, i, dgkn)

    return _row_call(
        "mla_prep_bwd", body, T, tm, [cq, ckv, kr, pos, dqf, dkf, dvf], [g_qa, g_kva, g_qn, g_kn, w_uq, w_ukv],
        [(Q_RANK, F32), (KV_RANK, F32), (HEAD_PAD, F32)],
        [((1, Q_RANK), F32), ((1, KV_RANK), F32), ((1, HEAD_PAD), F32), ((1, HEAD_PAD), F32),
         ((MLA_HEADS, HEAD_PAD, Q_RANK), F32), ((MLA_HEADS, KV_RANK, HEAD_PAD), F32)], VMEM_LIMIT)


def _in_proj_bwd(x, dx1, dsecs, g_mix, w_in, T, tm):
    def body(i, x_ref, dx1_ref, *rest):
        d_refs, (g_ref, w_ref, dx_ref, dp_ref, dg_ref) = rest[:len(SECTIONS)], rest[len(SECTIONS):]
        pieces = [d_ref[:, QK_NOPE:QK_DIM] if n == QK_ROPE else d_ref[...] for (_, n), d_ref in zip(COL_SECTIONS, d_refs)]
        dproj = jnp.concatenate(pieces, axis=1)
        dh = jnp.zeros((tm, D_MODEL), F32)
        for d in range(N_DEV):
            blk = dproj[:, d * IN_BLOCK:(d + 1) * IN_BLOCK].astype(MM)
            dp_ref[d] = blk
            dh = dh + jnp.dot(blk, w_ref[d], preferred_element_type=F32)
        xh, r = _rms(x_ref[...])
        dx_ref[...] = dx1_ref[...] + _rms_bwd(dh * g_ref[...], xh, r)
        _acc(dg_ref, i, jnp.sum(dh * xh, axis=0, keepdims=True))

    in_specs = [pl.BlockSpec((tm, a.shape[1]), lambda i: (i, 0)) for a in [x, dx1, *dsecs]]
    in_specs += [pl.BlockSpec(g_mix.shape, lambda i: (0, 0)),
                 pl.BlockSpec(w_in.shape, lambda i: (0, 0, 0), pipeline_mode=pl.Buffered(1))]

    def kern(*refs):
        body(pl.program_id(0), *refs)

    return pl.pallas_call(
        kern, name="in_proj_bwd", grid=(T // tm,), in_specs=in_specs,
        out_specs=[pl.BlockSpec((tm, D_MODEL), lambda i: (i, 0)),
                   pl.BlockSpec((N_DEV, tm, IN_BLOCK), lambda i: (0, i, 0)),
                   pl.BlockSpec((1, D_MODEL), lambda i: (0, 0))],
        out_shape=[jax.ShapeDtypeStruct((T, D_MODEL), F32), jax.ShapeDtypeStruct((N_DEV, T, IN_BLOCK), MM),
                   jax.ShapeDtypeStruct((1, D_MODEL), F32)],
        compiler_params=_cparams(("arbitrary",), VMEM_LIMIT),
    )(x, dx1, *dsecs, g_mix, w_in)


def _pick_block(n, cap):
    best = None
    for cand in range(128, min(n, cap) + 1, 128):
        if n % cand == 0:
            best = cand
    return n if best is None else best


def _pick_rows(n, cap):
    best = n
    for cand in range(8, min(n, cap) + 1, 8):
        if n % cand == 0:
            best = cand
    return best


def _matmul_tn(name, a, b):
    T, M = a.shape
    N = b.shape[1]
    bm, bk = _pick_block(M, 1408), min(512, T)
    bn = _pick_block(N, 2560)

    def body(a_ref, b_ref, c_ref):
        @pl.when(pl.program_id(2) == 0)
        def _():
            c_ref[...] = jnp.zeros_like(c_ref)

        c_ref[...] += _dot_tn(a_ref[...], b_ref[...])

    return pl.pallas_call(
        body, name=name, grid=(M // bm, N // bn, T // bk),
        in_specs=[pl.BlockSpec((bk, bm), lambda i, j, k: (k, i)), pl.BlockSpec((bk, bn), lambda i, j, k: (k, j))],
        out_specs=pl.BlockSpec((bm, bn), lambda i, j, k: (i, j)), out_shape=jax.ShapeDtypeStruct((M, N), F32),
        compiler_params=_cparams(("parallel", "parallel", "arbitrary"), VMEM_LIMIT),
    )(a, b)


def _matmul_tn_blocks(name, a, b):
    T, M = a.shape
    nd, _, c = b.shape
    bm, bk = _pick_block(M, 512), min(512, T)

    def body(a_ref, b_ref, c_ref):
        @pl.when(pl.program_id(1) == 0)
        def _():
            c_ref[...] = jnp.zeros_like(c_ref)

        a_blk = a_ref[...].astype(MM)
        for d in range(nd):
            c_ref[d] += _dot_tn(b_ref[d], a_blk)

    return pl.pallas_call(
        body, name=name, grid=(M // bm, T // bk),
        in_specs=[pl.BlockSpec((bk, bm), lambda i, k: (k, i)), pl.BlockSpec((nd, bk, c), lambda i, k: (0, k, 0))],
        out_specs=pl.BlockSpec((nd, c, bm), lambda i, k: (0, 0, i)),
        out_shape=jax.ShapeDtypeStruct((nd, c, M), F32),
        compiler_params=_cparams(("parallel", "arbitrary"), VMEM_LIMIT),
    )(a, b)


def _pad_gain(g, n):
    return jnp.pad(g.reshape(1, -1), ((0, 0), (0, n - g.shape[-1])))


GROUP_A = ("w_ffn_gate", "w_ffn_up", "w_ffn_down", "w_ple_gate", "w_ple_proj")
GROUP_B = ("w_branch", "w_out")
GROUP_C = ("w_in", "w_uq", "w_ukv")
EARLY = GROUP_C
LATE = GROUP_B + GROUP_A
TRANSPOSED = ("w_in", "w_uq", "w_ffn_gate", "w_ffn_up")


def _local_step(x, p, pos, tgt, small, big, late_blocks=None, core=None):
    T = x.shape[0]
    tm = min(ROW_TILE, T)
    w_in = big["w_in"]
    w_uq = jnp.pad(big["w_uq"], ((0, 0), (0, HEAD_PAD - QK_DIM), (0, 0)))
    w_ukv = big["w_ukv"]

    g_mix, g_qa, g_kva = small["mix_norm_g"], small["q_a_norm_g"], small["kv_a_norm_g"]
    g_qn, g_kn = _pad_gain(small["q_norm_g"], HEAD_PAD), _pad_gain(small["k_norm_g"], HEAD_PAD)
    g_out, g_ffn = small["hg_out_norm_g"], small["ffn_norm_g"]
    g_pg, g_post = small["ple_gate_norm_g"], small["ple_post_norm_g"]
    logits = small["hg_lb_logits"]
    lb = _lower_bound(logits)

    h, cq, ckv, kr, hq, hf, hi, hg, bg = _in_proj_fwd(x, g_mix, w_in, T, tm)
    qf, kf, vf = _mla_prep_fwd(cq, ckv, kr, pos, g_qa, g_kva, g_qn, g_kn, w_uq, w_ukv, T, tm)
    if late_blocks is None:
        attn, lse = _flash_fwd(qf, kf, vf, T)
    else:
        attn, lse, *late = _flash_fwd(qf, kf, vf, T, ag_blocks=[late_blocks[n] for n in LATE])
        big = {**big, **dict(zip(LATE, late))}
    w_branch = jnp.moveaxis(big["w_branch"].reshape(N_DEV, 2, HG_W, HEAD_PAD), 0, 2).reshape(2, HG_W, D_MODEL)
    w_bra = jnp.pad(w_branch[0].reshape(MLA_HEADS, V_DIM, D_MODEL),
                    ((0, 0), (0, HEAD_PAD - V_DIM), (0, 0))).reshape(MLA_HEADS * HEAD_PAD, D_MODEL)
    w_brb = w_branch[1]
    w_out = big["w_out"].reshape(D_MODEL, D_MODEL)
    w_g, w_u = big["w_ffn_gate"].reshape(FFN, D_MODEL), big["w_ffn_up"].reshape(FFN, D_MODEL)
    w_d = big["w_ffn_down"].reshape(FFN, D_MODEL)
    w_pg, w_pp = big["w_ple_gate"].reshape(D_MODEL, D_MODEL), big["w_ple_proj"]
    o, s0 = _hgrn_fwd(hq, hf, hi, lb, T)
    x1, ya, yb, m, rec = _merge_fwd(attn, o, hg, bg, x, g_out, w_bra, w_brb, w_out, T, tm)
    x2, gt, up, h2 = _ffn_fwd(x1, g_ffn, w_g, w_u, w_d, T, tm)
    dx2, loss_p, dg_post, dg_pg, d_pg, d_pp = _ple_loss(x2, p, tgt, g_pg, g_post, w_pg, w_pp, T, tm)

    grads, sibs, gots = {}, {}, {}

    def reduce_start(tag, names):
        if core is None:
            return ()
        got = _exchange_sibling("rs_sibling_" + tag, [grads[n] for n in names])
        sibs.update(zip(names, got))
        return [_chip_partial("rs_partial_" + n, grads[n], sibs[n], core) for n in names]

    dx1, a, dgt, dup, dg_ffn = _ffn_bwd(dx2, x1, gt, up, g_ffn, w_g, w_u, w_d, T, tm)
    grads["w_ffn_gate"] = _matmul_tn("dw_gate", dgt, h2).reshape(N_DEV, -1, D_MODEL)
    grads["w_ffn_up"] = _matmul_tn("dw_up", dup, h2).reshape(N_DEV, -1, D_MODEL)
    grads["w_ffn_down"] = _matmul_tn("dw_down", a, dx2).reshape(N_DEV, -1, D_MODEL)
    grads["w_ple_gate"] = d_pg.reshape(N_DEV, -1, D_MODEL)
    grads["w_ple_proj"] = d_pp
    parts_a = reduce_start("a", GROUP_A)

    dattn, do, dhg, dbg, dg_out, d_out, d_bra, d_brb = _merge_bwd(
        dx1, ya, yb, bg, o, hg, attn, m, rec, g_out, w_bra, w_brb, w_out, T, tm)
    d_bra = d_bra.reshape(N_DEV, MLA_HEADS, HEAD_PAD, HEAD_PAD)[:, :, :V_DIM].reshape(N_DEV, HG_W, HEAD_PAD)
    grads["w_branch"] = jnp.concatenate([d_bra, d_brb], axis=1)
    grads["w_out"] = d_out.reshape(N_DEV, -1, D_MODEL)
    parts_b = reduce_start("b", GROUP_B)

    dhq, dhf, dhi, dlb, *got_a = _hgrn_bwd(hq, hf, hi, do, s0, lb, T, xchg=parts_a)
    dqf, dkf, dvf, *got_b = _flash_bwd(qf, kf, vf, attn, dattn, lse, T, xchg=parts_b)
    (dcq, dckv, dkr, dg_qa, dg_kva, dg_qn, dg_kn, d_uq, d_ukv) = _mla_prep_bwd(
        cq, ckv, kr, pos, dqf, dkf, dvf, g_qa, g_kva, g_qn, g_kn, w_uq, w_ukv, T, tm)
    grad_x, dproj, dg_mix = _in_proj_bwd(x, dx1, [dcq, dckv, dkr, dhq, dhf, dhi, dhg, dbg], g_mix, w_in, T, tm)
    grads["w_in"] = _matmul_tn_blocks("dw_in", h, dproj)
    grads["w_uq"] = d_uq[:, :QK_DIM]
    grads["w_ukv"] = d_ukv
    parts_c = reduce_start("c", GROUP_C)
    if core is not None:
        gots.update(zip(GROUP_A, got_a))
        gots.update(zip(GROUP_B, got_b))
        gots.update(zip(GROUP_C, _exchange_chips(parts_c)))

    dl0 = dlb * lb * (1.0 - lb)
    small_g = {
        "mix_norm_g": dg_mix, "q_a_norm_g": dg_qa, "kv_a_norm_g": dg_kva,
        "q_norm_g": dg_qn[:, :QK_DIM], "k_norm_g": dg_kn[:, :QK_DIM],
        "hg_lb_logits": jnp.concatenate([dl0, -dl0], axis=0), "hg_out_norm_g": dg_out,
        "ffn_norm_g": dg_ffn, "ple_gate_norm_g": dg_pg, "ple_post_norm_g": dg_post,
    }
    return loss_p, grad_x, small_g, grads, sibs, gots


def _lower_bound(logits):
    def body(l_ref, lb_ref):
        l = l_ref[...]
        mx = jnp.max(l, axis=0, keepdims=True)
        e = jnp.exp(l - mx)
        lb_ref[...] = e[0:1] / jnp.sum(e, axis=0, keepdims=True)

    return pl.pallas_call(body, name="lower_bound", out_shape=jax.ShapeDtypeStruct((1, HG_W), F32))(logits)


def _my_place():
    return lax.axis_index("x"), lax.axis_index("y"), lax.axis_index("c")


def _all_gather(name, blocks):
    n = len(blocks)

    def body(*refs):
        x_refs, out_refs, sems = refs[:n], refs[n:2 * n], refs[2 * n:]
        _ag_start(x_refs, out_refs, sems)
        _ag_finish(x_refs, out_refs, sems)

    any_spec = pl.BlockSpec(memory_space=pl.ANY)
    return pl.pallas_call(
        body, name=name, out_shape=_ag_out_shapes(blocks),
        in_specs=[any_spec] * n, out_specs=[any_spec] * n, scratch_shapes=_ag_sems(n),
    )(*blocks)


def _ag_out_shapes(blocks):
    return [jax.ShapeDtypeStruct((N_DEV,) + b.shape, b.dtype) for b in blocks]


def _ag_sems(n):
    return [pltpu.SemaphoreType.DMA((7 * n,)), pltpu.SemaphoreType.DMA((7 * n,)), pltpu.SemaphoreType.DMA((n,))]


def _ag_parts(x_refs, out_refs, sems):
    send_sems, recv_sems, local_sems = sems
    x, y, c = _my_place()
    me, sibling = (x, y, c), (x, y, 1 - c)
    chips = [(1 - x, y), (x, 1 - y), (1 - x, 1 - y)]
    n = len(x_refs)

    def copy(a, k, block, to, own=False):
        px, py, pc = block
        dst = out_refs[a].at[4 * px + 2 * py + pc]
        return pltpu.make_async_remote_copy(
            src_ref=x_refs[a] if own else dst, dst_ref=dst, send_sem=send_sems.at[7 * a + k],
            recv_sem=recv_sems.at[7 * a + k], device_id=to, device_id_type=MESH_ID)

    mine = [pltpu.make_async_copy(x_refs[a], out_refs[a].at[4 * x + 2 * y + c], local_sems.at[a]) for a in range(n)]
    first = []
    for a in range(n):
        first.append(copy(a, 0, me, sibling, own=True))
        first += [copy(a, 1 + j, me, (*chip, c), own=True) for j, chip in enumerate(chips)]
    return copy, mine, first, me, sibling, chips, c, n


def _ag_start(x_refs, out_refs, sems):
    _, mine, first, *_ = _ag_parts(x_refs, out_refs, sems)
    for cp in mine + first:
        cp.start()


def _ag_finish(x_refs, out_refs, sems):
    copy, mine, first, me, sibling, chips, c, n = _ag_parts(x_refs, out_refs, sems)
    passed = []
    for j, chip in enumerate(chips):
        for a in range(n):
            copy(a, 1 + j, (*chip, c), me).wait_recv()
            passed.append(copy(a, 4 + j, (*chip, c), sibling))
            passed[-1].start()
    for a in range(n):
        copy(a, 0, sibling, me).wait_recv()
    for j, chip in enumerate(chips):
        for a in range(n):
            copy(a, 4 + j, (*chip, 1 - c), me).wait_recv()
    for cp in first + passed:
        cp.wait_send()
    for cp in mine:
        cp.wait()


def _exchange_sibling(name, gs):
    n = len(gs)

    def body(*refs):
        g_refs, out_refs, (send_sems, recv_sems) = refs[:n], refs[n:2 * n], refs[2 * n:]
        x, y, c = _my_place()
        copies = [pltpu.make_async_remote_copy(
            src_ref=g_refs[a].at[2 * j + 1 - c], dst_ref=out_refs[a].at[j], send_sem=send_sems.at[4 * a + j],
            recv_sem=recv_sems.at[4 * a + j], device_id=(x, y, 1 - c), device_id_type=MESH_ID)
            for a in range(n) for j in range(4)]
        for cp in copies:
            cp.start()
        for cp in copies:
            cp.wait()

    any_spec = pl.BlockSpec(memory_space=pl.ANY)
    return pl.pallas_call(
        body, name=name, out_shape=[jax.ShapeDtypeStruct((4,) + g.shape[1:], g.dtype) for g in gs],
        in_specs=[any_spec] * n, out_specs=[any_spec] * n,
        scratch_shapes=[pltpu.SemaphoreType.DMA((4 * n,)), pltpu.SemaphoreType.DMA((4 * n,))],
    )(*gs)


def _chip_partial(name, g, got, c_idx):
    _, rows, cols = g.shape
    tr, tc = _tile_2d(rows, cols, 512)

    def body(c_ref, g_ref, got_ref, out_ref):
        out_ref[...] = (g_ref[...] + got_ref[...]).astype(MM)

    grid_spec = pltpu.PrefetchScalarGridSpec(
        num_scalar_prefetch=1, grid=(4, rows // tr, cols // tc),
        in_specs=[pl.BlockSpec((1, tr, tc), lambda j, i, k, c_ref: (2 * j + c_ref[0], i, k)),
                  pl.BlockSpec((1, tr, tc), lambda j, i, k, c_ref: (j, i, k))],
        out_specs=pl.BlockSpec((1, tr, tc), lambda j, i, k, c_ref: (j, i, k)))
    return pl.pallas_call(
        body, name=name, grid_spec=grid_spec, out_shape=jax.ShapeDtypeStruct((4, rows, cols), MM),
        compiler_params=_cparams(("parallel", "parallel", "parallel")),
    )(c_idx, g, got)


def _tile_2d(rows, cols, row_cap):
    if rows % 8 == 0:
        return _pick_rows(rows, row_cap), cols
    return rows, 256 if cols % 256 == 0 else cols


def _exchange_chips(parts):
    n = len(parts)

    def body(*refs):
        p_refs, out_refs, sems = refs[:n], refs[n:2 * n], refs[2 * n:]
        for cp in _xchips_copies(p_refs, out_refs, sems):
            cp.start()
        for cp in _xchips_copies(p_refs, out_refs, sems):
            cp.wait()

    any_spec = pl.BlockSpec(memory_space=pl.ANY)
    return pl.pallas_call(
        body, name="rs_chips", out_shape=_xchips_out_shapes(parts),
        in_specs=[any_spec] * n, out_specs=[any_spec] * n, scratch_shapes=_xchips_sems(n),
    )(*parts)


def _xchips_out_shapes(parts):
    return [jax.ShapeDtypeStruct((3,) + p.shape[1:], p.dtype) for p in parts]


def _xchips_sems(n):
    return [pltpu.SemaphoreType.DMA((3 * n,)), pltpu.SemaphoreType.DMA((3 * n,))]


def _xchips_copies(p_refs, out_refs, sems):
    send_sems, recv_sems = sems
    x, y, c = _my_place()
    chips = [(1 - x, y), (x, 1 - y), (1 - x, 1 - y)]
    return [pltpu.make_async_remote_copy(
        src_ref=p_refs[a].at[2 * px + py], dst_ref=out_refs[a].at[k], send_sem=send_sems.at[3 * a + k],
        recv_sem=recv_sems.at[3 * a + k], device_id=(px, py, c), device_id_type=MESH_ID)
        for a in range(len(p_refs)) for k, (px, py) in enumerate(chips)]


def _adamw_math(w, g, m, v):
    m = ADAM_B1 * m + (1.0 - ADAM_B1) * g
    v = ADAM_B2 * v + (1.0 - ADAM_B2) * jnp.square(g)
    m_hat = m / (1.0 - ADAM_B1 ** ADAM_STEP)
    v_hat = v / (1.0 - ADAM_B2 ** ADAM_STEP)
    delta = -ADAM_LR * (m_hat / (jnp.sqrt(v_hat) + ADAM_EPS) + ADAM_WD * w)
    return delta, m, v


def _sum_adamw(name, g, sib, got, w, m, v, slot_idx, chip_idx):
    _, rows, cols = g.shape
    tr, tc = _tile_2d(rows, cols, 256)

    def body(s_ref, j_ref, g_ref, sib_ref, got_ref, w_ref, m_ref, v_ref, go_ref, d_ref, m2_ref, v2_ref):
        grad = g_ref[0] + sib_ref[0]
        for k in range(3):
            grad = grad + got_ref[k].astype(F32)
        go_ref[...] = grad
        d_ref[...], m2_ref[...], v2_ref[...] = _adamw_math(w_ref[...], grad, m_ref[...], v_ref[...])

    flat = pl.BlockSpec((tr, tc), lambda i, k, s_ref, j_ref: (i, k))
    grid_spec = pltpu.PrefetchScalarGridSpec(
        num_scalar_prefetch=2, grid=(rows // tr, cols // tc),
        in_specs=[pl.BlockSpec((1, tr, tc), lambda i, k, s_ref, j_ref: (s_ref[0], i, k)),
                  pl.BlockSpec((1, tr, tc), lambda i, k, s_ref, j_ref: (j_ref[0], i, k)),
                  pl.BlockSpec((3, tr, tc), lambda i, k, s_ref, j_ref: (0, i, k)), flat, flat, flat],
        out_specs=[flat] * 4)
    return pl.pallas_call(
        body, name=name, grid_spec=grid_spec, out_shape=[jax.ShapeDtypeStruct((rows, cols), F32)] * 4,
        compiler_params=_cparams(("parallel", "parallel")),
    )(slot_idx, chip_idx, g, sib, got, w, m, v)


def _adamw_small(parts, w, m, v):
    rows = w.shape[0]

    def body(p_ref, w_ref, m_ref, v_ref, g_ref, d_ref, m2_ref, v2_ref):
        g = p_ref[0]
        for d in range(1, N_DEV):
            g = g + p_ref[d]
        g_ref[...] = g
        d_ref[...], m2_ref[...], v2_ref[...] = _adamw_math(w_ref[...], g, m_ref[...], v_ref[...])

    return pl.pallas_call(
        body, name="adamw_small", out_shape=[jax.ShapeDtypeStruct((rows, 128), F32)] * 4,
    )(parts, w, m, v)


BIG = ("w_in", "w_uq", "w_ukv", "w_branch", "w_out", "w_ffn_gate", "w_ffn_up", "w_ffn_down", "w_ple_gate", "w_ple_proj")
SMALL = (
    ("mix_norm_g", 1024), ("q_a_norm_g", 384), ("kv_a_norm_g", 256), ("q_norm_g", 96), ("k_norm_g", 96),
    ("hg_lb_logits", 1024), ("hg_out_norm_g", 128), ("ffn_norm_g", 1024), ("ple_gate_norm_g", 1024),
    ("ple_post_norm_g", 1024),
)
SMALL_ROWS = 56


def _pack_small(vals):
    rows = []
    for name, n in SMALL:
        v = vals[name].reshape(1, -1).astype(F32)
        rows.append(jnp.pad(v, ((0, 0), (0, (-n) % 128))).reshape(-1, 128))
    return jnp.concatenate(rows, axis=0)


def _unpack_small(packed, shapes):
    out, r = {}, 0
    for name, n in SMALL:
        k = (n + 127) // 128
        out[name] = packed[r:r + k].reshape(1, -1)[:, :n].reshape(shapes[name])
        r += k
    return out


_WEIGHTS = ["mix_norm_g", "w_in", "q_a_norm_g", "w_uq", "kv_a_norm_g", "w_ukv", "q_norm_g", "k_norm_g", "hg_lb_logits",
            "hg_out_norm_g", "w_branch", "w_out", "ffn_norm_g", "w_ffn_gate", "w_ffn_up", "w_ffn_down",
            "ple_gate_norm_g", "w_ple_gate", "w_ple_proj", "ple_post_norm_g"]


def _step(x, p, positions, tgt, w, m, v):
    small_names = [n for n, _ in SMALL]
    T = x.shape[1]
    px, py, pc = _my_place()
    as_idx = lambda t: jnp.reshape(t, (1,)).astype(jnp.int32)

    def two_d(n, t):
        t = t.reshape(-1, t.shape[-1])
        return t.T if n in TRANSPOSED else t

    def full_shape(n, t):
        return (t.T if n in TRANSPOSED else t).reshape(w[n].shape)

    blocks = {n: two_d(n, w[n]).astype(MM) for n in BIG}
    big = dict(zip(EARLY, _all_gather("ag_weights", [blocks[n] for n in EARLY])))
    small = {n: (w[n] if n == "hg_lb_logits" else w[n].reshape(1, -1)) for n in small_names}

    loss_p, grad_x, small_g, grads, sibs, gots = _local_step(
        x[0], p[0, 0], positions.reshape(T, 1), tgt[0], small, big, late_blocks=blocks, core=as_idx(pc))

    out_g, out_d, out_m, out_v = {}, {}, {}, {}
    for n in BIG:
        res = _sum_adamw("adamw_" + n, grads[n], sibs[n], gots[n], two_d(n, w[n]), two_d(n, m[n]), two_d(n, v[n]),
                         as_idx(4 * px + 2 * py + pc), as_idx(2 * px + py))
        out_g[n], out_d[n], out_m[n], out_v[n] = [full_shape(n, r) for r in res]

    packed_g = _pack_small(small_g)
    loss_row = jnp.concatenate([jnp.pad(jnp.sum(loss_p).reshape(1, 1), ((0, 0), (0, 127))),
                                jnp.zeros((SMALL_ROWS - packed_g.shape[0] - 1, 128), F32)], axis=0)
    parts = _all_gather("ag_small", [jnp.concatenate([packed_g, loss_row], axis=0)])[0]
    pad_rows = lambda t: jnp.pad(t, ((0, SMALL_ROWS - t.shape[0]), (0, 0)))
    sw = pad_rows(_pack_small({n: w[n] for n in small_names}))
    sm = pad_rows(_pack_small({n: m[n] for n in small_names}))
    sv = pad_rows(_pack_small({n: v[n] for n in small_names}))
    g_s, d_s, m_s, v_s = _adamw_small(parts, sw, sm, sv)
    shapes = {n: w[n].shape for n in small_names}
    n_packed = packed_g.shape[0]
    loss = g_s[n_packed, 0]
    for src, dst in ((g_s, out_g), (d_s, out_d), (m_s, out_m), (v_s, out_v)):
        dst.update(_unpack_small(src, shapes))

    outs = [loss, grad_x[None]]
    for table in (out_g, out_d, out_m, out_v):
        outs += [table[n] for n in _WEIGHTS]
    return tuple(outs)


def kernel(x, p, positions, mix_norm_g, w_in, q_a_norm_g, w_uq, kv_a_norm_g, w_ukv, q_norm_g, k_norm_g, hg_lb_logits, hg_out_norm_g, w_branch, w_out, ffn_norm_g, w_ffn_gate, w_ffn_up, w_ffn_down, ple_gate_norm_g, w_ple_gate, w_ple_proj, ple_post_norm_g, loss_target, m_mix_norm_g, m_w_in, m_q_a_norm_g, m_w_uq, m_kv_a_norm_g, m_w_ukv, m_q_norm_g, m_k_norm_g, m_hg_lb_logits, m_hg_out_norm_g, m_w_branch, m_w_out, m_ffn_norm_g, m_w_ffn_gate, m_w_ffn_up, m_w_ffn_down, m_ple_gate_norm_g, m_w_ple_gate, m_w_ple_proj, m_ple_post_norm_g, v_mix_norm_g, v_w_in, v_q_a_norm_g, v_w_uq, v_kv_a_norm_g, v_w_ukv, v_q_norm_g, v_k_norm_g, v_hg_lb_logits, v_hg_out_norm_g, v_w_branch, v_w_out, v_ffn_norm_g, v_w_ffn_gate, v_w_ffn_up, v_w_ffn_down, v_ple_gate_norm_g, v_w_ple_gate, v_w_ple_proj, v_ple_post_norm_g):
    w = dict(mix_norm_g=mix_norm_g, w_in=w_in, q_a_norm_g=q_a_norm_g, w_uq=w_uq, kv_a_norm_g=kv_a_norm_g, w_ukv=w_ukv,
             q_norm_g=q_norm_g, k_norm_g=k_norm_g, hg_lb_logits=hg_lb_logits, hg_out_norm_g=hg_out_norm_g,
             w_branch=w_branch, w_out=w_out, ffn_norm_g=ffn_norm_g, w_ffn_gate=w_ffn_gate, w_ffn_up=w_ffn_up,
             w_ffn_down=w_ffn_down, ple_gate_norm_g=ple_gate_norm_g, w_ple_gate=w_ple_gate, w_ple_proj=w_ple_proj,
             ple_post_norm_g=ple_post_norm_g)
    m = dict(mix_norm_g=m_mix_norm_g, w_in=m_w_in, q_a_norm_g=m_q_a_norm_g, w_uq=m_w_uq, kv_a_norm_g=m_kv_a_norm_g,
             w_ukv=m_w_ukv, q_norm_g=m_q_norm_g, k_norm_g=m_k_norm_g, hg_lb_logits=m_hg_lb_logits,
             hg_out_norm_g=m_hg_out_norm_g, w_branch=m_w_branch, w_out=m_w_out, ffn_norm_g=m_ffn_norm_g,
             w_ffn_gate=m_w_ffn_gate, w_ffn_up=m_w_ffn_up, w_ffn_down=m_w_ffn_down,
             ple_gate_norm_g=m_ple_gate_norm_g, w_ple_gate=m_w_ple_gate, w_ple_proj=m_w_ple_proj,
             ple_post_norm_g=m_ple_post_norm_g)
    v = dict(mix_norm_g=v_mix_norm_g, w_in=v_w_in, q_a_norm_g=v_q_a_norm_g, w_uq=v_w_uq, kv_a_norm_g=v_kv_a_norm_g,
             w_ukv=v_w_ukv, q_norm_g=v_q_norm_g, k_norm_g=v_k_norm_g, hg_lb_logits=v_hg_lb_logits,
             hg_out_norm_g=v_hg_out_norm_g, w_branch=v_w_branch, w_out=v_w_out, ffn_norm_g=v_ffn_norm_g,
             w_ffn_gate=v_w_ffn_gate, w_ffn_up=v_w_ffn_up, w_ffn_down=v_w_ffn_down,
             ple_gate_norm_g=v_ple_gate_norm_g, w_ple_gate=v_w_ple_gate, w_ple_proj=v_w_ple_proj,
             ple_post_norm_g=v_ple_post_norm_g)
    return _step(x, p, positions, loss_target, w, m, v)
```

```python
import functools

import jax
import jax.numpy as jnp
import numpy as np
from jax import lax
from jax.experimental import pallas as pl
from jax.experimental.pallas import tpu as pltpu

F32 = jnp.float32
MM = jnp.bfloat16
HI = lax.Precision.HIGHEST
MESH_ID = pl.DeviceIdType.MESH

D_MODEL = 1024
N_DEV = 8
MLA_HEADS = 8
QK_NOPE = 64
QK_ROPE = 32
QK_DIM = 96
V_DIM = 64
HEAD_PAD = 128
Q_RANK = 384
KV_RANK = 256
ROPE_BASE = 10000.0
HG_HEADS = 4
HG_DIM = 128
HG_W = 512
HG_CHUNK = 64
HG_SUB = 16
FFN = 2816
PLE = 256
EPS = 1e-6
ATT_SCALE = QK_DIM ** -0.5
NEG = -1e30

ADAM_LR = 0.001
ADAM_B1 = 0.9
ADAM_B2 = 0.999
ADAM_EPS = 1e-08
ADAM_WD = 0.01
ADAM_STEP = 10

SEC_CQ = (0, 384)
SEC_CKV = (384, 256)
SEC_KR = (640, 128)
SEC_HQ = (768, 512)
SEC_HF = (1280, 512)
SEC_HI = (1792, 512)
SEC_HG = (2304, 512)
SEC_BG = (2816, 2048)
IN_PAD = 4864
SECTIONS = (SEC_CQ, SEC_CKV, SEC_KR, SEC_HQ, SEC_HF, SEC_HI, SEC_HG, SEC_BG)
COL_SECTIONS = ((0, 384), (384, 256), (640, 32), (672, 512), (1184, 512), (1696, 512), (2208, 512), (2720, 2048))
IN_COLS = 4768
IN_BLOCK = IN_COLS // 8

VMEM_LIMIT = 58 * 1024 * 1024
ROW_TILE = 256
ATT_TILE = 512
ATT_HEADS = 2
HG_BLOCK = 512


def _dot(a, b):
    return jnp.dot(a.astype(MM), b.astype(MM), preferred_element_type=F32)


def _dot_nt(a, b):
    return lax.dot_general(a.astype(MM), b.astype(MM), (((1,), (1,)), ((), ())), preferred_element_type=F32)


def _dot_tn(a, b):
    return lax.dot_general(a.astype(MM), b.astype(MM), (((0,), (0,)), ((), ())), preferred_element_type=F32)


def _dot_hi(a, b):
    return jnp.dot(a, b, preferred_element_type=F32, precision=HI)


def _sigmoid(x):
    return 1.0 / (1.0 + jnp.exp(-x))


def _rms(x, n=None):
    n = x.shape[-1] if n is None else n
    r = lax.rsqrt(jnp.sum(x * x, axis=-1, keepdims=True) * (1.0 / n) + EPS)
    return x * r, r


def _rms_bwd(dxh, xh, r, n=None):
    n = xh.shape[-1] if n is None else n
    return r * (dxh - xh * (jnp.sum(dxh * xh, axis=-1, keepdims=True) * (1.0 / n)))


def _rope_tables(pos, tm):
    lane = lax.broadcasted_iota(jnp.int32, (tm, HEAD_PAD), 1)
    idx = jnp.where(lane < QK_NOPE + QK_ROPE // 2, lane - QK_NOPE, lane - QK_NOPE - QK_ROPE // 2)
    inv = jnp.exp(idx.astype(F32) * (-np.log(ROPE_BASE) * 2.0 / QK_ROPE))
    ang = pos.astype(F32) * inv
    in_rope = (lane >= QK_NOPE) & (lane < QK_DIM)
    first = lane < QK_NOPE + QK_ROPE // 2
    cos_t = jnp.where(in_rope, jnp.cos(ang), 1.0)
    sin_t = jnp.where(in_rope, jnp.where(first, -jnp.sin(ang), jnp.sin(ang)), 0.0)
    return cos_t, sin_t, (first, in_rope)


def _rope_swap(x, halves):
    first, in_rope = halves
    half = QK_ROPE // 2
    return jnp.where(in_rope, jnp.where(first, pltpu.roll(x, HEAD_PAD - half, 1), pltpu.roll(x, half, 1)), 0.0)


def _cparams(sem, vmem=None):
    return pltpu.CompilerParams(dimension_semantics=sem, vmem_limit_bytes=vmem)


def _row_call(name, body, T, tm, row_ins, full_ins, row_outs, acc_outs, vmem=None, scratch=()):
    def kern(*refs):
        body(pl.program_id(0), *refs)

    in_specs = [pl.BlockSpec((tm, a.shape[1]), lambda i: (i, 0)) for a in row_ins]
    in_specs += [pl.BlockSpec(a.shape, lambda i, nd=a.ndim: (0,) * nd, pipeline_mode=pl.Buffered(1)) for a in full_ins]
    out_specs = [pl.BlockSpec((tm, n), lambda i: (i, 0)) for n, _ in row_outs]
    out_specs += [pl.BlockSpec(s, lambda i, nd=len(s): (0,) * nd) for s, _ in acc_outs]
    out_shape = [jax.ShapeDtypeStruct((T, n), dt) for n, dt in row_outs]
    out_shape += [jax.ShapeDtypeStruct(s, dt) for s, dt in acc_outs]
    return pl.pallas_call(
        kern, name=name, grid=(T // tm,), in_specs=in_specs, out_specs=out_specs, out_shape=out_shape,
        scratch_shapes=list(scratch), compiler_params=_cparams(("arbitrary",), vmem),
    )(*row_ins, *full_ins)


FFN_HALVES = (slice(0, FFN // 2), slice(FFN // 2, FFN))
ROW_CHUNK = 16
CHUNK_UNROLL = True


def _by_chunks(tm, fn):
    def step(c, carry):
        fn(pl.ds(pl.multiple_of(c * ROW_CHUNK, ROW_CHUNK), ROW_CHUNK))
        return carry

    lax.fori_loop(0, tm // ROW_CHUNK, step, 0, unroll=CHUNK_UNROLL)


def _fold8(x):
    return x[:8] + x[8:]


def _acc(ref, i, val):
    @pl.when(i == 0)
    def _():
        ref[...] = val

    @pl.when(i != 0)
    def _():
        ref[...] += val


def _in_proj_fwd(x, g_mix, w_in, T, tm):
    def body(i, x_ref, g_ref, w_ref, h_ref, *rest):
        outs, pj_s = rest[:-1], rest[-1]
        g = g_ref[...]

        def norm(rows):
            h_ref[rows, :] = (_rms(x_ref[rows, :])[0] * g).astype(MM)

        _by_chunks(tm, norm)
        for d in range(N_DEV):
            pj_s[d] = _dot_nt(h_ref[...], w_ref[d])

        def join_and_cut(rows):
            proj = jnp.concatenate([pj_s[d, rows, :] for d in range(N_DEV)], axis=1)
            for (s, n), o_ref in zip(COL_SECTIONS, outs):
                if n == QK_ROPE:
                    o_ref[rows, :] = jnp.concatenate(
                        [jnp.zeros((ROW_CHUNK, QK_NOPE), F32), proj[:, s:s + n],
                         jnp.zeros((ROW_CHUNK, HEAD_PAD - QK_DIM), F32)], axis=1)
                else:
                    o_ref[rows, :] = proj[:, s:s + n]

        _by_chunks(tm, join_and_cut)

    row_outs = [(D_MODEL, MM)] + [(n, F32) for _, n in SECTIONS]
    return _row_call("in_proj_fwd", body, T, tm, [x], [g_mix, w_in], row_outs, [], VMEM_LIMIT,
                     scratch=[pltpu.VMEM((N_DEV, tm, IN_BLOCK), F32)])


def _mla_heads_fwd(raw, g_pad, cos_t, sin_t, first):
    outs, saved = [], []
    for h in range(MLA_HEADS):
        xh, r = _rms(raw[:, h * HEAD_PAD:(h + 1) * HEAD_PAD], QK_DIM)
        y = xh * g_pad
        outs.append(y * cos_t + _rope_swap(y, first) * sin_t)
        saved.append((xh, r))
    return outs, saved


def _mla_raw_heads(cqn, ckvn, kr, wuq_ref, wukv_ref, tm):
    lane = lax.broadcasted_iota(jnp.int32, (tm, HEAD_PAD), 1)
    nope = lane < QK_NOPE
    one_lane = jnp.where(lane == V_DIM, 1.0, 0.0)
    qs, ks, vs = [], [], []
    for h in range(MLA_HEADS):
        qs.append(_dot_nt(cqn, wuq_ref[h]))
        kv = _dot(ckvn, wukv_ref[h])
        ks.append(jnp.where(nope, kv, kr))
        vs.append(jnp.where(nope, pltpu.roll(kv, V_DIM, 1), one_lane))
    return jnp.concatenate(qs, axis=1), jnp.concatenate(ks, axis=1), jnp.concatenate(vs, axis=1)


def _mla_prep_fwd(cq, ckv, kr, pos, g_qa, g_kva, g_qn, g_kn, w_uq, w_ukv, T, tm):
    def body(i, cq_ref, ckv_ref, kr_ref, pos_ref, gqa_ref, gkva_ref, gqn_ref, gkn_ref, wuq_ref, wukv_ref,
             q_ref, k_ref, v_ref):
        cos_t, sin_t, first = _rope_tables(pos_ref[...], tm)
        cqn = _rms(cq_ref[...])[0] * gqa_ref[...]
        ckvn = _rms(ckv_ref[...])[0] * gkva_ref[...]
        q_raw, k_raw, v = _mla_raw_heads(cqn, ckvn, kr_ref[...], wuq_ref, wukv_ref, tm)
        qs, _ = _mla_heads_fwd(q_raw, gqn_ref[...], cos_t, sin_t, first)
        ks, _ = _mla_heads_fwd(k_raw, gkn_ref[...], cos_t, sin_t, first)
        q_ref[...] = (jnp.concatenate(qs, axis=1) * ATT_SCALE).astype(MM)
        k_ref[...] = jnp.concatenate(ks, axis=1).astype(MM)
        v_ref[...] = v.astype(MM)

    w = MLA_HEADS * HEAD_PAD
    return _row_call("mla_prep_fwd", body, T, tm, [cq, ckv, kr, pos], [g_qa, g_kva, g_qn, g_kn, w_uq, w_ukv],
                     [(w, MM), (w, MM), (w, MM)], [])


def _causal_pairs(n, by_query):
    if by_query:
        pairs = [(q, k) for q in range(n) for k in range(q + 1)]
    else:
        pairs = [(q, k) for k in range(n) for q in range(k, n)]
    return np.array([p[0] for p in pairs], np.int32), np.array([p[1] for p in pairs], np.int32)


def _flash_fwd(qf, kf, vf, T, ag_blocks=()):
    tq = min(ATT_TILE, T)
    nq = T // tq

    qi_tab, ki_tab = _causal_pairs(nq, by_query=True)

    hp = ATT_HEADS

    n_ag = len(ag_blocks)
    n_heads, n_pairs = MLA_HEADS // hp, len(qi_tab)

    def body(qi_ref, ki_ref, q_ref, k_ref, v_ref, *rest):
        ag_in, (o_ref, lse_ref), rest = rest[:n_ag], rest[n_ag:n_ag + 2], rest[n_ag + 2:]
        ag_out, (m_s, acc_s), ag_sems = rest[:n_ag], rest[n_ag:n_ag + 2], rest[n_ag + 2:]
        t = pl.program_id(1)
        qi, ki = qi_ref[t], ki_ref[t]
        if n_ag:
            @pl.when((pl.program_id(0) == 0) & (t == 0))
            def _():
                _ag_start(ag_in, ag_out, ag_sems)

        @pl.when(ki == 0)
        def _():
            m_s[...] = jnp.full_like(m_s, NEG)
            acc_s[...] = jnp.zeros_like(acc_s)

        def step(masked):
            for hh in range(hp):
                hs = slice(hh * HEAD_PAD, (hh + 1) * HEAD_PAD)
                s_t = _dot_nt(k_ref[:, hs], q_ref[:, hs])
                if masked:
                    key = lax.broadcasted_iota(jnp.int32, (tq, tq), 0)
                    qry = lax.broadcasted_iota(jnp.int32, (tq, tq), 1)
                    s_t = jnp.where(key <= qry, s_t, NEG)
                m_old = m_s[hh]
                m_new = jnp.maximum(m_old, jnp.max(s_t, axis=0, keepdims=True))
                p_t = jnp.exp(s_t - m_new)
                acc_s[hh] = jnp.exp(m_old - m_new) * acc_s[hh] + _dot_tn(v_ref[:, hs], p_t)
                m_s[hh] = m_new

        @pl.when(ki < qi)
        def _():
            step(False)

        @pl.when(ki == qi)
        def _():
            step(True)
            real = lax.broadcasted_iota(jnp.int32, (HEAD_PAD, tq), 0) < V_DIM
            for hh in range(hp):
                hs = slice(hh * HEAD_PAD, (hh + 1) * HEAD_PAD)
                acc = acc_s[hh]
                l = acc[V_DIM:V_DIM + 1]
                o_ref[:, hs] = jnp.where(real, acc / l, 0.0).T
                lse_ref[:, hs] = jnp.broadcast_to(m_s[hh] + jnp.log(l), (HEAD_PAD, tq)).T

        if n_ag:
            @pl.when((pl.program_id(0) == n_heads - 1) & (t == n_pairs - 1))
            def _():
                _ag_finish(ag_in, ag_out, ag_sems)

    q_spec = pl.BlockSpec((tq, hp * HEAD_PAD), lambda h, t, qi_ref, ki_ref: (qi_ref[t], h))
    kv_spec = pl.BlockSpec((tq, hp * HEAD_PAD), lambda h, t, qi_ref, ki_ref: (ki_ref[t], h))
    any_spec = pl.BlockSpec(memory_space=pl.ANY)
    grid_spec = pltpu.PrefetchScalarGridSpec(
        num_scalar_prefetch=2, grid=(n_heads, n_pairs),
        in_specs=[q_spec, kv_spec, kv_spec] + [any_spec] * n_ag, out_specs=[q_spec, q_spec] + [any_spec] * n_ag,
        scratch_shapes=[pltpu.VMEM((hp, 1, tq), F32), pltpu.VMEM((hp, HEAD_PAD, tq), F32)]
        + (_ag_sems(n_ag) if n_ag else []))
    return pl.pallas_call(
        body, name="flash_fwd", grid_spec=grid_spec,
        out_shape=[jax.ShapeDtypeStruct((T, MLA_HEADS * HEAD_PAD), F32)] * 2 + _ag_out_shapes(ag_blocks),
        compiler_params=_cparams(("arbitrary", "arbitrary")),
    )(jnp.asarray(qi_tab), jnp.asarray(ki_tab), qf, kf, vf, *ag_blocks)


def _hg_gates(hf, lb):
    sg = _sigmoid(hf)
    f = lb + (1.0 - lb) * sg
    return sg, f, jnp.log(f), 1.0 - f


def _tri(n, lower):
    r = lax.broadcasted_iota(jnp.int32, (n, n), 0)
    c = lax.broadcasted_iota(jnp.int32, (n, n), 1)
    return jnp.where((c <= r) if lower else (c >= r), 1.0, 0.0).astype(F32)


def _hg_intra(q, k, b):
    C, S = HG_CHUNK, HG_SUB
    row_c = lax.broadcasted_iota(jnp.int32, (C, HG_DIM), 0)
    row_s = lax.broadcasted_iota(jnp.int32, (S, HG_DIM), 0)
    lane_c = lax.broadcasted_iota(jnp.int32, (S, C), 1)
    blocks, saved = [], []
    for blk in range(C // S):
        lo = blk * S
        q_b, k_b, b_b = q[lo:lo + S], k[lo:lo + S], b[lo:lo + S]
        a_b = jnp.zeros((S, C), F32)
        for j in range(S):
            w = jnp.exp(jnp.where(row_s >= j, b_b - b_b[j:j + 1], NEG))
            col = jnp.sum(q_b * (k_b[j:j + 1] * w), axis=1, keepdims=True)
            a_b = jnp.where(lane_c == lo + j, col, a_b)
        if blk > 0:
            ref = b[lo - 1:lo]
            q_e = jnp.exp(b_b - ref)
            q_t = q_b * q_e
            k_e = jnp.exp(jnp.where(row_c < lo, ref - b, NEG))
            a_b = a_b + _dot_nt(q_t, k * k_e)
            saved.append((q_t, k_e, q_e))
        else:
            saved.append(None)
        blocks.append(a_b)
    return jnp.concatenate(blocks, axis=0), saved


def _hgrn_fwd(hq, hf, hi, lb, T):
    rb = min(HG_BLOCK, T)
    ncb = rb // HG_CHUNK

    def body(hq_ref, hf_ref, hi_ref, lb_ref, o_ref, s0_ref, st_ref):
        @pl.when(pl.program_id(0) == 0)
        def _():
            st_ref[...] = jnp.zeros_like(st_ref)

        tril = _tri(HG_CHUNK, True)

        def chunk(c, carry):
            rows = pl.ds(pl.multiple_of(c * HG_CHUNK, HG_CHUNK), HG_CHUNK)
            _, _, logf, kk = _hg_gates(hf_ref[rows, :], lb_ref[...])
            b = _dot_hi(tril, logf)
            q_all, v_all = hq_ref[rows, :], hi_ref[rows, :]
            outs = []
            for h in range(HG_HEADS):
                ls = slice(h * HG_DIM, (h + 1) * HG_DIM)
                q, k, v, bh = q_all[:, ls], kk[:, ls], v_all[:, ls], b[:, ls]
                st = st_ref[h]
                s0_ref[c, h * HG_DIM:(h + 1) * HG_DIM, :] = st
                b_end = bh[HG_CHUNK - 1:HG_CHUNK]
                a, _ = _hg_intra(q, k, bh)
                outs.append(_dot_nt(q * jnp.exp(bh), st) + _dot(a, v))
                st_ref[h] = st * jnp.exp(b_end) + _dot_tn(v, k * jnp.exp(b_end - bh))
            o_ref[rows, :] = jnp.concatenate(outs, axis=1)
            return carry

        lax.fori_loop(0, ncb, chunk, 0)

    row = pl.BlockSpec((rb, HG_W), lambda i: (i, 0))
    return pl.pallas_call(
        body, name="hgrn_fwd", grid=(T // rb,),
        in_specs=[row, row, row, pl.BlockSpec((1, HG_W), lambda i: (0, 0))],
        out_specs=[row, pl.BlockSpec((ncb, HG_W, HG_DIM), lambda i: (i, 0, 0))],
        out_shape=[jax.ShapeDtypeStruct((T, HG_W), F32), jax.ShapeDtypeStruct((T // HG_CHUNK, HG_W, HG_DIM), F32)],
        scratch_shapes=[pltpu.VMEM((HG_HEADS, HG_DIM, HG_DIM), F32)],
        compiler_params=_cparams(("arbitrary",)),
    )(hq, hf, hi, lb)


def _hgrn_bwd(hq, hf, hi, do, s0, lb, T, xchg=()):
    rb = min(HG_BLOCK, T)
    ncb = rb // HG_CHUNK
    nb = T // rb
    C, S = HG_CHUNK, HG_SUB
    n_x = len(xchg)

    def body(hq_ref, hf_ref, hi_ref, do_ref, s0_ref, lb_ref, *rest):
        x_in, (dq_ref, df_ref, dv_ref, dlb_ref), rest = rest[:n_x], rest[n_x:n_x + 4], rest[n_x + 4:]
        x_out, dst_ref, x_sems = rest[:n_x], rest[n_x], rest[n_x + 1:]

        @pl.when(pl.program_id(0) == 0)
        def _():
            dst_ref[...] = jnp.zeros_like(dst_ref)
            dlb_ref[...] = jnp.zeros_like(dlb_ref)
            for cp in _xchips_copies(x_in, x_out, x_sems) if n_x else ():
                cp.start()

        tril, triu = _tri(C, True), _tri(C, False)
        row_cc = lax.broadcasted_iota(jnp.int32, (C, C), 0)
        col_cc = lax.broadcasted_iota(jnp.int32, (C, C), 1)
        row_s = lax.broadcasted_iota(jnp.int32, (S, HG_DIM), 0)
        lane_sc = lax.broadcasted_iota(jnp.int32, (S, C), 1)
        last_row = lax.broadcasted_iota(jnp.int32, (C, HG_DIM), 0) == C - 1
        lb_v = lb_ref[...]

        def chunk(cc, carry):
            c = ncb - 1 - cc
            rows = pl.ds(pl.multiple_of(c * C, C), C)
            hf_c = hf_ref[rows, :]
            sg, f, logf, kk = _hg_gates(hf_c, lb_v)
            b = _dot_hi(tril, logf)
            q_all, v_all, do_all = hq_ref[rows, :], hi_ref[rows, :], do_ref[rows, :]
            dq_o, dk_o, dv_o, db_o = [], [], [], []
            for h in range(HG_HEADS):
                ls = slice(h * HG_DIM, (h + 1) * HG_DIM)
                q, k, v, bh, d_o = q_all[:, ls], kk[:, ls], v_all[:, ls], b[:, ls], do_all[:, ls]
                st0 = s0_ref[c, h * HG_DIM:(h + 1) * HG_DIM, :]
                dst = dst_ref[h]
                b_end = bh[C - 1:C]
                e_b, e_end = jnp.exp(bh), jnp.exp(b_end)
                e_rem = jnp.exp(b_end - bh)
                qe, kd = q * e_b, k * e_rem
                st_end = st0 * e_end + _dot_tn(v, kd)
                a, saved = _hg_intra(q, k, bh)
                d_a = jnp.where(col_cc <= row_cc, _dot_nt(d_o, v), 0.0)
                dv = _dot_tn(a, d_o) + _dot_nt(kd, dst)
                dq = e_b * _dot(d_o, st0)
                dk = e_rem * _dot(v, dst)
                dq_blocks, dk_diag = [], []
                for blk in range(C // S):
                    lo = blk * S
                    q_b, k_b, b_b = q[lo:lo + S], k[lo:lo + S], bh[lo:lo + S]
                    da_b = d_a[lo:lo + S]
                    dq_b = jnp.zeros((S, HG_DIM), F32)
                    dk_b = jnp.zeros((S, HG_DIM), F32)
                    for j in range(S):
                        w = jnp.exp(jnp.where(row_s >= j, b_b - b_b[j:j + 1], NEG))
                        col = jnp.sum(jnp.where(lane_sc == lo + j, da_b, 0.0), axis=1, keepdims=True)
                        dq_b = dq_b + col * (k_b[j:j + 1] * w)
                        dk_row = jnp.sum(col * (q_b * w), axis=0, keepdims=True)
                        dk_b = jnp.where(row_s == j, dk_row, dk_b)
                    if blk > 0:
                        q_t, k_e, q_e = saved[blk]
                        da_off = jnp.where(lane_sc < lo, da_b, 0.0)
                        dq_b = dq_b + _dot(da_off, k * k_e) * q_e
                        dk = dk + _dot_tn(da_off, q_t) * k_e
                    dq_blocks.append(dq_b)
                    dk_diag.append(dk_b)
                dq = dq + jnp.concatenate(dq_blocks, axis=0)
                dk = dk + jnp.concatenate(dk_diag, axis=0)
                extra = jnp.sum(dst * st_end, axis=0, keepdims=True)
                db_o.append(q * dq - k * dk + jnp.where(last_row, extra, 0.0))
                dst_ref[h] = dst * e_end + _dot_tn(d_o, qe)
                dq_o.append(dq)
                dk_o.append(dk)
                dv_o.append(dv)
            dlogf = _dot_hi(triu, jnp.concatenate(db_o, axis=1))
            d_f = dlogf / f - jnp.concatenate(dk_o, axis=1)
            dq_ref[rows, :] = jnp.concatenate(dq_o, axis=1)
            dv_ref[rows, :] = jnp.concatenate(dv_o, axis=1)
            df_ref[rows, :] = d_f * (1.0 - lb_v) * sg * (1.0 - sg)
            dlb_ref[...] += jnp.sum(d_f * (1.0 - sg), axis=0, keepdims=True)
            return carry

        lax.fori_loop(0, ncb, chunk, 0)

        if n_x:
            @pl.when(pl.program_id(0) == nb - 1)
            def _():
                for cp in _xchips_copies(x_in, x_out, x_sems):
                    cp.wait()

    row = pl.BlockSpec((rb, HG_W), lambda i: (nb - 1 - i, 0))
    one = pl.BlockSpec((1, HG_W), lambda i: (0, 0))
    any_spec = pl.BlockSpec(memory_space=pl.ANY)
    return pl.pallas_call(
        body, name="hgrn_bwd", grid=(nb,),
        in_specs=[row, row, row, row, pl.BlockSpec((ncb, HG_W, HG_DIM), lambda i: (nb - 1 - i, 0, 0)), one]
        + [any_spec] * n_x,
        out_specs=[row, row, row, one] + [any_spec] * n_x,
        out_shape=[jax.ShapeDtypeStruct((T, HG_W), F32)] * 3 + [jax.ShapeDtypeStruct((1, HG_W), F32)]
        + _xchips_out_shapes(xchg),
        scratch_shapes=[pltpu.VMEM((HG_HEADS, HG_DIM, HG_DIM), F32)] + (_xchips_sems(n_x) if n_x else []),
        compiler_params=_cparams(("arbitrary",)),
    )(hq, hf, hi, do, s0, lb, *xchg)


def _silu_parts(x):
    sg = _sigmoid(x)
    return x * sg, sg * (1.0 + x * (1.0 - sg))


def _merge_fwd(attn, o, hg, bg, x, g_out, w_bra, w_brb, w_out, T, tm):
    def body(i, attn_ref, o_ref, hg_ref, bg_ref, x_ref, g_ref, wa_ref, wb_ref, wo_ref,
             x1_ref, ya_ref, yb_ref, m_ref, rec_ref):
        g = g_ref[...]

        def recurrent_out(rows):
            for h in range(HG_HEADS):
                ls = slice(h * HG_DIM, (h + 1) * HG_DIM)
                rec_ref[rows, ls] = (_rms(o_ref[rows, ls])[0] * g * _silu_parts(hg_ref[rows, ls])[0]).astype(MM)

        _by_chunks(tm, recurrent_out)
        ya_ref[...] = _dot(attn_ref[...], wa_ref[...])
        yb_ref[...] = jnp.dot(rec_ref[...], wb_ref[...], preferred_element_type=F32)

        def gate(rows):
            m_ref[rows, :] = (_sigmoid(bg_ref[rows, :D_MODEL]) * ya_ref[rows, :]
                              + _sigmoid(bg_ref[rows, D_MODEL:]) * yb_ref[rows, :]).astype(MM)

        _by_chunks(tm, gate)
        x1_ref[...] = x_ref[...] + jnp.dot(m_ref[...], wo_ref[...], preferred_element_type=F32)

    return _row_call("merge_fwd", body, T, tm, [attn, o, hg, bg, x], [g_out, w_bra, w_brb, w_out],
                     [(D_MODEL, F32), (D_MODEL, F32), (D_MODEL, F32), (D_MODEL, MM), (HG_W, MM)], [], VMEM_LIMIT)


def _ffn_fwd(x1, g_ffn, w_g, w_u, w_d, T, tm):
    def body(i, x1_ref, g_ref, wg_ref, wu_ref, wd_ref, x2_ref, gt_ref, up_ref, h2_ref, a_s):
        g = g_ref[...]

        def norm(rows):
            h2_ref[rows, :] = (_rms(x1_ref[rows, :])[0] * g).astype(MM)

        _by_chunks(tm, norm)
        gt_ref[...] = _dot_nt(h2_ref[...], wg_ref[...])
        up_ref[...] = _dot_nt(h2_ref[...], wu_ref[...])

        def act(rows):
            for cs in FFN_HALVES:
                a_s[rows, cs] = (_silu_parts(gt_ref[rows, cs])[0] * up_ref[rows, cs]).astype(MM)

        _by_chunks(tm, act)
        x2_ref[...] = x1_ref[...] + jnp.dot(a_s[...], wd_ref[...], preferred_element_type=F32)

    return _row_call("ffn_fwd", body, T, tm, [x1], [g_ffn, w_g, w_u, w_d],
                     [(D_MODEL, F32), (FFN, F32), (FFN, F32), (D_MODEL, MM)], [], VMEM_LIMIT,
                     scratch=[pltpu.VMEM((tm, FFN), MM)])


def _ple_loss(x2, p, tgt, g_pg, g_post, w_pg, w_pp, T, tm):
    def body(i, x2_ref, p_ref, t_ref, gpg_ref, gpo_ref, wpg_ref, wpp_ref,
             dx2_ref, loss_ref, dgpo_ref, dgpg_ref, dwpg_ref, dwpp_ref, u_s, n3_s, z_s, dz_s, du_s, dy_s, dn3_s):
        @pl.when(i == 0)
        def _():
            for ref in (loss_ref, dgpo_ref, dgpg_ref, dwpg_ref, dwpp_ref):
                ref[...] = jnp.zeros_like(ref)

        gpg, gpo = gpg_ref[...], gpo_ref[...]
        p_mm = p_ref[...].astype(MM)
        for d in range(N_DEV):
            u_s[:, d * HEAD_PAD:(d + 1) * HEAD_PAD] = jnp.dot(p_mm, wpp_ref[d], preferred_element_type=F32)

        def gate_input(rows):
            n3_s[rows, :] = (_rms(x2_ref[rows, :])[0] * gpg).astype(MM)

        _by_chunks(tm, gate_input)
        z_s[...] = jnp.dot(n3_s[...], wpg_ref[...], preferred_element_type=F32)

        def loss_and_back(rows):
            uh, ru = _rms(u_s[rows, :])
            e = uh * gpo
            gate = _sigmoid(z_s[rows, :])
            diff = x2_ref[rows, :] + gate * e - t_ref[rows, :]
            dy = diff * (1.0 / D_MODEL)
            de = dy * gate
            dz_s[rows, :] = (dy * e * gate * (1.0 - gate)).astype(MM)
            du_s[rows, :] = _rms_bwd(de * gpo, uh, ru).astype(MM)
            dy_s[rows, :] = dy
            loss_ref[...] += _fold8(diff * diff) * (0.5 / D_MODEL)
            dgpo_ref[...] += _fold8(de * uh)

        _by_chunks(tm, loss_and_back)
        dn3_s[...] = _dot_nt(dz_s[...], wpg_ref[...])

        def gate_norm_back(rows):
            x2h, r3 = _rms(x2_ref[rows, :])
            dn3 = dn3_s[rows, :]
            dx2_ref[rows, :] = dy_s[rows, :] + _rms_bwd(dn3 * gpg, x2h, r3)
            dgpg_ref[...] += _fold8(dn3 * x2h)

        _by_chunks(tm, gate_norm_back)
        dwpg_ref[...] += _dot_tn(n3_s[...], dz_s[...])
        for d in range(N_DEV):
            dwpp_ref[d] += _dot_tn(p_mm, du_s[:, d * HEAD_PAD:(d + 1) * HEAD_PAD])

    vec = ((8, D_MODEL), F32)
    wide = lambda dt: pltpu.VMEM((tm, D_MODEL), dt)
    return _row_call("ple_loss", body, T, tm, [x2, p, tgt], [g_pg, g_post, w_pg, w_pp], [(D_MODEL, F32)],
                     [vec, vec, vec, ((D_MODEL, D_MODEL), F32), ((N_DEV, PLE, HEAD_PAD), F32)], VMEM_LIMIT,
                     scratch=[wide(F32), wide(MM), wide(F32), wide(MM), wide(MM), wide(F32), wide(F32)])


def _ffn_bwd(dx2, x1, gt, up, g_ffn, w_g, w_u, w_d, T, tm):
    def body(i, dx2_ref, x1_ref, gt_ref, up_ref, g_ref, wg_ref, wu_ref, wd_ref,
             dx1_ref, a_ref, dgt_ref, dup_ref, dg_ref, da_s, dh2_s):
        @pl.when(i == 0)
        def _():
            dg_ref[...] = jnp.zeros_like(dg_ref)

        g = g_ref[...]
        da_s[...] = _dot_nt(dx2_ref[...], wd_ref[...])

        def act_back(rows):
            for cs in FFN_HALVES:
                up, da = up_ref[rows, cs], da_s[rows, cs]
                silu, dsilu = _silu_parts(gt_ref[rows, cs])
                dgt_ref[rows, cs] = (da * up * dsilu).astype(MM)
                dup_ref[rows, cs] = (da * silu).astype(MM)
                a_ref[rows, cs] = (silu * up).astype(MM)

        _by_chunks(tm, act_back)
        dh2_s[...] = (jnp.dot(dgt_ref[...], wg_ref[...], preferred_element_type=F32)
                      + jnp.dot(dup_ref[...], wu_ref[...], preferred_element_type=F32))

        def norm_back(rows):
            x1h, r = _rms(x1_ref[rows, :])
            dh2 = dh2_s[rows, :]
            dx1_ref[rows, :] = dx2_ref[rows, :] + _rms_bwd(dh2 * g, x1h, r)
            dg_ref[...] += _fold8(dh2 * x1h)

        _by_chunks(tm, norm_back)

    return _row_call("ffn_bwd", body, T, tm, [dx2, x1, gt, up], [g_ffn, w_g, w_u, w_d],
                     [(D_MODEL, F32), (FFN, MM), (FFN, MM), (FFN, MM)], [((8, D_MODEL), F32)], VMEM_LIMIT,
                     scratch=[pltpu.VMEM((tm, FFN), F32), pltpu.VMEM((tm, D_MODEL), F32)])


def _merge_bwd(dx1, ya, yb, bg, o, hg, attn, m, rec, g_out, w_bra, w_brb, w_out, T, tm):
    def body(i, dx1_ref, ya_ref, yb_ref, bg_ref, o_ref, hg_ref, attn_ref, m_ref, rec_ref, g_ref, wa_ref, wb_ref, wo_ref,
             dattn_ref, do_ref, dhg_ref, dbg_ref, dg_ref, dwo_ref, dwa_ref, dwb_ref, dm_s, dya_s, dyb_s, drec_s):
        @pl.when(i == 0)
        def _():
            for ref in (dg_ref, dwo_ref, dwa_ref, dwb_ref):
                ref[...] = jnp.zeros_like(ref)

        g = g_ref[...]
        dx1 = dx1_ref[...].astype(MM)
        dm_s[...] = _dot_nt(dx1, wo_ref[...])

        def gate_back(rows):
            dm = dm_s[rows, :]
            ga, gb = _sigmoid(bg_ref[rows, :D_MODEL]), _sigmoid(bg_ref[rows, D_MODEL:])
            dya_s[rows, :] = (dm * ga).astype(MM)
            dyb_s[rows, :] = (dm * gb).astype(MM)
            dbg_ref[rows, :D_MODEL] = dm * ya_ref[rows, :] * ga * (1.0 - ga)
            dbg_ref[rows, D_MODEL:] = dm * yb_ref[rows, :] * gb * (1.0 - gb)

        _by_chunks(tm, gate_back)
        dwo_ref[...] += _dot_tn(m_ref[...], dx1)
        attn_mm = attn_ref[...].astype(MM)
        for d in range(N_DEV):
            ds = slice(d * HEAD_PAD, (d + 1) * HEAD_PAD)
            dwa_ref[d] += _dot_tn(attn_mm, dya_s[:, ds])
            dwb_ref[d] += _dot_tn(rec_ref[...], dyb_s[:, ds])
        dattn_ref[...] = _dot_nt(dya_s[...], wa_ref[...])
        drec_s[...] = _dot_nt(dyb_s[...], wb_ref[...])

        def recurrent_out_back(rows):
            for h in range(HG_HEADS):
                ls = slice(h * HG_DIM, (h + 1) * HG_DIM)
                oh, r = _rms(o_ref[rows, ls])
                silu, dsilu = _silu_parts(hg_ref[rows, ls])
                dr = drec_s[rows, ls]
                dhg_ref[rows, ls] = dr * oh * g * dsilu
                don = dr * silu
                dg_ref[...] += _fold8(don * oh)
                do_ref[rows, ls] = _rms_bwd(don * g, oh, r)

        _by_chunks(tm, recurrent_out_back)

    wide = lambda n, dt: pltpu.VMEM((tm, n), dt)
    return _row_call("merge_bwd", body, T, tm, [dx1, ya, yb, bg, o, hg, attn, m, rec], [g_out, w_bra, w_brb, w_out],
                     [(D_MODEL, F32), (HG_W, F32), (HG_W, F32), (2 * D_MODEL, F32)],
                     [((8, HG_DIM), F32), ((D_MODEL, D_MODEL), F32), ((N_DEV, MLA_HEADS * HEAD_PAD, HEAD_PAD), F32),
                      ((N_DEV, HG_W, HEAD_PAD), F32)], VMEM_LIMIT,
                     scratch=[wide(D_MODEL, F32), wide(D_MODEL, MM), wide(D_MODEL, MM), wide(HG_W, F32)])


def _flash_bwd(qf, kf, vf, o, do, lse, T, xchg=()):
    tq = min(ATT_TILE, T)
    nq = T // tq

    qi_tab, ki_tab = _causal_pairs(nq, by_query=False)

    n_x = len(xchg)
    hp = ATT_HEADS
    n_heads, n_pairs = MLA_HEADS // hp, len(qi_tab)

    def body(qi_ref, ki_ref, q_ref, k_ref, v_ref, o_ref, do_ref, lse_ref, *rest):
        x_in, (dq_ref, dk_ref, dv_ref), rest = rest[:n_x], rest[n_x:n_x + 3], rest[n_x + 3:]
        x_out, x_sems = rest[:n_x], rest[n_x:]
        t = pl.program_id(1)
        qi, ki = qi_ref[t], ki_ref[t]
        if n_x:
            @pl.when((pl.program_id(0) == 0) & (t == 0))
            def _():
                for cp in _xchips_copies(x_in, x_out, x_sems):
                    cp.start()

        @pl.when(t == 0)
        def _():
            dq_ref[...] = jnp.zeros_like(dq_ref)

        def step(first):
            rows = pl.ds(pl.multiple_of(qi * tq, tq), tq)
            for hh in range(hp):
                hs = slice(hh * HEAD_PAD, (hh + 1) * HEAD_PAD)
                q, k, d_o = q_ref[:, hs], k_ref[:, hs], do_ref[:, hs]
                s = _dot_nt(q, k)
                if first:
                    row = lax.broadcasted_iota(jnp.int32, (tq, tq), 0)
                    col = lax.broadcasted_iota(jnp.int32, (tq, tq), 1)
                    s = jnp.where(col <= row, s, NEG)
                p = jnp.exp(s - lse_ref[:, hh * HEAD_PAD:hh * HEAD_PAD + 1])
                delta = jnp.sum(d_o * o_ref[:, hs], axis=1, keepdims=True)
                ds = p * (_dot_nt(d_o, v_ref[:, hs]) - delta)
                dq_ref[rows, hs] += _dot(ds, k)
                if first:
                    dv_ref[:, hs] = _dot_tn(p, d_o)
                    dk_ref[:, hs] = _dot_tn(ds, q)
                else:
                    dv_ref[:, hs] += _dot_tn(p, d_o)
                    dk_ref[:, hs] += _dot_tn(ds, q)

        @pl.when(qi == ki)
        def _():
            step(True)

        @pl.when(qi > ki)
        def _():
            step(False)

        if n_x:
            @pl.when((pl.program_id(0) == n_heads - 1) & (t == n_pairs - 1))
            def _():
                for cp in _xchips_copies(x_in, x_out, x_sems):
                    cp.wait()

    q_spec = pl.BlockSpec((tq, hp * HEAD_PAD), lambda h, t, qi_ref, ki_ref: (qi_ref[t], h))
    kv_spec = pl.BlockSpec((tq, hp * HEAD_PAD), lambda h, t, qi_ref, ki_ref: (ki_ref[t], h))
    any_spec = pl.BlockSpec(memory_space=pl.ANY)
    w = MLA_HEADS * HEAD_PAD
    grid_spec = pltpu.PrefetchScalarGridSpec(
        num_scalar_prefetch=2, grid=(n_heads, n_pairs),
        in_specs=[q_spec, kv_spec, kv_spec, q_spec, q_spec, q_spec] + [any_spec] * n_x,
        out_specs=[pl.BlockSpec((T, hp * HEAD_PAD), lambda h, t, qi_ref, ki_ref: (0, h)), kv_spec, kv_spec]
        + [any_spec] * n_x,
        scratch_shapes=_xchips_sems(n_x) if n_x else [])
    return pl.pallas_call(
        body, name="flash_bwd", grid_spec=grid_spec,
        out_shape=[jax.ShapeDtypeStruct((T, w), F32)] * 3 + _xchips_out_shapes(xchg),
        compiler_params=_cparams(("arbitrary", "arbitrary")),
    )(jnp.asarray(qi_tab), jnp.asarray(ki_tab), qf, kf, vf, o, do, lse, *xchg)


def _mla_heads_bwd(d_out, saved, g_pad, cos_t, sin_t, first):
    d_raw, dg = [], jnp.zeros((1, HEAD_PAD), F32)
    for h in range(MLA_HEADS):
        xh, r = saved[h]
        dy = d_out[:, h * HEAD_PAD:(h + 1) * HEAD_PAD]
        dn = dy * cos_t + _rope_swap(dy * sin_t, first)
        dg = dg + jnp.sum(dn * xh, axis=0, keepdims=True)
        d_raw.append(_rms_bwd(dn * g_pad, xh, r, QK_DIM))
    return d_raw, dg


def _mla_prep_bwd(cq, ckv, kr, pos, dqf, dkf, dvf, g_qa, g_kva, g_qn, g_kn, w_uq, w_ukv, T, tm):
    def body(i, cq_ref, ckv_ref, kr_ref, pos_ref, dq_ref, dk_ref, dv_ref,
             gqa_ref, gkva_ref, gqn_ref, gkn_ref, wuq_ref, wukv_ref,
             dcq_ref, dckv_ref, dkr_ref, dgqa_ref, dgkva_ref, dgqn_ref, dgkn_ref, dwuq_ref, dwukv_ref):
        cos_t, sin_t, first = _rope_tables(pos_ref[...], tm)
        cqh, rq = _rms(cq_ref[...])
        ckvh, rkv = _rms(ckv_ref[...])
        cqn, ckvn = cqh * gqa_ref[...], ckvh * gkva_ref[...]
        q_raw, k_raw, _ = _mla_raw_heads(cqn, ckvn, kr_ref[...], wuq_ref, wukv_ref, tm)
        _, q_saved = _mla_heads_fwd(q_raw, gqn_ref[...], cos_t, sin_t, first)
        _, k_saved = _mla_heads_fwd(k_raw, gkn_ref[...], cos_t, sin_t, first)
        dq_heads, dgqn = _mla_heads_bwd(dq_ref[...] * ATT_SCALE, q_saved, gqn_ref[...], cos_t, sin_t, first)
        dk_heads, dgkn = _mla_heads_bwd(dk_ref[...], k_saved, gkn_ref[...], cos_t, sin_t, first)
        lane = lax.broadcasted_iota(jnp.int32, (tm, HEAD_PAD), 1)
        nope = lane < QK_NOPE
        dcqn = jnp.zeros((tm, Q_RANK), F32)
        dckvn = jnp.zeros((tm, KV_RANK), F32)
        dkr = jnp.zeros((tm, HEAD_PAD), F32)
        cqn_mm, ckvn_mm = cqn.astype(MM), ckvn.astype(MM)
        for h in range(MLA_HEADS):
            hs = slice(h * HEAD_PAD, (h + 1) * HEAD_PAD)
            dq_h = dq_heads[h].astype(MM)
            dkv_h = jnp.where(nope, dk_heads[h], pltpu.roll(dv_ref[:, hs], V_DIM, 1)).astype(MM)
            _acc(dwuq_ref.at[h], i, _dot_tn(dq_h, cqn_mm))
            _acc(dwukv_ref.at[h], i, _dot_tn(ckvn_mm, dkv_h))
            dcqn = dcqn + jnp.dot(dq_h, wuq_ref[h], preferred_element_type=F32)
            dckvn = dckvn + lax.dot_general(dkv_h, wukv_ref[h], (((1,), (1,)), ((), ())), preferred_element_type=F32)
            dkr = dkr + dk_heads[h]
        dkr_ref[...] = jnp.where((lane >= QK_NOPE) & (lane < QK_DIM), dkr, 0.0)
        dcq_ref[...] = _rms_bwd(dcqn * gqa_ref[...], cqh, rq)
        dckv_ref[...] = _rms_bwd(dckvn * gkva_ref[...], ckvh, rkv)
        _acc(dgqa_ref, i, jnp.sum(dcqn * cqh, axis=0, keepdims=True))
        _acc(dgkva_ref, i, jnp.sum(dckvn * ckvh, axis=0, keepdims=True))
        _acc(dgqn_ref, i, dgqn)
        _acc(dgkn_ref, i, dgkn)

    return _row_call(
        "mla_prep_bwd", body, T, tm, [cq, ckv, kr, pos, dqf, dkf, dvf], [g_qa, g_kva, g_qn, g_kn, w_uq, w_ukv],
        [(Q_RANK, F32), (KV_RANK, F32), (HEAD_PAD, F32)],
        [((1, Q_RANK), F32), ((1, KV_RANK), F32), ((1, HEAD_PAD), F32), ((1, HEAD_PAD), F32),
         ((MLA_HEADS, HEAD_PAD, Q_RANK), F32), ((MLA_HEADS, KV_RANK, HEAD_PAD), F32)], VMEM_LIMIT)


def _in_proj_bwd(x, dx1, dsecs, g_mix, w_in, T, tm):
    def body(i, x_ref, dx1_ref, *rest):
        d_refs, (g_ref, w_ref, dx_ref, dp_ref, dg_ref, dh_s) = rest[:len(SECTIONS)], rest[len(SECTIONS):]

        @pl.when(i == 0)
        def _():
            dg_ref[...] = jnp.zeros_like(dg_ref)

        g = g_ref[...]

        def join_and_cut(rows):
            pieces = [d_ref[rows, QK_NOPE:QK_DIM] if n == QK_ROPE else d_ref[rows, :]
                      for (_, n), d_ref in zip(COL_SECTIONS, d_refs)]
            dproj = jnp.concatenate(pieces, axis=1)
            for d in range(N_DEV):
                dp_ref[d, rows, :] = dproj[:, d * IN_BLOCK:(d + 1) * IN_BLOCK].astype(MM)

        _by_chunks(tm, join_and_cut)
        dh = jnp.dot(dp_ref[0], w_ref[0], preferred_element_type=F32)
        for d in range(1, N_DEV):
            dh = dh + jnp.dot(dp_ref[d], w_ref[d], preferred_element_type=F32)
        dh_s[...] = dh

        def norm_back(rows):
            xh, r = _rms(x_ref[rows, :])
            dh_c = dh_s[rows, :]
            dx_ref[rows, :] = dx1_ref[rows, :] + _rms_bwd(dh_c * g, xh, r)
            dg_ref[...] += _fold8(dh_c * xh)

        _by_chunks(tm, norm_back)

    in_specs = [pl.BlockSpec((tm, a.shape[1]), lambda i: (i, 0)) for a in [x, dx1, *dsecs]]
    in_specs += [pl.BlockSpec(g_mix.shape, lambda i: (0, 0)),
                 pl.BlockSpec(w_in.shape, lambda i: (0, 0, 0), pipeline_mode=pl.Buffered(1))]

    def kern(*refs):
        body(pl.program_id(0), *refs)

    return pl.pallas_call(
        kern, name="in_proj_bwd", grid=(T // tm,), in_specs=in_specs,
        out_specs=[pl.BlockSpec((tm, D_MODEL), lambda i: (i, 0)),
                   pl.BlockSpec((N_DEV, tm, IN_BLOCK), lambda i: (0, i, 0)),
                   pl.BlockSpec((8, D_MODEL), lambda i: (0, 0))],
        out_shape=[jax.ShapeDtypeStruct((T, D_MODEL), F32), jax.ShapeDtypeStruct((N_DEV, T, IN_BLOCK), MM),
                   jax.ShapeDtypeStruct((8, D_MODEL), F32)],
        scratch_shapes=[pltpu.VMEM((tm, D_MODEL), F32)],
        compiler_params=_cparams(("arbitrary",), VMEM_LIMIT),
    )(x, dx1, *dsecs, g_mix, w_in)


def _pick_block(n, cap):
    best = None
    for cand in range(128, min(n, cap) + 1, 128):
        if n % cand == 0:
            best = cand
    return n if best is None else best


def _pick_rows(n, cap):
    best = n
    for cand in range(8, min(n, cap) + 1, 8):
        if n % cand == 0:
            best = cand
    return best


def _matmul_tn(name, a, b):
    T, M = a.shape
    N = b.shape[1]
    bm, bk = _pick_block(M, 1408), min(512, T)
    bn = _pick_block(N, 2560)

    def body(a_ref, b_ref, c_ref):
        @pl.when(pl.program_id(2) == 0)
        def _():
            c_ref[...] = jnp.zeros_like(c_ref)

        c_ref[...] += _dot_tn(a_ref[...], b_ref[...])

    return pl.pallas_call(
        body, name=name, grid=(M // bm, N // bn, T // bk),
        in_specs=[pl.BlockSpec((bk, bm), lambda i, j, k: (k, i)), pl.BlockSpec((bk, bn), lambda i, j, k: (k, j))],
        out_specs=pl.BlockSpec((bm, bn), lambda i, j, k: (i, j)), out_shape=jax.ShapeDtypeStruct((M, N), F32),
        compiler_params=_cparams(("parallel", "parallel", "arbitrary"), VMEM_LIMIT),
    )(a, b)


def _matmul_tn_blocks(name, a, b):
    T, M = a.shape
    nd, _, c = b.shape
    bm, bk = _pick_block(M, 512), min(512, T)

    def body(a_ref, b_ref, c_ref):
        @pl.when(pl.program_id(1) == 0)
        def _():
            c_ref[...] = jnp.zeros_like(c_ref)

        a_blk = a_ref[...].astype(MM)
        for d in range(nd):
            c_ref[d] += _dot_tn(b_ref[d], a_blk)

    return pl.pallas_call(
        body, name=name, grid=(M // bm, T // bk),
        in_specs=[pl.BlockSpec((bk, bm), lambda i, k: (k, i)), pl.BlockSpec((nd, bk, c), lambda i, k: (0, k, 0))],
        out_specs=pl.BlockSpec((nd, c, bm), lambda i, k: (0, 0, i)),
        out_shape=jax.ShapeDtypeStruct((nd, c, M), F32),
        compiler_params=_cparams(("parallel", "arbitrary"), VMEM_LIMIT),
    )(a, b)


def _pad_gain(g, n):
    return jnp.pad(g.reshape(1, -1), ((0, 0), (0, n - g.shape[-1])))


GROUP_A = ("w_ffn_gate", "w_ffn_up", "w_ffn_down", "w_ple_gate", "w_ple_proj")
GROUP_B = ("w_branch", "w_out")
GROUP_C = ("w_in", "w_uq", "w_ukv")
EARLY = GROUP_C
LATE = GROUP_B + GROUP_A
TRANSPOSED = ("w_in", "w_uq", "w_ffn_gate", "w_ffn_up")


def _local_step(x, p, pos, tgt, small, big, late_blocks=None, core=None):
    T = x.shape[0]
    tm = min(ROW_TILE, T)
    w_in = big["w_in"]
    w_uq = jnp.pad(big["w_uq"], ((0, 0), (0, HEAD_PAD - QK_DIM), (0, 0)))
    w_ukv = big["w_ukv"]

    g_mix, g_qa, g_kva = small["mix_norm_g"], small["q_a_norm_g"], small["kv_a_norm_g"]
    g_qn, g_kn = _pad_gain(small["q_norm_g"], HEAD_PAD), _pad_gain(small["k_norm_g"], HEAD_PAD)
    g_out, g_ffn = small["hg_out_norm_g"], small["ffn_norm_g"]
    g_pg, g_post = small["ple_gate_norm_g"], small["ple_post_norm_g"]
    logits = small["hg_lb_logits"]
    lb = _lower_bound(logits)

    h, cq, ckv, kr, hq, hf, hi, hg, bg = _in_proj_fwd(x, g_mix, w_in, T, tm)
    qf, kf, vf = _mla_prep_fwd(cq, ckv, kr, pos, g_qa, g_kva, g_qn, g_kn, w_uq, w_ukv, T, tm)
    if late_blocks is None:
        attn, lse = _flash_fwd(qf, kf, vf, T)
    else:
        attn, lse, *late = _flash_fwd(qf, kf, vf, T, ag_blocks=[late_blocks[n] for n in LATE])
        big = {**big, **dict(zip(LATE, late))}
    w_branch = jnp.moveaxis(big["w_branch"].reshape(N_DEV, 2, HG_W, HEAD_PAD), 0, 2).reshape(2, HG_W, D_MODEL)
    w_bra = jnp.pad(w_branch[0].reshape(MLA_HEADS, V_DIM, D_MODEL),
                    ((0, 0), (0, HEAD_PAD - V_DIM), (0, 0))).reshape(MLA_HEADS * HEAD_PAD, D_MODEL)
    w_brb = w_branch[1]
    w_out = big["w_out"].reshape(D_MODEL, D_MODEL)
    w_g, w_u = big["w_ffn_gate"].reshape(FFN, D_MODEL), big["w_ffn_up"].reshape(FFN, D_MODEL)
    w_d = big["w_ffn_down"].reshape(FFN, D_MODEL)
    w_pg, w_pp = big["w_ple_gate"].reshape(D_MODEL, D_MODEL), big["w_ple_proj"]
    o, s0 = _hgrn_fwd(hq, hf, hi, lb, T)
    x1, ya, yb, m, rec = _merge_fwd(attn, o, hg, bg, x, g_out, w_bra, w_brb, w_out, T, tm)
    x2, gt, up, h2 = _ffn_fwd(x1, g_ffn, w_g, w_u, w_d, T, tm)
    dx2, loss_p, dg_post, dg_pg, d_pg, d_pp = _ple_loss(x2, p, tgt, g_pg, g_post, w_pg, w_pp, T, tm)
    dg_post, dg_pg = (jnp.sum(t, axis=0, keepdims=True) for t in (dg_post, dg_pg))

    grads, sibs, gots = {}, {}, {}

    def reduce_start(tag, names):
        if core is None:
            return ()
        got = _exchange_sibling("rs_sibling_" + tag, [grads[n] for n in names])
        sibs.update(zip(names, got))
        return [_chip_partial("rs_partial_" + n, grads[n], sibs[n], core) for n in names]

    dx1, a, dgt, dup, dg_ffn = _ffn_bwd(dx2, x1, gt, up, g_ffn, w_g, w_u, w_d, T, tm)
    dg_ffn = jnp.sum(dg_ffn, axis=0, keepdims=True)
    grads["w_ffn_gate"] = _matmul_tn("dw_gate", dgt, h2).reshape(N_DEV, -1, D_MODEL)
    grads["w_ffn_up"] = _matmul_tn("dw_up", dup, h2).reshape(N_DEV, -1, D_MODEL)
    grads["w_ffn_down"] = _matmul_tn("dw_down", a, dx2).reshape(N_DEV, -1, D_MODEL)
    grads["w_ple_gate"] = d_pg.reshape(N_DEV, -1, D_MODEL)
    grads["w_ple_proj"] = d_pp
    parts_a = reduce_start("a", GROUP_A)

    dattn, do, dhg, dbg, dg_out, d_out, d_bra, d_brb = _merge_bwd(
        dx1, ya, yb, bg, o, hg, attn, m, rec, g_out, w_bra, w_brb, w_out, T, tm)
    dg_out = jnp.sum(dg_out, axis=0, keepdims=True)
    d_bra = d_bra.reshape(N_DEV, MLA_HEADS, HEAD_PAD, HEAD_PAD)[:, :, :V_DIM].reshape(N_DEV, HG_W, HEAD_PAD)
    grads["w_branch"] = jnp.concatenate([d_bra, d_brb], axis=1)
    grads["w_out"] = d_out.reshape(N_DEV, -1, D_MODEL)
    parts_b = reduce_start("b", GROUP_B)

    dhq, dhf, dhi, dlb, *got_a = _hgrn_bwd(hq, hf, hi, do, s0, lb, T, xchg=parts_a)
    dqf, dkf, dvf, *got_b = _flash_bwd(qf, kf, vf, attn, dattn, lse, T, xchg=parts_b)
    (dcq, dckv, dkr, dg_qa, dg_kva, dg_qn, dg_kn, d_uq, d_ukv) = _mla_prep_bwd(
        cq, ckv, kr, pos, dqf, dkf, dvf, g_qa, g_kva, g_qn, g_kn, w_uq, w_ukv, T, tm)
    grad_x, dproj, dg_mix = _in_proj_bwd(x, dx1, [dcq, dckv, dkr, dhq, dhf, dhi, dhg, dbg], g_mix, w_in, T, tm)
    dg_mix = jnp.sum(dg_mix, axis=0, keepdims=True)
    grads["w_in"] = _matmul_tn_blocks("dw_in", h, dproj)
    grads["w_uq"] = d_uq[:, :QK_DIM]
    grads["w_ukv"] = d_ukv
    parts_c = reduce_start("c", GROUP_C)
    if core is not None:
        gots.update(zip(GROUP_A, got_a))
        gots.update(zip(GROUP_B, got_b))
        gots.update(zip(GROUP_C, _exchange_chips(parts_c)))

    dl0 = dlb * lb * (1.0 - lb)
    small_g = {
        "mix_norm_g": dg_mix, "q_a_norm_g": dg_qa, "kv_a_norm_g": dg_kva,
        "q_norm_g": dg_qn[:, :QK_DIM], "k_norm_g": dg_kn[:, :QK_DIM],
        "hg_lb_logits": jnp.concatenate([dl0, -dl0], axis=0), "hg_out_norm_g": dg_out,
        "ffn_norm_g": dg_ffn, "ple_gate_norm_g": dg_pg, "ple_post_norm_g": dg_post,
    }
    return loss_p, grad_x, small_g, grads, sibs, gots


def _lower_bound(logits):
    def body(l_ref, lb_ref):
        l = l_ref[...]
        mx = jnp.max(l, axis=0, keepdims=True)
        e = jnp.exp(l - mx)
        lb_ref[...] = e[0:1] / jnp.sum(e, axis=0, keepdims=True)

    return pl.pallas_call(body, name="lower_bound", out_shape=jax.ShapeDtypeStruct((1, HG_W), F32))(logits)


def _my_place():
    return lax.axis_index("x"), lax.axis_index("y"), lax.axis_index("c")


def _all_gather(name, blocks):
    n = len(blocks)

    def body(*refs):
        x_refs, out_refs, sems = refs[:n], refs[n:2 * n], refs[2 * n:]
        _ag_start(x_refs, out_refs, sems)
        _ag_finish(x_refs, out_refs, sems)

    any_spec = pl.BlockSpec(memory_space=pl.ANY)
    return pl.pallas_call(
        body, name=name, out_shape=_ag_out_shapes(blocks),
        in_specs=[any_spec] * n, out_specs=[any_spec] * n, scratch_shapes=_ag_sems(n),
    )(*blocks)


def _ag_out_shapes(blocks):
    return [jax.ShapeDtypeStruct((N_DEV,) + b.shape, b.dtype) for b in blocks]


def _ag_sems(n):
    return [pltpu.SemaphoreType.DMA((7 * n,)), pltpu.SemaphoreType.DMA((7 * n,)), pltpu.SemaphoreType.DMA((n,))]


def _ag_parts(x_refs, out_refs, sems):
    send_sems, recv_sems, local_sems = sems
    x, y, c = _my_place()
    me, sibling = (x, y, c), (x, y, 1 - c)
    chips = [(1 - x, y), (x, 1 - y), (1 - x, 1 - y)]
    n = len(x_refs)

    def copy(a, k, block, to, own=False):
        px, py, pc = block
        dst = out_refs[a].at[4 * px + 2 * py + pc]
        return pltpu.make_async_remote_copy(
            src_ref=x_refs[a] if own else dst, dst_ref=dst, send_sem=send_sems.at[7 * a + k],
            recv_sem=recv_sems.at[7 * a + k], device_id=to, device_id_type=MESH_ID)

    mine = [pltpu.make_async_copy(x_refs[a], out_refs[a].at[4 * x + 2 * y + c], local_sems.at[a]) for a in range(n)]
    first = []
    for a in range(n):
        first.append(copy(a, 0, me, sibling, own=True))
        first += [copy(a, 1 + j, me, (*chip, c), own=True) for j, chip in enumerate(chips)]
    return copy, mine, first, me, sibling, chips, c, n


def _ag_start(x_refs, out_refs, sems):
    _, mine, first, *_ = _ag_parts(x_refs, out_refs, sems)
    for cp in mine + first:
        cp.start()


def _ag_finish(x_refs, out_refs, sems):
    copy, mine, first, me, sibling, chips, c, n = _ag_parts(x_refs, out_refs, sems)
    passed = []
    for j, chip in enumerate(chips):
        for a in range(n):
            copy(a, 1 + j, (*chip, c), me).wait_recv()
            passed.append(copy(a, 4 + j, (*chip, c), sibling))
            passed[-1].start()
    for a in range(n):
        copy(a, 0, sibling, me).wait_recv()
    for j, chip in enumerate(chips):
        for a in range(n):
            copy(a, 4 + j, (*chip, 1 - c), me).wait_recv()
    for cp in first + passed:
        cp.wait_send()
    for cp in mine:
        cp.wait()


def _exchange_sibling(name, gs):
    n = len(gs)

    def body(*refs):
        g_refs, out_refs, (send_sems, recv_sems) = refs[:n], refs[n:2 * n], refs[2 * n:]
        x, y, c = _my_place()
        copies = [pltpu.make_async_remote_copy(
            src_ref=g_refs[a].at[2 * j + 1 - c], dst_ref=out_refs[a].at[j], send_sem=send_sems.at[4 * a + j],
            recv_sem=recv_sems.at[4 * a + j], device_id=(x, y, 1 - c), device_id_type=MESH_ID)
            for a in range(n) for j in range(4)]
        for cp in copies:
            cp.start()
        for cp in copies:
            cp.wait()

    any_spec = pl.BlockSpec(memory_space=pl.ANY)
    return pl.pallas_call(
        body, name=name, out_shape=[jax.ShapeDtypeStruct((4,) + g.shape[1:], g.dtype) for g in gs],
        in_specs=[any_spec] * n, out_specs=[any_spec] * n,
        scratch_shapes=[pltpu.SemaphoreType.DMA((4 * n,)), pltpu.SemaphoreType.DMA((4 * n,))],
    )(*gs)


def _chip_partial(name, g, got, c_idx):
    _, rows, cols = g.shape
    tr, tc = _tile_2d(rows, cols, 512)

    def body(c_ref, g_ref, got_ref, out_ref):
        out_ref[...] = (g_ref[...] + got_ref[...]).astype(MM)

    grid_spec = pltpu.PrefetchScalarGridSpec(
        num_scalar_prefetch=1, grid=(4, rows // tr, cols // tc),
        in_specs=[pl.BlockSpec((1, tr, tc), lambda j, i, k, c_ref: (2 * j + c_ref[0], i, k)),
                  pl.BlockSpec((1, tr, tc), lambda j, i, k, c_ref: (j, i, k))],
        out_specs=pl.BlockSpec((1, tr, tc), lambda j, i, k, c_ref: (j, i, k)))
    return pl.pallas_call(
        body, name=name, grid_spec=grid_spec, out_shape=jax.ShapeDtypeStruct((4, rows, cols), MM),
        compiler_params=_cparams(("parallel", "parallel", "parallel")),
    )(c_idx, g, got)


def _tile_2d(rows, cols, row_cap):
    if rows % 8 == 0:
        return _pick_rows(rows, row_cap), cols
    return rows, 256 if cols % 256 == 0 else cols


def _exchange_chips(parts):
    n = len(parts)

    def body(*refs):
        p_refs, out_refs, sems = refs[:n], refs[n:2 * n], refs[2 * n:]
        for cp in _xchips_copies(p_refs, out_refs, sems):
            cp.start()
        for cp in _xchips_copies(p_refs, out_refs, sems):
            cp.wait()

    any_spec = pl.BlockSpec(memory_space=pl.ANY)
    return pl.pallas_call(
        body, name="rs_chips", out_shape=_xchips_out_shapes(parts),
        in_specs=[any_spec] * n, out_specs=[any_spec] * n, scratch_shapes=_xchips_sems(n),
    )(*parts)


def _xchips_out_shapes(parts):
    return [jax.ShapeDtypeStruct((3,) + p.shape[1:], p.dtype) for p in parts]


def _xchips_sems(n):
    return [pltpu.SemaphoreType.DMA((3 * n,)), pltpu.SemaphoreType.DMA((3 * n,))]


def _xchips_copies(p_refs, out_refs, sems):
    send_sems, recv_sems = sems
    x, y, c = _my_place()
    chips = [(1 - x, y), (x, 1 - y), (1 - x, 1 - y)]
    return [pltpu.make_async_remote_copy(
        src_ref=p_refs[a].at[2 * px + py], dst_ref=out_refs[a].at[k], send_sem=send_sems.at[3 * a + k],
        recv_sem=recv_sems.at[3 * a + k], device_id=(px, py, c), device_id_type=MESH_ID)
        for a in range(len(p_refs)) for k, (px, py) in enumerate(chips)]


def _adamw_math(w, g, m, v):
    m = ADAM_B1 * m + (1.0 - ADAM_B1) * g
    v = ADAM_B2 * v + (1.0 - ADAM_B2) * jnp.square(g)
    m_hat = m / (1.0 - ADAM_B1 ** ADAM_STEP)
    v_hat = v / (1.0 - ADAM_B2 ** ADAM_STEP)
    delta = -ADAM_LR * (m_hat / (jnp.sqrt(v_hat) + ADAM_EPS) + ADAM_WD * w)
    return delta, m, v


def _sum_adamw(name, g, sib, got, w, m, v, slot_idx, chip_idx):
    _, rows, cols = g.shape
    tr, tc = _tile_2d(rows, cols, 256)

    def body(s_ref, j_ref, g_ref, sib_ref, got_ref, w_ref, m_ref, v_ref, go_ref, d_ref, m2_ref, v2_ref):
        grad = g_ref[0] + sib_ref[0]
        for k in range(3):
            grad = grad + got_ref[k].astype(F32)
        go_ref[...] = grad
        d_ref[...], m2_ref[...], v2_ref[...] = _adamw_math(w_ref[...], grad, m_ref[...], v_ref[...])

    flat = pl.BlockSpec((tr, tc), lambda i, k, s_ref, j_ref: (i, k))
    grid_spec = pltpu.PrefetchScalarGridSpec(
        num_scalar_prefetch=2, grid=(rows // tr, cols // tc),
        in_specs=[pl.BlockSpec((1, tr, tc), lambda i, k, s_ref, j_ref: (s_ref[0], i, k)),
                  pl.BlockSpec((1, tr, tc), lambda i, k, s_ref, j_ref: (j_ref[0], i, k)),
                  pl.BlockSpec((3, tr, tc), lambda i, k, s_ref, j_ref: (0, i, k)), flat, flat, flat],
        out_specs=[flat] * 4)
    return pl.pallas_call(
        body, name=name, grid_spec=grid_spec, out_shape=[jax.ShapeDtypeStruct((rows, cols), F32)] * 4,
        compiler_params=_cparams(("parallel", "parallel")),
    )(slot_idx, chip_idx, g, sib, got, w, m, v)


def _adamw_small(parts, w, m, v):
    rows = w.shape[0]

    def body(p_ref, w_ref, m_ref, v_ref, g_ref, d_ref, m2_ref, v2_ref):
        g = p_ref[0]
        for d in range(1, N_DEV):
            g = g + p_ref[d]
        g_ref[...] = g
        d_ref[...], m2_ref[...], v2_ref[...] = _adamw_math(w_ref[...], g, m_ref[...], v_ref[...])

    return pl.pallas_call(
        body, name="adamw_small", out_shape=[jax.ShapeDtypeStruct((rows, 128), F32)] * 4,
    )(parts, w, m, v)


BIG = ("w_in", "w_uq", "w_ukv", "w_branch", "w_out", "w_ffn_gate", "w_ffn_up", "w_ffn_down", "w_ple_gate", "w_ple_proj")
SMALL = (
    ("mix_norm_g", 1024), ("q_a_norm_g", 384), ("kv_a_norm_g", 256), ("q_norm_g", 96), ("k_norm_g", 96),
    ("hg_lb_logits", 1024), ("hg_out_norm_g", 128), ("ffn_norm_g", 1024), ("ple_gate_norm_g", 1024),
    ("ple_post_norm_g", 1024),
)
SMALL_ROWS = 56


def _pack_small(vals):
    rows = []
    for name, n in SMALL:
        v = vals[name].reshape(1, -1).astype(F32)
        rows.append(jnp.pad(v, ((0, 0), (0, (-n) % 128))).reshape(-1, 128))
    return jnp.concatenate(rows, axis=0)


def _unpack_small(packed, shapes):
    out, r = {}, 0
    for name, n in SMALL:
        k = (n + 127) // 128
        out[name] = packed[r:r + k].reshape(1, -1)[:, :n].reshape(shapes[name])
        r += k
    return out


_WEIGHTS = ["mix_norm_g", "w_in", "q_a_norm_g", "w_uq", "kv_a_norm_g", "w_ukv", "q_norm_g", "k_norm_g", "hg_lb_logits",
            "hg_out_norm_g", "w_branch", "w_out", "ffn_norm_g", "w_ffn_gate", "w_ffn_up", "w_ffn_down",
            "ple_gate_norm_g", "w_ple_gate", "w_ple_proj", "ple_post_norm_g"]


def _step(x, p, positions, tgt, w, m, v):
    small_names = [n for n, _ in SMALL]
    T = x.shape[1]
    px, py, pc = _my_place()
    as_idx = lambda t: jnp.reshape(t, (1,)).astype(jnp.int32)

    def two_d(n, t):
        t = t.reshape(-1, t.shape[-1])
        return t.T if n in TRANSPOSED else t

    def full_shape(n, t):
        return (t.T if n in TRANSPOSED else t).reshape(w[n].shape)

    blocks = {n: two_d(n, w[n]).astype(MM) for n in BIG}
    big = dict(zip(EARLY, _all_gather("ag_weights", [blocks[n] for n in EARLY])))
    small = {n: (w[n] if n == "hg_lb_logits" else w[n].reshape(1, -1)) for n in small_names}

    loss_p, grad_x, small_g, grads, sibs, gots = _local_step(
        x[0], p[0, 0], positions.reshape(T, 1), tgt[0], small, big, late_blocks=blocks, core=as_idx(pc))

    out_g, out_d, out_m, out_v = {}, {}, {}, {}
    for n in BIG:
        res = _sum_adamw("adamw_" + n, grads[n], sibs[n], gots[n], two_d(n, w[n]), two_d(n, m[n]), two_d(n, v[n]),
                         as_idx(4 * px + 2 * py + pc), as_idx(2 * px + py))
        out_g[n], out_d[n], out_m[n], out_v[n] = [full_shape(n, r) for r in res]

    packed_g = _pack_small(small_g)
    loss_row = jnp.concatenate([jnp.pad(jnp.sum(loss_p).reshape(1, 1), ((0, 0), (0, 127))),
                                jnp.zeros((SMALL_ROWS - packed_g.shape[0] - 1, 128), F32)], axis=0)
    parts = _all_gather("ag_small", [jnp.concatenate([packed_g, loss_row], axis=0)])[0]
    pad_rows = lambda t: jnp.pad(t, ((0, SMALL_ROWS - t.shape[0]), (0, 0)))
    sw = pad_rows(_pack_small({n: w[n] for n in small_names}))
    sm = pad_rows(_pack_small({n: m[n] for n in small_names}))
    sv = pad_rows(_pack_small({n: v[n] for n in small_names}))
    g_s, d_s, m_s, v_s = _adamw_small(parts, sw, sm, sv)
    shapes = {n: w[n].shape for n in small_names}
    n_packed = packed_g.shape[0]
    loss = g_s[n_packed, 0]
    for src, dst in ((g_s, out_g), (d_s, out_d), (m_s, out_m), (v_s, out_v)):
        dst.update(_unpack_small(src, shapes))

    outs = [loss, grad_x[None]]
    for table in (out_g, out_d, out_m, out_v):
        outs += [table[n] for n in _WEIGHTS]
    return tuple(outs)


def kernel(x, p, positions, mix_norm_g, w_in, q_a_norm_g, w_uq, kv_a_norm_g, w_ukv, q_norm_g, k_norm_g, hg_lb_logits, hg_out_norm_g, w_branch, w_out, ffn_norm_g, w_ffn_gate, w_ffn_up, w_ffn_down, ple_gate_norm_g, w_ple_gate, w_ple_proj, ple_post_norm_g, loss_target, m_mix_norm_g, m_w_in, m_q_a_norm_g, m_w_uq, m_kv_a_norm_g, m_w_ukv, m_q_norm_g, m_k_norm_g, m_hg_lb_logits, m_hg_out_norm_g, m_w_branch, m_w_out, m_ffn_norm_g, m_w_ffn_gate, m_w_ffn_up, m_w_ffn_down, m_ple_gate_norm_g, m_w_ple_gate, m_w_ple_proj, m_ple_post_norm_g, v_mix_norm_g, v_w_in, v_q_a_norm_g, v_w_uq, v_kv_a_norm_g, v_w_ukv, v_q_norm_g, v_k_norm_g, v_hg_lb_logits, v_hg_out_norm_g, v_w_branch, v_w_out, v_ffn_norm_g, v_w_ffn_gate, v_w_ffn_up, v_w_ffn_down, v_ple_gate_norm_g, v_w_ple_gate, v_w_ple_proj, v_ple_post_norm_g):
    w = dict(mix_norm_g=mix_norm_g, w_in=w_in, q_a_norm_g=q_a_norm_g, w_uq=w_uq, kv_a_norm_g=kv_a_norm_g, w_ukv=w_ukv,
             q_norm_g=q_norm_g, k_norm_g=k_norm_g, hg_lb_logits=hg_lb_logits, hg_out_norm_g=hg_out_norm_g,
             w_branch=w_branch, w_out=w_out, ffn_norm_g=ffn_norm_g, w_ffn_gate=w_ffn_gate, w_ffn_up=w_ffn_up,
             w_ffn_down=w_ffn_down, ple_gate_norm_g=ple_gate_norm_g, w_ple_gate=w_ple_gate, w_ple_proj=w_ple_proj,
             ple_post_norm_g=ple_post_norm_g)
    m = dict(mix_norm_g=m_mix_norm_g, w_in=m_w_in, q_a_norm_g=m_q_a_norm_g, w_uq=m_w_uq, kv_a_norm_g=m_kv_a_norm_g,
             w_ukv=m_w_ukv, q_norm_g=m_q_norm_g, k_norm_g=m_k_norm_g, hg_lb_logits=m_hg_lb_logits,
             hg_out_norm_g=m_hg_out_norm_g, w_branch=m_w_branch, w_out=m_w_out, ffn_norm_g=m_ffn_norm_g,
             w_ffn_gate=m_w_ffn_gate, w_ffn_up=m_w_ffn_up, w_ffn_down=m_w_ffn_down,
             ple_gate_norm_g=m_ple_gate_norm_g, w_ple_gate=m_w_ple_gate, w_ple_proj=m_w_ple_proj,
             ple_post_norm_g=m_ple_post_norm_g)
    v = dict(mix_norm_g=v_mix_norm_g, w_in=v_w_in, q_a_norm_g=v_q_a_norm_g, w_uq=v_w_uq, kv_a_norm_g=v_kv_a_norm_g,
             w_ukv=v_w_ukv, q_norm_g=v_q_norm_g, k_norm_g=v_k_norm_g, hg_lb_logits=v_hg_lb_logits,
             hg_out_norm_g=v_hg_out_norm_g, w_branch=v_w_branch, w_out=v_w_out, ffn_norm_g=v_ffn_norm_g,
             w_ffn_gate=v_w_ffn_gate, w_ffn_up=v_w_ffn_up, w_ffn_down=v_w_ffn_down,
             ple_gate_norm_g=v_ple_gate_norm_g, w_ple_gate=v_w_ple_gate, w_ple_proj=v_w_ple_proj,
             ple_post_norm_g=v_ple_post_norm_g)
    return _step(x, p, positions, loss_target, w, m, v)
```

```python
import functools

import jax
import jax.numpy as jnp
import numpy as np
from jax import lax
from jax.experimental import pallas as pl
from jax.experimental.pallas import tpu as pltpu

F32 = jnp.float32
MM = jnp.bfloat16
HI = lax.Precision.HIGHEST
MESH_ID = pl.DeviceIdType.MESH

D_MODEL = 1024
N_DEV = 8
MLA_HEADS = 8
QK_NOPE = 64
QK_ROPE = 32
QK_DIM = 96
V_DIM = 64
HEAD_PAD = 128
Q_RANK = 384
KV_RANK = 256
ROPE_BASE = 10000.0
HG_HEADS = 4
HG_DIM = 128
HG_W = 512
HG_CHUNK = 64
HG_SUB = 16
FFN = 2816
PLE = 256
EPS = 1e-6
ATT_SCALE = QK_DIM ** -0.5
NEG = -1e30

ADAM_LR = 0.001
ADAM_B1 = 0.9
ADAM_B2 = 0.999
ADAM_EPS = 1e-08
ADAM_WD = 0.01
ADAM_STEP = 10

SEC_CQ = (0, 384)
SEC_CKV = (384, 256)
SEC_KR = (640, 128)
SEC_HQ = (768, 512)
SEC_HF = (1280, 512)
SEC_HI = (1792, 512)
SEC_HG = (2304, 512)
SEC_BG = (2816, 2048)
IN_PAD = 4864
SECTIONS = (SEC_CQ, SEC_CKV, SEC_KR, SEC_HQ, SEC_HF, SEC_HI, SEC_HG, SEC_BG)
COL_SECTIONS = ((0, 384), (384, 256), (640, 32), (672, 512), (1184, 512), (1696, 512), (2208, 512), (2720, 2048))
IN_COLS = 4768
IN_BLOCK = IN_COLS // 8

VMEM_LIMIT = 58 * 1024 * 1024
ROW_TILE = 256
ATT_TILE = 1024
ATT_HEADS = 2
HG_BLOCK = 512


def _dot(a, b):
    return jnp.dot(a.astype(MM), b.astype(MM), preferred_element_type=F32)


def _dot_nt(a, b):
    return lax.dot_general(a.astype(MM), b.astype(MM), (((1,), (1,)), ((), ())), preferred_element_type=F32)


def _dot_tn(a, b):
    return lax.dot_general(a.astype(MM), b.astype(MM), (((0,), (0,)), ((), ())), preferred_element_type=F32)


def _dot_hi(a, b):
    return jnp.dot(a, b, preferred_element_type=F32, precision=HI)


def _sigmoid(x):
    return 1.0 / (1.0 + jnp.exp(-x))


def _rms(x, n=None):
    n = x.shape[-1] if n is None else n
    r = lax.rsqrt(jnp.sum(x * x, axis=-1, keepdims=True) * (1.0 / n) + EPS)
    return x * r, r


def _rms_bwd(dxh, xh, r, n=None):
    n = xh.shape[-1] if n is None else n
    return r * (dxh - xh * (jnp.sum(dxh * xh, axis=-1, keepdims=True) * (1.0 / n)))


def _rope_tables(pos, tm):
    lane = lax.broadcasted_iota(jnp.int32, (tm, HEAD_PAD), 1)
    idx = jnp.where(lane < QK_NOPE + QK_ROPE // 2, lane - QK_NOPE, lane - QK_NOPE - QK_ROPE // 2)
    inv = jnp.exp(idx.astype(F32) * (-np.log(ROPE_BASE) * 2.0 / QK_ROPE))
    ang = pos.astype(F32) * inv
    in_rope = (lane >= QK_NOPE) & (lane < QK_DIM)
    first = lane < QK_NOPE + QK_ROPE // 2
    cos_t = jnp.where(in_rope, jnp.cos(ang), 1.0)
    sin_t = jnp.where(in_rope, jnp.where(first, -jnp.sin(ang), jnp.sin(ang)), 0.0)
    return cos_t, sin_t, (first, in_rope)


def _rope_swap(x, halves):
    first, in_rope = halves
    half = QK_ROPE // 2
    return jnp.where(in_rope, jnp.where(first, pltpu.roll(x, HEAD_PAD - half, 1), pltpu.roll(x, half, 1)), 0.0)


def _cparams(sem, vmem=None):
    return pltpu.CompilerParams(dimension_semantics=sem, vmem_limit_bytes=vmem)


def _row_call(name, body, T, tm, row_ins, full_ins, row_outs, acc_outs, vmem=None, scratch=()):
    def kern(*refs):
        body(pl.program_id(0), *refs)

    in_specs = [pl.BlockSpec((tm, a.shape[1]), lambda i: (i, 0)) for a in row_ins]
    in_specs += [pl.BlockSpec(a.shape, lambda i, nd=a.ndim: (0,) * nd, pipeline_mode=pl.Buffered(1)) for a in full_ins]
    out_specs = [pl.BlockSpec((tm, n), lambda i: (i, 0)) for n, _ in row_outs]
    out_specs += [pl.BlockSpec(s, lambda i, nd=len(s): (0,) * nd) for s, _ in acc_outs]
    out_shape = [jax.ShapeDtypeStruct((T, n), dt) for n, dt in row_outs]
    out_shape += [jax.ShapeDtypeStruct(s, dt) for s, dt in acc_outs]
    return pl.pallas_call(
        kern, name=name, grid=(T // tm,), in_specs=in_specs, out_specs=out_specs, out_shape=out_shape,
        scratch_shapes=list(scratch), compiler_params=_cparams(("arbitrary",), vmem),
    )(*row_ins, *full_ins)


FFN_HALVES = (slice(0, FFN // 2), slice(FFN // 2, FFN))
ROW_CHUNK = 16
CHUNK_UNROLL = True


def _by_chunks(tm, fn):
    def step(c, carry):
        fn(pl.ds(pl.multiple_of(c * ROW_CHUNK, ROW_CHUNK), ROW_CHUNK))
        return carry

    lax.fori_loop(0, tm // ROW_CHUNK, step, 0, unroll=CHUNK_UNROLL)


def _fold8(x):
    return x[:8] + x[8:]


def _acc(ref, i, val):
    @pl.when(i == 0)
    def _():
        ref[...] = val

    @pl.when(i != 0)
    def _():
        ref[...] += val


def _in_proj_fwd(x, g_mix, w_in, T, tm):
    def body(i, x_ref, g_ref, w_ref, h_ref, *rest):
        outs, pj_s = rest[:-1], rest[-1]
        g = g_ref[...]

        def norm(rows):
            h_ref[rows, :] = (_rms(x_ref[rows, :])[0] * g).astype(MM)

        _by_chunks(tm, norm)
        for d in range(N_DEV):
            pj_s[d] = _dot_nt(h_ref[...], w_ref[d])

        def join_and_cut(rows):
            proj = jnp.concatenate([pj_s[d, rows, :] for d in range(N_DEV)], axis=1)
            for (s, n), o_ref in zip(COL_SECTIONS, outs):
                if n == QK_ROPE:
                    o_ref[rows, :] = jnp.concatenate(
                        [jnp.zeros((ROW_CHUNK, QK_NOPE), F32), proj[:, s:s + n],
                         jnp.zeros((ROW_CHUNK, HEAD_PAD - QK_DIM), F32)], axis=1)
                else:
                    o_ref[rows, :] = proj[:, s:s + n]

        _by_chunks(tm, join_and_cut)

    row_outs = [(D_MODEL, MM)] + [(n, F32) for _, n in SECTIONS]
    return _row_call("in_proj_fwd", body, T, tm, [x], [g_mix, w_in], row_outs, [], VMEM_LIMIT,
                     scratch=[pltpu.VMEM((N_DEV, tm, IN_BLOCK), F32)])


def _mla_heads_fwd(raw, g_pad, cos_t, sin_t, first):
    outs, saved = [], []
    for h in range(MLA_HEADS):
        xh, r = _rms(raw[:, h * HEAD_PAD:(h + 1) * HEAD_PAD], QK_DIM)
        y = xh * g_pad
        outs.append(y * cos_t + _rope_swap(y, first) * sin_t)
        saved.append((xh, r))
    return outs, saved


def _mla_raw_heads(cqn, ckvn, kr, wuq_ref, wukv_ref, tm):
    lane = lax.broadcasted_iota(jnp.int32, (tm, HEAD_PAD), 1)
    nope = lane < QK_NOPE
    one_lane = jnp.where(lane == V_DIM, 1.0, 0.0)
    qs, ks, vs = [], [], []
    for h in range(MLA_HEADS):
        qs.append(_dot_nt(cqn, wuq_ref[h]))
        kv = _dot(ckvn, wukv_ref[h])
        ks.append(jnp.where(nope, kv, kr))
        vs.append(jnp.where(nope, pltpu.roll(kv, V_DIM, 1), one_lane))
    return jnp.concatenate(qs, axis=1), jnp.concatenate(ks, axis=1), jnp.concatenate(vs, axis=1)


def _mla_prep_fwd(cq, ckv, kr, pos, g_qa, g_kva, g_qn, g_kn, w_uq, w_ukv, T, tm):
    def body(i, cq_ref, ckv_ref, kr_ref, pos_ref, gqa_ref, gkva_ref, gqn_ref, gkn_ref, wuq_ref, wukv_ref,
             q_ref, k_ref, v_ref):
        cos_t, sin_t, first = _rope_tables(pos_ref[...], tm)
        cqn = _rms(cq_ref[...])[0] * gqa_ref[...]
        ckvn = _rms(ckv_ref[...])[0] * gkva_ref[...]
        q_raw, k_raw, v = _mla_raw_heads(cqn, ckvn, kr_ref[...], wuq_ref, wukv_ref, tm)
        qs, _ = _mla_heads_fwd(q_raw, gqn_ref[...], cos_t, sin_t, first)
        ks, _ = _mla_heads_fwd(k_raw, gkn_ref[...], cos_t, sin_t, first)
        q_ref[...] = (jnp.concatenate(qs, axis=1) * ATT_SCALE).astype(MM)
        k_ref[...] = jnp.concatenate(ks, axis=1).astype(MM)
        v_ref[...] = v.astype(MM)

    w = MLA_HEADS * HEAD_PAD
    return _row_call("mla_prep_fwd", body, T, tm, [cq, ckv, kr, pos], [g_qa, g_kva, g_qn, g_kn, w_uq, w_ukv],
                     [(w, MM), (w, MM), (w, MM)], [])


def _causal_pairs(n, by_query):
    if by_query:
        pairs = [(q, k) for q in range(n) for k in range(q + 1)]
    else:
        pairs = [(q, k) for k in range(n) for q in range(k, n)]
    return np.array([p[0] for p in pairs], np.int32), np.array([p[1] for p in pairs], np.int32)


def _flash_fwd(qf, kf, vf, T, ag_blocks=()):
    tq = min(ATT_TILE, T)
    nq = T // tq

    qi_tab, ki_tab = _causal_pairs(nq, by_query=True)

    hp = ATT_HEADS

    n_ag = len(ag_blocks)
    n_heads, n_pairs = MLA_HEADS // hp, len(qi_tab)

    def body(qi_ref, ki_ref, q_ref, k_ref, v_ref, *rest):
        ag_in, (o_ref, lse_ref), rest = rest[:n_ag], rest[n_ag:n_ag + 2], rest[n_ag + 2:]
        ag_out, (m_s, acc_s), ag_sems = rest[:n_ag], rest[n_ag:n_ag + 2], rest[n_ag + 2:]
        t = pl.program_id(1)
        qi, ki = qi_ref[t], ki_ref[t]
        if n_ag:
            @pl.when((pl.program_id(0) == 0) & (t == 0))
            def _():
                _ag_start(ag_in, ag_out, ag_sems)

        @pl.when(ki == 0)
        def _():
            m_s[...] = jnp.full_like(m_s, NEG)
            acc_s[...] = jnp.zeros_like(acc_s)

        def step(masked):
            for hh in range(hp):
                hs = slice(hh * HEAD_PAD, (hh + 1) * HEAD_PAD)
                s_t = _dot_nt(k_ref[:, hs], q_ref[:, hs])
                if masked:
                    key = lax.broadcasted_iota(jnp.int32, (tq, tq), 0)
                    qry = lax.broadcasted_iota(jnp.int32, (tq, tq), 1)
                    s_t = jnp.where(key <= qry, s_t, NEG)
                m_old = m_s[hh]
                m_new = jnp.maximum(m_old, jnp.max(s_t, axis=0, keepdims=True))
                p_t = jnp.exp(s_t - m_new)
                acc_s[hh] = jnp.exp(m_old - m_new) * acc_s[hh] + _dot_tn(v_ref[:, hs], p_t)
                m_s[hh] = m_new

        @pl.when(ki < qi)
        def _():
            step(False)

        @pl.when(ki == qi)
        def _():
            step(True)
            real = lax.broadcasted_iota(jnp.int32, (HEAD_PAD, tq), 0) < V_DIM
            for hh in range(hp):
                hs = slice(hh * HEAD_PAD, (hh + 1) * HEAD_PAD)
                acc = acc_s[hh]
                l = acc[V_DIM:V_DIM + 1]
                o_ref[:, hs] = jnp.where(real, acc / l, 0.0).T
                lse_ref[:, hs] = jnp.broadcast_to(m_s[hh] + jnp.log(l), (HEAD_PAD, tq)).T

        if n_ag:
            @pl.when((pl.program_id(0) == n_heads - 1) & (t == n_pairs - 1))
            def _():
                _ag_finish(ag_in, ag_out, ag_sems)

    q_spec = pl.BlockSpec((tq, hp * HEAD_PAD), lambda h, t, qi_ref, ki_ref: (qi_ref[t], h))
    kv_spec = pl.BlockSpec((tq, hp * HEAD_PAD), lambda h, t, qi_ref, ki_ref: (ki_ref[t], h))
    any_spec = pl.BlockSpec(memory_space=pl.ANY)
    grid_spec = pltpu.PrefetchScalarGridSpec(
        num_scalar_prefetch=2, grid=(n_heads, n_pairs),
        in_specs=[q_spec, kv_spec, kv_spec] + [any_spec] * n_ag, out_specs=[q_spec, q_spec] + [any_spec] * n_ag,
        scratch_shapes=[pltpu.VMEM((hp, 1, tq), F32), pltpu.VMEM((hp, HEAD_PAD, tq), F32)]
        + (_ag_sems(n_ag) if n_ag else []))
    return pl.pallas_call(
        body, name="flash_fwd", grid_spec=grid_spec,
        out_shape=[jax.ShapeDtypeStruct((T, MLA_HEADS * HEAD_PAD), F32)] * 2 + _ag_out_shapes(ag_blocks),
        compiler_params=_cparams(("arbitrary", "arbitrary")),
    )(jnp.asarray(qi_tab), jnp.asarray(ki_tab), qf, kf, vf, *ag_blocks)


def _hg_gates(hf, lb):
    sg = _sigmoid(hf)
    f = lb + (1.0 - lb) * sg
    return sg, f, jnp.log(f), 1.0 - f


def _tri(n, lower):
    r = lax.broadcasted_iota(jnp.int32, (n, n), 0)
    c = lax.broadcasted_iota(jnp.int32, (n, n), 1)
    return jnp.where((c <= r) if lower else (c >= r), 1.0, 0.0).astype(F32)


def _hg_intra(q, k, b):
    C, S = HG_CHUNK, HG_SUB
    row_c = lax.broadcasted_iota(jnp.int32, (C, HG_DIM), 0)
    row_s = lax.broadcasted_iota(jnp.int32, (S, HG_DIM), 0)
    lane_c = lax.broadcasted_iota(jnp.int32, (S, C), 1)
    blocks, saved = [], []
    for blk in range(C // S):
        lo = blk * S
        q_b, k_b, b_b = q[lo:lo + S], k[lo:lo + S], b[lo:lo + S]
        a_b = jnp.zeros((S, C), F32)
        for j in range(S):
            w = jnp.exp(jnp.where(row_s >= j, b_b - b_b[j:j + 1], NEG))
            col = jnp.sum(q_b * (k_b[j:j + 1] * w), axis=1, keepdims=True)
            a_b = jnp.where(lane_c == lo + j, col, a_b)
        if blk > 0:
            ref = b[lo - 1:lo]
            q_e = jnp.exp(b_b - ref)
            q_t = q_b * q_e
            k_e = jnp.exp(jnp.where(row_c < lo, ref - b, NEG))
            a_b = a_b + _dot_nt(q_t, k * k_e)
            saved.append((q_t, k_e, q_e))
        else:
            saved.append(None)
        blocks.append(a_b)
    return jnp.concatenate(blocks, axis=0), saved


def _hgrn_fwd(hq, hf, hi, lb, T):
    rb = min(HG_BLOCK, T)
    ncb = rb // HG_CHUNK

    def body(hq_ref, hf_ref, hi_ref, lb_ref, o_ref, s0_ref, st_ref):
        @pl.when(pl.program_id(0) == 0)
        def _():
            st_ref[...] = jnp.zeros_like(st_ref)

        tril = _tri(HG_CHUNK, True)

        def chunk(c, carry):
            rows = pl.ds(pl.multiple_of(c * HG_CHUNK, HG_CHUNK), HG_CHUNK)
            _, _, logf, kk = _hg_gates(hf_ref[rows, :], lb_ref[...])
            b = _dot_hi(tril, logf)
            q_all, v_all = hq_ref[rows, :], hi_ref[rows, :]
            outs = []
            for h in range(HG_HEADS):
                ls = slice(h * HG_DIM, (h + 1) * HG_DIM)
                q, k, v, bh = q_all[:, ls], kk[:, ls], v_all[:, ls], b[:, ls]
                st = st_ref[h]
                s0_ref[c, h * HG_DIM:(h + 1) * HG_DIM, :] = st
                b_end = bh[HG_CHUNK - 1:HG_CHUNK]
                a, _ = _hg_intra(q, k, bh)
                outs.append(_dot_nt(q * jnp.exp(bh), st) + _dot(a, v))
                st_ref[h] = st * jnp.exp(b_end) + _dot_tn(v, k * jnp.exp(b_end - bh))
            o_ref[rows, :] = jnp.concatenate(outs, axis=1)
            return carry

        lax.fori_loop(0, ncb, chunk, 0)

    row = pl.BlockSpec((rb, HG_W), lambda i: (i, 0))
    return pl.pallas_call(
        body, name="hgrn_fwd", grid=(T // rb,),
        in_specs=[row, row, row, pl.BlockSpec((1, HG_W), lambda i: (0, 0))],
        out_specs=[row, pl.BlockSpec((ncb, HG_W, HG_DIM), lambda i: (i, 0, 0))],
        out_shape=[jax.ShapeDtypeStruct((T, HG_W), F32), jax.ShapeDtypeStruct((T // HG_CHUNK, HG_W, HG_DIM), F32)],
        scratch_shapes=[pltpu.VMEM((HG_HEADS, HG_DIM, HG_DIM), F32)],
        compiler_params=_cparams(("arbitrary",)),
    )(hq, hf, hi, lb)


def _hgrn_bwd(hq, hf, hi, do, s0, lb, T, xchg=()):
    rb = min(HG_BLOCK, T)
    ncb = rb // HG_CHUNK
    nb = T // rb
    C, S = HG_CHUNK, HG_SUB
    n_x = len(xchg)

    def body(hq_ref, hf_ref, hi_ref, do_ref, s0_ref, lb_ref, *rest):
        x_in, (dq_ref, df_ref, dv_ref, dlb_ref), rest = rest[:n_x], rest[n_x:n_x + 4], rest[n_x + 4:]
        x_out, dst_ref, x_sems = rest[:n_x], rest[n_x], rest[n_x + 1:]

        @pl.when(pl.program_id(0) == 0)
        def _():
            dst_ref[...] = jnp.zeros_like(dst_ref)
            dlb_ref[...] = jnp.zeros_like(dlb_ref)
            for cp in _xchips_copies(x_in, x_out, x_sems) if n_x else ():
                cp.start()

        tril, triu = _tri(C, True), _tri(C, False)
        row_cc = lax.broadcasted_iota(jnp.int32, (C, C), 0)
        col_cc = lax.broadcasted_iota(jnp.int32, (C, C), 1)
        row_s = lax.broadcasted_iota(jnp.int32, (S, HG_DIM), 0)
        lane_sc = lax.broadcasted_iota(jnp.int32, (S, C), 1)
        last_row = lax.broadcasted_iota(jnp.int32, (C, HG_DIM), 0) == C - 1
        lb_v = lb_ref[...]

        def chunk(cc, carry):
            c = ncb - 1 - cc
            rows = pl.ds(pl.multiple_of(c * C, C), C)
            hf_c = hf_ref[rows, :]
            sg, f, logf, kk = _hg_gates(hf_c, lb_v)
            b = _dot_hi(tril, logf)
            q_all, v_all, do_all = hq_ref[rows, :], hi_ref[rows, :], do_ref[rows, :]
            dq_o, dk_o, dv_o, db_o = [], [], [], []
            for h in range(HG_HEADS):
                ls = slice(h * HG_DIM, (h + 1) * HG_DIM)
                q, k, v, bh, d_o = q_all[:, ls], kk[:, ls], v_all[:, ls], b[:, ls], do_all[:, ls]
                st0 = s0_ref[c, h * HG_DIM:(h + 1) * HG_DIM, :]
                dst = dst_ref[h]
                b_end = bh[C - 1:C]
                e_b, e_end = jnp.exp(bh), jnp.exp(b_end)
                e_rem = jnp.exp(b_end - bh)
                qe, kd = q * e_b, k * e_rem
                st_end = st0 * e_end + _dot_tn(v, kd)
                a, saved = _hg_intra(q, k, bh)
                d_a = jnp.where(col_cc <= row_cc, _dot_nt(d_o, v), 0.0)
                dv = _dot_tn(a, d_o) + _dot_nt(kd, dst)
                dq = e_b * _dot(d_o, st0)
                dk = e_rem * _dot(v, dst)
                dq_blocks, dk_diag = [], []
                for blk in range(C // S):
                    lo = blk * S
                    q_b, k_b, b_b = q[lo:lo + S], k[lo:lo + S], bh[lo:lo + S]
                    da_b = d_a[lo:lo + S]
                    dq_b = jnp.zeros((S, HG_DIM), F32)
                    dk_b = jnp.zeros((S, HG_DIM), F32)
                    for j in range(S):
                        w = jnp.exp(jnp.where(row_s >= j, b_b - b_b[j:j + 1], NEG))
                        col = jnp.sum(jnp.where(lane_sc == lo + j, da_b, 0.0), axis=1, keepdims=True)
                        dq_b = dq_b + col * (k_b[j:j + 1] * w)
                        dk_row = jnp.sum(col * (q_b * w), axis=0, keepdims=True)
                        dk_b = jnp.where(row_s == j, dk_row, dk_b)
                    if blk > 0:
                        q_t, k_e, q_e = saved[blk]
                        da_off = jnp.where(lane_sc < lo, da_b, 0.0)
                        dq_b = dq_b + _dot(da_off, k * k_e) * q_e
                        dk = dk + _dot_tn(da_off, q_t) * k_e
                    dq_blocks.append(dq_b)
                    dk_diag.append(dk_b)
                dq = dq + jnp.concatenate(dq_blocks, axis=0)
                dk = dk + jnp.concatenate(dk_diag, axis=0)
                extra = jnp.sum(dst * st_end, axis=0, keepdims=True)
                db_o.append(q * dq - k * dk + jnp.where(last_row, extra, 0.0))
                dst_ref[h] = dst * e_end + _dot_tn(d_o, qe)
                dq_o.append(dq)
                dk_o.append(dk)
                dv_o.append(dv)
            dlogf = _dot_hi(triu, jnp.concatenate(db_o, axis=1))
            d_f = dlogf / f - jnp.concatenate(dk_o, axis=1)
            dq_ref[rows, :] = jnp.concatenate(dq_o, axis=1).astype(MM)
            dv_ref[rows, :] = jnp.concatenate(dv_o, axis=1).astype(MM)
            df_ref[rows, :] = (d_f * (1.0 - lb_v) * sg * (1.0 - sg)).astype(MM)
            dlb_ref[...] += jnp.sum(d_f * (1.0 - sg), axis=0, keepdims=True)
            return carry

        lax.fori_loop(0, ncb, chunk, 0)

        if n_x:
            @pl.when(pl.program_id(0) == nb - 1)
            def _():
                for cp in _xchips_copies(x_in, x_out, x_sems):
                    cp.wait()

    row = pl.BlockSpec((rb, HG_W), lambda i: (nb - 1 - i, 0))
    one = pl.BlockSpec((1, HG_W), lambda i: (0, 0))
    any_spec = pl.BlockSpec(memory_space=pl.ANY)
    return pl.pallas_call(
        body, name="hgrn_bwd", grid=(nb,),
        in_specs=[row, row, row, row, pl.BlockSpec((ncb, HG_W, HG_DIM), lambda i: (nb - 1 - i, 0, 0)), one]
        + [any_spec] * n_x,
        out_specs=[row, row, row, one] + [any_spec] * n_x,
        out_shape=[jax.ShapeDtypeStruct((T, HG_W), MM)] * 3 + [jax.ShapeDtypeStruct((1, HG_W), F32)]
        + _xchips_out_shapes(xchg),
        scratch_shapes=[pltpu.VMEM((HG_HEADS, HG_DIM, HG_DIM), F32)] + (_xchips_sems(n_x) if n_x else []),
        compiler_params=_cparams(("arbitrary",)),
    )(hq, hf, hi, do, s0, lb, *xchg)


def _silu_parts(x):
    sg = _sigmoid(x)
    return x * sg, sg * (1.0 + x * (1.0 - sg))


def _merge_fwd(attn, o, hg, bg, x, g_out, w_bra, w_brb, w_out, T, tm):
    def body(i, attn_ref, o_ref, hg_ref, bg_ref, x_ref, g_ref, wa_ref, wb_ref, wo_ref,
             x1_ref, ya_ref, yb_ref, m_ref, rec_ref):
        g = g_ref[...]

        def recurrent_out(rows):
            for h in range(HG_HEADS):
                ls = slice(h * HG_DIM, (h + 1) * HG_DIM)
                rec_ref[rows, ls] = (_rms(o_ref[rows, ls])[0] * g * _silu_parts(hg_ref[rows, ls])[0]).astype(MM)

        _by_chunks(tm, recurrent_out)
        ya_ref[...] = _dot(attn_ref[...], wa_ref[...])
        yb_ref[...] = jnp.dot(rec_ref[...], wb_ref[...], preferred_element_type=F32)

        def gate(rows):
            m_ref[rows, :] = (_sigmoid(bg_ref[rows, :D_MODEL]) * ya_ref[rows, :]
                              + _sigmoid(bg_ref[rows, D_MODEL:]) * yb_ref[rows, :]).astype(MM)

        _by_chunks(tm, gate)
        x1_ref[...] = x_ref[...] + jnp.dot(m_ref[...], wo_ref[...], preferred_element_type=F32)

    return _row_call("merge_fwd", body, T, tm, [attn, o, hg, bg, x], [g_out, w_bra, w_brb, w_out],
                     [(D_MODEL, F32), (D_MODEL, F32), (D_MODEL, F32), (D_MODEL, MM), (HG_W, MM)], [], VMEM_LIMIT)


def _ffn_fwd(x1, g_ffn, w_g, w_u, w_d, T, tm):
    def body(i, x1_ref, g_ref, wg_ref, wu_ref, wd_ref, x2_ref, gt_ref, up_ref, h2_ref, a_s):
        g = g_ref[...]

        def norm(rows):
            h2_ref[rows, :] = (_rms(x1_ref[rows, :])[0] * g).astype(MM)

        _by_chunks(tm, norm)
        gt_ref[...] = _dot_nt(h2_ref[...], wg_ref[...])
        up_ref[...] = _dot_nt(h2_ref[...], wu_ref[...])

        def act(rows):
            for cs in FFN_HALVES:
                a_s[rows, cs] = (_silu_parts(gt_ref[rows, cs])[0] * up_ref[rows, cs]).astype(MM)

        _by_chunks(tm, act)
        x2_ref[...] = x1_ref[...] + jnp.dot(a_s[...], wd_ref[...], preferred_element_type=F32)

    return _row_call("ffn_fwd", body, T, tm, [x1], [g_ffn, w_g, w_u, w_d],
                     [(D_MODEL, F32), (FFN, F32), (FFN, F32), (D_MODEL, MM)], [], VMEM_LIMIT,
                     scratch=[pltpu.VMEM((tm, FFN), MM)])


def _ple_loss(x2, p, tgt, g_pg, g_post, w_pg, w_pp, T, tm):
    def body(i, x2_ref, p_ref, t_ref, gpg_ref, gpo_ref, wpg_ref, wpp_ref,
             dx2_ref, loss_ref, dgpo_ref, dgpg_ref, dwpg_ref, dwpp_ref, u_s, n3_s, z_s, dz_s, du_s, dy_s, dn3_s):
        @pl.when(i == 0)
        def _():
            for ref in (loss_ref, dgpo_ref, dgpg_ref, dwpg_ref, dwpp_ref):
                ref[...] = jnp.zeros_like(ref)

        gpg, gpo = gpg_ref[...], gpo_ref[...]
        p_mm = p_ref[...].astype(MM)
        for d in range(N_DEV):
            u_s[:, d * HEAD_PAD:(d + 1) * HEAD_PAD] = jnp.dot(p_mm, wpp_ref[d], preferred_element_type=F32)

        def gate_input(rows):
            n3_s[rows, :] = (_rms(x2_ref[rows, :])[0] * gpg).astype(MM)

        _by_chunks(tm, gate_input)
        z_s[...] = jnp.dot(n3_s[...], wpg_ref[...], preferred_element_type=F32)

        def loss_and_back(rows):
            uh, ru = _rms(u_s[rows, :])
            e = uh * gpo
            gate = _sigmoid(z_s[rows, :])
            diff = x2_ref[rows, :] + gate * e - t_ref[rows, :]
            dy = diff * (1.0 / D_MODEL)
            de = dy * gate
            dz_s[rows, :] = (dy * e * gate * (1.0 - gate)).astype(MM)
            du_s[rows, :] = _rms_bwd(de * gpo, uh, ru).astype(MM)
            dy_s[rows, :] = dy
            loss_ref[...] += _fold8(diff * diff) * (0.5 / D_MODEL)
            dgpo_ref[...] += _fold8(de * uh)

        _by_chunks(tm, loss_and_back)
        dn3_s[...] = _dot_nt(dz_s[...], wpg_ref[...])

        def gate_norm_back(rows):
            x2h, r3 = _rms(x2_ref[rows, :])
            dn3 = dn3_s[rows, :]
            dx2_ref[rows, :] = dy_s[rows, :] + _rms_bwd(dn3 * gpg, x2h, r3)
            dgpg_ref[...] += _fold8(dn3 * x2h)

        _by_chunks(tm, gate_norm_back)
        dwpg_ref[...] += _dot_tn(n3_s[...], dz_s[...])
        for d in range(N_DEV):
            dwpp_ref[d] += _dot_tn(p_mm, du_s[:, d * HEAD_PAD:(d + 1) * HEAD_PAD])

    vec = ((8, D_MODEL), F32)
    wide = lambda dt: pltpu.VMEM((tm, D_MODEL), dt)
    return _row_call("ple_loss", body, T, tm, [x2, p, tgt], [g_pg, g_post, w_pg, w_pp], [(D_MODEL, F32)],
                     [vec, vec, vec, ((D_MODEL, D_MODEL), F32), ((N_DEV, PLE, HEAD_PAD), F32)], VMEM_LIMIT,
                     scratch=[wide(F32), wide(MM), wide(F32), wide(MM), wide(MM), wide(F32), wide(F32)])


def _ffn_bwd(dx2, x1, gt, up, g_ffn, w_g, w_u, w_d, T, tm):
    def body(i, dx2_ref, x1_ref, gt_ref, up_ref, g_ref, wg_ref, wu_ref, wd_ref,
             dx1_ref, a_ref, dgt_ref, dup_ref, dg_ref, da_s, dh2_s):
        @pl.when(i == 0)
        def _():
            dg_ref[...] = jnp.zeros_like(dg_ref)

        g = g_ref[...]
        da_s[...] = _dot_nt(dx2_ref[...], wd_ref[...])

        def act_back(rows):
            for cs in FFN_HALVES:
                up, da = up_ref[rows, cs], da_s[rows, cs]
                silu, dsilu = _silu_parts(gt_ref[rows, cs])
                dgt_ref[rows, cs] = (da * up * dsilu).astype(MM)
                dup_ref[rows, cs] = (da * silu).astype(MM)
                a_ref[rows, cs] = (silu * up).astype(MM)

        _by_chunks(tm, act_back)
        dh2_s[...] = (jnp.dot(dgt_ref[...], wg_ref[...], preferred_element_type=F32)
                      + jnp.dot(dup_ref[...], wu_ref[...], preferred_element_type=F32))

        def norm_back(rows):
            x1h, r = _rms(x1_ref[rows, :])
            dh2 = dh2_s[rows, :]
            dx1_ref[rows, :] = dx2_ref[rows, :] + _rms_bwd(dh2 * g, x1h, r)
            dg_ref[...] += _fold8(dh2 * x1h)

        _by_chunks(tm, norm_back)

    return _row_call("ffn_bwd", body, T, tm, [dx2, x1, gt, up], [g_ffn, w_g, w_u, w_d],
                     [(D_MODEL, F32), (FFN, MM), (FFN, MM), (FFN, MM)], [((8, D_MODEL), F32)], VMEM_LIMIT,
                     scratch=[pltpu.VMEM((tm, FFN), F32), pltpu.VMEM((tm, D_MODEL), F32)])


def _merge_bwd(dx1, ya, yb, bg, o, hg, attn, m, rec, g_out, w_bra, w_brb, w_out, T, tm):
    def body(i, dx1_ref, ya_ref, yb_ref, bg_ref, o_ref, hg_ref, attn_ref, m_ref, rec_ref, g_ref, wa_ref, wb_ref, wo_ref,
             dattn_ref, do_ref, dhg_ref, dbg_ref, dg_ref, dwo_ref, dwa_ref, dwb_ref, dm_s, dya_s, dyb_s, drec_s):
        @pl.when(i == 0)
        def _():
            for ref in (dg_ref, dwo_ref, dwa_ref, dwb_ref):
                ref[...] = jnp.zeros_like(ref)

        g = g_ref[...]
        dx1 = dx1_ref[...].astype(MM)
        dm_s[...] = _dot_nt(dx1, wo_ref[...])

        def gate_back(rows):
            dm = dm_s[rows, :]
            ga, gb = _sigmoid(bg_ref[rows, :D_MODEL]), _sigmoid(bg_ref[rows, D_MODEL:])
            dya_s[rows, :] = (dm * ga).astype(MM)
            dyb_s[rows, :] = (dm * gb).astype(MM)
            dbg_ref[rows, :D_MODEL] = (dm * ya_ref[rows, :] * ga * (1.0 - ga)).astype(MM)
            dbg_ref[rows, D_MODEL:] = (dm * yb_ref[rows, :] * gb * (1.0 - gb)).astype(MM)

        _by_chunks(tm, gate_back)
        dwo_ref[...] += _dot_tn(m_ref[...], dx1)
        attn_mm = attn_ref[...].astype(MM)
        for d in range(N_DEV):
            ds = slice(d * HEAD_PAD, (d + 1) * HEAD_PAD)
            dwa_ref[d] += _dot_tn(attn_mm, dya_s[:, ds])
            dwb_ref[d] += _dot_tn(rec_ref[...], dyb_s[:, ds])
        dattn_ref[...] = _dot_nt(dya_s[...], wa_ref[...])
        drec_s[...] = _dot_nt(dyb_s[...], wb_ref[...])

        def recurrent_out_back(rows):
            for h in range(HG_HEADS):
                ls = slice(h * HG_DIM, (h + 1) * HG_DIM)
                oh, r = _rms(o_ref[rows, ls])
                silu, dsilu = _silu_parts(hg_ref[rows, ls])
                dr = drec_s[rows, ls]
                dhg_ref[rows, ls] = (dr * oh * g * dsilu).astype(MM)
                don = dr * silu
                dg_ref[...] += _fold8(don * oh)
                do_ref[rows, ls] = _rms_bwd(don * g, oh, r)

        _by_chunks(tm, recurrent_out_back)

    wide = lambda n, dt: pltpu.VMEM((tm, n), dt)
    return _row_call("merge_bwd", body, T, tm, [dx1, ya, yb, bg, o, hg, attn, m, rec], [g_out, w_bra, w_brb, w_out],
                     [(D_MODEL, F32), (HG_W, F32), (HG_W, MM), (2 * D_MODEL, MM)],
                     [((8, HG_DIM), F32), ((D_MODEL, D_MODEL), F32), ((N_DEV, MLA_HEADS * HEAD_PAD, HEAD_PAD), F32),
                      ((N_DEV, HG_W, HEAD_PAD), F32)], VMEM_LIMIT,
                     scratch=[wide(D_MODEL, F32), wide(D_MODEL, MM), wide(D_MODEL, MM), wide(HG_W, F32)])


def _flash_bwd(qf, kf, vf, o, do, lse, T, xchg=()):
    tq = min(ATT_TILE, T)
    nq = T // tq

    qi_tab, ki_tab = _causal_pairs(nq, by_query=False)

    n_x = len(xchg)
    hp = ATT_HEADS
    n_heads, n_pairs = MLA_HEADS // hp, len(qi_tab)

    def body(qi_ref, ki_ref, q_ref, k_ref, v_ref, o_ref, do_ref, lse_ref, *rest):
        x_in, (dq_ref, dk_ref, dv_ref), rest = rest[:n_x], rest[n_x:n_x + 3], rest[n_x + 3:]
        x_out, x_sems = rest[:n_x], rest[n_x:]
        t = pl.program_id(1)
        qi, ki = qi_ref[t], ki_ref[t]
        if n_x:
            @pl.when((pl.program_id(0) == 0) & (t == 0))
            def _():
                for cp in _xchips_copies(x_in, x_out, x_sems):
                    cp.start()

        @pl.when(t == 0)
        def _():
            dq_ref[...] = jnp.zeros_like(dq_ref)

        def step(first):
            rows = pl.ds(pl.multiple_of(qi * tq, tq), tq)
            for hh in range(hp):
                hs = slice(hh * HEAD_PAD, (hh + 1) * HEAD_PAD)
                q, k, d_o = q_ref[:, hs], k_ref[:, hs], do_ref[:, hs]
                s = _dot_nt(q, k)
                if first:
                    row = lax.broadcasted_iota(jnp.int32, (tq, tq), 0)
                    col = lax.broadcasted_iota(jnp.int32, (tq, tq), 1)
                    s = jnp.where(col <= row, s, NEG)
                p = jnp.exp(s - lse_ref[:, hh * HEAD_PAD:hh * HEAD_PAD + 1])
                delta = jnp.sum(d_o * o_ref[:, hs], axis=1, keepdims=True)
                ds = p * (_dot_nt(d_o, v_ref[:, hs]) - delta)
                dq_ref[rows, hs] += _dot(ds, k)
                if first:
                    dv_ref[:, hs] = _dot_tn(p, d_o)
                    dk_ref[:, hs] = _dot_tn(ds, q)
                else:
                    dv_ref[:, hs] += _dot_tn(p, d_o)
                    dk_ref[:, hs] += _dot_tn(ds, q)

        @pl.when(qi == ki)
        def _():
            step(True)

        @pl.when(qi > ki)
        def _():
            step(False)

        if n_x:
            @pl.when((pl.program_id(0) == n_heads - 1) & (t == n_pairs - 1))
            def _():
                for cp in _xchips_copies(x_in, x_out, x_sems):
                    cp.wait()

    q_spec = pl.BlockSpec((tq, hp * HEAD_PAD), lambda h, t, qi_ref, ki_ref: (qi_ref[t], h))
    kv_spec = pl.BlockSpec((tq, hp * HEAD_PAD), lambda h, t, qi_ref, ki_ref: (ki_ref[t], h))
    any_spec = pl.BlockSpec(memory_space=pl.ANY)
    w = MLA_HEADS * HEAD_PAD
    grid_spec = pltpu.PrefetchScalarGridSpec(
        num_scalar_prefetch=2, grid=(n_heads, n_pairs),
        in_specs=[q_spec, kv_spec, kv_spec, q_spec, q_spec, q_spec] + [any_spec] * n_x,
        out_specs=[pl.BlockSpec((T, hp * HEAD_PAD), lambda h, t, qi_ref, ki_ref: (0, h)), kv_spec, kv_spec]
        + [any_spec] * n_x,
        scratch_shapes=_xchips_sems(n_x) if n_x else [])
    return pl.pallas_call(
        body, name="flash_bwd", grid_spec=grid_spec,
        out_shape=[jax.ShapeDtypeStruct((T, w), F32)] * 3 + _xchips_out_shapes(xchg),
        compiler_params=_cparams(("arbitrary", "arbitrary")),
    )(jnp.asarray(qi_tab), jnp.asarray(ki_tab), qf, kf, vf, o, do, lse, *xchg)


def _mla_heads_bwd(d_out, saved, g_pad, cos_t, sin_t, first):
    d_raw, dg = [], jnp.zeros((1, HEAD_PAD), F32)
    for h in range(MLA_HEADS):
        xh, r = saved[h]
        dy = d_out[:, h * HEAD_PAD:(h + 1) * HEAD_PAD]
        dn = dy * cos_t + _rope_swap(dy * sin_t, first)
        dg = dg + jnp.sum(dn * xh, axis=0, keepdims=True)
        d_raw.append(_rms_bwd(dn * g_pad, xh, r, QK_DIM))
    return d_raw, dg


def _mla_prep_bwd(cq, ckv, kr, pos, dqf, dkf, dvf, g_qa, g_kva, g_qn, g_kn, w_uq, w_ukv, T, tm):
    def body(i, cq_ref, ckv_ref, kr_ref, pos_ref, dq_ref, dk_ref, dv_ref,
             gqa_ref, gkva_ref, gqn_ref, gkn_ref, wuq_ref, wukv_ref,
             dcq_ref, dckv_ref, dkr_ref, dgqa_ref, dgkva_ref, dgqn_ref, dgkn_ref, dwuq_ref, dwukv_ref):
        cos_t, sin_t, first = _rope_tables(pos_ref[...], tm)
        cqh, rq = _rms(cq_ref[...])
        ckvh, rkv = _rms(ckv_ref[...])
        cqn, ckvn = cqh * gqa_ref[...], ckvh * gkva_ref[...]
        q_raw, k_raw, _ = _mla_raw_heads(cqn, ckvn, kr_ref[...], wuq_ref, wukv_ref, tm)
        _, q_saved = _mla_heads_fwd(q_raw, gqn_ref[...], cos_t, sin_t, first)
        _, k_saved = _mla_heads_fwd(k_raw, gkn_ref[...], cos_t, sin_t, first)
        dq_heads, dgqn = _mla_heads_bwd(dq_ref[...] * ATT_SCALE, q_saved, gqn_ref[...], cos_t, sin_t, first)
        dk_heads, dgkn = _mla_heads_bwd(dk_ref[...], k_saved, gkn_ref[...], cos_t, sin_t, first)
        lane = lax.broadcasted_iota(jnp.int32, (tm, HEAD_PAD), 1)
        nope = lane < QK_NOPE
        dcqn = jnp.zeros((tm, Q_RANK), F32)
        dckvn = jnp.zeros((tm, KV_RANK), F32)
        dkr = jnp.zeros((tm, HEAD_PAD), F32)
        cqn_mm, ckvn_mm = cqn.astype(MM), ckvn.astype(MM)
        for h in range(MLA_HEADS):
            hs = slice(h * HEAD_PAD, (h + 1) * HEAD_PAD)
            dq_h = dq_heads[h].astype(MM)
            dkv_h = jnp.where(nope, dk_heads[h], pltpu.roll(dv_ref[:, hs], V_DIM, 1)).astype(MM)
            _acc(dwuq_ref.at[h], i, _dot_tn(dq_h, cqn_mm))
            _acc(dwukv_ref.at[h], i, _dot_tn(ckvn_mm, dkv_h))
            dcqn = dcqn + jnp.dot(dq_h, wuq_ref[h], preferred_element_type=F32)
            dckvn = dckvn + lax.dot_general(dkv_h, wukv_ref[h], (((1,), (1,)), ((), ())), preferred_element_type=F32)
            dkr = dkr + dk_heads[h]
        dkr_ref[...] = jnp.where((lane >= QK_NOPE) & (lane < QK_DIM), dkr, 0.0).astype(MM)
        dcq_ref[...] = _rms_bwd(dcqn * gqa_ref[...], cqh, rq).astype(MM)
        dckv_ref[...] = _rms_bwd(dckvn * gkva_ref[...], ckvh, rkv).astype(MM)
        _acc(dgqa_ref, i, jnp.sum(dcqn * cqh, axis=0, keepdims=True))
        _acc(dgkva_ref, i, jnp.sum(dckvn * ckvh, axis=0, keepdims=True))
        _acc(dgqn_ref, i, dgqn)
        _acc(dgkn_ref, i, dgkn)

    return _row_call(
        "mla_prep_bwd", body, T, tm, [cq, ckv, kr, pos, dqf, dkf, dvf], [g_qa, g_kva, g_qn, g_kn, w_uq, w_ukv],
        [(Q_RANK, MM), (KV_RANK, MM), (HEAD_PAD, MM)],
        [((1, Q_RANK), F32), ((1, KV_RANK), F32), ((1, HEAD_PAD), F32), ((1, HEAD_PAD), F32),
         ((MLA_HEADS, HEAD_PAD, Q_RANK), F32), ((MLA_HEADS, KV_RANK, HEAD_PAD), F32)], VMEM_LIMIT)


def _in_proj_bwd(x, dx1, dsecs, g_mix, w_in, T, tm):
    def body(i, x_ref, dx1_ref, *rest):
        d_refs, (g_ref, w_ref, dx_ref, dp_ref, dg_ref, dh_s) = rest[:len(SECTIONS)], rest[len(SECTIONS):]

        @pl.when(i == 0)
        def _():
            dg_ref[...] = jnp.zeros_like(dg_ref)

        g = g_ref[...]

        def join_and_cut(rows):
            pieces = [(d_ref[rows, QK_NOPE:QK_DIM] if n == QK_ROPE else d_ref[rows, :]).astype(F32)
                      for (_, n), d_ref in zip(COL_SECTIONS, d_refs)]
            dproj = jnp.concatenate(pieces, axis=1)
            for d in range(N_DEV):
                dp_ref[d, rows, :] = dproj[:, d * IN_BLOCK:(d + 1) * IN_BLOCK].astype(MM)

        _by_chunks(tm, join_and_cut)
        dh = jnp.dot(dp_ref[0], w_ref[0], preferred_element_type=F32)
        for d in range(1, N_DEV):
            dh = dh + jnp.dot(dp_ref[d], w_ref[d], preferred_element_type=F32)
        dh_s[...] = dh

        def norm_back(rows):
            xh, r = _rms(x_ref[rows, :])
            dh_c = dh_s[rows, :]
            dx_ref[rows, :] = dx1_ref[rows, :] + _rms_bwd(dh_c * g, xh, r)
            dg_ref[...] += _fold8(dh_c * xh)

        _by_chunks(tm, norm_back)

    in_specs = [pl.BlockSpec((tm, a.shape[1]), lambda i: (i, 0)) for a in [x, dx1, *dsecs]]
    in_specs += [pl.BlockSpec(g_mix.shape, lambda i: (0, 0)),
                 pl.BlockSpec(w_in.shape, lambda i: (0, 0, 0), pipeline_mode=pl.Buffered(1))]

    def kern(*refs):
        body(pl.program_id(0), *refs)

    return pl.pallas_call(
        kern, name="in_proj_bwd", grid=(T // tm,), in_specs=in_specs,
        out_specs=[pl.BlockSpec((tm, D_MODEL), lambda i: (i, 0)),
                   pl.BlockSpec((N_DEV, tm, IN_BLOCK), lambda i: (0, i, 0)),
                   pl.BlockSpec((8, D_MODEL), lambda i: (0, 0))],
        out_shape=[jax.ShapeDtypeStruct((T, D_MODEL), F32), jax.ShapeDtypeStruct((N_DEV, T, IN_BLOCK), MM),
                   jax.ShapeDtypeStruct((8, D_MODEL), F32)],
        scratch_shapes=[pltpu.VMEM((tm, D_MODEL), F32)],
        compiler_params=_cparams(("arbitrary",), VMEM_LIMIT),
    )(x, dx1, *dsecs, g_mix, w_in)


def _pick_block(n, cap):
    best = None
    for cand in range(128, min(n, cap) + 1, 128):
        if n % cand == 0:
            best = cand
    return n if best is None else best


def _pick_rows(n, cap):
    best = n
    for cand in range(8, min(n, cap) + 1, 8):
        if n % cand == 0:
            best = cand
    return best


def _matmul_tn(name, a, b):
    T, M = a.shape
    N = b.shape[1]
    bm, bk = _pick_block(M, 1408), min(512, T)
    bn = _pick_block(N, 2560)

    def body(a_ref, b_ref, c_ref):
        @pl.when(pl.program_id(2) == 0)
        def _():
            c_ref[...] = jnp.zeros_like(c_ref)

        c_ref[...] += _dot_tn(a_ref[...], b_ref[...])

    return pl.pallas_call(
        body, name=name, grid=(M // bm, N // bn, T // bk),
        in_specs=[pl.BlockSpec((bk, bm), lambda i, j, k: (k, i)), pl.BlockSpec((bk, bn), lambda i, j, k: (k, j))],
        out_specs=pl.BlockSpec((bm, bn), lambda i, j, k: (i, j)), out_shape=jax.ShapeDtypeStruct((M, N), F32),
        compiler_params=_cparams(("parallel", "parallel", "arbitrary"), VMEM_LIMIT),
    )(a, b)


def _matmul_tn_blocks(name, a, b):
    T, M = a.shape
    nd, _, c = b.shape
    bm, bk = _pick_block(M, 512), min(512, T)

    def body(a_ref, b_ref, c_ref):
        @pl.when(pl.program_id(1) == 0)
        def _():
            c_ref[...] = jnp.zeros_like(c_ref)

        a_blk = a_ref[...].astype(MM)
        for d in range(nd):
            c_ref[d] += _dot_tn(b_ref[d], a_blk)

    return pl.pallas_call(
        body, name=name, grid=(M // bm, T // bk),
        in_specs=[pl.BlockSpec((bk, bm), lambda i, k: (k, i)), pl.BlockSpec((nd, bk, c), lambda i, k: (0, k, 0))],
        out_specs=pl.BlockSpec((nd, c, bm), lambda i, k: (0, 0, i)),
        out_shape=jax.ShapeDtypeStruct((nd, c, M), F32),
        compiler_params=_cparams(("parallel", "arbitrary"), VMEM_LIMIT),
    )(a, b)


def _pad_gain(g, n):
    return jnp.pad(g.reshape(1, -1), ((0, 0), (0, n - g.shape[-1])))


GROUP_A = ("w_ffn_gate", "w_ffn_up", "w_ffn_down", "w_ple_gate", "w_ple_proj")
GROUP_B = ("w_branch", "w_out")
GROUP_C = ("w_in", "w_uq", "w_ukv")
EARLY = GROUP_C
LATE = GROUP_B + GROUP_A
TRANSPOSED = ("w_in", "w_uq", "w_ffn_gate", "w_ffn_up")


def _local_step(x, p, pos, tgt, small, big, late_blocks=None, core=None):
    T = x.shape[0]
    tm = min(ROW_TILE, T)
    w_in = big["w_in"]
    w_uq = jnp.pad(big["w_uq"], ((0, 0), (0, HEAD_PAD - QK_DIM), (0, 0)))
    w_ukv = big["w_ukv"]

    g_mix, g_qa, g_kva = small["mix_norm_g"], small["q_a_norm_g"], small["kv_a_norm_g"]
    g_qn, g_kn = _pad_gain(small["q_norm_g"], HEAD_PAD), _pad_gain(small["k_norm_g"], HEAD_PAD)
    g_out, g_ffn = small["hg_out_norm_g"], small["ffn_norm_g"]
    g_pg, g_post = small["ple_gate_norm_g"], small["ple_post_norm_g"]
    logits = small["hg_lb_logits"]
    lb = _lower_bound(logits)

    h, cq, ckv, kr, hq, hf, hi, hg, bg = _in_proj_fwd(x, g_mix, w_in, T, tm)
    qf, kf, vf = _mla_prep_fwd(cq, ckv, kr, pos, g_qa, g_kva, g_qn, g_kn, w_uq, w_ukv, T, tm)
    if late_blocks is None:
        attn, lse = _flash_fwd(qf, kf, vf, T)
    else:
        attn, lse, *late = _flash_fwd(qf, kf, vf, T, ag_blocks=[late_blocks[n] for n in LATE])
        big = {**big, **dict(zip(LATE, late))}
    w_branch = jnp.moveaxis(big["w_branch"].reshape(N_DEV, 2, HG_W, HEAD_PAD), 0, 2).reshape(2, HG_W, D_MODEL)
    w_bra = jnp.pad(w_branch[0].reshape(MLA_HEADS, V_DIM, D_MODEL),
                    ((0, 0), (0, HEAD_PAD - V_DIM), (0, 0))).reshape(MLA_HEADS * HEAD_PAD, D_MODEL)
    w_brb = w_branch[1]
    w_out = big["w_out"].reshape(D_MODEL, D_MODEL)
    w_g, w_u = big["w_ffn_gate"].reshape(FFN, D_MODEL), big["w_ffn_up"].reshape(FFN, D_MODEL)
    w_d = big["w_ffn_down"].reshape(FFN, D_MODEL)
    w_pg, w_pp = big["w_ple_gate"].reshape(D_MODEL, D_MODEL), big["w_ple_proj"]
    o, s0 = _hgrn_fwd(hq, hf, hi, lb, T)
    x1, ya, yb, m, rec = _merge_fwd(attn, o, hg, bg, x, g_out, w_bra, w_brb, w_out, T, tm)
    x2, gt, up, h2 = _ffn_fwd(x1, g_ffn, w_g, w_u, w_d, T, tm)
    dx2, loss_p, dg_post, dg_pg, d_pg, d_pp = _ple_loss(x2, p, tgt, g_pg, g_post, w_pg, w_pp, T, tm)
    dg_post, dg_pg = (jnp.sum(t, axis=0, keepdims=True) for t in (dg_post, dg_pg))

    grads, sibs, gots = {}, {}, {}

    def reduce_start(tag, names):
        if core is None:
            return ()
        got = _exchange_sibling("rs_sibling_" + tag, [grads[n] for n in names])
        sibs.update(zip(names, got))
        return [_chip_partial("rs_partial_" + n, grads[n], sibs[n], core) for n in names]

    dx1, a, dgt, dup, dg_ffn = _ffn_bwd(dx2, x1, gt, up, g_ffn, w_g, w_u, w_d, T, tm)
    dg_ffn = jnp.sum(dg_ffn, axis=0, keepdims=True)
    grads["w_ffn_gate"] = _matmul_tn("dw_gate", dgt, h2).reshape(N_DEV, -1, D_MODEL)
    grads["w_ffn_up"] = _matmul_tn("dw_up", dup, h2).reshape(N_DEV, -1, D_MODEL)
    grads["w_ffn_down"] = _matmul_tn("dw_down", a, dx2).reshape(N_DEV, -1, D_MODEL)
    grads["w_ple_gate"] = d_pg.reshape(N_DEV, -1, D_MODEL)
    grads["w_ple_proj"] = d_pp
    parts_a = reduce_start("a", GROUP_A)

    dattn, do, dhg, dbg, dg_out, d_out, d_bra, d_brb = _merge_bwd(
        dx1, ya, yb, bg, o, hg, attn, m, rec, g_out, w_bra, w_brb, w_out, T, tm)
    dg_out = jnp.sum(dg_out, axis=0, keepdims=True)
    d_bra = d_bra.reshape(N_DEV, MLA_HEADS, HEAD_PAD, HEAD_PAD)[:, :, :V_DIM].reshape(N_DEV, HG_W, HEAD_PAD)
    grads["w_branch"] = jnp.concatenate([d_bra, d_brb], axis=1)
    grads["w_out"] = d_out.reshape(N_DEV, -1, D_MODEL)
    parts_b = reduce_start("b", GROUP_B)

    dhq, dhf, dhi, dlb, *got_a = _hgrn_bwd(hq, hf, hi, do, s0, lb, T, xchg=parts_a)
    dqf, dkf, dvf, *got_b = _flash_bwd(qf, kf, vf, attn, dattn, lse, T, xchg=parts_b)
    (dcq, dckv, dkr, dg_qa, dg_kva, dg_qn, dg_kn, d_uq, d_ukv) = _mla_prep_bwd(
        cq, ckv, kr, pos, dqf, dkf, dvf, g_qa, g_kva, g_qn, g_kn, w_uq, w_ukv, T, tm)
    grad_x, dproj, dg_mix = _in_proj_bwd(x, dx1, [dcq, dckv, dkr, dhq, dhf, dhi, dhg, dbg], g_mix, w_in, T, tm)
    dg_mix = jnp.sum(dg_mix, axis=0, keepdims=True)
    grads["w_in"] = _matmul_tn_blocks("dw_in", h, dproj)
    grads["w_uq"] = d_uq[:, :QK_DIM]
    grads["w_ukv"] = d_ukv
    parts_c = reduce_start("c", GROUP_C)
    if core is not None:
        gots.update(zip(GROUP_A, got_a))
        gots.update(zip(GROUP_B, got_b))
        gots.update(zip(GROUP_C, _exchange_chips(parts_c)))

    dl0 = dlb * lb * (1.0 - lb)
    small_g = {
        "mix_norm_g": dg_mix, "q_a_norm_g": dg_qa, "kv_a_norm_g": dg_kva,
        "q_norm_g": dg_qn[:, :QK_DIM], "k_norm_g": dg_kn[:, :QK_DIM],
        "hg_lb_logits": jnp.concatenate([dl0, -dl0], axis=0), "hg_out_norm_g": dg_out,
        "ffn_norm_g": dg_ffn, "ple_gate_norm_g": dg_pg, "ple_post_norm_g": dg_post,
    }
    return loss_p, grad_x, small_g, grads, sibs, gots


def _lower_bound(logits):
    def body(l_ref, lb_ref):
        l = l_ref[...]
        mx = jnp.max(l, axis=0, keepdims=True)
        e = jnp.exp(l - mx)
        lb_ref[...] = e[0:1] / jnp.sum(e, axis=0, keepdims=True)

    return pl.pallas_call(body, name="lower_bound", out_shape=jax.ShapeDtypeStruct((1, HG_W), F32))(logits)


def _my_place():
    return lax.axis_index("x"), lax.axis_index("y"), lax.axis_index("c")


def _all_gather(name, blocks):
    n = len(blocks)

    def body(*refs):
        x_refs, out_refs, sems = refs[:n], refs[n:2 * n], refs[2 * n:]
        _ag_start(x_refs, out_refs, sems)
        _ag_finish(x_refs, out_refs, sems)

    any_spec = pl.BlockSpec(memory_space=pl.ANY)
    return pl.pallas_call(
        body, name=name, out_shape=_ag_out_shapes(blocks),
        in_specs=[any_spec] * n, out_specs=[any_spec] * n, scratch_shapes=_ag_sems(n),
    )(*blocks)


def _ag_out_shapes(blocks):
    return [jax.ShapeDtypeStruct((N_DEV,) + b.shape, b.dtype) for b in blocks]


def _ag_sems(n):
    return [pltpu.SemaphoreType.DMA((7 * n,)), pltpu.SemaphoreType.DMA((7 * n,)), pltpu.SemaphoreType.DMA((n,))]


def _ag_parts(x_refs, out_refs, sems):
    send_sems, recv_sems, local_sems = sems
    x, y, c = _my_place()
    me, sibling = (x, y, c), (x, y, 1 - c)
    chips = [(1 - x, y), (x, 1 - y), (1 - x, 1 - y)]
    n = len(x_refs)

    def copy(a, k, block, to, own=False):
        px, py, pc = block
        dst = out_refs[a].at[4 * px + 2 * py + pc]
        return pltpu.make_async_remote_copy(
            src_ref=x_refs[a] if own else dst, dst_ref=dst, send_sem=send_sems.at[7 * a + k],
            recv_sem=recv_sems.at[7 * a + k], device_id=to, device_id_type=MESH_ID)

    mine = [pltpu.make_async_copy(x_refs[a], out_refs[a].at[4 * x + 2 * y + c], local_sems.at[a]) for a in range(n)]
    first = []
    for a in range(n):
        first.append(copy(a, 0, me, sibling, own=True))
        first += [copy(a, 1 + j, me, (*chip, c), own=True) for j, chip in enumerate(chips)]
    return copy, mine, first, me, sibling, chips, c, n


def _ag_start(x_refs, out_refs, sems):
    _, mine, first, *_ = _ag_parts(x_refs, out_refs, sems)
    for cp in mine + first:
        cp.start()


def _ag_finish(x_refs, out_refs, sems):
    copy, mine, first, me, sibling, chips, c, n = _ag_parts(x_refs, out_refs, sems)
    passed = []
    for j, chip in enumerate(chips):
        for a in range(n):
            copy(a, 1 + j, (*chip, c), me).wait_recv()
            passed.append(copy(a, 4 + j, (*chip, c), sibling))
            passed[-1].start()
    for a in range(n):
        copy(a, 0, sibling, me).wait_recv()
    for j, chip in enumerate(chips):
        for a in range(n):
            copy(a, 4 + j, (*chip, 1 - c), me).wait_recv()
    for cp in first + passed:
        cp.wait_send()
    for cp in mine:
        cp.wait()


def _exchange_sibling(name, gs):
    n = len(gs)

    def body(*refs):
        g_refs, out_refs, (send_sems, recv_sems) = refs[:n], refs[n:2 * n], refs[2 * n:]
        x, y, c = _my_place()
        copies = [pltpu.make_async_remote_copy(
            src_ref=g_refs[a].at[2 * j + 1 - c], dst_ref=out_refs[a].at[j], send_sem=send_sems.at[4 * a + j],
            recv_sem=recv_sems.at[4 * a + j], device_id=(x, y, 1 - c), device_id_type=MESH_ID)
            for a in range(n) for j in range(4)]
        for cp in copies:
            cp.start()
        for cp in copies:
            cp.wait()

    any_spec = pl.BlockSpec(memory_space=pl.ANY)
    return pl.pallas_call(
        body, name=name, out_shape=[jax.ShapeDtypeStruct((4,) + g.shape[1:], g.dtype) for g in gs],
        in_specs=[any_spec] * n, out_specs=[any_spec] * n,
        scratch_shapes=[pltpu.SemaphoreType.DMA((4 * n,)), pltpu.SemaphoreType.DMA((4 * n,))],
    )(*gs)


def _chip_partial(name, g, got, c_idx):
    _, rows, cols = g.shape
    tr, tc = _tile_2d(rows, cols, 512)

    def body(c_ref, g_ref, got_ref, out_ref):
        out_ref[...] = (g_ref[...] + got_ref[...]).astype(MM)

    grid_spec = pltpu.PrefetchScalarGridSpec(
        num_scalar_prefetch=1, grid=(4, rows // tr, cols // tc),
        in_specs=[pl.BlockSpec((1, tr, tc), lambda j, i, k, c_ref: (2 * j + c_ref[0], i, k)),
                  pl.BlockSpec((1, tr, tc), lambda j, i, k, c_ref: (j, i, k))],
        out_specs=pl.BlockSpec((1, tr, tc), lambda j, i, k, c_ref: (j, i, k)))
    return pl.pallas_call(
        body, name=name, grid_spec=grid_spec, out_shape=jax.ShapeDtypeStruct((4, rows, cols), MM),
        compiler_params=_cparams(("parallel", "parallel", "parallel")),
    )(c_idx, g, got)


def _tile_2d(rows, cols, row_cap):
    if rows % 8 == 0:
        return _pick_rows(rows, row_cap), cols
    return rows, 256 if cols % 256 == 0 else cols


def _exchange_chips(parts):
    n = len(parts)

    def body(*refs):
        p_refs, out_refs, sems = refs[:n], refs[n:2 * n], refs[2 * n:]
        for cp in _xchips_copies(p_refs, out_refs, sems):
            cp.start()
        for cp in _xchips_copies(p_refs, out_refs, sems):
            cp.wait()

    any_spec = pl.BlockSpec(memory_space=pl.ANY)
    return pl.pallas_call(
        body, name="rs_chips", out_shape=_xchips_out_shapes(parts),
        in_specs=[any_spec] * n, out_specs=[any_spec] * n, scratch_shapes=_xchips_sems(n),
    )(*parts)


def _xchips_out_shapes(parts):
    return [jax.ShapeDtypeStruct((3,) + p.shape[1:], p.dtype) for p in parts]


def _xchips_sems(n):
    return [pltpu.SemaphoreType.DMA((3 * n,)), pltpu.SemaphoreType.DMA((3 * n,))]


def _xchips_copies(p_refs, out_refs, sems):
    send_sems, recv_sems = sems
    x, y, c = _my_place()
    chips = [(1 - x, y), (x, 1 - y), (1 - x, 1 - y)]
    return [pltpu.make_async_remote_copy(
        src_ref=p_refs[a].at[2 * px + py], dst_ref=out_refs[a].at[k], send_sem=send_sems.at[3 * a + k],
        recv_sem=recv_sems.at[3 * a + k], device_id=(px, py, c), device_id_type=MESH_ID)
        for a in range(len(p_refs)) for k, (px, py) in enumerate(chips)]


def _adamw_math(w, g, m, v):
    m = ADAM_B1 * m + (1.0 - ADAM_B1) * g
    v = ADAM_B2 * v + (1.0 - ADAM_B2) * jnp.square(g)
    m_hat = m / (1.0 - ADAM_B1 ** ADAM_STEP)
    v_hat = v / (1.0 - ADAM_B2 ** ADAM_STEP)
    delta = -ADAM_LR * (m_hat / (jnp.sqrt(v_hat) + ADAM_EPS) + ADAM_WD * w)
    return delta, m, v


def _sum_adamw(name, g, sib, got, w, m, v, slot_idx, chip_idx):
    _, rows, cols = g.shape
    tr, tc = _tile_2d(rows, cols, 256)

    def body(s_ref, j_ref, g_ref, sib_ref, got_ref, w_ref, m_ref, v_ref, go_ref, d_ref, m2_ref, v2_ref):
        grad = g_ref[0] + sib_ref[0]
        for k in range(3):
            grad = grad + got_ref[k].astype(F32)
        go_ref[...] = grad
        d_ref[...], m2_ref[...], v2_ref[...] = _adamw_math(w_ref[...], grad, m_ref[...], v_ref[...])

    flat = pl.BlockSpec((tr, tc), lambda i, k, s_ref, j_ref: (i, k))
    grid_spec = pltpu.PrefetchScalarGridSpec(
        num_scalar_prefetch=2, grid=(rows // tr, cols // tc),
        in_specs=[pl.BlockSpec((1, tr, tc), lambda i, k, s_ref, j_ref: (s_ref[0], i, k)),
                  pl.BlockSpec((1, tr, tc), lambda i, k, s_ref, j_ref: (j_ref[0], i, k)),
                  pl.BlockSpec((3, tr, tc), lambda i, k, s_ref, j_ref: (0, i, k)), flat, flat, flat],
        out_specs=[flat] * 4)
    return pl.pallas_call(
        body, name=name, grid_spec=grid_spec, out_shape=[jax.ShapeDtypeStruct((rows, cols), F32)] * 4,
        compiler_params=_cparams(("parallel", "parallel")),
    )(slot_idx, chip_idx, g, sib, got, w, m, v)


def _adamw_small(parts, w, m, v):
    rows = w.shape[0]

    def body(p_ref, w_ref, m_ref, v_ref, g_ref, d_ref, m2_ref, v2_ref):
        g = p_ref[0]
        for d in range(1, N_DEV):
            g = g + p_ref[d]
        g_ref[...] = g
        d_ref[...], m2_ref[...], v2_ref[...] = _adamw_math(w_ref[...], g, m_ref[...], v_ref[...])

    return pl.pallas_call(
        body, name="adamw_small", out_shape=[jax.ShapeDtypeStruct((rows, 128), F32)] * 4,
    )(parts, w, m, v)


BIG = ("w_in", "w_uq", "w_ukv", "w_branch", "w_out", "w_ffn_gate", "w_ffn_up", "w_ffn_down", "w_ple_gate", "w_ple_proj")
SMALL = (
    ("mix_norm_g", 1024), ("q_a_norm_g", 384), ("kv_a_norm_g", 256), ("q_norm_g", 96), ("k_norm_g", 96),
    ("hg_lb_logits", 1024), ("hg_out_norm_g", 128), ("ffn_norm_g", 1024), ("ple_gate_norm_g", 1024),
    ("ple_post_norm_g", 1024),
)
SMALL_ROWS = 56


def _pack_small(vals):
    rows = []
    for name, n in SMALL:
        v = vals[name].reshape(1, -1).astype(F32)
        rows.append(jnp.pad(v, ((0, 0), (0, (-n) % 128))).reshape(-1, 128))
    return jnp.concatenate(rows, axis=0)


def _unpack_small(packed, shapes):
    out, r = {}, 0
    for name, n in SMALL:
        k = (n + 127) // 128
        out[name] = packed[r:r + k].reshape(1, -1)[:, :n].reshape(shapes[name])
        r += k
    return out


_WEIGHTS = ["mix_norm_g", "w_in", "q_a_norm_g", "w_uq", "kv_a_norm_g", "w_ukv", "q_norm_g", "k_norm_g", "hg_lb_logits",
            "hg_out_norm_g", "w_branch", "w_out", "ffn_norm_g", "w_ffn_gate", "w_ffn_up", "w_ffn_down",
            "ple_gate_norm_g", "w_ple_gate", "w_ple_proj", "ple_post_norm_g"]


def _step(x, p, positions, tgt, w, m, v):
    small_names = [n for n, _ in SMALL]
    T = x.shape[1]
    px, py, pc = _my_place()
    as_idx = lambda t: jnp.reshape(t, (1,)).astype(jnp.int32)

    def two_d(n, t):
        t = t.reshape(-1, t.shape[-1])
        return t.T if n in TRANSPOSED else t

    def full_shape(n, t):
        return (t.T if n in TRANSPOSED else t).reshape(w[n].shape)

    blocks = {n: two_d(n, w[n]).astype(MM) for n in BIG}
    big = dict(zip(EARLY, _all_gather("ag_weights", [blocks[n] for n in EARLY])))
    small = {n: (w[n] if n == "hg_lb_logits" else w[n].reshape(1, -1)) for n in small_names}

    loss_p, grad_x, small_g, grads, sibs, gots = _local_step(
        x[0], p[0, 0], positions.reshape(T, 1), tgt[0], small, big, late_blocks=blocks, core=as_idx(pc))

    out_g, out_d, out_m, out_v = {}, {}, {}, {}
    for n in BIG:
        res = _sum_adamw("adamw_" + n, grads[n], sibs[n], gots[n], two_d(n, w[n]), two_d(n, m[n]), two_d(n, v[n]),
                         as_idx(4 * px + 2 * py + pc), as_idx(2 * px + py))
        out_g[n], out_d[n], out_m[n], out_v[n] = [full_shape(n, r) for r in res]

    packed_g = _pack_small(small_g)
    loss_row = jnp.concatenate([jnp.pad(jnp.sum(loss_p).reshape(1, 1), ((0, 0), (0, 127))),
                                jnp.zeros((SMALL_ROWS - packed_g.shape[0] - 1, 128), F32)], axis=0)
    parts = _all_gather("ag_small", [jnp.concatenate([packed_g, loss_row], axis=0)])[0]
    pad_rows = lambda t: jnp.pad(t, ((0, SMALL_ROWS - t.shape[0]), (0, 0)))
    sw = pad_rows(_pack_small({n: w[n] for n in small_names}))
    sm = pad_rows(_pack_small({n: m[n] for n in small_names}))
    sv = pad_rows(_pack_small({n: v[n] for n in small_names}))
    g_s, d_s, m_s, v_s = _adamw_small(parts, sw, sm, sv)
    shapes = {n: w[n].shape for n in small_names}
    n_packed = packed_g.shape[0]
    loss = g_s[n_packed, 0]
    for src, dst in ((g_s, out_g), (d_s, out_d), (m_s, out_m), (v_s, out_v)):
        dst.update(_unpack_small(src, shapes))

    outs = [loss, grad_x[None]]
    for table in (out_g, out_d, out_m, out_v):
        outs += [table[n] for n in _WEIGHTS]
    return tuple(outs)


def kernel(x, p, positions, mix_norm_g, w_in, q_a_norm_g, w_uq, kv_a_norm_g, w_ukv, q_norm_g, k_norm_g, hg_lb_logits, hg_out_norm_g, w_branch, w_out, ffn_norm_g, w_ffn_gate, w_ffn_up, w_ffn_down, ple_gate_norm_g, w_ple_gate, w_ple_proj, ple_post_norm_g, loss_target, m_mix_norm_g, m_w_in, m_q_a_norm_g, m_w_uq, m_kv_a_norm_g, m_w_ukv, m_q_norm_g, m_k_norm_g, m_hg_lb_logits, m_hg_out_norm_g, m_w_branch, m_w_out, m_ffn_norm_g, m_w_ffn_gate, m_w_ffn_up, m_w_ffn_down, m_ple_gate_norm_g, m_w_ple_gate, m_w_ple_proj, m_ple_post_norm_g, v_mix_norm_g, v_w_in, v_q_a_norm_g, v_w_uq, v_kv_a_norm_g, v_w_ukv, v_q_norm_g, v_k_norm_g, v_hg_lb_logits, v_hg_out_norm_g, v_w_branch, v_w_out, v_ffn_norm_g, v_w_ffn_gate, v_w_ffn_up, v_w_ffn_down, v_ple_gate_norm_g, v_w_ple_gate, v_w_ple_proj, v_ple_post_norm_g):
    w = dict(mix_norm_g=mix_norm_g, w_in=w_in, q_a_norm_g=q_a_norm_g, w_uq=w_uq, kv_a_norm_g=kv_a_norm_g, w_ukv=w_ukv,
             q_norm_g=q_norm_g, k_norm_g=k_norm_g, hg_lb_logits=hg_lb_logits, hg_out_norm_g=hg_out_norm_g,
             w_branch=w_branch, w_out=w_out, ffn_norm_g=ffn_norm_g, w_ffn_gate=w_ffn_gate, w_ffn_up=w_ffn_up,
             w_ffn_down=w_ffn_down, ple_gate_norm_g=ple_gate_norm_g, w_ple_gate=w_ple_gate, w_ple_proj=w_ple_proj,
             ple_post_norm_g=ple_post_norm_g)
    m = dict(mix_norm_g=m_mix_norm_g, w_in=m_w_in, q_a_norm_g=m_q_a_norm_g, w_uq=m_w_uq, kv_a_norm_g=m_kv_a_norm_g,
             w_ukv=m_w_ukv, q_norm_g=m_q_norm_g, k_norm_g=m_k_norm_g, hg_lb_logits=m_hg_lb_logits,
             hg_out_norm_g=m_hg_out_norm_g, w_branch=m_w_branch, w_out=m_w_out, ffn_norm_g=m_ffn_norm_g,
             w_ffn_gate=m_w_ffn_gate, w_ffn_up=m_w_ffn_up, w_ffn_down=m_w_ffn_down,
             ple_gate_norm_g=m_ple_gate_norm_g, w_ple_gate=m_w_ple_gate, w_ple_proj=m_w_ple_proj,
             ple_post_norm_g=m_ple_post_norm_g)
    v = dict(mix_norm_g=v_mix_norm_g, w_in=v_w_in, q_a_norm_g=v_q_a_norm_g, w_uq=v_w_uq, kv_a_norm_g=v_kv_a_norm_g,
             w_ukv=v_w_ukv, q_norm_g=v_q_norm_g, k_norm_g=v_k_norm_g, hg_lb_logits=v_hg_lb_logits,
             hg_out_norm_g=v_hg_out_norm_g, w_branch=v_w_branch, w_out=v_w_out, ffn_norm_g=v_ffn_norm_g,
             w_ffn_gate=v_w_ffn_gate, w_ffn_up=v_w_ffn_up, w_ffn_down=v_w_ffn_down,
             ple_gate_norm_g=v_ple_gate_norm_g, w_ple_gate=v_w_ple_gate, w_ple_proj=v_w_ple_proj,
             ple_post_norm_g=v_ple_post_norm_g)
    return _step(x, p, positions, loss_target, w, m, v)
```

```python
import functools

import jax
import jax.numpy as jnp
import numpy as np
from jax import lax
from jax.experimental import pallas as pl
from jax.experimental.pallas import tpu as pltpu

F32 = jnp.float32
MM = jnp.bfloat16
HI = lax.Precision.HIGHEST
MESH_ID = pl.DeviceIdType.MESH

D_MODEL = 1024
N_DEV = 8
MLA_HEADS = 8
QK_NOPE = 64
QK_ROPE = 32
QK_DIM = 96
V_DIM = 64
HEAD_PAD = 128
Q_RANK = 384
KV_RANK = 256
ROPE_BASE = 10000.0
HG_HEADS = 4
HG_DIM = 128
HG_W = 512
HG_CHUNK = 64
HG_SUB = 16
HG_SAFE_STEP = 5.0
FFN = 2816
PLE = 256
EPS = 1e-6
ATT_SCALE = QK_DIM ** -0.5
NEG = -1e30

ADAM_LR = 0.001
ADAM_B1 = 0.9
ADAM_B2 = 0.999
ADAM_EPS = 1e-08
ADAM_WD = 0.01
ADAM_STEP = 10

SEC_CQ = (0, 384)
SEC_CKV = (384, 256)
SEC_KR = (640, 128)
SEC_HQ = (768, 512)
SEC_HF = (1280, 512)
SEC_HI = (1792, 512)
SEC_HG = (2304, 512)
SEC_BG = (2816, 2048)
IN_PAD = 4864
SECTIONS = (SEC_CQ, SEC_CKV, SEC_KR, SEC_HQ, SEC_HF, SEC_HI, SEC_HG, SEC_BG)
COL_SECTIONS = ((0, 384), (384, 256), (640, 32), (672, 512), (1184, 512), (1696, 512), (2208, 512), (2720, 2048))
IN_COLS = 4768
IN_BLOCK = IN_COLS // 8

VMEM_LIMIT = 58 * 1024 * 1024
ROW_TILE = 256
ATT_TILE = 1024
ATT_HEADS = 2
HG_BLOCK = 512


def _dot(a, b):
    return jnp.dot(a.astype(MM), b.astype(MM), preferred_element_type=F32)


def _dot_nt(a, b):
    return lax.dot_general(a.astype(MM), b.astype(MM), (((1,), (1,)), ((), ())), preferred_element_type=F32)


def _dot_tn(a, b):
    return lax.dot_general(a.astype(MM), b.astype(MM), (((0,), (0,)), ((), ())), preferred_element_type=F32)


def _dot_hi(a, b):
    return jnp.dot(a, b, preferred_element_type=F32, precision=HI)


def _sigmoid(x):
    return 1.0 / (1.0 + jnp.exp(-x))


def _rms(x, n=None):
    n = x.shape[-1] if n is None else n
    r = lax.rsqrt(jnp.sum(x * x, axis=-1, keepdims=True) * (1.0 / n) + EPS)
    return x * r, r


def _rms_bwd(dxh, xh, r, n=None):
    n = xh.shape[-1] if n is None else n
    return r * (dxh - xh * (jnp.sum(dxh * xh, axis=-1, keepdims=True) * (1.0 / n)))


def _rope_tables(pos, tm):
    lane = lax.broadcasted_iota(jnp.int32, (tm, HEAD_PAD), 1)
    idx = jnp.where(lane < QK_NOPE + QK_ROPE // 2, lane - QK_NOPE, lane - QK_NOPE - QK_ROPE // 2)
    inv = jnp.exp(idx.astype(F32) * (-np.log(ROPE_BASE) * 2.0 / QK_ROPE))
    ang = pos.astype(F32) * inv
    in_rope = (lane >= QK_NOPE) & (lane < QK_DIM)
    first = lane < QK_NOPE + QK_ROPE // 2
    cos_t = jnp.where(in_rope, jnp.cos(ang), 1.0)
    sin_t = jnp.where(in_rope, jnp.where(first, -jnp.sin(ang), jnp.sin(ang)), 0.0)
    return cos_t, sin_t, (first, in_rope)


def _rope_swap(x, halves):
    first, in_rope = halves
    half = QK_ROPE // 2
    return jnp.where(in_rope, jnp.where(first, pltpu.roll(x, HEAD_PAD - half, 1), pltpu.roll(x, half, 1)), 0.0)


def _cparams(sem, vmem=None):
    return pltpu.CompilerParams(dimension_semantics=sem, vmem_limit_bytes=vmem)


def _row_call(name, body, T, tm, row_ins, full_ins, row_outs, acc_outs, vmem=None, scratch=()):
    def kern(*refs):
        body(pl.program_id(0), *refs)

    in_specs = [pl.BlockSpec((tm, a.shape[1]), lambda i: (i, 0)) for a in row_ins]
    in_specs += [pl.BlockSpec(a.shape, lambda i, nd=a.ndim: (0,) * nd, pipeline_mode=pl.Buffered(1)) for a in full_ins]
    out_specs = [pl.BlockSpec((tm, n), lambda i: (i, 0)) for n, _ in row_outs]
    out_specs += [pl.BlockSpec(s, lambda i, nd=len(s): (0,) * nd) for s, _ in acc_outs]
    out_shape = [jax.ShapeDtypeStruct((T, n), dt) for n, dt in row_outs]
    out_shape += [jax.ShapeDtypeStruct(s, dt) for s, dt in acc_outs]
    return pl.pallas_call(
        kern, name=name, grid=(T // tm,), in_specs=in_specs, out_specs=out_specs, out_shape=out_shape,
        scratch_shapes=list(scratch), compiler_params=_cparams(("arbitrary",), vmem),
    )(*row_ins, *full_ins)


FFN_HALVES = (slice(0, FFN // 2), slice(FFN // 2, FFN))
ROW_CHUNK = 16
CHUNK_UNROLL = True


def _by_chunks(tm, fn):
    def step(c, carry):
        fn(pl.ds(pl.multiple_of(c * ROW_CHUNK, ROW_CHUNK), ROW_CHUNK))
        return carry

    lax.fori_loop(0, tm // ROW_CHUNK, step, 0, unroll=CHUNK_UNROLL)


def _fold8(x):
    return x[:8] + x[8:]


def _acc(ref, i, val):
    @pl.when(i == 0)
    def _():
        ref[...] = val

    @pl.when(i != 0)
    def _():
        ref[...] += val


def _in_proj_fwd(x, g_mix, w_in, T, tm):
    def body(i, x_ref, g_ref, w_ref, h_ref, *rest):
        outs, pj_s = rest[:-1], rest[-1]
        g = g_ref[...]

        def norm(rows):
            h_ref[rows, :] = (_rms(x_ref[rows, :])[0] * g).astype(MM)

        _by_chunks(tm, norm)
        for d in range(N_DEV):
            pj_s[d] = _dot_nt(h_ref[...], w_ref[d])

        def join_and_cut(rows):
            proj = jnp.concatenate([pj_s[d, rows, :] for d in range(N_DEV)], axis=1)
            for (s, n), o_ref in zip(COL_SECTIONS, outs):
                if n == QK_ROPE:
                    o_ref[rows, :] = jnp.concatenate(
                        [jnp.zeros((ROW_CHUNK, QK_NOPE), F32), proj[:, s:s + n],
                         jnp.zeros((ROW_CHUNK, HEAD_PAD - QK_DIM), F32)], axis=1)
                else:
                    o_ref[rows, :] = proj[:, s:s + n]

        _by_chunks(tm, join_and_cut)

    row_outs = [(D_MODEL, MM)] + [(n, F32) for _, n in SECTIONS]
    return _row_call("in_proj_fwd", body, T, tm, [x], [g_mix, w_in], row_outs, [], VMEM_LIMIT,
                     scratch=[pltpu.VMEM((N_DEV, tm, IN_BLOCK), F32)])


def _mla_heads_fwd(raw, g_pad, cos_t, sin_t, first):
    outs, saved = [], []
    for h in range(MLA_HEADS):
        xh, r = _rms(raw[:, h * HEAD_PAD:(h + 1) * HEAD_PAD], QK_DIM)
        y = xh * g_pad
        outs.append(y * cos_t + _rope_swap(y, first) * sin_t)
        saved.append((xh, r))
    return outs, saved


def _mla_raw_heads(cqn, ckvn, kr, wuq_ref, wukv_ref, tm):
    lane = lax.broadcasted_iota(jnp.int32, (tm, HEAD_PAD), 1)
    nope = lane < QK_NOPE
    one_lane = jnp.where(lane == V_DIM, 1.0, 0.0)
    qs, ks, vs = [], [], []
    for h in range(MLA_HEADS):
        qs.append(_dot_nt(cqn, wuq_ref[h]))
        kv = _dot(ckvn, wukv_ref[h])
        ks.append(jnp.where(nope, kv, kr))
        vs.append(jnp.where(nope, pltpu.roll(kv, V_DIM, 1), one_lane))
    return jnp.concatenate(qs, axis=1), jnp.concatenate(ks, axis=1), jnp.concatenate(vs, axis=1)


def _mla_prep_fwd(cq, ckv, kr, pos, g_qa, g_kva, g_qn, g_kn, w_uq, w_ukv, T, tm):
    def body(i, cq_ref, ckv_ref, kr_ref, pos_ref, gqa_ref, gkva_ref, gqn_ref, gkn_ref, wuq_ref, wukv_ref,
             q_ref, k_ref, v_ref):
        cos_t, sin_t, first = _rope_tables(pos_ref[...], tm)
        cqn = _rms(cq_ref[...])[0] * gqa_ref[...]
        ckvn = _rms(ckv_ref[...])[0] * gkva_ref[...]
        q_raw, k_raw, v = _mla_raw_heads(cqn, ckvn, kr_ref[...], wuq_ref, wukv_ref, tm)
        qs, _ = _mla_heads_fwd(q_raw, gqn_ref[...], cos_t, sin_t, first)
        ks, _ = _mla_heads_fwd(k_raw, gkn_ref[...], cos_t, sin_t, first)
        q_ref[...] = (jnp.concatenate(qs, axis=1) * ATT_SCALE).astype(MM)
        k_ref[...] = jnp.concatenate(ks, axis=1).astype(MM)
        v_ref[...] = v.astype(MM)

    w = MLA_HEADS * HEAD_PAD
    return _row_call("mla_prep_fwd", body, T, tm, [cq, ckv, kr, pos], [g_qa, g_kva, g_qn, g_kn, w_uq, w_ukv],
                     [(w, MM), (w, MM), (w, MM)], [])


def _causal_pairs(n, by_query):
    if by_query:
        pairs = [(q, k) for q in range(n) for k in range(q + 1)]
    else:
        pairs = [(q, k) for k in range(n) for q in range(k, n)]
    return np.array([p[0] for p in pairs], np.int32), np.array([p[1] for p in pairs], np.int32)


def _flash_fwd(qf, kf, vf, T, ag_blocks=()):
    tq = min(ATT_TILE, T)
    nq = T // tq

    qi_tab, ki_tab = _causal_pairs(nq, by_query=True)

    hp = ATT_HEADS

    n_ag = len(ag_blocks)
    n_heads, n_pairs = MLA_HEADS // hp, len(qi_tab)

    def body(qi_ref, ki_ref, q_ref, k_ref, v_ref, *rest):
        ag_in, (o_ref, lse_ref), rest = rest[:n_ag], rest[n_ag:n_ag + 2], rest[n_ag + 2:]
        ag_out, (m_s, acc_s), ag_sems = rest[:n_ag], rest[n_ag:n_ag + 2], rest[n_ag + 2:]
        t = pl.program_id(1)
        qi, ki = qi_ref[t], ki_ref[t]
        if n_ag:
            @pl.when((pl.program_id(0) == 0) & (t == 0))
            def _():
                _ag_start(ag_in, ag_out, ag_sems)

        @pl.when(ki == 0)
        def _():
            m_s[...] = jnp.full_like(m_s, NEG)
            acc_s[...] = jnp.zeros_like(acc_s)

        def step(masked):
            for hh in range(hp):
                hs = slice(hh * HEAD_PAD, (hh + 1) * HEAD_PAD)
                s_t = _dot_nt(k_ref[:, hs], q_ref[:, hs])
                if masked:
                    key = lax.broadcasted_iota(jnp.int32, (tq, tq), 0)
                    qry = lax.broadcasted_iota(jnp.int32, (tq, tq), 1)
                    s_t = jnp.where(key <= qry, s_t, NEG)
                m_old = m_s[hh]
                m_new = jnp.maximum(m_old, jnp.max(s_t, axis=0, keepdims=True))
                p_t = jnp.exp(s_t - m_new)
                acc_s[hh] = jnp.exp(m_old - m_new) * acc_s[hh] + _dot_tn(v_ref[:, hs], p_t)
                m_s[hh] = m_new

        @pl.when(ki < qi)
        def _():
            step(False)

        @pl.when(ki == qi)
        def _():
            step(True)
            real = lax.broadcasted_iota(jnp.int32, (HEAD_PAD, tq), 0) < V_DIM
            for hh in range(hp):
                hs = slice(hh * HEAD_PAD, (hh + 1) * HEAD_PAD)
                acc = acc_s[hh]
                l = acc[V_DIM:V_DIM + 1]
                o_ref[:, hs] = jnp.where(real, acc / l, 0.0).T
                lse_ref[:, hs] = jnp.broadcast_to(m_s[hh] + jnp.log(l), (HEAD_PAD, tq)).T

        if n_ag:
            @pl.when((pl.program_id(0) == n_heads - 1) & (t == n_pairs - 1))
            def _():
                _ag_finish(ag_in, ag_out, ag_sems)

    q_spec = pl.BlockSpec((tq, hp * HEAD_PAD), lambda h, t, qi_ref, ki_ref: (qi_ref[t], h))
    kv_spec = pl.BlockSpec((tq, hp * HEAD_PAD), lambda h, t, qi_ref, ki_ref: (ki_ref[t], h))
    any_spec = pl.BlockSpec(memory_space=pl.ANY)
    grid_spec = pltpu.PrefetchScalarGridSpec(
        num_scalar_prefetch=2, grid=(n_heads, n_pairs),
        in_specs=[q_spec, kv_spec, kv_spec] + [any_spec] * n_ag, out_specs=[q_spec, q_spec] + [any_spec] * n_ag,
        scratch_shapes=[pltpu.VMEM((hp, 1, tq), F32), pltpu.VMEM((hp, HEAD_PAD, tq), F32)]
        + (_ag_sems(n_ag) if n_ag else []))
    return pl.pallas_call(
        body, name="flash_fwd", grid_spec=grid_spec,
        out_shape=[jax.ShapeDtypeStruct((T, MLA_HEADS * HEAD_PAD), F32)] * 2 + _ag_out_shapes(ag_blocks),
        compiler_params=_cparams(("arbitrary", "arbitrary")),
    )(jnp.asarray(qi_tab), jnp.asarray(ki_tab), qf, kf, vf, *ag_blocks)


def _hg_gates(hf, lb):
    sg = _sigmoid(hf)
    f = lb + (1.0 - lb) * sg
    return sg, f, jnp.log(f), 1.0 - f


def _tri(n, lower):
    r = lax.broadcasted_iota(jnp.int32, (n, n), 0)
    c = lax.broadcasted_iota(jnp.int32, (n, n), 1)
    return jnp.where((c <= r) if lower else (c >= r), 1.0, 0.0).astype(F32)


def _hg_intra(q, k, b, pairwise):
    C, S = HG_CHUNK, HG_SUB
    row_c = lax.broadcasted_iota(jnp.int32, (C, HG_DIM), 0)
    row_s = lax.broadcasted_iota(jnp.int32, (S, HG_DIM), 0)
    lane_c = lax.broadcasted_iota(jnp.int32, (S, C), 1)
    row_sc = lax.broadcasted_iota(jnp.int32, (S, C), 0)
    blocks, saved = [], []
    for blk in range(C // S):
        lo = blk * S
        q_b, k_b, b_b = q[lo:lo + S], k[lo:lo + S], b[lo:lo + S]
        keys = lo if pairwise else lo + S
        if keys:
            ref = b[lo - 1:lo] if blk else jnp.zeros((1, HG_DIM), F32)
            q_e = jnp.exp(b_b - ref)
            q_t = q_b * q_e
            k_e = jnp.exp(jnp.where(row_c < keys, ref - b, NEG))
            a_b = _dot_nt(q_t, k * k_e)
            if not pairwise:
                a_b = jnp.where(lane_c <= lo + row_sc, a_b, 0.0)
            saved.append((q_t, k_e, q_e))
        else:
            a_b = jnp.zeros((S, C), F32)
            saved.append(None)
        if pairwise:
            for j in range(S):
                w = jnp.exp(jnp.where(row_s >= j, b_b - b_b[j:j + 1], NEG))
                col = jnp.sum(q_b * (k_b[j:j + 1] * w), axis=1, keepdims=True)
                a_b = jnp.where(lane_c == lo + j, col, a_b)
        blocks.append(a_b)
    return jnp.concatenate(blocks, axis=0), saved


def _hg_intra_bwd(d_a, q, k, b, saved, pairwise):
    C, S = HG_CHUNK, HG_SUB
    row_s = lax.broadcasted_iota(jnp.int32, (S, HG_DIM), 0)
    lane_sc = lax.broadcasted_iota(jnp.int32, (S, C), 1)
    dk = jnp.zeros((C, HG_DIM), F32)
    dq_blocks, dk_diag = [], []
    for blk in range(C // S):
        lo = blk * S
        q_b, k_b, b_b = q[lo:lo + S], k[lo:lo + S], b[lo:lo + S]
        da_b = d_a[lo:lo + S]
        dq_b = jnp.zeros((S, HG_DIM), F32)
        dk_b = jnp.zeros((S, HG_DIM), F32)
        if pairwise:
            for j in range(S):
                w = jnp.exp(jnp.where(row_s >= j, b_b - b_b[j:j + 1], NEG))
                col = jnp.sum(jnp.where(lane_sc == lo + j, da_b, 0.0), axis=1, keepdims=True)
                dq_b = dq_b + col * (k_b[j:j + 1] * w)
                dk_row = jnp.sum(col * (q_b * w), axis=0, keepdims=True)
                dk_b = jnp.where(row_s == j, dk_row, dk_b)
        if saved[blk] is not None:
            q_t, k_e, q_e = saved[blk]
            da_mm = jnp.where(lane_sc < lo, da_b, 0.0) if pairwise else da_b
            dq_b = dq_b + _dot(da_mm, k * k_e) * q_e
            dk = dk + _dot_tn(da_mm, q_t) * k_e
        dq_blocks.append(dq_b)
        dk_diag.append(dk_b)
    return jnp.concatenate(dq_blocks, axis=0), dk + jnp.concatenate(dk_diag, axis=0)


def _hgrn_fwd(hq, hf, hi, lb, T):
    rb = min(HG_BLOCK, T)
    ncb = rb // HG_CHUNK

    def body(hq_ref, hf_ref, hi_ref, lb_ref, o_ref, s0_ref, st_ref, a_s):
        @pl.when(pl.program_id(0) == 0)
        def _():
            st_ref[...] = jnp.zeros_like(st_ref)

        tril = _tri(HG_CHUNK, True)

        def chunk(c, carry):
            rows = pl.ds(pl.multiple_of(c * HG_CHUNK, HG_CHUNK), HG_CHUNK)
            _, _, logf, kk = _hg_gates(hf_ref[rows, :], lb_ref[...])
            b = _dot_hi(tril, logf)
            q_all, v_all = hq_ref[rows, :], hi_ref[rows, :]
            heads = [slice(h * HG_DIM, (h + 1) * HG_DIM) for h in range(HG_HEADS)]

            def scores(pairwise):
                for h, ls in enumerate(heads):
                    a_s[h] = _hg_intra(q_all[:, ls], kk[:, ls], b[:, ls], pairwise)[0]

            mild = jnp.min(logf) >= -HG_SAFE_STEP
            pl.when(mild)(functools.partial(scores, False))
            pl.when(jnp.logical_not(mild))(functools.partial(scores, True))
            outs = []
            for h, ls in enumerate(heads):
                q, k, v, bh = q_all[:, ls], kk[:, ls], v_all[:, ls], b[:, ls]
                st = st_ref[h]
                s0_ref[c, h * HG_DIM:(h + 1) * HG_DIM, :] = st
                b_end = bh[HG_CHUNK - 1:HG_CHUNK]
                outs.append(_dot_nt(q * jnp.exp(bh), st) + _dot(a_s[h], v))
                st_ref[h] = st * jnp.exp(b_end) + _dot_tn(v, k * jnp.exp(b_end - bh))
            o_ref[rows, :] = jnp.concatenate(outs, axis=1)
            return carry

        lax.fori_loop(0, ncb, chunk, 0)

    row = pl.BlockSpec((rb, HG_W), lambda i: (i, 0))
    return pl.pallas_call(
        body, name="hgrn_fwd", grid=(T // rb,),
        in_specs=[row, row, row, pl.BlockSpec((1, HG_W), lambda i: (0, 0))],
        out_specs=[row, pl.BlockSpec((ncb, HG_W, HG_DIM), lambda i: (i, 0, 0))],
        out_shape=[jax.ShapeDtypeStruct((T, HG_W), F32), jax.ShapeDtypeStruct((T // HG_CHUNK, HG_W, HG_DIM), F32)],
        scratch_shapes=[pltpu.VMEM((HG_HEADS, HG_DIM, HG_DIM), F32), pltpu.VMEM((HG_HEADS, HG_CHUNK, HG_CHUNK), F32)],
        compiler_params=_cparams(("arbitrary",)),
    )(hq, hf, hi, lb)


def _hgrn_bwd(hq, hf, hi, do, s0, lb, T, xchg=()):
    rb = min(HG_BLOCK, T)
    ncb = rb // HG_CHUNK
    nb = T // rb
    C, S = HG_CHUNK, HG_SUB
    n_x = len(xchg)

    def body(hq_ref, hf_ref, hi_ref, do_ref, s0_ref, lb_ref, *rest):
        x_in, (dq_ref, df_ref, dv_ref, dlb_ref), rest = rest[:n_x], rest[n_x:n_x + 4], rest[n_x + 4:]
        x_out, (dst_ref, a_s, dqi_s, dki_s), x_sems = rest[:n_x], rest[n_x:n_x + 4], rest[n_x + 4:]

        @pl.when(pl.program_id(0) == 0)
        def _():
            dst_ref[...] = jnp.zeros_like(dst_ref)
            dlb_ref[...] = jnp.zeros_like(dlb_ref)
            for cp in _xchips_copies(x_in, x_out, x_sems) if n_x else ():
                cp.start()

        tril, triu = _tri(C, True), _tri(C, False)
        row_cc = lax.broadcasted_iota(jnp.int32, (C, C), 0)
        col_cc = lax.broadcasted_iota(jnp.int32, (C, C), 1)
        last_row = lax.broadcasted_iota(jnp.int32, (C, HG_DIM), 0) == C - 1
        lb_v = lb_ref[...]

        def chunk(cc, carry):
            c = ncb - 1 - cc
            rows = pl.ds(pl.multiple_of(c * C, C), C)
            hf_c = hf_ref[rows, :]
            sg, f, logf, kk = _hg_gates(hf_c, lb_v)
            b = _dot_hi(tril, logf)
            q_all, v_all, do_all = hq_ref[rows, :], hi_ref[rows, :], do_ref[rows, :]
            heads = [slice(h * HG_DIM, (h + 1) * HG_DIM) for h in range(HG_HEADS)]

            def intra(pairwise):
                for h, ls in enumerate(heads):
                    q, k, v, bh, d_o = q_all[:, ls], kk[:, ls], v_all[:, ls], b[:, ls], do_all[:, ls]
                    a, saved = _hg_intra(q, k, bh, pairwise)
                    d_a = jnp.where(col_cc <= row_cc, _dot_nt(d_o, v), 0.0)
                    a_s[h] = a
                    dqi_s[:, ls], dki_s[:, ls] = _hg_intra_bwd(d_a, q, k, bh, saved, pairwise)

            mild = jnp.min(logf) >= -HG_SAFE_STEP
            pl.when(mild)(functools.partial(intra, False))
            pl.when(jnp.logical_not(mild))(functools.partial(intra, True))
            dq_o, dk_o, dv_o, db_o = [], [], [], []
            for h, ls in enumerate(heads):
                q, k, v, bh, d_o = q_all[:, ls], kk[:, ls], v_all[:, ls], b[:, ls], do_all[:, ls]
                st0 = s0_ref[c, h * HG_DIM:(h + 1) * HG_DIM, :]
                dst = dst_ref[h]
                b_end = bh[C - 1:C]
                e_b, e_end = jnp.exp(bh), jnp.exp(b_end)
                e_rem = jnp.exp(b_end - bh)
                qe, kd = q * e_b, k * e_rem
                st_end = st0 * e_end + _dot_tn(v, kd)
                dv = _dot_tn(a_s[h], d_o) + _dot_nt(kd, dst)
                dq = e_b * _dot(d_o, st0) + dqi_s[:, ls]
                dk = e_rem * _dot(v, dst) + dki_s[:, ls]
                extra = jnp.sum(dst * st_end, axis=0, keepdims=True)
                db_o.append(q * dq - k * dk + jnp.where(last_row, extra, 0.0))
                dst_ref[h] = dst * e_end + _dot_tn(d_o, qe)
                dq_o.append(dq)
                dk_o.append(dk)
                dv_o.append(dv)
            dlogf = _dot_hi(triu, jnp.concatenate(db_o, axis=1))
            d_f = dlogf / f - jnp.concatenate(dk_o, axis=1)
            dq_ref[rows, :] = jnp.concatenate(dq_o, axis=1).astype(MM)
            dv_ref[rows, :] = jnp.concatenate(dv_o, axis=1).astype(MM)
            df_ref[rows, :] = (d_f * (1.0 - lb_v) * sg * (1.0 - sg)).astype(MM)
            dlb_ref[...] += jnp.sum(d_f * (1.0 - sg), axis=0, keepdims=True)
            return carry

        lax.fori_loop(0, ncb, chunk, 0)

        if n_x:
            @pl.when(pl.program_id(0) == nb - 1)
            def _():
                for cp in _xchips_copies(x_in, x_out, x_sems):
                    cp.wait()

    row = pl.BlockSpec((rb, HG_W), lambda i: (nb - 1 - i, 0))
    one = pl.BlockSpec((1, HG_W), lambda i: (0, 0))
    any_spec = pl.BlockSpec(memory_space=pl.ANY)
    return pl.pallas_call(
        body, name="hgrn_bwd", grid=(nb,),
        in_specs=[row, row, row, row, pl.BlockSpec((ncb, HG_W, HG_DIM), lambda i: (nb - 1 - i, 0, 0)), one]
        + [any_spec] * n_x,
        out_specs=[row, row, row, one] + [any_spec] * n_x,
        out_shape=[jax.ShapeDtypeStruct((T, HG_W), MM)] * 3 + [jax.ShapeDtypeStruct((1, HG_W), F32)]
        + _xchips_out_shapes(xchg),
        scratch_shapes=[pltpu.VMEM((HG_HEADS, HG_DIM, HG_DIM), F32), pltpu.VMEM((HG_HEADS, C, C), F32),
                        pltpu.VMEM((C, HG_W), F32), pltpu.VMEM((C, HG_W), F32)] + (_xchips_sems(n_x) if n_x else []),
        compiler_params=_cparams(("arbitrary",)),
    )(hq, hf, hi, do, s0, lb, *xchg)


def _silu_parts(x):
    sg = _sigmoid(x)
    return x * sg, sg * (1.0 + x * (1.0 - sg))


def _merge_fwd(attn, o, hg, bg, x, g_out, w_bra, w_brb, w_out, T, tm):
    def body(i, attn_ref, o_ref, hg_ref, bg_ref, x_ref, g_ref, wa_ref, wb_ref, wo_ref,
             x1_ref, ya_ref, yb_ref, m_ref, rec_ref):
        g = g_ref[...]

        def recurrent_out(rows):
            for h in range(HG_HEADS):
                ls = slice(h * HG_DIM, (h + 1) * HG_DIM)
                rec_ref[rows, ls] = (_rms(o_ref[rows, ls])[0] * g * _silu_parts(hg_ref[rows, ls])[0]).astype(MM)

        _by_chunks(tm, recurrent_out)
        ya_ref[...] = _dot(attn_ref[...], wa_ref[...])
        yb_ref[...] = jnp.dot(rec_ref[...], wb_ref[...], preferred_element_type=F32)

        def gate(rows):
            m_ref[rows, :] = (_sigmoid(bg_ref[rows, :D_MODEL]) * ya_ref[rows, :]
                              + _sigmoid(bg_ref[rows, D_MODEL:]) * yb_ref[rows, :]).astype(MM)

        _by_chunks(tm, gate)
        x1_ref[...] = x_ref[...] + jnp.dot(m_ref[...], wo_ref[...], preferred_element_type=F32)

    return _row_call("merge_fwd", body, T, tm, [attn, o, hg, bg, x], [g_out, w_bra, w_brb, w_out],
                     [(D_MODEL, F32), (D_MODEL, F32), (D_MODEL, F32), (D_MODEL, MM), (HG_W, MM)], [], VMEM_LIMIT)


def _ffn_fwd(x1, g_ffn, w_g, w_u, w_d, T, tm):
    def body(i, x1_ref, g_ref, wg_ref, wu_ref, wd_ref, x2_ref, gt_ref, up_ref, h2_ref, a_s):
        g = g_ref[...]

        def norm(rows):
            h2_ref[rows, :] = (_rms(x1_ref[rows, :])[0] * g).astype(MM)

        _by_chunks(tm, norm)
        gt_ref[...] = _dot_nt(h2_ref[...], wg_ref[...])
        up_ref[...] = _dot_nt(h2_ref[...], wu_ref[...])

        def act(rows):
            for cs in FFN_HALVES:
                a_s[rows, cs] = (_silu_parts(gt_ref[rows, cs])[0] * up_ref[rows, cs]).astype(MM)

        _by_chunks(tm, act)
        x2_ref[...] = x1_ref[...] + jnp.dot(a_s[...], wd_ref[...], preferred_element_type=F32)

    return _row_call("ffn_fwd", body, T, tm, [x1], [g_ffn, w_g, w_u, w_d],
                     [(D_MODEL, F32), (FFN, F32), (FFN, F32), (D_MODEL, MM)], [], VMEM_LIMIT,
                     scratch=[pltpu.VMEM((tm, FFN), MM)])


def _ple_loss(x2, p, tgt, g_pg, g_post, w_pg, w_pp, T, tm):
    def body(i, x2_ref, p_ref, t_ref, gpg_ref, gpo_ref, wpg_ref, wpp_ref,
             dx2_ref, loss_ref, dgpo_ref, dgpg_ref, dwpg_ref, dwpp_ref, u_s, n3_s, z_s, dz_s, du_s, dy_s, dn3_s):
        @pl.when(i == 0)
        def _():
            for ref in (loss_ref, dgpo_ref, dgpg_ref, dwpg_ref, dwpp_ref):
                ref[...] = jnp.zeros_like(ref)

        gpg, gpo = gpg_ref[...], gpo_ref[...]
        p_mm = p_ref[...].astype(MM)
        for d in range(N_DEV):
            u_s[:, d * HEAD_PAD:(d + 1) * HEAD_PAD] = jnp.dot(p_mm, wpp_ref[d], preferred_element_type=F32)

        def gate_input(rows):
            n3_s[rows, :] = (_rms(x2_ref[rows, :])[0] * gpg).astype(MM)

        _by_chunks(tm, gate_input)
        z_s[...] = jnp.dot(n3_s[...], wpg_ref[...], preferred_element_type=F32)

        def loss_and_back(rows):
            uh, ru = _rms(u_s[rows, :])
            e = uh * gpo
            gate = _sigmoid(z_s[rows, :])
            diff = x2_ref[rows, :] + gate * e - t_ref[rows, :]
            dy = diff * (1.0 / D_MODEL)
            de = dy * gate
            dz_s[rows, :] = (dy * e * gate * (1.0 - gate)).astype(MM)
            du_s[rows, :] = _rms_bwd(de * gpo, uh, ru).astype(MM)
            dy_s[rows, :] = dy
            loss_ref[...] += _fold8(diff * diff) * (0.5 / D_MODEL)
            dgpo_ref[...] += _fold8(de * uh)

        _by_chunks(tm, loss_and_back)
        dn3_s[...] = _dot_nt(dz_s[...], wpg_ref[...])

        def gate_norm_back(rows):
            x2h, r3 = _rms(x2_ref[rows, :])
            dn3 = dn3_s[rows, :]
            dx2_ref[rows, :] = dy_s[rows, :] + _rms_bwd(dn3 * gpg, x2h, r3)
            dgpg_ref[...] += _fold8(dn3 * x2h)

        _by_chunks(tm, gate_norm_back)
        dwpg_ref[...] += _dot_tn(n3_s[...], dz_s[...])
        for d in range(N_DEV):
            dwpp_ref[d] += _dot_tn(p_mm, du_s[:, d * HEAD_PAD:(d + 1) * HEAD_PAD])

    vec = ((8, D_MODEL), F32)
    wide = lambda dt: pltpu.VMEM((tm, D_MODEL), dt)
    return _row_call("ple_loss", body, T, tm, [x2, p, tgt], [g_pg, g_post, w_pg, w_pp], [(D_MODEL, F32)],
                     [vec, vec, vec, ((D_MODEL, D_MODEL), F32), ((N_DEV, PLE, HEAD_PAD), F32)], VMEM_LIMIT,
                     scratch=[wide(F32), wide(MM), wide(F32), wide(MM), wide(MM), wide(F32), wide(F32)])


def _ffn_bwd(dx2, x1, gt, up, g_ffn, w_g, w_u, w_d, T, tm):
    def body(i, dx2_ref, x1_ref, gt_ref, up_ref, g_ref, wg_ref, wu_ref, wd_ref,
             dx1_ref, a_ref, dgt_ref, dup_ref, dg_ref, da_s, dh2_s):
        @pl.when(i == 0)
        def _():
            dg_ref[...] = jnp.zeros_like(dg_ref)

        g = g_ref[...]
        da_s[...] = _dot_nt(dx2_ref[...], wd_ref[...])

        def act_back(rows):
            for cs in FFN_HALVES:
                up, da = up_ref[rows, cs], da_s[rows, cs]
                silu, dsilu = _silu_parts(gt_ref[rows, cs])
                dgt_ref[rows, cs] = (da * up * dsilu).astype(MM)
                dup_ref[rows, cs] = (da * silu).astype(MM)
                a_ref[rows, cs] = (silu * up).astype(MM)

        _by_chunks(tm, act_back)
        dh2_s[...] = (jnp.dot(dgt_ref[...], wg_ref[...], preferred_element_type=F32)
                      + jnp.dot(dup_ref[...], wu_ref[...], preferred_element_type=F32))

        def norm_back(rows):
            x1h, r = _rms(x1_ref[rows, :])
            dh2 = dh2_s[rows, :]
            dx1_ref[rows, :] = dx2_ref[rows, :] + _rms_bwd(dh2 * g, x1h, r)
            dg_ref[...] += _fold8(dh2 * x1h)

        _by_chunks(tm, norm_back)

    return _row_call("ffn_bwd", body, T, tm, [dx2, x1, gt, up], [g_ffn, w_g, w_u, w_d],
                     [(D_MODEL, F32), (FFN, MM), (FFN, MM), (FFN, MM)], [((8, D_MODEL), F32)], VMEM_LIMIT,
                     scratch=[pltpu.VMEM((tm, FFN), F32), pltpu.VMEM((tm, D_MODEL), F32)])


def _merge_bwd(dx1, ya, yb, bg, o, hg, attn, m, rec, g_out, w_bra, w_brb, w_out, T, tm):
    def body(i, dx1_ref, ya_ref, yb_ref, bg_ref, o_ref, hg_ref, attn_ref, m_ref, rec_ref, g_ref, wa_ref, wb_ref, wo_ref,
             dattn_ref, do_ref, dhg_ref, dbg_ref, dg_ref, dwo_ref, dwa_ref, dwb_ref, dm_s, dya_s, dyb_s, drec_s):
        @pl.when(i == 0)
        def _():
            for ref in (dg_ref, dwo_ref, dwa_ref, dwb_ref):
                ref[...] = jnp.zeros_like(ref)

        g = g_ref[...]
        dx1 = dx1_ref[...].astype(MM)
        dm_s[...] = _dot_nt(dx1, wo_ref[...])

        def gate_back(rows):
            dm = dm_s[rows, :]
            ga, gb = _sigmoid(bg_ref[rows, :D_MODEL]), _sigmoid(bg_ref[rows, D_MODEL:])
            dya_s[rows, :] = (dm * ga).astype(MM)
            dyb_s[rows, :] = (dm * gb).astype(MM)
            dbg_ref[rows, :D_MODEL] = (dm * ya_ref[rows, :] * ga * (1.0 - ga)).astype(MM)
            dbg_ref[rows, D_MODEL:] = (dm * yb_ref[rows, :] * gb * (1.0 - gb)).astype(MM)

        _by_chunks(tm, gate_back)
        dwo_ref[...] += _dot_tn(m_ref[...], dx1)
        attn_mm = attn_ref[...].astype(MM)
        for d in range(N_DEV):
            ds = slice(d * HEAD_PAD, (d + 1) * HEAD_PAD)
            dwa_ref[d] += _dot_tn(attn_mm, dya_s[:, ds])
            dwb_ref[d] += _dot_tn(rec_ref[...], dyb_s[:, ds])
        dattn_ref[...] = _dot_nt(dya_s[...], wa_ref[...])
        drec_s[...] = _dot_nt(dyb_s[...], wb_ref[...])

        def recurrent_out_back(rows):
            for h in range(HG_HEADS):
                ls = slice(h * HG_DIM, (h + 1) * HG_DIM)
                oh, r = _rms(o_ref[rows, ls])
                silu, dsilu = _silu_parts(hg_ref[rows, ls])
                dr = drec_s[rows, ls]
                dhg_ref[rows, ls] = (dr * oh * g * dsilu).astype(MM)
                don = dr * silu
                dg_ref[...] += _fold8(don * oh)
                do_ref[rows, ls] = _rms_bwd(don * g, oh, r)

        _by_chunks(tm, recurrent_out_back)

    wide = lambda n, dt: pltpu.VMEM((tm, n), dt)
    return _row_call("merge_bwd", body, T, tm, [dx1, ya, yb, bg, o, hg, attn, m, rec], [g_out, w_bra, w_brb, w_out],
                     [(D_MODEL, F32), (HG_W, F32), (HG_W, MM), (2 * D_MODEL, MM)],
                     [((8, HG_DIM), F32), ((D_MODEL, D_MODEL), F32), ((N_DEV, MLA_HEADS * HEAD_PAD, HEAD_PAD), F32),
                      ((N_DEV, HG_W, HEAD_PAD), F32)], VMEM_LIMIT,
                     scratch=[wide(D_MODEL, F32), wide(D_MODEL, MM), wide(D_MODEL, MM), wide(HG_W, F32)])


def _flash_bwd(qf, kf, vf, o, do, lse, T, xchg=()):
    tq = min(ATT_TILE, T)
    nq = T // tq

    qi_tab, ki_tab = _causal_pairs(nq, by_query=False)

    n_x = len(xchg)
    hp = ATT_HEADS
    n_heads, n_pairs = MLA_HEADS // hp, len(qi_tab)

    def body(qi_ref, ki_ref, q_ref, k_ref, v_ref, o_ref, do_ref, lse_ref, *rest):
        x_in, (dq_ref, dk_ref, dv_ref), rest = rest[:n_x], rest[n_x:n_x + 3], rest[n_x + 3:]
        x_out, x_sems = rest[:n_x], rest[n_x:]
        t = pl.program_id(1)
        qi, ki = qi_ref[t], ki_ref[t]
        if n_x:
            @pl.when((pl.program_id(0) == 0) & (t == 0))
            def _():
                for cp in _xchips_copies(x_in, x_out, x_sems):
                    cp.start()

        @pl.when(t == 0)
        def _():
            dq_ref[...] = jnp.zeros_like(dq_ref)

        def step(first):
            rows = pl.ds(pl.multiple_of(qi * tq, tq), tq)
            for hh in range(hp):
                hs = slice(hh * HEAD_PAD, (hh + 1) * HEAD_PAD)
                q, k, d_o = q_ref[:, hs], k_ref[:, hs], do_ref[:, hs]
                s = _dot_nt(q, k)
                if first:
                    row = lax.broadcasted_iota(jnp.int32, (tq, tq), 0)
                    col = lax.broadcasted_iota(jnp.int32, (tq, tq), 1)
                    s = jnp.where(col <= row, s, NEG)
                p = jnp.exp(s - lse_ref[:, hh * HEAD_PAD:hh * HEAD_PAD + 1])
                delta = jnp.sum(d_o * o_ref[:, hs], axis=1, keepdims=True)
                ds = p * (_dot_nt(d_o, v_ref[:, hs]) - delta)
                dq_ref[rows, hs] += _dot(ds, k)
                if first:
                    dv_ref[:, hs] = _dot_tn(p, d_o)
                    dk_ref[:, hs] = _dot_tn(ds, q)
                else:
                    dv_ref[:, hs] += _dot_tn(p, d_o)
                    dk_ref[:, hs] += _dot_tn(ds, q)

        @pl.when(qi == ki)
        def _():
            step(True)

        @pl.when(qi > ki)
        def _():
            step(False)

        if n_x:
            @pl.when((pl.program_id(0) == n_heads - 1) & (t == n_pairs - 1))
            def _():
                for cp in _xchips_copies(x_in, x_out, x_sems):
                    cp.wait()

    q_spec = pl.BlockSpec((tq, hp * HEAD_PAD), lambda h, t, qi_ref, ki_ref: (qi_ref[t], h))
    kv_spec = pl.BlockSpec((tq, hp * HEAD_PAD), lambda h, t, qi_ref, ki_ref: (ki_ref[t], h))
    any_spec = pl.BlockSpec(memory_space=pl.ANY)
    w = MLA_HEADS * HEAD_PAD
    grid_spec = pltpu.PrefetchScalarGridSpec(
        num_scalar_prefetch=2, grid=(n_heads, n_pairs),
        in_specs=[q_spec, kv_spec, kv_spec, q_spec, q_spec, q_spec] + [any_spec] * n_x,
        out_specs=[pl.BlockSpec((T, hp * HEAD_PAD), lambda h, t, qi_ref, ki_ref: (0, h)), kv_spec, kv_spec]
        + [any_spec] * n_x,
        scratch_shapes=_xchips_sems(n_x) if n_x else [])
    return pl.pallas_call(
        body, name="flash_bwd", grid_spec=grid_spec,
        out_shape=[jax.ShapeDtypeStruct((T, w), F32)] * 3 + _xchips_out_shapes(xchg),
        compiler_params=_cparams(("arbitrary", "arbitrary")),
    )(jnp.asarray(qi_tab), jnp.asarray(ki_tab), qf, kf, vf, o, do, lse, *xchg)


def _mla_heads_bwd(d_out, saved, g_pad, cos_t, sin_t, first):
    d_raw, dg = [], jnp.zeros((1, HEAD_PAD), F32)
    for h in range(MLA_HEADS):
        xh, r = saved[h]
        dy = d_out[:, h * HEAD_PAD:(h + 1) * HEAD_PAD]
        dn = dy * cos_t + _rope_swap(dy * sin_t, first)
        dg = dg + jnp.sum(dn * xh, axis=0, keepdims=True)
        d_raw.append(_rms_bwd(dn * g_pad, xh, r, QK_DIM))
    return d_raw, dg


def _mla_prep_bwd(cq, ckv, kr, pos, dqf, dkf, dvf, g_qa, g_kva, g_qn, g_kn, w_uq, w_ukv, T, tm):
    def body(i, cq_ref, ckv_ref, kr_ref, pos_ref, dq_ref, dk_ref, dv_ref,
             gqa_ref, gkva_ref, gqn_ref, gkn_ref, wuq_ref, wukv_ref,
             dcq_ref, dckv_ref, dkr_ref, dgqa_ref, dgkva_ref, dgqn_ref, dgkn_ref, dwuq_ref, dwukv_ref):
        cos_t, sin_t, first = _rope_tables(pos_ref[...], tm)
        cqh, rq = _rms(cq_ref[...])
        ckvh, rkv = _rms(ckv_ref[...])
        cqn, ckvn = cqh * gqa_ref[...], ckvh * gkva_ref[...]
        q_raw, k_raw, _ = _mla_raw_heads(cqn, ckvn, kr_ref[...], wuq_ref, wukv_ref, tm)
        _, q_saved = _mla_heads_fwd(q_raw, gqn_ref[...], cos_t, sin_t, first)
        _, k_saved = _mla_heads_fwd(k_raw, gkn_ref[...], cos_t, sin_t, first)
        dq_heads, dgqn = _mla_heads_bwd(dq_ref[...] * ATT_SCALE, q_saved, gqn_ref[...], cos_t, sin_t, first)
        dk_heads, dgkn = _mla_heads_bwd(dk_ref[...], k_saved, gkn_ref[...], cos_t, sin_t, first)
        lane = lax.broadcasted_iota(jnp.int32, (tm, HEAD_PAD), 1)
        nope = lane < QK_NOPE
        dcqn = jnp.zeros((tm, Q_RANK), F32)
        dckvn = jnp.zeros((tm, KV_RANK), F32)
        dkr = jnp.zeros((tm, HEAD_PAD), F32)
        cqn_mm, ckvn_mm = cqn.astype(MM), ckvn.astype(MM)
        for h in range(MLA_HEADS):
            hs = slice(h * HEAD_PAD, (h + 1) * HEAD_PAD)
            dq_h = dq_heads[h].astype(MM)
            dkv_h = jnp.where(nope, dk_heads[h], pltpu.roll(dv_ref[:, hs], V_DIM, 1)).astype(MM)
            _acc(dwuq_ref.at[h], i, _dot_tn(dq_h, cqn_mm))
            _acc(dwukv_ref.at[h], i, _dot_tn(ckvn_mm, dkv_h))
            dcqn = dcqn + jnp.dot(dq_h, wuq_ref[h], preferred_element_type=F32)
            dckvn = dckvn + lax.dot_general(dkv_h, wukv_ref[h], (((1,), (1,)), ((), ())), preferred_element_type=F32)
            dkr = dkr + dk_heads[h]
        dkr_ref[...] = jnp.where((lane >= QK_NOPE) & (lane < QK_DIM), dkr, 0.0).astype(MM)
        dcq_ref[...] = _rms_bwd(dcqn * gqa_ref[...], cqh, rq).astype(MM)
        dckv_ref[...] = _rms_bwd(dckvn * gkva_ref[...], ckvh, rkv).astype(MM)
        _acc(dgqa_ref, i, jnp.sum(dcqn * cqh, axis=0, keepdims=True))
        _acc(dgkva_ref, i, jnp.sum(dckvn * ckvh, axis=0, keepdims=True))
        _acc(dgqn_ref, i, dgqn)
        _acc(dgkn_ref, i, dgkn)

    return _row_call(
        "mla_prep_bwd", body, T, tm, [cq, ckv, kr, pos, dqf, dkf, dvf], [g_qa, g_kva, g_qn, g_kn, w_uq, w_ukv],
        [(Q_RANK, MM), (KV_RANK, MM), (HEAD_PAD, MM)],
        [((1, Q_RANK), F32), ((1, KV_RANK), F32), ((1, HEAD_PAD), F32), ((1, HEAD_PAD), F32),
         ((MLA_HEADS, HEAD_PAD, Q_RANK), F32), ((MLA_HEADS, KV_RANK, HEAD_PAD), F32)], VMEM_LIMIT)


def _in_proj_bwd(x, dx1, dsecs, g_mix, w_in, T, tm):
    def body(i, x_ref, dx1_ref, *rest):
        d_refs, (g_ref, w_ref, dx_ref, dp_ref, dg_ref, dh_s) = rest[:len(SECTIONS)], rest[len(SECTIONS):]

        @pl.when(i == 0)
        def _():
            dg_ref[...] = jnp.zeros_like(dg_ref)

        g = g_ref[...]

        def join_and_cut(rows):
            pieces = [(d_ref[rows, QK_NOPE:QK_DIM] if n == QK_ROPE else d_ref[rows, :]).astype(F32)
                      for (_, n), d_ref in zip(COL_SECTIONS, d_refs)]
            dproj = jnp.concatenate(pieces, axis=1)
            for d in range(N_DEV):
                dp_ref[d, rows, :] = dproj[:, d * IN_BLOCK:(d + 1) * IN_BLOCK].astype(MM)

        _by_chunks(tm, join_and_cut)
        dh = jnp.dot(dp_ref[0], w_ref[0], preferred_element_type=F32)
        for d in range(1, N_DEV):
            dh = dh + jnp.dot(dp_ref[d], w_ref[d], preferred_element_type=F32)
        dh_s[...] = dh

        def norm_back(rows):
            xh, r = _rms(x_ref[rows, :])
            dh_c = dh_s[rows, :]
            dx_ref[rows, :] = dx1_ref[rows, :] + _rms_bwd(dh_c * g, xh, r)
            dg_ref[...] += _fold8(dh_c * xh)

        _by_chunks(tm, norm_back)

    in_specs = [pl.BlockSpec((tm, a.shape[1]), lambda i: (i, 0)) for a in [x, dx1, *dsecs]]
    in_specs += [pl.BlockSpec(g_mix.shape, lambda i: (0, 0)),
                 pl.BlockSpec(w_in.shape, lambda i: (0, 0, 0), pipeline_mode=pl.Buffered(1))]

    def kern(*refs):
        body(pl.program_id(0), *refs)

    return pl.pallas_call(
        kern, name="in_proj_bwd", grid=(T // tm,), in_specs=in_specs,
        out_specs=[pl.BlockSpec((tm, D_MODEL), lambda i: (i, 0)),
                   pl.BlockSpec((N_DEV, tm, IN_BLOCK), lambda i: (0, i, 0)),
                   pl.BlockSpec((8, D_MODEL), lambda i: (0, 0))],
        out_shape=[jax.ShapeDtypeStruct((T, D_MODEL), F32), jax.ShapeDtypeStruct((N_DEV, T, IN_BLOCK), MM),
                   jax.ShapeDtypeStruct((8, D_MODEL), F32)],
        scratch_shapes=[pltpu.VMEM((tm, D_MODEL), F32)],
        compiler_params=_cparams(("arbitrary",), VMEM_LIMIT),
    )(x, dx1, *dsecs, g_mix, w_in)


def _pick_block(n, cap):
    best = None
    for cand in range(128, min(n, cap) + 1, 128):
        if n % cand == 0:
            best = cand
    return n if best is None else best


def _pick_rows(n, cap):
    best = n
    for cand in range(8, min(n, cap) + 1, 8):
        if n % cand == 0:
            best = cand
    return best


def _matmul_tn(name, a, b):
    T, M = a.shape
    N = b.shape[1]
    bm, bk = _pick_block(M, 1408), min(512, T)
    bn = _pick_block(N, 2560)

    def body(a_ref, b_ref, c_ref):
        @pl.when(pl.program_id(2) == 0)
        def _():
            c_ref[...] = jnp.zeros_like(c_ref)

        c_ref[...] += _dot_tn(a_ref[...], b_ref[...])

    return pl.pallas_call(
        body, name=name, grid=(M // bm, N // bn, T // bk),
        in_specs=[pl.BlockSpec((bk, bm), lambda i, j, k: (k, i)), pl.BlockSpec((bk, bn), lambda i, j, k: (k, j))],
        out_specs=pl.BlockSpec((bm, bn), lambda i, j, k: (i, j)), out_shape=jax.ShapeDtypeStruct((M, N), F32),
        compiler_params=_cparams(("parallel", "parallel", "arbitrary"), VMEM_LIMIT),
    )(a, b)


def _matmul_tn_blocks(name, a, b):
    T, M = a.shape
    nd, _, c = b.shape
    bm, bk = _pick_block(M, 512), min(512, T)

    def body(a_ref, b_ref, c_ref):
        @pl.when(pl.program_id(1) == 0)
        def _():
            c_ref[...] = jnp.zeros_like(c_ref)

        a_blk = a_ref[...].astype(MM)
        for d in range(nd):
            c_ref[d] += _dot_tn(b_ref[d], a_blk)

    return pl.pallas_call(
        body, name=name, grid=(M // bm, T // bk),
        in_specs=[pl.BlockSpec((bk, bm), lambda i, k: (k, i)), pl.BlockSpec((nd, bk, c), lambda i, k: (0, k, 0))],
        out_specs=pl.BlockSpec((nd, c, bm), lambda i, k: (0, 0, i)),
        out_shape=jax.ShapeDtypeStruct((nd, c, M), F32),
        compiler_params=_cparams(("parallel", "arbitrary"), VMEM_LIMIT),
    )(a, b)


def _pad_gain(g, n):
    return jnp.pad(g.reshape(1, -1), ((0, 0), (0, n - g.shape[-1])))


GROUP_A = ("w_ffn_gate", "w_ffn_up", "w_ffn_down", "w_ple_gate", "w_ple_proj")
GROUP_B = ("w_branch", "w_out")
GROUP_C = ("w_in", "w_uq", "w_ukv")
EARLY = GROUP_C
LATE = GROUP_B + GROUP_A
TRANSPOSED = ("w_in", "w_uq", "w_ffn_gate", "w_ffn_up")


def _local_step(x, p, pos, tgt, small, big, late_blocks=None, core=None):
    T = x.shape[0]
    tm = min(ROW_TILE, T)
    w_in = big["w_in"]
    w_uq = jnp.pad(big["w_uq"], ((0, 0), (0, HEAD_PAD - QK_DIM), (0, 0)))
    w_ukv = big["w_ukv"]

    g_mix, g_qa, g_kva = small["mix_norm_g"], small["q_a_norm_g"], small["kv_a_norm_g"]
    g_qn, g_kn = _pad_gain(small["q_norm_g"], HEAD_PAD), _pad_gain(small["k_norm_g"], HEAD_PAD)
    g_out, g_ffn = small["hg_out_norm_g"], small["ffn_norm_g"]
    g_pg, g_post = small["ple_gate_norm_g"], small["ple_post_norm_g"]
    logits = small["hg_lb_logits"]
    lb = _lower_bound(logits)

    h, cq, ckv, kr, hq, hf, hi, hg, bg = _in_proj_fwd(x, g_mix, w_in, T, tm)
    qf, kf, vf = _mla_prep_fwd(cq, ckv, kr, pos, g_qa, g_kva, g_qn, g_kn, w_uq, w_ukv, T, tm)
    if late_blocks is None:
        attn, lse = _flash_fwd(qf, kf, vf, T)
    else:
        attn, lse, *late = _flash_fwd(qf, kf, vf, T, ag_blocks=[late_blocks[n] for n in LATE])
        big = {**big, **dict(zip(LATE, late))}
    w_branch = jnp.moveaxis(big["w_branch"].reshape(N_DEV, 2, HG_W, HEAD_PAD), 0, 2).reshape(2, HG_W, D_MODEL)
    w_bra = jnp.pad(w_branch[0].reshape(MLA_HEADS, V_DIM, D_MODEL),
                    ((0, 0), (0, HEAD_PAD - V_DIM), (0, 0))).reshape(MLA_HEADS * HEAD_PAD, D_MODEL)
    w_brb = w_branch[1]
    w_out = big["w_out"].reshape(D_MODEL, D_MODEL)
    w_g, w_u = big["w_ffn_gate"].reshape(FFN, D_MODEL), big["w_ffn_up"].reshape(FFN, D_MODEL)
    w_d = big["w_ffn_down"].reshape(FFN, D_MODEL)
    w_pg, w_pp = big["w_ple_gate"].reshape(D_MODEL, D_MODEL), big["w_ple_proj"]
    o, s0 = _hgrn_fwd(hq, hf, hi, lb, T)
    x1, ya, yb, m, rec = _merge_fwd(attn, o, hg, bg, x, g_out, w_bra, w_brb, w_out, T, tm)
    x2, gt, up, h2 = _ffn_fwd(x1, g_ffn, w_g, w_u, w_d, T, tm)
    dx2, loss_p, dg_post, dg_pg, d_pg, d_pp = _ple_loss(x2, p, tgt, g_pg, g_post, w_pg, w_pp, T, tm)
    dg_post, dg_pg = (jnp.sum(t, axis=0, keepdims=True) for t in (dg_post, dg_pg))

    grads, sibs, gots = {}, {}, {}

    def reduce_start(tag, names):
        if core is None:
            return ()
        got = _exchange_sibling("rs_sibling_" + tag, [grads[n] for n in names])
        sibs.update(zip(names, got))
        return [_chip_partial("rs_partial_" + n, grads[n], sibs[n], core) for n in names]

    dx1, a, dgt, dup, dg_ffn = _ffn_bwd(dx2, x1, gt, up, g_ffn, w_g, w_u, w_d, T, tm)
    dg_ffn = jnp.sum(dg_ffn, axis=0, keepdims=True)
    grads["w_ffn_gate"] = _matmul_tn("dw_gate", dgt, h2).reshape(N_DEV, -1, D_MODEL)
    grads["w_ffn_up"] = _matmul_tn("dw_up", dup, h2).reshape(N_DEV, -1, D_MODEL)
    grads["w_ffn_down"] = _matmul_tn("dw_down", a, dx2).reshape(N_DEV, -1, D_MODEL)
    grads["w_ple_gate"] = d_pg.reshape(N_DEV, -1, D_MODEL)
    grads["w_ple_proj"] = d_pp
    parts_a = reduce_start("a", GROUP_A)

    dattn, do, dhg, dbg, dg_out, d_out, d_bra, d_brb = _merge_bwd(
        dx1, ya, yb, bg, o, hg, attn, m, rec, g_out, w_bra, w_brb, w_out, T, tm)
    dg_out = jnp.sum(dg_out, axis=0, keepdims=True)
    d_bra = d_bra.reshape(N_DEV, MLA_HEADS, HEAD_PAD, HEAD_PAD)[:, :, :V_DIM].reshape(N_DEV, HG_W, HEAD_PAD)
    grads["w_branch"] = jnp.concatenate([d_bra, d_brb], axis=1)
    grads["w_out"] = d_out.reshape(N_DEV, -1, D_MODEL)
    parts_b = reduce_start("b", GROUP_B)

    dhq, dhf, dhi, dlb, *got_a = _hgrn_bwd(hq, hf, hi, do, s0, lb, T, xchg=parts_a)
    dqf, dkf, dvf, *got_b = _flash_bwd(qf, kf, vf, attn, dattn, lse, T, xchg=parts_b)
    (dcq, dckv, dkr, dg_qa, dg_kva, dg_qn, dg_kn, d_uq, d_ukv) = _mla_prep_bwd(
        cq, ckv, kr, pos, dqf, dkf, dvf, g_qa, g_kva, g_qn, g_kn, w_uq, w_ukv, T, tm)
    grad_x, dproj, dg_mix = _in_proj_bwd(x, dx1, [dcq, dckv, dkr, dhq, dhf, dhi, dhg, dbg], g_mix, w_in, T, tm)
    dg_mix = jnp.sum(dg_mix, axis=0, keepdims=True)
    grads["w_in"] = _matmul_tn_blocks("dw_in", h, dproj)
    grads["w_uq"] = d_uq[:, :QK_DIM]
    grads["w_ukv"] = d_ukv
    parts_c = reduce_start("c", GROUP_C)
    if core is not None:
        gots.update(zip(GROUP_A, got_a))
        gots.update(zip(GROUP_B, got_b))
        gots.update(zip(GROUP_C, _exchange_chips(parts_c)))

    dl0 = dlb * lb * (1.0 - lb)
    small_g = {
        "mix_norm_g": dg_mix, "q_a_norm_g": dg_qa, "kv_a_norm_g": dg_kva,
        "q_norm_g": dg_qn[:, :QK_DIM], "k_norm_g": dg_kn[:, :QK_DIM],
        "hg_lb_logits": jnp.concatenate([dl0, -dl0], axis=0), "hg_out_norm_g": dg_out,
        "ffn_norm_g": dg_ffn, "ple_gate_norm_g": dg_pg, "ple_post_norm_g": dg_post,
    }
    return loss_p, grad_x, small_g, grads, sibs, gots


def _lower_bound(logits):
    def body(l_ref, lb_ref):
        l = l_ref[...]
        mx = jnp.max(l, axis=0, keepdims=True)
        e = jnp.exp(l - mx)
        lb_ref[...] = e[0:1] / jnp.sum(e, axis=0, keepdims=True)

    return pl.pallas_call(body, name="lower_bound", out_shape=jax.ShapeDtypeStruct((1, HG_W), F32))(logits)


def _my_place():
    return lax.axis_index("x"), lax.axis_index("y"), lax.axis_index("c")


def _all_gather(name, blocks):
    n = len(blocks)

    def body(*refs):
        x_refs, out_refs, sems = refs[:n], refs[n:2 * n], refs[2 * n:]
        _ag_start(x_refs, out_refs, sems)
        _ag_finish(x_refs, out_refs, sems)

    any_spec = pl.BlockSpec(memory_space=pl.ANY)
    return pl.pallas_call(
        body, name=name, out_shape=_ag_out_shapes(blocks),
        in_specs=[any_spec] * n, out_specs=[any_spec] * n, scratch_shapes=_ag_sems(n),
    )(*blocks)


def _ag_out_shapes(blocks):
    return [jax.ShapeDtypeStruct((N_DEV,) + b.shape, b.dtype) for b in blocks]


def _ag_sems(n):
    return [pltpu.SemaphoreType.DMA((7 * n,)), pltpu.SemaphoreType.DMA((7 * n,)), pltpu.SemaphoreType.DMA((n,))]


def _ag_parts(x_refs, out_refs, sems):
    send_sems, recv_sems, local_sems = sems
    x, y, c = _my_place()
    me, sibling = (x, y, c), (x, y, 1 - c)
    chips = [(1 - x, y), (x, 1 - y), (1 - x, 1 - y)]
    n = len(x_refs)

    def copy(a, k, block, to, own=False):
        px, py, pc = block
        dst = out_refs[a].at[4 * px + 2 * py + pc]
        return pltpu.make_async_remote_copy(
            src_ref=x_refs[a] if own else dst, dst_ref=dst, send_sem=send_sems.at[7 * a + k],
            recv_sem=recv_sems.at[7 * a + k], device_id=to, device_id_type=MESH_ID)

    mine = [pltpu.make_async_copy(x_refs[a], out_refs[a].at[4 * x + 2 * y + c], local_sems.at[a]) for a in range(n)]
    first = []
    for a in range(n):
        first.append(copy(a, 0, me, sibling, own=True))
        first += [copy(a, 1 + j, me, (*chip, c), own=True) for j, chip in enumerate(chips)]
    return copy, mine, first, me, sibling, chips, c, n


def _ag_start(x_refs, out_refs, sems):
    _, mine, first, *_ = _ag_parts(x_refs, out_refs, sems)
    for cp in mine + first:
        cp.start()


def _ag_finish(x_refs, out_refs, sems):
    copy, mine, first, me, sibling, chips, c, n = _ag_parts(x_refs, out_refs, sems)
    passed = []
    for j, chip in enumerate(chips):
        for a in range(n):
            copy(a, 1 + j, (*chip, c), me).wait_recv()
            passed.append(copy(a, 4 + j, (*chip, c), sibling))
            passed[-1].start()
    for a in range(n):
        copy(a, 0, sibling, me).wait_recv()
    for j, chip in enumerate(chips):
        for a in range(n):
            copy(a, 4 + j, (*chip, 1 - c), me).wait_recv()
    for cp in first + passed:
        cp.wait_send()
    for cp in mine:
        cp.wait()


def _exchange_sibling(name, gs):
    n = len(gs)

    def body(*refs):
        g_refs, out_refs, (send_sems, recv_sems) = refs[:n], refs[n:2 * n], refs[2 * n:]
        x, y, c = _my_place()
        copies = [pltpu.make_async_remote_copy(
            src_ref=g_refs[a].at[2 * j + 1 - c], dst_ref=out_refs[a].at[j], send_sem=send_sems.at[4 * a + j],
            recv_sem=recv_sems.at[4 * a + j], device_id=(x, y, 1 - c), device_id_type=MESH_ID)
            for a in range(n) for j in range(4)]
        for cp in copies:
            cp.start()
        for cp in copies:
            cp.wait()

    any_spec = pl.BlockSpec(memory_space=pl.ANY)
    return pl.pallas_call(
        body, name=name, out_shape=[jax.ShapeDtypeStruct((4,) + g.shape[1:], g.dtype) for g in gs],
        in_specs=[any_spec] * n, out_specs=[any_spec] * n,
        scratch_shapes=[pltpu.SemaphoreType.DMA((4 * n,)), pltpu.SemaphoreType.DMA((4 * n,))],
    )(*gs)


def _chip_partial(name, g, got, c_idx):
    _, rows, cols = g.shape
    tr, tc = _tile_2d(rows, cols, 512)

    def body(c_ref, g_ref, got_ref, out_ref):
        out_ref[...] = (g_ref[...] + got_ref[...]).astype(MM)

    grid_spec = pltpu.PrefetchScalarGridSpec(
        num_scalar_prefetch=1, grid=(4, rows // tr, cols // tc),
        in_specs=[pl.BlockSpec((1, tr, tc), lambda j, i, k, c_ref: (2 * j + c_ref[0], i, k)),
                  pl.BlockSpec((1, tr, tc), lambda j, i, k, c_ref: (j, i, k))],
        out_specs=pl.BlockSpec((1, tr, tc), lambda j, i, k, c_ref: (j, i, k)))
    return pl.pallas_call(
        body, name=name, grid_spec=grid_spec, out_shape=jax.ShapeDtypeStruct((4, rows, cols), MM),
        compiler_params=_cparams(("parallel", "parallel", "parallel")),
    )(c_idx, g, got)


def _tile_2d(rows, cols, row_cap):
    if rows % 8 == 0:
        return _pick_rows(rows, row_cap), cols
    return rows, 256 if cols % 256 == 0 else cols


def _exchange_chips(parts):
    n = len(parts)

    def body(*refs):
        p_refs, out_refs, sems = refs[:n], refs[n:2 * n], refs[2 * n:]
        for cp in _xchips_copies(p_refs, out_refs, sems):
            cp.start()
        for cp in _xchips_copies(p_refs, out_refs, sems):
            cp.wait()

    any_spec = pl.BlockSpec(memory_space=pl.ANY)
    return pl.pallas_call(
        body, name="rs_chips", out_shape=_xchips_out_shapes(parts),
        in_specs=[any_spec] * n, out_specs=[any_spec] * n, scratch_shapes=_xchips_sems(n),
    )(*parts)


def _xchips_out_shapes(parts):
    return [jax.ShapeDtypeStruct((3,) + p.shape[1:], p.dtype) for p in parts]


def _xchips_sems(n):
    return [pltpu.SemaphoreType.DMA((3 * n,)), pltpu.SemaphoreType.DMA((3 * n,))]


def _xchips_copies(p_refs, out_refs, sems):
    send_sems, recv_sems = sems
    x, y, c = _my_place()
    chips = [(1 - x, y), (x, 1 - y), (1 - x, 1 - y)]
    return [pltpu.make_async_remote_copy(
        src_ref=p_refs[a].at[2 * px + py], dst_ref=out_refs[a].at[k], send_sem=send_sems.at[3 * a + k],
        recv_sem=recv_sems.at[3 * a + k], device_id=(px, py, c), device_id_type=MESH_ID)
        for a in range(len(p_refs)) for k, (px, py) in enumerate(chips)]


def _adamw_math(w, g, m, v):
    m = ADAM_B1 * m + (1.0 - ADAM_B1) * g
    v = ADAM_B2 * v + (1.0 - ADAM_B2) * jnp.square(g)
    m_hat = m / (1.0 - ADAM_B1 ** ADAM_STEP)
    v_hat = v / (1.0 - ADAM_B2 ** ADAM_STEP)
    delta = -ADAM_LR * (m_hat / (jnp.sqrt(v_hat) + ADAM_EPS) + ADAM_WD * w)
    return delta, m, v


def _sum_adamw(name, g, sib, got, w, m, v, slot_idx, chip_idx):
    _, rows, cols = g.shape
    tr, tc = _tile_2d(rows, cols, 256)

    def body(s_ref, j_ref, g_ref, sib_ref, got_ref, w_ref, m_ref, v_ref, go_ref, d_ref, m2_ref, v2_ref):
        grad = g_ref[0] + sib_ref[0]
        for k in range(3):
            grad = grad + got_ref[k].astype(F32)
        go_ref[...] = grad
        d_ref[...], m2_ref[...], v2_ref[...] = _adamw_math(w_ref[...], grad, m_ref[...], v_ref[...])

    flat = pl.BlockSpec((tr, tc), lambda i, k, s_ref, j_ref: (i, k))
    grid_spec = pltpu.PrefetchScalarGridSpec(
        num_scalar_prefetch=2, grid=(rows // tr, cols // tc),
        in_specs=[pl.BlockSpec((1, tr, tc), lambda i, k, s_ref, j_ref: (s_ref[0], i, k)),
                  pl.BlockSpec((1, tr, tc), lambda i, k, s_ref, j_ref: (j_ref[0], i, k)),
                  pl.BlockSpec((3, tr, tc), lambda i, k, s_ref, j_ref: (0, i, k)), flat, flat, flat],
        out_specs=[flat] * 4)
    return pl.pallas_call(
        body, name=name, grid_spec=grid_spec, out_shape=[jax.ShapeDtypeStruct((rows, cols), F32)] * 4,
        compiler_params=_cparams(("parallel", "parallel")),
    )(slot_idx, chip_idx, g, sib, got, w, m, v)


def _adamw_small(parts, w, m, v):
    rows = w.shape[0]

    def body(p_ref, w_ref, m_ref, v_ref, g_ref, d_ref, m2_ref, v2_ref):
        g = p_ref[0]
        for d in range(1, N_DEV):
            g = g + p_ref[d]
        g_ref[...] = g
        d_ref[...], m2_ref[...], v2_ref[...] = _adamw_math(w_ref[...], g, m_ref[...], v_ref[...])

    return pl.pallas_call(
        body, name="adamw_small", out_shape=[jax.ShapeDtypeStruct((rows, 128), F32)] * 4,
    )(parts, w, m, v)


BIG = ("w_in", "w_uq", "w_ukv", "w_branch", "w_out", "w_ffn_gate", "w_ffn_up", "w_ffn_down", "w_ple_gate", "w_ple_proj")
SMALL = (
    ("mix_norm_g", 1024), ("q_a_norm_g", 384), ("kv_a_norm_g", 256), ("q_norm_g", 96), ("k_norm_g", 96),
    ("hg_lb_logits", 1024), ("hg_out_norm_g", 128), ("ffn_norm_g", 1024), ("ple_gate_norm_g", 1024),
    ("ple_post_norm_g", 1024),
)
SMALL_ROWS = 56


def _pack_small(vals):
    rows = []
    for name, n in SMALL:
        v = vals[name].reshape(1, -1).astype(F32)
        rows.append(jnp.pad(v, ((0, 0), (0, (-n) % 128))).reshape(-1, 128))
    return jnp.concatenate(rows, axis=0)


def _unpack_small(packed, shapes):
    out, r = {}, 0
    for name, n in SMALL:
        k = (n + 127) // 128
        out[name] = packed[r:r + k].reshape(1, -1)[:, :n].reshape(shapes[name])
        r += k
    return out


_WEIGHTS = ["mix_norm_g", "w_in", "q_a_norm_g", "w_uq", "kv_a_norm_g", "w_ukv", "q_norm_g", "k_norm_g", "hg_lb_logits",
            "hg_out_norm_g", "w_branch", "w_out", "ffn_norm_g", "w_ffn_gate", "w_ffn_up", "w_ffn_down",
            "ple_gate_norm_g", "w_ple_gate", "w_ple_proj", "ple_post_norm_g"]


def _step(x, p, positions, tgt, w, m, v):
    small_names = [n for n, _ in SMALL]
    T = x.shape[1]
    px, py, pc = _my_place()
    as_idx = lambda t: jnp.reshape(t, (1,)).astype(jnp.int32)

    def two_d(n, t):
        t = t.reshape(-1, t.shape[-1])
        return t.T if n in TRANSPOSED else t

    def full_shape(n, t):
        return (t.T if n in TRANSPOSED else t).reshape(w[n].shape)

    blocks = {n: two_d(n, w[n]).astype(MM) for n in BIG}
    big = dict(zip(EARLY, _all_gather("ag_weights", [blocks[n] for n in EARLY])))
    small = {n: (w[n] if n == "hg_lb_logits" else w[n].reshape(1, -1)) for n in small_names}

    loss_p, grad_x, small_g, grads, sibs, gots = _local_step(
        x[0], p[0, 0], positions.reshape(T, 1), tgt[0], small, big, late_blocks=blocks, core=as_idx(pc))

    out_g, out_d, out_m, out_v = {}, {}, {}, {}
    for n in BIG:
        res = _sum_adamw("adamw_" + n, grads[n], sibs[n], gots[n], two_d(n, w[n]), two_d(n, m[n]), two_d(n, v[n]),
                         as_idx(4 * px + 2 * py + pc), as_idx(2 * px + py))
        out_g[n], out_d[n], out_m[n], out_v[n] = [full_shape(n, r) for r in res]

    packed_g = _pack_small(small_g)
    loss_row = jnp.concatenate([jnp.pad(jnp.sum(loss_p).reshape(1, 1), ((0, 0), (0, 127))),
                                jnp.zeros((SMALL_ROWS - packed_g.shape[0] - 1, 128), F32)], axis=0)
    parts = _all_gather("ag_small", [jnp.concatenate([packed_g, loss_row], axis=0)])[0]
    pad_rows = lambda t: jnp.pad(t, ((0, SMALL_ROWS - t.shape[0]), (0, 0)))
    sw = pad_rows(_pack_small({n: w[n] for n in small_names}))
    sm = pad_rows(_pack_small({n: m[n] for n in small_names}))
    sv = pad_rows(_pack_small({n: v[n] for n in small_names}))
    g_s, d_s, m_s, v_s = _adamw_small(parts, sw, sm, sv)
    shapes = {n: w[n].shape for n in small_names}
    n_packed = packed_g.shape[0]
    loss = g_s[n_packed, 0]
    for src, dst in ((g_s, out_g), (d_s, out_d), (m_s, out_m), (v_s, out_v)):
        dst.update(_unpack_small(src, shapes))

    outs = [loss, grad_x[None]]
    for table in (out_g, out_d, out_m, out_v):
        outs += [table[n] for n in _WEIGHTS]
    return tuple(outs)


def kernel(x, p, positions, mix_norm_g, w_in, q_a_norm_g, w_uq, kv_a_norm_g, w_ukv, q_norm_g, k_norm_g, hg_lb_logits, hg_out_norm_g, w_branch, w_out, ffn_norm_g, w_ffn_gate, w_ffn_up, w_ffn_down, ple_gate_norm_g, w_ple_gate, w_ple_proj, ple_post_norm_g, loss_target, m_mix_norm_g, m_w_in, m_q_a_norm_g, m_w_uq, m_kv_a_norm_g, m_w_ukv, m_q_norm_g, m_k_norm_g, m_hg_lb_logits, m_hg_out_norm_g, m_w_branch, m_w_out, m_ffn_norm_g, m_w_ffn_gate, m_w_ffn_up, m_w_ffn_down, m_ple_gate_norm_g, m_w_ple_gate, m_w_ple_proj, m_ple_post_norm_g, v_mix_norm_g, v_w_in, v_q_a_norm_g, v_w_uq, v_kv_a_norm_g, v_w_ukv, v_q_norm_g, v_k_norm_g, v_hg_lb_logits, v_hg_out_norm_g, v_w_branch, v_w_out, v_ffn_norm_g, v_w_ffn_gate, v_w_ffn_up, v_w_ffn_down, v_ple_gate_norm_g, v_w_ple_gate, v_w_ple_proj, v_ple_post_norm_g):
    w = dict(mix_norm_g=mix_norm_g, w_in=w_in, q_a_norm_g=q_a_norm_g, w_uq=w_uq, kv_a_norm_g=kv_a_norm_g, w_ukv=w_ukv,
             q_norm_g=q_norm_g, k_norm_g=k_norm_g, hg_lb_logits=hg_lb_logits, hg_out_norm_g=hg_out_norm_g,
             w_branch=w_branch, w_out=w_out, ffn_norm_g=ffn_norm_g, w_ffn_gate=w_ffn_gate, w_ffn_up=w_ffn_up,
             w_ffn_down=w_ffn_down, ple_gate_norm_g=ple_gate_norm_g, w_ple_gate=w_ple_gate, w_ple_proj=w_ple_proj,
             ple_post_norm_g=ple_post_norm_g)
    m = dict(mix_norm_g=m_mix_norm_g, w_in=m_w_in, q_a_norm_g=m_q_a_norm_g, w_uq=m_w_uq, kv_a_norm_g=m_kv_a_norm_g,
             w_ukv=m_w_ukv, q_norm_g=m_q_norm_g, k_norm_g=m_k_norm_g, hg_lb_logits=m_hg_lb_logits,
             hg_out_norm_g=m_hg_out_norm_g, w_branch=m_w_branch, w_out=m_w_out, ffn_norm_g=m_ffn_norm_g,
             w_ffn_gate=m_w_ffn_gate, w_ffn_up=m_w_ffn_up, w_ffn_down=m_w_ffn_down,
             ple_gate_norm_g=m_ple_gate_norm_g, w_ple_gate=m_w_ple_gate, w_ple_proj=m_w_ple_proj,
             ple_post_norm_g=m_ple_post_norm_g)
    v = dict(mix_norm_g=v_mix_norm_g, w_in=v_w_in, q_a_norm_g=v_q_a_norm_g, w_uq=v_w_uq, kv_a_norm_g=v_kv_a_norm_g,
             w_ukv=v_w_ukv, q_norm_g=v_q_norm_g, k_norm_g=v_k_norm_g, hg_lb_logits=v_hg_lb_logits,
             hg_out_norm_g=v_hg_out_norm_g, w_branch=v_w_branch, w_out=v_w_out, ffn_norm_g=v_ffn_norm_g,
             w_ffn_gate=v_w_ffn_gate, w_ffn_up=v_w_ffn_up, w_ffn_down=v_w_ffn_down,
             ple_gate_norm_g=v_ple_gate_norm_g, w_ple_gate=v_w_ple_gate, w_ple_proj=v_w_ple_proj,
             ple_post_norm_g=v_ple_post_norm_g)
    return _step(x, p, positions, loss_target, w, m, v)
```

```python
import functools

import jax
import jax.numpy as jnp
import numpy as np
from jax import lax
from jax.experimental import pallas as pl
from jax.experimental.pallas import tpu as pltpu

F32 = jnp.float32
MM = jnp.bfloat16
HI = lax.Precision.HIGHEST
MESH_ID = pl.DeviceIdType.MESH

D_MODEL = 1024
N_DEV = 8
MLA_HEADS = 8
QK_NOPE = 64
QK_ROPE = 32
QK_DIM = 96
V_DIM = 64
HEAD_PAD = 128
Q_RANK = 384
KV_RANK = 256
ROPE_BASE = 10000.0
HG_HEADS = 4
HG_DIM = 128
HG_W = 512
HG_CHUNK = 64
FFN = 2816
PLE = 256
EPS = 1e-6
ATT_SCALE = QK_DIM ** -0.5
NEG = -1e30

ADAM_LR = 0.001
ADAM_B1 = 0.9
ADAM_B2 = 0.999
ADAM_EPS = 1e-08
ADAM_WD = 0.01
ADAM_STEP = 10

SEC_CQ = (0, 384)
SEC_CKV = (384, 256)
SEC_KR = (640, 128)
SEC_HQ = (768, 512)
SEC_HF = (1280, 512)
SEC_HI = (1792, 512)
SEC_HG = (2304, 512)
SEC_BG = (2816, 2048)
IN_PAD = 4864
SECTIONS = (SEC_CQ, SEC_CKV, SEC_KR, SEC_HQ, SEC_HF, SEC_HI, SEC_HG, SEC_BG)
COL_SECTIONS = ((0, 384), (384, 256), (640, 32), (672, 512), (1184, 512), (1696, 512), (2208, 512), (2720, 2048))
IN_COLS = 4768
IN_BLOCK = IN_COLS // 8

VMEM_LIMIT = 58 * 1024 * 1024
ROW_TILE = 256
ATT_TILE = 1024
ATT_HEADS = 2
HG_BLOCK = 512
HG_UNROLL = 4


def _dot(a, b):
    return jnp.dot(a.astype(MM), b.astype(MM), preferred_element_type=F32)


def _dot_nt(a, b):
    return lax.dot_general(a.astype(MM), b.astype(MM), (((1,), (1,)), ((), ())), preferred_element_type=F32)


def _dot_tn(a, b):
    return lax.dot_general(a.astype(MM), b.astype(MM), (((0,), (0,)), ((), ())), preferred_element_type=F32)


def _dot_hi(a, b):
    return jnp.dot(a, b, preferred_element_type=F32, precision=HI)


def _sigmoid(x):
    return 1.0 / (1.0 + jnp.exp(-x))


def _rms(x, n=None):
    n = x.shape[-1] if n is None else n
    r = lax.rsqrt(jnp.sum(x * x, axis=-1, keepdims=True) * (1.0 / n) + EPS)
    return x * r, r


def _rms_bwd(dxh, xh, r, n=None):
    n = xh.shape[-1] if n is None else n
    return r * (dxh - xh * (jnp.sum(dxh * xh, axis=-1, keepdims=True) * (1.0 / n)))


def _rope_tables(pos, tm):
    lane = lax.broadcasted_iota(jnp.int32, (tm, HEAD_PAD), 1)
    idx = jnp.where(lane < QK_NOPE + QK_ROPE // 2, lane - QK_NOPE, lane - QK_NOPE - QK_ROPE // 2)
    inv = jnp.exp(idx.astype(F32) * (-np.log(ROPE_BASE) * 2.0 / QK_ROPE))
    ang = pos.astype(F32) * inv
    in_rope = (lane >= QK_NOPE) & (lane < QK_DIM)
    first = lane < QK_NOPE + QK_ROPE // 2
    cos_t = jnp.where(in_rope, jnp.cos(ang), 1.0)
    sin_t = jnp.where(in_rope, jnp.where(first, -jnp.sin(ang), jnp.sin(ang)), 0.0)
    return cos_t, sin_t, (first, in_rope)


def _rope_swap(x, halves):
    first, in_rope = halves
    half = QK_ROPE // 2
    return jnp.where(in_rope, jnp.where(first, pltpu.roll(x, HEAD_PAD - half, 1), pltpu.roll(x, half, 1)), 0.0)


def _cparams(sem, vmem=None):
    return pltpu.CompilerParams(dimension_semantics=sem, vmem_limit_bytes=vmem)


def _row_call(name, body, T, tm, row_ins, full_ins, row_outs, acc_outs, vmem=None, scratch=()):
    def kern(*refs):
        body(pl.program_id(0), *refs)

    in_specs = [pl.BlockSpec((tm, a.shape[1]), lambda i: (i, 0)) for a in row_ins]
    in_specs += [pl.BlockSpec(a.shape, lambda i, nd=a.ndim: (0,) * nd, pipeline_mode=pl.Buffered(1)) for a in full_ins]
    out_specs = [pl.BlockSpec((tm, n), lambda i: (i, 0)) for n, _ in row_outs]
    out_specs += [pl.BlockSpec(s, lambda i, nd=len(s): (0,) * nd) for s, _ in acc_outs]
    out_shape = [jax.ShapeDtypeStruct((T, n), dt) for n, dt in row_outs]
    out_shape += [jax.ShapeDtypeStruct(s, dt) for s, dt in acc_outs]
    return pl.pallas_call(
        kern, name=name, grid=(T // tm,), in_specs=in_specs, out_specs=out_specs, out_shape=out_shape,
        scratch_shapes=list(scratch), compiler_params=_cparams(("arbitrary",), vmem),
    )(*row_ins, *full_ins)


FFN_HALVES = (slice(0, FFN // 2), slice(FFN // 2, FFN))
ROW_CHUNK = 16
CHUNK_UNROLL = True


def _by_chunks(tm, fn):
    def step(c, carry):
        fn(pl.ds(pl.multiple_of(c * ROW_CHUNK, ROW_CHUNK), ROW_CHUNK))
        return carry

    lax.fori_loop(0, tm // ROW_CHUNK, step, 0, unroll=CHUNK_UNROLL)


def _fold8(x):
    return x[:8] + x[8:]


def _acc(ref, i, val):
    @pl.when(i == 0)
    def _():
        ref[...] = val

    @pl.when(i != 0)
    def _():
        ref[...] += val


def _in_proj_fwd(x, g_mix, w_in, T, tm):
    def body(i, x_ref, g_ref, w_ref, h_ref, *rest):
        outs, pj_s = rest[:-1], rest[-1]
        g = g_ref[...]

        def norm(rows):
            h_ref[rows, :] = (_rms(x_ref[rows, :])[0] * g).astype(MM)

        _by_chunks(tm, norm)
        for d in range(N_DEV):
            pj_s[d] = _dot_nt(h_ref[...], w_ref[d])

        def join_and_cut(rows):
            proj = jnp.concatenate([pj_s[d, rows, :] for d in range(N_DEV)], axis=1)
            for (s, n), o_ref in zip(COL_SECTIONS, outs):
                if n == QK_ROPE:
                    o_ref[rows, :] = jnp.concatenate(
                        [jnp.zeros((ROW_CHUNK, QK_NOPE), F32), proj[:, s:s + n],
                         jnp.zeros((ROW_CHUNK, HEAD_PAD - QK_DIM), F32)], axis=1)
                else:
                    o_ref[rows, :] = proj[:, s:s + n]

        _by_chunks(tm, join_and_cut)

    row_outs = [(D_MODEL, MM)] + [(n, F32) for _, n in SECTIONS]
    return _row_call("in_proj_fwd", body, T, tm, [x], [g_mix, w_in], row_outs, [], VMEM_LIMIT,
                     scratch=[pltpu.VMEM((N_DEV, tm, IN_BLOCK), F32)])


def _mla_heads_fwd(raw, g_pad, cos_t, sin_t, first):
    outs, saved = [], []
    for h in range(MLA_HEADS):
        xh, r = _rms(raw[:, h * HEAD_PAD:(h + 1) * HEAD_PAD], QK_DIM)
        y = xh * g_pad
        outs.append(y * cos_t + _rope_swap(y, first) * sin_t)
        saved.append((xh, r))
    return outs, saved


def _mla_raw_heads(cqn, ckvn, kr, wuq_ref, wukv_ref, tm):
    lane = lax.broadcasted_iota(jnp.int32, (tm, HEAD_PAD), 1)
    nope = lane < QK_NOPE
    one_lane = jnp.where(lane == V_DIM, 1.0, 0.0)
    qs, ks, vs = [], [], []
    for h in range(MLA_HEADS):
        qs.append(_dot_nt(cqn, wuq_ref[h]))
        kv = _dot(ckvn, wukv_ref[h])
        ks.append(jnp.where(nope, kv, kr))
        vs.append(jnp.where(nope, pltpu.roll(kv, V_DIM, 1), one_lane))
    return jnp.concatenate(qs, axis=1), jnp.concatenate(ks, axis=1), jnp.concatenate(vs, axis=1)


def _mla_prep_fwd(cq, ckv, kr, pos, g_qa, g_kva, g_qn, g_kn, w_uq, w_ukv, T, tm):
    def body(i, cq_ref, ckv_ref, kr_ref, pos_ref, gqa_ref, gkva_ref, gqn_ref, gkn_ref, wuq_ref, wukv_ref,
             q_ref, k_ref, v_ref):
        cos_t, sin_t, first = _rope_tables(pos_ref[...], tm)
        cqn = _rms(cq_ref[...])[0] * gqa_ref[...]
        ckvn = _rms(ckv_ref[...])[0] * gkva_ref[...]
        q_raw, k_raw, v = _mla_raw_heads(cqn, ckvn, kr_ref[...], wuq_ref, wukv_ref, tm)
        qs, _ = _mla_heads_fwd(q_raw, gqn_ref[...], cos_t, sin_t, first)
        ks, _ = _mla_heads_fwd(k_raw, gkn_ref[...], cos_t, sin_t, first)
        q_ref[...] = (jnp.concatenate(qs, axis=1) * ATT_SCALE).astype(MM)
        k_ref[...] = jnp.concatenate(ks, axis=1).astype(MM)
        v_ref[...] = v.astype(MM)

    w = MLA_HEADS * HEAD_PAD
    return _row_call("mla_prep_fwd", body, T, tm, [cq, ckv, kr, pos], [g_qa, g_kva, g_qn, g_kn, w_uq, w_ukv],
                     [(w, MM), (w, MM), (w, MM)], [])


def _causal_pairs(n, by_query):
    if by_query:
        pairs = [(q, k) for q in range(n) for k in range(q + 1)]
    else:
        pairs = [(q, k) for k in range(n) for q in range(k, n)]
    return np.array([p[0] for p in pairs], np.int32), np.array([p[1] for p in pairs], np.int32)


def _flash_fwd(qf, kf, vf, T, ag_blocks=()):
    tq = min(ATT_TILE, T)
    nq = T // tq

    qi_tab, ki_tab = _causal_pairs(nq, by_query=True)

    hp = ATT_HEADS

    n_ag = len(ag_blocks)
    n_heads, n_pairs = MLA_HEADS // hp, len(qi_tab)

    def body(qi_ref, ki_ref, q_ref, k_ref, v_ref, *rest):
        ag_in, (o_ref, lse_ref), rest = rest[:n_ag], rest[n_ag:n_ag + 2], rest[n_ag + 2:]
        ag_out, (m_s, acc_s), ag_sems = rest[:n_ag], rest[n_ag:n_ag + 2], rest[n_ag + 2:]
        t = pl.program_id(1)
        qi, ki = qi_ref[t], ki_ref[t]
        if n_ag:
            @pl.when((pl.program_id(0) == 0) & (t == 0))
            def _():
                _ag_start(ag_in, ag_out, ag_sems)

        @pl.when(ki == 0)
        def _():
            m_s[...] = jnp.full_like(m_s, NEG)
            acc_s[...] = jnp.zeros_like(acc_s)

        def step(masked):
            for hh in range(hp):
                hs = slice(hh * HEAD_PAD, (hh + 1) * HEAD_PAD)
                s_t = _dot_nt(k_ref[:, hs], q_ref[:, hs])
                if masked:
                    key = lax.broadcasted_iota(jnp.int32, (tq, tq), 0)
                    qry = lax.broadcasted_iota(jnp.int32, (tq, tq), 1)
                    s_t = jnp.where(key <= qry, s_t, NEG)
                m_old = m_s[hh]
                m_new = jnp.maximum(m_old, jnp.max(s_t, axis=0, keepdims=True))
                p_t = jnp.exp(s_t - m_new)
                acc_s[hh] = jnp.exp(m_old - m_new) * acc_s[hh] + _dot_tn(v_ref[:, hs], p_t)
                m_s[hh] = m_new

        @pl.when(ki < qi)
        def _():
            step(False)

        @pl.when(ki == qi)
        def _():
            step(True)
            real = lax.broadcasted_iota(jnp.int32, (HEAD_PAD, tq), 0) < V_DIM
            for hh in range(hp):
                hs = slice(hh * HEAD_PAD, (hh + 1) * HEAD_PAD)
                acc = acc_s[hh]
                l = acc[V_DIM:V_DIM + 1]
                o_ref[:, hs] = jnp.where(real, acc / l, 0.0).T
                lse_ref[:, hs] = jnp.broadcast_to(m_s[hh] + jnp.log(l), (HEAD_PAD, tq)).T

        if n_ag:
            @pl.when((pl.program_id(0) == n_heads - 1) & (t == n_pairs - 1))
            def _():
                _ag_finish(ag_in, ag_out, ag_sems)

    q_spec = pl.BlockSpec((tq, hp * HEAD_PAD), lambda h, t, qi_ref, ki_ref: (qi_ref[t], h))
    kv_spec = pl.BlockSpec((tq, hp * HEAD_PAD), lambda h, t, qi_ref, ki_ref: (ki_ref[t], h))
    any_spec = pl.BlockSpec(memory_space=pl.ANY)
    grid_spec = pltpu.PrefetchScalarGridSpec(
        num_scalar_prefetch=2, grid=(n_heads, n_pairs),
        in_specs=[q_spec, kv_spec, kv_spec] + [any_spec] * n_ag, out_specs=[q_spec, q_spec] + [any_spec] * n_ag,
        scratch_shapes=[pltpu.VMEM((hp, 1, tq), F32), pltpu.VMEM((hp, HEAD_PAD, tq), F32)]
        + (_ag_sems(n_ag) if n_ag else []))
    return pl.pallas_call(
        body, name="flash_fwd", grid_spec=grid_spec,
        out_shape=[jax.ShapeDtypeStruct((T, MLA_HEADS * HEAD_PAD), F32)] * 2 + _ag_out_shapes(ag_blocks),
        compiler_params=_cparams(("arbitrary", "arbitrary")),
    )(jnp.asarray(qi_tab), jnp.asarray(ki_tab), qf, kf, vf, *ag_blocks)


def _hg_gates(hf, lb):
    sg = _sigmoid(hf)
    f = lb + (1.0 - lb) * sg
    return sg, f, jnp.log(f), 1.0 - f


def _tri(n, lower):
    r = lax.broadcasted_iota(jnp.int32, (n, n), 0)
    c = lax.broadcasted_iota(jnp.int32, (n, n), 1)
    return jnp.where((c <= r) if lower else (c >= r), 1.0, 0.0).astype(F32)


def _hg_levels():
    C = HG_CHUNK
    t = lax.broadcasted_iota(jnp.int32, (C, C), 0)
    s = lax.broadcasted_iota(jnp.int32, (C, C), 1)
    levels = []
    for shift in range(C.bit_length() - 2, -1, -1):
        pair_t, pair_s = lax.shift_right_logical(t, shift + 1), lax.shift_right_logical(s, shift + 1)
        later_t = (lax.shift_right_logical(t, shift) & 1) == 1
        earlier_s = (lax.shift_right_logical(s, shift) & 1) == 0
        levels.append((1 << shift, (pair_t == pair_s) & later_t & earlier_s))
    return levels, t == s


def _hg_refs(b):
    C, n = b.shape
    row = lax.broadcasted_iota(jnp.int32, (C, n), 0)
    back1, back2, ahead1 = pltpu.roll(b, 1, 0), pltpu.roll(b, 2, 0), pltpu.roll(b, C - 1, 0)
    refs = []
    for half in (32, 16, 8, 4):
        refs.append(jnp.concatenate(
            [jnp.broadcast_to(b[lo + half - 1:lo + half], (2 * half, n)) for lo in range(0, C, 2 * half)], axis=0))
    in4 = row & 3
    refs.append(jnp.where(in4 == 0, ahead1, jnp.where(in4 == 1, b, jnp.where(in4 == 2, back1, back2))))
    refs.append(jnp.where((row & 1) == 1, back1, b))
    return refs


def _hg_intra(q, k, b, refs, levels, eye):
    a = jnp.where(eye, jnp.sum(q * k, axis=1, keepdims=True), 0.0)
    saved = []
    for r, (_, mask) in zip(refs, levels):
        e = jnp.exp(-jnp.abs(b - r))
        q_t, k_t = q * e, k * e
        a = a + jnp.where(mask, _dot_nt(q_t, k_t), 0.0)
        saved.append((q_t, k_t, e))
    return a, saved


def _hg_intra_bwd(d_a, q, k, saved, levels, eye):
    diag = jnp.sum(jnp.where(eye, d_a, 0.0), axis=1, keepdims=True)
    dq, dk = diag * k, diag * q
    for (q_t, k_t, e), (_, mask) in zip(saved, levels):
        da = jnp.where(mask, d_a, 0.0)
        dq = dq + _dot(da, k_t) * e
        dk = dk + _dot_tn(da, q_t) * e
    return dq, dk


def _hgrn_fwd(hq, hf, hi, lb, T):
    rb = min(HG_BLOCK, T)
    ncb = rb // HG_CHUNK

    def body(hq_ref, hf_ref, hi_ref, lb_ref, o_ref, s0_ref, st_ref):
        @pl.when(pl.program_id(0) == 0)
        def _():
            st_ref[...] = jnp.zeros_like(st_ref)

        tril = _tri(HG_CHUNK, True)
        levels, eye = _hg_levels()

        def chunk(c, carry):
            rows = pl.ds(pl.multiple_of(c * HG_CHUNK, HG_CHUNK), HG_CHUNK)
            _, _, logf, kk = _hg_gates(hf_ref[rows, :], lb_ref[...])
            b = _dot_hi(tril, logf)
            refs = _hg_refs(b)
            q_all, v_all = hq_ref[rows, :], hi_ref[rows, :]
            outs = []
            for h in range(HG_HEADS):
                ls = slice(h * HG_DIM, (h + 1) * HG_DIM)
                q, k, v, bh = q_all[:, ls], kk[:, ls], v_all[:, ls], b[:, ls]
                st = st_ref[h]
                s0_ref[c, h * HG_DIM:(h + 1) * HG_DIM, :] = st
                b_end = bh[HG_CHUNK - 1:HG_CHUNK]
                a, _ = _hg_intra(q, k, bh, [r[:, ls] for r in refs], levels, eye)
                outs.append(_dot_nt(q * jnp.exp(bh), st) + _dot(a, v))
                st_ref[h] = st * jnp.exp(b_end) + _dot_tn(v, k * jnp.exp(b_end - bh))
            o_ref[rows, :] = jnp.concatenate(outs, axis=1)
            return carry

        lax.fori_loop(0, ncb, chunk, 0, unroll=HG_UNROLL)

    row = pl.BlockSpec((rb, HG_W), lambda i: (i, 0))
    return pl.pallas_call(
        body, name="hgrn_fwd", grid=(T // rb,),
        in_specs=[row, row, row, pl.BlockSpec((1, HG_W), lambda i: (0, 0))],
        out_specs=[row, pl.BlockSpec((ncb, HG_W, HG_DIM), lambda i: (i, 0, 0))],
        out_shape=[jax.ShapeDtypeStruct((T, HG_W), F32), jax.ShapeDtypeStruct((T // HG_CHUNK, HG_W, HG_DIM), F32)],
        scratch_shapes=[pltpu.VMEM((HG_HEADS, HG_DIM, HG_DIM), F32)],
        compiler_params=_cparams(("arbitrary",)),
    )(hq, hf, hi, lb)


def _hgrn_bwd(hq, hf, hi, do, s0, lb, T, xchg=()):
    rb = min(HG_BLOCK, T)
    ncb = rb // HG_CHUNK
    nb = T // rb
    C = HG_CHUNK
    n_x = len(xchg)

    def body(hq_ref, hf_ref, hi_ref, do_ref, s0_ref, lb_ref, *rest):
        x_in, (dq_ref, df_ref, dv_ref, dlb_ref), rest = rest[:n_x], rest[n_x:n_x + 4], rest[n_x + 4:]
        x_out, dst_ref, x_sems = rest[:n_x], rest[n_x], rest[n_x + 1:]

        @pl.when(pl.program_id(0) == 0)
        def _():
            dst_ref[...] = jnp.zeros_like(dst_ref)
            dlb_ref[...] = jnp.zeros_like(dlb_ref)
            for cp in _xchips_copies(x_in, x_out, x_sems) if n_x else ():
                cp.start()

        tril, triu = _tri(C, True), _tri(C, False)
        row_cc = lax.broadcasted_iota(jnp.int32, (C, C), 0)
        col_cc = lax.broadcasted_iota(jnp.int32, (C, C), 1)
        last_row = lax.broadcasted_iota(jnp.int32, (C, HG_DIM), 0) == C - 1
        lb_v = lb_ref[...]
        levels, eye = _hg_levels()

        def chunk(cc, carry):
            c = ncb - 1 - cc
            rows = pl.ds(pl.multiple_of(c * C, C), C)
            hf_c = hf_ref[rows, :]
            sg, f, logf, kk = _hg_gates(hf_c, lb_v)
            b = _dot_hi(tril, logf)
            refs = _hg_refs(b)
            q_all, v_all, do_all = hq_ref[rows, :], hi_ref[rows, :], do_ref[rows, :]
            dq_o, dk_o, dv_o, db_o = [], [], [], []
            for h in range(HG_HEADS):
                ls = slice(h * HG_DIM, (h + 1) * HG_DIM)
                q, k, v, bh, d_o = q_all[:, ls], kk[:, ls], v_all[:, ls], b[:, ls], do_all[:, ls]
                st0 = s0_ref[c, h * HG_DIM:(h + 1) * HG_DIM, :]
                dst = dst_ref[h]
                b_end = bh[C - 1:C]
                e_b, e_end = jnp.exp(bh), jnp.exp(b_end)
                e_rem = jnp.exp(b_end - bh)
                qe, kd = q * e_b, k * e_rem
                st_end = st0 * e_end + _dot_tn(v, kd)
                a, saved = _hg_intra(q, k, bh, [r[:, ls] for r in refs], levels, eye)
                d_a = jnp.where(col_cc <= row_cc, _dot_nt(d_o, v), 0.0)
                dq_i, dk_i = _hg_intra_bwd(d_a, q, k, saved, levels, eye)
                dv = _dot_tn(a, d_o) + _dot_nt(kd, dst)
                dq = e_b * _dot(d_o, st0) + dq_i
                dk = e_rem * _dot(v, dst) + dk_i
                extra = jnp.sum(dst * st_end, axis=0, keepdims=True)
                db_o.append(q * dq - k * dk + jnp.where(last_row, extra, 0.0))
                dst_ref[h] = dst * e_end + _dot_tn(d_o, qe)
                dq_o.append(dq)
                dk_o.append(dk)
                dv_o.append(dv)
            dlogf = _dot_hi(triu, jnp.concatenate(db_o, axis=1))
            d_f = dlogf / f - jnp.concatenate(dk_o, axis=1)
            dq_ref[rows, :] = jnp.concatenate(dq_o, axis=1).astype(MM)
            dv_ref[rows, :] = jnp.concatenate(dv_o, axis=1).astype(MM)
            df_ref[rows, :] = (d_f * (1.0 - lb_v) * sg * (1.0 - sg)).astype(MM)
            dlb_ref[...] += jnp.sum(d_f * (1.0 - sg), axis=0, keepdims=True)
            return carry

        lax.fori_loop(0, ncb, chunk, 0, unroll=HG_UNROLL)

        if n_x:
            @pl.when(pl.program_id(0) == nb - 1)
            def _():
                for cp in _xchips_copies(x_in, x_out, x_sems):
                    cp.wait()

    row = pl.BlockSpec((rb, HG_W), lambda i: (nb - 1 - i, 0))
    one = pl.BlockSpec((1, HG_W), lambda i: (0, 0))
    any_spec = pl.BlockSpec(memory_space=pl.ANY)
    return pl.pallas_call(
        body, name="hgrn_bwd", grid=(nb,),
        in_specs=[row, row, row, row, pl.BlockSpec((ncb, HG_W, HG_DIM), lambda i: (nb - 1 - i, 0, 0)), one]
        + [any_spec] * n_x,
        out_specs=[row, row, row, one] + [any_spec] * n_x,
        out_shape=[jax.ShapeDtypeStruct((T, HG_W), MM)] * 3 + [jax.ShapeDtypeStruct((1, HG_W), F32)]
        + _xchips_out_shapes(xchg),
        scratch_shapes=[pltpu.VMEM((HG_HEADS, HG_DIM, HG_DIM), F32)] + (_xchips_sems(n_x) if n_x else []),
        compiler_params=_cparams(("arbitrary",)),
    )(hq, hf, hi, do, s0, lb, *xchg)


def _silu_parts(x):
    sg = _sigmoid(x)
    return x * sg, sg * (1.0 + x * (1.0 - sg))


def _merge_fwd(attn, o, hg, bg, x, g_out, w_bra, w_brb, w_out, T, tm):
    def body(i, attn_ref, o_ref, hg_ref, bg_ref, x_ref, g_ref, wa_ref, wb_ref, wo_ref,
             x1_ref, ya_ref, yb_ref, m_ref, rec_ref):
        g = g_ref[...]

        def recurrent_out(rows):
            for h in range(HG_HEADS):
                ls = slice(h * HG_DIM, (h + 1) * HG_DIM)
                rec_ref[rows, ls] = (_rms(o_ref[rows, ls])[0] * g * _silu_parts(hg_ref[rows, ls])[0]).astype(MM)

        _by_chunks(tm, recurrent_out)
        ya_ref[...] = _dot(attn_ref[...], wa_ref[...])
        yb_ref[...] = jnp.dot(rec_ref[...], wb_ref[...], preferred_element_type=F32)

        def gate(rows):
            m_ref[rows, :] = (_sigmoid(bg_ref[rows, :D_MODEL]) * ya_ref[rows, :]
                              + _sigmoid(bg_ref[rows, D_MODEL:]) * yb_ref[rows, :]).astype(MM)

        _by_chunks(tm, gate)
        x1_ref[...] = x_ref[...] + jnp.dot(m_ref[...], wo_ref[...], preferred_element_type=F32)

    return _row_call("merge_fwd", body, T, tm, [attn, o, hg, bg, x], [g_out, w_bra, w_brb, w_out],
                     [(D_MODEL, F32), (D_MODEL, F32), (D_MODEL, F32), (D_MODEL, MM), (HG_W, MM)], [], VMEM_LIMIT)


def _ffn_fwd(x1, g_ffn, w_g, w_u, w_d, T, tm):
    def body(i, x1_ref, g_ref, wg_ref, wu_ref, wd_ref, x2_ref, gt_ref, up_ref, h2_ref, a_s):
        g = g_ref[...]

        def norm(rows):
            h2_ref[rows, :] = (_rms(x1_ref[rows, :])[0] * g).astype(MM)

        _by_chunks(tm, norm)
        gt_ref[...] = _dot_nt(h2_ref[...], wg_ref[...])
        up_ref[...] = _dot_nt(h2_ref[...], wu_ref[...])

        def act(rows):
            for cs in FFN_HALVES:
                a_s[rows, cs] = (_silu_parts(gt_ref[rows, cs])[0] * up_ref[rows, cs]).astype(MM)

        _by_chunks(tm, act)
        x2_ref[...] = x1_ref[...] + jnp.dot(a_s[...], wd_ref[...], preferred_element_type=F32)

    return _row_call("ffn_fwd", body, T, tm, [x1], [g_ffn, w_g, w_u, w_d],
                     [(D_MODEL, F32), (FFN, F32), (FFN, F32), (D_MODEL, MM)], [], VMEM_LIMIT,
                     scratch=[pltpu.VMEM((tm, FFN), MM)])


def _ple_loss(x2, p, tgt, g_pg, g_post, w_pg, w_pp, T, tm):
    def body(i, x2_ref, p_ref, t_ref, gpg_ref, gpo_ref, wpg_ref, wpp_ref,
             dx2_ref, loss_ref, dgpo_ref, dgpg_ref, dwpg_ref, dwpp_ref, u_s, n3_s, z_s, dz_s, du_s, dy_s, dn3_s):
        @pl.when(i == 0)
        def _():
            for ref in (loss_ref, dgpo_ref, dgpg_ref, dwpg_ref, dwpp_ref):
                ref[...] = jnp.zeros_like(ref)

        gpg, gpo = gpg_ref[...], gpo_ref[...]
        p_mm = p_ref[...].astype(MM)
        for d in range(N_DEV):
            u_s[:, d * HEAD_PAD:(d + 1) * HEAD_PAD] = jnp.dot(p_mm, wpp_ref[d], preferred_element_type=F32)

        def gate_input(rows):
            n3_s[rows, :] = (_rms(x2_ref[rows, :])[0] * gpg).astype(MM)

        _by_chunks(tm, gate_input)
        z_s[...] = jnp.dot(n3_s[...], wpg_ref[...], preferred_element_type=F32)

        def loss_and_back(rows):
            uh, ru = _rms(u_s[rows, :])
            e = uh * gpo
            gate = _sigmoid(z_s[rows, :])
            diff = x2_ref[rows, :] + gate * e - t_ref[rows, :]
            dy = diff * (1.0 / D_MODEL)
            de = dy * gate
            dz_s[rows, :] = (dy * e * gate * (1.0 - gate)).astype(MM)
            du_s[rows, :] = _rms_bwd(de * gpo, uh, ru).astype(MM)
            dy_s[rows, :] = dy
            loss_ref[...] += _fold8(diff * diff) * (0.5 / D_MODEL)
            dgpo_ref[...] += _fold8(de * uh)

        _by_chunks(tm, loss_and_back)
        dn3_s[...] = _dot_nt(dz_s[...], wpg_ref[...])

        def gate_norm_back(rows):
            x2h, r3 = _rms(x2_ref[rows, :])
            dn3 = dn3_s[rows, :]
            dx2_ref[rows, :] = dy_s[rows, :] + _rms_bwd(dn3 * gpg, x2h, r3)
            dgpg_ref[...] += _fold8(dn3 * x2h)

        _by_chunks(tm, gate_norm_back)
        dwpg_ref[...] += _dot_tn(n3_s[...], dz_s[...])
        for d in range(N_DEV):
            dwpp_ref[d] += _dot_tn(p_mm, du_s[:, d * HEAD_PAD:(d + 1) * HEAD_PAD])

    vec = ((8, D_MODEL), F32)
    wide = lambda dt: pltpu.VMEM((tm, D_MODEL), dt)
    return _row_call("ple_loss", body, T, tm, [x2, p, tgt], [g_pg, g_post, w_pg, w_pp], [(D_MODEL, F32)],
                     [vec, vec, vec, ((D_MODEL, D_MODEL), F32), ((N_DEV, PLE, HEAD_PAD), F32)], VMEM_LIMIT,
                     scratch=[wide(F32), wide(MM), wide(F32), wide(MM), wide(MM), wide(F32), wide(F32)])


def _ffn_bwd(dx2, x1, gt, up, g_ffn, w_g, w_u, w_d, T, tm):
    def body(i, dx2_ref, x1_ref, gt_ref, up_ref, g_ref, wg_ref, wu_ref, wd_ref,
             dx1_ref, a_ref, dgt_ref, dup_ref, dg_ref, da_s, dh2_s):
        @pl.when(i == 0)
        def _():
            dg_ref[...] = jnp.zeros_like(dg_ref)

        g = g_ref[...]
        da_s[...] = _dot_nt(dx2_ref[...], wd_ref[...])

        def act_back(rows):
            for cs in FFN_HALVES:
                up, da = up_ref[rows, cs], da_s[rows, cs]
                silu, dsilu = _silu_parts(gt_ref[rows, cs])
                dgt_ref[rows, cs] = (da * up * dsilu).astype(MM)
                dup_ref[rows, cs] = (da * silu).astype(MM)
                a_ref[rows, cs] = (silu * up).astype(MM)

        _by_chunks(tm, act_back)
        dh2_s[...] = (jnp.dot(dgt_ref[...], wg_ref[...], preferred_element_type=F32)
                      + jnp.dot(dup_ref[...], wu_ref[...], preferred_element_type=F32))

        def norm_back(rows):
            x1h, r = _rms(x1_ref[rows, :])
            dh2 = dh2_s[rows, :]
            dx1_ref[rows, :] = dx2_ref[rows, :] + _rms_bwd(dh2 * g, x1h, r)
            dg_ref[...] += _fold8(dh2 * x1h)

        _by_chunks(tm, norm_back)

    return _row_call("ffn_bwd", body, T, tm, [dx2, x1, gt, up], [g_ffn, w_g, w_u, w_d],
                     [(D_MODEL, F32), (FFN, MM), (FFN, MM), (FFN, MM)], [((8, D_MODEL), F32)], VMEM_LIMIT,
                     scratch=[pltpu.VMEM((tm, FFN), F32), pltpu.VMEM((tm, D_MODEL), F32)])


def _merge_bwd(dx1, ya, yb, bg, o, hg, attn, m, rec, g_out, w_bra, w_brb, w_out, T, tm):
    def body(i, dx1_ref, ya_ref, yb_ref, bg_ref, o_ref, hg_ref, attn_ref, m_ref, rec_ref, g_ref, wa_ref, wb_ref, wo_ref,
             dattn_ref, do_ref, dhg_ref, dbg_ref, dg_ref, dwo_ref, dwa_ref, dwb_ref, dm_s, dya_s, dyb_s, drec_s):
        @pl.when(i == 0)
        def _():
            for ref in (dg_ref, dwo_ref, dwa_ref, dwb_ref):
                ref[...] = jnp.zeros_like(ref)

        g = g_ref[...]
        dx1 = dx1_ref[...].astype(MM)
        dm_s[...] = _dot_nt(dx1, wo_ref[...])

        def gate_back(rows):
            dm = dm_s[rows, :]
            ga, gb = _sigmoid(bg_ref[rows, :D_MODEL]), _sigmoid(bg_ref[rows, D_MODEL:])
            dya_s[rows, :] = (dm * ga).astype(MM)
            dyb_s[rows, :] = (dm * gb).astype(MM)
            dbg_ref[rows, :D_MODEL] = (dm * ya_ref[rows, :] * ga * (1.0 - ga)).astype(MM)
            dbg_ref[rows, D_MODEL:] = (dm * yb_ref[rows, :] * gb * (1.0 - gb)).astype(MM)

        _by_chunks(tm, gate_back)
        dwo_ref[...] += _dot_tn(m_ref[...], dx1)
        attn_mm = attn_ref[...].astype(MM)
        for d in range(N_DEV):
            ds = slice(d * HEAD_PAD, (d + 1) * HEAD_PAD)
            dwa_ref[d] += _dot_tn(attn_mm, dya_s[:, ds])
            dwb_ref[d] += _dot_tn(rec_ref[...], dyb_s[:, ds])
        dattn_ref[...] = _dot_nt(dya_s[...], wa_ref[...])
        drec_s[...] = _dot_nt(dyb_s[...], wb_ref[...])

        def recurrent_out_back(rows):
            for h in range(HG_HEADS):
                ls = slice(h * HG_DIM, (h + 1) * HG_DIM)
                oh, r = _rms(o_ref[rows, ls])
                silu, dsilu = _silu_parts(hg_ref[rows, ls])
                dr = drec_s[rows, ls]
                dhg_ref[rows, ls] = (dr * oh * g * dsilu).astype(MM)
                don = dr * silu
                dg_ref[...] += _fold8(don * oh)
                do_ref[rows, ls] = _rms_bwd(don * g, oh, r)

        _by_chunks(tm, recurrent_out_back)

    wide = lambda n, dt: pltpu.VMEM((tm, n), dt)
    return _row_call("merge_bwd", body, T, tm, [dx1, ya, yb, bg, o, hg, attn, m, rec], [g_out, w_bra, w_brb, w_out],
                     [(D_MODEL, F32), (HG_W, F32), (HG_W, MM), (2 * D_MODEL, MM)],
                     [((8, HG_DIM), F32), ((D_MODEL, D_MODEL), F32), ((N_DEV, MLA_HEADS * HEAD_PAD, HEAD_PAD), F32),
                      ((N_DEV, HG_W, HEAD_PAD), F32)], VMEM_LIMIT,
                     scratch=[wide(D_MODEL, F32), wide(D_MODEL, MM), wide(D_MODEL, MM), wide(HG_W, F32)])


def _flash_bwd(qf, kf, vf, o, do, lse, T, xchg=()):
    tq = min(ATT_TILE, T)
    nq = T // tq

    qi_tab, ki_tab = _causal_pairs(nq, by_query=False)

    n_x = len(xchg)
    hp = ATT_HEADS
    n_heads, n_pairs = MLA_HEADS // hp, len(qi_tab)

    def body(qi_ref, ki_ref, q_ref, k_ref, v_ref, o_ref, do_ref, lse_ref, *rest):
        x_in, (dq_ref, dk_ref, dv_ref), rest = rest[:n_x], rest[n_x:n_x + 3], rest[n_x + 3:]
        x_out, x_sems = rest[:n_x], rest[n_x:]
        t = pl.program_id(1)
        qi, ki = qi_ref[t], ki_ref[t]
        if n_x:
            @pl.when((pl.program_id(0) == 0) & (t == 0))
            def _():
                for cp in _xchips_copies(x_in, x_out, x_sems):
                    cp.start()

        @pl.when(t == 0)
        def _():
            dq_ref[...] = jnp.zeros_like(dq_ref)

        def step(first):
            rows = pl.ds(pl.multiple_of(qi * tq, tq), tq)
            for hh in range(hp):
                hs = slice(hh * HEAD_PAD, (hh + 1) * HEAD_PAD)
                q, k, d_o = q_ref[:, hs], k_ref[:, hs], do_ref[:, hs]
                s = _dot_nt(q, k)
                if first:
                    row = lax.broadcasted_iota(jnp.int32, (tq, tq), 0)
                    col = lax.broadcasted_iota(jnp.int32, (tq, tq), 1)
                    s = jnp.where(col <= row, s, NEG)
                p = jnp.exp(s - lse_ref[:, hh * HEAD_PAD:hh * HEAD_PAD + 1])
                delta = jnp.sum(d_o * o_ref[:, hs], axis=1, keepdims=True)
                ds = p * (_dot_nt(d_o, v_ref[:, hs]) - delta)
                dq_ref[rows, hs] += _dot(ds, k)
                if first:
                    dv_ref[:, hs] = _dot_tn(p, d_o)
                    dk_ref[:, hs] = _dot_tn(ds, q)
                else:
                    dv_ref[:, hs] += _dot_tn(p, d_o)
                    dk_ref[:, hs] += _dot_tn(ds, q)

        @pl.when(qi == ki)
        def _():
            step(True)

        @pl.when(qi > ki)
        def _():
            step(False)

        if n_x:
            @pl.when((pl.program_id(0) == n_heads - 1) & (t == n_pairs - 1))
            def _():
                for cp in _xchips_copies(x_in, x_out, x_sems):
                    cp.wait()

    q_spec = pl.BlockSpec((tq, hp * HEAD_PAD), lambda h, t, qi_ref, ki_ref: (qi_ref[t], h))
    kv_spec = pl.BlockSpec((tq, hp * HEAD_PAD), lambda h, t, qi_ref, ki_ref: (ki_ref[t], h))
    any_spec = pl.BlockSpec(memory_space=pl.ANY)
    w = MLA_HEADS * HEAD_PAD
    grid_spec = pltpu.PrefetchScalarGridSpec(
        num_scalar_prefetch=2, grid=(n_heads, n_pairs),
        in_specs=[q_spec, kv_spec, kv_spec, q_spec, q_spec, q_spec] + [any_spec] * n_x,
        out_specs=[pl.BlockSpec((T, hp * HEAD_PAD), lambda h, t, qi_ref, ki_ref: (0, h)), kv_spec, kv_spec]
        + [any_spec] * n_x,
        scratch_shapes=_xchips_sems(n_x) if n_x else [])
    return pl.pallas_call(
        body, name="flash_bwd", grid_spec=grid_spec,
        out_shape=[jax.ShapeDtypeStruct((T, w), F32)] * 3 + _xchips_out_shapes(xchg),
        compiler_params=_cparams(("arbitrary", "arbitrary")),
    )(jnp.asarray(qi_tab), jnp.asarray(ki_tab), qf, kf, vf, o, do, lse, *xchg)


def _mla_heads_bwd(d_out, saved, g_pad, cos_t, sin_t, first):
    d_raw, dg = [], jnp.zeros((1, HEAD_PAD), F32)
    for h in range(MLA_HEADS):
        xh, r = saved[h]
        dy = d_out[:, h * HEAD_PAD:(h + 1) * HEAD_PAD]
        dn = dy * cos_t + _rope_swap(dy * sin_t, first)
        dg = dg + jnp.sum(dn * xh, axis=0, keepdims=True)
        d_raw.append(_rms_bwd(dn * g_pad, xh, r, QK_DIM))
    return d_raw, dg


def _mla_prep_bwd(cq, ckv, kr, pos, dqf, dkf, dvf, g_qa, g_kva, g_qn, g_kn, w_uq, w_ukv, T, tm):
    def body(i, cq_ref, ckv_ref, kr_ref, pos_ref, dq_ref, dk_ref, dv_ref,
             gqa_ref, gkva_ref, gqn_ref, gkn_ref, wuq_ref, wukv_ref,
             dcq_ref, dckv_ref, dkr_ref, dgqa_ref, dgkva_ref, dgqn_ref, dgkn_ref, dwuq_ref, dwukv_ref):
        cos_t, sin_t, first = _rope_tables(pos_ref[...], tm)
        cqh, rq = _rms(cq_ref[...])
        ckvh, rkv = _rms(ckv_ref[...])
        cqn, ckvn = cqh * gqa_ref[...], ckvh * gkva_ref[...]
        q_raw, k_raw, _ = _mla_raw_heads(cqn, ckvn, kr_ref[...], wuq_ref, wukv_ref, tm)
        _, q_saved = _mla_heads_fwd(q_raw, gqn_ref[...], cos_t, sin_t, first)
        _, k_saved = _mla_heads_fwd(k_raw, gkn_ref[...], cos_t, sin_t, first)
        dq_heads, dgqn = _mla_heads_bwd(dq_ref[...] * ATT_SCALE, q_saved, gqn_ref[...], cos_t, sin_t, first)
        dk_heads, dgkn = _mla_heads_bwd(dk_ref[...], k_saved, gkn_ref[...], cos_t, sin_t, first)
        lane = lax.broadcasted_iota(jnp.int32, (tm, HEAD_PAD), 1)
        nope = lane < QK_NOPE
        dcqn = jnp.zeros((tm, Q_RANK), F32)
        dckvn = jnp.zeros((tm, KV_RANK), F32)
        dkr = jnp.zeros((tm, HEAD_PAD), F32)
        cqn_mm, ckvn_mm = cqn.astype(MM), ckvn.astype(MM)
        for h in range(MLA_HEADS):
            hs = slice(h * HEAD_PAD, (h + 1) * HEAD_PAD)
            dq_h = dq_heads[h].astype(MM)
            dkv_h = jnp.where(nope, dk_heads[h], pltpu.roll(dv_ref[:, hs], V_DIM, 1)).astype(MM)
            _acc(dwuq_ref.at[h], i, _dot_tn(dq_h, cqn_mm))
            _acc(dwukv_ref.at[h], i, _dot_tn(ckvn_mm, dkv_h))
            dcqn = dcqn + jnp.dot(dq_h, wuq_ref[h], preferred_element_type=F32)
            dckvn = dckvn + lax.dot_general(dkv_h, wukv_ref[h], (((1,), (1,)), ((), ())), preferred_element_type=F32)
            dkr = dkr + dk_heads[h]
        dkr_ref[...] = jnp.where((lane >= QK_NOPE) & (lane < QK_DIM), dkr, 0.0).astype(MM)
        dcq_ref[...] = _rms_bwd(dcqn * gqa_ref[...], cqh, rq).astype(MM)
        dckv_ref[...] = _rms_bwd(dckvn * gkva_ref[...], ckvh, rkv).astype(MM)
        _acc(dgqa_ref, i, jnp.sum(dcqn * cqh, axis=0, keepdims=True))
        _acc(dgkva_ref, i, jnp.sum(dckvn * ckvh, axis=0, keepdims=True))
        _acc(dgqn_ref, i, dgqn)
        _acc(dgkn_ref, i, dgkn)

    return _row_call(
        "mla_prep_bwd", body, T, tm, [cq, ckv, kr, pos, dqf, dkf, dvf], [g_qa, g_kva, g_qn, g_kn, w_uq, w_ukv],
        [(Q_RANK, MM), (KV_RANK, MM), (HEAD_PAD, MM)],
        [((1, Q_RANK), F32), ((1, KV_RANK), F32), ((1, HEAD_PAD), F32), ((1, HEAD_PAD), F32),
         ((MLA_HEADS, HEAD_PAD, Q_RANK), F32), ((MLA_HEADS, KV_RANK, HEAD_PAD), F32)], VMEM_LIMIT)


def _in_proj_bwd(x, dx1, dsecs, g_mix, w_in, T, tm):
    def body(i, x_ref, dx1_ref, *rest):
        d_refs, (g_ref, w_ref, dx_ref, dp_ref, dg_ref, dh_s) = rest[:len(SECTIONS)], rest[len(SECTIONS):]

        @pl.when(i == 0)
        def _():
            dg_ref[...] = jnp.zeros_like(dg_ref)

        g = g_ref[...]

        def join_and_cut(rows):
            pieces = [(d_ref[rows, QK_NOPE:QK_DIM] if n == QK_ROPE else d_ref[rows, :]).astype(F32)
                      for (_, n), d_ref in zip(COL_SECTIONS, d_refs)]
            dproj = jnp.concatenate(pieces, axis=1)
            for d in range(N_DEV):
                dp_ref[d, rows, :] = dproj[:, d * IN_BLOCK:(d + 1) * IN_BLOCK].astype(MM)

        _by_chunks(tm, join_and_cut)
        dh = jnp.dot(dp_ref[0], w_ref[0], preferred_element_type=F32)
        for d in range(1, N_DEV):
            dh = dh + jnp.dot(dp_ref[d], w_ref[d], preferred_element_type=F32)
        dh_s[...] = dh

        def norm_back(rows):
            xh, r = _rms(x_ref[rows, :])
            dh_c = dh_s[rows, :]
            dx_ref[rows, :] = dx1_ref[rows, :] + _rms_bwd(dh_c * g, xh, r)
            dg_ref[...] += _fold8(dh_c * xh)

        _by_chunks(tm, norm_back)

    in_specs = [pl.BlockSpec((tm, a.shape[1]), lambda i: (i, 0)) for a in [x, dx1, *dsecs]]
    in_specs += [pl.BlockSpec(g_mix.shape, lambda i: (0, 0)),
                 pl.BlockSpec(w_in.shape, lambda i: (0, 0, 0), pipeline_mode=pl.Buffered(1))]

    def kern(*refs):
        body(pl.program_id(0), *refs)

    return pl.pallas_call(
        kern, name="in_proj_bwd", grid=(T // tm,), in_specs=in_specs,
        out_specs=[pl.BlockSpec((tm, D_MODEL), lambda i: (i, 0)),
                   pl.BlockSpec((N_DEV, tm, IN_BLOCK), lambda i: (0, i, 0)),
                   pl.BlockSpec((8, D_MODEL), lambda i: (0, 0))],
        out_shape=[jax.ShapeDtypeStruct((T, D_MODEL), F32), jax.ShapeDtypeStruct((N_DEV, T, IN_BLOCK), MM),
                   jax.ShapeDtypeStruct((8, D_MODEL), F32)],
        scratch_shapes=[pltpu.VMEM((tm, D_MODEL), F32)],
        compiler_params=_cparams(("arbitrary",), VMEM_LIMIT),
    )(x, dx1, *dsecs, g_mix, w_in)


def _pick_block(n, cap):
    best = None
    for cand in range(128, min(n, cap) + 1, 128):
        if n % cand == 0:
            best = cand
    return n if best is None else best


def _pick_rows(n, cap):
    best = n
    for cand in range(8, min(n, cap) + 1, 8):
        if n % cand == 0:
            best = cand
    return best


def _matmul_tn(name, a, b):
    T, M = a.shape
    N = b.shape[1]
    bm, bk = _pick_block(M, 1408), min(512, T)
    bn = _pick_block(N, 2560)

    def body(a_ref, b_ref, c_ref):
        @pl.when(pl.program_id(2) == 0)
        def _():
            c_ref[...] = jnp.zeros_like(c_ref)

        c_ref[...] += _dot_tn(a_ref[...], b_ref[...])

    return pl.pallas_call(
        body, name=name, grid=(M // bm, N // bn, T // bk),
        in_specs=[pl.BlockSpec((bk, bm), lambda i, j, k: (k, i)), pl.BlockSpec((bk, bn), lambda i, j, k: (k, j))],
        out_specs=pl.BlockSpec((bm, bn), lambda i, j, k: (i, j)), out_shape=jax.ShapeDtypeStruct((M, N), F32),
        compiler_params=_cparams(("parallel", "parallel", "arbitrary"), VMEM_LIMIT),
    )(a, b)


def _matmul_tn_blocks(name, a, b):
    T, M = a.shape
    nd, _, c = b.shape
    bm, bk = _pick_block(M, 512), min(512, T)

    def body(a_ref, b_ref, c_ref):
        @pl.when(pl.program_id(1) == 0)
        def _():
            c_ref[...] = jnp.zeros_like(c_ref)

        a_blk = a_ref[...].astype(MM)
        for d in range(nd):
            c_ref[d] += _dot_tn(b_ref[d], a_blk)

    return pl.pallas_call(
        body, name=name, grid=(M // bm, T // bk),
        in_specs=[pl.BlockSpec((bk, bm), lambda i, k: (k, i)), pl.BlockSpec((nd, bk, c), lambda i, k: (0, k, 0))],
        out_specs=pl.BlockSpec((nd, c, bm), lambda i, k: (0, 0, i)),
        out_shape=jax.ShapeDtypeStruct((nd, c, M), F32),
        compiler_params=_cparams(("parallel", "arbitrary"), VMEM_LIMIT),
    )(a, b)


def _pad_gain(g, n):
    return jnp.pad(g.reshape(1, -1), ((0, 0), (0, n - g.shape[-1])))


GROUP_A = ("w_ffn_gate", "w_ffn_up", "w_ffn_down", "w_ple_gate", "w_ple_proj")
GROUP_B = ("w_branch", "w_out")
GROUP_C = ("w_in", "w_uq", "w_ukv")
EARLY = GROUP_C
LATE = GROUP_B + GROUP_A
TRANSPOSED = ("w_in", "w_uq", "w_ffn_gate", "w_ffn_up")


def _local_step(x, p, pos, tgt, small, big, late_blocks=None, core=None):
    T = x.shape[0]
    tm = min(ROW_TILE, T)
    w_in = big["w_in"]
    w_uq = jnp.pad(big["w_uq"], ((0, 0), (0, HEAD_PAD - QK_DIM), (0, 0)))
    w_ukv = big["w_ukv"]

    g_mix, g_qa, g_kva = small["mix_norm_g"], small["q_a_norm_g"], small["kv_a_norm_g"]
    g_qn, g_kn = _pad_gain(small["q_norm_g"], HEAD_PAD), _pad_gain(small["k_norm_g"], HEAD_PAD)
    g_out, g_ffn = small["hg_out_norm_g"], small["ffn_norm_g"]
    g_pg, g_post = small["ple_gate_norm_g"], small["ple_post_norm_g"]
    logits = small["hg_lb_logits"]
    lb = _lower_bound(logits)

    h, cq, ckv, kr, hq, hf, hi, hg, bg = _in_proj_fwd(x, g_mix, w_in, T, tm)
    qf, kf, vf = _mla_prep_fwd(cq, ckv, kr, pos, g_qa, g_kva, g_qn, g_kn, w_uq, w_ukv, T, tm)
    if late_blocks is None:
        attn, lse = _flash_fwd(qf, kf, vf, T)
    else:
        attn, lse, *late = _flash_fwd(qf, kf, vf, T, ag_blocks=[late_blocks[n] for n in LATE])
        big = {**big, **dict(zip(LATE, late))}
    w_branch = jnp.moveaxis(big["w_branch"].reshape(N_DEV, 2, HG_W, HEAD_PAD), 0, 2).reshape(2, HG_W, D_MODEL)
    w_bra = jnp.pad(w_branch[0].reshape(MLA_HEADS, V_DIM, D_MODEL),
                    ((0, 0), (0, HEAD_PAD - V_DIM), (0, 0))).reshape(MLA_HEADS * HEAD_PAD, D_MODEL)
    w_brb = w_branch[1]
    w_out = big["w_out"].reshape(D_MODEL, D_MODEL)
    w_g, w_u = big["w_ffn_gate"].reshape(FFN, D_MODEL), big["w_ffn_up"].reshape(FFN, D_MODEL)
    w_d = big["w_ffn_down"].reshape(FFN, D_MODEL)
    w_pg, w_pp = big["w_ple_gate"].reshape(D_MODEL, D_MODEL), big["w_ple_proj"]
    o, s0 = _hgrn_fwd(hq, hf, hi, lb, T)
    x1, ya, yb, m, rec = _merge_fwd(attn, o, hg, bg, x, g_out, w_bra, w_brb, w_out, T, tm)
    x2, gt, up, h2 = _ffn_fwd(x1, g_ffn, w_g, w_u, w_d, T, tm)
    dx2, loss_p, dg_post, dg_pg, d_pg, d_pp = _ple_loss(x2, p, tgt, g_pg, g_post, w_pg, w_pp, T, tm)
    dg_post, dg_pg = (jnp.sum(t, axis=0, keepdims=True) for t in (dg_post, dg_pg))

    grads, sibs, gots = {}, {}, {}

    def reduce_start(tag, names):
        if core is None:
            return ()
        got = _exchange_sibling("rs_sibling_" + tag, [grads[n] for n in names])
        sibs.update(zip(names, got))
        return [_chip_partial("rs_partial_" + n, grads[n], sibs[n], core) for n in names]

    dx1, a, dgt, dup, dg_ffn = _ffn_bwd(dx2, x1, gt, up, g_ffn, w_g, w_u, w_d, T, tm)
    dg_ffn = jnp.sum(dg_ffn, axis=0, keepdims=True)
    grads["w_ffn_gate"] = _matmul_tn("dw_gate", dgt, h2).reshape(N_DEV, -1, D_MODEL)
    grads["w_ffn_up"] = _matmul_tn("dw_up", dup, h2).reshape(N_DEV, -1, D_MODEL)
    grads["w_ffn_down"] = _matmul_tn("dw_down", a, dx2).reshape(N_DEV, -1, D_MODEL)
    grads["w_ple_gate"] = d_pg.reshape(N_DEV, -1, D_MODEL)
    grads["w_ple_proj"] = d_pp
    parts_a = reduce_start("a", GROUP_A)

    dattn, do, dhg, dbg, dg_out, d_out, d_bra, d_brb = _merge_bwd(
        dx1, ya, yb, bg, o, hg, attn, m, rec, g_out, w_bra, w_brb, w_out, T, tm)
    dg_out = jnp.sum(dg_out, axis=0, keepdims=True)
    d_bra = d_bra.reshape(N_DEV, MLA_HEADS, HEAD_PAD, HEAD_PAD)[:, :, :V_DIM].reshape(N_DEV, HG_W, HEAD_PAD)
    grads["w_branch"] = jnp.concatenate([d_bra, d_brb], axis=1)
    grads["w_out"] = d_out.reshape(N_DEV, -1, D_MODEL)
    parts_b = reduce_start("b", GROUP_B)

    dhq, dhf, dhi, dlb, *got_a = _hgrn_bwd(hq, hf, hi, do, s0, lb, T, xchg=parts_a)
    dqf, dkf, dvf, *got_b = _flash_bwd(qf, kf, vf, attn, dattn, lse, T, xchg=parts_b)
    (dcq, dckv, dkr, dg_qa, dg_kva, dg_qn, dg_kn, d_uq, d_ukv) = _mla_prep_bwd(
        cq, ckv, kr, pos, dqf, dkf, dvf, g_qa, g_kva, g_qn, g_kn, w_uq, w_ukv, T, tm)
    grad_x, dproj, dg_mix = _in_proj_bwd(x, dx1, [dcq, dckv, dkr, dhq, dhf, dhi, dhg, dbg], g_mix, w_in, T, tm)
    dg_mix = jnp.sum(dg_mix, axis=0, keepdims=True)
    grads["w_in"] = _matmul_tn_blocks("dw_in", h, dproj)
    grads["w_uq"] = d_uq[:, :QK_DIM]
    grads["w_ukv"] = d_ukv
    parts_c = reduce_start("c", GROUP_C)
    if core is not None:
        gots.update(zip(GROUP_A, got_a))
        gots.update(zip(GROUP_B, got_b))
        gots.update(zip(GROUP_C, _exchange_chips(parts_c)))

    dl0 = dlb * lb * (1.0 - lb)
    small_g = {
        "mix_norm_g": dg_mix, "q_a_norm_g": dg_qa, "kv_a_norm_g": dg_kva,
        "q_norm_g": dg_qn[:, :QK_DIM], "k_norm_g": dg_kn[:, :QK_DIM],
        "hg_lb_logits": jnp.concatenate([dl0, -dl0], axis=0), "hg_out_norm_g": dg_out,
        "ffn_norm_g": dg_ffn, "ple_gate_norm_g": dg_pg, "ple_post_norm_g": dg_post,
    }
    return loss_p, grad_x, small_g, grads, sibs, gots


def _lower_bound(logits):
    def body(l_ref, lb_ref):
        l = l_ref[...]
        mx = jnp.max(l, axis=0, keepdims=True)
        e = jnp.exp(l - mx)
        lb_ref[...] = e[0:1] / jnp.sum(e, axis=0, keepdims=True)

    return pl.pallas_call(body, name="lower_bound", out_shape=jax.ShapeDtypeStruct((1, HG_W), F32))(logits)


def _my_place():
    return lax.axis_index("x"), lax.axis_index("y"), lax.axis_index("c")


def _all_gather(name, blocks):
    n = len(blocks)

    def body(*refs):
        x_refs, out_refs, sems = refs[:n], refs[n:2 * n], refs[2 * n:]
        _ag_start(x_refs, out_refs, sems)
        _ag_finish(x_refs, out_refs, sems)

    any_spec = pl.BlockSpec(memory_space=pl.ANY)
    return pl.pallas_call(
        body, name=name, out_shape=_ag_out_shapes(blocks),
        in_specs=[any_spec] * n, out_specs=[any_spec] * n, scratch_shapes=_ag_sems(n),
    )(*blocks)


def _ag_out_shapes(blocks):
    return [jax.ShapeDtypeStruct((N_DEV,) + b.shape, b.dtype) for b in blocks]


def _ag_sems(n):
    return [pltpu.SemaphoreType.DMA((7 * n,)), pltpu.SemaphoreType.DMA((7 * n,)), pltpu.SemaphoreType.DMA((n,))]


def _ag_parts(x_refs, out_refs, sems):
    send_sems, recv_sems, local_sems = sems
    x, y, c = _my_place()
    me, sibling = (x, y, c), (x, y, 1 - c)
    chips = [(1 - x, y), (x, 1 - y), (1 - x, 1 - y)]
    n = len(x_refs)

    def copy(a, k, block, to, own=False):
        px, py, pc = block
        dst = out_refs[a].at[4 * px + 2 * py + pc]
        return pltpu.make_async_remote_copy(
            src_ref=x_refs[a] if own else dst, dst_ref=dst, send_sem=send_sems.at[7 * a + k],
            recv_sem=recv_sems.at[7 * a + k], device_id=to, device_id_type=MESH_ID)

    mine = [pltpu.make_async_copy(x_refs[a], out_refs[a].at[4 * x + 2 * y + c], local_sems.at[a]) for a in range(n)]
    first = []
    for a in range(n):
        first.append(copy(a, 0, me, sibling, own=True))
        first += [copy(a, 1 + j, me, (*chip, c), own=True) for j, chip in enumerate(chips)]
    return copy, mine, first, me, sibling, chips, c, n


def _ag_start(x_refs, out_refs, sems):
    _, mine, first, *_ = _ag_parts(x_refs, out_refs, sems)
    for cp in mine + first:
        cp.start()


def _ag_finish(x_refs, out_refs, sems):
    copy, mine, first, me, sibling, chips, c, n = _ag_parts(x_refs, out_refs, sems)
    passed = []
    for j, chip in enumerate(chips):
        for a in range(n):
            copy(a, 1 + j, (*chip, c), me).wait_recv()
            passed.append(copy(a, 4 + j, (*chip, c), sibling))
            passed[-1].start()
    for a in range(n):
        copy(a, 0, sibling, me).wait_recv()
    for j, chip in enumerate(chips):
        for a in range(n):
            copy(a, 4 + j, (*chip, 1 - c), me).wait_recv()
    for cp in first + passed:
        cp.wait_send()
    for cp in mine:
        cp.wait()


def _exchange_sibling(name, gs):
    n = len(gs)

    def body(*refs):
        g_refs, out_refs, (send_sems, recv_sems) = refs[:n], refs[n:2 * n], refs[2 * n:]
        x, y, c = _my_place()
        copies = [pltpu.make_async_remote_copy(
            src_ref=g_refs[a].at[2 * j + 1 - c], dst_ref=out_refs[a].at[j], send_sem=send_sems.at[4 * a + j],
            recv_sem=recv_sems.at[4 * a + j], device_id=(x, y, 1 - c), device_id_type=MESH_ID)
            for a in range(n) for j in range(4)]
        for cp in copies:
            cp.start()
        for cp in copies:
            cp.wait()

    any_spec = pl.BlockSpec(memory_space=pl.ANY)
    return pl.pallas_call(
        body, name=name, out_shape=[jax.ShapeDtypeStruct((4,) + g.shape[1:], g.dtype) for g in gs],
        in_specs=[any_spec] * n, out_specs=[any_spec] * n,
        scratch_shapes=[pltpu.SemaphoreType.DMA((4 * n,)), pltpu.SemaphoreType.DMA((4 * n,))],
    )(*gs)


def _chip_partial(name, g, got, c_idx):
    _, rows, cols = g.shape
    tr, tc = _tile_2d(rows, cols, 512)

    def body(c_ref, g_ref, got_ref, out_ref):
        out_ref[...] = (g_ref[...] + got_ref[...]).astype(MM)

    grid_spec = pltpu.PrefetchScalarGridSpec(
        num_scalar_prefetch=1, grid=(4, rows // tr, cols // tc),
        in_specs=[pl.BlockSpec((1, tr, tc), lambda j, i, k, c_ref: (2 * j + c_ref[0], i, k)),
                  pl.BlockSpec((1, tr, tc), lambda j, i, k, c_ref: (j, i, k))],
        out_specs=pl.BlockSpec((1, tr, tc), lambda j, i, k, c_ref: (j, i, k)))
    return pl.pallas_call(
        body, name=name, grid_spec=grid_spec, out_shape=jax.ShapeDtypeStruct((4, rows, cols), MM),
        compiler_params=_cparams(("parallel", "parallel", "parallel")),
    )(c_idx, g, got)


def _tile_2d(rows, cols, row_cap):
    if rows % 8 == 0:
        return _pick_rows(rows, row_cap), cols
    return rows, 256 if cols % 256 == 0 else cols


def _exchange_chips(parts):
    n = len(parts)

    def body(*refs):
        p_refs, out_refs, sems = refs[:n], refs[n:2 * n], refs[2 * n:]
        for cp in _xchips_copies(p_refs, out_refs, sems):
            cp.start()
        for cp in _xchips_copies(p_refs, out_refs, sems):
            cp.wait()

    any_spec = pl.BlockSpec(memory_space=pl.ANY)
    return pl.pallas_call(
        body, name="rs_chips", out_shape=_xchips_out_shapes(parts),
        in_specs=[any_spec] * n, out_specs=[any_spec] * n, scratch_shapes=_xchips_sems(n),
    )(*parts)


def _xchips_out_shapes(parts):
    return [jax.ShapeDtypeStruct((3,) + p.shape[1:], p.dtype) for p in parts]


def _xchips_sems(n):
    return [pltpu.SemaphoreType.DMA((3 * n,)), pltpu.SemaphoreType.DMA((3 * n,))]


def _xchips_copies(p_refs, out_refs, sems):
    send_sems, recv_sems = sems
    x, y, c = _my_place()
    chips = [(1 - x, y), (x, 1 - y), (1 - x, 1 - y)]
    return [pltpu.make_async_remote_copy(
        src_ref=p_refs[a].at[2 * px + py], dst_ref=out_refs[a].at[k], send_sem=send_sems.at[3 * a + k],
        recv_sem=recv_sems.at[3 * a + k], device_id=(px, py, c), device_id_type=MESH_ID)
        for a in range(len(p_refs)) for k, (px, py) in enumerate(chips)]


def _adamw_math(w, g, m, v):
    m = ADAM_B1 * m + (1.0 - ADAM_B1) * g
    v = ADAM_B2 * v + (1.0 - ADAM_B2) * jnp.square(g)
    m_hat = m / (1.0 - ADAM_B1 ** ADAM_STEP)
    v_hat = v / (1.0 - ADAM_B2 ** ADAM_STEP)
    delta = -ADAM_LR * (m_hat / (jnp.sqrt(v_hat) + ADAM_EPS) + ADAM_WD * w)
    return delta, m, v


def _sum_adamw(name, g, sib, got, w, m, v, slot_idx, chip_idx):
    _, rows, cols = g.shape
    tr, tc = _tile_2d(rows, cols, 256)

    def body(s_ref, j_ref, g_ref, sib_ref, got_ref, w_ref, m_ref, v_ref, go_ref, d_ref, m2_ref, v2_ref):
        grad = g_ref[0] + sib_ref[0]
        for k in range(3):
            grad = grad + got_ref[k].astype(F32)
        go_ref[...] = grad
        d_ref[...], m2_ref[...], v2_ref[...] = _adamw_math(w_ref[...], grad, m_ref[...], v_ref[...])

    flat = pl.BlockSpec((tr, tc), lambda i, k, s_ref, j_ref: (i, k))
    grid_spec = pltpu.PrefetchScalarGridSpec(
        num_scalar_prefetch=2, grid=(rows // tr, cols // tc),
        in_specs=[pl.BlockSpec((1, tr, tc), lambda i, k, s_ref, j_ref: (s_ref[0], i, k)),
                  pl.BlockSpec((1, tr, tc), lambda i, k, s_ref, j_ref: (j_ref[0], i, k)),
                  pl.BlockSpec((3, tr, tc), lambda i, k, s_ref, j_ref: (0, i, k)), flat, flat, flat],
        out_specs=[flat] * 4)
    return pl.pallas_call(
        body, name=name, grid_spec=grid_spec, out_shape=[jax.ShapeDtypeStruct((rows, cols), F32)] * 4,
        compiler_params=_cparams(("parallel", "parallel")),
    )(slot_idx, chip_idx, g, sib, got, w, m, v)


def _adamw_small(parts, w, m, v):
    rows = w.shape[0]

    def body(p_ref, w_ref, m_ref, v_ref, g_ref, d_ref, m2_ref, v2_ref):
        g = p_ref[0]
        for d in range(1, N_DEV):
            g = g + p_ref[d]
        g_ref[...] = g
        d_ref[...], m2_ref[...], v2_ref[...] = _adamw_math(w_ref[...], g, m_ref[...], v_ref[...])

    return pl.pallas_call(
        body, name="adamw_small", out_shape=[jax.ShapeDtypeStruct((rows, 128), F32)] * 4,
    )(parts, w, m, v)


BIG = ("w_in", "w_uq", "w_ukv", "w_branch", "w_out", "w_ffn_gate", "w_ffn_up", "w_ffn_down", "w_ple_gate", "w_ple_proj")
SMALL = (
    ("mix_norm_g", 1024), ("q_a_norm_g", 384), ("kv_a_norm_g", 256), ("q_norm_g", 96), ("k_norm_g", 96),
    ("hg_lb_logits", 1024), ("hg_out_norm_g", 128), ("ffn_norm_g", 1024), ("ple_gate_norm_g", 1024),
    ("ple_post_norm_g", 1024),
)
SMALL_ROWS = 56


def _pack_small(vals):
    rows = []
    for name, n in SMALL:
        v = vals[name].reshape(1, -1).astype(F32)
        rows.append(jnp.pad(v, ((0, 0), (0, (-n) % 128))).reshape(-1, 128))
    return jnp.concatenate(rows, axis=0)


def _unpack_small(packed, shapes):
    out, r = {}, 0
    for name, n in SMALL:
        k = (n + 127) // 128
        out[name] = packed[r:r + k].reshape(1, -1)[:, :n].reshape(shapes[name])
        r += k
    return out


_WEIGHTS = ["mix_norm_g", "w_in", "q_a_norm_g", "w_uq", "kv_a_norm_g", "w_ukv", "q_norm_g", "k_norm_g", "hg_lb_logits",
            "hg_out_norm_g", "w_branch", "w_out", "ffn_norm_g", "w_ffn_gate", "w_ffn_up", "w_ffn_down",
            "ple_gate_norm_g", "w_ple_gate", "w_ple_proj", "ple_post_norm_g"]


def _step(x, p, positions, tgt, w, m, v):
    small_names = [n for n, _ in SMALL]
    T = x.shape[1]
    px, py, pc = _my_place()
    as_idx = lambda t: jnp.reshape(t, (1,)).astype(jnp.int32)

    def two_d(n, t):
        t = t.reshape(-1, t.shape[-1])
        return t.T if n in TRANSPOSED else t

    def full_shape(n, t):
        return (t.T if n in TRANSPOSED else t).reshape(w[n].shape)

    blocks = {n: two_d(n, w[n]).astype(MM) for n in BIG}
    big = dict(zip(EARLY, _all_gather("ag_weights", [blocks[n] for n in EARLY])))
    small = {n: (w[n] if n == "hg_lb_logits" else w[n].reshape(1, -1)) for n in small_names}

    loss_p, grad_x, small_g, grads, sibs, gots = _local_step(
        x[0], p[0, 0], positions.reshape(T, 1), tgt[0], small, big, late_blocks=blocks, core=as_idx(pc))

    out_g, out_d, out_m, out_v = {}, {}, {}, {}
    for n in BIG:
        res = _sum_adamw("adamw_" + n, grads[n], sibs[n], gots[n], two_d(n, w[n]), two_d(n, m[n]), two_d(n, v[n]),
                         as_idx(4 * px + 2 * py + pc), as_idx(2 * px + py))
        out_g[n], out_d[n], out_m[n], out_v[n] = [full_shape(n, r) for r in res]

    packed_g = _pack_small(small_g)
    loss_row = jnp.concatenate([jnp.pad(jnp.sum(loss_p).reshape(1, 1), ((0, 0), (0, 127))),
                                jnp.zeros((SMALL_ROWS - packed_g.shape[0] - 1, 128), F32)], axis=0)
    parts = _all_gather("ag_small", [jnp.concatenate([packed_g, loss_row], axis=0)])[0]
    pad_rows = lambda t: jnp.pad(t, ((0, SMALL_ROWS - t.shape[0]), (0, 0)))
    sw = pad_rows(_pack_small({n: w[n] for n in small_names}))
    sm = pad_rows(_pack_small({n: m[n] for n in small_names}))
    sv = pad_rows(_pack_small({n: v[n] for n in small_names}))
    g_s, d_s, m_s, v_s = _adamw_small(parts, sw, sm, sv)
    shapes = {n: w[n].shape for n in small_names}
    n_packed = packed_g.shape[0]
    loss = g_s[n_packed, 0]
    for src, dst in ((g_s, out_g), (d_s, out_d), (m_s, out_m), (v_s, out_v)):
        dst.update(_unpack_small(src, shapes))

    outs = [loss, grad_x[None]]
    for table in (out_g, out_d, out_m, out_v):
        outs += [table[n] for n in _WEIGHTS]
    return tuple(outs)


def kernel(x, p, positions, mix_norm_g, w_in, q_a_norm_g, w_uq, kv_a_norm_g, w_ukv, q_norm_g, k_norm_g, hg_lb_logits, hg_out_norm_g, w_branch, w_out, ffn_norm_g, w_ffn_gate, w_ffn_up, w_ffn_down, ple_gate_norm_g, w_ple_gate, w_ple_proj, ple_post_norm_g, loss_target, m_mix_norm_g, m_w_in, m_q_a_norm_g, m_w_uq, m_kv_a_norm_g, m_w_ukv, m_q_norm_g, m_k_norm_g, m_hg_lb_logits, m_hg_out_norm_g, m_w_branch, m_w_out, m_ffn_norm_g, m_w_ffn_gate, m_w_ffn_up, m_w_ffn_down, m_ple_gate_norm_g, m_w_ple_gate, m_w_ple_proj, m_ple_post_norm_g, v_mix_norm_g, v_w_in, v_q_a_norm_g, v_w_uq, v_kv_a_norm_g, v_w_ukv, v_q_norm_g, v_k_norm_g, v_hg_lb_logits, v_hg_out_norm_g, v_w_branch, v_w_out, v_ffn_norm_g, v_w_ffn_gate, v_w_ffn_up, v_w_ffn_down, v_ple_gate_norm_g, v_w_ple_gate, v_w_ple_proj, v_ple_post_norm_g):
    w = dict(mix_norm_g=mix_norm_g, w_in=w_in, q_a_norm_g=q_a_norm_g, w_uq=w_uq, kv_a_norm_g=kv_a_norm_g, w_ukv=w_ukv,
             q_norm_g=q_norm_g, k_norm_g=k_norm_g, hg_lb_logits=hg_lb_logits, hg_out_norm_g=hg_out_norm_g,
             w_branch=w_branch, w_out=w_out, ffn_norm_g=ffn_norm_g, w_ffn_gate=w_ffn_gate, w_ffn_up=w_ffn_up,
             w_ffn_down=w_ffn_down, ple_gate_norm_g=ple_gate_norm_g, w_ple_gate=w_ple_gate, w_ple_proj=w_ple_proj,
             ple_post_norm_g=ple_post_norm_g)
    m = dict(mix_norm_g=m_mix_norm_g, w_in=m_w_in, q_a_norm_g=m_q_a_norm_g, w_uq=m_w_uq, kv_a_norm_g=m_kv_a_norm_g,
             w_ukv=m_w_ukv, q_norm_g=m_q_norm_g, k_norm_g=m_k_norm_g, hg_lb_logits=m_hg_lb_logits,
             hg_out_norm_g=m_hg_out_norm_g, w_branch=m_w_branch, w_out=m_w_out, ffn_norm_g=m_ffn_norm_g,
             w_ffn_gate=m_w_ffn_gate, w_ffn_up=m_w_ffn_up, w_ffn_down=m_w_ffn_down,
             ple_gate_norm_g=m_ple_gate_norm_g, w_ple_gate=m_w_ple_gate, w_ple_proj=m_w_ple_proj,
             ple_post_norm_g=m_ple_post_norm_g)
    v = dict(mix_norm_g=v_mix_norm_g, w_in=v_w_in, q_a_norm_g=v_q_a_norm_g, w_uq=v_w_uq, kv_a_norm_g=v_kv_a_norm_g,
             w_ukv=v_w_ukv, q_norm_g=v_q_norm_g, k_norm_g=v_k_norm_g, hg_lb_logits=v_hg_lb_logits,
             hg_out_norm_g=v_hg_out_norm_g, w_branch=v_w_branch, w_out=v_w_out, ffn_norm_g=v_ffn_norm_g,
             w_ffn_gate=v_w_ffn_gate, w_ffn_up=v_w_ffn_up, w_ffn_down=v_w_ffn_down,
             ple_gate_norm_g=v_ple_gate_norm_g, w_ple_gate=v_w_ple_gate, w_ple_proj=v_w_ple_proj,
             ple_post_norm_g=v_ple_post_norm_g)
    return _step(x, p, positions, loss_target, w, m, v)
```

```python
import functools

import jax
import jax.numpy as jnp
import numpy as np
from jax import lax
from jax.experimental import pallas as pl
from jax.experimental.pallas import tpu as pltpu

F32 = jnp.float32
MM = jnp.bfloat16
HI = lax.Precision.HIGHEST
MESH_ID = pl.DeviceIdType.MESH

D_MODEL = 1024
N_DEV = 8
MLA_HEADS = 8
QK_NOPE = 64
QK_ROPE = 32
QK_DIM = 96
V_DIM = 64
HEAD_PAD = 128
Q_RANK = 384
KV_RANK = 256
ROPE_BASE = 10000.0
HG_HEADS = 4
HG_DIM = 128
HG_W = 512
HG_CHUNK = 64
FFN = 2816
PLE = 256
EPS = 1e-6
ATT_SCALE = QK_DIM ** -0.5
NEG = -1e30

ADAM_LR = 0.001
ADAM_B1 = 0.9
ADAM_B2 = 0.999
ADAM_EPS = 1e-08
ADAM_WD = 0.01
ADAM_STEP = 10

SEC_CQ = (0, 384)
SEC_CKV = (384, 256)
SEC_KR = (640, 128)
SEC_HQ = (768, 512)
SEC_HF = (1280, 512)
SEC_HI = (1792, 512)
SEC_HG = (2304, 512)
SEC_BG = (2816, 2048)
IN_PAD = 4864
SECTIONS = (SEC_CQ, SEC_CKV, SEC_KR, SEC_HQ, SEC_HF, SEC_HI, SEC_HG, SEC_BG)
COL_SECTIONS = ((0, 384), (384, 256), (640, 32), (672, 512), (1184, 512), (1696, 512), (2208, 512), (2720, 2048))
IN_COLS = 4768
IN_BLOCK = IN_COLS // 8

VMEM_LIMIT = 58 * 1024 * 1024
ROW_TILE = 256
ATT_TILE = 1024
ATT_HEADS = 2
HG_BLOCK = 512
HG_UNROLL = 4


def _dot(a, b):
    return jnp.dot(a.astype(MM), b.astype(MM), preferred_element_type=F32)


def _dot_nt(a, b):
    return lax.dot_general(a.astype(MM), b.astype(MM), (((1,), (1,)), ((), ())), preferred_element_type=F32)


def _dot_tn(a, b):
    return lax.dot_general(a.astype(MM), b.astype(MM), (((0,), (0,)), ((), ())), preferred_element_type=F32)


def _dot_hi(a, b):
    return jnp.dot(a, b, preferred_element_type=F32, precision=HI)


def _sigmoid(x):
    return 1.0 / (1.0 + jnp.exp(-x))


def _rms(x, n=None):
    n = x.shape[-1] if n is None else n
    r = lax.rsqrt(jnp.sum(x * x, axis=-1, keepdims=True) * (1.0 / n) + EPS)
    return x * r, r


def _rms_bwd(dxh, xh, r, n=None):
    n = xh.shape[-1] if n is None else n
    return r * (dxh - xh * (jnp.sum(dxh * xh, axis=-1, keepdims=True) * (1.0 / n)))


def _rope_tables(pos, tm):
    lane = lax.broadcasted_iota(jnp.int32, (tm, HEAD_PAD), 1)
    idx = jnp.where(lane < QK_NOPE + QK_ROPE // 2, lane - QK_NOPE, lane - QK_NOPE - QK_ROPE // 2)
    inv = jnp.exp(idx.astype(F32) * (-np.log(ROPE_BASE) * 2.0 / QK_ROPE))
    ang = pos.astype(F32) * inv
    in_rope = (lane >= QK_NOPE) & (lane < QK_DIM)
    first = lane < QK_NOPE + QK_ROPE // 2
    cos_t = jnp.where(in_rope, jnp.cos(ang), 1.0)
    sin_t = jnp.where(in_rope, jnp.where(first, -jnp.sin(ang), jnp.sin(ang)), 0.0)
    return cos_t, sin_t, (first, in_rope)


def _rope_swap(x, halves):
    first, in_rope = halves
    half = QK_ROPE // 2
    return jnp.where(in_rope, jnp.where(first, pltpu.roll(x, HEAD_PAD - half, 1), pltpu.roll(x, half, 1)), 0.0)


def _cparams(sem, vmem=None):
    return pltpu.CompilerParams(dimension_semantics=sem, vmem_limit_bytes=vmem)


def _row_call(name, body, T, tm, row_ins, full_ins, row_outs, acc_outs, vmem=None, scratch=()):
    def kern(*refs):
        body(pl.program_id(0), *refs)

    in_specs = [pl.BlockSpec((tm, a.shape[1]), lambda i: (i, 0)) for a in row_ins]
    in_specs += [pl.BlockSpec(a.shape, lambda i, nd=a.ndim: (0,) * nd, pipeline_mode=pl.Buffered(1)) for a in full_ins]
    out_specs = [pl.BlockSpec((tm, n), lambda i: (i, 0)) for n, _ in row_outs]
    out_specs += [pl.BlockSpec(s, lambda i, nd=len(s): (0,) * nd) for s, _ in acc_outs]
    out_shape = [jax.ShapeDtypeStruct((T, n), dt) for n, dt in row_outs]
    out_shape += [jax.ShapeDtypeStruct(s, dt) for s, dt in acc_outs]
    return pl.pallas_call(
        kern, name=name, grid=(T // tm,), in_specs=in_specs, out_specs=out_specs, out_shape=out_shape,
        scratch_shapes=list(scratch), compiler_params=_cparams(("arbitrary",), vmem),
    )(*row_ins, *full_ins)


FFN_HALVES = (slice(0, FFN // 2), slice(FFN // 2, FFN))
ROW_CHUNK = 16
CHUNK_UNROLL = True


def _by_chunks(tm, fn):
    def step(c, carry):
        fn(pl.ds(pl.multiple_of(c * ROW_CHUNK, ROW_CHUNK), ROW_CHUNK))
        return carry

    lax.fori_loop(0, tm // ROW_CHUNK, step, 0, unroll=CHUNK_UNROLL)


def _fold8(x):
    return x[:8] + x[8:]


def _acc(ref, i, val):
    @pl.when(i == 0)
    def _():
        ref[...] = val

    @pl.when(i != 0)
    def _():
        ref[...] += val


def _in_proj_fwd(x, g_mix, w_in, T, tm):
    def body(i, x_ref, g_ref, w_ref, h_ref, *rest):
        outs, pj_s = rest[:-1], rest[-1]
        g = g_ref[...]

        def norm(rows):
            h_ref[rows, :] = (_rms(x_ref[rows, :])[0] * g).astype(MM)

        _by_chunks(tm, norm)
        for d in range(N_DEV):
            pj_s[d] = _dot_nt(h_ref[...], w_ref[d])

        def join_and_cut(rows):
            proj = jnp.concatenate([pj_s[d, rows, :] for d in range(N_DEV)], axis=1)
            for (s, n), o_ref in zip(COL_SECTIONS, outs):
                if n == QK_ROPE:
                    o_ref[rows, :] = jnp.concatenate(
                        [jnp.zeros((ROW_CHUNK, QK_NOPE), F32), proj[:, s:s + n],
                         jnp.zeros((ROW_CHUNK, HEAD_PAD - QK_DIM), F32)], axis=1)
                else:
                    o_ref[rows, :] = proj[:, s:s + n]

        _by_chunks(tm, join_and_cut)

    row_outs = [(D_MODEL, MM)] + [(n, F32) for _, n in SECTIONS]
    return _row_call("in_proj_fwd", body, T, tm, [x], [g_mix, w_in], row_outs, [], VMEM_LIMIT,
                     scratch=[pltpu.VMEM((N_DEV, tm, IN_BLOCK), F32)])


def _mla_heads_fwd(raw, g_pad, cos_t, sin_t, first):
    outs, saved = [], []
    for h in range(MLA_HEADS):
        xh, r = _rms(raw[:, h * HEAD_PAD:(h + 1) * HEAD_PAD], QK_DIM)
        y = xh * g_pad
        outs.append(y * cos_t + _rope_swap(y, first) * sin_t)
        saved.append((xh, r))
    return outs, saved


def _mla_raw_heads(cqn, ckvn, kr, wuq_ref, wukv_ref, tm):
    lane = lax.broadcasted_iota(jnp.int32, (tm, HEAD_PAD), 1)
    nope = lane < QK_NOPE
    one_lane = jnp.where(lane == V_DIM, 1.0, 0.0)
    qs, ks, vs = [], [], []
    for h in range(MLA_HEADS):
        qs.append(_dot_nt(cqn, wuq_ref[h]))
        kv = _dot(ckvn, wukv_ref[h])
        ks.append(jnp.where(nope, kv, kr))
        vs.append(jnp.where(nope, pltpu.roll(kv, V_DIM, 1), one_lane))
    return jnp.concatenate(qs, axis=1), jnp.concatenate(ks, axis=1), jnp.concatenate(vs, axis=1)


def _mla_prep_fwd(cq, ckv, kr, pos, g_qa, g_kva, g_qn, g_kn, w_uq, w_ukv, T, tm):
    def body(i, cq_ref, ckv_ref, kr_ref, pos_ref, gqa_ref, gkva_ref, gqn_ref, gkn_ref, wuq_ref, wukv_ref,
             q_ref, k_ref, v_ref):
        cos_t, sin_t, first = _rope_tables(pos_ref[...], tm)
        cqn = _rms(cq_ref[...])[0] * gqa_ref[...]
        ckvn = _rms(ckv_ref[...])[0] * gkva_ref[...]
        q_raw, k_raw, v = _mla_raw_heads(cqn, ckvn, kr_ref[...], wuq_ref, wukv_ref, tm)
        qs, _ = _mla_heads_fwd(q_raw, gqn_ref[...], cos_t, sin_t, first)
        ks, _ = _mla_heads_fwd(k_raw, gkn_ref[...], cos_t, sin_t, first)
        q_ref[...] = (jnp.concatenate(qs, axis=1) * ATT_SCALE).astype(MM)
        k_ref[...] = jnp.concatenate(ks, axis=1).astype(MM)
        v_ref[...] = v.astype(MM)

    w = MLA_HEADS * HEAD_PAD
    return _row_call("mla_prep_fwd", body, T, tm, [cq, ckv, kr, pos], [g_qa, g_kva, g_qn, g_kn, w_uq, w_ukv],
                     [(w, MM), (w, MM), (w, MM)], [])


def _causal_pairs(n, by_query):
    if by_query:
        pairs = [(q, k) for q in range(n) for k in range(q + 1)]
    else:
        pairs = [(q, k) for k in range(n) for q in range(k, n)]
    return np.array([p[0] for p in pairs], np.int32), np.array([p[1] for p in pairs], np.int32)


def _flash_fwd(qf, kf, vf, T, ag_blocks=()):
    tq = min(ATT_TILE, T)
    nq = T // tq

    qi_tab, ki_tab = _causal_pairs(nq, by_query=True)

    hp = ATT_HEADS

    n_ag = len(ag_blocks)
    n_heads, n_pairs = MLA_HEADS // hp, len(qi_tab)

    def body(qi_ref, ki_ref, q_ref, k_ref, v_ref, *rest):
        ag_in, (o_ref, lse_ref), rest = rest[:n_ag], rest[n_ag:n_ag + 2], rest[n_ag + 2:]
        ag_out, (m_s, acc_s), ag_sems = rest[:n_ag], rest[n_ag:n_ag + 2], rest[n_ag + 2:]
        t = pl.program_id(1)
        qi, ki = qi_ref[t], ki_ref[t]
        if n_ag:
            @pl.when((pl.program_id(0) == 0) & (t == 0))
            def _():
                _ag_start(ag_in, ag_out, ag_sems)

        @pl.when(ki == 0)
        def _():
            m_s[...] = jnp.full_like(m_s, NEG)
            acc_s[...] = jnp.zeros_like(acc_s)

        def step(masked):
            for hh in range(hp):
                hs = slice(hh * HEAD_PAD, (hh + 1) * HEAD_PAD)
                s_t = _dot_nt(k_ref[:, hs], q_ref[:, hs])
                if masked:
                    key = lax.broadcasted_iota(jnp.int32, (tq, tq), 0)
                    qry = lax.broadcasted_iota(jnp.int32, (tq, tq), 1)
                    s_t = jnp.where(key <= qry, s_t, NEG)
                m_old = m_s[hh]
                m_new = jnp.maximum(m_old, jnp.max(s_t, axis=0, keepdims=True))
                p_t = jnp.exp(s_t - m_new)
                acc_s[hh] = jnp.exp(m_old - m_new) * acc_s[hh] + _dot_tn(v_ref[:, hs], p_t)
                m_s[hh] = m_new

        @pl.when(ki < qi)
        def _():
            step(False)

        @pl.when(ki == qi)
        def _():
            step(True)
            real = lax.broadcasted_iota(jnp.int32, (HEAD_PAD, tq), 0) < V_DIM
            for hh in range(hp):
                hs = slice(hh * HEAD_PAD, (hh + 1) * HEAD_PAD)
                acc = acc_s[hh]
                l = acc[V_DIM:V_DIM + 1]
                o_ref[:, hs] = jnp.where(real, acc / l, 0.0).T
                lse_ref[:, hs] = jnp.broadcast_to(m_s[hh] + jnp.log(l), (HEAD_PAD, tq)).T

        if n_ag:
            @pl.when((pl.program_id(0) == n_heads - 1) & (t == n_pairs - 1))
            def _():
                _ag_finish(ag_in, ag_out, ag_sems)

    q_spec = pl.BlockSpec((tq, hp * HEAD_PAD), lambda h, t, qi_ref, ki_ref: (qi_ref[t], h))
    kv_spec = pl.BlockSpec((tq, hp * HEAD_PAD), lambda h, t, qi_ref, ki_ref: (ki_ref[t], h))
    any_spec = pl.BlockSpec(memory_space=pl.ANY)
    grid_spec = pltpu.PrefetchScalarGridSpec(
        num_scalar_prefetch=2, grid=(n_heads, n_pairs),
        in_specs=[q_spec, kv_spec, kv_spec] + [any_spec] * n_ag, out_specs=[q_spec, q_spec] + [any_spec] * n_ag,
        scratch_shapes=[pltpu.VMEM((hp, 1, tq), F32), pltpu.VMEM((hp, HEAD_PAD, tq), F32)]
        + (_ag_sems(n_ag) if n_ag else []))
    return pl.pallas_call(
        body, name="flash_fwd", grid_spec=grid_spec,
        out_shape=[jax.ShapeDtypeStruct((T, MLA_HEADS * HEAD_PAD), F32)] * 2 + _ag_out_shapes(ag_blocks),
        compiler_params=_cparams(("arbitrary", "arbitrary")),
    )(jnp.asarray(qi_tab), jnp.asarray(ki_tab), qf, kf, vf, *ag_blocks)


def _hg_gates(hf, lb):
    sg = _sigmoid(hf)
    f = lb + (1.0 - lb) * sg
    return sg, f, jnp.log(f), 1.0 - f


def _tri(n, lower):
    r = lax.broadcasted_iota(jnp.int32, (n, n), 0)
    c = lax.broadcasted_iota(jnp.int32, (n, n), 1)
    return jnp.where((c <= r) if lower else (c >= r), 1.0, 0.0).astype(F32)


def _hg_levels():
    C = HG_CHUNK
    t = lax.broadcasted_iota(jnp.int32, (C, C), 0)
    s = lax.broadcasted_iota(jnp.int32, (C, C), 1)
    levels = []
    for shift in range(C.bit_length() - 2, -1, -1):
        pair_t, pair_s = lax.shift_right_logical(t, shift + 1), lax.shift_right_logical(s, shift + 1)
        later_t = (lax.shift_right_logical(t, shift) & 1) == 1
        earlier_s = (lax.shift_right_logical(s, shift) & 1) == 0
        levels.append((1 << shift, (pair_t == pair_s) & later_t & earlier_s))
    return levels, t == s


def _hg_refs(b):
    C, n = b.shape
    row = lax.broadcasted_iota(jnp.int32, (C, n), 0)
    back1, back2, ahead1 = pltpu.roll(b, 1, 0), pltpu.roll(b, 2, 0), pltpu.roll(b, C - 1, 0)
    refs = []
    for half in (32, 16, 8, 4):
        refs.append(jnp.concatenate(
            [jnp.broadcast_to(b[lo + half - 1:lo + half], (2 * half, n)) for lo in range(0, C, 2 * half)], axis=0))
    in4 = row & 3
    refs.append(jnp.where(in4 == 0, ahead1, jnp.where(in4 == 1, b, jnp.where(in4 == 2, back1, back2))))
    refs.append(jnp.where((row & 1) == 1, back1, b))
    return refs


def _hg_intra(q, k, b, refs, levels, eye):
    a = jnp.where(eye, jnp.sum(q * k, axis=1, keepdims=True), 0.0)
    saved = []
    for r, (_, mask) in zip(refs, levels):
        e = jnp.exp(-jnp.abs(b - r))
        q_t, k_t = q * e, k * e
        a = a + jnp.where(mask, _dot_nt(q_t, k_t), 0.0)
        saved.append((q_t, k_t, e))
    return a, saved


def _hg_intra_bwd(d_a, q, k, saved, levels, eye):
    diag = jnp.sum(jnp.where(eye, d_a, 0.0), axis=1, keepdims=True)
    dq, dk = diag * k, diag * q
    for (q_t, k_t, e), (_, mask) in zip(saved, levels):
        da = jnp.where(mask, d_a, 0.0)
        dq = dq + _dot(da, k_t) * e
        dk = dk + _dot_tn(da, q_t) * e
    return dq, dk


def _hgrn_fwd(hq, hf, hi, lb, T):
    rb = min(HG_BLOCK, T)
    ncb = rb // HG_CHUNK

    def body(hq_ref, hf_ref, hi_ref, lb_ref, o_ref, s0_ref, st_ref):
        @pl.when(pl.program_id(0) == 0)
        def _():
            st_ref[...] = jnp.zeros_like(st_ref)

        tril = _tri(HG_CHUNK, True)
        levels, eye = _hg_levels()

        def chunk(c, carry):
            rows = pl.ds(pl.multiple_of(c * HG_CHUNK, HG_CHUNK), HG_CHUNK)
            _, _, logf, kk = _hg_gates(hf_ref[rows, :], lb_ref[...])
            b = _dot_hi(tril, logf)
            refs = _hg_refs(b)
            q_all, v_all = hq_ref[rows, :], hi_ref[rows, :]
            outs = []
            for h in range(HG_HEADS):
                ls = slice(h * HG_DIM, (h + 1) * HG_DIM)
                q, k, v, bh = q_all[:, ls], kk[:, ls], v_all[:, ls], b[:, ls]
                st = st_ref[h]
                s0_ref[c, h * HG_DIM:(h + 1) * HG_DIM, :] = st
                b_end = bh[HG_CHUNK - 1:HG_CHUNK]
                a, _ = _hg_intra(q, k, bh, [r[:, ls] for r in refs], levels, eye)
                outs.append(_dot_nt(q * jnp.exp(bh), st) + _dot(a, v))
                st_ref[h] = st * jnp.exp(b_end) + _dot_tn(v, k * jnp.exp(b_end - bh))
            o_ref[rows, :] = jnp.concatenate(outs, axis=1)
            return carry

        lax.fori_loop(0, ncb, chunk, 0, unroll=HG_UNROLL)

    row = pl.BlockSpec((rb, HG_W), lambda i: (i, 0))
    return pl.pallas_call(
        body, name="hgrn_fwd", grid=(T // rb,),
        in_specs=[row, row, row, pl.BlockSpec((1, HG_W), lambda i: (0, 0))],
        out_specs=[row, pl.BlockSpec((ncb, HG_W, HG_DIM), lambda i: (i, 0, 0))],
        out_shape=[jax.ShapeDtypeStruct((T, HG_W), F32), jax.ShapeDtypeStruct((T // HG_CHUNK, HG_W, HG_DIM), F32)],
        scratch_shapes=[pltpu.VMEM((HG_HEADS, HG_DIM, HG_DIM), F32)],
        compiler_params=_cparams(("arbitrary",)),
    )(hq, hf, hi, lb)


def _hgrn_bwd(hq, hf, hi, do, s0, lb, T, xchg=()):
    rb = min(HG_BLOCK, T)
    ncb = rb // HG_CHUNK
    nb = T // rb
    C = HG_CHUNK
    n_x = len(xchg)

    def body(hq_ref, hf_ref, hi_ref, do_ref, s0_ref, lb_ref, *rest):
        x_in, (dq_ref, df_ref, dv_ref, dlb_ref), rest = rest[:n_x], rest[n_x:n_x + 4], rest[n_x + 4:]
        x_out, dst_ref, x_sems = rest[:n_x], rest[n_x], rest[n_x + 1:]

        @pl.when(pl.program_id(0) == 0)
        def _():
            dst_ref[...] = jnp.zeros_like(dst_ref)
            dlb_ref[...] = jnp.zeros_like(dlb_ref)
            for cp in _xchips_copies(x_in, x_out, x_sems) if n_x else ():
                cp.start()

        tril, triu = _tri(C, True), _tri(C, False)
        row_cc = lax.broadcasted_iota(jnp.int32, (C, C), 0)
        col_cc = lax.broadcasted_iota(jnp.int32, (C, C), 1)
        last_row = lax.broadcasted_iota(jnp.int32, (C, HG_DIM), 0) == C - 1
        lb_v = lb_ref[...]
        levels, eye = _hg_levels()

        def chunk(cc, carry):
            c = ncb - 1 - cc
            rows = pl.ds(pl.multiple_of(c * C, C), C)
            hf_c = hf_ref[rows, :]
            sg, f, logf, kk = _hg_gates(hf_c, lb_v)
            b = _dot_hi(tril, logf)
            refs = _hg_refs(b)
            q_all, v_all, do_all = hq_ref[rows, :], hi_ref[rows, :], do_ref[rows, :]
            dq_o, dk_o, dv_o, db_o = [], [], [], []
            for h in range(HG_HEADS):
                ls = slice(h * HG_DIM, (h + 1) * HG_DIM)
                q, k, v, bh, d_o = q_all[:, ls], kk[:, ls], v_all[:, ls], b[:, ls], do_all[:, ls]
                st0 = s0_ref[c, h * HG_DIM:(h + 1) * HG_DIM, :]
                dst = dst_ref[h]
                b_end = bh[C - 1:C]
                e_b, e_end = jnp.exp(bh), jnp.exp(b_end)
                e_rem = jnp.exp(b_end - bh)
                qe, kd = q * e_b, k * e_rem
                st_end = st0 * e_end + _dot_tn(v, kd)
                a, saved = _hg_intra(q, k, bh, [r[:, ls] for r in refs], levels, eye)
                d_a = jnp.where(col_cc <= row_cc, _dot_nt(d_o, v), 0.0)
                dq_i, dk_i = _hg_intra_bwd(d_a, q, k, saved, levels, eye)
                dv = _dot_tn(a, d_o) + _dot_nt(kd, dst)
                dq = e_b * _dot(d_o, st0) + dq_i
                dk = e_rem * _dot(v, dst) + dk_i
                extra = jnp.sum(dst * st_end, axis=0, keepdims=True)
                db_o.append(q * dq - k * dk + jnp.where(last_row, extra, 0.0))
                dst_ref[h] = dst * e_end + _dot_tn(d_o, qe)
                dq_o.append(dq)
                dk_o.append(dk)
                dv_o.append(dv)
            dlogf = _dot_hi(triu, jnp.concatenate(db_o, axis=1))
            d_f = dlogf / f - jnp.concatenate(dk_o, axis=1)
            dq_ref[rows, :] = jnp.concatenate(dq_o, axis=1).astype(MM)
            dv_ref[rows, :] = jnp.concatenate(dv_o, axis=1).astype(MM)
            df_ref[rows, :] = (d_f * (1.0 - lb_v) * sg * (1.0 - sg)).astype(MM)
            dlb_ref[...] += jnp.sum(d_f * (1.0 - sg), axis=0, keepdims=True)
            return carry

        lax.fori_loop(0, ncb, chunk, 0, unroll=HG_UNROLL)

        if n_x:
            @pl.when(pl.program_id(0) == nb - 1)
            def _():
                for cp in _xchips_copies(x_in, x_out, x_sems):
                    cp.wait()

    row = pl.BlockSpec((rb, HG_W), lambda i: (nb - 1 - i, 0))
    one = pl.BlockSpec((1, HG_W), lambda i: (0, 0))
    any_spec = pl.BlockSpec(memory_space=pl.ANY)
    return pl.pallas_call(
        body, name="hgrn_bwd", grid=(nb,),
        in_specs=[row, row, row, row, pl.BlockSpec((ncb, HG_W, HG_DIM), lambda i: (nb - 1 - i, 0, 0)), one]
        + [any_spec] * n_x,
        out_specs=[row, row, row, one] + [any_spec] * n_x,
        out_shape=[jax.ShapeDtypeStruct((T, HG_W), MM)] * 3 + [jax.ShapeDtypeStruct((1, HG_W), F32)]
        + _xchips_out_shapes(xchg),
        scratch_shapes=[pltpu.VMEM((HG_HEADS, HG_DIM, HG_DIM), F32)] + (_xchips_sems(n_x) if n_x else []),
        compiler_params=_cparams(("arbitrary",)),
    )(hq, hf, hi, do, s0, lb, *xchg)


def _silu_parts(x):
    sg = _sigmoid(x)
    return x * sg, sg * (1.0 + x * (1.0 - sg))


def _merge_fwd(attn, o, hg, bg, x, g_out, w_bra, w_brb, w_out, T, tm):
    def body(i, attn_ref, o_ref, hg_ref, bg_ref, x_ref, g_ref, wa_ref, wb_ref, wo_ref,
             x1_ref, ya_ref, yb_ref, m_ref, rec_ref):
        g = g_ref[...]

        def recurrent_out(rows):
            for h in range(HG_HEADS):
                ls = slice(h * HG_DIM, (h + 1) * HG_DIM)
                rec_ref[rows, ls] = (_rms(o_ref[rows, ls])[0] * g * _silu_parts(hg_ref[rows, ls])[0]).astype(MM)

        _by_chunks(tm, recurrent_out)
        ya_ref[...] = _dot(attn_ref[...], wa_ref[...])
        yb_ref[...] = jnp.dot(rec_ref[...], wb_ref[...], preferred_element_type=F32)

        def gate(rows):
            m_ref[rows, :] = (_sigmoid(bg_ref[rows, :D_MODEL]) * ya_ref[rows, :]
                              + _sigmoid(bg_ref[rows, D_MODEL:]) * yb_ref[rows, :]).astype(MM)

        _by_chunks(tm, gate)
        x1_ref[...] = x_ref[...] + jnp.dot(m_ref[...], wo_ref[...], preferred_element_type=F32)

    return _row_call("merge_fwd", body, T, tm, [attn, o, hg, bg, x], [g_out, w_bra, w_brb, w_out],
                     [(D_MODEL, F32), (D_MODEL, F32), (D_MODEL, F32), (D_MODEL, MM), (HG_W, MM)], [], VMEM_LIMIT)


def _ffn_fwd(x1, g_ffn, w_g, w_u, w_d, T, tm):
    def body(i, x1_ref, g_ref, wg_ref, wu_ref, wd_ref, x2_ref, gt_ref, up_ref, h2_ref, a_s):
        g = g_ref[...]

        def norm(rows):
            h2_ref[rows, :] = (_rms(x1_ref[rows, :])[0] * g).astype(MM)

        _by_chunks(tm, norm)
        gt_ref[...] = _dot_nt(h2_ref[...], wg_ref[...])
        up_ref[...] = _dot_nt(h2_ref[...], wu_ref[...])

        def act(rows):
            for cs in FFN_HALVES:
                a_s[rows, cs] = (_silu_parts(gt_ref[rows, cs])[0] * up_ref[rows, cs]).astype(MM)

        _by_chunks(tm, act)
        x2_ref[...] = x1_ref[...] + jnp.dot(a_s[...], wd_ref[...], preferred_element_type=F32)

    return _row_call("ffn_fwd", body, T, tm, [x1], [g_ffn, w_g, w_u, w_d],
                     [(D_MODEL, F32), (FFN, F32), (FFN, F32), (D_MODEL, MM)], [], VMEM_LIMIT,
                     scratch=[pltpu.VMEM((tm, FFN), MM)])


def _ple_loss(x2, p, tgt, g_pg, g_post, w_pg, w_pp, T, tm):
    def body(i, x2_ref, p_ref, t_ref, gpg_ref, gpo_ref, wpg_ref, wpp_ref,
             dx2_ref, loss_ref, dgpo_ref, dgpg_ref, dwpg_ref, dwpp_ref, u_s, n3_s, z_s, dz_s, du_s, dy_s, dn3_s):
        @pl.when(i == 0)
        def _():
            for ref in (loss_ref, dgpo_ref, dgpg_ref, dwpg_ref, dwpp_ref):
                ref[...] = jnp.zeros_like(ref)

        gpg, gpo = gpg_ref[...], gpo_ref[...]
        p_mm = p_ref[...].astype(MM)
        for d in range(N_DEV):
            u_s[:, d * HEAD_PAD:(d + 1) * HEAD_PAD] = jnp.dot(p_mm, wpp_ref[d], preferred_element_type=F32)

        def gate_input(rows):
            n3_s[rows, :] = (_rms(x2_ref[rows, :])[0] * gpg).astype(MM)

        _by_chunks(tm, gate_input)
        z_s[...] = jnp.dot(n3_s[...], wpg_ref[...], preferred_element_type=F32)

        def loss_and_back(rows):
            uh, ru = _rms(u_s[rows, :])
            e = uh * gpo
            gate = _sigmoid(z_s[rows, :])
            diff = x2_ref[rows, :] + gate * e - t_ref[rows, :]
            dy = diff * (1.0 / D_MODEL)
            de = dy * gate
            dz_s[rows, :] = (dy * e * gate * (1.0 - gate)).astype(MM)
            du_s[rows, :] = _rms_bwd(de * gpo, uh, ru).astype(MM)
            dy_s[rows, :] = dy
            loss_ref[...] += _fold8(diff * diff) * (0.5 / D_MODEL)
            dgpo_ref[...] += _fold8(de * uh)

        _by_chunks(tm, loss_and_back)
        dn3_s[...] = _dot_nt(dz_s[...], wpg_ref[...])

        def gate_norm_back(rows):
            x2h, r3 = _rms(x2_ref[rows, :])
            dn3 = dn3_s[rows, :]
            dx2_ref[rows, :] = dy_s[rows, :] + _rms_bwd(dn3 * gpg, x2h, r3)
            dgpg_ref[...] += _fold8(dn3 * x2h)

        _by_chunks(tm, gate_norm_back)
        dwpg_ref[...] += _dot_tn(n3_s[...], dz_s[...])
        for d in range(N_DEV):
            dwpp_ref[d] += _dot_tn(p_mm, du_s[:, d * HEAD_PAD:(d + 1) * HEAD_PAD])

    vec = ((8, D_MODEL), F32)
    wide = lambda dt: pltpu.VMEM((tm, D_MODEL), dt)
    return _row_call("ple_loss", body, T, tm, [x2, p, tgt], [g_pg, g_post, w_pg, w_pp], [(D_MODEL, F32)],
                     [vec, vec, vec, ((D_MODEL, D_MODEL), F32), ((N_DEV, PLE, HEAD_PAD), F32)], VMEM_LIMIT,
                     scratch=[wide(F32), wide(MM), wide(F32), wide(MM), wide(MM), wide(F32), wide(F32)])


def _ffn_bwd(dx2, x1, gt, up, g_ffn, w_g, w_u, w_d, T, tm):
    def body(i, dx2_ref, x1_ref, gt_ref, up_ref, g_ref, wg_ref, wu_ref, wd_ref,
             dx1_ref, a_ref, dgt_ref, dup_ref, dg_ref, da_s, dh2_s):
        @pl.when(i == 0)
        def _():
            dg_ref[...] = jnp.zeros_like(dg_ref)

        g = g_ref[...]
        da_s[...] = _dot_nt(dx2_ref[...], wd_ref[...])

        def act_back(rows):
            for cs in FFN_HALVES:
                up, da = up_ref[rows, cs], da_s[rows, cs]
                silu, dsilu = _silu_parts(gt_ref[rows, cs])
                dgt_ref[rows, cs] = (da * up * dsilu).astype(MM)
                dup_ref[rows, cs] = (da * silu).astype(MM)
                a_ref[rows, cs] = (silu * up).astype(MM)

        _by_chunks(tm, act_back)
        dh2_s[...] = (jnp.dot(dgt_ref[...], wg_ref[...], preferred_element_type=F32)
                      + jnp.dot(dup_ref[...], wu_ref[...], preferred_element_type=F32))

        def norm_back(rows):
            x1h, r = _rms(x1_ref[rows, :])
            dh2 = dh2_s[rows, :]
            dx1_ref[rows, :] = dx2_ref[rows, :] + _rms_bwd(dh2 * g, x1h, r)
            dg_ref[...] += _fold8(dh2 * x1h)

        _by_chunks(tm, norm_back)

    return _row_call("ffn_bwd", body, T, tm, [dx2, x1, gt, up], [g_ffn, w_g, w_u, w_d],
                     [(D_MODEL, F32), (FFN, MM), (FFN, MM), (FFN, MM)], [((8, D_MODEL), F32)], VMEM_LIMIT,
                     scratch=[pltpu.VMEM((tm, FFN), F32), pltpu.VMEM((tm, D_MODEL), F32)])


def _merge_bwd(dx1, ya, yb, bg, o, hg, attn, m, rec, g_out, w_bra, w_brb, w_out, T, tm):
    def body(i, dx1_ref, ya_ref, yb_ref, bg_ref, o_ref, hg_ref, attn_ref, m_ref, rec_ref, g_ref, wa_ref, wb_ref, wo_ref,
             dattn_ref, do_ref, dhg_ref, dbg_ref, dg_ref, dwo_ref, dwa_ref, dwb_ref, dm_s, dya_s, dyb_s, drec_s):
        @pl.when(i == 0)
        def _():
            for ref in (dg_ref, dwo_ref, dwa_ref, dwb_ref):
                ref[...] = jnp.zeros_like(ref)

        g = g_ref[...]
        dx1 = dx1_ref[...].astype(MM)
        dm_s[...] = _dot_nt(dx1, wo_ref[...])

        def gate_back(rows):
            dm = dm_s[rows, :]
            ga, gb = _sigmoid(bg_ref[rows, :D_MODEL]), _sigmoid(bg_ref[rows, D_MODEL:])
            dya_s[rows, :] = (dm * ga).astype(MM)
            dyb_s[rows, :] = (dm * gb).astype(MM)
            dbg_ref[rows, :D_MODEL] = (dm * ya_ref[rows, :] * ga * (1.0 - ga)).astype(MM)
            dbg_ref[rows, D_MODEL:] = (dm * yb_ref[rows, :] * gb * (1.0 - gb)).astype(MM)

        _by_chunks(tm, gate_back)
        dwo_ref[...] += _dot_tn(m_ref[...], dx1)
        attn_mm = attn_ref[...].astype(MM)
        for d in range(N_DEV):
            ds = slice(d * HEAD_PAD, (d + 1) * HEAD_PAD)
            dwa_ref[d] += _dot_tn(attn_mm, dya_s[:, ds])
            dwb_ref[d] += _dot_tn(rec_ref[...], dyb_s[:, ds])
        dattn_ref[...] = _dot_nt(dya_s[...], wa_ref[...])
        drec_s[...] = _dot_nt(dyb_s[...], wb_ref[...])

        def recurrent_out_back(rows):
            for h in range(HG_HEADS):
                ls = slice(h * HG_DIM, (h + 1) * HG_DIM)
                oh, r = _rms(o_ref[rows, ls])
                silu, dsilu = _silu_parts(hg_ref[rows, ls])
                dr = drec_s[rows, ls]
                dhg_ref[rows, ls] = (dr * oh * g * dsilu).astype(MM)
                don = dr * silu
                dg_ref[...] += _fold8(don * oh)
                do_ref[rows, ls] = _rms_bwd(don * g, oh, r)

        _by_chunks(tm, recurrent_out_back)

    wide = lambda n, dt: pltpu.VMEM((tm, n), dt)
    return _row_call("merge_bwd", body, T, tm, [dx1, ya, yb, bg, o, hg, attn, m, rec], [g_out, w_bra, w_brb, w_out],
                     [(D_MODEL, F32), (HG_W, F32), (HG_W, MM), (2 * D_MODEL, MM)],
                     [((8, HG_DIM), F32), ((D_MODEL, D_MODEL), F32), ((N_DEV, MLA_HEADS * HEAD_PAD, HEAD_PAD), F32),
                      ((N_DEV, HG_W, HEAD_PAD), F32)], VMEM_LIMIT,
                     scratch=[wide(D_MODEL, F32), wide(D_MODEL, MM), wide(D_MODEL, MM), wide(HG_W, F32)])


def _flash_bwd(qf, kf, vf, o, do, lse, T, xchg=()):
    tq = min(ATT_TILE, T)
    nq = T // tq

    qi_tab, ki_tab = _causal_pairs(nq, by_query=False)

    n_x = len(xchg)
    hp = ATT_HEADS
    n_heads, n_pairs = MLA_HEADS // hp, len(qi_tab)

    def body(qi_ref, ki_ref, q_ref, k_ref, v_ref, o_ref, do_ref, lse_ref, *rest):
        x_in, (dq_ref, dk_ref, dv_ref), rest = rest[:n_x], rest[n_x:n_x + 3], rest[n_x + 3:]
        x_out, x_sems = rest[:n_x], rest[n_x:]
        t = pl.program_id(1)
        qi, ki = qi_ref[t], ki_ref[t]
        if n_x:
            @pl.when((pl.program_id(0) == 0) & (t == 0))
            def _():
                for cp in _xchips_copies(x_in, x_out, x_sems):
                    cp.start()

        @pl.when(t == 0)
        def _():
            dq_ref[...] = jnp.zeros_like(dq_ref)

        def step(first):
            rows = pl.ds(pl.multiple_of(qi * tq, tq), tq)
            for hh in range(hp):
                hs = slice(hh * HEAD_PAD, (hh + 1) * HEAD_PAD)
                q, k, d_o = q_ref[:, hs], k_ref[:, hs], do_ref[:, hs]
                s = _dot_nt(q, k)
                if first:
                    row = lax.broadcasted_iota(jnp.int32, (tq, tq), 0)
                    col = lax.broadcasted_iota(jnp.int32, (tq, tq), 1)
                    s = jnp.where(col <= row, s, NEG)
                p = jnp.exp(s - lse_ref[:, hh * HEAD_PAD:hh * HEAD_PAD + 1])
                delta = jnp.sum(d_o * o_ref[:, hs], axis=1, keepdims=True)
                ds = p * (_dot_nt(d_o, v_ref[:, hs]) - delta)
                dq_ref[rows, hs] += _dot(ds, k)
                if first:
                    dv_ref[:, hs] = _dot_tn(p, d_o)
                    dk_ref[:, hs] = _dot_tn(ds, q)
                else:
                    dv_ref[:, hs] += _dot_tn(p, d_o)
                    dk_ref[:, hs] += _dot_tn(ds, q)

        @pl.when(qi == ki)
        def _():
            step(True)

        @pl.when(qi > ki)
        def _():
            step(False)

        if n_x:
            @pl.when((pl.program_id(0) == n_heads - 1) & (t == n_pairs - 1))
            def _():
                for cp in _xchips_copies(x_in, x_out, x_sems):
                    cp.wait()

    q_spec = pl.BlockSpec((tq, hp * HEAD_PAD), lambda h, t, qi_ref, ki_ref: (qi_ref[t], h))
    kv_spec = pl.BlockSpec((tq, hp * HEAD_PAD), lambda h, t, qi_ref, ki_ref: (ki_ref[t], h))
    any_spec = pl.BlockSpec(memory_space=pl.ANY)
    w = MLA_HEADS * HEAD_PAD
    grid_spec = pltpu.PrefetchScalarGridSpec(
        num_scalar_prefetch=2, grid=(n_heads, n_pairs),
        in_specs=[q_spec, kv_spec, kv_spec, q_spec, q_spec, q_spec] + [any_spec] * n_x,
        out_specs=[pl.BlockSpec((T, hp * HEAD_PAD), lambda h, t, qi_ref, ki_ref: (0, h)), kv_spec, kv_spec]
        + [any_spec] * n_x,
        scratch_shapes=_xchips_sems(n_x) if n_x else [])
    return pl.pallas_call(
        body, name="flash_bwd", grid_spec=grid_spec,
        out_shape=[jax.ShapeDtypeStruct((T, w), F32)] * 3 + _xchips_out_shapes(xchg),
        compiler_params=_cparams(("arbitrary", "arbitrary")),
    )(jnp.asarray(qi_tab), jnp.asarray(ki_tab), qf, kf, vf, o, do, lse, *xchg)


def _mla_heads_bwd(d_out, saved, g_pad, cos_t, sin_t, first):
    d_raw, dg = [], jnp.zeros((1, HEAD_PAD), F32)
    for h in range(MLA_HEADS):
        xh, r = saved[h]
        dy = d_out[:, h * HEAD_PAD:(h + 1) * HEAD_PAD]
        dn = dy * cos_t + _rope_swap(dy * sin_t, first)
        dg = dg + jnp.sum(dn * xh, axis=0, keepdims=True)
        d_raw.append(_rms_bwd(dn * g_pad, xh, r, QK_DIM))
    return d_raw, dg


def _mla_prep_bwd(cq, ckv, kr, pos, dqf, dkf, dvf, g_qa, g_kva, g_qn, g_kn, w_uq, w_ukv, T, tm):
    def body(i, cq_ref, ckv_ref, kr_ref, pos_ref, dq_ref, dk_ref, dv_ref,
             gqa_ref, gkva_ref, gqn_ref, gkn_ref, wuq_ref, wukv_ref,
             dcq_ref, dckv_ref, dkr_ref, dgqa_ref, dgkva_ref, dgqn_ref, dgkn_ref, dwuq_ref, dwukv_ref):
        cos_t, sin_t, first = _rope_tables(pos_ref[...], tm)
        cqh, rq = _rms(cq_ref[...])
        ckvh, rkv = _rms(ckv_ref[...])
        cqn, ckvn = cqh * gqa_ref[...], ckvh * gkva_ref[...]
        q_raw, k_raw, _ = _mla_raw_heads(cqn, ckvn, kr_ref[...], wuq_ref, wukv_ref, tm)
        _, q_saved = _mla_heads_fwd(q_raw, gqn_ref[...], cos_t, sin_t, first)
        _, k_saved = _mla_heads_fwd(k_raw, gkn_ref[...], cos_t, sin_t, first)
        dq_heads, dgqn = _mla_heads_bwd(dq_ref[...] * ATT_SCALE, q_saved, gqn_ref[...], cos_t, sin_t, first)
        dk_heads, dgkn = _mla_heads_bwd(dk_ref[...], k_saved, gkn_ref[...], cos_t, sin_t, first)
        lane = lax.broadcasted_iota(jnp.int32, (tm, HEAD_PAD), 1)
        nope = lane < QK_NOPE
        dcqn = jnp.zeros((tm, Q_RANK), F32)
        dckvn = jnp.zeros((tm, KV_RANK), F32)
        dkr = jnp.zeros((tm, HEAD_PAD), F32)
        cqn_mm, ckvn_mm = cqn.astype(MM), ckvn.astype(MM)
        for h in range(MLA_HEADS):
            hs = slice(h * HEAD_PAD, (h + 1) * HEAD_PAD)
            dq_h = dq_heads[h].astype(MM)
            dkv_h = jnp.where(nope, dk_heads[h], pltpu.roll(dv_ref[:, hs], V_DIM, 1)).astype(MM)
            _acc(dwuq_ref.at[h], i, _dot_tn(dq_h, cqn_mm))
            _acc(dwukv_ref.at[h], i, _dot_tn(ckvn_mm, dkv_h))
            dcqn = dcqn + jnp.dot(dq_h, wuq_ref[h], preferred_element_type=F32)
            dckvn = dckvn + lax.dot_general(dkv_h, wukv_ref[h], (((1,), (1,)), ((), ())), preferred_element_type=F32)
            dkr = dkr + dk_heads[h]
        dkr_ref[...] = jnp.where((lane >= QK_NOPE) & (lane < QK_DIM), dkr, 0.0).astype(MM)
        dcq_ref[...] = _rms_bwd(dcqn * gqa_ref[...], cqh, rq).astype(MM)
        dckv_ref[...] = _rms_bwd(dckvn * gkva_ref[...], ckvh, rkv).astype(MM)
        _acc(dgqa_ref, i, jnp.sum(dcqn * cqh, axis=0, keepdims=True))
        _acc(dgkva_ref, i, jnp.sum(dckvn * ckvh, axis=0, keepdims=True))
        _acc(dgqn_ref, i, dgqn)
        _acc(dgkn_ref, i, dgkn)

    return _row_call(
        "mla_prep_bwd", body, T, tm, [cq, ckv, kr, pos, dqf, dkf, dvf], [g_qa, g_kva, g_qn, g_kn, w_uq, w_ukv],
        [(Q_RANK, MM), (KV_RANK, MM), (HEAD_PAD, MM)],
        [((1, Q_RANK), F32), ((1, KV_RANK), F32), ((1, HEAD_PAD), F32), ((1, HEAD_PAD), F32),
         ((MLA_HEADS, HEAD_PAD, Q_RANK), F32), ((MLA_HEADS, KV_RANK, HEAD_PAD), F32)], VMEM_LIMIT)


def _in_proj_bwd(x, dx1, dsecs, g_mix, w_in, T, tm):
    def body(i, x_ref, dx1_ref, *rest):
        d_refs, (g_ref, w_ref, dx_ref, dp_ref, dg_ref, dh_s) = rest[:len(SECTIONS)], rest[len(SECTIONS):]

        @pl.when(i == 0)
        def _():
            dg_ref[...] = jnp.zeros_like(dg_ref)

        g = g_ref[...]

        def join_and_cut(rows):
            pieces = [(d_ref[rows, QK_NOPE:QK_DIM] if n == QK_ROPE else d_ref[rows, :]).astype(F32)
                      for (_, n), d_ref in zip(COL_SECTIONS, d_refs)]
            dproj = jnp.concatenate(pieces, axis=1)
            for d in range(N_DEV):
                dp_ref[d, rows, :] = dproj[:, d * IN_BLOCK:(d + 1) * IN_BLOCK].astype(MM)

        _by_chunks(tm, join_and_cut)
        dh = jnp.dot(dp_ref[0], w_ref[0], preferred_element_type=F32)
        for d in range(1, N_DEV):
            dh = dh + jnp.dot(dp_ref[d], w_ref[d], preferred_element_type=F32)
        dh_s[...] = dh

        def norm_back(rows):
            xh, r = _rms(x_ref[rows, :])
            dh_c = dh_s[rows, :]
            dx_ref[rows, :] = dx1_ref[rows, :] + _rms_bwd(dh_c * g, xh, r)
            dg_ref[...] += _fold8(dh_c * xh)

        _by_chunks(tm, norm_back)

    in_specs = [pl.BlockSpec((tm, a.shape[1]), lambda i: (i, 0)) for a in [x, dx1, *dsecs]]
    in_specs += [pl.BlockSpec(g_mix.shape, lambda i: (0, 0)),
                 pl.BlockSpec(w_in.shape, lambda i: (0, 0, 0), pipeline_mode=pl.Buffered(1))]

    def kern(*refs):
        body(pl.program_id(0), *refs)

    return pl.pallas_call(
        kern, name="in_proj_bwd", grid=(T // tm,), in_specs=in_specs,
        out_specs=[pl.BlockSpec((tm, D_MODEL), lambda i: (i, 0)),
                   pl.BlockSpec((N_DEV, tm, IN_BLOCK), lambda i: (0, i, 0)),
                   pl.BlockSpec((8, D_MODEL), lambda i: (0, 0))],
        out_shape=[jax.ShapeDtypeStruct((T, D_MODEL), F32), jax.ShapeDtypeStruct((N_DEV, T, IN_BLOCK), MM),
                   jax.ShapeDtypeStruct((8, D_MODEL), F32)],
        scratch_shapes=[pltpu.VMEM((tm, D_MODEL), F32)],
        compiler_params=_cparams(("arbitrary",), VMEM_LIMIT),
    )(x, dx1, *dsecs, g_mix, w_in)


def _pick_block(n, cap):
    best = None
    for cand in range(128, min(n, cap) + 1, 128):
        if n % cand == 0:
            best = cand
    return n if best is None else best


def _pick_rows(n, cap):
    best = n
    for cand in range(8, min(n, cap) + 1, 8):
        if n % cand == 0:
            best = cand
    return best


def _matmul_tn(name, a, b):
    T, M = a.shape
    N = b.shape[1]
    bm, bk = _pick_block(M, 1408), min(512, T)
    bn = _pick_block(N, 2560)

    def body(a_ref, b_ref, c_ref):
        @pl.when(pl.program_id(2) == 0)
        def _():
            c_ref[...] = jnp.zeros_like(c_ref)

        c_ref[...] += _dot_tn(a_ref[...], b_ref[...])

    return pl.pallas_call(
        body, name=name, grid=(M // bm, N // bn, T // bk),
        in_specs=[pl.BlockSpec((bk, bm), lambda i, j, k: (k, i)), pl.BlockSpec((bk, bn), lambda i, j, k: (k, j))],
        out_specs=pl.BlockSpec((bm, bn), lambda i, j, k: (i, j)), out_shape=jax.ShapeDtypeStruct((M, N), F32),
        compiler_params=_cparams(("parallel", "parallel", "arbitrary"), VMEM_LIMIT),
    )(a, b)


def _matmul_tn_blocks(name, a, b):
    T, M = a.shape
    nd, _, c = b.shape
    bm, bk = _pick_block(M, 512), min(512, T)

    def body(a_ref, b_ref, c_ref):
        @pl.when(pl.program_id(1) == 0)
        def _():
            c_ref[...] = jnp.zeros_like(c_ref)

        a_blk = a_ref[...].astype(MM)
        for d in range(nd):
            c_ref[d] += _dot_tn(b_ref[d], a_blk)

    return pl.pallas_call(
        body, name=name, grid=(M // bm, T // bk),
        in_specs=[pl.BlockSpec((bk, bm), lambda i, k: (k, i)), pl.BlockSpec((nd, bk, c), lambda i, k: (0, k, 0))],
        out_specs=pl.BlockSpec((nd, c, bm), lambda i, k: (0, 0, i)),
        out_shape=jax.ShapeDtypeStruct((nd, c, M), F32),
        compiler_params=_cparams(("parallel", "arbitrary"), VMEM_LIMIT),
    )(a, b)


def _pad_gain(g, n):
    return jnp.pad(g.reshape(1, -1), ((0, 0), (0, n - g.shape[-1])))


GROUP_A = ("w_ffn_gate", "w_ffn_up", "w_ffn_down", "w_ple_gate", "w_ple_proj")
GROUP_B = ("w_branch", "w_out")
GROUP_C = ("w_in", "w_uq", "w_ukv")
EARLY = GROUP_C
LATE = GROUP_B + GROUP_A
TRANSPOSED = ("w_in", "w_uq", "w_ffn_gate", "w_ffn_up")


def _local_step(x, p, pos, tgt, small, big, late_blocks=None, core=None):
    T = x.shape[0]
    tm = min(ROW_TILE, T)
    w_in = big["w_in"]
    w_uq = jnp.pad(big["w_uq"], ((0, 0), (0, HEAD_PAD - QK_DIM), (0, 0)))
    w_ukv = big["w_ukv"]

    g_mix, g_qa, g_kva = small["mix_norm_g"], small["q_a_norm_g"], small["kv_a_norm_g"]
    g_qn, g_kn = _pad_gain(small["q_norm_g"], HEAD_PAD), _pad_gain(small["k_norm_g"], HEAD_PAD)
    g_out, g_ffn = small["hg_out_norm_g"], small["ffn_norm_g"]
    g_pg, g_post = small["ple_gate_norm_g"], small["ple_post_norm_g"]
    logits = small["hg_lb_logits"]
    lb = _lower_bound(logits)

    h, cq, ckv, kr, hq, hf, hi, hg, bg = _in_proj_fwd(x, g_mix, w_in, T, tm)
    qf, kf, vf = _mla_prep_fwd(cq, ckv, kr, pos, g_qa, g_kva, g_qn, g_kn, w_uq, w_ukv, T, tm)
    if late_blocks is None:
        attn, lse = _flash_fwd(qf, kf, vf, T)
    else:
        attn, lse, *late = _flash_fwd(qf, kf, vf, T, ag_blocks=[late_blocks[n] for n in LATE])
        big = {**big, **dict(zip(LATE, late))}
    w_branch = jnp.moveaxis(big["w_branch"].reshape(N_DEV, 2, HG_W, HEAD_PAD), 0, 2).reshape(2, HG_W, D_MODEL)
    w_bra = jnp.pad(w_branch[0].reshape(MLA_HEADS, V_DIM, D_MODEL),
                    ((0, 0), (0, HEAD_PAD - V_DIM), (0, 0))).reshape(MLA_HEADS * HEAD_PAD, D_MODEL)
    w_brb = w_branch[1]
    w_out = big["w_out"].reshape(D_MODEL, D_MODEL)
    w_g, w_u = big["w_ffn_gate"].reshape(FFN, D_MODEL), big["w_ffn_up"].reshape(FFN, D_MODEL)
    w_d = big["w_ffn_down"].reshape(FFN, D_MODEL)
    w_pg, w_pp = big["w_ple_gate"].reshape(D_MODEL, D_MODEL), big["w_ple_proj"]
    o, s0 = _hgrn_fwd(hq, hf, hi, lb, T)
    x1, ya, yb, m, rec = _merge_fwd(attn, o, hg, bg, x, g_out, w_bra, w_brb, w_out, T, tm)
    x2, gt, up, h2 = _ffn_fwd(x1, g_ffn, w_g, w_u, w_d, T, tm)
    dx2, loss_p, dg_post, dg_pg, d_pg, d_pp = _ple_loss(x2, p, tgt, g_pg, g_post, w_pg, w_pp, T, tm)
    dg_post, dg_pg = (jnp.sum(t, axis=0, keepdims=True) for t in (dg_post, dg_pg))

    grads, sibs, gots = {}, {}, {}

    def reduce_start(tag, names):
        if core is None:
            return ()
        got = _exchange_sibling("rs_sibling_" + tag, [grads[n] for n in names])
        sibs.update(zip(names, got))
        return _chip_partials("rs_partial_" + tag, [grads[n] for n in names], got, core)

    dx1, a, dgt, dup, dg_ffn = _ffn_bwd(dx2, x1, gt, up, g_ffn, w_g, w_u, w_d, T, tm)
    dg_ffn = jnp.sum(dg_ffn, axis=0, keepdims=True)
    grads["w_ffn_gate"] = _matmul_tn("dw_gate", dgt, h2).reshape(N_DEV, -1, D_MODEL)
    grads["w_ffn_up"] = _matmul_tn("dw_up", dup, h2).reshape(N_DEV, -1, D_MODEL)
    grads["w_ffn_down"] = _matmul_tn("dw_down", a, dx2).reshape(N_DEV, -1, D_MODEL)
    grads["w_ple_gate"] = d_pg.reshape(N_DEV, -1, D_MODEL)
    grads["w_ple_proj"] = d_pp
    parts_a = reduce_start("a", GROUP_A)

    dattn, do, dhg, dbg, dg_out, d_out, d_bra, d_brb = _merge_bwd(
        dx1, ya, yb, bg, o, hg, attn, m, rec, g_out, w_bra, w_brb, w_out, T, tm)
    dg_out = jnp.sum(dg_out, axis=0, keepdims=True)
    d_bra = d_bra.reshape(N_DEV, MLA_HEADS, HEAD_PAD, HEAD_PAD)[:, :, :V_DIM].reshape(N_DEV, HG_W, HEAD_PAD)
    grads["w_branch"] = jnp.concatenate([d_bra, d_brb], axis=1)
    grads["w_out"] = d_out.reshape(N_DEV, -1, D_MODEL)
    parts_b = reduce_start("b", GROUP_B)

    dhq, dhf, dhi, dlb, *got_a = _hgrn_bwd(hq, hf, hi, do, s0, lb, T, xchg=parts_a)
    dqf, dkf, dvf, *got_b = _flash_bwd(qf, kf, vf, attn, dattn, lse, T, xchg=parts_b)
    (dcq, dckv, dkr, dg_qa, dg_kva, dg_qn, dg_kn, d_uq, d_ukv) = _mla_prep_bwd(
        cq, ckv, kr, pos, dqf, dkf, dvf, g_qa, g_kva, g_qn, g_kn, w_uq, w_ukv, T, tm)
    grad_x, dproj, dg_mix = _in_proj_bwd(x, dx1, [dcq, dckv, dkr, dhq, dhf, dhi, dhg, dbg], g_mix, w_in, T, tm)
    dg_mix = jnp.sum(dg_mix, axis=0, keepdims=True)
    grads["w_in"] = _matmul_tn_blocks("dw_in", h, dproj)
    grads["w_uq"] = d_uq[:, :QK_DIM]
    grads["w_ukv"] = d_ukv
    parts_c = reduce_start("c", GROUP_C)
    if core is not None:
        gots.update(zip(GROUP_A, got_a))
        gots.update(zip(GROUP_B, got_b))
        gots.update(zip(GROUP_C, _exchange_chips(parts_c)))

    dl0 = dlb * lb * (1.0 - lb)
    small_g = {
        "mix_norm_g": dg_mix, "q_a_norm_g": dg_qa, "kv_a_norm_g": dg_kva,
        "q_norm_g": dg_qn[:, :QK_DIM], "k_norm_g": dg_kn[:, :QK_DIM],
        "hg_lb_logits": jnp.concatenate([dl0, -dl0], axis=0), "hg_out_norm_g": dg_out,
        "ffn_norm_g": dg_ffn, "ple_gate_norm_g": dg_pg, "ple_post_norm_g": dg_post,
    }
    return loss_p, grad_x, small_g, grads, sibs, gots


def _lower_bound(logits):
    def body(l_ref, lb_ref):
        l = l_ref[...]
        mx = jnp.max(l, axis=0, keepdims=True)
        e = jnp.exp(l - mx)
        lb_ref[...] = e[0:1] / jnp.sum(e, axis=0, keepdims=True)

    return pl.pallas_call(body, name="lower_bound", out_shape=jax.ShapeDtypeStruct((1, HG_W), F32))(logits)


def _my_place():
    return lax.axis_index("x"), lax.axis_index("y"), lax.axis_index("c")


def _all_gather(name, blocks):
    n = len(blocks)

    def body(*refs):
        x_refs, out_refs, sems = refs[:n], refs[n:2 * n], refs[2 * n:]
        _ag_start(x_refs, out_refs, sems)
        _ag_finish(x_refs, out_refs, sems)

    any_spec = pl.BlockSpec(memory_space=pl.ANY)
    return pl.pallas_call(
        body, name=name, out_shape=_ag_out_shapes(blocks),
        in_specs=[any_spec] * n, out_specs=[any_spec] * n, scratch_shapes=_ag_sems(n),
    )(*blocks)


def _ag_out_shapes(blocks):
    return [jax.ShapeDtypeStruct((N_DEV,) + b.shape, b.dtype) for b in blocks]


def _ag_sems(n):
    return [pltpu.SemaphoreType.DMA((7 * n,)), pltpu.SemaphoreType.DMA((7 * n,)), pltpu.SemaphoreType.DMA((n,))]


def _ag_parts(x_refs, out_refs, sems):
    send_sems, recv_sems, local_sems = sems
    x, y, c = _my_place()
    me, sibling = (x, y, c), (x, y, 1 - c)
    chips = [(1 - x, y), (x, 1 - y), (1 - x, 1 - y)]
    n = len(x_refs)

    def copy(a, k, block, to, own=False):
        px, py, pc = block
        dst = out_refs[a].at[4 * px + 2 * py + pc]
        return pltpu.make_async_remote_copy(
            src_ref=x_refs[a] if own else dst, dst_ref=dst, send_sem=send_sems.at[7 * a + k],
            recv_sem=recv_sems.at[7 * a + k], device_id=to, device_id_type=MESH_ID)

    mine = [pltpu.make_async_copy(x_refs[a], out_refs[a].at[4 * x + 2 * y + c], local_sems.at[a]) for a in range(n)]
    first = []
    for a in range(n):
        first.append(copy(a, 0, me, sibling, own=True))
        first += [copy(a, 1 + j, me, (*chip, c), own=True) for j, chip in enumerate(chips)]
    return copy, mine, first, me, sibling, chips, c, n


def _ag_start(x_refs, out_refs, sems):
    _, mine, first, *_ = _ag_parts(x_refs, out_refs, sems)
    for cp in mine + first:
        cp.start()


def _ag_finish(x_refs, out_refs, sems):
    copy, mine, first, me, sibling, chips, c, n = _ag_parts(x_refs, out_refs, sems)
    passed = []
    for j, chip in enumerate(chips):
        for a in range(n):
            copy(a, 1 + j, (*chip, c), me).wait_recv()
            passed.append(copy(a, 4 + j, (*chip, c), sibling))
            passed[-1].start()
    for a in range(n):
        copy(a, 0, sibling, me).wait_recv()
    for j, chip in enumerate(chips):
        for a in range(n):
            copy(a, 4 + j, (*chip, 1 - c), me).wait_recv()
    for cp in first + passed:
        cp.wait_send()
    for cp in mine:
        cp.wait()


def _exchange_sibling(name, gs):
    n = len(gs)

    def body(*refs):
        g_refs, out_refs, (send_sems, recv_sems) = refs[:n], refs[n:2 * n], refs[2 * n:]
        x, y, c = _my_place()
        copies = [pltpu.make_async_remote_copy(
            src_ref=g_refs[a].at[2 * j + 1 - c], dst_ref=out_refs[a].at[j], send_sem=send_sems.at[4 * a + j],
            recv_sem=recv_sems.at[4 * a + j], device_id=(x, y, 1 - c), device_id_type=MESH_ID)
            for a in range(n) for j in range(4)]
        for cp in copies:
            cp.start()
        for cp in copies:
            cp.wait()

    any_spec = pl.BlockSpec(memory_space=pl.ANY)
    return pl.pallas_call(
        body, name=name, out_shape=[jax.ShapeDtypeStruct((4,) + g.shape[1:], g.dtype) for g in gs],
        in_specs=[any_spec] * n, out_specs=[any_spec] * n,
        scratch_shapes=[pltpu.SemaphoreType.DMA((4 * n,)), pltpu.SemaphoreType.DMA((4 * n,))],
    )(*gs)


N_PARTS = 4


def _part_spec(rows, cols, t_pos, lead_block=(), lead_index=lambda *args: ()):
    if rows % (16 * N_PARTS) == 0:
        axis, shape, count = 0, (rows // N_PARTS, cols), N_PARTS
    elif cols % (128 * N_PARTS) == 0:
        axis, shape, count = 1, (rows, cols // N_PARTS), N_PARTS
    else:
        axis, shape, count = 0, (rows, cols), 1

    def index(*args):
        i = jnp.minimum(args[t_pos], count - 1)
        return (*lead_index(*args), *((i, 0) if axis == 0 else (0, i)))

    return pl.BlockSpec((*lead_block, *shape), index)


def _chip_partials(name, gs, sibs, c_idx):
    n = len(gs)

    def body(c_ref, *refs):
        for g_ref, sib_ref, out_ref in zip(refs[:n], refs[n:2 * n], refs[2 * n:]):
            out_ref[...] = (g_ref[...] + sib_ref[...]).astype(MM)

    own = [_part_spec(*g.shape[1:], 1, (1,), lambda j, t, c_ref: (2 * j + c_ref[0],)) for g in gs]
    by_chip = [_part_spec(*g.shape[1:], 1, (1,), lambda j, t, c_ref: (j,)) for g in gs]
    grid_spec = pltpu.PrefetchScalarGridSpec(
        num_scalar_prefetch=1, grid=(4, N_PARTS), in_specs=own + by_chip, out_specs=by_chip)
    return pl.pallas_call(
        body, name=name, grid_spec=grid_spec, out_shape=[jax.ShapeDtypeStruct((4,) + g.shape[1:], MM) for g in gs],
        compiler_params=_cparams(("arbitrary", "arbitrary"), VMEM_LIMIT),
    )(c_idx, *gs, *sibs)


def _exchange_chips(parts):
    n = len(parts)

    def body(*refs):
        p_refs, out_refs, sems = refs[:n], refs[n:2 * n], refs[2 * n:]
        for cp in _xchips_copies(p_refs, out_refs, sems):
            cp.start()
        for cp in _xchips_copies(p_refs, out_refs, sems):
            cp.wait()

    any_spec = pl.BlockSpec(memory_space=pl.ANY)
    return pl.pallas_call(
        body, name="rs_chips", out_shape=_xchips_out_shapes(parts),
        in_specs=[any_spec] * n, out_specs=[any_spec] * n, scratch_shapes=_xchips_sems(n),
    )(*parts)


def _xchips_out_shapes(parts):
    return [jax.ShapeDtypeStruct((3,) + p.shape[1:], p.dtype) for p in parts]


def _xchips_sems(n):
    return [pltpu.SemaphoreType.DMA((3 * n,)), pltpu.SemaphoreType.DMA((3 * n,))]


def _xchips_copies(p_refs, out_refs, sems):
    send_sems, recv_sems = sems
    x, y, c = _my_place()
    chips = [(1 - x, y), (x, 1 - y), (1 - x, 1 - y)]
    return [pltpu.make_async_remote_copy(
        src_ref=p_refs[a].at[2 * px + py], dst_ref=out_refs[a].at[k], send_sem=send_sems.at[3 * a + k],
        recv_sem=recv_sems.at[3 * a + k], device_id=(px, py, c), device_id_type=MESH_ID)
        for a in range(len(p_refs)) for k, (px, py) in enumerate(chips)]


def _adamw_math(w, g, m, v):
    m = ADAM_B1 * m + (1.0 - ADAM_B1) * g
    v = ADAM_B2 * v + (1.0 - ADAM_B2) * jnp.square(g)
    m_hat = m / (1.0 - ADAM_B1 ** ADAM_STEP)
    v_hat = v / (1.0 - ADAM_B2 ** ADAM_STEP)
    delta = -ADAM_LR * (m_hat / (jnp.sqrt(v_hat) + ADAM_EPS) + ADAM_WD * w)
    return delta, m, v


def _sum_adamws(name, gs, sibs, gots, ws, ms, vs, slot_idx, chip_idx):
    n = len(gs)

    def body(s_ref, j_ref, *refs):
        ins, outs = refs[:6 * n], refs[6 * n:]
        for a in range(n):
            g_ref, sib_ref, got_ref, w_ref, m_ref, v_ref = (ins[k * n + a] for k in range(6))
            go_ref, d_ref, m2_ref, v2_ref = outs[4 * a:4 * a + 4]
            grad = g_ref[0] + sib_ref[0]
            for k in range(3):
                grad = grad + got_ref[k].astype(F32)
            go_ref[...] = grad
            d_ref[...], m2_ref[...], v2_ref[...] = _adamw_math(w_ref[...], grad, m_ref[...], v_ref[...])

    shapes = [g.shape[1:] for g in gs]
    flat = [_part_spec(*s, 0) for s in shapes]
    in_specs = ([_part_spec(*s, 0, (1,), lambda t, s_ref, j_ref: (s_ref[0],)) for s in shapes]
                + [_part_spec(*s, 0, (1,), lambda t, s_ref, j_ref: (j_ref[0],)) for s in shapes]
                + [_part_spec(*s, 0, (3,), lambda t, s_ref, j_ref: (0,)) for s in shapes] + flat * 3)
    grid_spec = pltpu.PrefetchScalarGridSpec(
        num_scalar_prefetch=2, grid=(N_PARTS,), in_specs=in_specs, out_specs=[f for f in flat for _ in range(4)])
    res = pl.pallas_call(
        body, name=name, grid_spec=grid_spec,
        out_shape=[jax.ShapeDtypeStruct(s, F32) for s in shapes for _ in range(4)],
        compiler_params=_cparams(("arbitrary",), VMEM_LIMIT),
    )(slot_idx, chip_idx, *gs, *sibs, *gots, *ws, *ms, *vs)
    return [res[4 * a:4 * a + 4] for a in range(n)]


def _adamw_small(parts, w, m, v):
    rows = w.shape[0]

    def body(p_ref, w_ref, m_ref, v_ref, g_ref, d_ref, m2_ref, v2_ref):
        g = p_ref[0]
        for d in range(1, N_DEV):
            g = g + p_ref[d]
        g_ref[...] = g
        d_ref[...], m2_ref[...], v2_ref[...] = _adamw_math(w_ref[...], g, m_ref[...], v_ref[...])

    return pl.pallas_call(
        body, name="adamw_small", out_shape=[jax.ShapeDtypeStruct((rows, 128), F32)] * 4,
    )(parts, w, m, v)


BIG = ("w_in", "w_uq", "w_ukv", "w_branch", "w_out", "w_ffn_gate", "w_ffn_up", "w_ffn_down", "w_ple_gate", "w_ple_proj")
SMALL = (
    ("mix_norm_g", 1024), ("q_a_norm_g", 384), ("kv_a_norm_g", 256), ("q_norm_g", 96), ("k_norm_g", 96),
    ("hg_lb_logits", 1024), ("hg_out_norm_g", 128), ("ffn_norm_g", 1024), ("ple_gate_norm_g", 1024),
    ("ple_post_norm_g", 1024),
)
SMALL_ROWS = 56


def _pack_small(vals):
    rows = []
    for name, n in SMALL:
        v = vals[name].reshape(1, -1).astype(F32)
        rows.append(jnp.pad(v, ((0, 0), (0, (-n) % 128))).reshape(-1, 128))
    return jnp.concatenate(rows, axis=0)


def _unpack_small(packed, shapes):
    out, r = {}, 0
    for name, n in SMALL:
        k = (n + 127) // 128
        out[name] = packed[r:r + k].reshape(1, -1)[:, :n].reshape(shapes[name])
        r += k
    return out


_WEIGHTS = ["mix_norm_g", "w_in", "q_a_norm_g", "w_uq", "kv_a_norm_g", "w_ukv", "q_norm_g", "k_norm_g", "hg_lb_logits",
            "hg_out_norm_g", "w_branch", "w_out", "ffn_norm_g", "w_ffn_gate", "w_ffn_up", "w_ffn_down",
            "ple_gate_norm_g", "w_ple_gate", "w_ple_proj", "ple_post_norm_g"]


def _step(x, p, positions, tgt, w, m, v):
    small_names = [n for n, _ in SMALL]
    T = x.shape[1]
    px, py, pc = _my_place()
    as_idx = lambda t: jnp.reshape(t, (1,)).astype(jnp.int32)

    def two_d(n, t):
        t = t.reshape(-1, t.shape[-1])
        return t.T if n in TRANSPOSED else t

    def full_shape(n, t):
        return (t.T if n in TRANSPOSED else t).reshape(w[n].shape)

    blocks = {n: two_d(n, w[n]).astype(MM) for n in BIG}
    big = dict(zip(EARLY, _all_gather("ag_weights", [blocks[n] for n in EARLY])))
    small = {n: (w[n] if n == "hg_lb_logits" else w[n].reshape(1, -1)) for n in small_names}

    loss_p, grad_x, small_g, grads, sibs, gots = _local_step(
        x[0], p[0, 0], positions.reshape(T, 1), tgt[0], small, big, late_blocks=blocks, core=as_idx(pc))

    out_g, out_d, out_m, out_v = {}, {}, {}, {}
    for tag, names in (("a", GROUP_A), ("b", GROUP_B), ("c", GROUP_C)):
        pick = lambda table: [table[n] for n in names]
        res = _sum_adamws("adamw_" + tag, pick(grads), pick(sibs), pick(gots), [two_d(n, w[n]) for n in names],
                          [two_d(n, m[n]) for n in names], [two_d(n, v[n]) for n in names],
                          as_idx(4 * px + 2 * py + pc), as_idx(2 * px + py))
        for n, r in zip(names, res):
            out_g[n], out_d[n], out_m[n], out_v[n] = [full_shape(n, t) for t in r]

    packed_g = _pack_small(small_g)
    loss_row = jnp.concatenate([jnp.pad(jnp.sum(loss_p).reshape(1, 1), ((0, 0), (0, 127))),
                                jnp.zeros((SMALL_ROWS - packed_g.shape[0] - 1, 128), F32)], axis=0)
    parts = _all_gather("ag_small", [jnp.concatenate([packed_g, loss_row], axis=0)])[0]
    pad_rows = lambda t: jnp.pad(t, ((0, SMALL_ROWS - t.shape[0]), (0, 0)))
    sw = pad_rows(_pack_small({n: w[n] for n in small_names}))
    sm = pad_rows(_pack_small({n: m[n] for n in small_names}))
    sv = pad_rows(_pack_small({n: v[n] for n in small_names}))
    g_s, d_s, m_s, v_s = _adamw_small(parts, sw, sm, sv)
    shapes = {n: w[n].shape for n in small_names}
    n_packed = packed_g.shape[0]
    loss = g_s[n_packed, 0]
    for src, dst in ((g_s, out_g), (d_s, out_d), (m_s, out_m), (v_s, out_v)):
        dst.update(_unpack_small(src, shapes))

    outs = [loss, grad_x[None]]
    for table in (out_g, out_d, out_m, out_v):
        outs += [table[n] for n in _WEIGHTS]
    return tuple(outs)


def kernel(x, p, positions, mix_norm_g, w_in, q_a_norm_g, w_uq, kv_a_norm_g, w_ukv, q_norm_g, k_norm_g, hg_lb_logits, hg_out_norm_g, w_branch, w_out, ffn_norm_g, w_ffn_gate, w_ffn_up, w_ffn_down, ple_gate_norm_g, w_ple_gate, w_ple_proj, ple_post_norm_g, loss_target, m_mix_norm_g, m_w_in, m_q_a_norm_g, m_w_uq, m_kv_a_norm_g, m_w_ukv, m_q_norm_g, m_k_norm_g, m_hg_lb_logits, m_hg_out_norm_g, m_w_branch, m_w_out, m_ffn_norm_g, m_w_ffn_gate, m_w_ffn_up, m_w_ffn_down, m_ple_gate_norm_g, m_w_ple_gate, m_w_ple_proj, m_ple_post_norm_g, v_mix_norm_g, v_w_in, v_q_a_norm_g, v_w_uq, v_kv_a_norm_g, v_w_ukv, v_q_norm_g, v_k_norm_g, v_hg_lb_logits, v_hg_out_norm_g, v_w_branch, v_w_out, v_ffn_norm_g, v_w_ffn_gate, v_w_ffn_up, v_w_ffn_down, v_ple_gate_norm_g, v_w_ple_gate, v_w_ple_proj, v_ple_post_norm_g):
    w = dict(mix_norm_g=mix_norm_g, w_in=w_in, q_a_norm_g=q_a_norm_g, w_uq=w_uq, kv_a_norm_g=kv_a_norm_g, w_ukv=w_ukv,
             q_norm_g=q_norm_g, k_norm_g=k_norm_g, hg_lb_logits=hg_lb_logits, hg_out_norm_g=hg_out_norm_g,
             w_branch=w_branch, w_out=w_out, ffn_norm_g=ffn_norm_g, w_ffn_gate=w_ffn_gate, w_ffn_up=w_ffn_up,
             w_ffn_down=w_ffn_down, ple_gate_norm_g=ple_gate_norm_g, w_ple_gate=w_ple_gate, w_ple_proj=w_ple_proj,
             ple_post_norm_g=ple_post_norm_g)
    m = dict(mix_norm_g=m_mix_norm_g, w_in=m_w_in, q_a_norm_g=m_q_a_norm_g, w_uq=m_w_uq, kv_a_norm_g=m_kv_a_norm_g,
             w_ukv=m_w_ukv, q_norm_g=m_q_norm_g, k_norm_g=m_k_norm_g, hg_lb_logits=m_hg_lb_logits,
             hg_out_norm_g=m_hg_out_norm_g, w_branch=m_w_branch, w_out=m_w_out, ffn_norm_g=m_ffn_norm_g,
             w_ffn_gate=m_w_ffn_gate, w_ffn_up=m_w_ffn_up, w_ffn_down=m_w_ffn_down,
             ple_gate_norm_g=m_ple_gate_norm_g, w_ple_gate=m_w_ple_gate, w_ple_proj=m_w_ple_proj,
             ple_post_norm_g=m_ple_post_norm_g)
    v = dict(mix_norm_g=v_mix_norm_g, w_in=v_w_in, q_a_norm_g=v_q_a_norm_g, w_uq=v_w_uq, kv_a_norm_g=v_kv_a_norm_g,
             w_ukv=v_w_ukv, q_norm_g=v_q_norm_g, k_norm_g=v_k_norm_g, hg_lb_logits=v_hg_lb_logits,
             hg_out_norm_g=v_hg_out_norm_g, w_branch=v_w_branch, w_out=v_w_out, ffn_norm_g=v_ffn_norm_g,
             w_ffn_gate=v_w_ffn_gate, w_ffn_up=v_w_ffn_up, w_ffn_down=v_w_ffn_down,
             ple_gate_norm_g=v_ple_gate_norm_g, w_ple_gate=v_w_ple_gate, w_ple_proj=v_w_ple_proj,
             ple_post_norm_g=v_ple_post_norm_g)
    return _step(x, p, positions, loss_target, w, m, v)
```

```python
import functools

import jax
import jax.numpy as jnp
import numpy as np
from jax import lax
from jax.experimental import pallas as pl
from jax.experimental.pallas import tpu as pltpu

F32 = jnp.float32
MM = jnp.bfloat16
HI = lax.Precision.HIGHEST
MESH_ID = pl.DeviceIdType.MESH

D_MODEL = 1024
N_DEV = 8
MLA_HEADS = 8
QK_NOPE = 64
QK_ROPE = 32
QK_DIM = 96
V_DIM = 64
HEAD_PAD = 128
Q_RANK = 384
KV_RANK = 256
ROPE_BASE = 10000.0
HG_HEADS = 4
HG_DIM = 128
HG_W = 512
HG_CHUNK = 64
FFN = 2816
PLE = 256
EPS = 1e-6
ATT_SCALE = QK_DIM ** -0.5
NEG = -1e30

ADAM_LR = 0.001
ADAM_B1 = 0.9
ADAM_B2 = 0.999
ADAM_EPS = 1e-08
ADAM_WD = 0.01
ADAM_STEP = 10

SEC_CQ = (0, 384)
SEC_CKV = (384, 256)
SEC_KR = (640, 128)
SEC_HQ = (768, 512)
SEC_HF = (1280, 512)
SEC_HI = (1792, 512)
SEC_HG = (2304, 512)
SEC_BG = (2816, 2048)
IN_PAD = 4864
SECTIONS = (SEC_CQ, SEC_CKV, SEC_KR, SEC_HQ, SEC_HF, SEC_HI, SEC_HG, SEC_BG)
COL_SECTIONS = ((0, 384), (384, 256), (640, 32), (672, 512), (1184, 512), (1696, 512), (2208, 512), (2720, 2048))
IN_COLS = 4768
IN_BLOCK = IN_COLS // 8

VMEM_LIMIT = 58 * 1024 * 1024
ROW_TILE = 256
ATT_TILE = 1024
ATT_HEADS = 2
HG_BLOCK = 512
HG_UNROLL = 4


def _dot(a, b):
    return jnp.dot(a.astype(MM), b.astype(MM), preferred_element_type=F32)


def _dot_nt(a, b):
    return lax.dot_general(a.astype(MM), b.astype(MM), (((1,), (1,)), ((), ())), preferred_element_type=F32)


def _dot_tn(a, b):
    return lax.dot_general(a.astype(MM), b.astype(MM), (((0,), (0,)), ((), ())), preferred_element_type=F32)


def _dot_hi(a, b):
    return jnp.dot(a, b, preferred_element_type=F32, precision=HI)


def _sigmoid(x):
    return 1.0 / (1.0 + jnp.exp(-x))


def _rms(x, n=None):
    n = x.shape[-1] if n is None else n
    r = lax.rsqrt(jnp.sum(x * x, axis=-1, keepdims=True) * (1.0 / n) + EPS)
    return x * r, r


def _rms_bwd(dxh, xh, r, n=None):
    n = xh.shape[-1] if n is None else n
    return r * (dxh - xh * (jnp.sum(dxh * xh, axis=-1, keepdims=True) * (1.0 / n)))


def _rope_tables(pos, tm):
    lane = lax.broadcasted_iota(jnp.int32, (tm, HEAD_PAD), 1)
    idx = jnp.where(lane < QK_NOPE + QK_ROPE // 2, lane - QK_NOPE, lane - QK_NOPE - QK_ROPE // 2)
    inv = jnp.exp(idx.astype(F32) * (-np.log(ROPE_BASE) * 2.0 / QK_ROPE))
    ang = pos.astype(F32) * inv
    in_rope = (lane >= QK_NOPE) & (lane < QK_DIM)
    first = lane < QK_NOPE + QK_ROPE // 2
    cos_t = jnp.where(in_rope, jnp.cos(ang), 1.0)
    sin_t = jnp.where(in_rope, jnp.where(first, -jnp.sin(ang), jnp.sin(ang)), 0.0)
    return cos_t, sin_t, (first, in_rope)


def _rope_swap(x, halves):
    first, in_rope = halves
    half = QK_ROPE // 2
    return jnp.where(in_rope, jnp.where(first, pltpu.roll(x, HEAD_PAD - half, 1), pltpu.roll(x, half, 1)), 0.0)


def _cparams(sem, vmem=None):
    return pltpu.CompilerParams(dimension_semantics=sem, vmem_limit_bytes=vmem)


def _row_call(name, body, T, tm, row_ins, full_ins, row_outs, acc_outs, vmem=None, scratch=(), xchg=((), ())):
    n_in, n_out, n_x = len(row_ins) + len(full_ins), len(row_outs) + len(acc_outs), _x_count(xchg)
    steps = T // tm

    def kern(*refs):
        ins, x_in, refs = refs[:n_in], refs[n_in:n_in + n_x], refs[n_in + n_x:]
        outs, x_out, refs = refs[:n_out], refs[n_out:n_out + n_x], refs[n_out + n_x:]
        scr, x_sems = refs[:len(scratch)], refs[len(scratch):]
        i = pl.program_id(0)
        if n_x:
            @pl.when(i == 0)
            def _():
                for cp in _x_copies(len(xchg[0]), x_in, x_out, x_sems):
                    cp.start()

        body(i, *ins, *outs, *scr)
        if n_x:
            @pl.when(i == steps - 1)
            def _():
                for cp in _x_copies(len(xchg[0]), x_in, x_out, x_sems):
                    cp.wait()

    any_spec = pl.BlockSpec(memory_space=pl.ANY)
    in_specs = [pl.BlockSpec((tm, a.shape[1]), lambda i: (i, 0)) for a in row_ins]
    in_specs += [pl.BlockSpec(a.shape, lambda i, nd=a.ndim: (0,) * nd, pipeline_mode=pl.Buffered(1)) for a in full_ins]
    out_specs = [pl.BlockSpec((tm, n), lambda i: (i, 0)) for n, _ in row_outs]
    out_specs += [pl.BlockSpec(s, lambda i, nd=len(s): (0,) * nd) for s, _ in acc_outs]
    out_shape = [jax.ShapeDtypeStruct((T, n), dt) for n, dt in row_outs]
    out_shape += [jax.ShapeDtypeStruct(s, dt) for s, dt in acc_outs]
    return pl.pallas_call(
        kern, name=name, grid=(steps,), in_specs=in_specs + [any_spec] * n_x, out_specs=out_specs + [any_spec] * n_x,
        out_shape=out_shape + _x_out_shapes(xchg), scratch_shapes=list(scratch) + _x_sems(xchg),
        compiler_params=_cparams(("arbitrary",), vmem),
    )(*row_ins, *full_ins, *xchg[0], *xchg[1])


FFN_HALVES = (slice(0, FFN // 2), slice(FFN // 2, FFN))
ROW_CHUNK = 16
CHUNK_UNROLL = True


def _by_chunks(tm, fn):
    def step(c, carry):
        fn(pl.ds(pl.multiple_of(c * ROW_CHUNK, ROW_CHUNK), ROW_CHUNK))
        return carry

    lax.fori_loop(0, tm // ROW_CHUNK, step, 0, unroll=CHUNK_UNROLL)


def _fold8(x):
    return x[:8] + x[8:]


def _acc(ref, i, val):
    @pl.when(i == 0)
    def _():
        ref[...] = val

    @pl.when(i != 0)
    def _():
        ref[...] += val


def _in_proj_fwd(x, g_mix, w_in, T, tm):
    def body(i, x_ref, g_ref, w_ref, h_ref, *rest):
        outs, pj_s = rest[:-1], rest[-1]
        g = g_ref[...]

        def norm(rows):
            h_ref[rows, :] = (_rms(x_ref[rows, :])[0] * g).astype(MM)

        _by_chunks(tm, norm)
        for d in range(N_DEV):
            pj_s[d] = _dot_nt(h_ref[...], w_ref[d])

        def join_and_cut(rows):
            proj = jnp.concatenate([pj_s[d, rows, :] for d in range(N_DEV)], axis=1)
            for (s, n), o_ref in zip(COL_SECTIONS, outs):
                if n == QK_ROPE:
                    o_ref[rows, :] = jnp.concatenate(
                        [jnp.zeros((ROW_CHUNK, QK_NOPE), F32), proj[:, s:s + n],
                         jnp.zeros((ROW_CHUNK, HEAD_PAD - QK_DIM), F32)], axis=1)
                else:
                    o_ref[rows, :] = proj[:, s:s + n]

        _by_chunks(tm, join_and_cut)

    row_outs = [(D_MODEL, MM)] + [(n, F32) for _, n in SECTIONS]
    return _row_call("in_proj_fwd", body, T, tm, [x], [g_mix, w_in], row_outs, [], VMEM_LIMIT,
                     scratch=[pltpu.VMEM((N_DEV, tm, IN_BLOCK), F32)])


def _mla_heads_fwd(raw, g_pad, cos_t, sin_t, first):
    outs, saved = [], []
    for h in range(MLA_HEADS):
        xh, r = _rms(raw[:, h * HEAD_PAD:(h + 1) * HEAD_PAD], QK_DIM)
        y = xh * g_pad
        outs.append(y * cos_t + _rope_swap(y, first) * sin_t)
        saved.append((xh, r))
    return outs, saved


def _mla_raw_heads(cqn, ckvn, kr, wuq_ref, wukv_ref, tm):
    lane = lax.broadcasted_iota(jnp.int32, (tm, HEAD_PAD), 1)
    nope = lane < QK_NOPE
    one_lane = jnp.where(lane == V_DIM, 1.0, 0.0)
    qs, ks, vs = [], [], []
    for h in range(MLA_HEADS):
        qs.append(_dot_nt(cqn, wuq_ref[h]))
        kv = _dot(ckvn, wukv_ref[h])
        ks.append(jnp.where(nope, kv, kr))
        vs.append(jnp.where(nope, pltpu.roll(kv, V_DIM, 1), one_lane))
    return jnp.concatenate(qs, axis=1), jnp.concatenate(ks, axis=1), jnp.concatenate(vs, axis=1)


def _mla_prep_fwd(cq, ckv, kr, pos, g_qa, g_kva, g_qn, g_kn, w_uq, w_ukv, T, tm):
    def body(i, cq_ref, ckv_ref, kr_ref, pos_ref, gqa_ref, gkva_ref, gqn_ref, gkn_ref, wuq_ref, wukv_ref,
             q_ref, k_ref, v_ref):
        cos_t, sin_t, first = _rope_tables(pos_ref[...], tm)
        cqn = _rms(cq_ref[...])[0] * gqa_ref[...]
        ckvn = _rms(ckv_ref[...])[0] * gkva_ref[...]
        q_raw, k_raw, v = _mla_raw_heads(cqn, ckvn, kr_ref[...], wuq_ref, wukv_ref, tm)
        qs, _ = _mla_heads_fwd(q_raw, gqn_ref[...], cos_t, sin_t, first)
        ks, _ = _mla_heads_fwd(k_raw, gkn_ref[...], cos_t, sin_t, first)
        q_ref[...] = (jnp.concatenate(qs, axis=1) * ATT_SCALE).astype(MM)
        k_ref[...] = jnp.concatenate(ks, axis=1).astype(MM)
        v_ref[...] = v.astype(MM)

    w = MLA_HEADS * HEAD_PAD
    return _row_call("mla_prep_fwd", body, T, tm, [cq, ckv, kr, pos], [g_qa, g_kva, g_qn, g_kn, w_uq, w_ukv],
                     [(w, MM), (w, MM), (w, MM)], [])


def _causal_pairs(n, by_query):
    if by_query:
        pairs = [(q, k) for q in range(n) for k in range(q + 1)]
    else:
        pairs = [(q, k) for k in range(n) for q in range(k, n)]
    return np.array([p[0] for p in pairs], np.int32), np.array([p[1] for p in pairs], np.int32)


def _flash_fwd(qf, kf, vf, T, ag_blocks=()):
    tq = min(ATT_TILE, T)
    nq = T // tq

    qi_tab, ki_tab = _causal_pairs(nq, by_query=True)

    hp = ATT_HEADS

    n_ag = len(ag_blocks)
    n_heads, n_pairs = MLA_HEADS // hp, len(qi_tab)

    def body(qi_ref, ki_ref, q_ref, k_ref, v_ref, *rest):
        ag_in, (o_ref, lse_ref), rest = rest[:n_ag], rest[n_ag:n_ag + 2], rest[n_ag + 2:]
        ag_out, (m_s, acc_s), ag_sems = rest[:n_ag], rest[n_ag:n_ag + 2], rest[n_ag + 2:]
        t = pl.program_id(1)
        qi, ki = qi_ref[t], ki_ref[t]
        if n_ag:
            @pl.when((pl.program_id(0) == 0) & (t == 0))
            def _():
                _ag_start(ag_in, ag_out, ag_sems)

        @pl.when(ki == 0)
        def _():
            m_s[...] = jnp.full_like(m_s, NEG)
            acc_s[...] = jnp.zeros_like(acc_s)

        def step(masked):
            for hh in range(hp):
                hs = slice(hh * HEAD_PAD, (hh + 1) * HEAD_PAD)
                s_t = _dot_nt(k_ref[:, hs], q_ref[:, hs])
                if masked:
                    key = lax.broadcasted_iota(jnp.int32, (tq, tq), 0)
                    qry = lax.broadcasted_iota(jnp.int32, (tq, tq), 1)
                    s_t = jnp.where(key <= qry, s_t, NEG)
                m_old = m_s[hh]
                m_new = jnp.maximum(m_old, jnp.max(s_t, axis=0, keepdims=True))
                p_t = jnp.exp(s_t - m_new)
                acc_s[hh] = jnp.exp(m_old - m_new) * acc_s[hh] + _dot_tn(v_ref[:, hs], p_t)
                m_s[hh] = m_new

        @pl.when(ki < qi)
        def _():
            step(False)

        @pl.when(ki == qi)
        def _():
            step(True)
            real = lax.broadcasted_iota(jnp.int32, (HEAD_PAD, tq), 0) < V_DIM
            for hh in range(hp):
                hs = slice(hh * HEAD_PAD, (hh + 1) * HEAD_PAD)
                acc = acc_s[hh]
                l = acc[V_DIM:V_DIM + 1]
                o_ref[:, hs] = jnp.where(real, acc / l, 0.0).T
                lse_ref[:, hs] = jnp.broadcast_to(m_s[hh] + jnp.log(l), (HEAD_PAD, tq)).T

        if n_ag:
            @pl.when((pl.program_id(0) == n_heads - 1) & (t == n_pairs - 1))
            def _():
                _ag_finish(ag_in, ag_out, ag_sems)

    q_spec = pl.BlockSpec((tq, hp * HEAD_PAD), lambda h, t, qi_ref, ki_ref: (qi_ref[t], h))
    kv_spec = pl.BlockSpec((tq, hp * HEAD_PAD), lambda h, t, qi_ref, ki_ref: (ki_ref[t], h))
    any_spec = pl.BlockSpec(memory_space=pl.ANY)
    grid_spec = pltpu.PrefetchScalarGridSpec(
        num_scalar_prefetch=2, grid=(n_heads, n_pairs),
        in_specs=[q_spec, kv_spec, kv_spec] + [any_spec] * n_ag, out_specs=[q_spec, q_spec] + [any_spec] * n_ag,
        scratch_shapes=[pltpu.VMEM((hp, 1, tq), F32), pltpu.VMEM((hp, HEAD_PAD, tq), F32)]
        + (_ag_sems(n_ag) if n_ag else []))
    return pl.pallas_call(
        body, name="flash_fwd", grid_spec=grid_spec,
        out_shape=[jax.ShapeDtypeStruct((T, MLA_HEADS * HEAD_PAD), F32)] * 2 + _ag_out_shapes(ag_blocks),
        compiler_params=_cparams(("arbitrary", "arbitrary")),
    )(jnp.asarray(qi_tab), jnp.asarray(ki_tab), qf, kf, vf, *ag_blocks)


def _hg_gates(hf, lb):
    sg = _sigmoid(hf)
    f = lb + (1.0 - lb) * sg
    return sg, f, jnp.log(f), 1.0 - f


def _tri(n, lower):
    r = lax.broadcasted_iota(jnp.int32, (n, n), 0)
    c = lax.broadcasted_iota(jnp.int32, (n, n), 1)
    return jnp.where((c <= r) if lower else (c >= r), 1.0, 0.0).astype(F32)


def _hg_levels():
    C = HG_CHUNK
    t = lax.broadcasted_iota(jnp.int32, (C, C), 0)
    s = lax.broadcasted_iota(jnp.int32, (C, C), 1)
    levels = []
    for shift in range(C.bit_length() - 2, -1, -1):
        pair_t, pair_s = lax.shift_right_logical(t, shift + 1), lax.shift_right_logical(s, shift + 1)
        later_t = (lax.shift_right_logical(t, shift) & 1) == 1
        earlier_s = (lax.shift_right_logical(s, shift) & 1) == 0
        levels.append((1 << shift, (pair_t == pair_s) & later_t & earlier_s))
    return levels, t == s


def _hg_refs(b):
    C, n = b.shape
    row = lax.broadcasted_iota(jnp.int32, (C, n), 0)
    back1, back2, ahead1 = pltpu.roll(b, 1, 0), pltpu.roll(b, 2, 0), pltpu.roll(b, C - 1, 0)
    refs = []
    for half in (32, 16, 8, 4):
        refs.append(jnp.concatenate(
            [jnp.broadcast_to(b[lo + half - 1:lo + half], (2 * half, n)) for lo in range(0, C, 2 * half)], axis=0))
    in4 = row & 3
    refs.append(jnp.where(in4 == 0, ahead1, jnp.where(in4 == 1, b, jnp.where(in4 == 2, back1, back2))))
    refs.append(jnp.where((row & 1) == 1, back1, b))
    return refs


def _hg_intra(q, k, b, refs, levels, eye):
    a = jnp.where(eye, jnp.sum(q * k, axis=1, keepdims=True), 0.0)
    saved = []
    for r, (_, mask) in zip(refs, levels):
        e = jnp.exp(-jnp.abs(b - r))
        q_t, k_t = q * e, k * e
        a = a + jnp.where(mask, _dot_nt(q_t, k_t), 0.0)
        saved.append((q_t, k_t, e))
    return a, saved


def _hg_intra_bwd(d_a, q, k, saved, levels, eye):
    diag = jnp.sum(jnp.where(eye, d_a, 0.0), axis=1, keepdims=True)
    dq, dk = diag * k, diag * q
    for (q_t, k_t, e), (_, mask) in zip(saved, levels):
        da = jnp.where(mask, d_a, 0.0)
        dq = dq + _dot(da, k_t) * e
        dk = dk + _dot_tn(da, q_t) * e
    return dq, dk


def _hgrn_fwd(hq, hf, hi, lb, T):
    rb = min(HG_BLOCK, T)
    ncb = rb // HG_CHUNK

    def body(hq_ref, hf_ref, hi_ref, lb_ref, o_ref, s0_ref, st_ref):
        @pl.when(pl.program_id(0) == 0)
        def _():
            st_ref[...] = jnp.zeros_like(st_ref)

        tril = _tri(HG_CHUNK, True)
        levels, eye = _hg_levels()

        def chunk(c, carry):
            rows = pl.ds(pl.multiple_of(c * HG_CHUNK, HG_CHUNK), HG_CHUNK)
            _, _, logf, kk = _hg_gates(hf_ref[rows, :], lb_ref[...])
            b = _dot_hi(tril, logf)
            refs = _hg_refs(b)
            q_all, v_all = hq_ref[rows, :], hi_ref[rows, :]
            outs = []
            for h in range(HG_HEADS):
                ls = slice(h * HG_DIM, (h + 1) * HG_DIM)
                q, k, v, bh = q_all[:, ls], kk[:, ls], v_all[:, ls], b[:, ls]
                st = st_ref[h]
                s0_ref[c, h * HG_DIM:(h + 1) * HG_DIM, :] = st
                b_end = bh[HG_CHUNK - 1:HG_CHUNK]
                a, _ = _hg_intra(q, k, bh, [r[:, ls] for r in refs], levels, eye)
                outs.append(_dot_nt(q * jnp.exp(bh), st) + _dot(a, v))
                st_ref[h] = st * jnp.exp(b_end) + _dot_tn(v, k * jnp.exp(b_end - bh))
            o_ref[rows, :] = jnp.concatenate(outs, axis=1)
            return carry

        lax.fori_loop(0, ncb, chunk, 0, unroll=HG_UNROLL)

    row = pl.BlockSpec((rb, HG_W), lambda i: (i, 0))
    return pl.pallas_call(
        body, name="hgrn_fwd", grid=(T // rb,),
        in_specs=[row, row, row, pl.BlockSpec((1, HG_W), lambda i: (0, 0))],
        out_specs=[row, pl.BlockSpec((ncb, HG_W, HG_DIM), lambda i: (i, 0, 0))],
        out_shape=[jax.ShapeDtypeStruct((T, HG_W), F32), jax.ShapeDtypeStruct((T // HG_CHUNK, HG_W, HG_DIM), F32)],
        scratch_shapes=[pltpu.VMEM((HG_HEADS, HG_DIM, HG_DIM), F32)],
        compiler_params=_cparams(("arbitrary",)),
    )(hq, hf, hi, lb)


def _hgrn_bwd(hq, hf, hi, do, s0, lb, T, xchg=((), ())):
    rb = min(HG_BLOCK, T)
    ncb = rb // HG_CHUNK
    nb = T // rb
    C = HG_CHUNK
    n_x, n_sib = _x_count(xchg), len(xchg[0])

    def body(hq_ref, hf_ref, hi_ref, do_ref, s0_ref, lb_ref, *rest):
        x_in, (dq_ref, df_ref, dv_ref, dlb_ref), rest = rest[:n_x], rest[n_x:n_x + 4], rest[n_x + 4:]
        x_out, dst_ref, x_sems = rest[:n_x], rest[n_x], rest[n_x + 1:]

        @pl.when(pl.program_id(0) == 0)
        def _():
            dst_ref[...] = jnp.zeros_like(dst_ref)
            dlb_ref[...] = jnp.zeros_like(dlb_ref)
            for cp in _x_copies(n_sib, x_in, x_out, x_sems):
                cp.start()

        tril, triu = _tri(C, True), _tri(C, False)
        row_cc = lax.broadcasted_iota(jnp.int32, (C, C), 0)
        col_cc = lax.broadcasted_iota(jnp.int32, (C, C), 1)
        last_row = lax.broadcasted_iota(jnp.int32, (C, HG_DIM), 0) == C - 1
        lb_v = lb_ref[...]
        levels, eye = _hg_levels()

        def chunk(cc, carry):
            c = ncb - 1 - cc
            rows = pl.ds(pl.multiple_of(c * C, C), C)
            hf_c = hf_ref[rows, :]
            sg, f, logf, kk = _hg_gates(hf_c, lb_v)
            b = _dot_hi(tril, logf)
            refs = _hg_refs(b)
            q_all, v_all, do_all = hq_ref[rows, :], hi_ref[rows, :], do_ref[rows, :]
            dq_o, dk_o, dv_o, db_o = [], [], [], []
            for h in range(HG_HEADS):
                ls = slice(h * HG_DIM, (h + 1) * HG_DIM)
                q, k, v, bh, d_o = q_all[:, ls], kk[:, ls], v_all[:, ls], b[:, ls], do_all[:, ls]
                st0 = s0_ref[c, h * HG_DIM:(h + 1) * HG_DIM, :]
                dst = dst_ref[h]
                b_end = bh[C - 1:C]
                e_b, e_end = jnp.exp(bh), jnp.exp(b_end)
                e_rem = jnp.exp(b_end - bh)
                qe, kd = q * e_b, k * e_rem
                st_end = st0 * e_end + _dot_tn(v, kd)
                a, saved = _hg_intra(q, k, bh, [r[:, ls] for r in refs], levels, eye)
                d_a = jnp.where(col_cc <= row_cc, _dot_nt(d_o, v), 0.0)
                dq_i, dk_i = _hg_intra_bwd(d_a, q, k, saved, levels, eye)
                dv = _dot_tn(a, d_o) + _dot_nt(kd, dst)
                dq = e_b * _dot(d_o, st0) + dq_i
                dk = e_rem * _dot(v, dst) + dk_i
                extra = jnp.sum(dst * st_end, axis=0, keepdims=True)
                db_o.append(q * dq - k * dk + jnp.where(last_row, extra, 0.0))
                dst_ref[h] = dst * e_end + _dot_tn(d_o, qe)
                dq_o.append(dq)
                dk_o.append(dk)
                dv_o.append(dv)
            dlogf = _dot_hi(triu, jnp.concatenate(db_o, axis=1))
            d_f = dlogf / f - jnp.concatenate(dk_o, axis=1)
            dq_ref[rows, :] = jnp.concatenate(dq_o, axis=1).astype(MM)
            dv_ref[rows, :] = jnp.concatenate(dv_o, axis=1).astype(MM)
            df_ref[rows, :] = (d_f * (1.0 - lb_v) * sg * (1.0 - sg)).astype(MM)
            dlb_ref[...] += jnp.sum(d_f * (1.0 - sg), axis=0, keepdims=True)
            return carry

        lax.fori_loop(0, ncb, chunk, 0, unroll=HG_UNROLL)

        if n_x:
            @pl.when(pl.program_id(0) == nb - 1)
            def _():
                for cp in _x_copies(n_sib, x_in, x_out, x_sems):
                    cp.wait()

    row = pl.BlockSpec((rb, HG_W), lambda i: (nb - 1 - i, 0))
    one = pl.BlockSpec((1, HG_W), lambda i: (0, 0))
    any_spec = pl.BlockSpec(memory_space=pl.ANY)
    return pl.pallas_call(
        body, name="hgrn_bwd", grid=(nb,),
        in_specs=[row, row, row, row, pl.BlockSpec((ncb, HG_W, HG_DIM), lambda i: (nb - 1 - i, 0, 0)), one]
        + [any_spec] * n_x,
        out_specs=[row, row, row, one] + [any_spec] * n_x,
        out_shape=[jax.ShapeDtypeStruct((T, HG_W), MM)] * 3 + [jax.ShapeDtypeStruct((1, HG_W), F32)]
        + _x_out_shapes(xchg),
        scratch_shapes=[pltpu.VMEM((HG_HEADS, HG_DIM, HG_DIM), F32)] + _x_sems(xchg),
        compiler_params=_cparams(("arbitrary",)),
    )(hq, hf, hi, do, s0, lb, *xchg[0], *xchg[1])


def _silu_parts(x):
    sg = _sigmoid(x)
    return x * sg, sg * (1.0 + x * (1.0 - sg))


def _merge_fwd(attn, o, hg, bg, x, g_out, w_bra, w_brb, w_out, T, tm):
    def body(i, attn_ref, o_ref, hg_ref, bg_ref, x_ref, g_ref, wa_ref, wb_ref, wo_ref,
             x1_ref, ya_ref, yb_ref, m_ref, rec_ref):
        g = g_ref[...]

        def recurrent_out(rows):
            for h in range(HG_HEADS):
                ls = slice(h * HG_DIM, (h + 1) * HG_DIM)
                rec_ref[rows, ls] = (_rms(o_ref[rows, ls])[0] * g * _silu_parts(hg_ref[rows, ls])[0]).astype(MM)

        _by_chunks(tm, recurrent_out)
        ya_ref[...] = _dot(attn_ref[...], wa_ref[...])
        yb_ref[...] = jnp.dot(rec_ref[...], wb_ref[...], preferred_element_type=F32)

        def gate(rows):
            m_ref[rows, :] = (_sigmoid(bg_ref[rows, :D_MODEL]) * ya_ref[rows, :]
                              + _sigmoid(bg_ref[rows, D_MODEL:]) * yb_ref[rows, :]).astype(MM)

        _by_chunks(tm, gate)
        x1_ref[...] = x_ref[...] + jnp.dot(m_ref[...], wo_ref[...], preferred_element_type=F32)

    return _row_call("merge_fwd", body, T, tm, [attn, o, hg, bg, x], [g_out, w_bra, w_brb, w_out],
                     [(D_MODEL, F32), (D_MODEL, F32), (D_MODEL, F32), (D_MODEL, MM), (HG_W, MM)], [], VMEM_LIMIT)


def _ffn_fwd(x1, g_ffn, w_g, w_u, w_d, T, tm):
    def body(i, x1_ref, g_ref, wg_ref, wu_ref, wd_ref, x2_ref, gt_ref, up_ref, h2_ref, a_s):
        g = g_ref[...]

        def norm(rows):
            h2_ref[rows, :] = (_rms(x1_ref[rows, :])[0] * g).astype(MM)

        _by_chunks(tm, norm)
        gt_ref[...] = _dot_nt(h2_ref[...], wg_ref[...])
        up_ref[...] = _dot_nt(h2_ref[...], wu_ref[...])

        def act(rows):
            for cs in FFN_HALVES:
                a_s[rows, cs] = (_silu_parts(gt_ref[rows, cs])[0] * up_ref[rows, cs]).astype(MM)

        _by_chunks(tm, act)
        x2_ref[...] = x1_ref[...] + jnp.dot(a_s[...], wd_ref[...], preferred_element_type=F32)

    return _row_call("ffn_fwd", body, T, tm, [x1], [g_ffn, w_g, w_u, w_d],
                     [(D_MODEL, F32), (FFN, F32), (FFN, F32), (D_MODEL, MM)], [], VMEM_LIMIT,
                     scratch=[pltpu.VMEM((tm, FFN), MM)])


def _ple_loss(x2, p, tgt, g_pg, g_post, w_pg, w_pp, T, tm):
    def body(i, x2_ref, p_ref, t_ref, gpg_ref, gpo_ref, wpg_ref, wpp_ref,
             dx2_ref, loss_ref, dgpo_ref, dgpg_ref, dwpg_ref, dwpp_ref, u_s, n3_s, z_s, dz_s, du_s, dy_s, dn3_s):
        @pl.when(i == 0)
        def _():
            for ref in (loss_ref, dgpo_ref, dgpg_ref, dwpg_ref, dwpp_ref):
                ref[...] = jnp.zeros_like(ref)

        gpg, gpo = gpg_ref[...], gpo_ref[...]
        p_mm = p_ref[...].astype(MM)
        for d in range(N_DEV):
            u_s[:, d * HEAD_PAD:(d + 1) * HEAD_PAD] = jnp.dot(p_mm, wpp_ref[d], preferred_element_type=F32)

        def gate_input(rows):
            n3_s[rows, :] = (_rms(x2_ref[rows, :])[0] * gpg).astype(MM)

        _by_chunks(tm, gate_input)
        z_s[...] = jnp.dot(n3_s[...], wpg_ref[...], preferred_element_type=F32)

        def loss_and_back(rows):
            uh, ru = _rms(u_s[rows, :])
            e = uh * gpo
            gate = _sigmoid(z_s[rows, :])
            diff = x2_ref[rows, :] + gate * e - t_ref[rows, :]
            dy = diff * (1.0 / D_MODEL)
            de = dy * gate
            dz_s[rows, :] = (dy * e * gate * (1.0 - gate)).astype(MM)
            du_s[rows, :] = _rms_bwd(de * gpo, uh, ru).astype(MM)
            dy_s[rows, :] = dy
            loss_ref[...] += _fold8(diff * diff) * (0.5 / D_MODEL)
            dgpo_ref[...] += _fold8(de * uh)

        _by_chunks(tm, loss_and_back)
        dn3_s[...] = _dot_nt(dz_s[...], wpg_ref[...])

        def gate_norm_back(rows):
            x2h, r3 = _rms(x2_ref[rows, :])
            dn3 = dn3_s[rows, :]
            dx2_ref[rows, :] = dy_s[rows, :] + _rms_bwd(dn3 * gpg, x2h, r3)
            dgpg_ref[...] += _fold8(dn3 * x2h)

        _by_chunks(tm, gate_norm_back)
        dwpg_ref[...] += _dot_tn(n3_s[...], dz_s[...])
        for d in range(N_DEV):
            dwpp_ref[d] += _dot_tn(p_mm, du_s[:, d * HEAD_PAD:(d + 1) * HEAD_PAD])

    vec = ((8, D_MODEL), F32)
    wide = lambda dt: pltpu.VMEM((tm, D_MODEL), dt)
    return _row_call("ple_loss", body, T, tm, [x2, p, tgt], [g_pg, g_post, w_pg, w_pp], [(D_MODEL, F32)],
                     [vec, vec, vec, ((D_MODEL, D_MODEL), F32), ((N_DEV, PLE, HEAD_PAD), F32)], VMEM_LIMIT,
                     scratch=[wide(F32), wide(MM), wide(F32), wide(MM), wide(MM), wide(F32), wide(F32)])


def _ffn_bwd(dx2, x1, gt, up, g_ffn, w_g, w_u, w_d, T, tm):
    def body(i, dx2_ref, x1_ref, gt_ref, up_ref, g_ref, wg_ref, wu_ref, wd_ref,
             dx1_ref, a_ref, dgt_ref, dup_ref, dg_ref, da_s, dh2_s):
        @pl.when(i == 0)
        def _():
            dg_ref[...] = jnp.zeros_like(dg_ref)

        g = g_ref[...]
        da_s[...] = _dot_nt(dx2_ref[...], wd_ref[...])

        def act_back(rows):
            for cs in FFN_HALVES:
                up, da = up_ref[rows, cs], da_s[rows, cs]
                silu, dsilu = _silu_parts(gt_ref[rows, cs])
                dgt_ref[rows, cs] = (da * up * dsilu).astype(MM)
                dup_ref[rows, cs] = (da * silu).astype(MM)
                a_ref[rows, cs] = (silu * up).astype(MM)

        _by_chunks(tm, act_back)
        dh2_s[...] = (jnp.dot(dgt_ref[...], wg_ref[...], preferred_element_type=F32)
                      + jnp.dot(dup_ref[...], wu_ref[...], preferred_element_type=F32))

        def norm_back(rows):
            x1h, r = _rms(x1_ref[rows, :])
            dh2 = dh2_s[rows, :]
            dx1_ref[rows, :] = dx2_ref[rows, :] + _rms_bwd(dh2 * g, x1h, r)
            dg_ref[...] += _fold8(dh2 * x1h)

        _by_chunks(tm, norm_back)

    return _row_call("ffn_bwd", body, T, tm, [dx2, x1, gt, up], [g_ffn, w_g, w_u, w_d],
                     [(D_MODEL, F32), (FFN, MM), (FFN, MM), (FFN, MM)], [((8, D_MODEL), F32)], VMEM_LIMIT,
                     scratch=[pltpu.VMEM((tm, FFN), F32), pltpu.VMEM((tm, D_MODEL), F32)])


def _merge_bwd(dx1, ya, yb, bg, o, hg, attn, m, rec, g_out, w_bra, w_brb, w_out, T, tm, xchg=((), ())):
    def body(i, dx1_ref, ya_ref, yb_ref, bg_ref, o_ref, hg_ref, attn_ref, m_ref, rec_ref, g_ref, wa_ref, wb_ref, wo_ref,
             dattn_ref, do_ref, dhg_ref, dbg_ref, dg_ref, dwo_ref, dwa_ref, dwb_ref, dm_s, dya_s, dyb_s, drec_s):
        @pl.when(i == 0)
        def _():
            for ref in (dg_ref, dwo_ref, dwa_ref, dwb_ref):
                ref[...] = jnp.zeros_like(ref)

        g = g_ref[...]
        dx1 = dx1_ref[...].astype(MM)
        dm_s[...] = _dot_nt(dx1, wo_ref[...])

        def gate_back(rows):
            dm = dm_s[rows, :]
            ga, gb = _sigmoid(bg_ref[rows, :D_MODEL]), _sigmoid(bg_ref[rows, D_MODEL:])
            dya_s[rows, :] = (dm * ga).astype(MM)
            dyb_s[rows, :] = (dm * gb).astype(MM)
            dbg_ref[rows, :D_MODEL] = (dm * ya_ref[rows, :] * ga * (1.0 - ga)).astype(MM)
            dbg_ref[rows, D_MODEL:] = (dm * yb_ref[rows, :] * gb * (1.0 - gb)).astype(MM)

        _by_chunks(tm, gate_back)
        dwo_ref[...] += _dot_tn(m_ref[...], dx1)
        attn_mm = attn_ref[...].astype(MM)
        for d in range(N_DEV):
            ds = slice(d * HEAD_PAD, (d + 1) * HEAD_PAD)
            dwa_ref[d] += _dot_tn(attn_mm, dya_s[:, ds])
            dwb_ref[d] += _dot_tn(rec_ref[...], dyb_s[:, ds])
        dattn_ref[...] = _dot_nt(dya_s[...], wa_ref[...])
        drec_s[...] = _dot_nt(dyb_s[...], wb_ref[...])

        def recurrent_out_back(rows):
            for h in range(HG_HEADS):
                ls = slice(h * HG_DIM, (h + 1) * HG_DIM)
                oh, r = _rms(o_ref[rows, ls])
                silu, dsilu = _silu_parts(hg_ref[rows, ls])
                dr = drec_s[rows, ls]
                dhg_ref[rows, ls] = (dr * oh * g * dsilu).astype(MM)
                don = dr * silu
                dg_ref[...] += _fold8(don * oh)
                do_ref[rows, ls] = _rms_bwd(don * g, oh, r)

        _by_chunks(tm, recurrent_out_back)

    wide = lambda n, dt: pltpu.VMEM((tm, n), dt)
    return _row_call("merge_bwd", body, T, tm, [dx1, ya, yb, bg, o, hg, attn, m, rec], [g_out, w_bra, w_brb, w_out],
                     [(D_MODEL, F32), (HG_W, F32), (HG_W, MM), (2 * D_MODEL, MM)],
                     [((8, HG_DIM), F32), ((D_MODEL, D_MODEL), F32), ((N_DEV, MLA_HEADS * HEAD_PAD, HEAD_PAD), F32),
                      ((N_DEV, HG_W, HEAD_PAD), F32)], VMEM_LIMIT,
                     scratch=[wide(D_MODEL, F32), wide(D_MODEL, MM), wide(D_MODEL, MM), wide(HG_W, F32)], xchg=xchg)


def _flash_bwd(qf, kf, vf, o, do, lse, T, xchg=((), ())):
    tq = min(ATT_TILE, T)
    nq = T // tq

    qi_tab, ki_tab = _causal_pairs(nq, by_query=False)

    n_x, n_sib = _x_count(xchg), len(xchg[0])
    hp = ATT_HEADS
    n_heads, n_pairs = MLA_HEADS // hp, len(qi_tab)

    def body(qi_ref, ki_ref, q_ref, k_ref, v_ref, o_ref, do_ref, lse_ref, *rest):
        x_in, (dq_ref, dk_ref, dv_ref), rest = rest[:n_x], rest[n_x:n_x + 3], rest[n_x + 3:]
        x_out, x_sems = rest[:n_x], rest[n_x:]
        t = pl.program_id(1)
        qi, ki = qi_ref[t], ki_ref[t]
        if n_x:
            @pl.when((pl.program_id(0) == 0) & (t == 0))
            def _():
                for cp in _x_copies(n_sib, x_in, x_out, x_sems):
                    cp.start()

        @pl.when(t == 0)
        def _():
            dq_ref[...] = jnp.zeros_like(dq_ref)

        def step(first):
            rows = pl.ds(pl.multiple_of(qi * tq, tq), tq)
            for hh in range(hp):
                hs = slice(hh * HEAD_PAD, (hh + 1) * HEAD_PAD)
                q, k, d_o = q_ref[:, hs], k_ref[:, hs], do_ref[:, hs]
                s = _dot_nt(q, k)
                if first:
                    row = lax.broadcasted_iota(jnp.int32, (tq, tq), 0)
                    col = lax.broadcasted_iota(jnp.int32, (tq, tq), 1)
                    s = jnp.where(col <= row, s, NEG)
                p = jnp.exp(s - lse_ref[:, hh * HEAD_PAD:hh * HEAD_PAD + 1])
                delta = jnp.sum(d_o * o_ref[:, hs], axis=1, keepdims=True)
                ds = p * (_dot_nt(d_o, v_ref[:, hs]) - delta)
                dq_ref[rows, hs] += _dot(ds, k)
                if first:
                    dv_ref[:, hs] = _dot_tn(p, d_o)
                    dk_ref[:, hs] = _dot_tn(ds, q)
                else:
                    dv_ref[:, hs] += _dot_tn(p, d_o)
                    dk_ref[:, hs] += _dot_tn(ds, q)

        @pl.when(qi == ki)
        def _():
            step(True)

        @pl.when(qi > ki)
        def _():
            step(False)

        if n_x:
            @pl.when((pl.program_id(0) == n_heads - 1) & (t == n_pairs - 1))
            def _():
                for cp in _x_copies(n_sib, x_in, x_out, x_sems):
                    cp.wait()

    q_spec = pl.BlockSpec((tq, hp * HEAD_PAD), lambda h, t, qi_ref, ki_ref: (qi_ref[t], h))
    kv_spec = pl.BlockSpec((tq, hp * HEAD_PAD), lambda h, t, qi_ref, ki_ref: (ki_ref[t], h))
    any_spec = pl.BlockSpec(memory_space=pl.ANY)
    w = MLA_HEADS * HEAD_PAD
    grid_spec = pltpu.PrefetchScalarGridSpec(
        num_scalar_prefetch=2, grid=(n_heads, n_pairs),
        in_specs=[q_spec, kv_spec, kv_spec, q_spec, q_spec, q_spec] + [any_spec] * n_x,
        out_specs=[pl.BlockSpec((T, hp * HEAD_PAD), lambda h, t, qi_ref, ki_ref: (0, h)), kv_spec, kv_spec]
        + [any_spec] * n_x,
        scratch_shapes=_x_sems(xchg))
    return pl.pallas_call(
        body, name="flash_bwd", grid_spec=grid_spec,
        out_shape=[jax.ShapeDtypeStruct((T, w), F32)] * 3 + _x_out_shapes(xchg),
        compiler_params=_cparams(("arbitrary", "arbitrary")),
    )(jnp.asarray(qi_tab), jnp.asarray(ki_tab), qf, kf, vf, o, do, lse, *xchg[0], *xchg[1])


def _mla_heads_bwd(d_out, saved, g_pad, cos_t, sin_t, first):
    d_raw, dg = [], jnp.zeros((1, HEAD_PAD), F32)
    for h in range(MLA_HEADS):
        xh, r = saved[h]
        dy = d_out[:, h * HEAD_PAD:(h + 1) * HEAD_PAD]
        dn = dy * cos_t + _rope_swap(dy * sin_t, first)
        dg = dg + jnp.sum(dn * xh, axis=0, keepdims=True)
        d_raw.append(_rms_bwd(dn * g_pad, xh, r, QK_DIM))
    return d_raw, dg


def _mla_prep_bwd(cq, ckv, kr, pos, dqf, dkf, dvf, g_qa, g_kva, g_qn, g_kn, w_uq, w_ukv, T, tm):
    def body(i, cq_ref, ckv_ref, kr_ref, pos_ref, dq_ref, dk_ref, dv_ref,
             gqa_ref, gkva_ref, gqn_ref, gkn_ref, wuq_ref, wukv_ref,
             dcq_ref, dckv_ref, dkr_ref, dgqa_ref, dgkva_ref, dgqn_ref, dgkn_ref, dwuq_ref, dwukv_ref):
        cos_t, sin_t, first = _rope_tables(pos_ref[...], tm)
        cqh, rq = _rms(cq_ref[...])
        ckvh, rkv = _rms(ckv_ref[...])
        cqn, ckvn = cqh * gqa_ref[...], ckvh * gkva_ref[...]
        q_raw, k_raw, _ = _mla_raw_heads(cqn, ckvn, kr_ref[...], wuq_ref, wukv_ref, tm)
        _, q_saved = _mla_heads_fwd(q_raw, gqn_ref[...], cos_t, sin_t, first)
        _, k_saved = _mla_heads_fwd(k_raw, gkn_ref[...], cos_t, sin_t, first)
        dq_heads, dgqn = _mla_heads_bwd(dq_ref[...] * ATT_SCALE, q_saved, gqn_ref[...], cos_t, sin_t, first)
        dk_heads, dgkn = _mla_heads_bwd(dk_ref[...], k_saved, gkn_ref[...], cos_t, sin_t, first)
        lane = lax.broadcasted_iota(jnp.int32, (tm, HEAD_PAD), 1)
        nope = lane < QK_NOPE
        dcqn = jnp.zeros((tm, Q_RANK), F32)
        dckvn = jnp.zeros((tm, KV_RANK), F32)
        dkr = jnp.zeros((tm, HEAD_PAD), F32)
        cqn_mm, ckvn_mm = cqn.astype(MM), ckvn.astype(MM)
        for h in range(MLA_HEADS):
            hs = slice(h * HEAD_PAD, (h + 1) * HEAD_PAD)
            dq_h = dq_heads[h].astype(MM)
            dkv_h = jnp.where(nope, dk_heads[h], pltpu.roll(dv_ref[:, hs], V_DIM, 1)).astype(MM)
            _acc(dwuq_ref.at[h], i, _dot_tn(dq_h, cqn_mm))
            _acc(dwukv_ref.at[h], i, _dot_tn(ckvn_mm, dkv_h))
            dcqn = dcqn + jnp.dot(dq_h, wuq_ref[h], preferred_element_type=F32)
            dckvn = dckvn + lax.dot_general(dkv_h, wukv_ref[h], (((1,), (1,)), ((), ())), preferred_element_type=F32)
            dkr = dkr + dk_heads[h]
        dkr_ref[...] = jnp.where((lane >= QK_NOPE) & (lane < QK_DIM), dkr, 0.0).astype(MM)
        dcq_ref[...] = _rms_bwd(dcqn * gqa_ref[...], cqh, rq).astype(MM)
        dckv_ref[...] = _rms_bwd(dckvn * gkva_ref[...], ckvh, rkv).astype(MM)
        _acc(dgqa_ref, i, jnp.sum(dcqn * cqh, axis=0, keepdims=True))
        _acc(dgkva_ref, i, jnp.sum(dckvn * ckvh, axis=0, keepdims=True))
        _acc(dgqn_ref, i, dgqn)
        _acc(dgkn_ref, i, dgkn)

    return _row_call(
        "mla_prep_bwd", body, T, tm, [cq, ckv, kr, pos, dqf, dkf, dvf], [g_qa, g_kva, g_qn, g_kn, w_uq, w_ukv],
        [(Q_RANK, MM), (KV_RANK, MM), (HEAD_PAD, MM)],
        [((1, Q_RANK), F32), ((1, KV_RANK), F32), ((1, HEAD_PAD), F32), ((1, HEAD_PAD), F32),
         ((MLA_HEADS, HEAD_PAD, Q_RANK), F32), ((MLA_HEADS, KV_RANK, HEAD_PAD), F32)], VMEM_LIMIT)


def _in_proj_bwd(x, dx1, dsecs, g_mix, w_in, T, tm):
    def body(i, x_ref, dx1_ref, *rest):
        d_refs, (g_ref, w_ref, dx_ref, dp_ref, dg_ref, dh_s) = rest[:len(SECTIONS)], rest[len(SECTIONS):]

        @pl.when(i == 0)
        def _():
            dg_ref[...] = jnp.zeros_like(dg_ref)

        g = g_ref[...]

        def join_and_cut(rows):
            pieces = [(d_ref[rows, QK_NOPE:QK_DIM] if n == QK_ROPE else d_ref[rows, :]).astype(F32)
                      for (_, n), d_ref in zip(COL_SECTIONS, d_refs)]
            dproj = jnp.concatenate(pieces, axis=1)
            for d in range(N_DEV):
                dp_ref[d, rows, :] = dproj[:, d * IN_BLOCK:(d + 1) * IN_BLOCK].astype(MM)

        _by_chunks(tm, join_and_cut)
        dh = jnp.dot(dp_ref[0], w_ref[0], preferred_element_type=F32)
        for d in range(1, N_DEV):
            dh = dh + jnp.dot(dp_ref[d], w_ref[d], preferred_element_type=F32)
        dh_s[...] = dh

        def norm_back(rows):
            xh, r = _rms(x_ref[rows, :])
            dh_c = dh_s[rows, :]
            dx_ref[rows, :] = dx1_ref[rows, :] + _rms_bwd(dh_c * g, xh, r)
            dg_ref[...] += _fold8(dh_c * xh)

        _by_chunks(tm, norm_back)

    in_specs = [pl.BlockSpec((tm, a.shape[1]), lambda i: (i, 0)) for a in [x, dx1, *dsecs]]
    in_specs += [pl.BlockSpec(g_mix.shape, lambda i: (0, 0)),
                 pl.BlockSpec(w_in.shape, lambda i: (0, 0, 0), pipeline_mode=pl.Buffered(1))]

    def kern(*refs):
        body(pl.program_id(0), *refs)

    return pl.pallas_call(
        kern, name="in_proj_bwd", grid=(T // tm,), in_specs=in_specs,
        out_specs=[pl.BlockSpec((tm, D_MODEL), lambda i: (i, 0)),
                   pl.BlockSpec((N_DEV, tm, IN_BLOCK), lambda i: (0, i, 0)),
                   pl.BlockSpec((8, D_MODEL), lambda i: (0, 0))],
        out_shape=[jax.ShapeDtypeStruct((T, D_MODEL), F32), jax.ShapeDtypeStruct((N_DEV, T, IN_BLOCK), MM),
                   jax.ShapeDtypeStruct((8, D_MODEL), F32)],
        scratch_shapes=[pltpu.VMEM((tm, D_MODEL), F32)],
        compiler_params=_cparams(("arbitrary",), VMEM_LIMIT),
    )(x, dx1, *dsecs, g_mix, w_in)


def _pick_block(n, cap):
    best = None
    for cand in range(128, min(n, cap) + 1, 128):
        if n % cand == 0:
            best = cand
    return n if best is None else best


def _pick_rows(n, cap):
    best = n
    for cand in range(8, min(n, cap) + 1, 8):
        if n % cand == 0:
            best = cand
    return best


def _matmul_tn(name, a, b):
    T, M = a.shape
    N = b.shape[1]
    bm, bk = _pick_block(M, 1408), min(512, T)
    bn = _pick_block(N, 2560)

    def body(a_ref, b_ref, c_ref):
        @pl.when(pl.program_id(2) == 0)
        def _():
            c_ref[...] = jnp.zeros_like(c_ref)

        c_ref[...] += _dot_tn(a_ref[...], b_ref[...])

    return pl.pallas_call(
        body, name=name, grid=(M // bm, N // bn, T // bk),
        in_specs=[pl.BlockSpec((bk, bm), lambda i, j, k: (k, i)), pl.BlockSpec((bk, bn), lambda i, j, k: (k, j))],
        out_specs=pl.BlockSpec((bm, bn), lambda i, j, k: (i, j)), out_shape=jax.ShapeDtypeStruct((M, N), F32),
        compiler_params=_cparams(("parallel", "parallel", "arbitrary"), VMEM_LIMIT),
    )(a, b)


def _matmul_tn_blocks(name, a, b):
    T, M = a.shape
    nd, _, c = b.shape
    bm, bk = _pick_block(M, 512), min(512, T)

    def body(a_ref, b_ref, c_ref):
        @pl.when(pl.program_id(1) == 0)
        def _():
            c_ref[...] = jnp.zeros_like(c_ref)

        a_blk = a_ref[...].astype(MM)
        for d in range(nd):
            c_ref[d] += _dot_tn(b_ref[d], a_blk)

    return pl.pallas_call(
        body, name=name, grid=(M // bm, T // bk),
        in_specs=[pl.BlockSpec((bk, bm), lambda i, k: (k, i)), pl.BlockSpec((nd, bk, c), lambda i, k: (0, k, 0))],
        out_specs=pl.BlockSpec((nd, c, bm), lambda i, k: (0, 0, i)),
        out_shape=jax.ShapeDtypeStruct((nd, c, M), F32),
        compiler_params=_cparams(("parallel", "arbitrary"), VMEM_LIMIT),
    )(a, b)


def _pad_gain(g, n):
    return jnp.pad(g.reshape(1, -1), ((0, 0), (0, n - g.shape[-1])))


GROUP_A = ("w_ffn_gate", "w_ffn_up", "w_ffn_down", "w_ple_gate", "w_ple_proj")
GROUP_B = ("w_branch", "w_out")
GROUP_C = ("w_in", "w_uq", "w_ukv")
EARLY = GROUP_C
LATE = GROUP_B + GROUP_A
TRANSPOSED = ("w_in", "w_uq", "w_ffn_gate", "w_ffn_up")


def _local_step(x, p, pos, tgt, small, big, late_blocks=None, core=None):
    T = x.shape[0]
    tm = min(ROW_TILE, T)
    w_in = big["w_in"]
    w_uq = jnp.pad(big["w_uq"], ((0, 0), (0, HEAD_PAD - QK_DIM), (0, 0)))
    w_ukv = big["w_ukv"]

    g_mix, g_qa, g_kva = small["mix_norm_g"], small["q_a_norm_g"], small["kv_a_norm_g"]
    g_qn, g_kn = _pad_gain(small["q_norm_g"], HEAD_PAD), _pad_gain(small["k_norm_g"], HEAD_PAD)
    g_out, g_ffn = small["hg_out_norm_g"], small["ffn_norm_g"]
    g_pg, g_post = small["ple_gate_norm_g"], small["ple_post_norm_g"]
    logits = small["hg_lb_logits"]
    lb = _lower_bound(logits)

    h, cq, ckv, kr, hq, hf, hi, hg, bg = _in_proj_fwd(x, g_mix, w_in, T, tm)
    qf, kf, vf = _mla_prep_fwd(cq, ckv, kr, pos, g_qa, g_kva, g_qn, g_kn, w_uq, w_ukv, T, tm)
    if late_blocks is None:
        attn, lse = _flash_fwd(qf, kf, vf, T)
    else:
        attn, lse, *late = _flash_fwd(qf, kf, vf, T, ag_blocks=[late_blocks[n] for n in LATE])
        big = {**big, **dict(zip(LATE, late))}
    w_branch = jnp.moveaxis(big["w_branch"].reshape(N_DEV, 2, HG_W, HEAD_PAD), 0, 2).reshape(2, HG_W, D_MODEL)
    w_bra = jnp.pad(w_branch[0].reshape(MLA_HEADS, V_DIM, D_MODEL),
                    ((0, 0), (0, HEAD_PAD - V_DIM), (0, 0))).reshape(MLA_HEADS * HEAD_PAD, D_MODEL)
    w_brb = w_branch[1]
    w_out = big["w_out"].reshape(D_MODEL, D_MODEL)
    w_g, w_u = big["w_ffn_gate"].reshape(FFN, D_MODEL), big["w_ffn_up"].reshape(FFN, D_MODEL)
    w_d = big["w_ffn_down"].reshape(FFN, D_MODEL)
    w_pg, w_pp = big["w_ple_gate"].reshape(D_MODEL, D_MODEL), big["w_ple_proj"]
    o, s0 = _hgrn_fwd(hq, hf, hi, lb, T)
    x1, ya, yb, m, rec = _merge_fwd(attn, o, hg, bg, x, g_out, w_bra, w_brb, w_out, T, tm)
    x2, gt, up, h2 = _ffn_fwd(x1, g_ffn, w_g, w_u, w_d, T, tm)
    dx2, loss_p, dg_post, dg_pg, d_pg, d_pp = _ple_loss(x2, p, tgt, g_pg, g_post, w_pg, w_pp, T, tm)
    dg_post, dg_pg = (jnp.sum(t, axis=0, keepdims=True) for t in (dg_post, dg_pg))

    grads, sibs, gots = {}, {}, {}
    dist = core is not None
    pick = lambda names: [grads[n] for n in names] if dist else ()

    def partials(tag, names, got):
        if not dist:
            return ()
        sibs.update(zip(names, got))
        return _chip_partials("rs_partial_" + tag, pick(names), got, core)

    dx1, a, dgt, dup, dg_ffn = _ffn_bwd(dx2, x1, gt, up, g_ffn, w_g, w_u, w_d, T, tm)
    dg_ffn = jnp.sum(dg_ffn, axis=0, keepdims=True)
    grads["w_ffn_gate"] = _matmul_tn("dw_gate", dgt, h2).reshape(N_DEV, -1, D_MODEL)
    grads["w_ffn_up"] = _matmul_tn("dw_up", dup, h2).reshape(N_DEV, -1, D_MODEL)
    grads["w_ffn_down"] = _matmul_tn("dw_down", a, dx2).reshape(N_DEV, -1, D_MODEL)
    grads["w_ple_gate"] = d_pg.reshape(N_DEV, -1, D_MODEL)
    grads["w_ple_proj"] = d_pp

    dattn, do, dhg, dbg, dg_out, d_out, d_bra, d_brb, *sib_a = _merge_bwd(
        dx1, ya, yb, bg, o, hg, attn, m, rec, g_out, w_bra, w_brb, w_out, T, tm, xchg=(pick(GROUP_A), ()))
    parts_a = partials("a", GROUP_A, sib_a)
    dg_out = jnp.sum(dg_out, axis=0, keepdims=True)
    d_bra = d_bra.reshape(N_DEV, MLA_HEADS, HEAD_PAD, HEAD_PAD)[:, :, :V_DIM].reshape(N_DEV, HG_W, HEAD_PAD)
    grads["w_branch"] = jnp.concatenate([d_bra, d_brb], axis=1)
    grads["w_out"] = d_out.reshape(N_DEV, -1, D_MODEL)

    dhq, dhf, dhi, dlb, *got = _hgrn_bwd(hq, hf, hi, do, s0, lb, T, xchg=(pick(GROUP_B), parts_a))
    sib_b, got_a = got[:len(GROUP_B)], got[len(GROUP_B):]
    parts_b = partials("b", GROUP_B, sib_b)
    dqf, dkf, dvf, *got_b = _flash_bwd(qf, kf, vf, attn, dattn, lse, T, xchg=((), parts_b))
    (dcq, dckv, dkr, dg_qa, dg_kva, dg_qn, dg_kn, d_uq, d_ukv) = _mla_prep_bwd(
        cq, ckv, kr, pos, dqf, dkf, dvf, g_qa, g_kva, g_qn, g_kn, w_uq, w_ukv, T, tm)
    grad_x, dproj, dg_mix = _in_proj_bwd(x, dx1, [dcq, dckv, dkr, dhq, dhf, dhi, dhg, dbg], g_mix, w_in, T, tm)
    dg_mix = jnp.sum(dg_mix, axis=0, keepdims=True)
    grads["w_in"] = _matmul_tn_blocks("dw_in", h, dproj)
    grads["w_uq"] = d_uq[:, :QK_DIM]
    grads["w_ukv"] = d_ukv
    parts_c = ()
    if dist:
        parts_c = partials("c", GROUP_C, _exchange_sibling("rs_sibling_c", pick(GROUP_C)))
        gots.update(zip(GROUP_A, got_a))
        gots.update(zip(GROUP_B, got_b))

    dl0 = dlb * lb * (1.0 - lb)
    small_g = {
        "mix_norm_g": dg_mix, "q_a_norm_g": dg_qa, "kv_a_norm_g": dg_kva,
        "q_norm_g": dg_qn[:, :QK_DIM], "k_norm_g": dg_kn[:, :QK_DIM],
        "hg_lb_logits": jnp.concatenate([dl0, -dl0], axis=0), "hg_out_norm_g": dg_out,
        "ffn_norm_g": dg_ffn, "ple_gate_norm_g": dg_pg, "ple_post_norm_g": dg_post,
    }
    return loss_p, grad_x, small_g, grads, sibs, gots, parts_c


def _lower_bound(logits):
    def body(l_ref, lb_ref):
        l = l_ref[...]
        mx = jnp.max(l, axis=0, keepdims=True)
        e = jnp.exp(l - mx)
        lb_ref[...] = e[0:1] / jnp.sum(e, axis=0, keepdims=True)

    return pl.pallas_call(body, name="lower_bound", out_shape=jax.ShapeDtypeStruct((1, HG_W), F32))(logits)


def _my_place():
    return lax.axis_index("x"), lax.axis_index("y"), lax.axis_index("c")


def _all_gather(name, blocks):
    n = len(blocks)

    def body(*refs):
        x_refs, out_refs, sems = refs[:n], refs[n:2 * n], refs[2 * n:]
        _ag_start(x_refs, out_refs, sems)
        _ag_finish(x_refs, out_refs, sems)

    any_spec = pl.BlockSpec(memory_space=pl.ANY)
    return pl.pallas_call(
        body, name=name, out_shape=_ag_out_shapes(blocks),
        in_specs=[any_spec] * n, out_specs=[any_spec] * n, scratch_shapes=_ag_sems(n),
    )(*blocks)


def _ag_out_shapes(blocks):
    return [jax.ShapeDtypeStruct((N_DEV,) + b.shape, b.dtype) for b in blocks]


def _ag_sems(n):
    return [pltpu.SemaphoreType.DMA((7 * n,)), pltpu.SemaphoreType.DMA((7 * n,)), pltpu.SemaphoreType.DMA((n,))]


def _ag_parts(x_refs, out_refs, sems):
    send_sems, recv_sems, local_sems = sems
    x, y, c = _my_place()
    me, sibling = (x, y, c), (x, y, 1 - c)
    chips = [(1 - x, y), (x, 1 - y), (1 - x, 1 - y)]
    n = len(x_refs)

    def copy(a, k, block, to, own=False):
        px, py, pc = block
        dst = out_refs[a].at[4 * px + 2 * py + pc]
        return pltpu.make_async_remote_copy(
            src_ref=x_refs[a] if own else dst, dst_ref=dst, send_sem=send_sems.at[7 * a + k],
            recv_sem=recv_sems.at[7 * a + k], device_id=to, device_id_type=MESH_ID)

    mine = [pltpu.make_async_copy(x_refs[a], out_refs[a].at[4 * x + 2 * y + c], local_sems.at[a]) for a in range(n)]
    first = []
    for a in range(n):
        first.append(copy(a, 0, me, sibling, own=True))
        first += [copy(a, 1 + j, me, (*chip, c), own=True) for j, chip in enumerate(chips)]
    return copy, mine, first, me, sibling, chips, c, n


def _ag_start(x_refs, out_refs, sems):
    _, mine, first, *_ = _ag_parts(x_refs, out_refs, sems)
    for cp in mine + first:
        cp.start()


def _ag_finish(x_refs, out_refs, sems):
    copy, mine, first, me, sibling, chips, c, n = _ag_parts(x_refs, out_refs, sems)
    passed = []
    for j, chip in enumerate(chips):
        for a in range(n):
            copy(a, 1 + j, (*chip, c), me).wait_recv()
            passed.append(copy(a, 4 + j, (*chip, c), sibling))
            passed[-1].start()
    for a in range(n):
        copy(a, 0, sibling, me).wait_recv()
    for j, chip in enumerate(chips):
        for a in range(n):
            copy(a, 4 + j, (*chip, 1 - c), me).wait_recv()
    for cp in first + passed:
        cp.wait_send()
    for cp in mine:
        cp.wait()


def _exchange_sibling(name, gs):
    return _exchange(name, (gs, ()))


def _exchange(name, xchg):
    n = _x_count(xchg)

    def body(*refs):
        in_refs, out_refs, sems = refs[:n], refs[n:2 * n], refs[2 * n:]
        for cp in _x_copies(len(xchg[0]), in_refs, out_refs, sems):
            cp.start()
        for cp in _x_copies(len(xchg[0]), in_refs, out_refs, sems):
            cp.wait()

    any_spec = pl.BlockSpec(memory_space=pl.ANY)
    return pl.pallas_call(
        body, name=name, out_shape=_x_out_shapes(xchg), in_specs=[any_spec] * n, out_specs=[any_spec] * n,
        scratch_shapes=_x_sems(xchg),
    )(*xchg[0], *xchg[1])


N_PARTS = 4


def _part_spec(rows, cols, t_pos, lead_block=(), lead_index=lambda *args: ()):
    if rows % (16 * N_PARTS) == 0:
        axis, shape, count = 0, (rows // N_PARTS, cols), N_PARTS
    elif cols % (128 * N_PARTS) == 0:
        axis, shape, count = 1, (rows, cols // N_PARTS), N_PARTS
    else:
        axis, shape, count = 0, (rows, cols), 1

    def index(*args):
        i = jnp.minimum(args[t_pos], count - 1)
        return (*lead_index(*args), *((i, 0) if axis == 0 else (0, i)))

    return pl.BlockSpec((*lead_block, *shape), index)


def _chip_partials(name, gs, sibs, c_idx):
    n = len(gs)

    def body(c_ref, *refs):
        for g_ref, sib_ref, out_ref in zip(refs[:n], refs[n:2 * n], refs[2 * n:]):
            out_ref[...] = (g_ref[...] + sib_ref[...]).astype(MM)

    own = [_part_spec(*g.shape[1:], 1, (1,), lambda j, t, c_ref: (2 * j + c_ref[0],)) for g in gs]
    by_chip = [_part_spec(*g.shape[1:], 1, (1,), lambda j, t, c_ref: (j,)) for g in gs]
    grid_spec = pltpu.PrefetchScalarGridSpec(
        num_scalar_prefetch=1, grid=(4, N_PARTS), in_specs=own + by_chip, out_specs=by_chip)
    return pl.pallas_call(
        body, name=name, grid_spec=grid_spec, out_shape=[jax.ShapeDtypeStruct((4,) + g.shape[1:], MM) for g in gs],
        compiler_params=_cparams(("arbitrary", "arbitrary"), VMEM_LIMIT),
    )(c_idx, *gs, *sibs)


def _exchange_chips(parts):
    return _exchange("rs_chips", ((), parts))


def _x_count(xchg):
    return len(xchg[0]) + len(xchg[1])


def _x_out_shapes(xchg):
    return ([jax.ShapeDtypeStruct((4,) + g.shape[1:], g.dtype) for g in xchg[0]]
            + [jax.ShapeDtypeStruct((3,) + p.shape[1:], p.dtype) for p in xchg[1]])


def _x_sems(xchg):
    n = 4 * len(xchg[0]) + 3 * len(xchg[1])
    return [pltpu.SemaphoreType.DMA((n,)), pltpu.SemaphoreType.DMA((n,))] if n else []


def _x_copies(n_sib, in_refs, out_refs, sems):
    if not in_refs:
        return []
    send_sems, recv_sems = sems
    x, y, c = _my_place()
    chips = [(1 - x, y), (x, 1 - y), (1 - x, 1 - y)]
    copies = []

    def add(src, dst, to):
        k = len(copies)
        copies.append(pltpu.make_async_remote_copy(
            src_ref=src, dst_ref=dst, send_sem=send_sems.at[k], recv_sem=recv_sems.at[k], device_id=to,
            device_id_type=MESH_ID))

    for a, (src, dst) in enumerate(zip(in_refs, out_refs)):
        if a < n_sib:
            for j in range(4):
                add(src.at[2 * j + 1 - c], dst.at[j], (x, y, 1 - c))
        else:
            for k, (px, py) in enumerate(chips):
                add(src.at[2 * px + py], dst.at[k], (px, py, c))
    return copies


def _adamw_math(w, g, m, v):
    m = ADAM_B1 * m + (1.0 - ADAM_B1) * g
    v = ADAM_B2 * v + (1.0 - ADAM_B2) * jnp.square(g)
    m_hat = m / (1.0 - ADAM_B1 ** ADAM_STEP)
    v_hat = v / (1.0 - ADAM_B2 ** ADAM_STEP)
    delta = -ADAM_LR * (m_hat / (jnp.sqrt(v_hat) + ADAM_EPS) + ADAM_WD * w)
    return delta, m, v


def _sum_adamws(name, gs, sibs, gots, ws, ms, vs, slot_idx, chip_idx, xchg=((), ())):
    n, n_x = len(gs), _x_count(xchg)

    def body(s_ref, j_ref, *refs):
        ins, x_in, refs = refs[:6 * n], refs[6 * n:6 * n + n_x], refs[6 * n + n_x:]
        outs, x_out, x_sems = refs[:4 * n], refs[4 * n:4 * n + n_x], refs[4 * n + n_x:]
        if n_x:
            @pl.when(pl.program_id(0) == 0)
            def _():
                for cp in _x_copies(len(xchg[0]), x_in, x_out, x_sems):
                    cp.start()

        for a in range(n):
            g_ref, sib_ref, got_ref, w_ref, m_ref, v_ref = (ins[k * n + a] for k in range(6))
            go_ref, d_ref, m2_ref, v2_ref = outs[4 * a:4 * a + 4]
            grad = g_ref[0] + sib_ref[0]
            for k in range(3):
                grad = grad + got_ref[k].astype(F32)
            go_ref[...] = grad
            d_ref[...], m2_ref[...], v2_ref[...] = _adamw_math(w_ref[...], grad, m_ref[...], v_ref[...])

        if n_x:
            @pl.when(pl.program_id(0) == N_PARTS - 1)
            def _():
                for cp in _x_copies(len(xchg[0]), x_in, x_out, x_sems):
                    cp.wait()

    shapes = [g.shape[1:] for g in gs]
    flat = [_part_spec(*s, 0) for s in shapes]
    any_spec = pl.BlockSpec(memory_space=pl.ANY)
    in_specs = ([_part_spec(*s, 0, (1,), lambda t, s_ref, j_ref: (s_ref[0],)) for s in shapes]
                + [_part_spec(*s, 0, (1,), lambda t, s_ref, j_ref: (j_ref[0],)) for s in shapes]
                + [_part_spec(*s, 0, (3,), lambda t, s_ref, j_ref: (0,)) for s in shapes] + flat * 3)
    grid_spec = pltpu.PrefetchScalarGridSpec(
        num_scalar_prefetch=2, grid=(N_PARTS,), in_specs=in_specs + [any_spec] * n_x,
        out_specs=[f for f in flat for _ in range(4)] + [any_spec] * n_x, scratch_shapes=_x_sems(xchg))
    res = pl.pallas_call(
        body, name=name, grid_spec=grid_spec,
        out_shape=[jax.ShapeDtypeStruct(s, F32) for s in shapes for _ in range(4)] + _x_out_shapes(xchg),
        compiler_params=_cparams(("arbitrary",), VMEM_LIMIT),
    )(slot_idx, chip_idx, *gs, *sibs, *gots, *ws, *ms, *vs, *xchg[0], *xchg[1])
    return [res[4 * a:4 * a + 4] for a in range(n)], res[4 * n:]


def _adamw_small(parts, w, m, v):
    rows = w.shape[0]

    def body(p_ref, w_ref, m_ref, v_ref, g_ref, d_ref, m2_ref, v2_ref):
        g = p_ref[0]
        for d in range(1, N_DEV):
            g = g + p_ref[d]
        g_ref[...] = g
        d_ref[...], m2_ref[...], v2_ref[...] = _adamw_math(w_ref[...], g, m_ref[...], v_ref[...])

    return pl.pallas_call(
        body, name="adamw_small", out_shape=[jax.ShapeDtypeStruct((rows, 128), F32)] * 4,
    )(parts, w, m, v)


BIG = ("w_in", "w_uq", "w_ukv", "w_branch", "w_out", "w_ffn_gate", "w_ffn_up", "w_ffn_down", "w_ple_gate", "w_ple_proj")
SMALL = (
    ("mix_norm_g", 1024), ("q_a_norm_g", 384), ("kv_a_norm_g", 256), ("q_norm_g", 96), ("k_norm_g", 96),
    ("hg_lb_logits", 1024), ("hg_out_norm_g", 128), ("ffn_norm_g", 1024), ("ple_gate_norm_g", 1024),
    ("ple_post_norm_g", 1024),
)
SMALL_ROWS = 56


def _pack_small(vals):
    rows = []
    for name, n in SMALL:
        v = vals[name].reshape(1, -1).astype(F32)
        rows.append(jnp.pad(v, ((0, 0), (0, (-n) % 128))).reshape(-1, 128))
    return jnp.concatenate(rows, axis=0)


def _unpack_small(packed, shapes):
    out, r = {}, 0
    for name, n in SMALL:
        k = (n + 127) // 128
        out[name] = packed[r:r + k].reshape(1, -1)[:, :n].reshape(shapes[name])
        r += k
    return out


_WEIGHTS = ["mix_norm_g", "w_in", "q_a_norm_g", "w_uq", "kv_a_norm_g", "w_ukv", "q_norm_g", "k_norm_g", "hg_lb_logits",
            "hg_out_norm_g", "w_branch", "w_out", "ffn_norm_g", "w_ffn_gate", "w_ffn_up", "w_ffn_down",
            "ple_gate_norm_g", "w_ple_gate", "w_ple_proj", "ple_post_norm_g"]


def _step(x, p, positions, tgt, w, m, v):
    small_names = [n for n, _ in SMALL]
    T = x.shape[1]
    px, py, pc = _my_place()
    as_idx = lambda t: jnp.reshape(t, (1,)).astype(jnp.int32)

    def two_d(n, t):
        t = t.reshape(-1, t.shape[-1])
        return t.T if n in TRANSPOSED else t

    def full_shape(n, t):
        return (t.T if n in TRANSPOSED else t).reshape(w[n].shape)

    blocks = {n: two_d(n, w[n]).astype(MM) for n in BIG}
    big = dict(zip(EARLY, _all_gather("ag_weights", [blocks[n] for n in EARLY])))
    small = {n: (w[n] if n == "hg_lb_logits" else w[n].reshape(1, -1)) for n in small_names}

    loss_p, grad_x, small_g, grads, sibs, gots, parts_c = _local_step(
        x[0], p[0, 0], positions.reshape(T, 1), tgt[0], small, big, late_blocks=blocks, core=as_idx(pc))

    out_g, out_d, out_m, out_v = {}, {}, {}, {}
    for tag, names, xchg in (("ab", GROUP_A + GROUP_B, ((), parts_c)), ("c", GROUP_C, ((), ()))):
        pick = lambda table: [table[n] for n in names]
        res, got_c = _sum_adamws("adamw_" + tag, pick(grads), pick(sibs), pick(gots), [two_d(n, w[n]) for n in names],
                                 [two_d(n, m[n]) for n in names], [two_d(n, v[n]) for n in names],
                                 as_idx(4 * px + 2 * py + pc), as_idx(2 * px + py), xchg=xchg)
        gots.update(zip(GROUP_C, got_c))
        for n, r in zip(names, res):
            out_g[n], out_d[n], out_m[n], out_v[n] = [full_shape(n, t) for t in r]

    packed_g = _pack_small(small_g)
    loss_row = jnp.concatenate([jnp.pad(jnp.sum(loss_p).reshape(1, 1), ((0, 0), (0, 127))),
                                jnp.zeros((SMALL_ROWS - packed_g.shape[0] - 1, 128), F32)], axis=0)
    parts = _all_gather("ag_small", [jnp.concatenate([packed_g, loss_row], axis=0)])[0]
    pad_rows = lambda t: jnp.pad(t, ((0, SMALL_ROWS - t.shape[0]), (0, 0)))
    sw = pad_rows(_pack_small({n: w[n] for n in small_names}))
    sm = pad_rows(_pack_small({n: m[n] for n in small_names}))
    sv = pad_rows(_pack_small({n: v[n] for n in small_names}))
    g_s, d_s, m_s, v_s = _adamw_small(parts, sw, sm, sv)
    shapes = {n: w[n].shape for n in small_names}
    n_packed = packed_g.shape[0]
    loss = g_s[n_packed, 0]
    for src, dst in ((g_s, out_g), (d_s, out_d), (m_s, out_m), (v_s, out_v)):
        dst.update(_unpack_small(src, shapes))

    outs = [loss, grad_x[None]]
    for table in (out_g, out_d, out_m, out_v):
        outs += [table[n] for n in _WEIGHTS]
    return tuple(outs)


def kernel(x, p, positions, mix_norm_g, w_in, q_a_norm_g, w_uq, kv_a_norm_g, w_ukv, q_norm_g, k_norm_g, hg_lb_logits, hg_out_norm_g, w_branch, w_out, ffn_norm_g, w_ffn_gate, w_ffn_up, w_ffn_down, ple_gate_norm_g, w_ple_gate, w_ple_proj, ple_post_norm_g, loss_target, m_mix_norm_g, m_w_in, m_q_a_norm_g, m_w_uq, m_kv_a_norm_g, m_w_ukv, m_q_norm_g, m_k_norm_g, m_hg_lb_logits, m_hg_out_norm_g, m_w_branch, m_w_out, m_ffn_norm_g, m_w_ffn_gate, m_w_ffn_up, m_w_ffn_down, m_ple_gate_norm_g, m_w_ple_gate, m_w_ple_proj, m_ple_post_norm_g, v_mix_norm_g, v_w_in, v_q_a_norm_g, v_w_uq, v_kv_a_norm_g, v_w_ukv, v_q_norm_g, v_k_norm_g, v_hg_lb_logits, v_hg_out_norm_g, v_w_branch, v_w_out, v_ffn_norm_g, v_w_ffn_gate, v_w_ffn_up, v_w_ffn_down, v_ple_gate_norm_g, v_w_ple_gate, v_w_ple_proj, v_ple_post_norm_g):
    w = dict(mix_norm_g=mix_norm_g, w_in=w_in, q_a_norm_g=q_a_norm_g, w_uq=w_uq, kv_a_norm_g=kv_a_norm_g, w_ukv=w_ukv,
             q_norm_g=q_norm_g, k_norm_g=k_norm_g, hg_lb_logits=hg_lb_logits, hg_out_norm_g=hg_out_norm_g,
             w_branch=w_branch, w_out=w_out, ffn_norm_g=ffn_norm_g, w_ffn_gate=w_ffn_gate, w_ffn_up=w_ffn_up,
             w_ffn_down=w_ffn_down, ple_gate_norm_g=ple_gate_norm_g, w_ple_gate=w_ple_gate, w_ple_proj=w_ple_proj,
             ple_post_norm_g=ple_post_norm_g)
    m = dict(mix_norm_g=m_mix_norm_g, w_in=m_w_in, q_a_norm_g=m_q_a_norm_g, w_uq=m_w_uq, kv_a_norm_g=m_kv_a_norm_g,
             w_ukv=m_w_ukv, q_norm_g=m_q_norm_g, k_norm_g=m_k_norm_g, hg_lb_logits=m_hg_lb_logits,
             hg_out_norm_g=m_hg_out_norm_g, w_branch=m_w_branch, w_out=m_w_out, ffn_norm_g=m_ffn_norm_g,
             w_ffn_gate=m_w_ffn_gate, w_ffn_up=m_w_ffn_up, w_ffn_down=m_w_ffn_down,
             ple_gate_norm_g=m_ple_gate_norm_g, w_ple_gate=m_w_ple_gate, w_ple_proj=m_w_ple_proj,
             ple_post_norm_g=m_ple_post_norm_g)
    v = dict(mix_norm_g=v_mix_norm_g, w_in=v_w_in, q_a_norm_g=v_q_a_norm_g, w_uq=v_w_uq, kv_a_norm_g=v_kv_a_norm_g,
             w_ukv=v_w_ukv, q_norm_g=v_q_norm_g, k_norm_g=v_k_norm_g, hg_lb_logits=v_hg_lb_logits,
             hg_out_norm_g=v_hg_out_norm_g, w_branch=v_w_branch, w_out=v_w_out, ffn_norm_g=v_ffn_norm_g,
             w_ffn_gate=v_w_ffn_gate, w_ffn_up=v_w_ffn_up, w_ffn_down=v_w_ffn_down,
             ple_gate_norm_g=v_ple_gate_norm_g, w_ple_gate=v_w_ple_gate, w_ple_proj=v_w_ple_proj,
             ple_post_norm_g=v_ple_post_norm_g)
    return _step(x, p, positions, loss_target, w, m, v)
```

```python
import functools

import jax
import jax.numpy as jnp
import numpy as np
from jax import lax
from jax.experimental import pallas as pl
from jax.experimental.pallas import tpu as pltpu

F32 = jnp.float32
MM = jnp.bfloat16
HI = lax.Precision.HIGHEST
MESH_ID = pl.DeviceIdType.MESH

D_MODEL = 1024
N_DEV = 8
MLA_HEADS = 8
QK_NOPE = 64
QK_ROPE = 32
QK_DIM = 96
V_DIM = 64
HEAD_PAD = 128
Q_RANK = 384
KV_RANK = 256
ROPE_BASE = 10000.0
HG_HEADS = 4
HG_DIM = 128
HG_W = 512
HG_CHUNK = 64
FFN = 2816
PLE = 256
EPS = 1e-6
ATT_SCALE = QK_DIM ** -0.5
NEG = -1e30

ADAM_LR = 0.001
ADAM_B1 = 0.9
ADAM_B2 = 0.999
ADAM_EPS = 1e-08
ADAM_WD = 0.01
ADAM_STEP = 10

SEC_CQ = (0, 384)
SEC_CKV = (384, 256)
SEC_KR = (640, 128)
SEC_HQ = (768, 512)
SEC_HF = (1280, 512)
SEC_HI = (1792, 512)
SEC_HG = (2304, 512)
SEC_BG = (2816, 2048)
IN_PAD = 4864
SECTIONS = (SEC_CQ, SEC_CKV, SEC_KR, SEC_HQ, SEC_HF, SEC_HI, SEC_HG, SEC_BG)
COL_SECTIONS = ((0, 384), (384, 256), (640, 32), (672, 512), (1184, 512), (1696, 512), (2208, 512), (2720, 2048))
IN_COLS = 4768
IN_BLOCK = IN_COLS // 8

VMEM_LIMIT = 58 * 1024 * 1024
ROW_TILE = 256
ATT_TILE = 1024
ATT_HEADS = 2
HG_BLOCK = 512
HG_UNROLL = 4


def _dot(a, b):
    return jnp.dot(a.astype(MM), b.astype(MM), preferred_element_type=F32)


def _dot_nt(a, b):
    return lax.dot_general(a.astype(MM), b.astype(MM), (((1,), (1,)), ((), ())), preferred_element_type=F32)


def _dot_tn(a, b):
    return lax.dot_general(a.astype(MM), b.astype(MM), (((0,), (0,)), ((), ())), preferred_element_type=F32)


def _dot_hi(a, b):
    return jnp.dot(a, b, preferred_element_type=F32, precision=HI)


def _sigmoid(x):
    return 1.0 / (1.0 + jnp.exp(-x))


def _rms(x, n=None):
    n = x.shape[-1] if n is None else n
    r = lax.rsqrt(jnp.sum(x * x, axis=-1, keepdims=True) * (1.0 / n) + EPS)
    return x * r, r


def _rms_bwd(dxh, xh, r, n=None):
    n = xh.shape[-1] if n is None else n
    return r * (dxh - xh * (jnp.sum(dxh * xh, axis=-1, keepdims=True) * (1.0 / n)))


def _rope_tables(pos, tm):
    lane = lax.broadcasted_iota(jnp.int32, (tm, HEAD_PAD), 1)
    idx = jnp.where(lane < QK_NOPE + QK_ROPE // 2, lane - QK_NOPE, lane - QK_NOPE - QK_ROPE // 2)
    inv = jnp.exp(idx.astype(F32) * (-np.log(ROPE_BASE) * 2.0 / QK_ROPE))
    ang = pos.astype(F32) * inv
    in_rope = (lane >= QK_NOPE) & (lane < QK_DIM)
    first = lane < QK_NOPE + QK_ROPE // 2
    cos_t = jnp.where(in_rope, jnp.cos(ang), 1.0)
    sin_t = jnp.where(in_rope, jnp.where(first, -jnp.sin(ang), jnp.sin(ang)), 0.0)
    return cos_t, sin_t, (first, in_rope)


def _rope_swap(x, halves):
    first, in_rope = halves
    half = QK_ROPE // 2
    return jnp.where(in_rope, jnp.where(first, pltpu.roll(x, HEAD_PAD - half, 1), pltpu.roll(x, half, 1)), 0.0)


def _cparams(sem, vmem=None):
    return pltpu.CompilerParams(dimension_semantics=sem, vmem_limit_bytes=vmem)


def _row_call(name, body, T, tm, row_ins, full_ins, row_outs, acc_outs, vmem=None, scratch=(), xchg=((), ())):
    n_in, n_out, n_x = len(row_ins) + len(full_ins), len(row_outs) + len(acc_outs), _x_count(xchg)
    steps = T // tm

    def kern(*refs):
        ins, x_in, refs = refs[:n_in], refs[n_in:n_in + n_x], refs[n_in + n_x:]
        outs, x_out, refs = refs[:n_out], refs[n_out:n_out + n_x], refs[n_out + n_x:]
        scr, x_sems = refs[:len(scratch)], refs[len(scratch):]
        i = pl.program_id(0)
        if n_x:
            @pl.when(i == 0)
            def _():
                for cp in _x_copies(len(xchg[0]), x_in, x_out, x_sems):
                    cp.start()

        body(i, *ins, *outs, *scr)
        if n_x:
            @pl.when(i == steps - 1)
            def _():
                for cp in _x_copies(len(xchg[0]), x_in, x_out, x_sems):
                    cp.wait()

    any_spec = pl.BlockSpec(memory_space=pl.ANY)
    in_specs = [pl.BlockSpec((tm, a.shape[1]), lambda i: (i, 0)) for a in row_ins]
    in_specs += [pl.BlockSpec(a.shape, lambda i, nd=a.ndim: (0,) * nd, pipeline_mode=pl.Buffered(1)) for a in full_ins]
    out_specs = [pl.BlockSpec((tm, n), lambda i: (i, 0)) for n, _ in row_outs]
    out_specs += [pl.BlockSpec(s, lambda i, nd=len(s): (0,) * nd) for s, _ in acc_outs]
    out_shape = [jax.ShapeDtypeStruct((T, n), dt) for n, dt in row_outs]
    out_shape += [jax.ShapeDtypeStruct(s, dt) for s, dt in acc_outs]
    return pl.pallas_call(
        kern, name=name, grid=(steps,), in_specs=in_specs + [any_spec] * n_x, out_specs=out_specs + [any_spec] * n_x,
        out_shape=out_shape + _x_out_shapes(xchg), scratch_shapes=list(scratch) + _x_sems(xchg),
        compiler_params=_cparams(("arbitrary",), vmem),
    )(*row_ins, *full_ins, *xchg[0], *xchg[1])


FFN_HALVES = (slice(0, FFN // 2), slice(FFN // 2, FFN))
ROW_CHUNK = 16
CHUNK_UNROLL = True


def _by_chunks(tm, fn):
    def step(c, carry):
        fn(pl.ds(pl.multiple_of(c * ROW_CHUNK, ROW_CHUNK), ROW_CHUNK))
        return carry

    lax.fori_loop(0, tm // ROW_CHUNK, step, 0, unroll=CHUNK_UNROLL)


def _fold8(x):
    return x[:8] + x[8:]


def _acc(ref, i, val):
    @pl.when(i == 0)
    def _():
        ref[...] = val

    @pl.when(i != 0)
    def _():
        ref[...] += val


def _in_proj_fwd(x, g_mix, w_in, T, tm):
    def body(i, x_ref, g_ref, w_ref, h_ref, *rest):
        outs, pj_s = rest[:-1], rest[-1]
        g = g_ref[...]

        def norm(rows):
            h_ref[rows, :] = (_rms(x_ref[rows, :])[0] * g).astype(MM)

        _by_chunks(tm, norm)
        for d in range(N_DEV):
            pj_s[d] = _dot_nt(h_ref[...], w_ref[d])

        def join_and_cut(rows):
            proj = jnp.concatenate([pj_s[d, rows, :] for d in range(N_DEV)], axis=1)
            for (s, n), o_ref in zip(COL_SECTIONS, outs):
                if n == QK_ROPE:
                    o_ref[rows, :] = jnp.concatenate(
                        [jnp.zeros((ROW_CHUNK, QK_NOPE), F32), proj[:, s:s + n],
                         jnp.zeros((ROW_CHUNK, HEAD_PAD - QK_DIM), F32)], axis=1)
                else:
                    o_ref[rows, :] = proj[:, s:s + n]

        _by_chunks(tm, join_and_cut)

    row_outs = [(D_MODEL, MM)] + [(n, F32) for _, n in SECTIONS]
    return _row_call("in_proj_fwd", body, T, tm, [x], [g_mix, w_in], row_outs, [], VMEM_LIMIT,
                     scratch=[pltpu.VMEM((N_DEV, tm, IN_BLOCK), F32)])


def _mla_heads_fwd(raw, g_pad, cos_t, sin_t, first):
    outs, saved = [], []
    for h in range(MLA_HEADS):
        xh, r = _rms(raw[:, h * HEAD_PAD:(h + 1) * HEAD_PAD], QK_DIM)
        y = xh * g_pad
        outs.append(y * cos_t + _rope_swap(y, first) * sin_t)
        saved.append((xh, r))
    return outs, saved


def _mla_raw_heads(cqn, ckvn, kr, wuq_ref, wukv_ref, tm):
    lane = lax.broadcasted_iota(jnp.int32, (tm, HEAD_PAD), 1)
    nope = lane < QK_NOPE
    one_lane = jnp.where(lane == V_DIM, 1.0, 0.0)
    qs, ks, vs = [], [], []
    for h in range(MLA_HEADS):
        qs.append(_dot_nt(cqn, wuq_ref[h]))
        kv = _dot(ckvn, wukv_ref[h])
        ks.append(jnp.where(nope, kv, kr))
        vs.append(jnp.where(nope, pltpu.roll(kv, V_DIM, 1), one_lane))
    return jnp.concatenate(qs, axis=1), jnp.concatenate(ks, axis=1), jnp.concatenate(vs, axis=1)


def _mla_prep_fwd(cq, ckv, kr, pos, g_qa, g_kva, g_qn, g_kn, w_uq, w_ukv, T, tm):
    def body(i, cq_ref, ckv_ref, kr_ref, pos_ref, gqa_ref, gkva_ref, gqn_ref, gkn_ref, wuq_ref, wukv_ref,
             q_ref, k_ref, v_ref):
        cos_t, sin_t, first = _rope_tables(pos_ref[...], tm)
        cqn = _rms(cq_ref[...])[0] * gqa_ref[...]
        ckvn = _rms(ckv_ref[...])[0] * gkva_ref[...]
        q_raw, k_raw, v = _mla_raw_heads(cqn, ckvn, kr_ref[...], wuq_ref, wukv_ref, tm)
        qs, _ = _mla_heads_fwd(q_raw, gqn_ref[...], cos_t, sin_t, first)
        ks, _ = _mla_heads_fwd(k_raw, gkn_ref[...], cos_t, sin_t, first)
        q_ref[...] = (jnp.concatenate(qs, axis=1) * ATT_SCALE).astype(MM)
        k_ref[...] = jnp.concatenate(ks, axis=1).astype(MM)
        v_ref[...] = v.astype(MM)

    w = MLA_HEADS * HEAD_PAD
    return _row_call("mla_prep_fwd", body, T, tm, [cq, ckv, kr, pos], [g_qa, g_kva, g_qn, g_kn, w_uq, w_ukv],
                     [(w, MM), (w, MM), (w, MM)], [])


def _causal_pairs(n, by_query):
    if by_query:
        pairs = [(q, k) for q in range(n) for k in range(q + 1)]
    else:
        pairs = [(q, k) for k in range(n) for q in range(k, n)]
    return np.array([p[0] for p in pairs], np.int32), np.array([p[1] for p in pairs], np.int32)


def _flash_fwd(qf, kf, vf, T, ag_blocks=()):
    tq = min(ATT_TILE, T)
    nq = T // tq

    qi_tab, ki_tab = _causal_pairs(nq, by_query=True)

    hp = ATT_HEADS

    n_ag = len(ag_blocks)
    n_heads, n_pairs = MLA_HEADS // hp, len(qi_tab)

    def body(qi_ref, ki_ref, q_ref, k_ref, v_ref, *rest):
        ag_in, (o_ref, lse_ref), rest = rest[:n_ag], rest[n_ag:n_ag + 2], rest[n_ag + 2:]
        ag_out, (m_s, acc_s), ag_sems = rest[:n_ag], rest[n_ag:n_ag + 2], rest[n_ag + 2:]
        t = pl.program_id(1)
        qi, ki = qi_ref[t], ki_ref[t]
        if n_ag:
            @pl.when((pl.program_id(0) == 0) & (t == 0))
            def _():
                _ag_start(ag_in, ag_out, ag_sems)

        @pl.when(ki == 0)
        def _():
            m_s[...] = jnp.full_like(m_s, NEG)
            acc_s[...] = jnp.zeros_like(acc_s)

        def step(masked):
            for hh in range(hp):
                hs = slice(hh * HEAD_PAD, (hh + 1) * HEAD_PAD)
                s_t = _dot_nt(k_ref[:, hs], q_ref[:, hs])
                if masked:
                    key = lax.broadcasted_iota(jnp.int32, (tq, tq), 0)
                    qry = lax.broadcasted_iota(jnp.int32, (tq, tq), 1)
                    s_t = jnp.where(key <= qry, s_t, NEG)
                m_old = m_s[hh]
                m_new = jnp.maximum(m_old, jnp.max(s_t, axis=0, keepdims=True))
                p_t = jnp.exp(s_t - m_new)
                acc_s[hh] = jnp.exp(m_old - m_new) * acc_s[hh] + _dot_tn(v_ref[:, hs], p_t)
                m_s[hh] = m_new

        @pl.when(ki < qi)
        def _():
            step(False)

        @pl.when(ki == qi)
        def _():
            step(True)
            real = lax.broadcasted_iota(jnp.int32, (HEAD_PAD, tq), 0) < V_DIM
            for hh in range(hp):
                hs = slice(hh * HEAD_PAD, (hh + 1) * HEAD_PAD)
                acc = acc_s[hh]
                l = acc[V_DIM:V_DIM + 1]
                o_ref[:, hs] = jnp.where(real, acc / l, 0.0).T
                lse_ref[:, hs] = jnp.broadcast_to(m_s[hh] + jnp.log(l), (HEAD_PAD, tq)).T

        if n_ag:
            @pl.when((pl.program_id(0) == n_heads - 1) & (t == n_pairs - 1))
            def _():
                _ag_finish(ag_in, ag_out, ag_sems)

    q_spec = pl.BlockSpec((tq, hp * HEAD_PAD), lambda h, t, qi_ref, ki_ref: (qi_ref[t], h))
    kv_spec = pl.BlockSpec((tq, hp * HEAD_PAD), lambda h, t, qi_ref, ki_ref: (ki_ref[t], h))
    any_spec = pl.BlockSpec(memory_space=pl.ANY)
    grid_spec = pltpu.PrefetchScalarGridSpec(
        num_scalar_prefetch=2, grid=(n_heads, n_pairs),
        in_specs=[q_spec, kv_spec, kv_spec] + [any_spec] * n_ag, out_specs=[q_spec, q_spec] + [any_spec] * n_ag,
        scratch_shapes=[pltpu.VMEM((hp, 1, tq), F32), pltpu.VMEM((hp, HEAD_PAD, tq), F32)]
        + (_ag_sems(n_ag) if n_ag else []))
    return pl.pallas_call(
        body, name="flash_fwd", grid_spec=grid_spec,
        out_shape=[jax.ShapeDtypeStruct((T, MLA_HEADS * HEAD_PAD), F32)] * 2 + _ag_out_shapes(ag_blocks),
        compiler_params=_cparams(("arbitrary", "arbitrary")),
    )(jnp.asarray(qi_tab), jnp.asarray(ki_tab), qf, kf, vf, *ag_blocks)


def _hg_gates(hf, lb):
    sg = _sigmoid(hf)
    f = lb + (1.0 - lb) * sg
    return sg, f, jnp.log(f), 1.0 - f


def _tri(n, lower):
    r = lax.broadcasted_iota(jnp.int32, (n, n), 0)
    c = lax.broadcasted_iota(jnp.int32, (n, n), 1)
    return jnp.where((c <= r) if lower else (c >= r), 1.0, 0.0).astype(F32)


def _hg_levels():
    C = HG_CHUNK
    t = lax.broadcasted_iota(jnp.int32, (C, C), 0)
    s = lax.broadcasted_iota(jnp.int32, (C, C), 1)
    levels = []
    for shift in range(C.bit_length() - 2, -1, -1):
        pair_t, pair_s = lax.shift_right_logical(t, shift + 1), lax.shift_right_logical(s, shift + 1)
        later_t = (lax.shift_right_logical(t, shift) & 1) == 1
        earlier_s = (lax.shift_right_logical(s, shift) & 1) == 0
        levels.append((1 << shift, (pair_t == pair_s) & later_t & earlier_s))
    return levels, t == s


def _hg_refs(b):
    C, n = b.shape
    row = lax.broadcasted_iota(jnp.int32, (C, n), 0)
    back1, back2, ahead1 = pltpu.roll(b, 1, 0), pltpu.roll(b, 2, 0), pltpu.roll(b, C - 1, 0)
    refs = []
    for half in (32, 16, 8, 4):
        refs.append(jnp.concatenate(
            [jnp.broadcast_to(b[lo + half - 1:lo + half], (2 * half, n)) for lo in range(0, C, 2 * half)], axis=0))
    in4 = row & 3
    refs.append(jnp.where(in4 == 0, ahead1, jnp.where(in4 == 1, b, jnp.where(in4 == 2, back1, back2))))
    refs.append(jnp.where((row & 1) == 1, back1, b))
    return refs


def _hg_intra(q, k, b, refs, levels, eye):
    a = jnp.where(eye, jnp.sum(q * k, axis=1, keepdims=True), 0.0)
    saved = []
    for r, (_, mask) in zip(refs, levels):
        e = jnp.exp(-jnp.abs(b - r))
        q_t, k_t = q * e, k * e
        a = a + jnp.where(mask, _dot_nt(q_t, k_t), 0.0)
        saved.append((q_t, k_t, e))
    return a, saved


def _hg_intra_bwd(d_a, q, k, saved, levels, eye):
    diag = jnp.sum(jnp.where(eye, d_a, 0.0), axis=1, keepdims=True)
    dq, dk = diag * k, diag * q
    for (q_t, k_t, e), (_, mask) in zip(saved, levels):
        da = jnp.where(mask, d_a, 0.0)
        dq = dq + _dot(da, k_t) * e
        dk = dk + _dot_tn(da, q_t) * e
    return dq, dk


def _hgrn_fwd(hq, hf, hi, lb, T):
    rb = min(HG_BLOCK, T)
    ncb = rb // HG_CHUNK

    def body(hq_ref, hf_ref, hi_ref, lb_ref, o_ref, s0_ref, st_ref):
        @pl.when(pl.program_id(0) == 0)
        def _():
            st_ref[...] = jnp.zeros_like(st_ref)

        tril = _tri(HG_CHUNK, True)
        levels, eye = _hg_levels()

        def chunk(c, carry):
            rows = pl.ds(pl.multiple_of(c * HG_CHUNK, HG_CHUNK), HG_CHUNK)
            _, _, logf, kk = _hg_gates(hf_ref[rows, :], lb_ref[...])
            b = _dot_hi(tril, logf)
            refs = _hg_refs(b)
            q_all, v_all = hq_ref[rows, :], hi_ref[rows, :]
            outs = []
            for h in range(HG_HEADS):
                ls = slice(h * HG_DIM, (h + 1) * HG_DIM)
                q, k, v, bh = q_all[:, ls], kk[:, ls], v_all[:, ls], b[:, ls]
                st = st_ref[h]
                s0_ref[c, h * HG_DIM:(h + 1) * HG_DIM, :] = st
                b_end = bh[HG_CHUNK - 1:HG_CHUNK]
                a, _ = _hg_intra(q, k, bh, [r[:, ls] for r in refs], levels, eye)
                outs.append(_dot_nt(q * jnp.exp(bh), st) + _dot(a, v))
                st_ref[h] = st * jnp.exp(b_end) + _dot_tn(v, k * jnp.exp(b_end - bh))
            o_ref[rows, :] = jnp.concatenate(outs, axis=1)
            return carry

        lax.fori_loop(0, ncb, chunk, 0, unroll=HG_UNROLL)

    row = pl.BlockSpec((rb, HG_W), lambda i: (i, 0))
    return pl.pallas_call(
        body, name="hgrn_fwd", grid=(T // rb,),
        in_specs=[row, row, row, pl.BlockSpec((1, HG_W), lambda i: (0, 0))],
        out_specs=[row, pl.BlockSpec((ncb, HG_W, HG_DIM), lambda i: (i, 0, 0))],
        out_shape=[jax.ShapeDtypeStruct((T, HG_W), F32), jax.ShapeDtypeStruct((T // HG_CHUNK, HG_W, HG_DIM), F32)],
        scratch_shapes=[pltpu.VMEM((HG_HEADS, HG_DIM, HG_DIM), F32)],
        compiler_params=_cparams(("arbitrary",)),
    )(hq, hf, hi, lb)


def _hgrn_bwd(hq, hf, hi, do, s0, lb, T, xchg=((), ())):
    rb = min(HG_BLOCK, T)
    ncb = rb // HG_CHUNK
    nb = T // rb
    C = HG_CHUNK
    n_x, n_sib = _x_count(xchg), len(xchg[0])

    def body(hq_ref, hf_ref, hi_ref, do_ref, s0_ref, lb_ref, *rest):
        x_in, (dq_ref, df_ref, dv_ref, dlb_ref), rest = rest[:n_x], rest[n_x:n_x + 4], rest[n_x + 4:]
        x_out, dst_ref, x_sems = rest[:n_x], rest[n_x], rest[n_x + 1:]

        @pl.when(pl.program_id(0) == 0)
        def _():
            dst_ref[...] = jnp.zeros_like(dst_ref)
            dlb_ref[...] = jnp.zeros_like(dlb_ref)
            for cp in _x_copies(n_sib, x_in, x_out, x_sems):
                cp.start()

        tril, triu = _tri(C, True), _tri(C, False)
        row_cc = lax.broadcasted_iota(jnp.int32, (C, C), 0)
        col_cc = lax.broadcasted_iota(jnp.int32, (C, C), 1)
        last_row = lax.broadcasted_iota(jnp.int32, (C, HG_DIM), 0) == C - 1
        lb_v = lb_ref[...]
        levels, eye = _hg_levels()

        def chunk(cc, carry):
            c = ncb - 1 - cc
            rows = pl.ds(pl.multiple_of(c * C, C), C)
            hf_c = hf_ref[rows, :]
            sg, f, logf, kk = _hg_gates(hf_c, lb_v)
            b = _dot_hi(tril, logf)
            refs = _hg_refs(b)
            q_all, v_all, do_all = hq_ref[rows, :], hi_ref[rows, :], do_ref[rows, :]
            dq_o, dk_o, dv_o, db_o = [], [], [], []
            for h in range(HG_HEADS):
                ls = slice(h * HG_DIM, (h + 1) * HG_DIM)
                q, k, v, bh, d_o = q_all[:, ls], kk[:, ls], v_all[:, ls], b[:, ls], do_all[:, ls]
                st0 = s0_ref[c, h * HG_DIM:(h + 1) * HG_DIM, :]
                dst = dst_ref[h]
                b_end = bh[C - 1:C]
                e_b, e_end = jnp.exp(bh), jnp.exp(b_end)
                e_rem = jnp.exp(b_end - bh)
                qe, kd = q * e_b, k * e_rem
                st_end = st0 * e_end + _dot_tn(v, kd)
                a, saved = _hg_intra(q, k, bh, [r[:, ls] for r in refs], levels, eye)
                d_a = jnp.where(col_cc <= row_cc, _dot_nt(d_o, v), 0.0)
                dq_i, dk_i = _hg_intra_bwd(d_a, q, k, saved, levels, eye)
                dv = _dot_tn(a, d_o) + _dot_nt(kd, dst)
                dq = e_b * _dot(d_o, st0) + dq_i
                dk = e_rem * _dot(v, dst) + dk_i
                extra = jnp.sum(dst * st_end, axis=0, keepdims=True)
                db_o.append(q * dq - k * dk + jnp.where(last_row, extra, 0.0))
                dst_ref[h] = dst * e_end + _dot_tn(d_o, qe)
                dq_o.append(dq)
                dk_o.append(dk)
                dv_o.append(dv)
            dlogf = _dot_hi(triu, jnp.concatenate(db_o, axis=1))
            d_f = dlogf / f - jnp.concatenate(dk_o, axis=1)
            dq_ref[rows, :] = jnp.concatenate(dq_o, axis=1).astype(MM)
            dv_ref[rows, :] = jnp.concatenate(dv_o, axis=1).astype(MM)
            df_ref[rows, :] = (d_f * (1.0 - lb_v) * sg * (1.0 - sg)).astype(MM)
            dlb_ref[...] += jnp.sum(d_f * (1.0 - sg), axis=0, keepdims=True)
            return carry

        lax.fori_loop(0, ncb, chunk, 0, unroll=HG_UNROLL)

        if n_x:
            @pl.when(pl.program_id(0) == nb - 1)
            def _():
                for cp in _x_copies(n_sib, x_in, x_out, x_sems):
                    cp.wait()

    row = pl.BlockSpec((rb, HG_W), lambda i: (nb - 1 - i, 0))
    one = pl.BlockSpec((1, HG_W), lambda i: (0, 0))
    any_spec = pl.BlockSpec(memory_space=pl.ANY)
    return pl.pallas_call(
        body, name="hgrn_bwd", grid=(nb,),
        in_specs=[row, row, row, row, pl.BlockSpec((ncb, HG_W, HG_DIM), lambda i: (nb - 1 - i, 0, 0)), one]
        + [any_spec] * n_x,
        out_specs=[row, row, row, one] + [any_spec] * n_x,
        out_shape=[jax.ShapeDtypeStruct((T, HG_W), MM)] * 3 + [jax.ShapeDtypeStruct((1, HG_W), F32)]
        + _x_out_shapes(xchg),
        scratch_shapes=[pltpu.VMEM((HG_HEADS, HG_DIM, HG_DIM), F32)] + _x_sems(xchg),
        compiler_params=_cparams(("arbitrary",)),
    )(hq, hf, hi, do, s0, lb, *xchg[0], *xchg[1])


def _silu_parts(x):
    sg = _sigmoid(x)
    return x * sg, sg * (1.0 + x * (1.0 - sg))


def _merge_fwd(attn, o, hg, bg, x, g_out, w_bra, w_brb, w_out, T, tm):
    def body(i, attn_ref, o_ref, hg_ref, bg_ref, x_ref, g_ref, wa_ref, wb_ref, wo_ref,
             x1_ref, ya_ref, yb_ref, m_ref, rec_ref):
        g = g_ref[...]

        def recurrent_out(rows):
            for h in range(HG_HEADS):
                ls = slice(h * HG_DIM, (h + 1) * HG_DIM)
                rec_ref[rows, ls] = (_rms(o_ref[rows, ls])[0] * g * _silu_parts(hg_ref[rows, ls])[0]).astype(MM)

        _by_chunks(tm, recurrent_out)
        ya_ref[...] = _dot(attn_ref[...], wa_ref[...])
        yb_ref[...] = jnp.dot(rec_ref[...], wb_ref[...], preferred_element_type=F32)

        def gate(rows):
            m_ref[rows, :] = (_sigmoid(bg_ref[rows, :D_MODEL]) * ya_ref[rows, :]
                              + _sigmoid(bg_ref[rows, D_MODEL:]) * yb_ref[rows, :]).astype(MM)

        _by_chunks(tm, gate)
        x1_ref[...] = x_ref[...] + jnp.dot(m_ref[...], wo_ref[...], preferred_element_type=F32)

    return _row_call("merge_fwd", body, T, tm, [attn, o, hg, bg, x], [g_out, w_bra, w_brb, w_out],
                     [(D_MODEL, F32), (D_MODEL, F32), (D_MODEL, F32), (D_MODEL, MM), (HG_W, MM)], [], VMEM_LIMIT)


def _ffn_fwd(x1, g_ffn, w_g, w_u, w_d, T, tm):
    def body(i, x1_ref, g_ref, wg_ref, wu_ref, wd_ref, x2_ref, gt_ref, up_ref, h2_ref, a_s, gt_s, up_s):
        g = g_ref[...]

        def norm(rows):
            h2_ref[rows, :] = (_rms(x1_ref[rows, :])[0] * g).astype(MM)

        _by_chunks(tm, norm)
        gt_s[...] = _dot_nt(h2_ref[...], wg_ref[...])
        up_s[...] = _dot_nt(h2_ref[...], wu_ref[...])

        def act(rows):
            for cs in FFN_HALVES:
                gt, up = gt_s[rows, cs], up_s[rows, cs]
                a_s[rows, cs] = (_silu_parts(gt)[0] * up).astype(MM)
                gt_ref[rows, cs] = gt.astype(MM)
                up_ref[rows, cs] = up.astype(MM)

        _by_chunks(tm, act)
        x2_ref[...] = x1_ref[...] + jnp.dot(a_s[...], wd_ref[...], preferred_element_type=F32)

    return _row_call("ffn_fwd", body, T, tm, [x1], [g_ffn, w_g, w_u, w_d],
                     [(D_MODEL, F32), (FFN, MM), (FFN, MM), (D_MODEL, MM)], [], VMEM_LIMIT,
                     scratch=[pltpu.VMEM((tm, FFN), MM), pltpu.VMEM((tm, FFN), F32), pltpu.VMEM((tm, FFN), F32)])


def _ple_loss(x2, p, tgt, g_pg, g_post, w_pg, w_pp, T, tm):
    def body(i, x2_ref, p_ref, t_ref, gpg_ref, gpo_ref, wpg_ref, wpp_ref,
             dx2_ref, loss_ref, dgpo_ref, dgpg_ref, dwpg_ref, dwpp_ref, u_s, n3_s, z_s, dz_s, du_s, dy_s, dn3_s):
        @pl.when(i == 0)
        def _():
            for ref in (loss_ref, dgpo_ref, dgpg_ref, dwpg_ref, dwpp_ref):
                ref[...] = jnp.zeros_like(ref)

        gpg, gpo = gpg_ref[...], gpo_ref[...]
        p_mm = p_ref[...].astype(MM)
        for d in range(N_DEV):
            u_s[:, d * HEAD_PAD:(d + 1) * HEAD_PAD] = jnp.dot(p_mm, wpp_ref[d], preferred_element_type=F32)

        def gate_input(rows):
            n3_s[rows, :] = (_rms(x2_ref[rows, :])[0] * gpg).astype(MM)

        _by_chunks(tm, gate_input)
        z_s[...] = jnp.dot(n3_s[...], wpg_ref[...], preferred_element_type=F32)

        def loss_and_back(rows):
            uh, ru = _rms(u_s[rows, :])
            e = uh * gpo
            gate = _sigmoid(z_s[rows, :])
            diff = x2_ref[rows, :] + gate * e - t_ref[rows, :]
            dy = diff * (1.0 / D_MODEL)
            de = dy * gate
            dz_s[rows, :] = (dy * e * gate * (1.0 - gate)).astype(MM)
            du_s[rows, :] = _rms_bwd(de * gpo, uh, ru).astype(MM)
            dy_s[rows, :] = dy
            loss_ref[...] += _fold8(diff * diff) * (0.5 / D_MODEL)
            dgpo_ref[...] += _fold8(de * uh)

        _by_chunks(tm, loss_and_back)
        dn3_s[...] = _dot_nt(dz_s[...], wpg_ref[...])

        def gate_norm_back(rows):
            x2h, r3 = _rms(x2_ref[rows, :])
            dn3 = dn3_s[rows, :]
            dx2_ref[rows, :] = dy_s[rows, :] + _rms_bwd(dn3 * gpg, x2h, r3)
            dgpg_ref[...] += _fold8(dn3 * x2h)

        _by_chunks(tm, gate_norm_back)
        dwpg_ref[...] += _dot_tn(n3_s[...], dz_s[...])
        for d in range(N_DEV):
            dwpp_ref[d] += _dot_tn(p_mm, du_s[:, d * HEAD_PAD:(d + 1) * HEAD_PAD])

    vec = ((8, D_MODEL), F32)
    wide = lambda dt: pltpu.VMEM((tm, D_MODEL), dt)
    return _row_call("ple_loss", body, T, tm, [x2, p, tgt], [g_pg, g_post, w_pg, w_pp], [(D_MODEL, F32)],
                     [vec, vec, vec, ((D_MODEL, D_MODEL), F32), ((N_DEV, PLE, HEAD_PAD), F32)], VMEM_LIMIT,
                     scratch=[wide(F32), wide(MM), wide(F32), wide(MM), wide(MM), wide(F32), wide(F32)])


def _ffn_bwd(dx2, x1, gt, up, g_ffn, w_g, w_u, w_d, T, tm):
    def body(i, dx2_ref, x1_ref, gt_ref, up_ref, g_ref, wg_ref, wu_ref, wd_ref,
             dx1_ref, a_ref, dgt_ref, dup_ref, dg_ref, da_s, dh2_s):
        @pl.when(i == 0)
        def _():
            dg_ref[...] = jnp.zeros_like(dg_ref)

        g = g_ref[...]
        da_s[...] = _dot_nt(dx2_ref[...], wd_ref[...])

        def act_back(rows):
            for cs in FFN_HALVES:
                up, da = up_ref[rows, cs].astype(F32), da_s[rows, cs]
                silu, dsilu = _silu_parts(gt_ref[rows, cs].astype(F32))
                dgt_ref[rows, cs] = (da * up * dsilu).astype(MM)
                dup_ref[rows, cs] = (da * silu).astype(MM)
                a_ref[rows, cs] = (silu * up).astype(MM)

        _by_chunks(tm, act_back)
        dh2_s[...] = (jnp.dot(dgt_ref[...], wg_ref[...], preferred_element_type=F32)
                      + jnp.dot(dup_ref[...], wu_ref[...], preferred_element_type=F32))

        def norm_back(rows):
            x1h, r = _rms(x1_ref[rows, :])
            dh2 = dh2_s[rows, :]
            dx1_ref[rows, :] = dx2_ref[rows, :] + _rms_bwd(dh2 * g, x1h, r)
            dg_ref[...] += _fold8(dh2 * x1h)

        _by_chunks(tm, norm_back)

    return _row_call("ffn_bwd", body, T, tm, [dx2, x1, gt, up], [g_ffn, w_g, w_u, w_d],
                     [(D_MODEL, F32), (FFN, MM), (FFN, MM), (FFN, MM)], [((8, D_MODEL), F32)], VMEM_LIMIT,
                     scratch=[pltpu.VMEM((tm, FFN), F32), pltpu.VMEM((tm, D_MODEL), F32)])


def _merge_bwd(dx1, ya, yb, bg, o, hg, attn, m, rec, g_out, w_bra, w_brb, w_out, T, tm, xchg=((), ())):
    def body(i, dx1_ref, ya_ref, yb_ref, bg_ref, o_ref, hg_ref, attn_ref, m_ref, rec_ref, g_ref, wa_ref, wb_ref, wo_ref,
             dattn_ref, do_ref, dhg_ref, dbg_ref, dg_ref, dwo_ref, dwa_ref, dwb_ref, dm_s, dya_s, dyb_s, drec_s):
        @pl.when(i == 0)
        def _():
            for ref in (dg_ref, dwo_ref, dwa_ref, dwb_ref):
                ref[...] = jnp.zeros_like(ref)

        g = g_ref[...]
        dx1 = dx1_ref[...].astype(MM)
        dm_s[...] = _dot_nt(dx1, wo_ref[...])

        def gate_back(rows):
            dm = dm_s[rows, :]
            ga, gb = _sigmoid(bg_ref[rows, :D_MODEL]), _sigmoid(bg_ref[rows, D_MODEL:])
            dya_s[rows, :] = (dm * ga).astype(MM)
            dyb_s[rows, :] = (dm * gb).astype(MM)
            dbg_ref[rows, :D_MODEL] = (dm * ya_ref[rows, :] * ga * (1.0 - ga)).astype(MM)
            dbg_ref[rows, D_MODEL:] = (dm * yb_ref[rows, :] * gb * (1.0 - gb)).astype(MM)

        _by_chunks(tm, gate_back)
        dwo_ref[...] += _dot_tn(m_ref[...], dx1)
        attn_mm = attn_ref[...].astype(MM)
        for d in range(N_DEV):
            ds = slice(d * HEAD_PAD, (d + 1) * HEAD_PAD)
            dwa_ref[d] += _dot_tn(attn_mm, dya_s[:, ds])
            dwb_ref[d] += _dot_tn(rec_ref[...], dyb_s[:, ds])
        dattn_ref[...] = _dot_nt(dya_s[...], wa_ref[...])
        drec_s[...] = _dot_nt(dyb_s[...], wb_ref[...])

        def recurrent_out_back(rows):
            for h in range(HG_HEADS):
                ls = slice(h * HG_DIM, (h + 1) * HG_DIM)
                oh, r = _rms(o_ref[rows, ls])
                silu, dsilu = _silu_parts(hg_ref[rows, ls])
                dr = drec_s[rows, ls]
                dhg_ref[rows, ls] = (dr * oh * g * dsilu).astype(MM)
                don = dr * silu
                dg_ref[...] += _fold8(don * oh)
                do_ref[rows, ls] = _rms_bwd(don * g, oh, r)

        _by_chunks(tm, recurrent_out_back)

    wide = lambda n, dt: pltpu.VMEM((tm, n), dt)
    return _row_call("merge_bwd", body, T, tm, [dx1, ya, yb, bg, o, hg, attn, m, rec], [g_out, w_bra, w_brb, w_out],
                     [(D_MODEL, F32), (HG_W, F32), (HG_W, MM), (2 * D_MODEL, MM)],
                     [((8, HG_DIM), F32), ((D_MODEL, D_MODEL), F32), ((N_DEV, MLA_HEADS * HEAD_PAD, HEAD_PAD), F32),
                      ((N_DEV, HG_W, HEAD_PAD), F32)], VMEM_LIMIT,
                     scratch=[wide(D_MODEL, F32), wide(D_MODEL, MM), wide(D_MODEL, MM), wide(HG_W, F32)], xchg=xchg)


def _flash_bwd(qf, kf, vf, o, do, lse, T, xchg=((), ())):
    tq = min(ATT_TILE, T)
    nq = T // tq

    qi_tab, ki_tab = _causal_pairs(nq, by_query=False)

    n_x, n_sib = _x_count(xchg), len(xchg[0])
    hp = ATT_HEADS
    n_heads, n_pairs = MLA_HEADS // hp, len(qi_tab)

    def body(qi_ref, ki_ref, q_ref, k_ref, v_ref, o_ref, do_ref, lse_ref, *rest):
        x_in, (dq_ref, dk_ref, dv_ref), rest = rest[:n_x], rest[n_x:n_x + 3], rest[n_x + 3:]
        x_out, x_sems = rest[:n_x], rest[n_x:]
        t = pl.program_id(1)
        qi, ki = qi_ref[t], ki_ref[t]
        if n_x:
            @pl.when((pl.program_id(0) == 0) & (t == 0))
            def _():
                for cp in _x_copies(n_sib, x_in, x_out, x_sems):
                    cp.start()

        @pl.when(t == 0)
        def _():
            dq_ref[...] = jnp.zeros_like(dq_ref)

        def step(first):
            rows = pl.ds(pl.multiple_of(qi * tq, tq), tq)
            for hh in range(hp):
                hs = slice(hh * HEAD_PAD, (hh + 1) * HEAD_PAD)
                q, k, d_o = q_ref[:, hs], k_ref[:, hs], do_ref[:, hs]
                s = _dot_nt(q, k)
                if first:
                    row = lax.broadcasted_iota(jnp.int32, (tq, tq), 0)
                    col = lax.broadcasted_iota(jnp.int32, (tq, tq), 1)
                    s = jnp.where(col <= row, s, NEG)
                p = jnp.exp(s - lse_ref[:, hh * HEAD_PAD:hh * HEAD_PAD + 1])
                delta = jnp.sum(d_o * o_ref[:, hs], axis=1, keepdims=True)
                ds = p * (_dot_nt(d_o, v_ref[:, hs]) - delta)
                dq_ref[rows, hs] += _dot(ds, k)
                if first:
                    dv_ref[:, hs] = _dot_tn(p, d_o)
                    dk_ref[:, hs] = _dot_tn(ds, q)
                else:
                    dv_ref[:, hs] += _dot_tn(p, d_o)
                    dk_ref[:, hs] += _dot_tn(ds, q)

        @pl.when(qi == ki)
        def _():
            step(True)

        @pl.when(qi > ki)
        def _():
            step(False)

        if n_x:
            @pl.when((pl.program_id(0) == n_heads - 1) & (t == n_pairs - 1))
            def _():
                for cp in _x_copies(n_sib, x_in, x_out, x_sems):
                    cp.wait()

    q_spec = pl.BlockSpec((tq, hp * HEAD_PAD), lambda h, t, qi_ref, ki_ref: (qi_ref[t], h))
    kv_spec = pl.BlockSpec((tq, hp * HEAD_PAD), lambda h, t, qi_ref, ki_ref: (ki_ref[t], h))
    any_spec = pl.BlockSpec(memory_space=pl.ANY)
    w = MLA_HEADS * HEAD_PAD
    grid_spec = pltpu.PrefetchScalarGridSpec(
        num_scalar_prefetch=2, grid=(n_heads, n_pairs),
        in_specs=[q_spec, kv_spec, kv_spec, q_spec, q_spec, q_spec] + [any_spec] * n_x,
        out_specs=[pl.BlockSpec((T, hp * HEAD_PAD), lambda h, t, qi_ref, ki_ref: (0, h)), kv_spec, kv_spec]
        + [any_spec] * n_x,
        scratch_shapes=_x_sems(xchg))
    return pl.pallas_call(
        body, name="flash_bwd", grid_spec=grid_spec,
        out_shape=[jax.ShapeDtypeStruct((T, w), F32)] * 3 + _x_out_shapes(xchg),
        compiler_params=_cparams(("arbitrary", "arbitrary")),
    )(jnp.asarray(qi_tab), jnp.asarray(ki_tab), qf, kf, vf, o, do, lse, *xchg[0], *xchg[1])


def _mla_heads_bwd(d_out, saved, g_pad, cos_t, sin_t, first):
    d_raw, dg = [], jnp.zeros((1, HEAD_PAD), F32)
    for h in range(MLA_HEADS):
        xh, r = saved[h]
        dy = d_out[:, h * HEAD_PAD:(h + 1) * HEAD_PAD]
        dn = dy * cos_t + _rope_swap(dy * sin_t, first)
        dg = dg + jnp.sum(dn * xh, axis=0, keepdims=True)
        d_raw.append(_rms_bwd(dn * g_pad, xh, r, QK_DIM))
    return d_raw, dg


def _mla_prep_bwd(cq, ckv, kr, pos, dqf, dkf, dvf, g_qa, g_kva, g_qn, g_kn, w_uq, w_ukv, T, tm):
    def body(i, cq_ref, ckv_ref, kr_ref, pos_ref, dq_ref, dk_ref, dv_ref,
             gqa_ref, gkva_ref, gqn_ref, gkn_ref, wuq_ref, wukv_ref,
             dcq_ref, dckv_ref, dkr_ref, dgqa_ref, dgkva_ref, dgqn_ref, dgkn_ref, dwuq_ref, dwukv_ref):
        cos_t, sin_t, first = _rope_tables(pos_ref[...], tm)
        cqh, rq = _rms(cq_ref[...])
        ckvh, rkv = _rms(ckv_ref[...])
        cqn, ckvn = cqh * gqa_ref[...], ckvh * gkva_ref[...]
        q_raw, k_raw, _ = _mla_raw_heads(cqn, ckvn, kr_ref[...], wuq_ref, wukv_ref, tm)
        _, q_saved = _mla_heads_fwd(q_raw, gqn_ref[...], cos_t, sin_t, first)
        _, k_saved = _mla_heads_fwd(k_raw, gkn_ref[...], cos_t, sin_t, first)
        dq_heads, dgqn = _mla_heads_bwd(dq_ref[...] * ATT_SCALE, q_saved, gqn_ref[...], cos_t, sin_t, first)
        dk_heads, dgkn = _mla_heads_bwd(dk_ref[...], k_saved, gkn_ref[...], cos_t, sin_t, first)
        lane = lax.broadcasted_iota(jnp.int32, (tm, HEAD_PAD), 1)
        nope = lane < QK_NOPE
        dcqn = jnp.zeros((tm, Q_RANK), F32)
        dckvn = jnp.zeros((tm, KV_RANK), F32)
        dkr = jnp.zeros((tm, HEAD_PAD), F32)
        cqn_mm, ckvn_mm = cqn.astype(MM), ckvn.astype(MM)
        for h in range(MLA_HEADS):
            hs = slice(h * HEAD_PAD, (h + 1) * HEAD_PAD)
            dq_h = dq_heads[h].astype(MM)
            dkv_h = jnp.where(nope, dk_heads[h], pltpu.roll(dv_ref[:, hs], V_DIM, 1)).astype(MM)
            _acc(dwuq_ref.at[h], i, _dot_tn(dq_h, cqn_mm))
            _acc(dwukv_ref.at[h], i, _dot_tn(ckvn_mm, dkv_h))
            dcqn = dcqn + jnp.dot(dq_h, wuq_ref[h], preferred_element_type=F32)
            dckvn = dckvn + lax.dot_general(dkv_h, wukv_ref[h], (((1,), (1,)), ((), ())), preferred_element_type=F32)
            dkr = dkr + dk_heads[h]
        dkr_ref[...] = jnp.where((lane >= QK_NOPE) & (lane < QK_DIM), dkr, 0.0).astype(MM)
        dcq_ref[...] = _rms_bwd(dcqn * gqa_ref[...], cqh, rq).astype(MM)
        dckv_ref[...] = _rms_bwd(dckvn * gkva_ref[...], ckvh, rkv).astype(MM)
        _acc(dgqa_ref, i, jnp.sum(dcqn * cqh, axis=0, keepdims=True))
        _acc(dgkva_ref, i, jnp.sum(dckvn * ckvh, axis=0, keepdims=True))
        _acc(dgqn_ref, i, dgqn)
        _acc(dgkn_ref, i, dgkn)

    return _row_call(
        "mla_prep_bwd", body, T, tm, [cq, ckv, kr, pos, dqf, dkf, dvf], [g_qa, g_kva, g_qn, g_kn, w_uq, w_ukv],
        [(Q_RANK, MM), (KV_RANK, MM), (HEAD_PAD, MM)],
        [((1, Q_RANK), F32), ((1, KV_RANK), F32), ((1, HEAD_PAD), F32), ((1, HEAD_PAD), F32),
         ((MLA_HEADS, HEAD_PAD, Q_RANK), F32), ((MLA_HEADS, KV_RANK, HEAD_PAD), F32)], VMEM_LIMIT)


def _in_proj_bwd(x, dx1, dsecs, g_mix, w_in, T, tm):
    def body(i, x_ref, dx1_ref, *rest):
        d_refs, (g_ref, w_ref, dx_ref, dp_ref, dg_ref, dh_s) = rest[:len(SECTIONS)], rest[len(SECTIONS):]

        @pl.when(i == 0)
        def _():
            dg_ref[...] = jnp.zeros_like(dg_ref)

        g = g_ref[...]

        def join_and_cut(rows):
            pieces = [(d_ref[rows, QK_NOPE:QK_DIM] if n == QK_ROPE else d_ref[rows, :]).astype(F32)
                      for (_, n), d_ref in zip(COL_SECTIONS, d_refs)]
            dproj = jnp.concatenate(pieces, axis=1)
            for d in range(N_DEV):
                dp_ref[d, rows, :] = dproj[:, d * IN_BLOCK:(d + 1) * IN_BLOCK].astype(MM)

        _by_chunks(tm, join_and_cut)
        dh = jnp.dot(dp_ref[0], w_ref[0], preferred_element_type=F32)
        for d in range(1, N_DEV):
            dh = dh + jnp.dot(dp_ref[d], w_ref[d], preferred_element_type=F32)
        dh_s[...] = dh

        def norm_back(rows):
            xh, r = _rms(x_ref[rows, :])
            dh_c = dh_s[rows, :]
            dx_ref[rows, :] = dx1_ref[rows, :] + _rms_bwd(dh_c * g, xh, r)
            dg_ref[...] += _fold8(dh_c * xh)

        _by_chunks(tm, norm_back)

    in_specs = [pl.BlockSpec((tm, a.shape[1]), lambda i: (i, 0)) for a in [x, dx1, *dsecs]]
    in_specs += [pl.BlockSpec(g_mix.shape, lambda i: (0, 0)),
                 pl.BlockSpec(w_in.shape, lambda i: (0, 0, 0), pipeline_mode=pl.Buffered(1))]

    def kern(*refs):
        body(pl.program_id(0), *refs)

    return pl.pallas_call(
        kern, name="in_proj_bwd", grid=(T // tm,), in_specs=in_specs,
        out_specs=[pl.BlockSpec((tm, D_MODEL), lambda i: (i, 0)),
                   pl.BlockSpec((N_DEV, tm, IN_BLOCK), lambda i: (0, i, 0)),
                   pl.BlockSpec((8, D_MODEL), lambda i: (0, 0))],
        out_shape=[jax.ShapeDtypeStruct((T, D_MODEL), F32), jax.ShapeDtypeStruct((N_DEV, T, IN_BLOCK), MM),
                   jax.ShapeDtypeStruct((8, D_MODEL), F32)],
        scratch_shapes=[pltpu.VMEM((tm, D_MODEL), F32)],
        compiler_params=_cparams(("arbitrary",), VMEM_LIMIT),
    )(x, dx1, *dsecs, g_mix, w_in)


def _pick_block(n, cap):
    best = None
    for cand in range(128, min(n, cap) + 1, 128):
        if n % cand == 0:
            best = cand
    return n if best is None else best


def _pick_rows(n, cap):
    best = n
    for cand in range(8, min(n, cap) + 1, 8):
        if n % cand == 0:
            best = cand
    return best


def _matmul_tn(name, a, b):
    T, M = a.shape
    N = b.shape[1]
    bm, bk = _pick_block(M, 1408), min(512, T)
    bn = _pick_block(N, 2560)

    def body(a_ref, b_ref, c_ref):
        @pl.when(pl.program_id(2) == 0)
        def _():
            c_ref[...] = jnp.zeros_like(c_ref)

        c_ref[...] += _dot_tn(a_ref[...], b_ref[...])

    return pl.pallas_call(
        body, name=name, grid=(M // bm, N // bn, T // bk),
        in_specs=[pl.BlockSpec((bk, bm), lambda i, j, k: (k, i)), pl.BlockSpec((bk, bn), lambda i, j, k: (k, j))],
        out_specs=pl.BlockSpec((bm, bn), lambda i, j, k: (i, j)), out_shape=jax.ShapeDtypeStruct((M, N), F32),
        compiler_params=_cparams(("parallel", "parallel", "arbitrary"), VMEM_LIMIT),
    )(a, b)


def _matmul_tn_blocks(name, a, b):
    T, M = a.shape
    nd, _, c = b.shape
    bm, bk = _pick_block(M, 512), min(512, T)

    def body(a_ref, b_ref, c_ref):
        @pl.when(pl.program_id(1) == 0)
        def _():
            c_ref[...] = jnp.zeros_like(c_ref)

        a_blk = a_ref[...].astype(MM)
        for d in range(nd):
            c_ref[d] += _dot_tn(b_ref[d], a_blk)

    return pl.pallas_call(
        body, name=name, grid=(M // bm, T // bk),
        in_specs=[pl.BlockSpec((bk, bm), lambda i, k: (k, i)), pl.BlockSpec((nd, bk, c), lambda i, k: (0, k, 0))],
        out_specs=pl.BlockSpec((nd, c, bm), lambda i, k: (0, 0, i)),
        out_shape=jax.ShapeDtypeStruct((nd, c, M), F32),
        compiler_params=_cparams(("parallel", "arbitrary"), VMEM_LIMIT),
    )(a, b)


def _pad_gain(g, n):
    return jnp.pad(g.reshape(1, -1), ((0, 0), (0, n - g.shape[-1])))


GROUP_A = ("w_ffn_gate", "w_ffn_up", "w_ffn_down", "w_ple_gate", "w_ple_proj")
GROUP_B = ("w_branch", "w_out")
GROUP_C = ("w_in", "w_uq", "w_ukv")
EARLY = GROUP_C
LATE = GROUP_B + GROUP_A
TRANSPOSED = ("w_in", "w_uq", "w_ffn_gate", "w_ffn_up")


def _local_step(x, p, pos, tgt, small, big, late_blocks=None, core=None):
    T = x.shape[0]
    tm = min(ROW_TILE, T)
    w_in = big["w_in"]
    w_uq = jnp.pad(big["w_uq"], ((0, 0), (0, HEAD_PAD - QK_DIM), (0, 0)))
    w_ukv = big["w_ukv"]

    g_mix, g_qa, g_kva = small["mix_norm_g"], small["q_a_norm_g"], small["kv_a_norm_g"]
    g_qn, g_kn = _pad_gain(small["q_norm_g"], HEAD_PAD), _pad_gain(small["k_norm_g"], HEAD_PAD)
    g_out, g_ffn = small["hg_out_norm_g"], small["ffn_norm_g"]
    g_pg, g_post = small["ple_gate_norm_g"], small["ple_post_norm_g"]
    logits = small["hg_lb_logits"]
    lb = _lower_bound(logits)

    h, cq, ckv, kr, hq, hf, hi, hg, bg = _in_proj_fwd(x, g_mix, w_in, T, tm)
    qf, kf, vf = _mla_prep_fwd(cq, ckv, kr, pos, g_qa, g_kva, g_qn, g_kn, w_uq, w_ukv, T, tm)
    if late_blocks is None:
        attn, lse = _flash_fwd(qf, kf, vf, T)
    else:
        attn, lse, *late = _flash_fwd(qf, kf, vf, T, ag_blocks=[late_blocks[n] for n in LATE])
        big = {**big, **dict(zip(LATE, late))}
    w_branch = jnp.moveaxis(big["w_branch"].reshape(N_DEV, 2, HG_W, HEAD_PAD), 0, 2).reshape(2, HG_W, D_MODEL)
    w_bra = jnp.pad(w_branch[0].reshape(MLA_HEADS, V_DIM, D_MODEL),
                    ((0, 0), (0, HEAD_PAD - V_DIM), (0, 0))).reshape(MLA_HEADS * HEAD_PAD, D_MODEL)
    w_brb = w_branch[1]
    w_out = big["w_out"].reshape(D_MODEL, D_MODEL)
    w_g, w_u = big["w_ffn_gate"].reshape(FFN, D_MODEL), big["w_ffn_up"].reshape(FFN, D_MODEL)
    w_d = big["w_ffn_down"].reshape(FFN, D_MODEL)
    w_pg, w_pp = big["w_ple_gate"].reshape(D_MODEL, D_MODEL), big["w_ple_proj"]
    o, s0 = _hgrn_fwd(hq, hf, hi, lb, T)
    x1, ya, yb, m, rec = _merge_fwd(attn, o, hg, bg, x, g_out, w_bra, w_brb, w_out, T, tm)
    x2, gt, up, h2 = _ffn_fwd(x1, g_ffn, w_g, w_u, w_d, T, tm)
    dx2, loss_p, dg_post, dg_pg, d_pg, d_pp = _ple_loss(x2, p, tgt, g_pg, g_post, w_pg, w_pp, T, tm)
    dg_post, dg_pg = (jnp.sum(t, axis=0, keepdims=True) for t in (dg_post, dg_pg))

    grads, sibs, gots = {}, {}, {}
    dist = core is not None
    pick = lambda names: [grads[n] for n in names] if dist else ()

    def partials(tag, names, got):
        if not dist:
            return ()
        sibs.update(zip(names, got))
        return _chip_partials("rs_partial_" + tag, pick(names), got, core)

    dx1, a, dgt, dup, dg_ffn = _ffn_bwd(dx2, x1, gt, up, g_ffn, w_g, w_u, w_d, T, tm)
    dg_ffn = jnp.sum(dg_ffn, axis=0, keepdims=True)
    grads["w_ffn_gate"] = _matmul_tn("dw_gate", dgt, h2).reshape(N_DEV, -1, D_MODEL)
    grads["w_ffn_up"] = _matmul_tn("dw_up", dup, h2).reshape(N_DEV, -1, D_MODEL)
    grads["w_ffn_down"] = _matmul_tn("dw_down", a, dx2).reshape(N_DEV, -1, D_MODEL)
    grads["w_ple_gate"] = d_pg.reshape(N_DEV, -1, D_MODEL)
    grads["w_ple_proj"] = d_pp

    dattn, do, dhg, dbg, dg_out, d_out, d_bra, d_brb, *sib_a = _merge_bwd(
        dx1, ya, yb, bg, o, hg, attn, m, rec, g_out, w_bra, w_brb, w_out, T, tm, xchg=(pick(GROUP_A), ()))
    parts_a = partials("a", GROUP_A, sib_a)
    dg_out = jnp.sum(dg_out, axis=0, keepdims=True)
    d_bra = d_bra.reshape(N_DEV, MLA_HEADS, HEAD_PAD, HEAD_PAD)[:, :, :V_DIM].reshape(N_DEV, HG_W, HEAD_PAD)
    grads["w_branch"] = jnp.concatenate([d_bra, d_brb], axis=1)
    grads["w_out"] = d_out.reshape(N_DEV, -1, D_MODEL)

    dhq, dhf, dhi, dlb, *got = _hgrn_bwd(hq, hf, hi, do, s0, lb, T, xchg=(pick(GROUP_B), parts_a))
    sib_b, got_a = got[:len(GROUP_B)], got[len(GROUP_B):]
    parts_b = partials("b", GROUP_B, sib_b)
    dqf, dkf, dvf, *got_b = _flash_bwd(qf, kf, vf, attn, dattn, lse, T, xchg=((), parts_b))
    (dcq, dckv, dkr, dg_qa, dg_kva, dg_qn, dg_kn, d_uq, d_ukv) = _mla_prep_bwd(
        cq, ckv, kr, pos, dqf, dkf, dvf, g_qa, g_kva, g_qn, g_kn, w_uq, w_ukv, T, tm)
    grad_x, dproj, dg_mix = _in_proj_bwd(x, dx1, [dcq, dckv, dkr, dhq, dhf, dhi, dhg, dbg], g_mix, w_in, T, tm)
    dg_mix = jnp.sum(dg_mix, axis=0, keepdims=True)
    grads["w_in"] = _matmul_tn_blocks("dw_in", h, dproj)
    grads["w_uq"] = d_uq[:, :QK_DIM]
    grads["w_ukv"] = d_ukv
    parts_c = ()
    if dist:
        parts_c = partials("c", GROUP_C, _exchange_sibling("rs_sibling_c", pick(GROUP_C)))
        gots.update(zip(GROUP_A, got_a))
        gots.update(zip(GROUP_B, got_b))

    dl0 = dlb * lb * (1.0 - lb)
    small_g = {
        "mix_norm_g": dg_mix, "q_a_norm_g": dg_qa, "kv_a_norm_g": dg_kva,
        "q_norm_g": dg_qn[:, :QK_DIM], "k_norm_g": dg_kn[:, :QK_DIM],
        "hg_lb_logits": jnp.concatenate([dl0, -dl0], axis=0), "hg_out_norm_g": dg_out,
        "ffn_norm_g": dg_ffn, "ple_gate_norm_g": dg_pg, "ple_post_norm_g": dg_post,
    }
    return loss_p, grad_x, small_g, grads, sibs, gots, parts_c


def _lower_bound(logits):
    def body(l_ref, lb_ref):
        l = l_ref[...]
        mx = jnp.max(l, axis=0, keepdims=True)
        e = jnp.exp(l - mx)
        lb_ref[...] = e[0:1] / jnp.sum(e, axis=0, keepdims=True)

    return pl.pallas_call(body, name="lower_bound", out_shape=jax.ShapeDtypeStruct((1, HG_W), F32))(logits)


def _my_place():
    return lax.axis_index("x"), lax.axis_index("y"), lax.axis_index("c")


def _all_gather(name, blocks):
    n = len(blocks)

    def body(*refs):
        x_refs, out_refs, sems = refs[:n], refs[n:2 * n], refs[2 * n:]
        _ag_start(x_refs, out_refs, sems)
        _ag_finish(x_refs, out_refs, sems)

    any_spec = pl.BlockSpec(memory_space=pl.ANY)
    return pl.pallas_call(
        body, name=name, out_shape=_ag_out_shapes(blocks),
        in_specs=[any_spec] * n, out_specs=[any_spec] * n, scratch_shapes=_ag_sems(n),
    )(*blocks)


def _ag_out_shapes(blocks):
    return [jax.ShapeDtypeStruct((N_DEV,) + b.shape, b.dtype) for b in blocks]


def _ag_sems(n):
    return [pltpu.SemaphoreType.DMA((7 * n,)), pltpu.SemaphoreType.DMA((7 * n,)), pltpu.SemaphoreType.DMA((n,))]


def _ag_parts(x_refs, out_refs, sems):
    send_sems, recv_sems, local_sems = sems
    x, y, c = _my_place()
    me, sibling = (x, y, c), (x, y, 1 - c)
    chips = [(1 - x, y), (x, 1 - y), (1 - x, 1 - y)]
    n = len(x_refs)

    def copy(a, k, block, to, own=False):
        px, py, pc = block
        dst = out_refs[a].at[4 * px + 2 * py + pc]
        return pltpu.make_async_remote_copy(
            src_ref=x_refs[a] if own else dst, dst_ref=dst, send_sem=send_sems.at[7 * a + k],
            recv_sem=recv_sems.at[7 * a + k], device_id=to, device_id_type=MESH_ID)

    mine = [pltpu.make_async_copy(x_refs[a], out_refs[a].at[4 * x + 2 * y + c], local_sems.at[a]) for a in range(n)]
    first = []
    for a in range(n):
        first.append(copy(a, 0, me, sibling, own=True))
        first += [copy(a, 1 + j, me, (*chip, c), own=True) for j, chip in enumerate(chips)]
    return copy, mine, first, me, sibling, chips, c, n


def _ag_start(x_refs, out_refs, sems):
    _, mine, first, *_ = _ag_parts(x_refs, out_refs, sems)
    for cp in mine + first:
        cp.start()


def _ag_finish(x_refs, out_refs, sems):
    copy, mine, first, me, sibling, chips, c, n = _ag_parts(x_refs, out_refs, sems)
    passed = []
    for j, chip in enumerate(chips):
        for a in range(n):
            copy(a, 1 + j, (*chip, c), me).wait_recv()
            passed.append(copy(a, 4 + j, (*chip, c), sibling))
            passed[-1].start()
    for a in range(n):
        copy(a, 0, sibling, me).wait_recv()
    for j, chip in enumerate(chips):
        for a in range(n):
            copy(a, 4 + j, (*chip, 1 - c), me).wait_recv()
    for cp in first + passed:
        cp.wait_send()
    for cp in mine:
        cp.wait()


def _exchange_sibling(name, gs):
    return _exchange(name, (gs, ()))


def _exchange(name, xchg):
    n = _x_count(xchg)

    def body(*refs):
        in_refs, out_refs, sems = refs[:n], refs[n:2 * n], refs[2 * n:]
        for cp in _x_copies(len(xchg[0]), in_refs, out_refs, sems):
            cp.start()
        for cp in _x_copies(len(xchg[0]), in_refs, out_refs, sems):
            cp.wait()

    any_spec = pl.BlockSpec(memory_space=pl.ANY)
    return pl.pallas_call(
        body, name=name, out_shape=_x_out_shapes(xchg), in_specs=[any_spec] * n, out_specs=[any_spec] * n,
        scratch_shapes=_x_sems(xchg),
    )(*xchg[0], *xchg[1])


N_PARTS = 4


def _part_spec(rows, cols, t_pos, lead_block=(), lead_index=lambda *args: ()):
    if rows % (16 * N_PARTS) == 0:
        axis, shape, count = 0, (rows // N_PARTS, cols), N_PARTS
    elif cols % (128 * N_PARTS) == 0:
        axis, shape, count = 1, (rows, cols // N_PARTS), N_PARTS
    else:
        axis, shape, count = 0, (rows, cols), 1

    def index(*args):
        i = jnp.minimum(args[t_pos], count - 1)
        return (*lead_index(*args), *((i, 0) if axis == 0 else (0, i)))

    return pl.BlockSpec((*lead_block, *shape), index)


def _chip_partials(name, gs, sibs, c_idx):
    n = len(gs)

    def body(c_ref, *refs):
        for g_ref, sib_ref, out_ref in zip(refs[:n], refs[n:2 * n], refs[2 * n:]):
            out_ref[...] = (g_ref[...] + sib_ref[...]).astype(MM)

    own = [_part_spec(*g.shape[1:], 1, (1,), lambda j, t, c_ref: (2 * j + c_ref[0],)) for g in gs]
    by_chip = [_part_spec(*g.shape[1:], 1, (1,), lambda j, t, c_ref: (j,)) for g in gs]
    grid_spec = pltpu.PrefetchScalarGridSpec(
        num_scalar_prefetch=1, grid=(4, N_PARTS), in_specs=own + by_chip, out_specs=by_chip)
    return pl.pallas_call(
        body, name=name, grid_spec=grid_spec, out_shape=[jax.ShapeDtypeStruct((4,) + g.shape[1:], MM) for g in gs],
        compiler_params=_cparams(("arbitrary", "arbitrary"), VMEM_LIMIT),
    )(c_idx, *gs, *sibs)


def _exchange_chips(parts):
    return _exchange("rs_chips", ((), parts))


def _x_count(xchg):
    return len(xchg[0]) + len(xchg[1])


def _x_out_shapes(xchg):
    return ([jax.ShapeDtypeStruct((4,) + g.shape[1:], g.dtype) for g in xchg[0]]
            + [jax.ShapeDtypeStruct((3,) + p.shape[1:], p.dtype) for p in xchg[1]])


def _x_sems(xchg):
    n = 4 * len(xchg[0]) + 3 * len(xchg[1])
    return [pltpu.SemaphoreType.DMA((n,)), pltpu.SemaphoreType.DMA((n,))] if n else []


def _x_copies(n_sib, in_refs, out_refs, sems):
    if not in_refs:
        return []
    send_sems, recv_sems = sems
    x, y, c = _my_place()
    chips = [(1 - x, y), (x, 1 - y), (1 - x, 1 - y)]
    copies = []

    def add(src, dst, to):
        k = len(copies)
        copies.append(pltpu.make_async_remote_copy(
            src_ref=src, dst_ref=dst, send_sem=send_sems.at[k], recv_sem=recv_sems.at[k], device_id=to,
            device_id_type=MESH_ID))

    for a, (src, dst) in enumerate(zip(in_refs, out_refs)):
        if a < n_sib:
            for j in range(4):
                add(src.at[2 * j + 1 - c], dst.at[j], (x, y, 1 - c))
        else:
            for k, (px, py) in enumerate(chips):
                add(src.at[2 * px + py], dst.at[k], (px, py, c))
    return copies


def _adamw_math(w, g, m, v):
    m = ADAM_B1 * m + (1.0 - ADAM_B1) * g
    v = ADAM_B2 * v + (1.0 - ADAM_B2) * jnp.square(g)
    m_hat = m / (1.0 - ADAM_B1 ** ADAM_STEP)
    v_hat = v / (1.0 - ADAM_B2 ** ADAM_STEP)
    delta = -ADAM_LR * (m_hat / (jnp.sqrt(v_hat) + ADAM_EPS) + ADAM_WD * w)
    return delta, m, v


def _sum_adamws(name, gs, sibs, gots, ws, ms, vs, slot_idx, chip_idx):
    n = len(gs)

    def body(s_ref, j_ref, *refs):
        ins, outs = refs[:6 * n], refs[6 * n:]
        for a in range(n):
            g_ref, sib_ref, got_ref, w_ref, m_ref, v_ref = (ins[k * n + a] for k in range(6))
            go_ref, d_ref, m2_ref, v2_ref = outs[4 * a:4 * a + 4]
            grad = g_ref[0] + sib_ref[0]
            for k in range(3):
                grad = grad + got_ref[k].astype(F32)
            go_ref[...] = grad
            d_ref[...], m2_ref[...], v2_ref[...] = _adamw_math(w_ref[...], grad, m_ref[...], v_ref[...])

    shapes = [g.shape[1:] for g in gs]
    flat = [_part_spec(*s, 0) for s in shapes]
    in_specs = ([_part_spec(*s, 0, (1,), lambda t, s_ref, j_ref: (s_ref[0],)) for s in shapes]
                + [_part_spec(*s, 0, (1,), lambda t, s_ref, j_ref: (j_ref[0],)) for s in shapes]
                + [_part_spec(*s, 0, (3,), lambda t, s_ref, j_ref: (0,)) for s in shapes] + flat * 3)
    grid_spec = pltpu.PrefetchScalarGridSpec(
        num_scalar_prefetch=2, grid=(N_PARTS,), in_specs=in_specs, out_specs=[f for f in flat for _ in range(4)])
    res = pl.pallas_call(
        body, name=name, grid_spec=grid_spec,
        out_shape=[jax.ShapeDtypeStruct(s, F32) for s in shapes for _ in range(4)],
        compiler_params=_cparams(("arbitrary",), VMEM_LIMIT),
    )(slot_idx, chip_idx, *gs, *sibs, *gots, *ws, *ms, *vs)
    return [res[4 * a:4 * a + 4] for a in range(n)]


def _adamw_small(parts, w, m, v):
    rows = w.shape[0]

    def body(p_ref, w_ref, m_ref, v_ref, g_ref, d_ref, m2_ref, v2_ref):
        g = p_ref[0]
        for d in range(1, N_DEV):
            g = g + p_ref[d]
        g_ref[...] = g
        d_ref[...], m2_ref[...], v2_ref[...] = _adamw_math(w_ref[...], g, m_ref[...], v_ref[...])

    return pl.pallas_call(
        body, name="adamw_small", out_shape=[jax.ShapeDtypeStruct((rows, 128), F32)] * 4,
    )(parts, w, m, v)


BIG = ("w_in", "w_uq", "w_ukv", "w_branch", "w_out", "w_ffn_gate", "w_ffn_up", "w_ffn_down", "w_ple_gate", "w_ple_proj")
SMALL = (
    ("mix_norm_g", 1024), ("q_a_norm_g", 384), ("kv_a_norm_g", 256), ("q_norm_g", 96), ("k_norm_g", 96),
    ("hg_lb_logits", 1024), ("hg_out_norm_g", 128), ("ffn_norm_g", 1024), ("ple_gate_norm_g", 1024),
    ("ple_post_norm_g", 1024),
)
SMALL_ROWS = 56


def _pack_small(vals):
    rows = []
    for name, n in SMALL:
        v = vals[name].reshape(1, -1).astype(F32)
        rows.append(jnp.pad(v, ((0, 0), (0, (-n) % 128))).reshape(-1, 128))
    return jnp.concatenate(rows, axis=0)


def _unpack_small(packed, shapes):
    out, r = {}, 0
    for name, n in SMALL:
        k = (n + 127) // 128
        out[name] = packed[r:r + k].reshape(1, -1)[:, :n].reshape(shapes[name])
        r += k
    return out


_WEIGHTS = ["mix_norm_g", "w_in", "q_a_norm_g", "w_uq", "kv_a_norm_g", "w_ukv", "q_norm_g", "k_norm_g", "hg_lb_logits",
            "hg_out_norm_g", "w_branch", "w_out", "ffn_norm_g", "w_ffn_gate", "w_ffn_up", "w_ffn_down",
            "ple_gate_norm_g", "w_ple_gate", "w_ple_proj", "ple_post_norm_g"]


def _step(x, p, positions, tgt, w, m, v):
    small_names = [n for n, _ in SMALL]
    T = x.shape[1]
    px, py, pc = _my_place()
    as_idx = lambda t: jnp.reshape(t, (1,)).astype(jnp.int32)

    def two_d(n, t):
        t = t.reshape(-1, t.shape[-1])
        return t.T if n in TRANSPOSED else t

    def full_shape(n, t):
        return (t.T if n in TRANSPOSED else t).reshape(w[n].shape)

    blocks = {n: two_d(n, w[n]).astype(MM) for n in BIG}
    big = dict(zip(EARLY, _all_gather("ag_weights", [blocks[n] for n in EARLY])))
    small = {n: (w[n] if n == "hg_lb_logits" else w[n].reshape(1, -1)) for n in small_names}

    loss_p, grad_x, small_g, grads, sibs, gots, parts_c = _local_step(
        x[0], p[0, 0], positions.reshape(T, 1), tgt[0], small, big, late_blocks=blocks, core=as_idx(pc))

    gots.update(zip(GROUP_C, _exchange_chips(parts_c)))
    out_g, out_d, out_m, out_v = {}, {}, {}, {}
    for tag, names in (("ab", GROUP_A + GROUP_B), ("c", GROUP_C)):
        pick = lambda table: [table[n] for n in names]
        res = _sum_adamws("adamw_" + tag, pick(grads), pick(sibs), pick(gots), [two_d(n, w[n]) for n in names],
                          [two_d(n, m[n]) for n in names], [two_d(n, v[n]) for n in names],
                          as_idx(4 * px + 2 * py + pc), as_idx(2 * px + py))
        for n, r in zip(names, res):
            out_g[n], out_d[n], out_m[n], out_v[n] = [full_shape(n, t) for t in r]

    packed_g = _pack_small(small_g)
    loss_row = jnp.concatenate([jnp.pad(jnp.sum(loss_p).reshape(1, 1), ((0, 0), (0, 127))),
                                jnp.zeros((SMALL_ROWS - packed_g.shape[0] - 1, 128), F32)], axis=0)
    parts = _all_gather("ag_small", [jnp.concatenate([packed_g, loss_row], axis=0)])[0]
    pad_rows = lambda t: jnp.pad(t, ((0, SMALL_ROWS - t.shape[0]), (0, 0)))
    sw = pad_rows(_pack_small({n: w[n] for n in small_names}))
    sm = pad_rows(_pack_small({n: m[n] for n in small_names}))
    sv = pad_rows(_pack_small({n: v[n] for n in small_names}))
    g_s, d_s, m_s, v_s = _adamw_small(parts, sw, sm, sv)
    shapes = {n: w[n].shape for n in small_names}
    n_packed = packed_g.shape[0]
    loss = g_s[n_packed, 0]
    for src, dst in ((g_s, out_g), (d_s, out_d), (m_s, out_m), (v_s, out_v)):
        dst.update(_unpack_small(src, shapes))

    outs = [loss, grad_x[None]]
    for table in (out_g, out_d, out_m, out_v):
        outs += [table[n] for n in _WEIGHTS]
    return tuple(outs)


def kernel(x, p, positions, mix_norm_g, w_in, q_a_norm_g, w_uq, kv_a_norm_g, w_ukv, q_norm_g, k_norm_g, hg_lb_logits, hg_out_norm_g, w_branch, w_out, ffn_norm_g, w_ffn_gate, w_ffn_up, w_ffn_down, ple_gate_norm_g, w_ple_gate, w_ple_proj, ple_post_norm_g, loss_target, m_mix_norm_g, m_w_in, m_q_a_norm_g, m_w_uq, m_kv_a_norm_g, m_w_ukv, m_q_norm_g, m_k_norm_g, m_hg_lb_logits, m_hg_out_norm_g, m_w_branch, m_w_out, m_ffn_norm_g, m_w_ffn_gate, m_w_ffn_up, m_w_ffn_down, m_ple_gate_norm_g, m_w_ple_gate, m_w_ple_proj, m_ple_post_norm_g, v_mix_norm_g, v_w_in, v_q_a_norm_g, v_w_uq, v_kv_a_norm_g, v_w_ukv, v_q_norm_g, v_k_norm_g, v_hg_lb_logits, v_hg_out_norm_g, v_w_branch, v_w_out, v_ffn_norm_g, v_w_ffn_gate, v_w_ffn_up, v_w_ffn_down, v_ple_gate_norm_g, v_w_ple_gate, v_w_ple_proj, v_ple_post_norm_g):
    w = dict(mix_norm_g=mix_norm_g, w_in=w_in, q_a_norm_g=q_a_norm_g, w_uq=w_uq, kv_a_norm_g=kv_a_norm_g, w_ukv=w_ukv,
             q_norm_g=q_norm_g, k_norm_g=k_norm_g, hg_lb_logits=hg_lb_logits, hg_out_norm_g=hg_out_norm_g,
             w_branch=w_branch, w_out=w_out, ffn_norm_g=ffn_norm_g, w_ffn_gate=w_ffn_gate, w_ffn_up=w_ffn_up,
             w_ffn_down=w_ffn_down, ple_gate_norm_g=ple_gate_norm_g, w_ple_gate=w_ple_gate, w_ple_proj=w_ple_proj,
             ple_post_norm_g=ple_post_norm_g)
    m = dict(mix_norm_g=m_mix_norm_g, w_in=m_w_in, q_a_norm_g=m_q_a_norm_g, w_uq=m_w_uq, kv_a_norm_g=m_kv_a_norm_g,
             w_ukv=m_w_ukv, q_norm_g=m_q_norm_g, k_norm_g=m_k_norm_g, hg_lb_logits=m_hg_lb_logits,
             hg_out_norm_g=m_hg_out_norm_g, w_branch=m_w_branch, w_out=m_w_out, ffn_norm_g=m_ffn_norm_g,
             w_ffn_gate=m_w_ffn_gate, w_ffn_up=m_w_ffn_up, w_ffn_down=m_w_ffn_down,
             ple_gate_norm_g=m_ple_gate_norm_g, w_ple_gate=m_w_ple_gate, w_ple_proj=m_w_ple_proj,
             ple_post_norm_g=m_ple_post_norm_g)
    v = dict(mix_norm_g=v_mix_norm_g, w_in=v_w_in, q_a_norm_g=v_q_a_norm_g, w_uq=v_w_uq, kv_a_norm_g=v_kv_a_norm_g,
             w_ukv=v_w_ukv, q_norm_g=v_q_norm_g, k_norm_g=v_k_norm_g, hg_lb_logits=v_hg_lb_logits,
             hg_out_norm_g=v_hg_out_norm_g, w_branch=v_w_branch, w_out=v_w_out, ffn_norm_g=v_ffn_norm_g,
             w_ffn_gate=v_w_ffn_gate, w_ffn_up=v_w_ffn_up, w_ffn_down=v_w_ffn_down,
             ple_gate_norm_g=v_ple_gate_norm_g, w_ple_gate=v_w_ple_gate, w_ple_proj=v_w_ple_proj,
             ple_post_norm_g=v_ple_post_norm_g)
    return _step(x, p, positions, loss_target, w, m, v)
```

```python
import functools

import jax
import jax.numpy as jnp
import numpy as np
from jax import lax
from jax.experimental import pallas as pl
from jax.experimental.pallas import tpu as pltpu

F32 = jnp.float32
MM = jnp.bfloat16
HI = lax.Precision.HIGHEST
MESH_ID = pl.DeviceIdType.MESH

D_MODEL = 1024
N_DEV = 8
MLA_HEADS = 8
QK_NOPE = 64
QK_ROPE = 32
QK_DIM = 96
V_DIM = 64
HEAD_PAD = 128
Q_RANK = 384
KV_RANK = 256
ROPE_BASE = 10000.0
HG_HEADS = 4
HG_DIM = 128
HG_W = 512
HG_CHUNK = 64
FFN = 2816
PLE = 256
EPS = 1e-6
ATT_SCALE = QK_DIM ** -0.5
NEG = -1e30

ADAM_LR = 0.001
ADAM_B1 = 0.9
ADAM_B2 = 0.999
ADAM_EPS = 1e-08
ADAM_WD = 0.01
ADAM_STEP = 10

SEC_CQ = (0, 384)
SEC_CKV = (384, 256)
SEC_KR = (640, 128)
SEC_HQ = (768, 512)
SEC_HF = (1280, 512)
SEC_HI = (1792, 512)
SEC_HG = (2304, 512)
SEC_BG = (2816, 2048)
IN_PAD = 4864
SECTIONS = (SEC_CQ, SEC_CKV, SEC_KR, SEC_HQ, SEC_HF, SEC_HI, SEC_HG, SEC_BG)
COL_SECTIONS = ((0, 384), (384, 256), (640, 32), (672, 512), (1184, 512), (1696, 512), (2208, 512), (2720, 2048))
IN_COLS = 4768
IN_BLOCK = IN_COLS // 8

VMEM_LIMIT = 58 * 1024 * 1024
ROW_TILE = 256
ATT_TILE = 1024
ATT_HEADS = 4
HG_BLOCK = 512
HG_UNROLL = 4


def _dot(a, b):
    return jnp.dot(a.astype(MM), b.astype(MM), preferred_element_type=F32)


def _dot_nt(a, b):
    return lax.dot_general(a.astype(MM), b.astype(MM), (((1,), (1,)), ((), ())), preferred_element_type=F32)


def _dot_tn(a, b):
    return lax.dot_general(a.astype(MM), b.astype(MM), (((0,), (0,)), ((), ())), preferred_element_type=F32)


def _dot_hi(a, b):
    return jnp.dot(a, b, preferred_element_type=F32, precision=HI)


def _sigmoid(x):
    return 1.0 / (1.0 + jnp.exp(-x))


def _rms(x, n=None):
    n = x.shape[-1] if n is None else n
    r = lax.rsqrt(jnp.sum(x * x, axis=-1, keepdims=True) * (1.0 / n) + EPS)
    return x * r, r


def _rms_bwd(dxh, xh, r, n=None):
    n = xh.shape[-1] if n is None else n
    return r * (dxh - xh * (jnp.sum(dxh * xh, axis=-1, keepdims=True) * (1.0 / n)))


def _rope_tables(pos, tm):
    lane = lax.broadcasted_iota(jnp.int32, (tm, HEAD_PAD), 1)
    idx = jnp.where(lane < QK_NOPE + QK_ROPE // 2, lane - QK_NOPE, lane - QK_NOPE - QK_ROPE // 2)
    inv = jnp.exp(idx.astype(F32) * (-np.log(ROPE_BASE) * 2.0 / QK_ROPE))
    ang = pos.astype(F32) * inv
    in_rope = (lane >= QK_NOPE) & (lane < QK_DIM)
    first = lane < QK_NOPE + QK_ROPE // 2
    cos_t = jnp.where(in_rope, jnp.cos(ang), 1.0)
    sin_t = jnp.where(in_rope, jnp.where(first, -jnp.sin(ang), jnp.sin(ang)), 0.0)
    return cos_t, sin_t, (first, in_rope)


def _rope_swap(x, halves):
    first, in_rope = halves
    half = QK_ROPE // 2
    return jnp.where(in_rope, jnp.where(first, pltpu.roll(x, HEAD_PAD - half, 1), pltpu.roll(x, half, 1)), 0.0)


def _cparams(sem, vmem=None):
    return pltpu.CompilerParams(dimension_semantics=sem, vmem_limit_bytes=vmem)


def _row_call(name, body, T, tm, row_ins, full_ins, row_outs, acc_outs, vmem=None, scratch=(), xchg=((), ())):
    n_in, n_out, n_x = len(row_ins) + len(full_ins), len(row_outs) + len(acc_outs), _x_count(xchg)
    steps = T // tm

    def kern(*refs):
        ins, x_in, refs = refs[:n_in], refs[n_in:n_in + n_x], refs[n_in + n_x:]
        outs, x_out, refs = refs[:n_out], refs[n_out:n_out + n_x], refs[n_out + n_x:]
        scr, x_sems = refs[:len(scratch)], refs[len(scratch):]
        i = pl.program_id(0)
        if n_x:
            @pl.when(i == 0)
            def _():
                for cp in _x_copies(len(xchg[0]), x_in, x_out, x_sems):
                    cp.start()

        body(i, *ins, *outs, *scr)
        if n_x:
            @pl.when(i == steps - 1)
            def _():
                for cp in _x_copies(len(xchg[0]), x_in, x_out, x_sems):
                    cp.wait()

    any_spec = pl.BlockSpec(memory_space=pl.ANY)
    in_specs = [pl.BlockSpec((tm, a.shape[1]), lambda i: (i, 0)) for a in row_ins]
    in_specs += [pl.BlockSpec(a.shape, lambda i, nd=a.ndim: (0,) * nd, pipeline_mode=pl.Buffered(1)) for a in full_ins]
    out_specs = [pl.BlockSpec((tm, n), lambda i: (i, 0)) for n, _ in row_outs]
    out_specs += [pl.BlockSpec(s, lambda i, nd=len(s): (0,) * nd) for s, _ in acc_outs]
    out_shape = [jax.ShapeDtypeStruct((T, n), dt) for n, dt in row_outs]
    out_shape += [jax.ShapeDtypeStruct(s, dt) for s, dt in acc_outs]
    return pl.pallas_call(
        kern, name=name, grid=(steps,), in_specs=in_specs + [any_spec] * n_x, out_specs=out_specs + [any_spec] * n_x,
        out_shape=out_shape + _x_out_shapes(xchg), scratch_shapes=list(scratch) + _x_sems(xchg),
        compiler_params=_cparams(("arbitrary",), vmem),
    )(*row_ins, *full_ins, *xchg[0], *xchg[1])


FFN_HALVES = (slice(0, FFN // 2), slice(FFN // 2, FFN))
ROW_CHUNK = 16
CHUNK_UNROLL = True


def _by_chunks(tm, fn):
    def step(c, carry):
        fn(pl.ds(pl.multiple_of(c * ROW_CHUNK, ROW_CHUNK), ROW_CHUNK))
        return carry

    lax.fori_loop(0, tm // ROW_CHUNK, step, 0, unroll=CHUNK_UNROLL)


def _fold8(x):
    return x[:8] + x[8:]


def _acc(ref, i, val):
    @pl.when(i == 0)
    def _():
        ref[...] = val

    @pl.when(i != 0)
    def _():
        ref[...] += val


def _in_proj_fwd(x, g_mix, w_in, T, tm):
    def body(i, x_ref, g_ref, w_ref, h_ref, *rest):
        outs, pj_s = rest[:-1], rest[-1]
        g = g_ref[...]

        def norm(rows):
            h_ref[rows, :] = (_rms(x_ref[rows, :])[0] * g).astype(MM)

        _by_chunks(tm, norm)
        for d in range(N_DEV):
            pj_s[d] = _dot_nt(h_ref[...], w_ref[d])

        def join_and_cut(rows):
            proj = jnp.concatenate([pj_s[d, rows, :] for d in range(N_DEV)], axis=1)
            for (s, n), o_ref in zip(COL_SECTIONS, outs):
                if n == QK_ROPE:
                    o_ref[rows, :] = jnp.concatenate(
                        [jnp.zeros((ROW_CHUNK, QK_NOPE), F32), proj[:, s:s + n],
                         jnp.zeros((ROW_CHUNK, HEAD_PAD - QK_DIM), F32)], axis=1)
                else:
                    o_ref[rows, :] = proj[:, s:s + n]

        _by_chunks(tm, join_and_cut)

    row_outs = [(D_MODEL, MM)] + [(n, F32) for _, n in SECTIONS]
    return _row_call("in_proj_fwd", body, T, tm, [x], [g_mix, w_in], row_outs, [], VMEM_LIMIT,
                     scratch=[pltpu.VMEM((N_DEV, tm, IN_BLOCK), F32)])


def _mla_heads_fwd(raw, g_pad, cos_t, sin_t, first):
    outs, saved = [], []
    for h in range(MLA_HEADS):
        xh, r = _rms(raw[:, h * HEAD_PAD:(h + 1) * HEAD_PAD], QK_DIM)
        y = xh * g_pad
        outs.append(y * cos_t + _rope_swap(y, first) * sin_t)
        saved.append((xh, r))
    return outs, saved


def _mla_raw_heads(cqn, ckvn, kr, wuq_ref, wukv_ref, tm):
    lane = lax.broadcasted_iota(jnp.int32, (tm, HEAD_PAD), 1)
    nope = lane < QK_NOPE
    one_lane = jnp.where(lane == V_DIM, 1.0, 0.0)
    qs, ks, vs = [], [], []
    for h in range(MLA_HEADS):
        qs.append(_dot_nt(cqn, wuq_ref[h]))
        kv = _dot(ckvn, wukv_ref[h])
        ks.append(jnp.where(nope, kv, kr))
        vs.append(jnp.where(nope, pltpu.roll(kv, V_DIM, 1), one_lane))
    return jnp.concatenate(qs, axis=1), jnp.concatenate(ks, axis=1), jnp.concatenate(vs, axis=1)


def _mla_prep_fwd(cq, ckv, kr, pos, g_qa, g_kva, g_qn, g_kn, w_uq, w_ukv, T, tm):
    def body(i, cq_ref, ckv_ref, kr_ref, pos_ref, gqa_ref, gkva_ref, gqn_ref, gkn_ref, wuq_ref, wukv_ref,
             q_ref, k_ref, v_ref):
        cos_t, sin_t, first = _rope_tables(pos_ref[...], tm)
        cqn = _rms(cq_ref[...])[0] * gqa_ref[...]
        ckvn = _rms(ckv_ref[...])[0] * gkva_ref[...]
        q_raw, k_raw, v = _mla_raw_heads(cqn, ckvn, kr_ref[...], wuq_ref, wukv_ref, tm)
        qs, _ = _mla_heads_fwd(q_raw, gqn_ref[...], cos_t, sin_t, first)
        ks, _ = _mla_heads_fwd(k_raw, gkn_ref[...], cos_t, sin_t, first)
        q_ref[...] = (jnp.concatenate(qs, axis=1) * ATT_SCALE).astype(MM)
        k_ref[...] = jnp.concatenate(ks, axis=1).astype(MM)
        v_ref[...] = v.astype(MM)

    w = MLA_HEADS * HEAD_PAD
    return _row_call("mla_prep_fwd", body, T, tm, [cq, ckv, kr, pos], [g_qa, g_kva, g_qn, g_kn, w_uq, w_ukv],
                     [(w, MM), (w, MM), (w, MM)], [])


def _causal_pairs(n, by_query):
    if by_query:
        pairs = [(q, k) for q in range(n) for k in range(q + 1)]
    else:
        pairs = [(q, k) for k in range(n) for q in range(k, n)]
    return np.array([p[0] for p in pairs], np.int32), np.array([p[1] for p in pairs], np.int32)


def _flash_fwd(qf, kf, vf, T, ag_blocks=()):
    tq = min(ATT_TILE, T)
    nq = T // tq

    qi_tab, ki_tab = _causal_pairs(nq, by_query=True)

    hp = ATT_HEADS

    n_ag = len(ag_blocks)
    n_heads, n_pairs = MLA_HEADS // hp, len(qi_tab)

    def body(qi_ref, ki_ref, q_ref, k_ref, v_ref, *rest):
        ag_in, (o_ref, lse_ref), rest = rest[:n_ag], rest[n_ag:n_ag + 2], rest[n_ag + 2:]
        ag_out, (m_s, acc_s), ag_sems = rest[:n_ag], rest[n_ag:n_ag + 2], rest[n_ag + 2:]
        t = pl.program_id(1)
        qi, ki = qi_ref[t], ki_ref[t]
        if n_ag:
            @pl.when((pl.program_id(0) == 0) & (t == 0))
            def _():
                _ag_start(ag_in, ag_out, ag_sems)

        @pl.when(ki == 0)
        def _():
            m_s[...] = jnp.full_like(m_s, NEG)
            acc_s[...] = jnp.zeros_like(acc_s)

        def step(masked):
            for hh in range(hp):
                hs = slice(hh * HEAD_PAD, (hh + 1) * HEAD_PAD)
                s_t = _dot_nt(k_ref[:, hs], q_ref[:, hs])
                if masked:
                    key = lax.broadcasted_iota(jnp.int32, (tq, tq), 0)
                    qry = lax.broadcasted_iota(jnp.int32, (tq, tq), 1)
                    s_t = jnp.where(key <= qry, s_t, NEG)
                m_old = m_s[hh]
                m_new = jnp.maximum(m_old, jnp.max(s_t, axis=0, keepdims=True))
                p_t = jnp.exp(s_t - m_new)
                acc_s[hh] = jnp.exp(m_old - m_new) * acc_s[hh] + _dot_tn(v_ref[:, hs], p_t)
                m_s[hh] = m_new

        @pl.when(ki < qi)
        def _():
            step(False)

        @pl.when(ki == qi)
        def _():
            step(True)
            real = lax.broadcasted_iota(jnp.int32, (HEAD_PAD, tq), 0) < V_DIM
            for hh in range(hp):
                hs = slice(hh * HEAD_PAD, (hh + 1) * HEAD_PAD)
                acc = acc_s[hh]
                l = acc[V_DIM:V_DIM + 1]
                o_ref[:, hs] = jnp.where(real, acc / l, 0.0).T
                lse_ref[:, hs] = jnp.broadcast_to(m_s[hh] + jnp.log(l), (HEAD_PAD, tq)).T

        if n_ag:
            @pl.when((pl.program_id(0) == n_heads - 1) & (t == n_pairs - 1))
            def _():
                _ag_finish(ag_in, ag_out, ag_sems)

    q_spec = pl.BlockSpec((tq, hp * HEAD_PAD), lambda h, t, qi_ref, ki_ref: (qi_ref[t], h))
    kv_spec = pl.BlockSpec((tq, hp * HEAD_PAD), lambda h, t, qi_ref, ki_ref: (ki_ref[t], h))
    any_spec = pl.BlockSpec(memory_space=pl.ANY)
    grid_spec = pltpu.PrefetchScalarGridSpec(
        num_scalar_prefetch=2, grid=(n_heads, n_pairs),
        in_specs=[q_spec, kv_spec, kv_spec] + [any_spec] * n_ag, out_specs=[q_spec, q_spec] + [any_spec] * n_ag,
        scratch_shapes=[pltpu.VMEM((hp, 1, tq), F32), pltpu.VMEM((hp, HEAD_PAD, tq), F32)]
        + (_ag_sems(n_ag) if n_ag else []))
    return pl.pallas_call(
        body, name="flash_fwd", grid_spec=grid_spec,
        out_shape=[jax.ShapeDtypeStruct((T, MLA_HEADS * HEAD_PAD), F32)] * 2 + _ag_out_shapes(ag_blocks),
        compiler_params=_cparams(("arbitrary", "arbitrary")),
    )(jnp.asarray(qi_tab), jnp.asarray(ki_tab), qf, kf, vf, *ag_blocks)


def _hg_gates(hf, lb):
    sg = _sigmoid(hf)
    f = lb + (1.0 - lb) * sg
    return sg, f, jnp.log(f), 1.0 - f


def _tri(n, lower):
    r = lax.broadcasted_iota(jnp.int32, (n, n), 0)
    c = lax.broadcasted_iota(jnp.int32, (n, n), 1)
    return jnp.where((c <= r) if lower else (c >= r), 1.0, 0.0).astype(F32)


def _hg_levels():
    C = HG_CHUNK
    t = lax.broadcasted_iota(jnp.int32, (C, C), 0)
    s = lax.broadcasted_iota(jnp.int32, (C, C), 1)
    levels = []
    for shift in range(C.bit_length() - 2, -1, -1):
        pair_t, pair_s = lax.shift_right_logical(t, shift + 1), lax.shift_right_logical(s, shift + 1)
        later_t = (lax.shift_right_logical(t, shift) & 1) == 1
        earlier_s = (lax.shift_right_logical(s, shift) & 1) == 0
        levels.append((1 << shift, (pair_t == pair_s) & later_t & earlier_s))
    return levels, t == s


def _hg_refs(b):
    C, n = b.shape
    row = lax.broadcasted_iota(jnp.int32, (C, n), 0)
    back1, back2, ahead1 = pltpu.roll(b, 1, 0), pltpu.roll(b, 2, 0), pltpu.roll(b, C - 1, 0)
    refs = []
    for half in (32, 16, 8, 4):
        refs.append(jnp.concatenate(
            [jnp.broadcast_to(b[lo + half - 1:lo + half], (2 * half, n)) for lo in range(0, C, 2 * half)], axis=0))
    in4 = row & 3
    refs.append(jnp.where(in4 == 0, ahead1, jnp.where(in4 == 1, b, jnp.where(in4 == 2, back1, back2))))
    refs.append(jnp.where((row & 1) == 1, back1, b))
    return refs


def _hg_intra(q, k, b, refs, levels, eye):
    a = jnp.where(eye, jnp.sum(q * k, axis=1, keepdims=True), 0.0)
    saved = []
    for r, (_, mask) in zip(refs, levels):
        e = jnp.exp(-jnp.abs(b - r))
        q_t, k_t = q * e, k * e
        a = a + jnp.where(mask, _dot_nt(q_t, k_t), 0.0)
        saved.append((q_t, k_t, e))
    return a, saved


def _hg_intra_bwd(d_a, q, k, saved, levels, eye):
    diag = jnp.sum(jnp.where(eye, d_a, 0.0), axis=1, keepdims=True)
    dq, dk = diag * k, diag * q
    for (q_t, k_t, e), (_, mask) in zip(saved, levels):
        da = jnp.where(mask, d_a, 0.0)
        dq = dq + _dot(da, k_t) * e
        dk = dk + _dot_tn(da, q_t) * e
    return dq, dk


def _hgrn_fwd(hq, hf, hi, lb, T):
    rb = min(HG_BLOCK, T)
    ncb = rb // HG_CHUNK

    def body(hq_ref, hf_ref, hi_ref, lb_ref, o_ref, s0_ref, st_ref):
        @pl.when(pl.program_id(0) == 0)
        def _():
            st_ref[...] = jnp.zeros_like(st_ref)

        tril = _tri(HG_CHUNK, True)
        levels, eye = _hg_levels()

        def chunk(c, carry):
            rows = pl.ds(pl.multiple_of(c * HG_CHUNK, HG_CHUNK), HG_CHUNK)
            _, _, logf, kk = _hg_gates(hf_ref[rows, :], lb_ref[...])
            b = _dot_hi(tril, logf)
            refs = _hg_refs(b)
            q_all, v_all = hq_ref[rows, :], hi_ref[rows, :]
            outs = []
            for h in range(HG_HEADS):
                ls = slice(h * HG_DIM, (h + 1) * HG_DIM)
                q, k, v, bh = q_all[:, ls], kk[:, ls], v_all[:, ls], b[:, ls]
                st = st_ref[h]
                s0_ref[c, h * HG_DIM:(h + 1) * HG_DIM, :] = st
                b_end = bh[HG_CHUNK - 1:HG_CHUNK]
                a, _ = _hg_intra(q, k, bh, [r[:, ls] for r in refs], levels, eye)
                outs.append(_dot_nt(q * jnp.exp(bh), st) + _dot(a, v))
                st_ref[h] = st * jnp.exp(b_end) + _dot_tn(v, k * jnp.exp(b_end - bh))
            o_ref[rows, :] = jnp.concatenate(outs, axis=1)
            return carry

        lax.fori_loop(0, ncb, chunk, 0, unroll=HG_UNROLL)

    row = pl.BlockSpec((rb, HG_W), lambda i: (i, 0))
    return pl.pallas_call(
        body, name="hgrn_fwd", grid=(T // rb,),
        in_specs=[row, row, row, pl.BlockSpec((1, HG_W), lambda i: (0, 0))],
        out_specs=[row, pl.BlockSpec((ncb, HG_W, HG_DIM), lambda i: (i, 0, 0))],
        out_shape=[jax.ShapeDtypeStruct((T, HG_W), F32), jax.ShapeDtypeStruct((T // HG_CHUNK, HG_W, HG_DIM), F32)],
        scratch_shapes=[pltpu.VMEM((HG_HEADS, HG_DIM, HG_DIM), F32)],
        compiler_params=_cparams(("arbitrary",)),
    )(hq, hf, hi, lb)


def _hgrn_bwd(hq, hf, hi, do, s0, lb, T, xchg=((), ())):
    rb = min(HG_BLOCK, T)
    ncb = rb // HG_CHUNK
    nb = T // rb
    C = HG_CHUNK
    n_x, n_sib = _x_count(xchg), len(xchg[0])

    def body(hq_ref, hf_ref, hi_ref, do_ref, s0_ref, lb_ref, *rest):
        x_in, (dq_ref, df_ref, dv_ref, dlb_ref), rest = rest[:n_x], rest[n_x:n_x + 4], rest[n_x + 4:]
        x_out, dst_ref, x_sems = rest[:n_x], rest[n_x], rest[n_x + 1:]

        @pl.when(pl.program_id(0) == 0)
        def _():
            dst_ref[...] = jnp.zeros_like(dst_ref)
            dlb_ref[...] = jnp.zeros_like(dlb_ref)
            for cp in _x_copies(n_sib, x_in, x_out, x_sems):
                cp.start()

        tril, triu = _tri(C, True), _tri(C, False)
        row_cc = lax.broadcasted_iota(jnp.int32, (C, C), 0)
        col_cc = lax.broadcasted_iota(jnp.int32, (C, C), 1)
        last_row = lax.broadcasted_iota(jnp.int32, (C, HG_DIM), 0) == C - 1
        lb_v = lb_ref[...]
        levels, eye = _hg_levels()

        def chunk(cc, carry):
            c = ncb - 1 - cc
            rows = pl.ds(pl.multiple_of(c * C, C), C)
            hf_c = hf_ref[rows, :]
            sg, f, logf, kk = _hg_gates(hf_c, lb_v)
            b = _dot_hi(tril, logf)
            refs = _hg_refs(b)
            q_all, v_all, do_all = hq_ref[rows, :], hi_ref[rows, :], do_ref[rows, :]
            dq_o, dk_o, dv_o, db_o = [], [], [], []
            for h in range(HG_HEADS):
                ls = slice(h * HG_DIM, (h + 1) * HG_DIM)
                q, k, v, bh, d_o = q_all[:, ls], kk[:, ls], v_all[:, ls], b[:, ls], do_all[:, ls]
                st0 = s0_ref[c, h * HG_DIM:(h + 1) * HG_DIM, :]
                dst = dst_ref[h]
                b_end = bh[C - 1:C]
                e_b, e_end = jnp.exp(bh), jnp.exp(b_end)
                e_rem = jnp.exp(b_end - bh)
                qe, kd = q * e_b, k * e_rem
                st_end = st0 * e_end + _dot_tn(v, kd)
                a, saved = _hg_intra(q, k, bh, [r[:, ls] for r in refs], levels, eye)
                d_a = jnp.where(col_cc <= row_cc, _dot_nt(d_o, v), 0.0)
                dq_i, dk_i = _hg_intra_bwd(d_a, q, k, saved, levels, eye)
                dv = _dot_tn(a, d_o) + _dot_nt(kd, dst)
                dq = e_b * _dot(d_o, st0) + dq_i
                dk = e_rem * _dot(v, dst) + dk_i
                extra = jnp.sum(dst * st_end, axis=0, keepdims=True)
                db_o.append(q * dq - k * dk + jnp.where(last_row, extra, 0.0))
                dst_ref[h] = dst * e_end + _dot_tn(d_o, qe)
                dq_o.append(dq)
                dk_o.append(dk)
                dv_o.append(dv)
            dlogf = _dot_hi(triu, jnp.concatenate(db_o, axis=1))
            d_f = dlogf / f - jnp.concatenate(dk_o, axis=1)
            dq_ref[rows, :] = jnp.concatenate(dq_o, axis=1).astype(MM)
            dv_ref[rows, :] = jnp.concatenate(dv_o, axis=1).astype(MM)
            df_ref[rows, :] = (d_f * (1.0 - lb_v) * sg * (1.0 - sg)).astype(MM)
            dlb_ref[...] += jnp.sum(d_f * (1.0 - sg), axis=0, keepdims=True)
            return carry

        lax.fori_loop(0, ncb, chunk, 0, unroll=HG_UNROLL)

        if n_x:
            @pl.when(pl.program_id(0) == nb - 1)
            def _():
                for cp in _x_copies(n_sib, x_in, x_out, x_sems):
                    cp.wait()

    row = pl.BlockSpec((rb, HG_W), lambda i: (nb - 1 - i, 0))
    one = pl.BlockSpec((1, HG_W), lambda i: (0, 0))
    any_spec = pl.BlockSpec(memory_space=pl.ANY)
    return pl.pallas_call(
        body, name="hgrn_bwd", grid=(nb,),
        in_specs=[row, row, row, row, pl.BlockSpec((ncb, HG_W, HG_DIM), lambda i: (nb - 1 - i, 0, 0)), one]
        + [any_spec] * n_x,
        out_specs=[row, row, row, one] + [any_spec] * n_x,
        out_shape=[jax.ShapeDtypeStruct((T, HG_W), MM)] * 3 + [jax.ShapeDtypeStruct((1, HG_W), F32)]
        + _x_out_shapes(xchg),
        scratch_shapes=[pltpu.VMEM((HG_HEADS, HG_DIM, HG_DIM), F32)] + _x_sems(xchg),
        compiler_params=_cparams(("arbitrary",)),
    )(hq, hf, hi, do, s0, lb, *xchg[0], *xchg[1])


def _silu_parts(x):
    sg = _sigmoid(x)
    return x * sg, sg * (1.0 + x * (1.0 - sg))


def _merge_fwd(attn, o, hg, bg, x, g_out, w_bra, w_brb, w_out, T, tm):
    def body(i, attn_ref, o_ref, hg_ref, bg_ref, x_ref, g_ref, wa_ref, wb_ref, wo_ref,
             x1_ref, ya_ref, yb_ref, m_ref, rec_ref):
        g = g_ref[...]

        def recurrent_out(rows):
            for h in range(HG_HEADS):
                ls = slice(h * HG_DIM, (h + 1) * HG_DIM)
                rec_ref[rows, ls] = (_rms(o_ref[rows, ls])[0] * g * _silu_parts(hg_ref[rows, ls])[0]).astype(MM)

        _by_chunks(tm, recurrent_out)
        ya_ref[...] = _dot(attn_ref[...], wa_ref[...])
        yb_ref[...] = jnp.dot(rec_ref[...], wb_ref[...], preferred_element_type=F32)

        def gate(rows):
            m_ref[rows, :] = (_sigmoid(bg_ref[rows, :D_MODEL]) * ya_ref[rows, :]
                              + _sigmoid(bg_ref[rows, D_MODEL:]) * yb_ref[rows, :]).astype(MM)

        _by_chunks(tm, gate)
        x1_ref[...] = x_ref[...] + jnp.dot(m_ref[...], wo_ref[...], preferred_element_type=F32)

    return _row_call("merge_fwd", body, T, tm, [attn, o, hg, bg, x], [g_out, w_bra, w_brb, w_out],
                     [(D_MODEL, F32), (D_MODEL, F32), (D_MODEL, F32), (D_MODEL, MM), (HG_W, MM)], [], VMEM_LIMIT)


def _ffn_fwd(x1, g_ffn, w_g, w_u, w_d, T, tm):
    def body(i, x1_ref, g_ref, wg_ref, wu_ref, wd_ref, x2_ref, gt_ref, up_ref, h2_ref, a_s):
        g = g_ref[...]

        def norm(rows):
            h2_ref[rows, :] = (_rms(x1_ref[rows, :])[0] * g).astype(MM)

        _by_chunks(tm, norm)
        gt_ref[...] = _dot_nt(h2_ref[...], wg_ref[...])
        up_ref[...] = _dot_nt(h2_ref[...], wu_ref[...])

        def act(rows):
            for cs in FFN_HALVES:
                a_s[rows, cs] = (_silu_parts(gt_ref[rows, cs])[0] * up_ref[rows, cs]).astype(MM)

        _by_chunks(tm, act)
        x2_ref[...] = x1_ref[...] + jnp.dot(a_s[...], wd_ref[...], preferred_element_type=F32)

    return _row_call("ffn_fwd", body, T, tm, [x1], [g_ffn, w_g, w_u, w_d],
                     [(D_MODEL, F32), (FFN, F32), (FFN, F32), (D_MODEL, MM)], [], VMEM_LIMIT,
                     scratch=[pltpu.VMEM((tm, FFN), MM)])


def _ple_loss(x2, p, tgt, g_pg, g_post, w_pg, w_pp, T, tm):
    def body(i, x2_ref, p_ref, t_ref, gpg_ref, gpo_ref, wpg_ref, wpp_ref,
             dx2_ref, loss_ref, dgpo_ref, dgpg_ref, dwpg_ref, dwpp_ref, u_s, n3_s, z_s, dz_s, du_s, dy_s, dn3_s):
        @pl.when(i == 0)
        def _():
            for ref in (loss_ref, dgpo_ref, dgpg_ref, dwpg_ref, dwpp_ref):
                ref[...] = jnp.zeros_like(ref)

        gpg, gpo = gpg_ref[...], gpo_ref[...]
        p_mm = p_ref[...].astype(MM)
        for d in range(N_DEV):
            u_s[:, d * HEAD_PAD:(d + 1) * HEAD_PAD] = jnp.dot(p_mm, wpp_ref[d], preferred_element_type=F32)

        def gate_input(rows):
            n3_s[rows, :] = (_rms(x2_ref[rows, :])[0] * gpg).astype(MM)

        _by_chunks(tm, gate_input)
        z_s[...] = jnp.dot(n3_s[...], wpg_ref[...], preferred_element_type=F32)

        def loss_and_back(rows):
            uh, ru = _rms(u_s[rows, :])
            e = uh * gpo
            gate = _sigmoid(z_s[rows, :])
            diff = x2_ref[rows, :] + gate * e - t_ref[rows, :]
            dy = diff * (1.0 / D_MODEL)
            de = dy * gate
            dz_s[rows, :] = (dy * e * gate * (1.0 - gate)).astype(MM)
            du_s[rows, :] = _rms_bwd(de * gpo, uh, ru).astype(MM)
            dy_s[rows, :] = dy
            loss_ref[...] += _fold8(diff * diff) * (0.5 / D_MODEL)
            dgpo_ref[...] += _fold8(de * uh)

        _by_chunks(tm, loss_and_back)
        dn3_s[...] = _dot_nt(dz_s[...], wpg_ref[...])

        def gate_norm_back(rows):
            x2h, r3 = _rms(x2_ref[rows, :])
            dn3 = dn3_s[rows, :]
            dx2_ref[rows, :] = dy_s[rows, :] + _rms_bwd(dn3 * gpg, x2h, r3)
            dgpg_ref[...] += _fold8(dn3 * x2h)

        _by_chunks(tm, gate_norm_back)
        dwpg_ref[...] += _dot_tn(n3_s[...], dz_s[...])
        for d in range(N_DEV):
            dwpp_ref[d] += _dot_tn(p_mm, du_s[:, d * HEAD_PAD:(d + 1) * HEAD_PAD])

    vec = ((8, D_MODEL), F32)
    wide = lambda dt: pltpu.VMEM((tm, D_MODEL), dt)
    return _row_call("ple_loss", body, T, tm, [x2, p, tgt], [g_pg, g_post, w_pg, w_pp], [(D_MODEL, F32)],
                     [vec, vec, vec, ((D_MODEL, D_MODEL), F32), ((N_DEV, PLE, HEAD_PAD), F32)], VMEM_LIMIT,
                     scratch=[wide(F32), wide(MM), wide(F32), wide(MM), wide(MM), wide(F32), wide(F32)])


def _ffn_bwd(dx2, x1, gt, up, g_ffn, w_g, w_u, w_d, T, tm):
    def body(i, dx2_ref, x1_ref, gt_ref, up_ref, g_ref, wg_ref, wu_ref, wd_ref,
             dx1_ref, a_ref, dgt_ref, dup_ref, dg_ref, da_s, dh2_s):
        @pl.when(i == 0)
        def _():
            dg_ref[...] = jnp.zeros_like(dg_ref)

        g = g_ref[...]
        da_s[...] = _dot_nt(dx2_ref[...], wd_ref[...])

        def act_back(rows):
            for cs in FFN_HALVES:
                up, da = up_ref[rows, cs], da_s[rows, cs]
                silu, dsilu = _silu_parts(gt_ref[rows, cs])
                dgt_ref[rows, cs] = (da * up * dsilu).astype(MM)
                dup_ref[rows, cs] = (da * silu).astype(MM)
                a_ref[rows, cs] = (silu * up).astype(MM)

        _by_chunks(tm, act_back)
        dh2_s[...] = (jnp.dot(dgt_ref[...], wg_ref[...], preferred_element_type=F32)
                      + jnp.dot(dup_ref[...], wu_ref[...], preferred_element_type=F32))

        def norm_back(rows):
            x1h, r = _rms(x1_ref[rows, :])
            dh2 = dh2_s[rows, :]
            dx1_ref[rows, :] = dx2_ref[rows, :] + _rms_bwd(dh2 * g, x1h, r)
            dg_ref[...] += _fold8(dh2 * x1h)

        _by_chunks(tm, norm_back)

    return _row_call("ffn_bwd", body, T, tm, [dx2, x1, gt, up], [g_ffn, w_g, w_u, w_d],
                     [(D_MODEL, F32), (FFN, MM), (FFN, MM), (FFN, MM)], [((8, D_MODEL), F32)], VMEM_LIMIT,
                     scratch=[pltpu.VMEM((tm, FFN), F32), pltpu.VMEM((tm, D_MODEL), F32)])


def _merge_bwd(dx1, ya, yb, bg, o, hg, attn, m, rec, g_out, w_bra, w_brb, w_out, T, tm, xchg=((), ())):
    def body(i, dx1_ref, ya_ref, yb_ref, bg_ref, o_ref, hg_ref, attn_ref, m_ref, rec_ref, g_ref, wa_ref, wb_ref, wo_ref,
             dattn_ref, do_ref, dhg_ref, dbg_ref, dg_ref, dwo_ref, dwa_ref, dwb_ref, dm_s, dya_s, dyb_s, drec_s):
        @pl.when(i == 0)
        def _():
            for ref in (dg_ref, dwo_ref, dwa_ref, dwb_ref):
                ref[...] = jnp.zeros_like(ref)

        g = g_ref[...]
        dx1 = dx1_ref[...].astype(MM)
        dm_s[...] = _dot_nt(dx1, wo_ref[...])

        def gate_back(rows):
            dm = dm_s[rows, :]
            ga, gb = _sigmoid(bg_ref[rows, :D_MODEL]), _sigmoid(bg_ref[rows, D_MODEL:])
            dya_s[rows, :] = (dm * ga).astype(MM)
            dyb_s[rows, :] = (dm * gb).astype(MM)
            dbg_ref[rows, :D_MODEL] = (dm * ya_ref[rows, :] * ga * (1.0 - ga)).astype(MM)
            dbg_ref[rows, D_MODEL:] = (dm * yb_ref[rows, :] * gb * (1.0 - gb)).astype(MM)

        _by_chunks(tm, gate_back)
        dwo_ref[...] += _dot_tn(m_ref[...], dx1)
        attn_mm = attn_ref[...].astype(MM)
        for d in range(N_DEV):
            ds = slice(d * HEAD_PAD, (d + 1) * HEAD_PAD)
            dwa_ref[d] += _dot_tn(attn_mm, dya_s[:, ds])
            dwb_ref[d] += _dot_tn(rec_ref[...], dyb_s[:, ds])
        dattn_ref[...] = _dot_nt(dya_s[...], wa_ref[...])
        drec_s[...] = _dot_nt(dyb_s[...], wb_ref[...])

        def recurrent_out_back(rows):
            for h in range(HG_HEADS):
                ls = slice(h * HG_DIM, (h + 1) * HG_DIM)
                oh, r = _rms(o_ref[rows, ls])
                silu, dsilu = _silu_parts(hg_ref[rows, ls])
                dr = drec_s[rows, ls]
                dhg_ref[rows, ls] = (dr * oh * g * dsilu).astype(MM)
                don = dr * silu
                dg_ref[...] += _fold8(don * oh)
                do_ref[rows, ls] = _rms_bwd(don * g, oh, r)

        _by_chunks(tm, recurrent_out_back)

    wide = lambda n, dt: pltpu.VMEM((tm, n), dt)
    return _row_call("merge_bwd", body, T, tm, [dx1, ya, yb, bg, o, hg, attn, m, rec], [g_out, w_bra, w_brb, w_out],
                     [(D_MODEL, F32), (HG_W, F32), (HG_W, MM), (2 * D_MODEL, MM)],
                     [((8, HG_DIM), F32), ((D_MODEL, D_MODEL), F32), ((N_DEV, MLA_HEADS * HEAD_PAD, HEAD_PAD), F32),
                      ((N_DEV, HG_W, HEAD_PAD), F32)], VMEM_LIMIT,
                     scratch=[wide(D_MODEL, F32), wide(D_MODEL, MM), wide(D_MODEL, MM), wide(HG_W, F32)], xchg=xchg)


def _flash_bwd(qf, kf, vf, o, do, lse, T, xchg=((), ())):
    tq = min(ATT_TILE, T)
    nq = T // tq

    qi_tab, ki_tab = _causal_pairs(nq, by_query=False)

    n_x, n_sib = _x_count(xchg), len(xchg[0])
    hp = ATT_HEADS
    n_heads, n_pairs = MLA_HEADS // hp, len(qi_tab)

    def body(qi_ref, ki_ref, q_ref, k_ref, v_ref, o_ref, do_ref, lse_ref, *rest):
        x_in, (dq_ref, dk_ref, dv_ref), rest = rest[:n_x], rest[n_x:n_x + 3], rest[n_x + 3:]
        x_out, x_sems = rest[:n_x], rest[n_x:]
        t = pl.program_id(1)
        qi, ki = qi_ref[t], ki_ref[t]
        if n_x:
            @pl.when((pl.program_id(0) == 0) & (t == 0))
            def _():
                for cp in _x_copies(n_sib, x_in, x_out, x_sems):
                    cp.start()

        @pl.when(t == 0)
        def _():
            dq_ref[...] = jnp.zeros_like(dq_ref)

        def step(first):
            rows = pl.ds(pl.multiple_of(qi * tq, tq), tq)
            for hh in range(hp):
                hs = slice(hh * HEAD_PAD, (hh + 1) * HEAD_PAD)
                q, k, d_o = q_ref[:, hs], k_ref[:, hs], do_ref[:, hs]
                s = _dot_nt(q, k)
                if first:
                    row = lax.broadcasted_iota(jnp.int32, (tq, tq), 0)
                    col = lax.broadcasted_iota(jnp.int32, (tq, tq), 1)
                    s = jnp.where(col <= row, s, NEG)
                p = jnp.exp(s - lse_ref[:, hh * HEAD_PAD:hh * HEAD_PAD + 1])
                delta = jnp.sum(d_o * o_ref[:, hs], axis=1, keepdims=True)
                ds = p * (_dot_nt(d_o, v_ref[:, hs]) - delta)
                dq_ref[rows, hs] += _dot(ds, k)
                if first:
                    dv_ref[:, hs] = _dot_tn(p, d_o)
                    dk_ref[:, hs] = _dot_tn(ds, q)
                else:
                    dv_ref[:, hs] += _dot_tn(p, d_o)
                    dk_ref[:, hs] += _dot_tn(ds, q)

        @pl.when(qi == ki)
        def _():
            step(True)

        @pl.when(qi > ki)
        def _():
            step(False)

        if n_x:
            @pl.when((pl.program_id(0) == n_heads - 1) & (t == n_pairs - 1))
            def _():
                for cp in _x_copies(n_sib, x_in, x_out, x_sems):
                    cp.wait()

    q_spec = pl.BlockSpec((tq, hp * HEAD_PAD), lambda h, t, qi_ref, ki_ref: (qi_ref[t], h))
    kv_spec = pl.BlockSpec((tq, hp * HEAD_PAD), lambda h, t, qi_ref, ki_ref: (ki_ref[t], h))
    any_spec = pl.BlockSpec(memory_space=pl.ANY)
    w = MLA_HEADS * HEAD_PAD
    grid_spec = pltpu.PrefetchScalarGridSpec(
        num_scalar_prefetch=2, grid=(n_heads, n_pairs),
        in_specs=[q_spec, kv_spec, kv_spec, q_spec, q_spec, q_spec] + [any_spec] * n_x,
        out_specs=[pl.BlockSpec((T, hp * HEAD_PAD), lambda h, t, qi_ref, ki_ref: (0, h)), kv_spec, kv_spec]
        + [any_spec] * n_x,
        scratch_shapes=_x_sems(xchg))
    return pl.pallas_call(
        body, name="flash_bwd", grid_spec=grid_spec,
        out_shape=[jax.ShapeDtypeStruct((T, w), F32)] * 3 + _x_out_shapes(xchg),
        compiler_params=_cparams(("arbitrary", "arbitrary")),
    )(jnp.asarray(qi_tab), jnp.asarray(ki_tab), qf, kf, vf, o, do, lse, *xchg[0], *xchg[1])


def _mla_heads_bwd(d_out, saved, g_pad, cos_t, sin_t, first):
    d_raw, dg = [], jnp.zeros((1, HEAD_PAD), F32)
    for h in range(MLA_HEADS):
        xh, r = saved[h]
        dy = d_out[:, h * HEAD_PAD:(h + 1) * HEAD_PAD]
        dn = dy * cos_t + _rope_swap(dy * sin_t, first)
        dg = dg + jnp.sum(dn * xh, axis=0, keepdims=True)
        d_raw.append(_rms_bwd(dn * g_pad, xh, r, QK_DIM))
    return d_raw, dg


def _mla_prep_bwd(cq, ckv, kr, pos, dqf, dkf, dvf, g_qa, g_kva, g_qn, g_kn, w_uq, w_ukv, T, tm):
    def body(i, cq_ref, ckv_ref, kr_ref, pos_ref, dq_ref, dk_ref, dv_ref,
             gqa_ref, gkva_ref, gqn_ref, gkn_ref, wuq_ref, wukv_ref,
             dcq_ref, dckv_ref, dkr_ref, dgqa_ref, dgkva_ref, dgqn_ref, dgkn_ref, dwuq_ref, dwukv_ref):
        cos_t, sin_t, first = _rope_tables(pos_ref[...], tm)
        cqh, rq = _rms(cq_ref[...])
        ckvh, rkv = _rms(ckv_ref[...])
        cqn, ckvn = cqh * gqa_ref[...], ckvh * gkva_ref[...]
        q_raw, k_raw, _ = _mla_raw_heads(cqn, ckvn, kr_ref[...], wuq_ref, wukv_ref, tm)
        _, q_saved = _mla_heads_fwd(q_raw, gqn_ref[...], cos_t, sin_t, first)
        _, k_saved = _mla_heads_fwd(k_raw, gkn_ref[...], cos_t, sin_t, first)
        dq_heads, dgqn = _mla_heads_bwd(dq_ref[...] * ATT_SCALE, q_saved, gqn_ref[...], cos_t, sin_t, first)
        dk_heads, dgkn = _mla_heads_bwd(dk_ref[...], k_saved, gkn_ref[...], cos_t, sin_t, first)
        lane = lax.broadcasted_iota(jnp.int32, (tm, HEAD_PAD), 1)
        nope = lane < QK_NOPE
        dcqn = jnp.zeros((tm, Q_RANK), F32)
        dckvn = jnp.zeros((tm, KV_RANK), F32)
        dkr = jnp.zeros((tm, HEAD_PAD), F32)
        cqn_mm, ckvn_mm = cqn.astype(MM), ckvn.astype(MM)
        for h in range(MLA_HEADS):
            hs = slice(h * HEAD_PAD, (h + 1) * HEAD_PAD)
            dq_h = dq_heads[h].astype(MM)
            dkv_h = jnp.where(nope, dk_heads[h], pltpu.roll(dv_ref[:, hs], V_DIM, 1)).astype(MM)
            _acc(dwuq_ref.at[h], i, _dot_tn(dq_h, cqn_mm))
            _acc(dwukv_ref.at[h], i, _dot_tn(ckvn_mm, dkv_h))
            dcqn = dcqn + jnp.dot(dq_h, wuq_ref[h], preferred_element_type=F32)
            dckvn = dckvn + lax.dot_general(dkv_h, wukv_ref[h], (((1,), (1,)), ((), ())), preferred_element_type=F32)
            dkr = dkr + dk_heads[h]
        dkr_ref[...] = jnp.where((lane >= QK_NOPE) & (lane < QK_DIM), dkr, 0.0).astype(MM)
        dcq_ref[...] = _rms_bwd(dcqn * gqa_ref[...], cqh, rq).astype(MM)
        dckv_ref[...] = _rms_bwd(dckvn * gkva_ref[...], ckvh, rkv).astype(MM)
        _acc(dgqa_ref, i, jnp.sum(dcqn * cqh, axis=0, keepdims=True))
        _acc(dgkva_ref, i, jnp.sum(dckvn * ckvh, axis=0, keepdims=True))
        _acc(dgqn_ref, i, dgqn)
        _acc(dgkn_ref, i, dgkn)

    return _row_call(
        "mla_prep_bwd", body, T, tm, [cq, ckv, kr, pos, dqf, dkf, dvf], [g_qa, g_kva, g_qn, g_kn, w_uq, w_ukv],
        [(Q_RANK, MM), (KV_RANK, MM), (HEAD_PAD, MM)],
        [((1, Q_RANK), F32), ((1, KV_RANK), F32), ((1, HEAD_PAD), F32), ((1, HEAD_PAD), F32),
         ((MLA_HEADS, HEAD_PAD, Q_RANK), F32), ((MLA_HEADS, KV_RANK, HEAD_PAD), F32)], VMEM_LIMIT)


def _in_proj_bwd(x, dx1, dsecs, g_mix, w_in, T, tm):
    def body(i, x_ref, dx1_ref, *rest):
        d_refs, (g_ref, w_ref, dx_ref, dp_ref, dg_ref, dh_s) = rest[:len(SECTIONS)], rest[len(SECTIONS):]

        @pl.when(i == 0)
        def _():
            dg_ref[...] = jnp.zeros_like(dg_ref)

        g = g_ref[...]

        def join_and_cut(rows):
            pieces = [(d_ref[rows, QK_NOPE:QK_DIM] if n == QK_ROPE else d_ref[rows, :]).astype(F32)
                      for (_, n), d_ref in zip(COL_SECTIONS, d_refs)]
            dproj = jnp.concatenate(pieces, axis=1)
            for d in range(N_DEV):
                dp_ref[d, rows, :] = dproj[:, d * IN_BLOCK:(d + 1) * IN_BLOCK].astype(MM)

        _by_chunks(tm, join_and_cut)
        dh = jnp.dot(dp_ref[0], w_ref[0], preferred_element_type=F32)
        for d in range(1, N_DEV):
            dh = dh + jnp.dot(dp_ref[d], w_ref[d], preferred_element_type=F32)
        dh_s[...] = dh

        def norm_back(rows):
            xh, r = _rms(x_ref[rows, :])
            dh_c = dh_s[rows, :]
            dx_ref[rows, :] = dx1_ref[rows, :] + _rms_bwd(dh_c * g, xh, r)
            dg_ref[...] += _fold8(dh_c * xh)

        _by_chunks(tm, norm_back)

    in_specs = [pl.BlockSpec((tm, a.shape[1]), lambda i: (i, 0)) for a in [x, dx1, *dsecs]]
    in_specs += [pl.BlockSpec(g_mix.shape, lambda i: (0, 0)),
                 pl.BlockSpec(w_in.shape, lambda i: (0, 0, 0), pipeline_mode=pl.Buffered(1))]

    def kern(*refs):
        body(pl.program_id(0), *refs)

    return pl.pallas_call(
        kern, name="in_proj_bwd", grid=(T // tm,), in_specs=in_specs,
        out_specs=[pl.BlockSpec((tm, D_MODEL), lambda i: (i, 0)),
                   pl.BlockSpec((N_DEV, tm, IN_BLOCK), lambda i: (0, i, 0)),
                   pl.BlockSpec((8, D_MODEL), lambda i: (0, 0))],
        out_shape=[jax.ShapeDtypeStruct((T, D_MODEL), F32), jax.ShapeDtypeStruct((N_DEV, T, IN_BLOCK), MM),
                   jax.ShapeDtypeStruct((8, D_MODEL), F32)],
        scratch_shapes=[pltpu.VMEM((tm, D_MODEL), F32)],
        compiler_params=_cparams(("arbitrary",), VMEM_LIMIT),
    )(x, dx1, *dsecs, g_mix, w_in)


def _pick_block(n, cap):
    best = None
    for cand in range(128, min(n, cap) + 1, 128):
        if n % cand == 0:
            best = cand
    return n if best is None else best


def _pick_rows(n, cap):
    best = n
    for cand in range(8, min(n, cap) + 1, 8):
        if n % cand == 0:
            best = cand
    return best


def _matmul_tn(name, a, b):
    T, M = a.shape
    N = b.shape[1]
    bm, bk = _pick_block(M, 1408), min(512, T)
    bn = _pick_block(N, 2560)

    def body(a_ref, b_ref, c_ref):
        @pl.when(pl.program_id(2) == 0)
        def _():
            c_ref[...] = jnp.zeros_like(c_ref)

        c_ref[...] += _dot_tn(a_ref[...], b_ref[...])

    return pl.pallas_call(
        body, name=name, grid=(M // bm, N // bn, T // bk),
        in_specs=[pl.BlockSpec((bk, bm), lambda i, j, k: (k, i)), pl.BlockSpec((bk, bn), lambda i, j, k: (k, j))],
        out_specs=pl.BlockSpec((bm, bn), lambda i, j, k: (i, j)), out_shape=jax.ShapeDtypeStruct((M, N), F32),
        compiler_params=_cparams(("parallel", "parallel", "arbitrary"), VMEM_LIMIT),
    )(a, b)


def _matmul_tn_blocks(name, a, b):
    T, M = a.shape
    nd, _, c = b.shape
    bm, bk = _pick_block(M, 512), min(512, T)

    def body(a_ref, b_ref, c_ref):
        @pl.when(pl.program_id(1) == 0)
        def _():
            c_ref[...] = jnp.zeros_like(c_ref)

        a_blk = a_ref[...].astype(MM)
        for d in range(nd):
            c_ref[d] += _dot_tn(b_ref[d], a_blk)

    return pl.pallas_call(
        body, name=name, grid=(M // bm, T // bk),
        in_specs=[pl.BlockSpec((bk, bm), lambda i, k: (k, i)), pl.BlockSpec((nd, bk, c), lambda i, k: (0, k, 0))],
        out_specs=pl.BlockSpec((nd, c, bm), lambda i, k: (0, 0, i)),
        out_shape=jax.ShapeDtypeStruct((nd, c, M), F32),
        compiler_params=_cparams(("parallel", "arbitrary"), VMEM_LIMIT),
    )(a, b)


def _pad_gain(g, n):
    return jnp.pad(g.reshape(1, -1), ((0, 0), (0, n - g.shape[-1])))


GROUP_A = ("w_ffn_gate", "w_ffn_up", "w_ffn_down", "w_ple_gate", "w_ple_proj")
GROUP_B = ("w_branch", "w_out")
GROUP_C = ("w_in", "w_uq", "w_ukv")
EARLY = GROUP_C
LATE = GROUP_B + GROUP_A
TRANSPOSED = ("w_in", "w_uq", "w_ffn_gate", "w_ffn_up")


def _local_step(x, p, pos, tgt, small, big, late_blocks=None, core=None):
    T = x.shape[0]
    tm = min(ROW_TILE, T)
    w_in = big["w_in"]
    w_uq = jnp.pad(big["w_uq"], ((0, 0), (0, HEAD_PAD - QK_DIM), (0, 0)))
    w_ukv = big["w_ukv"]

    g_mix, g_qa, g_kva = small["mix_norm_g"], small["q_a_norm_g"], small["kv_a_norm_g"]
    g_qn, g_kn = _pad_gain(small["q_norm_g"], HEAD_PAD), _pad_gain(small["k_norm_g"], HEAD_PAD)
    g_out, g_ffn = small["hg_out_norm_g"], small["ffn_norm_g"]
    g_pg, g_post = small["ple_gate_norm_g"], small["ple_post_norm_g"]
    logits = small["hg_lb_logits"]
    lb = _lower_bound(logits)

    h, cq, ckv, kr, hq, hf, hi, hg, bg = _in_proj_fwd(x, g_mix, w_in, T, tm)
    qf, kf, vf = _mla_prep_fwd(cq, ckv, kr, pos, g_qa, g_kva, g_qn, g_kn, w_uq, w_ukv, T, tm)
    if late_blocks is None:
        attn, lse = _flash_fwd(qf, kf, vf, T)
    else:
        attn, lse, *late = _flash_fwd(qf, kf, vf, T, ag_blocks=[late_blocks[n] for n in LATE])
        big = {**big, **dict(zip(LATE, late))}
    w_branch = jnp.moveaxis(big["w_branch"].reshape(N_DEV, 2, HG_W, HEAD_PAD), 0, 2).reshape(2, HG_W, D_MODEL)
    w_bra = jnp.pad(w_branch[0].reshape(MLA_HEADS, V_DIM, D_MODEL),
                    ((0, 0), (0, HEAD_PAD - V_DIM), (0, 0))).reshape(MLA_HEADS * HEAD_PAD, D_MODEL)
    w_brb = w_branch[1]
    w_out = big["w_out"].reshape(D_MODEL, D_MODEL)
    w_g, w_u = big["w_ffn_gate"].reshape(FFN, D_MODEL), big["w_ffn_up"].reshape(FFN, D_MODEL)
    w_d = big["w_ffn_down"].reshape(FFN, D_MODEL)
    w_pg, w_pp = big["w_ple_gate"].reshape(D_MODEL, D_MODEL), big["w_ple_proj"]
    o, s0 = _hgrn_fwd(hq, hf, hi, lb, T)
    x1, ya, yb, m, rec = _merge_fwd(attn, o, hg, bg, x, g_out, w_bra, w_brb, w_out, T, tm)
    x2, gt, up, h2 = _ffn_fwd(x1, g_ffn, w_g, w_u, w_d, T, tm)
    dx2, loss_p, dg_post, dg_pg, d_pg, d_pp = _ple_loss(x2, p, tgt, g_pg, g_post, w_pg, w_pp, T, tm)
    dg_post, dg_pg = (jnp.sum(t, axis=0, keepdims=True) for t in (dg_post, dg_pg))

    grads, sibs, gots = {}, {}, {}
    dist = core is not None
    pick = lambda names: [grads[n] for n in names] if dist else ()

    def partials(tag, names, got):
        if not dist:
            return ()
        sibs.update(zip(names, got))
        return _chip_partials("rs_partial_" + tag, pick(names), got, core)

    dx1, a, dgt, dup, dg_ffn = _ffn_bwd(dx2, x1, gt, up, g_ffn, w_g, w_u, w_d, T, tm)
    dg_ffn = jnp.sum(dg_ffn, axis=0, keepdims=True)
    grads["w_ffn_gate"] = _matmul_tn("dw_gate", dgt, h2).reshape(N_DEV, -1, D_MODEL)
    grads["w_ffn_up"] = _matmul_tn("dw_up", dup, h2).reshape(N_DEV, -1, D_MODEL)
    grads["w_ffn_down"] = _matmul_tn("dw_down", a, dx2).reshape(N_DEV, -1, D_MODEL)
    grads["w_ple_gate"] = d_pg.reshape(N_DEV, -1, D_MODEL)
    grads["w_ple_proj"] = d_pp

    dattn, do, dhg, dbg, dg_out, d_out, d_bra, d_brb, *sib_a = _merge_bwd(
        dx1, ya, yb, bg, o, hg, attn, m, rec, g_out, w_bra, w_brb, w_out, T, tm, xchg=(pick(GROUP_A), ()))
    parts_a = partials("a", GROUP_A, sib_a)
    dg_out = jnp.sum(dg_out, axis=0, keepdims=True)
    d_bra = d_bra.reshape(N_DEV, MLA_HEADS, HEAD_PAD, HEAD_PAD)[:, :, :V_DIM].reshape(N_DEV, HG_W, HEAD_PAD)
    grads["w_branch"] = jnp.concatenate([d_bra, d_brb], axis=1)
    grads["w_out"] = d_out.reshape(N_DEV, -1, D_MODEL)

    dhq, dhf, dhi, dlb, *got = _hgrn_bwd(hq, hf, hi, do, s0, lb, T, xchg=(pick(GROUP_B), parts_a))
    sib_b, got_a = got[:len(GROUP_B)], got[len(GROUP_B):]
    parts_b = partials("b", GROUP_B, sib_b)
    dqf, dkf, dvf, *got_b = _flash_bwd(qf, kf, vf, attn, dattn, lse, T, xchg=((), parts_b))
    (dcq, dckv, dkr, dg_qa, dg_kva, dg_qn, dg_kn, d_uq, d_ukv) = _mla_prep_bwd(
        cq, ckv, kr, pos, dqf, dkf, dvf, g_qa, g_kva, g_qn, g_kn, w_uq, w_ukv, T, tm)
    grad_x, dproj, dg_mix = _in_proj_bwd(x, dx1, [dcq, dckv, dkr, dhq, dhf, dhi, dhg, dbg], g_mix, w_in, T, tm)
    dg_mix = jnp.sum(dg_mix, axis=0, keepdims=True)
    grads["w_in"] = _matmul_tn_blocks("dw_in", h, dproj)
    grads["w_uq"] = d_uq[:, :QK_DIM]
    grads["w_ukv"] = d_ukv
    parts_c = ()
    if dist:
        parts_c = partials("c", GROUP_C, _exchange_sibling("rs_sibling_c", pick(GROUP_C)))
        gots.update(zip(GROUP_A, got_a))
        gots.update(zip(GROUP_B, got_b))

    dl0 = dlb * lb * (1.0 - lb)
    small_g = {
        "mix_norm_g": dg_mix, "q_a_norm_g": dg_qa, "kv_a_norm_g": dg_kva,
        "q_norm_g": dg_qn[:, :QK_DIM], "k_norm_g": dg_kn[:, :QK_DIM],
        "hg_lb_logits": jnp.concatenate([dl0, -dl0], axis=0), "hg_out_norm_g": dg_out,
        "ffn_norm_g": dg_ffn, "ple_gate_norm_g": dg_pg, "ple_post_norm_g": dg_post,
    }
    return loss_p, grad_x, small_g, grads, sibs, gots, parts_c


def _lower_bound(logits):
    def body(l_ref, lb_ref):
        l = l_ref[...]
        mx = jnp.max(l, axis=0, keepdims=True)
        e = jnp.exp(l - mx)
        lb_ref[...] = e[0:1] / jnp.sum(e, axis=0, keepdims=True)

    return pl.pallas_call(body, name="lower_bound", out_shape=jax.ShapeDtypeStruct((1, HG_W), F32))(logits)


def _my_place():
    return lax.axis_index("x"), lax.axis_index("y"), lax.axis_index("c")


def _all_gather(name, blocks):
    n = len(blocks)

    def body(*refs):
        x_refs, out_refs, sems = refs[:n], refs[n:2 * n], refs[2 * n:]
        _ag_start(x_refs, out_refs, sems)
        _ag_finish(x_refs, out_refs, sems)

    any_spec = pl.BlockSpec(memory_space=pl.ANY)
    return pl.pallas_call(
        body, name=name, out_shape=_ag_out_shapes(blocks),
        in_specs=[any_spec] * n, out_specs=[any_spec] * n, scratch_shapes=_ag_sems(n),
    )(*blocks)


def _ag_out_shapes(blocks):
    return [jax.ShapeDtypeStruct((N_DEV,) + b.shape, b.dtype) for b in blocks]


def _ag_sems(n):
    return [pltpu.SemaphoreType.DMA((7 * n,)), pltpu.SemaphoreType.DMA((7 * n,)), pltpu.SemaphoreType.DMA((n,))]


def _ag_parts(x_refs, out_refs, sems):
    send_sems, recv_sems, local_sems = sems
    x, y, c = _my_place()
    me, sibling = (x, y, c), (x, y, 1 - c)
    chips = [(1 - x, y), (x, 1 - y), (1 - x, 1 - y)]
    n = len(x_refs)

    def copy(a, k, block, to, own=False):
        px, py, pc = block
        dst = out_refs[a].at[4 * px + 2 * py + pc]
        return pltpu.make_async_remote_copy(
            src_ref=x_refs[a] if own else dst, dst_ref=dst, send_sem=send_sems.at[7 * a + k],
            recv_sem=recv_sems.at[7 * a + k], device_id=to, device_id_type=MESH_ID)

    mine = [pltpu.make_async_copy(x_refs[a], out_refs[a].at[4 * x + 2 * y + c], local_sems.at[a]) for a in range(n)]
    first = []
    for a in range(n):
        first.append(copy(a, 0, me, sibling, own=True))
        first += [copy(a, 1 + j, me, (*chip, c), own=True) for j, chip in enumerate(chips)]
    return copy, mine, first, me, sibling, chips, c, n


def _ag_start(x_refs, out_refs, sems):
    _, mine, first, *_ = _ag_parts(x_refs, out_refs, sems)
    for cp in mine + first:
        cp.start()


def _ag_finish(x_refs, out_refs, sems):
    copy, mine, first, me, sibling, chips, c, n = _ag_parts(x_refs, out_refs, sems)
    passed = []
    for j, chip in enumerate(chips):
        for a in range(n):
            copy(a, 1 + j, (*chip, c), me).wait_recv()
            passed.append(copy(a, 4 + j, (*chip, c), sibling))
            passed[-1].start()
    for a in range(n):
        copy(a, 0, sibling, me).wait_recv()
    for j, chip in enumerate(chips):
        for a in range(n):
            copy(a, 4 + j, (*chip, 1 - c), me).wait_recv()
    for cp in first + passed:
        cp.wait_send()
    for cp in mine:
        cp.wait()


def _exchange_sibling(name, gs):
    return _exchange(name, (gs, ()))


def _exchange(name, xchg):
    n = _x_count(xchg)

    def body(*refs):
        in_refs, out_refs, sems = refs[:n], refs[n:2 * n], refs[2 * n:]
        for cp in _x_copies(len(xchg[0]), in_refs, out_refs, sems):
            cp.start()
        for cp in _x_copies(len(xchg[0]), in_refs, out_refs, sems):
            cp.wait()

    any_spec = pl.BlockSpec(memory_space=pl.ANY)
    return pl.pallas_call(
        body, name=name, out_shape=_x_out_shapes(xchg), in_specs=[any_spec] * n, out_specs=[any_spec] * n,
        scratch_shapes=_x_sems(xchg),
    )(*xchg[0], *xchg[1])


N_PARTS = 4


def _part_spec(rows, cols, t_pos, lead_block=(), lead_index=lambda *args: ()):
    if rows % (16 * N_PARTS) == 0:
        axis, shape, count = 0, (rows // N_PARTS, cols), N_PARTS
    elif cols % (128 * N_PARTS) == 0:
        axis, shape, count = 1, (rows, cols // N_PARTS), N_PARTS
    else:
        axis, shape, count = 0, (rows, cols), 1

    def index(*args):
        i = jnp.minimum(args[t_pos], count - 1)
        return (*lead_index(*args), *((i, 0) if axis == 0 else (0, i)))

    return pl.BlockSpec((*lead_block, *shape), index)


def _chip_partials(name, gs, sibs, c_idx):
    n = len(gs)

    def body(c_ref, *refs):
        for g_ref, sib_ref, out_ref in zip(refs[:n], refs[n:2 * n], refs[2 * n:]):
            out_ref[...] = (g_ref[...] + sib_ref[...]).astype(MM)

    own = [_part_spec(*g.shape[1:], 1, (1,), lambda j, t, c_ref: (2 * j + c_ref[0],)) for g in gs]
    by_chip = [_part_spec(*g.shape[1:], 1, (1,), lambda j, t, c_ref: (j,)) for g in gs]
    grid_spec = pltpu.PrefetchScalarGridSpec(
        num_scalar_prefetch=1, grid=(4, N_PARTS), in_specs=own + by_chip, out_specs=by_chip)
    return pl.pallas_call(
        body, name=name, grid_spec=grid_spec, out_shape=[jax.ShapeDtypeStruct((4,) + g.shape[1:], MM) for g in gs],
        compiler_params=_cparams(("arbitrary", "arbitrary"), VMEM_LIMIT),
    )(c_idx, *gs, *sibs)


def _exchange_chips(parts):
    return _exchange("rs_chips", ((), parts))


def _x_count(xchg):
    return len(xchg[0]) + len(xchg[1])


def _x_out_shapes(xchg):
    return ([jax.ShapeDtypeStruct((4,) + g.shape[1:], g.dtype) for g in xchg[0]]
            + [jax.ShapeDtypeStruct((3,) + p.shape[1:], p.dtype) for p in xchg[1]])


def _x_sems(xchg):
    n = 4 * len(xchg[0]) + 3 * len(xchg[1])
    return [pltpu.SemaphoreType.DMA((n,)), pltpu.SemaphoreType.DMA((n,))] if n else []


def _x_copies(n_sib, in_refs, out_refs, sems):
    if not in_refs:
        return []
    send_sems, recv_sems = sems
    x, y, c = _my_place()
    chips = [(1 - x, y), (x, 1 - y), (1 - x, 1 - y)]
    copies = []

    def add(src, dst, to):
        k = len(copies)
        copies.append(pltpu.make_async_remote_copy(
            src_ref=src, dst_ref=dst, send_sem=send_sems.at[k], recv_sem=recv_sems.at[k], device_id=to,
            device_id_type=MESH_ID))

    for a, (src, dst) in enumerate(zip(in_refs, out_refs)):
        if a < n_sib:
            for j in range(4):
                add(src.at[2 * j + 1 - c], dst.at[j], (x, y, 1 - c))
        else:
            for k, (px, py) in enumerate(chips):
                add(src.at[2 * px + py], dst.at[k], (px, py, c))
    return copies


def _adamw_math(w, g, m, v):
    m = ADAM_B1 * m + (1.0 - ADAM_B1) * g
    v = ADAM_B2 * v + (1.0 - ADAM_B2) * jnp.square(g)
    m_hat = m / (1.0 - ADAM_B1 ** ADAM_STEP)
    v_hat = v / (1.0 - ADAM_B2 ** ADAM_STEP)
    delta = -ADAM_LR * (m_hat / (jnp.sqrt(v_hat) + ADAM_EPS) + ADAM_WD * w)
    return delta, m, v


def _sum_adamws(name, gs, sibs, gots, ws, ms, vs, slot_idx, chip_idx):
    n = len(gs)

    def body(s_ref, j_ref, *refs):
        ins, outs = refs[:6 * n], refs[6 * n:]
        for a in range(n):
            g_ref, sib_ref, got_ref, w_ref, m_ref, v_ref = (ins[k * n + a] for k in range(6))
            go_ref, d_ref, m2_ref, v2_ref = outs[4 * a:4 * a + 4]
            grad = g_ref[0] + sib_ref[0]
            for k in range(3):
                grad = grad + got_ref[k].astype(F32)
            go_ref[...] = grad
            d_ref[...], m2_ref[...], v2_ref[...] = _adamw_math(w_ref[...], grad, m_ref[...], v_ref[...])

    shapes = [g.shape[1:] for g in gs]
    flat = [_part_spec(*s, 0) for s in shapes]
    in_specs = ([_part_spec(*s, 0, (1,), lambda t, s_ref, j_ref: (s_ref[0],)) for s in shapes]
                + [_part_spec(*s, 0, (1,), lambda t, s_ref, j_ref: (j_ref[0],)) for s in shapes]
                + [_part_spec(*s, 0, (3,), lambda t, s_ref, j_ref: (0,)) for s in shapes] + flat * 3)
    grid_spec = pltpu.PrefetchScalarGridSpec(
        num_scalar_prefetch=2, grid=(N_PARTS,), in_specs=in_specs, out_specs=[f for f in flat for _ in range(4)])
    res = pl.pallas_call(
        body, name=name, grid_spec=grid_spec,
        out_shape=[jax.ShapeDtypeStruct(s, F32) for s in shapes for _ in range(4)],
        compiler_params=_cparams(("arbitrary",), VMEM_LIMIT),
    )(slot_idx, chip_idx, *gs, *sibs, *gots, *ws, *ms, *vs)
    return [res[4 * a:4 * a + 4] for a in range(n)]


def _adamw_small(parts, w, m, v):
    rows = w.shape[0]

    def body(p_ref, w_ref, m_ref, v_ref, g_ref, d_ref, m2_ref, v2_ref):
        g = p_ref[0]
        for d in range(1, N_DEV):
            g = g + p_ref[d]
        g_ref[...] = g
        d_ref[...], m2_ref[...], v2_ref[...] = _adamw_math(w_ref[...], g, m_ref[...], v_ref[...])

    return pl.pallas_call(
        body, name="adamw_small", out_shape=[jax.ShapeDtypeStruct((rows, 128), F32)] * 4,
    )(parts, w, m, v)


BIG = ("w_in", "w_uq", "w_ukv", "w_branch", "w_out", "w_ffn_gate", "w_ffn_up", "w_ffn_down", "w_ple_gate", "w_ple_proj")
SMALL = (
    ("mix_norm_g", 1024), ("q_a_norm_g", 384), ("kv_a_norm_g", 256), ("q_norm_g", 96), ("k_norm_g", 96),
    ("hg_lb_logits", 1024), ("hg_out_norm_g", 128), ("ffn_norm_g", 1024), ("ple_gate_norm_g", 1024),
    ("ple_post_norm_g", 1024),
)
SMALL_ROWS = 56


def _pack_small(vals):
    rows = []
    for name, n in SMALL:
        v = vals[name].reshape(1, -1).astype(F32)
        rows.append(jnp.pad(v, ((0, 0), (0, (-n) % 128))).reshape(-1, 128))
    return jnp.concatenate(rows, axis=0)


def _unpack_small(packed, shapes):
    out, r = {}, 0
    for name, n in SMALL:
        k = (n + 127) // 128
        out[name] = packed[r:r + k].reshape(1, -1)[:, :n].reshape(shapes[name])
        r += k
    return out


_WEIGHTS = ["mix_norm_g", "w_in", "q_a_norm_g", "w_uq", "kv_a_norm_g", "w_ukv", "q_norm_g", "k_norm_g", "hg_lb_logits",
            "hg_out_norm_g", "w_branch", "w_out", "ffn_norm_g", "w_ffn_gate", "w_ffn_up", "w_ffn_down",
            "ple_gate_norm_g", "w_ple_gate", "w_ple_proj", "ple_post_norm_g"]


def _step(x, p, positions, tgt, w, m, v):
    small_names = [n for n, _ in SMALL]
    T = x.shape[1]
    px, py, pc = _my_place()
    as_idx = lambda t: jnp.reshape(t, (1,)).astype(jnp.int32)

    def two_d(n, t):
        t = t.reshape(-1, t.shape[-1])
        return t.T if n in TRANSPOSED else t

    def full_shape(n, t):
        return (t.T if n in TRANSPOSED else t).reshape(w[n].shape)

    blocks = {n: two_d(n, w[n]).astype(MM) for n in BIG}
    big = dict(zip(EARLY, _all_gather("ag_weights", [blocks[n] for n in EARLY])))
    small = {n: (w[n] if n == "hg_lb_logits" else w[n].reshape(1, -1)) for n in small_names}

    loss_p, grad_x, small_g, grads, sibs, gots, parts_c = _local_step(
        x[0], p[0, 0], positions.reshape(T, 1), tgt[0], small, big, late_blocks=blocks, core=as_idx(pc))

    gots.update(zip(GROUP_C, _exchange_chips(parts_c)))
    out_g, out_d, out_m, out_v = {}, {}, {}, {}
    for tag, names in (("ab", GROUP_A + GROUP_B), ("c", GROUP_C)):
        pick = lambda table: [table[n] for n in names]
        res = _sum_adamws("adamw_" + tag, pick(grads), pick(sibs), pick(gots), [two_d(n, w[n]) for n in names],
                          [two_d(n, m[n]) for n in names], [two_d(n, v[n]) for n in names],
                          as_idx(4 * px + 2 * py + pc), as_idx(2 * px + py))
        for n, r in zip(names, res):
            out_g[n], out_d[n], out_m[n], out_v[n] = [full_shape(n, t) for t in r]

    packed_g = _pack_small(small_g)
    loss_row = jnp.concatenate([jnp.pad(jnp.sum(loss_p).reshape(1, 1), ((0, 0), (0, 127))),
                                jnp.zeros((SMALL_ROWS - packed_g.shape[0] - 1, 128), F32)], axis=0)
    parts = _all_gather("ag_small", [jnp.concatenate([packed_g, loss_row], axis=0)])[0]
    pad_rows = lambda t: jnp.pad(t, ((0, SMALL_ROWS - t.shape[0]), (0, 0)))
    sw = pad_rows(_pack_small({n: w[n] for n in small_names}))
    sm = pad_rows(_pack_small({n: m[n] for n in small_names}))
    sv = pad_rows(_pack_small({n: v[n] for n in small_names}))
    g_s, d_s, m_s, v_s = _adamw_small(parts, sw, sm, sv)
    shapes = {n: w[n].shape for n in small_names}
    n_packed = packed_g.shape[0]
    loss = g_s[n_packed, 0]
    for src, dst in ((g_s, out_g), (d_s, out_d), (m_s, out_m), (v_s, out_v)):
        dst.update(_unpack_small(src, shapes))

    outs = [loss, grad_x[None]]
    for table in (out_g, out_d, out_m, out_v):
        outs += [table[n] for n in _WEIGHTS]
    return tuple(outs)


def kernel(x, p, positions, mix_norm_g, w_in, q_a_norm_g, w_uq, kv_a_norm_g, w_ukv, q_norm_g, k_norm_g, hg_lb_logits, hg_out_norm_g, w_branch, w_out, ffn_norm_g, w_ffn_gate, w_ffn_up, w_ffn_down, ple_gate_norm_g, w_ple_gate, w_ple_proj, ple_post_norm_g, loss_target, m_mix_norm_g, m_w_in, m_q_a_norm_g, m_w_uq, m_kv_a_norm_g, m_w_ukv, m_q_norm_g, m_k_norm_g, m_hg_lb_logits, m_hg_out_norm_g, m_w_branch, m_w_out, m_ffn_norm_g, m_w_ffn_gate, m_w_ffn_up, m_w_ffn_down, m_ple_gate_norm_g, m_w_ple_gate, m_w_ple_proj, m_ple_post_norm_g, v_mix_norm_g, v_w_in, v_q_a_norm_g, v_w_uq, v_kv_a_norm_g, v_w_ukv, v_q_norm_g, v_k_norm_g, v_hg_lb_logits, v_hg_out_norm_g, v_w_branch, v_w_out, v_ffn_norm_g, v_w_ffn_gate, v_w_ffn_up, v_w_ffn_down, v_ple_gate_norm_g, v_w_ple_gate, v_w_ple_proj, v_ple_post_norm_g):
    w = dict(mix_norm_g=mix_norm_g, w_in=w_in, q_a_norm_g=q_a_norm_g, w_uq=w_uq, kv_a_norm_g=kv_a_norm_g, w_ukv=w_ukv,
             q_norm_g=q_norm_g, k_norm_g=k_norm_g, hg_lb_logits=hg_lb_logits, hg_out_norm_g=hg_out_norm_g,
             w_branch=w_branch, w_out=w_out, ffn_norm_g=ffn_norm_g, w_ffn_gate=w_ffn_gate, w_ffn_up=w_ffn_up,
             w_ffn_down=w_ffn_down, ple_gate_norm_g=ple_gate_norm_g, w_ple_gate=w_ple_gate, w_ple_proj=w_ple_proj,
             ple_post_norm_g=ple_post_norm_g)
    m = dict(mix_norm_g=m_mix_norm_g, w_in=m_w_in, q_a_norm_g=m_q_a_norm_g, w_uq=m_w_uq, kv_a_norm_g=m_kv_a_norm_g,
             w_ukv=m_w_ukv, q_norm_g=m_q_norm_g, k_norm_g=m_k_norm_g, hg_lb_logits=m_hg_lb_logits,
             hg_out_norm_g=m_hg_out_norm_g, w_branch=m_w_branch, w_out=m_w_out, ffn_norm_g=m_ffn_norm_g,
             w_ffn_gate=m_w_ffn_gate, w_ffn_up=m_w_ffn_up, w_ffn_down=m_w_ffn_down,
             ple_gate_norm_g=m_ple_gate_norm_g, w_ple_gate=m_w_ple_gate, w_ple_proj=m_w_ple_proj,
             ple_post_norm_g=m_ple_post_norm_g)
    v = dict(mix_norm_g=v_mix_norm_g, w_in=v_w_in, q_a_norm_g=v_q_a_norm_g, w_uq=v_w_uq, kv_a_norm_g=v_kv_a_norm_g,
             w_ukv=v_w_ukv, q_norm_g=v_q_norm_g, k_norm_g=v_k_norm_g, hg_lb_logits=v_hg_lb_logits,
             hg_out_norm_g=v_hg_out_norm_g, w_branch=v_w_branch, w_out=v_w_out, ffn_norm_g=v_ffn_norm_g,
             w_ffn_gate=v_w_ffn_gate, w_ffn_up=v_w_ffn_up, w_ffn_down=v_w_ffn_down,
             ple_gate_norm_g=v_ple_gate_norm_g, w_ple_gate=v_w_ple_gate, w_ple_proj=v_w_ple_proj,
             ple_post_norm_g=v_ple_post_norm_g)
    return _step(x, p, positions, loss_target, w, m, v)
```

```python
import functools

import jax
import jax.numpy as jnp
import numpy as np
from jax import lax
from jax.experimental import pallas as pl
from jax.experimental.pallas import tpu as pltpu

F32 = jnp.float32
MM = jnp.bfloat16
HI = lax.Precision.HIGHEST
MESH_ID = pl.DeviceIdType.MESH

D_MODEL = 1024
N_DEV = 8
MLA_HEADS = 8
QK_NOPE = 64
QK_ROPE = 32
QK_DIM = 96
V_DIM = 64
HEAD_PAD = 128
Q_RANK = 384
KV_RANK = 256
ROPE_BASE = 10000.0
HG_HEADS = 4
HG_DIM = 128
HG_W = 512
HG_CHUNK = 64
FFN = 2816
PLE = 256
EPS = 1e-6
ATT_SCALE = QK_DIM ** -0.5
NEG = -1e30

ADAM_LR = 0.001
ADAM_B1 = 0.9
ADAM_B2 = 0.999
ADAM_EPS = 1e-08
ADAM_WD = 0.01
ADAM_STEP = 10

SEC_CQ = (0, 384)
SEC_CKV = (384, 256)
SEC_KR = (640, 128)
SEC_HQ = (768, 512)
SEC_HF = (1280, 512)
SEC_HI = (1792, 512)
SEC_HG = (2304, 512)
SEC_BG = (2816, 2048)
IN_PAD = 4864
SECTIONS = (SEC_CQ, SEC_CKV, SEC_KR, SEC_HQ, SEC_HF, SEC_HI, SEC_HG, SEC_BG)
COL_SECTIONS = ((0, 384), (384, 256), (640, 32), (672, 512), (1184, 512), (1696, 512), (2208, 512), (2720, 2048))
IN_COLS = 4768
IN_BLOCK = IN_COLS // 8

VMEM_LIMIT = 58 * 1024 * 1024
ROW_TILE = 256
ATT_TILE = 1024
ATT_HEADS = 4
HG_BLOCK = 512
HG_UNROLL = 4


def _dot(a, b):
    return jnp.dot(a.astype(MM), b.astype(MM), preferred_element_type=F32)


def _dot_nt(a, b):
    return lax.dot_general(a.astype(MM), b.astype(MM), (((1,), (1,)), ((), ())), preferred_element_type=F32)


def _dot_tn(a, b):
    return lax.dot_general(a.astype(MM), b.astype(MM), (((0,), (0,)), ((), ())), preferred_element_type=F32)


def _dot_hi(a, b):
    return jnp.dot(a, b, preferred_element_type=F32, precision=HI)


def _sigmoid(x):
    return 1.0 / (1.0 + jnp.exp(-x))


def _rms(x, n=None):
    n = x.shape[-1] if n is None else n
    r = lax.rsqrt(jnp.sum(x * x, axis=-1, keepdims=True) * (1.0 / n) + EPS)
    return x * r, r


def _rms_bwd(dxh, xh, r, n=None):
    n = xh.shape[-1] if n is None else n
    return r * (dxh - xh * (jnp.sum(dxh * xh, axis=-1, keepdims=True) * (1.0 / n)))


def _rope_tables(pos, tm):
    lane = lax.broadcasted_iota(jnp.int32, (tm, HEAD_PAD), 1)
    idx = jnp.where(lane < QK_NOPE + QK_ROPE // 2, lane - QK_NOPE, lane - QK_NOPE - QK_ROPE // 2)
    inv = jnp.exp(idx.astype(F32) * (-np.log(ROPE_BASE) * 2.0 / QK_ROPE))
    ang = pos.astype(F32) * inv
    in_rope = (lane >= QK_NOPE) & (lane < QK_DIM)
    first = lane < QK_NOPE + QK_ROPE // 2
    cos_t = jnp.where(in_rope, jnp.cos(ang), 1.0)
    sin_t = jnp.where(in_rope, jnp.where(first, -jnp.sin(ang), jnp.sin(ang)), 0.0)
    return cos_t, sin_t, (first, in_rope)


def _rope_swap(x, halves):
    first, in_rope = halves
    half = QK_ROPE // 2
    return jnp.where(in_rope, jnp.where(first, pltpu.roll(x, HEAD_PAD - half, 1), pltpu.roll(x, half, 1)), 0.0)


def _cparams(sem, vmem=None):
    return pltpu.CompilerParams(dimension_semantics=sem, vmem_limit_bytes=vmem)


def _row_call(name, body, T, tm, row_ins, full_ins, row_outs, acc_outs, vmem=None, scratch=(), xchg=((), ())):
    n_in, n_out, n_x = len(row_ins) + len(full_ins), len(row_outs) + len(acc_outs), _x_count(xchg)
    steps = T // tm

    def kern(*refs):
        ins, x_in, refs = refs[:n_in], refs[n_in:n_in + n_x], refs[n_in + n_x:]
        outs, x_out, refs = refs[:n_out], refs[n_out:n_out + n_x], refs[n_out + n_x:]
        scr, x_sems = refs[:len(scratch)], refs[len(scratch):]
        i = pl.program_id(0)
        if n_x:
            @pl.when(i == 0)
            def _():
                for cp in _x_copies(len(xchg[0]), x_in, x_out, x_sems):
                    cp.start()

        body(i, *ins, *outs, *scr)
        if n_x:
            @pl.when(i == steps - 1)
            def _():
                for cp in _x_copies(len(xchg[0]), x_in, x_out, x_sems):
                    cp.wait()

    any_spec = pl.BlockSpec(memory_space=pl.ANY)
    in_specs = [pl.BlockSpec((tm, a.shape[1]), lambda i: (i, 0)) for a in row_ins]
    in_specs += [pl.BlockSpec(a.shape, lambda i, nd=a.ndim: (0,) * nd, pipeline_mode=pl.Buffered(1)) for a in full_ins]
    out_specs = [pl.BlockSpec((tm, n), lambda i: (i, 0)) for n, _ in row_outs]
    out_specs += [pl.BlockSpec(s, lambda i, nd=len(s): (0,) * nd) for s, _ in acc_outs]
    out_shape = [jax.ShapeDtypeStruct((T, n), dt) for n, dt in row_outs]
    out_shape += [jax.ShapeDtypeStruct(s, dt) for s, dt in acc_outs]
    return pl.pallas_call(
        kern, name=name, grid=(steps,), in_specs=in_specs + [any_spec] * n_x, out_specs=out_specs + [any_spec] * n_x,
        out_shape=out_shape + _x_out_shapes(xchg), scratch_shapes=list(scratch) + _x_sems(xchg),
        compiler_params=_cparams(("arbitrary",), vmem),
    )(*row_ins, *full_ins, *xchg[0], *xchg[1])


FFN_HALVES = (slice(0, FFN // 2), slice(FFN // 2, FFN))
ROW_CHUNK = 16
CHUNK_UNROLL = True


def _by_chunks(tm, fn):
    def step(c, carry):
        fn(pl.ds(pl.multiple_of(c * ROW_CHUNK, ROW_CHUNK), ROW_CHUNK))
        return carry

    lax.fori_loop(0, tm // ROW_CHUNK, step, 0, unroll=CHUNK_UNROLL)


def _fold8(x):
    return x[:8] + x[8:]


def _acc(ref, i, val):
    @pl.when(i == 0)
    def _():
        ref[...] = val

    @pl.when(i != 0)
    def _():
        ref[...] += val


def _in_proj_fwd(x, g_mix, w_in, T, tm):
    def body(i, x_ref, g_ref, w_ref, h_ref, *rest):
        outs, pj_s = rest[:-1], rest[-1]
        g = g_ref[...]

        def norm(rows):
            h_ref[rows, :] = (_rms(x_ref[rows, :])[0] * g).astype(MM)

        _by_chunks(tm, norm)
        for d in range(N_DEV):
            pj_s[d] = _dot_nt(h_ref[...], w_ref[d])

        def join_and_cut(rows):
            proj = jnp.concatenate([pj_s[d, rows, :] for d in range(N_DEV)], axis=1)
            for (s, n), o_ref in zip(COL_SECTIONS, outs):
                if n == QK_ROPE:
                    o_ref[rows, :] = jnp.concatenate(
                        [jnp.zeros((ROW_CHUNK, QK_NOPE), F32), proj[:, s:s + n],
                         jnp.zeros((ROW_CHUNK, HEAD_PAD - QK_DIM), F32)], axis=1)
                else:
                    o_ref[rows, :] = proj[:, s:s + n]

        _by_chunks(tm, join_and_cut)

    row_outs = [(D_MODEL, MM)] + [(n, F32) for _, n in SECTIONS]
    return _row_call("in_proj_fwd", body, T, tm, [x], [g_mix, w_in], row_outs, [], VMEM_LIMIT,
                     scratch=[pltpu.VMEM((N_DEV, tm, IN_BLOCK), F32)])


def _mla_heads_fwd(raw, g_pad, cos_t, sin_t, first):
    outs, saved = [], []
    for h in range(MLA_HEADS):
        xh, r = _rms(raw[:, h * HEAD_PAD:(h + 1) * HEAD_PAD], QK_DIM)
        y = xh * g_pad
        outs.append(y * cos_t + _rope_swap(y, first) * sin_t)
        saved.append((xh, r))
    return outs, saved


def _mla_raw_heads(cqn, ckvn, kr, wuq_ref, wukv_ref, tm):
    lane = lax.broadcasted_iota(jnp.int32, (tm, HEAD_PAD), 1)
    nope = lane < QK_NOPE
    one_lane = jnp.where(lane == V_DIM, 1.0, 0.0)
    qs, ks, vs = [], [], []
    for h in range(MLA_HEADS):
        qs.append(_dot_nt(cqn, wuq_ref[h]))
        kv = _dot(ckvn, wukv_ref[h])
        ks.append(jnp.where(nope, kv, kr))
        vs.append(jnp.where(nope, pltpu.roll(kv, V_DIM, 1), one_lane))
    return jnp.concatenate(qs, axis=1), jnp.concatenate(ks, axis=1), jnp.concatenate(vs, axis=1)


def _mla_prep_fwd(cq, ckv, kr, pos, g_qa, g_kva, g_qn, g_kn, w_uq, w_ukv, T, tm):
    def body(i, cq_ref, ckv_ref, kr_ref, pos_ref, gqa_ref, gkva_ref, gqn_ref, gkn_ref, wuq_ref, wukv_ref,
             q_ref, k_ref, v_ref):
        cos_t, sin_t, first = _rope_tables(pos_ref[...], tm)
        cqn = _rms(cq_ref[...])[0] * gqa_ref[...]
        ckvn = _rms(ckv_ref[...])[0] * gkva_ref[...]
        q_raw, k_raw, v = _mla_raw_heads(cqn, ckvn, kr_ref[...], wuq_ref, wukv_ref, tm)
        qs, _ = _mla_heads_fwd(q_raw, gqn_ref[...], cos_t, sin_t, first)
        ks, _ = _mla_heads_fwd(k_raw, gkn_ref[...], cos_t, sin_t, first)
        q_ref[...] = (jnp.concatenate(qs, axis=1) * ATT_SCALE).astype(MM)
        k_ref[...] = jnp.concatenate(ks, axis=1).astype(MM)
        v_ref[...] = v.astype(MM)

    w = MLA_HEADS * HEAD_PAD
    return _row_call("mla_prep_fwd", body, T, tm, [cq, ckv, kr, pos], [g_qa, g_kva, g_qn, g_kn, w_uq, w_ukv],
                     [(w, MM), (w, MM), (w, MM)], [])


def _causal_pairs(n, by_query):
    if by_query:
        pairs = [(q, k) for q in range(n) for k in range(q + 1)]
    else:
        pairs = [(q, k) for k in range(n) for q in range(k, n)]
    return np.array([p[0] for p in pairs], np.int32), np.array([p[1] for p in pairs], np.int32)


def _flash_fwd(qf, kf, vf, T, ag_blocks=()):
    tq = min(ATT_TILE, T)
    nq = T // tq

    qi_tab, ki_tab = _causal_pairs(nq, by_query=True)

    hp = ATT_HEADS

    n_ag = len(ag_blocks)
    n_heads, n_pairs = MLA_HEADS // hp, len(qi_tab)

    def body(qi_ref, ki_ref, q_ref, k_ref, v_ref, *rest):
        ag_in, (o_ref, lse_ref), rest = rest[:n_ag], rest[n_ag:n_ag + 2], rest[n_ag + 2:]
        ag_out, (m_s, acc_s), ag_sems = rest[:n_ag], rest[n_ag:n_ag + 2], rest[n_ag + 2:]
        t = pl.program_id(1)
        qi, ki = qi_ref[t], ki_ref[t]
        if n_ag:
            @pl.when((pl.program_id(0) == 0) & (t == 0))
            def _():
                _ag_start(ag_in, ag_out, ag_sems)

        @pl.when(ki == 0)
        def _():
            m_s[...] = jnp.full_like(m_s, NEG)
            acc_s[...] = jnp.zeros_like(acc_s)

        def step(masked):
            for hh in range(hp):
                hs = slice(hh * HEAD_PAD, (hh + 1) * HEAD_PAD)
                s_t = _dot_nt(k_ref[:, hs], q_ref[:, hs])
                if masked:
                    key = lax.broadcasted_iota(jnp.int32, (tq, tq), 0)
                    qry = lax.broadcasted_iota(jnp.int32, (tq, tq), 1)
                    s_t = jnp.where(key <= qry, s_t, NEG)
                m_old = m_s[hh]
                m_new = jnp.maximum(m_old, jnp.max(s_t, axis=0, keepdims=True))
                p_t = jnp.exp(s_t - m_new)
                acc_s[hh] = jnp.exp(m_old - m_new) * acc_s[hh] + _dot_tn(v_ref[:, hs], p_t)
                m_s[hh] = m_new

        @pl.when(ki < qi)
        def _():
            step(False)

        @pl.when(ki == qi)
        def _():
            step(True)
            real = lax.broadcasted_iota(jnp.int32, (HEAD_PAD, tq), 0) < V_DIM
            for hh in range(hp):
                hs = slice(hh * HEAD_PAD, (hh + 1) * HEAD_PAD)
                acc = acc_s[hh]
                l = acc[V_DIM:V_DIM + 1]
                o_ref[:, hs] = jnp.where(real, acc / l, 0.0).T
                lse_ref[:, hs] = jnp.broadcast_to(m_s[hh] + jnp.log(l), (HEAD_PAD, tq)).T

        if n_ag:
            @pl.when((pl.program_id(0) == n_heads - 1) & (t == n_pairs - 1))
            def _():
                _ag_finish(ag_in, ag_out, ag_sems)

    q_spec = pl.BlockSpec((tq, hp * HEAD_PAD), lambda h, t, qi_ref, ki_ref: (qi_ref[t], h))
    kv_spec = pl.BlockSpec((tq, hp * HEAD_PAD), lambda h, t, qi_ref, ki_ref: (ki_ref[t], h))
    any_spec = pl.BlockSpec(memory_space=pl.ANY)
    grid_spec = pltpu.PrefetchScalarGridSpec(
        num_scalar_prefetch=2, grid=(n_heads, n_pairs),
        in_specs=[q_spec, kv_spec, kv_spec] + [any_spec] * n_ag, out_specs=[q_spec, q_spec] + [any_spec] * n_ag,
        scratch_shapes=[pltpu.VMEM((hp, 1, tq), F32), pltpu.VMEM((hp, HEAD_PAD, tq), F32)]
        + (_ag_sems(n_ag) if n_ag else []))
    return pl.pallas_call(
        body, name="flash_fwd", grid_spec=grid_spec,
        out_shape=[jax.ShapeDtypeStruct((T, MLA_HEADS * HEAD_PAD), F32)] * 2 + _ag_out_shapes(ag_blocks),
        compiler_params=_cparams(("arbitrary", "arbitrary")),
    )(jnp.asarray(qi_tab), jnp.asarray(ki_tab), qf, kf, vf, *ag_blocks)


def _hg_gates(hf, lb):
    sg = _sigmoid(hf)
    f = lb + (1.0 - lb) * sg
    return sg, f, jnp.log(f), 1.0 - f


def _tri(n, lower):
    r = lax.broadcasted_iota(jnp.int32, (n, n), 0)
    c = lax.broadcasted_iota(jnp.int32, (n, n), 1)
    return jnp.where((c <= r) if lower else (c >= r), 1.0, 0.0).astype(F32)


def _hg_levels():
    C = HG_CHUNK
    t = lax.broadcasted_iota(jnp.int32, (C, C), 0)
    s = lax.broadcasted_iota(jnp.int32, (C, C), 1)
    levels = []
    for shift in range(C.bit_length() - 2, -1, -1):
        pair_t, pair_s = lax.shift_right_logical(t, shift + 1), lax.shift_right_logical(s, shift + 1)
        later_t = (lax.shift_right_logical(t, shift) & 1) == 1
        earlier_s = (lax.shift_right_logical(s, shift) & 1) == 0
        levels.append((1 << shift, (pair_t == pair_s) & later_t & earlier_s))
    return levels, t == s


def _hg_refs(b):
    C, n = b.shape
    row = lax.broadcasted_iota(jnp.int32, (C, n), 0)
    back1, back2, ahead1 = pltpu.roll(b, 1, 0), pltpu.roll(b, 2, 0), pltpu.roll(b, C - 1, 0)
    refs = []
    for half in (32, 16, 8, 4):
        refs.append(jnp.concatenate(
            [jnp.broadcast_to(b[lo + half - 1:lo + half], (2 * half, n)) for lo in range(0, C, 2 * half)], axis=0))
    in4 = row & 3
    refs.append(jnp.where(in4 == 0, ahead1, jnp.where(in4 == 1, b, jnp.where(in4 == 2, back1, back2))))
    refs.append(jnp.where((row & 1) == 1, back1, b))
    return refs


def _hg_intra(q, k, b, refs, levels, eye):
    a = jnp.where(eye, jnp.sum(q * k, axis=1, keepdims=True), 0.0)
    saved = []
    for r, (_, mask) in zip(refs, levels):
        e = jnp.exp(-jnp.abs(b - r))
        q_t, k_t = q * e, k * e
        a = a + jnp.where(mask, _dot_nt(q_t, k_t), 0.0)
        saved.append((q_t, k_t, e))
    return a, saved


def _hg_intra_bwd(d_a, q, k, saved, levels, eye):
    diag = jnp.sum(jnp.where(eye, d_a, 0.0), axis=1, keepdims=True)
    dq, dk = diag * k, diag * q
    for (q_t, k_t, e), (_, mask) in zip(saved, levels):
        da = jnp.where(mask, d_a, 0.0)
        dq = dq + _dot(da, k_t) * e
        dk = dk + _dot_tn(da, q_t) * e
    return dq, dk


def _hgrn_fwd(hq, hf, hi, lb, T):
    rb = min(HG_BLOCK, T)
    ncb = rb // HG_CHUNK

    def body(hq_ref, hf_ref, hi_ref, lb_ref, o_ref, s0_ref, st_ref):
        @pl.when(pl.program_id(0) == 0)
        def _():
            st_ref[...] = jnp.zeros_like(st_ref)

        tril = _tri(HG_CHUNK, True)
        levels, eye = _hg_levels()

        def chunk(c, carry):
            rows = pl.ds(pl.multiple_of(c * HG_CHUNK, HG_CHUNK), HG_CHUNK)
            _, _, logf, kk = _hg_gates(hf_ref[rows, :], lb_ref[...])
            b = _dot_hi(tril, logf)
            refs = _hg_refs(b)
            q_all, v_all = hq_ref[rows, :], hi_ref[rows, :]
            outs = []
            for h in range(HG_HEADS):
                ls = slice(h * HG_DIM, (h + 1) * HG_DIM)
                q, k, v, bh = q_all[:, ls], kk[:, ls], v_all[:, ls], b[:, ls]
                st = st_ref[h]
                s0_ref[c, h * HG_DIM:(h + 1) * HG_DIM, :] = st
                b_end = bh[HG_CHUNK - 1:HG_CHUNK]
                a, _ = _hg_intra(q, k, bh, [r[:, ls] for r in refs], levels, eye)
                outs.append(_dot_nt(q * jnp.exp(bh), st) + _dot(a, v))
                st_ref[h] = st * jnp.exp(b_end) + _dot_tn(v, k * jnp.exp(b_end - bh))
            o_ref[rows, :] = jnp.concatenate(outs, axis=1)
            return carry

        lax.fori_loop(0, ncb, chunk, 0, unroll=HG_UNROLL)

    row = pl.BlockSpec((rb, HG_W), lambda i: (i, 0))
    return pl.pallas_call(
        body, name="hgrn_fwd", grid=(T // rb,),
        in_specs=[row, row, row, pl.BlockSpec((1, HG_W), lambda i: (0, 0))],
        out_specs=[row, pl.BlockSpec((ncb, HG_W, HG_DIM), lambda i: (i, 0, 0))],
        out_shape=[jax.ShapeDtypeStruct((T, HG_W), F32), jax.ShapeDtypeStruct((T // HG_CHUNK, HG_W, HG_DIM), F32)],
        scratch_shapes=[pltpu.VMEM((HG_HEADS, HG_DIM, HG_DIM), F32)],
        compiler_params=_cparams(("arbitrary",)),
    )(hq, hf, hi, lb)


def _hgrn_bwd(hq, hf, hi, do, s0, lb, T, xchg=((), ())):
    rb = min(HG_BLOCK, T)
    ncb = rb // HG_CHUNK
    nb = T // rb
    C = HG_CHUNK
    n_x, n_sib = _x_count(xchg), len(xchg[0])

    def body(hq_ref, hf_ref, hi_ref, do_ref, s0_ref, lb_ref, *rest):
        x_in, (dq_ref, df_ref, dv_ref, dlb_ref), rest = rest[:n_x], rest[n_x:n_x + 4], rest[n_x + 4:]
        x_out, dst_ref, x_sems = rest[:n_x], rest[n_x], rest[n_x + 1:]

        @pl.when(pl.program_id(0) == 0)
        def _():
            dst_ref[...] = jnp.zeros_like(dst_ref)
            dlb_ref[...] = jnp.zeros_like(dlb_ref)
            for cp in _x_copies(n_sib, x_in, x_out, x_sems):
                cp.start()

        tril, triu = _tri(C, True), _tri(C, False)
        row_cc = lax.broadcasted_iota(jnp.int32, (C, C), 0)
        col_cc = lax.broadcasted_iota(jnp.int32, (C, C), 1)
        last_row = lax.broadcasted_iota(jnp.int32, (C, HG_DIM), 0) == C - 1
        lb_v = lb_ref[...]
        levels, eye = _hg_levels()

        def chunk(cc, carry):
            c = ncb - 1 - cc
            rows = pl.ds(pl.multiple_of(c * C, C), C)
            hf_c = hf_ref[rows, :]
            sg, f, logf, kk = _hg_gates(hf_c, lb_v)
            b = _dot_hi(tril, logf)
            refs = _hg_refs(b)
            q_all, v_all, do_all = hq_ref[rows, :], hi_ref[rows, :], do_ref[rows, :]
            dq_o, dk_o, dv_o, db_o = [], [], [], []
            for h in range(HG_HEADS):
                ls = slice(h * HG_DIM, (h + 1) * HG_DIM)
                q, k, v, bh, d_o = q_all[:, ls], kk[:, ls], v_all[:, ls], b[:, ls], do_all[:, ls]
                st0 = s0_ref[c, h * HG_DIM:(h + 1) * HG_DIM, :]
                dst = dst_ref[h]
                b_end = bh[C - 1:C]
                e_b, e_end = jnp.exp(bh), jnp.exp(b_end)
                e_rem = jnp.exp(b_end - bh)
                qe, kd = q * e_b, k * e_rem
                st_end = st0 * e_end + _dot_tn(v, kd)
                a, saved = _hg_intra(q, k, bh, [r[:, ls] for r in refs], levels, eye)
                d_a = jnp.where(col_cc <= row_cc, _dot_nt(d_o, v), 0.0)
                dq_i, dk_i = _hg_intra_bwd(d_a, q, k, saved, levels, eye)
                dv = _dot_tn(a, d_o) + _dot_nt(kd, dst)
                dq = e_b * _dot(d_o, st0) + dq_i
                dk = e_rem * _dot(v, dst) + dk_i
                extra = jnp.sum(dst * st_end, axis=0, keepdims=True)
                db_o.append(q * dq - k * dk + jnp.where(last_row, extra, 0.0))
                dst_ref[h] = dst * e_end + _dot_tn(d_o, qe)
                dq_o.append(dq)
                dk_o.append(dk)
                dv_o.append(dv)
            dlogf = _dot_hi(triu, jnp.concatenate(db_o, axis=1))
            d_f = dlogf / f - jnp.concatenate(dk_o, axis=1)
            dq_ref[rows, :] = jnp.concatenate(dq_o, axis=1).astype(MM)
            dv_ref[rows, :] = jnp.concatenate(dv_o, axis=1).astype(MM)
            df_ref[rows, :] = (d_f * (1.0 - lb_v) * sg * (1.0 - sg)).astype(MM)
            dlb_ref[...] += jnp.sum(d_f * (1.0 - sg), axis=0, keepdims=True)
            return carry

        lax.fori_loop(0, ncb, chunk, 0, unroll=HG_UNROLL)

        if n_x:
            @pl.when(pl.program_id(0) == nb - 1)
            def _():
                for cp in _x_copies(n_sib, x_in, x_out, x_sems):
                    cp.wait()

    row = pl.BlockSpec((rb, HG_W), lambda i: (nb - 1 - i, 0))
    one = pl.BlockSpec((1, HG_W), lambda i: (0, 0))
    any_spec = pl.BlockSpec(memory_space=pl.ANY)
    return pl.pallas_call(
        body, name="hgrn_bwd", grid=(nb,),
        in_specs=[row, row, row, row, pl.BlockSpec((ncb, HG_W, HG_DIM), lambda i: (nb - 1 - i, 0, 0)), one]
        + [any_spec] * n_x,
        out_specs=[row, row, row, one] + [any_spec] * n_x,
        out_shape=[jax.ShapeDtypeStruct((T, HG_W), MM)] * 3 + [jax.ShapeDtypeStruct((1, HG_W), F32)]
        + _x_out_shapes(xchg),
        scratch_shapes=[pltpu.VMEM((HG_HEADS, HG_DIM, HG_DIM), F32)] + _x_sems(xchg),
        compiler_params=_cparams(("arbitrary",)),
    )(hq, hf, hi, do, s0, lb, *xchg[0], *xchg[1])


def _silu_parts(x):
    sg = _sigmoid(x)
    return x * sg, sg * (1.0 + x * (1.0 - sg))


def _merge_fwd(attn, o, hg, bg, x, g_out, w_bra, w_brb, w_out, T, tm):
    def body(i, attn_ref, o_ref, hg_ref, bg_ref, x_ref, g_ref, wa_ref, wb_ref, wo_ref,
             x1_ref, ya_ref, yb_ref, m_ref, rec_ref):
        g = g_ref[...]

        def recurrent_out(rows):
            for h in range(HG_HEADS):
                ls = slice(h * HG_DIM, (h + 1) * HG_DIM)
                rec_ref[rows, ls] = (_rms(o_ref[rows, ls])[0] * g * _silu_parts(hg_ref[rows, ls])[0]).astype(MM)

        _by_chunks(tm, recurrent_out)
        ya_ref[...] = _dot(attn_ref[...], wa_ref[...])
        yb_ref[...] = jnp.dot(rec_ref[...], wb_ref[...], preferred_element_type=F32)

        def gate(rows):
            m_ref[rows, :] = (_sigmoid(bg_ref[rows, :D_MODEL]) * ya_ref[rows, :]
                              + _sigmoid(bg_ref[rows, D_MODEL:]) * yb_ref[rows, :]).astype(MM)

        _by_chunks(tm, gate)
        x1_ref[...] = x_ref[...] + jnp.dot(m_ref[...], wo_ref[...], preferred_element_type=F32)

    return _row_call("merge_fwd", body, T, tm, [attn, o, hg, bg, x], [g_out, w_bra, w_brb, w_out],
                     [(D_MODEL, F32), (D_MODEL, F32), (D_MODEL, F32), (D_MODEL, MM), (HG_W, MM)], [], VMEM_LIMIT)


def _ffn_fwd(x1, g_ffn, w_g, w_u, w_d, T, tm):
    def body(i, x1_ref, g_ref, wg_ref, wu_ref, wd_ref, x2_ref, gt_ref, up_ref, h2_ref, a_s):
        g = g_ref[...]

        def norm(rows):
            h2_ref[rows, :] = (_rms(x1_ref[rows, :])[0] * g).astype(MM)

        _by_chunks(tm, norm)
        gt_ref[...] = _dot_nt(h2_ref[...], wg_ref[...])
        up_ref[...] = _dot_nt(h2_ref[...], wu_ref[...])

        def act(rows):
            for cs in FFN_HALVES:
                a_s[rows, cs] = (_silu_parts(gt_ref[rows, cs])[0] * up_ref[rows, cs]).astype(MM)

        _by_chunks(tm, act)
        x2_ref[...] = x1_ref[...] + jnp.dot(a_s[...], wd_ref[...], preferred_element_type=F32)

    return _row_call("ffn_fwd", body, T, tm, [x1], [g_ffn, w_g, w_u, w_d],
                     [(D_MODEL, F32), (FFN, F32), (FFN, F32), (D_MODEL, MM)], [], VMEM_LIMIT,
                     scratch=[pltpu.VMEM((tm, FFN), MM)])


def _ple_loss(x2, p, tgt, g_pg, g_post, w_pg, w_pp, T, tm):
    def body(i, x2_ref, p_ref, t_ref, gpg_ref, gpo_ref, wpg_ref, wpp_ref,
             dx2_ref, loss_ref, dgpo_ref, dgpg_ref, dwpg_ref, dwpp_ref, u_s, n3_s, z_s, dz_s, du_s, dy_s, dn3_s):
        @pl.when(i == 0)
        def _():
            for ref in (loss_ref, dgpo_ref, dgpg_ref, dwpg_ref, dwpp_ref):
                ref[...] = jnp.zeros_like(ref)

        gpg, gpo = gpg_ref[...], gpo_ref[...]
        p_mm = p_ref[...].astype(MM)
        for d in range(N_DEV):
            u_s[:, d * HEAD_PAD:(d + 1) * HEAD_PAD] = jnp.dot(p_mm, wpp_ref[d], preferred_element_type=F32)

        def gate_input(rows):
            n3_s[rows, :] = (_rms(x2_ref[rows, :])[0] * gpg).astype(MM)

        _by_chunks(tm, gate_input)
        z_s[...] = jnp.dot(n3_s[...], wpg_ref[...], preferred_element_type=F32)

        def loss_and_back(rows):
            uh, ru = _rms(u_s[rows, :])
            e = uh * gpo
            gate = _sigmoid(z_s[rows, :])
            diff = x2_ref[rows, :] + gate * e - t_ref[rows, :]
            dy = diff * (1.0 / D_MODEL)
            de = dy * gate
            dz_s[rows, :] = (dy * e * gate * (1.0 - gate)).astype(MM)
            du_s[rows, :] = _rms_bwd(de * gpo, uh, ru).astype(MM)
            dy_s[rows, :] = dy
            loss_ref[...] += _fold8(diff * diff) * (0.5 / D_MODEL)
            dgpo_ref[...] += _fold8(de * uh)

        _by_chunks(tm, loss_and_back)
        dn3_s[...] = _dot_nt(dz_s[...], wpg_ref[...])

        def gate_norm_back(rows):
            x2h, r3 = _rms(x2_ref[rows, :])
            dn3 = dn3_s[rows, :]
            dx2_ref[rows, :] = dy_s[rows, :] + _rms_bwd(dn3 * gpg, x2h, r3)
            dgpg_ref[...] += _fold8(dn3 * x2h)

        _by_chunks(tm, gate_norm_back)
        dwpg_ref[...] += _dot_tn(n3_s[...], dz_s[...])
        for d in range(N_DEV):
            dwpp_ref[d] += _dot_tn(p_mm, du_s[:, d * HEAD_PAD:(d + 1) * HEAD_PAD])

    vec = ((8, D_MODEL), F32)
    wide = lambda dt: pltpu.VMEM((tm, D_MODEL), dt)
    return _row_call("ple_loss", body, T, tm, [x2, p, tgt], [g_pg, g_post, w_pg, w_pp], [(D_MODEL, F32)],
                     [vec, vec, vec, ((D_MODEL, D_MODEL), F32), ((N_DEV, PLE, HEAD_PAD), F32)], VMEM_LIMIT,
                     scratch=[wide(F32), wide(MM), wide(F32), wide(MM), wide(MM), wide(F32), wide(F32)])


def _ffn_bwd(dx2, x1, gt, up, g_ffn, w_g, w_u, w_d, T, tm):
    def body(i, dx2_ref, x1_ref, gt_ref, up_ref, g_ref, wg_ref, wu_ref, wd_ref,
             dx1_ref, a_ref, dgt_ref, dup_ref, dg_ref, da_s, dh2_s):
        @pl.when(i == 0)
        def _():
            dg_ref[...] = jnp.zeros_like(dg_ref)

        g = g_ref[...]
        da_s[...] = _dot_nt(dx2_ref[...], wd_ref[...])

        def act_back(rows):
            for cs in FFN_HALVES:
                up, da = up_ref[rows, cs], da_s[rows, cs]
                silu, dsilu = _silu_parts(gt_ref[rows, cs])
                dgt_ref[rows, cs] = (da * up * dsilu).astype(MM)
                dup_ref[rows, cs] = (da * silu).astype(MM)
                a_ref[rows, cs] = (silu * up).astype(MM)

        _by_chunks(tm, act_back)
        dh2_s[...] = (jnp.dot(dgt_ref[...], wg_ref[...], preferred_element_type=F32)
                      + jnp.dot(dup_ref[...], wu_ref[...], preferred_element_type=F32))

        def norm_back(rows):
            x1h, r = _rms(x1_ref[rows, :])
            dh2 = dh2_s[rows, :]
            dx1_ref[rows, :] = dx2_ref[rows, :] + _rms_bwd(dh2 * g, x1h, r)
            dg_ref[...] += _fold8(dh2 * x1h)

        _by_chunks(tm, norm_back)

    return _row_call("ffn_bwd", body, T, tm, [dx2, x1, gt, up], [g_ffn, w_g, w_u, w_d],
                     [(D_MODEL, F32), (FFN, MM), (FFN, MM), (FFN, MM)], [((8, D_MODEL), F32)], VMEM_LIMIT,
                     scratch=[pltpu.VMEM((tm, FFN), F32), pltpu.VMEM((tm, D_MODEL), F32)])


def _merge_bwd(dx1, ya, yb, bg, o, hg, attn, m, rec, g_out, w_bra, w_brb, w_out, T, tm, xchg=((), ())):
    def body(i, dx1_ref, ya_ref, yb_ref, bg_ref, o_ref, hg_ref, attn_ref, m_ref, rec_ref, g_ref, wa_ref, wb_ref, wo_ref,
             dattn_ref, do_ref, dhg_ref, dbg_ref, dg_ref, dwo_ref, dwa_ref, dwb_ref, dm_s, dya_s, dyb_s, drec_s):
        @pl.when(i == 0)
        def _():
            for ref in (dg_ref, dwo_ref, dwa_ref, dwb_ref):
                ref[...] = jnp.zeros_like(ref)

        g = g_ref[...]
        dx1 = dx1_ref[...].astype(MM)
        dm_s[...] = _dot_nt(dx1, wo_ref[...])

        def gate_back(rows):
            dm = dm_s[rows, :]
            ga, gb = _sigmoid(bg_ref[rows, :D_MODEL]), _sigmoid(bg_ref[rows, D_MODEL:])
            dya_s[rows, :] = (dm * ga).astype(MM)
            dyb_s[rows, :] = (dm * gb).astype(MM)
            dbg_ref[rows, :D_MODEL] = (dm * ya_ref[rows, :] * ga * (1.0 - ga)).astype(MM)
            dbg_ref[rows, D_MODEL:] = (dm * yb_ref[rows, :] * gb * (1.0 - gb)).astype(MM)

        _by_chunks(tm, gate_back)
        dwo_ref[...] += _dot_tn(m_ref[...], dx1)
        attn_mm = attn_ref[...].astype(MM)
        for d in range(N_DEV):
            ds = slice(d * HEAD_PAD, (d + 1) * HEAD_PAD)
            dwa_ref[d] += _dot_tn(attn_mm, dya_s[:, ds])
            dwb_ref[d] += _dot_tn(rec_ref[...], dyb_s[:, ds])
        dattn_ref[...] = _dot_nt(dya_s[...], wa_ref[...])
        drec_s[...] = _dot_nt(dyb_s[...], wb_ref[...])

        def recurrent_out_back(rows):
            for h in range(HG_HEADS):
                ls = slice(h * HG_DIM, (h + 1) * HG_DIM)
                oh, r = _rms(o_ref[rows, ls])
                silu, dsilu = _silu_parts(hg_ref[rows, ls])
                dr = drec_s[rows, ls]
                dhg_ref[rows, ls] = (dr * oh * g * dsilu).astype(MM)
                don = dr * silu
                dg_ref[...] += _fold8(don * oh)
                do_ref[rows, ls] = _rms_bwd(don * g, oh, r)

        _by_chunks(tm, recurrent_out_back)

    wide = lambda n, dt: pltpu.VMEM((tm, n), dt)
    return _row_call("merge_bwd", body, T, tm, [dx1, ya, yb, bg, o, hg, attn, m, rec], [g_out, w_bra, w_brb, w_out],
                     [(D_MODEL, F32), (HG_W, F32), (HG_W, MM), (2 * D_MODEL, MM)],
                     [((8, HG_DIM), F32), ((D_MODEL, D_MODEL), F32), ((N_DEV, MLA_HEADS * HEAD_PAD, HEAD_PAD), F32),
                      ((N_DEV, HG_W, HEAD_PAD), F32)], VMEM_LIMIT,
                     scratch=[wide(D_MODEL, F32), wide(D_MODEL, MM), wide(D_MODEL, MM), wide(HG_W, F32)], xchg=xchg)


def _flash_bwd(qf, kf, vf, o, do, lse, T, xchg=((), ())):
    tq = min(ATT_TILE, T)
    nq = T // tq

    qi_tab, ki_tab = _causal_pairs(nq, by_query=False)

    n_x, n_sib = _x_count(xchg), len(xchg[0])
    hp = ATT_HEADS
    n_heads, n_pairs = MLA_HEADS // hp, len(qi_tab)

    def body(qi_ref, ki_ref, q_ref, k_ref, v_ref, o_ref, do_ref, lse_ref, *rest):
        x_in, (dq_ref, dk_ref, dv_ref), rest = rest[:n_x], rest[n_x:n_x + 3], rest[n_x + 3:]
        x_out, x_sems = rest[:n_x], rest[n_x:]
        t = pl.program_id(1)
        qi, ki = qi_ref[t], ki_ref[t]
        if n_x:
            @pl.when((pl.program_id(0) == 0) & (t == 0))
            def _():
                for cp in _x_copies(n_sib, x_in, x_out, x_sems):
                    cp.start()

        @pl.when(t == 0)
        def _():
            dq_ref[...] = jnp.zeros_like(dq_ref)

        def step(first):
            rows = pl.ds(pl.multiple_of(qi * tq, tq), tq)
            for hh in range(hp):
                hs = slice(hh * HEAD_PAD, (hh + 1) * HEAD_PAD)
                q, k, d_o = q_ref[:, hs], k_ref[:, hs], do_ref[:, hs]
                s = _dot_nt(q, k)
                if first:
                    row = lax.broadcasted_iota(jnp.int32, (tq, tq), 0)
                    col = lax.broadcasted_iota(jnp.int32, (tq, tq), 1)
                    s = jnp.where(col <= row, s, NEG)
                p = jnp.exp(s - lse_ref[:, hh * HEAD_PAD:hh * HEAD_PAD + 1])
                delta = jnp.sum(d_o * o_ref[:, hs], axis=1, keepdims=True)
                ds = p * (_dot_nt(d_o, v_ref[:, hs]) - delta)
                dq_ref[rows, hs] += _dot(ds, k)
                if first:
                    dv_ref[:, hs] = _dot_tn(p, d_o)
                    dk_ref[:, hs] = _dot_tn(ds, q)
                else:
                    dv_ref[:, hs] += _dot_tn(p, d_o)
                    dk_ref[:, hs] += _dot_tn(ds, q)

        @pl.when(qi == ki)
        def _():
            step(True)

        @pl.when(qi > ki)
        def _():
            step(False)

        if n_x:
            @pl.when((pl.program_id(0) == n_heads - 1) & (t == n_pairs - 1))
            def _():
                for cp in _x_copies(n_sib, x_in, x_out, x_sems):
                    cp.wait()

    q_spec = pl.BlockSpec((tq, hp * HEAD_PAD), lambda h, t, qi_ref, ki_ref: (qi_ref[t], h))
    kv_spec = pl.BlockSpec((tq, hp * HEAD_PAD), lambda h, t, qi_ref, ki_ref: (ki_ref[t], h))
    any_spec = pl.BlockSpec(memory_space=pl.ANY)
    w = MLA_HEADS * HEAD_PAD
    grid_spec = pltpu.PrefetchScalarGridSpec(
        num_scalar_prefetch=2, grid=(n_heads, n_pairs),
        in_specs=[q_spec, kv_spec, kv_spec, q_spec, q_spec, q_spec] + [any_spec] * n_x,
        out_specs=[pl.BlockSpec((T, hp * HEAD_PAD), lambda h, t, qi_ref, ki_ref: (0, h)), kv_spec, kv_spec]
        + [any_spec] * n_x,
        scratch_shapes=_x_sems(xchg))
    return pl.pallas_call(
        body, name="flash_bwd", grid_spec=grid_spec,
        out_shape=[jax.ShapeDtypeStruct((T, w), F32)] * 3 + _x_out_shapes(xchg),
        compiler_params=_cparams(("arbitrary", "arbitrary")),
    )(jnp.asarray(qi_tab), jnp.asarray(ki_tab), qf, kf, vf, o, do, lse, *xchg[0], *xchg[1])


def _mla_heads_bwd(d_out, saved, g_pad, cos_t, sin_t, first):
    d_raw, dg = [], jnp.zeros((1, HEAD_PAD), F32)
    for h in range(MLA_HEADS):
        xh, r = saved[h]
        dy = d_out[:, h * HEAD_PAD:(h + 1) * HEAD_PAD]
        dn = dy * cos_t + _rope_swap(dy * sin_t, first)
        dg = dg + jnp.sum(dn * xh, axis=0, keepdims=True)
        d_raw.append(_rms_bwd(dn * g_pad, xh, r, QK_DIM))
    return d_raw, dg


def _mla_prep_bwd(cq, ckv, kr, pos, dqf, dkf, dvf, g_qa, g_kva, g_qn, g_kn, w_uq, w_ukv, T, tm):
    def body(i, cq_ref, ckv_ref, kr_ref, pos_ref, dq_ref, dk_ref, dv_ref,
             gqa_ref, gkva_ref, gqn_ref, gkn_ref, wuq_ref, wukv_ref,
             dcq_ref, dckv_ref, dkr_ref, dgqa_ref, dgkva_ref, dgqn_ref, dgkn_ref, dwuq_ref, dwukv_ref):
        cos_t, sin_t, first = _rope_tables(pos_ref[...], tm)
        cqh, rq = _rms(cq_ref[...])
        ckvh, rkv = _rms(ckv_ref[...])
        cqn, ckvn = cqh * gqa_ref[...], ckvh * gkva_ref[...]
        q_raw, k_raw, _ = _mla_raw_heads(cqn, ckvn, kr_ref[...], wuq_ref, wukv_ref, tm)
        _, q_saved = _mla_heads_fwd(q_raw, gqn_ref[...], cos_t, sin_t, first)
        _, k_saved = _mla_heads_fwd(k_raw, gkn_ref[...], cos_t, sin_t, first)
        dq_heads, dgqn = _mla_heads_bwd(dq_ref[...] * ATT_SCALE, q_saved, gqn_ref[...], cos_t, sin_t, first)
        dk_heads, dgkn = _mla_heads_bwd(dk_ref[...], k_saved, gkn_ref[...], cos_t, sin_t, first)
        lane = lax.broadcasted_iota(jnp.int32, (tm, HEAD_PAD), 1)
        nope = lane < QK_NOPE
        dcqn = jnp.zeros((tm, Q_RANK), F32)
        dckvn = jnp.zeros((tm, KV_RANK), F32)
        dkr = jnp.zeros((tm, HEAD_PAD), F32)
        cqn_mm, ckvn_mm = cqn.astype(MM), ckvn.astype(MM)
        for h in range(MLA_HEADS):
            hs = slice(h * HEAD_PAD, (h + 1) * HEAD_PAD)
            dq_h = dq_heads[h].astype(MM)
            dkv_h = jnp.where(nope, dk_heads[h], pltpu.roll(dv_ref[:, hs], V_DIM, 1)).astype(MM)
            _acc(dwuq_ref.at[h], i, _dot_tn(dq_h, cqn_mm))
            _acc(dwukv_ref.at[h], i, _dot_tn(ckvn_mm, dkv_h))
            dcqn = dcqn + jnp.dot(dq_h, wuq_ref[h], preferred_element_type=F32)
            dckvn = dckvn + lax.dot_general(dkv_h, wukv_ref[h], (((1,), (1,)), ((), ())), preferred_element_type=F32)
            dkr = dkr + dk_heads[h]
        dkr_ref[...] = jnp.where((lane >= QK_NOPE) & (lane < QK_DIM), dkr, 0.0).astype(MM)
        dcq_ref[...] = _rms_bwd(dcqn * gqa_ref[...], cqh, rq).astype(MM)
        dckv_ref[...] = _rms_bwd(dckvn * gkva_ref[...], ckvh, rkv).astype(MM)
        _acc(dgqa_ref, i, jnp.sum(dcqn * cqh, axis=0, keepdims=True))
        _acc(dgkva_ref, i, jnp.sum(dckvn * ckvh, axis=0, keepdims=True))
        _acc(dgqn_ref, i, dgqn)
        _acc(dgkn_ref, i, dgkn)

    return _row_call(
        "mla_prep_bwd", body, T, tm, [cq, ckv, kr, pos, dqf, dkf, dvf], [g_qa, g_kva, g_qn, g_kn, w_uq, w_ukv],
        [(Q_RANK, MM), (KV_RANK, MM), (HEAD_PAD, MM)],
        [((1, Q_RANK), F32), ((1, KV_RANK), F32), ((1, HEAD_PAD), F32), ((1, HEAD_PAD), F32),
         ((MLA_HEADS, HEAD_PAD, Q_RANK), F32), ((MLA_HEADS, KV_RANK, HEAD_PAD), F32)], VMEM_LIMIT)


def _in_proj_bwd(x, dx1, dsecs, g_mix, w_in, T, tm):
    def body(i, x_ref, dx1_ref, *rest):
        d_refs, (g_ref, w_ref, dx_ref, dp_ref, dg_ref, dh_s) = rest[:len(SECTIONS)], rest[len(SECTIONS):]

        @pl.when(i == 0)
        def _():
            dg_ref[...] = jnp.zeros_like(dg_ref)

        g = g_ref[...]

        def join_and_cut(rows):
            pieces = [(d_ref[rows, QK_NOPE:QK_DIM] if n == QK_ROPE else d_ref[rows, :]).astype(F32)
                      for (_, n), d_ref in zip(COL_SECTIONS, d_refs)]
            dproj = jnp.concatenate(pieces, axis=1)
            for d in range(N_DEV):
                dp_ref[d, rows, :] = dproj[:, d * IN_BLOCK:(d + 1) * IN_BLOCK].astype(MM)

        _by_chunks(tm, join_and_cut)
        dh = jnp.dot(dp_ref[0], w_ref[0], preferred_element_type=F32)
        for d in range(1, N_DEV):
            dh = dh + jnp.dot(dp_ref[d], w_ref[d], preferred_element_type=F32)
        dh_s[...] = dh

        def norm_back(rows):
            xh, r = _rms(x_ref[rows, :])
            dh_c = dh_s[rows, :]
            dx_ref[rows, :] = dx1_ref[rows, :] + _rms_bwd(dh_c * g, xh, r)
            dg_ref[...] += _fold8(dh_c * xh)

        _by_chunks(tm, norm_back)

    in_specs = [pl.BlockSpec((tm, a.shape[1]), lambda i: (i, 0)) for a in [x, dx1, *dsecs]]
    in_specs += [pl.BlockSpec(g_mix.shape, lambda i: (0, 0)),
                 pl.BlockSpec(w_in.shape, lambda i: (0, 0, 0), pipeline_mode=pl.Buffered(1))]

    def kern(*refs):
        body(pl.program_id(0), *refs)

    return pl.pallas_call(
        kern, name="in_proj_bwd", grid=(T // tm,), in_specs=in_specs,
        out_specs=[pl.BlockSpec((tm, D_MODEL), lambda i: (i, 0)),
                   pl.BlockSpec((N_DEV, tm, IN_BLOCK), lambda i: (0, i, 0)),
                   pl.BlockSpec((8, D_MODEL), lambda i: (0, 0))],
        out_shape=[jax.ShapeDtypeStruct((T, D_MODEL), F32), jax.ShapeDtypeStruct((N_DEV, T, IN_BLOCK), MM),
                   jax.ShapeDtypeStruct((8, D_MODEL), F32)],
        scratch_shapes=[pltpu.VMEM((tm, D_MODEL), F32)],
        compiler_params=_cparams(("arbitrary",), VMEM_LIMIT),
    )(x, dx1, *dsecs, g_mix, w_in)


def _pick_block(n, cap):
    best = None
    for cand in range(128, min(n, cap) + 1, 128):
        if n % cand == 0:
            best = cand
    return n if best is None else best


def _pick_rows(n, cap):
    best = n
    for cand in range(8, min(n, cap) + 1, 8):
        if n % cand == 0:
            best = cand
    return best


def _matmul_tn(name, a, b):
    T, M = a.shape
    N = b.shape[1]
    bm, bk = _pick_block(M, 1408), min(512, T)
    bn = _pick_block(N, 2560)

    def body(a_ref, b_ref, c_ref):
        @pl.when(pl.program_id(2) == 0)
        def _():
            c_ref[...] = jnp.zeros_like(c_ref)

        c_ref[...] += _dot_tn(a_ref[...], b_ref[...])

    return pl.pallas_call(
        body, name=name, grid=(M // bm, N // bn, T // bk),
        in_specs=[pl.BlockSpec((bk, bm), lambda i, j, k: (k, i)), pl.BlockSpec((bk, bn), lambda i, j, k: (k, j))],
        out_specs=pl.BlockSpec((bm, bn), lambda i, j, k: (i, j)), out_shape=jax.ShapeDtypeStruct((M, N), F32),
        compiler_params=_cparams(("parallel", "parallel", "arbitrary"), VMEM_LIMIT),
    )(a, b)


def _matmul_tn_blocks(name, a, b):
    T, M = a.shape
    nd, _, c = b.shape
    bm, bk = _pick_block(M, 512), min(512, T)

    def body(a_ref, b_ref, c_ref):
        @pl.when(pl.program_id(1) == 0)
        def _():
            c_ref[...] = jnp.zeros_like(c_ref)

        a_blk = a_ref[...].astype(MM)
        for d in range(nd):
            c_ref[d] += _dot_tn(b_ref[d], a_blk)

    return pl.pallas_call(
        body, name=name, grid=(M // bm, T // bk),
        in_specs=[pl.BlockSpec((bk, bm), lambda i, k: (k, i)), pl.BlockSpec((nd, bk, c), lambda i, k: (0, k, 0))],
        out_specs=pl.BlockSpec((nd, c, bm), lambda i, k: (0, 0, i)),
        out_shape=jax.ShapeDtypeStruct((nd, c, M), F32),
        compiler_params=_cparams(("parallel", "arbitrary"), VMEM_LIMIT),
    )(a, b)


def _pad_gain(g, n):
    return jnp.pad(g.reshape(1, -1), ((0, 0), (0, n - g.shape[-1])))


GROUP_A = ("w_ffn_gate", "w_ffn_up", "w_ffn_down", "w_ple_gate", "w_ple_proj")
GROUP_B = ("w_branch", "w_out")
GROUP_C = ("w_in", "w_uq", "w_ukv")
EARLY = GROUP_C
LATE = GROUP_B + GROUP_A
TRANSPOSED = ("w_in", "w_uq", "w_ffn_gate", "w_ffn_up")


def _local_step(x, p, pos, tgt, small, big, late_blocks=None, core=None):
    T = x.shape[0]
    tm = min(ROW_TILE, T)
    w_in = big["w_in"]
    w_uq = jnp.pad(big["w_uq"], ((0, 0), (0, HEAD_PAD - QK_DIM), (0, 0)))
    w_ukv = big["w_ukv"]

    g_mix, g_qa, g_kva = small["mix_norm_g"], small["q_a_norm_g"], small["kv_a_norm_g"]
    g_qn, g_kn = _pad_gain(small["q_norm_g"], HEAD_PAD), _pad_gain(small["k_norm_g"], HEAD_PAD)
    g_out, g_ffn = small["hg_out_norm_g"], small["ffn_norm_g"]
    g_pg, g_post = small["ple_gate_norm_g"], small["ple_post_norm_g"]
    logits = small["hg_lb_logits"]
    lb = _lower_bound(logits)

    h, cq, ckv, kr, hq, hf, hi, hg, bg = _in_proj_fwd(x, g_mix, w_in, T, tm)
    qf, kf, vf = _mla_prep_fwd(cq, ckv, kr, pos, g_qa, g_kva, g_qn, g_kn, w_uq, w_ukv, T, tm)
    if late_blocks is None:
        attn, lse = _flash_fwd(qf, kf, vf, T)
    else:
        attn, lse, *late = _flash_fwd(qf, kf, vf, T, ag_blocks=[late_blocks[n] for n in LATE])
        big = {**big, **dict(zip(LATE, late))}
    w_branch = jnp.moveaxis(big["w_branch"].reshape(N_DEV, 2, HG_W, HEAD_PAD), 0, 2).reshape(2, HG_W, D_MODEL)
    w_bra = jnp.pad(w_branch[0].reshape(MLA_HEADS, V_DIM, D_MODEL),
                    ((0, 0), (0, HEAD_PAD - V_DIM), (0, 0))).reshape(MLA_HEADS * HEAD_PAD, D_MODEL)
    w_brb = w_branch[1]
    w_out = big["w_out"].reshape(D_MODEL, D_MODEL)
    w_g, w_u = big["w_ffn_gate"].reshape(FFN, D_MODEL), big["w_ffn_up"].reshape(FFN, D_MODEL)
    w_d = big["w_ffn_down"].reshape(FFN, D_MODEL)
    w_pg, w_pp = big["w_ple_gate"].reshape(D_MODEL, D_MODEL), big["w_ple_proj"]
    o, s0 = _hgrn_fwd(hq, hf, hi, lb, T)
    x1, ya, yb, m, rec = _merge_fwd(attn, o, hg, bg, x, g_out, w_bra, w_brb, w_out, T, tm)
    x2, gt, up, h2 = _ffn_fwd(x1, g_ffn, w_g, w_u, w_d, T, tm)
    dx2, loss_p, dg_post, dg_pg, d_pg, d_pp = _ple_loss(x2, p, tgt, g_pg, g_post, w_pg, w_pp, T, tm)

    grads, sibs, gots = {}, {}, {}
    dist = core is not None
    pick = lambda names: [grads[n] for n in names] if dist else ()

    def partials(tag, names, got):
        if not dist:
            return ()
        sibs.update(zip(names, got))
        return _chip_partials("rs_partial_" + tag, pick(names), got, core)

    dx1, a, dgt, dup, dg_ffn = _ffn_bwd(dx2, x1, gt, up, g_ffn, w_g, w_u, w_d, T, tm)
    grads["w_ffn_gate"] = _matmul_tn("dw_gate", dgt, h2).reshape(N_DEV, -1, D_MODEL)
    grads["w_ffn_up"] = _matmul_tn("dw_up", dup, h2).reshape(N_DEV, -1, D_MODEL)
    grads["w_ffn_down"] = _matmul_tn("dw_down", a, dx2).reshape(N_DEV, -1, D_MODEL)
    grads["w_ple_gate"] = d_pg.reshape(N_DEV, -1, D_MODEL)
    grads["w_ple_proj"] = d_pp

    dattn, do, dhg, dbg, dg_out, d_out, d_bra, d_brb, *sib_a = _merge_bwd(
        dx1, ya, yb, bg, o, hg, attn, m, rec, g_out, w_bra, w_brb, w_out, T, tm, xchg=(pick(GROUP_A), ()))
    parts_a = partials("a", GROUP_A, sib_a)
    d_bra = d_bra.reshape(N_DEV, MLA_HEADS, HEAD_PAD, HEAD_PAD)[:, :, :V_DIM].reshape(N_DEV, HG_W, HEAD_PAD)
    grads["w_branch"] = jnp.concatenate([d_bra, d_brb], axis=1)
    grads["w_out"] = d_out.reshape(N_DEV, -1, D_MODEL)

    dhq, dhf, dhi, dlb, *got = _hgrn_bwd(hq, hf, hi, do, s0, lb, T, xchg=(pick(GROUP_B), parts_a))
    sib_b, got_a = got[:len(GROUP_B)], got[len(GROUP_B):]
    parts_b = partials("b", GROUP_B, sib_b)
    dqf, dkf, dvf, *got_b = _flash_bwd(qf, kf, vf, attn, dattn, lse, T, xchg=((), parts_b))
    (dcq, dckv, dkr, dg_qa, dg_kva, dg_qn, dg_kn, d_uq, d_ukv) = _mla_prep_bwd(
        cq, ckv, kr, pos, dqf, dkf, dvf, g_qa, g_kva, g_qn, g_kn, w_uq, w_ukv, T, tm)
    grad_x, dproj, dg_mix = _in_proj_bwd(x, dx1, [dcq, dckv, dkr, dhq, dhf, dhi, dhg, dbg], g_mix, w_in, T, tm)
    grads["w_in"] = _matmul_tn_blocks("dw_in", h, dproj)
    grads["w_uq"] = d_uq[:, :QK_DIM]
    grads["w_ukv"] = d_ukv
    parts_c = ()
    if dist:
        parts_c = partials("c", GROUP_C, _exchange_sibling("rs_sibling_c", pick(GROUP_C)))
        gots.update(zip(GROUP_A, got_a))
        gots.update(zip(GROUP_B, got_b))

    dl0 = dlb * lb * (1.0 - lb)
    small_g = {
        "mix_norm_g": dg_mix, "q_a_norm_g": dg_qa, "kv_a_norm_g": dg_kva, "q_norm_g": dg_qn, "k_norm_g": dg_kn,
        "hg_lb_logits": jnp.concatenate([dl0, -dl0], axis=0), "hg_out_norm_g": dg_out,
        "ffn_norm_g": dg_ffn, "ple_gate_norm_g": dg_pg, "ple_post_norm_g": dg_post,
    }
    return loss_p, grad_x, small_g, grads, sibs, gots, parts_c


def _lower_bound(logits):
    def body(l_ref, lb_ref):
        l = l_ref[...]
        mx = jnp.max(l, axis=0, keepdims=True)
        e = jnp.exp(l - mx)
        lb_ref[...] = e[0:1] / jnp.sum(e, axis=0, keepdims=True)

    return pl.pallas_call(body, name="lower_bound", out_shape=jax.ShapeDtypeStruct((1, HG_W), F32))(logits)


def _my_place():
    return lax.axis_index("x"), lax.axis_index("y"), lax.axis_index("c")


def _all_gather(name, blocks):
    n = len(blocks)

    def body(*refs):
        x_refs, out_refs, sems = refs[:n], refs[n:2 * n], refs[2 * n:]
        _ag_start(x_refs, out_refs, sems)
        _ag_finish(x_refs, out_refs, sems)

    any_spec = pl.BlockSpec(memory_space=pl.ANY)
    return pl.pallas_call(
        body, name=name, out_shape=_ag_out_shapes(blocks),
        in_specs=[any_spec] * n, out_specs=[any_spec] * n, scratch_shapes=_ag_sems(n),
    )(*blocks)


def _ag_out_shapes(blocks):
    return [jax.ShapeDtypeStruct((N_DEV,) + b.shape, b.dtype) for b in blocks]


def _ag_sems(n):
    return [pltpu.SemaphoreType.DMA((7 * n,)), pltpu.SemaphoreType.DMA((7 * n,)), pltpu.SemaphoreType.DMA((n,))]


def _ag_parts(x_refs, out_refs, sems):
    send_sems, recv_sems, local_sems = sems
    x, y, c = _my_place()
    me, sibling = (x, y, c), (x, y, 1 - c)
    chips = [(1 - x, y), (x, 1 - y), (1 - x, 1 - y)]
    n = len(x_refs)

    def copy(a, k, block, to, own=False):
        px, py, pc = block
        dst = out_refs[a].at[4 * px + 2 * py + pc]
        return pltpu.make_async_remote_copy(
            src_ref=x_refs[a] if own else dst, dst_ref=dst, send_sem=send_sems.at[7 * a + k],
            recv_sem=recv_sems.at[7 * a + k], device_id=to, device_id_type=MESH_ID)

    mine = [pltpu.make_async_copy(x_refs[a], out_refs[a].at[4 * x + 2 * y + c], local_sems.at[a]) for a in range(n)]
    first = []
    for a in range(n):
        first.append(copy(a, 0, me, sibling, own=True))
        first += [copy(a, 1 + j, me, (*chip, c), own=True) for j, chip in enumerate(chips)]
    return copy, mine, first, me, sibling, chips, c, n


def _ag_start(x_refs, out_refs, sems):
    _, mine, first, *_ = _ag_parts(x_refs, out_refs, sems)
    for cp in mine + first:
        cp.start()


def _ag_finish(x_refs, out_refs, sems):
    copy, mine, first, me, sibling, chips, c, n = _ag_parts(x_refs, out_refs, sems)
    passed = []
    for j, chip in enumerate(chips):
        for a in range(n):
            copy(a, 1 + j, (*chip, c), me).wait_recv()
            passed.append(copy(a, 4 + j, (*chip, c), sibling))
            passed[-1].start()
    for a in range(n):
        copy(a, 0, sibling, me).wait_recv()
    for j, chip in enumerate(chips):
        for a in range(n):
            copy(a, 4 + j, (*chip, 1 - c), me).wait_recv()
    for cp in first + passed:
        cp.wait_send()
    for cp in mine:
        cp.wait()


def _exchange_sibling(name, gs):
    return _exchange(name, (gs, ()))


def _exchange(name, xchg):
    n = _x_count(xchg)

    def body(*refs):
        in_refs, out_refs, sems = refs[:n], refs[n:2 * n], refs[2 * n:]
        for cp in _x_copies(len(xchg[0]), in_refs, out_refs, sems):
            cp.start()
        for cp in _x_copies(len(xchg[0]), in_refs, out_refs, sems):
            cp.wait()

    any_spec = pl.BlockSpec(memory_space=pl.ANY)
    return pl.pallas_call(
        body, name=name, out_shape=_x_out_shapes(xchg), in_specs=[any_spec] * n, out_specs=[any_spec] * n,
        scratch_shapes=_x_sems(xchg),
    )(*xchg[0], *xchg[1])


N_PARTS = 4


def _part_spec(rows, cols, t_pos, lead_block=(), lead_index=lambda *args: ()):
    if rows % (16 * N_PARTS) == 0:
        axis, shape, count = 0, (rows // N_PARTS, cols), N_PARTS
    elif cols % (128 * N_PARTS) == 0:
        axis, shape, count = 1, (rows, cols // N_PARTS), N_PARTS
    else:
        axis, shape, count = 0, (rows, cols), 1

    def index(*args):
        i = jnp.minimum(args[t_pos], count - 1)
        return (*lead_index(*args), *((i, 0) if axis == 0 else (0, i)))

    return pl.BlockSpec((*lead_block, *shape), index)


def _chip_partials(name, gs, sibs, c_idx):
    n = len(gs)

    def body(c_ref, *refs):
        for g_ref, sib_ref, out_ref in zip(refs[:n], refs[n:2 * n], refs[2 * n:]):
            out_ref[...] = (g_ref[...] + sib_ref[...]).astype(MM)

    own = [_part_spec(*g.shape[1:], 1, (1,), lambda j, t, c_ref: (2 * j + c_ref[0],)) for g in gs]
    by_chip = [_part_spec(*g.shape[1:], 1, (1,), lambda j, t, c_ref: (j,)) for g in gs]
    grid_spec = pltpu.PrefetchScalarGridSpec(
        num_scalar_prefetch=1, grid=(4, N_PARTS), in_specs=own + by_chip, out_specs=by_chip)
    return pl.pallas_call(
        body, name=name, grid_spec=grid_spec, out_shape=[jax.ShapeDtypeStruct((4,) + g.shape[1:], MM) for g in gs],
        compiler_params=_cparams(("arbitrary", "arbitrary"), VMEM_LIMIT),
    )(c_idx, *gs, *sibs)


def _exchange_chips(parts):
    return _exchange("rs_chips", ((), parts))


def _x_count(xchg):
    return len(xchg[0]) + len(xchg[1])


def _x_out_shapes(xchg):
    return ([jax.ShapeDtypeStruct((4,) + g.shape[1:], g.dtype) for g in xchg[0]]
            + [jax.ShapeDtypeStruct((3,) + p.shape[1:], p.dtype) for p in xchg[1]])


def _x_sems(xchg):
    n = 4 * len(xchg[0]) + 3 * len(xchg[1])
    return [pltpu.SemaphoreType.DMA((n,)), pltpu.SemaphoreType.DMA((n,))] if n else []


def _x_copies(n_sib, in_refs, out_refs, sems):
    if not in_refs:
        return []
    send_sems, recv_sems = sems
    x, y, c = _my_place()
    chips = [(1 - x, y), (x, 1 - y), (1 - x, 1 - y)]
    copies = []

    def add(src, dst, to):
        k = len(copies)
        copies.append(pltpu.make_async_remote_copy(
            src_ref=src, dst_ref=dst, send_sem=send_sems.at[k], recv_sem=recv_sems.at[k], device_id=to,
            device_id_type=MESH_ID))

    for a, (src, dst) in enumerate(zip(in_refs, out_refs)):
        if a < n_sib:
            for j in range(4):
                add(src.at[2 * j + 1 - c], dst.at[j], (x, y, 1 - c))
        else:
            for k, (px, py) in enumerate(chips):
                add(src.at[2 * px + py], dst.at[k], (px, py, c))
    return copies


def _adamw_math(w, g, m, v):
    m = ADAM_B1 * m + (1.0 - ADAM_B1) * g
    v = ADAM_B2 * v + (1.0 - ADAM_B2) * jnp.square(g)
    m_hat = m / (1.0 - ADAM_B1 ** ADAM_STEP)
    v_hat = v / (1.0 - ADAM_B2 ** ADAM_STEP)
    delta = -ADAM_LR * (m_hat / (jnp.sqrt(v_hat) + ADAM_EPS) + ADAM_WD * w)
    return delta, m, v


def _sum_adamws(name, gs, sibs, gots, ws, ms, vs, slot_idx, chip_idx):
    n = len(gs)

    def body(s_ref, j_ref, *refs):
        ins, outs = refs[:6 * n], refs[6 * n:]
        for a in range(n):
            g_ref, sib_ref, got_ref, w_ref, m_ref, v_ref = (ins[k * n + a] for k in range(6))
            go_ref, d_ref, m2_ref, v2_ref = outs[4 * a:4 * a + 4]
            grad = g_ref[0] + sib_ref[0]
            for k in range(3):
                grad = grad + got_ref[k].astype(F32)
            go_ref[...] = grad
            d_ref[...], m2_ref[...], v2_ref[...] = _adamw_math(w_ref[...], grad, m_ref[...], v_ref[...])

    shapes = [g.shape[1:] for g in gs]
    flat = [_part_spec(*s, 0) for s in shapes]
    in_specs = ([_part_spec(*s, 0, (1,), lambda t, s_ref, j_ref: (s_ref[0],)) for s in shapes]
                + [_part_spec(*s, 0, (1,), lambda t, s_ref, j_ref: (j_ref[0],)) for s in shapes]
                + [_part_spec(*s, 0, (3,), lambda t, s_ref, j_ref: (0,)) for s in shapes] + flat * 3)
    grid_spec = pltpu.PrefetchScalarGridSpec(
        num_scalar_prefetch=2, grid=(N_PARTS,), in_specs=in_specs, out_specs=[f for f in flat for _ in range(4)])
    res = pl.pallas_call(
        body, name=name, grid_spec=grid_spec,
        out_shape=[jax.ShapeDtypeStruct(s, F32) for s in shapes for _ in range(4)],
        compiler_params=_cparams(("arbitrary",), VMEM_LIMIT),
    )(slot_idx, chip_idx, *gs, *sibs, *gots, *ws, *ms, *vs)
    return [res[4 * a:4 * a + 4] for a in range(n)]


BIG = ("w_in", "w_uq", "w_ukv", "w_branch", "w_out", "w_ffn_gate", "w_ffn_up", "w_ffn_down", "w_ple_gate", "w_ple_proj")
SMALL = (
    ("mix_norm_g", 0, 1, 1024), ("q_a_norm_g", 1, 1, 384), ("kv_a_norm_g", 2, 1, 256), ("q_norm_g", 3, 1, 96),
    ("k_norm_g", 4, 1, 96), ("hg_lb_logits", 5, 2, 512), ("hg_out_norm_g", 7, 1, 128), ("ffn_norm_g", 8, 1, 1024),
    ("ple_gate_norm_g", 9, 1, 1024), ("ple_post_norm_g", 10, 1, 1024),
)
SLAB_ROWS, LOSS_ROW = 16, 15


def _pack_partials(small_g, loss_p):
    def body(*refs):
        val_refs, loss_ref, out_ref = refs[:len(SMALL)], refs[len(SMALL)], refs[len(SMALL) + 1]
        out_ref[...] = jnp.zeros_like(out_ref)
        for (_, r0, rows, cols), ref in zip(SMALL, val_refs):
            val = ref[...]
            if val.shape[0] != rows:
                val = jnp.sum(val, axis=0, keepdims=True)
            out_ref[r0:r0 + rows, :cols] = val[:, :cols]
        out_ref[LOSS_ROW:LOSS_ROW + 1, :HEAD_PAD] = jnp.full((1, HEAD_PAD), jnp.sum(loss_ref[...]), F32)

    return pl.pallas_call(
        body, name="pack_partials", out_shape=jax.ShapeDtypeStruct((SLAB_ROWS, D_MODEL), F32),
    )(*[small_g[n] for n, *_ in SMALL], loss_p)


def _adamw_small(parts, ws, ms, vs):
    n = len(SMALL)

    def body(p_ref, *refs):
        ins, loss_ref, outs = refs[:3 * n], refs[3 * n], refs[3 * n + 1:]
        total = p_ref[0]
        for d in range(1, N_DEV):
            total = total + p_ref[d]
        loss_ref[...] = total[LOSS_ROW:LOSS_ROW + 1, 0:1]
        for a, (_, r0, rows, cols) in enumerate(SMALL):
            g = total[r0:r0 + rows, :cols]
            outs[4 * a][...] = g
            outs[4 * a + 1][...], outs[4 * a + 2][...], outs[4 * a + 3][...] = _adamw_math(
                ins[a][...], g, ins[n + a][...], ins[2 * n + a][...])

    shapes = [jax.ShapeDtypeStruct((rows, cols), F32) for _, _, rows, cols in SMALL]
    res = pl.pallas_call(
        body, name="adamw_small", out_shape=[jax.ShapeDtypeStruct((1, 1), F32)] + [s for s in shapes for _ in range(4)],
    )(parts, *ws, *ms, *vs)
    return res[0], [res[1 + 4 * a:5 + 4 * a] for a in range(n)]


_WEIGHTS = ["mix_norm_g", "w_in", "q_a_norm_g", "w_uq", "kv_a_norm_g", "w_ukv", "q_norm_g", "k_norm_g", "hg_lb_logits",
            "hg_out_norm_g", "w_branch", "w_out", "ffn_norm_g", "w_ffn_gate", "w_ffn_up", "w_ffn_down",
            "ple_gate_norm_g", "w_ple_gate", "w_ple_proj", "ple_post_norm_g"]


def _step(x, p, positions, tgt, w, m, v):
    small_names = [n for n, *_ in SMALL]
    T = x.shape[1]
    px, py, pc = _my_place()
    as_idx = lambda t: jnp.reshape(t, (1,)).astype(jnp.int32)

    def two_d(n, t):
        t = t.reshape(-1, t.shape[-1])
        return t.T if n in TRANSPOSED else t

    def full_shape(n, t):
        return (t.T if n in TRANSPOSED else t).reshape(w[n].shape)

    blocks = {n: two_d(n, w[n]).astype(MM) for n in BIG}
    big = dict(zip(EARLY, _all_gather("ag_weights", [blocks[n] for n in EARLY])))
    small = {n: (w[n] if n == "hg_lb_logits" else w[n].reshape(1, -1)) for n in small_names}

    loss_p, grad_x, small_g, grads, sibs, gots, parts_c = _local_step(
        x[0], p[0, 0], positions.reshape(T, 1), tgt[0], small, big, late_blocks=blocks, core=as_idx(pc))

    gots.update(zip(GROUP_C, _exchange_chips(parts_c)))
    out_g, out_d, out_m, out_v = {}, {}, {}, {}
    for tag, names in (("ab", GROUP_A + GROUP_B), ("c", GROUP_C)):
        pick = lambda table: [table[n] for n in names]
        res = _sum_adamws("adamw_" + tag, pick(grads), pick(sibs), pick(gots), [two_d(n, w[n]) for n in names],
                          [two_d(n, m[n]) for n in names], [two_d(n, v[n]) for n in names],
                          as_idx(4 * px + 2 * py + pc), as_idx(2 * px + py))
        for n, r in zip(names, res):
            out_g[n], out_d[n], out_m[n], out_v[n] = [full_shape(n, t) for t in r]

    parts = _all_gather("ag_small", [_pack_partials(small_g, loss_p)])[0]
    loss, res = _adamw_small(parts, *([t[n] for n in small_names] for t in (w, m, v)))
    for n, r in zip(small_names, res):
        out_g[n], out_d[n], out_m[n], out_v[n] = r

    outs = [loss.reshape(()), grad_x[None]]
    for table in (out_g, out_d, out_m, out_v):
        outs += [table[n] for n in _WEIGHTS]
    return tuple(outs)


def kernel(x, p, positions, mix_norm_g, w_in, q_a_norm_g, w_uq, kv_a_norm_g, w_ukv, q_norm_g, k_norm_g, hg_lb_logits, hg_out_norm_g, w_branch, w_out, ffn_norm_g, w_ffn_gate, w_ffn_up, w_ffn_down, ple_gate_norm_g, w_ple_gate, w_ple_proj, ple_post_norm_g, loss_target, m_mix_norm_g, m_w_in, m_q_a_norm_g, m_w_uq, m_kv_a_norm_g, m_w_ukv, m_q_norm_g, m_k_norm_g, m_hg_lb_logits, m_hg_out_norm_g, m_w_branch, m_w_out, m_ffn_norm_g, m_w_ffn_gate, m_w_ffn_up, m_w_ffn_down, m_ple_gate_norm_g, m_w_ple_gate, m_w_ple_proj, m_ple_post_norm_g, v_mix_norm_g, v_w_in, v_q_a_norm_g, v_w_uq, v_kv_a_norm_g, v_w_ukv, v_q_norm_g, v_k_norm_g, v_hg_lb_logits, v_hg_out_norm_g, v_w_branch, v_w_out, v_ffn_norm_g, v_w_ffn_gate, v_w_ffn_up, v_w_ffn_down, v_ple_gate_norm_g, v_w_ple_gate, v_w_ple_proj, v_ple_post_norm_g):
    w = dict(mix_norm_g=mix_norm_g, w_in=w_in, q_a_norm_g=q_a_norm_g, w_uq=w_uq, kv_a_norm_g=kv_a_norm_g, w_ukv=w_ukv,
             q_norm_g=q_norm_g, k_norm_g=k_norm_g, hg_lb_logits=hg_lb_logits, hg_out_norm_g=hg_out_norm_g,
             w_branch=w_branch, w_out=w_out, ffn_norm_g=ffn_norm_g, w_ffn_gate=w_ffn_gate, w_ffn_up=w_ffn_up,
             w_ffn_down=w_ffn_down, ple_gate_norm_g=ple_gate_norm_g, w_ple_gate=w_ple_gate, w_ple_proj=w_ple_proj,
             ple_post_norm_g=ple_post_norm_g)
    m = dict(mix_norm_g=m_mix_norm_g, w_in=m_w_in, q_a_norm_g=m_q_a_norm_g, w_uq=m_w_uq, kv_a_norm_g=m_kv_a_norm_g,
             w_ukv=m_w_ukv, q_norm_g=m_q_norm_g, k_norm_g=m_k_norm_g, hg_lb_logits=m_hg_lb_logits,
             hg_out_norm_g=m_hg_out_norm_g, w_branch=m_w_branch, w_out=m_w_out, ffn_norm_g=m_ffn_norm_g,
             w_ffn_gate=m_w_ffn_gate, w_ffn_up=m_w_ffn_up, w_ffn_down=m_w_ffn_down,
             ple_gate_norm_g=m_ple_gate_norm_g, w_ple_gate=m_w_ple_gate, w_ple_proj=m_w_ple_proj,
             ple_post_norm_g=m_ple_post_norm_g)
    v = dict(mix_norm_g=v_mix_norm_g, w_in=v_w_in, q_a_norm_g=v_q_a_norm_g, w_uq=v_w_uq, kv_a_norm_g=v_kv_a_norm_g,
             w_ukv=v_w_ukv, q_norm_g=v_q_norm_g, k_norm_g=v_k_norm_g, hg_lb_logits=v_hg_lb_logits,
             hg_out_norm_g=v_hg_out_norm_g, w_branch=v_w_branch, w_out=v_w_out, ffn_norm_g=v_ffn_norm_g,
             w_ffn_gate=v_w_ffn_gate, w_ffn_up=v_w_ffn_up, w_ffn_down=v_w_ffn_down,
             ple_gate_norm_g=v_ple_gate_norm_g, w_ple_gate=v_w_ple_gate, w_ple_proj=v_w_ple_proj,
             ple_post_norm_g=v_ple_post_norm_g)
    return _step(x, p, positions, loss_target, w, m, v)
```

```python
import functools

import jax
import jax.numpy as jnp
import numpy as np
from jax import lax
from jax.experimental import pallas as pl
from jax.experimental.pallas import tpu as pltpu

F32 = jnp.float32
MM = jnp.bfloat16
HI = lax.Precision.HIGHEST
MESH_ID = pl.DeviceIdType.MESH

D_MODEL = 1024
N_DEV = 8
MLA_HEADS = 8
QK_NOPE = 64
QK_ROPE = 32
QK_DIM = 96
V_DIM = 64
HEAD_PAD = 128
Q_RANK = 384
KV_RANK = 256
ROPE_BASE = 10000.0
HG_HEADS = 4
HG_DIM = 128
HG_W = 512
HG_CHUNK = 64
FFN = 2816
PLE = 256
EPS = 1e-6
ATT_SCALE = QK_DIM ** -0.5
NEG = -1e30

ADAM_LR = 0.001
ADAM_B1 = 0.9
ADAM_B2 = 0.999
ADAM_EPS = 1e-08
ADAM_WD = 0.01
ADAM_STEP = 10

SEC_CQ = (0, 384)
SEC_CKV = (384, 256)
SEC_KR = (640, 128)
SEC_HQ = (768, 512)
SEC_HF = (1280, 512)
SEC_HI = (1792, 512)
SEC_HG = (2304, 512)
SEC_BG = (2816, 2048)
IN_PAD = 4864
SECTIONS = (SEC_CQ, SEC_CKV, SEC_KR, SEC_HQ, SEC_HF, SEC_HI, SEC_HG, SEC_BG)
COL_SECTIONS = ((0, 384), (384, 256), (640, 32), (672, 512), (1184, 512), (1696, 512), (2208, 512), (2720, 2048))
IN_COLS = 4768
IN_BLOCK = IN_COLS // 8

VMEM_LIMIT = 58 * 1024 * 1024
ROW_TILE = 256
ATT_TILE = 1024
ATT_HEADS = 4
HG_BLOCK = 512
HG_UNROLL = 4


def _dot(a, b):
    return jnp.dot(a.astype(MM), b.astype(MM), preferred_element_type=F32)


def _dot_nt(a, b):
    return lax.dot_general(a.astype(MM), b.astype(MM), (((1,), (1,)), ((), ())), preferred_element_type=F32)


def _dot_tn(a, b):
    return lax.dot_general(a.astype(MM), b.astype(MM), (((0,), (0,)), ((), ())), preferred_element_type=F32)


def _dot_hi(a, b):
    return jnp.dot(a, b, preferred_element_type=F32, precision=HI)


def _sigmoid(x):
    return 1.0 / (1.0 + jnp.exp(-x))


def _rms(x, n=None):
    n = x.shape[-1] if n is None else n
    r = lax.rsqrt(jnp.sum(x * x, axis=-1, keepdims=True) * (1.0 / n) + EPS)
    return x * r, r


def _rms_bwd(dxh, xh, r, n=None):
    n = xh.shape[-1] if n is None else n
    return r * (dxh - xh * (jnp.sum(dxh * xh, axis=-1, keepdims=True) * (1.0 / n)))


def _rope_tables(pos, tm):
    lane = lax.broadcasted_iota(jnp.int32, (tm, HEAD_PAD), 1)
    idx = jnp.where(lane < QK_NOPE + QK_ROPE // 2, lane - QK_NOPE, lane - QK_NOPE - QK_ROPE // 2)
    inv = jnp.exp(idx.astype(F32) * (-np.log(ROPE_BASE) * 2.0 / QK_ROPE))
    ang = pos.astype(F32) * inv
    in_rope = (lane >= QK_NOPE) & (lane < QK_DIM)
    first = lane < QK_NOPE + QK_ROPE // 2
    cos_t = jnp.where(in_rope, jnp.cos(ang), 1.0)
    sin_t = jnp.where(in_rope, jnp.where(first, -jnp.sin(ang), jnp.sin(ang)), 0.0)
    return cos_t, sin_t, (first, in_rope)


def _rope_swap(x, halves):
    first, in_rope = halves
    half = QK_ROPE // 2
    return jnp.where(in_rope, jnp.where(first, pltpu.roll(x, HEAD_PAD - half, 1), pltpu.roll(x, half, 1)), 0.0)


def _cparams(sem, vmem=None):
    return pltpu.CompilerParams(dimension_semantics=sem, vmem_limit_bytes=vmem)


def _row_call(name, body, T, tm, row_ins, full_ins, row_outs, acc_outs, vmem=None, scratch=(), xchg=((), ())):
    n_in, n_out, n_x = len(row_ins) + len(full_ins), len(row_outs) + len(acc_outs), _x_count(xchg)
    steps = T // tm

    def kern(*refs):
        ins, x_in, refs = refs[:n_in], refs[n_in:n_in + n_x], refs[n_in + n_x:]
        outs, x_out, refs = refs[:n_out], refs[n_out:n_out + n_x], refs[n_out + n_x:]
        scr, x_sems = refs[:len(scratch)], refs[len(scratch):]
        i = pl.program_id(0)
        if n_x:
            @pl.when(i == 0)
            def _():
                for cp in _x_copies(len(xchg[0]), x_in, x_out, x_sems):
                    cp.start()

        body(i, *ins, *outs, *scr)
        if n_x:
            @pl.when(i == steps - 1)
            def _():
                for cp in _x_copies(len(xchg[0]), x_in, x_out, x_sems):
                    cp.wait()

    any_spec = pl.BlockSpec(memory_space=pl.ANY)
    in_specs = [pl.BlockSpec((tm, a.shape[1]), lambda i: (i, 0)) for a in row_ins]
    in_specs += [pl.BlockSpec(a.shape, lambda i, nd=a.ndim: (0,) * nd, pipeline_mode=pl.Buffered(1)) for a in full_ins]
    out_specs = [pl.BlockSpec((tm, n), lambda i: (i, 0)) for n, _ in row_outs]
    out_specs += [pl.BlockSpec(s, lambda i, nd=len(s): (0,) * nd) for s, _ in acc_outs]
    out_shape = [jax.ShapeDtypeStruct((T, n), dt) for n, dt in row_outs]
    out_shape += [jax.ShapeDtypeStruct(s, dt) for s, dt in acc_outs]
    return pl.pallas_call(
        kern, name=name, grid=(steps,), in_specs=in_specs + [any_spec] * n_x, out_specs=out_specs + [any_spec] * n_x,
        out_shape=out_shape + _x_out_shapes(xchg), scratch_shapes=list(scratch) + _x_sems(xchg),
        compiler_params=_cparams(("arbitrary",), vmem),
    )(*row_ins, *full_ins, *xchg[0], *xchg[1])


FFN_HALVES = (slice(0, FFN // 2), slice(FFN // 2, FFN))
ROW_CHUNK = 16
CHUNK_UNROLL = True


def _by_chunks(tm, fn):
    def step(c, carry):
        fn(pl.ds(pl.multiple_of(c * ROW_CHUNK, ROW_CHUNK), ROW_CHUNK))
        return carry

    lax.fori_loop(0, tm // ROW_CHUNK, step, 0, unroll=CHUNK_UNROLL)


def _fold8(x):
    return x[:8] + x[8:]


def _acc(ref, i, val):
    @pl.when(i == 0)
    def _():
        ref[...] = val

    @pl.when(i != 0)
    def _():
        ref[...] += val


def _in_proj_fwd(x, g_mix, w_in, T, tm):
    def body(i, x_ref, g_ref, w_ref, h_ref, *rest):
        outs, pj_s = rest[:-1], rest[-1]
        g = g_ref[...]

        def norm(rows):
            h_ref[rows, :] = (_rms(x_ref[rows, :])[0] * g).astype(MM)

        _by_chunks(tm, norm)
        for d in range(N_DEV):
            pj_s[d] = _dot_nt(h_ref[...], w_ref[d])

        def join_and_cut(rows):
            proj = jnp.concatenate([pj_s[d, rows, :] for d in range(N_DEV)], axis=1)
            for (s, n), o_ref in zip(COL_SECTIONS, outs):
                if n == QK_ROPE:
                    o_ref[rows, :] = jnp.concatenate(
                        [jnp.zeros((ROW_CHUNK, QK_NOPE), F32), proj[:, s:s + n],
                         jnp.zeros((ROW_CHUNK, HEAD_PAD - QK_DIM), F32)], axis=1)
                else:
                    o_ref[rows, :] = proj[:, s:s + n]

        _by_chunks(tm, join_and_cut)

    row_outs = [(D_MODEL, MM)] + [(n, F32) for _, n in SECTIONS]
    return _row_call("in_proj_fwd", body, T, tm, [x], [g_mix, w_in], row_outs, [], VMEM_LIMIT,
                     scratch=[pltpu.VMEM((N_DEV, tm, IN_BLOCK), F32)])


def _mla_heads_fwd(raw, g_pad, cos_t, sin_t, first):
    outs, saved = [], []
    for h in range(MLA_HEADS):
        xh, r = _rms(raw[:, h * HEAD_PAD:(h + 1) * HEAD_PAD], QK_DIM)
        y = xh * g_pad
        outs.append(y * cos_t + _rope_swap(y, first) * sin_t)
        saved.append((xh, r))
    return outs, saved


def _mla_raw_heads(cqn, ckvn, kr, wuq_ref, wukv_ref, tm):
    lane = lax.broadcasted_iota(jnp.int32, (tm, HEAD_PAD), 1)
    nope = lane < QK_NOPE
    one_lane = jnp.where(lane == V_DIM, 1.0, 0.0)
    qs, ks, vs = [], [], []
    for h in range(MLA_HEADS):
        qs.append(_dot_nt(cqn, wuq_ref[h]))
        kv = _dot(ckvn, wukv_ref[h])
        ks.append(jnp.where(nope, kv, kr))
        vs.append(jnp.where(nope, pltpu.roll(kv, V_DIM, 1), one_lane))
    return jnp.concatenate(qs, axis=1), jnp.concatenate(ks, axis=1), jnp.concatenate(vs, axis=1)


def _mla_prep_fwd(cq, ckv, kr, pos, g_qa, g_kva, g_qn, g_kn, w_uq, w_ukv, T, tm):
    def body(i, cq_ref, ckv_ref, kr_ref, pos_ref, gqa_ref, gkva_ref, gqn_ref, gkn_ref, wuq_ref, wukv_ref,
             q_ref, k_ref, v_ref):
        cos_t, sin_t, first = _rope_tables(pos_ref[...], tm)
        cqn = _rms(cq_ref[...])[0] * gqa_ref[...]
        ckvn = _rms(ckv_ref[...])[0] * gkva_ref[...]
        q_raw, k_raw, v = _mla_raw_heads(cqn, ckvn, kr_ref[...], wuq_ref, wukv_ref, tm)
        qs, _ = _mla_heads_fwd(q_raw, gqn_ref[...], cos_t, sin_t, first)
        ks, _ = _mla_heads_fwd(k_raw, gkn_ref[...], cos_t, sin_t, first)
        q_ref[...] = (jnp.concatenate(qs, axis=1) * ATT_SCALE).astype(MM)
        k_ref[...] = jnp.concatenate(ks, axis=1).astype(MM)
        v_ref[...] = v.astype(MM)

    w = MLA_HEADS * HEAD_PAD
    return _row_call("mla_prep_fwd", body, T, tm, [cq, ckv, kr, pos], [g_qa, g_kva, g_qn, g_kn, w_uq, w_ukv],
                     [(w, MM), (w, MM), (w, MM)], [])


def _causal_pairs(n, by_query):
    if by_query:
        pairs = [(q, k) for q in range(n) for k in range(q + 1)]
    else:
        pairs = [(q, k) for k in range(n) for q in range(k, n)]
    return np.array([p[0] for p in pairs], np.int32), np.array([p[1] for p in pairs], np.int32)


def _flash_fwd(qf, kf, vf, T, ag_blocks=()):
    tq = min(ATT_TILE, T)
    nq = T // tq

    qi_tab, ki_tab = _causal_pairs(nq, by_query=True)

    hp = ATT_HEADS

    n_ag = len(ag_blocks)
    n_heads, n_pairs = MLA_HEADS // hp, len(qi_tab)

    def body(qi_ref, ki_ref, q_ref, k_ref, v_ref, *rest):
        ag_in, (o_ref, lse_ref), rest = rest[:n_ag], rest[n_ag:n_ag + 2], rest[n_ag + 2:]
        ag_out, (m_s, acc_s), ag_sems = rest[:n_ag], rest[n_ag:n_ag + 2], rest[n_ag + 2:]
        t = pl.program_id(1)
        qi, ki = qi_ref[t], ki_ref[t]
        if n_ag:
            @pl.when((pl.program_id(0) == 0) & (t == 0))
            def _():
                _ag_start(ag_in, ag_out, ag_sems)

        @pl.when(ki == 0)
        def _():
            m_s[...] = jnp.full_like(m_s, NEG)
            acc_s[...] = jnp.zeros_like(acc_s)

        def step(masked):
            halves = 2 if masked and tq % (2 * HEAD_PAD) == 0 else 1
            w = tq // halves
            for hh in range(hp):
                hs = slice(hh * HEAD_PAD, (hh + 1) * HEAD_PAD)
                for part in range(halves):
                    cols, nk = slice(part * w, (part + 1) * w), (part + 1) * w
                    s_t = _dot_nt(k_ref[:nk, hs], q_ref[cols, hs])
                    if masked:
                        key = lax.broadcasted_iota(jnp.int32, (nk, w), 0)
                        qry = lax.broadcasted_iota(jnp.int32, (nk, w), 1) + part * w
                        s_t = jnp.where(key <= qry, s_t, NEG)
                    m_old = m_s[hh, :, cols]
                    m_new = jnp.maximum(m_old, jnp.max(s_t, axis=0, keepdims=True))
                    p_t = jnp.exp(s_t - m_new)
                    acc_s[hh, :, cols] = jnp.exp(m_old - m_new) * acc_s[hh, :, cols] + _dot_tn(v_ref[:nk, hs], p_t)
                    m_s[hh, :, cols] = m_new

        @pl.when(ki < qi)
        def _():
            step(False)

        @pl.when(ki == qi)
        def _():
            step(True)
            real = lax.broadcasted_iota(jnp.int32, (HEAD_PAD, tq), 0) < V_DIM
            for hh in range(hp):
                hs = slice(hh * HEAD_PAD, (hh + 1) * HEAD_PAD)
                acc = acc_s[hh]
                l = acc[V_DIM:V_DIM + 1]
                o_ref[:, hs] = jnp.where(real, acc / l, 0.0).T
                lse_ref[:, hs] = jnp.broadcast_to(m_s[hh] + jnp.log(l), (HEAD_PAD, tq)).T

        if n_ag:
            @pl.when((pl.program_id(0) == n_heads - 1) & (t == n_pairs - 1))
            def _():
                _ag_finish(ag_in, ag_out, ag_sems)

    q_spec = pl.BlockSpec((tq, hp * HEAD_PAD), lambda h, t, qi_ref, ki_ref: (qi_ref[t], h))
    kv_spec = pl.BlockSpec((tq, hp * HEAD_PAD), lambda h, t, qi_ref, ki_ref: (ki_ref[t], h))
    any_spec = pl.BlockSpec(memory_space=pl.ANY)
    grid_spec = pltpu.PrefetchScalarGridSpec(
        num_scalar_prefetch=2, grid=(n_heads, n_pairs),
        in_specs=[q_spec, kv_spec, kv_spec] + [any_spec] * n_ag, out_specs=[q_spec, q_spec] + [any_spec] * n_ag,
        scratch_shapes=[pltpu.VMEM((hp, 1, tq), F32), pltpu.VMEM((hp, HEAD_PAD, tq), F32)]
        + (_ag_sems(n_ag) if n_ag else []))
    return pl.pallas_call(
        body, name="flash_fwd", grid_spec=grid_spec,
        out_shape=[jax.ShapeDtypeStruct((T, MLA_HEADS * HEAD_PAD), F32)] * 2 + _ag_out_shapes(ag_blocks),
        compiler_params=_cparams(("arbitrary", "arbitrary")),
    )(jnp.asarray(qi_tab), jnp.asarray(ki_tab), qf, kf, vf, *ag_blocks)


def _hg_gates(hf, lb):
    sg = _sigmoid(hf)
    f = lb + (1.0 - lb) * sg
    return sg, f, jnp.log(f), 1.0 - f


def _tri(n, lower):
    r = lax.broadcasted_iota(jnp.int32, (n, n), 0)
    c = lax.broadcasted_iota(jnp.int32, (n, n), 1)
    return jnp.where((c <= r) if lower else (c >= r), 1.0, 0.0).astype(F32)


def _hg_levels():
    C = HG_CHUNK
    t = lax.broadcasted_iota(jnp.int32, (C, C), 0)
    s = lax.broadcasted_iota(jnp.int32, (C, C), 1)
    levels = []
    for shift in range(C.bit_length() - 2, -1, -1):
        pair_t, pair_s = lax.shift_right_logical(t, shift + 1), lax.shift_right_logical(s, shift + 1)
        later_t = (lax.shift_right_logical(t, shift) & 1) == 1
        earlier_s = (lax.shift_right_logical(s, shift) & 1) == 0
        levels.append((1 << shift, (pair_t == pair_s) & later_t & earlier_s))
    return levels, t == s


def _hg_refs(b):
    C, n = b.shape
    row = lax.broadcasted_iota(jnp.int32, (C, n), 0)
    back1, back2, ahead1 = pltpu.roll(b, 1, 0), pltpu.roll(b, 2, 0), pltpu.roll(b, C - 1, 0)
    refs = []
    for half in (32, 16, 8, 4):
        refs.append(jnp.concatenate(
            [jnp.broadcast_to(b[lo + half - 1:lo + half], (2 * half, n)) for lo in range(0, C, 2 * half)], axis=0))
    in4 = row & 3
    refs.append(jnp.where(in4 == 0, ahead1, jnp.where(in4 == 1, b, jnp.where(in4 == 2, back1, back2))))
    refs.append(jnp.where((row & 1) == 1, back1, b))
    return refs


def _hg_intra(q, k, b, refs, levels, eye):
    a = jnp.where(eye, jnp.sum(q * k, axis=1, keepdims=True), 0.0)
    saved = []
    for r, (_, mask) in zip(refs, levels):
        e = jnp.exp(-jnp.abs(b - r))
        q_t, k_t = q * e, k * e
        a = a + jnp.where(mask, _dot_nt(q_t, k_t), 0.0)
        saved.append((q_t, k_t, e))
    return a, saved


def _hg_intra_bwd(d_a, q, k, saved, levels, eye):
    diag = jnp.sum(jnp.where(eye, d_a, 0.0), axis=1, keepdims=True)
    dq, dk = diag * k, diag * q
    for (q_t, k_t, e), (_, mask) in zip(saved, levels):
        da = jnp.where(mask, d_a, 0.0)
        dq = dq + _dot(da, k_t) * e
        dk = dk + _dot_tn(da, q_t) * e
    return dq, dk


def _hgrn_fwd(hq, hf, hi, lb, T):
    rb = min(HG_BLOCK, T)
    ncb = rb // HG_CHUNK

    def body(hq_ref, hf_ref, hi_ref, lb_ref, o_ref, s0_ref, st_ref):
        @pl.when(pl.program_id(0) == 0)
        def _():
            st_ref[...] = jnp.zeros_like(st_ref)

        tril = _tri(HG_CHUNK, True)
        levels, eye = _hg_levels()

        def chunk(c, carry):
            rows = pl.ds(pl.multiple_of(c * HG_CHUNK, HG_CHUNK), HG_CHUNK)
            _, _, logf, kk = _hg_gates(hf_ref[rows, :], lb_ref[...])
            b = _dot_hi(tril, logf)
            refs = _hg_refs(b)
            q_all, v_all = hq_ref[rows, :], hi_ref[rows, :]
            outs = []
            for h in range(HG_HEADS):
                ls = slice(h * HG_DIM, (h + 1) * HG_DIM)
                q, k, v, bh = q_all[:, ls], kk[:, ls], v_all[:, ls], b[:, ls]
                st = st_ref[h]
                s0_ref[c, h * HG_DIM:(h + 1) * HG_DIM, :] = st
                b_end = bh[HG_CHUNK - 1:HG_CHUNK]
                a, _ = _hg_intra(q, k, bh, [r[:, ls] for r in refs], levels, eye)
                outs.append(_dot_nt(q * jnp.exp(bh), st) + _dot(a, v))
                st_ref[h] = st * jnp.exp(b_end) + _dot_tn(v, k * jnp.exp(b_end - bh))
            o_ref[rows, :] = jnp.concatenate(outs, axis=1)
            return carry

        lax.fori_loop(0, ncb, chunk, 0, unroll=HG_UNROLL)

    row = pl.BlockSpec((rb, HG_W), lambda i: (i, 0))
    return pl.pallas_call(
        body, name="hgrn_fwd", grid=(T // rb,),
        in_specs=[row, row, row, pl.BlockSpec((1, HG_W), lambda i: (0, 0))],
        out_specs=[row, pl.BlockSpec((ncb, HG_W, HG_DIM), lambda i: (i, 0, 0))],
        out_shape=[jax.ShapeDtypeStruct((T, HG_W), F32), jax.ShapeDtypeStruct((T // HG_CHUNK, HG_W, HG_DIM), F32)],
        scratch_shapes=[pltpu.VMEM((HG_HEADS, HG_DIM, HG_DIM), F32)],
        compiler_params=_cparams(("arbitrary",)),
    )(hq, hf, hi, lb)


def _hgrn_bwd(hq, hf, hi, do, s0, lb, T, xchg=((), ())):
    rb = min(HG_BLOCK, T)
    ncb = rb // HG_CHUNK
    nb = T // rb
    C = HG_CHUNK
    n_x, n_sib = _x_count(xchg), len(xchg[0])

    def body(hq_ref, hf_ref, hi_ref, do_ref, s0_ref, lb_ref, *rest):
        x_in, (dq_ref, df_ref, dv_ref, dlb_ref), rest = rest[:n_x], rest[n_x:n_x + 4], rest[n_x + 4:]
        x_out, dst_ref, x_sems = rest[:n_x], rest[n_x], rest[n_x + 1:]

        @pl.when(pl.program_id(0) == 0)
        def _():
            dst_ref[...] = jnp.zeros_like(dst_ref)
            dlb_ref[...] = jnp.zeros_like(dlb_ref)
            for cp in _x_copies(n_sib, x_in, x_out, x_sems):
                cp.start()

        tril, triu = _tri(C, True), _tri(C, False)
        row_cc = lax.broadcasted_iota(jnp.int32, (C, C), 0)
        col_cc = lax.broadcasted_iota(jnp.int32, (C, C), 1)
        last_row = lax.broadcasted_iota(jnp.int32, (C, HG_DIM), 0) == C - 1
        lb_v = lb_ref[...]
        levels, eye = _hg_levels()

        def chunk(cc, carry):
            c = ncb - 1 - cc
            rows = pl.ds(pl.multiple_of(c * C, C), C)
            hf_c = hf_ref[rows, :]
            sg, f, logf, kk = _hg_gates(hf_c, lb_v)
            b = _dot_hi(tril, logf)
            refs = _hg_refs(b)
            q_all, v_all, do_all = hq_ref[rows, :], hi_ref[rows, :], do_ref[rows, :]
            dq_o, dk_o, dv_o, db_o = [], [], [], []
            for h in range(HG_HEADS):
                ls = slice(h * HG_DIM, (h + 1) * HG_DIM)
                q, k, v, bh, d_o = q_all[:, ls], kk[:, ls], v_all[:, ls], b[:, ls], do_all[:, ls]
                st0 = s0_ref[c, h * HG_DIM:(h + 1) * HG_DIM, :]
                dst = dst_ref[h]
                b_end = bh[C - 1:C]
                e_b, e_end = jnp.exp(bh), jnp.exp(b_end)
                e_rem = jnp.exp(b_end - bh)
                qe, kd = q * e_b, k * e_rem
                st_end = st0 * e_end + _dot_tn(v, kd)
                a, saved = _hg_intra(q, k, bh, [r[:, ls] for r in refs], levels, eye)
                d_a = jnp.where(col_cc <= row_cc, _dot_nt(d_o, v), 0.0)
                dq_i, dk_i = _hg_intra_bwd(d_a, q, k, saved, levels, eye)
                dv = _dot_tn(a, d_o) + _dot_nt(kd, dst)
                dq = e_b * _dot(d_o, st0) + dq_i
                dk = e_rem * _dot(v, dst) + dk_i
                extra = jnp.sum(dst * st_end, axis=0, keepdims=True)
                db_o.append(q * dq - k * dk + jnp.where(last_row, extra, 0.0))
                dst_ref[h] = dst * e_end + _dot_tn(d_o, qe)
                dq_o.append(dq)
                dk_o.append(dk)
                dv_o.append(dv)
            dlogf = _dot_hi(triu, jnp.concatenate(db_o, axis=1))
            d_f = dlogf / f - jnp.concatenate(dk_o, axis=1)
            dq_ref[rows, :] = jnp.concatenate(dq_o, axis=1).astype(MM)
            dv_ref[rows, :] = jnp.concatenate(dv_o, axis=1).astype(MM)
            df_ref[rows, :] = (d_f * (1.0 - lb_v) * sg * (1.0 - sg)).astype(MM)
            dlb_ref[...] += jnp.sum(d_f * (1.0 - sg), axis=0, keepdims=True)
            return carry

        lax.fori_loop(0, ncb, chunk, 0, unroll=HG_UNROLL)

        if n_x:
            @pl.when(pl.program_id(0) == nb - 1)
            def _():
                for cp in _x_copies(n_sib, x_in, x_out, x_sems):
                    cp.wait()

    row = pl.BlockSpec((rb, HG_W), lambda i: (nb - 1 - i, 0))
    one = pl.BlockSpec((1, HG_W), lambda i: (0, 0))
    any_spec = pl.BlockSpec(memory_space=pl.ANY)
    return pl.pallas_call(
        body, name="hgrn_bwd", grid=(nb,),
        in_specs=[row, row, row, row, pl.BlockSpec((ncb, HG_W, HG_DIM), lambda i: (nb - 1 - i, 0, 0)), one]
        + [any_spec] * n_x,
        out_specs=[row, row, row, one] + [any_spec] * n_x,
        out_shape=[jax.ShapeDtypeStruct((T, HG_W), MM)] * 3 + [jax.ShapeDtypeStruct((1, HG_W), F32)]
        + _x_out_shapes(xchg),
        scratch_shapes=[pltpu.VMEM((HG_HEADS, HG_DIM, HG_DIM), F32)] + _x_sems(xchg),
        compiler_params=_cparams(("arbitrary",)),
    )(hq, hf, hi, do, s0, lb, *xchg[0], *xchg[1])


def _silu_parts(x):
    sg = _sigmoid(x)
    return x * sg, sg * (1.0 + x * (1.0 - sg))


def _merge_fwd(attn, o, hg, bg, x, g_out, w_bra, w_brb, w_out, T, tm):
    def body(i, attn_ref, o_ref, hg_ref, bg_ref, x_ref, g_ref, wa_ref, wb_ref, wo_ref,
             x1_ref, ya_ref, yb_ref, m_ref, rec_ref):
        g = g_ref[...]

        def recurrent_out(rows):
            for h in range(HG_HEADS):
                ls = slice(h * HG_DIM, (h + 1) * HG_DIM)
                rec_ref[rows, ls] = (_rms(o_ref[rows, ls])[0] * g * _silu_parts(hg_ref[rows, ls])[0]).astype(MM)

        _by_chunks(tm, recurrent_out)
        ya_ref[...] = _dot(attn_ref[...], wa_ref[...])
        yb_ref[...] = jnp.dot(rec_ref[...], wb_ref[...], preferred_element_type=F32)

        def gate(rows):
            m_ref[rows, :] = (_sigmoid(bg_ref[rows, :D_MODEL]) * ya_ref[rows, :]
                              + _sigmoid(bg_ref[rows, D_MODEL:]) * yb_ref[rows, :]).astype(MM)

        _by_chunks(tm, gate)
        x1_ref[...] = x_ref[...] + jnp.dot(m_ref[...], wo_ref[...], preferred_element_type=F32)

    return _row_call("merge_fwd", body, T, tm, [attn, o, hg, bg, x], [g_out, w_bra, w_brb, w_out],
                     [(D_MODEL, F32), (D_MODEL, F32), (D_MODEL, F32), (D_MODEL, MM), (HG_W, MM)], [], VMEM_LIMIT)


def _ffn_fwd(x1, g_ffn, w_g, w_u, w_d, T, tm):
    def body(i, x1_ref, g_ref, wg_ref, wu_ref, wd_ref, x2_ref, gt_ref, up_ref, h2_ref, a_s):
        g = g_ref[...]

        def norm(rows):
            h2_ref[rows, :] = (_rms(x1_ref[rows, :])[0] * g).astype(MM)

        _by_chunks(tm, norm)
        gt_ref[...] = _dot_nt(h2_ref[...], wg_ref[...])
        up_ref[...] = _dot_nt(h2_ref[...], wu_ref[...])

        def act(rows):
            for cs in FFN_HALVES:
                a_s[rows, cs] = (_silu_parts(gt_ref[rows, cs])[0] * up_ref[rows, cs]).astype(MM)

        _by_chunks(tm, act)
        x2_ref[...] = x1_ref[...] + jnp.dot(a_s[...], wd_ref[...], preferred_element_type=F32)

    return _row_call("ffn_fwd", body, T, tm, [x1], [g_ffn, w_g, w_u, w_d],
                     [(D_MODEL, F32), (FFN, F32), (FFN, F32), (D_MODEL, MM)], [], VMEM_LIMIT,
                     scratch=[pltpu.VMEM((tm, FFN), MM)])


def _ple_loss(x2, p, tgt, g_pg, g_post, w_pg, w_pp, T, tm):
    def body(i, x2_ref, p_ref, t_ref, gpg_ref, gpo_ref, wpg_ref, wpp_ref,
             dx2_ref, loss_ref, dgpo_ref, dgpg_ref, dwpg_ref, dwpp_ref, u_s, n3_s, z_s, dz_s, du_s, dy_s, dn3_s):
        @pl.when(i == 0)
        def _():
            for ref in (loss_ref, dgpo_ref, dgpg_ref, dwpg_ref, dwpp_ref):
                ref[...] = jnp.zeros_like(ref)

        gpg, gpo = gpg_ref[...], gpo_ref[...]
        p_mm = p_ref[...].astype(MM)
        for d in range(N_DEV):
            u_s[:, d * HEAD_PAD:(d + 1) * HEAD_PAD] = jnp.dot(p_mm, wpp_ref[d], preferred_element_type=F32)

        def gate_input(rows):
            n3_s[rows, :] = (_rms(x2_ref[rows, :])[0] * gpg).astype(MM)

        _by_chunks(tm, gate_input)
        z_s[...] = jnp.dot(n3_s[...], wpg_ref[...], preferred_element_type=F32)

        def loss_and_back(rows):
            uh, ru = _rms(u_s[rows, :])
            e = uh * gpo
            gate = _sigmoid(z_s[rows, :])
            diff = x2_ref[rows, :] + gate * e - t_ref[rows, :]
            dy = diff * (1.0 / D_MODEL)
            de = dy * gate
            dz_s[rows, :] = (dy * e * gate * (1.0 - gate)).astype(MM)
            du_s[rows, :] = _rms_bwd(de * gpo, uh, ru).astype(MM)
            dy_s[rows, :] = dy
            loss_ref[...] += _fold8(diff * diff) * (0.5 / D_MODEL)
            dgpo_ref[...] += _fold8(de * uh)

        _by_chunks(tm, loss_and_back)
        dn3_s[...] = _dot_nt(dz_s[...], wpg_ref[...])

        def gate_norm_back(rows):
            x2h, r3 = _rms(x2_ref[rows, :])
            dn3 = dn3_s[rows, :]
            dx2_ref[rows, :] = dy_s[rows, :] + _rms_bwd(dn3 * gpg, x2h, r3)
            dgpg_ref[...] += _fold8(dn3 * x2h)

        _by_chunks(tm, gate_norm_back)
        dwpg_ref[...] += _dot_tn(n3_s[...], dz_s[...])
        for d in range(N_DEV):
            dwpp_ref[d] += _dot_tn(p_mm, du_s[:, d * HEAD_PAD:(d + 1) * HEAD_PAD])

    vec = ((8, D_MODEL), F32)
    wide = lambda dt: pltpu.VMEM((tm, D_MODEL), dt)
    return _row_call("ple_loss", body, T, tm, [x2, p, tgt], [g_pg, g_post, w_pg, w_pp], [(D_MODEL, F32)],
                     [vec, vec, vec, ((D_MODEL, D_MODEL), F32), ((N_DEV, PLE, HEAD_PAD), F32)], VMEM_LIMIT,
                     scratch=[wide(F32), wide(MM), wide(F32), wide(MM), wide(MM), wide(F32), wide(F32)])


def _ffn_bwd(dx2, x1, gt, up, g_ffn, w_g, w_u, w_d, T, tm):
    def body(i, dx2_ref, x1_ref, gt_ref, up_ref, g_ref, wg_ref, wu_ref, wd_ref,
             dx1_ref, a_ref, dgt_ref, dup_ref, dg_ref, da_s, dh2_s):
        @pl.when(i == 0)
        def _():
            dg_ref[...] = jnp.zeros_like(dg_ref)

        g = g_ref[...]
        da_s[...] = _dot_nt(dx2_ref[...], wd_ref[...])

        def act_back(rows):
            for cs in FFN_HALVES:
                up, da = up_ref[rows, cs], da_s[rows, cs]
                silu, dsilu = _silu_parts(gt_ref[rows, cs])
                dgt_ref[rows, cs] = (da * up * dsilu).astype(MM)
                dup_ref[rows, cs] = (da * silu).astype(MM)
                a_ref[rows, cs] = (silu * up).astype(MM)

        _by_chunks(tm, act_back)
        dh2_s[...] = (jnp.dot(dgt_ref[...], wg_ref[...], preferred_element_type=F32)
                      + jnp.dot(dup_ref[...], wu_ref[...], preferred_element_type=F32))

        def norm_back(rows):
            x1h, r = _rms(x1_ref[rows, :])
            dh2 = dh2_s[rows, :]
            dx1_ref[rows, :] = dx2_ref[rows, :] + _rms_bwd(dh2 * g, x1h, r)
            dg_ref[...] += _fold8(dh2 * x1h)

        _by_chunks(tm, norm_back)

    return _row_call("ffn_bwd", body, T, tm, [dx2, x1, gt, up], [g_ffn, w_g, w_u, w_d],
                     [(D_MODEL, F32), (FFN, MM), (FFN, MM), (FFN, MM)], [((8, D_MODEL), F32)], VMEM_LIMIT,
                     scratch=[pltpu.VMEM((tm, FFN), F32), pltpu.VMEM((tm, D_MODEL), F32)])


def _merge_bwd(dx1, ya, yb, bg, o, hg, attn, m, rec, g_out, w_bra, w_brb, w_out, T, tm, xchg=((), ())):
    def body(i, dx1_ref, ya_ref, yb_ref, bg_ref, o_ref, hg_ref, attn_ref, m_ref, rec_ref, g_ref, wa_ref, wb_ref, wo_ref,
             dattn_ref, do_ref, dhg_ref, dbg_ref, dg_ref, dwo_ref, dwa_ref, dwb_ref, dm_s, dya_s, dyb_s, drec_s):
        @pl.when(i == 0)
        def _():
            for ref in (dg_ref, dwo_ref, dwa_ref, dwb_ref):
                ref[...] = jnp.zeros_like(ref)

        g = g_ref[...]
        dx1 = dx1_ref[...].astype(MM)
        dm_s[...] = _dot_nt(dx1, wo_ref[...])

        def gate_back(rows):
            dm = dm_s[rows, :]
            ga, gb = _sigmoid(bg_ref[rows, :D_MODEL]), _sigmoid(bg_ref[rows, D_MODEL:])
            dya_s[rows, :] = (dm * ga).astype(MM)
            dyb_s[rows, :] = (dm * gb).astype(MM)
            dbg_ref[rows, :D_MODEL] = (dm * ya_ref[rows, :] * ga * (1.0 - ga)).astype(MM)
            dbg_ref[rows, D_MODEL:] = (dm * yb_ref[rows, :] * gb * (1.0 - gb)).astype(MM)

        _by_chunks(tm, gate_back)
        dwo_ref[...] += _dot_tn(m_ref[...], dx1)
        attn_mm = attn_ref[...].astype(MM)
        for d in range(N_DEV):
            ds = slice(d * HEAD_PAD, (d + 1) * HEAD_PAD)
            dwa_ref[d] += _dot_tn(attn_mm, dya_s[:, ds])
            dwb_ref[d] += _dot_tn(rec_ref[...], dyb_s[:, ds])
        dattn_ref[...] = _dot_nt(dya_s[...], wa_ref[...])
        drec_s[...] = _dot_nt(dyb_s[...], wb_ref[...])

        def recurrent_out_back(rows):
            for h in range(HG_HEADS):
                ls = slice(h * HG_DIM, (h + 1) * HG_DIM)
                oh, r = _rms(o_ref[rows, ls])
                silu, dsilu = _silu_parts(hg_ref[rows, ls])
                dr = drec_s[rows, ls]
                dhg_ref[rows, ls] = (dr * oh * g * dsilu).astype(MM)
                don = dr * silu
                dg_ref[...] += _fold8(don * oh)
                do_ref[rows, ls] = _rms_bwd(don * g, oh, r)

        _by_chunks(tm, recurrent_out_back)

    wide = lambda n, dt: pltpu.VMEM((tm, n), dt)
    return _row_call("merge_bwd", body, T, tm, [dx1, ya, yb, bg, o, hg, attn, m, rec], [g_out, w_bra, w_brb, w_out],
                     [(D_MODEL, F32), (HG_W, F32), (HG_W, MM), (2 * D_MODEL, MM)],
                     [((8, HG_DIM), F32), ((D_MODEL, D_MODEL), F32), ((N_DEV, MLA_HEADS * HEAD_PAD, HEAD_PAD), F32),
                      ((N_DEV, HG_W, HEAD_PAD), F32)], VMEM_LIMIT,
                     scratch=[wide(D_MODEL, F32), wide(D_MODEL, MM), wide(D_MODEL, MM), wide(HG_W, F32)], xchg=xchg)


def _flash_bwd(qf, kf, vf, o, do, lse, T, xchg=((), ())):
    tq = min(ATT_TILE, T)
    nq = T // tq

    qi_tab, ki_tab = _causal_pairs(nq, by_query=False)

    n_x, n_sib = _x_count(xchg), len(xchg[0])
    hp = ATT_HEADS
    n_heads, n_pairs = MLA_HEADS // hp, len(qi_tab)

    def body(qi_ref, ki_ref, q_ref, k_ref, v_ref, o_ref, do_ref, lse_ref, *rest):
        x_in, (dq_ref, dk_ref, dv_ref), rest = rest[:n_x], rest[n_x:n_x + 3], rest[n_x + 3:]
        x_out, x_sems = rest[:n_x], rest[n_x:]
        t = pl.program_id(1)
        qi, ki = qi_ref[t], ki_ref[t]
        if n_x:
            @pl.when((pl.program_id(0) == 0) & (t == 0))
            def _():
                for cp in _x_copies(n_sib, x_in, x_out, x_sems):
                    cp.start()

        @pl.when(t == 0)
        def _():
            dq_ref[...] = jnp.zeros_like(dq_ref)

        def step(first):
            halves = 2 if first and tq % (2 * HEAD_PAD) == 0 else 1
            w = tq // halves
            for hh in range(hp):
                hs = slice(hh * HEAD_PAD, (hh + 1) * HEAD_PAD)
                for part in range(halves):
                    keys, qs = slice(part * w, (part + 1) * w), slice(part * w, tq)
                    nq_ = tq - part * w
                    q, k, d_o = q_ref[qs, hs], k_ref[keys, hs], do_ref[qs, hs]
                    s = _dot_nt(q, k)
                    if first:
                        row = lax.broadcasted_iota(jnp.int32, (nq_, w), 0)
                        col = lax.broadcasted_iota(jnp.int32, (nq_, w), 1)
                        s = jnp.where(col <= row, s, NEG)
                    p = jnp.exp(s - lse_ref[qs, hh * HEAD_PAD:hh * HEAD_PAD + 1])
                    delta = jnp.sum(d_o * o_ref[qs, hs], axis=1, keepdims=True)
                    ds = p * (_dot_nt(d_o, v_ref[keys, hs]) - delta)
                    rows = pl.ds(pl.multiple_of(qi * tq + part * w, w), nq_)
                    dq_ref[rows, hs] += _dot(ds, k)
                    if first:
                        dv_ref[keys, hs] = _dot_tn(p, d_o)
                        dk_ref[keys, hs] = _dot_tn(ds, q)
                    else:
                        dv_ref[keys, hs] += _dot_tn(p, d_o)
                        dk_ref[keys, hs] += _dot_tn(ds, q)

        @pl.when(qi == ki)
        def _():
            step(True)

        @pl.when(qi > ki)
        def _():
            step(False)

        if n_x:
            @pl.when((pl.program_id(0) == n_heads - 1) & (t == n_pairs - 1))
            def _():
                for cp in _x_copies(n_sib, x_in, x_out, x_sems):
                    cp.wait()

    q_spec = pl.BlockSpec((tq, hp * HEAD_PAD), lambda h, t, qi_ref, ki_ref: (qi_ref[t], h))
    kv_spec = pl.BlockSpec((tq, hp * HEAD_PAD), lambda h, t, qi_ref, ki_ref: (ki_ref[t], h))
    any_spec = pl.BlockSpec(memory_space=pl.ANY)
    w = MLA_HEADS * HEAD_PAD
    grid_spec = pltpu.PrefetchScalarGridSpec(
        num_scalar_prefetch=2, grid=(n_heads, n_pairs),
        in_specs=[q_spec, kv_spec, kv_spec, q_spec, q_spec, q_spec] + [any_spec] * n_x,
        out_specs=[pl.BlockSpec((T, hp * HEAD_PAD), lambda h, t, qi_ref, ki_ref: (0, h)), kv_spec, kv_spec]
        + [any_spec] * n_x,
        scratch_shapes=_x_sems(xchg))
    return pl.pallas_call(
        body, name="flash_bwd", grid_spec=grid_spec,
        out_shape=[jax.ShapeDtypeStruct((T, w), F32)] * 3 + _x_out_shapes(xchg),
        compiler_params=_cparams(("arbitrary", "arbitrary")),
    )(jnp.asarray(qi_tab), jnp.asarray(ki_tab), qf, kf, vf, o, do, lse, *xchg[0], *xchg[1])


def _mla_heads_bwd(d_out, saved, g_pad, cos_t, sin_t, first):
    d_raw, dg = [], jnp.zeros((1, HEAD_PAD), F32)
    for h in range(MLA_HEADS):
        xh, r = saved[h]
        dy = d_out[:, h * HEAD_PAD:(h + 1) * HEAD_PAD]
        dn = dy * cos_t + _rope_swap(dy * sin_t, first)
        dg = dg + jnp.sum(dn * xh, axis=0, keepdims=True)
        d_raw.append(_rms_bwd(dn * g_pad, xh, r, QK_DIM))
    return d_raw, dg


def _mla_prep_bwd(cq, ckv, kr, pos, dqf, dkf, dvf, g_qa, g_kva, g_qn, g_kn, w_uq, w_ukv, T, tm):
    def body(i, cq_ref, ckv_ref, kr_ref, pos_ref, dq_ref, dk_ref, dv_ref,
             gqa_ref, gkva_ref, gqn_ref, gkn_ref, wuq_ref, wukv_ref,
             dcq_ref, dckv_ref, dkr_ref, dgqa_ref, dgkva_ref, dgqn_ref, dgkn_ref, dwuq_ref, dwukv_ref):
        cos_t, sin_t, first = _rope_tables(pos_ref[...], tm)
        cqh, rq = _rms(cq_ref[...])
        ckvh, rkv = _rms(ckv_ref[...])
        cqn, ckvn = cqh * gqa_ref[...], ckvh * gkva_ref[...]
        q_raw, k_raw, _ = _mla_raw_heads(cqn, ckvn, kr_ref[...], wuq_ref, wukv_ref, tm)
        _, q_saved = _mla_heads_fwd(q_raw, gqn_ref[...], cos_t, sin_t, first)
        _, k_saved = _mla_heads_fwd(k_raw, gkn_ref[...], cos_t, sin_t, first)
        dq_heads, dgqn = _mla_heads_bwd(dq_ref[...] * ATT_SCALE, q_saved, gqn_ref[...], cos_t, sin_t, first)
        dk_heads, dgkn = _mla_heads_bwd(dk_ref[...], k_saved, gkn_ref[...], cos_t, sin_t, first)
        lane = lax.broadcasted_iota(jnp.int32, (tm, HEAD_PAD), 1)
        nope = lane < QK_NOPE
        dcqn = jnp.zeros((tm, Q_RANK), F32)
        dckvn = jnp.zeros((tm, KV_RANK), F32)
        dkr = jnp.zeros((tm, HEAD_PAD), F32)
        cqn_mm, ckvn_mm = cqn.astype(MM), ckvn.astype(MM)
        for h in range(MLA_HEADS):
            hs = slice(h * HEAD_PAD, (h + 1) * HEAD_PAD)
            dq_h = dq_heads[h].astype(MM)
            dkv_h = jnp.where(nope, dk_heads[h], pltpu.roll(dv_ref[:, hs], V_DIM, 1)).astype(MM)
            _acc(dwuq_ref.at[h], i, _dot_tn(dq_h, cqn_mm))
            _acc(dwukv_ref.at[h], i, _dot_tn(ckvn_mm, dkv_h))
            dcqn = dcqn + jnp.dot(dq_h, wuq_ref[h], preferred_element_type=F32)
            dckvn = dckvn + lax.dot_general(dkv_h, wukv_ref[h], (((1,), (1,)), ((), ())), preferred_element_type=F32)
            dkr = dkr + dk_heads[h]
        dkr_ref[...] = jnp.where((lane >= QK_NOPE) & (lane < QK_DIM), dkr, 0.0).astype(MM)
        dcq_ref[...] = _rms_bwd(dcqn * gqa_ref[...], cqh, rq).astype(MM)
        dckv_ref[...] = _rms_bwd(dckvn * gkva_ref[...], ckvh, rkv).astype(MM)
        _acc(dgqa_ref, i, jnp.sum(dcqn * cqh, axis=0, keepdims=True))
        _acc(dgkva_ref, i, jnp.sum(dckvn * ckvh, axis=0, keepdims=True))
        _acc(dgqn_ref, i, dgqn)
        _acc(dgkn_ref, i, dgkn)

    return _row_call(
        "mla_prep_bwd", body, T, tm, [cq, ckv, kr, pos, dqf, dkf, dvf], [g_qa, g_kva, g_qn, g_kn, w_uq, w_ukv],
        [(Q_RANK, MM), (KV_RANK, MM), (HEAD_PAD, MM)],
        [((1, Q_RANK), F32), ((1, KV_RANK), F32), ((1, HEAD_PAD), F32), ((1, HEAD_PAD), F32),
         ((MLA_HEADS, HEAD_PAD, Q_RANK), F32), ((MLA_HEADS, KV_RANK, HEAD_PAD), F32)], VMEM_LIMIT)


def _in_proj_bwd(x, dx1, dsecs, g_mix, w_in, T, tm):
    def body(i, x_ref, dx1_ref, *rest):
        d_refs, (g_ref, w_ref, dx_ref, dp_ref, dg_ref, dh_s) = rest[:len(SECTIONS)], rest[len(SECTIONS):]

        @pl.when(i == 0)
        def _():
            dg_ref[...] = jnp.zeros_like(dg_ref)

        g = g_ref[...]

        def join_and_cut(rows):
            pieces = [(d_ref[rows, QK_NOPE:QK_DIM] if n == QK_ROPE else d_ref[rows, :]).astype(F32)
                      for (_, n), d_ref in zip(COL_SECTIONS, d_refs)]
            dproj = jnp.concatenate(pieces, axis=1)
            for d in range(N_DEV):
                dp_ref[d, rows, :] = dproj[:, d * IN_BLOCK:(d + 1) * IN_BLOCK].astype(MM)

        _by_chunks(tm, join_and_cut)
        dh = jnp.dot(dp_ref[0], w_ref[0], preferred_element_type=F32)
        for d in range(1, N_DEV):
            dh = dh + jnp.dot(dp_ref[d], w_ref[d], preferred_element_type=F32)
        dh_s[...] = dh

        def norm_back(rows):
            xh, r = _rms(x_ref[rows, :])
            dh_c = dh_s[rows, :]
            dx_ref[rows, :] = dx1_ref[rows, :] + _rms_bwd(dh_c * g, xh, r)
            dg_ref[...] += _fold8(dh_c * xh)

        _by_chunks(tm, norm_back)

    in_specs = [pl.BlockSpec((tm, a.shape[1]), lambda i: (i, 0)) for a in [x, dx1, *dsecs]]
    in_specs += [pl.BlockSpec(g_mix.shape, lambda i: (0, 0)),
                 pl.BlockSpec(w_in.shape, lambda i: (0, 0, 0), pipeline_mode=pl.Buffered(1))]

    def kern(*refs):
        body(pl.program_id(0), *refs)

    return pl.pallas_call(
        kern, name="in_proj_bwd", grid=(T // tm,), in_specs=in_specs,
        out_specs=[pl.BlockSpec((tm, D_MODEL), lambda i: (i, 0)),
                   pl.BlockSpec((N_DEV, tm, IN_BLOCK), lambda i: (0, i, 0)),
                   pl.BlockSpec((8, D_MODEL), lambda i: (0, 0))],
        out_shape=[jax.ShapeDtypeStruct((T, D_MODEL), F32), jax.ShapeDtypeStruct((N_DEV, T, IN_BLOCK), MM),
                   jax.ShapeDtypeStruct((8, D_MODEL), F32)],
        scratch_shapes=[pltpu.VMEM((tm, D_MODEL), F32)],
        compiler_params=_cparams(("arbitrary",), VMEM_LIMIT),
    )(x, dx1, *dsecs, g_mix, w_in)


def _pick_block(n, cap):
    best = None
    for cand in range(128, min(n, cap) + 1, 128):
        if n % cand == 0:
            best = cand
    return n if best is None else best


def _pick_rows(n, cap):
    best = n
    for cand in range(8, min(n, cap) + 1, 8):
        if n % cand == 0:
            best = cand
    return best


def _matmul_tn(name, a, b):
    T, M = a.shape
    N = b.shape[1]
    bm, bk = _pick_block(M, 1408), min(512, T)
    bn = _pick_block(N, 2560)

    def body(a_ref, b_ref, c_ref):
        @pl.when(pl.program_id(2) == 0)
        def _():
            c_ref[...] = jnp.zeros_like(c_ref)

        c_ref[...] += _dot_tn(a_ref[...], b_ref[...])

    return pl.pallas_call(
        body, name=name, grid=(M // bm, N // bn, T // bk),
        in_specs=[pl.BlockSpec((bk, bm), lambda i, j, k: (k, i)), pl.BlockSpec((bk, bn), lambda i, j, k: (k, j))],
        out_specs=pl.BlockSpec((bm, bn), lambda i, j, k: (i, j)), out_shape=jax.ShapeDtypeStruct((M, N), F32),
        compiler_params=_cparams(("parallel", "parallel", "arbitrary"), VMEM_LIMIT),
    )(a, b)


def _matmul_tn_blocks(name, a, b):
    T, M = a.shape
    nd, _, c = b.shape
    bm, bk = _pick_block(M, 512), min(512, T)

    def body(a_ref, b_ref, c_ref):
        @pl.when(pl.program_id(1) == 0)
        def _():
            c_ref[...] = jnp.zeros_like(c_ref)

        a_blk = a_ref[...].astype(MM)
        for d in range(nd):
            c_ref[d] += _dot_tn(b_ref[d], a_blk)

    return pl.pallas_call(
        body, name=name, grid=(M // bm, T // bk),
        in_specs=[pl.BlockSpec((bk, bm), lambda i, k: (k, i)), pl.BlockSpec((nd, bk, c), lambda i, k: (0, k, 0))],
        out_specs=pl.BlockSpec((nd, c, bm), lambda i, k: (0, 0, i)),
        out_shape=jax.ShapeDtypeStruct((nd, c, M), F32),
        compiler_params=_cparams(("parallel", "arbitrary"), VMEM_LIMIT),
    )(a, b)


def _pad_gain(g, n):
    return jnp.pad(g.reshape(1, -1), ((0, 0), (0, n - g.shape[-1])))


GROUP_A = ("w_ffn_gate", "w_ffn_up", "w_ffn_down", "w_ple_gate", "w_ple_proj")
GROUP_B = ("w_branch", "w_out")
GROUP_C = ("w_in", "w_uq", "w_ukv")
EARLY = GROUP_C
LATE = GROUP_B + GROUP_A
TRANSPOSED = ("w_in", "w_uq", "w_ffn_gate", "w_ffn_up")


def _local_step(x, p, pos, tgt, small, big, late_blocks=None, core=None):
    T = x.shape[0]
    tm = min(ROW_TILE, T)
    w_in = big["w_in"]
    w_uq = jnp.pad(big["w_uq"], ((0, 0), (0, HEAD_PAD - QK_DIM), (0, 0)))
    w_ukv = big["w_ukv"]

    g_mix, g_qa, g_kva = small["mix_norm_g"], small["q_a_norm_g"], small["kv_a_norm_g"]
    g_qn, g_kn = _pad_gain(small["q_norm_g"], HEAD_PAD), _pad_gain(small["k_norm_g"], HEAD_PAD)
    g_out, g_ffn = small["hg_out_norm_g"], small["ffn_norm_g"]
    g_pg, g_post = small["ple_gate_norm_g"], small["ple_post_norm_g"]
    logits = small["hg_lb_logits"]
    lb = _lower_bound(logits)

    h, cq, ckv, kr, hq, hf, hi, hg, bg = _in_proj_fwd(x, g_mix, w_in, T, tm)
    qf, kf, vf = _mla_prep_fwd(cq, ckv, kr, pos, g_qa, g_kva, g_qn, g_kn, w_uq, w_ukv, T, tm)
    if late_blocks is None:
        attn, lse = _flash_fwd(qf, kf, vf, T)
    else:
        attn, lse, *late = _flash_fwd(qf, kf, vf, T, ag_blocks=[late_blocks[n] for n in LATE])
        big = {**big, **dict(zip(LATE, late))}
    w_branch = jnp.moveaxis(big["w_branch"].reshape(N_DEV, 2, HG_W, HEAD_PAD), 0, 2).reshape(2, HG_W, D_MODEL)
    w_bra = jnp.pad(w_branch[0].reshape(MLA_HEADS, V_DIM, D_MODEL),
                    ((0, 0), (0, HEAD_PAD - V_DIM), (0, 0))).reshape(MLA_HEADS * HEAD_PAD, D_MODEL)
    w_brb = w_branch[1]
    w_out = big["w_out"].reshape(D_MODEL, D_MODEL)
    w_g, w_u = big["w_ffn_gate"].reshape(FFN, D_MODEL), big["w_ffn_up"].reshape(FFN, D_MODEL)
    w_d = big["w_ffn_down"].reshape(FFN, D_MODEL)
    w_pg, w_pp = big["w_ple_gate"].reshape(D_MODEL, D_MODEL), big["w_ple_proj"]
    o, s0 = _hgrn_fwd(hq, hf, hi, lb, T)
    x1, ya, yb, m, rec = _merge_fwd(attn, o, hg, bg, x, g_out, w_bra, w_brb, w_out, T, tm)
    x2, gt, up, h2 = _ffn_fwd(x1, g_ffn, w_g, w_u, w_d, T, tm)
    dx2, loss_p, dg_post, dg_pg, d_pg, d_pp = _ple_loss(x2, p, tgt, g_pg, g_post, w_pg, w_pp, T, tm)

    grads, sibs, gots = {}, {}, {}
    dist = core is not None
    pick = lambda names: [grads[n] for n in names] if dist else ()

    def partials(tag, names, got):
        if not dist:
            return ()
        sibs.update(zip(names, got))
        return _chip_partials("rs_partial_" + tag, pick(names), got, core)

    dx1, a, dgt, dup, dg_ffn = _ffn_bwd(dx2, x1, gt, up, g_ffn, w_g, w_u, w_d, T, tm)
    grads["w_ffn_gate"] = _matmul_tn("dw_gate", dgt, h2).reshape(N_DEV, -1, D_MODEL)
    grads["w_ffn_up"] = _matmul_tn("dw_up", dup, h2).reshape(N_DEV, -1, D_MODEL)
    grads["w_ffn_down"] = _matmul_tn("dw_down", a, dx2).reshape(N_DEV, -1, D_MODEL)
    grads["w_ple_gate"] = d_pg.reshape(N_DEV, -1, D_MODEL)
    grads["w_ple_proj"] = d_pp

    dattn, do, dhg, dbg, dg_out, d_out, d_bra, d_brb, *sib_a = _merge_bwd(
        dx1, ya, yb, bg, o, hg, attn, m, rec, g_out, w_bra, w_brb, w_out, T, tm, xchg=(pick(GROUP_A), ()))
    parts_a = partials("a", GROUP_A, sib_a)
    d_bra = d_bra.reshape(N_DEV, MLA_HEADS, HEAD_PAD, HEAD_PAD)[:, :, :V_DIM].reshape(N_DEV, HG_W, HEAD_PAD)
    grads["w_branch"] = jnp.concatenate([d_bra, d_brb], axis=1)
    grads["w_out"] = d_out.reshape(N_DEV, -1, D_MODEL)

    dhq, dhf, dhi, dlb, *got = _hgrn_bwd(hq, hf, hi, do, s0, lb, T, xchg=(pick(GROUP_B), parts_a))
    sib_b, got_a = got[:len(GROUP_B)], got[len(GROUP_B):]
    parts_b = partials("b", GROUP_B, sib_b)
    dqf, dkf, dvf, *got_b = _flash_bwd(qf, kf, vf, attn, dattn, lse, T, xchg=((), parts_b))
    (dcq, dckv, dkr, dg_qa, dg_kva, dg_qn, dg_kn, d_uq, d_ukv) = _mla_prep_bwd(
        cq, ckv, kr, pos, dqf, dkf, dvf, g_qa, g_kva, g_qn, g_kn, w_uq, w_ukv, T, tm)
    grad_x, dproj, dg_mix = _in_proj_bwd(x, dx1, [dcq, dckv, dkr, dhq, dhf, dhi, dhg, dbg], g_mix, w_in, T, tm)
    grads["w_in"] = _matmul_tn_blocks("dw_in", h, dproj)
    grads["w_uq"] = d_uq[:, :QK_DIM]
    grads["w_ukv"] = d_ukv
    parts_c = ()
    if dist:
        parts_c = partials("c", GROUP_C, _exchange_sibling("rs_sibling_c", pick(GROUP_C)))
        gots.update(zip(GROUP_A, got_a))
        gots.update(zip(GROUP_B, got_b))

    dl0 = dlb * lb * (1.0 - lb)
    small_g = {
        "mix_norm_g": dg_mix, "q_a_norm_g": dg_qa, "kv_a_norm_g": dg_kva, "q_norm_g": dg_qn, "k_norm_g": dg_kn,
        "hg_lb_logits": jnp.concatenate([dl0, -dl0], axis=0), "hg_out_norm_g": dg_out,
        "ffn_norm_g": dg_ffn, "ple_gate_norm_g": dg_pg, "ple_post_norm_g": dg_post,
    }
    return loss_p, grad_x, small_g, grads, sibs, gots, parts_c


def _lower_bound(logits):
    def body(l_ref, lb_ref):
        l = l_ref[...]
        mx = jnp.max(l, axis=0, keepdims=True)
        e = jnp.exp(l - mx)
        lb_ref[...] = e[0:1] / jnp.sum(e, axis=0, keepdims=True)

    return pl.pallas_call(body, name="lower_bound", out_shape=jax.ShapeDtypeStruct((1, HG_W), F32))(logits)


def _my_place():
    return lax.axis_index("x"), lax.axis_index("y"), lax.axis_index("c")


def _all_gather(name, blocks):
    n = len(blocks)

    def body(*refs):
        x_refs, out_refs, sems = refs[:n], refs[n:2 * n], refs[2 * n:]
        _ag_start(x_refs, out_refs, sems)
        _ag_finish(x_refs, out_refs, sems)

    any_spec = pl.BlockSpec(memory_space=pl.ANY)
    return pl.pallas_call(
        body, name=name, out_shape=_ag_out_shapes(blocks),
        in_specs=[any_spec] * n, out_specs=[any_spec] * n, scratch_shapes=_ag_sems(n),
    )(*blocks)


def _ag_out_shapes(blocks):
    return [jax.ShapeDtypeStruct((N_DEV,) + b.shape, b.dtype) for b in blocks]


def _ag_sems(n):
    return [pltpu.SemaphoreType.DMA((7 * n,)), pltpu.SemaphoreType.DMA((7 * n,)), pltpu.SemaphoreType.DMA((n,))]


def _ag_parts(x_refs, out_refs, sems):
    send_sems, recv_sems, local_sems = sems
    x, y, c = _my_place()
    me, sibling = (x, y, c), (x, y, 1 - c)
    chips = [(1 - x, y), (x, 1 - y), (1 - x, 1 - y)]
    n = len(x_refs)

    def copy(a, k, block, to, own=False):
        px, py, pc = block
        dst = out_refs[a].at[4 * px + 2 * py + pc]
        return pltpu.make_async_remote_copy(
            src_ref=x_refs[a] if own else dst, dst_ref=dst, send_sem=send_sems.at[7 * a + k],
            recv_sem=recv_sems.at[7 * a + k], device_id=to, device_id_type=MESH_ID)

    mine = [pltpu.make_async_copy(x_refs[a], out_refs[a].at[4 * x + 2 * y + c], local_sems.at[a]) for a in range(n)]
    first = []
    for a in range(n):
        first.append(copy(a, 0, me, sibling, own=True))
        first += [copy(a, 1 + j, me, (*chip, c), own=True) for j, chip in enumerate(chips)]
    return copy, mine, first, me, sibling, chips, c, n


def _ag_start(x_refs, out_refs, sems):
    _, mine, first, *_ = _ag_parts(x_refs, out_refs, sems)
    for cp in mine + first:
        cp.start()


def _ag_finish(x_refs, out_refs, sems):
    copy, mine, first, me, sibling, chips, c, n = _ag_parts(x_refs, out_refs, sems)
    passed = []
    for j, chip in enumerate(chips):
        for a in range(n):
            copy(a, 1 + j, (*chip, c), me).wait_recv()
            passed.append(copy(a, 4 + j, (*chip, c), sibling))
            passed[-1].start()
    for a in range(n):
        copy(a, 0, sibling, me).wait_recv()
    for j, chip in enumerate(chips):
        for a in range(n):
            copy(a, 4 + j, (*chip, 1 - c), me).wait_recv()
    for cp in first + passed:
        cp.wait_send()
    for cp in mine:
        cp.wait()


def _exchange_sibling(name, gs):
    return _exchange(name, (gs, ()))


def _exchange(name, xchg):
    n = _x_count(xchg)

    def body(*refs):
        in_refs, out_refs, sems = refs[:n], refs[n:2 * n], refs[2 * n:]
        for cp in _x_copies(len(xchg[0]), in_refs, out_refs, sems):
            cp.start()
        for cp in _x_copies(len(xchg[0]), in_refs, out_refs, sems):
            cp.wait()

    any_spec = pl.BlockSpec(memory_space=pl.ANY)
    return pl.pallas_call(
        body, name=name, out_shape=_x_out_shapes(xchg), in_specs=[any_spec] * n, out_specs=[any_spec] * n,
        scratch_shapes=_x_sems(xchg),
    )(*xchg[0], *xchg[1])


N_PARTS = 4


def _part_spec(rows, cols, t_pos, lead_block=(), lead_index=lambda *args: ()):
    if rows % (16 * N_PARTS) == 0:
        axis, shape, count = 0, (rows // N_PARTS, cols), N_PARTS
    elif cols % (128 * N_PARTS) == 0:
        axis, shape, count = 1, (rows, cols // N_PARTS), N_PARTS
    else:
        axis, shape, count = 0, (rows, cols), 1

    def index(*args):
        i = jnp.minimum(args[t_pos], count - 1)
        return (*lead_index(*args), *((i, 0) if axis == 0 else (0, i)))

    return pl.BlockSpec((*lead_block, *shape), index)


def _chip_partials(name, gs, sibs, c_idx):
    n = len(gs)

    def body(c_ref, *refs):
        for g_ref, sib_ref, out_ref in zip(refs[:n], refs[n:2 * n], refs[2 * n:]):
            out_ref[...] = (g_ref[...] + sib_ref[...]).astype(MM)

    own = [_part_spec(*g.shape[1:], 1, (1,), lambda j, t, c_ref: (2 * j + c_ref[0],)) for g in gs]
    by_chip = [_part_spec(*g.shape[1:], 1, (1,), lambda j, t, c_ref: (j,)) for g in gs]
    grid_spec = pltpu.PrefetchScalarGridSpec(
        num_scalar_prefetch=1, grid=(4, N_PARTS), in_specs=own + by_chip, out_specs=by_chip)
    return pl.pallas_call(
        body, name=name, grid_spec=grid_spec, out_shape=[jax.ShapeDtypeStruct((4,) + g.shape[1:], MM) for g in gs],
        compiler_params=_cparams(("arbitrary", "arbitrary"), VMEM_LIMIT),
    )(c_idx, *gs, *sibs)


def _exchange_chips(parts):
    return _exchange("rs_chips", ((), parts))


def _x_count(xchg):
    return len(xchg[0]) + len(xchg[1])


def _x_out_shapes(xchg):
    return ([jax.ShapeDtypeStruct((4,) + g.shape[1:], g.dtype) for g in xchg[0]]
            + [jax.ShapeDtypeStruct((3,) + p.shape[1:], p.dtype) for p in xchg[1]])


def _x_sems(xchg):
    n = 4 * len(xchg[0]) + 3 * len(xchg[1])
    return [pltpu.SemaphoreType.DMA((n,)), pltpu.SemaphoreType.DMA((n,))] if n else []


def _x_copies(n_sib, in_refs, out_refs, sems):
    if not in_refs:
        return []
    send_sems, recv_sems = sems
    x, y, c = _my_place()
    chips = [(1 - x, y), (x, 1 - y), (1 - x, 1 - y)]
    copies = []

    def add(src, dst, to):
        k = len(copies)
        copies.append(pltpu.make_async_remote_copy(
            src_ref=src, dst_ref=dst, send_sem=send_sems.at[k], recv_sem=recv_sems.at[k], device_id=to,
            device_id_type=MESH_ID))

    for a, (src, dst) in enumerate(zip(in_refs, out_refs)):
        if a < n_sib:
            for j in range(4):
                add(src.at[2 * j + 1 - c], dst.at[j], (x, y, 1 - c))
        else:
            for k, (px, py) in enumerate(chips):
                add(src.at[2 * px + py], dst.at[k], (px, py, c))
    return copies


def _adamw_math(w, g, m, v):
    m = ADAM_B1 * m + (1.0 - ADAM_B1) * g
    v = ADAM_B2 * v + (1.0 - ADAM_B2) * jnp.square(g)
    m_hat = m / (1.0 - ADAM_B1 ** ADAM_STEP)
    v_hat = v / (1.0 - ADAM_B2 ** ADAM_STEP)
    delta = -ADAM_LR * (m_hat / (jnp.sqrt(v_hat) + ADAM_EPS) + ADAM_WD * w)
    return delta, m, v


def _sum_adamws(name, gs, sibs, gots, ws, ms, vs, slot_idx, chip_idx):
    n = len(gs)

    def body(s_ref, j_ref, *refs):
        ins, outs = refs[:6 * n], refs[6 * n:]
        for a in range(n):
            g_ref, sib_ref, got_ref, w_ref, m_ref, v_ref = (ins[k * n + a] for k in range(6))
            go_ref, d_ref, m2_ref, v2_ref = outs[4 * a:4 * a + 4]
            grad = g_ref[0] + sib_ref[0]
            for k in range(3):
                grad = grad + got_ref[k].astype(F32)
            go_ref[...] = grad
            d_ref[...], m2_ref[...], v2_ref[...] = _adamw_math(w_ref[...], grad, m_ref[...], v_ref[...])

    shapes = [g.shape[1:] for g in gs]
    flat = [_part_spec(*s, 0) for s in shapes]
    in_specs = ([_part_spec(*s, 0, (1,), lambda t, s_ref, j_ref: (s_ref[0],)) for s in shapes]
                + [_part_spec(*s, 0, (1,), lambda t, s_ref, j_ref: (j_ref[0],)) for s in shapes]
                + [_part_spec(*s, 0, (3,), lambda t, s_ref, j_ref: (0,)) for s in shapes] + flat * 3)
    grid_spec = pltpu.PrefetchScalarGridSpec(
        num_scalar_prefetch=2, grid=(N_PARTS,), in_specs=in_specs, out_specs=[f for f in flat for _ in range(4)])
    res = pl.pallas_call(
        body, name=name, grid_spec=grid_spec,
        out_shape=[jax.ShapeDtypeStruct(s, F32) for s in shapes for _ in range(4)],
        compiler_params=_cparams(("arbitrary",), VMEM_LIMIT),
    )(slot_idx, chip_idx, *gs, *sibs, *gots, *ws, *ms, *vs)
    return [res[4 * a:4 * a + 4] for a in range(n)]


BIG = ("w_in", "w_uq", "w_ukv", "w_branch", "w_out", "w_ffn_gate", "w_ffn_up", "w_ffn_down", "w_ple_gate", "w_ple_proj")
SMALL = (
    ("mix_norm_g", 0, 1, 1024), ("q_a_norm_g", 1, 1, 384), ("kv_a_norm_g", 2, 1, 256), ("q_norm_g", 3, 1, 96),
    ("k_norm_g", 4, 1, 96), ("hg_lb_logits", 5, 2, 512), ("hg_out_norm_g", 7, 1, 128), ("ffn_norm_g", 8, 1, 1024),
    ("ple_gate_norm_g", 9, 1, 1024), ("ple_post_norm_g", 10, 1, 1024),
)
SLAB_ROWS, LOSS_ROW = 16, 15


def _pack_partials(small_g, loss_p):
    def body(*refs):
        val_refs, loss_ref, out_ref = refs[:len(SMALL)], refs[len(SMALL)], refs[len(SMALL) + 1]
        out_ref[...] = jnp.zeros_like(out_ref)
        for (_, r0, rows, cols), ref in zip(SMALL, val_refs):
            val = ref[...]
            if val.shape[0] != rows:
                val = jnp.sum(val, axis=0, keepdims=True)
            out_ref[r0:r0 + rows, :cols] = val[:, :cols]
        out_ref[LOSS_ROW:LOSS_ROW + 1, :HEAD_PAD] = jnp.full((1, HEAD_PAD), jnp.sum(loss_ref[...]), F32)

    return pl.pallas_call(
        body, name="pack_partials", out_shape=jax.ShapeDtypeStruct((SLAB_ROWS, D_MODEL), F32),
    )(*[small_g[n] for n, *_ in SMALL], loss_p)


def _adamw_small(parts, ws, ms, vs):
    n = len(SMALL)

    def body(p_ref, *refs):
        ins, loss_ref, outs = refs[:3 * n], refs[3 * n], refs[3 * n + 1:]
        total = p_ref[0]
        for d in range(1, N_DEV):
            total = total + p_ref[d]
        loss_ref[...] = total[LOSS_ROW:LOSS_ROW + 1, 0:1]
        for a, (_, r0, rows, cols) in enumerate(SMALL):
            g = total[r0:r0 + rows, :cols]
            outs[4 * a][...] = g
            outs[4 * a + 1][...], outs[4 * a + 2][...], outs[4 * a + 3][...] = _adamw_math(
                ins[a][...], g, ins[n + a][...], ins[2 * n + a][...])

    shapes = [jax.ShapeDtypeStruct((rows, cols), F32) for _, _, rows, cols in SMALL]
    res = pl.pallas_call(
        body, name="adamw_small", out_shape=[jax.ShapeDtypeStruct((1, 1), F32)] + [s for s in shapes for _ in range(4)],
    )(parts, *ws, *ms, *vs)
    return res[0], [res[1 + 4 * a:5 + 4 * a] for a in range(n)]


_WEIGHTS = ["mix_norm_g", "w_in", "q_a_norm_g", "w_uq", "kv_a_norm_g", "w_ukv", "q_norm_g", "k_norm_g", "hg_lb_logits",
            "hg_out_norm_g", "w_branch", "w_out", "ffn_norm_g", "w_ffn_gate", "w_ffn_up", "w_ffn_down",
            "ple_gate_norm_g", "w_ple_gate", "w_ple_proj", "ple_post_norm_g"]


def _step(x, p, positions, tgt, w, m, v):
    small_names = [n for n, *_ in SMALL]
    T = x.shape[1]
    px, py, pc = _my_place()
    as_idx = lambda t: jnp.reshape(t, (1,)).astype(jnp.int32)

    def two_d(n, t):
        t = t.reshape(-1, t.shape[-1])
        return t.T if n in TRANSPOSED else t

    def full_shape(n, t):
        return (t.T if n in TRANSPOSED else t).reshape(w[n].shape)

    blocks = {n: two_d(n, w[n]).astype(MM) for n in BIG}
    big = dict(zip(EARLY, _all_gather("ag_weights", [blocks[n] for n in EARLY])))
    small = {n: (w[n] if n == "hg_lb_logits" else w[n].reshape(1, -1)) for n in small_names}

    loss_p, grad_x, small_g, grads, sibs, gots, parts_c = _local_step(
        x[0], p[0, 0], positions.reshape(T, 1), tgt[0], small, big, late_blocks=blocks, core=as_idx(pc))

    gots.update(zip(GROUP_C, _exchange_chips(parts_c)))
    out_g, out_d, out_m, out_v = {}, {}, {}, {}
    for tag, names in (("ab", GROUP_A + GROUP_B), ("c", GROUP_C)):
        pick = lambda table: [table[n] for n in names]
        res = _sum_adamws("adamw_" + tag, pick(grads), pick(sibs), pick(gots), [two_d(n, w[n]) for n in names],
                          [two_d(n, m[n]) for n in names], [two_d(n, v[n]) for n in names],
                          as_idx(4 * px + 2 * py + pc), as_idx(2 * px + py))
        for n, r in zip(names, res):
            out_g[n], out_d[n], out_m[n], out_v[n] = [full_shape(n, t) for t in r]

    parts = _all_gather("ag_small", [_pack_partials(small_g, loss_p)])[0]
    loss, res = _adamw_small(parts, *([t[n] for n in small_names] for t in (w, m, v)))
    for n, r in zip(small_names, res):
        out_g[n], out_d[n], out_m[n], out_v[n] = r

    outs = [loss.reshape(()), grad_x[None]]
    for table in (out_g, out_d, out_m, out_v):
        outs += [table[n] for n in _WEIGHTS]
    return tuple(outs)


def kernel(x, p, positions, mix_norm_g, w_in, q_a_norm_g, w_uq, kv_a_norm_g, w_ukv, q_norm_g, k_norm_g, hg_lb_logits, hg_out_norm_g, w_branch, w_out, ffn_norm_g, w_ffn_gate, w_ffn_up, w_ffn_down, ple_gate_norm_g, w_ple_gate, w_ple_proj, ple_post_norm_g, loss_target, m_mix_norm_g, m_w_in, m_q_a_norm_g, m_w_uq, m_kv_a_norm_g, m_w_ukv, m_q_norm_g, m_k_norm_g, m_hg_lb_logits, m_hg_out_norm_g, m_w_branch, m_w_out, m_ffn_norm_g, m_w_ffn_gate, m_w_ffn_up, m_w_ffn_down, m_ple_gate_norm_g, m_w_ple_gate, m_w_ple_proj, m_ple_post_norm_g, v_mix_norm_g, v_w_in, v_q_a_norm_g, v_w_uq, v_kv_a_norm_g, v_w_ukv, v_q_norm_g, v_k_norm_g, v_hg_lb_logits, v_hg_out_norm_g, v_w_branch, v_w_out, v_ffn_norm_g, v_w_ffn_gate, v_w_ffn_up, v_w_ffn_down, v_ple_gate_norm_g, v_w_ple_gate, v_w_ple_proj, v_ple_post_norm_g):
    w = dict(mix_norm_g=mix_norm_g, w_in=w_in, q_a_norm_g=q_a_norm_g, w_uq=w_uq, kv_a_norm_g=kv_a_norm_g, w_ukv=w_ukv,
             q_norm_g=q_norm_g, k_norm_g=k_norm_g, hg_lb_logits=hg_lb_logits, hg_out_norm_g=hg_out_norm_g,
             w_branch=w_branch, w_out=w_out, ffn_norm_g=ffn_norm_g, w_ffn_gate=w_ffn_gate, w_ffn_up=w_ffn_up,
             w_ffn_down=w_ffn_down, ple_gate_norm_g=ple_gate_norm_g, w_ple_gate=w_ple_gate, w_ple_proj=w_ple_proj,
             ple_post_norm_g=ple_post_norm_g)
    m = dict(mix_norm_g=m_mix_norm_g, w_in=m_w_in, q_a_norm_g=m_q_a_norm_g, w_uq=m_w_uq, kv_a_norm_g=m_kv_a_norm_g,
             w_ukv=m_w_ukv, q_norm_g=m_q_norm_g, k_norm_g=m_k_norm_g, hg_lb_logits=m_hg_lb_logits,
             hg_out_norm_g=m_hg_out_norm_g, w_branch=m_w_branch, w_out=m_w_out, ffn_norm_g=m_ffn_norm_g,
             w_ffn_gate=m_w_ffn_gate, w_ffn_up=m_w_ffn_up, w_ffn_down=m_w_ffn_down,
             ple_gate_norm_g=m_ple_gate_norm_g, w_ple_gate=m_w_ple_gate, w_ple_proj=m_w_ple_proj,
             ple_post_norm_g=m_ple_post_norm_g)
    v = dict(mix_norm_g=v_mix_norm_g, w_in=v_w_in, q_a_norm_g=v_q_a_norm_g, w_uq=v_w_uq, kv_a_norm_g=v_kv_a_norm_g,
             w_ukv=v_w_ukv, q_norm_g=v_q_norm_g, k_norm_g=v_k_norm_g, hg_lb_logits=v_hg_lb_logits,
             hg_out_norm_g=v_hg_out_norm_g, w_branch=v_w_branch, w_out=v_w_out, ffn_norm_g=v_ffn_norm_g,
             w_ffn_gate=v_w_ffn_gate, w_ffn_up=v_w_ffn_up, w_ffn_down=v_w_ffn_down,
             ple_gate_norm_g=v_ple_gate_norm_g, w_ple_gate=v_w_ple_gate, w_ple_proj=v_w_ple_proj,
             ple_post_norm_g=v_ple_post_norm_g)
    return _step(x, p, positions, loss_target, w, m, v)
```

```python
import functools

import jax
import jax.numpy as jnp
import numpy as np
from jax import lax
from jax.experimental import pallas as pl
from jax.experimental.pallas import tpu as pltpu

F32 = jnp.float32
MM = jnp.bfloat16
HI = lax.Precision.HIGHEST
MESH_ID = pl.DeviceIdType.MESH

D_MODEL = 1024
N_DEV = 8
MLA_HEADS = 8
QK_NOPE = 64
QK_ROPE = 32
QK_DIM = 96
V_DIM = 64
HEAD_PAD = 128
Q_RANK = 384
KV_RANK = 256
ROPE_BASE = 10000.0
HG_HEADS = 4
HG_DIM = 128
HG_W = 512
HG_CHUNK = 64
FFN = 2816
PLE = 256
EPS = 1e-6
ATT_SCALE = QK_DIM ** -0.5
NEG = -1e30

ADAM_LR = 0.001
ADAM_B1 = 0.9
ADAM_B2 = 0.999
ADAM_EPS = 1e-08
ADAM_WD = 0.01
ADAM_STEP = 10

SEC_CQ = (0, 384)
SEC_CKV = (384, 256)
SEC_KR = (640, 128)
SEC_HQ = (768, 512)
SEC_HF = (1280, 512)
SEC_HI = (1792, 512)
SEC_HG = (2304, 512)
SEC_BG = (2816, 2048)
IN_PAD = 4864
SECTIONS = (SEC_CQ, SEC_CKV, SEC_KR, SEC_HQ, SEC_HF, SEC_HI, SEC_HG, SEC_BG)
COL_SECTIONS = ((0, 384), (384, 256), (640, 32), (672, 512), (1184, 512), (1696, 512), (2208, 512), (2720, 2048))
IN_COLS = 4768
IN_BLOCK = IN_COLS // 8

VMEM_LIMIT = 58 * 1024 * 1024
ROW_TILE = 256
ATT_TILE = 1024
ATT_HEADS = 4
HG_BLOCK = 512
HG_UNROLL = 4


def _dot(a, b):
    return jnp.dot(a.astype(MM), b.astype(MM), preferred_element_type=F32)


def _dot_nt(a, b):
    return lax.dot_general(a.astype(MM), b.astype(MM), (((1,), (1,)), ((), ())), preferred_element_type=F32)


def _dot_tn(a, b):
    return lax.dot_general(a.astype(MM), b.astype(MM), (((0,), (0,)), ((), ())), preferred_element_type=F32)


def _dot_hi(a, b):
    return jnp.dot(a, b, preferred_element_type=F32, precision=HI)


def _sigmoid(x):
    return 1.0 / (1.0 + jnp.exp(-x))


def _rms(x, n=None):
    n = x.shape[-1] if n is None else n
    r = lax.rsqrt(jnp.sum(x * x, axis=-1, keepdims=True) * (1.0 / n) + EPS)
    return x * r, r


def _rms_bwd(dxh, xh, r, n=None):
    n = xh.shape[-1] if n is None else n
    return r * (dxh - xh * (jnp.sum(dxh * xh, axis=-1, keepdims=True) * (1.0 / n)))


def _rope_tables(pos, tm):
    lane = lax.broadcasted_iota(jnp.int32, (tm, HEAD_PAD), 1)
    idx = jnp.where(lane < QK_NOPE + QK_ROPE // 2, lane - QK_NOPE, lane - QK_NOPE - QK_ROPE // 2)
    inv = jnp.exp(idx.astype(F32) * (-np.log(ROPE_BASE) * 2.0 / QK_ROPE))
    ang = pos.astype(F32) * inv
    in_rope = (lane >= QK_NOPE) & (lane < QK_DIM)
    first = lane < QK_NOPE + QK_ROPE // 2
    cos_t = jnp.where(in_rope, jnp.cos(ang), 1.0)
    sin_t = jnp.where(in_rope, jnp.where(first, -jnp.sin(ang), jnp.sin(ang)), 0.0)
    return cos_t, sin_t, (first, in_rope)


def _rope_swap(x, halves):
    first, in_rope = halves
    half = QK_ROPE // 2
    return jnp.where(in_rope, jnp.where(first, pltpu.roll(x, HEAD_PAD - half, 1), pltpu.roll(x, half, 1)), 0.0)


def _cparams(sem, vmem=None):
    return pltpu.CompilerParams(dimension_semantics=sem, vmem_limit_bytes=vmem)


def _row_call(name, body, T, tm, row_ins, full_ins, row_outs, acc_outs, vmem=None, scratch=(), xchg=((), ())):
    n_in, n_out, n_x = len(row_ins) + len(full_ins), len(row_outs) + len(acc_outs), _x_count(xchg)
    steps = T // tm

    def kern(*refs):
        ins, x_in, refs = refs[:n_in], refs[n_in:n_in + n_x], refs[n_in + n_x:]
        outs, x_out, refs = refs[:n_out], refs[n_out:n_out + n_x], refs[n_out + n_x:]
        scr, x_sems = refs[:len(scratch)], refs[len(scratch):]
        i = pl.program_id(0)
        if n_x:
            @pl.when(i == 0)
            def _():
                for cp in _x_copies(len(xchg[0]), x_in, x_out, x_sems):
                    cp.start()

        body(i, *ins, *outs, *scr)
        if n_x:
            @pl.when(i == steps - 1)
            def _():
                for cp in _x_copies(len(xchg[0]), x_in, x_out, x_sems):
                    cp.wait()

    any_spec = pl.BlockSpec(memory_space=pl.ANY)
    in_specs = [pl.BlockSpec((tm, a.shape[1]), lambda i: (i, 0)) for a in row_ins]
    in_specs += [pl.BlockSpec(a.shape, lambda i, nd=a.ndim: (0,) * nd, pipeline_mode=pl.Buffered(1)) for a in full_ins]
    out_specs = [pl.BlockSpec((tm, n), lambda i: (i, 0)) for n, _ in row_outs]
    out_specs += [pl.BlockSpec(s, lambda i, nd=len(s): (0,) * nd) for s, _ in acc_outs]
    out_shape = [jax.ShapeDtypeStruct((T, n), dt) for n, dt in row_outs]
    out_shape += [jax.ShapeDtypeStruct(s, dt) for s, dt in acc_outs]
    return pl.pallas_call(
        kern, name=name, grid=(steps,), in_specs=in_specs + [any_spec] * n_x, out_specs=out_specs + [any_spec] * n_x,
        out_shape=out_shape + _x_out_shapes(xchg), scratch_shapes=list(scratch) + _x_sems(xchg),
        compiler_params=_cparams(("arbitrary",), vmem),
    )(*row_ins, *full_ins, *xchg[0], *xchg[1])


FFN_HALVES = (slice(0, FFN // 2), slice(FFN // 2, FFN))
ROW_CHUNK = 16
CHUNK_UNROLL = True


def _by_chunks(tm, fn):
    def step(c, carry):
        fn(pl.ds(pl.multiple_of(c * ROW_CHUNK, ROW_CHUNK), ROW_CHUNK))
        return carry

    lax.fori_loop(0, tm // ROW_CHUNK, step, 0, unroll=CHUNK_UNROLL)


def _fold8(x):
    return x[:8] + x[8:]


def _acc(ref, i, val):
    @pl.when(i == 0)
    def _():
        ref[...] = val

    @pl.when(i != 0)
    def _():
        ref[...] += val


def _in_proj_fwd(x, g_mix, w_in, T, tm):
    def body(i, x_ref, g_ref, w_ref, h_ref, *rest):
        outs, pj_s = rest[:-1], rest[-1]
        g = g_ref[...]

        def norm(rows):
            h_ref[rows, :] = (_rms(x_ref[rows, :])[0] * g).astype(MM)

        _by_chunks(tm, norm)
        for d in range(N_DEV):
            pj_s[d] = _dot_nt(h_ref[...], w_ref[d])

        def join_and_cut(rows):
            proj = jnp.concatenate([pj_s[d, rows, :] for d in range(N_DEV)], axis=1)
            for (s, n), o_ref in zip(COL_SECTIONS, outs):
                if n == QK_ROPE:
                    o_ref[rows, :] = jnp.concatenate(
                        [jnp.zeros((ROW_CHUNK, QK_NOPE), F32), proj[:, s:s + n],
                         jnp.zeros((ROW_CHUNK, HEAD_PAD - QK_DIM), F32)], axis=1)
                else:
                    o_ref[rows, :] = proj[:, s:s + n]

        _by_chunks(tm, join_and_cut)

    row_outs = [(D_MODEL, MM)] + [(n, F32) for _, n in SECTIONS]
    return _row_call("in_proj_fwd", body, T, tm, [x], [g_mix, w_in], row_outs, [], VMEM_LIMIT,
                     scratch=[pltpu.VMEM((N_DEV, tm, IN_BLOCK), F32)])


def _mla_heads_fwd(raw, g_pad, cos_t, sin_t, first):
    outs, saved = [], []
    for h in range(MLA_HEADS):
        xh, r = _rms(raw[:, h * HEAD_PAD:(h + 1) * HEAD_PAD], QK_DIM)
        y = xh * g_pad
        outs.append(y * cos_t + _rope_swap(y, first) * sin_t)
        saved.append((xh, r))
    return outs, saved


def _mla_raw_heads(cqn, ckvn, kr, wuq_ref, wukv_ref, tm):
    lane = lax.broadcasted_iota(jnp.int32, (tm, HEAD_PAD), 1)
    nope = lane < QK_NOPE
    one_lane = jnp.where(lane == V_DIM, 1.0, 0.0)
    qs, ks, vs = [], [], []
    for h in range(MLA_HEADS):
        qs.append(_dot_nt(cqn, wuq_ref[h]))
        kv = _dot(ckvn, wukv_ref[h])
        ks.append(jnp.where(nope, kv, kr))
        vs.append(jnp.where(nope, pltpu.roll(kv, V_DIM, 1), one_lane))
    return jnp.concatenate(qs, axis=1), jnp.concatenate(ks, axis=1), jnp.concatenate(vs, axis=1)


def _mla_prep_fwd(cq, ckv, kr, pos, g_qa, g_kva, g_qn, g_kn, w_uq, w_ukv, T, tm):
    def body(i, cq_ref, ckv_ref, kr_ref, pos_ref, gqa_ref, gkva_ref, gqn_ref, gkn_ref, wuq_ref, wukv_ref,
             q_ref, k_ref, v_ref):
        cos_t, sin_t, first = _rope_tables(pos_ref[...], tm)
        cqn = _rms(cq_ref[...])[0] * gqa_ref[...]
        ckvn = _rms(ckv_ref[...])[0] * gkva_ref[...]
        q_raw, k_raw, v = _mla_raw_heads(cqn, ckvn, kr_ref[...], wuq_ref, wukv_ref, tm)
        qs, _ = _mla_heads_fwd(q_raw, gqn_ref[...], cos_t, sin_t, first)
        ks, _ = _mla_heads_fwd(k_raw, gkn_ref[...], cos_t, sin_t, first)
        q_ref[...] = (jnp.concatenate(qs, axis=1) * ATT_SCALE).astype(MM)
        k_ref[...] = jnp.concatenate(ks, axis=1).astype(MM)
        v_ref[...] = v.astype(MM)

    w = MLA_HEADS * HEAD_PAD
    return _row_call("mla_prep_fwd", body, T, tm, [cq, ckv, kr, pos], [g_qa, g_kva, g_qn, g_kn, w_uq, w_ukv],
                     [(w, MM), (w, MM), (w, MM)], [])


def _causal_pairs(n, by_query):
    if by_query:
        pairs = [(q, k) for q in range(n) for k in range(q + 1)]
    else:
        pairs = [(q, k) for k in range(n) for q in range(k, n)]
    return np.array([p[0] for p in pairs], np.int32), np.array([p[1] for p in pairs], np.int32)


def _flash_fwd(qf, kf, vf, T, ag_blocks=()):
    tq = min(ATT_TILE, T)
    nq = T // tq

    qi_tab, ki_tab = _causal_pairs(nq, by_query=True)

    hp = ATT_HEADS

    n_ag = len(ag_blocks)
    n_heads, n_pairs = MLA_HEADS // hp, len(qi_tab)

    def body(qi_ref, ki_ref, q_ref, k_ref, v_ref, *rest):
        ag_in, (o_ref, lse_ref), rest = rest[:n_ag], rest[n_ag:n_ag + 2], rest[n_ag + 2:]
        ag_out, (m_s, acc_s), ag_sems = rest[:n_ag], rest[n_ag:n_ag + 2], rest[n_ag + 2:]
        t = pl.program_id(1)
        qi, ki = qi_ref[t], ki_ref[t]
        if n_ag:
            @pl.when((pl.program_id(0) == 0) & (t == 0))
            def _():
                _ag_start(ag_in, ag_out, ag_sems)

        @pl.when(ki == 0)
        def _():
            m_s[...] = jnp.full_like(m_s, NEG)
            acc_s[...] = jnp.zeros_like(acc_s)

        def step(masked):
            halves = 2 if masked and tq % (2 * HEAD_PAD) == 0 else 1
            w = tq // halves
            for hh in range(hp):
                hs = slice(hh * HEAD_PAD, (hh + 1) * HEAD_PAD)
                for part in range(halves):
                    cols, nk = slice(part * w, (part + 1) * w), (part + 1) * w
                    s_t = _dot_nt(k_ref[:nk, hs], q_ref[cols, hs])
                    if masked:
                        key = lax.broadcasted_iota(jnp.int32, (nk, w), 0)
                        qry = lax.broadcasted_iota(jnp.int32, (nk, w), 1) + part * w
                        s_t = jnp.where(key <= qry, s_t, NEG)
                    m_old = m_s[hh, :, cols]
                    m_new = jnp.maximum(m_old, jnp.max(s_t, axis=0, keepdims=True))
                    p_t = jnp.exp(s_t - m_new)
                    acc_s[hh, :, cols] = jnp.exp(m_old - m_new) * acc_s[hh, :, cols] + _dot_tn(v_ref[:nk, hs], p_t)
                    m_s[hh, :, cols] = m_new

        @pl.when(ki < qi)
        def _():
            step(False)

        @pl.when(ki == qi)
        def _():
            step(True)
            real = lax.broadcasted_iota(jnp.int32, (HEAD_PAD, tq), 0) < V_DIM
            for hh in range(hp):
                hs = slice(hh * HEAD_PAD, (hh + 1) * HEAD_PAD)
                acc = acc_s[hh]
                l = acc[V_DIM:V_DIM + 1]
                o_ref[:, hs] = jnp.where(real, acc / l, 0.0).T
                lse_ref[:, hs] = jnp.broadcast_to(m_s[hh] + jnp.log(l), (HEAD_PAD, tq)).T

        if n_ag:
            @pl.when((pl.program_id(0) == n_heads - 1) & (t == n_pairs - 1))
            def _():
                _ag_finish(ag_in, ag_out, ag_sems)

    q_spec = pl.BlockSpec((tq, hp * HEAD_PAD), lambda h, t, qi_ref, ki_ref: (qi_ref[t], h))
    kv_spec = pl.BlockSpec((tq, hp * HEAD_PAD), lambda h, t, qi_ref, ki_ref: (ki_ref[t], h))
    any_spec = pl.BlockSpec(memory_space=pl.ANY)
    grid_spec = pltpu.PrefetchScalarGridSpec(
        num_scalar_prefetch=2, grid=(n_heads, n_pairs),
        in_specs=[q_spec, kv_spec, kv_spec] + [any_spec] * n_ag, out_specs=[q_spec, q_spec] + [any_spec] * n_ag,
        scratch_shapes=[pltpu.VMEM((hp, 1, tq), F32), pltpu.VMEM((hp, HEAD_PAD, tq), F32)]
        + (_ag_sems(n_ag) if n_ag else []))
    return pl.pallas_call(
        body, name="flash_fwd", grid_spec=grid_spec,
        out_shape=[jax.ShapeDtypeStruct((T, MLA_HEADS * HEAD_PAD), F32)] * 2 + _ag_out_shapes(ag_blocks),
        compiler_params=_cparams(("arbitrary", "arbitrary")),
    )(jnp.asarray(qi_tab), jnp.asarray(ki_tab), qf, kf, vf, *ag_blocks)


def _hg_gates(hf, lb):
    sg = _sigmoid(hf)
    f = lb + (1.0 - lb) * sg
    return sg, f, jnp.log(f), 1.0 - f


def _tri(n, lower):
    r = lax.broadcasted_iota(jnp.int32, (n, n), 0)
    c = lax.broadcasted_iota(jnp.int32, (n, n), 1)
    return jnp.where((c <= r) if lower else (c >= r), 1.0, 0.0).astype(F32)


def _hg_levels():
    C = HG_CHUNK
    t = lax.broadcasted_iota(jnp.int32, (C, C), 0)
    s = lax.broadcasted_iota(jnp.int32, (C, C), 1)
    levels = []
    for shift in range(C.bit_length() - 2, -1, -1):
        pair_t, pair_s = lax.shift_right_logical(t, shift + 1), lax.shift_right_logical(s, shift + 1)
        later_t = (lax.shift_right_logical(t, shift) & 1) == 1
        earlier_s = (lax.shift_right_logical(s, shift) & 1) == 0
        levels.append((1 << shift, (pair_t == pair_s) & later_t & earlier_s))
    return levels, t == s


def _hg_refs(b):
    C, n = b.shape
    row = lax.broadcasted_iota(jnp.int32, (C, n), 0)
    back1, back2, ahead1 = pltpu.roll(b, 1, 0), pltpu.roll(b, 2, 0), pltpu.roll(b, C - 1, 0)
    refs = []
    for half in (32, 16, 8, 4):
        refs.append(jnp.concatenate(
            [jnp.broadcast_to(b[lo + half - 1:lo + half], (2 * half, n)) for lo in range(0, C, 2 * half)], axis=0))
    in4 = row & 3
    refs.append(jnp.where(in4 == 0, ahead1, jnp.where(in4 == 1, b, jnp.where(in4 == 2, back1, back2))))
    refs.append(jnp.where((row & 1) == 1, back1, b))
    return refs


def _hg_intra(q, k, b, refs, levels, eye):
    a = jnp.where(eye, jnp.sum(q * k, axis=1, keepdims=True), 0.0)
    saved = []
    for r, (_, mask) in zip(refs, levels):
        e = jnp.exp(-jnp.abs(b - r))
        q_t, k_t = q * e, k * e
        a = a + jnp.where(mask, _dot_nt(q_t, k_t), 0.0)
        saved.append((q_t, k_t, e))
    return a, saved


def _hg_intra_bwd(d_a, q, k, saved, levels, eye):
    diag = jnp.sum(jnp.where(eye, d_a, 0.0), axis=1, keepdims=True)
    dq, dk = diag * k, diag * q
    for (q_t, k_t, e), (_, mask) in zip(saved, levels):
        da = jnp.where(mask, d_a, 0.0)
        dq = dq + _dot(da, k_t) * e
        dk = dk + _dot_tn(da, q_t) * e
    return dq, dk


def _hgrn_fwd(hq, hf, hi, lb, T, ag_blocks=()):
    rb = min(HG_BLOCK, T)
    ncb = rb // HG_CHUNK
    n_ag, nb = len(ag_blocks), T // rb

    def body(hq_ref, hf_ref, hi_ref, lb_ref, *rest):
        ag_in, (o_ref, s0_ref), rest = rest[:n_ag], rest[n_ag:n_ag + 2], rest[n_ag + 2:]
        ag_out, st_ref, ag_sems = rest[:n_ag], rest[n_ag], rest[n_ag + 1:]

        @pl.when(pl.program_id(0) == 0)
        def _():
            st_ref[...] = jnp.zeros_like(st_ref)
            if n_ag:
                _ag_start(ag_in, ag_out, ag_sems)

        tril = _tri(HG_CHUNK, True)
        levels, eye = _hg_levels()

        def chunk(c, carry):
            rows = pl.ds(pl.multiple_of(c * HG_CHUNK, HG_CHUNK), HG_CHUNK)
            _, _, logf, kk = _hg_gates(hf_ref[rows, :], lb_ref[...])
            b = _dot_hi(tril, logf)
            refs = _hg_refs(b)
            q_all, v_all = hq_ref[rows, :], hi_ref[rows, :]
            outs = []
            for h in range(HG_HEADS):
                ls = slice(h * HG_DIM, (h + 1) * HG_DIM)
                q, k, v, bh = q_all[:, ls], kk[:, ls], v_all[:, ls], b[:, ls]
                st = st_ref[h]
                s0_ref[c, h * HG_DIM:(h + 1) * HG_DIM, :] = st
                b_end = bh[HG_CHUNK - 1:HG_CHUNK]
                a, _ = _hg_intra(q, k, bh, [r[:, ls] for r in refs], levels, eye)
                outs.append(_dot_nt(q * jnp.exp(bh), st) + _dot(a, v))
                st_ref[h] = st * jnp.exp(b_end) + _dot_tn(v, k * jnp.exp(b_end - bh))
            o_ref[rows, :] = jnp.concatenate(outs, axis=1)
            return carry

        lax.fori_loop(0, ncb, chunk, 0, unroll=HG_UNROLL)

        if n_ag:
            @pl.when(pl.program_id(0) == nb - 1)
            def _():
                _ag_finish(ag_in, ag_out, ag_sems)

    row = pl.BlockSpec((rb, HG_W), lambda i: (i, 0))
    any_spec = pl.BlockSpec(memory_space=pl.ANY)
    return pl.pallas_call(
        body, name="hgrn_fwd", grid=(nb,),
        in_specs=[row, row, row, pl.BlockSpec((1, HG_W), lambda i: (0, 0))] + [any_spec] * n_ag,
        out_specs=[row, pl.BlockSpec((ncb, HG_W, HG_DIM), lambda i: (i, 0, 0))] + [any_spec] * n_ag,
        out_shape=[jax.ShapeDtypeStruct((T, HG_W), F32), jax.ShapeDtypeStruct((T // HG_CHUNK, HG_W, HG_DIM), F32)]
        + _ag_out_shapes(ag_blocks),
        scratch_shapes=[pltpu.VMEM((HG_HEADS, HG_DIM, HG_DIM), F32)] + (_ag_sems(n_ag) if n_ag else []),
        compiler_params=_cparams(("arbitrary",)),
    )(hq, hf, hi, lb, *ag_blocks)


def _hgrn_bwd(hq, hf, hi, do, s0, lb, T, xchg=((), ())):
    rb = min(HG_BLOCK, T)
    ncb = rb // HG_CHUNK
    nb = T // rb
    C = HG_CHUNK
    n_x, n_sib = _x_count(xchg), len(xchg[0])

    def body(hq_ref, hf_ref, hi_ref, do_ref, s0_ref, lb_ref, *rest):
        x_in, (dq_ref, df_ref, dv_ref, dlb_ref), rest = rest[:n_x], rest[n_x:n_x + 4], rest[n_x + 4:]
        x_out, dst_ref, x_sems = rest[:n_x], rest[n_x], rest[n_x + 1:]

        @pl.when(pl.program_id(0) == 0)
        def _():
            dst_ref[...] = jnp.zeros_like(dst_ref)
            dlb_ref[...] = jnp.zeros_like(dlb_ref)
            for cp in _x_copies(n_sib, x_in, x_out, x_sems):
                cp.start()

        tril, triu = _tri(C, True), _tri(C, False)
        row_cc = lax.broadcasted_iota(jnp.int32, (C, C), 0)
        col_cc = lax.broadcasted_iota(jnp.int32, (C, C), 1)
        last_row = lax.broadcasted_iota(jnp.int32, (C, HG_DIM), 0) == C - 1
        lb_v = lb_ref[...]
        levels, eye = _hg_levels()

        def chunk(cc, carry):
            c = ncb - 1 - cc
            rows = pl.ds(pl.multiple_of(c * C, C), C)
            hf_c = hf_ref[rows, :]
            sg, f, logf, kk = _hg_gates(hf_c, lb_v)
            b = _dot_hi(tril, logf)
            refs = _hg_refs(b)
            q_all, v_all, do_all = hq_ref[rows, :], hi_ref[rows, :], do_ref[rows, :]
            dq_o, dk_o, dv_o, db_o = [], [], [], []
            for h in range(HG_HEADS):
                ls = slice(h * HG_DIM, (h + 1) * HG_DIM)
                q, k, v, bh, d_o = q_all[:, ls], kk[:, ls], v_all[:, ls], b[:, ls], do_all[:, ls]
                st0 = s0_ref[c, h * HG_DIM:(h + 1) * HG_DIM, :]
                dst = dst_ref[h]
                b_end = bh[C - 1:C]
                e_b, e_end = jnp.exp(bh), jnp.exp(b_end)
                e_rem = jnp.exp(b_end - bh)
                qe, kd = q * e_b, k * e_rem
                st_end = st0 * e_end + _dot_tn(v, kd)
                a, saved = _hg_intra(q, k, bh, [r[:, ls] for r in refs], levels, eye)
                d_a = jnp.where(col_cc <= row_cc, _dot_nt(d_o, v), 0.0)
                dq_i, dk_i = _hg_intra_bwd(d_a, q, k, saved, levels, eye)
                dv = _dot_tn(a, d_o) + _dot_nt(kd, dst)
                dq = e_b * _dot(d_o, st0) + dq_i
                dk = e_rem * _dot(v, dst) + dk_i
                extra = jnp.sum(dst * st_end, axis=0, keepdims=True)
                db_o.append(q * dq - k * dk + jnp.where(last_row, extra, 0.0))
                dst_ref[h] = dst * e_end + _dot_tn(d_o, qe)
                dq_o.append(dq)
                dk_o.append(dk)
                dv_o.append(dv)
            dlogf = _dot_hi(triu, jnp.concatenate(db_o, axis=1))
            d_f = dlogf / f - jnp.concatenate(dk_o, axis=1)
            dq_ref[rows, :] = jnp.concatenate(dq_o, axis=1).astype(MM)
            dv_ref[rows, :] = jnp.concatenate(dv_o, axis=1).astype(MM)
            df_ref[rows, :] = (d_f * (1.0 - lb_v) * sg * (1.0 - sg)).astype(MM)
            dlb_ref[...] += jnp.sum(d_f * (1.0 - sg), axis=0, keepdims=True)
            return carry

        lax.fori_loop(0, ncb, chunk, 0, unroll=HG_UNROLL)

        if n_x:
            @pl.when(pl.program_id(0) == nb - 1)
            def _():
                for cp in _x_copies(n_sib, x_in, x_out, x_sems):
                    cp.wait()

    row = pl.BlockSpec((rb, HG_W), lambda i: (nb - 1 - i, 0))
    one = pl.BlockSpec((1, HG_W), lambda i: (0, 0))
    any_spec = pl.BlockSpec(memory_space=pl.ANY)
    return pl.pallas_call(
        body, name="hgrn_bwd", grid=(nb,),
        in_specs=[row, row, row, row, pl.BlockSpec((ncb, HG_W, HG_DIM), lambda i: (nb - 1 - i, 0, 0)), one]
        + [any_spec] * n_x,
        out_specs=[row, row, row, one] + [any_spec] * n_x,
        out_shape=[jax.ShapeDtypeStruct((T, HG_W), MM)] * 3 + [jax.ShapeDtypeStruct((1, HG_W), F32)]
        + _x_out_shapes(xchg),
        scratch_shapes=[pltpu.VMEM((HG_HEADS, HG_DIM, HG_DIM), F32)] + _x_sems(xchg),
        compiler_params=_cparams(("arbitrary",)),
    )(hq, hf, hi, do, s0, lb, *xchg[0], *xchg[1])


def _silu_parts(x):
    sg = _sigmoid(x)
    return x * sg, sg * (1.0 + x * (1.0 - sg))


def _merge_fwd(attn, o, hg, bg, x, g_out, w_bra, w_brb, w_out, T, tm):
    def body(i, attn_ref, o_ref, hg_ref, bg_ref, x_ref, g_ref, wa_ref, wb_ref, wo_ref,
             x1_ref, ya_ref, yb_ref, m_ref, rec_ref):
        g = g_ref[...]

        def recurrent_out(rows):
            for h in range(HG_HEADS):
                ls = slice(h * HG_DIM, (h + 1) * HG_DIM)
                rec_ref[rows, ls] = (_rms(o_ref[rows, ls])[0] * g * _silu_parts(hg_ref[rows, ls])[0]).astype(MM)

        _by_chunks(tm, recurrent_out)
        ya_ref[...] = _dot(attn_ref[...], wa_ref[...])
        yb_ref[...] = jnp.dot(rec_ref[...], wb_ref[...], preferred_element_type=F32)

        def gate(rows):
            m_ref[rows, :] = (_sigmoid(bg_ref[rows, :D_MODEL]) * ya_ref[rows, :]
                              + _sigmoid(bg_ref[rows, D_MODEL:]) * yb_ref[rows, :]).astype(MM)

        _by_chunks(tm, gate)
        x1_ref[...] = x_ref[...] + jnp.dot(m_ref[...], wo_ref[...], preferred_element_type=F32)

    return _row_call("merge_fwd", body, T, tm, [attn, o, hg, bg, x], [g_out, w_bra, w_brb, w_out],
                     [(D_MODEL, F32), (D_MODEL, F32), (D_MODEL, F32), (D_MODEL, MM), (HG_W, MM)], [], VMEM_LIMIT)


def _ffn_fwd(x1, g_ffn, w_g, w_u, w_d, T, tm):
    def body(i, x1_ref, g_ref, wg_ref, wu_ref, wd_ref, x2_ref, gt_ref, up_ref, h2_ref, a_s):
        g = g_ref[...]

        def norm(rows):
            h2_ref[rows, :] = (_rms(x1_ref[rows, :])[0] * g).astype(MM)

        _by_chunks(tm, norm)
        gt_ref[...] = _dot_nt(h2_ref[...], wg_ref[...])
        up_ref[...] = _dot_nt(h2_ref[...], wu_ref[...])

        def act(rows):
            for cs in FFN_HALVES:
                a_s[rows, cs] = (_silu_parts(gt_ref[rows, cs])[0] * up_ref[rows, cs]).astype(MM)

        _by_chunks(tm, act)
        x2_ref[...] = x1_ref[...] + jnp.dot(a_s[...], wd_ref[...], preferred_element_type=F32)

    return _row_call("ffn_fwd", body, T, tm, [x1], [g_ffn, w_g, w_u, w_d],
                     [(D_MODEL, F32), (FFN, F32), (FFN, F32), (D_MODEL, MM)], [], VMEM_LIMIT,
                     scratch=[pltpu.VMEM((tm, FFN), MM)])


def _ple_loss(x2, p, tgt, g_pg, g_post, w_pg, w_pp, T, tm):
    def body(i, x2_ref, p_ref, t_ref, gpg_ref, gpo_ref, wpg_ref, wpp_ref,
             dx2_ref, loss_ref, dgpo_ref, dgpg_ref, dwpg_ref, dwpp_ref, u_s, n3_s, z_s, dz_s, du_s, dy_s, dn3_s):
        @pl.when(i == 0)
        def _():
            for ref in (loss_ref, dgpo_ref, dgpg_ref, dwpg_ref, dwpp_ref):
                ref[...] = jnp.zeros_like(ref)

        gpg, gpo = gpg_ref[...], gpo_ref[...]
        p_mm = p_ref[...].astype(MM)
        for d in range(N_DEV):
            u_s[:, d * HEAD_PAD:(d + 1) * HEAD_PAD] = jnp.dot(p_mm, wpp_ref[d], preferred_element_type=F32)

        def gate_input(rows):
            n3_s[rows, :] = (_rms(x2_ref[rows, :])[0] * gpg).astype(MM)

        _by_chunks(tm, gate_input)
        z_s[...] = jnp.dot(n3_s[...], wpg_ref[...], preferred_element_type=F32)

        def loss_and_back(rows):
            uh, ru = _rms(u_s[rows, :])
            e = uh * gpo
            gate = _sigmoid(z_s[rows, :])
            diff = x2_ref[rows, :] + gate * e - t_ref[rows, :]
            dy = diff * (1.0 / D_MODEL)
            de = dy * gate
            dz_s[rows, :] = (dy * e * gate * (1.0 - gate)).astype(MM)
            du_s[rows, :] = _rms_bwd(de * gpo, uh, ru).astype(MM)
            dy_s[rows, :] = dy
            loss_ref[...] += _fold8(diff * diff) * (0.5 / D_MODEL)
            dgpo_ref[...] += _fold8(de * uh)

        _by_chunks(tm, loss_and_back)
        dn3_s[...] = _dot_nt(dz_s[...], wpg_ref[...])

        def gate_norm_back(rows):
            x2h, r3 = _rms(x2_ref[rows, :])
            dn3 = dn3_s[rows, :]
            dx2_ref[rows, :] = dy_s[rows, :] + _rms_bwd(dn3 * gpg, x2h, r3)
            dgpg_ref[...] += _fold8(dn3 * x2h)

        _by_chunks(tm, gate_norm_back)
        dwpg_ref[...] += _dot_tn(n3_s[...], dz_s[...])
        for d in range(N_DEV):
            dwpp_ref[d] += _dot_tn(p_mm, du_s[:, d * HEAD_PAD:(d + 1) * HEAD_PAD])

    vec = ((8, D_MODEL), F32)
    wide = lambda dt: pltpu.VMEM((tm, D_MODEL), dt)
    return _row_call("ple_loss", body, T, tm, [x2, p, tgt], [g_pg, g_post, w_pg, w_pp], [(D_MODEL, F32)],
                     [vec, vec, vec, ((D_MODEL, D_MODEL), F32), ((N_DEV, PLE, HEAD_PAD), F32)], VMEM_LIMIT,
                     scratch=[wide(F32), wide(MM), wide(F32), wide(MM), wide(MM), wide(F32), wide(F32)])


def _ffn_bwd(dx2, x1, gt, up, g_ffn, w_g, w_u, w_d, T, tm):
    def body(i, dx2_ref, x1_ref, gt_ref, up_ref, g_ref, wg_ref, wu_ref, wd_ref,
             dx1_ref, a_ref, dgt_ref, dup_ref, dg_ref, da_s, dh2_s):
        @pl.when(i == 0)
        def _():
            dg_ref[...] = jnp.zeros_like(dg_ref)

        g = g_ref[...]
        da_s[...] = _dot_nt(dx2_ref[...], wd_ref[...])

        def act_back(rows):
            for cs in FFN_HALVES:
                up, da = up_ref[rows, cs], da_s[rows, cs]
                silu, dsilu = _silu_parts(gt_ref[rows, cs])
                dgt_ref[rows, cs] = (da * up * dsilu).astype(MM)
                dup_ref[rows, cs] = (da * silu).astype(MM)
                a_ref[rows, cs] = (silu * up).astype(MM)

        _by_chunks(tm, act_back)
        dh2_s[...] = (jnp.dot(dgt_ref[...], wg_ref[...], preferred_element_type=F32)
                      + jnp.dot(dup_ref[...], wu_ref[...], preferred_element_type=F32))

        def norm_back(rows):
            x1h, r = _rms(x1_ref[rows, :])
            dh2 = dh2_s[rows, :]
            dx1_ref[rows, :] = dx2_ref[rows, :] + _rms_bwd(dh2 * g, x1h, r)
            dg_ref[...] += _fold8(dh2 * x1h)

        _by_chunks(tm, norm_back)

    return _row_call("ffn_bwd", body, T, tm, [dx2, x1, gt, up], [g_ffn, w_g, w_u, w_d],
                     [(D_MODEL, F32), (FFN, MM), (FFN, MM), (FFN, MM)], [((8, D_MODEL), F32)], VMEM_LIMIT,
                     scratch=[pltpu.VMEM((tm, FFN), F32), pltpu.VMEM((tm, D_MODEL), F32)])


def _merge_bwd(dx1, ya, yb, bg, o, hg, attn, m, rec, g_out, w_bra, w_brb, w_out, T, tm, xchg=((), ())):
    def body(i, dx1_ref, ya_ref, yb_ref, bg_ref, o_ref, hg_ref, attn_ref, m_ref, rec_ref, g_ref, wa_ref, wb_ref, wo_ref,
             dattn_ref, do_ref, dhg_ref, dbg_ref, dg_ref, dwo_ref, dwa_ref, dwb_ref, dm_s, dya_s, dyb_s, drec_s):
        @pl.when(i == 0)
        def _():
            for ref in (dg_ref, dwo_ref, dwa_ref, dwb_ref):
                ref[...] = jnp.zeros_like(ref)

        g = g_ref[...]
        dx1 = dx1_ref[...].astype(MM)
        dm_s[...] = _dot_nt(dx1, wo_ref[...])

        def gate_back(rows):
            dm = dm_s[rows, :]
            ga, gb = _sigmoid(bg_ref[rows, :D_MODEL]), _sigmoid(bg_ref[rows, D_MODEL:])
            dya_s[rows, :] = (dm * ga).astype(MM)
            dyb_s[rows, :] = (dm * gb).astype(MM)
            dbg_ref[rows, :D_MODEL] = (dm * ya_ref[rows, :] * ga * (1.0 - ga)).astype(MM)
            dbg_ref[rows, D_MODEL:] = (dm * yb_ref[rows, :] * gb * (1.0 - gb)).astype(MM)

        _by_chunks(tm, gate_back)
        dwo_ref[...] += _dot_tn(m_ref[...], dx1)
        attn_mm = attn_ref[...].astype(MM)
        for d in range(N_DEV):
            ds = slice(d * HEAD_PAD, (d + 1) * HEAD_PAD)
            dwa_ref[d] += _dot_tn(attn_mm, dya_s[:, ds])
            dwb_ref[d] += _dot_tn(rec_ref[...], dyb_s[:, ds])
        dattn_ref[...] = _dot_nt(dya_s[...], wa_ref[...])
        drec_s[...] = _dot_nt(dyb_s[...], wb_ref[...])

        def recurrent_out_back(rows):
            for h in range(HG_HEADS):
                ls = slice(h * HG_DIM, (h + 1) * HG_DIM)
                oh, r = _rms(o_ref[rows, ls])
                silu, dsilu = _silu_parts(hg_ref[rows, ls])
                dr = drec_s[rows, ls]
                dhg_ref[rows, ls] = (dr * oh * g * dsilu).astype(MM)
                don = dr * silu
                dg_ref[...] += _fold8(don * oh)
                do_ref[rows, ls] = _rms_bwd(don * g, oh, r)

        _by_chunks(tm, recurrent_out_back)

    wide = lambda n, dt: pltpu.VMEM((tm, n), dt)
    return _row_call("merge_bwd", body, T, tm, [dx1, ya, yb, bg, o, hg, attn, m, rec], [g_out, w_bra, w_brb, w_out],
                     [(D_MODEL, F32), (HG_W, F32), (HG_W, MM), (2 * D_MODEL, MM)],
                     [((8, HG_DIM), F32), ((D_MODEL, D_MODEL), F32), ((N_DEV, MLA_HEADS * HEAD_PAD, HEAD_PAD), F32),
                      ((N_DEV, HG_W, HEAD_PAD), F32)], VMEM_LIMIT,
                     scratch=[wide(D_MODEL, F32), wide(D_MODEL, MM), wide(D_MODEL, MM), wide(HG_W, F32)], xchg=xchg)


def _flash_bwd(qf, kf, vf, o, do, lse, T, xchg=((), ())):
    tq = min(ATT_TILE, T)
    nq = T // tq

    qi_tab, ki_tab = _causal_pairs(nq, by_query=False)

    n_x, n_sib = _x_count(xchg), len(xchg[0])
    hp = ATT_HEADS
    n_heads, n_pairs = MLA_HEADS // hp, len(qi_tab)

    def body(qi_ref, ki_ref, q_ref, k_ref, v_ref, o_ref, do_ref, lse_ref, *rest):
        x_in, (dq_ref, dk_ref, dv_ref), rest = rest[:n_x], rest[n_x:n_x + 3], rest[n_x + 3:]
        x_out, x_sems = rest[:n_x], rest[n_x:]
        t = pl.program_id(1)
        qi, ki = qi_ref[t], ki_ref[t]
        if n_x:
            @pl.when((pl.program_id(0) == 0) & (t == 0))
            def _():
                for cp in _x_copies(n_sib, x_in, x_out, x_sems):
                    cp.start()

        @pl.when(t == 0)
        def _():
            dq_ref[...] = jnp.zeros_like(dq_ref)

        def step(first):
            halves = 2 if first and tq % (2 * HEAD_PAD) == 0 else 1
            w = tq // halves
            for hh in range(hp):
                hs = slice(hh * HEAD_PAD, (hh + 1) * HEAD_PAD)
                for part in range(halves):
                    keys, qs = slice(part * w, (part + 1) * w), slice(part * w, tq)
                    nq_ = tq - part * w
                    q, k, d_o = q_ref[qs, hs], k_ref[keys, hs], do_ref[qs, hs]
                    s = _dot_nt(q, k)
                    if first:
                        row = lax.broadcasted_iota(jnp.int32, (nq_, w), 0)
                        col = lax.broadcasted_iota(jnp.int32, (nq_, w), 1)
                        s = jnp.where(col <= row, s, NEG)
                    p = jnp.exp(s - lse_ref[qs, hh * HEAD_PAD:hh * HEAD_PAD + 1])
                    delta = jnp.sum(d_o * o_ref[qs, hs], axis=1, keepdims=True)
                    ds = p * (_dot_nt(d_o, v_ref[keys, hs]) - delta)
                    rows = pl.ds(pl.multiple_of(qi * tq + part * w, w), nq_)
                    dq_ref[rows, hs] += _dot(ds, k)
                    if first:
                        dv_ref[keys, hs] = _dot_tn(p, d_o)
                        dk_ref[keys, hs] = _dot_tn(ds, q)
                    else:
                        dv_ref[keys, hs] += _dot_tn(p, d_o)
                        dk_ref[keys, hs] += _dot_tn(ds, q)

        @pl.when(qi == ki)
        def _():
            step(True)

        @pl.when(qi > ki)
        def _():
            step(False)

        if n_x:
            @pl.when((pl.program_id(0) == n_heads - 1) & (t == n_pairs - 1))
            def _():
                for cp in _x_copies(n_sib, x_in, x_out, x_sems):
                    cp.wait()

    q_spec = pl.BlockSpec((tq, hp * HEAD_PAD), lambda h, t, qi_ref, ki_ref: (qi_ref[t], h))
    kv_spec = pl.BlockSpec((tq, hp * HEAD_PAD), lambda h, t, qi_ref, ki_ref: (ki_ref[t], h))
    any_spec = pl.BlockSpec(memory_space=pl.ANY)
    w = MLA_HEADS * HEAD_PAD
    grid_spec = pltpu.PrefetchScalarGridSpec(
        num_scalar_prefetch=2, grid=(n_heads, n_pairs),
        in_specs=[q_spec, kv_spec, kv_spec, q_spec, q_spec, q_spec] + [any_spec] * n_x,
        out_specs=[pl.BlockSpec((T, hp * HEAD_PAD), lambda h, t, qi_ref, ki_ref: (0, h)), kv_spec, kv_spec]
        + [any_spec] * n_x,
        scratch_shapes=_x_sems(xchg))
    return pl.pallas_call(
        body, name="flash_bwd", grid_spec=grid_spec,
        out_shape=[jax.ShapeDtypeStruct((T, w), F32)] * 3 + _x_out_shapes(xchg),
        compiler_params=_cparams(("arbitrary", "arbitrary")),
    )(jnp.asarray(qi_tab), jnp.asarray(ki_tab), qf, kf, vf, o, do, lse, *xchg[0], *xchg[1])


def _mla_heads_bwd(d_out, saved, g_pad, cos_t, sin_t, first):
    d_raw, dg = [], jnp.zeros((1, HEAD_PAD), F32)
    for h in range(MLA_HEADS):
        xh, r = saved[h]
        dy = d_out[:, h * HEAD_PAD:(h + 1) * HEAD_PAD]
        dn = dy * cos_t + _rope_swap(dy * sin_t, first)
        dg = dg + jnp.sum(dn * xh, axis=0, keepdims=True)
        d_raw.append(_rms_bwd(dn * g_pad, xh, r, QK_DIM))
    return d_raw, dg


def _mla_prep_bwd(cq, ckv, kr, pos, dqf, dkf, dvf, g_qa, g_kva, g_qn, g_kn, w_uq, w_ukv, T, tm):
    def body(i, cq_ref, ckv_ref, kr_ref, pos_ref, dq_ref, dk_ref, dv_ref,
             gqa_ref, gkva_ref, gqn_ref, gkn_ref, wuq_ref, wukv_ref,
             dcq_ref, dckv_ref, dkr_ref, dgqa_ref, dgkva_ref, dgqn_ref, dgkn_ref, dwuq_ref, dwukv_ref):
        cos_t, sin_t, first = _rope_tables(pos_ref[...], tm)
        cqh, rq = _rms(cq_ref[...])
        ckvh, rkv = _rms(ckv_ref[...])
        cqn, ckvn = cqh * gqa_ref[...], ckvh * gkva_ref[...]
        q_raw, k_raw, _ = _mla_raw_heads(cqn, ckvn, kr_ref[...], wuq_ref, wukv_ref, tm)
        _, q_saved = _mla_heads_fwd(q_raw, gqn_ref[...], cos_t, sin_t, first)
        _, k_saved = _mla_heads_fwd(k_raw, gkn_ref[...], cos_t, sin_t, first)
        dq_heads, dgqn = _mla_heads_bwd(dq_ref[...] * ATT_SCALE, q_saved, gqn_ref[...], cos_t, sin_t, first)
        dk_heads, dgkn = _mla_heads_bwd(dk_ref[...], k_saved, gkn_ref[...], cos_t, sin_t, first)
        lane = lax.broadcasted_iota(jnp.int32, (tm, HEAD_PAD), 1)
        nope = lane < QK_NOPE
        dcqn = jnp.zeros((tm, Q_RANK), F32)
        dckvn = jnp.zeros((tm, KV_RANK), F32)
        dkr = jnp.zeros((tm, HEAD_PAD), F32)
        cqn_mm, ckvn_mm = cqn.astype(MM), ckvn.astype(MM)
        for h in range(MLA_HEADS):
            hs = slice(h * HEAD_PAD, (h + 1) * HEAD_PAD)
            dq_h = dq_heads[h].astype(MM)
            dkv_h = jnp.where(nope, dk_heads[h], pltpu.roll(dv_ref[:, hs], V_DIM, 1)).astype(MM)
            _acc(dwuq_ref.at[h], i, _dot_tn(dq_h, cqn_mm))
            _acc(dwukv_ref.at[h], i, _dot_tn(ckvn_mm, dkv_h))
            dcqn = dcqn + jnp.dot(dq_h, wuq_ref[h], preferred_element_type=F32)
            dckvn = dckvn + lax.dot_general(dkv_h, wukv_ref[h], (((1,), (1,)), ((), ())), preferred_element_type=F32)
            dkr = dkr + dk_heads[h]
        dkr_ref[...] = jnp.where((lane >= QK_NOPE) & (lane < QK_DIM), dkr, 0.0).astype(MM)
        dcq_ref[...] = _rms_bwd(dcqn * gqa_ref[...], cqh, rq).astype(MM)
        dckv_ref[...] = _rms_bwd(dckvn * gkva_ref[...], ckvh, rkv).astype(MM)
        _acc(dgqa_ref, i, jnp.sum(dcqn * cqh, axis=0, keepdims=True))
        _acc(dgkva_ref, i, jnp.sum(dckvn * ckvh, axis=0, keepdims=True))
        _acc(dgqn_ref, i, dgqn)
        _acc(dgkn_ref, i, dgkn)

    return _row_call(
        "mla_prep_bwd", body, T, tm, [cq, ckv, kr, pos, dqf, dkf, dvf], [g_qa, g_kva, g_qn, g_kn, w_uq, w_ukv],
        [(Q_RANK, MM), (KV_RANK, MM), (HEAD_PAD, MM)],
        [((1, Q_RANK), F32), ((1, KV_RANK), F32), ((1, HEAD_PAD), F32), ((1, HEAD_PAD), F32),
         ((MLA_HEADS, HEAD_PAD, Q_RANK), F32), ((MLA_HEADS, KV_RANK, HEAD_PAD), F32)], VMEM_LIMIT)


def _in_proj_bwd(x, dx1, dsecs, g_mix, w_in, T, tm):
    def body(i, x_ref, dx1_ref, *rest):
        d_refs, (g_ref, w_ref, dx_ref, dp_ref, dg_ref, dh_s) = rest[:len(SECTIONS)], rest[len(SECTIONS):]

        @pl.when(i == 0)
        def _():
            dg_ref[...] = jnp.zeros_like(dg_ref)

        g = g_ref[...]

        def join_and_cut(rows):
            pieces = [(d_ref[rows, QK_NOPE:QK_DIM] if n == QK_ROPE else d_ref[rows, :]).astype(F32)
                      for (_, n), d_ref in zip(COL_SECTIONS, d_refs)]
            dproj = jnp.concatenate(pieces, axis=1)
            for d in range(N_DEV):
                dp_ref[d, rows, :] = dproj[:, d * IN_BLOCK:(d + 1) * IN_BLOCK].astype(MM)

        _by_chunks(tm, join_and_cut)
        dh = jnp.dot(dp_ref[0], w_ref[0], preferred_element_type=F32)
        for d in range(1, N_DEV):
            dh = dh + jnp.dot(dp_ref[d], w_ref[d], preferred_element_type=F32)
        dh_s[...] = dh

        def norm_back(rows):
            xh, r = _rms(x_ref[rows, :])
            dh_c = dh_s[rows, :]
            dx_ref[rows, :] = dx1_ref[rows, :] + _rms_bwd(dh_c * g, xh, r)
            dg_ref[...] += _fold8(dh_c * xh)

        _by_chunks(tm, norm_back)

    in_specs = [pl.BlockSpec((tm, a.shape[1]), lambda i: (i, 0)) for a in [x, dx1, *dsecs]]
    in_specs += [pl.BlockSpec(g_mix.shape, lambda i: (0, 0)),
                 pl.BlockSpec(w_in.shape, lambda i: (0, 0, 0), pipeline_mode=pl.Buffered(1))]

    def kern(*refs):
        body(pl.program_id(0), *refs)

    return pl.pallas_call(
        kern, name="in_proj_bwd", grid=(T // tm,), in_specs=in_specs,
        out_specs=[pl.BlockSpec((tm, D_MODEL), lambda i: (i, 0)),
                   pl.BlockSpec((N_DEV, tm, IN_BLOCK), lambda i: (0, i, 0)),
                   pl.BlockSpec((8, D_MODEL), lambda i: (0, 0))],
        out_shape=[jax.ShapeDtypeStruct((T, D_MODEL), F32), jax.ShapeDtypeStruct((N_DEV, T, IN_BLOCK), MM),
                   jax.ShapeDtypeStruct((8, D_MODEL), F32)],
        scratch_shapes=[pltpu.VMEM((tm, D_MODEL), F32)],
        compiler_params=_cparams(("arbitrary",), VMEM_LIMIT),
    )(x, dx1, *dsecs, g_mix, w_in)


def _pick_block(n, cap):
    best = None
    for cand in range(128, min(n, cap) + 1, 128):
        if n % cand == 0:
            best = cand
    return n if best is None else best


def _pick_rows(n, cap):
    best = n
    for cand in range(8, min(n, cap) + 1, 8):
        if n % cand == 0:
            best = cand
    return best


def _matmul_tn(name, a, b):
    T, M = a.shape
    N = b.shape[1]
    bm, bk = _pick_block(M, 1408), min(512, T)
    bn = _pick_block(N, 2560)

    def body(a_ref, b_ref, c_ref):
        @pl.when(pl.program_id(2) == 0)
        def _():
            c_ref[...] = jnp.zeros_like(c_ref)

        c_ref[...] += _dot_tn(a_ref[...], b_ref[...])

    return pl.pallas_call(
        body, name=name, grid=(M // bm, N // bn, T // bk),
        in_specs=[pl.BlockSpec((bk, bm), lambda i, j, k: (k, i)), pl.BlockSpec((bk, bn), lambda i, j, k: (k, j))],
        out_specs=pl.BlockSpec((bm, bn), lambda i, j, k: (i, j)), out_shape=jax.ShapeDtypeStruct((M, N), F32),
        compiler_params=_cparams(("parallel", "parallel", "arbitrary"), VMEM_LIMIT),
    )(a, b)


def _matmul_tn_blocks(name, a, b):
    T, M = a.shape
    nd, _, c = b.shape
    bm, bk = _pick_block(M, 512), min(512, T)

    def body(a_ref, b_ref, c_ref):
        @pl.when(pl.program_id(1) == 0)
        def _():
            c_ref[...] = jnp.zeros_like(c_ref)

        a_blk = a_ref[...].astype(MM)
        for d in range(nd):
            c_ref[d] += _dot_tn(b_ref[d], a_blk)

    return pl.pallas_call(
        body, name=name, grid=(M // bm, T // bk),
        in_specs=[pl.BlockSpec((bk, bm), lambda i, k: (k, i)), pl.BlockSpec((nd, bk, c), lambda i, k: (0, k, 0))],
        out_specs=pl.BlockSpec((nd, c, bm), lambda i, k: (0, 0, i)),
        out_shape=jax.ShapeDtypeStruct((nd, c, M), F32),
        compiler_params=_cparams(("parallel", "arbitrary"), VMEM_LIMIT),
    )(a, b)


def _pad_gain(g, n):
    return jnp.pad(g.reshape(1, -1), ((0, 0), (0, n - g.shape[-1])))


GROUP_A = ("w_ffn_gate", "w_ffn_up", "w_ffn_down", "w_ple_gate", "w_ple_proj")
GROUP_B = ("w_branch", "w_out")
GROUP_C = ("w_in", "w_uq", "w_ukv")
EARLY = GROUP_C
LATE_1 = ("w_branch", "w_out", "w_ffn_gate", "w_ffn_up")
LATE_2 = ("w_ffn_down", "w_ple_gate", "w_ple_proj")
TRANSPOSED = ("w_in", "w_uq", "w_ffn_gate", "w_ffn_up")


def _local_step(x, p, pos, tgt, small, big, late_blocks=None, core=None):
    T = x.shape[0]
    tm = min(ROW_TILE, T)
    w_in = big["w_in"]
    w_uq = jnp.pad(big["w_uq"], ((0, 0), (0, HEAD_PAD - QK_DIM), (0, 0)))
    w_ukv = big["w_ukv"]

    g_mix, g_qa, g_kva = small["mix_norm_g"], small["q_a_norm_g"], small["kv_a_norm_g"]
    g_qn, g_kn = _pad_gain(small["q_norm_g"], HEAD_PAD), _pad_gain(small["k_norm_g"], HEAD_PAD)
    g_out, g_ffn = small["hg_out_norm_g"], small["ffn_norm_g"]
    g_pg, g_post = small["ple_gate_norm_g"], small["ple_post_norm_g"]
    logits = small["hg_lb_logits"]
    lb = _lower_bound(logits)

    h, cq, ckv, kr, hq, hf, hi, hg, bg = _in_proj_fwd(x, g_mix, w_in, T, tm)
    qf, kf, vf = _mla_prep_fwd(cq, ckv, kr, pos, g_qa, g_kva, g_qn, g_kn, w_uq, w_ukv, T, tm)
    if late_blocks is None:
        attn, lse = _flash_fwd(qf, kf, vf, T)
        o, s0 = _hgrn_fwd(hq, hf, hi, lb, T)
    else:
        attn, lse, *late_1 = _flash_fwd(qf, kf, vf, T, ag_blocks=[late_blocks[n] for n in LATE_1])
        o, s0, *late_2 = _hgrn_fwd(hq, hf, hi, lb, T, ag_blocks=[late_blocks[n] for n in LATE_2])
        big = {**big, **dict(zip(LATE_1, late_1)), **dict(zip(LATE_2, late_2))}
    w_branch = jnp.moveaxis(big["w_branch"].reshape(N_DEV, 2, HG_W, HEAD_PAD), 0, 2).reshape(2, HG_W, D_MODEL)
    w_bra = jnp.pad(w_branch[0].reshape(MLA_HEADS, V_DIM, D_MODEL),
                    ((0, 0), (0, HEAD_PAD - V_DIM), (0, 0))).reshape(MLA_HEADS * HEAD_PAD, D_MODEL)
    w_brb = w_branch[1]
    w_out = big["w_out"].reshape(D_MODEL, D_MODEL)
    w_g, w_u = big["w_ffn_gate"].reshape(FFN, D_MODEL), big["w_ffn_up"].reshape(FFN, D_MODEL)
    w_d = big["w_ffn_down"].reshape(FFN, D_MODEL)
    w_pg, w_pp = big["w_ple_gate"].reshape(D_MODEL, D_MODEL), big["w_ple_proj"]
    x1, ya, yb, m, rec = _merge_fwd(attn, o, hg, bg, x, g_out, w_bra, w_brb, w_out, T, tm)
    x2, gt, up, h2 = _ffn_fwd(x1, g_ffn, w_g, w_u, w_d, T, tm)
    dx2, loss_p, dg_post, dg_pg, d_pg, d_pp = _ple_loss(x2, p, tgt, g_pg, g_post, w_pg, w_pp, T, tm)

    grads, sibs, gots = {}, {}, {}
    dist = core is not None
    pick = lambda names: [grads[n] for n in names] if dist else ()

    def partials(tag, names, got):
        if not dist:
            return ()
        sibs.update(zip(names, got))
        return _chip_partials("rs_partial_" + tag, pick(names), got, core)

    dx1, a, dgt, dup, dg_ffn = _ffn_bwd(dx2, x1, gt, up, g_ffn, w_g, w_u, w_d, T, tm)
    grads["w_ffn_gate"] = _matmul_tn("dw_gate", dgt, h2).reshape(N_DEV, -1, D_MODEL)
    grads["w_ffn_up"] = _matmul_tn("dw_up", dup, h2).reshape(N_DEV, -1, D_MODEL)
    grads["w_ffn_down"] = _matmul_tn("dw_down", a, dx2).reshape(N_DEV, -1, D_MODEL)
    grads["w_ple_gate"] = d_pg.reshape(N_DEV, -1, D_MODEL)
    grads["w_ple_proj"] = d_pp

    dattn, do, dhg, dbg, dg_out, d_out, d_bra, d_brb, *sib_a = _merge_bwd(
        dx1, ya, yb, bg, o, hg, attn, m, rec, g_out, w_bra, w_brb, w_out, T, tm, xchg=(pick(GROUP_A), ()))
    parts_a = partials("a", GROUP_A, sib_a)
    d_bra = d_bra.reshape(N_DEV, MLA_HEADS, HEAD_PAD, HEAD_PAD)[:, :, :V_DIM].reshape(N_DEV, HG_W, HEAD_PAD)
    grads["w_branch"] = jnp.concatenate([d_bra, d_brb], axis=1)
    grads["w_out"] = d_out.reshape(N_DEV, -1, D_MODEL)

    dhq, dhf, dhi, dlb, *got = _hgrn_bwd(hq, hf, hi, do, s0, lb, T, xchg=(pick(GROUP_B), parts_a))
    sib_b, got_a = got[:len(GROUP_B)], got[len(GROUP_B):]
    parts_b = partials("b", GROUP_B, sib_b)
    dqf, dkf, dvf, *got_b = _flash_bwd(qf, kf, vf, attn, dattn, lse, T, xchg=((), parts_b))
    (dcq, dckv, dkr, dg_qa, dg_kva, dg_qn, dg_kn, d_uq, d_ukv) = _mla_prep_bwd(
        cq, ckv, kr, pos, dqf, dkf, dvf, g_qa, g_kva, g_qn, g_kn, w_uq, w_ukv, T, tm)
    grad_x, dproj, dg_mix = _in_proj_bwd(x, dx1, [dcq, dckv, dkr, dhq, dhf, dhi, dhg, dbg], g_mix, w_in, T, tm)
    grads["w_in"] = _matmul_tn_blocks("dw_in", h, dproj)
    grads["w_uq"] = d_uq[:, :QK_DIM]
    grads["w_ukv"] = d_ukv
    parts_c = ()
    if dist:
        parts_c = partials("c", GROUP_C, _exchange_sibling("rs_sibling_c", pick(GROUP_C)))
        gots.update(zip(GROUP_A, got_a))
        gots.update(zip(GROUP_B, got_b))

    dl0 = dlb * lb * (1.0 - lb)
    small_g = {
        "mix_norm_g": dg_mix, "q_a_norm_g": dg_qa, "kv_a_norm_g": dg_kva, "q_norm_g": dg_qn, "k_norm_g": dg_kn,
        "hg_lb_logits": jnp.concatenate([dl0, -dl0], axis=0), "hg_out_norm_g": dg_out,
        "ffn_norm_g": dg_ffn, "ple_gate_norm_g": dg_pg, "ple_post_norm_g": dg_post,
    }
    return loss_p, grad_x, small_g, grads, sibs, gots, parts_c


def _lower_bound(logits):
    def body(l_ref, lb_ref):
        l = l_ref[...]
        mx = jnp.max(l, axis=0, keepdims=True)
        e = jnp.exp(l - mx)
        lb_ref[...] = e[0:1] / jnp.sum(e, axis=0, keepdims=True)

    return pl.pallas_call(body, name="lower_bound", out_shape=jax.ShapeDtypeStruct((1, HG_W), F32))(logits)


def _my_place():
    return lax.axis_index("x"), lax.axis_index("y"), lax.axis_index("c")


def _all_gather(name, blocks):
    n = len(blocks)

    def body(*refs):
        x_refs, out_refs, sems = refs[:n], refs[n:2 * n], refs[2 * n:]
        _ag_start(x_refs, out_refs, sems)
        _ag_finish(x_refs, out_refs, sems)

    any_spec = pl.BlockSpec(memory_space=pl.ANY)
    return pl.pallas_call(
        body, name=name, out_shape=_ag_out_shapes(blocks),
        in_specs=[any_spec] * n, out_specs=[any_spec] * n, scratch_shapes=_ag_sems(n),
    )(*blocks)


def _ag_out_shapes(blocks):
    return [jax.ShapeDtypeStruct((N_DEV,) + b.shape, b.dtype) for b in blocks]


def _ag_sems(n):
    return [pltpu.SemaphoreType.DMA((7 * n,)), pltpu.SemaphoreType.DMA((7 * n,)), pltpu.SemaphoreType.DMA((n,))]


def _ag_parts(x_refs, out_refs, sems):
    send_sems, recv_sems, local_sems = sems
    x, y, c = _my_place()
    me, sibling = (x, y, c), (x, y, 1 - c)
    chips = [(1 - x, y), (x, 1 - y), (1 - x, 1 - y)]
    n = len(x_refs)

    def copy(a, k, block, to, own=False):
        px, py, pc = block
        dst = out_refs[a].at[4 * px + 2 * py + pc]
        return pltpu.make_async_remote_copy(
            src_ref=x_refs[a] if own else dst, dst_ref=dst, send_sem=send_sems.at[7 * a + k],
            recv_sem=recv_sems.at[7 * a + k], device_id=to, device_id_type=MESH_ID)

    mine = [pltpu.make_async_copy(x_refs[a], out_refs[a].at[4 * x + 2 * y + c], local_sems.at[a]) for a in range(n)]
    first = []
    for a in range(n):
        first.append(copy(a, 0, me, sibling, own=True))
        first += [copy(a, 1 + j, me, (*chip, c), own=True) for j, chip in enumerate(chips)]
    return copy, mine, first, me, sibling, chips, c, n


def _ag_start(x_refs, out_refs, sems):
    _, mine, first, *_ = _ag_parts(x_refs, out_refs, sems)
    for cp in mine + first:
        cp.start()


def _ag_finish(x_refs, out_refs, sems):
    copy, mine, first, me, sibling, chips, c, n = _ag_parts(x_refs, out_refs, sems)
    passed = []
    for j, chip in enumerate(chips):
        for a in range(n):
            copy(a, 1 + j, (*chip, c), me).wait_recv()
            passed.append(copy(a, 4 + j, (*chip, c), sibling))
            passed[-1].start()
    for a in range(n):
        copy(a, 0, sibling, me).wait_recv()
    for j, chip in enumerate(chips):
        for a in range(n):
            copy(a, 4 + j, (*chip, 1 - c), me).wait_recv()
    for cp in first + passed:
        cp.wait_send()
    for cp in mine:
        cp.wait()


def _exchange_sibling(name, gs):
    return _exchange(name, (gs, ()))


def _exchange(name, xchg):
    n = _x_count(xchg)

    def body(*refs):
        in_refs, out_refs, sems = refs[:n], refs[n:2 * n], refs[2 * n:]
        for cp in _x_copies(len(xchg[0]), in_refs, out_refs, sems):
            cp.start()
        for cp in _x_copies(len(xchg[0]), in_refs, out_refs, sems):
            cp.wait()

    any_spec = pl.BlockSpec(memory_space=pl.ANY)
    return pl.pallas_call(
        body, name=name, out_shape=_x_out_shapes(xchg), in_specs=[any_spec] * n, out_specs=[any_spec] * n,
        scratch_shapes=_x_sems(xchg),
    )(*xchg[0], *xchg[1])


N_PARTS = 4


def _part_spec(rows, cols, t_pos, lead_block=(), lead_index=lambda *args: ()):
    if rows % (16 * N_PARTS) == 0:
        axis, shape, count = 0, (rows // N_PARTS, cols), N_PARTS
    elif cols % (128 * N_PARTS) == 0:
        axis, shape, count = 1, (rows, cols // N_PARTS), N_PARTS
    else:
        axis, shape, count = 0, (rows, cols), 1

    def index(*args):
        i = jnp.minimum(args[t_pos], count - 1)
        return (*lead_index(*args), *((i, 0) if axis == 0 else (0, i)))

    return pl.BlockSpec((*lead_block, *shape), index)


def _chip_partials(name, gs, sibs, c_idx):
    n = len(gs)

    def body(c_ref, *refs):
        for g_ref, sib_ref, out_ref in zip(refs[:n], refs[n:2 * n], refs[2 * n:]):
            out_ref[...] = (g_ref[...] + sib_ref[...]).astype(MM)

    own = [_part_spec(*g.shape[1:], 1, (1,), lambda j, t, c_ref: (2 * j + c_ref[0],)) for g in gs]
    by_chip = [_part_spec(*g.shape[1:], 1, (1,), lambda j, t, c_ref: (j,)) for g in gs]
    grid_spec = pltpu.PrefetchScalarGridSpec(
        num_scalar_prefetch=1, grid=(4, N_PARTS), in_specs=own + by_chip, out_specs=by_chip)
    return pl.pallas_call(
        body, name=name, grid_spec=grid_spec, out_shape=[jax.ShapeDtypeStruct((4,) + g.shape[1:], MM) for g in gs],
        compiler_params=_cparams(("arbitrary", "arbitrary"), VMEM_LIMIT),
    )(c_idx, *gs, *sibs)


def _exchange_chips(parts):
    return _exchange("rs_chips", ((), parts))


def _x_count(xchg):
    return len(xchg[0]) + len(xchg[1])


def _x_out_shapes(xchg):
    return ([jax.ShapeDtypeStruct((4,) + g.shape[1:], g.dtype) for g in xchg[0]]
            + [jax.ShapeDtypeStruct((3,) + p.shape[1:], p.dtype) for p in xchg[1]])


def _x_sems(xchg):
    n = 4 * len(xchg[0]) + 3 * len(xchg[1])
    return [pltpu.SemaphoreType.DMA((n,)), pltpu.SemaphoreType.DMA((n,))] if n else []


def _x_copies(n_sib, in_refs, out_refs, sems):
    if not in_refs:
        return []
    send_sems, recv_sems = sems
    x, y, c = _my_place()
    chips = [(1 - x, y), (x, 1 - y), (1 - x, 1 - y)]
    copies = []

    def add(src, dst, to):
        k = len(copies)
        copies.append(pltpu.make_async_remote_copy(
            src_ref=src, dst_ref=dst, send_sem=send_sems.at[k], recv_sem=recv_sems.at[k], device_id=to,
            device_id_type=MESH_ID))

    for a, (src, dst) in enumerate(zip(in_refs, out_refs)):
        if a < n_sib:
            for j in range(4):
                add(src.at[2 * j + 1 - c], dst.at[j], (x, y, 1 - c))
        else:
            for k, (px, py) in enumerate(chips):
                add(src.at[2 * px + py], dst.at[k], (px, py, c))
    return copies


def _adamw_math(w, g, m, v):
    m = ADAM_B1 * m + (1.0 - ADAM_B1) * g
    v = ADAM_B2 * v + (1.0 - ADAM_B2) * jnp.square(g)
    m_hat = m / (1.0 - ADAM_B1 ** ADAM_STEP)
    v_hat = v / (1.0 - ADAM_B2 ** ADAM_STEP)
    delta = -ADAM_LR * (m_hat / (jnp.sqrt(v_hat) + ADAM_EPS) + ADAM_WD * w)
    return delta, m, v


def _sum_adamws(name, gs, sibs, gots, ws, ms, vs, slot_idx, chip_idx):
    n = len(gs)

    def body(s_ref, j_ref, *refs):
        ins, outs = refs[:6 * n], refs[6 * n:]
        for a in range(n):
            g_ref, sib_ref, got_ref, w_ref, m_ref, v_ref = (ins[k * n + a] for k in range(6))
            go_ref, d_ref, m2_ref, v2_ref = outs[4 * a:4 * a + 4]
            grad = g_ref[0] + sib_ref[0]
            for k in range(3):
                grad = grad + got_ref[k].astype(F32)
            go_ref[...] = grad
            d_ref[...], m2_ref[...], v2_ref[...] = _adamw_math(w_ref[...], grad, m_ref[...], v_ref[...])

    shapes = [g.shape[1:] for g in gs]
    flat = [_part_spec(*s, 0) for s in shapes]
    in_specs = ([_part_spec(*s, 0, (1,), lambda t, s_ref, j_ref: (s_ref[0],)) for s in shapes]
                + [_part_spec(*s, 0, (1,), lambda t, s_ref, j_ref: (j_ref[0],)) for s in shapes]
                + [_part_spec(*s, 0, (3,), lambda t, s_ref, j_ref: (0,)) for s in shapes] + flat * 3)
    grid_spec = pltpu.PrefetchScalarGridSpec(
        num_scalar_prefetch=2, grid=(N_PARTS,), in_specs=in_specs, out_specs=[f for f in flat for _ in range(4)])
    res = pl.pallas_call(
        body, name=name, grid_spec=grid_spec,
        out_shape=[jax.ShapeDtypeStruct(s, F32) for s in shapes for _ in range(4)],
        compiler_params=_cparams(("arbitrary",), VMEM_LIMIT),
    )(slot_idx, chip_idx, *gs, *sibs, *gots, *ws, *ms, *vs)
    return [res[4 * a:4 * a + 4] for a in range(n)]


BIG = ("w_in", "w_uq", "w_ukv", "w_branch", "w_out", "w_ffn_gate", "w_ffn_up", "w_ffn_down", "w_ple_gate", "w_ple_proj")
SMALL = (
    ("mix_norm_g", 0, 1, 1024), ("q_a_norm_g", 1, 1, 384), ("kv_a_norm_g", 2, 1, 256), ("q_norm_g", 3, 1, 96),
    ("k_norm_g", 4, 1, 96), ("hg_lb_logits", 5, 2, 512), ("hg_out_norm_g", 7, 1, 128), ("ffn_norm_g", 8, 1, 1024),
    ("ple_gate_norm_g", 9, 1, 1024), ("ple_post_norm_g", 10, 1, 1024),
)
SLAB_ROWS, LOSS_ROW = 16, 15


def _pack_partials(small_g, loss_p):
    def body(*refs):
        val_refs, loss_ref, out_ref = refs[:len(SMALL)], refs[len(SMALL)], refs[len(SMALL) + 1]
        out_ref[...] = jnp.zeros_like(out_ref)
        for (_, r0, rows, cols), ref in zip(SMALL, val_refs):
            val = ref[...]
            if val.shape[0] != rows:
                val = jnp.sum(val, axis=0, keepdims=True)
            out_ref[r0:r0 + rows, :cols] = val[:, :cols]
        out_ref[LOSS_ROW:LOSS_ROW + 1, :HEAD_PAD] = jnp.full((1, HEAD_PAD), jnp.sum(loss_ref[...]), F32)

    return pl.pallas_call(
        body, name="pack_partials", out_shape=jax.ShapeDtypeStruct((SLAB_ROWS, D_MODEL), F32),
    )(*[small_g[n] for n, *_ in SMALL], loss_p)


def _adamw_small(parts, ws, ms, vs):
    n = len(SMALL)

    def body(p_ref, *refs):
        ins, loss_ref, outs = refs[:3 * n], refs[3 * n], refs[3 * n + 1:]
        total = p_ref[0]
        for d in range(1, N_DEV):
            total = total + p_ref[d]
        loss_ref[...] = total[LOSS_ROW:LOSS_ROW + 1, 0:1]
        for a, (_, r0, rows, cols) in enumerate(SMALL):
            g = total[r0:r0 + rows, :cols]
            outs[4 * a][...] = g
            outs[4 * a + 1][...], outs[4 * a + 2][...], outs[4 * a + 3][...] = _adamw_math(
                ins[a][...], g, ins[n + a][...], ins[2 * n + a][...])

    shapes = [jax.ShapeDtypeStruct((rows, cols), F32) for _, _, rows, cols in SMALL]
    res = pl.pallas_call(
        body, name="adamw_small", out_shape=[jax.ShapeDtypeStruct((1, 1), F32)] + [s for s in shapes for _ in range(4)],
    )(parts, *ws, *ms, *vs)
    return res[0], [res[1 + 4 * a:5 + 4 * a] for a in range(n)]


_WEIGHTS = ["mix_norm_g", "w_in", "q_a_norm_g", "w_uq", "kv_a_norm_g", "w_ukv", "q_norm_g", "k_norm_g", "hg_lb_logits",
            "hg_out_norm_g", "w_branch", "w_out", "ffn_norm_g", "w_ffn_gate", "w_ffn_up", "w_ffn_down",
            "ple_gate_norm_g", "w_ple_gate", "w_ple_proj", "ple_post_norm_g"]


def _step(x, p, positions, tgt, w, m, v):
    small_names = [n for n, *_ in SMALL]
    T = x.shape[1]
    px, py, pc = _my_place()
    as_idx = lambda t: jnp.reshape(t, (1,)).astype(jnp.int32)

    def two_d(n, t):
        t = t.reshape(-1, t.shape[-1])
        return t.T if n in TRANSPOSED else t

    def full_shape(n, t):
        return (t.T if n in TRANSPOSED else t).reshape(w[n].shape)

    blocks = {n: two_d(n, w[n]).astype(MM) for n in BIG}
    big = dict(zip(EARLY, _all_gather("ag_weights", [blocks[n] for n in EARLY])))
    small = {n: (w[n] if n == "hg_lb_logits" else w[n].reshape(1, -1)) for n in small_names}

    loss_p, grad_x, small_g, grads, sibs, gots, parts_c = _local_step(
        x[0], p[0, 0], positions.reshape(T, 1), tgt[0], small, big, late_blocks=blocks, core=as_idx(pc))

    gots.update(zip(GROUP_C, _exchange_chips(parts_c)))
    out_g, out_d, out_m, out_v = {}, {}, {}, {}
    for tag, names in (("ab", GROUP_A + GROUP_B), ("c", GROUP_C)):
        pick = lambda table: [table[n] for n in names]
        res = _sum_adamws("adamw_" + tag, pick(grads), pick(sibs), pick(gots), [two_d(n, w[n]) for n in names],
                          [two_d(n, m[n]) for n in names], [two_d(n, v[n]) for n in names],
                          as_idx(4 * px + 2 * py + pc), as_idx(2 * px + py))
        for n, r in zip(names, res):
            out_g[n], out_d[n], out_m[n], out_v[n] = [full_shape(n, t) for t in r]

    parts = _all_gather("ag_small", [_pack_partials(small_g, loss_p)])[0]
    loss, res = _adamw_small(parts, *([t[n] for n in small_names] for t in (w, m, v)))
    for n, r in zip(small_names, res):
        out_g[n], out_d[n], out_m[n], out_v[n] = r

    outs = [loss.reshape(()), grad_x[None]]
    for table in (out_g, out_d, out_m, out_v):
        outs += [table[n] for n in _WEIGHTS]
    return tuple(outs)


def kernel(x, p, positions, mix_norm_g, w_in, q_a_norm_g, w_uq, kv_a_norm_g, w_ukv, q_norm_g, k_norm_g, hg_lb_logits, hg_out_norm_g, w_branch, w_out, ffn_norm_g, w_ffn_gate, w_ffn_up, w_ffn_down, ple_gate_norm_g, w_ple_gate, w_ple_proj, ple_post_norm_g, loss_target, m_mix_norm_g, m_w_in, m_q_a_norm_g, m_w_uq, m_kv_a_norm_g, m_w_ukv, m_q_norm_g, m_k_norm_g, m_hg_lb_logits, m_hg_out_norm_g, m_w_branch, m_w_out, m_ffn_norm_g, m_w_ffn_gate, m_w_ffn_up, m_w_ffn_down, m_ple_gate_norm_g, m_w_ple_gate, m_w_ple_proj, m_ple_post_norm_g, v_mix_norm_g, v_w_in, v_q_a_norm_g, v_w_uq, v_kv_a_norm_g, v_w_ukv, v_q_norm_g, v_k_norm_g, v_hg_lb_logits, v_hg_out_norm_g, v_w_branch, v_w_out, v_ffn_norm_g, v_w_ffn_gate, v_w_ffn_up, v_w_ffn_down, v_ple_gate_norm_g, v_w_ple_gate, v_w_ple_proj, v_ple_post_norm_g):
    w = dict(mix_norm_g=mix_norm_g, w_in=w_in, q_a_norm_g=q_a_norm_g, w_uq=w_uq, kv_a_norm_g=kv_a_norm_g, w_ukv=w_ukv,
             q_norm_g=q_norm_g, k_norm_g=k_norm_g, hg_lb_logits=hg_lb_logits, hg_out_norm_g=hg_out_norm_g,
             w_branch=w_branch, w_out=w_out, ffn_norm_g=ffn_norm_g, w_ffn_gate=w_ffn_gate, w_ffn_up=w_ffn_up,
             w_ffn_down=w_ffn_down, ple_gate_norm_g=ple_gate_norm_g, w_ple_gate=w_ple_gate, w_ple_proj=w_ple_proj,
             ple_post_norm_g=ple_post_norm_g)
    m = dict(mix_norm_g=m_mix_norm_g, w_in=m_w_in, q_a_norm_g=m_q_a_norm_g, w_uq=m_w_uq, kv_a_norm_g=m_kv_a_norm_g,
             w_ukv=m_w_ukv, q_norm_g=m_q_norm_g, k_norm_g=m_k_norm_g, hg_lb_logits=m_hg_lb_logits,
             hg_out_norm_g=m_hg_out_norm_g, w_branch=m_w_branch, w_out=m_w_out, ffn_norm_g=m_ffn_norm_g,
             w_ffn_gate=m_w_ffn_gate, w_ffn_up=m_w_ffn_up, w_ffn_down=m_w_ffn_down,
             ple_gate_norm_g=m_ple_gate_norm_g, w_ple_gate=m_w_ple_gate, w_ple_proj=m_w_ple_proj,
             ple_post_norm_g=m_ple_post_norm_g)
    v = dict(mix_norm_g=v_mix_norm_g, w_in=v_w_in, q_a_norm_g=v_q_a_norm_g, w_uq=v_w_uq, kv_a_norm_g=v_kv_a_norm_g,
             w_ukv=v_w_ukv, q_norm_g=v_q_norm_g, k_norm_g=v_k_norm_g, hg_lb_logits=v_hg_lb_logits,
             hg_out_norm_g=v_hg_out_norm_g, w_branch=v_w_branch, w_out=v_w_out, ffn_norm_g=v_ffn_norm_g,
             w_ffn_gate=v_w_ffn_gate, w_ffn_up=v_w_ffn_up, w_ffn_down=v_w_ffn_down,
             ple_gate_norm_g=v_ple_gate_norm_g, w_ple_gate=v_w_ple_gate, w_ple_proj=v_w_ple_proj,
             ple_post_norm_g=v_ple_post_norm_g)
    return _step(x, p, positions, loss_target, w, m, v)
```

```python
import jax
import jax.numpy as jnp
import numpy as np
from jax import lax
from jax.experimental import pallas as pl
from jax.experimental.pallas import tpu as pltpu

F32 = jnp.float32
MM = jnp.bfloat16
MESH_ID = pl.DeviceIdType.MESH

D_MODEL = 1024
N_DEV = 8
MLA_HEADS = 8
QK_NOPE = 64
QK_ROPE = 32
QK_DIM = 96
V_DIM = 64
HEAD_PAD = 128
Q_RANK = 384
KV_RANK = 256
ROPE_BASE = 10000.0
HG_HEADS = 4
HG_DIM = 128
HG_W = 512
HG_CHUNK = 64
FFN = 2816
PLE = 256
EPS = 1e-6
ATT_SCALE = QK_DIM ** -0.5
NEG = -1e30

ADAM_LR = 0.001
ADAM_B1 = 0.9
ADAM_B2 = 0.999
ADAM_EPS = 1e-08
ADAM_WD = 0.01
ADAM_STEP = 10

COL_SECTIONS = ((0, 384), (384, 256), (640, 32), (672, 512), (1184, 512), (1696, 512), (2208, 512), (2720, 2048))
STORED_WIDTHS = tuple(HEAD_PAD if n == QK_ROPE else n for _, n in COL_SECTIONS)
IN_COLS = 4768
IN_BLOCK = IN_COLS // N_DEV

VMEM_LIMIT = 58 * 1024 * 1024
ROW_TILE = 256
ATT_TILE = 1024
ATT_HEADS = 4
HG_BLOCK = 512
HG_UNROLL = 4


def _dot(a, b):
    return jnp.dot(a.astype(MM), b.astype(MM), preferred_element_type=F32)


def _dot_nt(a, b):
    return lax.dot_general(a.astype(MM), b.astype(MM), (((1,), (1,)), ((), ())), preferred_element_type=F32)


def _dot_tn(a, b):
    return lax.dot_general(a.astype(MM), b.astype(MM), (((0,), (0,)), ((), ())), preferred_element_type=F32)


def _sigmoid(x):
    return 1.0 / (1.0 + jnp.exp(-x))


def _rms(x, n=None):
    n = x.shape[-1] if n is None else n
    r = lax.rsqrt(jnp.sum(x * x, axis=-1, keepdims=True) * (1.0 / n) + EPS)
    return x * r, r


def _rms_bwd(dxh, xh, r, n=None):
    n = xh.shape[-1] if n is None else n
    return r * (dxh - xh * (jnp.sum(dxh * xh, axis=-1, keepdims=True) * (1.0 / n)))


def _rope_tables(pos, tm):
    lane = lax.broadcasted_iota(jnp.int32, (tm, HEAD_PAD), 1)
    idx = jnp.where(lane < QK_NOPE + QK_ROPE // 2, lane - QK_NOPE, lane - QK_NOPE - QK_ROPE // 2)
    inv = jnp.exp(idx.astype(F32) * (-np.log(ROPE_BASE) * 2.0 / QK_ROPE))
    ang = pos.astype(F32) * inv
    in_rope = (lane >= QK_NOPE) & (lane < QK_DIM)
    first = lane < QK_NOPE + QK_ROPE // 2
    cos_t = jnp.where(in_rope, jnp.cos(ang), 1.0)
    sin_t = jnp.where(in_rope, jnp.where(first, -jnp.sin(ang), jnp.sin(ang)), 0.0)
    return cos_t, sin_t, (first, in_rope)


def _rope_swap(x, halves):
    first, in_rope = halves
    half = QK_ROPE // 2
    return jnp.where(in_rope, jnp.where(first, pltpu.roll(x, HEAD_PAD - half, 1), pltpu.roll(x, half, 1)), 0.0)


def _cparams(sem, vmem=None):
    return pltpu.CompilerParams(dimension_semantics=sem, vmem_limit_bytes=vmem)


def _row_call(name, body, T, tm, row_ins, full_ins, row_outs, acc_outs, vmem=None, scratch=(), xchg=((), ())):
    n_in, n_out, n_x = len(row_ins) + len(full_ins), len(row_outs) + len(acc_outs), _x_count(xchg)
    steps = T // tm

    def kern(*refs):
        ins, x_in, refs = refs[:n_in], refs[n_in:n_in + n_x], refs[n_in + n_x:]
        outs, x_out, refs = refs[:n_out], refs[n_out:n_out + n_x], refs[n_out + n_x:]
        scr, x_sems = refs[:len(scratch)], refs[len(scratch):]
        i = pl.program_id(0)
        if n_x:
            @pl.when(i == 0)
            def _():
                for cp in _x_copies(len(xchg[0]), x_in, x_out, x_sems):
                    cp.start()

        body(i, *ins, *outs, *scr)
        if n_x:
            @pl.when(i == steps - 1)
            def _():
                for cp in _x_copies(len(xchg[0]), x_in, x_out, x_sems):
                    cp.wait()

    any_spec = pl.BlockSpec(memory_space=pl.ANY)
    in_specs = [pl.BlockSpec((tm, a.shape[1]), lambda i: (i, 0)) for a in row_ins]
    in_specs += [pl.BlockSpec(a.shape, lambda i, nd=a.ndim: (0,) * nd, pipeline_mode=pl.Buffered(1)) for a in full_ins]
    out_specs = [pl.BlockSpec((tm, n), lambda i: (i, 0)) for n, _ in row_outs]
    out_specs += [pl.BlockSpec(s, lambda i, nd=len(s): (0,) * nd) for s, _ in acc_outs]
    out_shape = [jax.ShapeDtypeStruct((T, n), dt) for n, dt in row_outs]
    out_shape += [jax.ShapeDtypeStruct(s, dt) for s, dt in acc_outs]
    return pl.pallas_call(
        kern, name=name, grid=(steps,), in_specs=in_specs + [any_spec] * n_x, out_specs=out_specs + [any_spec] * n_x,
        out_shape=out_shape + _x_out_shapes(xchg), scratch_shapes=list(scratch) + _x_sems(xchg),
        compiler_params=_cparams(("arbitrary",), vmem),
    )(*row_ins, *full_ins, *xchg[0], *xchg[1])


FFN_HALVES = (slice(0, FFN // 2), slice(FFN // 2, FFN))
ROW_CHUNK = 16
CHUNK_UNROLL = True


def _by_chunks(tm, fn):
    def step(c, carry):
        fn(pl.ds(pl.multiple_of(c * ROW_CHUNK, ROW_CHUNK), ROW_CHUNK))
        return carry

    lax.fori_loop(0, tm // ROW_CHUNK, step, 0, unroll=CHUNK_UNROLL)


def _fold8(x):
    return x[:8] + x[8:]


def _acc(ref, i, val):
    @pl.when(i == 0)
    def _():
        ref[...] = val

    @pl.when(i != 0)
    def _():
        ref[...] += val


def _in_proj_fwd(x, g_mix, w_in, T, tm):
    def body(i, x_ref, g_ref, w_ref, h_ref, *rest):
        outs, pj_s = rest[:-1], rest[-1]
        g = g_ref[...]

        def norm(rows):
            h_ref[rows, :] = (_rms(x_ref[rows, :])[0] * g).astype(MM)

        _by_chunks(tm, norm)
        for d in range(N_DEV):
            pj_s[d] = _dot_nt(h_ref[...], w_ref[d])

        def join_and_cut(rows):
            proj = jnp.concatenate([pj_s[d, rows, :] for d in range(N_DEV)], axis=1)
            for (s, n), o_ref in zip(COL_SECTIONS, outs):
                if n == QK_ROPE:
                    o_ref[rows, :] = jnp.concatenate(
                        [jnp.zeros((ROW_CHUNK, QK_NOPE), F32), proj[:, s:s + n],
                         jnp.zeros((ROW_CHUNK, HEAD_PAD - QK_DIM), F32)], axis=1)
                else:
                    o_ref[rows, :] = proj[:, s:s + n]

        _by_chunks(tm, join_and_cut)

    row_outs = [(D_MODEL, MM)] + [(n, F32) for n in STORED_WIDTHS]
    return _row_call("in_proj_fwd", body, T, tm, [x], [g_mix, w_in], row_outs, [], VMEM_LIMIT,
                     scratch=[pltpu.VMEM((N_DEV, tm, IN_BLOCK), F32)])


def _mla_heads_fwd(raw, g_pad, cos_t, sin_t, first):
    outs, saved = [], []
    for h in range(MLA_HEADS):
        xh, r = _rms(raw[:, h * HEAD_PAD:(h + 1) * HEAD_PAD], QK_DIM)
        y = xh * g_pad
        outs.append(y * cos_t + _rope_swap(y, first) * sin_t)
        saved.append((xh, r))
    return outs, saved


def _mla_raw_heads(cqn, ckvn, kr, wuq_ref, wukv_ref, tm):
    lane = lax.broadcasted_iota(jnp.int32, (tm, HEAD_PAD), 1)
    nope = lane < QK_NOPE
    one_lane = jnp.where(lane == V_DIM, 1.0, 0.0)
    qs, ks, vs = [], [], []
    for h in range(MLA_HEADS):
        qs.append(_dot_nt(cqn, wuq_ref[h]))
        kv = _dot(ckvn, wukv_ref[h])
        ks.append(jnp.where(nope, kv, kr))
        vs.append(jnp.where(nope, pltpu.roll(kv, V_DIM, 1), one_lane))
    return jnp.concatenate(qs, axis=1), jnp.concatenate(ks, axis=1), jnp.concatenate(vs, axis=1)


def _mla_prep_fwd(cq, ckv, kr, pos, g_qa, g_kva, g_qn, g_kn, w_uq, w_ukv, T, tm):
    def body(i, cq_ref, ckv_ref, kr_ref, pos_ref, gqa_ref, gkva_ref, gqn_ref, gkn_ref, wuq_ref, wukv_ref,
             q_ref, k_ref, v_ref):
        cos_t, sin_t, first = _rope_tables(pos_ref[...], tm)
        cqn = _rms(cq_ref[...])[0] * gqa_ref[...]
        ckvn = _rms(ckv_ref[...])[0] * gkva_ref[...]
        q_raw, k_raw, v = _mla_raw_heads(cqn, ckvn, kr_ref[...], wuq_ref, wukv_ref, tm)
        qs, _ = _mla_heads_fwd(q_raw, gqn_ref[...], cos_t, sin_t, first)
        ks, _ = _mla_heads_fwd(k_raw, gkn_ref[...], cos_t, sin_t, first)
        q_ref[...] = (jnp.concatenate(qs, axis=1) * ATT_SCALE).astype(MM)
        k_ref[...] = jnp.concatenate(ks, axis=1).astype(MM)
        v_ref[...] = v.astype(MM)

    w = MLA_HEADS * HEAD_PAD
    return _row_call("mla_prep_fwd", body, T, tm, [cq, ckv, kr, pos], [g_qa, g_kva, g_qn, g_kn, w_uq, w_ukv],
                     [(w, MM), (w, MM), (w, MM)], [])


def _causal_pairs(n, by_query):
    if by_query:
        pairs = [(q, k) for q in range(n) for k in range(q + 1)]
    else:
        pairs = [(q, k) for k in range(n) for q in range(k, n)]
    return np.array([p[0] for p in pairs], np.int32), np.array([p[1] for p in pairs], np.int32)


def _flash_fwd(qf, kf, vf, T, ag_blocks=()):
    tq = min(ATT_TILE, T)
    nq = T // tq

    qi_tab, ki_tab = _causal_pairs(nq, by_query=True)

    hp = ATT_HEADS

    n_ag = len(ag_blocks)
    n_heads, n_pairs = MLA_HEADS // hp, len(qi_tab)

    def body(qi_ref, ki_ref, q_ref, k_ref, v_ref, *rest):
        ag_in, (o_ref, lse_ref), rest = rest[:n_ag], rest[n_ag:n_ag + 2], rest[n_ag + 2:]
        ag_out, (m_s, acc_s), ag_sems = rest[:n_ag], rest[n_ag:n_ag + 2], rest[n_ag + 2:]
        t = pl.program_id(1)
        qi, ki = qi_ref[t], ki_ref[t]
        if n_ag:
            @pl.when((pl.program_id(0) == 0) & (t == 0))
            def _():
                _ag_start(ag_in, ag_out, ag_sems)

        @pl.when(ki == 0)
        def _():
            m_s[...] = jnp.full_like(m_s, NEG)
            acc_s[...] = jnp.zeros_like(acc_s)

        def step(masked):
            halves = 2 if masked and tq % (2 * HEAD_PAD) == 0 else 1
            w = tq // halves
            for hh in range(hp):
                hs = slice(hh * HEAD_PAD, (hh + 1) * HEAD_PAD)
                for part in range(halves):
                    cols, nk = slice(part * w, (part + 1) * w), (part + 1) * w
                    s_t = _dot_nt(k_ref[:nk, hs], q_ref[cols, hs])
                    if masked:
                        key = lax.broadcasted_iota(jnp.int32, (nk, w), 0)
                        qry = lax.broadcasted_iota(jnp.int32, (nk, w), 1) + part * w
                        s_t = jnp.where(key <= qry, s_t, NEG)
                    m_old = m_s[hh, :, cols]
                    m_new = jnp.maximum(m_old, jnp.max(s_t, axis=0, keepdims=True))
                    p_t = jnp.exp(s_t - m_new)
                    acc_s[hh, :, cols] = jnp.exp(m_old - m_new) * acc_s[hh, :, cols] + _dot_tn(v_ref[:nk, hs], p_t)
                    m_s[hh, :, cols] = m_new

        @pl.when(ki < qi)
        def _():
            step(False)

        @pl.when(ki == qi)
        def _():
            step(True)
            real = lax.broadcasted_iota(jnp.int32, (HEAD_PAD, tq), 0) < V_DIM
            for hh in range(hp):
                hs = slice(hh * HEAD_PAD, (hh + 1) * HEAD_PAD)
                acc = acc_s[hh]
                l = acc[V_DIM:V_DIM + 1]
                o_ref[:, hs] = jnp.where(real, acc / l, 0.0).T
                lse_ref[:, hs] = jnp.broadcast_to(m_s[hh] + jnp.log(l), (HEAD_PAD, tq)).T

        if n_ag:
            @pl.when((pl.program_id(0) == n_heads - 1) & (t == n_pairs - 1))
            def _():
                _ag_finish(ag_in, ag_out, ag_sems)

    q_spec = pl.BlockSpec((tq, hp * HEAD_PAD), lambda h, t, qi_ref, ki_ref: (qi_ref[t], h))
    kv_spec = pl.BlockSpec((tq, hp * HEAD_PAD), lambda h, t, qi_ref, ki_ref: (ki_ref[t], h))
    any_spec = pl.BlockSpec(memory_space=pl.ANY)
    grid_spec = pltpu.PrefetchScalarGridSpec(
        num_scalar_prefetch=2, grid=(n_heads, n_pairs),
        in_specs=[q_spec, kv_spec, kv_spec] + [any_spec] * n_ag, out_specs=[q_spec, q_spec] + [any_spec] * n_ag,
        scratch_shapes=[pltpu.VMEM((hp, 1, tq), F32), pltpu.VMEM((hp, HEAD_PAD, tq), F32)]
        + (_ag_sems(n_ag) if n_ag else []))
    return pl.pallas_call(
        body, name="flash_fwd", grid_spec=grid_spec,
        out_shape=[jax.ShapeDtypeStruct((T, MLA_HEADS * HEAD_PAD), F32)] * 2 + _ag_out_shapes(ag_blocks),
        compiler_params=_cparams(("arbitrary", "arbitrary")),
    )(jnp.asarray(qi_tab), jnp.asarray(ki_tab), qf, kf, vf, *ag_blocks)


def _hg_gates(hf, lb):
    sg = _sigmoid(hf)
    f = lb + (1.0 - lb) * sg
    return sg, f, jnp.log(f), 1.0 - f


def _prefix_sum(x, reverse=False):
    n = x.shape[0]
    row = lax.broadcasted_iota(jnp.int32, x.shape, 0)
    step = 1
    while step < n:
        if reverse:
            x = x + jnp.where(row < n - step, pltpu.roll(x, n - step, 0), 0.0)
        else:
            x = x + jnp.where(row >= step, pltpu.roll(x, step, 0), 0.0)
        step *= 2
    return x


def _hg_levels():
    C = HG_CHUNK
    t = lax.broadcasted_iota(jnp.int32, (C, C), 0)
    s = lax.broadcasted_iota(jnp.int32, (C, C), 1)
    levels = []
    for shift in range(C.bit_length() - 2, -1, -1):
        pair_t, pair_s = lax.shift_right_logical(t, shift + 1), lax.shift_right_logical(s, shift + 1)
        later_t = (lax.shift_right_logical(t, shift) & 1) == 1
        earlier_s = (lax.shift_right_logical(s, shift) & 1) == 0
        levels.append((1 << shift, (pair_t == pair_s) & later_t & earlier_s))
    return levels, t == s


def _hg_refs(b):
    C, n = b.shape
    row = lax.broadcasted_iota(jnp.int32, (C, n), 0)
    back1, back2, ahead1 = pltpu.roll(b, 1, 0), pltpu.roll(b, 2, 0), pltpu.roll(b, C - 1, 0)
    refs = []
    for half in (32, 16, 8, 4):
        refs.append(jnp.concatenate(
            [jnp.broadcast_to(b[lo + half - 1:lo + half], (2 * half, n)) for lo in range(0, C, 2 * half)], axis=0))
    in4 = row & 3
    refs.append(jnp.where(in4 == 0, ahead1, jnp.where(in4 == 1, b, jnp.where(in4 == 2, back1, back2))))
    refs.append(jnp.where((row & 1) == 1, back1, b))
    return refs


def _hg_intra(q, k, b, refs, levels, eye):
    a = jnp.where(eye, jnp.sum(q * k, axis=1, keepdims=True), 0.0)
    saved = []
    for r, (_, mask) in zip(refs, levels):
        e = jnp.exp(-jnp.abs(b - r))
        q_t, k_t = q * e, k * e
        a = a + jnp.where(mask, _dot_nt(q_t, k_t), 0.0)
        saved.append((q_t, k_t, e))
    return a, saved


def _hg_intra_bwd(d_a, q, k, saved, levels, eye):
    diag = jnp.sum(jnp.where(eye, d_a, 0.0), axis=1, keepdims=True)
    dq, dk = diag * k, diag * q
    for (q_t, k_t, e), (_, mask) in zip(saved, levels):
        da = jnp.where(mask, d_a, 0.0)
        dq = dq + _dot(da, k_t) * e
        dk = dk + _dot_tn(da, q_t) * e
    return dq, dk


def _hgrn_fwd(hq, hf, hi, lb, T):
    rb = min(HG_BLOCK, T)
    ncb = rb // HG_CHUNK

    def body(hq_ref, hf_ref, hi_ref, lb_ref, o_ref, s0_ref, st_ref):
        @pl.when(pl.program_id(0) == 0)
        def _():
            st_ref[...] = jnp.zeros_like(st_ref)

        levels, eye = _hg_levels()

        def chunk(c, carry):
            rows = pl.ds(pl.multiple_of(c * HG_CHUNK, HG_CHUNK), HG_CHUNK)
            _, _, logf, kk = _hg_gates(hf_ref[rows, :], lb_ref[...])
            b = _prefix_sum(logf)
            refs = _hg_refs(b)
            q_all, v_all = hq_ref[rows, :], hi_ref[rows, :]
            outs = []
            for h in range(HG_HEADS):
                ls = slice(h * HG_DIM, (h + 1) * HG_DIM)
                q, k, v, bh = q_all[:, ls], kk[:, ls], v_all[:, ls], b[:, ls]
                st = st_ref[h]
                s0_ref[c, h * HG_DIM:(h + 1) * HG_DIM, :] = st
                b_end = bh[HG_CHUNK - 1:HG_CHUNK]
                a, _ = _hg_intra(q, k, bh, [r[:, ls] for r in refs], levels, eye)
                outs.append(_dot_nt(q * jnp.exp(bh), st) + _dot(a, v))
                st_ref[h] = st * jnp.exp(b_end) + _dot_tn(v, k * jnp.exp(b_end - bh))
            o_ref[rows, :] = jnp.concatenate(outs, axis=1)
            return carry

        lax.fori_loop(0, ncb, chunk, 0, unroll=HG_UNROLL)

    row = pl.BlockSpec((rb, HG_W), lambda i: (i, 0))
    return pl.pallas_call(
        body, name="hgrn_fwd", grid=(T // rb,),
        in_specs=[row, row, row, pl.BlockSpec((1, HG_W), lambda i: (0, 0))],
        out_specs=[row, pl.BlockSpec((ncb, HG_W, HG_DIM), lambda i: (i, 0, 0))],
        out_shape=[jax.ShapeDtypeStruct((T, HG_W), F32), jax.ShapeDtypeStruct((T // HG_CHUNK, HG_W, HG_DIM), F32)],
        scratch_shapes=[pltpu.VMEM((HG_HEADS, HG_DIM, HG_DIM), F32)],
        compiler_params=_cparams(("arbitrary",)),
    )(hq, hf, hi, lb)


def _hgrn_bwd(hq, hf, hi, do, s0, lb, T, xchg=((), ())):
    rb = min(HG_BLOCK, T)
    ncb = rb // HG_CHUNK
    nb = T // rb
    C = HG_CHUNK
    n_x, n_sib = _x_count(xchg), len(xchg[0])

    def body(hq_ref, hf_ref, hi_ref, do_ref, s0_ref, lb_ref, *rest):
        x_in, (dq_ref, df_ref, dv_ref, dlb_ref), rest = rest[:n_x], rest[n_x:n_x + 4], rest[n_x + 4:]
        x_out, dst_ref, x_sems = rest[:n_x], rest[n_x], rest[n_x + 1:]

        @pl.when(pl.program_id(0) == 0)
        def _():
            dst_ref[...] = jnp.zeros_like(dst_ref)
            dlb_ref[...] = jnp.zeros_like(dlb_ref)
            for cp in _x_copies(n_sib, x_in, x_out, x_sems):
                cp.start()

        row_cc = lax.broadcasted_iota(jnp.int32, (C, C), 0)
        col_cc = lax.broadcasted_iota(jnp.int32, (C, C), 1)
        last_row = lax.broadcasted_iota(jnp.int32, (C, HG_DIM), 0) == C - 1
        lb_v = lb_ref[...]
        levels, eye = _hg_levels()

        def chunk(cc, carry):
            c = ncb - 1 - cc
            rows = pl.ds(pl.multiple_of(c * C, C), C)
            hf_c = hf_ref[rows, :]
            sg, f, logf, kk = _hg_gates(hf_c, lb_v)
            b = _prefix_sum(logf)
            refs = _hg_refs(b)
            q_all, v_all, do_all = hq_ref[rows, :], hi_ref[rows, :], do_ref[rows, :]
            dq_o, dk_o, dv_o, db_o = [], [], [], []
            for h in range(HG_HEADS):
                ls = slice(h * HG_DIM, (h + 1) * HG_DIM)
                q, k, v, bh, d_o = q_all[:, ls], kk[:, ls], v_all[:, ls], b[:, ls], do_all[:, ls]
                st0 = s0_ref[c, h * HG_DIM:(h + 1) * HG_DIM, :]
                dst = dst_ref[h]
                b_end = bh[C - 1:C]
                e_b, e_end = jnp.exp(bh), jnp.exp(b_end)
                e_rem = jnp.exp(b_end - bh)
                qe, kd = q * e_b, k * e_rem
                st_end = st0 * e_end + _dot_tn(v, kd)
                a, saved = _hg_intra(q, k, bh, [r[:, ls] for r in refs], levels, eye)
                d_a = jnp.where(col_cc <= row_cc, _dot_nt(d_o, v), 0.0)
                dq_i, dk_i = _hg_intra_bwd(d_a, q, k, saved, levels, eye)
                dv = _dot_tn(a, d_o) + _dot_nt(kd, dst)
                dq = e_b * _dot(d_o, st0) + dq_i
                dk = e_rem * _dot(v, dst) + dk_i
                extra = jnp.sum(dst * st_end, axis=0, keepdims=True)
                db_o.append(q * dq - k * dk + jnp.where(last_row, extra, 0.0))
                dst_ref[h] = dst * e_end + _dot_tn(d_o, qe)
                dq_o.append(dq)
                dk_o.append(dk)
                dv_o.append(dv)
            dlogf = _prefix_sum(jnp.concatenate(db_o, axis=1), reverse=True)
            d_f = dlogf / f - jnp.concatenate(dk_o, axis=1)
            dq_ref[rows, :] = jnp.concatenate(dq_o, axis=1).astype(MM)
            dv_ref[rows, :] = jnp.concatenate(dv_o, axis=1).astype(MM)
            df_ref[rows, :] = (d_f * (1.0 - lb_v) * sg * (1.0 - sg)).astype(MM)
            dlb_ref[...] += jnp.sum(d_f * (1.0 - sg), axis=0, keepdims=True)
            return carry

        lax.fori_loop(0, ncb, chunk, 0, unroll=HG_UNROLL)

        if n_x:
            @pl.when(pl.program_id(0) == nb - 1)
            def _():
                for cp in _x_copies(n_sib, x_in, x_out, x_sems):
                    cp.wait()

    row = pl.BlockSpec((rb, HG_W), lambda i: (nb - 1 - i, 0))
    one = pl.BlockSpec((1, HG_W), lambda i: (0, 0))
    any_spec = pl.BlockSpec(memory_space=pl.ANY)
    return pl.pallas_call(
        body, name="hgrn_bwd", grid=(nb,),
        in_specs=[row, row, row, row, pl.BlockSpec((ncb, HG_W, HG_DIM), lambda i: (nb - 1 - i, 0, 0)), one]
        + [any_spec] * n_x,
        out_specs=[row, row, row, one] + [any_spec] * n_x,
        out_shape=[jax.ShapeDtypeStruct((T, HG_W), MM)] * 3 + [jax.ShapeDtypeStruct((1, HG_W), F32)]
        + _x_out_shapes(xchg),
        scratch_shapes=[pltpu.VMEM((HG_HEADS, HG_DIM, HG_DIM), F32)] + _x_sems(xchg),
        compiler_params=_cparams(("arbitrary",)),
    )(hq, hf, hi, do, s0, lb, *xchg[0], *xchg[1])


def _silu_parts(x):
    sg = _sigmoid(x)
    return x * sg, sg * (1.0 + x * (1.0 - sg))


def _merge_fwd(attn, o, hg, bg, x, g_out, w_bra, w_brb, w_out, T, tm):
    def body(i, attn_ref, o_ref, hg_ref, bg_ref, x_ref, g_ref, wa_ref, wb_ref, wo_ref,
             x1_ref, ya_ref, yb_ref, m_ref, rec_ref):
        g = g_ref[...]

        def recurrent_out(rows):
            for h in range(HG_HEADS):
                ls = slice(h * HG_DIM, (h + 1) * HG_DIM)
                rec_ref[rows, ls] = (_rms(o_ref[rows, ls])[0] * g * _silu_parts(hg_ref[rows, ls])[0]).astype(MM)

        _by_chunks(tm, recurrent_out)
        ya_ref[...] = _dot(attn_ref[...], wa_ref[...])
        yb_ref[...] = jnp.dot(rec_ref[...], wb_ref[...], preferred_element_type=F32)

        def gate(rows):
            m_ref[rows, :] = (_sigmoid(bg_ref[rows, :D_MODEL]) * ya_ref[rows, :]
                              + _sigmoid(bg_ref[rows, D_MODEL:]) * yb_ref[rows, :]).astype(MM)

        _by_chunks(tm, gate)
        x1_ref[...] = x_ref[...] + jnp.dot(m_ref[...], wo_ref[...], preferred_element_type=F32)

    return _row_call("merge_fwd", body, T, tm, [attn, o, hg, bg, x], [g_out, w_bra, w_brb, w_out],
                     [(D_MODEL, F32), (D_MODEL, F32), (D_MODEL, F32), (D_MODEL, MM), (HG_W, MM)], [], VMEM_LIMIT)


def _ffn_fwd(x1, g_ffn, w_g, w_u, w_d, T, tm):
    def body(i, x1_ref, g_ref, wg_ref, wu_ref, wd_ref, x2_ref, gt_ref, up_ref, h2_ref, a_s):
        g = g_ref[...]

        def norm(rows):
            h2_ref[rows, :] = (_rms(x1_ref[rows, :])[0] * g).astype(MM)

        _by_chunks(tm, norm)
        gt_ref[...] = _dot_nt(h2_ref[...], wg_ref[...])
        up_ref[...] = _dot_nt(h2_ref[...], wu_ref[...])

        def act(rows):
            for cs in FFN_HALVES:
                a_s[rows, cs] = (_silu_parts(gt_ref[rows, cs])[0] * up_ref[rows, cs]).astype(MM)

        _by_chunks(tm, act)
        x2_ref[...] = x1_ref[...] + jnp.dot(a_s[...], wd_ref[...], preferred_element_type=F32)

    return _row_call("ffn_fwd", body, T, tm, [x1], [g_ffn, w_g, w_u, w_d],
                     [(D_MODEL, F32), (FFN, F32), (FFN, F32), (D_MODEL, MM)], [], VMEM_LIMIT,
                     scratch=[pltpu.VMEM((tm, FFN), MM)])


def _ple_loss(x2, p, tgt, g_pg, g_post, w_pg, w_pp, T, tm):
    def body(i, x2_ref, p_ref, t_ref, gpg_ref, gpo_ref, wpg_ref, wpp_ref,
             dx2_ref, loss_ref, dgpo_ref, dgpg_ref, dwpg_ref, dwpp_ref, u_s, n3_s, z_s, dz_s, du_s, dy_s, dn3_s):
        @pl.when(i == 0)
        def _():
            for ref in (loss_ref, dgpo_ref, dgpg_ref, dwpg_ref, dwpp_ref):
                ref[...] = jnp.zeros_like(ref)

        gpg, gpo = gpg_ref[...], gpo_ref[...]
        p_mm = p_ref[...].astype(MM)
        for d in range(N_DEV):
            u_s[:, d * HEAD_PAD:(d + 1) * HEAD_PAD] = jnp.dot(p_mm, wpp_ref[d], preferred_element_type=F32)

        def gate_input(rows):
            n3_s[rows, :] = (_rms(x2_ref[rows, :])[0] * gpg).astype(MM)

        _by_chunks(tm, gate_input)
        z_s[...] = jnp.dot(n3_s[...], wpg_ref[...], preferred_element_type=F32)

        def loss_and_back(rows):
            uh, ru = _rms(u_s[rows, :])
            e = uh * gpo
            gate = _sigmoid(z_s[rows, :])
            diff = x2_ref[rows, :] + gate * e - t_ref[rows, :]
            dy = diff * (1.0 / D_MODEL)
            de = dy * gate
            dz_s[rows, :] = (dy * e * gate * (1.0 - gate)).astype(MM)
            du_s[rows, :] = _rms_bwd(de * gpo, uh, ru).astype(MM)
            dy_s[rows, :] = dy
            loss_ref[...] += _fold8(diff * diff) * (0.5 / D_MODEL)
            dgpo_ref[...] += _fold8(de * uh)

        _by_chunks(tm, loss_and_back)
        dn3_s[...] = _dot_nt(dz_s[...], wpg_ref[...])

        def gate_norm_back(rows):
            x2h, r3 = _rms(x2_ref[rows, :])
            dn3 = dn3_s[rows, :]
            dx2_ref[rows, :] = dy_s[rows, :] + _rms_bwd(dn3 * gpg, x2h, r3)
            dgpg_ref[...] += _fold8(dn3 * x2h)

        _by_chunks(tm, gate_norm_back)
        dwpg_ref[...] += _dot_tn(n3_s[...], dz_s[...])
        for d in range(N_DEV):
            dwpp_ref[d] += _dot_tn(p_mm, du_s[:, d * HEAD_PAD:(d + 1) * HEAD_PAD])

    vec = ((8, D_MODEL), F32)
    wide = lambda dt: pltpu.VMEM((tm, D_MODEL), dt)
    return _row_call("ple_loss", body, T, tm, [x2, p, tgt], [g_pg, g_post, w_pg, w_pp], [(D_MODEL, F32)],
                     [vec, vec, vec, ((D_MODEL, D_MODEL), F32), ((N_DEV, PLE, HEAD_PAD), F32)], VMEM_LIMIT,
                     scratch=[wide(F32), wide(MM), wide(F32), wide(MM), wide(MM), wide(F32), wide(F32)])


def _ffn_bwd(dx2, x1, gt, up, g_ffn, w_g, w_u, w_d, T, tm):
    def body(i, dx2_ref, x1_ref, gt_ref, up_ref, g_ref, wg_ref, wu_ref, wd_ref,
             dx1_ref, a_ref, dgt_ref, dup_ref, dg_ref, da_s, dh2_s):
        @pl.when(i == 0)
        def _():
            dg_ref[...] = jnp.zeros_like(dg_ref)

        g = g_ref[...]
        da_s[...] = _dot_nt(dx2_ref[...], wd_ref[...])

        def act_back(rows):
            for cs in FFN_HALVES:
                up, da = up_ref[rows, cs], da_s[rows, cs]
                silu, dsilu = _silu_parts(gt_ref[rows, cs])
                dgt_ref[rows, cs] = (da * up * dsilu).astype(MM)
                dup_ref[rows, cs] = (da * silu).astype(MM)
                a_ref[rows, cs] = (silu * up).astype(MM)

        _by_chunks(tm, act_back)
        dh2_s[...] = (jnp.dot(dgt_ref[...], wg_ref[...], preferred_element_type=F32)
                      + jnp.dot(dup_ref[...], wu_ref[...], preferred_element_type=F32))

        def norm_back(rows):
            x1h, r = _rms(x1_ref[rows, :])
            dh2 = dh2_s[rows, :]
            dx1_ref[rows, :] = dx2_ref[rows, :] + _rms_bwd(dh2 * g, x1h, r)
            dg_ref[...] += _fold8(dh2 * x1h)

        _by_chunks(tm, norm_back)

    return _row_call("ffn_bwd", body, T, tm, [dx2, x1, gt, up], [g_ffn, w_g, w_u, w_d],
                     [(D_MODEL, F32), (FFN, MM), (FFN, MM), (FFN, MM)], [((8, D_MODEL), F32)], VMEM_LIMIT,
                     scratch=[pltpu.VMEM((tm, FFN), F32), pltpu.VMEM((tm, D_MODEL), F32)])


def _merge_bwd(dx1, ya, yb, bg, o, hg, attn, m, rec, g_out, w_bra, w_brb, w_out, T, tm, xchg=((), ())):
    def body(i, dx1_ref, ya_ref, yb_ref, bg_ref, o_ref, hg_ref, attn_ref, m_ref, rec_ref, g_ref, wa_ref, wb_ref, wo_ref,
             dattn_ref, do_ref, dhg_ref, dbg_ref, dg_ref, dwo_ref, dwa_ref, dwb_ref, dm_s, dya_s, dyb_s, drec_s):
        @pl.when(i == 0)
        def _():
            for ref in (dg_ref, dwo_ref, dwa_ref, dwb_ref):
                ref[...] = jnp.zeros_like(ref)

        g = g_ref[...]
        dx1 = dx1_ref[...].astype(MM)
        dm_s[...] = _dot_nt(dx1, wo_ref[...])

        def gate_back(rows):
            dm = dm_s[rows, :]
            ga, gb = _sigmoid(bg_ref[rows, :D_MODEL]), _sigmoid(bg_ref[rows, D_MODEL:])
            dya_s[rows, :] = (dm * ga).astype(MM)
            dyb_s[rows, :] = (dm * gb).astype(MM)
            dbg_ref[rows, :D_MODEL] = (dm * ya_ref[rows, :] * ga * (1.0 - ga)).astype(MM)
            dbg_ref[rows, D_MODEL:] = (dm * yb_ref[rows, :] * gb * (1.0 - gb)).astype(MM)

        _by_chunks(tm, gate_back)
        dwo_ref[...] += _dot_tn(m_ref[...], dx1)
        attn_mm = attn_ref[...].astype(MM)
        for d in range(N_DEV):
            ds = slice(d * HEAD_PAD, (d + 1) * HEAD_PAD)
            dwa_ref[d] += _dot_tn(attn_mm, dya_s[:, ds])
            dwb_ref[d] += _dot_tn(rec_ref[...], dyb_s[:, ds])
        dattn_ref[...] = _dot_nt(dya_s[...], wa_ref[...])
        drec_s[...] = _dot_nt(dyb_s[...], wb_ref[...])

        def recurrent_out_back(rows):
            for h in range(HG_HEADS):
                ls = slice(h * HG_DIM, (h + 1) * HG_DIM)
                oh, r = _rms(o_ref[rows, ls])
                silu, dsilu = _silu_parts(hg_ref[rows, ls])
                dr = drec_s[rows, ls]
                dhg_ref[rows, ls] = (dr * oh * g * dsilu).astype(MM)
                don = dr * silu
                dg_ref[...] += _fold8(don * oh)
                do_ref[rows, ls] = _rms_bwd(don * g, oh, r)

        _by_chunks(tm, recurrent_out_back)

    wide = lambda n, dt: pltpu.VMEM((tm, n), dt)
    return _row_call("merge_bwd", body, T, tm, [dx1, ya, yb, bg, o, hg, attn, m, rec], [g_out, w_bra, w_brb, w_out],
                     [(D_MODEL, F32), (HG_W, F32), (HG_W, MM), (2 * D_MODEL, MM)],
                     [((8, HG_DIM), F32), ((D_MODEL, D_MODEL), F32), ((N_DEV, MLA_HEADS * HEAD_PAD, HEAD_PAD), F32),
                      ((N_DEV, HG_W, HEAD_PAD), F32)], VMEM_LIMIT,
                     scratch=[wide(D_MODEL, F32), wide(D_MODEL, MM), wide(D_MODEL, MM), wide(HG_W, F32)], xchg=xchg)


def _flash_bwd(qf, kf, vf, o, do, lse, T, xchg=((), ())):
    tq = min(ATT_TILE, T)
    nq = T // tq

    qi_tab, ki_tab = _causal_pairs(nq, by_query=False)

    n_x, n_sib = _x_count(xchg), len(xchg[0])
    hp = ATT_HEADS
    n_heads, n_pairs = MLA_HEADS // hp, len(qi_tab)

    def body(qi_ref, ki_ref, q_ref, k_ref, v_ref, o_ref, do_ref, lse_ref, *rest):
        x_in, (dq_ref, dk_ref, dv_ref), rest = rest[:n_x], rest[n_x:n_x + 3], rest[n_x + 3:]
        x_out, x_sems = rest[:n_x], rest[n_x:]
        t = pl.program_id(1)
        qi, ki = qi_ref[t], ki_ref[t]
        if n_x:
            @pl.when((pl.program_id(0) == 0) & (t == 0))
            def _():
                for cp in _x_copies(n_sib, x_in, x_out, x_sems):
                    cp.start()

        @pl.when(t == 0)
        def _():
            dq_ref[...] = jnp.zeros_like(dq_ref)

        def step(first):
            halves = 2 if first and tq % (2 * HEAD_PAD) == 0 else 1
            w = tq // halves
            for hh in range(hp):
                hs = slice(hh * HEAD_PAD, (hh + 1) * HEAD_PAD)
                for part in range(halves):
                    keys, qs = slice(part * w, (part + 1) * w), slice(part * w, tq)
                    nq_ = tq - part * w
                    q, k, d_o = q_ref[qs, hs], k_ref[keys, hs], do_ref[qs, hs]
                    s = _dot_nt(q, k)
                    if first:
                        row = lax.broadcasted_iota(jnp.int32, (nq_, w), 0)
                        col = lax.broadcasted_iota(jnp.int32, (nq_, w), 1)
                        s = jnp.where(col <= row, s, NEG)
                    p = jnp.exp(s - lse_ref[qs, hh * HEAD_PAD:hh * HEAD_PAD + 1])
                    delta = jnp.sum(d_o * o_ref[qs, hs], axis=1, keepdims=True)
                    ds = p * (_dot_nt(d_o, v_ref[keys, hs]) - delta)
                    rows = pl.ds(pl.multiple_of(qi * tq + part * w, w), nq_)
                    dq_ref[rows, hs] += _dot(ds, k)
                    if first:
                        dv_ref[keys, hs] = _dot_tn(p, d_o)
                        dk_ref[keys, hs] = _dot_tn(ds, q)
                    else:
                        dv_ref[keys, hs] += _dot_tn(p, d_o)
                        dk_ref[keys, hs] += _dot_tn(ds, q)

        @pl.when(qi == ki)
        def _():
            step(True)

        @pl.when(qi > ki)
        def _():
            step(False)

        if n_x:
            @pl.when((pl.program_id(0) == n_heads - 1) & (t == n_pairs - 1))
            def _():
                for cp in _x_copies(n_sib, x_in, x_out, x_sems):
                    cp.wait()

    q_spec = pl.BlockSpec((tq, hp * HEAD_PAD), lambda h, t, qi_ref, ki_ref: (qi_ref[t], h))
    kv_spec = pl.BlockSpec((tq, hp * HEAD_PAD), lambda h, t, qi_ref, ki_ref: (ki_ref[t], h))
    any_spec = pl.BlockSpec(memory_space=pl.ANY)
    w = MLA_HEADS * HEAD_PAD
    grid_spec = pltpu.PrefetchScalarGridSpec(
        num_scalar_prefetch=2, grid=(n_heads, n_pairs),
        in_specs=[q_spec, kv_spec, kv_spec, q_spec, q_spec, q_spec] + [any_spec] * n_x,
        out_specs=[pl.BlockSpec((T, hp * HEAD_PAD), lambda h, t, qi_ref, ki_ref: (0, h)), kv_spec, kv_spec]
        + [any_spec] * n_x,
        scratch_shapes=_x_sems(xchg))
    return pl.pallas_call(
        body, name="flash_bwd", grid_spec=grid_spec,
        out_shape=[jax.ShapeDtypeStruct((T, w), F32)] * 3 + _x_out_shapes(xchg),
        compiler_params=_cparams(("arbitrary", "arbitrary")),
    )(jnp.asarray(qi_tab), jnp.asarray(ki_tab), qf, kf, vf, o, do, lse, *xchg[0], *xchg[1])


def _mla_heads_bwd(d_out, saved, g_pad, cos_t, sin_t, first):
    d_raw, dg = [], jnp.zeros((1, HEAD_PAD), F32)
    for h in range(MLA_HEADS):
        xh, r = saved[h]
        dy = d_out[:, h * HEAD_PAD:(h + 1) * HEAD_PAD]
        dn = dy * cos_t + _rope_swap(dy * sin_t, first)
        dg = dg + jnp.sum(dn * xh, axis=0, keepdims=True)
        d_raw.append(_rms_bwd(dn * g_pad, xh, r, QK_DIM))
    return d_raw, dg


def _mla_prep_bwd(cq, ckv, kr, pos, dqf, dkf, dvf, g_qa, g_kva, g_qn, g_kn, w_uq, w_ukv, T, tm):
    def body(i, cq_ref, ckv_ref, kr_ref, pos_ref, dq_ref, dk_ref, dv_ref,
             gqa_ref, gkva_ref, gqn_ref, gkn_ref, wuq_ref, wukv_ref,
             dcq_ref, dckv_ref, dkr_ref, dgqa_ref, dgkva_ref, dgqn_ref, dgkn_ref, dwuq_ref, dwukv_ref):
        cos_t, sin_t, first = _rope_tables(pos_ref[...], tm)
        cqh, rq = _rms(cq_ref[...])
        ckvh, rkv = _rms(ckv_ref[...])
        cqn, ckvn = cqh * gqa_ref[...], ckvh * gkva_ref[...]
        q_raw, k_raw, _ = _mla_raw_heads(cqn, ckvn, kr_ref[...], wuq_ref, wukv_ref, tm)
        _, q_saved = _mla_heads_fwd(q_raw, gqn_ref[...], cos_t, sin_t, first)
        _, k_saved = _mla_heads_fwd(k_raw, gkn_ref[...], cos_t, sin_t, first)
        dq_heads, dgqn = _mla_heads_bwd(dq_ref[...] * ATT_SCALE, q_saved, gqn_ref[...], cos_t, sin_t, first)
        dk_heads, dgkn = _mla_heads_bwd(dk_ref[...], k_saved, gkn_ref[...], cos_t, sin_t, first)
        lane = lax.broadcasted_iota(jnp.int32, (tm, HEAD_PAD), 1)
        nope = lane < QK_NOPE
        dcqn = jnp.zeros((tm, Q_RANK), F32)
        dckvn = jnp.zeros((tm, KV_RANK), F32)
        dkr = jnp.zeros((tm, HEAD_PAD), F32)
        cqn_mm, ckvn_mm = cqn.astype(MM), ckvn.astype(MM)
        for h in range(MLA_HEADS):
            hs = slice(h * HEAD_PAD, (h + 1) * HEAD_PAD)
            dq_h = dq_heads[h].astype(MM)
            dkv_h = jnp.where(nope, dk_heads[h], pltpu.roll(dv_ref[:, hs], V_DIM, 1)).astype(MM)
            _acc(dwuq_ref.at[h], i, _dot_tn(dq_h, cqn_mm))
            _acc(dwukv_ref.at[h], i, _dot_tn(ckvn_mm, dkv_h))
            dcqn = dcqn + jnp.dot(dq_h, wuq_ref[h], preferred_element_type=F32)
            dckvn = dckvn + lax.dot_general(dkv_h, wukv_ref[h], (((1,), (1,)), ((), ())), preferred_element_type=F32)
            dkr = dkr + dk_heads[h]
        dkr_ref[...] = jnp.where((lane >= QK_NOPE) & (lane < QK_DIM), dkr, 0.0).astype(MM)
        dcq_ref[...] = _rms_bwd(dcqn * gqa_ref[...], cqh, rq).astype(MM)
        dckv_ref[...] = _rms_bwd(dckvn * gkva_ref[...], ckvh, rkv).astype(MM)
        _acc(dgqa_ref, i, jnp.sum(dcqn * cqh, axis=0, keepdims=True))
        _acc(dgkva_ref, i, jnp.sum(dckvn * ckvh, axis=0, keepdims=True))
        _acc(dgqn_ref, i, dgqn)
        _acc(dgkn_ref, i, dgkn)

    return _row_call(
        "mla_prep_bwd", body, T, tm, [cq, ckv, kr, pos, dqf, dkf, dvf], [g_qa, g_kva, g_qn, g_kn, w_uq, w_ukv],
        [(Q_RANK, MM), (KV_RANK, MM), (HEAD_PAD, MM)],
        [((1, Q_RANK), F32), ((1, KV_RANK), F32), ((1, HEAD_PAD), F32), ((1, HEAD_PAD), F32),
         ((MLA_HEADS, HEAD_PAD, Q_RANK), F32), ((MLA_HEADS, KV_RANK, HEAD_PAD), F32)], VMEM_LIMIT)


def _in_proj_bwd(x, dx1, dsecs, g_mix, w_in, T, tm):
    def body(i, x_ref, dx1_ref, *rest):
        d_refs, (g_ref, w_ref, dx_ref, dp_ref, dg_ref, dh_s) = rest[:len(COL_SECTIONS)], rest[len(COL_SECTIONS):]

        @pl.when(i == 0)
        def _():
            dg_ref[...] = jnp.zeros_like(dg_ref)

        g = g_ref[...]

        def join_and_cut(rows):
            pieces = [(d_ref[rows, QK_NOPE:QK_DIM] if n == QK_ROPE else d_ref[rows, :]).astype(F32)
                      for (_, n), d_ref in zip(COL_SECTIONS, d_refs)]
            dproj = jnp.concatenate(pieces, axis=1)
            for d in range(N_DEV):
                dp_ref[d, rows, :] = dproj[:, d * IN_BLOCK:(d + 1) * IN_BLOCK].astype(MM)

        _by_chunks(tm, join_and_cut)
        dh = jnp.dot(dp_ref[0], w_ref[0], preferred_element_type=F32)
        for d in range(1, N_DEV):
            dh = dh + jnp.dot(dp_ref[d], w_ref[d], preferred_element_type=F32)
        dh_s[...] = dh

        def norm_back(rows):
            xh, r = _rms(x_ref[rows, :])
            dh_c = dh_s[rows, :]
            dx_ref[rows, :] = dx1_ref[rows, :] + _rms_bwd(dh_c * g, xh, r)
            dg_ref[...] += _fold8(dh_c * xh)

        _by_chunks(tm, norm_back)

    in_specs = [pl.BlockSpec((tm, a.shape[1]), lambda i: (i, 0)) for a in [x, dx1, *dsecs]]
    in_specs += [pl.BlockSpec(g_mix.shape, lambda i: (0, 0)),
                 pl.BlockSpec(w_in.shape, lambda i: (0, 0, 0), pipeline_mode=pl.Buffered(1))]

    def kern(*refs):
        body(pl.program_id(0), *refs)

    return pl.pallas_call(
        kern, name="in_proj_bwd", grid=(T // tm,), in_specs=in_specs,
        out_specs=[pl.BlockSpec((tm, D_MODEL), lambda i: (i, 0)),
                   pl.BlockSpec((N_DEV, tm, IN_BLOCK), lambda i: (0, i, 0)),
                   pl.BlockSpec((8, D_MODEL), lambda i: (0, 0))],
        out_shape=[jax.ShapeDtypeStruct((T, D_MODEL), F32), jax.ShapeDtypeStruct((N_DEV, T, IN_BLOCK), MM),
                   jax.ShapeDtypeStruct((8, D_MODEL), F32)],
        scratch_shapes=[pltpu.VMEM((tm, D_MODEL), F32)],
        compiler_params=_cparams(("arbitrary",), VMEM_LIMIT),
    )(x, dx1, *dsecs, g_mix, w_in)


def _pick_block(n, cap):
    best = None
    for cand in range(128, min(n, cap) + 1, 128):
        if n % cand == 0:
            best = cand
    return n if best is None else best


def _matmul_tn(name, a, b):
    T, M = a.shape
    N = b.shape[1]
    bm, bk = _pick_block(M, 1408), min(512, T)
    bn = _pick_block(N, 2560)

    def body(a_ref, b_ref, c_ref):
        @pl.when(pl.program_id(2) == 0)
        def _():
            c_ref[...] = jnp.zeros_like(c_ref)

        c_ref[...] += _dot_tn(a_ref[...], b_ref[...])

    return pl.pallas_call(
        body, name=name, grid=(M // bm, N // bn, T // bk),
        in_specs=[pl.BlockSpec((bk, bm), lambda i, j, k: (k, i)), pl.BlockSpec((bk, bn), lambda i, j, k: (k, j))],
        out_specs=pl.BlockSpec((bm, bn), lambda i, j, k: (i, j)), out_shape=jax.ShapeDtypeStruct((M, N), F32),
        compiler_params=_cparams(("parallel", "parallel", "arbitrary"), VMEM_LIMIT),
    )(a, b)


def _matmul_tn_blocks(name, a, b):
    T, M = a.shape
    nd, _, c = b.shape
    bm, bk = _pick_block(M, 512), min(512, T)

    def body(a_ref, b_ref, c_ref):
        @pl.when(pl.program_id(1) == 0)
        def _():
            c_ref[...] = jnp.zeros_like(c_ref)

        a_blk = a_ref[...].astype(MM)
        for d in range(nd):
            c_ref[d] += _dot_tn(b_ref[d], a_blk)

    return pl.pallas_call(
        body, name=name, grid=(M // bm, T // bk),
        in_specs=[pl.BlockSpec((bk, bm), lambda i, k: (k, i)), pl.BlockSpec((nd, bk, c), lambda i, k: (0, k, 0))],
        out_specs=pl.BlockSpec((nd, c, bm), lambda i, k: (0, 0, i)),
        out_shape=jax.ShapeDtypeStruct((nd, c, M), F32),
        compiler_params=_cparams(("parallel", "arbitrary"), VMEM_LIMIT),
    )(a, b)


def _pad_gain(g, n):
    return jnp.pad(g.reshape(1, -1), ((0, 0), (0, n - g.shape[-1])))


GROUP_A = ("w_ffn_gate", "w_ffn_up", "w_ffn_down", "w_ple_gate", "w_ple_proj")
GROUP_B = ("w_branch", "w_out")
GROUP_C = ("w_in", "w_uq", "w_ukv")
EARLY = GROUP_C
LATE = GROUP_B + GROUP_A
TRANSPOSED = ("w_in", "w_uq", "w_ffn_gate", "w_ffn_up")


def _local_step(x, p, pos, tgt, small, big, late_blocks=None, core=None):
    T = x.shape[0]
    tm = min(ROW_TILE, T)
    w_in = big["w_in"]
    w_uq = jnp.pad(big["w_uq"], ((0, 0), (0, HEAD_PAD - QK_DIM), (0, 0)))
    w_ukv = big["w_ukv"]

    g_mix, g_qa, g_kva = small["mix_norm_g"], small["q_a_norm_g"], small["kv_a_norm_g"]
    g_qn, g_kn = _pad_gain(small["q_norm_g"], HEAD_PAD), _pad_gain(small["k_norm_g"], HEAD_PAD)
    g_out, g_ffn = small["hg_out_norm_g"], small["ffn_norm_g"]
    g_pg, g_post = small["ple_gate_norm_g"], small["ple_post_norm_g"]
    logits = small["hg_lb_logits"]
    lb = _lower_bound(logits)

    h, cq, ckv, kr, hq, hf, hi, hg, bg = _in_proj_fwd(x, g_mix, w_in, T, tm)
    qf, kf, vf = _mla_prep_fwd(cq, ckv, kr, pos, g_qa, g_kva, g_qn, g_kn, w_uq, w_ukv, T, tm)
    if late_blocks is None:
        attn, lse = _flash_fwd(qf, kf, vf, T)
    else:
        attn, lse, *late = _flash_fwd(qf, kf, vf, T, ag_blocks=[late_blocks[n] for n in LATE])
        big = {**big, **dict(zip(LATE, late))}
    o, s0 = _hgrn_fwd(hq, hf, hi, lb, T)
    w_branch = jnp.moveaxis(big["w_branch"].reshape(N_DEV, 2, HG_W, HEAD_PAD), 0, 2).reshape(2, HG_W, D_MODEL)
    w_bra = jnp.pad(w_branch[0].reshape(MLA_HEADS, V_DIM, D_MODEL),
                    ((0, 0), (0, HEAD_PAD - V_DIM), (0, 0))).reshape(MLA_HEADS * HEAD_PAD, D_MODEL)
    w_brb = w_branch[1]
    w_out = big["w_out"].reshape(D_MODEL, D_MODEL)
    w_g, w_u = big["w_ffn_gate"].reshape(FFN, D_MODEL), big["w_ffn_up"].reshape(FFN, D_MODEL)
    w_d = big["w_ffn_down"].reshape(FFN, D_MODEL)
    w_pg, w_pp = big["w_ple_gate"].reshape(D_MODEL, D_MODEL), big["w_ple_proj"]
    x1, ya, yb, m, rec = _merge_fwd(attn, o, hg, bg, x, g_out, w_bra, w_brb, w_out, T, tm)
    x2, gt, up, h2 = _ffn_fwd(x1, g_ffn, w_g, w_u, w_d, T, tm)
    dx2, loss_p, dg_post, dg_pg, d_pg, d_pp = _ple_loss(x2, p, tgt, g_pg, g_post, w_pg, w_pp, T, tm)

    grads, sibs, gots = {}, {}, {}
    dist = core is not None
    pick = lambda names: [grads[n] for n in names] if dist else ()

    def partials(tag, names, got):
        if not dist:
            return ()
        sibs.update(zip(names, got))
        return _chip_partials("rs_partial_" + tag, pick(names), got, core)

    dx1, a, dgt, dup, dg_ffn = _ffn_bwd(dx2, x1, gt, up, g_ffn, w_g, w_u, w_d, T, tm)
    grads["w_ffn_gate"] = _matmul_tn("dw_gate", dgt, h2).reshape(N_DEV, -1, D_MODEL)
    grads["w_ffn_up"] = _matmul_tn("dw_up", dup, h2).reshape(N_DEV, -1, D_MODEL)
    grads["w_ffn_down"] = _matmul_tn("dw_down", a, dx2).reshape(N_DEV, -1, D_MODEL)
    grads["w_ple_gate"] = d_pg.reshape(N_DEV, -1, D_MODEL)
    grads["w_ple_proj"] = d_pp

    dattn, do, dhg, dbg, dg_out, d_out, d_bra, d_brb, *sib_a = _merge_bwd(
        dx1, ya, yb, bg, o, hg, attn, m, rec, g_out, w_bra, w_brb, w_out, T, tm, xchg=(pick(GROUP_A), ()))
    parts_a = partials("a", GROUP_A, sib_a)
    d_bra = d_bra.reshape(N_DEV, MLA_HEADS, HEAD_PAD, HEAD_PAD)[:, :, :V_DIM].reshape(N_DEV, HG_W, HEAD_PAD)
    grads["w_branch"] = jnp.concatenate([d_bra, d_brb], axis=1)
    grads["w_out"] = d_out.reshape(N_DEV, -1, D_MODEL)

    dhq, dhf, dhi, dlb, *got = _hgrn_bwd(hq, hf, hi, do, s0, lb, T, xchg=(pick(GROUP_B), parts_a))
    sib_b, got_a = got[:len(GROUP_B)], got[len(GROUP_B):]
    parts_b = partials("b", GROUP_B, sib_b)
    dqf, dkf, dvf, *got_b = _flash_bwd(qf, kf, vf, attn, dattn, lse, T, xchg=((), parts_b))
    (dcq, dckv, dkr, dg_qa, dg_kva, dg_qn, dg_kn, d_uq, d_ukv) = _mla_prep_bwd(
        cq, ckv, kr, pos, dqf, dkf, dvf, g_qa, g_kva, g_qn, g_kn, w_uq, w_ukv, T, tm)
    grad_x, dproj, dg_mix = _in_proj_bwd(x, dx1, [dcq, dckv, dkr, dhq, dhf, dhi, dhg, dbg], g_mix, w_in, T, tm)
    grads["w_in"] = _matmul_tn_blocks("dw_in", h, dproj)
    grads["w_uq"] = d_uq[:, :QK_DIM]
    grads["w_ukv"] = d_ukv
    parts_c = ()
    if dist:
        parts_c = partials("c", GROUP_C, _exchange_sibling("rs_sibling_c", pick(GROUP_C)))
        gots.update(zip(GROUP_A, got_a))
        gots.update(zip(GROUP_B, got_b))

    dl0 = dlb * lb * (1.0 - lb)
    small_g = {
        "mix_norm_g": dg_mix, "q_a_norm_g": dg_qa, "kv_a_norm_g": dg_kva, "q_norm_g": dg_qn, "k_norm_g": dg_kn,
        "hg_lb_logits": jnp.concatenate([dl0, -dl0], axis=0), "hg_out_norm_g": dg_out,
        "ffn_norm_g": dg_ffn, "ple_gate_norm_g": dg_pg, "ple_post_norm_g": dg_post,
    }
    return loss_p, grad_x, small_g, grads, sibs, gots, parts_c


def _lower_bound(logits):
    def body(l_ref, lb_ref):
        l = l_ref[...]
        mx = jnp.max(l, axis=0, keepdims=True)
        e = jnp.exp(l - mx)
        lb_ref[...] = e[0:1] / jnp.sum(e, axis=0, keepdims=True)

    return pl.pallas_call(body, name="lower_bound", out_shape=jax.ShapeDtypeStruct((1, HG_W), F32))(logits)


def _my_place():
    return lax.axis_index("x"), lax.axis_index("y"), lax.axis_index("c")


def _all_gather(name, blocks):
    n = len(blocks)

    def body(*refs):
        x_refs, out_refs, sems = refs[:n], refs[n:2 * n], refs[2 * n:]
        _ag_start(x_refs, out_refs, sems)
        _ag_finish(x_refs, out_refs, sems)

    any_spec = pl.BlockSpec(memory_space=pl.ANY)
    return pl.pallas_call(
        body, name=name, out_shape=_ag_out_shapes(blocks),
        in_specs=[any_spec] * n, out_specs=[any_spec] * n, scratch_shapes=_ag_sems(n),
    )(*blocks)


def _ag_out_shapes(blocks):
    return [jax.ShapeDtypeStruct((N_DEV,) + b.shape, b.dtype) for b in blocks]


def _ag_sems(n):
    return [pltpu.SemaphoreType.DMA((7 * n,)), pltpu.SemaphoreType.DMA((7 * n,)), pltpu.SemaphoreType.DMA((n,))]


def _ag_parts(x_refs, out_refs, sems):
    send_sems, recv_sems, local_sems = sems
    x, y, c = _my_place()
    me, sibling = (x, y, c), (x, y, 1 - c)
    chips = [(1 - x, y), (x, 1 - y), (1 - x, 1 - y)]
    n = len(x_refs)

    def copy(a, k, block, to, own=False):
        px, py, pc = block
        dst = out_refs[a].at[4 * px + 2 * py + pc]
        return pltpu.make_async_remote_copy(
            src_ref=x_refs[a] if own else dst, dst_ref=dst, send_sem=send_sems.at[7 * a + k],
            recv_sem=recv_sems.at[7 * a + k], device_id=to, device_id_type=MESH_ID)

    mine = [pltpu.make_async_copy(x_refs[a], out_refs[a].at[4 * x + 2 * y + c], local_sems.at[a]) for a in range(n)]
    first = []
    for a in range(n):
        first.append(copy(a, 0, me, sibling, own=True))
        first += [copy(a, 1 + j, me, (*chip, c), own=True) for j, chip in enumerate(chips)]
    return copy, mine, first, me, sibling, chips, c, n


def _ag_start(x_refs, out_refs, sems):
    _, mine, first, *_ = _ag_parts(x_refs, out_refs, sems)
    for cp in mine + first:
        cp.start()


def _ag_finish(x_refs, out_refs, sems):
    copy, mine, first, me, sibling, chips, c, n = _ag_parts(x_refs, out_refs, sems)
    passed = []
    for j, chip in enumerate(chips):
        for a in range(n):
            copy(a, 1 + j, (*chip, c), me).wait_recv()
            passed.append(copy(a, 4 + j, (*chip, c), sibling))
            passed[-1].start()
    for a in range(n):
        copy(a, 0, sibling, me).wait_recv()
    for j, chip in enumerate(chips):
        for a in range(n):
            copy(a, 4 + j, (*chip, 1 - c), me).wait_recv()
    for cp in first + passed:
        cp.wait_send()
    for cp in mine:
        cp.wait()


def _exchange_sibling(name, gs):
    return _exchange(name, (gs, ()))


def _exchange(name, xchg):
    n = _x_count(xchg)

    def body(*refs):
        in_refs, out_refs, sems = refs[:n], refs[n:2 * n], refs[2 * n:]
        for cp in _x_copies(len(xchg[0]), in_refs, out_refs, sems):
            cp.start()
        for cp in _x_copies(len(xchg[0]), in_refs, out_refs, sems):
            cp.wait()

    any_spec = pl.BlockSpec(memory_space=pl.ANY)
    return pl.pallas_call(
        body, name=name, out_shape=_x_out_shapes(xchg), in_specs=[any_spec] * n, out_specs=[any_spec] * n,
        scratch_shapes=_x_sems(xchg),
    )(*xchg[0], *xchg[1])


N_PARTS = 4


def _part_spec(rows, cols, t_pos, lead_block=(), lead_index=lambda *args: ()):
    if rows % (16 * N_PARTS) == 0:
        axis, shape, count = 0, (rows // N_PARTS, cols), N_PARTS
    elif cols % (128 * N_PARTS) == 0:
        axis, shape, count = 1, (rows, cols // N_PARTS), N_PARTS
    else:
        axis, shape, count = 0, (rows, cols), 1

    def index(*args):
        i = jnp.minimum(args[t_pos], count - 1)
        return (*lead_index(*args), *((i, 0) if axis == 0 else (0, i)))

    return pl.BlockSpec((*lead_block, *shape), index)


def _chip_partials(name, gs, sibs, c_idx):
    n = len(gs)

    def body(c_ref, *refs):
        for g_ref, sib_ref, out_ref in zip(refs[:n], refs[n:2 * n], refs[2 * n:]):
            out_ref[...] = (g_ref[...] + sib_ref[...]).astype(MM)

    own = [_part_spec(*g.shape[1:], 1, (1,), lambda j, t, c_ref: (2 * j + c_ref[0],)) for g in gs]
    by_chip = [_part_spec(*g.shape[1:], 1, (1,), lambda j, t, c_ref: (j,)) for g in gs]
    grid_spec = pltpu.PrefetchScalarGridSpec(
        num_scalar_prefetch=1, grid=(4, N_PARTS), in_specs=own + by_chip, out_specs=by_chip)
    return pl.pallas_call(
        body, name=name, grid_spec=grid_spec, out_shape=[jax.ShapeDtypeStruct((4,) + g.shape[1:], MM) for g in gs],
        compiler_params=_cparams(("arbitrary", "arbitrary"), VMEM_LIMIT),
    )(c_idx, *gs, *sibs)


def _exchange_chips(parts):
    return _exchange("rs_chips", ((), parts))


def _x_count(xchg):
    return len(xchg[0]) + len(xchg[1])


def _x_out_shapes(xchg):
    return ([jax.ShapeDtypeStruct((4,) + g.shape[1:], g.dtype) for g in xchg[0]]
            + [jax.ShapeDtypeStruct((3,) + p.shape[1:], p.dtype) for p in xchg[1]])


def _x_sems(xchg):
    n = 4 * len(xchg[0]) + 3 * len(xchg[1])
    return [pltpu.SemaphoreType.DMA((n,)), pltpu.SemaphoreType.DMA((n,))] if n else []


def _x_copies(n_sib, in_refs, out_refs, sems):
    if not in_refs:
        return []
    send_sems, recv_sems = sems
    x, y, c = _my_place()
    chips = [(1 - x, y), (x, 1 - y), (1 - x, 1 - y)]
    copies = []

    def add(src, dst, to):
        k = len(copies)
        copies.append(pltpu.make_async_remote_copy(
            src_ref=src, dst_ref=dst, send_sem=send_sems.at[k], recv_sem=recv_sems.at[k], device_id=to,
            device_id_type=MESH_ID))

    for a, (src, dst) in enumerate(zip(in_refs, out_refs)):
        if a < n_sib:
            for j in range(4):
                add(src.at[2 * j + 1 - c], dst.at[j], (x, y, 1 - c))
        else:
            for k, (px, py) in enumerate(chips):
                add(src.at[2 * px + py], dst.at[k], (px, py, c))
    return copies


def _adamw_math(w, g, m, v):
    m = ADAM_B1 * m + (1.0 - ADAM_B1) * g
    v = ADAM_B2 * v + (1.0 - ADAM_B2) * jnp.square(g)
    m_hat = m / (1.0 - ADAM_B1 ** ADAM_STEP)
    v_hat = v / (1.0 - ADAM_B2 ** ADAM_STEP)
    delta = -ADAM_LR * (m_hat / (jnp.sqrt(v_hat) + ADAM_EPS) + ADAM_WD * w)
    return delta, m, v


def _sum_adamws(name, gs, sibs, gots, ws, ms, vs, slot_idx, chip_idx):
    n = len(gs)

    def body(s_ref, j_ref, *refs):
        ins, outs = refs[:6 * n], refs[6 * n:]
        for a in range(n):
            g_ref, sib_ref, got_ref, w_ref, m_ref, v_ref = (ins[k * n + a] for k in range(6))
            go_ref, d_ref, m2_ref, v2_ref = outs[4 * a:4 * a + 4]
            grad = g_ref[0] + sib_ref[0]
            for k in range(3):
                grad = grad + got_ref[k].astype(F32)
            go_ref[...] = grad
            d_ref[...], m2_ref[...], v2_ref[...] = _adamw_math(w_ref[...], grad, m_ref[...], v_ref[...])

    shapes = [g.shape[1:] for g in gs]
    flat = [_part_spec(*s, 0) for s in shapes]
    in_specs = ([_part_spec(*s, 0, (1,), lambda t, s_ref, j_ref: (s_ref[0],)) for s in shapes]
                + [_part_spec(*s, 0, (1,), lambda t, s_ref, j_ref: (j_ref[0],)) for s in shapes]
                + [_part_spec(*s, 0, (3,), lambda t, s_ref, j_ref: (0,)) for s in shapes] + flat * 3)
    grid_spec = pltpu.PrefetchScalarGridSpec(
        num_scalar_prefetch=2, grid=(N_PARTS,), in_specs=in_specs, out_specs=[f for f in flat for _ in range(4)])
    res = pl.pallas_call(
        body, name=name, grid_spec=grid_spec,
        out_shape=[jax.ShapeDtypeStruct(s, F32) for s in shapes for _ in range(4)],
        compiler_params=_cparams(("arbitrary",), VMEM_LIMIT),
    )(slot_idx, chip_idx, *gs, *sibs, *gots, *ws, *ms, *vs)
    return [res[4 * a:4 * a + 4] for a in range(n)]


BIG = ("w_in", "w_uq", "w_ukv", "w_branch", "w_out", "w_ffn_gate", "w_ffn_up", "w_ffn_down", "w_ple_gate", "w_ple_proj")
SMALL = (
    ("mix_norm_g", 0, 1, 1024), ("q_a_norm_g", 1, 1, 384), ("kv_a_norm_g", 2, 1, 256), ("q_norm_g", 3, 1, 96),
    ("k_norm_g", 4, 1, 96), ("hg_lb_logits", 5, 2, 512), ("hg_out_norm_g", 7, 1, 128), ("ffn_norm_g", 8, 1, 1024),
    ("ple_gate_norm_g", 9, 1, 1024), ("ple_post_norm_g", 10, 1, 1024),
)
SLAB_ROWS, LOSS_ROW = 16, 15


def _pack_partials(small_g, loss_p):
    def body(*refs):
        val_refs, loss_ref, out_ref = refs[:len(SMALL)], refs[len(SMALL)], refs[len(SMALL) + 1]
        out_ref[...] = jnp.zeros_like(out_ref)
        for (_, r0, rows, cols), ref in zip(SMALL, val_refs):
            val = ref[...]
            if val.shape[0] != rows:
                val = jnp.sum(val, axis=0, keepdims=True)
            out_ref[r0:r0 + rows, :cols] = val[:, :cols]
        out_ref[LOSS_ROW:LOSS_ROW + 1, :HEAD_PAD] = jnp.full((1, HEAD_PAD), jnp.sum(loss_ref[...]), F32)

    return pl.pallas_call(
        body, name="pack_partials", out_shape=jax.ShapeDtypeStruct((SLAB_ROWS, D_MODEL), F32),
    )(*[small_g[n] for n, *_ in SMALL], loss_p)


def _adamw_small(parts, ws, ms, vs):
    n = len(SMALL)

    def body(p_ref, *refs):
        ins, loss_ref, outs = refs[:3 * n], refs[3 * n], refs[3 * n + 1:]
        total = p_ref[0]
        for d in range(1, N_DEV):
            total = total + p_ref[d]
        loss_ref[...] = total[LOSS_ROW:LOSS_ROW + 1, 0:1]
        for a, (_, r0, rows, cols) in enumerate(SMALL):
            g = total[r0:r0 + rows, :cols]
            outs[4 * a][...] = g
            outs[4 * a + 1][...], outs[4 * a + 2][...], outs[4 * a + 3][...] = _adamw_math(
                ins[a][...], g, ins[n + a][...], ins[2 * n + a][...])

    shapes = [jax.ShapeDtypeStruct((rows, cols), F32) for _, _, rows, cols in SMALL]
    res = pl.pallas_call(
        body, name="adamw_small", out_shape=[jax.ShapeDtypeStruct((1, 1), F32)] + [s for s in shapes for _ in range(4)],
    )(parts, *ws, *ms, *vs)
    return res[0], [res[1 + 4 * a:5 + 4 * a] for a in range(n)]


_WEIGHTS = ["mix_norm_g", "w_in", "q_a_norm_g", "w_uq", "kv_a_norm_g", "w_ukv", "q_norm_g", "k_norm_g", "hg_lb_logits",
            "hg_out_norm_g", "w_branch", "w_out", "ffn_norm_g", "w_ffn_gate", "w_ffn_up", "w_ffn_down",
            "ple_gate_norm_g", "w_ple_gate", "w_ple_proj", "ple_post_norm_g"]


def _step(x, p, positions, tgt, w, m, v):
    small_names = [n for n, *_ in SMALL]
    T = x.shape[1]
    px, py, pc = _my_place()
    as_idx = lambda t: jnp.reshape(t, (1,)).astype(jnp.int32)

    def two_d(n, t):
        t = t.reshape(-1, t.shape[-1])
        return t.T if n in TRANSPOSED else t

    def full_shape(n, t):
        return (t.T if n in TRANSPOSED else t).reshape(w[n].shape)

    blocks = {n: two_d(n, w[n]).astype(MM) for n in BIG}
    big = dict(zip(EARLY, _all_gather("ag_weights", [blocks[n] for n in EARLY])))
    small = {n: (w[n] if n == "hg_lb_logits" else w[n].reshape(1, -1)) for n in small_names}

    loss_p, grad_x, small_g, grads, sibs, gots, parts_c = _local_step(
        x[0], p[0, 0], positions.reshape(T, 1), tgt[0], small, big, late_blocks=blocks, core=as_idx(pc))

    gots.update(zip(GROUP_C, _exchange_chips(parts_c)))
    out_g, out_d, out_m, out_v = {}, {}, {}, {}
    for tag, names in (("ab", GROUP_A + GROUP_B), ("c", GROUP_C)):
        pick = lambda table: [table[n] for n in names]
        res = _sum_adamws("adamw_" + tag, pick(grads), pick(sibs), pick(gots), [two_d(n, w[n]) for n in names],
                          [two_d(n, m[n]) for n in names], [two_d(n, v[n]) for n in names],
                          as_idx(4 * px + 2 * py + pc), as_idx(2 * px + py))
        for n, r in zip(names, res):
            out_g[n], out_d[n], out_m[n], out_v[n] = [full_shape(n, t) for t in r]

    parts = _all_gather("ag_small", [_pack_partials(small_g, loss_p)])[0]
    loss, res = _adamw_small(parts, *([t[n] for n in small_names] for t in (w, m, v)))
    for n, r in zip(small_names, res):
        out_g[n], out_d[n], out_m[n], out_v[n] = r

    outs = [loss.reshape(()), grad_x[None]]
    for table in (out_g, out_d, out_m, out_v):
        outs += [table[n] for n in _WEIGHTS]
    return tuple(outs)


def kernel(x, p, positions, mix_norm_g, w_in, q_a_norm_g, w_uq, kv_a_norm_g, w_ukv, q_norm_g, k_norm_g, hg_lb_logits, hg_out_norm_g, w_branch, w_out, ffn_norm_g, w_ffn_gate, w_ffn_up, w_ffn_down, ple_gate_norm_g, w_ple_gate, w_ple_proj, ple_post_norm_g, loss_target, m_mix_norm_g, m_w_in, m_q_a_norm_g, m_w_uq, m_kv_a_norm_g, m_w_ukv, m_q_norm_g, m_k_norm_g, m_hg_lb_logits, m_hg_out_norm_g, m_w_branch, m_w_out, m_ffn_norm_g, m_w_ffn_gate, m_w_ffn_up, m_w_ffn_down, m_ple_gate_norm_g, m_w_ple_gate, m_w_ple_proj, m_ple_post_norm_g, v_mix_norm_g, v_w_in, v_q_a_norm_g, v_w_uq, v_kv_a_norm_g, v_w_ukv, v_q_norm_g, v_k_norm_g, v_hg_lb_logits, v_hg_out_norm_g, v_w_branch, v_w_out, v_ffn_norm_g, v_w_ffn_gate, v_w_ffn_up, v_w_ffn_down, v_ple_gate_norm_g, v_w_ple_gate, v_w_ple_proj, v_ple_post_norm_g):
    w = dict(mix_norm_g=mix_norm_g, w_in=w_in, q_a_norm_g=q_a_norm_g, w_uq=w_uq, kv_a_norm_g=kv_a_norm_g, w_ukv=w_ukv,
             q_norm_g=q_norm_g, k_norm_g=k_norm_g, hg_lb_logits=hg_lb_logits, hg_out_norm_g=hg_out_norm_g,
             w_branch=w_branch, w_out=w_out, ffn_norm_g=ffn_norm_g, w_ffn_gate=w_ffn_gate, w_ffn_up=w_ffn_up,
             w_ffn_down=w_ffn_down, ple_gate_norm_g=ple_gate_norm_g, w_ple_gate=w_ple_gate, w_ple_proj=w_ple_proj,
             ple_post_norm_g=ple_post_norm_g)
    m = dict(mix_norm_g=m_mix_norm_g, w_in=m_w_in, q_a_norm_g=m_q_a_norm_g, w_uq=m_w_uq, kv_a_norm_g=m_kv_a_norm_g,
             w_ukv=m_w_ukv, q_norm_g=m_q_norm_g, k_norm_g=m_k_norm_g, hg_lb_logits=m_hg_lb_logits,
             hg_out_norm_g=m_hg_out_norm_g, w_branch=m_w_branch, w_out=m_w_out, ffn_norm_g=m_ffn_norm_g,
             w_ffn_gate=m_w_ffn_gate, w_ffn_up=m_w_ffn_up, w_ffn_down=m_w_ffn_down,
             ple_gate_norm_g=m_ple_gate_norm_g, w_ple_gate=m_w_ple_gate, w_ple_proj=m_w_ple_proj,
             ple_post_norm_g=m_ple_post_norm_g)
    v = dict(mix_norm_g=v_mix_norm_g, w_in=v_w_in, q_a_norm_g=v_q_a_norm_g, w_uq=v_w_uq, kv_a_norm_g=v_kv_a_norm_g,
             w_ukv=v_w_ukv, q_norm_g=v_q_norm_g, k_norm_g=v_k_norm_g, hg_lb_logits=v_hg_lb_logits,
             hg_out_norm_g=v_hg_out_norm_g, w_branch=v_w_branch, w_out=v_w_out, ffn_norm_g=v_ffn_norm_g,
             w_ffn_gate=v_w_ffn_gate, w_ffn_up=v_w_ffn_up, w_ffn_down=v_w_ffn_down,
             ple_gate_norm_g=v_ple_gate_norm_g, w_ple_gate=v_w_ple_gate, w_ple_proj=v_w_ple_proj,
             ple_post_norm_g=v_ple_post_norm_g)
    return _step(x, p, positions, loss_target, w, m, v)
```

```python
import jax
import jax.numpy as jnp
import numpy as np
from jax import lax
from jax.experimental import pallas as pl
from jax.experimental.pallas import tpu as pltpu

F32 = jnp.float32
MM = jnp.bfloat16
MESH_ID = pl.DeviceIdType.MESH

D_MODEL = 1024
N_DEV = 8
MLA_HEADS = 8
QK_NOPE = 64
QK_ROPE = 32
QK_DIM = 96
V_DIM = 64
HEAD_PAD = 128
Q_RANK = 384
KV_RANK = 256
ROPE_BASE = 10000.0
HG_HEADS = 4
HG_DIM = 128
HG_W = 512
HG_CHUNK = 64
FFN = 2816
PLE = 256
EPS = 1e-6
ATT_SCALE = QK_DIM ** -0.5
NEG = -1e30

ADAM_LR = 0.001
ADAM_B1 = 0.9
ADAM_B2 = 0.999
ADAM_EPS = 1e-08
ADAM_WD = 0.01
ADAM_STEP = 10

COL_SECTIONS = ((0, 384), (384, 256), (640, 32), (672, 512), (1184, 512), (1696, 512), (2208, 512), (2720, 2048))
STORED_WIDTHS = tuple(HEAD_PAD if n == QK_ROPE else n for _, n in COL_SECTIONS)
IN_COLS = 4768
IN_BLOCK = IN_COLS // N_DEV

VMEM_LIMIT = 58 * 1024 * 1024
ROW_TILE = 256
WIDE_TILE = 512
ATT_TILE = 1024
ATT_HEADS = 4
HG_BLOCK = 512
HG_UNROLL = 4


def _dot(a, b):
    return jnp.dot(a.astype(MM), b.astype(MM), preferred_element_type=F32)


def _dot_nt(a, b):
    return lax.dot_general(a.astype(MM), b.astype(MM), (((1,), (1,)), ((), ())), preferred_element_type=F32)


def _dot_tn(a, b):
    return lax.dot_general(a.astype(MM), b.astype(MM), (((0,), (0,)), ((), ())), preferred_element_type=F32)


def _sigmoid(x):
    return 1.0 / (1.0 + jnp.exp(-x))


def _rms(x, n=None):
    n = x.shape[-1] if n is None else n
    r = lax.rsqrt(jnp.sum(x * x, axis=-1, keepdims=True) * (1.0 / n) + EPS)
    return x * r, r


def _rms_bwd(dxh, xh, r, n=None):
    n = xh.shape[-1] if n is None else n
    return r * (dxh - xh * (jnp.sum(dxh * xh, axis=-1, keepdims=True) * (1.0 / n)))


def _rope_tables(pos, tm):
    lane = lax.broadcasted_iota(jnp.int32, (tm, HEAD_PAD), 1)
    idx = jnp.where(lane < QK_NOPE + QK_ROPE // 2, lane - QK_NOPE, lane - QK_NOPE - QK_ROPE // 2)
    inv = jnp.exp(idx.astype(F32) * (-np.log(ROPE_BASE) * 2.0 / QK_ROPE))
    ang = pos.astype(F32) * inv
    in_rope = (lane >= QK_NOPE) & (lane < QK_DIM)
    first = lane < QK_NOPE + QK_ROPE // 2
    cos_t = jnp.where(in_rope, jnp.cos(ang), 1.0)
    sin_t = jnp.where(in_rope, jnp.where(first, -jnp.sin(ang), jnp.sin(ang)), 0.0)
    return cos_t, sin_t, (first, in_rope)


def _rope_swap(x, halves):
    first, in_rope = halves
    half = QK_ROPE // 2
    return jnp.where(in_rope, jnp.where(first, pltpu.roll(x, HEAD_PAD - half, 1), pltpu.roll(x, half, 1)), 0.0)


def _cparams(sem, vmem=None):
    return pltpu.CompilerParams(dimension_semantics=sem, vmem_limit_bytes=vmem)


def _row_call(name, body, T, tm, row_ins, full_ins, row_outs, acc_outs, vmem=None, scratch=(), xchg=((), ())):
    n_in, n_out, n_x = len(row_ins) + len(full_ins), len(row_outs) + len(acc_outs), _x_count(xchg)
    steps = T // tm

    def kern(*refs):
        ins, x_in, refs = refs[:n_in], refs[n_in:n_in + n_x], refs[n_in + n_x:]
        outs, x_out, refs = refs[:n_out], refs[n_out:n_out + n_x], refs[n_out + n_x:]
        scr, x_sems = refs[:len(scratch)], refs[len(scratch):]
        i = pl.program_id(0)
        if n_x:
            @pl.when(i == 0)
            def _():
                for cp in _x_copies(len(xchg[0]), x_in, x_out, x_sems):
                    cp.start()

        body(i, *ins, *outs, *scr)
        if n_x:
            @pl.when(i == steps - 1)
            def _():
                for cp in _x_copies(len(xchg[0]), x_in, x_out, x_sems):
                    cp.wait()

    any_spec = pl.BlockSpec(memory_space=pl.ANY)
    in_specs = [pl.BlockSpec((tm, a.shape[1]), lambda i: (i, 0)) for a in row_ins]
    in_specs += [pl.BlockSpec(a.shape, lambda i, nd=a.ndim: (0,) * nd, pipeline_mode=pl.Buffered(1)) for a in full_ins]
    out_specs = [pl.BlockSpec((tm, n), lambda i: (i, 0)) for n, _ in row_outs]
    out_specs += [pl.BlockSpec(s, lambda i, nd=len(s): (0,) * nd) for s, _ in acc_outs]
    out_shape = [jax.ShapeDtypeStruct((T, n), dt) for n, dt in row_outs]
    out_shape += [jax.ShapeDtypeStruct(s, dt) for s, dt in acc_outs]
    return pl.pallas_call(
        kern, name=name, grid=(steps,), in_specs=in_specs + [any_spec] * n_x, out_specs=out_specs + [any_spec] * n_x,
        out_shape=out_shape + _x_out_shapes(xchg), scratch_shapes=list(scratch) + _x_sems(xchg),
        compiler_params=_cparams(("arbitrary",), vmem),
    )(*row_ins, *full_ins, *xchg[0], *xchg[1])


FFN_HALVES = (slice(0, FFN // 2), slice(FFN // 2, FFN))
ROW_CHUNK = 16
CHUNK_UNROLL = True


def _by_chunks(tm, fn):
    def step(c, carry):
        fn(pl.ds(pl.multiple_of(c * ROW_CHUNK, ROW_CHUNK), ROW_CHUNK))
        return carry

    lax.fori_loop(0, tm // ROW_CHUNK, step, 0, unroll=CHUNK_UNROLL)


def _fold8(x):
    return x[:8] + x[8:]


def _acc(ref, i, val):
    @pl.when(i == 0)
    def _():
        ref[...] = val

    @pl.when(i != 0)
    def _():
        ref[...] += val


def _in_proj_fwd(x, g_mix, w_in, T, tm):
    def body(i, x_ref, g_ref, w_ref, h_ref, *rest):
        outs, pj_s = rest[:-1], rest[-1]
        g = g_ref[...]

        def norm(rows):
            h_ref[rows, :] = (_rms(x_ref[rows, :])[0] * g).astype(MM)

        _by_chunks(tm, norm)
        for d in range(N_DEV):
            pj_s[d] = _dot_nt(h_ref[...], w_ref[d])

        def join_and_cut(rows):
            proj = jnp.concatenate([pj_s[d, rows, :] for d in range(N_DEV)], axis=1)
            for (s, n), o_ref in zip(COL_SECTIONS, outs):
                if n == QK_ROPE:
                    o_ref[rows, :] = jnp.concatenate(
                        [jnp.zeros((ROW_CHUNK, QK_NOPE), F32), proj[:, s:s + n],
                         jnp.zeros((ROW_CHUNK, HEAD_PAD - QK_DIM), F32)], axis=1)
                else:
                    o_ref[rows, :] = proj[:, s:s + n]

        _by_chunks(tm, join_and_cut)

    row_outs = [(D_MODEL, MM)] + [(n, F32) for n in STORED_WIDTHS]
    return _row_call("in_proj_fwd", body, T, tm, [x], [g_mix, w_in], row_outs, [], VMEM_LIMIT,
                     scratch=[pltpu.VMEM((N_DEV, tm, IN_BLOCK), F32)])


def _mla_heads_fwd(raw, g_pad, cos_t, sin_t, first):
    outs, saved = [], []
    for h in range(MLA_HEADS):
        xh, r = _rms(raw[:, h * HEAD_PAD:(h + 1) * HEAD_PAD], QK_DIM)
        y = xh * g_pad
        outs.append(y * cos_t + _rope_swap(y, first) * sin_t)
        saved.append((xh, r))
    return outs, saved


def _mla_raw_heads(cqn, ckvn, kr, wuq_ref, wukv_ref, tm):
    lane = lax.broadcasted_iota(jnp.int32, (tm, HEAD_PAD), 1)
    nope = lane < QK_NOPE
    one_lane = jnp.where(lane == V_DIM, 1.0, 0.0)
    qs, ks, vs = [], [], []
    for h in range(MLA_HEADS):
        qs.append(_dot_nt(cqn, wuq_ref[h]))
        kv = _dot(ckvn, wukv_ref[h])
        ks.append(jnp.where(nope, kv, kr))
        vs.append(jnp.where(nope, pltpu.roll(kv, V_DIM, 1), one_lane))
    return jnp.concatenate(qs, axis=1), jnp.concatenate(ks, axis=1), jnp.concatenate(vs, axis=1)


def _mla_prep_fwd(cq, ckv, kr, pos, g_qa, g_kva, g_qn, g_kn, w_uq, w_ukv, T, tm):
    def body(i, cq_ref, ckv_ref, kr_ref, pos_ref, gqa_ref, gkva_ref, gqn_ref, gkn_ref, wuq_ref, wukv_ref,
             q_ref, k_ref, v_ref):
        cos_t, sin_t, first = _rope_tables(pos_ref[...], tm)
        cqn = _rms(cq_ref[...])[0] * gqa_ref[...]
        ckvn = _rms(ckv_ref[...])[0] * gkva_ref[...]
        q_raw, k_raw, v = _mla_raw_heads(cqn, ckvn, kr_ref[...], wuq_ref, wukv_ref, tm)
        qs, _ = _mla_heads_fwd(q_raw, gqn_ref[...], cos_t, sin_t, first)
        ks, _ = _mla_heads_fwd(k_raw, gkn_ref[...], cos_t, sin_t, first)
        q_ref[...] = (jnp.concatenate(qs, axis=1) * ATT_SCALE).astype(MM)
        k_ref[...] = jnp.concatenate(ks, axis=1).astype(MM)
        v_ref[...] = v.astype(MM)

    w = MLA_HEADS * HEAD_PAD
    return _row_call("mla_prep_fwd", body, T, tm, [cq, ckv, kr, pos], [g_qa, g_kva, g_qn, g_kn, w_uq, w_ukv],
                     [(w, MM), (w, MM), (w, MM)], [])


def _causal_pairs(n, by_query):
    if by_query:
        pairs = [(q, k) for q in range(n) for k in range(q + 1)]
    else:
        pairs = [(q, k) for k in range(n) for q in range(k, n)]
    return np.array([p[0] for p in pairs], np.int32), np.array([p[1] for p in pairs], np.int32)


def _flash_fwd(qf, kf, vf, T, ag_blocks=()):
    tq = min(ATT_TILE, T)
    nq = T // tq

    qi_tab, ki_tab = _causal_pairs(nq, by_query=True)

    hp = ATT_HEADS

    n_ag = len(ag_blocks)
    n_heads, n_pairs = MLA_HEADS // hp, len(qi_tab)

    def body(qi_ref, ki_ref, q_ref, k_ref, v_ref, *rest):
        ag_in, (o_ref, lse_ref), rest = rest[:n_ag], rest[n_ag:n_ag + 2], rest[n_ag + 2:]
        ag_out, (m_s, acc_s), ag_sems = rest[:n_ag], rest[n_ag:n_ag + 2], rest[n_ag + 2:]
        t = pl.program_id(1)
        qi, ki = qi_ref[t], ki_ref[t]
        if n_ag:
            @pl.when((pl.program_id(0) == 0) & (t == 0))
            def _():
                _ag_start(ag_in, ag_out, ag_sems)

        @pl.when(ki == 0)
        def _():
            m_s[...] = jnp.full_like(m_s, NEG)
            acc_s[...] = jnp.zeros_like(acc_s)

        def step(masked):
            halves = 2 if masked and tq % (2 * HEAD_PAD) == 0 else 1
            w = tq // halves
            for hh in range(hp):
                hs = slice(hh * HEAD_PAD, (hh + 1) * HEAD_PAD)
                for part in range(halves):
                    cols, nk = slice(part * w, (part + 1) * w), (part + 1) * w
                    s_t = _dot_nt(k_ref[:nk, hs], q_ref[cols, hs])
                    if masked:
                        key = lax.broadcasted_iota(jnp.int32, (nk, w), 0)
                        qry = lax.broadcasted_iota(jnp.int32, (nk, w), 1) + part * w
                        s_t = jnp.where(key <= qry, s_t, NEG)
                    m_old = m_s[hh, :, cols]
                    m_new = jnp.maximum(m_old, jnp.max(s_t, axis=0, keepdims=True))
                    p_t = jnp.exp(s_t - m_new)
                    acc_s[hh, :, cols] = jnp.exp(m_old - m_new) * acc_s[hh, :, cols] + _dot_tn(v_ref[:nk, hs], p_t)
                    m_s[hh, :, cols] = m_new

        @pl.when(ki < qi)
        def _():
            step(False)

        @pl.when(ki == qi)
        def _():
            step(True)
            real = lax.broadcasted_iota(jnp.int32, (HEAD_PAD, tq), 0) < V_DIM
            for hh in range(hp):
                hs = slice(hh * HEAD_PAD, (hh + 1) * HEAD_PAD)
                acc = acc_s[hh]
                l = acc[V_DIM:V_DIM + 1]
                o_ref[:, hs] = jnp.where(real, acc / l, 0.0).T
                lse_ref[:, hs] = jnp.broadcast_to(m_s[hh] + jnp.log(l), (HEAD_PAD, tq)).T

        if n_ag:
            @pl.when((pl.program_id(0) == n_heads - 1) & (t == n_pairs - 1))
            def _():
                _ag_finish(ag_in, ag_out, ag_sems)

    q_spec = pl.BlockSpec((tq, hp * HEAD_PAD), lambda h, t, qi_ref, ki_ref: (qi_ref[t], h))
    kv_spec = pl.BlockSpec((tq, hp * HEAD_PAD), lambda h, t, qi_ref, ki_ref: (ki_ref[t], h))
    any_spec = pl.BlockSpec(memory_space=pl.ANY)
    grid_spec = pltpu.PrefetchScalarGridSpec(
        num_scalar_prefetch=2, grid=(n_heads, n_pairs),
        in_specs=[q_spec, kv_spec, kv_spec] + [any_spec] * n_ag, out_specs=[q_spec, q_spec] + [any_spec] * n_ag,
        scratch_shapes=[pltpu.VMEM((hp, 1, tq), F32), pltpu.VMEM((hp, HEAD_PAD, tq), F32)]
        + (_ag_sems(n_ag) if n_ag else []))
    return pl.pallas_call(
        body, name="flash_fwd", grid_spec=grid_spec,
        out_shape=[jax.ShapeDtypeStruct((T, MLA_HEADS * HEAD_PAD), F32)] * 2 + _ag_out_shapes(ag_blocks),
        compiler_params=_cparams(("arbitrary", "arbitrary")),
    )(jnp.asarray(qi_tab), jnp.asarray(ki_tab), qf, kf, vf, *ag_blocks)


def _hg_gates(hf, lb):
    sg = _sigmoid(hf)
    f = lb + (1.0 - lb) * sg
    return sg, f, jnp.log(f), 1.0 - f


def _prefix_sum(x, reverse=False):
    n = x.shape[0]
    row = lax.broadcasted_iota(jnp.int32, x.shape, 0)
    step = 1
    while step < n:
        if reverse:
            x = x + jnp.where(row < n - step, pltpu.roll(x, n - step, 0), 0.0)
        else:
            x = x + jnp.where(row >= step, pltpu.roll(x, step, 0), 0.0)
        step *= 2
    return x


def _hg_levels():
    C = HG_CHUNK
    t = lax.broadcasted_iota(jnp.int32, (C, C), 0)
    s = lax.broadcasted_iota(jnp.int32, (C, C), 1)
    levels = []
    for shift in range(C.bit_length() - 2, -1, -1):
        pair_t, pair_s = lax.shift_right_logical(t, shift + 1), lax.shift_right_logical(s, shift + 1)
        later_t = (lax.shift_right_logical(t, shift) & 1) == 1
        earlier_s = (lax.shift_right_logical(s, shift) & 1) == 0
        levels.append((1 << shift, (pair_t == pair_s) & later_t & earlier_s))
    return levels, t == s


def _hg_refs(b):
    C, n = b.shape
    row = lax.broadcasted_iota(jnp.int32, (C, n), 0)
    back1, back2, ahead1 = pltpu.roll(b, 1, 0), pltpu.roll(b, 2, 0), pltpu.roll(b, C - 1, 0)
    refs = []
    for half in (32, 16, 8, 4):
        refs.append(jnp.concatenate(
            [jnp.broadcast_to(b[lo + half - 1:lo + half], (2 * half, n)) for lo in range(0, C, 2 * half)], axis=0))
    in4 = row & 3
    refs.append(jnp.where(in4 == 0, ahead1, jnp.where(in4 == 1, b, jnp.where(in4 == 2, back1, back2))))
    refs.append(jnp.where((row & 1) == 1, back1, b))
    return refs


def _hg_intra(q, k, b, refs, levels, eye):
    a = jnp.where(eye, jnp.sum(q * k, axis=1, keepdims=True), 0.0)
    saved = []
    for r, (_, mask) in zip(refs, levels):
        e = jnp.exp(-jnp.abs(b - r))
        q_t, k_t = q * e, k * e
        a = a + jnp.where(mask, _dot_nt(q_t, k_t), 0.0)
        saved.append((q_t, k_t, e))
    return a, saved


def _hg_intra_bwd(d_a, q, k, saved, levels, eye):
    diag = jnp.sum(jnp.where(eye, d_a, 0.0), axis=1, keepdims=True)
    dq, dk = diag * k, diag * q
    for (q_t, k_t, e), (_, mask) in zip(saved, levels):
        da = jnp.where(mask, d_a, 0.0)
        dq = dq + _dot(da, k_t) * e
        dk = dk + _dot_tn(da, q_t) * e
    return dq, dk


def _hgrn_fwd(hq, hf, hi, lb, T):
    rb = min(HG_BLOCK, T)
    ncb = rb // HG_CHUNK

    def body(hq_ref, hf_ref, hi_ref, lb_ref, o_ref, s0_ref, st_ref):
        @pl.when(pl.program_id(0) == 0)
        def _():
            st_ref[...] = jnp.zeros_like(st_ref)

        levels, eye = _hg_levels()

        def chunk(c, carry):
            rows = pl.ds(pl.multiple_of(c * HG_CHUNK, HG_CHUNK), HG_CHUNK)
            _, _, logf, kk = _hg_gates(hf_ref[rows, :], lb_ref[...])
            b = _prefix_sum(logf)
            refs = _hg_refs(b)
            q_all, v_all = hq_ref[rows, :], hi_ref[rows, :]
            outs = []
            for h in range(HG_HEADS):
                ls = slice(h * HG_DIM, (h + 1) * HG_DIM)
                q, k, v, bh = q_all[:, ls], kk[:, ls], v_all[:, ls], b[:, ls]
                st = st_ref[h]
                s0_ref[c, h * HG_DIM:(h + 1) * HG_DIM, :] = st
                b_end = bh[HG_CHUNK - 1:HG_CHUNK]
                a, _ = _hg_intra(q, k, bh, [r[:, ls] for r in refs], levels, eye)
                outs.append(_dot_nt(q * jnp.exp(bh), st) + _dot(a, v))
                st_ref[h] = st * jnp.exp(b_end) + _dot_tn(v, k * jnp.exp(b_end - bh))
            o_ref[rows, :] = jnp.concatenate(outs, axis=1)
            return carry

        lax.fori_loop(0, ncb, chunk, 0, unroll=HG_UNROLL)

    row = pl.BlockSpec((rb, HG_W), lambda i: (i, 0))
    return pl.pallas_call(
        body, name="hgrn_fwd", grid=(T // rb,),
        in_specs=[row, row, row, pl.BlockSpec((1, HG_W), lambda i: (0, 0))],
        out_specs=[row, pl.BlockSpec((ncb, HG_W, HG_DIM), lambda i: (i, 0, 0))],
        out_shape=[jax.ShapeDtypeStruct((T, HG_W), F32), jax.ShapeDtypeStruct((T // HG_CHUNK, HG_W, HG_DIM), F32)],
        scratch_shapes=[pltpu.VMEM((HG_HEADS, HG_DIM, HG_DIM), F32)],
        compiler_params=_cparams(("arbitrary",)),
    )(hq, hf, hi, lb)


def _hgrn_bwd(hq, hf, hi, do, s0, lb, T, xchg=((), ())):
    rb = min(HG_BLOCK, T)
    ncb = rb // HG_CHUNK
    nb = T // rb
    C = HG_CHUNK
    n_x, n_sib = _x_count(xchg), len(xchg[0])

    def body(hq_ref, hf_ref, hi_ref, do_ref, s0_ref, lb_ref, *rest):
        x_in, (dq_ref, df_ref, dv_ref, dlb_ref), rest = rest[:n_x], rest[n_x:n_x + 4], rest[n_x + 4:]
        x_out, dst_ref, x_sems = rest[:n_x], rest[n_x], rest[n_x + 1:]

        @pl.when(pl.program_id(0) == 0)
        def _():
            dst_ref[...] = jnp.zeros_like(dst_ref)
            dlb_ref[...] = jnp.zeros_like(dlb_ref)
            for cp in _x_copies(n_sib, x_in, x_out, x_sems):
                cp.start()

        row_cc = lax.broadcasted_iota(jnp.int32, (C, C), 0)
        col_cc = lax.broadcasted_iota(jnp.int32, (C, C), 1)
        last_row = lax.broadcasted_iota(jnp.int32, (C, HG_DIM), 0) == C - 1
        lb_v = lb_ref[...]
        levels, eye = _hg_levels()

        def chunk(cc, carry):
            c = ncb - 1 - cc
            rows = pl.ds(pl.multiple_of(c * C, C), C)
            hf_c = hf_ref[rows, :]
            sg, f, logf, kk = _hg_gates(hf_c, lb_v)
            b = _prefix_sum(logf)
            refs = _hg_refs(b)
            q_all, v_all, do_all = hq_ref[rows, :], hi_ref[rows, :], do_ref[rows, :]
            dq_o, dk_o, dv_o, db_o = [], [], [], []
            for h in range(HG_HEADS):
                ls = slice(h * HG_DIM, (h + 1) * HG_DIM)
                q, k, v, bh, d_o = q_all[:, ls], kk[:, ls], v_all[:, ls], b[:, ls], do_all[:, ls]
                st0 = s0_ref[c, h * HG_DIM:(h + 1) * HG_DIM, :]
                dst = dst_ref[h]
                b_end = bh[C - 1:C]
                e_b, e_end = jnp.exp(bh), jnp.exp(b_end)
                e_rem = jnp.exp(b_end - bh)
                qe, kd = q * e_b, k * e_rem
                st_end = st0 * e_end + _dot_tn(v, kd)
                a, saved = _hg_intra(q, k, bh, [r[:, ls] for r in refs], levels, eye)
                d_a = jnp.where(col_cc <= row_cc, _dot_nt(d_o, v), 0.0)
                dq_i, dk_i = _hg_intra_bwd(d_a, q, k, saved, levels, eye)
                dv = _dot_tn(a, d_o) + _dot_nt(kd, dst)
                dq = e_b * _dot(d_o, st0) + dq_i
                dk = e_rem * _dot(v, dst) + dk_i
                extra = jnp.sum(dst * st_end, axis=0, keepdims=True)
                db_o.append(q * dq - k * dk + jnp.where(last_row, extra, 0.0))
                dst_ref[h] = dst * e_end + _dot_tn(d_o, qe)
                dq_o.append(dq)
                dk_o.append(dk)
                dv_o.append(dv)
            dlogf = _prefix_sum(jnp.concatenate(db_o, axis=1), reverse=True)
            d_f = dlogf / f - jnp.concatenate(dk_o, axis=1)
            dq_ref[rows, :] = jnp.concatenate(dq_o, axis=1).astype(MM)
            dv_ref[rows, :] = jnp.concatenate(dv_o, axis=1).astype(MM)
            df_ref[rows, :] = (d_f * (1.0 - lb_v) * sg * (1.0 - sg)).astype(MM)
            dlb_ref[...] += jnp.sum(d_f * (1.0 - sg), axis=0, keepdims=True)
            return carry

        lax.fori_loop(0, ncb, chunk, 0, unroll=HG_UNROLL)

        if n_x:
            @pl.when(pl.program_id(0) == nb - 1)
            def _():
                for cp in _x_copies(n_sib, x_in, x_out, x_sems):
                    cp.wait()

    row = pl.BlockSpec((rb, HG_W), lambda i: (nb - 1 - i, 0))
    one = pl.BlockSpec((1, HG_W), lambda i: (0, 0))
    any_spec = pl.BlockSpec(memory_space=pl.ANY)
    return pl.pallas_call(
        body, name="hgrn_bwd", grid=(nb,),
        in_specs=[row, row, row, row, pl.BlockSpec((ncb, HG_W, HG_DIM), lambda i: (nb - 1 - i, 0, 0)), one]
        + [any_spec] * n_x,
        out_specs=[row, row, row, one] + [any_spec] * n_x,
        out_shape=[jax.ShapeDtypeStruct((T, HG_W), MM)] * 3 + [jax.ShapeDtypeStruct((1, HG_W), F32)]
        + _x_out_shapes(xchg),
        scratch_shapes=[pltpu.VMEM((HG_HEADS, HG_DIM, HG_DIM), F32)] + _x_sems(xchg),
        compiler_params=_cparams(("arbitrary",)),
    )(hq, hf, hi, do, s0, lb, *xchg[0], *xchg[1])


def _silu_parts(x):
    sg = _sigmoid(x)
    return x * sg, sg * (1.0 + x * (1.0 - sg))


def _merge_fwd(attn, o, hg, bg, x, g_out, w_bra, w_brb, w_out, T, tm):
    def body(i, attn_ref, o_ref, hg_ref, bg_ref, x_ref, g_ref, wa_ref, wb_ref, wo_ref,
             x1_ref, ya_ref, yb_ref, m_ref, rec_ref):
        g = g_ref[...]

        def recurrent_out(rows):
            for h in range(HG_HEADS):
                ls = slice(h * HG_DIM, (h + 1) * HG_DIM)
                rec_ref[rows, ls] = (_rms(o_ref[rows, ls])[0] * g * _silu_parts(hg_ref[rows, ls])[0]).astype(MM)

        _by_chunks(tm, recurrent_out)
        ya_ref[...] = _dot(attn_ref[...], wa_ref[...])
        yb_ref[...] = jnp.dot(rec_ref[...], wb_ref[...], preferred_element_type=F32)

        def gate(rows):
            m_ref[rows, :] = (_sigmoid(bg_ref[rows, :D_MODEL]) * ya_ref[rows, :]
                              + _sigmoid(bg_ref[rows, D_MODEL:]) * yb_ref[rows, :]).astype(MM)

        _by_chunks(tm, gate)
        x1_ref[...] = x_ref[...] + jnp.dot(m_ref[...], wo_ref[...], preferred_element_type=F32)

    return _row_call("merge_fwd", body, T, tm, [attn, o, hg, bg, x], [g_out, w_bra, w_brb, w_out],
                     [(D_MODEL, F32), (D_MODEL, F32), (D_MODEL, F32), (D_MODEL, MM), (HG_W, MM)], [], VMEM_LIMIT)


def _ffn_fwd(x1, g_ffn, w_g, w_u, w_d, T, tm):
    def body(i, x1_ref, g_ref, wg_ref, wu_ref, wd_ref, x2_ref, gt_ref, up_ref, h2_ref, a_s):
        g = g_ref[...]

        def norm(rows):
            h2_ref[rows, :] = (_rms(x1_ref[rows, :])[0] * g).astype(MM)

        _by_chunks(tm, norm)
        gt_ref[...] = _dot_nt(h2_ref[...], wg_ref[...])
        up_ref[...] = _dot_nt(h2_ref[...], wu_ref[...])

        def act(rows):
            for cs in FFN_HALVES:
                a_s[rows, cs] = (_silu_parts(gt_ref[rows, cs])[0] * up_ref[rows, cs]).astype(MM)

        _by_chunks(tm, act)
        x2_ref[...] = x1_ref[...] + jnp.dot(a_s[...], wd_ref[...], preferred_element_type=F32)

    return _row_call("ffn_fwd", body, T, tm, [x1], [g_ffn, w_g, w_u, w_d],
                     [(D_MODEL, F32), (FFN, F32), (FFN, F32), (D_MODEL, MM)], [], VMEM_LIMIT,
                     scratch=[pltpu.VMEM((tm, FFN), MM)])


def _ple_loss(x2, p, tgt, g_pg, g_post, w_pg, w_pp, T, tm):
    def body(i, x2_ref, p_ref, t_ref, gpg_ref, gpo_ref, wpg_ref, wpp_ref,
             dx2_ref, loss_ref, dgpo_ref, dgpg_ref, dwpg_ref, dwpp_ref, u_s, n3_s, z_s, dz_s, du_s, dy_s, dn3_s):
        @pl.when(i == 0)
        def _():
            for ref in (loss_ref, dgpo_ref, dgpg_ref, dwpg_ref, dwpp_ref):
                ref[...] = jnp.zeros_like(ref)

        gpg, gpo = gpg_ref[...], gpo_ref[...]
        p_mm = p_ref[...].astype(MM)
        for d in range(N_DEV):
            u_s[:, d * HEAD_PAD:(d + 1) * HEAD_PAD] = jnp.dot(p_mm, wpp_ref[d], preferred_element_type=F32)

        def gate_input(rows):
            n3_s[rows, :] = (_rms(x2_ref[rows, :])[0] * gpg).astype(MM)

        _by_chunks(tm, gate_input)
        z_s[...] = jnp.dot(n3_s[...], wpg_ref[...], preferred_element_type=F32)

        def loss_and_back(rows):
            uh, ru = _rms(u_s[rows, :])
            e = uh * gpo
            gate = _sigmoid(z_s[rows, :])
            diff = x2_ref[rows, :] + gate * e - t_ref[rows, :]
            dy = diff * (1.0 / D_MODEL)
            de = dy * gate
            dz_s[rows, :] = (dy * e * gate * (1.0 - gate)).astype(MM)
            du_s[rows, :] = _rms_bwd(de * gpo, uh, ru).astype(MM)
            dy_s[rows, :] = dy
            loss_ref[...] += _fold8(diff * diff) * (0.5 / D_MODEL)
            dgpo_ref[...] += _fold8(de * uh)

        _by_chunks(tm, loss_and_back)
        dn3_s[...] = _dot_nt(dz_s[...], wpg_ref[...])

        def gate_norm_back(rows):
            x2h, r3 = _rms(x2_ref[rows, :])
            dn3 = dn3_s[rows, :]
            dx2_ref[rows, :] = dy_s[rows, :] + _rms_bwd(dn3 * gpg, x2h, r3)
            dgpg_ref[...] += _fold8(dn3 * x2h)

        _by_chunks(tm, gate_norm_back)
        dwpg_ref[...] += _dot_tn(n3_s[...], dz_s[...])
        for d in range(N_DEV):
            dwpp_ref[d] += _dot_tn(p_mm, du_s[:, d * HEAD_PAD:(d + 1) * HEAD_PAD])

    vec = ((8, D_MODEL), F32)
    wide = lambda dt: pltpu.VMEM((tm, D_MODEL), dt)
    return _row_call("ple_loss", body, T, tm, [x2, p, tgt], [g_pg, g_post, w_pg, w_pp], [(D_MODEL, F32)],
                     [vec, vec, vec, ((D_MODEL, D_MODEL), F32), ((N_DEV, PLE, HEAD_PAD), F32)], VMEM_LIMIT,
                     scratch=[wide(F32), wide(MM), wide(F32), wide(MM), wide(MM), wide(F32), wide(F32)])


def _ffn_bwd(dx2, x1, gt, up, g_ffn, w_g, w_u, w_d, T, tm):
    def body(i, dx2_ref, x1_ref, gt_ref, up_ref, g_ref, wg_ref, wu_ref, wd_ref,
             dx1_ref, a_ref, dgt_ref, dup_ref, dg_ref, da_s, dh2_s):
        @pl.when(i == 0)
        def _():
            dg_ref[...] = jnp.zeros_like(dg_ref)

        g = g_ref[...]
        da_s[...] = _dot_nt(dx2_ref[...], wd_ref[...])

        def act_back(rows):
            for cs in FFN_HALVES:
                up, da = up_ref[rows, cs], da_s[rows, cs]
                silu, dsilu = _silu_parts(gt_ref[rows, cs])
                dgt_ref[rows, cs] = (da * up * dsilu).astype(MM)
                dup_ref[rows, cs] = (da * silu).astype(MM)
                a_ref[rows, cs] = (silu * up).astype(MM)

        _by_chunks(tm, act_back)
        dh2_s[...] = (jnp.dot(dgt_ref[...], wg_ref[...], preferred_element_type=F32)
                      + jnp.dot(dup_ref[...], wu_ref[...], preferred_element_type=F32))

        def norm_back(rows):
            x1h, r = _rms(x1_ref[rows, :])
            dh2 = dh2_s[rows, :]
            dx1_ref[rows, :] = dx2_ref[rows, :] + _rms_bwd(dh2 * g, x1h, r)
            dg_ref[...] += _fold8(dh2 * x1h)

        _by_chunks(tm, norm_back)

    return _row_call("ffn_bwd", body, T, tm, [dx2, x1, gt, up], [g_ffn, w_g, w_u, w_d],
                     [(D_MODEL, F32), (FFN, MM), (FFN, MM), (FFN, MM)], [((8, D_MODEL), F32)], VMEM_LIMIT,
                     scratch=[pltpu.VMEM((tm, FFN), F32), pltpu.VMEM((tm, D_MODEL), F32)])


def _merge_bwd(dx1, ya, yb, bg, o, hg, attn, m, rec, g_out, w_bra, w_brb, w_out, T, tm, xchg=((), ())):
    def body(i, dx1_ref, ya_ref, yb_ref, bg_ref, o_ref, hg_ref, attn_ref, m_ref, rec_ref, g_ref, wa_ref, wb_ref, wo_ref,
             dattn_ref, do_ref, dhg_ref, dbg_ref, dg_ref, dwo_ref, dwa_ref, dwb_ref, dm_s, dya_s, dyb_s, drec_s):
        @pl.when(i == 0)
        def _():
            for ref in (dg_ref, dwo_ref, dwa_ref, dwb_ref):
                ref[...] = jnp.zeros_like(ref)

        g = g_ref[...]
        dx1 = dx1_ref[...].astype(MM)
        dm_s[...] = _dot_nt(dx1, wo_ref[...])

        def gate_back(rows):
            dm = dm_s[rows, :]
            ga, gb = _sigmoid(bg_ref[rows, :D_MODEL]), _sigmoid(bg_ref[rows, D_MODEL:])
            dya_s[rows, :] = (dm * ga).astype(MM)
            dyb_s[rows, :] = (dm * gb).astype(MM)
            dbg_ref[rows, :D_MODEL] = (dm * ya_ref[rows, :] * ga * (1.0 - ga)).astype(MM)
            dbg_ref[rows, D_MODEL:] = (dm * yb_ref[rows, :] * gb * (1.0 - gb)).astype(MM)

        _by_chunks(tm, gate_back)
        dwo_ref[...] += _dot_tn(m_ref[...], dx1)
        attn_mm = attn_ref[...].astype(MM)
        for d in range(N_DEV):
            ds = slice(d * HEAD_PAD, (d + 1) * HEAD_PAD)
            dwa_ref[d] += _dot_tn(attn_mm, dya_s[:, ds])
            dwb_ref[d] += _dot_tn(rec_ref[...], dyb_s[:, ds])
        dattn_ref[...] = _dot_nt(dya_s[...], wa_ref[...])
        drec_s[...] = _dot_nt(dyb_s[...], wb_ref[...])

        def recurrent_out_back(rows):
            for h in range(HG_HEADS):
                ls = slice(h * HG_DIM, (h + 1) * HG_DIM)
                oh, r = _rms(o_ref[rows, ls])
                silu, dsilu = _silu_parts(hg_ref[rows, ls])
                dr = drec_s[rows, ls]
                dhg_ref[rows, ls] = (dr * oh * g * dsilu).astype(MM)
                don = dr * silu
                dg_ref[...] += _fold8(don * oh)
                do_ref[rows, ls] = _rms_bwd(don * g, oh, r)

        _by_chunks(tm, recurrent_out_back)

    wide = lambda n, dt: pltpu.VMEM((tm, n), dt)
    return _row_call("merge_bwd", body, T, tm, [dx1, ya, yb, bg, o, hg, attn, m, rec], [g_out, w_bra, w_brb, w_out],
                     [(D_MODEL, F32), (HG_W, F32), (HG_W, MM), (2 * D_MODEL, MM)],
                     [((8, HG_DIM), F32), ((D_MODEL, D_MODEL), F32), ((N_DEV, MLA_HEADS * HEAD_PAD, HEAD_PAD), F32),
                      ((N_DEV, HG_W, HEAD_PAD), F32)], VMEM_LIMIT,
                     scratch=[wide(D_MODEL, F32), wide(D_MODEL, MM), wide(D_MODEL, MM), wide(HG_W, F32)], xchg=xchg)


def _flash_bwd(qf, kf, vf, o, do, lse, T, xchg=((), ())):
    tq = min(ATT_TILE, T)
    nq = T // tq

    qi_tab, ki_tab = _causal_pairs(nq, by_query=False)

    n_x, n_sib = _x_count(xchg), len(xchg[0])
    hp = ATT_HEADS
    n_heads, n_pairs = MLA_HEADS // hp, len(qi_tab)

    def body(qi_ref, ki_ref, q_ref, k_ref, v_ref, o_ref, do_ref, lse_ref, *rest):
        x_in, (dq_ref, dk_ref, dv_ref), rest = rest[:n_x], rest[n_x:n_x + 3], rest[n_x + 3:]
        x_out, x_sems = rest[:n_x], rest[n_x:]
        t = pl.program_id(1)
        qi, ki = qi_ref[t], ki_ref[t]
        if n_x:
            @pl.when((pl.program_id(0) == 0) & (t == 0))
            def _():
                for cp in _x_copies(n_sib, x_in, x_out, x_sems):
                    cp.start()

        @pl.when(t == 0)
        def _():
            dq_ref[...] = jnp.zeros_like(dq_ref)

        def step(first):
            halves = 2 if first and tq % (2 * HEAD_PAD) == 0 else 1
            w = tq // halves
            for hh in range(hp):
                hs = slice(hh * HEAD_PAD, (hh + 1) * HEAD_PAD)
                for part in range(halves):
                    keys, qs = slice(part * w, (part + 1) * w), slice(part * w, tq)
                    nq_ = tq - part * w
                    q, k, d_o = q_ref[qs, hs], k_ref[keys, hs], do_ref[qs, hs]
                    s = _dot_nt(q, k)
                    if first:
                        row = lax.broadcasted_iota(jnp.int32, (nq_, w), 0)
                        col = lax.broadcasted_iota(jnp.int32, (nq_, w), 1)
                        s = jnp.where(col <= row, s, NEG)
                    p = jnp.exp(s - lse_ref[qs, hh * HEAD_PAD:hh * HEAD_PAD + 1])
                    delta = jnp.sum(d_o * o_ref[qs, hs], axis=1, keepdims=True)
                    ds = p * (_dot_nt(d_o, v_ref[keys, hs]) - delta)
                    rows = pl.ds(pl.multiple_of(qi * tq + part * w, w), nq_)
                    dq_ref[rows, hs] += _dot(ds, k)
                    if first:
                        dv_ref[keys, hs] = _dot_tn(p, d_o)
                        dk_ref[keys, hs] = _dot_tn(ds, q)
                    else:
                        dv_ref[keys, hs] += _dot_tn(p, d_o)
                        dk_ref[keys, hs] += _dot_tn(ds, q)

        @pl.when(qi == ki)
        def _():
            step(True)

        @pl.when(qi > ki)
        def _():
            step(False)

        if n_x:
            @pl.when((pl.program_id(0) == n_heads - 1) & (t == n_pairs - 1))
            def _():
                for cp in _x_copies(n_sib, x_in, x_out, x_sems):
                    cp.wait()

    q_spec = pl.BlockSpec((tq, hp * HEAD_PAD), lambda h, t, qi_ref, ki_ref: (qi_ref[t], h))
    kv_spec = pl.BlockSpec((tq, hp * HEAD_PAD), lambda h, t, qi_ref, ki_ref: (ki_ref[t], h))
    any_spec = pl.BlockSpec(memory_space=pl.ANY)
    w = MLA_HEADS * HEAD_PAD
    grid_spec = pltpu.PrefetchScalarGridSpec(
        num_scalar_prefetch=2, grid=(n_heads, n_pairs),
        in_specs=[q_spec, kv_spec, kv_spec, q_spec, q_spec, q_spec] + [any_spec] * n_x,
        out_specs=[pl.BlockSpec((T, hp * HEAD_PAD), lambda h, t, qi_ref, ki_ref: (0, h)), kv_spec, kv_spec]
        + [any_spec] * n_x,
        scratch_shapes=_x_sems(xchg))
    return pl.pallas_call(
        body, name="flash_bwd", grid_spec=grid_spec,
        out_shape=[jax.ShapeDtypeStruct((T, w), F32)] * 3 + _x_out_shapes(xchg),
        compiler_params=_cparams(("arbitrary", "arbitrary")),
    )(jnp.asarray(qi_tab), jnp.asarray(ki_tab), qf, kf, vf, o, do, lse, *xchg[0], *xchg[1])


def _mla_heads_bwd(d_out, saved, g_pad, cos_t, sin_t, first):
    d_raw, dg = [], jnp.zeros((1, HEAD_PAD), F32)
    for h in range(MLA_HEADS):
        xh, r = saved[h]
        dy = d_out[:, h * HEAD_PAD:(h + 1) * HEAD_PAD]
        dn = dy * cos_t + _rope_swap(dy * sin_t, first)
        dg = dg + jnp.sum(dn * xh, axis=0, keepdims=True)
        d_raw.append(_rms_bwd(dn * g_pad, xh, r, QK_DIM))
    return d_raw, dg


def _mla_prep_bwd(cq, ckv, kr, pos, dqf, dkf, dvf, g_qa, g_kva, g_qn, g_kn, w_uq, w_ukv, T, tm):
    def body(i, cq_ref, ckv_ref, kr_ref, pos_ref, dq_ref, dk_ref, dv_ref,
             gqa_ref, gkva_ref, gqn_ref, gkn_ref, wuq_ref, wukv_ref,
             dcq_ref, dckv_ref, dkr_ref, dgqa_ref, dgkva_ref, dgqn_ref, dgkn_ref, dwuq_ref, dwukv_ref):
        cos_t, sin_t, first = _rope_tables(pos_ref[...], tm)
        cqh, rq = _rms(cq_ref[...])
        ckvh, rkv = _rms(ckv_ref[...])
        cqn, ckvn = cqh * gqa_ref[...], ckvh * gkva_ref[...]
        q_raw, k_raw, _ = _mla_raw_heads(cqn, ckvn, kr_ref[...], wuq_ref, wukv_ref, tm)
        _, q_saved = _mla_heads_fwd(q_raw, gqn_ref[...], cos_t, sin_t, first)
        _, k_saved = _mla_heads_fwd(k_raw, gkn_ref[...], cos_t, sin_t, first)
        dq_heads, dgqn = _mla_heads_bwd(dq_ref[...] * ATT_SCALE, q_saved, gqn_ref[...], cos_t, sin_t, first)
        dk_heads, dgkn = _mla_heads_bwd(dk_ref[...], k_saved, gkn_ref[...], cos_t, sin_t, first)
        lane = lax.broadcasted_iota(jnp.int32, (tm, HEAD_PAD), 1)
        nope = lane < QK_NOPE
        dcqn = jnp.zeros((tm, Q_RANK), F32)
        dckvn = jnp.zeros((tm, KV_RANK), F32)
        dkr = jnp.zeros((tm, HEAD_PAD), F32)
        cqn_mm, ckvn_mm = cqn.astype(MM), ckvn.astype(MM)
        for h in range(MLA_HEADS):
            hs = slice(h * HEAD_PAD, (h + 1) * HEAD_PAD)
            dq_h = dq_heads[h].astype(MM)
            dkv_h = jnp.where(nope, dk_heads[h], pltpu.roll(dv_ref[:, hs], V_DIM, 1)).astype(MM)
            _acc(dwuq_ref.at[h], i, _dot_tn(dq_h, cqn_mm))
            _acc(dwukv_ref.at[h], i, _dot_tn(ckvn_mm, dkv_h))
            dcqn = dcqn + jnp.dot(dq_h, wuq_ref[h], preferred_element_type=F32)
            dckvn = dckvn + lax.dot_general(dkv_h, wukv_ref[h], (((1,), (1,)), ((), ())), preferred_element_type=F32)
            dkr = dkr + dk_heads[h]
        dkr_ref[...] = jnp.where((lane >= QK_NOPE) & (lane < QK_DIM), dkr, 0.0).astype(MM)
        dcq_ref[...] = _rms_bwd(dcqn * gqa_ref[...], cqh, rq).astype(MM)
        dckv_ref[...] = _rms_bwd(dckvn * gkva_ref[...], ckvh, rkv).astype(MM)
        _acc(dgqa_ref, i, jnp.sum(dcqn * cqh, axis=0, keepdims=True))
        _acc(dgkva_ref, i, jnp.sum(dckvn * ckvh, axis=0, keepdims=True))
        _acc(dgqn_ref, i, dgqn)
        _acc(dgkn_ref, i, dgkn)

    return _row_call(
        "mla_prep_bwd", body, T, tm, [cq, ckv, kr, pos, dqf, dkf, dvf], [g_qa, g_kva, g_qn, g_kn, w_uq, w_ukv],
        [(Q_RANK, MM), (KV_RANK, MM), (HEAD_PAD, MM)],
        [((1, Q_RANK), F32), ((1, KV_RANK), F32), ((1, HEAD_PAD), F32), ((1, HEAD_PAD), F32),
         ((MLA_HEADS, HEAD_PAD, Q_RANK), F32), ((MLA_HEADS, KV_RANK, HEAD_PAD), F32)], VMEM_LIMIT)


def _in_proj_bwd(x, dx1, dsecs, g_mix, w_in, T, tm):
    def body(i, x_ref, dx1_ref, *rest):
        d_refs, (g_ref, w_ref, dx_ref, dp_ref, dg_ref, dh_s) = rest[:len(COL_SECTIONS)], rest[len(COL_SECTIONS):]

        @pl.when(i == 0)
        def _():
            dg_ref[...] = jnp.zeros_like(dg_ref)

        g = g_ref[...]

        def join_and_cut(rows):
            pieces = [(d_ref[rows, QK_NOPE:QK_DIM] if n == QK_ROPE else d_ref[rows, :]).astype(F32)
                      for (_, n), d_ref in zip(COL_SECTIONS, d_refs)]
            dproj = jnp.concatenate(pieces, axis=1)
            for d in range(N_DEV):
                dp_ref[d, rows, :] = dproj[:, d * IN_BLOCK:(d + 1) * IN_BLOCK].astype(MM)

        _by_chunks(tm, join_and_cut)
        dh = jnp.dot(dp_ref[0], w_ref[0], preferred_element_type=F32)
        for d in range(1, N_DEV):
            dh = dh + jnp.dot(dp_ref[d], w_ref[d], preferred_element_type=F32)
        dh_s[...] = dh

        def norm_back(rows):
            xh, r = _rms(x_ref[rows, :])
            dh_c = dh_s[rows, :]
            dx_ref[rows, :] = dx1_ref[rows, :] + _rms_bwd(dh_c * g, xh, r)
            dg_ref[...] += _fold8(dh_c * xh)

        _by_chunks(tm, norm_back)

    in_specs = [pl.BlockSpec((tm, a.shape[1]), lambda i: (i, 0)) for a in [x, dx1, *dsecs]]
    in_specs += [pl.BlockSpec(g_mix.shape, lambda i: (0, 0)),
                 pl.BlockSpec(w_in.shape, lambda i: (0, 0, 0), pipeline_mode=pl.Buffered(1))]

    def kern(*refs):
        body(pl.program_id(0), *refs)

    return pl.pallas_call(
        kern, name="in_proj_bwd", grid=(T // tm,), in_specs=in_specs,
        out_specs=[pl.BlockSpec((tm, D_MODEL), lambda i: (i, 0)),
                   pl.BlockSpec((N_DEV, tm, IN_BLOCK), lambda i: (0, i, 0)),
                   pl.BlockSpec((8, D_MODEL), lambda i: (0, 0))],
        out_shape=[jax.ShapeDtypeStruct((T, D_MODEL), F32), jax.ShapeDtypeStruct((N_DEV, T, IN_BLOCK), MM),
                   jax.ShapeDtypeStruct((8, D_MODEL), F32)],
        scratch_shapes=[pltpu.VMEM((tm, D_MODEL), F32)],
        compiler_params=_cparams(("arbitrary",), VMEM_LIMIT),
    )(x, dx1, *dsecs, g_mix, w_in)


def _pick_block(n, cap):
    best = None
    for cand in range(128, min(n, cap) + 1, 128):
        if n % cand == 0:
            best = cand
    return n if best is None else best


def _matmul_tn(name, a, b):
    T, M = a.shape
    N = b.shape[1]
    bm, bk = _pick_block(M, 1408), min(512, T)
    bn = _pick_block(N, 2560)

    def body(a_ref, b_ref, c_ref):
        @pl.when(pl.program_id(2) == 0)
        def _():
            c_ref[...] = jnp.zeros_like(c_ref)

        c_ref[...] += _dot_tn(a_ref[...], b_ref[...])

    return pl.pallas_call(
        body, name=name, grid=(M // bm, N // bn, T // bk),
        in_specs=[pl.BlockSpec((bk, bm), lambda i, j, k: (k, i)), pl.BlockSpec((bk, bn), lambda i, j, k: (k, j))],
        out_specs=pl.BlockSpec((bm, bn), lambda i, j, k: (i, j)), out_shape=jax.ShapeDtypeStruct((M, N), F32),
        compiler_params=_cparams(("parallel", "parallel", "arbitrary"), VMEM_LIMIT),
    )(a, b)


def _matmul_tn_blocks(name, a, b):
    T, M = a.shape
    nd, _, c = b.shape
    bm, bk = _pick_block(M, 512), min(512, T)

    def body(a_ref, b_ref, c_ref):
        @pl.when(pl.program_id(1) == 0)
        def _():
            c_ref[...] = jnp.zeros_like(c_ref)

        a_blk = a_ref[...].astype(MM)
        for d in range(nd):
            c_ref[d] += _dot_tn(b_ref[d], a_blk)

    return pl.pallas_call(
        body, name=name, grid=(M // bm, T // bk),
        in_specs=[pl.BlockSpec((bk, bm), lambda i, k: (k, i)), pl.BlockSpec((nd, bk, c), lambda i, k: (0, k, 0))],
        out_specs=pl.BlockSpec((nd, c, bm), lambda i, k: (0, 0, i)),
        out_shape=jax.ShapeDtypeStruct((nd, c, M), F32),
        compiler_params=_cparams(("parallel", "arbitrary"), VMEM_LIMIT),
    )(a, b)


def _pad_gain(g, n):
    return jnp.pad(g.reshape(1, -1), ((0, 0), (0, n - g.shape[-1])))


GROUP_A = ("w_ffn_gate", "w_ffn_up", "w_ffn_down", "w_ple_gate", "w_ple_proj")
GROUP_B = ("w_branch", "w_out")
GROUP_C = ("w_in", "w_uq", "w_ukv")
EARLY = GROUP_C
LATE = GROUP_B + GROUP_A
TRANSPOSED = ("w_in", "w_uq", "w_ffn_gate", "w_ffn_up")


def _local_step(x, p, pos, tgt, small, big, late_blocks=None, core=None):
    T = x.shape[0]
    tm = min(ROW_TILE, T)
    tw = min(WIDE_TILE, T)
    w_in = big["w_in"]
    w_uq = jnp.pad(big["w_uq"], ((0, 0), (0, HEAD_PAD - QK_DIM), (0, 0)))
    w_ukv = big["w_ukv"]

    g_mix, g_qa, g_kva = small["mix_norm_g"], small["q_a_norm_g"], small["kv_a_norm_g"]
    g_qn, g_kn = _pad_gain(small["q_norm_g"], HEAD_PAD), _pad_gain(small["k_norm_g"], HEAD_PAD)
    g_out, g_ffn = small["hg_out_norm_g"], small["ffn_norm_g"]
    g_pg, g_post = small["ple_gate_norm_g"], small["ple_post_norm_g"]
    logits = small["hg_lb_logits"]
    lb = _lower_bound(logits)

    h, cq, ckv, kr, hq, hf, hi, hg, bg = _in_proj_fwd(x, g_mix, w_in, T, tm)
    qf, kf, vf = _mla_prep_fwd(cq, ckv, kr, pos, g_qa, g_kva, g_qn, g_kn, w_uq, w_ukv, T, tw)
    if late_blocks is None:
        attn, lse = _flash_fwd(qf, kf, vf, T)
    else:
        attn, lse, *late = _flash_fwd(qf, kf, vf, T, ag_blocks=[late_blocks[n] for n in LATE])
        big = {**big, **dict(zip(LATE, late))}
    o, s0 = _hgrn_fwd(hq, hf, hi, lb, T)
    w_branch = jnp.moveaxis(big["w_branch"].reshape(N_DEV, 2, HG_W, HEAD_PAD), 0, 2).reshape(2, HG_W, D_MODEL)
    w_bra = jnp.pad(w_branch[0].reshape(MLA_HEADS, V_DIM, D_MODEL),
                    ((0, 0), (0, HEAD_PAD - V_DIM), (0, 0))).reshape(MLA_HEADS * HEAD_PAD, D_MODEL)
    w_brb = w_branch[1]
    w_out = big["w_out"].reshape(D_MODEL, D_MODEL)
    w_g, w_u = big["w_ffn_gate"].reshape(FFN, D_MODEL), big["w_ffn_up"].reshape(FFN, D_MODEL)
    w_d = big["w_ffn_down"].reshape(FFN, D_MODEL)
    w_pg, w_pp = big["w_ple_gate"].reshape(D_MODEL, D_MODEL), big["w_ple_proj"]
    x1, ya, yb, m, rec = _merge_fwd(attn, o, hg, bg, x, g_out, w_bra, w_brb, w_out, T, tw)
    x2, gt, up, h2 = _ffn_fwd(x1, g_ffn, w_g, w_u, w_d, T, tm)
    dx2, loss_p, dg_post, dg_pg, d_pg, d_pp = _ple_loss(x2, p, tgt, g_pg, g_post, w_pg, w_pp, T, tw)

    grads, sibs, gots = {}, {}, {}
    dist = core is not None
    pick = lambda names: [grads[n] for n in names] if dist else ()

    def partials(tag, names, got):
        if not dist:
            return ()
        sibs.update(zip(names, got))
        return _chip_partials("rs_partial_" + tag, pick(names), got, core)

    dx1, a, dgt, dup, dg_ffn = _ffn_bwd(dx2, x1, gt, up, g_ffn, w_g, w_u, w_d, T, tm)
    grads["w_ffn_gate"] = _matmul_tn("dw_gate", dgt, h2).reshape(N_DEV, -1, D_MODEL)
    grads["w_ffn_up"] = _matmul_tn("dw_up", dup, h2).reshape(N_DEV, -1, D_MODEL)
    grads["w_ffn_down"] = _matmul_tn("dw_down", a, dx2).reshape(N_DEV, -1, D_MODEL)
    grads["w_ple_gate"] = d_pg.reshape(N_DEV, -1, D_MODEL)
    grads["w_ple_proj"] = d_pp

    dattn, do, dhg, dbg, dg_out, d_out, d_bra, d_brb, *sib_a = _merge_bwd(
        dx1, ya, yb, bg, o, hg, attn, m, rec, g_out, w_bra, w_brb, w_out, T, tm, xchg=(pick(GROUP_A), ()))
    parts_a = partials("a", GROUP_A, sib_a)
    d_bra = d_bra.reshape(N_DEV, MLA_HEADS, HEAD_PAD, HEAD_PAD)[:, :, :V_DIM].reshape(N_DEV, HG_W, HEAD_PAD)
    grads["w_branch"] = jnp.concatenate([d_bra, d_brb], axis=1)
    grads["w_out"] = d_out.reshape(N_DEV, -1, D_MODEL)

    dhq, dhf, dhi, dlb, *got = _hgrn_bwd(hq, hf, hi, do, s0, lb, T, xchg=(pick(GROUP_B), parts_a))
    sib_b, got_a = got[:len(GROUP_B)], got[len(GROUP_B):]
    parts_b = partials("b", GROUP_B, sib_b)
    dqf, dkf, dvf, *got_b = _flash_bwd(qf, kf, vf, attn, dattn, lse, T, xchg=((), parts_b))
    (dcq, dckv, dkr, dg_qa, dg_kva, dg_qn, dg_kn, d_uq, d_ukv) = _mla_prep_bwd(
        cq, ckv, kr, pos, dqf, dkf, dvf, g_qa, g_kva, g_qn, g_kn, w_uq, w_ukv, T, tw)
    grad_x, dproj, dg_mix = _in_proj_bwd(x, dx1, [dcq, dckv, dkr, dhq, dhf, dhi, dhg, dbg], g_mix, w_in, T, tm)
    grads["w_in"] = _matmul_tn_blocks("dw_in", h, dproj)
    grads["w_uq"] = d_uq[:, :QK_DIM]
    grads["w_ukv"] = d_ukv
    parts_c = ()
    if dist:
        parts_c = partials("c", GROUP_C, _exchange_sibling("rs_sibling_c", pick(GROUP_C)))
        gots.update(zip(GROUP_A, got_a))
        gots.update(zip(GROUP_B, got_b))

    dl0 = dlb * lb * (1.0 - lb)
    small_g = {
        "mix_norm_g": dg_mix, "q_a_norm_g": dg_qa, "kv_a_norm_g": dg_kva, "q_norm_g": dg_qn, "k_norm_g": dg_kn,
        "hg_lb_logits": jnp.concatenate([dl0, -dl0], axis=0), "hg_out_norm_g": dg_out,
        "ffn_norm_g": dg_ffn, "ple_gate_norm_g": dg_pg, "ple_post_norm_g": dg_post,
    }
    return loss_p, grad_x, small_g, grads, sibs, gots, parts_c


def _lower_bound(logits):
    def body(l_ref, lb_ref):
        l = l_ref[...]
        mx = jnp.max(l, axis=0, keepdims=True)
        e = jnp.exp(l - mx)
        lb_ref[...] = e[0:1] / jnp.sum(e, axis=0, keepdims=True)

    return pl.pallas_call(body, name="lower_bound", out_shape=jax.ShapeDtypeStruct((1, HG_W), F32))(logits)


def _my_place():
    return lax.axis_index("x"), lax.axis_index("y"), lax.axis_index("c")


def _all_gather(name, blocks):
    n = len(blocks)

    def body(*refs):
        x_refs, out_refs, sems = refs[:n], refs[n:2 * n], refs[2 * n:]
        _ag_start(x_refs, out_refs, sems)
        _ag_finish(x_refs, out_refs, sems)

    any_spec = pl.BlockSpec(memory_space=pl.ANY)
    return pl.pallas_call(
        body, name=name, out_shape=_ag_out_shapes(blocks),
        in_specs=[any_spec] * n, out_specs=[any_spec] * n, scratch_shapes=_ag_sems(n),
    )(*blocks)


def _ag_out_shapes(blocks):
    return [jax.ShapeDtypeStruct((N_DEV,) + b.shape, b.dtype) for b in blocks]


def _ag_sems(n):
    return [pltpu.SemaphoreType.DMA((7 * n,)), pltpu.SemaphoreType.DMA((7 * n,)), pltpu.SemaphoreType.DMA((n,))]


def _ag_parts(x_refs, out_refs, sems):
    send_sems, recv_sems, local_sems = sems
    x, y, c = _my_place()
    me, sibling = (x, y, c), (x, y, 1 - c)
    chips = [(1 - x, y), (x, 1 - y), (1 - x, 1 - y)]
    n = len(x_refs)

    def copy(a, k, block, to, own=False):
        px, py, pc = block
        dst = out_refs[a].at[4 * px + 2 * py + pc]
        return pltpu.make_async_remote_copy(
            src_ref=x_refs[a] if own else dst, dst_ref=dst, send_sem=send_sems.at[7 * a + k],
            recv_sem=recv_sems.at[7 * a + k], device_id=to, device_id_type=MESH_ID)

    mine = [pltpu.make_async_copy(x_refs[a], out_refs[a].at[4 * x + 2 * y + c], local_sems.at[a]) for a in range(n)]
    first = []
    for a in range(n):
        first.append(copy(a, 0, me, sibling, own=True))
        first += [copy(a, 1 + j, me, (*chip, c), own=True) for j, chip in enumerate(chips)]
    return copy, mine, first, me, sibling, chips, c, n


def _ag_start(x_refs, out_refs, sems):
    _, mine, first, *_ = _ag_parts(x_refs, out_refs, sems)
    for cp in mine + first:
        cp.start()


def _ag_finish(x_refs, out_refs, sems):
    copy, mine, first, me, sibling, chips, c, n = _ag_parts(x_refs, out_refs, sems)
    passed = []
    for j, chip in enumerate(chips):
        for a in range(n):
            copy(a, 1 + j, (*chip, c), me).wait_recv()
            passed.append(copy(a, 4 + j, (*chip, c), sibling))
            passed[-1].start()
    for a in range(n):
        copy(a, 0, sibling, me).wait_recv()
    for j, chip in enumerate(chips):
        for a in range(n):
            copy(a, 4 + j, (*chip, 1 - c), me).wait_recv()
    for cp in first + passed:
        cp.wait_send()
    for cp in mine:
        cp.wait()


def _exchange_sibling(name, gs):
    return _exchange(name, (gs, ()))


def _exchange(name, xchg):
    n = _x_count(xchg)

    def body(*refs):
        in_refs, out_refs, sems = refs[:n], refs[n:2 * n], refs[2 * n:]
        for cp in _x_copies(len(xchg[0]), in_refs, out_refs, sems):
            cp.start()
        for cp in _x_copies(len(xchg[0]), in_refs, out_refs, sems):
            cp.wait()

    any_spec = pl.BlockSpec(memory_space=pl.ANY)
    return pl.pallas_call(
        body, name=name, out_shape=_x_out_shapes(xchg), in_specs=[any_spec] * n, out_specs=[any_spec] * n,
        scratch_shapes=_x_sems(xchg),
    )(*xchg[0], *xchg[1])


N_PARTS = 4


def _part_spec(rows, cols, t_pos, lead_block=(), lead_index=lambda *args: ()):
    if rows % (16 * N_PARTS) == 0:
        axis, shape, count = 0, (rows // N_PARTS, cols), N_PARTS
    elif cols % (128 * N_PARTS) == 0:
        axis, shape, count = 1, (rows, cols // N_PARTS), N_PARTS
    else:
        axis, shape, count = 0, (rows, cols), 1

    def index(*args):
        i = jnp.minimum(args[t_pos], count - 1)
        return (*lead_index(*args), *((i, 0) if axis == 0 else (0, i)))

    return pl.BlockSpec((*lead_block, *shape), index)


def _chip_partials(name, gs, sibs, c_idx):
    n = len(gs)

    def body(c_ref, *refs):
        for g_ref, sib_ref, out_ref in zip(refs[:n], refs[n:2 * n], refs[2 * n:]):
            out_ref[...] = (g_ref[...] + sib_ref[...]).astype(MM)

    own = [_part_spec(*g.shape[1:], 1, (1,), lambda j, t, c_ref: (2 * j + c_ref[0],)) for g in gs]
    by_chip = [_part_spec(*g.shape[1:], 1, (1,), lambda j, t, c_ref: (j,)) for g in gs]
    grid_spec = pltpu.PrefetchScalarGridSpec(
        num_scalar_prefetch=1, grid=(4, N_PARTS), in_specs=own + by_chip, out_specs=by_chip)
    return pl.pallas_call(
        body, name=name, grid_spec=grid_spec, out_shape=[jax.ShapeDtypeStruct((4,) + g.shape[1:], MM) for g in gs],
        compiler_params=_cparams(("arbitrary", "arbitrary"), VMEM_LIMIT),
    )(c_idx, *gs, *sibs)


def _exchange_chips(parts):
    return _exchange("rs_chips", ((), parts))


def _x_count(xchg):
    return len(xchg[0]) + len(xchg[1])


def _x_out_shapes(xchg):
    return ([jax.ShapeDtypeStruct((4,) + g.shape[1:], g.dtype) for g in xchg[0]]
            + [jax.ShapeDtypeStruct((3,) + p.shape[1:], p.dtype) for p in xchg[1]])


def _x_sems(xchg):
    n = 4 * len(xchg[0]) + 3 * len(xchg[1])
    return [pltpu.SemaphoreType.DMA((n,)), pltpu.SemaphoreType.DMA((n,))] if n else []


def _x_copies(n_sib, in_refs, out_refs, sems):
    if not in_refs:
        return []
    send_sems, recv_sems = sems
    x, y, c = _my_place()
    chips = [(1 - x, y), (x, 1 - y), (1 - x, 1 - y)]
    copies = []

    def add(src, dst, to):
        k = len(copies)
        copies.append(pltpu.make_async_remote_copy(
            src_ref=src, dst_ref=dst, send_sem=send_sems.at[k], recv_sem=recv_sems.at[k], device_id=to,
            device_id_type=MESH_ID))

    for a, (src, dst) in enumerate(zip(in_refs, out_refs)):
        if a < n_sib:
            for j in range(4):
                add(src.at[2 * j + 1 - c], dst.at[j], (x, y, 1 - c))
        else:
            for k, (px, py) in enumerate(chips):
                add(src.at[2 * px + py], dst.at[k], (px, py, c))
    return copies


def _adamw_math(w, g, m, v):
    m = ADAM_B1 * m + (1.0 - ADAM_B1) * g
    v = ADAM_B2 * v + (1.0 - ADAM_B2) * jnp.square(g)
    m_hat = m / (1.0 - ADAM_B1 ** ADAM_STEP)
    v_hat = v / (1.0 - ADAM_B2 ** ADAM_STEP)
    delta = -ADAM_LR * (m_hat / (jnp.sqrt(v_hat) + ADAM_EPS) + ADAM_WD * w)
    return delta, m, v


def _sum_adamws(name, gs, sibs, gots, ws, ms, vs, slot_idx, chip_idx):
    n = len(gs)

    def body(s_ref, j_ref, *refs):
        ins, outs = refs[:6 * n], refs[6 * n:]
        for a in range(n):
            g_ref, sib_ref, got_ref, w_ref, m_ref, v_ref = (ins[k * n + a] for k in range(6))
            go_ref, d_ref, m2_ref, v2_ref = outs[4 * a:4 * a + 4]
            grad = g_ref[0] + sib_ref[0]
            for k in range(3):
                grad = grad + got_ref[k].astype(F32)
            go_ref[...] = grad
            d_ref[...], m2_ref[...], v2_ref[...] = _adamw_math(w_ref[...], grad, m_ref[...], v_ref[...])

    shapes = [g.shape[1:] for g in gs]
    flat = [_part_spec(*s, 0) for s in shapes]
    in_specs = ([_part_spec(*s, 0, (1,), lambda t, s_ref, j_ref: (s_ref[0],)) for s in shapes]
                + [_part_spec(*s, 0, (1,), lambda t, s_ref, j_ref: (j_ref[0],)) for s in shapes]
                + [_part_spec(*s, 0, (3,), lambda t, s_ref, j_ref: (0,)) for s in shapes] + flat * 3)
    grid_spec = pltpu.PrefetchScalarGridSpec(
        num_scalar_prefetch=2, grid=(N_PARTS,), in_specs=in_specs, out_specs=[f for f in flat for _ in range(4)])
    res = pl.pallas_call(
        body, name=name, grid_spec=grid_spec,
        out_shape=[jax.ShapeDtypeStruct(s, F32) for s in shapes for _ in range(4)],
        compiler_params=_cparams(("arbitrary",), VMEM_LIMIT),
    )(slot_idx, chip_idx, *gs, *sibs, *gots, *ws, *ms, *vs)
    return [res[4 * a:4 * a + 4] for a in range(n)]


BIG = ("w_in", "w_uq", "w_ukv", "w_branch", "w_out", "w_ffn_gate", "w_ffn_up", "w_ffn_down", "w_ple_gate", "w_ple_proj")
SMALL = (
    ("mix_norm_g", 0, 1, 1024), ("q_a_norm_g", 1, 1, 384), ("kv_a_norm_g", 2, 1, 256), ("q_norm_g", 3, 1, 96),
    ("k_norm_g", 4, 1, 96), ("hg_lb_logits", 5, 2, 512), ("hg_out_norm_g", 7, 1, 128), ("ffn_norm_g", 8, 1, 1024),
    ("ple_gate_norm_g", 9, 1, 1024), ("ple_post_norm_g", 10, 1, 1024),
)
SLAB_ROWS, LOSS_ROW = 16, 15


def _pack_partials(small_g, loss_p):
    def body(*refs):
        val_refs, loss_ref, out_ref = refs[:len(SMALL)], refs[len(SMALL)], refs[len(SMALL) + 1]
        out_ref[...] = jnp.zeros_like(out_ref)
        for (_, r0, rows, cols), ref in zip(SMALL, val_refs):
            val = ref[...]
            if val.shape[0] != rows:
                val = jnp.sum(val, axis=0, keepdims=True)
            out_ref[r0:r0 + rows, :cols] = val[:, :cols]
        out_ref[LOSS_ROW:LOSS_ROW + 1, :HEAD_PAD] = jnp.full((1, HEAD_PAD), jnp.sum(loss_ref[...]), F32)

    return pl.pallas_call(
        body, name="pack_partials", out_shape=jax.ShapeDtypeStruct((SLAB_ROWS, D_MODEL), F32),
    )(*[small_g[n] for n, *_ in SMALL], loss_p)


def _adamw_small(parts, ws, ms, vs):
    n = len(SMALL)

    def body(p_ref, *refs):
        ins, loss_ref, outs = refs[:3 * n], refs[3 * n], refs[3 * n + 1:]
        total = p_ref[0]
        for d in range(1, N_DEV):
            total = total + p_ref[d]
        loss_ref[...] = total[LOSS_ROW:LOSS_ROW + 1, 0:1]
        for a, (_, r0, rows, cols) in enumerate(SMALL):
            g = total[r0:r0 + rows, :cols]
            outs[4 * a][...] = g
            outs[4 * a + 1][...], outs[4 * a + 2][...], outs[4 * a + 3][...] = _adamw_math(
                ins[a][...], g, ins[n + a][...], ins[2 * n + a][...])

    shapes = [jax.ShapeDtypeStruct((rows, cols), F32) for _, _, rows, cols in SMALL]
    res = pl.pallas_call(
        body, name="adamw_small", out_shape=[jax.ShapeDtypeStruct((1, 1), F32)] + [s for s in shapes for _ in range(4)],
    )(parts, *ws, *ms, *vs)
    return res[0], [res[1 + 4 * a:5 + 4 * a] for a in range(n)]


_WEIGHTS = ["mix_norm_g", "w_in", "q_a_norm_g", "w_uq", "kv_a_norm_g", "w_ukv", "q_norm_g", "k_norm_g", "hg_lb_logits",
            "hg_out_norm_g", "w_branch", "w_out", "ffn_norm_g", "w_ffn_gate", "w_ffn_up", "w_ffn_down",
            "ple_gate_norm_g", "w_ple_gate", "w_ple_proj", "ple_post_norm_g"]


def _step(x, p, positions, tgt, w, m, v):
    small_names = [n for n, *_ in SMALL]
    T = x.shape[1]
    px, py, pc = _my_place()
    as_idx = lambda t: jnp.reshape(t, (1,)).astype(jnp.int32)

    def two_d(n, t):
        t = t.reshape(-1, t.shape[-1])
        return t.T if n in TRANSPOSED else t

    def full_shape(n, t):
        return (t.T if n in TRANSPOSED else t).reshape(w[n].shape)

    blocks = {n: two_d(n, w[n]).astype(MM) for n in BIG}
    big = dict(zip(EARLY, _all_gather("ag_weights", [blocks[n] for n in EARLY])))
    small = {n: (w[n] if n == "hg_lb_logits" else w[n].reshape(1, -1)) for n in small_names}

    loss_p, grad_x, small_g, grads, sibs, gots, parts_c = _local_step(
        x[0], p[0, 0], positions.reshape(T, 1), tgt[0], small, big, late_blocks=blocks, core=as_idx(pc))

    gots.update(zip(GROUP_C, _exchange_chips(parts_c)))
    out_g, out_d, out_m, out_v = {}, {}, {}, {}
    for tag, names in (("ab", GROUP_A + GROUP_B), ("c", GROUP_C)):
        pick = lambda table: [table[n] for n in names]
        res = _sum_adamws("adamw_" + tag, pick(grads), pick(sibs), pick(gots), [two_d(n, w[n]) for n in names],
                          [two_d(n, m[n]) for n in names], [two_d(n, v[n]) for n in names],
                          as_idx(4 * px + 2 * py + pc), as_idx(2 * px + py))
        for n, r in zip(names, res):
            out_g[n], out_d[n], out_m[n], out_v[n] = [full_shape(n, t) for t in r]

    parts = _all_gather("ag_small", [_pack_partials(small_g, loss_p)])[0]
    loss, res = _adamw_small(parts, *([t[n] for n in small_names] for t in (w, m, v)))
    for n, r in zip(small_names, res):
        out_g[n], out_d[n], out_m[n], out_v[n] = r

    outs = [loss.reshape(()), grad_x[None]]
    for table in (out_g, out_d, out_m, out_v):
        outs += [table[n] for n in _WEIGHTS]
    return tuple(outs)


def kernel(x, p, positions, mix_norm_g, w_in, q_a_norm_g, w_uq, kv_a_norm_g, w_ukv, q_norm_g, k_norm_g, hg_lb_logits, hg_out_norm_g, w_branch, w_out, ffn_norm_g, w_ffn_gate, w_ffn_up, w_ffn_down, ple_gate_norm_g, w_ple_gate, w_ple_proj, ple_post_norm_g, loss_target, m_mix_norm_g, m_w_in, m_q_a_norm_g, m_w_uq, m_kv_a_norm_g, m_w_ukv, m_q_norm_g, m_k_norm_g, m_hg_lb_logits, m_hg_out_norm_g, m_w_branch, m_w_out, m_ffn_norm_g, m_w_ffn_gate, m_w_ffn_up, m_w_ffn_down, m_ple_gate_norm_g, m_w_ple_gate, m_w_ple_proj, m_ple_post_norm_g, v_mix_norm_g, v_w_in, v_q_a_norm_g, v_w_uq, v_kv_a_norm_g, v_w_ukv, v_q_norm_g, v_k_norm_g, v_hg_lb_logits, v_hg_out_norm_g, v_w_branch, v_w_out, v_ffn_norm_g, v_w_ffn_gate, v_w_ffn_up, v_w_ffn_down, v_ple_gate_norm_g, v_w_ple_gate, v_w_ple_proj, v_ple_post_norm_g):
    w = dict(mix_norm_g=mix_norm_g, w_in=w_in, q_a_norm_g=q_a_norm_g, w_uq=w_uq, kv_a_norm_g=kv_a_norm_g, w_ukv=w_ukv,
             q_norm_g=q_norm_g, k_norm_g=k_norm_g, hg_lb_logits=hg_lb_logits, hg_out_norm_g=hg_out_norm_g,
             w_branch=w_branch, w_out=w_out, ffn_norm_g=ffn_norm_g, w_ffn_gate=w_ffn_gate, w_ffn_up=w_ffn_up,
             w_ffn_down=w_ffn_down, ple_gate_norm_g=ple_gate_norm_g, w_ple_gate=w_ple_gate, w_ple_proj=w_ple_proj,
             ple_post_norm_g=ple_post_norm_g)
    m = dict(mix_norm_g=m_mix_norm_g, w_in=m_w_in, q_a_norm_g=m_q_a_norm_g, w_uq=m_w_uq, kv_a_norm_g=m_kv_a_norm_g,
             w_ukv=m_w_ukv, q_norm_g=m_q_norm_g, k_norm_g=m_k_norm_g, hg_lb_logits=m_hg_lb_logits,
             hg_out_norm_g=m_hg_out_norm_g, w_branch=m_w_branch, w_out=m_w_out, ffn_norm_g=m_ffn_norm_g,
             w_ffn_gate=m_w_ffn_gate, w_ffn_up=m_w_ffn_up, w_ffn_down=m_w_ffn_down,
             ple_gate_norm_g=m_ple_gate_norm_g, w_ple_gate=m_w_ple_gate, w_ple_proj=m_w_ple_proj,
             ple_post_norm_g=m_ple_post_norm_g)
    v = dict(mix_norm_g=v_mix_norm_g, w_in=v_w_in, q_a_norm_g=v_q_a_norm_g, w_uq=v_w_uq, kv_a_norm_g=v_kv_a_norm_g,
             w_ukv=v_w_ukv, q_norm_g=v_q_norm_g, k_norm_g=v_k_norm_g, hg_lb_logits=v_hg_lb_logits,
             hg_out_norm_g=v_hg_out_norm_g, w_branch=v_w_branch, w_out=v_w_out, ffn_norm_g=v_ffn_norm_g,
             w_ffn_gate=v_w_ffn_gate, w_ffn_up=v_w_ffn_up, w_ffn_down=v_w_ffn_down,
             ple_gate_norm_g=v_ple_gate_norm_g, w_ple_gate=v_w_ple_gate, w_ple_proj=v_w_ple_proj,
             ple_post_norm_g=v_ple_post_norm_g)
    return _step(x, p, positions, loss_target, w, m, v)
```

```python
import jax
import jax.numpy as jnp
import numpy as np
from jax import lax
from jax.experimental import pallas as pl
from jax.experimental.pallas import tpu as pltpu

F32 = jnp.float32
MM = jnp.bfloat16
MESH_ID = pl.DeviceIdType.MESH

D_MODEL = 1024
N_DEV = 8
MLA_HEADS = 8
QK_NOPE = 64
QK_ROPE = 32
QK_DIM = 96
V_DIM = 64
HEAD_PAD = 128
Q_RANK = 384
KV_RANK = 256
ROPE_BASE = 10000.0
HG_HEADS = 4
HG_DIM = 128
HG_W = 512
HG_CHUNK = 64
FFN = 2816
PLE = 256
EPS = 1e-6
ATT_SCALE = QK_DIM ** -0.5
NEG = -1e30

ADAM_LR = 0.001
ADAM_B1 = 0.9
ADAM_B2 = 0.999
ADAM_EPS = 1e-08
ADAM_WD = 0.01
ADAM_STEP = 10

COL_SECTIONS = ((0, 384), (384, 256), (640, 32), (672, 512), (1184, 512), (1696, 512), (2208, 512), (2720, 2048))
STORED_WIDTHS = tuple(HEAD_PAD if n == QK_ROPE else n for _, n in COL_SECTIONS)
IN_COLS = 4768
IN_BLOCK = IN_COLS // N_DEV

VMEM_LIMIT = 58 * 1024 * 1024
WIDE_TILE = 512
ROW_TILE = 256
DW_TOKENS = 1024
ATT_TILE = 1024
ATT_HEADS = 4
HG_BLOCK = 512
HG_UNROLL = 4


def _dot(a, b):
    return jnp.dot(a.astype(MM), b.astype(MM), preferred_element_type=F32)


def _dot_nt(a, b):
    return lax.dot_general(a.astype(MM), b.astype(MM), (((1,), (1,)), ((), ())), preferred_element_type=F32)


def _dot_tn(a, b):
    return lax.dot_general(a.astype(MM), b.astype(MM), (((0,), (0,)), ((), ())), preferred_element_type=F32)


def _sigmoid(x):
    return 1.0 / (1.0 + jnp.exp(-x))


def _rms(x, n=None):
    n = x.shape[-1] if n is None else n
    r = lax.rsqrt(jnp.sum(x * x, axis=-1, keepdims=True) * (1.0 / n) + EPS)
    return x * r, r


def _rms_bwd(dxh, xh, r, n=None):
    n = xh.shape[-1] if n is None else n
    return r * (dxh - xh * (jnp.sum(dxh * xh, axis=-1, keepdims=True) * (1.0 / n)))


def _rope_tables(pos, tm):
    lane = lax.broadcasted_iota(jnp.int32, (tm, HEAD_PAD), 1)
    idx = jnp.where(lane < QK_NOPE + QK_ROPE // 2, lane - QK_NOPE, lane - QK_NOPE - QK_ROPE // 2)
    inv = jnp.exp(idx.astype(F32) * (-np.log(ROPE_BASE) * 2.0 / QK_ROPE))
    ang = pos.astype(F32) * inv
    in_rope = (lane >= QK_NOPE) & (lane < QK_DIM)
    first = lane < QK_NOPE + QK_ROPE // 2
    cos_t = jnp.where(in_rope, jnp.cos(ang), 1.0)
    sin_t = jnp.where(in_rope, jnp.where(first, -jnp.sin(ang), jnp.sin(ang)), 0.0)
    return cos_t, sin_t, (first, in_rope)


def _rope_swap(x, halves):
    first, in_rope = halves
    half = QK_ROPE // 2
    return jnp.where(in_rope, jnp.where(first, pltpu.roll(x, HEAD_PAD - half, 1), pltpu.roll(x, half, 1)), 0.0)


def _cparams(sem, vmem=None):
    return pltpu.CompilerParams(dimension_semantics=sem, vmem_limit_bytes=vmem)


def _row_call(name, body, T, tm, row_ins, full_ins, row_outs, acc_outs, vmem=None, scratch=(), xchg=((), ())):
    n_in, n_out, n_x = len(row_ins) + len(full_ins), len(row_outs) + len(acc_outs), _x_count(xchg)
    steps = T // tm

    def kern(*refs):
        ins, x_in, refs = refs[:n_in], refs[n_in:n_in + n_x], refs[n_in + n_x:]
        outs, x_out, refs = refs[:n_out], refs[n_out:n_out + n_x], refs[n_out + n_x:]
        scr, x_sems = refs[:len(scratch)], refs[len(scratch):]
        i = pl.program_id(0)
        if n_x:
            @pl.when(i == 0)
            def _():
                for cp in _x_copies(len(xchg[0]), x_in, x_out, x_sems):
                    cp.start()

        body(i, *ins, *outs, *scr)
        if n_x:
            @pl.when(i == steps - 1)
            def _():
                for cp in _x_copies(len(xchg[0]), x_in, x_out, x_sems):
                    cp.wait()

    any_spec = pl.BlockSpec(memory_space=pl.ANY)
    in_specs = [pl.BlockSpec((tm, a.shape[1]), lambda i: (i, 0)) for a in row_ins]
    in_specs += [pl.BlockSpec(a.shape, lambda i, nd=a.ndim: (0,) * nd, pipeline_mode=pl.Buffered(1)) for a in full_ins]
    out_specs = [pl.BlockSpec((tm, n), lambda i: (i, 0)) for n, _ in row_outs]
    out_specs += [pl.BlockSpec(s, lambda i, nd=len(s): (0,) * nd) for s, _ in acc_outs]
    out_shape = [jax.ShapeDtypeStruct((T, n), dt) for n, dt in row_outs]
    out_shape += [jax.ShapeDtypeStruct(s, dt) for s, dt in acc_outs]
    return pl.pallas_call(
        kern, name=name, grid=(steps,), in_specs=in_specs + [any_spec] * n_x, out_specs=out_specs + [any_spec] * n_x,
        out_shape=out_shape + _x_out_shapes(xchg), scratch_shapes=list(scratch) + _x_sems(xchg),
        compiler_params=_cparams(("arbitrary",), vmem),
    )(*row_ins, *full_ins, *xchg[0], *xchg[1])


FFN_HALVES = (slice(0, FFN // 2), slice(FFN // 2, FFN))
ROW_CHUNK = 16
CHUNK_UNROLL = True


def _by_chunks(tm, fn):
    def step(c, carry):
        fn(pl.ds(pl.multiple_of(c * ROW_CHUNK, ROW_CHUNK), ROW_CHUNK))
        return carry

    lax.fori_loop(0, tm // ROW_CHUNK, step, 0, unroll=CHUNK_UNROLL)


def _fold8(x):
    return x[:8] + x[8:]


def _acc(ref, i, val):
    @pl.when(i == 0)
    def _():
        ref[...] = val

    @pl.when(i != 0)
    def _():
        ref[...] += val


def _in_proj_fwd(x, g_mix, w_in, T, tm):
    def body(i, x_ref, g_ref, w_ref, h_ref, *rest):
        outs, pj_s = rest[:-1], rest[-1]
        g = g_ref[...]

        def norm(rows):
            h_ref[rows, :] = (_rms(x_ref[rows, :])[0] * g).astype(MM)

        _by_chunks(tm, norm)
        for d in range(N_DEV):
            pj_s[d] = _dot_nt(h_ref[...], w_ref[d])

        def join_and_cut(rows):
            proj = jnp.concatenate([pj_s[d, rows, :] for d in range(N_DEV)], axis=1)
            for (s, n), o_ref in zip(COL_SECTIONS, outs):
                if n == QK_ROPE:
                    o_ref[rows, :] = jnp.concatenate(
                        [jnp.zeros((ROW_CHUNK, QK_NOPE), F32), proj[:, s:s + n],
                         jnp.zeros((ROW_CHUNK, HEAD_PAD - QK_DIM), F32)], axis=1)
                else:
                    o_ref[rows, :] = proj[:, s:s + n]

        _by_chunks(tm, join_and_cut)

    row_outs = [(D_MODEL, MM)] + [(n, F32) for n in STORED_WIDTHS]
    return _row_call("in_proj_fwd", body, T, tm, [x], [g_mix, w_in], row_outs, [], VMEM_LIMIT,
                     scratch=[pltpu.VMEM((N_DEV, tm, IN_BLOCK), F32)])


def _mla_heads_fwd(raw, g_pad, cos_t, sin_t, first):
    outs, saved = [], []
    for h in range(MLA_HEADS):
        xh, r = _rms(raw[:, h * HEAD_PAD:(h + 1) * HEAD_PAD], QK_DIM)
        y = xh * g_pad
        outs.append(y * cos_t + _rope_swap(y, first) * sin_t)
        saved.append((xh, r))
    return outs, saved


def _mla_raw_heads(cqn, ckvn, kr, wuq_ref, wukv_ref, tm):
    lane = lax.broadcasted_iota(jnp.int32, (tm, HEAD_PAD), 1)
    nope = lane < QK_NOPE
    one_lane = jnp.where(lane == V_DIM, 1.0, 0.0)
    qs, ks, vs = [], [], []
    for h in range(MLA_HEADS):
        qs.append(_dot_nt(cqn, wuq_ref[h]))
        kv = _dot(ckvn, wukv_ref[h])
        ks.append(jnp.where(nope, kv, kr))
        vs.append(jnp.where(nope, pltpu.roll(kv, V_DIM, 1), one_lane))
    return jnp.concatenate(qs, axis=1), jnp.concatenate(ks, axis=1), jnp.concatenate(vs, axis=1)


def _mla_prep_fwd(cq, ckv, kr, pos, g_qa, g_kva, g_qn, g_kn, w_uq, w_ukv, T, tm):
    def body(i, cq_ref, ckv_ref, kr_ref, pos_ref, gqa_ref, gkva_ref, gqn_ref, gkn_ref, wuq_ref, wukv_ref,
             q_ref, k_ref, v_ref):
        cos_t, sin_t, first = _rope_tables(pos_ref[...], tm)
        cqn = _rms(cq_ref[...])[0] * gqa_ref[...]
        ckvn = _rms(ckv_ref[...])[0] * gkva_ref[...]
        q_raw, k_raw, v = _mla_raw_heads(cqn, ckvn, kr_ref[...], wuq_ref, wukv_ref, tm)
        qs, _ = _mla_heads_fwd(q_raw, gqn_ref[...], cos_t, sin_t, first)
        ks, _ = _mla_heads_fwd(k_raw, gkn_ref[...], cos_t, sin_t, first)
        q_ref[...] = (jnp.concatenate(qs, axis=1) * ATT_SCALE).astype(MM)
        k_ref[...] = jnp.concatenate(ks, axis=1).astype(MM)
        v_ref[...] = v.astype(MM)

    w = MLA_HEADS * HEAD_PAD
    return _row_call("mla_prep_fwd", body, T, tm, [cq, ckv, kr, pos], [g_qa, g_kva, g_qn, g_kn, w_uq, w_ukv],
                     [(w, MM), (w, MM), (w, MM)], [])


def _causal_pairs(n, by_query):
    if by_query:
        pairs = [(q, k) for q in range(n) for k in range(q + 1)]
    else:
        pairs = [(q, k) for k in range(n) for q in range(k, n)]
    return np.array([p[0] for p in pairs], np.int32), np.array([p[1] for p in pairs], np.int32)


def _flash_fwd(qf, kf, vf, T, ag_blocks=()):
    tq = min(ATT_TILE, T)
    nq = T // tq

    qi_tab, ki_tab = _causal_pairs(nq, by_query=True)

    hp = ATT_HEADS

    n_ag = len(ag_blocks)
    n_heads, n_pairs = MLA_HEADS // hp, len(qi_tab)

    def body(qi_ref, ki_ref, q_ref, k_ref, v_ref, *rest):
        ag_in, (o_ref, lse_ref), rest = rest[:n_ag], rest[n_ag:n_ag + 2], rest[n_ag + 2:]
        ag_out, (m_s, acc_s), ag_sems = rest[:n_ag], rest[n_ag:n_ag + 2], rest[n_ag + 2:]
        t = pl.program_id(1)
        qi, ki = qi_ref[t], ki_ref[t]
        if n_ag:
            @pl.when((pl.program_id(0) == 0) & (t == 0))
            def _():
                _ag_start(ag_in, ag_out, ag_sems)

        @pl.when(ki == 0)
        def _():
            m_s[...] = jnp.full_like(m_s, NEG)
            acc_s[...] = jnp.zeros_like(acc_s)

        def step(masked):
            halves = 2 if masked and tq % (2 * HEAD_PAD) == 0 else 1
            w = tq // halves
            for hh in range(hp):
                hs = slice(hh * HEAD_PAD, (hh + 1) * HEAD_PAD)
                for part in range(halves):
                    cols, nk = slice(part * w, (part + 1) * w), (part + 1) * w
                    s_t = _dot_nt(k_ref[:nk, hs], q_ref[cols, hs])
                    if masked:
                        key = lax.broadcasted_iota(jnp.int32, (nk, w), 0)
                        qry = lax.broadcasted_iota(jnp.int32, (nk, w), 1) + part * w
                        s_t = jnp.where(key <= qry, s_t, NEG)
                    m_old = m_s[hh, :, cols]
                    m_new = jnp.maximum(m_old, jnp.max(s_t, axis=0, keepdims=True))
                    p_t = jnp.exp(s_t - m_new)
                    acc_s[hh, :, cols] = jnp.exp(m_old - m_new) * acc_s[hh, :, cols] + _dot_tn(v_ref[:nk, hs], p_t)
                    m_s[hh, :, cols] = m_new

        @pl.when(ki < qi)
        def _():
            step(False)

        @pl.when(ki == qi)
        def _():
            step(True)
            real = lax.broadcasted_iota(jnp.int32, (HEAD_PAD, tq), 0) < V_DIM
            for hh in range(hp):
                hs = slice(hh * HEAD_PAD, (hh + 1) * HEAD_PAD)
                acc = acc_s[hh]
                l = acc[V_DIM:V_DIM + 1]
                o_ref[:, hs] = jnp.where(real, acc / l, 0.0).T
                lse_ref[:, hs] = jnp.broadcast_to(m_s[hh] + jnp.log(l), (HEAD_PAD, tq)).T

        if n_ag:
            @pl.when((pl.program_id(0) == n_heads - 1) & (t == n_pairs - 1))
            def _():
                _ag_finish(ag_in, ag_out, ag_sems)

    q_spec = pl.BlockSpec((tq, hp * HEAD_PAD), lambda h, t, qi_ref, ki_ref: (qi_ref[t], h))
    kv_spec = pl.BlockSpec((tq, hp * HEAD_PAD), lambda h, t, qi_ref, ki_ref: (ki_ref[t], h))
    any_spec = pl.BlockSpec(memory_space=pl.ANY)
    grid_spec = pltpu.PrefetchScalarGridSpec(
        num_scalar_prefetch=2, grid=(n_heads, n_pairs),
        in_specs=[q_spec, kv_spec, kv_spec] + [any_spec] * n_ag, out_specs=[q_spec, q_spec] + [any_spec] * n_ag,
        scratch_shapes=[pltpu.VMEM((hp, 1, tq), F32), pltpu.VMEM((hp, HEAD_PAD, tq), F32)]
        + (_ag_sems(n_ag) if n_ag else []))
    return pl.pallas_call(
        body, name="flash_fwd", grid_spec=grid_spec,
        out_shape=[jax.ShapeDtypeStruct((T, MLA_HEADS * HEAD_PAD), F32)] * 2 + _ag_out_shapes(ag_blocks),
        compiler_params=_cparams(("arbitrary", "arbitrary")),
    )(jnp.asarray(qi_tab), jnp.asarray(ki_tab), qf, kf, vf, *ag_blocks)


def _hg_gates(hf, lb):
    sg = _sigmoid(hf)
    f = lb + (1.0 - lb) * sg
    return sg, f, jnp.log(f), 1.0 - f


def _prefix_sum(x, reverse=False):
    n = x.shape[0]
    row = lax.broadcasted_iota(jnp.int32, x.shape, 0)
    step = 1
    while step < n:
        if reverse:
            x = x + jnp.where(row < n - step, pltpu.roll(x, n - step, 0), 0.0)
        else:
            x = x + jnp.where(row >= step, pltpu.roll(x, step, 0), 0.0)
        step *= 2
    return x


def _hg_levels():
    C = HG_CHUNK
    t = lax.broadcasted_iota(jnp.int32, (C, C), 0)
    s = lax.broadcasted_iota(jnp.int32, (C, C), 1)
    levels = []
    for shift in range(C.bit_length() - 2, -1, -1):
        pair_t, pair_s = lax.shift_right_logical(t, shift + 1), lax.shift_right_logical(s, shift + 1)
        later_t = (lax.shift_right_logical(t, shift) & 1) == 1
        earlier_s = (lax.shift_right_logical(s, shift) & 1) == 0
        levels.append((1 << shift, (pair_t == pair_s) & later_t & earlier_s))
    return levels, t == s


def _hg_refs(b):
    C, n = b.shape
    row = lax.broadcasted_iota(jnp.int32, (C, n), 0)
    back1, back2, ahead1 = pltpu.roll(b, 1, 0), pltpu.roll(b, 2, 0), pltpu.roll(b, C - 1, 0)
    refs = []
    for half in (32, 16, 8, 4):
        refs.append(jnp.concatenate(
            [jnp.broadcast_to(b[lo + half - 1:lo + half], (2 * half, n)) for lo in range(0, C, 2 * half)], axis=0))
    in4 = row & 3
    refs.append(jnp.where(in4 == 0, ahead1, jnp.where(in4 == 1, b, jnp.where(in4 == 2, back1, back2))))
    refs.append(jnp.where((row & 1) == 1, back1, b))
    return refs


def _hg_intra(q, k, b, refs, levels, eye):
    a = jnp.where(eye, jnp.sum(q * k, axis=1, keepdims=True), 0.0)
    saved = []
    for r, (_, mask) in zip(refs, levels):
        e = jnp.exp(-jnp.abs(b - r))
        q_t, k_t = q * e, k * e
        a = a + jnp.where(mask, _dot_nt(q_t, k_t), 0.0)
        saved.append((q_t, k_t, e))
    return a, saved


def _hg_intra_bwd(d_a, q, k, saved, levels, eye):
    diag = jnp.sum(jnp.where(eye, d_a, 0.0), axis=1, keepdims=True)
    dq, dk = diag * k, diag * q
    for (q_t, k_t, e), (_, mask) in zip(saved, levels):
        da = jnp.where(mask, d_a, 0.0)
        dq = dq + _dot(da, k_t) * e
        dk = dk + _dot_tn(da, q_t) * e
    return dq, dk


def _hgrn_fwd(hq, hf, hi, lb, T):
    rb = min(HG_BLOCK, T)
    ncb = rb // HG_CHUNK

    def body(hq_ref, hf_ref, hi_ref, lb_ref, o_ref, s0_ref, st_ref):
        @pl.when(pl.program_id(0) == 0)
        def _():
            st_ref[...] = jnp.zeros_like(st_ref)

        levels, eye = _hg_levels()

        def chunk(c, carry):
            rows = pl.ds(pl.multiple_of(c * HG_CHUNK, HG_CHUNK), HG_CHUNK)
            _, _, logf, kk = _hg_gates(hf_ref[rows, :], lb_ref[...])
            b = _prefix_sum(logf)
            refs = _hg_refs(b)
            q_all, v_all = hq_ref[rows, :], hi_ref[rows, :]
            outs = []
            for h in range(HG_HEADS):
                ls = slice(h * HG_DIM, (h + 1) * HG_DIM)
                q, k, v, bh = q_all[:, ls], kk[:, ls], v_all[:, ls], b[:, ls]
                st = st_ref[h]
                s0_ref[c, h * HG_DIM:(h + 1) * HG_DIM, :] = st
                b_end = bh[HG_CHUNK - 1:HG_CHUNK]
                a, _ = _hg_intra(q, k, bh, [r[:, ls] for r in refs], levels, eye)
                outs.append(_dot_nt(q * jnp.exp(bh), st) + _dot(a, v))
                st_ref[h] = st * jnp.exp(b_end) + _dot_tn(v, k * jnp.exp(b_end - bh))
            o_ref[rows, :] = jnp.concatenate(outs, axis=1)
            return carry

        lax.fori_loop(0, ncb, chunk, 0, unroll=HG_UNROLL)

    row = pl.BlockSpec((rb, HG_W), lambda i: (i, 0))
    return pl.pallas_call(
        body, name="hgrn_fwd", grid=(T // rb,),
        in_specs=[row, row, row, pl.BlockSpec((1, HG_W), lambda i: (0, 0))],
        out_specs=[row, pl.BlockSpec((ncb, HG_W, HG_DIM), lambda i: (i, 0, 0))],
        out_shape=[jax.ShapeDtypeStruct((T, HG_W), F32), jax.ShapeDtypeStruct((T // HG_CHUNK, HG_W, HG_DIM), F32)],
        scratch_shapes=[pltpu.VMEM((HG_HEADS, HG_DIM, HG_DIM), F32)],
        compiler_params=_cparams(("arbitrary",)),
    )(hq, hf, hi, lb)


def _hgrn_bwd(hq, hf, hi, do, s0, lb, T, xchg=((), ())):
    rb = min(HG_BLOCK, T)
    ncb = rb // HG_CHUNK
    nb = T // rb
    C = HG_CHUNK
    n_x, n_sib = _x_count(xchg), len(xchg[0])

    def body(hq_ref, hf_ref, hi_ref, do_ref, s0_ref, lb_ref, *rest):
        x_in, (dq_ref, df_ref, dv_ref, dlb_ref), rest = rest[:n_x], rest[n_x:n_x + 4], rest[n_x + 4:]
        x_out, dst_ref, x_sems = rest[:n_x], rest[n_x], rest[n_x + 1:]

        @pl.when(pl.program_id(0) == 0)
        def _():
            dst_ref[...] = jnp.zeros_like(dst_ref)
            dlb_ref[...] = jnp.zeros_like(dlb_ref)
            for cp in _x_copies(n_sib, x_in, x_out, x_sems):
                cp.start()

        row_cc = lax.broadcasted_iota(jnp.int32, (C, C), 0)
        col_cc = lax.broadcasted_iota(jnp.int32, (C, C), 1)
        last_row = lax.broadcasted_iota(jnp.int32, (C, HG_DIM), 0) == C - 1
        lb_v = lb_ref[...]
        levels, eye = _hg_levels()

        def chunk(cc, carry):
            c = ncb - 1 - cc
            rows = pl.ds(pl.multiple_of(c * C, C), C)
            hf_c = hf_ref[rows, :]
            sg, f, logf, kk = _hg_gates(hf_c, lb_v)
            b = _prefix_sum(logf)
            refs = _hg_refs(b)
            q_all, v_all, do_all = hq_ref[rows, :], hi_ref[rows, :], do_ref[rows, :]
            dq_o, dk_o, dv_o, db_o = [], [], [], []
            for h in range(HG_HEADS):
                ls = slice(h * HG_DIM, (h + 1) * HG_DIM)
                q, k, v, bh, d_o = q_all[:, ls], kk[:, ls], v_all[:, ls], b[:, ls], do_all[:, ls]
                st0 = s0_ref[c, h * HG_DIM:(h + 1) * HG_DIM, :]
                dst = dst_ref[h]
                b_end = bh[C - 1:C]
                e_b, e_end = jnp.exp(bh), jnp.exp(b_end)
                e_rem = jnp.exp(b_end - bh)
                qe, kd = q * e_b, k * e_rem
                st_end = st0 * e_end + _dot_tn(v, kd)
                a, saved = _hg_intra(q, k, bh, [r[:, ls] for r in refs], levels, eye)
                d_a = jnp.where(col_cc <= row_cc, _dot_nt(d_o, v), 0.0)
                dq_i, dk_i = _hg_intra_bwd(d_a, q, k, saved, levels, eye)
                dv = _dot_tn(a, d_o) + _dot_nt(kd, dst)
                dq = e_b * _dot(d_o, st0) + dq_i
                dk = e_rem * _dot(v, dst) + dk_i
                extra = jnp.sum(dst * st_end, axis=0, keepdims=True)
                db_o.append(q * dq - k * dk + jnp.where(last_row, extra, 0.0))
                dst_ref[h] = dst * e_end + _dot_tn(d_o, qe)
                dq_o.append(dq)
                dk_o.append(dk)
                dv_o.append(dv)
            dlogf = _prefix_sum(jnp.concatenate(db_o, axis=1), reverse=True)
            d_f = dlogf / f - jnp.concatenate(dk_o, axis=1)
            dq_ref[rows, :] = jnp.concatenate(dq_o, axis=1).astype(MM)
            dv_ref[rows, :] = jnp.concatenate(dv_o, axis=1).astype(MM)
            df_ref[rows, :] = (d_f * (1.0 - lb_v) * sg * (1.0 - sg)).astype(MM)
            dlb_ref[...] += jnp.sum(d_f * (1.0 - sg), axis=0, keepdims=True)
            return carry

        lax.fori_loop(0, ncb, chunk, 0, unroll=HG_UNROLL)

        if n_x:
            @pl.when(pl.program_id(0) == nb - 1)
            def _():
                for cp in _x_copies(n_sib, x_in, x_out, x_sems):
                    cp.wait()

    row = pl.BlockSpec((rb, HG_W), lambda i: (nb - 1 - i, 0))
    one = pl.BlockSpec((1, HG_W), lambda i: (0, 0))
    any_spec = pl.BlockSpec(memory_space=pl.ANY)
    return pl.pallas_call(
        body, name="hgrn_bwd", grid=(nb,),
        in_specs=[row, row, row, row, pl.BlockSpec((ncb, HG_W, HG_DIM), lambda i: (nb - 1 - i, 0, 0)), one]
        + [any_spec] * n_x,
        out_specs=[row, row, row, one] + [any_spec] * n_x,
        out_shape=[jax.ShapeDtypeStruct((T, HG_W), MM)] * 3 + [jax.ShapeDtypeStruct((1, HG_W), F32)]
        + _x_out_shapes(xchg),
        scratch_shapes=[pltpu.VMEM((HG_HEADS, HG_DIM, HG_DIM), F32)] + _x_sems(xchg),
        compiler_params=_cparams(("arbitrary",)),
    )(hq, hf, hi, do, s0, lb, *xchg[0], *xchg[1])


def _silu_parts(x):
    sg = _sigmoid(x)
    return x * sg, sg * (1.0 + x * (1.0 - sg))


def _merge_fwd(attn, o, hg, bg, x, g_out, w_bra, w_brb, w_out, T, tm):
    def body(i, attn_ref, o_ref, hg_ref, bg_ref, x_ref, g_ref, wa_ref, wb_ref, wo_ref,
             x1_ref, ya_ref, yb_ref, m_ref, rec_ref):
        g = g_ref[...]

        def recurrent_out(rows):
            for h in range(HG_HEADS):
                ls = slice(h * HG_DIM, (h + 1) * HG_DIM)
                rec_ref[rows, ls] = (_rms(o_ref[rows, ls])[0] * g * _silu_parts(hg_ref[rows, ls])[0]).astype(MM)

        _by_chunks(tm, recurrent_out)
        ya_ref[...] = _dot(attn_ref[...], wa_ref[...])
        yb_ref[...] = jnp.dot(rec_ref[...], wb_ref[...], preferred_element_type=F32)

        def gate(rows):
            m_ref[rows, :] = (_sigmoid(bg_ref[rows, :D_MODEL]) * ya_ref[rows, :]
                              + _sigmoid(bg_ref[rows, D_MODEL:]) * yb_ref[rows, :]).astype(MM)

        _by_chunks(tm, gate)
        x1_ref[...] = x_ref[...] + jnp.dot(m_ref[...], wo_ref[...], preferred_element_type=F32)

    return _row_call("merge_fwd", body, T, tm, [attn, o, hg, bg, x], [g_out, w_bra, w_brb, w_out],
                     [(D_MODEL, F32), (D_MODEL, F32), (D_MODEL, F32), (D_MODEL, MM), (HG_W, MM)], [], VMEM_LIMIT)


def _ffn_fwd(x1, g_ffn, w_g, w_u, w_d, T, tm):
    def body(i, x1_ref, g_ref, wg_ref, wu_ref, wd_ref, x2_ref, gt_ref, up_ref, h2_ref, a_s):
        g = g_ref[...]

        def norm(rows):
            h2_ref[rows, :] = (_rms(x1_ref[rows, :])[0] * g).astype(MM)

        _by_chunks(tm, norm)
        gt_ref[...] = _dot_nt(h2_ref[...], wg_ref[...])
        up_ref[...] = _dot_nt(h2_ref[...], wu_ref[...])

        def act(rows):
            for cs in FFN_HALVES:
                a_s[rows, cs] = (_silu_parts(gt_ref[rows, cs])[0] * up_ref[rows, cs]).astype(MM)

        _by_chunks(tm, act)
        x2_ref[...] = x1_ref[...] + jnp.dot(a_s[...], wd_ref[...], preferred_element_type=F32)

    return _row_call("ffn_fwd", body, T, tm, [x1], [g_ffn, w_g, w_u, w_d],
                     [(D_MODEL, F32), (FFN, F32), (FFN, F32), (D_MODEL, MM)], [], VMEM_LIMIT,
                     scratch=[pltpu.VMEM((tm, FFN), MM)])


def _ple_loss(x2, p, tgt, g_pg, g_post, w_pg, w_pp, T, tm):
    def body(i, x2_ref, p_ref, t_ref, gpg_ref, gpo_ref, wpg_ref, wpp_ref,
             dx2_ref, loss_ref, dgpo_ref, dgpg_ref, dwpg_ref, dwpp_ref, u_s, n3_s, z_s, dz_s, du_s, dy_s, dn3_s):
        @pl.when(i == 0)
        def _():
            for ref in (loss_ref, dgpo_ref, dgpg_ref, dwpg_ref, dwpp_ref):
                ref[...] = jnp.zeros_like(ref)

        gpg, gpo = gpg_ref[...], gpo_ref[...]
        p_mm = p_ref[...].astype(MM)
        for d in range(N_DEV):
            u_s[:, d * HEAD_PAD:(d + 1) * HEAD_PAD] = jnp.dot(p_mm, wpp_ref[d], preferred_element_type=F32)

        def gate_input(rows):
            n3_s[rows, :] = (_rms(x2_ref[rows, :])[0] * gpg).astype(MM)

        _by_chunks(tm, gate_input)
        z_s[...] = jnp.dot(n3_s[...], wpg_ref[...], preferred_element_type=F32)

        def loss_and_back(rows):
            uh, ru = _rms(u_s[rows, :])
            e = uh * gpo
            gate = _sigmoid(z_s[rows, :])
            diff = x2_ref[rows, :] + gate * e - t_ref[rows, :]
            dy = diff * (1.0 / D_MODEL)
            de = dy * gate
            dz_s[rows, :] = (dy * e * gate * (1.0 - gate)).astype(MM)
            du_s[rows, :] = _rms_bwd(de * gpo, uh, ru).astype(MM)
            dy_s[rows, :] = dy
            loss_ref[...] += _fold8(diff * diff) * (0.5 / D_MODEL)
            dgpo_ref[...] += _fold8(de * uh)

        _by_chunks(tm, loss_and_back)
        dn3_s[...] = _dot_nt(dz_s[...], wpg_ref[...])

        def gate_norm_back(rows):
            x2h, r3 = _rms(x2_ref[rows, :])
            dn3 = dn3_s[rows, :]
            dx2_ref[rows, :] = dy_s[rows, :] + _rms_bwd(dn3 * gpg, x2h, r3)
            dgpg_ref[...] += _fold8(dn3 * x2h)

        _by_chunks(tm, gate_norm_back)
        dwpg_ref[...] += _dot_tn(n3_s[...], dz_s[...])
        for d in range(N_DEV):
            dwpp_ref[d] += _dot_tn(p_mm, du_s[:, d * HEAD_PAD:(d + 1) * HEAD_PAD])

    vec = ((8, D_MODEL), F32)
    wide = lambda dt: pltpu.VMEM((tm, D_MODEL), dt)
    return _row_call("ple_loss", body, T, tm, [x2, p, tgt], [g_pg, g_post, w_pg, w_pp], [(D_MODEL, F32)],
                     [vec, vec, vec, ((D_MODEL, D_MODEL), F32), ((N_DEV, PLE, HEAD_PAD), F32)], VMEM_LIMIT,
                     scratch=[wide(F32), wide(MM), wide(F32), wide(MM), wide(MM), wide(F32), wide(F32)])


def _ffn_bwd(dx2, x1, gt, up, g_ffn, w_g, w_u, w_d, T, tm):
    def body(i, dx2_ref, x1_ref, gt_ref, up_ref, g_ref, wg_ref, wu_ref, wd_ref,
             dx1_ref, a_ref, dgt_ref, dup_ref, dg_ref, da_s, dh2_s):
        @pl.when(i == 0)
        def _():
            dg_ref[...] = jnp.zeros_like(dg_ref)

        g = g_ref[...]
        da_s[...] = _dot_nt(dx2_ref[...], wd_ref[...])

        def act_back(rows):
            for cs in FFN_HALVES:
                up, da = up_ref[rows, cs], da_s[rows, cs]
                silu, dsilu = _silu_parts(gt_ref[rows, cs])
                dgt_ref[rows, cs] = (da * up * dsilu).astype(MM)
                dup_ref[rows, cs] = (da * silu).astype(MM)
                a_ref[rows, cs] = (silu * up).astype(MM)

        _by_chunks(tm, act_back)
        dh2_s[...] = (jnp.dot(dgt_ref[...], wg_ref[...], preferred_element_type=F32)
                      + jnp.dot(dup_ref[...], wu_ref[...], preferred_element_type=F32))

        def norm_back(rows):
            x1h, r = _rms(x1_ref[rows, :])
            dh2 = dh2_s[rows, :]
            dx1_ref[rows, :] = dx2_ref[rows, :] + _rms_bwd(dh2 * g, x1h, r)
            dg_ref[...] += _fold8(dh2 * x1h)

        _by_chunks(tm, norm_back)

    return _row_call("ffn_bwd", body, T, tm, [dx2, x1, gt, up], [g_ffn, w_g, w_u, w_d],
                     [(D_MODEL, F32), (FFN, MM), (FFN, MM), (FFN, MM)], [((8, D_MODEL), F32)], VMEM_LIMIT,
                     scratch=[pltpu.VMEM((tm, FFN), F32), pltpu.VMEM((tm, D_MODEL), F32)])


def _merge_bwd(dx1, ya, yb, bg, o, hg, attn, m, rec, g_out, w_bra, w_brb, w_out, T, tm, xchg=((), ())):
    def body(i, dx1_ref, ya_ref, yb_ref, bg_ref, o_ref, hg_ref, attn_ref, m_ref, rec_ref, g_ref, wa_ref, wb_ref, wo_ref,
             dattn_ref, do_ref, dhg_ref, dbg_ref, dg_ref, dwo_ref, dwa_ref, dwb_ref, dm_s, dya_s, dyb_s, drec_s):
        @pl.when(i == 0)
        def _():
            for ref in (dg_ref, dwo_ref, dwa_ref, dwb_ref):
                ref[...] = jnp.zeros_like(ref)

        g = g_ref[...]
        dx1 = dx1_ref[...].astype(MM)
        dm_s[...] = _dot_nt(dx1, wo_ref[...])

        def gate_back(rows):
            dm = dm_s[rows, :]
            ga, gb = _sigmoid(bg_ref[rows, :D_MODEL]), _sigmoid(bg_ref[rows, D_MODEL:])
            dya_s[rows, :] = (dm * ga).astype(MM)
            dyb_s[rows, :] = (dm * gb).astype(MM)
            dbg_ref[rows, :D_MODEL] = (dm * ya_ref[rows, :] * ga * (1.0 - ga)).astype(MM)
            dbg_ref[rows, D_MODEL:] = (dm * yb_ref[rows, :] * gb * (1.0 - gb)).astype(MM)

        _by_chunks(tm, gate_back)
        dwo_ref[...] += _dot_tn(m_ref[...], dx1)
        attn_mm = attn_ref[...].astype(MM)
        for d in range(N_DEV):
            ds = slice(d * HEAD_PAD, (d + 1) * HEAD_PAD)
            dwa_ref[d] += _dot_tn(attn_mm, dya_s[:, ds])
            dwb_ref[d] += _dot_tn(rec_ref[...], dyb_s[:, ds])
        dattn_ref[...] = _dot_nt(dya_s[...], wa_ref[...])
        drec_s[...] = _dot_nt(dyb_s[...], wb_ref[...])

        def recurrent_out_back(rows):
            for h in range(HG_HEADS):
                ls = slice(h * HG_DIM, (h + 1) * HG_DIM)
                oh, r = _rms(o_ref[rows, ls])
                silu, dsilu = _silu_parts(hg_ref[rows, ls])
                dr = drec_s[rows, ls]
                dhg_ref[rows, ls] = (dr * oh * g * dsilu).astype(MM)
                don = dr * silu
                dg_ref[...] += _fold8(don * oh)
                do_ref[rows, ls] = _rms_bwd(don * g, oh, r)

        _by_chunks(tm, recurrent_out_back)

    wide = lambda n, dt: pltpu.VMEM((tm, n), dt)
    return _row_call("merge_bwd", body, T, tm, [dx1, ya, yb, bg, o, hg, attn, m, rec], [g_out, w_bra, w_brb, w_out],
                     [(D_MODEL, F32), (HG_W, F32), (HG_W, MM), (2 * D_MODEL, MM)],
                     [((8, HG_DIM), F32), ((D_MODEL, D_MODEL), F32), ((N_DEV, MLA_HEADS * HEAD_PAD, HEAD_PAD), F32),
                      ((N_DEV, HG_W, HEAD_PAD), F32)], VMEM_LIMIT,
                     scratch=[wide(D_MODEL, F32), wide(D_MODEL, MM), wide(D_MODEL, MM), wide(HG_W, F32)], xchg=xchg)


def _flash_bwd(qf, kf, vf, o, do, lse, T, xchg=((), ())):
    tq = min(ATT_TILE, T)
    nq = T // tq

    qi_tab, ki_tab = _causal_pairs(nq, by_query=False)

    n_x, n_sib = _x_count(xchg), len(xchg[0])
    hp = ATT_HEADS
    n_heads, n_pairs = MLA_HEADS // hp, len(qi_tab)

    def body(qi_ref, ki_ref, q_ref, k_ref, v_ref, o_ref, do_ref, lse_ref, *rest):
        x_in, (dq_ref, dk_ref, dv_ref), rest = rest[:n_x], rest[n_x:n_x + 3], rest[n_x + 3:]
        x_out, x_sems = rest[:n_x], rest[n_x:]
        t = pl.program_id(1)
        qi, ki = qi_ref[t], ki_ref[t]
        if n_x:
            @pl.when((pl.program_id(0) == 0) & (t == 0))
            def _():
                for cp in _x_copies(n_sib, x_in, x_out, x_sems):
                    cp.start()

        @pl.when(t == 0)
        def _():
            dq_ref[...] = jnp.zeros_like(dq_ref)

        def step(first):
            halves = 2 if first and tq % (2 * HEAD_PAD) == 0 else 1
            w = tq // halves
            for hh in range(hp):
                hs = slice(hh * HEAD_PAD, (hh + 1) * HEAD_PAD)
                for part in range(halves):
                    keys, qs = slice(part * w, (part + 1) * w), slice(part * w, tq)
                    nq_ = tq - part * w
                    q, k, d_o = q_ref[qs, hs], k_ref[keys, hs], do_ref[qs, hs]
                    s = _dot_nt(q, k)
                    if first:
                        row = lax.broadcasted_iota(jnp.int32, (nq_, w), 0)
                        col = lax.broadcasted_iota(jnp.int32, (nq_, w), 1)
                        s = jnp.where(col <= row, s, NEG)
                    p = jnp.exp(s - lse_ref[qs, hh * HEAD_PAD:hh * HEAD_PAD + 1])
                    delta = jnp.sum(d_o * o_ref[qs, hs], axis=1, keepdims=True)
                    ds = p * (_dot_nt(d_o, v_ref[keys, hs]) - delta)
                    rows = pl.ds(pl.multiple_of(qi * tq + part * w, w), nq_)
                    dq_ref[rows, hs] += _dot(ds, k)
                    if first:
                        dv_ref[keys, hs] = _dot_tn(p, d_o)
                        dk_ref[keys, hs] = _dot_tn(ds, q)
                    else:
                        dv_ref[keys, hs] += _dot_tn(p, d_o)
                        dk_ref[keys, hs] += _dot_tn(ds, q)

        @pl.when(qi == ki)
        def _():
            step(True)

        @pl.when(qi > ki)
        def _():
            step(False)

        if n_x:
            @pl.when((pl.program_id(0) == n_heads - 1) & (t == n_pairs - 1))
            def _():
                for cp in _x_copies(n_sib, x_in, x_out, x_sems):
                    cp.wait()

    q_spec = pl.BlockSpec((tq, hp * HEAD_PAD), lambda h, t, qi_ref, ki_ref: (qi_ref[t], h))
    kv_spec = pl.BlockSpec((tq, hp * HEAD_PAD), lambda h, t, qi_ref, ki_ref: (ki_ref[t], h))
    any_spec = pl.BlockSpec(memory_space=pl.ANY)
    w = MLA_HEADS * HEAD_PAD
    grid_spec = pltpu.PrefetchScalarGridSpec(
        num_scalar_prefetch=2, grid=(n_heads, n_pairs),
        in_specs=[q_spec, kv_spec, kv_spec, q_spec, q_spec, q_spec] + [any_spec] * n_x,
        out_specs=[pl.BlockSpec((T, hp * HEAD_PAD), lambda h, t, qi_ref, ki_ref: (0, h)), kv_spec, kv_spec]
        + [any_spec] * n_x,
        scratch_shapes=_x_sems(xchg))
    return pl.pallas_call(
        body, name="flash_bwd", grid_spec=grid_spec,
        out_shape=[jax.ShapeDtypeStruct((T, w), F32)] * 3 + _x_out_shapes(xchg),
        compiler_params=_cparams(("arbitrary", "arbitrary")),
    )(jnp.asarray(qi_tab), jnp.asarray(ki_tab), qf, kf, vf, o, do, lse, *xchg[0], *xchg[1])


def _mla_heads_bwd(d_out, saved, g_pad, cos_t, sin_t, first):
    d_raw, dg = [], jnp.zeros((1, HEAD_PAD), F32)
    for h in range(MLA_HEADS):
        xh, r = saved[h]
        dy = d_out[:, h * HEAD_PAD:(h + 1) * HEAD_PAD]
        dn = dy * cos_t + _rope_swap(dy * sin_t, first)
        dg = dg + jnp.sum(dn * xh, axis=0, keepdims=True)
        d_raw.append(_rms_bwd(dn * g_pad, xh, r, QK_DIM))
    return d_raw, dg


def _mla_prep_bwd(cq, ckv, kr, pos, dqf, dkf, dvf, g_qa, g_kva, g_qn, g_kn, w_uq, w_ukv, T, tm):
    def body(i, cq_ref, ckv_ref, kr_ref, pos_ref, dq_ref, dk_ref, dv_ref,
             gqa_ref, gkva_ref, gqn_ref, gkn_ref, wuq_ref, wukv_ref,
             dcq_ref, dckv_ref, dkr_ref, dgqa_ref, dgkva_ref, dgqn_ref, dgkn_ref, dwuq_ref, dwukv_ref):
        cos_t, sin_t, first = _rope_tables(pos_ref[...], tm)
        cqh, rq = _rms(cq_ref[...])
        ckvh, rkv = _rms(ckv_ref[...])
        cqn, ckvn = cqh * gqa_ref[...], ckvh * gkva_ref[...]
        q_raw, k_raw, _ = _mla_raw_heads(cqn, ckvn, kr_ref[...], wuq_ref, wukv_ref, tm)
        _, q_saved = _mla_heads_fwd(q_raw, gqn_ref[...], cos_t, sin_t, first)
        _, k_saved = _mla_heads_fwd(k_raw, gkn_ref[...], cos_t, sin_t, first)
        dq_heads, dgqn = _mla_heads_bwd(dq_ref[...] * ATT_SCALE, q_saved, gqn_ref[...], cos_t, sin_t, first)
        dk_heads, dgkn = _mla_heads_bwd(dk_ref[...], k_saved, gkn_ref[...], cos_t, sin_t, first)
        lane = lax.broadcasted_iota(jnp.int32, (tm, HEAD_PAD), 1)
        nope = lane < QK_NOPE
        dcqn = jnp.zeros((tm, Q_RANK), F32)
        dckvn = jnp.zeros((tm, KV_RANK), F32)
        dkr = jnp.zeros((tm, HEAD_PAD), F32)
        cqn_mm, ckvn_mm = cqn.astype(MM), ckvn.astype(MM)
        for h in range(MLA_HEADS):
            hs = slice(h * HEAD_PAD, (h + 1) * HEAD_PAD)
            dq_h = dq_heads[h].astype(MM)
            dkv_h = jnp.where(nope, dk_heads[h], pltpu.roll(dv_ref[:, hs], V_DIM, 1)).astype(MM)
            _acc(dwuq_ref.at[h], i, _dot_tn(dq_h, cqn_mm))
            _acc(dwukv_ref.at[h], i, _dot_tn(ckvn_mm, dkv_h))
            dcqn = dcqn + jnp.dot(dq_h, wuq_ref[h], preferred_element_type=F32)
            dckvn = dckvn + lax.dot_general(dkv_h, wukv_ref[h], (((1,), (1,)), ((), ())), preferred_element_type=F32)
            dkr = dkr + dk_heads[h]
        dkr_ref[...] = jnp.where((lane >= QK_NOPE) & (lane < QK_DIM), dkr, 0.0).astype(MM)
        dcq_ref[...] = _rms_bwd(dcqn * gqa_ref[...], cqh, rq).astype(MM)
        dckv_ref[...] = _rms_bwd(dckvn * gkva_ref[...], ckvh, rkv).astype(MM)
        _acc(dgqa_ref, i, jnp.sum(dcqn * cqh, axis=0, keepdims=True))
        _acc(dgkva_ref, i, jnp.sum(dckvn * ckvh, axis=0, keepdims=True))
        _acc(dgqn_ref, i, dgqn)
        _acc(dgkn_ref, i, dgkn)

    return _row_call(
        "mla_prep_bwd", body, T, tm, [cq, ckv, kr, pos, dqf, dkf, dvf], [g_qa, g_kva, g_qn, g_kn, w_uq, w_ukv],
        [(Q_RANK, MM), (KV_RANK, MM), (HEAD_PAD, MM)],
        [((1, Q_RANK), F32), ((1, KV_RANK), F32), ((1, HEAD_PAD), F32), ((1, HEAD_PAD), F32),
         ((MLA_HEADS, HEAD_PAD, Q_RANK), F32), ((MLA_HEADS, KV_RANK, HEAD_PAD), F32)], VMEM_LIMIT)


def _in_proj_bwd(x, dx1, dsecs, g_mix, w_in, T, tm):
    def body(i, x_ref, dx1_ref, *rest):
        d_refs, (g_ref, w_ref, dx_ref, dp_ref, dg_ref, dh_s) = rest[:len(COL_SECTIONS)], rest[len(COL_SECTIONS):]

        @pl.when(i == 0)
        def _():
            dg_ref[...] = jnp.zeros_like(dg_ref)

        g = g_ref[...]

        def join_and_cut(rows):
            pieces = [(d_ref[rows, QK_NOPE:QK_DIM] if n == QK_ROPE else d_ref[rows, :]).astype(F32)
                      for (_, n), d_ref in zip(COL_SECTIONS, d_refs)]
            dproj = jnp.concatenate(pieces, axis=1)
            for d in range(N_DEV):
                dp_ref[d, rows, :] = dproj[:, d * IN_BLOCK:(d + 1) * IN_BLOCK].astype(MM)

        _by_chunks(tm, join_and_cut)
        dh = jnp.dot(dp_ref[0], w_ref[0], preferred_element_type=F32)
        for d in range(1, N_DEV):
            dh = dh + jnp.dot(dp_ref[d], w_ref[d], preferred_element_type=F32)
        dh_s[...] = dh

        def norm_back(rows):
            xh, r = _rms(x_ref[rows, :])
            dh_c = dh_s[rows, :]
            dx_ref[rows, :] = dx1_ref[rows, :] + _rms_bwd(dh_c * g, xh, r)
            dg_ref[...] += _fold8(dh_c * xh)

        _by_chunks(tm, norm_back)

    in_specs = [pl.BlockSpec((tm, a.shape[1]), lambda i: (i, 0)) for a in [x, dx1, *dsecs]]
    in_specs += [pl.BlockSpec(g_mix.shape, lambda i: (0, 0)),
                 pl.BlockSpec(w_in.shape, lambda i: (0, 0, 0), pipeline_mode=pl.Buffered(1))]

    def kern(*refs):
        body(pl.program_id(0), *refs)

    return pl.pallas_call(
        kern, name="in_proj_bwd", grid=(T // tm,), in_specs=in_specs,
        out_specs=[pl.BlockSpec((tm, D_MODEL), lambda i: (i, 0)),
                   pl.BlockSpec((N_DEV, tm, IN_BLOCK), lambda i: (0, i, 0)),
                   pl.BlockSpec((8, D_MODEL), lambda i: (0, 0))],
        out_shape=[jax.ShapeDtypeStruct((T, D_MODEL), F32), jax.ShapeDtypeStruct((N_DEV, T, IN_BLOCK), MM),
                   jax.ShapeDtypeStruct((8, D_MODEL), F32)],
        scratch_shapes=[pltpu.VMEM((tm, D_MODEL), F32)],
        compiler_params=_cparams(("arbitrary",), VMEM_LIMIT),
    )(x, dx1, *dsecs, g_mix, w_in)


def _pick_block(n, cap):
    best = None
    for cand in range(128, min(n, cap) + 1, 128):
        if n % cand == 0:
            best = cand
    return n if best is None else best


def _matmul_tn(name, a, b):
    T, M = a.shape
    N = b.shape[1]
    bm, bk = _pick_block(M, 1408), min(DW_TOKENS, T)
    bn = _pick_block(N, 2560)

    def body(a_ref, b_ref, c_ref):
        @pl.when(pl.program_id(2) == 0)
        def _():
            c_ref[...] = jnp.zeros_like(c_ref)

        c_ref[...] += _dot_tn(a_ref[...], b_ref[...])

    return pl.pallas_call(
        body, name=name, grid=(M // bm, N // bn, T // bk),
        in_specs=[pl.BlockSpec((bk, bm), lambda i, j, k: (k, i)), pl.BlockSpec((bk, bn), lambda i, j, k: (k, j))],
        out_specs=pl.BlockSpec((bm, bn), lambda i, j, k: (i, j)), out_shape=jax.ShapeDtypeStruct((M, N), F32),
        compiler_params=_cparams(("parallel", "parallel", "arbitrary"), VMEM_LIMIT),
    )(a, b)


def _matmul_tn_blocks(name, a, b):
    T, M = a.shape
    nd, _, c = b.shape
    bm, bk = _pick_block(M, 512), min(DW_TOKENS, T)

    def body(a_ref, b_ref, c_ref):
        @pl.when(pl.program_id(1) == 0)
        def _():
            c_ref[...] = jnp.zeros_like(c_ref)

        a_blk = a_ref[...].astype(MM)
        for d in range(nd):
            c_ref[d] += _dot_tn(b_ref[d], a_blk)

    return pl.pallas_call(
        body, name=name, grid=(M // bm, T // bk),
        in_specs=[pl.BlockSpec((bk, bm), lambda i, k: (k, i)), pl.BlockSpec((nd, bk, c), lambda i, k: (0, k, 0))],
        out_specs=pl.BlockSpec((nd, c, bm), lambda i, k: (0, 0, i)),
        out_shape=jax.ShapeDtypeStruct((nd, c, M), F32),
        compiler_params=_cparams(("parallel", "arbitrary"), VMEM_LIMIT),
    )(a, b)


def _pad_gain(g, n):
    return jnp.pad(g.reshape(1, -1), ((0, 0), (0, n - g.shape[-1])))


GROUP_A = ("w_ffn_gate", "w_ffn_up", "w_ffn_down", "w_ple_gate", "w_ple_proj")
GROUP_B = ("w_branch", "w_out")
GROUP_C = ("w_in", "w_uq", "w_ukv")
EARLY = GROUP_C
LATE = GROUP_B + GROUP_A
TRANSPOSED = ("w_in", "w_uq", "w_ffn_gate", "w_ffn_up")


def _local_step(x, p, pos, tgt, small, big, late_blocks=None, core=None):
    T = x.shape[0]
    tm = min(ROW_TILE, T)
    tw = min(WIDE_TILE, T)
    w_in = big["w_in"]
    w_uq = jnp.pad(big["w_uq"], ((0, 0), (0, HEAD_PAD - QK_DIM), (0, 0)))
    w_ukv = big["w_ukv"]

    g_mix, g_qa, g_kva = small["mix_norm_g"], small["q_a_norm_g"], small["kv_a_norm_g"]
    g_qn, g_kn = _pad_gain(small["q_norm_g"], HEAD_PAD), _pad_gain(small["k_norm_g"], HEAD_PAD)
    g_out, g_ffn = small["hg_out_norm_g"], small["ffn_norm_g"]
    g_pg, g_post = small["ple_gate_norm_g"], small["ple_post_norm_g"]
    logits = small["hg_lb_logits"]
    lb = _lower_bound(logits)

    h, cq, ckv, kr, hq, hf, hi, hg, bg = _in_proj_fwd(x, g_mix, w_in, T, tw)
    qf, kf, vf = _mla_prep_fwd(cq, ckv, kr, pos, g_qa, g_kva, g_qn, g_kn, w_uq, w_ukv, T, tw)
    if late_blocks is None:
        attn, lse = _flash_fwd(qf, kf, vf, T)
    else:
        attn, lse, *late = _flash_fwd(qf, kf, vf, T, ag_blocks=[late_blocks[n] for n in LATE])
        big = {**big, **dict(zip(LATE, late))}
    o, s0 = _hgrn_fwd(hq, hf, hi, lb, T)
    w_branch = jnp.moveaxis(big["w_branch"].reshape(N_DEV, 2, HG_W, HEAD_PAD), 0, 2).reshape(2, HG_W, D_MODEL)
    w_bra = jnp.pad(w_branch[0].reshape(MLA_HEADS, V_DIM, D_MODEL),
                    ((0, 0), (0, HEAD_PAD - V_DIM), (0, 0))).reshape(MLA_HEADS * HEAD_PAD, D_MODEL)
    w_brb = w_branch[1]
    w_out = big["w_out"].reshape(D_MODEL, D_MODEL)
    w_g, w_u = big["w_ffn_gate"].reshape(FFN, D_MODEL), big["w_ffn_up"].reshape(FFN, D_MODEL)
    w_d = big["w_ffn_down"].reshape(FFN, D_MODEL)
    w_pg, w_pp = big["w_ple_gate"].reshape(D_MODEL, D_MODEL), big["w_ple_proj"]
    x1, ya, yb, m, rec = _merge_fwd(attn, o, hg, bg, x, g_out, w_bra, w_brb, w_out, T, tw)
    x2, gt, up, h2 = _ffn_fwd(x1, g_ffn, w_g, w_u, w_d, T, tw)
    dx2, loss_p, dg_post, dg_pg, d_pg, d_pp = _ple_loss(x2, p, tgt, g_pg, g_post, w_pg, w_pp, T, tw)

    grads, sibs, gots = {}, {}, {}
    dist = core is not None
    pick = lambda names: [grads[n] for n in names] if dist else ()

    def partials(tag, names, got):
        if not dist:
            return ()
        sibs.update(zip(names, got))
        return _chip_partials("rs_partial_" + tag, pick(names), got, core)

    dx1, a, dgt, dup, dg_ffn = _ffn_bwd(dx2, x1, gt, up, g_ffn, w_g, w_u, w_d, T, tm)
    grads["w_ffn_gate"] = _matmul_tn("dw_gate", dgt, h2).reshape(N_DEV, -1, D_MODEL)
    grads["w_ffn_up"] = _matmul_tn("dw_up", dup, h2).reshape(N_DEV, -1, D_MODEL)
    grads["w_ffn_down"] = _matmul_tn("dw_down", a, dx2).reshape(N_DEV, -1, D_MODEL)
    grads["w_ple_gate"] = d_pg.reshape(N_DEV, -1, D_MODEL)
    grads["w_ple_proj"] = d_pp

    dattn, do, dhg, dbg, dg_out, d_out, d_bra, d_brb, *sib_a = _merge_bwd(
        dx1, ya, yb, bg, o, hg, attn, m, rec, g_out, w_bra, w_brb, w_out, T, tm, xchg=(pick(GROUP_A), ()))
    parts_a = partials("a", GROUP_A, sib_a)
    d_bra = d_bra.reshape(N_DEV, MLA_HEADS, HEAD_PAD, HEAD_PAD)[:, :, :V_DIM].reshape(N_DEV, HG_W, HEAD_PAD)
    grads["w_branch"] = jnp.concatenate([d_bra, d_brb], axis=1)
    grads["w_out"] = d_out.reshape(N_DEV, -1, D_MODEL)

    dhq, dhf, dhi, dlb, *got = _hgrn_bwd(hq, hf, hi, do, s0, lb, T, xchg=(pick(GROUP_B), parts_a))
    sib_b, got_a = got[:len(GROUP_B)], got[len(GROUP_B):]
    parts_b = partials("b", GROUP_B, sib_b)
    dqf, dkf, dvf, *got_b = _flash_bwd(qf, kf, vf, attn, dattn, lse, T, xchg=((), parts_b))
    (dcq, dckv, dkr, dg_qa, dg_kva, dg_qn, dg_kn, d_uq, d_ukv) = _mla_prep_bwd(
        cq, ckv, kr, pos, dqf, dkf, dvf, g_qa, g_kva, g_qn, g_kn, w_uq, w_ukv, T, tw)
    grad_x, dproj, dg_mix = _in_proj_bwd(x, dx1, [dcq, dckv, dkr, dhq, dhf, dhi, dhg, dbg], g_mix, w_in, T, tw)
    grads["w_in"] = _matmul_tn_blocks("dw_in", h, dproj)
    grads["w_uq"] = d_uq[:, :QK_DIM]
    grads["w_ukv"] = d_ukv
    parts_c = ()
    if dist:
        parts_c = partials("c", GROUP_C, _exchange_sibling("rs_sibling_c", pick(GROUP_C)))
        gots.update(zip(GROUP_A, got_a))
        gots.update(zip(GROUP_B, got_b))

    dl0 = dlb * lb * (1.0 - lb)
    small_g = {
        "mix_norm_g": dg_mix, "q_a_norm_g": dg_qa, "kv_a_norm_g": dg_kva, "q_norm_g": dg_qn, "k_norm_g": dg_kn,
        "hg_lb_logits": jnp.concatenate([dl0, -dl0], axis=0), "hg_out_norm_g": dg_out,
        "ffn_norm_g": dg_ffn, "ple_gate_norm_g": dg_pg, "ple_post_norm_g": dg_post,
    }
    return loss_p, grad_x, small_g, grads, sibs, gots, parts_c


def _lower_bound(logits):
    def body(l_ref, lb_ref):
        l = l_ref[...]
        mx = jnp.max(l, axis=0, keepdims=True)
        e = jnp.exp(l - mx)
        lb_ref[...] = e[0:1] / jnp.sum(e, axis=0, keepdims=True)

    return pl.pallas_call(body, name="lower_bound", out_shape=jax.ShapeDtypeStruct((1, HG_W), F32))(logits)


def _my_place():
    return lax.axis_index("x"), lax.axis_index("y"), lax.axis_index("c")


def _all_gather(name, blocks):
    n = len(blocks)

    def body(*refs):
        x_refs, out_refs, sems = refs[:n], refs[n:2 * n], refs[2 * n:]
        _ag_start(x_refs, out_refs, sems)
        _ag_finish(x_refs, out_refs, sems)

    any_spec = pl.BlockSpec(memory_space=pl.ANY)
    return pl.pallas_call(
        body, name=name, out_shape=_ag_out_shapes(blocks),
        in_specs=[any_spec] * n, out_specs=[any_spec] * n, scratch_shapes=_ag_sems(n),
    )(*blocks)


def _ag_out_shapes(blocks):
    return [jax.ShapeDtypeStruct((N_DEV,) + b.shape, b.dtype) for b in blocks]


def _ag_sems(n):
    return [pltpu.SemaphoreType.DMA((7 * n,)), pltpu.SemaphoreType.DMA((7 * n,)), pltpu.SemaphoreType.DMA((n,))]


def _ag_parts(x_refs, out_refs, sems):
    send_sems, recv_sems, local_sems = sems
    x, y, c = _my_place()
    me, sibling = (x, y, c), (x, y, 1 - c)
    chips = [(1 - x, y), (x, 1 - y), (1 - x, 1 - y)]
    n = len(x_refs)

    def copy(a, k, block, to, own=False):
        px, py, pc = block
        dst = out_refs[a].at[4 * px + 2 * py + pc]
        return pltpu.make_async_remote_copy(
            src_ref=x_refs[a] if own else dst, dst_ref=dst, send_sem=send_sems.at[7 * a + k],
            recv_sem=recv_sems.at[7 * a + k], device_id=to, device_id_type=MESH_ID)

    mine = [pltpu.make_async_copy(x_refs[a], out_refs[a].at[4 * x + 2 * y + c], local_sems.at[a]) for a in range(n)]
    first = []
    for a in range(n):
        first.append(copy(a, 0, me, sibling, own=True))
        first += [copy(a, 1 + j, me, (*chip, c), own=True) for j, chip in enumerate(chips)]
    return copy, mine, first, me, sibling, chips, c, n


def _ag_start(x_refs, out_refs, sems):
    _, mine, first, *_ = _ag_parts(x_refs, out_refs, sems)
    for cp in mine + first:
        cp.start()


def _ag_finish(x_refs, out_refs, sems):
    copy, mine, first, me, sibling, chips, c, n = _ag_parts(x_refs, out_refs, sems)
    passed = []
    for j, chip in enumerate(chips):
        for a in range(n):
            copy(a, 1 + j, (*chip, c), me).wait_recv()
            passed.append(copy(a, 4 + j, (*chip, c), sibling))
            passed[-1].start()
    for a in range(n):
        copy(a, 0, sibling, me).wait_recv()
    for j, chip in enumerate(chips):
        for a in range(n):
            copy(a, 4 + j, (*chip, 1 - c), me).wait_recv()
    for cp in first + passed:
        cp.wait_send()
    for cp in mine:
        cp.wait()


def _exchange_sibling(name, gs):
    return _exchange(name, (gs, ()))


def _exchange(name, xchg):
    n = _x_count(xchg)

    def body(*refs):
        in_refs, out_refs, sems = refs[:n], refs[n:2 * n], refs[2 * n:]
        for cp in _x_copies(len(xchg[0]), in_refs, out_refs, sems):
            cp.start()
        for cp in _x_copies(len(xchg[0]), in_refs, out_refs, sems):
            cp.wait()

    any_spec = pl.BlockSpec(memory_space=pl.ANY)
    return pl.pallas_call(
        body, name=name, out_shape=_x_out_shapes(xchg), in_specs=[any_spec] * n, out_specs=[any_spec] * n,
        scratch_shapes=_x_sems(xchg),
    )(*xchg[0], *xchg[1])


N_PARTS = 4


def _part_spec(rows, cols, t_pos, lead_block=(), lead_index=lambda *args: ()):
    if rows % (16 * N_PARTS) == 0:
        axis, shape, count = 0, (rows // N_PARTS, cols), N_PARTS
    elif cols % (128 * N_PARTS) == 0:
        axis, shape, count = 1, (rows, cols // N_PARTS), N_PARTS
    else:
        axis, shape, count = 0, (rows, cols), 1

    def index(*args):
        i = jnp.minimum(args[t_pos], count - 1)
        return (*lead_index(*args), *((i, 0) if axis == 0 else (0, i)))

    return pl.BlockSpec((*lead_block, *shape), index)


def _chip_partials(name, gs, sibs, c_idx):
    n = len(gs)

    def body(c_ref, *refs):
        for g_ref, sib_ref, out_ref in zip(refs[:n], refs[n:2 * n], refs[2 * n:]):
            out_ref[...] = (g_ref[...] + sib_ref[...]).astype(MM)

    own = [_part_spec(*g.shape[1:], 1, (1,), lambda j, t, c_ref: (2 * j + c_ref[0],)) for g in gs]
    by_chip = [_part_spec(*g.shape[1:], 1, (1,), lambda j, t, c_ref: (j,)) for g in gs]
    grid_spec = pltpu.PrefetchScalarGridSpec(
        num_scalar_prefetch=1, grid=(4, N_PARTS), in_specs=own + by_chip, out_specs=by_chip)
    return pl.pallas_call(
        body, name=name, grid_spec=grid_spec, out_shape=[jax.ShapeDtypeStruct((4,) + g.shape[1:], MM) for g in gs],
        compiler_params=_cparams(("arbitrary", "arbitrary"), VMEM_LIMIT),
    )(c_idx, *gs, *sibs)


def _exchange_chips(parts):
    return _exchange("rs_chips", ((), parts))


def _x_count(xchg):
    return len(xchg[0]) + len(xchg[1])


def _x_out_shapes(xchg):
    return ([jax.ShapeDtypeStruct((4,) + g.shape[1:], g.dtype) for g in xchg[0]]
            + [jax.ShapeDtypeStruct((3,) + p.shape[1:], p.dtype) for p in xchg[1]])


def _x_sems(xchg):
    n = 4 * len(xchg[0]) + 3 * len(xchg[1])
    return [pltpu.SemaphoreType.DMA((n,)), pltpu.SemaphoreType.DMA((n,))] if n else []


def _x_copies(n_sib, in_refs, out_refs, sems):
    if not in_refs:
        return []
    send_sems, recv_sems = sems
    x, y, c = _my_place()
    chips = [(1 - x, y), (x, 1 - y), (1 - x, 1 - y)]
    copies = []

    def add(src, dst, to):
        k = len(copies)
        copies.append(pltpu.make_async_remote_copy(
            src_ref=src, dst_ref=dst, send_sem=send_sems.at[k], recv_sem=recv_sems.at[k], device_id=to,
            device_id_type=MESH_ID))

    for a, (src, dst) in enumerate(zip(in_refs, out_refs)):
        if a < n_sib:
            for j in range(4):
                add(src.at[2 * j + 1 - c], dst.at[j], (x, y, 1 - c))
        else:
            for k, (px, py) in enumerate(chips):
                add(src.at[2 * px + py], dst.at[k], (px, py, c))
    return copies


def _adamw_math(w, g, m, v):
    m = ADAM_B1 * m + (1.0 - ADAM_B1) * g
    v = ADAM_B2 * v + (1.0 - ADAM_B2) * jnp.square(g)
    m_hat = m / (1.0 - ADAM_B1 ** ADAM_STEP)
    v_hat = v / (1.0 - ADAM_B2 ** ADAM_STEP)
    delta = -ADAM_LR * (m_hat / (jnp.sqrt(v_hat) + ADAM_EPS) + ADAM_WD * w)
    return delta, m, v


def _sum_adamws(name, gs, sibs, gots, ws, ms, vs, slot_idx, chip_idx):
    n = len(gs)

    def body(s_ref, j_ref, *refs):
        ins, outs = refs[:6 * n], refs[6 * n:]
        for a in range(n):
            g_ref, sib_ref, got_ref, w_ref, m_ref, v_ref = (ins[k * n + a] for k in range(6))
            go_ref, d_ref, m2_ref, v2_ref = outs[4 * a:4 * a + 4]
            grad = g_ref[0] + sib_ref[0]
            for k in range(3):
                grad = grad + got_ref[k].astype(F32)
            go_ref[...] = grad
            d_ref[...], m2_ref[...], v2_ref[...] = _adamw_math(w_ref[...], grad, m_ref[...], v_ref[...])

    shapes = [g.shape[1:] for g in gs]
    flat = [_part_spec(*s, 0) for s in shapes]
    in_specs = ([_part_spec(*s, 0, (1,), lambda t, s_ref, j_ref: (s_ref[0],)) for s in shapes]
                + [_part_spec(*s, 0, (1,), lambda t, s_ref, j_ref: (j_ref[0],)) for s in shapes]
                + [_part_spec(*s, 0, (3,), lambda t, s_ref, j_ref: (0,)) for s in shapes] + flat * 3)
    grid_spec = pltpu.PrefetchScalarGridSpec(
        num_scalar_prefetch=2, grid=(N_PARTS,), in_specs=in_specs, out_specs=[f for f in flat for _ in range(4)])
    res = pl.pallas_call(
        body, name=name, grid_spec=grid_spec,
        out_shape=[jax.ShapeDtypeStruct(s, F32) for s in shapes for _ in range(4)],
        compiler_params=_cparams(("arbitrary",), VMEM_LIMIT),
    )(slot_idx, chip_idx, *gs, *sibs, *gots, *ws, *ms, *vs)
    return [res[4 * a:4 * a + 4] for a in range(n)]


BIG = ("w_in", "w_uq", "w_ukv", "w_branch", "w_out", "w_ffn_gate", "w_ffn_up", "w_ffn_down", "w_ple_gate", "w_ple_proj")
SMALL = (
    ("mix_norm_g", 0, 1, 1024), ("q_a_norm_g", 1, 1, 384), ("kv_a_norm_g", 2, 1, 256), ("q_norm_g", 3, 1, 96),
    ("k_norm_g", 4, 1, 96), ("hg_lb_logits", 5, 2, 512), ("hg_out_norm_g", 7, 1, 128), ("ffn_norm_g", 8, 1, 1024),
    ("ple_gate_norm_g", 9, 1, 1024), ("ple_post_norm_g", 10, 1, 1024),
)
SLAB_ROWS, LOSS_ROW = 16, 15


def _pack_partials(small_g, loss_p):
    def body(*refs):
        val_refs, loss_ref, out_ref = refs[:len(SMALL)], refs[len(SMALL)], refs[len(SMALL) + 1]
        out_ref[...] = jnp.zeros_like(out_ref)
        for (_, r0, rows, cols), ref in zip(SMALL, val_refs):
            val = ref[...]
            if val.shape[0] != rows:
                val = jnp.sum(val, axis=0, keepdims=True)
            out_ref[r0:r0 + rows, :cols] = val[:, :cols]
        out_ref[LOSS_ROW:LOSS_ROW + 1, :HEAD_PAD] = jnp.full((1, HEAD_PAD), jnp.sum(loss_ref[...]), F32)

    return pl.pallas_call(
        body, name="pack_partials", out_shape=jax.ShapeDtypeStruct((SLAB_ROWS, D_MODEL), F32),
    )(*[small_g[n] for n, *_ in SMALL], loss_p)


def _adamw_small(parts, ws, ms, vs):
    n = len(SMALL)

    def body(p_ref, *refs):
        ins, loss_ref, outs = refs[:3 * n], refs[3 * n], refs[3 * n + 1:]
        total = p_ref[0]
        for d in range(1, N_DEV):
            total = total + p_ref[d]
        loss_ref[...] = total[LOSS_ROW:LOSS_ROW + 1, 0:1]
        for a, (_, r0, rows, cols) in enumerate(SMALL):
            g = total[r0:r0 + rows, :cols]
            outs[4 * a][...] = g
            outs[4 * a + 1][...], outs[4 * a + 2][...], outs[4 * a + 3][...] = _adamw_math(
                ins[a][...], g, ins[n + a][...], ins[2 * n + a][...])

    shapes = [jax.ShapeDtypeStruct((rows, cols), F32) for _, _, rows, cols in SMALL]
    res = pl.pallas_call(
        body, name="adamw_small", out_shape=[jax.ShapeDtypeStruct((1, 1), F32)] + [s for s in shapes for _ in range(4)],
    )(parts, *ws, *ms, *vs)
    return res[0], [res[1 + 4 * a:5 + 4 * a] for a in range(n)]


_WEIGHTS = ["mix_norm_g", "w_in", "q_a_norm_g", "w_uq", "kv_a_norm_g", "w_ukv", "q_norm_g", "k_norm_g", "hg_lb_logits",
            "hg_out_norm_g", "w_branch", "w_out", "ffn_norm_g", "w_ffn_gate", "w_ffn_up", "w_ffn_down",
            "ple_gate_norm_g", "w_ple_gate", "w_ple_proj", "ple_post_norm_g"]


def _step(x, p, positions, tgt, w, m, v):
    small_names = [n for n, *_ in SMALL]
    T = x.shape[1]
    px, py, pc = _my_place()
    as_idx = lambda t: jnp.reshape(t, (1,)).astype(jnp.int32)

    def two_d(n, t):
        t = t.reshape(-1, t.shape[-1])
        return t.T if n in TRANSPOSED else t

    def full_shape(n, t):
        return (t.T if n in TRANSPOSED else t).reshape(w[n].shape)

    blocks = {n: two_d(n, w[n]).astype(MM) for n in BIG}
    big = dict(zip(EARLY, _all_gather("ag_weights", [blocks[n] for n in EARLY])))
    small = {n: (w[n] if n == "hg_lb_logits" else w[n].reshape(1, -1)) for n in small_names}

    loss_p, grad_x, small_g, grads, sibs, gots, parts_c = _local_step(
        x[0], p[0, 0], positions.reshape(T, 1), tgt[0], small, big, late_blocks=blocks, core=as_idx(pc))

    gots.update(zip(GROUP_C, _exchange_chips(parts_c)))
    out_g, out_d, out_m, out_v = {}, {}, {}, {}
    for tag, names in (("ab", GROUP_A + GROUP_B), ("c", GROUP_C)):
        pick = lambda table: [table[n] for n in names]
        res = _sum_adamws("adamw_" + tag, pick(grads), pick(sibs), pick(gots), [two_d(n, w[n]) for n in names],
                          [two_d(n, m[n]) for n in names], [two_d(n, v[n]) for n in names],
                          as_idx(4 * px + 2 * py + pc), as_idx(2 * px + py))
        for n, r in zip(names, res):
            out_g[n], out_d[n], out_m[n], out_v[n] = [full_shape(n, t) for t in r]

    parts = _all_gather("ag_small", [_pack_partials(small_g, loss_p)])[0]
    loss, res = _adamw_small(parts, *([t[n] for n in small_names] for t in (w, m, v)))
    for n, r in zip(small_names, res):
        out_g[n], out_d[n], out_m[n], out_v[n] = r

    outs = [loss.reshape(()), grad_x[None]]
    for table in (out_g, out_d, out_m, out_v):
        outs += [table[n] for n in _WEIGHTS]
    return tuple(outs)


def kernel(x, p, positions, mix_norm_g, w_in, q_a_norm_g, w_uq, kv_a_norm_g, w_ukv, q_norm_g, k_norm_g, hg_lb_logits, hg_out_norm_g, w_branch, w_out, ffn_norm_g, w_ffn_gate, w_ffn_up, w_ffn_down, ple_gate_norm_g, w_ple_gate, w_ple_proj, ple_post_norm_g, loss_target, m_mix_norm_g, m_w_in, m_q_a_norm_g, m_w_uq, m_kv_a_norm_g, m_w_ukv, m_q_norm_g, m_k_norm_g, m_hg_lb_logits, m_hg_out_norm_g, m_w_branch, m_w_out, m_ffn_norm_g, m_w_ffn_gate, m_w_ffn_up, m_w_ffn_down, m_ple_gate_norm_g, m_w_ple_gate, m_w_ple_proj, m_ple_post_norm_g, v_mix_norm_g, v_w_in, v_q_a_norm_g, v_w_uq, v_kv_a_norm_g, v_w_ukv, v_q_norm_g, v_k_norm_g, v_hg_lb_logits, v_hg_out_norm_g, v_w_branch, v_w_out, v_ffn_norm_g, v_w_ffn_gate, v_w_ffn_up, v_w_ffn_down, v_ple_gate_norm_g, v_w_ple_gate, v_w_ple_proj, v_ple_post_norm_g):
    w = dict(mix_norm_g=mix_norm_g, w_in=w_in, q_a_norm_g=q_a_norm_g, w_uq=w_uq, kv_a_norm_g=kv_a_norm_g, w_ukv=w_ukv,
             q_norm_g=q_norm_g, k_norm_g=k_norm_g, hg_lb_logits=hg_lb_logits, hg_out_norm_g=hg_out_norm_g,
             w_branch=w_branch, w_out=w_out, ffn_norm_g=ffn_norm_g, w_ffn_gate=w_ffn_gate, w_ffn_up=w_ffn_up,
             w_ffn_down=w_ffn_down, ple_gate_norm_g=ple_gate_norm_g, w_ple_gate=w_ple_gate, w_ple_proj=w_ple_proj,
             ple_post_norm_g=ple_post_norm_g)
    m = dict(mix_norm_g=m_mix_norm_g, w_in=m_w_in, q_a_norm_g=m_q_a_norm_g, w_uq=m_w_uq, kv_a_norm_g=m_kv_a_norm_g,
             w_ukv=m_w_ukv, q_norm_g=m_q_norm_g, k_norm_g=m_k_norm_g, hg_lb_logits=m_hg_lb_logits,
             hg_out_norm_g=m_hg_out_norm_g, w_branch=m_w_branch, w_out=m_w_out, ffn_norm_g=m_ffn_norm_g,
             w_ffn_gate=m_w_ffn_gate, w_ffn_up=m_w_ffn_up, w_ffn_down=m_w_ffn_down,
             ple_gate_norm_g=m_ple_gate_norm_g, w_ple_gate=m_w_ple_gate, w_ple_proj=m_w_ple_proj,
             ple_post_norm_g=m_ple_post_norm_g)
    v = dict(mix_norm_g=v_mix_norm_g, w_in=v_w_in, q_a_norm_g=v_q_a_norm_g, w_uq=v_w_uq, kv_a_norm_g=v_kv_a_norm_g,
             w_ukv=v_w_ukv, q_norm_g=v_q_norm_g, k_norm_g=v_k_norm_g, hg_lb_logits=v_hg_lb_logits,
             hg_out_norm_g=v_hg_out_norm_g, w_branch=v_w_branch, w_out=v_w_out, ffn_norm_g=v_ffn_norm_g,
             w_ffn_gate=v_w_ffn_gate, w_ffn_up=v_w_ffn_up, w_ffn_down=v_w_ffn_down,
             ple_gate_norm_g=v_ple_gate_norm_g, w_ple_gate=v_w_ple_gate, w_ple_proj=v_w_ple_proj,
             ple_post_norm_g=v_ple_post_norm_g)
    return _step(x, p, positions, loss_target, w, m, v)
```

```python
import jax
import jax.numpy as jnp
import numpy as np
from jax import lax
from jax.experimental import pallas as pl
from jax.experimental.pallas import tpu as pltpu

F32 = jnp.float32
MM = jnp.bfloat16
MESH_ID = pl.DeviceIdType.MESH

D_MODEL = 1024
N_DEV = 8
MLA_HEADS = 8
QK_NOPE = 64
QK_ROPE = 32
QK_DIM = 96
V_DIM = 64
HEAD_PAD = 128
Q_RANK = 384
KV_RANK = 256
ROPE_BASE = 10000.0
HG_HEADS = 4
HG_DIM = 128
HG_W = 512
HG_CHUNK = 64
FFN = 2816
PLE = 256
EPS = 1e-6
ATT_SCALE = QK_DIM ** -0.5
NEG = -1e30

ADAM_LR = 0.001
ADAM_B1 = 0.9
ADAM_B2 = 0.999
ADAM_EPS = 1e-08
ADAM_WD = 0.01
ADAM_STEP = 10

COL_SECTIONS = ((0, 384), (384, 256), (640, 32), (672, 512), (1184, 512), (1696, 512), (2208, 512), (2720, 2048))
STORED_WIDTHS = tuple(HEAD_PAD if n == QK_ROPE else n for _, n in COL_SECTIONS)
IN_COLS = 4768
IN_BLOCK = IN_COLS // N_DEV

VMEM_LIMIT = 58 * 1024 * 1024
WIDE_TILE = 512
ROW_TILE = 256
DW_TOKENS = 1024
ATT_TILE = 1024
ATT_HEADS = 4
HG_BLOCK = 512
HG_UNROLL = 4


def _dot(a, b):
    return jnp.dot(a.astype(MM), b.astype(MM), preferred_element_type=F32)


def _dot_nt(a, b):
    return lax.dot_general(a.astype(MM), b.astype(MM), (((1,), (1,)), ((), ())), preferred_element_type=F32)


def _dot_tn(a, b):
    return lax.dot_general(a.astype(MM), b.astype(MM), (((0,), (0,)), ((), ())), preferred_element_type=F32)


def _sigmoid(x):
    return 1.0 / (1.0 + jnp.exp(-x))


def _rms(x, n=None):
    n = x.shape[-1] if n is None else n
    r = lax.rsqrt(jnp.sum(x * x, axis=-1, keepdims=True) * (1.0 / n) + EPS)
    return x * r, r


def _rms_bwd(dxh, xh, r, n=None):
    n = xh.shape[-1] if n is None else n
    return r * (dxh - xh * (jnp.sum(dxh * xh, axis=-1, keepdims=True) * (1.0 / n)))


def _rope_tables(pos, tm):
    lane = lax.broadcasted_iota(jnp.int32, (tm, HEAD_PAD), 1)
    idx = jnp.where(lane < QK_NOPE + QK_ROPE // 2, lane - QK_NOPE, lane - QK_NOPE - QK_ROPE // 2)
    inv = jnp.exp(idx.astype(F32) * (-np.log(ROPE_BASE) * 2.0 / QK_ROPE))
    ang = pos.astype(F32) * inv
    in_rope = (lane >= QK_NOPE) & (lane < QK_DIM)
    first = lane < QK_NOPE + QK_ROPE // 2
    cos_t = jnp.where(in_rope, jnp.cos(ang), 1.0)
    sin_t = jnp.where(in_rope, jnp.where(first, -jnp.sin(ang), jnp.sin(ang)), 0.0)
    return cos_t, sin_t, (first, in_rope)


def _rope_swap(x, halves):
    first, in_rope = halves
    half = QK_ROPE // 2
    return jnp.where(in_rope, jnp.where(first, pltpu.roll(x, HEAD_PAD - half, 1), pltpu.roll(x, half, 1)), 0.0)


def _cparams(sem, vmem=None):
    return pltpu.CompilerParams(dimension_semantics=sem, vmem_limit_bytes=vmem)


def _row_call(name, body, T, tm, row_ins, full_ins, row_outs, acc_outs, vmem=None, scratch=(), xchg=((), ())):
    n_in, n_out, n_x = len(row_ins) + len(full_ins), len(row_outs) + len(acc_outs), _x_count(xchg)
    steps = T // tm

    def kern(*refs):
        ins, x_in, refs = refs[:n_in], refs[n_in:n_in + n_x], refs[n_in + n_x:]
        outs, x_out, refs = refs[:n_out], refs[n_out:n_out + n_x], refs[n_out + n_x:]
        scr, x_sems = refs[:len(scratch)], refs[len(scratch):]
        i = pl.program_id(0)
        if n_x:
            @pl.when(i == 0)
            def _():
                for cp in _x_copies(len(xchg[0]), x_in, x_out, x_sems):
                    cp.start()

        body(i, *ins, *outs, *scr)
        if n_x:
            @pl.when(i == steps - 1)
            def _():
                for cp in _x_copies(len(xchg[0]), x_in, x_out, x_sems):
                    cp.wait()

    any_spec = pl.BlockSpec(memory_space=pl.ANY)
    in_specs = [pl.BlockSpec((tm, a.shape[1]), lambda i: (i, 0)) for a in row_ins]
    in_specs += [pl.BlockSpec(a.shape, lambda i, nd=a.ndim: (0,) * nd, pipeline_mode=pl.Buffered(1)) for a in full_ins]
    out_specs = [pl.BlockSpec((tm, n), lambda i: (i, 0)) for n, _ in row_outs]
    out_specs += [pl.BlockSpec(s, lambda i, nd=len(s): (0,) * nd) for s, _ in acc_outs]
    out_shape = [jax.ShapeDtypeStruct((T, n), dt) for n, dt in row_outs]
    out_shape += [jax.ShapeDtypeStruct(s, dt) for s, dt in acc_outs]
    return pl.pallas_call(
        kern, name=name, grid=(steps,), in_specs=in_specs + [any_spec] * n_x, out_specs=out_specs + [any_spec] * n_x,
        out_shape=out_shape + _x_out_shapes(xchg), scratch_shapes=list(scratch) + _x_sems(xchg),
        compiler_params=_cparams(("arbitrary",), vmem),
    )(*row_ins, *full_ins, *xchg[0], *xchg[1])


FFN_HALVES = (slice(0, FFN // 2), slice(FFN // 2, FFN))
ROW_CHUNK = 16
CHUNK_UNROLL = True


def _by_chunks(tm, fn):
    def step(c, carry):
        fn(pl.ds(pl.multiple_of(c * ROW_CHUNK, ROW_CHUNK), ROW_CHUNK))
        return carry

    lax.fori_loop(0, tm // ROW_CHUNK, step, 0, unroll=CHUNK_UNROLL)


def _fold8(x):
    return x[:8] + x[8:]


def _acc(ref, i, val):
    @pl.when(i == 0)
    def _():
        ref[...] = val

    @pl.when(i != 0)
    def _():
        ref[...] += val


def _in_proj_fwd(x, g_mix, w_in, T, tm):
    def body(i, x_ref, g_ref, w_ref, h_ref, *rest):
        outs, pj_s = rest[:-1], rest[-1]
        g = g_ref[...]

        def norm(rows):
            h_ref[rows, :] = (_rms(x_ref[rows, :])[0] * g).astype(MM)

        _by_chunks(tm, norm)
        for d in range(N_DEV):
            pj_s[d] = _dot_nt(h_ref[...], w_ref[d])

        def join_and_cut(rows):
            proj = jnp.concatenate([pj_s[d, rows, :] for d in range(N_DEV)], axis=1)
            for (s, n), o_ref in zip(COL_SECTIONS, outs):
                if n == QK_ROPE:
                    o_ref[rows, :] = jnp.concatenate(
                        [jnp.zeros((ROW_CHUNK, QK_NOPE), F32), proj[:, s:s + n],
                         jnp.zeros((ROW_CHUNK, HEAD_PAD - QK_DIM), F32)], axis=1)
                else:
                    o_ref[rows, :] = proj[:, s:s + n]

        _by_chunks(tm, join_and_cut)

    row_outs = [(D_MODEL, MM)] + [(n, F32) for n in STORED_WIDTHS]
    return _row_call("in_proj_fwd", body, T, tm, [x], [g_mix, w_in], row_outs, [], VMEM_LIMIT,
                     scratch=[pltpu.VMEM((N_DEV, tm, IN_BLOCK), F32)])


def _mla_heads_fwd(raw, g_pad, cos_t, sin_t, first):
    outs, saved = [], []
    for h in range(MLA_HEADS):
        xh, r = _rms(raw[:, h * HEAD_PAD:(h + 1) * HEAD_PAD], QK_DIM)
        y = xh * g_pad
        outs.append(y * cos_t + _rope_swap(y, first) * sin_t)
        saved.append((xh, r))
    return outs, saved


def _mla_raw_heads(cqn, ckvn, kr, wuq_ref, wukv_ref, tm):
    lane = lax.broadcasted_iota(jnp.int32, (tm, HEAD_PAD), 1)
    nope = lane < QK_NOPE
    one_lane = jnp.where(lane == V_DIM, 1.0, 0.0)
    qs, ks, vs = [], [], []
    for h in range(MLA_HEADS):
        qs.append(_dot_nt(cqn, wuq_ref[h]))
        kv = _dot(ckvn, wukv_ref[h])
        ks.append(jnp.where(nope, kv, kr))
        vs.append(jnp.where(nope, pltpu.roll(kv, V_DIM, 1), one_lane))
    return jnp.concatenate(qs, axis=1), jnp.concatenate(ks, axis=1), jnp.concatenate(vs, axis=1)


def _mla_prep_fwd(cq, ckv, kr, pos, g_qa, g_kva, g_qn, g_kn, w_uq, w_ukv, T, tm):
    def body(i, cq_ref, ckv_ref, kr_ref, pos_ref, gqa_ref, gkva_ref, gqn_ref, gkn_ref, wuq_ref, wukv_ref,
             q_ref, k_ref, v_ref):
        cos_t, sin_t, first = _rope_tables(pos_ref[...], tm)
        cqn = _rms(cq_ref[...])[0] * gqa_ref[...]
        ckvn = _rms(ckv_ref[...])[0] * gkva_ref[...]
        q_raw, k_raw, v = _mla_raw_heads(cqn, ckvn, kr_ref[...], wuq_ref, wukv_ref, tm)
        qs, _ = _mla_heads_fwd(q_raw, gqn_ref[...], cos_t, sin_t, first)
        ks, _ = _mla_heads_fwd(k_raw, gkn_ref[...], cos_t, sin_t, first)
        q_ref[...] = (jnp.concatenate(qs, axis=1) * ATT_SCALE).astype(MM)
        k_ref[...] = jnp.concatenate(ks, axis=1).astype(MM)
        v_ref[...] = v.astype(MM)

    w = MLA_HEADS * HEAD_PAD
    return _row_call("mla_prep_fwd", body, T, tm, [cq, ckv, kr, pos], [g_qa, g_kva, g_qn, g_kn, w_uq, w_ukv],
                     [(w, MM), (w, MM), (w, MM)], [])


def _causal_pairs(n, by_query):
    if by_query:
        pairs = [(q, k) for q in range(n) for k in range(q + 1)]
    else:
        pairs = [(q, k) for k in range(n) for q in range(k, n)]
    return np.array([p[0] for p in pairs], np.int32), np.array([p[1] for p in pairs], np.int32)


def _flash_fwd(qf, kf, vf, T, ag_blocks=()):
    tq = min(ATT_TILE, T)
    nq = T // tq

    qi_tab, ki_tab = _causal_pairs(nq, by_query=True)

    hp = ATT_HEADS

    n_ag = len(ag_blocks)
    n_heads, n_pairs = MLA_HEADS // hp, len(qi_tab)

    def body(qi_ref, ki_ref, q_ref, k_ref, v_ref, *rest):
        ag_in, (o_ref, lse_ref), rest = rest[:n_ag], rest[n_ag:n_ag + 2], rest[n_ag + 2:]
        ag_out, (m_s, acc_s), ag_sems = rest[:n_ag], rest[n_ag:n_ag + 2], rest[n_ag + 2:]
        t = pl.program_id(1)
        qi, ki = qi_ref[t], ki_ref[t]
        if n_ag:
            @pl.when((pl.program_id(0) == 0) & (t == 0))
            def _():
                _ag_start(ag_in, ag_out, ag_sems)

        @pl.when(ki == 0)
        def _():
            m_s[...] = jnp.full_like(m_s, NEG)
            acc_s[...] = jnp.zeros_like(acc_s)

        def step(masked):
            halves = 2 if masked and tq % (2 * HEAD_PAD) == 0 else 1
            w = tq // halves
            for hh in range(hp):
                hs = slice(hh * HEAD_PAD, (hh + 1) * HEAD_PAD)
                for part in range(halves):
                    cols, nk = slice(part * w, (part + 1) * w), (part + 1) * w
                    s_t = _dot_nt(k_ref[:nk, hs], q_ref[cols, hs])
                    if masked:
                        key = lax.broadcasted_iota(jnp.int32, (nk, w), 0)
                        qry = lax.broadcasted_iota(jnp.int32, (nk, w), 1) + part * w
                        s_t = jnp.where(key <= qry, s_t, NEG)
                    m_old = m_s[hh, :, cols]
                    m_new = jnp.maximum(m_old, jnp.max(s_t, axis=0, keepdims=True))
                    p_t = jnp.exp(s_t - m_new)
                    acc_s[hh, :, cols] = jnp.exp(m_old - m_new) * acc_s[hh, :, cols] + _dot_tn(v_ref[:nk, hs], p_t)
                    m_s[hh, :, cols] = m_new

        @pl.when(ki < qi)
        def _():
            step(False)

        @pl.when(ki == qi)
        def _():
            step(True)
            real = lax.broadcasted_iota(jnp.int32, (HEAD_PAD, tq), 0) < V_DIM
            for hh in range(hp):
                hs = slice(hh * HEAD_PAD, (hh + 1) * HEAD_PAD)
                acc = acc_s[hh]
                l = acc[V_DIM:V_DIM + 1]
                o_ref[:, hs] = jnp.where(real, acc / l, 0.0).T
                lse_ref[:, hs] = jnp.broadcast_to(m_s[hh] + jnp.log(l), (HEAD_PAD, tq)).T

        if n_ag:
            @pl.when((pl.program_id(0) == n_heads - 1) & (t == n_pairs - 1))
            def _():
                _ag_finish(ag_in, ag_out, ag_sems)

    q_spec = pl.BlockSpec((tq, hp * HEAD_PAD), lambda h, t, qi_ref, ki_ref: (qi_ref[t], h))
    kv_spec = pl.BlockSpec((tq, hp * HEAD_PAD), lambda h, t, qi_ref, ki_ref: (ki_ref[t], h))
    any_spec = pl.BlockSpec(memory_space=pl.ANY)
    grid_spec = pltpu.PrefetchScalarGridSpec(
        num_scalar_prefetch=2, grid=(n_heads, n_pairs),
        in_specs=[q_spec, kv_spec, kv_spec] + [any_spec] * n_ag, out_specs=[q_spec, q_spec] + [any_spec] * n_ag,
        scratch_shapes=[pltpu.VMEM((hp, 1, tq), F32), pltpu.VMEM((hp, HEAD_PAD, tq), F32)]
        + (_ag_sems(n_ag) if n_ag else []))
    return pl.pallas_call(
        body, name="flash_fwd", grid_spec=grid_spec,
        out_shape=[jax.ShapeDtypeStruct((T, MLA_HEADS * HEAD_PAD), F32)] * 2 + _ag_out_shapes(ag_blocks),
        compiler_params=_cparams(("arbitrary", "arbitrary")),
    )(jnp.asarray(qi_tab), jnp.asarray(ki_tab), qf, kf, vf, *ag_blocks)


def _hg_gates(hf, lb):
    sg = _sigmoid(hf)
    f = lb + (1.0 - lb) * sg
    return sg, f, jnp.log(f), 1.0 - f


def _prefix_sum(x, reverse=False):
    n = x.shape[0]
    row = lax.broadcasted_iota(jnp.int32, x.shape, 0)
    step = 1
    while step < n:
        if reverse:
            x = x + jnp.where(row < n - step, pltpu.roll(x, n - step, 0), 0.0)
        else:
            x = x + jnp.where(row >= step, pltpu.roll(x, step, 0), 0.0)
        step *= 2
    return x


def _hg_levels():
    C = HG_CHUNK
    t = lax.broadcasted_iota(jnp.int32, (C, C), 0)
    s = lax.broadcasted_iota(jnp.int32, (C, C), 1)
    levels = []
    for shift in range(C.bit_length() - 2, -1, -1):
        pair_t, pair_s = lax.shift_right_logical(t, shift + 1), lax.shift_right_logical(s, shift + 1)
        later_t = (lax.shift_right_logical(t, shift) & 1) == 1
        earlier_s = (lax.shift_right_logical(s, shift) & 1) == 0
        levels.append((1 << shift, (pair_t == pair_s) & later_t & earlier_s))
    return levels, t == s


def _hg_refs(b):
    C, n = b.shape
    row = lax.broadcasted_iota(jnp.int32, (C, n), 0)
    back1, back2, ahead1 = pltpu.roll(b, 1, 0), pltpu.roll(b, 2, 0), pltpu.roll(b, C - 1, 0)
    refs = []
    for half in (32, 16, 8, 4):
        refs.append(jnp.concatenate(
            [jnp.broadcast_to(b[lo + half - 1:lo + half], (2 * half, n)) for lo in range(0, C, 2 * half)], axis=0))
    in4 = row & 3
    refs.append(jnp.where(in4 == 0, ahead1, jnp.where(in4 == 1, b, jnp.where(in4 == 2, back1, back2))))
    refs.append(jnp.where((row & 1) == 1, back1, b))
    return refs


def _hg_intra(q, k, b, refs, levels, eye):
    a = jnp.where(eye, jnp.sum(q * k, axis=1, keepdims=True), 0.0)
    saved = []
    for r, (_, mask) in zip(refs, levels):
        e = jnp.exp(-jnp.abs(b - r))
        q_t, k_t = q * e, k * e
        a = a + jnp.where(mask, _dot_nt(q_t, k_t), 0.0)
        saved.append((q_t, k_t, e))
    return a, saved


def _hg_intra_bwd(d_a, q, k, saved, levels, eye):
    diag = jnp.sum(jnp.where(eye, d_a, 0.0), axis=1, keepdims=True)
    dq, dk = diag * k, diag * q
    for (q_t, k_t, e), (_, mask) in zip(saved, levels):
        da = jnp.where(mask, d_a, 0.0)
        dq = dq + _dot(da, k_t) * e
        dk = dk + _dot_tn(da, q_t) * e
    return dq, dk


def _hgrn_fwd(hq, hf, hi, lb, T):
    rb = min(HG_BLOCK, T)
    ncb = rb // HG_CHUNK

    def body(hq_ref, hf_ref, hi_ref, lb_ref, o_ref, s0_ref, st_ref):
        @pl.when(pl.program_id(0) == 0)
        def _():
            st_ref[...] = jnp.zeros_like(st_ref)

        levels, eye = _hg_levels()

        def chunk(c, carry):
            rows = pl.ds(pl.multiple_of(c * HG_CHUNK, HG_CHUNK), HG_CHUNK)
            _, _, logf, kk = _hg_gates(hf_ref[rows, :], lb_ref[...])
            b = _prefix_sum(logf)
            refs = _hg_refs(b)
            q_all, v_all = hq_ref[rows, :], hi_ref[rows, :]
            outs = []
            for h in range(HG_HEADS):
                ls = slice(h * HG_DIM, (h + 1) * HG_DIM)
                q, k, v, bh = q_all[:, ls], kk[:, ls], v_all[:, ls], b[:, ls]
                st = st_ref[h]
                s0_ref[c, h * HG_DIM:(h + 1) * HG_DIM, :] = st
                b_end = bh[HG_CHUNK - 1:HG_CHUNK]
                a, _ = _hg_intra(q, k, bh, [r[:, ls] for r in refs], levels, eye)
                outs.append(_dot_nt(q * jnp.exp(bh), st) + _dot(a, v))
                st_ref[h] = st * jnp.exp(b_end) + _dot_tn(v, k * jnp.exp(b_end - bh))
            o_ref[rows, :] = jnp.concatenate(outs, axis=1)
            return carry

        lax.fori_loop(0, ncb, chunk, 0, unroll=HG_UNROLL)

    row = pl.BlockSpec((rb, HG_W), lambda i: (i, 0))
    return pl.pallas_call(
        body, name="hgrn_fwd", grid=(T // rb,),
        in_specs=[row, row, row, pl.BlockSpec((1, HG_W), lambda i: (0, 0))],
        out_specs=[row, pl.BlockSpec((ncb, HG_W, HG_DIM), lambda i: (i, 0, 0))],
        out_shape=[jax.ShapeDtypeStruct((T, HG_W), F32), jax.ShapeDtypeStruct((T // HG_CHUNK, HG_W, HG_DIM), F32)],
        scratch_shapes=[pltpu.VMEM((HG_HEADS, HG_DIM, HG_DIM), F32)],
        compiler_params=_cparams(("arbitrary",)),
    )(hq, hf, hi, lb)


def _hgrn_bwd(hq, hf, hi, do, s0, lb, T, xchg=((), ())):
    rb = min(HG_BLOCK, T)
    ncb = rb // HG_CHUNK
    nb = T // rb
    C = HG_CHUNK
    n_x, n_sib = _x_count(xchg), len(xchg[0])

    def body(hq_ref, hf_ref, hi_ref, do_ref, s0_ref, lb_ref, *rest):
        x_in, (dq_ref, df_ref, dv_ref, dlb_ref), rest = rest[:n_x], rest[n_x:n_x + 4], rest[n_x + 4:]
        x_out, dst_ref, x_sems = rest[:n_x], rest[n_x], rest[n_x + 1:]

        @pl.when(pl.program_id(0) == 0)
        def _():
            dst_ref[...] = jnp.zeros_like(dst_ref)
            dlb_ref[...] = jnp.zeros_like(dlb_ref)
            for cp in _x_copies(n_sib, x_in, x_out, x_sems):
                cp.start()

        row_cc = lax.broadcasted_iota(jnp.int32, (C, C), 0)
        col_cc = lax.broadcasted_iota(jnp.int32, (C, C), 1)
        last_row = lax.broadcasted_iota(jnp.int32, (C, HG_DIM), 0) == C - 1
        lb_v = lb_ref[...]
        levels, eye = _hg_levels()

        def chunk(cc, carry):
            c = ncb - 1 - cc
            rows = pl.ds(pl.multiple_of(c * C, C), C)
            hf_c = hf_ref[rows, :]
            sg, f, logf, kk = _hg_gates(hf_c, lb_v)
            b = _prefix_sum(logf)
            refs = _hg_refs(b)
            q_all, v_all, do_all = hq_ref[rows, :], hi_ref[rows, :], do_ref[rows, :]
            dq_o, dk_o, dv_o, db_o = [], [], [], []
            for h in range(HG_HEADS):
                ls = slice(h * HG_DIM, (h + 1) * HG_DIM)
                q, k, v, bh, d_o = q_all[:, ls], kk[:, ls], v_all[:, ls], b[:, ls], do_all[:, ls]
                st0 = s0_ref[c, h * HG_DIM:(h + 1) * HG_DIM, :]
                dst = dst_ref[h]
                b_end = bh[C - 1:C]
                e_b, e_end = jnp.exp(bh), jnp.exp(b_end)
                e_rem = jnp.exp(b_end - bh)
                qe, kd = q * e_b, k * e_rem
                st_end = st0 * e_end + _dot_tn(v, kd)
                a, saved = _hg_intra(q, k, bh, [r[:, ls] for r in refs], levels, eye)
                d_a = jnp.where(col_cc <= row_cc, _dot_nt(d_o, v), 0.0)
                dq_i, dk_i = _hg_intra_bwd(d_a, q, k, saved, levels, eye)
                dv = _dot_tn(a, d_o) + _dot_nt(kd, dst)
                dq = e_b * _dot(d_o, st0) + dq_i
                dk = e_rem * _dot(v, dst) + dk_i
                extra = jnp.sum(dst * st_end, axis=0, keepdims=True)
                db_o.append(q * dq - k * dk + jnp.where(last_row, extra, 0.0))
                dst_ref[h] = dst * e_end + _dot_tn(d_o, qe)
                dq_o.append(dq)
                dk_o.append(dk)
                dv_o.append(dv)
            dlogf = _prefix_sum(jnp.concatenate(db_o, axis=1), reverse=True)
            d_f = dlogf / f - jnp.concatenate(dk_o, axis=1)
            dq_ref[rows, :] = jnp.concatenate(dq_o, axis=1).astype(MM)
            dv_ref[rows, :] = jnp.concatenate(dv_o, axis=1).astype(MM)
            df_ref[rows, :] = (d_f * (1.0 - lb_v) * sg * (1.0 - sg)).astype(MM)
            dlb_ref[...] += jnp.sum(d_f * (1.0 - sg), axis=0, keepdims=True)
            return carry

        lax.fori_loop(0, ncb, chunk, 0, unroll=HG_UNROLL)

        if n_x:
            @pl.when(pl.program_id(0) == nb - 1)
            def _():
                for cp in _x_copies(n_sib, x_in, x_out, x_sems):
                    cp.wait()

    row = pl.BlockSpec((rb, HG_W), lambda i: (nb - 1 - i, 0))
    one = pl.BlockSpec((1, HG_W), lambda i: (0, 0))
    any_spec = pl.BlockSpec(memory_space=pl.ANY)
    return pl.pallas_call(
        body, name="hgrn_bwd", grid=(nb,),
        in_specs=[row, row, row, row, pl.BlockSpec((ncb, HG_W, HG_DIM), lambda i: (nb - 1 - i, 0, 0)), one]
        + [any_spec] * n_x,
        out_specs=[row, row, row, one] + [any_spec] * n_x,
        out_shape=[jax.ShapeDtypeStruct((T, HG_W), MM)] * 3 + [jax.ShapeDtypeStruct((1, HG_W), F32)]
        + _x_out_shapes(xchg),
        scratch_shapes=[pltpu.VMEM((HG_HEADS, HG_DIM, HG_DIM), F32)] + _x_sems(xchg),
        compiler_params=_cparams(("arbitrary",)),
    )(hq, hf, hi, do, s0, lb, *xchg[0], *xchg[1])


def _silu_parts(x):
    sg = _sigmoid(x)
    return x * sg, sg * (1.0 + x * (1.0 - sg))


def _merge_fwd(attn, o, hg, bg, x, g_out, w_bra, w_brb, w_out, T, tm):
    def body(i, attn_ref, o_ref, hg_ref, bg_ref, x_ref, g_ref, wa_ref, wb_ref, wo_ref,
             x1_ref, ya_ref, yb_ref, m_ref, rec_ref):
        g = g_ref[...]

        def recurrent_out(rows):
            for h in range(HG_HEADS):
                ls = slice(h * HG_DIM, (h + 1) * HG_DIM)
                rec_ref[rows, ls] = (_rms(o_ref[rows, ls])[0] * g * _silu_parts(hg_ref[rows, ls])[0]).astype(MM)

        _by_chunks(tm, recurrent_out)
        ya_ref[...] = _dot(attn_ref[...], wa_ref[...])
        yb_ref[...] = jnp.dot(rec_ref[...], wb_ref[...], preferred_element_type=F32)

        def gate(rows):
            m_ref[rows, :] = (_sigmoid(bg_ref[rows, :D_MODEL]) * ya_ref[rows, :]
                              + _sigmoid(bg_ref[rows, D_MODEL:]) * yb_ref[rows, :]).astype(MM)

        _by_chunks(tm, gate)
        x1_ref[...] = x_ref[...] + jnp.dot(m_ref[...], wo_ref[...], preferred_element_type=F32)

    return _row_call("merge_fwd", body, T, tm, [attn, o, hg, bg, x], [g_out, w_bra, w_brb, w_out],
                     [(D_MODEL, F32), (D_MODEL, F32), (D_MODEL, F32), (D_MODEL, MM), (HG_W, MM)], [], VMEM_LIMIT)


def _ffn_fwd(x1, g_ffn, w_g, w_u, w_d, T, tm):
    def body(i, x1_ref, g_ref, wg_ref, wu_ref, wd_ref, x2_ref, gt_ref, up_ref, h2_ref, a_s):
        g = g_ref[...]

        def norm(rows):
            h2_ref[rows, :] = (_rms(x1_ref[rows, :])[0] * g).astype(MM)

        _by_chunks(tm, norm)
        gt_ref[...] = _dot_nt(h2_ref[...], wg_ref[...])
        up_ref[...] = _dot_nt(h2_ref[...], wu_ref[...])

        def act(rows):
            for cs in FFN_HALVES:
                a_s[rows, cs] = (_silu_parts(gt_ref[rows, cs])[0] * up_ref[rows, cs]).astype(MM)

        _by_chunks(tm, act)
        x2_ref[...] = x1_ref[...] + jnp.dot(a_s[...], wd_ref[...], preferred_element_type=F32)

    return _row_call("ffn_fwd", body, T, tm, [x1], [g_ffn, w_g, w_u, w_d],
                     [(D_MODEL, F32), (FFN, F32), (FFN, F32), (D_MODEL, MM)], [], VMEM_LIMIT,
                     scratch=[pltpu.VMEM((tm, FFN), MM)])


def _ple_loss(x2, p, tgt, g_pg, g_post, w_pg, w_pp, T, tm):
    def body(i, x2_ref, p_ref, t_ref, gpg_ref, gpo_ref, wpg_ref, wpp_ref,
             dx2_ref, loss_ref, dgpo_ref, dgpg_ref, dwpg_ref, dwpp_ref, u_s, n3_s, z_s, dz_s, du_s, dy_s, dn3_s):
        @pl.when(i == 0)
        def _():
            for ref in (loss_ref, dgpo_ref, dgpg_ref, dwpg_ref, dwpp_ref):
                ref[...] = jnp.zeros_like(ref)

        gpg, gpo = gpg_ref[...], gpo_ref[...]
        p_mm = p_ref[...].astype(MM)
        for d in range(N_DEV):
            u_s[:, d * HEAD_PAD:(d + 1) * HEAD_PAD] = jnp.dot(p_mm, wpp_ref[d], preferred_element_type=F32)

        def gate_input(rows):
            n3_s[rows, :] = (_rms(x2_ref[rows, :])[0] * gpg).astype(MM)

        _by_chunks(tm, gate_input)
        z_s[...] = jnp.dot(n3_s[...], wpg_ref[...], preferred_element_type=F32)

        def loss_and_back(rows):
            uh, ru = _rms(u_s[rows, :])
            e = uh * gpo
            gate = _sigmoid(z_s[rows, :])
            diff = x2_ref[rows, :] + gate * e - t_ref[rows, :]
            dy = diff * (1.0 / D_MODEL)
            de = dy * gate
            dz_s[rows, :] = (dy * e * gate * (1.0 - gate)).astype(MM)
            du_s[rows, :] = _rms_bwd(de * gpo, uh, ru).astype(MM)
            dy_s[rows, :] = dy
            loss_ref[...] += _fold8(diff * diff) * (0.5 / D_MODEL)
            dgpo_ref[...] += _fold8(de * uh)

        _by_chunks(tm, loss_and_back)
        dn3_s[...] = _dot_nt(dz_s[...], wpg_ref[...])

        def gate_norm_back(rows):
            x2h, r3 = _rms(x2_ref[rows, :])
            dn3 = dn3_s[rows, :]
            dx2_ref[rows, :] = dy_s[rows, :] + _rms_bwd(dn3 * gpg, x2h, r3)
            dgpg_ref[...] += _fold8(dn3 * x2h)

        _by_chunks(tm, gate_norm_back)
        dwpg_ref[...] += _dot_tn(n3_s[...], dz_s[...])
        for d in range(N_DEV):
            dwpp_ref[d] += _dot_tn(p_mm, du_s[:, d * HEAD_PAD:(d + 1) * HEAD_PAD])

    vec = ((8, D_MODEL), F32)
    wide = lambda dt: pltpu.VMEM((tm, D_MODEL), dt)
    return _row_call("ple_loss", body, T, tm, [x2, p, tgt], [g_pg, g_post, w_pg, w_pp], [(D_MODEL, F32)],
                     [vec, vec, vec, ((D_MODEL, D_MODEL), F32), ((N_DEV, PLE, HEAD_PAD), F32)], VMEM_LIMIT,
                     scratch=[wide(F32), wide(MM), wide(F32), wide(MM), wide(MM), wide(F32), wide(F32)])


def _ffn_bwd(dx2, x1, gt, up, g_ffn, w_g, w_u, w_d, T, tm):
    def body(i, dx2_ref, x1_ref, gt_ref, up_ref, g_ref, wg_ref, wu_ref, wd_ref,
             dx1_ref, a_ref, dgt_ref, dup_ref, dg_ref, da_s, dh2_s):
        @pl.when(i == 0)
        def _():
            dg_ref[...] = jnp.zeros_like(dg_ref)

        g = g_ref[...]
        da_s[...] = _dot_nt(dx2_ref[...], wd_ref[...])

        def act_back(rows):
            for cs in FFN_HALVES:
                up, da = up_ref[rows, cs], da_s[rows, cs]
                silu, dsilu = _silu_parts(gt_ref[rows, cs])
                dgt_ref[rows, cs] = (da * up * dsilu).astype(MM)
                dup_ref[rows, cs] = (da * silu).astype(MM)
                a_ref[rows, cs] = (silu * up).astype(MM)

        _by_chunks(tm, act_back)
        dh2_s[...] = (jnp.dot(dgt_ref[...], wg_ref[...], preferred_element_type=F32)
                      + jnp.dot(dup_ref[...], wu_ref[...], preferred_element_type=F32))

        def norm_back(rows):
            x1h, r = _rms(x1_ref[rows, :])
            dh2 = dh2_s[rows, :]
            dx1_ref[rows, :] = dx2_ref[rows, :] + _rms_bwd(dh2 * g, x1h, r)
            dg_ref[...] += _fold8(dh2 * x1h)

        _by_chunks(tm, norm_back)

    return _row_call("ffn_bwd", body, T, tm, [dx2, x1, gt, up], [g_ffn, w_g, w_u, w_d],
                     [(D_MODEL, F32), (FFN, MM), (FFN, MM), (FFN, MM)], [((8, D_MODEL), F32)], VMEM_LIMIT,
                     scratch=[pltpu.VMEM((tm, FFN), F32), pltpu.VMEM((tm, D_MODEL), F32)])


def _merge_bwd(dx1, ya, yb, bg, o, hg, attn, m, rec, g_out, w_bra, w_brb, w_out, T, tm, xchg=((), ())):
    def body(i, dx1_ref, ya_ref, yb_ref, bg_ref, o_ref, hg_ref, attn_ref, m_ref, rec_ref, g_ref, wa_ref, wb_ref, wo_ref,
             dattn_ref, do_ref, dhg_ref, dbg_ref, dg_ref, dwo_ref, dwa_ref, dwb_ref, dm_s, dya_s, dyb_s, drec_s):
        @pl.when(i == 0)
        def _():
            for ref in (dg_ref, dwo_ref, dwa_ref, dwb_ref):
                ref[...] = jnp.zeros_like(ref)

        g = g_ref[...]
        dx1 = dx1_ref[...].astype(MM)
        dm_s[...] = _dot_nt(dx1, wo_ref[...])

        def gate_back(rows):
            dm = dm_s[rows, :]
            ga, gb = _sigmoid(bg_ref[rows, :D_MODEL]), _sigmoid(bg_ref[rows, D_MODEL:])
            dya_s[rows, :] = (dm * ga).astype(MM)
            dyb_s[rows, :] = (dm * gb).astype(MM)
            dbg_ref[rows, :D_MODEL] = (dm * ya_ref[rows, :] * ga * (1.0 - ga)).astype(MM)
            dbg_ref[rows, D_MODEL:] = (dm * yb_ref[rows, :] * gb * (1.0 - gb)).astype(MM)

        _by_chunks(tm, gate_back)
        dwo_ref[...] += _dot_tn(m_ref[...], dx1)
        attn_mm = attn_ref[...].astype(MM)
        for d in range(N_DEV):
            ds = slice(d * HEAD_PAD, (d + 1) * HEAD_PAD)
            dwa_ref[d] += _dot_tn(attn_mm, dya_s[:, ds])
            dwb_ref[d] += _dot_tn(rec_ref[...], dyb_s[:, ds])
        dattn_ref[...] = _dot_nt(dya_s[...], wa_ref[...])
        drec_s[...] = _dot_nt(dyb_s[...], wb_ref[...])

        def recurrent_out_back(rows):
            for h in range(HG_HEADS):
                ls = slice(h * HG_DIM, (h + 1) * HG_DIM)
                oh, r = _rms(o_ref[rows, ls])
                silu, dsilu = _silu_parts(hg_ref[rows, ls])
                dr = drec_s[rows, ls]
                dhg_ref[rows, ls] = (dr * oh * g * dsilu).astype(MM)
                don = dr * silu
                dg_ref[...] += _fold8(don * oh)
                do_ref[rows, ls] = _rms_bwd(don * g, oh, r)

        _by_chunks(tm, recurrent_out_back)

    wide = lambda n, dt: pltpu.VMEM((tm, n), dt)
    return _row_call("merge_bwd", body, T, tm, [dx1, ya, yb, bg, o, hg, attn, m, rec], [g_out, w_bra, w_brb, w_out],
                     [(D_MODEL, F32), (HG_W, F32), (HG_W, MM), (2 * D_MODEL, MM)],
                     [((8, HG_DIM), F32), ((D_MODEL, D_MODEL), F32), ((N_DEV, MLA_HEADS * HEAD_PAD, HEAD_PAD), F32),
                      ((N_DEV, HG_W, HEAD_PAD), F32)], VMEM_LIMIT,
                     scratch=[wide(D_MODEL, F32), wide(D_MODEL, MM), wide(D_MODEL, MM), wide(HG_W, F32)], xchg=xchg)


def _flash_bwd(qf, kf, vf, o, do, lse, T, xchg=((), ())):
    tq = min(ATT_TILE, T)
    nq = T // tq

    qi_tab, ki_tab = _causal_pairs(nq, by_query=False)

    n_x, n_sib = _x_count(xchg), len(xchg[0])
    hp = ATT_HEADS
    n_heads, n_pairs = MLA_HEADS // hp, len(qi_tab)

    def body(qi_ref, ki_ref, q_ref, k_ref, v_ref, o_ref, do_ref, lse_ref, *rest):
        x_in, (dq_ref, dk_ref, dv_ref), rest = rest[:n_x], rest[n_x:n_x + 3], rest[n_x + 3:]
        x_out, x_sems = rest[:n_x], rest[n_x:]
        t = pl.program_id(1)
        qi, ki = qi_ref[t], ki_ref[t]
        if n_x:
            @pl.when((pl.program_id(0) == 0) & (t == 0))
            def _():
                for cp in _x_copies(n_sib, x_in, x_out, x_sems):
                    cp.start()

        @pl.when(t == 0)
        def _():
            dq_ref[...] = jnp.zeros_like(dq_ref)

        def step(first):
            halves = 2 if first and tq % (2 * HEAD_PAD) == 0 else 1
            w = tq // halves
            for hh in range(hp):
                hs = slice(hh * HEAD_PAD, (hh + 1) * HEAD_PAD)
                for part in range(halves):
                    keys, qs = slice(part * w, (part + 1) * w), slice(part * w, tq)
                    nq_ = tq - part * w
                    q, k, d_o = q_ref[qs, hs], k_ref[keys, hs], do_ref[qs, hs]
                    s = _dot_nt(q, k)
                    if first:
                        row = lax.broadcasted_iota(jnp.int32, (nq_, w), 0)
                        col = lax.broadcasted_iota(jnp.int32, (nq_, w), 1)
                        s = jnp.where(col <= row, s, NEG)
                    p = jnp.exp(s - lse_ref[qs, hh * HEAD_PAD:hh * HEAD_PAD + 1])
                    delta = jnp.sum(d_o * o_ref[qs, hs], axis=1, keepdims=True)
                    ds = p * (_dot_nt(d_o, v_ref[keys, hs]) - delta)
                    rows = pl.ds(pl.multiple_of(qi * tq + part * w, w), nq_)
                    dq_ref[rows, hs] += _dot(ds, k)
                    if first:
                        dv_ref[keys, hs] = _dot_tn(p, d_o)
                        dk_ref[keys, hs] = _dot_tn(ds, q)
                    else:
                        dv_ref[keys, hs] += _dot_tn(p, d_o)
                        dk_ref[keys, hs] += _dot_tn(ds, q)

        @pl.when(qi == ki)
        def _():
            step(True)

        @pl.when(qi > ki)
        def _():
            step(False)

        if n_x:
            @pl.when((pl.program_id(0) == n_heads - 1) & (t == n_pairs - 1))
            def _():
                for cp in _x_copies(n_sib, x_in, x_out, x_sems):
                    cp.wait()

    q_spec = pl.BlockSpec((tq, hp * HEAD_PAD), lambda h, t, qi_ref, ki_ref: (qi_ref[t], h))
    kv_spec = pl.BlockSpec((tq, hp * HEAD_PAD), lambda h, t, qi_ref, ki_ref: (ki_ref[t], h))
    any_spec = pl.BlockSpec(memory_space=pl.ANY)
    w = MLA_HEADS * HEAD_PAD
    grid_spec = pltpu.PrefetchScalarGridSpec(
        num_scalar_prefetch=2, grid=(n_heads, n_pairs),
        in_specs=[q_spec, kv_spec, kv_spec, q_spec, q_spec, q_spec] + [any_spec] * n_x,
        out_specs=[pl.BlockSpec((T, hp * HEAD_PAD), lambda h, t, qi_ref, ki_ref: (0, h)), kv_spec, kv_spec]
        + [any_spec] * n_x,
        scratch_shapes=_x_sems(xchg))
    return pl.pallas_call(
        body, name="flash_bwd", grid_spec=grid_spec,
        out_shape=[jax.ShapeDtypeStruct((T, w), F32)] * 3 + _x_out_shapes(xchg),
        compiler_params=_cparams(("arbitrary", "arbitrary")),
    )(jnp.asarray(qi_tab), jnp.asarray(ki_tab), qf, kf, vf, o, do, lse, *xchg[0], *xchg[1])


def _mla_heads_bwd(d_out, saved, g_pad, cos_t, sin_t, first):
    d_raw, dg = [], jnp.zeros((1, HEAD_PAD), F32)
    for h in range(MLA_HEADS):
        xh, r = saved[h]
        dy = d_out[:, h * HEAD_PAD:(h + 1) * HEAD_PAD]
        dn = dy * cos_t + _rope_swap(dy * sin_t, first)
        dg = dg + jnp.sum(dn * xh, axis=0, keepdims=True)
        d_raw.append(_rms_bwd(dn * g_pad, xh, r, QK_DIM))
    return d_raw, dg


def _mla_prep_bwd(cq, ckv, kr, pos, dqf, dkf, dvf, g_qa, g_kva, g_qn, g_kn, w_uq, w_ukv, T, tm):
    def body(i, cq_ref, ckv_ref, kr_ref, pos_ref, dq_ref, dk_ref, dv_ref,
             gqa_ref, gkva_ref, gqn_ref, gkn_ref, wuq_ref, wukv_ref,
             dcq_ref, dckv_ref, dkr_ref, dgqa_ref, dgkva_ref, dgqn_ref, dgkn_ref, dwuq_ref, dwukv_ref):
        cos_t, sin_t, first = _rope_tables(pos_ref[...], tm)
        cqh, rq = _rms(cq_ref[...])
        ckvh, rkv = _rms(ckv_ref[...])
        cqn, ckvn = cqh * gqa_ref[...], ckvh * gkva_ref[...]
        q_raw, k_raw, _ = _mla_raw_heads(cqn, ckvn, kr_ref[...], wuq_ref, wukv_ref, tm)
        _, q_saved = _mla_heads_fwd(q_raw, gqn_ref[...], cos_t, sin_t, first)
        _, k_saved = _mla_heads_fwd(k_raw, gkn_ref[...], cos_t, sin_t, first)
        dq_heads, dgqn = _mla_heads_bwd(dq_ref[...] * ATT_SCALE, q_saved, gqn_ref[...], cos_t, sin_t, first)
        dk_heads, dgkn = _mla_heads_bwd(dk_ref[...], k_saved, gkn_ref[...], cos_t, sin_t, first)
        lane = lax.broadcasted_iota(jnp.int32, (tm, HEAD_PAD), 1)
        nope = lane < QK_NOPE
        dcqn = jnp.zeros((tm, Q_RANK), F32)
        dckvn = jnp.zeros((tm, KV_RANK), F32)
        dkr = jnp.zeros((tm, HEAD_PAD), F32)
        cqn_mm, ckvn_mm = cqn.astype(MM), ckvn.astype(MM)
        for h in range(MLA_HEADS):
            hs = slice(h * HEAD_PAD, (h + 1) * HEAD_PAD)
            dq_h = dq_heads[h].astype(MM)
            dkv_h = jnp.where(nope, dk_heads[h], pltpu.roll(dv_ref[:, hs], V_DIM, 1)).astype(MM)
            _acc(dwuq_ref.at[h], i, _dot_tn(dq_h, cqn_mm))
            _acc(dwukv_ref.at[h], i, _dot_tn(ckvn_mm, dkv_h))
            dcqn = dcqn + jnp.dot(dq_h, wuq_ref[h], preferred_element_type=F32)
            dckvn = dckvn + lax.dot_general(dkv_h, wukv_ref[h], (((1,), (1,)), ((), ())), preferred_element_type=F32)
            dkr = dkr + dk_heads[h]
        dkr_ref[...] = jnp.where((lane >= QK_NOPE) & (lane < QK_DIM), dkr, 0.0).astype(MM)
        dcq_ref[...] = _rms_bwd(dcqn * gqa_ref[...], cqh, rq).astype(MM)
        dckv_ref[...] = _rms_bwd(dckvn * gkva_ref[...], ckvh, rkv).astype(MM)
        _acc(dgqa_ref, i, jnp.sum(dcqn * cqh, axis=0, keepdims=True))
        _acc(dgkva_ref, i, jnp.sum(dckvn * ckvh, axis=0, keepdims=True))
        _acc(dgqn_ref, i, dgqn)
        _acc(dgkn_ref, i, dgkn)

    return _row_call(
        "mla_prep_bwd", body, T, tm, [cq, ckv, kr, pos, dqf, dkf, dvf], [g_qa, g_kva, g_qn, g_kn, w_uq, w_ukv],
        [(Q_RANK, MM), (KV_RANK, MM), (HEAD_PAD, MM)],
        [((1, Q_RANK), F32), ((1, KV_RANK), F32), ((1, HEAD_PAD), F32), ((1, HEAD_PAD), F32),
         ((MLA_HEADS, HEAD_PAD, Q_RANK), F32), ((MLA_HEADS, KV_RANK, HEAD_PAD), F32)], VMEM_LIMIT)


def _in_proj_bwd(x, dx1, dsecs, g_mix, w_in, T, tm):
    def body(i, x_ref, dx1_ref, *rest):
        d_refs, (g_ref, w_ref, dx_ref, dp_ref, dg_ref, dh_s) = rest[:len(COL_SECTIONS)], rest[len(COL_SECTIONS):]

        @pl.when(i == 0)
        def _():
            dg_ref[...] = jnp.zeros_like(dg_ref)

        g = g_ref[...]

        def join_and_cut(rows):
            pieces = [(d_ref[rows, QK_NOPE:QK_DIM] if n == QK_ROPE else d_ref[rows, :]).astype(F32)
                      for (_, n), d_ref in zip(COL_SECTIONS, d_refs)]
            dproj = jnp.concatenate(pieces, axis=1)
            for d in range(N_DEV):
                dp_ref[d, rows, :] = dproj[:, d * IN_BLOCK:(d + 1) * IN_BLOCK].astype(MM)

        _by_chunks(tm, join_and_cut)
        dh = jnp.dot(dp_ref[0], w_ref[0], preferred_element_type=F32)
        for d in range(1, N_DEV):
            dh = dh + jnp.dot(dp_ref[d], w_ref[d], preferred_element_type=F32)
        dh_s[...] = dh

        def norm_back(rows):
            xh, r = _rms(x_ref[rows, :])
            dh_c = dh_s[rows, :]
            dx_ref[rows, :] = dx1_ref[rows, :] + _rms_bwd(dh_c * g, xh, r)
            dg_ref[...] += _fold8(dh_c * xh)

        _by_chunks(tm, norm_back)

    in_specs = [pl.BlockSpec((tm, a.shape[1]), lambda i: (i, 0)) for a in [x, dx1, *dsecs]]
    in_specs += [pl.BlockSpec(g_mix.shape, lambda i: (0, 0)),
                 pl.BlockSpec(w_in.shape, lambda i: (0, 0, 0), pipeline_mode=pl.Buffered(1))]

    def kern(*refs):
        body(pl.program_id(0), *refs)

    return pl.pallas_call(
        kern, name="in_proj_bwd", grid=(T // tm,), in_specs=in_specs,
        out_specs=[pl.BlockSpec((tm, D_MODEL), lambda i: (i, 0)),
                   pl.BlockSpec((N_DEV, tm, IN_BLOCK), lambda i: (0, i, 0)),
                   pl.BlockSpec((8, D_MODEL), lambda i: (0, 0))],
        out_shape=[jax.ShapeDtypeStruct((T, D_MODEL), F32), jax.ShapeDtypeStruct((N_DEV, T, IN_BLOCK), MM),
                   jax.ShapeDtypeStruct((8, D_MODEL), F32)],
        scratch_shapes=[pltpu.VMEM((tm, D_MODEL), F32)],
        compiler_params=_cparams(("arbitrary",), VMEM_LIMIT),
    )(x, dx1, *dsecs, g_mix, w_in)


def _pick_block(n, cap):
    best = None
    for cand in range(128, min(n, cap) + 1, 128):
        if n % cand == 0:
            best = cand
    return n if best is None else best


def _matmul_tn(name, a, b):
    T, M = a.shape
    N = b.shape[1]
    bm, bk = _pick_block(M, 1408), min(DW_TOKENS, T)
    bn = _pick_block(N, 2560)

    def body(a_ref, b_ref, c_ref):
        @pl.when(pl.program_id(2) == 0)
        def _():
            c_ref[...] = jnp.zeros_like(c_ref)

        c_ref[...] += _dot_tn(a_ref[...], b_ref[...])

    return pl.pallas_call(
        body, name=name, grid=(M // bm, N // bn, T // bk),
        in_specs=[pl.BlockSpec((bk, bm), lambda i, j, k: (k, i)), pl.BlockSpec((bk, bn), lambda i, j, k: (k, j))],
        out_specs=pl.BlockSpec((bm, bn), lambda i, j, k: (i, j)), out_shape=jax.ShapeDtypeStruct((M, N), F32),
        compiler_params=_cparams(("parallel", "parallel", "arbitrary"), VMEM_LIMIT),
    )(a, b)


def _matmul_tn_blocks(name, a, b):
    T, M = a.shape
    nd, _, c = b.shape
    bm, bk = _pick_block(M, 512), min(DW_TOKENS, T)

    def body(a_ref, b_ref, c_ref):
        @pl.when(pl.program_id(1) == 0)
        def _():
            c_ref[...] = jnp.zeros_like(c_ref)

        a_blk = a_ref[...].astype(MM)
        for d in range(nd):
            c_ref[d] += _dot_tn(b_ref[d], a_blk)

    return pl.pallas_call(
        body, name=name, grid=(M // bm, T // bk),
        in_specs=[pl.BlockSpec((bk, bm), lambda i, k: (k, i)), pl.BlockSpec((nd, bk, c), lambda i, k: (0, k, 0))],
        out_specs=pl.BlockSpec((nd, c, bm), lambda i, k: (0, 0, i)),
        out_shape=jax.ShapeDtypeStruct((nd, c, M), F32),
        compiler_params=_cparams(("parallel", "arbitrary"), VMEM_LIMIT),
    )(a, b)


def _pad_gain(g, n):
    return jnp.pad(g.reshape(1, -1), ((0, 0), (0, n - g.shape[-1])))


GROUP_A = ("w_ffn_gate", "w_ffn_up", "w_ffn_down", "w_ple_gate", "w_ple_proj")
GROUP_B = ("w_branch", "w_out")
GROUP_C = ("w_in", "w_uq", "w_ukv")
EARLY = GROUP_C
LATE = GROUP_B + GROUP_A
TRANSPOSED = ("w_in", "w_uq", "w_ffn_gate", "w_ffn_up")


def _local_step(x, p, pos, tgt, small, big, late_blocks=None, core=None):
    T = x.shape[0]
    tm = min(ROW_TILE, T)
    tw = min(WIDE_TILE, T)
    w_in = big["w_in"]
    w_uq = jnp.pad(big["w_uq"], ((0, 0), (0, HEAD_PAD - QK_DIM), (0, 0)))
    w_ukv = big["w_ukv"]

    g_mix, g_qa, g_kva = small["mix_norm_g"], small["q_a_norm_g"], small["kv_a_norm_g"]
    g_qn, g_kn = _pad_gain(small["q_norm_g"], HEAD_PAD), _pad_gain(small["k_norm_g"], HEAD_PAD)
    g_out, g_ffn = small["hg_out_norm_g"], small["ffn_norm_g"]
    g_pg, g_post = small["ple_gate_norm_g"], small["ple_post_norm_g"]
    logits = small["hg_lb_logits"]
    lb = _lower_bound(logits)

    h, cq, ckv, kr, hq, hf, hi, hg, bg = _in_proj_fwd(x, g_mix, w_in, T, tw)
    qf, kf, vf = _mla_prep_fwd(cq, ckv, kr, pos, g_qa, g_kva, g_qn, g_kn, w_uq, w_ukv, T, tw)
    if late_blocks is None:
        attn, lse = _flash_fwd(qf, kf, vf, T)
    else:
        attn, lse, *late = _flash_fwd(qf, kf, vf, T, ag_blocks=[late_blocks[n] for n in LATE])
        big = {**big, **dict(zip(LATE, late))}
    o, s0 = _hgrn_fwd(hq, hf, hi, lb, T)
    w_branch = jnp.moveaxis(big["w_branch"].reshape(N_DEV, 2, HG_W, HEAD_PAD), 0, 2).reshape(2, HG_W, D_MODEL)
    w_bra = jnp.pad(w_branch[0].reshape(MLA_HEADS, V_DIM, D_MODEL),
                    ((0, 0), (0, HEAD_PAD - V_DIM), (0, 0))).reshape(MLA_HEADS * HEAD_PAD, D_MODEL)
    w_brb = w_branch[1]
    w_out = big["w_out"].reshape(D_MODEL, D_MODEL)
    w_g, w_u = big["w_ffn_gate"].reshape(FFN, D_MODEL), big["w_ffn_up"].reshape(FFN, D_MODEL)
    w_d = big["w_ffn_down"].reshape(FFN, D_MODEL)
    w_pg, w_pp = big["w_ple_gate"].reshape(D_MODEL, D_MODEL), big["w_ple_proj"]
    x1, ya, yb, m, rec = _merge_fwd(attn, o, hg, bg, x, g_out, w_bra, w_brb, w_out, T, tw)
    x2, gt, up, h2 = _ffn_fwd(x1, g_ffn, w_g, w_u, w_d, T, tw)
    dx2, loss_p, dg_post, dg_pg, d_pg, d_pp = _ple_loss(x2, p, tgt, g_pg, g_post, w_pg, w_pp, T, tw)

    grads, sibs, gots = {}, {}, {}
    dist = core is not None
    pick = lambda names: [grads[n] for n in names] if dist else ()

    def partials(tag, names, got):
        if not dist:
            return ()
        sibs.update(zip(names, got))
        return _chip_partials("rs_partial_" + tag, pick(names), got, core)

    dx1, a, dgt, dup, dg_ffn = _ffn_bwd(dx2, x1, gt, up, g_ffn, w_g, w_u, w_d, T, tm)
    grads["w_ffn_gate"] = _matmul_tn("dw_gate", dgt, h2).reshape(N_DEV, -1, D_MODEL)
    grads["w_ffn_up"] = _matmul_tn("dw_up", dup, h2).reshape(N_DEV, -1, D_MODEL)
    grads["w_ffn_down"] = _matmul_tn("dw_down", a, dx2).reshape(N_DEV, -1, D_MODEL)
    grads["w_ple_gate"] = d_pg.reshape(N_DEV, -1, D_MODEL)
    grads["w_ple_proj"] = d_pp

    dattn, do, dhg, dbg, dg_out, d_out, d_bra, d_brb, *sib_a = _merge_bwd(
        dx1, ya, yb, bg, o, hg, attn, m, rec, g_out, w_bra, w_brb, w_out, T, tm, xchg=(pick(GROUP_A), ()))
    parts_a = partials("a", GROUP_A, sib_a)
    d_bra = d_bra.reshape(N_DEV, MLA_HEADS, HEAD_PAD, HEAD_PAD)[:, :, :V_DIM].reshape(N_DEV, HG_W, HEAD_PAD)
    grads["w_branch"] = jnp.concatenate([d_bra, d_brb], axis=1)
    grads["w_out"] = d_out.reshape(N_DEV, -1, D_MODEL)

    dhq, dhf, dhi, dlb, *got = _hgrn_bwd(hq, hf, hi, do, s0, lb, T, xchg=(pick(GROUP_B), parts_a))
    sib_b, got_a = got[:len(GROUP_B)], got[len(GROUP_B):]
    parts_b = partials("b", GROUP_B, sib_b)
    dqf, dkf, dvf, *got_b = _flash_bwd(qf, kf, vf, attn, dattn, lse, T, xchg=((), parts_b))
    (dcq, dckv, dkr, dg_qa, dg_kva, dg_qn, dg_kn, d_uq, d_ukv) = _mla_prep_bwd(
        cq, ckv, kr, pos, dqf, dkf, dvf, g_qa, g_kva, g_qn, g_kn, w_uq, w_ukv, T, tw)
    grad_x, dproj, dg_mix = _in_proj_bwd(x, dx1, [dcq, dckv, dkr, dhq, dhf, dhi, dhg, dbg], g_mix, w_in, T, tw)
    grads["w_in"] = _matmul_tn_blocks("dw_in", h, dproj)
    grads["w_uq"] = d_uq[:, :QK_DIM]
    grads["w_ukv"] = d_ukv
    parts_c = ()
    if dist:
        parts_c = partials("c", GROUP_C, _exchange_sibling("rs_sibling_c", pick(GROUP_C)))
        gots.update(zip(GROUP_A, got_a))
        gots.update(zip(GROUP_B, got_b))

    dl0 = dlb * lb * (1.0 - lb)
    small_g = {
        "mix_norm_g": dg_mix, "q_a_norm_g": dg_qa, "kv_a_norm_g": dg_kva, "q_norm_g": dg_qn, "k_norm_g": dg_kn,
        "hg_lb_logits": jnp.concatenate([dl0, -dl0], axis=0), "hg_out_norm_g": dg_out,
        "ffn_norm_g": dg_ffn, "ple_gate_norm_g": dg_pg, "ple_post_norm_g": dg_post,
    }
    return loss_p, grad_x, small_g, grads, sibs, gots, parts_c


def _lower_bound(logits):
    def body(l_ref, lb_ref):
        l = l_ref[...]
        mx = jnp.max(l, axis=0, keepdims=True)
        e = jnp.exp(l - mx)
        lb_ref[...] = e[0:1] / jnp.sum(e, axis=0, keepdims=True)

    return pl.pallas_call(body, name="lower_bound", out_shape=jax.ShapeDtypeStruct((1, HG_W), F32))(logits)


def _my_place():
    return lax.axis_index("x"), lax.axis_index("y"), lax.axis_index("c")


def _all_gather(name, blocks):
    n = len(blocks)

    def body(*refs):
        x_refs, out_refs, sems = refs[:n], refs[n:2 * n], refs[2 * n:]
        _ag_start(x_refs, out_refs, sems)
        _ag_finish(x_refs, out_refs, sems)

    any_spec = pl.BlockSpec(memory_space=pl.ANY)
    return pl.pallas_call(
        body, name=name, out_shape=_ag_out_shapes(blocks),
        in_specs=[any_spec] * n, out_specs=[any_spec] * n, scratch_shapes=_ag_sems(n),
    )(*blocks)


def _ag_out_shapes(blocks):
    return [jax.ShapeDtypeStruct((N_DEV,) + b.shape, b.dtype) for b in blocks]


def _ag_sems(n):
    return [pltpu.SemaphoreType.DMA((7 * n,)), pltpu.SemaphoreType.DMA((7 * n,)), pltpu.SemaphoreType.DMA((n,))]


def _ag_parts(x_refs, out_refs, sems):
    send_sems, recv_sems, local_sems = sems
    x, y, c = _my_place()
    me, sibling = (x, y, c), (x, y, 1 - c)
    chips = [(1 - x, y), (x, 1 - y), (1 - x, 1 - y)]
    n = len(x_refs)

    def copy(a, k, block, to, own=False):
        px, py, pc = block
        dst = out_refs[a].at[4 * px + 2 * py + pc]
        return pltpu.make_async_remote_copy(
            src_ref=x_refs[a] if own else dst, dst_ref=dst, send_sem=send_sems.at[7 * a + k],
            recv_sem=recv_sems.at[7 * a + k], device_id=to, device_id_type=MESH_ID)

    mine = [pltpu.make_async_copy(x_refs[a], out_refs[a].at[4 * x + 2 * y + c], local_sems.at[a]) for a in range(n)]
    first = []
    for a in range(n):
        first.append(copy(a, 0, me, sibling, own=True))
        first += [copy(a, 1 + j, me, (*chip, c), own=True) for j, chip in enumerate(chips)]
    return copy, mine, first, me, sibling, chips, c, n


def _ag_start(x_refs, out_refs, sems):
    _, mine, first, *_ = _ag_parts(x_refs, out_refs, sems)
    for cp in mine + first:
        cp.start()


def _ag_finish(x_refs, out_refs, sems):
    copy, mine, first, me, sibling, chips, c, n = _ag_parts(x_refs, out_refs, sems)
    passed = []
    for j, chip in enumerate(chips):
        for a in range(n):
            copy(a, 1 + j, (*chip, c), me).wait_recv()
            passed.append(copy(a, 4 + j, (*chip, c), sibling))
            passed[-1].start()
    for a in range(n):
        copy(a, 0, sibling, me).wait_recv()
    for j, chip in enumerate(chips):
        for a in range(n):
            copy(a, 4 + j, (*chip, 1 - c), me).wait_recv()
    for cp in first + passed:
        cp.wait_send()
    for cp in mine:
        cp.wait()


def _exchange_sibling(name, gs):
    return _exchange(name, (gs, ()))


def _exchange(name, xchg, gather=()):
    n, n_ag = _x_count(xchg), len(gather)

    def body(*refs):
        in_refs, ag_in, refs = refs[:n], refs[n:n + n_ag], refs[n + n_ag:]
        out_refs, ag_out, refs = refs[:n], refs[n:n + n_ag], refs[n + n_ag:]
        sems, ag_sems = refs[:len(refs) - 3 * bool(n_ag)], refs[len(refs) - 3 * bool(n_ag):]
        if n_ag:
            _ag_start(ag_in, ag_out, ag_sems)
        for cp in _x_copies(len(xchg[0]), in_refs, out_refs, sems):
            cp.start()
        for cp in _x_copies(len(xchg[0]), in_refs, out_refs, sems):
            cp.wait()
        if n_ag:
            _ag_finish(ag_in, ag_out, ag_sems)

    any_spec = pl.BlockSpec(memory_space=pl.ANY)
    return pl.pallas_call(
        body, name=name, out_shape=_x_out_shapes(xchg) + _ag_out_shapes(gather),
        in_specs=[any_spec] * (n + n_ag), out_specs=[any_spec] * (n + n_ag),
        scratch_shapes=_x_sems(xchg) + (_ag_sems(n_ag) if n_ag else []),
    )(*xchg[0], *xchg[1], *gather)


N_PARTS = 4


def _part_spec(rows, cols, t_pos, lead_block=(), lead_index=lambda *args: ()):
    if rows % (16 * N_PARTS) == 0:
        axis, shape, count = 0, (rows // N_PARTS, cols), N_PARTS
    elif cols % (128 * N_PARTS) == 0:
        axis, shape, count = 1, (rows, cols // N_PARTS), N_PARTS
    else:
        axis, shape, count = 0, (rows, cols), 1

    def index(*args):
        i = jnp.minimum(args[t_pos], count - 1)
        return (*lead_index(*args), *((i, 0) if axis == 0 else (0, i)))

    return pl.BlockSpec((*lead_block, *shape), index)


def _chip_partials(name, gs, sibs, c_idx):
    n = len(gs)

    def body(c_ref, *refs):
        for g_ref, sib_ref, out_ref in zip(refs[:n], refs[n:2 * n], refs[2 * n:]):
            out_ref[...] = (g_ref[...] + sib_ref[...]).astype(MM)

    own = [_part_spec(*g.shape[1:], 1, (1,), lambda j, t, c_ref: (2 * j + c_ref[0],)) for g in gs]
    by_chip = [_part_spec(*g.shape[1:], 1, (1,), lambda j, t, c_ref: (j,)) for g in gs]
    grid_spec = pltpu.PrefetchScalarGridSpec(
        num_scalar_prefetch=1, grid=(4, N_PARTS), in_specs=own + by_chip, out_specs=by_chip)
    return pl.pallas_call(
        body, name=name, grid_spec=grid_spec, out_shape=[jax.ShapeDtypeStruct((4,) + g.shape[1:], MM) for g in gs],
        compiler_params=_cparams(("arbitrary", "arbitrary"), VMEM_LIMIT),
    )(c_idx, *gs, *sibs)


def _x_count(xchg):
    return len(xchg[0]) + len(xchg[1])


def _x_out_shapes(xchg):
    return ([jax.ShapeDtypeStruct((4,) + g.shape[1:], g.dtype) for g in xchg[0]]
            + [jax.ShapeDtypeStruct((3,) + p.shape[1:], p.dtype) for p in xchg[1]])


def _x_sems(xchg):
    n = 4 * len(xchg[0]) + 3 * len(xchg[1])
    return [pltpu.SemaphoreType.DMA((n,)), pltpu.SemaphoreType.DMA((n,))] if n else []


def _x_copies(n_sib, in_refs, out_refs, sems):
    if not in_refs:
        return []
    send_sems, recv_sems = sems
    x, y, c = _my_place()
    chips = [(1 - x, y), (x, 1 - y), (1 - x, 1 - y)]
    copies = []

    def add(src, dst, to):
        k = len(copies)
        copies.append(pltpu.make_async_remote_copy(
            src_ref=src, dst_ref=dst, send_sem=send_sems.at[k], recv_sem=recv_sems.at[k], device_id=to,
            device_id_type=MESH_ID))

    for a, (src, dst) in enumerate(zip(in_refs, out_refs)):
        if a < n_sib:
            for j in range(4):
                add(src.at[2 * j + 1 - c], dst.at[j], (x, y, 1 - c))
        else:
            for k, (px, py) in enumerate(chips):
                add(src.at[2 * px + py], dst.at[k], (px, py, c))
    return copies


def _adamw_math(w, g, m, v):
    m = ADAM_B1 * m + (1.0 - ADAM_B1) * g
    v = ADAM_B2 * v + (1.0 - ADAM_B2) * jnp.square(g)
    m_hat = m / (1.0 - ADAM_B1 ** ADAM_STEP)
    v_hat = v / (1.0 - ADAM_B2 ** ADAM_STEP)
    delta = -ADAM_LR * (m_hat / (jnp.sqrt(v_hat) + ADAM_EPS) + ADAM_WD * w)
    return delta, m, v


def _sum_adamws(name, gs, sibs, gots, ws, ms, vs, slot_idx, chip_idx):
    n = len(gs)

    def body(s_ref, j_ref, *refs):
        ins, outs = refs[:6 * n], refs[6 * n:]
        for a in range(n):
            g_ref, sib_ref, got_ref, w_ref, m_ref, v_ref = (ins[k * n + a] for k in range(6))
            go_ref, d_ref, m2_ref, v2_ref = outs[4 * a:4 * a + 4]
            grad = g_ref[0] + sib_ref[0]
            for k in range(3):
                grad = grad + got_ref[k].astype(F32)
            go_ref[...] = grad
            d_ref[...], m2_ref[...], v2_ref[...] = _adamw_math(w_ref[...], grad, m_ref[...], v_ref[...])

    shapes = [g.shape[1:] for g in gs]
    flat = [_part_spec(*s, 0) for s in shapes]
    in_specs = ([_part_spec(*s, 0, (1,), lambda t, s_ref, j_ref: (s_ref[0],)) for s in shapes]
                + [_part_spec(*s, 0, (1,), lambda t, s_ref, j_ref: (j_ref[0],)) for s in shapes]
                + [_part_spec(*s, 0, (3,), lambda t, s_ref, j_ref: (0,)) for s in shapes] + flat * 3)
    grid_spec = pltpu.PrefetchScalarGridSpec(
        num_scalar_prefetch=2, grid=(N_PARTS,), in_specs=in_specs, out_specs=[f for f in flat for _ in range(4)])
    res = pl.pallas_call(
        body, name=name, grid_spec=grid_spec,
        out_shape=[jax.ShapeDtypeStruct(s, F32) for s in shapes for _ in range(4)],
        compiler_params=_cparams(("arbitrary",), VMEM_LIMIT),
    )(slot_idx, chip_idx, *gs, *sibs, *gots, *ws, *ms, *vs)
    return [res[4 * a:4 * a + 4] for a in range(n)]


BIG = ("w_in", "w_uq", "w_ukv", "w_branch", "w_out", "w_ffn_gate", "w_ffn_up", "w_ffn_down", "w_ple_gate", "w_ple_proj")
SMALL = (
    ("mix_norm_g", 0, 1, 1024), ("q_a_norm_g", 1, 1, 384), ("kv_a_norm_g", 2, 1, 256), ("q_norm_g", 3, 1, 96),
    ("k_norm_g", 4, 1, 96), ("hg_lb_logits", 5, 2, 512), ("hg_out_norm_g", 7, 1, 128), ("ffn_norm_g", 8, 1, 1024),
    ("ple_gate_norm_g", 9, 1, 1024), ("ple_post_norm_g", 10, 1, 1024),
)
SLAB_ROWS, LOSS_ROW = 16, 15


def _pack_partials(small_g, loss_p):
    def body(*refs):
        val_refs, loss_ref, out_ref = refs[:len(SMALL)], refs[len(SMALL)], refs[len(SMALL) + 1]
        out_ref[...] = jnp.zeros_like(out_ref)
        for (_, r0, rows, cols), ref in zip(SMALL, val_refs):
            val = ref[...]
            if val.shape[0] != rows:
                val = jnp.sum(val, axis=0, keepdims=True)
            out_ref[r0:r0 + rows, :cols] = val[:, :cols]
        out_ref[LOSS_ROW:LOSS_ROW + 1, :HEAD_PAD] = jnp.full((1, HEAD_PAD), jnp.sum(loss_ref[...]), F32)

    return pl.pallas_call(
        body, name="pack_partials", out_shape=jax.ShapeDtypeStruct((SLAB_ROWS, D_MODEL), F32),
    )(*[small_g[n] for n, *_ in SMALL], loss_p)


def _adamw_small(parts, ws, ms, vs):
    n = len(SMALL)

    def body(p_ref, *refs):
        ins, loss_ref, outs = refs[:3 * n], refs[3 * n], refs[3 * n + 1:]
        total = p_ref[0]
        for d in range(1, N_DEV):
            total = total + p_ref[d]
        loss_ref[...] = total[LOSS_ROW:LOSS_ROW + 1, 0:1]
        for a, (_, r0, rows, cols) in enumerate(SMALL):
            g = total[r0:r0 + rows, :cols]
            outs[4 * a][...] = g
            outs[4 * a + 1][...], outs[4 * a + 2][...], outs[4 * a + 3][...] = _adamw_math(
                ins[a][...], g, ins[n + a][...], ins[2 * n + a][...])

    shapes = [jax.ShapeDtypeStruct((rows, cols), F32) for _, _, rows, cols in SMALL]
    res = pl.pallas_call(
        body, name="adamw_small", out_shape=[jax.ShapeDtypeStruct((1, 1), F32)] + [s for s in shapes for _ in range(4)],
    )(parts, *ws, *ms, *vs)
    return res[0], [res[1 + 4 * a:5 + 4 * a] for a in range(n)]


_WEIGHTS = ["mix_norm_g", "w_in", "q_a_norm_g", "w_uq", "kv_a_norm_g", "w_ukv", "q_norm_g", "k_norm_g", "hg_lb_logits",
            "hg_out_norm_g", "w_branch", "w_out", "ffn_norm_g", "w_ffn_gate", "w_ffn_up", "w_ffn_down",
            "ple_gate_norm_g", "w_ple_gate", "w_ple_proj", "ple_post_norm_g"]


def _step(x, p, positions, tgt, w, m, v):
    small_names = [n for n, *_ in SMALL]
    T = x.shape[1]
    px, py, pc = _my_place()
    as_idx = lambda t: jnp.reshape(t, (1,)).astype(jnp.int32)

    def two_d(n, t):
        t = t.reshape(-1, t.shape[-1])
        return t.T if n in TRANSPOSED else t

    def full_shape(n, t):
        return (t.T if n in TRANSPOSED else t).reshape(w[n].shape)

    blocks = {n: two_d(n, w[n]).astype(MM) for n in BIG}
    big = dict(zip(EARLY, _all_gather("ag_weights", [blocks[n] for n in EARLY])))
    small = {n: (w[n] if n == "hg_lb_logits" else w[n].reshape(1, -1)) for n in small_names}

    loss_p, grad_x, small_g, grads, sibs, gots, parts_c = _local_step(
        x[0], p[0, 0], positions.reshape(T, 1), tgt[0], small, big, late_blocks=blocks, core=as_idx(pc))

    *got_c, slabs = _exchange("rs_chips", ((), parts_c), gather=[_pack_partials(small_g, loss_p)])
    gots.update(zip(GROUP_C, got_c))
    out_g, out_d, out_m, out_v = {}, {}, {}, {}
    for tag, names in (("ab", GROUP_A + GROUP_B), ("c", GROUP_C)):
        pick = lambda table: [table[n] for n in names]
        res = _sum_adamws("adamw_" + tag, pick(grads), pick(sibs), pick(gots), [two_d(n, w[n]) for n in names],
                          [two_d(n, m[n]) for n in names], [two_d(n, v[n]) for n in names],
                          as_idx(4 * px + 2 * py + pc), as_idx(2 * px + py))
        for n, r in zip(names, res):
            out_g[n], out_d[n], out_m[n], out_v[n] = [full_shape(n, t) for t in r]

    loss, res = _adamw_small(slabs, *([t[n] for n in small_names] for t in (w, m, v)))
    for n, r in zip(small_names, res):
        out_g[n], out_d[n], out_m[n], out_v[n] = r

    outs = [loss.reshape(()), grad_x[None]]
    for table in (out_g, out_d, out_m, out_v):
        outs += [table[n] for n in _WEIGHTS]
    return tuple(outs)


def kernel(x, p, positions, mix_norm_g, w_in, q_a_norm_g, w_uq, kv_a_norm_g, w_ukv, q_norm_g, k_norm_g, hg_lb_logits, hg_out_norm_g, w_branch, w_out, ffn_norm_g, w_ffn_gate, w_ffn_up, w_ffn_down, ple_gate_norm_g, w_ple_gate, w_ple_proj, ple_post_norm_g, loss_target, m_mix_norm_g, m_w_in, m_q_a_norm_g, m_w_uq, m_kv_a_norm_g, m_w_ukv, m_q_norm_g, m_k_norm_g, m_hg_lb_logits, m_hg_out_norm_g, m_w_branch, m_w_out, m_ffn_norm_g, m_w_ffn_gate, m_w_ffn_up, m_w_ffn_down, m_ple_gate_norm_g, m_w_ple_gate, m_w_ple_proj, m_ple_post_norm_g, v_mix_norm_g, v_w_in, v_q_a_norm_g, v_w_uq, v_kv_a_norm_g, v_w_ukv, v_q_norm_g, v_k_norm_g, v_hg_lb_logits, v_hg_out_norm_g, v_w_branch, v_w_out, v_ffn_norm_g, v_w_ffn_gate, v_w_ffn_up, v_w_ffn_down, v_ple_gate_norm_g, v_w_ple_gate, v_w_ple_proj, v_ple_post_norm_g):
    w = dict(mix_norm_g=mix_norm_g, w_in=w_in, q_a_norm_g=q_a_norm_g, w_uq=w_uq, kv_a_norm_g=kv_a_norm_g, w_ukv=w_ukv,
             q_norm_g=q_norm_g, k_norm_g=k_norm_g, hg_lb_logits=hg_lb_logits, hg_out_norm_g=hg_out_norm_g,
             w_branch=w_branch, w_out=w_out, ffn_norm_g=ffn_norm_g, w_ffn_gate=w_ffn_gate, w_ffn_up=w_ffn_up,
             w_ffn_down=w_ffn_down, ple_gate_norm_g=ple_gate_norm_g, w_ple_gate=w_ple_gate, w_ple_proj=w_ple_proj,
             ple_post_norm_g=ple_post_norm_g)
    m = dict(mix_norm_g=m_mix_norm_g, w_in=m_w_in, q_a_norm_g=m_q_a_norm_g, w_uq=m_w_uq, kv_a_norm_g=m_kv_a_norm_g,
             w_ukv=m_w_ukv, q_norm_g=m_q_norm_g, k_norm_g=m_k_norm_g, hg_lb_logits=m_hg_lb_logits,
             hg_out_norm_g=m_hg_out_norm_g, w_branch=m_w_branch, w_out=m_w_out, ffn_norm_g=m_ffn_norm_g,
             w_ffn_gate=m_w_ffn_gate, w_ffn_up=m_w_ffn_up, w_ffn_down=m_w_ffn_down,
             ple_gate_norm_g=m_ple_gate_norm_g, w_ple_gate=m_w_ple_gate, w_ple_proj=m_w_ple_proj,
             ple_post_norm_g=m_ple_post_norm_g)
    v = dict(mix_norm_g=v_mix_norm_g, w_in=v_w_in, q_a_norm_g=v_q_a_norm_g, w_uq=v_w_uq, kv_a_norm_g=v_kv_a_norm_g,
             w_ukv=v_w_ukv, q_norm_g=v_q_norm_g, k_norm_g=v_k_norm_g, hg_lb_logits=v_hg_lb_logits,
             hg_out_norm_g=v_hg_out_norm_g, w_branch=v_w_branch, w_out=v_w_out, ffn_norm_g=v_ffn_norm_g,
             w_ffn_gate=v_w_ffn_gate, w_ffn_up=v_w_ffn_up, w_ffn_down=v_w_ffn_down,
             ple_gate_norm_g=v_ple_gate_norm_g, w_ple_gate=v_w_ple_gate, w_ple_proj=v_w_ple_proj,
             ple_post_norm_g=v_ple_post_norm_g)
    return _step(x, p, positions, loss_target, w, m, v)
```

```python
import jax
import jax.numpy as jnp
import numpy as np
from jax import lax
from jax.experimental import pallas as pl
from jax.experimental.pallas import tpu as pltpu

F32 = jnp.float32
MM = jnp.bfloat16
MESH_ID = pl.DeviceIdType.MESH

D_MODEL = 1024
N_DEV = 8
MLA_HEADS = 8
QK_NOPE = 64
QK_ROPE = 32
QK_DIM = 96
V_DIM = 64
HEAD_PAD = 128
Q_RANK = 384
KV_RANK = 256
ROPE_BASE = 10000.0
HG_HEADS = 4
HG_DIM = 128
HG_W = 512
HG_CHUNK = 64
FFN = 2816
PLE = 256
EPS = 1e-6
ATT_SCALE = QK_DIM ** -0.5
NEG = -1e30

ADAM_LR = 0.001
ADAM_B1 = 0.9
ADAM_B2 = 0.999
ADAM_EPS = 1e-08
ADAM_WD = 0.01
ADAM_STEP = 10

COL_SECTIONS = ((0, 384), (384, 256), (640, 32), (672, 512), (1184, 512), (1696, 512), (2208, 512), (2720, 2048))
STORED_WIDTHS = tuple(HEAD_PAD if n == QK_ROPE else n for _, n in COL_SECTIONS)
IN_COLS = 4768
IN_BLOCK = IN_COLS // N_DEV

VMEM_LIMIT = 58 * 1024 * 1024
WIDE_TILE = 512
ROW_TILE = 256
DW_TOKENS = 1024
ATT_TILE = 1024
ATT_HEADS = 4
HG_BLOCK = 512
HG_UNROLL = 4


def _dot(a, b):
    return jnp.dot(a.astype(MM), b.astype(MM), preferred_element_type=F32)


def _dot_nt(a, b):
    return lax.dot_general(a.astype(MM), b.astype(MM), (((1,), (1,)), ((), ())), preferred_element_type=F32)


def _dot_tn(a, b):
    return lax.dot_general(a.astype(MM), b.astype(MM), (((0,), (0,)), ((), ())), preferred_element_type=F32)


def _sigmoid(x):
    return 1.0 / (1.0 + jnp.exp(-x))


def _rms(x, n=None):
    n = x.shape[-1] if n is None else n
    r = lax.rsqrt(jnp.sum(x * x, axis=-1, keepdims=True) * (1.0 / n) + EPS)
    return x * r, r


def _rms_bwd(dxh, xh, r, n=None):
    n = xh.shape[-1] if n is None else n
    return r * (dxh - xh * (jnp.sum(dxh * xh, axis=-1, keepdims=True) * (1.0 / n)))


def _rope_tables(pos, tm):
    lane = lax.broadcasted_iota(jnp.int32, (tm, HEAD_PAD), 1)
    idx = jnp.where(lane < QK_NOPE + QK_ROPE // 2, lane - QK_NOPE, lane - QK_NOPE - QK_ROPE // 2)
    inv = jnp.exp(idx.astype(F32) * (-np.log(ROPE_BASE) * 2.0 / QK_ROPE))
    ang = pos.astype(F32) * inv
    in_rope = (lane >= QK_NOPE) & (lane < QK_DIM)
    first = lane < QK_NOPE + QK_ROPE // 2
    cos_t = jnp.where(in_rope, jnp.cos(ang), 1.0)
    sin_t = jnp.where(in_rope, jnp.where(first, -jnp.sin(ang), jnp.sin(ang)), 0.0)
    return cos_t, sin_t, (first, in_rope)


def _rope_swap(x, halves):
    first, in_rope = halves
    half = QK_ROPE // 2
    return jnp.where(in_rope, jnp.where(first, pltpu.roll(x, HEAD_PAD - half, 1), pltpu.roll(x, half, 1)), 0.0)


def _cparams(sem, vmem=None):
    return pltpu.CompilerParams(dimension_semantics=sem, vmem_limit_bytes=vmem)


def _row_call(name, body, T, tm, row_ins, full_ins, row_outs, acc_outs, vmem=None, scratch=(), xchg=((), ())):
    n_in, n_out, n_x = len(row_ins) + len(full_ins), len(row_outs) + len(acc_outs), _x_count(xchg)
    steps = T // tm

    def kern(*refs):
        ins, x_in, refs = refs[:n_in], refs[n_in:n_in + n_x], refs[n_in + n_x:]
        outs, x_out, refs = refs[:n_out], refs[n_out:n_out + n_x], refs[n_out + n_x:]
        scr, x_sems = refs[:len(scratch)], refs[len(scratch):]
        i = pl.program_id(0)
        if n_x:
            @pl.when(i == 0)
            def _():
                for cp in _x_copies(len(xchg[0]), x_in, x_out, x_sems):
                    cp.start()

        body(i, *ins, *outs, *scr)
        if n_x:
            @pl.when(i == steps - 1)
            def _():
                for cp in _x_copies(len(xchg[0]), x_in, x_out, x_sems):
                    cp.wait()

    any_spec = pl.BlockSpec(memory_space=pl.ANY)
    in_specs = [pl.BlockSpec((tm, a.shape[1]), lambda i: (i, 0)) for a in row_ins]
    in_specs += [pl.BlockSpec(a.shape, lambda i, nd=a.ndim: (0,) * nd, pipeline_mode=pl.Buffered(1)) for a in full_ins]
    out_specs = [pl.BlockSpec((tm, n), lambda i: (i, 0)) for n, _ in row_outs]
    out_specs += [pl.BlockSpec(s, lambda i, nd=len(s): (0,) * nd) for s, _ in acc_outs]
    out_shape = [jax.ShapeDtypeStruct((T, n), dt) for n, dt in row_outs]
    out_shape += [jax.ShapeDtypeStruct(s, dt) for s, dt in acc_outs]
    return pl.pallas_call(
        kern, name=name, grid=(steps,), in_specs=in_specs + [any_spec] * n_x, out_specs=out_specs + [any_spec] * n_x,
        out_shape=out_shape + _x_out_shapes(xchg), scratch_shapes=list(scratch) + _x_sems(xchg),
        compiler_params=_cparams(("arbitrary",), vmem),
    )(*row_ins, *full_ins, *xchg[0], *xchg[1])


FFN_HALVES = (slice(0, FFN // 2), slice(FFN // 2, FFN))
ROW_CHUNK = 16
CHUNK_UNROLL = True


def _by_chunks(tm, fn):
    def step(c, carry):
        fn(pl.ds(pl.multiple_of(c * ROW_CHUNK, ROW_CHUNK), ROW_CHUNK))
        return carry

    lax.fori_loop(0, tm // ROW_CHUNK, step, 0, unroll=CHUNK_UNROLL)


def _fold8(x):
    return x[:8] + x[8:]


def _acc(ref, i, val):
    @pl.when(i == 0)
    def _():
        ref[...] = val

    @pl.when(i != 0)
    def _():
        ref[...] += val


def _in_proj_fwd(x, g_mix, w_in, T, tm):
    def body(i, x_ref, g_ref, w_ref, h_ref, *rest):
        outs, pj_s = rest[:-1], rest[-1]
        g = g_ref[...]

        def norm(rows):
            h_ref[rows, :] = (_rms(x_ref[rows, :])[0] * g).astype(MM)

        _by_chunks(tm, norm)
        for d in range(N_DEV):
            pj_s[d] = _dot_nt(h_ref[...], w_ref[d])

        def join_and_cut(rows):
            proj = jnp.concatenate([pj_s[d, rows, :] for d in range(N_DEV)], axis=1)
            for (s, n), o_ref in zip(COL_SECTIONS, outs):
                if n == QK_ROPE:
                    o_ref[rows, :] = jnp.concatenate(
                        [jnp.zeros((ROW_CHUNK, QK_NOPE), F32), proj[:, s:s + n],
                         jnp.zeros((ROW_CHUNK, HEAD_PAD - QK_DIM), F32)], axis=1)
                else:
                    o_ref[rows, :] = proj[:, s:s + n]

        _by_chunks(tm, join_and_cut)

    row_outs = [(D_MODEL, MM)] + [(n, F32) for n in STORED_WIDTHS]
    return _row_call("in_proj_fwd", body, T, tm, [x], [g_mix, w_in], row_outs, [], VMEM_LIMIT,
                     scratch=[pltpu.VMEM((N_DEV, tm, IN_BLOCK), F32)])


def _mla_heads_fwd(raw, g_pad, cos_t, sin_t, first):
    outs, saved = [], []
    for h in range(MLA_HEADS):
        xh, r = _rms(raw[:, h * HEAD_PAD:(h + 1) * HEAD_PAD], QK_DIM)
        y = xh * g_pad
        outs.append(y * cos_t + _rope_swap(y, first) * sin_t)
        saved.append((xh, r))
    return outs, saved


def _mla_raw_heads(cqn, ckvn, kr, wuq_ref, wukv_ref, tm):
    lane = lax.broadcasted_iota(jnp.int32, (tm, HEAD_PAD), 1)
    nope = lane < QK_NOPE
    one_lane = jnp.where(lane == V_DIM, 1.0, 0.0)
    qs, ks, vs = [], [], []
    for h in range(MLA_HEADS):
        qs.append(_dot_nt(cqn, wuq_ref[h]))
        kv = _dot(ckvn, wukv_ref[h])
        ks.append(jnp.where(nope, kv, kr))
        vs.append(jnp.where(nope, pltpu.roll(kv, V_DIM, 1), one_lane))
    return jnp.concatenate(qs, axis=1), jnp.concatenate(ks, axis=1), jnp.concatenate(vs, axis=1)


def _mla_prep_fwd(cq, ckv, kr, pos, g_qa, g_kva, g_qn, g_kn, w_uq, w_ukv, T, tm):
    def body(i, cq_ref, ckv_ref, kr_ref, pos_ref, gqa_ref, gkva_ref, gqn_ref, gkn_ref, wuq_ref, wukv_ref,
             q_ref, k_ref, v_ref):
        cos_t, sin_t, first = _rope_tables(pos_ref[...], tm)
        cqn = _rms(cq_ref[...])[0] * gqa_ref[...]
        ckvn = _rms(ckv_ref[...])[0] * gkva_ref[...]
        q_raw, k_raw, v = _mla_raw_heads(cqn, ckvn, kr_ref[...], wuq_ref, wukv_ref, tm)
        qs, _ = _mla_heads_fwd(q_raw, gqn_ref[...], cos_t, sin_t, first)
        ks, _ = _mla_heads_fwd(k_raw, gkn_ref[...], cos_t, sin_t, first)
        q_ref[...] = (jnp.concatenate(qs, axis=1) * ATT_SCALE).astype(MM)
        k_ref[...] = jnp.concatenate(ks, axis=1).astype(MM)
        v_ref[...] = v.astype(MM)

    w = MLA_HEADS * HEAD_PAD
    return _row_call("mla_prep_fwd", body, T, tm, [cq, ckv, kr, pos], [g_qa, g_kva, g_qn, g_kn, w_uq, w_ukv],
                     [(w, MM), (w, MM), (w, MM)], [])


def _causal_pairs(n, by_query):
    if by_query:
        pairs = [(q, k) for q in range(n) for k in range(q + 1)]
    else:
        pairs = [(q, k) for k in range(n) for q in range(k, n)]
    return np.array([p[0] for p in pairs], np.int32), np.array([p[1] for p in pairs], np.int32)


def _flash_fwd(qf, kf, vf, T, ag_blocks=()):
    tq = min(ATT_TILE, T)
    nq = T // tq

    qi_tab, ki_tab = _causal_pairs(nq, by_query=True)

    hp = ATT_HEADS

    n_ag = len(ag_blocks)
    n_heads, n_pairs = MLA_HEADS // hp, len(qi_tab)

    def body(qi_ref, ki_ref, q_ref, k_ref, v_ref, *rest):
        ag_in, (o_ref, lse_ref), rest = rest[:n_ag], rest[n_ag:n_ag + 2], rest[n_ag + 2:]
        ag_out, (m_s, acc_s), ag_sems = rest[:n_ag], rest[n_ag:n_ag + 2], rest[n_ag + 2:]
        t = pl.program_id(1)
        qi, ki = qi_ref[t], ki_ref[t]
        if n_ag:
            @pl.when((pl.program_id(0) == 0) & (t == 0))
            def _():
                _ag_start(ag_in, ag_out, ag_sems)

        @pl.when(ki == 0)
        def _():
            m_s[...] = jnp.full_like(m_s, NEG)
            acc_s[...] = jnp.zeros_like(acc_s)

        def step(masked):
            halves = 2 if masked and tq % (2 * HEAD_PAD) == 0 else 1
            w = tq // halves
            for hh in range(hp):
                hs = slice(hh * HEAD_PAD, (hh + 1) * HEAD_PAD)
                for part in range(halves):
                    cols, nk = slice(part * w, (part + 1) * w), (part + 1) * w
                    s_t = _dot_nt(k_ref[:nk, hs], q_ref[cols, hs])
                    if masked:
                        key = lax.broadcasted_iota(jnp.int32, (nk, w), 0)
                        qry = lax.broadcasted_iota(jnp.int32, (nk, w), 1) + part * w
                        s_t = jnp.where(key <= qry, s_t, NEG)
                    m_old = m_s[hh, :, cols]
                    m_new = jnp.maximum(m_old, jnp.max(s_t, axis=0, keepdims=True))
                    p_t = jnp.exp(s_t - m_new)
                    acc_s[hh, :, cols] = jnp.exp(m_old - m_new) * acc_s[hh, :, cols] + _dot_tn(v_ref[:nk, hs], p_t)
                    m_s[hh, :, cols] = m_new

        @pl.when(ki < qi)
        def _():
            step(False)

        @pl.when(ki == qi)
        def _():
            step(True)
            real = lax.broadcasted_iota(jnp.int32, (HEAD_PAD, tq), 0) < V_DIM
            for hh in range(hp):
                hs = slice(hh * HEAD_PAD, (hh + 1) * HEAD_PAD)
                acc = acc_s[hh]
                l = acc[V_DIM:V_DIM + 1]
                o_ref[:, hs] = jnp.where(real, acc / l, 0.0).T
                lse_ref[:, hs] = jnp.broadcast_to(m_s[hh] + jnp.log(l), (HEAD_PAD, tq)).T

        if n_ag:
            @pl.when((pl.program_id(0) == n_heads - 1) & (t == n_pairs - 1))
            def _():
                _ag_finish(ag_in, ag_out, ag_sems)

    q_spec = pl.BlockSpec((tq, hp * HEAD_PAD), lambda h, t, qi_ref, ki_ref: (qi_ref[t], h))
    kv_spec = pl.BlockSpec((tq, hp * HEAD_PAD), lambda h, t, qi_ref, ki_ref: (ki_ref[t], h))
    any_spec = pl.BlockSpec(memory_space=pl.ANY)
    grid_spec = pltpu.PrefetchScalarGridSpec(
        num_scalar_prefetch=2, grid=(n_heads, n_pairs),
        in_specs=[q_spec, kv_spec, kv_spec] + [any_spec] * n_ag, out_specs=[q_spec, q_spec] + [any_spec] * n_ag,
        scratch_shapes=[pltpu.VMEM((hp, 1, tq), F32), pltpu.VMEM((hp, HEAD_PAD, tq), F32)]
        + (_ag_sems(n_ag) if n_ag else []))
    return pl.pallas_call(
        body, name="flash_fwd", grid_spec=grid_spec,
        out_shape=[jax.ShapeDtypeStruct((T, MLA_HEADS * HEAD_PAD), F32)] * 2 + _ag_out_shapes(ag_blocks),
        compiler_params=_cparams(("arbitrary", "arbitrary")),
    )(jnp.asarray(qi_tab), jnp.asarray(ki_tab), qf, kf, vf, *ag_blocks)


def _hg_gates(hf, lb):
    sg = _sigmoid(hf)
    f = lb + (1.0 - lb) * sg
    return sg, f, jnp.log(f), 1.0 - f


def _prefix_sum(x, reverse=False):
    n = x.shape[0]
    row = lax.broadcasted_iota(jnp.int32, x.shape, 0)
    step = 1
    while step < n:
        if reverse:
            x = x + jnp.where(row < n - step, pltpu.roll(x, n - step, 0), 0.0)
        else:
            x = x + jnp.where(row >= step, pltpu.roll(x, step, 0), 0.0)
        step *= 2
    return x


def _hg_levels():
    C = HG_CHUNK
    t = lax.broadcasted_iota(jnp.int32, (C, C), 0)
    s = lax.broadcasted_iota(jnp.int32, (C, C), 1)
    levels = []
    for shift in range(C.bit_length() - 2, -1, -1):
        pair_t, pair_s = lax.shift_right_logical(t, shift + 1), lax.shift_right_logical(s, shift + 1)
        later_t = (lax.shift_right_logical(t, shift) & 1) == 1
        earlier_s = (lax.shift_right_logical(s, shift) & 1) == 0
        levels.append((1 << shift, (pair_t == pair_s) & later_t & earlier_s))
    return levels, t == s


def _hg_refs(b):
    C, n = b.shape
    row = lax.broadcasted_iota(jnp.int32, (C, n), 0)
    back1, back2, ahead1 = pltpu.roll(b, 1, 0), pltpu.roll(b, 2, 0), pltpu.roll(b, C - 1, 0)
    refs = []
    for half in (32, 16, 8, 4):
        refs.append(jnp.concatenate(
            [jnp.broadcast_to(b[lo + half - 1:lo + half], (2 * half, n)) for lo in range(0, C, 2 * half)], axis=0))
    in4 = row & 3
    refs.append(jnp.where(in4 == 0, ahead1, jnp.where(in4 == 1, b, jnp.where(in4 == 2, back1, back2))))
    refs.append(jnp.where((row & 1) == 1, back1, b))
    return refs


def _hg_intra(q, k, b, refs, levels, eye):
    a = jnp.where(eye, jnp.sum(q * k, axis=1, keepdims=True), 0.0)
    saved = []
    for r, (_, mask) in zip(refs, levels):
        e = jnp.exp(-jnp.abs(b - r))
        q_t, k_t = q * e, k * e
        a = a + jnp.where(mask, _dot_nt(q_t, k_t), 0.0)
        saved.append((q_t, k_t, e))
    return a, saved


def _hg_intra_bwd(d_a, q, k, saved, levels, eye):
    diag = jnp.sum(jnp.where(eye, d_a, 0.0), axis=1, keepdims=True)
    dq, dk = diag * k, diag * q
    for (q_t, k_t, e), (_, mask) in zip(saved, levels):
        da = jnp.where(mask, d_a, 0.0)
        dq = dq + _dot(da, k_t) * e
        dk = dk + _dot_tn(da, q_t) * e
    return dq, dk


def _hgrn_fwd(hq, hf, hi, lb, T):
    rb = min(HG_BLOCK, T)
    ncb = rb // HG_CHUNK

    def body(hq_ref, hf_ref, hi_ref, lb_ref, o_ref, s0_ref, st_ref):
        @pl.when(pl.program_id(0) == 0)
        def _():
            st_ref[...] = jnp.zeros_like(st_ref)

        levels, eye = _hg_levels()

        def chunk(c, carry):
            rows = pl.ds(pl.multiple_of(c * HG_CHUNK, HG_CHUNK), HG_CHUNK)
            _, _, logf, kk = _hg_gates(hf_ref[rows, :], lb_ref[...])
            b = _prefix_sum(logf)
            refs = _hg_refs(b)
            q_all, v_all = hq_ref[rows, :], hi_ref[rows, :]
            outs = []
            for h in range(HG_HEADS):
                ls = slice(h * HG_DIM, (h + 1) * HG_DIM)
                q, k, v, bh = q_all[:, ls], kk[:, ls], v_all[:, ls], b[:, ls]
                st = st_ref[h]
                s0_ref[c, h * HG_DIM:(h + 1) * HG_DIM, :] = st
                b_end = bh[HG_CHUNK - 1:HG_CHUNK]
                a, _ = _hg_intra(q, k, bh, [r[:, ls] for r in refs], levels, eye)
                outs.append(_dot_nt(q * jnp.exp(bh), st) + _dot(a, v))
                st_ref[h] = st * jnp.exp(b_end) + _dot_tn(v, k * jnp.exp(b_end - bh))
            o_ref[rows, :] = jnp.concatenate(outs, axis=1)
            return carry

        lax.fori_loop(0, ncb, chunk, 0, unroll=HG_UNROLL)

    row = pl.BlockSpec((rb, HG_W), lambda i: (i, 0))
    return pl.pallas_call(
        body, name="hgrn_fwd", grid=(T // rb,),
        in_specs=[row, row, row, pl.BlockSpec((1, HG_W), lambda i: (0, 0))],
        out_specs=[row, pl.BlockSpec((ncb, HG_W, HG_DIM), lambda i: (i, 0, 0))],
        out_shape=[jax.ShapeDtypeStruct((T, HG_W), F32), jax.ShapeDtypeStruct((T // HG_CHUNK, HG_W, HG_DIM), F32)],
        scratch_shapes=[pltpu.VMEM((HG_HEADS, HG_DIM, HG_DIM), F32)],
        compiler_params=_cparams(("arbitrary",)),
    )(hq, hf, hi, lb)


def _hgrn_bwd(hq, hf, hi, do, s0, lb, T, xchg=((), ())):
    rb = min(HG_BLOCK, T)
    ncb = rb // HG_CHUNK
    nb = T // rb
    C = HG_CHUNK
    n_x, n_sib = _x_count(xchg), len(xchg[0])

    def body(hq_ref, hf_ref, hi_ref, do_ref, s0_ref, lb_ref, *rest):
        x_in, (dq_ref, df_ref, dv_ref, dlb_ref), rest = rest[:n_x], rest[n_x:n_x + 4], rest[n_x + 4:]
        x_out, dst_ref, x_sems = rest[:n_x], rest[n_x], rest[n_x + 1:]

        @pl.when(pl.program_id(0) == 0)
        def _():
            dst_ref[...] = jnp.zeros_like(dst_ref)
            dlb_ref[...] = jnp.zeros_like(dlb_ref)
            for cp in _x_copies(n_sib, x_in, x_out, x_sems):
                cp.start()

        row_cc = lax.broadcasted_iota(jnp.int32, (C, C), 0)
        col_cc = lax.broadcasted_iota(jnp.int32, (C, C), 1)
        last_row = lax.broadcasted_iota(jnp.int32, (C, HG_DIM), 0) == C - 1
        lb_v = lb_ref[...]
        levels, eye = _hg_levels()

        def chunk(cc, carry):
            c = ncb - 1 - cc
            rows = pl.ds(pl.multiple_of(c * C, C), C)
            hf_c = hf_ref[rows, :]
            sg, f, logf, kk = _hg_gates(hf_c, lb_v)
            b = _prefix_sum(logf)
            refs = _hg_refs(b)
            q_all, v_all, do_all = hq_ref[rows, :], hi_ref[rows, :], do_ref[rows, :]
            dq_o, dk_o, dv_o, db_o = [], [], [], []
            for h in range(HG_HEADS):
                ls = slice(h * HG_DIM, (h + 1) * HG_DIM)
                q, k, v, bh, d_o = q_all[:, ls], kk[:, ls], v_all[:, ls], b[:, ls], do_all[:, ls]
                st0 = s0_ref[c, h * HG_DIM:(h + 1) * HG_DIM, :]
                dst = dst_ref[h]
                b_end = bh[C - 1:C]
                e_b, e_end = jnp.exp(bh), jnp.exp(b_end)
                e_rem = jnp.exp(b_end - bh)
                qe, kd = q * e_b, k * e_rem
                st_end = st0 * e_end + _dot_tn(v, kd)
                a, saved = _hg_intra(q, k, bh, [r[:, ls] for r in refs], levels, eye)
                d_a = jnp.where(col_cc <= row_cc, _dot_nt(d_o, v), 0.0)
                dq_i, dk_i = _hg_intra_bwd(d_a, q, k, saved, levels, eye)
                dv = _dot_tn(a, d_o) + _dot_nt(kd, dst)
                dq = e_b * _dot(d_o, st0) + dq_i
                dk = e_rem * _dot(v, dst) + dk_i
                extra = jnp.sum(dst * st_end, axis=0, keepdims=True)
                db_o.append(q * dq - k * dk + jnp.where(last_row, extra, 0.0))
                dst_ref[h] = dst * e_end + _dot_tn(d_o, qe)
                dq_o.append(dq)
                dk_o.append(dk)
                dv_o.append(dv)
            dlogf = _prefix_sum(jnp.concatenate(db_o, axis=1), reverse=True)
            d_f = dlogf / f - jnp.concatenate(dk_o, axis=1)
            dq_ref[rows, :] = jnp.concatenate(dq_o, axis=1).astype(MM)
            dv_ref[rows, :] = jnp.concatenate(dv_o, axis=1).astype(MM)
            df_ref[rows, :] = (d_f * (1.0 - lb_v) * sg * (1.0 - sg)).astype(MM)
            dlb_ref[...] += jnp.sum(d_f * (1.0 - sg), axis=0, keepdims=True)
            return carry

        lax.fori_loop(0, ncb, chunk, 0, unroll=HG_UNROLL)

        if n_x:
            @pl.when(pl.program_id(0) == nb - 1)
            def _():
                for cp in _x_copies(n_sib, x_in, x_out, x_sems):
                    cp.wait()

    row = pl.BlockSpec((rb, HG_W), lambda i: (nb - 1 - i, 0))
    one = pl.BlockSpec((1, HG_W), lambda i: (0, 0))
    any_spec = pl.BlockSpec(memory_space=pl.ANY)
    return pl.pallas_call(
        body, name="hgrn_bwd", grid=(nb,),
        in_specs=[row, row, row, row, pl.BlockSpec((ncb, HG_W, HG_DIM), lambda i: (nb - 1 - i, 0, 0)), one]
        + [any_spec] * n_x,
        out_specs=[row, row, row, one] + [any_spec] * n_x,
        out_shape=[jax.ShapeDtypeStruct((T, HG_W), MM)] * 3 + [jax.ShapeDtypeStruct((1, HG_W), F32)]
        + _x_out_shapes(xchg),
        scratch_shapes=[pltpu.VMEM((HG_HEADS, HG_DIM, HG_DIM), F32)] + _x_sems(xchg),
        compiler_params=_cparams(("arbitrary",)),
    )(hq, hf, hi, do, s0, lb, *xchg[0], *xchg[1])


def _silu_parts(x):
    sg = _sigmoid(x)
    return x * sg, sg * (1.0 + x * (1.0 - sg))


def _merge_fwd(attn, o, hg, bg, x, g_out, w_bra, w_brb, w_out, T, tm):
    def body(i, attn_ref, o_ref, hg_ref, bg_ref, x_ref, g_ref, wa_ref, wb_ref, wo_ref,
             x1_ref, ya_ref, yb_ref, m_ref, rec_ref):
        g = g_ref[...]

        def recurrent_out(rows):
            for h in range(HG_HEADS):
                ls = slice(h * HG_DIM, (h + 1) * HG_DIM)
                rec_ref[rows, ls] = (_rms(o_ref[rows, ls])[0] * g * _silu_parts(hg_ref[rows, ls])[0]).astype(MM)

        _by_chunks(tm, recurrent_out)
        ya_ref[...] = _dot(attn_ref[...], wa_ref[...])
        yb_ref[...] = jnp.dot(rec_ref[...], wb_ref[...], preferred_element_type=F32)

        def gate(rows):
            m_ref[rows, :] = (_sigmoid(bg_ref[rows, :D_MODEL]) * ya_ref[rows, :]
                              + _sigmoid(bg_ref[rows, D_MODEL:]) * yb_ref[rows, :]).astype(MM)

        _by_chunks(tm, gate)
        x1_ref[...] = x_ref[...] + jnp.dot(m_ref[...], wo_ref[...], preferred_element_type=F32)

    return _row_call("merge_fwd", body, T, tm, [attn, o, hg, bg, x], [g_out, w_bra, w_brb, w_out],
                     [(D_MODEL, F32), (D_MODEL, F32), (D_MODEL, F32), (D_MODEL, MM), (HG_W, MM)], [], VMEM_LIMIT)


def _ffn_fwd(x1, g_ffn, w_g, w_u, w_d, T, tm):
    def body(i, x1_ref, g_ref, wg_ref, wu_ref, wd_ref, x2_ref, gt_ref, up_ref, h2_ref, a_s):
        g = g_ref[...]

        def norm(rows):
            h2_ref[rows, :] = (_rms(x1_ref[rows, :])[0] * g).astype(MM)

        _by_chunks(tm, norm)
        gt_ref[...] = _dot_nt(h2_ref[...], wg_ref[...])
        up_ref[...] = _dot_nt(h2_ref[...], wu_ref[...])

        def act(rows):
            for cs in FFN_HALVES:
                a_s[rows, cs] = (_silu_parts(gt_ref[rows, cs])[0] * up_ref[rows, cs]).astype(MM)

        _by_chunks(tm, act)
        x2_ref[...] = x1_ref[...] + jnp.dot(a_s[...], wd_ref[...], preferred_element_type=F32)

    return _row_call("ffn_fwd", body, T, tm, [x1], [g_ffn, w_g, w_u, w_d],
                     [(D_MODEL, F32), (FFN, F32), (FFN, F32), (D_MODEL, MM)], [], VMEM_LIMIT,
                     scratch=[pltpu.VMEM((tm, FFN), MM)])


def _ple_loss(x2, p, tgt, g_pg, g_post, w_pg, w_pp, T, tm):
    def body(i, x2_ref, p_ref, t_ref, gpg_ref, gpo_ref, wpg_ref, wpp_ref,
             dx2_ref, loss_ref, dgpo_ref, dgpg_ref, dwpg_ref, dwpp_ref, u_s, n3_s, z_s, dz_s, du_s, dy_s, dn3_s):
        @pl.when(i == 0)
        def _():
            for ref in (loss_ref, dgpo_ref, dgpg_ref, dwpg_ref, dwpp_ref):
                ref[...] = jnp.zeros_like(ref)

        gpg, gpo = gpg_ref[...], gpo_ref[...]
        p_mm = p_ref[...].astype(MM)
        for d in range(N_DEV):
            u_s[:, d * HEAD_PAD:(d + 1) * HEAD_PAD] = jnp.dot(p_mm, wpp_ref[d], preferred_element_type=F32)

        def gate_input(rows):
            n3_s[rows, :] = (_rms(x2_ref[rows, :])[0] * gpg).astype(MM)

        _by_chunks(tm, gate_input)
        z_s[...] = jnp.dot(n3_s[...], wpg_ref[...], preferred_element_type=F32)

        def loss_and_back(rows):
            uh, ru = _rms(u_s[rows, :])
            e = uh * gpo
            gate = _sigmoid(z_s[rows, :])
            diff = x2_ref[rows, :] + gate * e - t_ref[rows, :]
            dy = diff * (1.0 / D_MODEL)
            de = dy * gate
            dz_s[rows, :] = (dy * e * gate * (1.0 - gate)).astype(MM)
            du_s[rows, :] = _rms_bwd(de * gpo, uh, ru).astype(MM)
            dy_s[rows, :] = dy
            loss_ref[...] += _fold8(diff * diff) * (0.5 / D_MODEL)
            dgpo_ref[...] += _fold8(de * uh)

        _by_chunks(tm, loss_and_back)
        dn3_s[...] = _dot_nt(dz_s[...], wpg_ref[...])

        def gate_norm_back(rows):
            x2h, r3 = _rms(x2_ref[rows, :])
            dn3 = dn3_s[rows, :]
            dx2_ref[rows, :] = dy_s[rows, :] + _rms_bwd(dn3 * gpg, x2h, r3)
            dgpg_ref[...] += _fold8(dn3 * x2h)

        _by_chunks(tm, gate_norm_back)
        dwpg_ref[...] += _dot_tn(n3_s[...], dz_s[...])
        for d in range(N_DEV):
            dwpp_ref[d] += _dot_tn(p_mm, du_s[:, d * HEAD_PAD:(d + 1) * HEAD_PAD])

    vec = ((8, D_MODEL), F32)
    wide = lambda dt: pltpu.VMEM((tm, D_MODEL), dt)
    return _row_call("ple_loss", body, T, tm, [x2, p, tgt], [g_pg, g_post, w_pg, w_pp], [(D_MODEL, F32)],
                     [vec, vec, vec, ((D_MODEL, D_MODEL), F32), ((N_DEV, PLE, HEAD_PAD), F32)], VMEM_LIMIT,
                     scratch=[wide(F32), wide(MM), wide(F32), wide(MM), wide(MM), wide(F32), wide(F32)])


def _ffn_bwd(dx2, x1, gt, up, g_ffn, w_g, w_u, w_d, T, tm):
    def body(i, dx2_ref, x1_ref, gt_ref, up_ref, g_ref, wg_ref, wu_ref, wd_ref,
             dx1_ref, a_ref, dgt_ref, dup_ref, dg_ref, da_s, dh2_s):
        @pl.when(i == 0)
        def _():
            dg_ref[...] = jnp.zeros_like(dg_ref)

        g = g_ref[...]
        da_s[...] = _dot_nt(dx2_ref[...], wd_ref[...])

        def act_back(rows):
            for cs in FFN_HALVES:
                up, da = up_ref[rows, cs], da_s[rows, cs]
                silu, dsilu = _silu_parts(gt_ref[rows, cs])
                dgt_ref[rows, cs] = (da * up * dsilu).astype(MM)
                dup_ref[rows, cs] = (da * silu).astype(MM)
                a_ref[rows, cs] = (silu * up).astype(MM)

        _by_chunks(tm, act_back)
        dh2_s[...] = (jnp.dot(dgt_ref[...], wg_ref[...], preferred_element_type=F32)
                      + jnp.dot(dup_ref[...], wu_ref[...], preferred_element_type=F32))

        def norm_back(rows):
            x1h, r = _rms(x1_ref[rows, :])
            dh2 = dh2_s[rows, :]
            dx1_ref[rows, :] = dx2_ref[rows, :] + _rms_bwd(dh2 * g, x1h, r)
            dg_ref[...] += _fold8(dh2 * x1h)

        _by_chunks(tm, norm_back)

    return _row_call("ffn_bwd", body, T, tm, [dx2, x1, gt, up], [g_ffn, w_g, w_u, w_d],
                     [(D_MODEL, F32), (FFN, MM), (FFN, MM), (FFN, MM)], [((8, D_MODEL), F32)], VMEM_LIMIT,
                     scratch=[pltpu.VMEM((tm, FFN), F32), pltpu.VMEM((tm, D_MODEL), F32)])


def _merge_bwd(dx1, ya, yb, bg, o, hg, attn, m, rec, g_out, w_bra, w_brb, w_out, T, tm, xchg=((), ())):
    def body(i, dx1_ref, ya_ref, yb_ref, bg_ref, o_ref, hg_ref, attn_ref, m_ref, rec_ref, g_ref, wa_ref, wb_ref, wo_ref,
             dattn_ref, do_ref, dhg_ref, dbg_ref, dg_ref, dwo_ref, dwa_ref, dwb_ref, dm_s, dya_s, dyb_s, drec_s):
        @pl.when(i == 0)
        def _():
            for ref in (dg_ref, dwo_ref, dwa_ref, dwb_ref):
                ref[...] = jnp.zeros_like(ref)

        g = g_ref[...]
        dx1 = dx1_ref[...].astype(MM)
        dm_s[...] = _dot_nt(dx1, wo_ref[...])

        def gate_back(rows):
            dm = dm_s[rows, :]
            ga, gb = _sigmoid(bg_ref[rows, :D_MODEL]), _sigmoid(bg_ref[rows, D_MODEL:])
            dya_s[rows, :] = (dm * ga).astype(MM)
            dyb_s[rows, :] = (dm * gb).astype(MM)
            dbg_ref[rows, :D_MODEL] = (dm * ya_ref[rows, :] * ga * (1.0 - ga)).astype(MM)
            dbg_ref[rows, D_MODEL:] = (dm * yb_ref[rows, :] * gb * (1.0 - gb)).astype(MM)

        _by_chunks(tm, gate_back)
        dwo_ref[...] += _dot_tn(m_ref[...], dx1)
        attn_mm = attn_ref[...].astype(MM)
        for d in range(N_DEV):
            ds = slice(d * HEAD_PAD, (d + 1) * HEAD_PAD)
            dwa_ref[d] += _dot_tn(attn_mm, dya_s[:, ds])
            dwb_ref[d] += _dot_tn(rec_ref[...], dyb_s[:, ds])
        dattn_ref[...] = _dot_nt(dya_s[...], wa_ref[...])
        drec_s[...] = _dot_nt(dyb_s[...], wb_ref[...])

        def recurrent_out_back(rows):
            for h in range(HG_HEADS):
                ls = slice(h * HG_DIM, (h + 1) * HG_DIM)
                oh, r = _rms(o_ref[rows, ls])
                silu, dsilu = _silu_parts(hg_ref[rows, ls])
                dr = drec_s[rows, ls]
                dhg_ref[rows, ls] = (dr * oh * g * dsilu).astype(MM)
                don = dr * silu
                dg_ref[...] += _fold8(don * oh)
                do_ref[rows, ls] = _rms_bwd(don * g, oh, r)

        _by_chunks(tm, recurrent_out_back)

    wide = lambda n, dt: pltpu.VMEM((tm, n), dt)
    return _row_call("merge_bwd", body, T, tm, [dx1, ya, yb, bg, o, hg, attn, m, rec], [g_out, w_bra, w_brb, w_out],
                     [(D_MODEL, F32), (HG_W, F32), (HG_W, MM), (2 * D_MODEL, MM)],
                     [((8, HG_DIM), F32), ((D_MODEL, D_MODEL), F32), ((N_DEV, MLA_HEADS * HEAD_PAD, HEAD_PAD), F32),
                      ((N_DEV, HG_W, HEAD_PAD), F32)], VMEM_LIMIT,
                     scratch=[wide(D_MODEL, F32), wide(D_MODEL, MM), wide(D_MODEL, MM), wide(HG_W, F32)], xchg=xchg)


def _flash_bwd(qf, kf, vf, o, do, lse, T, xchg=((), ())):
    tq = min(ATT_TILE, T)
    nq = T // tq

    qi_tab, ki_tab = _causal_pairs(nq, by_query=False)

    n_x, n_sib = _x_count(xchg), len(xchg[0])
    hp = ATT_HEADS
    n_heads, n_pairs = MLA_HEADS // hp, len(qi_tab)

    def body(qi_ref, ki_ref, q_ref, k_ref, v_ref, o_ref, do_ref, lse_ref, *rest):
        x_in, (dq_ref, dk_ref, dv_ref), rest = rest[:n_x], rest[n_x:n_x + 3], rest[n_x + 3:]
        x_out, x_sems = rest[:n_x], rest[n_x:]
        t = pl.program_id(1)
        qi, ki = qi_ref[t], ki_ref[t]
        if n_x:
            @pl.when((pl.program_id(0) == 0) & (t == 0))
            def _():
                for cp in _x_copies(n_sib, x_in, x_out, x_sems):
                    cp.start()

        @pl.when(t == 0)
        def _():
            dq_ref[...] = jnp.zeros_like(dq_ref)

        def step(first):
            halves = 2 if first and tq % (2 * HEAD_PAD) == 0 else 1
            w = tq // halves
            for hh in range(hp):
                hs = slice(hh * HEAD_PAD, (hh + 1) * HEAD_PAD)
                for part in range(halves):
                    keys, qs = slice(part * w, (part + 1) * w), slice(part * w, tq)
                    nq_ = tq - part * w
                    q, k, d_o = q_ref[qs, hs], k_ref[keys, hs], do_ref[qs, hs]
                    s = _dot_nt(q, k)
                    if first:
                        row = lax.broadcasted_iota(jnp.int32, (nq_, w), 0)
                        col = lax.broadcasted_iota(jnp.int32, (nq_, w), 1)
                        s = jnp.where(col <= row, s, NEG)
                    p = jnp.exp(s - lse_ref[qs, hh * HEAD_PAD:hh * HEAD_PAD + 1])
                    delta = jnp.sum(d_o * o_ref[qs, hs], axis=1, keepdims=True)
                    ds = p * (_dot_nt(d_o, v_ref[keys, hs]) - delta)
                    rows = pl.ds(pl.multiple_of(qi * tq + part * w, w), nq_)
                    dq_ref[rows, hs] += _dot(ds, k)
                    if first:
                        dv_ref[keys, hs] = _dot_tn(p, d_o)
                        dk_ref[keys, hs] = _dot_tn(ds, q)
                    else:
                        dv_ref[keys, hs] += _dot_tn(p, d_o)
                        dk_ref[keys, hs] += _dot_tn(ds, q)

        @pl.when(qi == ki)
        def _():
            step(True)

        @pl.when(qi > ki)
        def _():
            step(False)

        if n_x:
            @pl.when((pl.program_id(0) == n_heads - 1) & (t == n_pairs - 1))
            def _():
                for cp in _x_copies(n_sib, x_in, x_out, x_sems):
                    cp.wait()

    q_spec = pl.BlockSpec((tq, hp * HEAD_PAD), lambda h, t, qi_ref, ki_ref: (qi_ref[t], h))
    kv_spec = pl.BlockSpec((tq, hp * HEAD_PAD), lambda h, t, qi_ref, ki_ref: (ki_ref[t], h))
    any_spec = pl.BlockSpec(memory_space=pl.ANY)
    w = MLA_HEADS * HEAD_PAD
    grid_spec = pltpu.PrefetchScalarGridSpec(
        num_scalar_prefetch=2, grid=(n_heads, n_pairs),
        in_specs=[q_spec, kv_spec, kv_spec, q_spec, q_spec, q_spec] + [any_spec] * n_x,
        out_specs=[pl.BlockSpec((T, hp * HEAD_PAD), lambda h, t, qi_ref, ki_ref: (0, h)), kv_spec, kv_spec]
        + [any_spec] * n_x,
        scratch_shapes=_x_sems(xchg))
    return pl.pallas_call(
        body, name="flash_bwd", grid_spec=grid_spec,
        out_shape=[jax.ShapeDtypeStruct((T, w), F32)] * 3 + _x_out_shapes(xchg),
        compiler_params=_cparams(("arbitrary", "arbitrary")),
    )(jnp.asarray(qi_tab), jnp.asarray(ki_tab), qf, kf, vf, o, do, lse, *xchg[0], *xchg[1])


def _mla_heads_bwd(d_out, saved, g_pad, cos_t, sin_t, first):
    d_raw, dg = [], jnp.zeros((1, HEAD_PAD), F32)
    for h in range(MLA_HEADS):
        xh, r = saved[h]
        dy = d_out[:, h * HEAD_PAD:(h + 1) * HEAD_PAD]
        dn = dy * cos_t + _rope_swap(dy * sin_t, first)
        dg = dg + jnp.sum(dn * xh, axis=0, keepdims=True)
        d_raw.append(_rms_bwd(dn * g_pad, xh, r, QK_DIM))
    return d_raw, dg


def _mla_prep_bwd(cq, ckv, kr, pos, dqf, dkf, dvf, g_qa, g_kva, g_qn, g_kn, w_uq, w_ukv, T, tm):
    def body(i, cq_ref, ckv_ref, kr_ref, pos_ref, dq_ref, dk_ref, dv_ref,
             gqa_ref, gkva_ref, gqn_ref, gkn_ref, wuq_ref, wukv_ref,
             dcq_ref, dckv_ref, dkr_ref, dgqa_ref, dgkva_ref, dgqn_ref, dgkn_ref, dwuq_ref, dwukv_ref):
        cos_t, sin_t, first = _rope_tables(pos_ref[...], tm)
        cqh, rq = _rms(cq_ref[...])
        ckvh, rkv = _rms(ckv_ref[...])
        cqn, ckvn = cqh * gqa_ref[...], ckvh * gkva_ref[...]
        q_raw, k_raw, _ = _mla_raw_heads(cqn, ckvn, kr_ref[...], wuq_ref, wukv_ref, tm)
        _, q_saved = _mla_heads_fwd(q_raw, gqn_ref[...], cos_t, sin_t, first)
        _, k_saved = _mla_heads_fwd(k_raw, gkn_ref[...], cos_t, sin_t, first)
        dq_heads, dgqn = _mla_heads_bwd(dq_ref[...] * ATT_SCALE, q_saved, gqn_ref[...], cos_t, sin_t, first)
        dk_heads, dgkn = _mla_heads_bwd(dk_ref[...], k_saved, gkn_ref[...], cos_t, sin_t, first)
        lane = lax.broadcasted_iota(jnp.int32, (tm, HEAD_PAD), 1)
        nope = lane < QK_NOPE
        dcqn = jnp.zeros((tm, Q_RANK), F32)
        dckvn = jnp.zeros((tm, KV_RANK), F32)
        dkr = jnp.zeros((tm, HEAD_PAD), F32)
        cqn_mm, ckvn_mm = cqn.astype(MM), ckvn.astype(MM)
        for h in range(MLA_HEADS):
            hs = slice(h * HEAD_PAD, (h + 1) * HEAD_PAD)
            dq_h = dq_heads[h].astype(MM)
            dkv_h = jnp.where(nope, dk_heads[h], pltpu.roll(dv_ref[:, hs], V_DIM, 1)).astype(MM)
            _acc(dwuq_ref.at[h], i, _dot_tn(dq_h, cqn_mm))
            _acc(dwukv_ref.at[h], i, _dot_tn(ckvn_mm, dkv_h))
            dcqn = dcqn + jnp.dot(dq_h, wuq_ref[h], preferred_element_type=F32)
            dckvn = dckvn + lax.dot_general(dkv_h, wukv_ref[h], (((1,), (1,)), ((), ())), preferred_element_type=F32)
            dkr = dkr + dk_heads[h]
        dkr_ref[...] = jnp.where((lane >= QK_NOPE) & (lane < QK_DIM), dkr, 0.0).astype(MM)
        dcq_ref[...] = _rms_bwd(dcqn * gqa_ref[...], cqh, rq).astype(MM)
        dckv_ref[...] = _rms_bwd(dckvn * gkva_ref[...], ckvh, rkv).astype(MM)
        _acc(dgqa_ref, i, jnp.sum(dcqn * cqh, axis=0, keepdims=True))
        _acc(dgkva_ref, i, jnp.sum(dckvn * ckvh, axis=0, keepdims=True))
        _acc(dgqn_ref, i, dgqn)
        _acc(dgkn_ref, i, dgkn)

    return _row_call(
        "mla_prep_bwd", body, T, tm, [cq, ckv, kr, pos, dqf, dkf, dvf], [g_qa, g_kva, g_qn, g_kn, w_uq, w_ukv],
        [(Q_RANK, MM), (KV_RANK, MM), (HEAD_PAD, MM)],
        [((1, Q_RANK), F32), ((1, KV_RANK), F32), ((1, HEAD_PAD), F32), ((1, HEAD_PAD), F32),
         ((MLA_HEADS, HEAD_PAD, Q_RANK), F32), ((MLA_HEADS, KV_RANK, HEAD_PAD), F32)], VMEM_LIMIT)


def _in_proj_bwd(x, dx1, dsecs, g_mix, w_in, T, tm):
    def body(i, x_ref, dx1_ref, *rest):
        d_refs, (g_ref, w_ref, dx_ref, dp_ref, dg_ref, dh_s) = rest[:len(COL_SECTIONS)], rest[len(COL_SECTIONS):]

        @pl.when(i == 0)
        def _():
            dg_ref[...] = jnp.zeros_like(dg_ref)

        g = g_ref[...]

        def join_and_cut(rows):
            pieces = [(d_ref[rows, QK_NOPE:QK_DIM] if n == QK_ROPE else d_ref[rows, :]).astype(F32)
                      for (_, n), d_ref in zip(COL_SECTIONS, d_refs)]
            dproj = jnp.concatenate(pieces, axis=1)
            for d in range(N_DEV):
                dp_ref[d, rows, :] = dproj[:, d * IN_BLOCK:(d + 1) * IN_BLOCK].astype(MM)

        _by_chunks(tm, join_and_cut)
        dh = jnp.dot(dp_ref[0], w_ref[0], preferred_element_type=F32)
        for d in range(1, N_DEV):
            dh = dh + jnp.dot(dp_ref[d], w_ref[d], preferred_element_type=F32)
        dh_s[...] = dh

        def norm_back(rows):
            xh, r = _rms(x_ref[rows, :])
            dh_c = dh_s[rows, :]
            dx_ref[rows, :] = dx1_ref[rows, :] + _rms_bwd(dh_c * g, xh, r)
            dg_ref[...] += _fold8(dh_c * xh)

        _by_chunks(tm, norm_back)

    in_specs = [pl.BlockSpec((tm, a.shape[1]), lambda i: (i, 0)) for a in [x, dx1, *dsecs]]
    in_specs += [pl.BlockSpec(g_mix.shape, lambda i: (0, 0)),
                 pl.BlockSpec(w_in.shape, lambda i: (0, 0, 0), pipeline_mode=pl.Buffered(1))]

    def kern(*refs):
        body(pl.program_id(0), *refs)

    return pl.pallas_call(
        kern, name="in_proj_bwd", grid=(T // tm,), in_specs=in_specs,
        out_specs=[pl.BlockSpec((tm, D_MODEL), lambda i: (i, 0)),
                   pl.BlockSpec((N_DEV, tm, IN_BLOCK), lambda i: (0, i, 0)),
                   pl.BlockSpec((8, D_MODEL), lambda i: (0, 0))],
        out_shape=[jax.ShapeDtypeStruct((T, D_MODEL), F32), jax.ShapeDtypeStruct((N_DEV, T, IN_BLOCK), MM),
                   jax.ShapeDtypeStruct((8, D_MODEL), F32)],
        scratch_shapes=[pltpu.VMEM((tm, D_MODEL), F32)],
        compiler_params=_cparams(("arbitrary",), VMEM_LIMIT),
    )(x, dx1, *dsecs, g_mix, w_in)


def _pick_block(n, cap):
    best = None
    for cand in range(128, min(n, cap) + 1, 128):
        if n % cand == 0:
            best = cand
    return n if best is None else best


def _matmul_tn(name, a, b):
    T, M = a.shape
    N = b.shape[1]
    bm, bk = _pick_block(M, 2816), min(DW_TOKENS, T)
    bn = _pick_block(N, 2560)
    sub = _pick_block(bm, 1408)

    def body(a_ref, b_ref, c_ref):
        @pl.when(pl.program_id(2) == 0)
        def _():
            c_ref[...] = jnp.zeros_like(c_ref)

        b_blk = b_ref[...].astype(MM)
        for r in range(0, bm, sub):
            c_ref[r:r + sub, :] += _dot_tn(a_ref[:, r:r + sub], b_blk)

    return pl.pallas_call(
        body, name=name, grid=(M // bm, N // bn, T // bk),
        in_specs=[pl.BlockSpec((bk, bm), lambda i, j, k: (k, i)), pl.BlockSpec((bk, bn), lambda i, j, k: (k, j))],
        out_specs=pl.BlockSpec((bm, bn), lambda i, j, k: (i, j)), out_shape=jax.ShapeDtypeStruct((M, N), F32),
        compiler_params=_cparams(("parallel", "parallel", "arbitrary"), VMEM_LIMIT),
    )(a, b)


def _matmul_tn_blocks(name, a, b):
    T, M = a.shape
    nd, _, c = b.shape
    bm, bk = _pick_block(M, 512), min(DW_TOKENS, T)

    def body(a_ref, b_ref, c_ref):
        @pl.when(pl.program_id(1) == 0)
        def _():
            c_ref[...] = jnp.zeros_like(c_ref)

        a_blk = a_ref[...].astype(MM)
        for d in range(nd):
            c_ref[d] += _dot_tn(b_ref[d], a_blk)

    return pl.pallas_call(
        body, name=name, grid=(M // bm, T // bk),
        in_specs=[pl.BlockSpec((bk, bm), lambda i, k: (k, i)), pl.BlockSpec((nd, bk, c), lambda i, k: (0, k, 0))],
        out_specs=pl.BlockSpec((nd, c, bm), lambda i, k: (0, 0, i)),
        out_shape=jax.ShapeDtypeStruct((nd, c, M), F32),
        compiler_params=_cparams(("parallel", "arbitrary"), VMEM_LIMIT),
    )(a, b)


def _pad_gain(g, n):
    return jnp.pad(g.reshape(1, -1), ((0, 0), (0, n - g.shape[-1])))


GROUP_A = ("w_ffn_gate", "w_ffn_up", "w_ffn_down", "w_ple_gate", "w_ple_proj")
GROUP_B = ("w_branch", "w_out")
GROUP_C = ("w_in", "w_uq", "w_ukv")
EARLY = GROUP_C
LATE = GROUP_B + GROUP_A
TRANSPOSED = ("w_in", "w_uq", "w_ffn_gate", "w_ffn_up")


def _local_step(x, p, pos, tgt, small, big, late_blocks=None, core=None):
    T = x.shape[0]
    tm = min(ROW_TILE, T)
    tw = min(WIDE_TILE, T)
    w_in = big["w_in"]
    w_uq = jnp.pad(big["w_uq"], ((0, 0), (0, HEAD_PAD - QK_DIM), (0, 0)))
    w_ukv = big["w_ukv"]

    g_mix, g_qa, g_kva = small["mix_norm_g"], small["q_a_norm_g"], small["kv_a_norm_g"]
    g_qn, g_kn = _pad_gain(small["q_norm_g"], HEAD_PAD), _pad_gain(small["k_norm_g"], HEAD_PAD)
    g_out, g_ffn = small["hg_out_norm_g"], small["ffn_norm_g"]
    g_pg, g_post = small["ple_gate_norm_g"], small["ple_post_norm_g"]
    logits = small["hg_lb_logits"]
    lb = _lower_bound(logits)

    h, cq, ckv, kr, hq, hf, hi, hg, bg = _in_proj_fwd(x, g_mix, w_in, T, tw)
    qf, kf, vf = _mla_prep_fwd(cq, ckv, kr, pos, g_qa, g_kva, g_qn, g_kn, w_uq, w_ukv, T, tw)
    if late_blocks is None:
        attn, lse = _flash_fwd(qf, kf, vf, T)
    else:
        attn, lse, *late = _flash_fwd(qf, kf, vf, T, ag_blocks=[late_blocks[n] for n in LATE])
        big = {**big, **dict(zip(LATE, late))}
    o, s0 = _hgrn_fwd(hq, hf, hi, lb, T)
    w_branch = jnp.moveaxis(big["w_branch"].reshape(N_DEV, 2, HG_W, HEAD_PAD), 0, 2).reshape(2, HG_W, D_MODEL)
    w_bra = jnp.pad(w_branch[0].reshape(MLA_HEADS, V_DIM, D_MODEL),
                    ((0, 0), (0, HEAD_PAD - V_DIM), (0, 0))).reshape(MLA_HEADS * HEAD_PAD, D_MODEL)
    w_brb = w_branch[1]
    w_out = big["w_out"].reshape(D_MODEL, D_MODEL)
    w_g, w_u = big["w_ffn_gate"].reshape(FFN, D_MODEL), big["w_ffn_up"].reshape(FFN, D_MODEL)
    w_d = big["w_ffn_down"].reshape(FFN, D_MODEL)
    w_pg, w_pp = big["w_ple_gate"].reshape(D_MODEL, D_MODEL), big["w_ple_proj"]
    x1, ya, yb, m, rec = _merge_fwd(attn, o, hg, bg, x, g_out, w_bra, w_brb, w_out, T, tw)
    x2, gt, up, h2 = _ffn_fwd(x1, g_ffn, w_g, w_u, w_d, T, tw)
    dx2, loss_p, dg_post, dg_pg, d_pg, d_pp = _ple_loss(x2, p, tgt, g_pg, g_post, w_pg, w_pp, T, tw)

    grads, sibs, gots = {}, {}, {}
    dist = core is not None
    pick = lambda names: [grads[n] for n in names] if dist else ()

    def partials(tag, names, got):
        if not dist:
            return ()
        sibs.update(zip(names, got))
        return _chip_partials("rs_partial_" + tag, pick(names), got, core)

    dx1, a, dgt, dup, dg_ffn = _ffn_bwd(dx2, x1, gt, up, g_ffn, w_g, w_u, w_d, T, tm)
    grads["w_ffn_gate"] = _matmul_tn("dw_gate", dgt, h2).reshape(N_DEV, -1, D_MODEL)
    grads["w_ffn_up"] = _matmul_tn("dw_up", dup, h2).reshape(N_DEV, -1, D_MODEL)
    grads["w_ffn_down"] = _matmul_tn("dw_down", a, dx2).reshape(N_DEV, -1, D_MODEL)
    grads["w_ple_gate"] = d_pg.reshape(N_DEV, -1, D_MODEL)
    grads["w_ple_proj"] = d_pp

    dattn, do, dhg, dbg, dg_out, d_out, d_bra, d_brb, *sib_a = _merge_bwd(
        dx1, ya, yb, bg, o, hg, attn, m, rec, g_out, w_bra, w_brb, w_out, T, tm, xchg=(pick(GROUP_A), ()))
    parts_a = partials("a", GROUP_A, sib_a)
    d_bra = d_bra.reshape(N_DEV, MLA_HEADS, HEAD_PAD, HEAD_PAD)[:, :, :V_DIM].reshape(N_DEV, HG_W, HEAD_PAD)
    grads["w_branch"] = jnp.concatenate([d_bra, d_brb], axis=1)
    grads["w_out"] = d_out.reshape(N_DEV, -1, D_MODEL)

    dhq, dhf, dhi, dlb, *got = _hgrn_bwd(hq, hf, hi, do, s0, lb, T, xchg=(pick(GROUP_B), parts_a))
    sib_b, got_a = got[:len(GROUP_B)], got[len(GROUP_B):]
    parts_b = partials("b", GROUP_B, sib_b)
    dqf, dkf, dvf, *got_b = _flash_bwd(qf, kf, vf, attn, dattn, lse, T, xchg=((), parts_b))
    (dcq, dckv, dkr, dg_qa, dg_kva, dg_qn, dg_kn, d_uq, d_ukv) = _mla_prep_bwd(
        cq, ckv, kr, pos, dqf, dkf, dvf, g_qa, g_kva, g_qn, g_kn, w_uq, w_ukv, T, tw)
    grad_x, dproj, dg_mix = _in_proj_bwd(x, dx1, [dcq, dckv, dkr, dhq, dhf, dhi, dhg, dbg], g_mix, w_in, T, tw)
    grads["w_in"] = _matmul_tn_blocks("dw_in", h, dproj)
    grads["w_uq"] = d_uq[:, :QK_DIM]
    grads["w_ukv"] = d_ukv
    parts_c = ()
    if dist:
        parts_c = partials("c", GROUP_C, _exchange_sibling("rs_sibling_c", pick(GROUP_C)))
        gots.update(zip(GROUP_A, got_a))
        gots.update(zip(GROUP_B, got_b))

    dl0 = dlb * lb * (1.0 - lb)
    small_g = {
        "mix_norm_g": dg_mix, "q_a_norm_g": dg_qa, "kv_a_norm_g": dg_kva, "q_norm_g": dg_qn, "k_norm_g": dg_kn,
        "hg_lb_logits": jnp.concatenate([dl0, -dl0], axis=0), "hg_out_norm_g": dg_out,
        "ffn_norm_g": dg_ffn, "ple_gate_norm_g": dg_pg, "ple_post_norm_g": dg_post,
    }
    return loss_p, grad_x, small_g, grads, sibs, gots, parts_c


def _lower_bound(logits):
    def body(l_ref, lb_ref):
        l = l_ref[...]
        mx = jnp.max(l, axis=0, keepdims=True)
        e = jnp.exp(l - mx)
        lb_ref[...] = e[0:1] / jnp.sum(e, axis=0, keepdims=True)

    return pl.pallas_call(body, name="lower_bound", out_shape=jax.ShapeDtypeStruct((1, HG_W), F32))(logits)


def _my_place():
    return lax.axis_index("x"), lax.axis_index("y"), lax.axis_index("c")


def _all_gather(name, blocks):
    n = len(blocks)

    def body(*refs):
        x_refs, out_refs, sems = refs[:n], refs[n:2 * n], refs[2 * n:]
        _ag_start(x_refs, out_refs, sems)
        _ag_finish(x_refs, out_refs, sems)

    any_spec = pl.BlockSpec(memory_space=pl.ANY)
    return pl.pallas_call(
        body, name=name, out_shape=_ag_out_shapes(blocks),
        in_specs=[any_spec] * n, out_specs=[any_spec] * n, scratch_shapes=_ag_sems(n),
    )(*blocks)


def _ag_out_shapes(blocks):
    return [jax.ShapeDtypeStruct((N_DEV,) + b.shape, b.dtype) for b in blocks]


def _ag_sems(n):
    return [pltpu.SemaphoreType.DMA((7 * n,)), pltpu.SemaphoreType.DMA((7 * n,)), pltpu.SemaphoreType.DMA((n,))]


def _ag_parts(x_refs, out_refs, sems):
    send_sems, recv_sems, local_sems = sems
    x, y, c = _my_place()
    me, sibling = (x, y, c), (x, y, 1 - c)
    chips = [(1 - x, y), (x, 1 - y), (1 - x, 1 - y)]
    n = len(x_refs)

    def copy(a, k, block, to, own=False):
        px, py, pc = block
        dst = out_refs[a].at[4 * px + 2 * py + pc]
        return pltpu.make_async_remote_copy(
            src_ref=x_refs[a] if own else dst, dst_ref=dst, send_sem=send_sems.at[7 * a + k],
            recv_sem=recv_sems.at[7 * a + k], device_id=to, device_id_type=MESH_ID)

    mine = [pltpu.make_async_copy(x_refs[a], out_refs[a].at[4 * x + 2 * y + c], local_sems.at[a]) for a in range(n)]
    first = []
    for a in range(n):
        first.append(copy(a, 0, me, sibling, own=True))
        first += [copy(a, 1 + j, me, (*chip, c), own=True) for j, chip in enumerate(chips)]
    return copy, mine, first, me, sibling, chips, c, n


def _ag_start(x_refs, out_refs, sems):
    _, mine, first, *_ = _ag_parts(x_refs, out_refs, sems)
    for cp in mine + first:
        cp.start()


def _ag_finish(x_refs, out_refs, sems):
    copy, mine, first, me, sibling, chips, c, n = _ag_parts(x_refs, out_refs, sems)
    passed = []
    for j, chip in enumerate(chips):
        for a in range(n):
            copy(a, 1 + j, (*chip, c), me).wait_recv()
            passed.append(copy(a, 4 + j, (*chip, c), sibling))
            passed[-1].start()
    for a in range(n):
        copy(a, 0, sibling, me).wait_recv()
    for j, chip in enumerate(chips):
        for a in range(n):
            copy(a, 4 + j, (*chip, 1 - c), me).wait_recv()
    for cp in first + passed:
        cp.wait_send()
    for cp in mine:
        cp.wait()


def _exchange_sibling(name, gs):
    return _exchange(name, (gs, ()))


def _exchange(name, xchg, gather=()):
    n, n_ag = _x_count(xchg), len(gather)

    def body(*refs):
        in_refs, ag_in, refs = refs[:n], refs[n:n + n_ag], refs[n + n_ag:]
        out_refs, ag_out, refs = refs[:n], refs[n:n + n_ag], refs[n + n_ag:]
        sems, ag_sems = refs[:len(refs) - 3 * bool(n_ag)], refs[len(refs) - 3 * bool(n_ag):]
        if n_ag:
            _ag_start(ag_in, ag_out, ag_sems)
        for cp in _x_copies(len(xchg[0]), in_refs, out_refs, sems):
            cp.start()
        for cp in _x_copies(len(xchg[0]), in_refs, out_refs, sems):
            cp.wait()
        if n_ag:
            _ag_finish(ag_in, ag_out, ag_sems)

    any_spec = pl.BlockSpec(memory_space=pl.ANY)
    return pl.pallas_call(
        body, name=name, out_shape=_x_out_shapes(xchg) + _ag_out_shapes(gather),
        in_specs=[any_spec] * (n + n_ag), out_specs=[any_spec] * (n + n_ag),
        scratch_shapes=_x_sems(xchg) + (_ag_sems(n_ag) if n_ag else []),
    )(*xchg[0], *xchg[1], *gather)


N_PARTS = 4


def _part_spec(rows, cols, t_pos, lead_block=(), lead_index=lambda *args: ()):
    if rows % (16 * N_PARTS) == 0:
        axis, shape, count = 0, (rows // N_PARTS, cols), N_PARTS
    elif cols % (128 * N_PARTS) == 0:
        axis, shape, count = 1, (rows, cols // N_PARTS), N_PARTS
    else:
        axis, shape, count = 0, (rows, cols), 1

    def index(*args):
        i = jnp.minimum(args[t_pos], count - 1)
        return (*lead_index(*args), *((i, 0) if axis == 0 else (0, i)))

    return pl.BlockSpec((*lead_block, *shape), index)


def _chip_partials(name, gs, sibs, c_idx):
    n = len(gs)

    def body(c_ref, *refs):
        for g_ref, sib_ref, out_ref in zip(refs[:n], refs[n:2 * n], refs[2 * n:]):
            out_ref[...] = (g_ref[...] + sib_ref[...]).astype(MM)

    own = [_part_spec(*g.shape[1:], 1, (1,), lambda j, t, c_ref: (2 * j + c_ref[0],)) for g in gs]
    by_chip = [_part_spec(*g.shape[1:], 1, (1,), lambda j, t, c_ref: (j,)) for g in gs]
    grid_spec = pltpu.PrefetchScalarGridSpec(
        num_scalar_prefetch=1, grid=(4, N_PARTS), in_specs=own + by_chip, out_specs=by_chip)
    return pl.pallas_call(
        body, name=name, grid_spec=grid_spec, out_shape=[jax.ShapeDtypeStruct((4,) + g.shape[1:], MM) for g in gs],
        compiler_params=_cparams(("arbitrary", "arbitrary"), VMEM_LIMIT),
    )(c_idx, *gs, *sibs)


def _x_count(xchg):
    return len(xchg[0]) + len(xchg[1])


def _x_out_shapes(xchg):
    return ([jax.ShapeDtypeStruct((4,) + g.shape[1:], g.dtype) for g in xchg[0]]
            + [jax.ShapeDtypeStruct((3,) + p.shape[1:], p.dtype) for p in xchg[1]])


def _x_sems(xchg):
    n = 4 * len(xchg[0]) + 3 * len(xchg[1])
    return [pltpu.SemaphoreType.DMA((n,)), pltpu.SemaphoreType.DMA((n,))] if n else []


def _x_copies(n_sib, in_refs, out_refs, sems):
    if not in_refs:
        return []
    send_sems, recv_sems = sems
    x, y, c = _my_place()
    chips = [(1 - x, y), (x, 1 - y), (1 - x, 1 - y)]
    copies = []

    def add(src, dst, to):
        k = len(copies)
        copies.append(pltpu.make_async_remote_copy(
            src_ref=src, dst_ref=dst, send_sem=send_sems.at[k], recv_sem=recv_sems.at[k], device_id=to,
            device_id_type=MESH_ID))

    for a, (src, dst) in enumerate(zip(in_refs, out_refs)):
        if a < n_sib:
            for j in range(4):
                add(src.at[2 * j + 1 - c], dst.at[j], (x, y, 1 - c))
        else:
            for k, (px, py) in enumerate(chips):
                add(src.at[2 * px + py], dst.at[k], (px, py, c))
    return copies


def _adamw_math(w, g, m, v):
    m = ADAM_B1 * m + (1.0 - ADAM_B1) * g
    v = ADAM_B2 * v + (1.0 - ADAM_B2) * jnp.square(g)
    m_hat = m / (1.0 - ADAM_B1 ** ADAM_STEP)
    v_hat = v / (1.0 - ADAM_B2 ** ADAM_STEP)
    delta = -ADAM_LR * (m_hat / (jnp.sqrt(v_hat) + ADAM_EPS) + ADAM_WD * w)
    return delta, m, v


def _sum_adamws(name, gs, sibs, gots, ws, ms, vs, slot_idx, chip_idx):
    n = len(gs)

    def body(s_ref, j_ref, *refs):
        ins, outs = refs[:6 * n], refs[6 * n:]
        for a in range(n):
            g_ref, sib_ref, got_ref, w_ref, m_ref, v_ref = (ins[k * n + a] for k in range(6))
            go_ref, d_ref, m2_ref, v2_ref = outs[4 * a:4 * a + 4]
            grad = g_ref[0] + sib_ref[0]
            for k in range(3):
                grad = grad + got_ref[k].astype(F32)
            go_ref[...] = grad
            d_ref[...], m2_ref[...], v2_ref[...] = _adamw_math(w_ref[...], grad, m_ref[...], v_ref[...])

    shapes = [g.shape[1:] for g in gs]
    flat = [_part_spec(*s, 0) for s in shapes]
    in_specs = ([_part_spec(*s, 0, (1,), lambda t, s_ref, j_ref: (s_ref[0],)) for s in shapes]
                + [_part_spec(*s, 0, (1,), lambda t, s_ref, j_ref: (j_ref[0],)) for s in shapes]
                + [_part_spec(*s, 0, (3,), lambda t, s_ref, j_ref: (0,)) for s in shapes] + flat * 3)
    grid_spec = pltpu.PrefetchScalarGridSpec(
        num_scalar_prefetch=2, grid=(N_PARTS,), in_specs=in_specs, out_specs=[f for f in flat for _ in range(4)])
    res = pl.pallas_call(
        body, name=name, grid_spec=grid_spec,
        out_shape=[jax.ShapeDtypeStruct(s, F32) for s in shapes for _ in range(4)],
        compiler_params=_cparams(("arbitrary",), VMEM_LIMIT),
    )(slot_idx, chip_idx, *gs, *sibs, *gots, *ws, *ms, *vs)
    return [res[4 * a:4 * a + 4] for a in range(n)]


BIG = ("w_in", "w_uq", "w_ukv", "w_branch", "w_out", "w_ffn_gate", "w_ffn_up", "w_ffn_down", "w_ple_gate", "w_ple_proj")
SMALL = (
    ("mix_norm_g", 0, 1, 1024), ("q_a_norm_g", 1, 1, 384), ("kv_a_norm_g", 2, 1, 256), ("q_norm_g", 3, 1, 96),
    ("k_norm_g", 4, 1, 96), ("hg_lb_logits", 5, 2, 512), ("hg_out_norm_g", 7, 1, 128), ("ffn_norm_g", 8, 1, 1024),
    ("ple_gate_norm_g", 9, 1, 1024), ("ple_post_norm_g", 10, 1, 1024),
)
SLAB_ROWS, LOSS_ROW = 16, 15


def _pack_partials(small_g, loss_p):
    def body(*refs):
        val_refs, loss_ref, out_ref = refs[:len(SMALL)], refs[len(SMALL)], refs[len(SMALL) + 1]
        out_ref[...] = jnp.zeros_like(out_ref)
        for (_, r0, rows, cols), ref in zip(SMALL, val_refs):
            val = ref[...]
            if val.shape[0] != rows:
                val = jnp.sum(val, axis=0, keepdims=True)
            out_ref[r0:r0 + rows, :cols] = val[:, :cols]
        out_ref[LOSS_ROW:LOSS_ROW + 1, :HEAD_PAD] = jnp.full((1, HEAD_PAD), jnp.sum(loss_ref[...]), F32)

    return pl.pallas_call(
        body, name="pack_partials", out_shape=jax.ShapeDtypeStruct((SLAB_ROWS, D_MODEL), F32),
    )(*[small_g[n] for n, *_ in SMALL], loss_p)


def _adamw_small(parts, ws, ms, vs):
    n = len(SMALL)

    def body(p_ref, *refs):
        ins, loss_ref, outs = refs[:3 * n], refs[3 * n], refs[3 * n + 1:]
        total = p_ref[0]
        for d in range(1, N_DEV):
            total = total + p_ref[d]
        loss_ref[...] = total[LOSS_ROW:LOSS_ROW + 1, 0:1]
        for a, (_, r0, rows, cols) in enumerate(SMALL):
            g = total[r0:r0 + rows, :cols]
            outs[4 * a][...] = g
            outs[4 * a + 1][...], outs[4 * a + 2][...], outs[4 * a + 3][...] = _adamw_math(
                ins[a][...], g, ins[n + a][...], ins[2 * n + a][...])

    shapes = [jax.ShapeDtypeStruct((rows, cols), F32) for _, _, rows, cols in SMALL]
    res = pl.pallas_call(
        body, name="adamw_small", out_shape=[jax.ShapeDtypeStruct((1, 1), F32)] + [s for s in shapes for _ in range(4)],
    )(parts, *ws, *ms, *vs)
    return res[0], [res[1 + 4 * a:5 + 4 * a] for a in range(n)]


_WEIGHTS = ["mix_norm_g", "w_in", "q_a_norm_g", "w_uq", "kv_a_norm_g", "w_ukv", "q_norm_g", "k_norm_g", "hg_lb_logits",
            "hg_out_norm_g", "w_branch", "w_out", "ffn_norm_g", "w_ffn_gate", "w_ffn_up", "w_ffn_down",
            "ple_gate_norm_g", "w_ple_gate", "w_ple_proj", "ple_post_norm_g"]


def _step(x, p, positions, tgt, w, m, v):
    small_names = [n for n, *_ in SMALL]
    T = x.shape[1]
    px, py, pc = _my_place()
    as_idx = lambda t: jnp.reshape(t, (1,)).astype(jnp.int32)

    def two_d(n, t):
        t = t.reshape(-1, t.shape[-1])
        return t.T if n in TRANSPOSED else t

    def full_shape(n, t):
        return (t.T if n in TRANSPOSED else t).reshape(w[n].shape)

    blocks = {n: two_d(n, w[n]).astype(MM) for n in BIG}
    big = dict(zip(EARLY, _all_gather("ag_weights", [blocks[n] for n in EARLY])))
    small = {n: (w[n] if n == "hg_lb_logits" else w[n].reshape(1, -1)) for n in small_names}

    loss_p, grad_x, small_g, grads, sibs, gots, parts_c = _local_step(
        x[0], p[0, 0], positions.reshape(T, 1), tgt[0], small, big, late_blocks=blocks, core=as_idx(pc))

    *got_c, slabs = _exchange("rs_chips", ((), parts_c), gather=[_pack_partials(small_g, loss_p)])
    gots.update(zip(GROUP_C, got_c))
    out_g, out_d, out_m, out_v = {}, {}, {}, {}
    for tag, names in (("ab", GROUP_A + GROUP_B), ("c", GROUP_C)):
        pick = lambda table: [table[n] for n in names]
        res = _sum_adamws("adamw_" + tag, pick(grads), pick(sibs), pick(gots), [two_d(n, w[n]) for n in names],
                          [two_d(n, m[n]) for n in names], [two_d(n, v[n]) for n in names],
                          as_idx(4 * px + 2 * py + pc), as_idx(2 * px + py))
        for n, r in zip(names, res):
            out_g[n], out_d[n], out_m[n], out_v[n] = [full_shape(n, t) for t in r]

    loss, res = _adamw_small(slabs, *([t[n] for n in small_names] for t in (w, m, v)))
    for n, r in zip(small_names, res):
        out_g[n], out_d[n], out_m[n], out_v[n] = r

    outs = [loss.reshape(()), grad_x[None]]
    for table in (out_g, out_d, out_m, out_v):
        outs += [table[n] for n in _WEIGHTS]
    return tuple(outs)


def kernel(x, p, positions, mix_norm_g, w_in, q_a_norm_g, w_uq, kv_a_norm_g, w_ukv, q_norm_g, k_norm_g, hg_lb_logits, hg_out_norm_g, w_branch, w_out, ffn_norm_g, w_ffn_gate, w_ffn_up, w_ffn_down, ple_gate_norm_g, w_ple_gate, w_ple_proj, ple_post_norm_g, loss_target, m_mix_norm_g, m_w_in, m_q_a_norm_g, m_w_uq, m_kv_a_norm_g, m_w_ukv, m_q_norm_g, m_k_norm_g, m_hg_lb_logits, m_hg_out_norm_g, m_w_branch, m_w_out, m_ffn_norm_g, m_w_ffn_gate, m_w_ffn_up, m_w_ffn_down, m_ple_gate_norm_g, m_w_ple_gate, m_w_ple_proj, m_ple_post_norm_g, v_mix_norm_g, v_w_in, v_q_a_norm_g, v_w_uq, v_kv_a_norm_g, v_w_ukv, v_q_norm_g, v_k_norm_g, v_hg_lb_logits, v_hg_out_norm_g, v_w_branch, v_w_out, v_ffn_norm_g, v_w_ffn_gate, v_w_ffn_up, v_w_ffn_down, v_ple_gate_norm_g, v_w_ple_gate, v_w_ple_proj, v_ple_post_norm_g):
    w = dict(mix_norm_g=mix_norm_g, w_in=w_in, q_a_norm_g=q_a_norm_g, w_uq=w_uq, kv_a_norm_g=kv_a_norm_g, w_ukv=w_ukv,
             q_norm_g=q_norm_g, k_norm_g=k_norm_g, hg_lb_logits=hg_lb_logits, hg_out_norm_g=hg_out_norm_g,
             w_branch=w_branch, w_out=w_out, ffn_norm_g=ffn_norm_g, w_ffn_gate=w_ffn_gate, w_ffn_up=w_ffn_up,
             w_ffn_down=w_ffn_down, ple_gate_norm_g=ple_gate_norm_g, w_ple_gate=w_ple_gate, w_ple_proj=w_ple_proj,
             ple_post_norm_g=ple_post_norm_g)
    m = dict(mix_norm_g=m_mix_norm_g, w_in=m_w_in, q_a_norm_g=m_q_a_norm_g, w_uq=m_w_uq, kv_a_norm_g=m_kv_a_norm_g,
             w_ukv=m_w_ukv, q_norm_g=m_q_norm_g, k_norm_g=m_k_norm_g, hg_lb_logits=m_hg_lb_logits,
             hg_out_norm_g=m_hg_out_norm_g, w_branch=m_w_branch, w_out=m_w_out, ffn_norm_g=m_ffn_norm_g,
             w_ffn_gate=m_w_ffn_gate, w_ffn_up=m_w_ffn_up, w_ffn_down=m_w_ffn_down,
             ple_gate_norm_g=m_ple_gate_norm_g, w_ple_gate=m_w_ple_gate, w_ple_proj=m_w_ple_proj,
             ple_post_norm_g=m_ple_post_norm_g)
    v = dict(mix_norm_g=v_mix_norm_g, w_in=v_w_in, q_a_norm_g=v_q_a_norm_g, w_uq=v_w_uq, kv_a_norm_g=v_kv_a_norm_g,
             w_ukv=v_w_ukv, q_norm_g=v_q_norm_g, k_norm_g=v_k_norm_g, hg_lb_logits=v_hg_lb_logits,
             hg_out_norm_g=v_hg_out_norm_g, w_branch=v_w_branch, w_out=v_w_out, ffn_norm_g=v_ffn_norm_g,
             w_ffn_gate=v_w_ffn_gate, w_ffn_up=v_w_ffn_up, w_ffn_down=v_w_ffn_down,
             ple_gate_norm_g=v_ple_gate_norm_g, w_ple_gate=v_w_ple_gate, w_ple_proj=v_w_ple_proj,
             ple_post_norm_g=v_ple_post_norm_g)
    return _step(x, p, positions, loss_target, w, m, v)
```

```python
import jax
import jax.numpy as jnp
import numpy as np
from jax import lax
from jax.experimental import pallas as pl
from jax.experimental.pallas import tpu as pltpu

F32 = jnp.float32
MM = jnp.bfloat16
MESH_ID = pl.DeviceIdType.MESH

D_MODEL = 1024
N_DEV = 8
MLA_HEADS = 8
QK_NOPE = 64
QK_ROPE = 32
QK_DIM = 96
V_DIM = 64
HEAD_PAD = 128
Q_RANK = 384
KV_RANK = 256
ROPE_BASE = 10000.0
HG_HEADS = 4
HG_DIM = 128
HG_W = 512
HG_CHUNK = 64
FFN = 2816
PLE = 256
EPS = 1e-6
ATT_SCALE = QK_DIM ** -0.5
NEG = -1e30

ADAM_LR = 0.001
ADAM_B1 = 0.9
ADAM_B2 = 0.999
ADAM_EPS = 1e-08
ADAM_WD = 0.01
ADAM_STEP = 10

COL_SECTIONS = ((0, 384), (384, 256), (640, 32), (672, 512), (1184, 512), (1696, 512), (2208, 512), (2720, 2048))
STORED_WIDTHS = tuple(HEAD_PAD if n == QK_ROPE else n for _, n in COL_SECTIONS)
IN_COLS = 4768
IN_BLOCK = IN_COLS // N_DEV

VMEM_LIMIT = 58 * 1024 * 1024
WIDE_TILE = 512
ROW_TILE = 256
DW_TOKENS = 1024
ATT_TILE = 1024
ATT_HEADS = 4
HG_BLOCK = 512
HG_UNROLL = 4


def _dot(a, b):
    return jnp.dot(a.astype(MM), b.astype(MM), preferred_element_type=F32)


def _dot_nt(a, b):
    return lax.dot_general(a.astype(MM), b.astype(MM), (((1,), (1,)), ((), ())), preferred_element_type=F32)


def _dot_tn(a, b):
    return lax.dot_general(a.astype(MM), b.astype(MM), (((0,), (0,)), ((), ())), preferred_element_type=F32)


def _sigmoid(x):
    return 1.0 / (1.0 + jnp.exp(-x))


def _rms(x, n=None):
    n = x.shape[-1] if n is None else n
    r = lax.rsqrt(jnp.sum(x * x, axis=-1, keepdims=True) * (1.0 / n) + EPS)
    return x * r, r


def _rms_bwd(dxh, xh, r, n=None):
    n = xh.shape[-1] if n is None else n
    return r * (dxh - xh * (jnp.sum(dxh * xh, axis=-1, keepdims=True) * (1.0 / n)))


def _rope_tables(pos, tm):
    lane = lax.broadcasted_iota(jnp.int32, (tm, HEAD_PAD), 1)
    idx = jnp.where(lane < QK_NOPE + QK_ROPE // 2, lane - QK_NOPE, lane - QK_NOPE - QK_ROPE // 2)
    inv = jnp.exp(idx.astype(F32) * (-np.log(ROPE_BASE) * 2.0 / QK_ROPE))
    ang = pos.astype(F32) * inv
    in_rope = (lane >= QK_NOPE) & (lane < QK_DIM)
    first = lane < QK_NOPE + QK_ROPE // 2
    cos_t = jnp.where(in_rope, jnp.cos(ang), 1.0)
    sin_t = jnp.where(in_rope, jnp.where(first, -jnp.sin(ang), jnp.sin(ang)), 0.0)
    return cos_t, sin_t, (first, in_rope)


def _rope_swap(x, halves):
    first, in_rope = halves
    half = QK_ROPE // 2
    return jnp.where(in_rope, jnp.where(first, pltpu.roll(x, HEAD_PAD - half, 1), pltpu.roll(x, half, 1)), 0.0)


def _cparams(sem, vmem=None):
    return pltpu.CompilerParams(dimension_semantics=sem, vmem_limit_bytes=vmem)


def _row_call(name, body, T, tm, row_ins, full_ins, row_outs, acc_outs, vmem=None, scratch=(), xchg=((), ())):
    n_in, n_out, n_x = len(row_ins) + len(full_ins), len(row_outs) + len(acc_outs), _x_count(xchg)
    steps = T // tm

    def kern(*refs):
        ins, x_in, refs = refs[:n_in], refs[n_in:n_in + n_x], refs[n_in + n_x:]
        outs, x_out, refs = refs[:n_out], refs[n_out:n_out + n_x], refs[n_out + n_x:]
        scr, x_sems = refs[:len(scratch)], refs[len(scratch):]
        i = pl.program_id(0)
        if n_x:
            @pl.when(i == 0)
            def _():
                for cp in _x_copies(len(xchg[0]), x_in, x_out, x_sems):
                    cp.start()

        body(i, *ins, *outs, *scr)
        if n_x:
            @pl.when(i == steps - 1)
            def _():
                for cp in _x_copies(len(xchg[0]), x_in, x_out, x_sems):
                    cp.wait()

    any_spec = pl.BlockSpec(memory_space=pl.ANY)
    in_specs = [pl.BlockSpec((tm, a.shape[1]), lambda i: (i, 0)) for a in row_ins]
    in_specs += [pl.BlockSpec(a.shape, lambda i, nd=a.ndim: (0,) * nd, pipeline_mode=pl.Buffered(1)) for a in full_ins]
    out_specs = [pl.BlockSpec((tm, n), lambda i: (i, 0)) for n, _ in row_outs]
    out_specs += [pl.BlockSpec(s, lambda i, nd=len(s): (0,) * nd) for s, _ in acc_outs]
    out_shape = [jax.ShapeDtypeStruct((T, n), dt) for n, dt in row_outs]
    out_shape += [jax.ShapeDtypeStruct(s, dt) for s, dt in acc_outs]
    return pl.pallas_call(
        kern, name=name, grid=(steps,), in_specs=in_specs + [any_spec] * n_x, out_specs=out_specs + [any_spec] * n_x,
        out_shape=out_shape + _x_out_shapes(xchg), scratch_shapes=list(scratch) + _x_sems(xchg),
        compiler_params=_cparams(("arbitrary",), vmem),
    )(*row_ins, *full_ins, *xchg[0], *xchg[1])


FFN_HALVES = (slice(0, FFN // 2), slice(FFN // 2, FFN))
ROW_CHUNK = 16
CHUNK_UNROLL = True


def _by_chunks(tm, fn):
    def step(c, carry):
        fn(pl.ds(pl.multiple_of(c * ROW_CHUNK, ROW_CHUNK), ROW_CHUNK))
        return carry

    lax.fori_loop(0, tm // ROW_CHUNK, step, 0, unroll=CHUNK_UNROLL)


def _fold8(x):
    return x[:8] + x[8:]


def _acc(ref, i, val):
    @pl.when(i == 0)
    def _():
        ref[...] = val

    @pl.when(i != 0)
    def _():
        ref[...] += val


def _in_proj_fwd(x, g_mix, w_in, T, tm):
    def body(i, x_ref, g_ref, w_ref, h_ref, *rest):
        outs, pj_s = rest[:-1], rest[-1]
        g = g_ref[...]

        def norm(rows):
            h_ref[rows, :] = (_rms(x_ref[rows, :])[0] * g).astype(MM)

        _by_chunks(tm, norm)
        for d in range(N_DEV):
            pj_s[d] = _dot_nt(h_ref[...], w_ref[d])

        def join_and_cut(rows):
            proj = jnp.concatenate([pj_s[d, rows, :] for d in range(N_DEV)], axis=1)
            for (s, n), o_ref in zip(COL_SECTIONS, outs):
                if n == QK_ROPE:
                    o_ref[rows, :] = jnp.concatenate(
                        [jnp.zeros((ROW_CHUNK, QK_NOPE), F32), proj[:, s:s + n],
                         jnp.zeros((ROW_CHUNK, HEAD_PAD - QK_DIM), F32)], axis=1)
                else:
                    o_ref[rows, :] = proj[:, s:s + n]

        _by_chunks(tm, join_and_cut)

    row_outs = [(D_MODEL, MM)] + [(n, F32) for n in STORED_WIDTHS]
    return _row_call("in_proj_fwd", body, T, tm, [x], [g_mix, w_in], row_outs, [], VMEM_LIMIT,
                     scratch=[pltpu.VMEM((N_DEV, tm, IN_BLOCK), F32)])


def _mla_heads_fwd(raw, g_pad, cos_t, sin_t, first):
    outs, saved = [], []
    for h in range(MLA_HEADS):
        xh, r = _rms(raw[:, h * HEAD_PAD:(h + 1) * HEAD_PAD], QK_DIM)
        y = xh * g_pad
        outs.append(y * cos_t + _rope_swap(y, first) * sin_t)
        saved.append((xh, r))
    return outs, saved


def _mla_raw_heads(cqn, ckvn, kr, wuq_ref, wukv_ref, tm):
    lane = lax.broadcasted_iota(jnp.int32, (tm, HEAD_PAD), 1)
    nope = lane < QK_NOPE
    one_lane = jnp.where(lane == V_DIM, 1.0, 0.0)
    qs, ks, vs = [], [], []
    for h in range(MLA_HEADS):
        qs.append(_dot_nt(cqn, wuq_ref[h]))
        kv = _dot(ckvn, wukv_ref[h])
        ks.append(jnp.where(nope, kv, kr))
        vs.append(jnp.where(nope, pltpu.roll(kv, V_DIM, 1), one_lane))
    return jnp.concatenate(qs, axis=1), jnp.concatenate(ks, axis=1), jnp.concatenate(vs, axis=1)


def _mla_prep_fwd(cq, ckv, kr, pos, g_qa, g_kva, g_qn, g_kn, w_uq, w_ukv, T, tm):
    def body(i, cq_ref, ckv_ref, kr_ref, pos_ref, gqa_ref, gkva_ref, gqn_ref, gkn_ref, wuq_ref, wukv_ref,
             q_ref, k_ref, v_ref):
        cos_t, sin_t, first = _rope_tables(pos_ref[...], tm)
        cqn = _rms(cq_ref[...])[0] * gqa_ref[...]
        ckvn = _rms(ckv_ref[...])[0] * gkva_ref[...]
        q_raw, k_raw, v = _mla_raw_heads(cqn, ckvn, kr_ref[...], wuq_ref, wukv_ref, tm)
        qs, _ = _mla_heads_fwd(q_raw, gqn_ref[...], cos_t, sin_t, first)
        ks, _ = _mla_heads_fwd(k_raw, gkn_ref[...], cos_t, sin_t, first)
        q_ref[...] = (jnp.concatenate(qs, axis=1) * ATT_SCALE).astype(MM)
        k_ref[...] = jnp.concatenate(ks, axis=1).astype(MM)
        v_ref[...] = v.astype(MM)

    w = MLA_HEADS * HEAD_PAD
    return _row_call("mla_prep_fwd", body, T, tm, [cq, ckv, kr, pos], [g_qa, g_kva, g_qn, g_kn, w_uq, w_ukv],
                     [(w, MM), (w, MM), (w, MM)], [])


def _causal_pairs(n, by_query):
    if by_query:
        pairs = [(q, k) for q in range(n) for k in range(q + 1)]
    else:
        pairs = [(q, k) for k in range(n) for q in range(k, n)]
    return np.array([p[0] for p in pairs], np.int32), np.array([p[1] for p in pairs], np.int32)


def _flash_fwd(qf, kf, vf, T, ag_blocks=()):
    tq = min(ATT_TILE, T)
    nq = T // tq

    qi_tab, ki_tab = _causal_pairs(nq, by_query=True)

    hp = ATT_HEADS

    n_ag = len(ag_blocks)
    n_heads, n_pairs = MLA_HEADS // hp, len(qi_tab)

    def body(qi_ref, ki_ref, q_ref, k_ref, v_ref, *rest):
        ag_in, (o_ref, lse_ref), rest = rest[:n_ag], rest[n_ag:n_ag + 2], rest[n_ag + 2:]
        ag_out, (m_s, acc_s), ag_sems = rest[:n_ag], rest[n_ag:n_ag + 2], rest[n_ag + 2:]
        t = pl.program_id(1)
        qi, ki = qi_ref[t], ki_ref[t]
        if n_ag:
            @pl.when((pl.program_id(0) == 0) & (t == 0))
            def _():
                _ag_start(ag_in, ag_out, ag_sems)

        @pl.when(ki == 0)
        def _():
            m_s[...] = jnp.full_like(m_s, NEG)
            acc_s[...] = jnp.zeros_like(acc_s)

        def step(masked):
            halves = 2 if masked and tq % (2 * HEAD_PAD) == 0 else 1
            w = tq // halves
            for hh in range(hp):
                hs = slice(hh * HEAD_PAD, (hh + 1) * HEAD_PAD)
                for part in range(halves):
                    cols, nk = slice(part * w, (part + 1) * w), (part + 1) * w
                    s_t = _dot_nt(k_ref[:nk, hs], q_ref[cols, hs])
                    if masked:
                        key = lax.broadcasted_iota(jnp.int32, (nk, w), 0)
                        qry = lax.broadcasted_iota(jnp.int32, (nk, w), 1) + part * w
                        s_t = jnp.where(key <= qry, s_t, NEG)
                    m_old = m_s[hh, :, cols]
                    m_new = jnp.maximum(m_old, jnp.max(s_t, axis=0, keepdims=True))
                    p_t = jnp.exp(s_t - m_new)
                    acc_s[hh, :, cols] = jnp.exp(m_old - m_new) * acc_s[hh, :, cols] + _dot_tn(v_ref[:nk, hs], p_t)
                    m_s[hh, :, cols] = m_new

        @pl.when(ki < qi)
        def _():
            step(False)

        @pl.when(ki == qi)
        def _():
            step(True)
            real = lax.broadcasted_iota(jnp.int32, (HEAD_PAD, tq), 0) < V_DIM
            for hh in range(hp):
                hs = slice(hh * HEAD_PAD, (hh + 1) * HEAD_PAD)
                acc = acc_s[hh]
                l = acc[V_DIM:V_DIM + 1]
                o_ref[:, hs] = jnp.where(real, acc / l, 0.0).T
                lse_ref[:, hs] = jnp.broadcast_to(m_s[hh] + jnp.log(l), (HEAD_PAD, tq)).T

        if n_ag:
            @pl.when((pl.program_id(0) == n_heads - 1) & (t == n_pairs - 1))
            def _():
                _ag_finish(ag_in, ag_out, ag_sems)

    q_spec = pl.BlockSpec((tq, hp * HEAD_PAD), lambda h, t, qi_ref, ki_ref: (qi_ref[t], h))
    kv_spec = pl.BlockSpec((tq, hp * HEAD_PAD), lambda h, t, qi_ref, ki_ref: (ki_ref[t], h))
    any_spec = pl.BlockSpec(memory_space=pl.ANY)
    grid_spec = pltpu.PrefetchScalarGridSpec(
        num_scalar_prefetch=2, grid=(n_heads, n_pairs),
        in_specs=[q_spec, kv_spec, kv_spec] + [any_spec] * n_ag, out_specs=[q_spec, q_spec] + [any_spec] * n_ag,
        scratch_shapes=[pltpu.VMEM((hp, 1, tq), F32), pltpu.VMEM((hp, HEAD_PAD, tq), F32)]
        + (_ag_sems(n_ag) if n_ag else []))
    return pl.pallas_call(
        body, name="flash_fwd", grid_spec=grid_spec,
        out_shape=[jax.ShapeDtypeStruct((T, MLA_HEADS * HEAD_PAD), F32)] * 2 + _ag_out_shapes(ag_blocks),
        compiler_params=_cparams(("arbitrary", "arbitrary")),
    )(jnp.asarray(qi_tab), jnp.asarray(ki_tab), qf, kf, vf, *ag_blocks)


def _hg_gates(hf, lb):
    sg = _sigmoid(hf)
    f = lb + (1.0 - lb) * sg
    return sg, f, jnp.log(f), 1.0 - f


def _prefix_sum(x, reverse=False):
    n = x.shape[0]
    row = lax.broadcasted_iota(jnp.int32, x.shape, 0)
    step = 1
    while step < n:
        if reverse:
            x = x + jnp.where(row < n - step, pltpu.roll(x, n - step, 0), 0.0)
        else:
            x = x + jnp.where(row >= step, pltpu.roll(x, step, 0), 0.0)
        step *= 2
    return x


def _hg_levels():
    C = HG_CHUNK
    t = lax.broadcasted_iota(jnp.int32, (C, C), 0)
    s = lax.broadcasted_iota(jnp.int32, (C, C), 1)
    levels = []
    for shift in range(C.bit_length() - 2, -1, -1):
        pair_t, pair_s = lax.shift_right_logical(t, shift + 1), lax.shift_right_logical(s, shift + 1)
        later_t = (lax.shift_right_logical(t, shift) & 1) == 1
        earlier_s = (lax.shift_right_logical(s, shift) & 1) == 0
        levels.append((1 << shift, (pair_t == pair_s) & later_t & earlier_s))
    return levels, t == s


def _hg_refs(b):
    C, n = b.shape
    row = lax.broadcasted_iota(jnp.int32, (C, n), 0)
    back1, back2, ahead1 = pltpu.roll(b, 1, 0), pltpu.roll(b, 2, 0), pltpu.roll(b, C - 1, 0)
    refs = []
    for half in (32, 16, 8, 4):
        refs.append(jnp.concatenate(
            [jnp.broadcast_to(b[lo + half - 1:lo + half], (2 * half, n)) for lo in range(0, C, 2 * half)], axis=0))
    in4 = row & 3
    refs.append(jnp.where(in4 == 0, ahead1, jnp.where(in4 == 1, b, jnp.where(in4 == 2, back1, back2))))
    refs.append(jnp.where((row & 1) == 1, back1, b))
    return refs


def _hg_intra(q, k, b, refs, levels, eye):
    a = jnp.where(eye, jnp.sum(q * k, axis=1, keepdims=True), 0.0)
    saved = []
    for r, (_, mask) in zip(refs, levels):
        e = jnp.exp(-jnp.abs(b - r))
        q_t, k_t = q * e, k * e
        a = a + jnp.where(mask, _dot_nt(q_t, k_t), 0.0)
        saved.append((q_t, k_t, e))
    return a, saved


def _hg_intra_bwd(d_a, q, k, saved, levels, eye):
    diag = jnp.sum(jnp.where(eye, d_a, 0.0), axis=1, keepdims=True)
    dq, dk = diag * k, diag * q
    for (q_t, k_t, e), (_, mask) in zip(saved, levels):
        da = jnp.where(mask, d_a, 0.0)
        dq = dq + _dot(da, k_t) * e
        dk = dk + _dot_tn(da, q_t) * e
    return dq, dk


def _hgrn_fwd(hq, hf, hi, lb, T):
    rb = min(HG_BLOCK, T)
    ncb = rb // HG_CHUNK

    def body(hq_ref, hf_ref, hi_ref, lb_ref, o_ref, s0_ref, st_ref):
        @pl.when(pl.program_id(0) == 0)
        def _():
            st_ref[...] = jnp.zeros_like(st_ref)

        levels, eye = _hg_levels()

        def chunk(c, carry):
            rows = pl.ds(pl.multiple_of(c * HG_CHUNK, HG_CHUNK), HG_CHUNK)
            _, _, logf, kk = _hg_gates(hf_ref[rows, :], lb_ref[...])
            b = _prefix_sum(logf)
            refs = _hg_refs(b)
            q_all, v_all = hq_ref[rows, :], hi_ref[rows, :]
            outs = []
            for h in range(HG_HEADS):
                ls = slice(h * HG_DIM, (h + 1) * HG_DIM)
                q, k, v, bh = q_all[:, ls], kk[:, ls], v_all[:, ls], b[:, ls]
                st = st_ref[h]
                s0_ref[c, h * HG_DIM:(h + 1) * HG_DIM, :] = st
                b_end = bh[HG_CHUNK - 1:HG_CHUNK]
                a, _ = _hg_intra(q, k, bh, [r[:, ls] for r in refs], levels, eye)
                outs.append(_dot_nt(q * jnp.exp(bh), st) + _dot(a, v))
                st_ref[h] = st * jnp.exp(b_end) + _dot_tn(v, k * jnp.exp(b_end - bh))
            o_ref[rows, :] = jnp.concatenate(outs, axis=1)
            return carry

        lax.fori_loop(0, ncb, chunk, 0, unroll=HG_UNROLL)

    row = pl.BlockSpec((rb, HG_W), lambda i: (i, 0))
    return pl.pallas_call(
        body, name="hgrn_fwd", grid=(T // rb,),
        in_specs=[row, row, row, pl.BlockSpec((1, HG_W), lambda i: (0, 0))],
        out_specs=[row, pl.BlockSpec((ncb, HG_W, HG_DIM), lambda i: (i, 0, 0))],
        out_shape=[jax.ShapeDtypeStruct((T, HG_W), F32), jax.ShapeDtypeStruct((T // HG_CHUNK, HG_W, HG_DIM), F32)],
        scratch_shapes=[pltpu.VMEM((HG_HEADS, HG_DIM, HG_DIM), F32)],
        compiler_params=_cparams(("arbitrary",)),
    )(hq, hf, hi, lb)


def _hgrn_bwd(hq, hf, hi, do, s0, lb, T, xchg=((), ())):
    rb = min(HG_BLOCK, T)
    ncb = rb // HG_CHUNK
    nb = T // rb
    C = HG_CHUNK
    n_x, n_sib = _x_count(xchg), len(xchg[0])

    def body(hq_ref, hf_ref, hi_ref, do_ref, s0_ref, lb_ref, *rest):
        x_in, (dq_ref, df_ref, dv_ref, dlb_ref), rest = rest[:n_x], rest[n_x:n_x + 4], rest[n_x + 4:]
        x_out, dst_ref, x_sems = rest[:n_x], rest[n_x], rest[n_x + 1:]

        @pl.when(pl.program_id(0) == 0)
        def _():
            dst_ref[...] = jnp.zeros_like(dst_ref)
            dlb_ref[...] = jnp.zeros_like(dlb_ref)
            for cp in _x_copies(n_sib, x_in, x_out, x_sems):
                cp.start()

        row_cc = lax.broadcasted_iota(jnp.int32, (C, C), 0)
        col_cc = lax.broadcasted_iota(jnp.int32, (C, C), 1)
        last_row = lax.broadcasted_iota(jnp.int32, (C, HG_DIM), 0) == C - 1
        lb_v = lb_ref[...]
        levels, eye = _hg_levels()

        def chunk(cc, carry):
            c = ncb - 1 - cc
            rows = pl.ds(pl.multiple_of(c * C, C), C)
            hf_c = hf_ref[rows, :]
            sg, f, logf, kk = _hg_gates(hf_c, lb_v)
            b = _prefix_sum(logf)
            refs = _hg_refs(b)
            q_all, v_all, do_all = hq_ref[rows, :], hi_ref[rows, :], do_ref[rows, :]
            dq_o, dk_o, dv_o, db_o = [], [], [], []
            for h in range(HG_HEADS):
                ls = slice(h * HG_DIM, (h + 1) * HG_DIM)
                q, k, v, bh, d_o = q_all[:, ls], kk[:, ls], v_all[:, ls], b[:, ls], do_all[:, ls]
                st0 = s0_ref[c, h * HG_DIM:(h + 1) * HG_DIM, :]
                dst = dst_ref[h]
                b_end = bh[C - 1:C]
                e_b, e_end = jnp.exp(bh), jnp.exp(b_end)
                e_rem = jnp.exp(b_end - bh)
                qe, kd = q * e_b, k * e_rem
                st_end = st0 * e_end + _dot_tn(v, kd)
                a, saved = _hg_intra(q, k, bh, [r[:, ls] for r in refs], levels, eye)
                d_a = jnp.where(col_cc <= row_cc, _dot_nt(d_o, v), 0.0)
                dq_i, dk_i = _hg_intra_bwd(d_a, q, k, saved, levels, eye)
                dv = _dot_tn(a, d_o) + _dot_nt(kd, dst)
                dq = e_b * _dot(d_o, st0) + dq_i
                dk = e_rem * _dot(v, dst) + dk_i
                extra = jnp.sum(dst * st_end, axis=0, keepdims=True)
                db_o.append(q * dq - k * dk + jnp.where(last_row, extra, 0.0))
                dst_ref[h] = dst * e_end + _dot_tn(d_o, qe)
                dq_o.append(dq)
                dk_o.append(dk)
                dv_o.append(dv)
            dlogf = _prefix_sum(jnp.concatenate(db_o, axis=1), reverse=True)
            d_f = dlogf / f - jnp.concatenate(dk_o, axis=1)
            dq_ref[rows, :] = jnp.concatenate(dq_o, axis=1).astype(MM)
            dv_ref[rows, :] = jnp.concatenate(dv_o, axis=1).astype(MM)
            df_ref[rows, :] = (d_f * (1.0 - lb_v) * sg * (1.0 - sg)).astype(MM)
            dlb_ref[...] += jnp.sum(d_f * (1.0 - sg), axis=0, keepdims=True)
            return carry

        lax.fori_loop(0, ncb, chunk, 0, unroll=HG_UNROLL)

        if n_x:
            @pl.when(pl.program_id(0) == nb - 1)
            def _():
                for cp in _x_copies(n_sib, x_in, x_out, x_sems):
                    cp.wait()

    row = pl.BlockSpec((rb, HG_W), lambda i: (nb - 1 - i, 0))
    one = pl.BlockSpec((1, HG_W), lambda i: (0, 0))
    any_spec = pl.BlockSpec(memory_space=pl.ANY)
    return pl.pallas_call(
        body, name="hgrn_bwd", grid=(nb,),
        in_specs=[row, row, row, row, pl.BlockSpec((ncb, HG_W, HG_DIM), lambda i: (nb - 1 - i, 0, 0)), one]
        + [any_spec] * n_x,
        out_specs=[row, row, row, one] + [any_spec] * n_x,
        out_shape=[jax.ShapeDtypeStruct((T, HG_W), MM)] * 3 + [jax.ShapeDtypeStruct((1, HG_W), F32)]
        + _x_out_shapes(xchg),
        scratch_shapes=[pltpu.VMEM((HG_HEADS, HG_DIM, HG_DIM), F32)] + _x_sems(xchg),
        compiler_params=_cparams(("arbitrary",)),
    )(hq, hf, hi, do, s0, lb, *xchg[0], *xchg[1])


def _silu_parts(x):
    sg = _sigmoid(x)
    return x * sg, sg * (1.0 + x * (1.0 - sg))


def _merge_fwd(attn, o, hg, bg, x, g_out, w_bra, w_brb, w_out, T, tm):
    def body(i, attn_ref, o_ref, hg_ref, bg_ref, x_ref, g_ref, wa_ref, wb_ref, wo_ref,
             x1_ref, ya_ref, yb_ref, m_ref, rec_ref):
        g = g_ref[...]

        def recurrent_out(rows):
            for h in range(HG_HEADS):
                ls = slice(h * HG_DIM, (h + 1) * HG_DIM)
                rec_ref[rows, ls] = (_rms(o_ref[rows, ls])[0] * g * _silu_parts(hg_ref[rows, ls])[0]).astype(MM)

        _by_chunks(tm, recurrent_out)
        ya_ref[...] = _dot(attn_ref[...], wa_ref[...])
        yb_ref[...] = jnp.dot(rec_ref[...], wb_ref[...], preferred_element_type=F32)

        def gate(rows):
            m_ref[rows, :] = (_sigmoid(bg_ref[rows, :D_MODEL]) * ya_ref[rows, :]
                              + _sigmoid(bg_ref[rows, D_MODEL:]) * yb_ref[rows, :]).astype(MM)

        _by_chunks(tm, gate)
        x1_ref[...] = x_ref[...] + jnp.dot(m_ref[...], wo_ref[...], preferred_element_type=F32)

    return _row_call("merge_fwd", body, T, tm, [attn, o, hg, bg, x], [g_out, w_bra, w_brb, w_out],
                     [(D_MODEL, F32), (D_MODEL, F32), (D_MODEL, F32), (D_MODEL, MM), (HG_W, MM)], [], VMEM_LIMIT)


def _ffn_fwd(x1, g_ffn, w_g, w_u, w_d, T, tm):
    def body(i, x1_ref, g_ref, wg_ref, wu_ref, wd_ref, x2_ref, gt_ref, up_ref, h2_ref, a_s):
        g = g_ref[...]

        def norm(rows):
            h2_ref[rows, :] = (_rms(x1_ref[rows, :])[0] * g).astype(MM)

        _by_chunks(tm, norm)
        gt_ref[...] = _dot_nt(h2_ref[...], wg_ref[...])
        up_ref[...] = _dot_nt(h2_ref[...], wu_ref[...])

        def act(rows):
            for cs in FFN_HALVES:
                a_s[rows, cs] = (_silu_parts(gt_ref[rows, cs])[0] * up_ref[rows, cs]).astype(MM)

        _by_chunks(tm, act)
        x2_ref[...] = x1_ref[...] + jnp.dot(a_s[...], wd_ref[...], preferred_element_type=F32)

    return _row_call("ffn_fwd", body, T, tm, [x1], [g_ffn, w_g, w_u, w_d],
                     [(D_MODEL, F32), (FFN, F32), (FFN, F32), (D_MODEL, MM)], [], VMEM_LIMIT,
                     scratch=[pltpu.VMEM((tm, FFN), MM)])


def _ple_loss(x2, p, tgt, g_pg, g_post, w_pg, w_pp, T, tm):
    def body(i, x2_ref, p_ref, t_ref, gpg_ref, gpo_ref, wpg_ref, wpp_ref,
             dx2_ref, loss_ref, dgpo_ref, dgpg_ref, dwpg_ref, dwpp_ref, u_s, n3_s, z_s, dz_s, du_s, dy_s, dn3_s):
        @pl.when(i == 0)
        def _():
            for ref in (loss_ref, dgpo_ref, dgpg_ref, dwpg_ref, dwpp_ref):
                ref[...] = jnp.zeros_like(ref)

        gpg, gpo = gpg_ref[...], gpo_ref[...]
        p_mm = p_ref[...].astype(MM)
        for d in range(N_DEV):
            u_s[:, d * HEAD_PAD:(d + 1) * HEAD_PAD] = jnp.dot(p_mm, wpp_ref[d], preferred_element_type=F32)

        def gate_input(rows):
            n3_s[rows, :] = (_rms(x2_ref[rows, :])[0] * gpg).astype(MM)

        _by_chunks(tm, gate_input)
        z_s[...] = jnp.dot(n3_s[...], wpg_ref[...], preferred_element_type=F32)

        def loss_and_back(rows):
            uh, ru = _rms(u_s[rows, :])
            e = uh * gpo
            gate = _sigmoid(z_s[rows, :])
            diff = x2_ref[rows, :] + gate * e - t_ref[rows, :]
            dy = diff * (1.0 / D_MODEL)
            de = dy * gate
            dz_s[rows, :] = (dy * e * gate * (1.0 - gate)).astype(MM)
            du_s[rows, :] = _rms_bwd(de * gpo, uh, ru).astype(MM)
            dy_s[rows, :] = dy
            loss_ref[...] += _fold8(diff * diff) * (0.5 / D_MODEL)
            dgpo_ref[...] += _fold8(de * uh)

        _by_chunks(tm, loss_and_back)
        dn3_s[...] = _dot_nt(dz_s[...], wpg_ref[...])

        def gate_norm_back(rows):
            x2h, r3 = _rms(x2_ref[rows, :])
            dn3 = dn3_s[rows, :]
            dx2_ref[rows, :] = dy_s[rows, :] + _rms_bwd(dn3 * gpg, x2h, r3)
            dgpg_ref[...] += _fold8(dn3 * x2h)

        _by_chunks(tm, gate_norm_back)
        dwpg_ref[...] += _dot_tn(n3_s[...], dz_s[...])
        for d in range(N_DEV):
            dwpp_ref[d] += _dot_tn(p_mm, du_s[:, d * HEAD_PAD:(d + 1) * HEAD_PAD])

    vec = ((8, D_MODEL), F32)
    wide = lambda dt: pltpu.VMEM((tm, D_MODEL), dt)
    return _row_call("ple_loss", body, T, tm, [x2, p, tgt], [g_pg, g_post, w_pg, w_pp], [(D_MODEL, F32)],
                     [vec, vec, vec, ((D_MODEL, D_MODEL), F32), ((N_DEV, PLE, HEAD_PAD), F32)], VMEM_LIMIT,
                     scratch=[wide(F32), wide(MM), wide(F32), wide(MM), wide(MM), wide(F32), wide(F32)])


def _ffn_bwd(dx2, x1, gt, up, g_ffn, w_g, w_u, w_d, T, tm):
    def body(i, dx2_ref, x1_ref, gt_ref, up_ref, g_ref, wg_ref, wu_ref, wd_ref,
             dx1_ref, a_ref, dgt_ref, dup_ref, dg_ref, da_s, dh2_s):
        @pl.when(i == 0)
        def _():
            dg_ref[...] = jnp.zeros_like(dg_ref)

        g = g_ref[...]
        da_s[...] = _dot_nt(dx2_ref[...], wd_ref[...])

        def act_back(rows):
            for cs in FFN_HALVES:
                up, da = up_ref[rows, cs], da_s[rows, cs]
                silu, dsilu = _silu_parts(gt_ref[rows, cs])
                dgt_ref[rows, cs] = (da * up * dsilu).astype(MM)
                dup_ref[rows, cs] = (da * silu).astype(MM)
                a_ref[rows, cs] = (silu * up).astype(MM)

        _by_chunks(tm, act_back)
        dh2_s[...] = (jnp.dot(dgt_ref[...], wg_ref[...], preferred_element_type=F32)
                      + jnp.dot(dup_ref[...], wu_ref[...], preferred_element_type=F32))

        def norm_back(rows):
            x1h, r = _rms(x1_ref[rows, :])
            dh2 = dh2_s[rows, :]
            dx1_ref[rows, :] = dx2_ref[rows, :] + _rms_bwd(dh2 * g, x1h, r)
            dg_ref[...] += _fold8(dh2 * x1h)

        _by_chunks(tm, norm_back)

    return _row_call("ffn_bwd", body, T, tm, [dx2, x1, gt, up], [g_ffn, w_g, w_u, w_d],
                     [(D_MODEL, F32), (FFN, MM), (FFN, MM), (FFN, MM)], [((8, D_MODEL), F32)], VMEM_LIMIT,
                     scratch=[pltpu.VMEM((tm, FFN), F32), pltpu.VMEM((tm, D_MODEL), F32)])


def _merge_bwd(dx1, ya, yb, bg, o, hg, attn, m, rec, g_out, w_bra, w_brb, w_out, T, tm, xchg=((), ())):
    def body(i, dx1_ref, ya_ref, yb_ref, bg_ref, o_ref, hg_ref, attn_ref, m_ref, rec_ref, g_ref, wa_ref, wb_ref, wo_ref,
             dattn_ref, do_ref, dhg_ref, dbg_ref, dg_ref, dwo_ref, dwa_ref, dwb_ref, dm_s, dya_s, dyb_s, drec_s):
        @pl.when(i == 0)
        def _():
            for ref in (dg_ref, dwo_ref, dwa_ref, dwb_ref):
                ref[...] = jnp.zeros_like(ref)

        g = g_ref[...]
        dx1 = dx1_ref[...].astype(MM)
        dm_s[...] = _dot_nt(dx1, wo_ref[...])

        def gate_back(rows):
            dm = dm_s[rows, :]
            ga, gb = _sigmoid(bg_ref[rows, :D_MODEL]), _sigmoid(bg_ref[rows, D_MODEL:])
            dya_s[rows, :] = (dm * ga).astype(MM)
            dyb_s[rows, :] = (dm * gb).astype(MM)
            dbg_ref[rows, :D_MODEL] = (dm * ya_ref[rows, :] * ga * (1.0 - ga)).astype(MM)
            dbg_ref[rows, D_MODEL:] = (dm * yb_ref[rows, :] * gb * (1.0 - gb)).astype(MM)

        _by_chunks(tm, gate_back)
        dwo_ref[...] += _dot_tn(m_ref[...], dx1)
        attn_mm = attn_ref[...].astype(MM)
        for d in range(N_DEV):
            ds = slice(d * HEAD_PAD, (d + 1) * HEAD_PAD)
            dwa_ref[d] += _dot_tn(attn_mm, dya_s[:, ds])
            dwb_ref[d] += _dot_tn(rec_ref[...], dyb_s[:, ds])
        dattn_ref[...] = _dot_nt(dya_s[...], wa_ref[...])
        drec_s[...] = _dot_nt(dyb_s[...], wb_ref[...])

        def recurrent_out_back(rows):
            for h in range(HG_HEADS):
                ls = slice(h * HG_DIM, (h + 1) * HG_DIM)
                oh, r = _rms(o_ref[rows, ls])
                silu, dsilu = _silu_parts(hg_ref[rows, ls])
                dr = drec_s[rows, ls]
                dhg_ref[rows, ls] = (dr * oh * g * dsilu).astype(MM)
                don = dr * silu
                dg_ref[...] += _fold8(don * oh)
                do_ref[rows, ls] = _rms_bwd(don * g, oh, r)

        _by_chunks(tm, recurrent_out_back)

    wide = lambda n, dt: pltpu.VMEM((tm, n), dt)
    return _row_call("merge_bwd", body, T, tm, [dx1, ya, yb, bg, o, hg, attn, m, rec], [g_out, w_bra, w_brb, w_out],
                     [(D_MODEL, F32), (HG_W, F32), (HG_W, MM), (2 * D_MODEL, MM)],
                     [((8, HG_DIM), F32), ((D_MODEL, D_MODEL), F32), ((N_DEV, MLA_HEADS * HEAD_PAD, HEAD_PAD), F32),
                      ((N_DEV, HG_W, HEAD_PAD), F32)], VMEM_LIMIT,
                     scratch=[wide(D_MODEL, F32), wide(D_MODEL, MM), wide(D_MODEL, MM), wide(HG_W, F32)], xchg=xchg)


def _flash_bwd(qf, kf, vf, o, do, lse, T, xchg=((), ())):
    tq = min(ATT_TILE, T)
    nq = T // tq

    qi_tab, ki_tab = _causal_pairs(nq, by_query=False)

    n_x, n_sib = _x_count(xchg), len(xchg[0])
    hp = ATT_HEADS
    n_heads, n_pairs = MLA_HEADS // hp, len(qi_tab)

    def body(qi_ref, ki_ref, q_ref, k_ref, v_ref, o_ref, do_ref, lse_ref, *rest):
        x_in, (dq_ref, dk_ref, dv_ref), rest = rest[:n_x], rest[n_x:n_x + 3], rest[n_x + 3:]
        x_out, x_sems = rest[:n_x], rest[n_x:]
        t = pl.program_id(1)
        qi, ki = qi_ref[t], ki_ref[t]
        if n_x:
            @pl.when((pl.program_id(0) == 0) & (t == 0))
            def _():
                for cp in _x_copies(n_sib, x_in, x_out, x_sems):
                    cp.start()

        @pl.when(t == 0)
        def _():
            dq_ref[...] = jnp.zeros_like(dq_ref)

        def step(first):
            halves = 2 if first and tq % (2 * HEAD_PAD) == 0 else 1
            w = tq // halves
            for hh in range(hp):
                hs = slice(hh * HEAD_PAD, (hh + 1) * HEAD_PAD)
                for part in range(halves):
                    keys, qs = slice(part * w, (part + 1) * w), slice(part * w, tq)
                    nq_ = tq - part * w
                    q, k, d_o = q_ref[qs, hs], k_ref[keys, hs], do_ref[qs, hs]
                    s = _dot_nt(q, k)
                    if first:
                        row = lax.broadcasted_iota(jnp.int32, (nq_, w), 0)
                        col = lax.broadcasted_iota(jnp.int32, (nq_, w), 1)
                        s = jnp.where(col <= row, s, NEG)
                    p = jnp.exp(s - lse_ref[qs, hh * HEAD_PAD:hh * HEAD_PAD + 1])
                    delta = jnp.sum(d_o * o_ref[qs, hs], axis=1, keepdims=True)
                    ds = p * (_dot_nt(d_o, v_ref[keys, hs]) - delta)
                    rows = pl.ds(pl.multiple_of(qi * tq + part * w, w), nq_)
                    dq_ref[rows, hs] += _dot(ds, k)
                    if first:
                        dv_ref[keys, hs] = _dot_tn(p, d_o)
                        dk_ref[keys, hs] = _dot_tn(ds, q)
                    else:
                        dv_ref[keys, hs] += _dot_tn(p, d_o)
                        dk_ref[keys, hs] += _dot_tn(ds, q)

        @pl.when(qi == ki)
        def _():
            step(True)

        @pl.when(qi > ki)
        def _():
            step(False)

        if n_x:
            @pl.when((pl.program_id(0) == n_heads - 1) & (t == n_pairs - 1))
            def _():
                for cp in _x_copies(n_sib, x_in, x_out, x_sems):
                    cp.wait()

    q_spec = pl.BlockSpec((tq, hp * HEAD_PAD), lambda h, t, qi_ref, ki_ref: (qi_ref[t], h))
    kv_spec = pl.BlockSpec((tq, hp * HEAD_PAD), lambda h, t, qi_ref, ki_ref: (ki_ref[t], h))
    any_spec = pl.BlockSpec(memory_space=pl.ANY)
    w = MLA_HEADS * HEAD_PAD
    grid_spec = pltpu.PrefetchScalarGridSpec(
        num_scalar_prefetch=2, grid=(n_heads, n_pairs),
        in_specs=[q_spec, kv_spec, kv_spec, q_spec, q_spec, q_spec] + [any_spec] * n_x,
        out_specs=[pl.BlockSpec((T, hp * HEAD_PAD), lambda h, t, qi_ref, ki_ref: (0, h)), kv_spec, kv_spec]
        + [any_spec] * n_x,
        scratch_shapes=_x_sems(xchg))
    return pl.pallas_call(
        body, name="flash_bwd", grid_spec=grid_spec,
        out_shape=[jax.ShapeDtypeStruct((T, w), F32)] * 3 + _x_out_shapes(xchg),
        compiler_params=_cparams(("arbitrary", "arbitrary")),
    )(jnp.asarray(qi_tab), jnp.asarray(ki_tab), qf, kf, vf, o, do, lse, *xchg[0], *xchg[1])


def _mla_heads_bwd(d_out, saved, g_pad, cos_t, sin_t, first):
    d_raw, dg = [], jnp.zeros((1, HEAD_PAD), F32)
    for h in range(MLA_HEADS):
        xh, r = saved[h]
        dy = d_out[:, h * HEAD_PAD:(h + 1) * HEAD_PAD]
        dn = dy * cos_t + _rope_swap(dy * sin_t, first)
        dg = dg + jnp.sum(dn * xh, axis=0, keepdims=True)
        d_raw.append(_rms_bwd(dn * g_pad, xh, r, QK_DIM))
    return d_raw, dg


def _mla_prep_bwd(cq, ckv, kr, pos, dqf, dkf, dvf, g_qa, g_kva, g_qn, g_kn, w_uq, w_ukv, T, tm):
    def body(i, cq_ref, ckv_ref, kr_ref, pos_ref, dq_ref, dk_ref, dv_ref,
             gqa_ref, gkva_ref, gqn_ref, gkn_ref, wuq_ref, wukv_ref,
             dcq_ref, dckv_ref, dkr_ref, dgqa_ref, dgkva_ref, dgqn_ref, dgkn_ref, dwuq_ref, dwukv_ref):
        cos_t, sin_t, first = _rope_tables(pos_ref[...], tm)
        cqh, rq = _rms(cq_ref[...])
        ckvh, rkv = _rms(ckv_ref[...])
        cqn, ckvn = cqh * gqa_ref[...], ckvh * gkva_ref[...]
        q_raw, k_raw, _ = _mla_raw_heads(cqn, ckvn, kr_ref[...], wuq_ref, wukv_ref, tm)
        _, q_saved = _mla_heads_fwd(q_raw, gqn_ref[...], cos_t, sin_t, first)
        _, k_saved = _mla_heads_fwd(k_raw, gkn_ref[...], cos_t, sin_t, first)
        dq_heads, dgqn = _mla_heads_bwd(dq_ref[...] * ATT_SCALE, q_saved, gqn_ref[...], cos_t, sin_t, first)
        dk_heads, dgkn = _mla_heads_bwd(dk_ref[...], k_saved, gkn_ref[...], cos_t, sin_t, first)
        lane = lax.broadcasted_iota(jnp.int32, (tm, HEAD_PAD), 1)
        nope = lane < QK_NOPE
        dcqn = jnp.zeros((tm, Q_RANK), F32)
        dckvn = jnp.zeros((tm, KV_RANK), F32)
        dkr = jnp.zeros((tm, HEAD_PAD), F32)
        cqn_mm, ckvn_mm = cqn.astype(MM), ckvn.astype(MM)
        for h in range(MLA_HEADS):
            hs = slice(h * HEAD_PAD, (h + 1) * HEAD_PAD)
            dq_h = dq_heads[h].astype(MM)
            dkv_h = jnp.where(nope, dk_heads[h], pltpu.roll(dv_ref[:, hs], V_DIM, 1)).astype(MM)
            _acc(dwuq_ref.at[h], i, _dot_tn(dq_h, cqn_mm))
            _acc(dwukv_ref.at[h], i, _dot_tn(ckvn_mm, dkv_h))
            dcqn = dcqn + jnp.dot(dq_h, wuq_ref[h], preferred_element_type=F32)
            dckvn = dckvn + lax.dot_general(dkv_h, wukv_ref[h], (((1,), (1,)), ((), ())), preferred_element_type=F32)
            dkr = dkr + dk_heads[h]
        dkr_ref[...] = jnp.where((lane >= QK_NOPE) & (lane < QK_DIM), dkr, 0.0).astype(MM)
        dcq_ref[...] = _rms_bwd(dcqn * gqa_ref[...], cqh, rq).astype(MM)
        dckv_ref[...] = _rms_bwd(dckvn * gkva_ref[...], ckvh, rkv).astype(MM)
        _acc(dgqa_ref, i, jnp.sum(dcqn * cqh, axis=0, keepdims=True))
        _acc(dgkva_ref, i, jnp.sum(dckvn * ckvh, axis=0, keepdims=True))
        _acc(dgqn_ref, i, dgqn)
        _acc(dgkn_ref, i, dgkn)

    return _row_call(
        "mla_prep_bwd", body, T, tm, [cq, ckv, kr, pos, dqf, dkf, dvf], [g_qa, g_kva, g_qn, g_kn, w_uq, w_ukv],
        [(Q_RANK, MM), (KV_RANK, MM), (HEAD_PAD, MM)],
        [((1, Q_RANK), F32), ((1, KV_RANK), F32), ((1, HEAD_PAD), F32), ((1, HEAD_PAD), F32),
         ((MLA_HEADS, HEAD_PAD, Q_RANK), F32), ((MLA_HEADS, KV_RANK, HEAD_PAD), F32)], VMEM_LIMIT)


def _in_proj_bwd(x, dx1, dsecs, g_mix, w_in, T, tm):
    def body(i, x_ref, dx1_ref, *rest):
        d_refs, (g_ref, w_ref, dx_ref, dp_ref, dg_ref, dh_s) = rest[:len(COL_SECTIONS)], rest[len(COL_SECTIONS):]

        @pl.when(i == 0)
        def _():
            dg_ref[...] = jnp.zeros_like(dg_ref)

        g = g_ref[...]

        def join_and_cut(rows):
            pieces = [(d_ref[rows, QK_NOPE:QK_DIM] if n == QK_ROPE else d_ref[rows, :]).astype(F32)
                      for (_, n), d_ref in zip(COL_SECTIONS, d_refs)]
            dproj = jnp.concatenate(pieces, axis=1)
            for d in range(N_DEV):
                dp_ref[d, rows, :] = dproj[:, d * IN_BLOCK:(d + 1) * IN_BLOCK].astype(MM)

        _by_chunks(tm, join_and_cut)
        dh = jnp.dot(dp_ref[0], w_ref[0], preferred_element_type=F32)
        for d in range(1, N_DEV):
            dh = dh + jnp.dot(dp_ref[d], w_ref[d], preferred_element_type=F32)
        dh_s[...] = dh

        def norm_back(rows):
            xh, r = _rms(x_ref[rows, :])
            dh_c = dh_s[rows, :]
            dx_ref[rows, :] = dx1_ref[rows, :] + _rms_bwd(dh_c * g, xh, r)
            dg_ref[...] += _fold8(dh_c * xh)

        _by_chunks(tm, norm_back)

    in_specs = [pl.BlockSpec((tm, a.shape[1]), lambda i: (i, 0)) for a in [x, dx1, *dsecs]]
    in_specs += [pl.BlockSpec(g_mix.shape, lambda i: (0, 0)),
                 pl.BlockSpec(w_in.shape, lambda i: (0, 0, 0), pipeline_mode=pl.Buffered(1))]

    def kern(*refs):
        body(pl.program_id(0), *refs)

    return pl.pallas_call(
        kern, name="in_proj_bwd", grid=(T // tm,), in_specs=in_specs,
        out_specs=[pl.BlockSpec((tm, D_MODEL), lambda i: (i, 0)),
                   pl.BlockSpec((N_DEV, tm, IN_BLOCK), lambda i: (0, i, 0)),
                   pl.BlockSpec((8, D_MODEL), lambda i: (0, 0))],
        out_shape=[jax.ShapeDtypeStruct((T, D_MODEL), F32), jax.ShapeDtypeStruct((N_DEV, T, IN_BLOCK), MM),
                   jax.ShapeDtypeStruct((8, D_MODEL), F32)],
        scratch_shapes=[pltpu.VMEM((tm, D_MODEL), F32)],
        compiler_params=_cparams(("arbitrary",), VMEM_LIMIT),
    )(x, dx1, *dsecs, g_mix, w_in)


def _pick_block(n, cap):
    best = None
    for cand in range(128, min(n, cap) + 1, 128):
        if n % cand == 0:
            best = cand
    return n if best is None else best


def _matmul_tn(name, a, b):
    T, M = a.shape
    N = b.shape[1]
    bm, bk = _pick_block(M, 1408), min(DW_TOKENS, T)
    bn = _pick_block(N, 2560)

    def body(a_ref, b_ref, c_ref):
        @pl.when(pl.program_id(2) == 0)
        def _():
            c_ref[...] = jnp.zeros_like(c_ref)

        c_ref[...] += _dot_tn(a_ref[...], b_ref[...])

    return pl.pallas_call(
        body, name=name, grid=(M // bm, N // bn, T // bk),
        in_specs=[pl.BlockSpec((bk, bm), lambda i, j, k: (k, i)), pl.BlockSpec((bk, bn), lambda i, j, k: (k, j))],
        out_specs=pl.BlockSpec((bm, bn), lambda i, j, k: (i, j)), out_shape=jax.ShapeDtypeStruct((M, N), F32),
        compiler_params=_cparams(("parallel", "parallel", "arbitrary"), VMEM_LIMIT),
    )(a, b)


def _matmul_tn_blocks(name, a, b):
    T, M = a.shape
    nd, _, c = b.shape
    bm, bk = _pick_block(M, 512), min(DW_TOKENS, T)

    def body(a_ref, b_ref, c_ref):
        @pl.when(pl.program_id(1) == 0)
        def _():
            c_ref[...] = jnp.zeros_like(c_ref)

        a_blk = a_ref[...].astype(MM)
        for d in range(nd):
            c_ref[d] += _dot_tn(b_ref[d], a_blk)

    return pl.pallas_call(
        body, name=name, grid=(M // bm, T // bk),
        in_specs=[pl.BlockSpec((bk, bm), lambda i, k: (k, i)), pl.BlockSpec((nd, bk, c), lambda i, k: (0, k, 0))],
        out_specs=pl.BlockSpec((nd, c, bm), lambda i, k: (0, 0, i)),
        out_shape=jax.ShapeDtypeStruct((nd, c, M), F32),
        compiler_params=_cparams(("parallel", "arbitrary"), VMEM_LIMIT),
    )(a, b)


def _pad_gain(g, n):
    return jnp.pad(g.reshape(1, -1), ((0, 0), (0, n - g.shape[-1])))


GROUP_A = ("w_ffn_gate", "w_ffn_up", "w_ffn_down", "w_ple_gate", "w_ple_proj")
GROUP_B = ("w_branch", "w_out")
GROUP_C = ("w_in", "w_uq", "w_ukv")
EARLY = GROUP_C
LATE = GROUP_B + GROUP_A
TRANSPOSED = ("w_in", "w_uq", "w_ffn_gate", "w_ffn_up")


def _local_step(x, p, pos, tgt, small, big, late_blocks=None, core=None):
    T = x.shape[0]
    tm = min(ROW_TILE, T)
    tw = min(WIDE_TILE, T)
    w_in = big["w_in"]
    w_uq = jnp.pad(big["w_uq"], ((0, 0), (0, HEAD_PAD - QK_DIM), (0, 0)))
    w_ukv = big["w_ukv"]

    g_mix, g_qa, g_kva = small["mix_norm_g"], small["q_a_norm_g"], small["kv_a_norm_g"]
    g_qn, g_kn = _pad_gain(small["q_norm_g"], HEAD_PAD), _pad_gain(small["k_norm_g"], HEAD_PAD)
    g_out, g_ffn = small["hg_out_norm_g"], small["ffn_norm_g"]
    g_pg, g_post = small["ple_gate_norm_g"], small["ple_post_norm_g"]
    logits = small["hg_lb_logits"]
    lb = _lower_bound(logits)

    h, cq, ckv, kr, hq, hf, hi, hg, bg = _in_proj_fwd(x, g_mix, w_in, T, tw)
    qf, kf, vf = _mla_prep_fwd(cq, ckv, kr, pos, g_qa, g_kva, g_qn, g_kn, w_uq, w_ukv, T, tw)
    if late_blocks is None:
        attn, lse = _flash_fwd(qf, kf, vf, T)
    else:
        attn, lse, *late = _flash_fwd(qf, kf, vf, T, ag_blocks=[late_blocks[n] for n in LATE])
        big = {**big, **dict(zip(LATE, late))}
    o, s0 = _hgrn_fwd(hq, hf, hi, lb, T)
    w_branch = jnp.moveaxis(big["w_branch"].reshape(N_DEV, 2, HG_W, HEAD_PAD), 0, 2).reshape(2, HG_W, D_MODEL)
    w_bra = jnp.pad(w_branch[0].reshape(MLA_HEADS, V_DIM, D_MODEL),
                    ((0, 0), (0, HEAD_PAD - V_DIM), (0, 0))).reshape(MLA_HEADS * HEAD_PAD, D_MODEL)
    w_brb = w_branch[1]
    w_out = big["w_out"].reshape(D_MODEL, D_MODEL)
    w_g, w_u = big["w_ffn_gate"].reshape(FFN, D_MODEL), big["w_ffn_up"].reshape(FFN, D_MODEL)
    w_d = big["w_ffn_down"].reshape(FFN, D_MODEL)
    w_pg, w_pp = big["w_ple_gate"].reshape(D_MODEL, D_MODEL), big["w_ple_proj"]
    x1, ya, yb, m, rec = _merge_fwd(attn, o, hg, bg, x, g_out, w_bra, w_brb, w_out, T, tw)
    x2, gt, up, h2 = _ffn_fwd(x1, g_ffn, w_g, w_u, w_d, T, tw)
    dx2, loss_p, dg_post, dg_pg, d_pg, d_pp = _ple_loss(x2, p, tgt, g_pg, g_post, w_pg, w_pp, T, tw)

    grads, sibs, gots = {}, {}, {}
    dist = core is not None
    pick = lambda names: [grads[n] for n in names] if dist else ()

    def partials(tag, names, got):
        if not dist:
            return ()
        sibs.update(zip(names, got))
        return _chip_partials("rs_partial_" + tag, pick(names), got, core)

    dx1, a, dgt, dup, dg_ffn = _ffn_bwd(dx2, x1, gt, up, g_ffn, w_g, w_u, w_d, T, tm)
    grads["w_ffn_gate"] = _matmul_tn("dw_gate", dgt, h2).reshape(N_DEV, -1, D_MODEL)
    grads["w_ffn_up"] = _matmul_tn("dw_up", dup, h2).reshape(N_DEV, -1, D_MODEL)
    grads["w_ffn_down"] = _matmul_tn("dw_down", a, dx2).reshape(N_DEV, -1, D_MODEL)
    grads["w_ple_gate"] = d_pg.reshape(N_DEV, -1, D_MODEL)
    grads["w_ple_proj"] = d_pp

    dattn, do, dhg, dbg, dg_out, d_out, d_bra, d_brb, *sib_a = _merge_bwd(
        dx1, ya, yb, bg, o, hg, attn, m, rec, g_out, w_bra, w_brb, w_out, T, tm, xchg=(pick(GROUP_A), ()))
    parts_a = partials("a", GROUP_A, sib_a)
    d_bra = d_bra.reshape(N_DEV, MLA_HEADS, HEAD_PAD, HEAD_PAD)[:, :, :V_DIM].reshape(N_DEV, HG_W, HEAD_PAD)
    grads["w_branch"] = jnp.concatenate([d_bra, d_brb], axis=1)
    grads["w_out"] = d_out.reshape(N_DEV, -1, D_MODEL)

    dhq, dhf, dhi, dlb, *got = _hgrn_bwd(hq, hf, hi, do, s0, lb, T, xchg=(pick(GROUP_B), parts_a))
    sib_b, got_a = got[:len(GROUP_B)], got[len(GROUP_B):]
    parts_b = partials("b", GROUP_B, sib_b)
    dqf, dkf, dvf, *got_b = _flash_bwd(qf, kf, vf, attn, dattn, lse, T, xchg=((), parts_b))
    (dcq, dckv, dkr, dg_qa, dg_kva, dg_qn, dg_kn, d_uq, d_ukv) = _mla_prep_bwd(
        cq, ckv, kr, pos, dqf, dkf, dvf, g_qa, g_kva, g_qn, g_kn, w_uq, w_ukv, T, tw)
    grad_x, dproj, dg_mix = _in_proj_bwd(x, dx1, [dcq, dckv, dkr, dhq, dhf, dhi, dhg, dbg], g_mix, w_in, T, tw)
    grads["w_in"] = _matmul_tn_blocks("dw_in", h, dproj)
    grads["w_uq"] = d_uq[:, :QK_DIM]
    grads["w_ukv"] = d_ukv
    parts_c = ()
    if dist:
        gots.update(zip(GROUP_A, got_a))
        gots.update(zip(GROUP_B, got_b))

    dl0 = dlb * lb * (1.0 - lb)
    small_g = {
        "mix_norm_g": dg_mix, "q_a_norm_g": dg_qa, "kv_a_norm_g": dg_kva, "q_norm_g": dg_qn, "k_norm_g": dg_kn,
        "hg_lb_logits": jnp.concatenate([dl0, -dl0], axis=0), "hg_out_norm_g": dg_out,
        "ffn_norm_g": dg_ffn, "ple_gate_norm_g": dg_pg, "ple_post_norm_g": dg_post,
    }
    return loss_p, grad_x, small_g, grads, sibs, gots, parts_c


def _lower_bound(logits):
    def body(l_ref, lb_ref):
        l = l_ref[...]
        mx = jnp.max(l, axis=0, keepdims=True)
        e = jnp.exp(l - mx)
        lb_ref[...] = e[0:1] / jnp.sum(e, axis=0, keepdims=True)

    return pl.pallas_call(body, name="lower_bound", out_shape=jax.ShapeDtypeStruct((1, HG_W), F32))(logits)


def _my_place():
    return lax.axis_index("x"), lax.axis_index("y"), lax.axis_index("c")


def _all_gather(name, blocks):
    n = len(blocks)

    def body(*refs):
        x_refs, out_refs, sems = refs[:n], refs[n:2 * n], refs[2 * n:]
        _ag_start(x_refs, out_refs, sems)
        _ag_finish(x_refs, out_refs, sems)

    any_spec = pl.BlockSpec(memory_space=pl.ANY)
    return pl.pallas_call(
        body, name=name, out_shape=_ag_out_shapes(blocks),
        in_specs=[any_spec] * n, out_specs=[any_spec] * n, scratch_shapes=_ag_sems(n),
    )(*blocks)


def _ag_out_shapes(blocks):
    return [jax.ShapeDtypeStruct((N_DEV,) + b.shape, b.dtype) for b in blocks]


def _ag_sems(n):
    return [pltpu.SemaphoreType.DMA((7 * n,)), pltpu.SemaphoreType.DMA((7 * n,)), pltpu.SemaphoreType.DMA((n,))]


def _ag_parts(x_refs, out_refs, sems):
    send_sems, recv_sems, local_sems = sems
    x, y, c = _my_place()
    me, sibling = (x, y, c), (x, y, 1 - c)
    chips = [(1 - x, y), (x, 1 - y), (1 - x, 1 - y)]
    n = len(x_refs)

    def copy(a, k, block, to, own=False):
        px, py, pc = block
        dst = out_refs[a].at[4 * px + 2 * py + pc]
        return pltpu.make_async_remote_copy(
            src_ref=x_refs[a] if own else dst, dst_ref=dst, send_sem=send_sems.at[7 * a + k],
            recv_sem=recv_sems.at[7 * a + k], device_id=to, device_id_type=MESH_ID)

    mine = [pltpu.make_async_copy(x_refs[a], out_refs[a].at[4 * x + 2 * y + c], local_sems.at[a]) for a in range(n)]
    first = []
    for a in range(n):
        first.append(copy(a, 0, me, sibling, own=True))
        first += [copy(a, 1 + j, me, (*chip, c), own=True) for j, chip in enumerate(chips)]
    return copy, mine, first, me, sibling, chips, c, n


def _ag_start(x_refs, out_refs, sems):
    _, mine, first, *_ = _ag_parts(x_refs, out_refs, sems)
    for cp in mine + first:
        cp.start()


def _ag_finish(x_refs, out_refs, sems):
    copy, mine, first, me, sibling, chips, c, n = _ag_parts(x_refs, out_refs, sems)
    passed = []
    for j, chip in enumerate(chips):
        for a in range(n):
            copy(a, 1 + j, (*chip, c), me).wait_recv()
            passed.append(copy(a, 4 + j, (*chip, c), sibling))
            passed[-1].start()
    for a in range(n):
        copy(a, 0, sibling, me).wait_recv()
    for j, chip in enumerate(chips):
        for a in range(n):
            copy(a, 4 + j, (*chip, 1 - c), me).wait_recv()
    for cp in first + passed:
        cp.wait_send()
    for cp in mine:
        cp.wait()


def _exchange_sibling(name, gs):
    return _exchange(name, (gs, ()))


def _exchange(name, xchg, gather=()):
    n, n_ag = _x_count(xchg), len(gather)

    def body(*refs):
        in_refs, ag_in, refs = refs[:n], refs[n:n + n_ag], refs[n + n_ag:]
        out_refs, ag_out, refs = refs[:n], refs[n:n + n_ag], refs[n + n_ag:]
        sems, ag_sems = refs[:len(refs) - 3 * bool(n_ag)], refs[len(refs) - 3 * bool(n_ag):]
        if n_ag:
            _ag_start(ag_in, ag_out, ag_sems)
        for cp in _x_copies(len(xchg[0]), in_refs, out_refs, sems):
            cp.start()
        for cp in _x_copies(len(xchg[0]), in_refs, out_refs, sems):
            cp.wait()
        if n_ag:
            _ag_finish(ag_in, ag_out, ag_sems)

    any_spec = pl.BlockSpec(memory_space=pl.ANY)
    return pl.pallas_call(
        body, name=name, out_shape=_x_out_shapes(xchg) + _ag_out_shapes(gather),
        in_specs=[any_spec] * (n + n_ag), out_specs=[any_spec] * (n + n_ag),
        scratch_shapes=_x_sems(xchg) + (_ag_sems(n_ag) if n_ag else []),
    )(*xchg[0], *xchg[1], *gather)


N_PARTS = 4


def _part_spec(rows, cols, t_pos, lead_block=(), lead_index=lambda *args: ()):
    if rows % (16 * N_PARTS) == 0:
        axis, shape, count = 0, (rows // N_PARTS, cols), N_PARTS
    elif cols % (128 * N_PARTS) == 0:
        axis, shape, count = 1, (rows, cols // N_PARTS), N_PARTS
    else:
        axis, shape, count = 0, (rows, cols), 1

    def index(*args):
        i = jnp.minimum(args[t_pos], count - 1)
        return (*lead_index(*args), *((i, 0) if axis == 0 else (0, i)))

    return pl.BlockSpec((*lead_block, *shape), index)


def _chip_partials(name, gs, sibs, c_idx):
    n = len(gs)

    def body(c_ref, *refs):
        for g_ref, sib_ref, out_ref in zip(refs[:n], refs[n:2 * n], refs[2 * n:]):
            out_ref[...] = (g_ref[...] + sib_ref[...]).astype(MM)

    own = [_part_spec(*g.shape[1:], 1, (1,), lambda j, t, c_ref: (2 * j + c_ref[0],)) for g in gs]
    by_chip = [_part_spec(*g.shape[1:], 1, (1,), lambda j, t, c_ref: (j,)) for g in gs]
    grid_spec = pltpu.PrefetchScalarGridSpec(
        num_scalar_prefetch=1, grid=(4, N_PARTS), in_specs=own + by_chip, out_specs=by_chip)
    return pl.pallas_call(
        body, name=name, grid_spec=grid_spec, out_shape=[jax.ShapeDtypeStruct((4,) + g.shape[1:], MM) for g in gs],
        compiler_params=_cparams(("arbitrary", "arbitrary"), VMEM_LIMIT),
    )(c_idx, *gs, *sibs)


def _x_count(xchg):
    return len(xchg[0]) + len(xchg[1])


def _x_out_shapes(xchg):
    return ([jax.ShapeDtypeStruct((4,) + g.shape[1:], g.dtype) for g in xchg[0]]
            + [jax.ShapeDtypeStruct((3,) + p.shape[1:], p.dtype) for p in xchg[1]])


def _x_sems(xchg):
    n = 4 * len(xchg[0]) + 3 * len(xchg[1])
    return [pltpu.SemaphoreType.DMA((n,)), pltpu.SemaphoreType.DMA((n,))] if n else []


def _x_copies(n_sib, in_refs, out_refs, sems):
    if not in_refs:
        return []
    send_sems, recv_sems = sems
    x, y, c = _my_place()
    chips = [(1 - x, y), (x, 1 - y), (1 - x, 1 - y)]
    copies = []

    def add(src, dst, to):
        k = len(copies)
        copies.append(pltpu.make_async_remote_copy(
            src_ref=src, dst_ref=dst, send_sem=send_sems.at[k], recv_sem=recv_sems.at[k], device_id=to,
            device_id_type=MESH_ID))

    for a, (src, dst) in enumerate(zip(in_refs, out_refs)):
        if a < n_sib:
            for j in range(4):
                add(src.at[2 * j + 1 - c], dst.at[j], (x, y, 1 - c))
        else:
            for k, (px, py) in enumerate(chips):
                add(src.at[2 * px + py], dst.at[k], (px, py, c))
    return copies


def _reduce_chips(name, gs, gather):
    n, n_ag = len(gs), len(gather)

    def body(*refs):
        g_refs, ag_in, refs = refs[:n], refs[n:n + n_ag], refs[n + n_ag:]
        sib_refs, got_refs, ag_out, refs = refs[:n], refs[n:2 * n], refs[2 * n:2 * n + n_ag], refs[2 * n + n_ag:]
        own_s, oth_s, part_s, refs = refs[:n], refs[n:2 * n], refs[2 * n:3 * n], refs[3 * n:]
        sib_send, sib_recv, chip_send, chip_recv, load_sems = refs[:5]
        x, y, c = _my_place()
        chips = [(1 - x, y), (x, 1 - y), (1 - x, 1 - y), (x, y)]
        slot = lambda k: 2 * chips[k][0] + chips[k][1]

        def to_sibling(a, k):
            return pltpu.make_async_remote_copy(
                src_ref=g_refs[a].at[2 * slot(k) + 1 - c], dst_ref=sib_refs[a].at[slot(k)],
                send_sem=sib_send.at[4 * a + k], recv_sem=sib_recv.at[4 * a + k], device_id=(x, y, 1 - c),
                device_id_type=MESH_ID)

        def to_chip(a, k):
            return pltpu.make_async_remote_copy(
                src_ref=part_s[a].at[k], dst_ref=got_refs[a].at[k], send_sem=chip_send.at[3 * a + k],
                recv_sem=chip_recv.at[3 * a + k], device_id=(*chips[k], c), device_id_type=MESH_ID)

        def load_own(a, k):
            return pltpu.make_async_copy(g_refs[a].at[2 * slot(k) + c], own_s[a].at[k], load_sems.at[6 * a + k])

        def load_other(a, k):
            return pltpu.make_async_copy(sib_refs[a].at[slot(k)], oth_s[a].at[k], load_sems.at[6 * a + 3 + k])

        if n_ag:
            _ag_start(ag_in, ag_out, refs[5:])
        for k in range(4):
            for a in range(n):
                to_sibling(a, k).start()
        for k in range(3):
            for a in range(n):
                load_own(a, k).start()
        for k in range(3):
            for a in range(n):
                to_sibling(a, k).wait_recv()
                load_other(a, k).start()
            for a in range(n):
                load_own(a, k).wait()
                load_other(a, k).wait()
                part_s[a][k] = (own_s[a][k] + oth_s[a][k]).astype(MM)
                to_chip(a, k).start()
        for a in range(n):
            to_sibling(a, 3).wait_recv()
        for k in range(3):
            for a in range(n):
                to_chip(a, k).wait()
        for k in range(4):
            for a in range(n):
                to_sibling(a, k).wait_send()
        if n_ag:
            _ag_finish(ag_in, ag_out, refs[5:])

    any_spec = pl.BlockSpec(memory_space=pl.ANY)
    shapes = [g.shape[1:] for g in gs]
    dma_sems = lambda count: pltpu.SemaphoreType.DMA((count,))
    return pl.pallas_call(
        body, name=name,
        out_shape=([jax.ShapeDtypeStruct((4,) + s, F32) for s in shapes]
                   + [jax.ShapeDtypeStruct((3,) + s, MM) for s in shapes] + _ag_out_shapes(gather)),
        in_specs=[any_spec] * (n + n_ag), out_specs=[any_spec] * (2 * n + n_ag),
        scratch_shapes=([pltpu.VMEM((3,) + s, F32) for s in shapes] * 2 + [pltpu.VMEM((3,) + s, MM) for s in shapes]
                        + [dma_sems(4 * n), dma_sems(4 * n), dma_sems(3 * n), dma_sems(3 * n), dma_sems(6 * n)]
                        + (_ag_sems(n_ag) if n_ag else [])),
        compiler_params=_cparams((), VMEM_LIMIT),
    )(*gs, *gather)


def _adamw_math(w, g, m, v):
    m = ADAM_B1 * m + (1.0 - ADAM_B1) * g
    v = ADAM_B2 * v + (1.0 - ADAM_B2) * jnp.square(g)
    m_hat = m / (1.0 - ADAM_B1 ** ADAM_STEP)
    v_hat = v / (1.0 - ADAM_B2 ** ADAM_STEP)
    delta = -ADAM_LR * (m_hat / (jnp.sqrt(v_hat) + ADAM_EPS) + ADAM_WD * w)
    return delta, m, v


def _sum_adamws(name, gs, sibs, gots, ws, ms, vs, slot_idx, chip_idx):
    n = len(gs)

    def body(s_ref, j_ref, *refs):
        ins, outs = refs[:6 * n], refs[6 * n:]
        for a in range(n):
            g_ref, sib_ref, got_ref, w_ref, m_ref, v_ref = (ins[k * n + a] for k in range(6))
            go_ref, d_ref, m2_ref, v2_ref = outs[4 * a:4 * a + 4]
            grad = g_ref[0] + sib_ref[0]
            for k in range(3):
                grad = grad + got_ref[k].astype(F32)
            go_ref[...] = grad
            d_ref[...], m2_ref[...], v2_ref[...] = _adamw_math(w_ref[...], grad, m_ref[...], v_ref[...])

    shapes = [g.shape[1:] for g in gs]
    flat = [_part_spec(*s, 0) for s in shapes]
    in_specs = ([_part_spec(*s, 0, (1,), lambda t, s_ref, j_ref: (s_ref[0],)) for s in shapes]
                + [_part_spec(*s, 0, (1,), lambda t, s_ref, j_ref: (j_ref[0],)) for s in shapes]
                + [_part_spec(*s, 0, (3,), lambda t, s_ref, j_ref: (0,)) for s in shapes] + flat * 3)
    grid_spec = pltpu.PrefetchScalarGridSpec(
        num_scalar_prefetch=2, grid=(N_PARTS,), in_specs=in_specs, out_specs=[f for f in flat for _ in range(4)])
    res = pl.pallas_call(
        body, name=name, grid_spec=grid_spec,
        out_shape=[jax.ShapeDtypeStruct(s, F32) for s in shapes for _ in range(4)],
        compiler_params=_cparams(("arbitrary",), VMEM_LIMIT),
    )(slot_idx, chip_idx, *gs, *sibs, *gots, *ws, *ms, *vs)
    return [res[4 * a:4 * a + 4] for a in range(n)]


BIG = ("w_in", "w_uq", "w_ukv", "w_branch", "w_out", "w_ffn_gate", "w_ffn_up", "w_ffn_down", "w_ple_gate", "w_ple_proj")
SMALL = (
    ("mix_norm_g", 0, 1, 1024), ("q_a_norm_g", 1, 1, 384), ("kv_a_norm_g", 2, 1, 256), ("q_norm_g", 3, 1, 96),
    ("k_norm_g", 4, 1, 96), ("hg_lb_logits", 5, 2, 512), ("hg_out_norm_g", 7, 1, 128), ("ffn_norm_g", 8, 1, 1024),
    ("ple_gate_norm_g", 9, 1, 1024), ("ple_post_norm_g", 10, 1, 1024),
)
SLAB_ROWS, LOSS_ROW = 16, 15


def _pack_partials(small_g, loss_p):
    def body(*refs):
        val_refs, loss_ref, out_ref = refs[:len(SMALL)], refs[len(SMALL)], refs[len(SMALL) + 1]
        out_ref[...] = jnp.zeros_like(out_ref)
        for (_, r0, rows, cols), ref in zip(SMALL, val_refs):
            val = ref[...]
            if val.shape[0] != rows:
                val = jnp.sum(val, axis=0, keepdims=True)
            out_ref[r0:r0 + rows, :cols] = val[:, :cols]
        out_ref[LOSS_ROW:LOSS_ROW + 1, :HEAD_PAD] = jnp.full((1, HEAD_PAD), jnp.sum(loss_ref[...]), F32)

    return pl.pallas_call(
        body, name="pack_partials", out_shape=jax.ShapeDtypeStruct((SLAB_ROWS, D_MODEL), F32),
    )(*[small_g[n] for n, *_ in SMALL], loss_p)


def _adamw_small(parts, ws, ms, vs):
    n = len(SMALL)

    def body(p_ref, *refs):
        ins, loss_ref, outs = refs[:3 * n], refs[3 * n], refs[3 * n + 1:]
        total = p_ref[0]
        for d in range(1, N_DEV):
            total = total + p_ref[d]
        loss_ref[...] = total[LOSS_ROW:LOSS_ROW + 1, 0:1]
        for a, (_, r0, rows, cols) in enumerate(SMALL):
            g = total[r0:r0 + rows, :cols]
            outs[4 * a][...] = g
            outs[4 * a + 1][...], outs[4 * a + 2][...], outs[4 * a + 3][...] = _adamw_math(
                ins[a][...], g, ins[n + a][...], ins[2 * n + a][...])

    shapes = [jax.ShapeDtypeStruct((rows, cols), F32) for _, _, rows, cols in SMALL]
    res = pl.pallas_call(
        body, name="adamw_small", out_shape=[jax.ShapeDtypeStruct((1, 1), F32)] + [s for s in shapes for _ in range(4)],
    )(parts, *ws, *ms, *vs)
    return res[0], [res[1 + 4 * a:5 + 4 * a] for a in range(n)]


_WEIGHTS = ["mix_norm_g", "w_in", "q_a_norm_g", "w_uq", "kv_a_norm_g", "w_ukv", "q_norm_g", "k_norm_g", "hg_lb_logits",
            "hg_out_norm_g", "w_branch", "w_out", "ffn_norm_g", "w_ffn_gate", "w_ffn_up", "w_ffn_down",
            "ple_gate_norm_g", "w_ple_gate", "w_ple_proj", "ple_post_norm_g"]


def _step(x, p, positions, tgt, w, m, v):
    small_names = [n for n, *_ in SMALL]
    T = x.shape[1]
    px, py, pc = _my_place()
    as_idx = lambda t: jnp.reshape(t, (1,)).astype(jnp.int32)

    def two_d(n, t):
        t = t.reshape(-1, t.shape[-1])
        return t.T if n in TRANSPOSED else t

    def full_shape(n, t):
        return (t.T if n in TRANSPOSED else t).reshape(w[n].shape)

    blocks = {n: two_d(n, w[n]).astype(MM) for n in BIG}
    big = dict(zip(EARLY, _all_gather("ag_weights", [blocks[n] for n in EARLY])))
    small = {n: (w[n] if n == "hg_lb_logits" else w[n].reshape(1, -1)) for n in small_names}

    loss_p, grad_x, small_g, grads, sibs, gots, parts_c = _local_step(
        x[0], p[0, 0], positions.reshape(T, 1), tgt[0], small, big, late_blocks=blocks, core=as_idx(pc))

    *res_c, slabs = _reduce_chips("rs_chips", [grads[n] for n in GROUP_C], [_pack_partials(small_g, loss_p)])
    sibs.update(zip(GROUP_C, res_c[:len(GROUP_C)]))
    gots.update(zip(GROUP_C, res_c[len(GROUP_C):]))
    out_g, out_d, out_m, out_v = {}, {}, {}, {}
    for tag, names in (("ab", GROUP_A + GROUP_B), ("c", GROUP_C)):
        pick = lambda table: [table[n] for n in names]
        res = _sum_adamws("adamw_" + tag, pick(grads), pick(sibs), pick(gots), [two_d(n, w[n]) for n in names],
                          [two_d(n, m[n]) for n in names], [two_d(n, v[n]) for n in names],
                          as_idx(4 * px + 2 * py + pc), as_idx(2 * px + py))
        for n, r in zip(names, res):
            out_g[n], out_d[n], out_m[n], out_v[n] = [full_shape(n, t) for t in r]

    loss, res = _adamw_small(slabs, *([t[n] for n in small_names] for t in (w, m, v)))
    for n, r in zip(small_names, res):
        out_g[n], out_d[n], out_m[n], out_v[n] = r

    outs = [loss.reshape(()), grad_x[None]]
    for table in (out_g, out_d, out_m, out_v):
        outs += [table[n] for n in _WEIGHTS]
    return tuple(outs)


def kernel(x, p, positions, mix_norm_g, w_in, q_a_norm_g, w_uq, kv_a_norm_g, w_ukv, q_norm_g, k_norm_g, hg_lb_logits, hg_out_norm_g, w_branch, w_out, ffn_norm_g, w_ffn_gate, w_ffn_up, w_ffn_down, ple_gate_norm_g, w_ple_gate, w_ple_proj, ple_post_norm_g, loss_target, m_mix_norm_g, m_w_in, m_q_a_norm_g, m_w_uq, m_kv_a_norm_g, m_w_ukv, m_q_norm_g, m_k_norm_g, m_hg_lb_logits, m_hg_out_norm_g, m_w_branch, m_w_out, m_ffn_norm_g, m_w_ffn_gate, m_w_ffn_up, m_w_ffn_down, m_ple_gate_norm_g, m_w_ple_gate, m_w_ple_proj, m_ple_post_norm_g, v_mix_norm_g, v_w_in, v_q_a_norm_g, v_w_uq, v_kv_a_norm_g, v_w_ukv, v_q_norm_g, v_k_norm_g, v_hg_lb_logits, v_hg_out_norm_g, v_w_branch, v_w_out, v_ffn_norm_g, v_w_ffn_gate, v_w_ffn_up, v_w_ffn_down, v_ple_gate_norm_g, v_w_ple_gate, v_w_ple_proj, v_ple_post_norm_g):
    w = dict(mix_norm_g=mix_norm_g, w_in=w_in, q_a_norm_g=q_a_norm_g, w_uq=w_uq, kv_a_norm_g=kv_a_norm_g, w_ukv=w_ukv,
             q_norm_g=q_norm_g, k_norm_g=k_norm_g, hg_lb_logits=hg_lb_logits, hg_out_norm_g=hg_out_norm_g,
             w_branch=w_branch, w_out=w_out, ffn_norm_g=ffn_norm_g, w_ffn_gate=w_ffn_gate, w_ffn_up=w_ffn_up,
             w_ffn_down=w_ffn_down, ple_gate_norm_g=ple_gate_norm_g, w_ple_gate=w_ple_gate, w_ple_proj=w_ple_proj,
             ple_post_norm_g=ple_post_norm_g)
    m = dict(mix_norm_g=m_mix_norm_g, w_in=m_w_in, q_a_norm_g=m_q_a_norm_g, w_uq=m_w_uq, kv_a_norm_g=m_kv_a_norm_g,
             w_ukv=m_w_ukv, q_norm_g=m_q_norm_g, k_norm_g=m_k_norm_g, hg_lb_logits=m_hg_lb_logits,
             hg_out_norm_g=m_hg_out_norm_g, w_branch=m_w_branch, w_out=m_w_out, ffn_norm_g=m_ffn_norm_g,
             w_ffn_gate=m_w_ffn_gate, w_ffn_up=m_w_ffn_up, w_ffn_down=m_w_ffn_down,
             ple_gate_norm_g=m_ple_gate_norm_g, w_ple_gate=m_w_ple_gate, w_ple_proj=m_w_ple_proj,
             ple_post_norm_g=m_ple_post_norm_g)
    v = dict(mix_norm_g=v_mix_norm_g, w_in=v_w_in, q_a_norm_g=v_q_a_norm_g, w_uq=v_w_uq, kv_a_norm_g=v_kv_a_norm_g,
             w_ukv=v_w_ukv, q_norm_g=v_q_norm_g, k_norm_g=v_k_norm_g, hg_lb_logits=v_hg_lb_logits,
             hg_out_norm_g=v_hg_out_norm_g, w_branch=v_w_branch, w_out=v_w_out, ffn_norm_g=v_ffn_norm_g,
             w_ffn_gate=v_w_ffn_gate, w_ffn_up=v_w_ffn_up, w_ffn_down=v_w_ffn_down,
             ple_gate_norm_g=v_ple_gate_norm_g, w_ple_gate=v_w_ple_gate, w_ple_proj=v_w_ple_proj,
             ple_post_norm_g=v_ple_post_norm_g)
    return _step(x, p, positions, loss_target, w, m, v)
```

```python
import jax
import jax.numpy as jnp
import numpy as np
from jax import lax
from jax.experimental import pallas as pl
from jax.experimental.pallas import tpu as pltpu

F32 = jnp.float32
MM = jnp.bfloat16
MESH_ID = pl.DeviceIdType.MESH

D_MODEL = 1024
N_DEV = 8
MLA_HEADS = 8
QK_NOPE = 64
QK_ROPE = 32
QK_DIM = 96
V_DIM = 64
HEAD_PAD = 128
Q_RANK = 384
KV_RANK = 256
ROPE_BASE = 10000.0
HG_HEADS = 4
HG_DIM = 128
HG_W = 512
HG_CHUNK = 64
FFN = 2816
PLE = 256
EPS = 1e-6
ATT_SCALE = QK_DIM ** -0.5
NEG = -1e30

ADAM_LR = 0.001
ADAM_B1 = 0.9
ADAM_B2 = 0.999
ADAM_EPS = 1e-08
ADAM_WD = 0.01
ADAM_STEP = 10

COL_SECTIONS = ((0, 384), (384, 256), (640, 32), (672, 512), (1184, 512), (1696, 512), (2208, 512), (2720, 2048))
STORED_WIDTHS = tuple(HEAD_PAD if n == QK_ROPE else n for _, n in COL_SECTIONS)
IN_COLS = 4768
IN_BLOCK = IN_COLS // N_DEV

VMEM_LIMIT = 58 * 1024 * 1024
WIDE_TILE = 512
ROW_TILE = 256
DW_TOKENS = 1024
ATT_TILE = 1024
ATT_HEADS = 4
HG_BLOCK = 512
HG_UNROLL = 4


def _dot(a, b):
    return jnp.dot(a.astype(MM), b.astype(MM), preferred_element_type=F32)


def _dot_nt(a, b):
    return lax.dot_general(a.astype(MM), b.astype(MM), (((1,), (1,)), ((), ())), preferred_element_type=F32)


def _dot_tn(a, b):
    return lax.dot_general(a.astype(MM), b.astype(MM), (((0,), (0,)), ((), ())), preferred_element_type=F32)


def _sigmoid(x):
    return 1.0 / (1.0 + jnp.exp(-x))


def _rms(x, n=None):
    n = x.shape[-1] if n is None else n
    r = lax.rsqrt(jnp.sum(x * x, axis=-1, keepdims=True) * (1.0 / n) + EPS)
    return x * r, r


def _rms_bwd(dxh, xh, r, n=None):
    n = xh.shape[-1] if n is None else n
    return r * (dxh - xh * (jnp.sum(dxh * xh, axis=-1, keepdims=True) * (1.0 / n)))


def _rope_tables(pos, tm):
    lane = lax.broadcasted_iota(jnp.int32, (tm, HEAD_PAD), 1)
    idx = jnp.where(lane < QK_NOPE + QK_ROPE // 2, lane - QK_NOPE, lane - QK_NOPE - QK_ROPE // 2)
    inv = jnp.exp(idx.astype(F32) * (-np.log(ROPE_BASE) * 2.0 / QK_ROPE))
    ang = pos.astype(F32) * inv
    in_rope = (lane >= QK_NOPE) & (lane < QK_DIM)
    first = lane < QK_NOPE + QK_ROPE // 2
    cos_t = jnp.where(in_rope, jnp.cos(ang), 1.0)
    sin_t = jnp.where(in_rope, jnp.where(first, -jnp.sin(ang), jnp.sin(ang)), 0.0)
    return cos_t, sin_t, (first, in_rope)


def _rope_swap(x, halves):
    first, in_rope = halves
    half = QK_ROPE // 2
    return jnp.where(in_rope, jnp.where(first, pltpu.roll(x, HEAD_PAD - half, 1), pltpu.roll(x, half, 1)), 0.0)


def _cparams(sem, vmem=None):
    return pltpu.CompilerParams(dimension_semantics=sem, vmem_limit_bytes=vmem)


def _row_call(name, body, T, tm, row_ins, full_ins, row_outs, acc_outs, vmem=None, scratch=(), xchg=((), ())):
    n_in, n_out, n_x = len(row_ins) + len(full_ins), len(row_outs) + len(acc_outs), _x_count(xchg)
    steps = T // tm

    def kern(*refs):
        ins, x_in, refs = refs[:n_in], refs[n_in:n_in + n_x], refs[n_in + n_x:]
        outs, x_out, refs = refs[:n_out], refs[n_out:n_out + n_x], refs[n_out + n_x:]
        scr, x_sems = refs[:len(scratch)], refs[len(scratch):]
        i = pl.program_id(0)
        if n_x:
            @pl.when(i == 0)
            def _():
                for cp in _x_copies(len(xchg[0]), x_in, x_out, x_sems):
                    cp.start()

        body(i, *ins, *outs, *scr)
        if n_x:
            @pl.when(i == steps - 1)
            def _():
                for cp in _x_copies(len(xchg[0]), x_in, x_out, x_sems):
                    cp.wait()

    any_spec = pl.BlockSpec(memory_space=pl.ANY)
    in_specs = [pl.BlockSpec((tm, a.shape[1]), lambda i: (i, 0)) for a in row_ins]
    in_specs += [pl.BlockSpec(a.shape, lambda i, nd=a.ndim: (0,) * nd, pipeline_mode=pl.Buffered(1)) for a in full_ins]
    out_specs = [pl.BlockSpec((tm, n), lambda i: (i, 0)) for n, _ in row_outs]
    out_specs += [pl.BlockSpec(s, lambda i, nd=len(s): (0,) * nd) for s, _ in acc_outs]
    out_shape = [jax.ShapeDtypeStruct((T, n), dt) for n, dt in row_outs]
    out_shape += [jax.ShapeDtypeStruct(s, dt) for s, dt in acc_outs]
    return pl.pallas_call(
        kern, name=name, grid=(steps,), in_specs=in_specs + [any_spec] * n_x, out_specs=out_specs + [any_spec] * n_x,
        out_shape=out_shape + _x_out_shapes(xchg), scratch_shapes=list(scratch) + _x_sems(xchg),
        compiler_params=_cparams(("arbitrary",), vmem),
    )(*row_ins, *full_ins, *xchg[0], *xchg[1])


FFN_HALVES = (slice(0, FFN // 2), slice(FFN // 2, FFN))
ROW_CHUNK = 16
CHUNK_UNROLL = True


def _by_chunks(tm, fn):
    def step(c, carry):
        fn(pl.ds(pl.multiple_of(c * ROW_CHUNK, ROW_CHUNK), ROW_CHUNK))
        return carry

    lax.fori_loop(0, tm // ROW_CHUNK, step, 0, unroll=CHUNK_UNROLL)


def _fold8(x):
    return x[:8] + x[8:]


def _acc(ref, i, val):
    @pl.when(i == 0)
    def _():
        ref[...] = val

    @pl.when(i != 0)
    def _():
        ref[...] += val


def _in_proj_fwd(x, g_mix, w_in, T, tm):
    def body(i, x_ref, g_ref, w_ref, h_ref, *rest):
        outs, pj_s = rest[:-1], rest[-1]
        g = g_ref[...]

        def norm(rows):
            h_ref[rows, :] = (_rms(x_ref[rows, :])[0] * g).astype(MM)

        _by_chunks(tm, norm)
        for d in range(N_DEV):
            pj_s[d] = _dot_nt(h_ref[...], w_ref[d])

        def join_and_cut(rows):
            proj = jnp.concatenate([pj_s[d, rows, :] for d in range(N_DEV)], axis=1)
            for (s, n), o_ref in zip(COL_SECTIONS, outs):
                if n == QK_ROPE:
                    o_ref[rows, :] = jnp.concatenate(
                        [jnp.zeros((ROW_CHUNK, QK_NOPE), F32), proj[:, s:s + n],
                         jnp.zeros((ROW_CHUNK, HEAD_PAD - QK_DIM), F32)], axis=1)
                else:
                    o_ref[rows, :] = proj[:, s:s + n]

        _by_chunks(tm, join_and_cut)

    row_outs = [(D_MODEL, MM)] + [(n, F32) for n in STORED_WIDTHS]
    return _row_call("in_proj_fwd", body, T, tm, [x], [g_mix, w_in], row_outs, [], VMEM_LIMIT,
                     scratch=[pltpu.VMEM((N_DEV, tm, IN_BLOCK), F32)])


def _mla_heads_fwd(raw, g_pad, cos_t, sin_t, first):
    outs, saved = [], []
    for h in range(MLA_HEADS):
        xh, r = _rms(raw[:, h * HEAD_PAD:(h + 1) * HEAD_PAD], QK_DIM)
        y = xh * g_pad
        outs.append(y * cos_t + _rope_swap(y, first) * sin_t)
        saved.append((xh, r))
    return outs, saved


def _mla_raw_heads(cqn, ckvn, kr, wuq_ref, wukv_ref, tm):
    lane = lax.broadcasted_iota(jnp.int32, (tm, HEAD_PAD), 1)
    nope = lane < QK_NOPE
    one_lane = jnp.where(lane == V_DIM, 1.0, 0.0)
    qs, ks, vs = [], [], []
    for h in range(MLA_HEADS):
        qs.append(_dot_nt(cqn, wuq_ref[h]))
        kv = _dot(ckvn, wukv_ref[h])
        ks.append(jnp.where(nope, kv, kr))
        vs.append(jnp.where(nope, pltpu.roll(kv, V_DIM, 1), one_lane))
    return jnp.concatenate(qs, axis=1), jnp.concatenate(ks, axis=1), jnp.concatenate(vs, axis=1)


def _mla_prep_fwd(cq, ckv, kr, pos, g_qa, g_kva, g_qn, g_kn, w_uq, w_ukv, T, tm):
    def body(i, cq_ref, ckv_ref, kr_ref, pos_ref, gqa_ref, gkva_ref, gqn_ref, gkn_ref, wuq_ref, wukv_ref,
             q_ref, k_ref, v_ref):
        cos_t, sin_t, first = _rope_tables(pos_ref[...], tm)
        cqn = _rms(cq_ref[...])[0] * gqa_ref[...]
        ckvn = _rms(ckv_ref[...])[0] * gkva_ref[...]
        q_raw, k_raw, v = _mla_raw_heads(cqn, ckvn, kr_ref[...], wuq_ref, wukv_ref, tm)
        qs, _ = _mla_heads_fwd(q_raw, gqn_ref[...], cos_t, sin_t, first)
        ks, _ = _mla_heads_fwd(k_raw, gkn_ref[...], cos_t, sin_t, first)
        q_ref[...] = (jnp.concatenate(qs, axis=1) * ATT_SCALE).astype(MM)
        k_ref[...] = jnp.concatenate(ks, axis=1).astype(MM)
        v_ref[...] = v.astype(MM)

    w = MLA_HEADS * HEAD_PAD
    return _row_call("mla_prep_fwd", body, T, tm, [cq, ckv, kr, pos], [g_qa, g_kva, g_qn, g_kn, w_uq, w_ukv],
                     [(w, MM), (w, MM), (w, MM)], [])


def _causal_pairs(n, by_query):
    if by_query:
        pairs = [(q, k) for q in range(n) for k in range(q + 1)]
    else:
        pairs = [(q, k) for k in range(n) for q in range(k, n)]
    return np.array([p[0] for p in pairs], np.int32), np.array([p[1] for p in pairs], np.int32)


def _flash_fwd(qf, kf, vf, T, ag_blocks=()):
    tq = min(ATT_TILE, T)
    nq = T // tq

    qi_tab, ki_tab = _causal_pairs(nq, by_query=True)

    hp = ATT_HEADS

    n_ag = len(ag_blocks)
    n_heads, n_pairs = MLA_HEADS // hp, len(qi_tab)

    def body(qi_ref, ki_ref, q_ref, k_ref, v_ref, *rest):
        ag_in, (o_ref, lse_ref), rest = rest[:n_ag], rest[n_ag:n_ag + 2], rest[n_ag + 2:]
        ag_out, (m_s, acc_s), ag_sems = rest[:n_ag], rest[n_ag:n_ag + 2], rest[n_ag + 2:]
        t = pl.program_id(1)
        qi, ki = qi_ref[t], ki_ref[t]
        if n_ag:
            @pl.when((pl.program_id(0) == 0) & (t == 0))
            def _():
                _ag_start(ag_in, ag_out, ag_sems)

        @pl.when(ki == 0)
        def _():
            m_s[...] = jnp.full_like(m_s, NEG)
            acc_s[...] = jnp.zeros_like(acc_s)

        def step(masked):
            halves = 2 if masked and tq % (2 * HEAD_PAD) == 0 else 1
            w = tq // halves
            for hh in range(hp):
                hs = slice(hh * HEAD_PAD, (hh + 1) * HEAD_PAD)
                for part in range(halves):
                    cols, nk = slice(part * w, (part + 1) * w), (part + 1) * w
                    s_t = _dot_nt(k_ref[:nk, hs], q_ref[cols, hs])
                    if masked:
                        key = lax.broadcasted_iota(jnp.int32, (nk, w), 0)
                        qry = lax.broadcasted_iota(jnp.int32, (nk, w), 1) + part * w
                        s_t = jnp.where(key <= qry, s_t, NEG)
                    m_old = m_s[hh, :, cols]
                    m_new = jnp.maximum(m_old, jnp.max(s_t, axis=0, keepdims=True))
                    p_t = jnp.exp(s_t - m_new)
                    acc_s[hh, :, cols] = jnp.exp(m_old - m_new) * acc_s[hh, :, cols] + _dot_tn(v_ref[:nk, hs], p_t)
                    m_s[hh, :, cols] = m_new

        @pl.when(ki < qi)
        def _():
            step(False)

        @pl.when(ki == qi)
        def _():
            step(True)
            real = lax.broadcasted_iota(jnp.int32, (HEAD_PAD, tq), 0) < V_DIM
            for hh in range(hp):
                hs = slice(hh * HEAD_PAD, (hh + 1) * HEAD_PAD)
                acc = acc_s[hh]
                l = acc[V_DIM:V_DIM + 1]
                o_ref[:, hs] = jnp.where(real, acc / l, 0.0).T
                lse_ref[:, hs] = jnp.broadcast_to(m_s[hh] + jnp.log(l), (HEAD_PAD, tq)).T

        if n_ag:
            @pl.when((pl.program_id(0) == n_heads - 1) & (t == n_pairs - 1))
            def _():
                _ag_finish(ag_in, ag_out, ag_sems)

    q_spec = pl.BlockSpec((tq, hp * HEAD_PAD), lambda h, t, qi_ref, ki_ref: (qi_ref[t], h))
    kv_spec = pl.BlockSpec((tq, hp * HEAD_PAD), lambda h, t, qi_ref, ki_ref: (ki_ref[t], h))
    any_spec = pl.BlockSpec(memory_space=pl.ANY)
    grid_spec = pltpu.PrefetchScalarGridSpec(
        num_scalar_prefetch=2, grid=(n_heads, n_pairs),
        in_specs=[q_spec, kv_spec, kv_spec] + [any_spec] * n_ag, out_specs=[q_spec, q_spec] + [any_spec] * n_ag,
        scratch_shapes=[pltpu.VMEM((hp, 1, tq), F32), pltpu.VMEM((hp, HEAD_PAD, tq), F32)]
        + (_ag_sems(n_ag) if n_ag else []))
    return pl.pallas_call(
        body, name="flash_fwd", grid_spec=grid_spec,
        out_shape=[jax.ShapeDtypeStruct((T, MLA_HEADS * HEAD_PAD), F32)] * 2 + _ag_out_shapes(ag_blocks),
        compiler_params=_cparams(("arbitrary", "arbitrary")),
    )(jnp.asarray(qi_tab), jnp.asarray(ki_tab), qf, kf, vf, *ag_blocks)


def _hg_gates(hf, lb):
    sg = _sigmoid(hf)
    f = lb + (1.0 - lb) * sg
    return sg, f, jnp.log(f), 1.0 - f


def _prefix_sum(x, reverse=False):
    n = x.shape[0]
    row = lax.broadcasted_iota(jnp.int32, x.shape, 0)
    step = 1
    while step < n:
        if reverse:
            x = x + jnp.where(row < n - step, pltpu.roll(x, n - step, 0), 0.0)
        else:
            x = x + jnp.where(row >= step, pltpu.roll(x, step, 0), 0.0)
        step *= 2
    return x


def _hg_levels():
    C = HG_CHUNK
    t = lax.broadcasted_iota(jnp.int32, (C, C), 0)
    s = lax.broadcasted_iota(jnp.int32, (C, C), 1)
    levels = []
    for shift in range(C.bit_length() - 2, -1, -1):
        pair_t, pair_s = lax.shift_right_logical(t, shift + 1), lax.shift_right_logical(s, shift + 1)
        later_t = (lax.shift_right_logical(t, shift) & 1) == 1
        earlier_s = (lax.shift_right_logical(s, shift) & 1) == 0
        levels.append((1 << shift, (pair_t == pair_s) & later_t & earlier_s))
    return levels, t == s


def _hg_refs(b):
    C, n = b.shape
    row = lax.broadcasted_iota(jnp.int32, (C, n), 0)
    back1, back2, ahead1 = pltpu.roll(b, 1, 0), pltpu.roll(b, 2, 0), pltpu.roll(b, C - 1, 0)
    refs = []
    for half in (32, 16, 8, 4):
        refs.append(jnp.concatenate(
            [jnp.broadcast_to(b[lo + half - 1:lo + half], (2 * half, n)) for lo in range(0, C, 2 * half)], axis=0))
    in4 = row & 3
    refs.append(jnp.where(in4 == 0, ahead1, jnp.where(in4 == 1, b, jnp.where(in4 == 2, back1, back2))))
    refs.append(jnp.where((row & 1) == 1, back1, b))
    return refs


def _hg_intra(q, k, b, refs, levels, eye):
    a = jnp.where(eye, jnp.sum(q * k, axis=1, keepdims=True), 0.0)
    saved = []
    for r, (_, mask) in zip(refs, levels):
        e = jnp.exp(-jnp.abs(b - r))
        q_t, k_t = q * e, k * e
        a = a + jnp.where(mask, _dot_nt(q_t, k_t), 0.0)
        saved.append((q_t, k_t, e))
    return a, saved


def _hg_intra_bwd(d_a, q, k, saved, levels, eye):
    diag = jnp.sum(jnp.where(eye, d_a, 0.0), axis=1, keepdims=True)
    dq, dk = diag * k, diag * q
    for (q_t, k_t, e), (_, mask) in zip(saved, levels):
        da = jnp.where(mask, d_a, 0.0)
        dq = dq + _dot(da, k_t) * e
        dk = dk + _dot_tn(da, q_t) * e
    return dq, dk


def _hgrn_fwd(hq, hf, hi, lb, T):
    rb = min(HG_BLOCK, T)
    ncb = rb // HG_CHUNK

    def body(hq_ref, hf_ref, hi_ref, lb_ref, o_ref, s0_ref, st_ref):
        @pl.when(pl.program_id(0) == 0)
        def _():
            st_ref[...] = jnp.zeros_like(st_ref)

        levels, eye = _hg_levels()

        def chunk(c, carry):
            rows = pl.ds(pl.multiple_of(c * HG_CHUNK, HG_CHUNK), HG_CHUNK)
            _, _, logf, kk = _hg_gates(hf_ref[rows, :], lb_ref[...])
            b = _prefix_sum(logf)
            refs = _hg_refs(b)
            q_all, v_all = hq_ref[rows, :], hi_ref[rows, :]
            outs = []
            for h in range(HG_HEADS):
                ls = slice(h * HG_DIM, (h + 1) * HG_DIM)
                q, k, v, bh = q_all[:, ls], kk[:, ls], v_all[:, ls], b[:, ls]
                st = st_ref[h]
                s0_ref[c, h * HG_DIM:(h + 1) * HG_DIM, :] = st
                b_end = bh[HG_CHUNK - 1:HG_CHUNK]
                a, _ = _hg_intra(q, k, bh, [r[:, ls] for r in refs], levels, eye)
                outs.append(_dot_nt(q * jnp.exp(bh), st) + _dot(a, v))
                st_ref[h] = st * jnp.exp(b_end) + _dot_tn(v, k * jnp.exp(b_end - bh))
            o_ref[rows, :] = jnp.concatenate(outs, axis=1)
            return carry

        lax.fori_loop(0, ncb, chunk, 0, unroll=HG_UNROLL)

    row = pl.BlockSpec((rb, HG_W), lambda i: (i, 0))
    return pl.pallas_call(
        body, name="hgrn_fwd", grid=(T // rb,),
        in_specs=[row, row, row, pl.BlockSpec((1, HG_W), lambda i: (0, 0))],
        out_specs=[row, pl.BlockSpec((ncb, HG_W, HG_DIM), lambda i: (i, 0, 0))],
        out_shape=[jax.ShapeDtypeStruct((T, HG_W), F32), jax.ShapeDtypeStruct((T // HG_CHUNK, HG_W, HG_DIM), F32)],
        scratch_shapes=[pltpu.VMEM((HG_HEADS, HG_DIM, HG_DIM), F32)],
        compiler_params=_cparams(("arbitrary",)),
    )(hq, hf, hi, lb)


def _hgrn_bwd(hq, hf, hi, do, s0, lb, T, xchg=((), ())):
    rb = min(HG_BLOCK, T)
    ncb = rb // HG_CHUNK
    nb = T // rb
    C = HG_CHUNK
    n_x, n_sib = _x_count(xchg), len(xchg[0])

    def body(hq_ref, hf_ref, hi_ref, do_ref, s0_ref, lb_ref, *rest):
        x_in, (dq_ref, df_ref, dv_ref, dlb_ref), rest = rest[:n_x], rest[n_x:n_x + 4], rest[n_x + 4:]
        x_out, dst_ref, x_sems = rest[:n_x], rest[n_x], rest[n_x + 1:]

        @pl.when(pl.program_id(0) == 0)
        def _():
            dst_ref[...] = jnp.zeros_like(dst_ref)
            dlb_ref[...] = jnp.zeros_like(dlb_ref)
            for cp in _x_copies(n_sib, x_in, x_out, x_sems):
                cp.start()

        row_cc = lax.broadcasted_iota(jnp.int32, (C, C), 0)
        col_cc = lax.broadcasted_iota(jnp.int32, (C, C), 1)
        last_row = lax.broadcasted_iota(jnp.int32, (C, HG_DIM), 0) == C - 1
        lb_v = lb_ref[...]
        levels, eye = _hg_levels()

        def chunk(cc, carry):
            c = ncb - 1 - cc
            rows = pl.ds(pl.multiple_of(c * C, C), C)
            hf_c = hf_ref[rows, :]
            sg, f, logf, kk = _hg_gates(hf_c, lb_v)
            b = _prefix_sum(logf)
            refs = _hg_refs(b)
            q_all, v_all, do_all = hq_ref[rows, :], hi_ref[rows, :], do_ref[rows, :]
            dq_o, dk_o, dv_o, db_o = [], [], [], []
            for h in range(HG_HEADS):
                ls = slice(h * HG_DIM, (h + 1) * HG_DIM)
                q, k, v, bh, d_o = q_all[:, ls], kk[:, ls], v_all[:, ls], b[:, ls], do_all[:, ls]
                st0 = s0_ref[c, h * HG_DIM:(h + 1) * HG_DIM, :]
                dst = dst_ref[h]
                b_end = bh[C - 1:C]
                e_b, e_end = jnp.exp(bh), jnp.exp(b_end)
                e_rem = jnp.exp(b_end - bh)
                qe, kd = q * e_b, k * e_rem
                st_end = st0 * e_end + _dot_tn(v, kd)
                a, saved = _hg_intra(q, k, bh, [r[:, ls] for r in refs], levels, eye)
                d_a = jnp.where(col_cc <= row_cc, _dot_nt(d_o, v), 0.0)
                dq_i, dk_i = _hg_intra_bwd(d_a, q, k, saved, levels, eye)
                dv = _dot_tn(a, d_o) + _dot_nt(kd, dst)
                dq = e_b * _dot(d_o, st0) + dq_i
                dk = e_rem * _dot(v, dst) + dk_i
                extra = jnp.sum(dst * st_end, axis=0, keepdims=True)
                db_o.append(q * dq - k * dk + jnp.where(last_row, extra, 0.0))
                dst_ref[h] = dst * e_end + _dot_tn(d_o, qe)
                dq_o.append(dq)
                dk_o.append(dk)
                dv_o.append(dv)
            dlogf = _prefix_sum(jnp.concatenate(db_o, axis=1), reverse=True)
            d_f = dlogf / f - jnp.concatenate(dk_o, axis=1)
            dq_ref[rows, :] = jnp.concatenate(dq_o, axis=1).astype(MM)
            dv_ref[rows, :] = jnp.concatenate(dv_o, axis=1).astype(MM)
            df_ref[rows, :] = (d_f * (1.0 - lb_v) * sg * (1.0 - sg)).astype(MM)
            dlb_ref[...] += jnp.sum(d_f * (1.0 - sg), axis=0, keepdims=True)
            return carry

        lax.fori_loop(0, ncb, chunk, 0, unroll=HG_UNROLL)

        if n_x:
            @pl.when(pl.program_id(0) == nb - 1)
            def _():
                for cp in _x_copies(n_sib, x_in, x_out, x_sems):
                    cp.wait()

    row = pl.BlockSpec((rb, HG_W), lambda i: (nb - 1 - i, 0))
    one = pl.BlockSpec((1, HG_W), lambda i: (0, 0))
    any_spec = pl.BlockSpec(memory_space=pl.ANY)
    return pl.pallas_call(
        body, name="hgrn_bwd", grid=(nb,),
        in_specs=[row, row, row, row, pl.BlockSpec((ncb, HG_W, HG_DIM), lambda i: (nb - 1 - i, 0, 0)), one]
        + [any_spec] * n_x,
        out_specs=[row, row, row, one] + [any_spec] * n_x,
        out_shape=[jax.ShapeDtypeStruct((T, HG_W), MM)] * 3 + [jax.ShapeDtypeStruct((1, HG_W), F32)]
        + _x_out_shapes(xchg),
        scratch_shapes=[pltpu.VMEM((HG_HEADS, HG_DIM, HG_DIM), F32)] + _x_sems(xchg),
        compiler_params=_cparams(("arbitrary",)),
    )(hq, hf, hi, do, s0, lb, *xchg[0], *xchg[1])


def _silu_parts(x):
    sg = _sigmoid(x)
    return x * sg, sg * (1.0 + x * (1.0 - sg))


def _merge_fwd(attn, o, hg, bg, x, g_out, w_bra, w_brb, w_out, T, tm):
    def body(i, attn_ref, o_ref, hg_ref, bg_ref, x_ref, g_ref, wa_ref, wb_ref, wo_ref,
             x1_ref, ya_ref, yb_ref, m_ref, rec_ref):
        g = g_ref[...]

        def recurrent_out(rows):
            for h in range(HG_HEADS):
                ls = slice(h * HG_DIM, (h + 1) * HG_DIM)
                rec_ref[rows, ls] = (_rms(o_ref[rows, ls])[0] * g * _silu_parts(hg_ref[rows, ls])[0]).astype(MM)

        _by_chunks(tm, recurrent_out)
        ya_ref[...] = _dot(attn_ref[...], wa_ref[...])
        yb_ref[...] = jnp.dot(rec_ref[...], wb_ref[...], preferred_element_type=F32)

        def gate(rows):
            m_ref[rows, :] = (_sigmoid(bg_ref[rows, :D_MODEL]) * ya_ref[rows, :]
                              + _sigmoid(bg_ref[rows, D_MODEL:]) * yb_ref[rows, :]).astype(MM)

        _by_chunks(tm, gate)
        x1_ref[...] = x_ref[...] + jnp.dot(m_ref[...], wo_ref[...], preferred_element_type=F32)

    return _row_call("merge_fwd", body, T, tm, [attn, o, hg, bg, x], [g_out, w_bra, w_brb, w_out],
                     [(D_MODEL, F32), (D_MODEL, F32), (D_MODEL, F32), (D_MODEL, MM), (HG_W, MM)], [], VMEM_LIMIT)


def _ffn_fwd(x1, g_ffn, w_g, w_u, w_d, T, tm):
    def body(i, x1_ref, g_ref, wg_ref, wu_ref, wd_ref, x2_ref, gt_ref, up_ref, h2_ref, a_s):
        g = g_ref[...]

        def norm(rows):
            h2_ref[rows, :] = (_rms(x1_ref[rows, :])[0] * g).astype(MM)

        _by_chunks(tm, norm)
        gt_ref[...] = _dot_nt(h2_ref[...], wg_ref[...])
        up_ref[...] = _dot_nt(h2_ref[...], wu_ref[...])

        def act(rows):
            for cs in FFN_HALVES:
                a_s[rows, cs] = (_silu_parts(gt_ref[rows, cs])[0] * up_ref[rows, cs]).astype(MM)

        _by_chunks(tm, act)
        x2_ref[...] = x1_ref[...] + jnp.dot(a_s[...], wd_ref[...], preferred_element_type=F32)

    return _row_call("ffn_fwd", body, T, tm, [x1], [g_ffn, w_g, w_u, w_d],
                     [(D_MODEL, F32), (FFN, F32), (FFN, F32), (D_MODEL, MM)], [], VMEM_LIMIT,
                     scratch=[pltpu.VMEM((tm, FFN), MM)])


def _ple_loss(x2, p, tgt, g_pg, g_post, w_pg, w_pp, T, tm):
    def body(i, x2_ref, p_ref, t_ref, gpg_ref, gpo_ref, wpg_ref, wpp_ref,
             dx2_ref, loss_ref, dgpo_ref, dgpg_ref, dwpg_ref, dwpp_ref, u_s, n3_s, z_s, dz_s, du_s, dy_s, dn3_s):
        @pl.when(i == 0)
        def _():
            for ref in (loss_ref, dgpo_ref, dgpg_ref, dwpg_ref, dwpp_ref):
                ref[...] = jnp.zeros_like(ref)

        gpg, gpo = gpg_ref[...], gpo_ref[...]
        p_mm = p_ref[...].astype(MM)
        for d in range(N_DEV):
            u_s[:, d * HEAD_PAD:(d + 1) * HEAD_PAD] = jnp.dot(p_mm, wpp_ref[d], preferred_element_type=F32)

        def gate_input(rows):
            n3_s[rows, :] = (_rms(x2_ref[rows, :])[0] * gpg).astype(MM)

        _by_chunks(tm, gate_input)
        z_s[...] = jnp.dot(n3_s[...], wpg_ref[...], preferred_element_type=F32)

        def loss_and_back(rows):
            uh, ru = _rms(u_s[rows, :])
            e = uh * gpo
            gate = _sigmoid(z_s[rows, :])
            diff = x2_ref[rows, :] + gate * e - t_ref[rows, :]
            dy = diff * (1.0 / D_MODEL)
            de = dy * gate
            dz_s[rows, :] = (dy * e * gate * (1.0 - gate)).astype(MM)
            du_s[rows, :] = _rms_bwd(de * gpo, uh, ru).astype(MM)
            dy_s[rows, :] = dy
            loss_ref[...] += _fold8(diff * diff) * (0.5 / D_MODEL)
            dgpo_ref[...] += _fold8(de * uh)

        _by_chunks(tm, loss_and_back)
        dn3_s[...] = _dot_nt(dz_s[...], wpg_ref[...])

        def gate_norm_back(rows):
            x2h, r3 = _rms(x2_ref[rows, :])
            dn3 = dn3_s[rows, :]
            dx2_ref[rows, :] = dy_s[rows, :] + _rms_bwd(dn3 * gpg, x2h, r3)
            dgpg_ref[...] += _fold8(dn3 * x2h)

        _by_chunks(tm, gate_norm_back)
        dwpg_ref[...] += _dot_tn(n3_s[...], dz_s[...])
        for d in range(N_DEV):
            dwpp_ref[d] += _dot_tn(p_mm, du_s[:, d * HEAD_PAD:(d + 1) * HEAD_PAD])

    vec = ((8, D_MODEL), F32)
    wide = lambda dt: pltpu.VMEM((tm, D_MODEL), dt)
    return _row_call("ple_loss", body, T, tm, [x2, p, tgt], [g_pg, g_post, w_pg, w_pp], [(D_MODEL, F32)],
                     [vec, vec, vec, ((D_MODEL, D_MODEL), F32), ((N_DEV, PLE, HEAD_PAD), F32)], VMEM_LIMIT,
                     scratch=[wide(F32), wide(MM), wide(F32), wide(MM), wide(MM), wide(F32), wide(F32)])


def _ffn_bwd(dx2, x1, gt, up, g_ffn, w_g, w_u, w_d, T, tm):
    def body(i, dx2_ref, x1_ref, gt_ref, up_ref, g_ref, wg_ref, wu_ref, wd_ref,
             dx1_ref, a_ref, dgt_ref, dup_ref, dg_ref, da_s, dh2_s):
        @pl.when(i == 0)
        def _():
            dg_ref[...] = jnp.zeros_like(dg_ref)

        g = g_ref[...]
        da_s[...] = _dot_nt(dx2_ref[...], wd_ref[...])

        def act_back(rows):
            for cs in FFN_HALVES:
                up, da = up_ref[rows, cs], da_s[rows, cs]
                silu, dsilu = _silu_parts(gt_ref[rows, cs])
                dgt_ref[rows, cs] = (da * up * dsilu).astype(MM)
                dup_ref[rows, cs] = (da * silu).astype(MM)
                a_ref[rows, cs] = (silu * up).astype(MM)

        _by_chunks(tm, act_back)
        dh2_s[...] = (jnp.dot(dgt_ref[...], wg_ref[...], preferred_element_type=F32)
                      + jnp.dot(dup_ref[...], wu_ref[...], preferred_element_type=F32))

        def norm_back(rows):
            x1h, r = _rms(x1_ref[rows, :])
            dh2 = dh2_s[rows, :]
            dx1_ref[rows, :] = dx2_ref[rows, :] + _rms_bwd(dh2 * g, x1h, r)
            dg_ref[...] += _fold8(dh2 * x1h)

        _by_chunks(tm, norm_back)

    return _row_call("ffn_bwd", body, T, tm, [dx2, x1, gt, up], [g_ffn, w_g, w_u, w_d],
                     [(D_MODEL, F32), (FFN, MM), (FFN, MM), (FFN, MM)], [((8, D_MODEL), F32)], VMEM_LIMIT,
                     scratch=[pltpu.VMEM((tm, FFN), F32), pltpu.VMEM((tm, D_MODEL), F32)])


def _merge_bwd(dx1, ya, yb, bg, o, hg, attn, m, rec, g_out, w_bra, w_brb, w_out, T, tm, xchg=((), ())):
    def body(i, dx1_ref, ya_ref, yb_ref, bg_ref, o_ref, hg_ref, attn_ref, m_ref, rec_ref, g_ref, wa_ref, wb_ref, wo_ref,
             dattn_ref, do_ref, dhg_ref, dbg_ref, dg_ref, dwo_ref, dwa_ref, dwb_ref, dm_s, dya_s, dyb_s, drec_s):
        @pl.when(i == 0)
        def _():
            for ref in (dg_ref, dwo_ref, dwa_ref, dwb_ref):
                ref[...] = jnp.zeros_like(ref)

        g = g_ref[...]
        dx1 = dx1_ref[...].astype(MM)
        dm_s[...] = _dot_nt(dx1, wo_ref[...])

        def gate_back(rows):
            dm = dm_s[rows, :]
            ga, gb = _sigmoid(bg_ref[rows, :D_MODEL]), _sigmoid(bg_ref[rows, D_MODEL:])
            dya_s[rows, :] = (dm * ga).astype(MM)
            dyb_s[rows, :] = (dm * gb).astype(MM)
            dbg_ref[rows, :D_MODEL] = (dm * ya_ref[rows, :] * ga * (1.0 - ga)).astype(MM)
            dbg_ref[rows, D_MODEL:] = (dm * yb_ref[rows, :] * gb * (1.0 - gb)).astype(MM)

        _by_chunks(tm, gate_back)
        dwo_ref[...] += _dot_tn(m_ref[...], dx1)
        attn_mm = attn_ref[...].astype(MM)
        for d in range(N_DEV):
            ds = slice(d * HEAD_PAD, (d + 1) * HEAD_PAD)
            dwa_ref[d] += _dot_tn(attn_mm, dya_s[:, ds])
            dwb_ref[d] += _dot_tn(rec_ref[...], dyb_s[:, ds])
        dattn_ref[...] = _dot_nt(dya_s[...], wa_ref[...])
        drec_s[...] = _dot_nt(dyb_s[...], wb_ref[...])

        def recurrent_out_back(rows):
            for h in range(HG_HEADS):
                ls = slice(h * HG_DIM, (h + 1) * HG_DIM)
                oh, r = _rms(o_ref[rows, ls])
                silu, dsilu = _silu_parts(hg_ref[rows, ls])
                dr = drec_s[rows, ls]
                dhg_ref[rows, ls] = (dr * oh * g * dsilu).astype(MM)
                don = dr * silu
                dg_ref[...] += _fold8(don * oh)
                do_ref[rows, ls] = _rms_bwd(don * g, oh, r)

        _by_chunks(tm, recurrent_out_back)

    wide = lambda n, dt: pltpu.VMEM((tm, n), dt)
    return _row_call("merge_bwd", body, T, tm, [dx1, ya, yb, bg, o, hg, attn, m, rec], [g_out, w_bra, w_brb, w_out],
                     [(D_MODEL, F32), (HG_W, F32), (HG_W, MM), (2 * D_MODEL, MM)],
                     [((8, HG_DIM), F32), ((D_MODEL, D_MODEL), F32), ((N_DEV, MLA_HEADS * HEAD_PAD, HEAD_PAD), F32),
                      ((N_DEV, HG_W, HEAD_PAD), F32)], VMEM_LIMIT,
                     scratch=[wide(D_MODEL, F32), wide(D_MODEL, MM), wide(D_MODEL, MM), wide(HG_W, F32)], xchg=xchg)


def _flash_bwd(qf, kf, vf, o, do, lse, T, xchg=((), ())):
    tq = min(ATT_TILE, T)
    nq = T // tq

    qi_tab, ki_tab = _causal_pairs(nq, by_query=False)

    n_x, n_sib = _x_count(xchg), len(xchg[0])
    hp = ATT_HEADS
    n_heads, n_pairs = MLA_HEADS // hp, len(qi_tab)

    def body(qi_ref, ki_ref, q_ref, k_ref, v_ref, o_ref, do_ref, lse_ref, *rest):
        x_in, (dq_ref, dk_ref, dv_ref), rest = rest[:n_x], rest[n_x:n_x + 3], rest[n_x + 3:]
        x_out, x_sems = rest[:n_x], rest[n_x:]
        t = pl.program_id(1)
        qi, ki = qi_ref[t], ki_ref[t]
        if n_x:
            @pl.when((pl.program_id(0) == 0) & (t == 0))
            def _():
                for cp in _x_copies(n_sib, x_in, x_out, x_sems):
                    cp.start()

        @pl.when(t == 0)
        def _():
            dq_ref[...] = jnp.zeros_like(dq_ref)

        def step(first):
            halves = 2 if first and tq % (2 * HEAD_PAD) == 0 else 1
            w = tq // halves
            for hh in range(hp):
                hs = slice(hh * HEAD_PAD, (hh + 1) * HEAD_PAD)
                for part in range(halves):
                    keys, qs = slice(part * w, (part + 1) * w), slice(part * w, tq)
                    nq_ = tq - part * w
                    q, k, d_o = q_ref[qs, hs], k_ref[keys, hs], do_ref[qs, hs]
                    s = _dot_nt(q, k)
                    if first:
                        row = lax.broadcasted_iota(jnp.int32, (nq_, w), 0)
                        col = lax.broadcasted_iota(jnp.int32, (nq_, w), 1)
                        s = jnp.where(col <= row, s, NEG)
                    p = jnp.exp(s - lse_ref[qs, hh * HEAD_PAD:hh * HEAD_PAD + 1])
                    delta = jnp.sum(d_o * o_ref[qs, hs], axis=1, keepdims=True)
                    ds = p * (_dot_nt(d_o, v_ref[keys, hs]) - delta)
                    rows = pl.ds(pl.multiple_of(qi * tq + part * w, w), nq_)
                    dq_ref[rows, hs] += _dot(ds, k)
                    if first:
                        dv_ref[keys, hs] = _dot_tn(p, d_o)
                        dk_ref[keys, hs] = _dot_tn(ds, q)
                    else:
                        dv_ref[keys, hs] += _dot_tn(p, d_o)
                        dk_ref[keys, hs] += _dot_tn(ds, q)

        @pl.when(qi == ki)
        def _():
            step(True)

        @pl.when(qi > ki)
        def _():
            step(False)

        if n_x:
            @pl.when((pl.program_id(0) == n_heads - 1) & (t == n_pairs - 1))
            def _():
                for cp in _x_copies(n_sib, x_in, x_out, x_sems):
                    cp.wait()

    q_spec = pl.BlockSpec((tq, hp * HEAD_PAD), lambda h, t, qi_ref, ki_ref: (qi_ref[t], h))
    kv_spec = pl.BlockSpec((tq, hp * HEAD_PAD), lambda h, t, qi_ref, ki_ref: (ki_ref[t], h))
    any_spec = pl.BlockSpec(memory_space=pl.ANY)
    w = MLA_HEADS * HEAD_PAD
    grid_spec = pltpu.PrefetchScalarGridSpec(
        num_scalar_prefetch=2, grid=(n_heads, n_pairs),
        in_specs=[q_spec, kv_spec, kv_spec, q_spec, q_spec, q_spec] + [any_spec] * n_x,
        out_specs=[pl.BlockSpec((T, hp * HEAD_PAD), lambda h, t, qi_ref, ki_ref: (0, h)), kv_spec, kv_spec]
        + [any_spec] * n_x,
        scratch_shapes=_x_sems(xchg))
    return pl.pallas_call(
        body, name="flash_bwd", grid_spec=grid_spec,
        out_shape=[jax.ShapeDtypeStruct((T, w), F32)] * 3 + _x_out_shapes(xchg),
        compiler_params=_cparams(("arbitrary", "arbitrary")),
    )(jnp.asarray(qi_tab), jnp.asarray(ki_tab), qf, kf, vf, o, do, lse, *xchg[0], *xchg[1])


def _mla_heads_bwd(d_out, saved, g_pad, cos_t, sin_t, first):
    d_raw, dg = [], jnp.zeros((1, HEAD_PAD), F32)
    for h in range(MLA_HEADS):
        xh, r = saved[h]
        dy = d_out[:, h * HEAD_PAD:(h + 1) * HEAD_PAD]
        dn = dy * cos_t + _rope_swap(dy * sin_t, first)
        dg = dg + jnp.sum(dn * xh, axis=0, keepdims=True)
        d_raw.append(_rms_bwd(dn * g_pad, xh, r, QK_DIM))
    return d_raw, dg


def _mla_prep_bwd(cq, ckv, kr, pos, dqf, dkf, dvf, g_qa, g_kva, g_qn, g_kn, w_uq, w_ukv, T, tm):
    def body(i, cq_ref, ckv_ref, kr_ref, pos_ref, dq_ref, dk_ref, dv_ref,
             gqa_ref, gkva_ref, gqn_ref, gkn_ref, wuq_ref, wukv_ref,
             dcq_ref, dckv_ref, dkr_ref, dgqa_ref, dgkva_ref, dgqn_ref, dgkn_ref, dwuq_ref, dwukv_ref):
        cos_t, sin_t, first = _rope_tables(pos_ref[...], tm)
        cqh, rq = _rms(cq_ref[...])
        ckvh, rkv = _rms(ckv_ref[...])
        cqn, ckvn = cqh * gqa_ref[...], ckvh * gkva_ref[...]
        q_raw, k_raw, _ = _mla_raw_heads(cqn, ckvn, kr_ref[...], wuq_ref, wukv_ref, tm)
        _, q_saved = _mla_heads_fwd(q_raw, gqn_ref[...], cos_t, sin_t, first)
        _, k_saved = _mla_heads_fwd(k_raw, gkn_ref[...], cos_t, sin_t, first)
        dq_heads, dgqn = _mla_heads_bwd(dq_ref[...] * ATT_SCALE, q_saved, gqn_ref[...], cos_t, sin_t, first)
        dk_heads, dgkn = _mla_heads_bwd(dk_ref[...], k_saved, gkn_ref[...], cos_t, sin_t, first)
        lane = lax.broadcasted_iota(jnp.int32, (tm, HEAD_PAD), 1)
        nope = lane < QK_NOPE
        dcqn = jnp.zeros((tm, Q_RANK), F32)
        dckvn = jnp.zeros((tm, KV_RANK), F32)
        dkr = jnp.zeros((tm, HEAD_PAD), F32)
        cqn_mm, ckvn_mm = cqn.astype(MM), ckvn.astype(MM)
        for h in range(MLA_HEADS):
            hs = slice(h * HEAD_PAD, (h + 1) * HEAD_PAD)
            dq_h = dq_heads[h].astype(MM)
            dkv_h = jnp.where(nope, dk_heads[h], pltpu.roll(dv_ref[:, hs], V_DIM, 1)).astype(MM)
            _acc(dwuq_ref.at[h], i, _dot_tn(dq_h, cqn_mm))
            _acc(dwukv_ref.at[h], i, _dot_tn(ckvn_mm, dkv_h))
            dcqn = dcqn + jnp.dot(dq_h, wuq_ref[h], preferred_element_type=F32)
            dckvn = dckvn + lax.dot_general(dkv_h, wukv_ref[h], (((1,), (1,)), ((), ())), preferred_element_type=F32)
            dkr = dkr + dk_heads[h]
        dkr_ref[...] = jnp.where((lane >= QK_NOPE) & (lane < QK_DIM), dkr, 0.0).astype(MM)
        dcq_ref[...] = _rms_bwd(dcqn * gqa_ref[...], cqh, rq).astype(MM)
        dckv_ref[...] = _rms_bwd(dckvn * gkva_ref[...], ckvh, rkv).astype(MM)
        _acc(dgqa_ref, i, jnp.sum(dcqn * cqh, axis=0, keepdims=True))
        _acc(dgkva_ref, i, jnp.sum(dckvn * ckvh, axis=0, keepdims=True))
        _acc(dgqn_ref, i, dgqn)
        _acc(dgkn_ref, i, dgkn)

    return _row_call(
        "mla_prep_bwd", body, T, tm, [cq, ckv, kr, pos, dqf, dkf, dvf], [g_qa, g_kva, g_qn, g_kn, w_uq, w_ukv],
        [(Q_RANK, MM), (KV_RANK, MM), (HEAD_PAD, MM)],
        [((1, Q_RANK), F32), ((1, KV_RANK), F32), ((1, HEAD_PAD), F32), ((1, HEAD_PAD), F32),
         ((MLA_HEADS, HEAD_PAD, Q_RANK), F32), ((MLA_HEADS, KV_RANK, HEAD_PAD), F32)], VMEM_LIMIT)


def _in_proj_bwd(x, dx1, dsecs, g_mix, w_in, T, tm):
    def body(i, x_ref, dx1_ref, *rest):
        d_refs, (g_ref, w_ref, dx_ref, dp_ref, dg_ref, dh_s) = rest[:len(COL_SECTIONS)], rest[len(COL_SECTIONS):]

        @pl.when(i == 0)
        def _():
            dg_ref[...] = jnp.zeros_like(dg_ref)

        g = g_ref[...]

        def join_and_cut(rows):
            pieces = [(d_ref[rows, QK_NOPE:QK_DIM] if n == QK_ROPE else d_ref[rows, :]).astype(F32)
                      for (_, n), d_ref in zip(COL_SECTIONS, d_refs)]
            dproj = jnp.concatenate(pieces, axis=1)
            for d in range(N_DEV):
                dp_ref[d, rows, :] = dproj[:, d * IN_BLOCK:(d + 1) * IN_BLOCK].astype(MM)

        _by_chunks(tm, join_and_cut)
        dh = jnp.dot(dp_ref[0], w_ref[0], preferred_element_type=F32)
        for d in range(1, N_DEV):
            dh = dh + jnp.dot(dp_ref[d], w_ref[d], preferred_element_type=F32)
        dh_s[...] = dh

        def norm_back(rows):
            xh, r = _rms(x_ref[rows, :])
            dh_c = dh_s[rows, :]
            dx_ref[rows, :] = dx1_ref[rows, :] + _rms_bwd(dh_c * g, xh, r)
            dg_ref[...] += _fold8(dh_c * xh)

        _by_chunks(tm, norm_back)

    in_specs = [pl.BlockSpec((tm, a.shape[1]), lambda i: (i, 0)) for a in [x, dx1, *dsecs]]
    in_specs += [pl.BlockSpec(g_mix.shape, lambda i: (0, 0)),
                 pl.BlockSpec(w_in.shape, lambda i: (0, 0, 0), pipeline_mode=pl.Buffered(1))]

    def kern(*refs):
        body(pl.program_id(0), *refs)

    return pl.pallas_call(
        kern, name="in_proj_bwd", grid=(T // tm,), in_specs=in_specs,
        out_specs=[pl.BlockSpec((tm, D_MODEL), lambda i: (i, 0)),
                   pl.BlockSpec((N_DEV, tm, IN_BLOCK), lambda i: (0, i, 0)),
                   pl.BlockSpec((8, D_MODEL), lambda i: (0, 0))],
        out_shape=[jax.ShapeDtypeStruct((T, D_MODEL), F32), jax.ShapeDtypeStruct((N_DEV, T, IN_BLOCK), MM),
                   jax.ShapeDtypeStruct((8, D_MODEL), F32)],
        scratch_shapes=[pltpu.VMEM((tm, D_MODEL), F32)],
        compiler_params=_cparams(("arbitrary",), VMEM_LIMIT),
    )(x, dx1, *dsecs, g_mix, w_in)


def _pick_block(n, cap):
    best = None
    for cand in range(128, min(n, cap) + 1, 128):
        if n % cand == 0:
            best = cand
    return n if best is None else best


def _matmul_tn(name, a, b):
    T, M = a.shape
    N = b.shape[1]
    bm, bk = _pick_block(M, 1408), min(DW_TOKENS, T)
    bn = _pick_block(N, 2560)

    def body(a_ref, b_ref, c_ref):
        @pl.when(pl.program_id(2) == 0)
        def _():
            c_ref[...] = jnp.zeros_like(c_ref)

        c_ref[...] += _dot_tn(a_ref[...], b_ref[...])

    return pl.pallas_call(
        body, name=name, grid=(M // bm, N // bn, T // bk),
        in_specs=[pl.BlockSpec((bk, bm), lambda i, j, k: (k, i)), pl.BlockSpec((bk, bn), lambda i, j, k: (k, j))],
        out_specs=pl.BlockSpec((bm, bn), lambda i, j, k: (i, j)), out_shape=jax.ShapeDtypeStruct((M, N), F32),
        compiler_params=_cparams(("parallel", "parallel", "arbitrary"), VMEM_LIMIT),
    )(a, b)


def _matmul_tn_blocks(name, a, b):
    T, M = a.shape
    nd, _, c = b.shape
    bm, bk = _pick_block(M, 512), min(DW_TOKENS, T)

    def body(a_ref, b_ref, c_ref):
        @pl.when(pl.program_id(1) == 0)
        def _():
            c_ref[...] = jnp.zeros_like(c_ref)

        a_blk = a_ref[...].astype(MM)
        for d in range(nd):
            c_ref[d] += _dot_tn(b_ref[d], a_blk)

    return pl.pallas_call(
        body, name=name, grid=(M // bm, T // bk),
        in_specs=[pl.BlockSpec((bk, bm), lambda i, k: (k, i)), pl.BlockSpec((nd, bk, c), lambda i, k: (0, k, 0))],
        out_specs=pl.BlockSpec((nd, c, bm), lambda i, k: (0, 0, i)),
        out_shape=jax.ShapeDtypeStruct((nd, c, M), F32),
        compiler_params=_cparams(("parallel", "arbitrary"), VMEM_LIMIT),
    )(a, b)


def _pad_gain(g, n):
    return jnp.pad(g.reshape(1, -1), ((0, 0), (0, n - g.shape[-1])))


GROUP_A = ("w_ffn_gate", "w_ffn_up", "w_ffn_down", "w_ple_gate", "w_ple_proj")
GROUP_B = ("w_branch", "w_out")
GROUP_C = ("w_in", "w_uq", "w_ukv")
EARLY = GROUP_C
LATE = GROUP_B + GROUP_A
TRANSPOSED = ("w_in", "w_uq", "w_ffn_gate", "w_ffn_up")


def _local_step(x, p, pos, tgt, small, big, late_blocks=None, core=None):
    T = x.shape[0]
    tm = min(ROW_TILE, T)
    tw = min(WIDE_TILE, T)
    w_in = big["w_in"]
    w_uq = jnp.pad(big["w_uq"], ((0, 0), (0, HEAD_PAD - QK_DIM), (0, 0)))
    w_ukv = big["w_ukv"]

    g_mix, g_qa, g_kva = small["mix_norm_g"], small["q_a_norm_g"], small["kv_a_norm_g"]
    g_qn, g_kn = _pad_gain(small["q_norm_g"], HEAD_PAD), _pad_gain(small["k_norm_g"], HEAD_PAD)
    g_out, g_ffn = small["hg_out_norm_g"], small["ffn_norm_g"]
    g_pg, g_post = small["ple_gate_norm_g"], small["ple_post_norm_g"]
    logits = small["hg_lb_logits"]
    lb = _lower_bound(logits)

    h, cq, ckv, kr, hq, hf, hi, hg, bg = _in_proj_fwd(x, g_mix, w_in, T, tw)
    qf, kf, vf = _mla_prep_fwd(cq, ckv, kr, pos, g_qa, g_kva, g_qn, g_kn, w_uq, w_ukv, T, tw)
    if late_blocks is None:
        attn, lse = _flash_fwd(qf, kf, vf, T)
    else:
        attn, lse, *late = _flash_fwd(qf, kf, vf, T, ag_blocks=[late_blocks[n] for n in LATE])
        big = {**big, **dict(zip(LATE, late))}
    o, s0 = _hgrn_fwd(hq, hf, hi, lb, T)
    w_branch = jnp.moveaxis(big["w_branch"].reshape(N_DEV, 2, HG_W, HEAD_PAD), 0, 2).reshape(2, HG_W, D_MODEL)
    w_bra = jnp.pad(w_branch[0].reshape(MLA_HEADS, V_DIM, D_MODEL),
                    ((0, 0), (0, HEAD_PAD - V_DIM), (0, 0))).reshape(MLA_HEADS * HEAD_PAD, D_MODEL)
    w_brb = w_branch[1]
    w_out = big["w_out"].reshape(D_MODEL, D_MODEL)
    w_g, w_u = big["w_ffn_gate"].reshape(FFN, D_MODEL), big["w_ffn_up"].reshape(FFN, D_MODEL)
    w_d = big["w_ffn_down"].reshape(FFN, D_MODEL)
    w_pg, w_pp = big["w_ple_gate"].reshape(D_MODEL, D_MODEL), big["w_ple_proj"]
    x1, ya, yb, m, rec = _merge_fwd(attn, o, hg, bg, x, g_out, w_bra, w_brb, w_out, T, tw)
    x2, gt, up, h2 = _ffn_fwd(x1, g_ffn, w_g, w_u, w_d, T, tw)
    dx2, loss_p, dg_post, dg_pg, d_pg, d_pp = _ple_loss(x2, p, tgt, g_pg, g_post, w_pg, w_pp, T, tw)

    grads, sibs, gots = {}, {}, {}
    dist = core is not None
    pick = lambda names: [grads[n] for n in names] if dist else ()

    def partials(tag, names, got):
        if not dist:
            return ()
        sibs.update(zip(names, got))
        return _chip_partials("rs_partial_" + tag, pick(names), got, core)

    dx1, a, dgt, dup, dg_ffn = _ffn_bwd(dx2, x1, gt, up, g_ffn, w_g, w_u, w_d, T, tm)
    grads["w_ffn_gate"] = _matmul_tn("dw_gate", dgt, h2).reshape(N_DEV, -1, D_MODEL)
    grads["w_ffn_up"] = _matmul_tn("dw_up", dup, h2).reshape(N_DEV, -1, D_MODEL)
    grads["w_ffn_down"] = _matmul_tn("dw_down", a, dx2).reshape(N_DEV, -1, D_MODEL)
    grads["w_ple_gate"] = d_pg.reshape(N_DEV, -1, D_MODEL)
    grads["w_ple_proj"] = d_pp

    dattn, do, dhg, dbg, dg_out, d_out, d_bra, d_brb, *sib_a = _merge_bwd(
        dx1, ya, yb, bg, o, hg, attn, m, rec, g_out, w_bra, w_brb, w_out, T, tm, xchg=(pick(GROUP_A), ()))
    parts_a = partials("a", GROUP_A, sib_a)
    d_bra = d_bra.reshape(N_DEV, MLA_HEADS, HEAD_PAD, HEAD_PAD)[:, :, :V_DIM].reshape(N_DEV, HG_W, HEAD_PAD)
    grads["w_branch"] = jnp.concatenate([d_bra, d_brb], axis=1)
    grads["w_out"] = d_out.reshape(N_DEV, -1, D_MODEL)

    dhq, dhf, dhi, dlb, *got = _hgrn_bwd(hq, hf, hi, do, s0, lb, T, xchg=(pick(GROUP_B), parts_a))
    sib_b, got_a = got[:len(GROUP_B)], got[len(GROUP_B):]
    parts_b = partials("b", GROUP_B, sib_b)
    dqf, dkf, dvf, *got_b = _flash_bwd(qf, kf, vf, attn, dattn, lse, T, xchg=((), parts_b))
    (dcq, dckv, dkr, dg_qa, dg_kva, dg_qn, dg_kn, d_uq, d_ukv) = _mla_prep_bwd(
        cq, ckv, kr, pos, dqf, dkf, dvf, g_qa, g_kva, g_qn, g_kn, w_uq, w_ukv, T, tw)
    grad_x, dproj, dg_mix = _in_proj_bwd(x, dx1, [dcq, dckv, dkr, dhq, dhf, dhi, dhg, dbg], g_mix, w_in, T, tw)
    grads["w_in"] = _matmul_tn_blocks("dw_in", h, dproj)
    grads["w_uq"] = d_uq[:, :QK_DIM]
    grads["w_ukv"] = d_ukv
    if dist:
        gots.update(zip(GROUP_A, got_a))
        gots.update(zip(GROUP_B, got_b))

    dl0 = dlb * lb * (1.0 - lb)
    small_g = {
        "mix_norm_g": dg_mix, "q_a_norm_g": dg_qa, "kv_a_norm_g": dg_kva, "q_norm_g": dg_qn, "k_norm_g": dg_kn,
        "hg_lb_logits": jnp.concatenate([dl0, -dl0], axis=0), "hg_out_norm_g": dg_out,
        "ffn_norm_g": dg_ffn, "ple_gate_norm_g": dg_pg, "ple_post_norm_g": dg_post,
    }
    return loss_p, grad_x, small_g, grads, sibs, gots


def _lower_bound(logits):
    def body(l_ref, lb_ref):
        l = l_ref[...]
        mx = jnp.max(l, axis=0, keepdims=True)
        e = jnp.exp(l - mx)
        lb_ref[...] = e[0:1] / jnp.sum(e, axis=0, keepdims=True)

    return pl.pallas_call(body, name="lower_bound", out_shape=jax.ShapeDtypeStruct((1, HG_W), F32))(logits)


def _my_place():
    return lax.axis_index("x"), lax.axis_index("y"), lax.axis_index("c")


def _all_gather(name, blocks):
    n = len(blocks)

    def body(*refs):
        x_refs, out_refs, sems = refs[:n], refs[n:2 * n], refs[2 * n:]
        _ag_start(x_refs, out_refs, sems)
        _ag_finish(x_refs, out_refs, sems)

    any_spec = pl.BlockSpec(memory_space=pl.ANY)
    return pl.pallas_call(
        body, name=name, out_shape=_ag_out_shapes(blocks),
        in_specs=[any_spec] * n, out_specs=[any_spec] * n, scratch_shapes=_ag_sems(n),
    )(*blocks)


def _ag_out_shapes(blocks):
    return [jax.ShapeDtypeStruct((N_DEV,) + b.shape, b.dtype) for b in blocks]


def _ag_sems(n):
    return [pltpu.SemaphoreType.DMA((7 * n,)), pltpu.SemaphoreType.DMA((7 * n,)), pltpu.SemaphoreType.DMA((n,))]


def _ag_parts(x_refs, out_refs, sems):
    send_sems, recv_sems, local_sems = sems
    x, y, c = _my_place()
    me, sibling = (x, y, c), (x, y, 1 - c)
    chips = [(1 - x, y), (x, 1 - y), (1 - x, 1 - y)]
    n = len(x_refs)

    def copy(a, k, block, to, own=False):
        px, py, pc = block
        dst = out_refs[a].at[4 * px + 2 * py + pc]
        return pltpu.make_async_remote_copy(
            src_ref=x_refs[a] if own else dst, dst_ref=dst, send_sem=send_sems.at[7 * a + k],
            recv_sem=recv_sems.at[7 * a + k], device_id=to, device_id_type=MESH_ID)

    mine = [pltpu.make_async_copy(x_refs[a], out_refs[a].at[4 * x + 2 * y + c], local_sems.at[a]) for a in range(n)]
    first = []
    for a in range(n):
        first.append(copy(a, 0, me, sibling, own=True))
        first += [copy(a, 1 + j, me, (*chip, c), own=True) for j, chip in enumerate(chips)]
    return copy, mine, first, me, sibling, chips, c, n


def _ag_start(x_refs, out_refs, sems):
    _, mine, first, *_ = _ag_parts(x_refs, out_refs, sems)
    for cp in mine + first:
        cp.start()


def _ag_finish(x_refs, out_refs, sems):
    copy, mine, first, me, sibling, chips, c, n = _ag_parts(x_refs, out_refs, sems)
    passed = []
    for j, chip in enumerate(chips):
        for a in range(n):
            copy(a, 1 + j, (*chip, c), me).wait_recv()
            passed.append(copy(a, 4 + j, (*chip, c), sibling))
            passed[-1].start()
    for a in range(n):
        copy(a, 0, sibling, me).wait_recv()
    for j, chip in enumerate(chips):
        for a in range(n):
            copy(a, 4 + j, (*chip, 1 - c), me).wait_recv()
    for cp in first + passed:
        cp.wait_send()
    for cp in mine:
        cp.wait()


N_PARTS = 4


def _part_spec(rows, cols, t_pos, lead_block=(), lead_index=lambda *args: ()):
    if rows % (16 * N_PARTS) == 0:
        axis, shape, count = 0, (rows // N_PARTS, cols), N_PARTS
    elif cols % (128 * N_PARTS) == 0:
        axis, shape, count = 1, (rows, cols // N_PARTS), N_PARTS
    else:
        axis, shape, count = 0, (rows, cols), 1

    def index(*args):
        i = jnp.minimum(args[t_pos], count - 1)
        return (*lead_index(*args), *((i, 0) if axis == 0 else (0, i)))

    return pl.BlockSpec((*lead_block, *shape), index)


def _chip_partials(name, gs, sibs, c_idx):
    n = len(gs)

    def body(c_ref, *refs):
        for g_ref, sib_ref, out_ref in zip(refs[:n], refs[n:2 * n], refs[2 * n:]):
            out_ref[...] = (g_ref[...] + sib_ref[...]).astype(MM)

    own = [_part_spec(*g.shape[1:], 1, (1,), lambda j, t, c_ref: (2 * j + c_ref[0],)) for g in gs]
    by_chip = [_part_spec(*g.shape[1:], 1, (1,), lambda j, t, c_ref: (j,)) for g in gs]
    grid_spec = pltpu.PrefetchScalarGridSpec(
        num_scalar_prefetch=1, grid=(4, N_PARTS), in_specs=own + by_chip, out_specs=by_chip)
    return pl.pallas_call(
        body, name=name, grid_spec=grid_spec, out_shape=[jax.ShapeDtypeStruct((4,) + g.shape[1:], MM) for g in gs],
        compiler_params=_cparams(("arbitrary", "arbitrary"), VMEM_LIMIT),
    )(c_idx, *gs, *sibs)


def _x_count(xchg):
    return len(xchg[0]) + len(xchg[1])


def _x_out_shapes(xchg):
    return ([jax.ShapeDtypeStruct((4,) + g.shape[1:], g.dtype) for g in xchg[0]]
            + [jax.ShapeDtypeStruct((3,) + p.shape[1:], p.dtype) for p in xchg[1]])


def _x_sems(xchg):
    n = 4 * len(xchg[0]) + 3 * len(xchg[1])
    return [pltpu.SemaphoreType.DMA((n,)), pltpu.SemaphoreType.DMA((n,))] if n else []


def _x_copies(n_sib, in_refs, out_refs, sems):
    if not in_refs:
        return []
    send_sems, recv_sems = sems
    x, y, c = _my_place()
    chips = [(1 - x, y), (x, 1 - y), (1 - x, 1 - y)]
    copies = []

    def add(src, dst, to):
        k = len(copies)
        copies.append(pltpu.make_async_remote_copy(
            src_ref=src, dst_ref=dst, send_sem=send_sems.at[k], recv_sem=recv_sems.at[k], device_id=to,
            device_id_type=MESH_ID))

    for a, (src, dst) in enumerate(zip(in_refs, out_refs)):
        if a < n_sib:
            for j in range(4):
                add(src.at[2 * j + 1 - c], dst.at[j], (x, y, 1 - c))
        else:
            for k, (px, py) in enumerate(chips):
                add(src.at[2 * px + py], dst.at[k], (px, py, c))
    return copies


def _reduce_chips(name, gs, gather):
    n, n_ag = len(gs), len(gather)

    def body(*refs):
        g_refs, ag_in, refs = refs[:n], refs[n:n + n_ag], refs[n + n_ag:]
        sib_refs, got_refs, ag_out, refs = refs[:n], refs[n:2 * n], refs[2 * n:2 * n + n_ag], refs[2 * n + n_ag:]
        own_s, oth_s, part_s, refs = refs[:n], refs[n:2 * n], refs[2 * n:3 * n], refs[3 * n:]
        sib_send, sib_recv, chip_send, chip_recv, load_sems = refs[:5]
        x, y, c = _my_place()
        chips = [(1 - x, y), (x, 1 - y), (1 - x, 1 - y), (x, y)]
        slot = lambda k: 2 * chips[k][0] + chips[k][1]

        def to_sibling(a, k):
            return pltpu.make_async_remote_copy(
                src_ref=g_refs[a].at[2 * slot(k) + 1 - c], dst_ref=sib_refs[a].at[slot(k)],
                send_sem=sib_send.at[4 * a + k], recv_sem=sib_recv.at[4 * a + k], device_id=(x, y, 1 - c),
                device_id_type=MESH_ID)

        def to_chip(a, k):
            return pltpu.make_async_remote_copy(
                src_ref=part_s[a].at[k], dst_ref=got_refs[a].at[k], send_sem=chip_send.at[3 * a + k],
                recv_sem=chip_recv.at[3 * a + k], device_id=(*chips[k], c), device_id_type=MESH_ID)

        def load_own(a, k):
            return pltpu.make_async_copy(g_refs[a].at[2 * slot(k) + c], own_s[a].at[k], load_sems.at[6 * a + k])

        def load_other(a, k):
            return pltpu.make_async_copy(sib_refs[a].at[slot(k)], oth_s[a].at[k], load_sems.at[6 * a + 3 + k])

        if n_ag:
            _ag_start(ag_in, ag_out, refs[5:])
        for k in range(4):
            for a in range(n):
                to_sibling(a, k).start()
        for k in range(3):
            for a in range(n):
                load_own(a, k).start()
        for k in range(3):
            for a in range(n):
                to_sibling(a, k).wait_recv()
                load_other(a, k).start()
            for a in range(n):
                load_own(a, k).wait()
                load_other(a, k).wait()
                part_s[a][k] = (own_s[a][k] + oth_s[a][k]).astype(MM)
                to_chip(a, k).start()
        if n_ag:
            _ag_finish(ag_in, ag_out, refs[5:])
        for a in range(n):
            to_sibling(a, 3).wait_recv()
        for k in range(3):
            for a in range(n):
                to_chip(a, k).wait()
        for k in range(4):
            for a in range(n):
                to_sibling(a, k).wait_send()

    any_spec = pl.BlockSpec(memory_space=pl.ANY)
    shapes = [g.shape[1:] for g in gs]
    dma_sems = lambda count: pltpu.SemaphoreType.DMA((count,))
    return pl.pallas_call(
        body, name=name,
        out_shape=([jax.ShapeDtypeStruct((4,) + s, F32) for s in shapes]
                   + [jax.ShapeDtypeStruct((3,) + s, MM) for s in shapes] + _ag_out_shapes(gather)),
        in_specs=[any_spec] * (n + n_ag), out_specs=[any_spec] * (2 * n + n_ag),
        scratch_shapes=([pltpu.VMEM((3,) + s, F32) for s in shapes] * 2 + [pltpu.VMEM((3,) + s, MM) for s in shapes]
                        + [dma_sems(4 * n), dma_sems(4 * n), dma_sems(3 * n), dma_sems(3 * n), dma_sems(6 * n)]
                        + (_ag_sems(n_ag) if n_ag else [])),
        compiler_params=_cparams((), VMEM_LIMIT),
    )(*gs, *gather)


def _adamw_math(w, g, m, v):
    m = ADAM_B1 * m + (1.0 - ADAM_B1) * g
    v = ADAM_B2 * v + (1.0 - ADAM_B2) * jnp.square(g)
    m_hat = m / (1.0 - ADAM_B1 ** ADAM_STEP)
    v_hat = v / (1.0 - ADAM_B2 ** ADAM_STEP)
    delta = -ADAM_LR * (m_hat / (jnp.sqrt(v_hat) + ADAM_EPS) + ADAM_WD * w)
    return delta, m, v


def _sum_adamws(name, gs, sibs, gots, ws, ms, vs, slot_idx, chip_idx):
    n = len(gs)

    def body(s_ref, j_ref, *refs):
        ins, outs = refs[:6 * n], refs[6 * n:]
        for a in range(n):
            g_ref, sib_ref, got_ref, w_ref, m_ref, v_ref = (ins[k * n + a] for k in range(6))
            go_ref, d_ref, m2_ref, v2_ref = outs[4 * a:4 * a + 4]
            grad = g_ref[0] + sib_ref[0]
            for k in range(3):
                grad = grad + got_ref[k].astype(F32)
            go_ref[...] = grad
            d_ref[...], m2_ref[...], v2_ref[...] = _adamw_math(w_ref[...], grad, m_ref[...], v_ref[...])

    shapes = [g.shape[1:] for g in gs]
    flat = [_part_spec(*s, 0) for s in shapes]
    in_specs = ([_part_spec(*s, 0, (1,), lambda t, s_ref, j_ref: (s_ref[0],)) for s in shapes]
                + [_part_spec(*s, 0, (1,), lambda t, s_ref, j_ref: (j_ref[0],)) for s in shapes]
                + [_part_spec(*s, 0, (3,), lambda t, s_ref, j_ref: (0,)) for s in shapes] + flat * 3)
    grid_spec = pltpu.PrefetchScalarGridSpec(
        num_scalar_prefetch=2, grid=(N_PARTS,), in_specs=in_specs, out_specs=[f for f in flat for _ in range(4)])
    res = pl.pallas_call(
        body, name=name, grid_spec=grid_spec,
        out_shape=[jax.ShapeDtypeStruct(s, F32) for s in shapes for _ in range(4)],
        compiler_params=_cparams(("arbitrary",), VMEM_LIMIT),
    )(slot_idx, chip_idx, *gs, *sibs, *gots, *ws, *ms, *vs)
    return [res[4 * a:4 * a + 4] for a in range(n)]


BIG = ("w_in", "w_uq", "w_ukv", "w_branch", "w_out", "w_ffn_gate", "w_ffn_up", "w_ffn_down", "w_ple_gate", "w_ple_proj")
SMALL = (
    ("mix_norm_g", 0, 1, 1024), ("q_a_norm_g", 1, 1, 384), ("kv_a_norm_g", 2, 1, 256), ("q_norm_g", 3, 1, 96),
    ("k_norm_g", 4, 1, 96), ("hg_lb_logits", 5, 2, 512), ("hg_out_norm_g", 7, 1, 128), ("ffn_norm_g", 8, 1, 1024),
    ("ple_gate_norm_g", 9, 1, 1024), ("ple_post_norm_g", 10, 1, 1024),
)
SLAB_ROWS, LOSS_ROW = 16, 15


def _pack_partials(small_g, loss_p):
    def body(*refs):
        val_refs, loss_ref, out_ref = refs[:len(SMALL)], refs[len(SMALL)], refs[len(SMALL) + 1]
        out_ref[...] = jnp.zeros_like(out_ref)
        for (_, r0, rows, cols), ref in zip(SMALL, val_refs):
            val = ref[...]
            if val.shape[0] != rows:
                val = jnp.sum(val, axis=0, keepdims=True)
            out_ref[r0:r0 + rows, :cols] = val[:, :cols]
        out_ref[LOSS_ROW:LOSS_ROW + 1, :HEAD_PAD] = jnp.full((1, HEAD_PAD), jnp.sum(loss_ref[...]), F32)

    return pl.pallas_call(
        body, name="pack_partials", out_shape=jax.ShapeDtypeStruct((SLAB_ROWS, D_MODEL), F32),
    )(*[small_g[n] for n, *_ in SMALL], loss_p)


def _adamw_small(parts, ws, ms, vs):
    n = len(SMALL)

    def body(p_ref, *refs):
        ins, loss_ref, outs = refs[:3 * n], refs[3 * n], refs[3 * n + 1:]
        total = p_ref[0]
        for d in range(1, N_DEV):
            total = total + p_ref[d]
        loss_ref[...] = total[LOSS_ROW:LOSS_ROW + 1, 0:1]
        for a, (_, r0, rows, cols) in enumerate(SMALL):
            g = total[r0:r0 + rows, :cols]
            outs[4 * a][...] = g
            outs[4 * a + 1][...], outs[4 * a + 2][...], outs[4 * a + 3][...] = _adamw_math(
                ins[a][...], g, ins[n + a][...], ins[2 * n + a][...])

    shapes = [jax.ShapeDtypeStruct((rows, cols), F32) for _, _, rows, cols in SMALL]
    res = pl.pallas_call(
        body, name="adamw_small", out_shape=[jax.ShapeDtypeStruct((1, 1), F32)] + [s for s in shapes for _ in range(4)],
    )(parts, *ws, *ms, *vs)
    return res[0], [res[1 + 4 * a:5 + 4 * a] for a in range(n)]


_WEIGHTS = ["mix_norm_g", "w_in", "q_a_norm_g", "w_uq", "kv_a_norm_g", "w_ukv", "q_norm_g", "k_norm_g", "hg_lb_logits",
            "hg_out_norm_g", "w_branch", "w_out", "ffn_norm_g", "w_ffn_gate", "w_ffn_up", "w_ffn_down",
            "ple_gate_norm_g", "w_ple_gate", "w_ple_proj", "ple_post_norm_g"]


def _step(x, p, positions, tgt, w, m, v):
    small_names = [n for n, *_ in SMALL]
    T = x.shape[1]
    px, py, pc = _my_place()
    as_idx = lambda t: jnp.reshape(t, (1,)).astype(jnp.int32)

    def two_d(n, t):
        t = t.reshape(-1, t.shape[-1])
        return t.T if n in TRANSPOSED else t

    def full_shape(n, t):
        return (t.T if n in TRANSPOSED else t).reshape(w[n].shape)

    blocks = {n: two_d(n, w[n]).astype(MM) for n in BIG}
    big = dict(zip(EARLY, _all_gather("ag_weights", [blocks[n] for n in EARLY])))
    small = {n: (w[n] if n == "hg_lb_logits" else w[n].reshape(1, -1)) for n in small_names}

    loss_p, grad_x, small_g, grads, sibs, gots = _local_step(
        x[0], p[0, 0], positions.reshape(T, 1), tgt[0], small, big, late_blocks=blocks, core=as_idx(pc))

    *res_c, slabs = _reduce_chips("rs_chips", [grads[n] for n in GROUP_C], [_pack_partials(small_g, loss_p)])
    sibs.update(zip(GROUP_C, res_c[:len(GROUP_C)]))
    gots.update(zip(GROUP_C, res_c[len(GROUP_C):]))
    out_g, out_d, out_m, out_v = {}, {}, {}, {}
    for tag, names in (("ab", GROUP_A + GROUP_B), ("c", GROUP_C)):
        pick = lambda table: [table[n] for n in names]
        res = _sum_adamws("adamw_" + tag, pick(grads), pick(sibs), pick(gots), [two_d(n, w[n]) for n in names],
                          [two_d(n, m[n]) for n in names], [two_d(n, v[n]) for n in names],
                          as_idx(4 * px + 2 * py + pc), as_idx(2 * px + py))
        for n, r in zip(names, res):
            out_g[n], out_d[n], out_m[n], out_v[n] = [full_shape(n, t) for t in r]

    loss, res = _adamw_small(slabs, *([t[n] for n in small_names] for t in (w, m, v)))
    for n, r in zip(small_names, res):
        out_g[n], out_d[n], out_m[n], out_v[n] = r

    outs = [loss.reshape(()), grad_x[None]]
    for table in (out_g, out_d, out_m, out_v):
        outs += [table[n] for n in _WEIGHTS]
    return tuple(outs)


def kernel(x, p, positions, mix_norm_g, w_in, q_a_norm_g, w_uq, kv_a_norm_g, w_ukv, q_norm_g, k_norm_g, hg_lb_logits, hg_out_norm_g, w_branch, w_out, ffn_norm_g, w_ffn_gate, w_ffn_up, w_ffn_down, ple_gate_norm_g, w_ple_gate, w_ple_proj, ple_post_norm_g, loss_target, m_mix_norm_g, m_w_in, m_q_a_norm_g, m_w_uq, m_kv_a_norm_g, m_w_ukv, m_q_norm_g, m_k_norm_g, m_hg_lb_logits, m_hg_out_norm_g, m_w_branch, m_w_out, m_ffn_norm_g, m_w_ffn_gate, m_w_ffn_up, m_w_ffn_down, m_ple_gate_norm_g, m_w_ple_gate, m_w_ple_proj, m_ple_post_norm_g, v_mix_norm_g, v_w_in, v_q_a_norm_g, v_w_uq, v_kv_a_norm_g, v_w_ukv, v_q_norm_g, v_k_norm_g, v_hg_lb_logits, v_hg_out_norm_g, v_w_branch, v_w_out, v_ffn_norm_g, v_w_ffn_gate, v_w_ffn_up, v_w_ffn_down, v_ple_gate_norm_g, v_w_ple_gate, v_w_ple_proj, v_ple_post_norm_g):
    w = dict(mix_norm_g=mix_norm_g, w_in=w_in, q_a_norm_g=q_a_norm_g, w_uq=w_uq, kv_a_norm_g=kv_a_norm_g, w_ukv=w_ukv,
             q_norm_g=q_norm_g, k_norm_g=k_norm_g, hg_lb_logits=hg_lb_logits, hg_out_norm_g=hg_out_norm_g,
             w_branch=w_branch, w_out=w_out, ffn_norm_g=ffn_norm_g, w_ffn_gate=w_ffn_gate, w_ffn_up=w_ffn_up,
             w_ffn_down=w_ffn_down, ple_gate_norm_g=ple_gate_norm_g, w_ple_gate=w_ple_gate, w_ple_proj=w_ple_proj,
             ple_post_norm_g=ple_post_norm_g)
    m = dict(mix_norm_g=m_mix_norm_g, w_in=m_w_in, q_a_norm_g=m_q_a_norm_g, w_uq=m_w_uq, kv_a_norm_g=m_kv_a_norm_g,
             w_ukv=m_w_ukv, q_norm_g=m_q_norm_g, k_norm_g=m_k_norm_g, hg_lb_logits=m_hg_lb_logits,
             hg_out_norm_g=m_hg_out_norm_g, w_branch=m_w_branch, w_out=m_w_out, ffn_norm_g=m_ffn_norm_g,
             w_ffn_gate=m_w_ffn_gate, w_ffn_up=m_w_ffn_up, w_ffn_down=m_w_ffn_down,
             ple_gate_norm_g=m_ple_gate_norm_g, w_ple_gate=m_w_ple_gate, w_ple_proj=m_w_ple_proj,
             ple_post_norm_g=m_ple_post_norm_g)
    v = dict(mix_norm_g=v_mix_norm_g, w_in=v_w_in, q_a_norm_g=v_q_a_norm_g, w_uq=v_w_uq, kv_a_norm_g=v_kv_a_norm_g,
             w_ukv=v_w_ukv, q_norm_g=v_q_norm_g, k_norm_g=v_k_norm_g, hg_lb_logits=v_hg_lb_logits,
             hg_out_norm_g=v_hg_out_norm_g, w_branch=v_w_branch, w_out=v_w_out, ffn_norm_g=v_ffn_norm_g,
             w_ffn_gate=v_w_ffn_gate, w_ffn_up=v_w_ffn_up, w_ffn_down=v_w_ffn_down,
             ple_gate_norm_g=v_ple_gate_norm_g, w_ple_gate=v_w_ple_gate, w_ple_proj=v_w_ple_proj,
             ple_post_norm_g=v_ple_post_norm_g)
    return _step(x, p, positions, loss_target, w, m, v)
```

```python
import jax
import jax.numpy as jnp
import numpy as np
from jax import lax
from jax.experimental import pallas as pl
from jax.experimental.pallas import tpu as pltpu

F32 = jnp.float32
MM = jnp.bfloat16
MESH_ID = pl.DeviceIdType.MESH

D_MODEL = 1024
N_DEV = 8
MLA_HEADS = 8
QK_NOPE = 64
QK_ROPE = 32
QK_DIM = 96
V_DIM = 64
HEAD_PAD = 128
Q_RANK = 384
KV_RANK = 256
ROPE_BASE = 10000.0
HG_HEADS = 4
HG_DIM = 128
HG_W = 512
HG_CHUNK = 64
FFN = 2816
PLE = 256
EPS = 1e-6
ATT_SCALE = QK_DIM ** -0.5
NEG = -1e30

ADAM_LR = 0.001
ADAM_B1 = 0.9
ADAM_B2 = 0.999
ADAM_EPS = 1e-08
ADAM_WD = 0.01
ADAM_STEP = 10

COL_SECTIONS = ((0, 384), (384, 256), (640, 32), (672, 512), (1184, 512), (1696, 512), (2208, 512), (2720, 2048))
STORED_WIDTHS = tuple(HEAD_PAD if n == QK_ROPE else n for _, n in COL_SECTIONS)
IN_COLS = 4768
IN_BLOCK = IN_COLS // N_DEV

VMEM_LIMIT = 58 * 1024 * 1024
WIDE_TILE = 512
ROW_TILE = 256
DW_TOKENS = 1024
ATT_TILE = 1024
ATT_HEADS = 4
HG_BLOCK = 512
HG_UNROLL = 4


def _dot(a, b):
    return jnp.dot(a.astype(MM), b.astype(MM), preferred_element_type=F32)


def _dot_nt(a, b):
    return lax.dot_general(a.astype(MM), b.astype(MM), (((1,), (1,)), ((), ())), preferred_element_type=F32)


def _dot_tn(a, b):
    return lax.dot_general(a.astype(MM), b.astype(MM), (((0,), (0,)), ((), ())), preferred_element_type=F32)


def _sigmoid(x):
    return 1.0 / (1.0 + jnp.exp(-x))


def _rms(x, n=None):
    n = x.shape[-1] if n is None else n
    r = lax.rsqrt(jnp.sum(x * x, axis=-1, keepdims=True) * (1.0 / n) + EPS)
    return x * r, r


def _rms_bwd(dxh, xh, r, n=None):
    n = xh.shape[-1] if n is None else n
    return r * (dxh - xh * (jnp.sum(dxh * xh, axis=-1, keepdims=True) * (1.0 / n)))


def _rope_tables(pos, tm):
    lane = lax.broadcasted_iota(jnp.int32, (tm, HEAD_PAD), 1)
    idx = jnp.where(lane < QK_NOPE + QK_ROPE // 2, lane - QK_NOPE, lane - QK_NOPE - QK_ROPE // 2)
    inv = jnp.exp(idx.astype(F32) * (-np.log(ROPE_BASE) * 2.0 / QK_ROPE))
    ang = pos.astype(F32) * inv
    in_rope = (lane >= QK_NOPE) & (lane < QK_DIM)
    first = lane < QK_NOPE + QK_ROPE // 2
    cos_t = jnp.where(in_rope, jnp.cos(ang), 1.0)
    sin_t = jnp.where(in_rope, jnp.where(first, -jnp.sin(ang), jnp.sin(ang)), 0.0)
    return cos_t, sin_t, (first, in_rope)


def _rope_swap(x, halves):
    first, in_rope = halves
    half = QK_ROPE // 2
    return jnp.where(in_rope, jnp.where(first, pltpu.roll(x, HEAD_PAD - half, 1), pltpu.roll(x, half, 1)), 0.0)


def _cparams(sem, vmem=None):
    return pltpu.CompilerParams(dimension_semantics=sem, vmem_limit_bytes=vmem)


def _row_call(name, body, T, tm, row_ins, full_ins, row_outs, acc_outs, vmem=None, scratch=(), xchg=((), ())):
    n_in, n_out, n_x = len(row_ins) + len(full_ins), len(row_outs) + len(acc_outs), _x_count(xchg)
    steps = T // tm

    def kern(*refs):
        ins, x_in, refs = refs[:n_in], refs[n_in:n_in + n_x], refs[n_in + n_x:]
        outs, x_out, refs = refs[:n_out], refs[n_out:n_out + n_x], refs[n_out + n_x:]
        scr, x_sems = refs[:len(scratch)], refs[len(scratch):]
        i = pl.program_id(0)
        if n_x:
            @pl.when(i == 0)
            def _():
                for cp in _x_copies(len(xchg[0]), x_in, x_out, x_sems):
                    cp.start()

        body(i, *ins, *outs, *scr)
        if n_x:
            @pl.when(i == steps - 1)
            def _():
                for cp in _x_copies(len(xchg[0]), x_in, x_out, x_sems):
                    cp.wait()

    any_spec = pl.BlockSpec(memory_space=pl.ANY)
    in_specs = [pl.BlockSpec((tm, a.shape[1]), lambda i: (i, 0)) for a in row_ins]
    in_specs += [pl.BlockSpec(a.shape, lambda i, nd=a.ndim: (0,) * nd, pipeline_mode=pl.Buffered(1)) for a in full_ins]
    out_specs = [pl.BlockSpec((tm, n), lambda i: (i, 0)) for n, _ in row_outs]
    out_specs += [pl.BlockSpec(s, lambda i, nd=len(s): (0,) * nd) for s, _ in acc_outs]
    out_shape = [jax.ShapeDtypeStruct((T, n), dt) for n, dt in row_outs]
    out_shape += [jax.ShapeDtypeStruct(s, dt) for s, dt in acc_outs]
    return pl.pallas_call(
        kern, name=name, grid=(steps,), in_specs=in_specs + [any_spec] * n_x, out_specs=out_specs + [any_spec] * n_x,
        out_shape=out_shape + _x_out_shapes(xchg), scratch_shapes=list(scratch) + _x_sems(xchg),
        compiler_params=_cparams(("arbitrary",), vmem),
    )(*row_ins, *full_ins, *xchg[0], *xchg[1])


FFN_HALVES = (slice(0, FFN // 2), slice(FFN // 2, FFN))
ROW_CHUNK = 16
CHUNK_UNROLL = True


def _by_chunks(tm, fn):
    def step(c, carry):
        fn(pl.ds(pl.multiple_of(c * ROW_CHUNK, ROW_CHUNK), ROW_CHUNK))
        return carry

    lax.fori_loop(0, tm // ROW_CHUNK, step, 0, unroll=CHUNK_UNROLL)


def _fold8(x):
    return x[:8] + x[8:]


def _acc(ref, i, val):
    @pl.when(i == 0)
    def _():
        ref[...] = val

    @pl.when(i != 0)
    def _():
        ref[...] += val


def _in_proj_fwd(x, g_mix, w_in, T, tm):
    def body(i, x_ref, g_ref, w_ref, h_ref, *rest):
        outs, pj_s = rest[:-1], rest[-1]
        g = g_ref[...]

        def norm(rows):
            h_ref[rows, :] = (_rms(x_ref[rows, :])[0] * g).astype(MM)

        _by_chunks(tm, norm)
        for d in range(N_DEV):
            pj_s[d] = _dot_nt(h_ref[...], w_ref[d])

        def join_and_cut(rows):
            proj = jnp.concatenate([pj_s[d, rows, :] for d in range(N_DEV)], axis=1)
            for (s, n), o_ref in zip(COL_SECTIONS, outs):
                if n == QK_ROPE:
                    o_ref[rows, :] = jnp.concatenate(
                        [jnp.zeros((ROW_CHUNK, QK_NOPE), F32), proj[:, s:s + n],
                         jnp.zeros((ROW_CHUNK, HEAD_PAD - QK_DIM), F32)], axis=1)
                else:
                    o_ref[rows, :] = proj[:, s:s + n]

        _by_chunks(tm, join_and_cut)

    row_outs = [(D_MODEL, MM)] + [(n, F32) for n in STORED_WIDTHS]
    return _row_call("in_proj_fwd", body, T, tm, [x], [g_mix, w_in], row_outs, [], VMEM_LIMIT,
                     scratch=[pltpu.VMEM((N_DEV, tm, IN_BLOCK), F32)])


def _mla_heads_fwd(raw, g_pad, cos_t, sin_t, first):
    outs, saved = [], []
    for h in range(MLA_HEADS):
        xh, r = _rms(raw[:, h * HEAD_PAD:(h + 1) * HEAD_PAD], QK_DIM)
        y = xh * g_pad
        outs.append(y * cos_t + _rope_swap(y, first) * sin_t)
        saved.append((xh, r))
    return outs, saved


def _mla_raw_heads(cqn, ckvn, kr, wuq_ref, wukv_ref, tm):
    lane = lax.broadcasted_iota(jnp.int32, (tm, HEAD_PAD), 1)
    nope = lane < QK_NOPE
    one_lane = jnp.where(lane == V_DIM, 1.0, 0.0)
    qs, ks, vs = [], [], []
    for h in range(MLA_HEADS):
        qs.append(_dot_nt(cqn, wuq_ref[h]))
        kv = _dot(ckvn, wukv_ref[h])
        ks.append(jnp.where(nope, kv, kr))
        vs.append(jnp.where(nope, pltpu.roll(kv, V_DIM, 1), one_lane))
    return jnp.concatenate(qs, axis=1), jnp.concatenate(ks, axis=1), jnp.concatenate(vs, axis=1)


def _mla_prep_fwd(cq, ckv, kr, pos, g_qa, g_kva, g_qn, g_kn, w_uq, w_ukv, T, tm):
    def body(i, cq_ref, ckv_ref, kr_ref, pos_ref, gqa_ref, gkva_ref, gqn_ref, gkn_ref, wuq_ref, wukv_ref,
             q_ref, k_ref, v_ref):
        cos_t, sin_t, first = _rope_tables(pos_ref[...], tm)
        cqn = _rms(cq_ref[...])[0] * gqa_ref[...]
        ckvn = _rms(ckv_ref[...])[0] * gkva_ref[...]
        q_raw, k_raw, v = _mla_raw_heads(cqn, ckvn, kr_ref[...], wuq_ref, wukv_ref, tm)
        qs, _ = _mla_heads_fwd(q_raw, gqn_ref[...], cos_t, sin_t, first)
        ks, _ = _mla_heads_fwd(k_raw, gkn_ref[...], cos_t, sin_t, first)
        q_ref[...] = (jnp.concatenate(qs, axis=1) * ATT_SCALE).astype(MM)
        k_ref[...] = jnp.concatenate(ks, axis=1).astype(MM)
        v_ref[...] = v.astype(MM)

    w = MLA_HEADS * HEAD_PAD
    return _row_call("mla_prep_fwd", body, T, tm, [cq, ckv, kr, pos], [g_qa, g_kva, g_qn, g_kn, w_uq, w_ukv],
                     [(w, MM), (w, MM), (w, MM)], [])


def _causal_pairs(n, by_query):
    if by_query:
        pairs = [(q, k) for q in range(n) for k in range(q + 1)]
    else:
        pairs = [(q, k) for k in range(n) for q in range(k, n)]
    return np.array([p[0] for p in pairs], np.int32), np.array([p[1] for p in pairs], np.int32)


def _flash_fwd(qf, kf, vf, T, ag_blocks=()):
    tq = min(ATT_TILE, T)
    nq = T // tq

    qi_tab, ki_tab = _causal_pairs(nq, by_query=True)

    hp = ATT_HEADS

    n_ag = len(ag_blocks)
    n_heads, n_pairs = MLA_HEADS // hp, len(qi_tab)

    def body(qi_ref, ki_ref, q_ref, k_ref, v_ref, *rest):
        ag_in, (o_ref, lse_ref), rest = rest[:n_ag], rest[n_ag:n_ag + 2], rest[n_ag + 2:]
        ag_out, (m_s, acc_s), ag_sems = rest[:n_ag], rest[n_ag:n_ag + 2], rest[n_ag + 2:]
        t = pl.program_id(1)
        qi, ki = qi_ref[t], ki_ref[t]
        if n_ag:
            @pl.when((pl.program_id(0) == 0) & (t == 0))
            def _():
                _ag_start(ag_in, ag_out, ag_sems)

        @pl.when(ki == 0)
        def _():
            m_s[...] = jnp.full_like(m_s, NEG)
            acc_s[...] = jnp.zeros_like(acc_s)

        def step(masked):
            halves = 2 if masked and tq % (2 * HEAD_PAD) == 0 else 1
            w = tq // halves
            for hh in range(hp):
                hs = slice(hh * HEAD_PAD, (hh + 1) * HEAD_PAD)
                for part in range(halves):
                    cols, nk = slice(part * w, (part + 1) * w), (part + 1) * w
                    s_t = _dot_nt(k_ref[:nk, hs], q_ref[cols, hs])
                    if masked:
                        key = lax.broadcasted_iota(jnp.int32, (nk, w), 0)
                        qry = lax.broadcasted_iota(jnp.int32, (nk, w), 1) + part * w
                        s_t = jnp.where(key <= qry, s_t, NEG)
                    m_old = m_s[hh, :, cols]
                    m_new = jnp.maximum(m_old, jnp.max(s_t, axis=0, keepdims=True))
                    p_t = jnp.exp(s_t - m_new)
                    acc_s[hh, :, cols] = jnp.exp(m_old - m_new) * acc_s[hh, :, cols] + _dot_tn(v_ref[:nk, hs], p_t)
                    m_s[hh, :, cols] = m_new

        @pl.when(ki < qi)
        def _():
            step(False)

        @pl.when(ki == qi)
        def _():
            step(True)
            real = lax.broadcasted_iota(jnp.int32, (HEAD_PAD, tq), 0) < V_DIM
            for hh in range(hp):
                hs = slice(hh * HEAD_PAD, (hh + 1) * HEAD_PAD)
                acc = acc_s[hh]
                l = acc[V_DIM:V_DIM + 1]
                o_ref[:, hs] = jnp.where(real, acc / l, 0.0).T
                lse_ref[:, hs] = jnp.broadcast_to(m_s[hh] + jnp.log(l), (HEAD_PAD, tq)).T

        if n_ag:
            @pl.when((pl.program_id(0) == n_heads - 1) & (t == n_pairs - 1))
            def _():
                _ag_finish(ag_in, ag_out, ag_sems)

    q_spec = pl.BlockSpec((tq, hp * HEAD_PAD), lambda h, t, qi_ref, ki_ref: (qi_ref[t], h))
    kv_spec = pl.BlockSpec((tq, hp * HEAD_PAD), lambda h, t, qi_ref, ki_ref: (ki_ref[t], h))
    any_spec = pl.BlockSpec(memory_space=pl.ANY)
    grid_spec = pltpu.PrefetchScalarGridSpec(
        num_scalar_prefetch=2, grid=(n_heads, n_pairs),
        in_specs=[q_spec, kv_spec, kv_spec] + [any_spec] * n_ag, out_specs=[q_spec, q_spec] + [any_spec] * n_ag,
        scratch_shapes=[pltpu.VMEM((hp, 1, tq), F32), pltpu.VMEM((hp, HEAD_PAD, tq), F32)]
        + (_ag_sems(n_ag) if n_ag else []))
    return pl.pallas_call(
        body, name="flash_fwd", grid_spec=grid_spec,
        out_shape=[jax.ShapeDtypeStruct((T, MLA_HEADS * HEAD_PAD), F32)] * 2 + _ag_out_shapes(ag_blocks),
        compiler_params=_cparams(("arbitrary", "arbitrary")),
    )(jnp.asarray(qi_tab), jnp.asarray(ki_tab), qf, kf, vf, *ag_blocks)


def _hg_gates(hf, lb):
    sg = _sigmoid(hf)
    f = lb + (1.0 - lb) * sg
    return sg, f, jnp.log(f), 1.0 - f


def _prefix_sum(x, reverse=False):
    n = x.shape[0]
    row = lax.broadcasted_iota(jnp.int32, x.shape, 0)
    step = 1
    while step < n:
        if reverse:
            x = x + jnp.where(row < n - step, pltpu.roll(x, n - step, 0), 0.0)
        else:
            x = x + jnp.where(row >= step, pltpu.roll(x, step, 0), 0.0)
        step *= 2
    return x


def _hg_levels():
    C = HG_CHUNK
    t = lax.broadcasted_iota(jnp.int32, (C, C), 0)
    s = lax.broadcasted_iota(jnp.int32, (C, C), 1)
    levels = []
    for shift in range(C.bit_length() - 2, -1, -1):
        pair_t, pair_s = lax.shift_right_logical(t, shift + 1), lax.shift_right_logical(s, shift + 1)
        later_t = (lax.shift_right_logical(t, shift) & 1) == 1
        earlier_s = (lax.shift_right_logical(s, shift) & 1) == 0
        levels.append((1 << shift, (pair_t == pair_s) & later_t & earlier_s))
    return levels, t == s


def _hg_refs(b):
    C, n = b.shape
    row = lax.broadcasted_iota(jnp.int32, (C, n), 0)
    back1, back2, ahead1 = pltpu.roll(b, 1, 0), pltpu.roll(b, 2, 0), pltpu.roll(b, C - 1, 0)
    refs = []
    for half in (32, 16, 8, 4):
        refs.append(jnp.concatenate(
            [jnp.broadcast_to(b[lo + half - 1:lo + half], (2 * half, n)) for lo in range(0, C, 2 * half)], axis=0))
    in4 = row & 3
    refs.append(jnp.where(in4 == 0, ahead1, jnp.where(in4 == 1, b, jnp.where(in4 == 2, back1, back2))))
    refs.append(jnp.where((row & 1) == 1, back1, b))
    return refs


def _hg_intra(q, k, b, refs, levels, eye):
    a = jnp.where(eye, jnp.sum(q * k, axis=1, keepdims=True), 0.0)
    saved = []
    for r, (_, mask) in zip(refs, levels):
        e = jnp.exp(-jnp.abs(b - r))
        q_t, k_t = q * e, k * e
        a = a + jnp.where(mask, _dot_nt(q_t, k_t), 0.0)
        saved.append((q_t, k_t, e))
    return a, saved


def _hg_intra_bwd(d_a, q, k, saved, levels, eye):
    diag = jnp.sum(jnp.where(eye, d_a, 0.0), axis=1, keepdims=True)
    dq, dk = diag * k, diag * q
    for (q_t, k_t, e), (_, mask) in zip(saved, levels):
        da = jnp.where(mask, d_a, 0.0)
        dq = dq + _dot(da, k_t) * e
        dk = dk + _dot_tn(da, q_t) * e
    return dq, dk


def _hgrn_fwd(hq, hf, hi, lb, T):
    rb = min(HG_BLOCK, T)
    ncb = rb // HG_CHUNK

    def body(hq_ref, hf_ref, hi_ref, lb_ref, o_ref, s0_ref, st_ref):
        @pl.when(pl.program_id(0) == 0)
        def _():
            st_ref[...] = jnp.zeros_like(st_ref)

        levels, eye = _hg_levels()

        def chunk(c, carry):
            rows = pl.ds(pl.multiple_of(c * HG_CHUNK, HG_CHUNK), HG_CHUNK)
            _, _, logf, kk = _hg_gates(hf_ref[rows, :], lb_ref[...])
            b = _prefix_sum(logf)
            refs = _hg_refs(b)
            q_all, v_all = hq_ref[rows, :], hi_ref[rows, :]
            outs = []
            for h in range(HG_HEADS):
                ls = slice(h * HG_DIM, (h + 1) * HG_DIM)
                q, k, v, bh = q_all[:, ls], kk[:, ls], v_all[:, ls], b[:, ls]
                st = st_ref[h]
                s0_ref[c, h * HG_DIM:(h + 1) * HG_DIM, :] = st
                b_end = bh[HG_CHUNK - 1:HG_CHUNK]
                a, _ = _hg_intra(q, k, bh, [r[:, ls] for r in refs], levels, eye)
                outs.append(_dot_nt(q * jnp.exp(bh), st) + _dot(a, v))
                st_ref[h] = st * jnp.exp(b_end) + _dot_tn(v, k * jnp.exp(b_end - bh))
            o_ref[rows, :] = jnp.concatenate(outs, axis=1)
            return carry

        lax.fori_loop(0, ncb, chunk, 0, unroll=HG_UNROLL)

    row = pl.BlockSpec((rb, HG_W), lambda i: (i, 0))
    return pl.pallas_call(
        body, name="hgrn_fwd", grid=(T // rb,),
        in_specs=[row, row, row, pl.BlockSpec((1, HG_W), lambda i: (0, 0))],
        out_specs=[row, pl.BlockSpec((ncb, HG_W, HG_DIM), lambda i: (i, 0, 0))],
        out_shape=[jax.ShapeDtypeStruct((T, HG_W), F32), jax.ShapeDtypeStruct((T // HG_CHUNK, HG_W, HG_DIM), F32)],
        scratch_shapes=[pltpu.VMEM((HG_HEADS, HG_DIM, HG_DIM), F32)],
        compiler_params=_cparams(("arbitrary",)),
    )(hq, hf, hi, lb)


def _hgrn_bwd(hq, hf, hi, do, s0, lb, T, xchg=((), ())):
    rb = min(HG_BLOCK, T)
    ncb = rb // HG_CHUNK
    nb = T // rb
    C = HG_CHUNK
    n_x, n_sib = _x_count(xchg), len(xchg[0])

    def body(hq_ref, hf_ref, hi_ref, do_ref, s0_ref, lb_ref, *rest):
        x_in, (dq_ref, df_ref, dv_ref, dlb_ref), rest = rest[:n_x], rest[n_x:n_x + 4], rest[n_x + 4:]
        x_out, dst_ref, x_sems = rest[:n_x], rest[n_x], rest[n_x + 1:]

        @pl.when(pl.program_id(0) == 0)
        def _():
            dst_ref[...] = jnp.zeros_like(dst_ref)
            dlb_ref[...] = jnp.zeros_like(dlb_ref)
            for cp in _x_copies(n_sib, x_in, x_out, x_sems):
                cp.start()

        row_cc = lax.broadcasted_iota(jnp.int32, (C, C), 0)
        col_cc = lax.broadcasted_iota(jnp.int32, (C, C), 1)
        last_row = lax.broadcasted_iota(jnp.int32, (C, HG_DIM), 0) == C - 1
        lb_v = lb_ref[...]
        levels, eye = _hg_levels()

        def chunk(cc, carry):
            c = ncb - 1 - cc
            rows = pl.ds(pl.multiple_of(c * C, C), C)
            hf_c = hf_ref[rows, :]
            sg, f, logf, kk = _hg_gates(hf_c, lb_v)
            b = _prefix_sum(logf)
            refs = _hg_refs(b)
            q_all, v_all, do_all = hq_ref[rows, :], hi_ref[rows, :], do_ref[rows, :]
            dq_o, dk_o, dv_o, db_o = [], [], [], []
            for h in range(HG_HEADS):
                ls = slice(h * HG_DIM, (h + 1) * HG_DIM)
                q, k, v, bh, d_o = q_all[:, ls], kk[:, ls], v_all[:, ls], b[:, ls], do_all[:, ls]
                st0 = s0_ref[c, h * HG_DIM:(h + 1) * HG_DIM, :]
                dst = dst_ref[h]
                b_end = bh[C - 1:C]
                e_b, e_end = jnp.exp(bh), jnp.exp(b_end)
                e_rem = jnp.exp(b_end - bh)
                qe, kd = q * e_b, k * e_rem
                st_end = st0 * e_end + _dot_tn(v, kd)
                a, saved = _hg_intra(q, k, bh, [r[:, ls] for r in refs], levels, eye)
                d_a = jnp.where(col_cc <= row_cc, _dot_nt(d_o, v), 0.0)
                dq_i, dk_i = _hg_intra_bwd(d_a, q, k, saved, levels, eye)
                dv = _dot_tn(a, d_o) + _dot_nt(kd, dst)
                dq = e_b * _dot(d_o, st0) + dq_i
                dk = e_rem * _dot(v, dst) + dk_i
                extra = jnp.sum(dst * st_end, axis=0, keepdims=True)
                db_o.append(q * dq - k * dk + jnp.where(last_row, extra, 0.0))
                dst_ref[h] = dst * e_end + _dot_tn(d_o, qe)
                dq_o.append(dq)
                dk_o.append(dk)
                dv_o.append(dv)
            dlogf = _prefix_sum(jnp.concatenate(db_o, axis=1), reverse=True)
            d_f = dlogf / f - jnp.concatenate(dk_o, axis=1)
            dq_ref[rows, :] = jnp.concatenate(dq_o, axis=1).astype(MM)
            dv_ref[rows, :] = jnp.concatenate(dv_o, axis=1).astype(MM)
            df_ref[rows, :] = (d_f * (1.0 - lb_v) * sg * (1.0 - sg)).astype(MM)
            dlb_ref[...] += jnp.sum(d_f * (1.0 - sg), axis=0, keepdims=True)
            return carry

        lax.fori_loop(0, ncb, chunk, 0, unroll=HG_UNROLL)

        if n_x:
            @pl.when(pl.program_id(0) == nb - 1)
            def _():
                for cp in _x_copies(n_sib, x_in, x_out, x_sems):
                    cp.wait()

    row = pl.BlockSpec((rb, HG_W), lambda i: (nb - 1 - i, 0))
    one = pl.BlockSpec((1, HG_W), lambda i: (0, 0))
    any_spec = pl.BlockSpec(memory_space=pl.ANY)
    return pl.pallas_call(
        body, name="hgrn_bwd", grid=(nb,),
        in_specs=[row, row, row, row, pl.BlockSpec((ncb, HG_W, HG_DIM), lambda i: (nb - 1 - i, 0, 0)), one]
        + [any_spec] * n_x,
        out_specs=[row, row, row, one] + [any_spec] * n_x,
        out_shape=[jax.ShapeDtypeStruct((T, HG_W), MM)] * 3 + [jax.ShapeDtypeStruct((1, HG_W), F32)]
        + _x_out_shapes(xchg),
        scratch_shapes=[pltpu.VMEM((HG_HEADS, HG_DIM, HG_DIM), F32)] + _x_sems(xchg),
        compiler_params=_cparams(("arbitrary",)),
    )(hq, hf, hi, do, s0, lb, *xchg[0], *xchg[1])


def _silu_parts(x):
    sg = _sigmoid(x)
    return x * sg, sg * (1.0 + x * (1.0 - sg))


def _merge_fwd(attn, o, hg, bg, x, g_out, w_bra, w_brb, w_out, T, tm):
    def body(i, attn_ref, o_ref, hg_ref, bg_ref, x_ref, g_ref, wa_ref, wb_ref, wo_ref,
             x1_ref, ya_ref, yb_ref, m_ref, rec_ref):
        g = g_ref[...]

        def recurrent_out(rows):
            for h in range(HG_HEADS):
                ls = slice(h * HG_DIM, (h + 1) * HG_DIM)
                rec_ref[rows, ls] = (_rms(o_ref[rows, ls])[0] * g * _silu_parts(hg_ref[rows, ls])[0]).astype(MM)

        _by_chunks(tm, recurrent_out)
        ya_ref[...] = _dot(attn_ref[...], wa_ref[...])
        yb_ref[...] = jnp.dot(rec_ref[...], wb_ref[...], preferred_element_type=F32)

        def gate(rows):
            m_ref[rows, :] = (_sigmoid(bg_ref[rows, :D_MODEL]) * ya_ref[rows, :]
                              + _sigmoid(bg_ref[rows, D_MODEL:]) * yb_ref[rows, :]).astype(MM)

        _by_chunks(tm, gate)
        x1_ref[...] = x_ref[...] + jnp.dot(m_ref[...], wo_ref[...], preferred_element_type=F32)

    return _row_call("merge_fwd", body, T, tm, [attn, o, hg, bg, x], [g_out, w_bra, w_brb, w_out],
                     [(D_MODEL, F32), (D_MODEL, F32), (D_MODEL, F32), (D_MODEL, MM), (HG_W, MM)], [], VMEM_LIMIT)


def _ffn_fwd(x1, g_ffn, w_g, w_u, w_d, T, tm):
    def body(i, x1_ref, g_ref, wg_ref, wu_ref, wd_ref, x2_ref, gt_ref, up_ref, h2_ref, a_s):
        g = g_ref[...]

        def norm(rows):
            h2_ref[rows, :] = (_rms(x1_ref[rows, :])[0] * g).astype(MM)

        _by_chunks(tm, norm)
        gt_ref[...] = _dot_nt(h2_ref[...], wg_ref[...])
        up_ref[...] = _dot_nt(h2_ref[...], wu_ref[...])

        def act(rows):
            for cs in FFN_HALVES:
                a_s[rows, cs] = (_silu_parts(gt_ref[rows, cs])[0] * up_ref[rows, cs]).astype(MM)

        _by_chunks(tm, act)
        x2_ref[...] = x1_ref[...] + jnp.dot(a_s[...], wd_ref[...], preferred_element_type=F32)

    return _row_call("ffn_fwd", body, T, tm, [x1], [g_ffn, w_g, w_u, w_d],
                     [(D_MODEL, F32), (FFN, F32), (FFN, F32), (D_MODEL, MM)], [], VMEM_LIMIT,
                     scratch=[pltpu.VMEM((tm, FFN), MM)])


def _ple_loss(x2, p, tgt, g_pg, g_post, w_pg, w_pp, T, tm):
    def body(i, x2_ref, p_ref, t_ref, gpg_ref, gpo_ref, wpg_ref, wpp_ref,
             dx2_ref, loss_ref, dgpo_ref, dgpg_ref, dwpg_ref, dwpp_ref, u_s, n3_s, z_s, dz_s, du_s, dy_s, dn3_s):
        @pl.when(i == 0)
        def _():
            for ref in (loss_ref, dgpo_ref, dgpg_ref, dwpg_ref, dwpp_ref):
                ref[...] = jnp.zeros_like(ref)

        gpg, gpo = gpg_ref[...], gpo_ref[...]
        p_mm = p_ref[...].astype(MM)
        for d in range(N_DEV):
            u_s[:, d * HEAD_PAD:(d + 1) * HEAD_PAD] = jnp.dot(p_mm, wpp_ref[d], preferred_element_type=F32)

        def gate_input(rows):
            n3_s[rows, :] = (_rms(x2_ref[rows, :])[0] * gpg).astype(MM)

        _by_chunks(tm, gate_input)
        z_s[...] = jnp.dot(n3_s[...], wpg_ref[...], preferred_element_type=F32)

        def loss_and_back(rows):
            uh, ru = _rms(u_s[rows, :])
            e = uh * gpo
            gate = _sigmoid(z_s[rows, :])
            diff = x2_ref[rows, :] + gate * e - t_ref[rows, :]
            dy = diff * (1.0 / D_MODEL)
            de = dy * gate
            dz_s[rows, :] = (dy * e * gate * (1.0 - gate)).astype(MM)
            du_s[rows, :] = _rms_bwd(de * gpo, uh, ru).astype(MM)
            dy_s[rows, :] = dy
            loss_ref[...] += _fold8(diff * diff) * (0.5 / D_MODEL)
            dgpo_ref[...] += _fold8(de * uh)

        _by_chunks(tm, loss_and_back)
        dn3_s[...] = _dot_nt(dz_s[...], wpg_ref[...])

        def gate_norm_back(rows):
            x2h, r3 = _rms(x2_ref[rows, :])
            dn3 = dn3_s[rows, :]
            dx2_ref[rows, :] = dy_s[rows, :] + _rms_bwd(dn3 * gpg, x2h, r3)
            dgpg_ref[...] += _fold8(dn3 * x2h)

        _by_chunks(tm, gate_norm_back)
        dwpg_ref[...] += _dot_tn(n3_s[...], dz_s[...])
        for d in range(N_DEV):
            dwpp_ref[d] += _dot_tn(p_mm, du_s[:, d * HEAD_PAD:(d + 1) * HEAD_PAD])

    vec = ((8, D_MODEL), F32)
    wide = lambda dt: pltpu.VMEM((tm, D_MODEL), dt)
    return _row_call("ple_loss", body, T, tm, [x2, p, tgt], [g_pg, g_post, w_pg, w_pp], [(D_MODEL, F32)],
                     [vec, vec, vec, ((D_MODEL, D_MODEL), F32), ((N_DEV, PLE, HEAD_PAD), F32)], VMEM_LIMIT,
                     scratch=[wide(F32), wide(MM), wide(F32), wide(MM), wide(MM), wide(F32), wide(F32)])


def _ffn_bwd(dx2, x1, gt, up, g_ffn, w_g, w_u, w_d, T, tm):
    def body(i, dx2_ref, x1_ref, gt_ref, up_ref, g_ref, wg_ref, wu_ref, wd_ref,
             dx1_ref, a_ref, dgt_ref, dup_ref, dg_ref, da_s, dh2_s):
        @pl.when(i == 0)
        def _():
            dg_ref[...] = jnp.zeros_like(dg_ref)

        g = g_ref[...]
        da_s[...] = _dot_nt(dx2_ref[...], wd_ref[...])

        def act_back(rows):
            for cs in FFN_HALVES:
                up, da = up_ref[rows, cs], da_s[rows, cs]
                silu, dsilu = _silu_parts(gt_ref[rows, cs])
                dgt_ref[rows, cs] = (da * up * dsilu).astype(MM)
                dup_ref[rows, cs] = (da * silu).astype(MM)
                a_ref[rows, cs] = (silu * up).astype(MM)

        _by_chunks(tm, act_back)
        dh2_s[...] = (jnp.dot(dgt_ref[...], wg_ref[...], preferred_element_type=F32)
                      + jnp.dot(dup_ref[...], wu_ref[...], preferred_element_type=F32))

        def norm_back(rows):
            x1h, r = _rms(x1_ref[rows, :])
            dh2 = dh2_s[rows, :]
            dx1_ref[rows, :] = dx2_ref[rows, :] + _rms_bwd(dh2 * g, x1h, r)
            dg_ref[...] += _fold8(dh2 * x1h)

        _by_chunks(tm, norm_back)

    return _row_call("ffn_bwd", body, T, tm, [dx2, x1, gt, up], [g_ffn, w_g, w_u, w_d],
                     [(D_MODEL, F32), (FFN, MM), (FFN, MM), (FFN, MM)], [((8, D_MODEL), F32)], VMEM_LIMIT,
                     scratch=[pltpu.VMEM((tm, FFN), F32), pltpu.VMEM((tm, D_MODEL), F32)])


def _merge_bwd(dx1, ya, yb, bg, o, hg, attn, m, rec, g_out, w_bra, w_brb, w_out, T, tm, xchg=((), ())):
    def body(i, dx1_ref, ya_ref, yb_ref, bg_ref, o_ref, hg_ref, attn_ref, m_ref, rec_ref, g_ref, wa_ref, wb_ref, wo_ref,
             dattn_ref, do_ref, dhg_ref, dbg_ref, dg_ref, dwo_ref, dwa_ref, dwb_ref, dm_s, dya_s, dyb_s, drec_s):
        @pl.when(i == 0)
        def _():
            for ref in (dg_ref, dwo_ref, dwa_ref, dwb_ref):
                ref[...] = jnp.zeros_like(ref)

        g = g_ref[...]
        dx1 = dx1_ref[...].astype(MM)
        dm_s[...] = _dot_nt(dx1, wo_ref[...])

        def gate_back(rows):
            dm = dm_s[rows, :]
            ga, gb = _sigmoid(bg_ref[rows, :D_MODEL]), _sigmoid(bg_ref[rows, D_MODEL:])
            dya_s[rows, :] = (dm * ga).astype(MM)
            dyb_s[rows, :] = (dm * gb).astype(MM)
            dbg_ref[rows, :D_MODEL] = (dm * ya_ref[rows, :] * ga * (1.0 - ga)).astype(MM)
            dbg_ref[rows, D_MODEL:] = (dm * yb_ref[rows, :] * gb * (1.0 - gb)).astype(MM)

        _by_chunks(tm, gate_back)
        dwo_ref[...] += _dot_tn(m_ref[...], dx1)
        attn_mm = attn_ref[...].astype(MM)
        for d in range(N_DEV):
            ds = slice(d * HEAD_PAD, (d + 1) * HEAD_PAD)
            dwa_ref[d] += _dot_tn(attn_mm, dya_s[:, ds])
            dwb_ref[d] += _dot_tn(rec_ref[...], dyb_s[:, ds])
        dattn_ref[...] = _dot_nt(dya_s[...], wa_ref[...])
        drec_s[...] = _dot_nt(dyb_s[...], wb_ref[...])

        def recurrent_out_back(rows):
            for h in range(HG_HEADS):
                ls = slice(h * HG_DIM, (h + 1) * HG_DIM)
                oh, r = _rms(o_ref[rows, ls])
                silu, dsilu = _silu_parts(hg_ref[rows, ls])
                dr = drec_s[rows, ls]
                dhg_ref[rows, ls] = (dr * oh * g * dsilu).astype(MM)
                don = dr * silu
                dg_ref[...] += _fold8(don * oh)
                do_ref[rows, ls] = _rms_bwd(don * g, oh, r)

        _by_chunks(tm, recurrent_out_back)

    wide = lambda n, dt: pltpu.VMEM((tm, n), dt)
    return _row_call("merge_bwd", body, T, tm, [dx1, ya, yb, bg, o, hg, attn, m, rec], [g_out, w_bra, w_brb, w_out],
                     [(D_MODEL, F32), (HG_W, F32), (HG_W, MM), (2 * D_MODEL, MM)],
                     [((8, HG_DIM), F32), ((D_MODEL, D_MODEL), F32), ((N_DEV, MLA_HEADS * HEAD_PAD, HEAD_PAD), F32),
                      ((N_DEV, HG_W, HEAD_PAD), F32)], VMEM_LIMIT,
                     scratch=[wide(D_MODEL, F32), wide(D_MODEL, MM), wide(D_MODEL, MM), wide(HG_W, F32)], xchg=xchg)


def _flash_bwd(qf, kf, vf, o, do, lse, T, xchg=((), ())):
    tq = min(ATT_TILE, T)
    nq = T // tq

    qi_tab, ki_tab = _causal_pairs(nq, by_query=False)

    n_x, n_sib = _x_count(xchg), len(xchg[0])
    hp = ATT_HEADS
    n_heads, n_pairs = MLA_HEADS // hp, len(qi_tab)

    def body(qi_ref, ki_ref, q_ref, k_ref, v_ref, o_ref, do_ref, lse_ref, *rest):
        x_in, (dq_ref, dk_ref, dv_ref), rest = rest[:n_x], rest[n_x:n_x + 3], rest[n_x + 3:]
        x_out, x_sems = rest[:n_x], rest[n_x:]
        t = pl.program_id(1)
        qi, ki = qi_ref[t], ki_ref[t]
        if n_x:
            @pl.when((pl.program_id(0) == 0) & (t == 0))
            def _():
                for cp in _x_copies(n_sib, x_in, x_out, x_sems):
                    cp.start()

        @pl.when(t == 0)
        def _():
            dq_ref[...] = jnp.zeros_like(dq_ref)

        def step(first):
            halves = 2 if first and tq % (2 * HEAD_PAD) == 0 else 1
            w = tq // halves
            for hh in range(hp):
                hs = slice(hh * HEAD_PAD, (hh + 1) * HEAD_PAD)
                for part in range(halves):
                    keys, qs = slice(part * w, (part + 1) * w), slice(part * w, tq)
                    nq_ = tq - part * w
                    q, k, d_o = q_ref[qs, hs], k_ref[keys, hs], do_ref[qs, hs]
                    s = _dot_nt(q, k)
                    if first:
                        row = lax.broadcasted_iota(jnp.int32, (nq_, w), 0)
                        col = lax.broadcasted_iota(jnp.int32, (nq_, w), 1)
                        s = jnp.where(col <= row, s, NEG)
                    p = jnp.exp(s - lse_ref[qs, hh * HEAD_PAD:hh * HEAD_PAD + 1])
                    delta = jnp.sum(d_o * o_ref[qs, hs], axis=1, keepdims=True)
                    ds = p * (_dot_nt(d_o, v_ref[keys, hs]) - delta)
                    rows = pl.ds(pl.multiple_of(qi * tq + part * w, w), nq_)
                    dq_ref[rows, hs] += _dot(ds, k)
                    if first:
                        dv_ref[keys, hs] = _dot_tn(p, d_o)
                        dk_ref[keys, hs] = _dot_tn(ds, q)
                    else:
                        dv_ref[keys, hs] += _dot_tn(p, d_o)
                        dk_ref[keys, hs] += _dot_tn(ds, q)

        @pl.when(qi == ki)
        def _():
            step(True)

        @pl.when(qi > ki)
        def _():
            step(False)

        if n_x:
            @pl.when((pl.program_id(0) == n_heads - 1) & (t == n_pairs - 1))
            def _():
                for cp in _x_copies(n_sib, x_in, x_out, x_sems):
                    cp.wait()

    q_spec = pl.BlockSpec((tq, hp * HEAD_PAD), lambda h, t, qi_ref, ki_ref: (qi_ref[t], h))
    kv_spec = pl.BlockSpec((tq, hp * HEAD_PAD), lambda h, t, qi_ref, ki_ref: (ki_ref[t], h))
    any_spec = pl.BlockSpec(memory_space=pl.ANY)
    w = MLA_HEADS * HEAD_PAD
    grid_spec = pltpu.PrefetchScalarGridSpec(
        num_scalar_prefetch=2, grid=(n_heads, n_pairs),
        in_specs=[q_spec, kv_spec, kv_spec, q_spec, q_spec, q_spec] + [any_spec] * n_x,
        out_specs=[pl.BlockSpec((T, hp * HEAD_PAD), lambda h, t, qi_ref, ki_ref: (0, h)), kv_spec, kv_spec]
        + [any_spec] * n_x,
        scratch_shapes=_x_sems(xchg))
    return pl.pallas_call(
        body, name="flash_bwd", grid_spec=grid_spec,
        out_shape=[jax.ShapeDtypeStruct((T, w), F32)] * 3 + _x_out_shapes(xchg),
        compiler_params=_cparams(("arbitrary", "arbitrary")),
    )(jnp.asarray(qi_tab), jnp.asarray(ki_tab), qf, kf, vf, o, do, lse, *xchg[0], *xchg[1])


def _mla_heads_bwd(d_out, saved, g_pad, cos_t, sin_t, first):
    d_raw, dg = [], jnp.zeros((1, HEAD_PAD), F32)
    for h in range(MLA_HEADS):
        xh, r = saved[h]
        dy = d_out[:, h * HEAD_PAD:(h + 1) * HEAD_PAD]
        dn = dy * cos_t + _rope_swap(dy * sin_t, first)
        dg = dg + jnp.sum(dn * xh, axis=0, keepdims=True)
        d_raw.append(_rms_bwd(dn * g_pad, xh, r, QK_DIM))
    return d_raw, dg


def _mla_prep_bwd(cq, ckv, kr, pos, dqf, dkf, dvf, g_qa, g_kva, g_qn, g_kn, w_uq, w_ukv, T, tm):
    def body(i, cq_ref, ckv_ref, kr_ref, pos_ref, dq_ref, dk_ref, dv_ref,
             gqa_ref, gkva_ref, gqn_ref, gkn_ref, wuq_ref, wukv_ref,
             dcq_ref, dckv_ref, dkr_ref, dgqa_ref, dgkva_ref, dgqn_ref, dgkn_ref, dwuq_ref, dwukv_ref):
        cos_t, sin_t, first = _rope_tables(pos_ref[...], tm)
        cqh, rq = _rms(cq_ref[...])
        ckvh, rkv = _rms(ckv_ref[...])
        cqn, ckvn = cqh * gqa_ref[...], ckvh * gkva_ref[...]
        q_raw, k_raw, _ = _mla_raw_heads(cqn, ckvn, kr_ref[...], wuq_ref, wukv_ref, tm)
        _, q_saved = _mla_heads_fwd(q_raw, gqn_ref[...], cos_t, sin_t, first)
        _, k_saved = _mla_heads_fwd(k_raw, gkn_ref[...], cos_t, sin_t, first)
        dq_heads, dgqn = _mla_heads_bwd(dq_ref[...] * ATT_SCALE, q_saved, gqn_ref[...], cos_t, sin_t, first)
        dk_heads, dgkn = _mla_heads_bwd(dk_ref[...], k_saved, gkn_ref[...], cos_t, sin_t, first)
        lane = lax.broadcasted_iota(jnp.int32, (tm, HEAD_PAD), 1)
        nope = lane < QK_NOPE
        dcqn = jnp.zeros((tm, Q_RANK), F32)
        dckvn = jnp.zeros((tm, KV_RANK), F32)
        dkr = jnp.zeros((tm, HEAD_PAD), F32)
        cqn_mm, ckvn_mm = cqn.astype(MM), ckvn.astype(MM)
        for h in range(MLA_HEADS):
            hs = slice(h * HEAD_PAD, (h + 1) * HEAD_PAD)
            dq_h = dq_heads[h].astype(MM)
            dkv_h = jnp.where(nope, dk_heads[h], pltpu.roll(dv_ref[:, hs], V_DIM, 1)).astype(MM)
            _acc(dwuq_ref.at[h], i, _dot_tn(dq_h, cqn_mm))
            _acc(dwukv_ref.at[h], i, _dot_tn(ckvn_mm, dkv_h))
            dcqn = dcqn + jnp.dot(dq_h, wuq_ref[h], preferred_element_type=F32)
            dckvn = dckvn + lax.dot_general(dkv_h, wukv_ref[h], (((1,), (1,)), ((), ())), preferred_element_type=F32)
            dkr = dkr + dk_heads[h]
        dkr_ref[...] = jnp.where((lane >= QK_NOPE) & (lane < QK_DIM), dkr, 0.0).astype(MM)
        dcq_ref[...] = _rms_bwd(dcqn * gqa_ref[...], cqh, rq).astype(MM)
        dckv_ref[...] = _rms_bwd(dckvn * gkva_ref[...], ckvh, rkv).astype(MM)
        _acc(dgqa_ref, i, jnp.sum(dcqn * cqh, axis=0, keepdims=True))
        _acc(dgkva_ref, i, jnp.sum(dckvn * ckvh, axis=0, keepdims=True))
        _acc(dgqn_ref, i, dgqn)
        _acc(dgkn_ref, i, dgkn)

    return _row_call(
        "mla_prep_bwd", body, T, tm, [cq, ckv, kr, pos, dqf, dkf, dvf], [g_qa, g_kva, g_qn, g_kn, w_uq, w_ukv],
        [(Q_RANK, MM), (KV_RANK, MM), (HEAD_PAD, MM)],
        [((1, Q_RANK), F32), ((1, KV_RANK), F32), ((1, HEAD_PAD), F32), ((1, HEAD_PAD), F32),
         ((MLA_HEADS, HEAD_PAD, Q_RANK), F32), ((MLA_HEADS, KV_RANK, HEAD_PAD), F32)], VMEM_LIMIT)


def _in_proj_bwd(x, dx1, dsecs, g_mix, w_in, T, tm):
    def body(i, x_ref, dx1_ref, *rest):
        d_refs, (g_ref, w_ref, dx_ref, dp_ref, dg_ref, dh_s) = rest[:len(COL_SECTIONS)], rest[len(COL_SECTIONS):]

        @pl.when(i == 0)
        def _():
            dg_ref[...] = jnp.zeros_like(dg_ref)

        g = g_ref[...]

        def join_and_cut(rows):
            pieces = [(d_ref[rows, QK_NOPE:QK_DIM] if n == QK_ROPE else d_ref[rows, :]).astype(F32)
                      for (_, n), d_ref in zip(COL_SECTIONS, d_refs)]
            dproj = jnp.concatenate(pieces, axis=1)
            for d in range(N_DEV):
                dp_ref[d, rows, :] = dproj[:, d * IN_BLOCK:(d + 1) * IN_BLOCK].astype(MM)

        _by_chunks(tm, join_and_cut)
        dh = jnp.dot(dp_ref[0], w_ref[0], preferred_element_type=F32)
        for d in range(1, N_DEV):
            dh = dh + jnp.dot(dp_ref[d], w_ref[d], preferred_element_type=F32)
        dh_s[...] = dh

        def norm_back(rows):
            xh, r = _rms(x_ref[rows, :])
            dh_c = dh_s[rows, :]
            dx_ref[rows, :] = dx1_ref[rows, :] + _rms_bwd(dh_c * g, xh, r)
            dg_ref[...] += _fold8(dh_c * xh)

        _by_chunks(tm, norm_back)

    in_specs = [pl.BlockSpec((tm, a.shape[1]), lambda i: (i, 0)) for a in [x, dx1, *dsecs]]
    in_specs += [pl.BlockSpec(g_mix.shape, lambda i: (0, 0)),
                 pl.BlockSpec(w_in.shape, lambda i: (0, 0, 0), pipeline_mode=pl.Buffered(1))]

    def kern(*refs):
        body(pl.program_id(0), *refs)

    return pl.pallas_call(
        kern, name="in_proj_bwd", grid=(T // tm,), in_specs=in_specs,
        out_specs=[pl.BlockSpec((tm, D_MODEL), lambda i: (i, 0)),
                   pl.BlockSpec((N_DEV, tm, IN_BLOCK), lambda i: (0, i, 0)),
                   pl.BlockSpec((8, D_MODEL), lambda i: (0, 0))],
        out_shape=[jax.ShapeDtypeStruct((T, D_MODEL), F32), jax.ShapeDtypeStruct((N_DEV, T, IN_BLOCK), MM),
                   jax.ShapeDtypeStruct((8, D_MODEL), F32)],
        scratch_shapes=[pltpu.VMEM((tm, D_MODEL), F32)],
        compiler_params=_cparams(("arbitrary",), VMEM_LIMIT),
    )(x, dx1, *dsecs, g_mix, w_in)


def _pick_block(n, cap):
    best = None
    for cand in range(128, min(n, cap) + 1, 128):
        if n % cand == 0:
            best = cand
    return n if best is None else best


def _matmul_tn(name, a, b):
    T, M = a.shape
    N = b.shape[1]
    bm, bk = _pick_block(M, 1408), min(DW_TOKENS, T)
    bn = _pick_block(N, 2560)

    def body(a_ref, b_ref, c_ref):
        @pl.when(pl.program_id(2) == 0)
        def _():
            c_ref[...] = jnp.zeros_like(c_ref)

        c_ref[...] += _dot_tn(a_ref[...], b_ref[...])

    return pl.pallas_call(
        body, name=name, grid=(M // bm, N // bn, T // bk),
        in_specs=[pl.BlockSpec((bk, bm), lambda i, j, k: (k, i)), pl.BlockSpec((bk, bn), lambda i, j, k: (k, j))],
        out_specs=pl.BlockSpec((bm, bn), lambda i, j, k: (i, j)), out_shape=jax.ShapeDtypeStruct((M, N), F32),
        compiler_params=_cparams(("parallel", "parallel", "arbitrary"), VMEM_LIMIT),
    )(a, b)


def _matmul_tn_blocks(name, a, b):
    T, M = a.shape
    nd, _, c = b.shape
    bm, bk = _pick_block(M, 512), min(DW_TOKENS, T)

    def body(a_ref, b_ref, c_ref):
        @pl.when(pl.program_id(1) == 0)
        def _():
            c_ref[...] = jnp.zeros_like(c_ref)

        a_blk = a_ref[...].astype(MM)
        for d in range(nd):
            c_ref[d] += _dot_tn(b_ref[d], a_blk)

    return pl.pallas_call(
        body, name=name, grid=(M // bm, T // bk),
        in_specs=[pl.BlockSpec((bk, bm), lambda i, k: (k, i)), pl.BlockSpec((nd, bk, c), lambda i, k: (0, k, 0))],
        out_specs=pl.BlockSpec((nd, c, bm), lambda i, k: (0, 0, i)),
        out_shape=jax.ShapeDtypeStruct((nd, c, M), F32),
        compiler_params=_cparams(("parallel", "arbitrary"), VMEM_LIMIT),
    )(a, b)


def _pad_gain(g, n):
    return jnp.pad(g.reshape(1, -1), ((0, 0), (0, n - g.shape[-1])))


GROUP_A = ("w_ffn_gate", "w_ffn_up", "w_ffn_down", "w_ple_gate", "w_ple_proj")
GROUP_B = ("w_branch", "w_out")
GROUP_C = ("w_in", "w_uq", "w_ukv")
EARLY = GROUP_C
LATE = GROUP_B + GROUP_A
TRANSPOSED = ("w_in", "w_uq", "w_ffn_gate", "w_ffn_up")


def _local_step(x, p, pos, tgt, small, big, late_blocks=None, core=None):
    T = x.shape[0]
    tm = min(ROW_TILE, T)
    tw = min(WIDE_TILE, T)
    w_in = big["w_in"]
    w_uq = jnp.pad(big["w_uq"], ((0, 0), (0, HEAD_PAD - QK_DIM), (0, 0)))
    w_ukv = big["w_ukv"]

    g_mix, g_qa, g_kva = small["mix_norm_g"], small["q_a_norm_g"], small["kv_a_norm_g"]
    g_qn, g_kn = _pad_gain(small["q_norm_g"], HEAD_PAD), _pad_gain(small["k_norm_g"], HEAD_PAD)
    g_out, g_ffn = small["hg_out_norm_g"], small["ffn_norm_g"]
    g_pg, g_post = small["ple_gate_norm_g"], small["ple_post_norm_g"]
    logits = small["hg_lb_logits"]
    lb = _lower_bound(logits)

    h, cq, ckv, kr, hq, hf, hi, hg, bg = _in_proj_fwd(x, g_mix, w_in, T, tw)
    qf, kf, vf = _mla_prep_fwd(cq, ckv, kr, pos, g_qa, g_kva, g_qn, g_kn, w_uq, w_ukv, T, tw)
    if late_blocks is None:
        attn, lse = _flash_fwd(qf, kf, vf, T)
    else:
        attn, lse, *late = _flash_fwd(qf, kf, vf, T, ag_blocks=[late_blocks[n] for n in LATE])
        big = {**big, **dict(zip(LATE, late))}
    o, s0 = _hgrn_fwd(hq, hf, hi, lb, T)
    w_branch = jnp.moveaxis(big["w_branch"].reshape(N_DEV, 2, HG_W, HEAD_PAD), 0, 2).reshape(2, HG_W, D_MODEL)
    w_bra = jnp.pad(w_branch[0].reshape(MLA_HEADS, V_DIM, D_MODEL),
                    ((0, 0), (0, HEAD_PAD - V_DIM), (0, 0))).reshape(MLA_HEADS * HEAD_PAD, D_MODEL)
    w_brb = w_branch[1]
    w_out = big["w_out"].reshape(D_MODEL, D_MODEL)
    w_g, w_u = big["w_ffn_gate"].reshape(FFN, D_MODEL), big["w_ffn_up"].reshape(FFN, D_MODEL)
    w_d = big["w_ffn_down"].reshape(FFN, D_MODEL)
    w_pg, w_pp = big["w_ple_gate"].reshape(D_MODEL, D_MODEL), big["w_ple_proj"]
    x1, ya, yb, m, rec = _merge_fwd(attn, o, hg, bg, x, g_out, w_bra, w_brb, w_out, T, tw)
    x2, gt, up, h2 = _ffn_fwd(x1, g_ffn, w_g, w_u, w_d, T, tw)
    dx2, loss_p, dg_post, dg_pg, d_pg, d_pp = _ple_loss(x2, p, tgt, g_pg, g_post, w_pg, w_pp, T, tw)

    grads, sibs, gots = {}, {}, {}
    dist = core is not None
    pick = lambda names: [grads[n] for n in names] if dist else ()

    def partials(tag, names, got):
        if not dist:
            return ()
        sibs.update(zip(names, got))
        return _chip_partials("rs_partial_" + tag, pick(names), got, core)

    dx1, a, dgt, dup, dg_ffn = _ffn_bwd(dx2, x1, gt, up, g_ffn, w_g, w_u, w_d, T, tm)
    grads["w_ffn_gate"] = _matmul_tn("dw_gate", dgt, h2).reshape(N_DEV, -1, D_MODEL)
    grads["w_ffn_up"] = _matmul_tn("dw_up", dup, h2).reshape(N_DEV, -1, D_MODEL)
    grads["w_ffn_down"] = _matmul_tn("dw_down", a, dx2).reshape(N_DEV, -1, D_MODEL)
    grads["w_ple_gate"] = d_pg.reshape(N_DEV, -1, D_MODEL)
    grads["w_ple_proj"] = d_pp

    dattn, do, dhg, dbg, dg_out, d_out, d_bra, d_brb, *sib_a = _merge_bwd(
        dx1, ya, yb, bg, o, hg, attn, m, rec, g_out, w_bra, w_brb, w_out, T, tm, xchg=(pick(GROUP_A), ()))
    parts_a = partials("a", GROUP_A, sib_a)
    d_bra = d_bra.reshape(N_DEV, MLA_HEADS, HEAD_PAD, HEAD_PAD)[:, :, :V_DIM].reshape(N_DEV, HG_W, HEAD_PAD)
    grads["w_branch"] = jnp.concatenate([d_bra, d_brb], axis=1)
    grads["w_out"] = d_out.reshape(N_DEV, -1, D_MODEL)

    dhq, dhf, dhi, dlb, *got = _hgrn_bwd(hq, hf, hi, do, s0, lb, T, xchg=(pick(GROUP_B), parts_a))
    sib_b, got_a = got[:len(GROUP_B)], got[len(GROUP_B):]
    parts_b = partials("b", GROUP_B, sib_b)
    dqf, dkf, dvf, *got_b = _flash_bwd(qf, kf, vf, attn, dattn, lse, T, xchg=((), parts_b))
    (dcq, dckv, dkr, dg_qa, dg_kva, dg_qn, dg_kn, d_uq, d_ukv) = _mla_prep_bwd(
        cq, ckv, kr, pos, dqf, dkf, dvf, g_qa, g_kva, g_qn, g_kn, w_uq, w_ukv, T, tw)
    grad_x, dproj, dg_mix = _in_proj_bwd(x, dx1, [dcq, dckv, dkr, dhq, dhf, dhi, dhg, dbg], g_mix, w_in, T, tw)
    grads["w_in"] = _matmul_tn_blocks("dw_in", h, dproj)
    grads["w_uq"] = d_uq[:, :QK_DIM]
    grads["w_ukv"] = d_ukv
    if dist:
        gots.update(zip(GROUP_A, got_a))
        gots.update(zip(GROUP_B, got_b))

    dl0 = dlb * lb * (1.0 - lb)
    small_g = {
        "mix_norm_g": dg_mix, "q_a_norm_g": dg_qa, "kv_a_norm_g": dg_kva, "q_norm_g": dg_qn, "k_norm_g": dg_kn,
        "hg_lb_logits": jnp.concatenate([dl0, -dl0], axis=0), "hg_out_norm_g": dg_out,
        "ffn_norm_g": dg_ffn, "ple_gate_norm_g": dg_pg, "ple_post_norm_g": dg_post,
    }
    return loss_p, grad_x, small_g, grads, sibs, gots


def _lower_bound(logits):
    def body(l_ref, lb_ref):
        l = l_ref[...]
        mx = jnp.max(l, axis=0, keepdims=True)
        e = jnp.exp(l - mx)
        lb_ref[...] = e[0:1] / jnp.sum(e, axis=0, keepdims=True)

    return pl.pallas_call(body, name="lower_bound", out_shape=jax.ShapeDtypeStruct((1, HG_W), F32))(logits)


def _my_place():
    return lax.axis_index("x"), lax.axis_index("y"), lax.axis_index("c")


def _all_gather(name, blocks):
    n = len(blocks)

    def body(*refs):
        x_refs, out_refs, sems = refs[:n], refs[n:2 * n], refs[2 * n:]
        _ag_start(x_refs, out_refs, sems)
        _ag_finish(x_refs, out_refs, sems)

    any_spec = pl.BlockSpec(memory_space=pl.ANY)
    return pl.pallas_call(
        body, name=name, out_shape=_ag_out_shapes(blocks),
        in_specs=[any_spec] * n, out_specs=[any_spec] * n, scratch_shapes=_ag_sems(n),
    )(*blocks)


def _ag_out_shapes(blocks):
    return [jax.ShapeDtypeStruct((N_DEV,) + b.shape, b.dtype) for b in blocks]


def _ag_sems(n):
    return [pltpu.SemaphoreType.DMA((7 * n,)), pltpu.SemaphoreType.DMA((7 * n,)), pltpu.SemaphoreType.DMA((n,))]


def _ag_parts(x_refs, out_refs, sems):
    send_sems, recv_sems, local_sems = sems
    x, y, c = _my_place()
    me, sibling = (x, y, c), (x, y, 1 - c)
    chips = [(1 - x, y), (x, 1 - y), (1 - x, 1 - y)]
    n = len(x_refs)

    def copy(a, k, block, to, own=False):
        px, py, pc = block
        dst = out_refs[a].at[4 * px + 2 * py + pc]
        return pltpu.make_async_remote_copy(
            src_ref=x_refs[a] if own else dst, dst_ref=dst, send_sem=send_sems.at[7 * a + k],
            recv_sem=recv_sems.at[7 * a + k], device_id=to, device_id_type=MESH_ID)

    mine = [pltpu.make_async_copy(x_refs[a], out_refs[a].at[4 * x + 2 * y + c], local_sems.at[a]) for a in range(n)]
    first = []
    for a in range(n):
        first.append(copy(a, 0, me, sibling, own=True))
        first += [copy(a, 1 + j, me, (*chip, c), own=True) for j, chip in enumerate(chips)]
    return copy, mine, first, me, sibling, chips, c, n


def _ag_start(x_refs, out_refs, sems):
    _, mine, first, *_ = _ag_parts(x_refs, out_refs, sems)
    for cp in mine + first:
        cp.start()


def _ag_finish(x_refs, out_refs, sems):
    copy, mine, first, me, sibling, chips, c, n = _ag_parts(x_refs, out_refs, sems)
    passed = []
    for j, chip in enumerate(chips):
        for a in range(n):
            copy(a, 1 + j, (*chip, c), me).wait_recv()
            passed.append(copy(a, 4 + j, (*chip, c), sibling))
            passed[-1].start()
    for a in range(n):
        copy(a, 0, sibling, me).wait_recv()
    for j, chip in enumerate(chips):
        for a in range(n):
            copy(a, 4 + j, (*chip, 1 - c), me).wait_recv()
    for cp in first + passed:
        cp.wait_send()
    for cp in mine:
        cp.wait()


N_PARTS = 4


def _part_spec(rows, cols, t_pos, lead_block=(), lead_index=lambda *args: ()):
    if rows % (16 * N_PARTS) == 0:
        axis, shape, count = 0, (rows // N_PARTS, cols), N_PARTS
    elif cols % (128 * N_PARTS) == 0:
        axis, shape, count = 1, (rows, cols // N_PARTS), N_PARTS
    else:
        axis, shape, count = 0, (rows, cols), 1

    def index(*args):
        i = jnp.minimum(args[t_pos], count - 1)
        return (*lead_index(*args), *((i, 0) if axis == 0 else (0, i)))

    return pl.BlockSpec((*lead_block, *shape), index)


def _chip_partials(name, gs, sibs, place):
    n = len(gs)

    def body(place_ref, *refs):
        for g_ref, sib_ref, out_ref in zip(refs[:n], refs[n:2 * n], refs[2 * n:]):
            out_ref[...] = (g_ref[...] + sib_ref[...]).astype(MM)

    own = [_part_spec(*g.shape[1:], 1, (1,), lambda k, t, place_ref: (2 * place_ref[1 + k] + place_ref[0],))
           for g in gs]
    theirs = [_part_spec(*g.shape[1:], 1, (1,), lambda k, t, place_ref: (place_ref[1 + k],)) for g in gs]
    out = [_part_spec(*g.shape[1:], 1, (1,), lambda k, t, place_ref: (k,)) for g in gs]
    grid_spec = pltpu.PrefetchScalarGridSpec(
        num_scalar_prefetch=1, grid=(3, N_PARTS), in_specs=own + theirs, out_specs=out)
    return pl.pallas_call(
        body, name=name, grid_spec=grid_spec, out_shape=[jax.ShapeDtypeStruct((3,) + g.shape[1:], MM) for g in gs],
        compiler_params=_cparams(("arbitrary", "arbitrary"), VMEM_LIMIT),
    )(place, *gs, *sibs)


def _x_count(xchg):
    return len(xchg[0]) + len(xchg[1])


def _x_out_shapes(xchg):
    return ([jax.ShapeDtypeStruct((4,) + g.shape[1:], g.dtype) for g in xchg[0]]
            + [jax.ShapeDtypeStruct((3,) + p.shape[1:], p.dtype) for p in xchg[1]])


def _x_sems(xchg):
    n = 4 * len(xchg[0]) + 3 * len(xchg[1])
    return [pltpu.SemaphoreType.DMA((n,)), pltpu.SemaphoreType.DMA((n,))] if n else []


def _x_copies(n_sib, in_refs, out_refs, sems):
    if not in_refs:
        return []
    send_sems, recv_sems = sems
    x, y, c = _my_place()
    chips = [(1 - x, y), (x, 1 - y), (1 - x, 1 - y)]
    copies = []

    def add(src, dst, to):
        k = len(copies)
        copies.append(pltpu.make_async_remote_copy(
            src_ref=src, dst_ref=dst, send_sem=send_sems.at[k], recv_sem=recv_sems.at[k], device_id=to,
            device_id_type=MESH_ID))

    for a, (src, dst) in enumerate(zip(in_refs, out_refs)):
        if a < n_sib:
            for j in range(4):
                add(src.at[2 * j + 1 - c], dst.at[j], (x, y, 1 - c))
        else:
            for k, (px, py) in enumerate(chips):
                add(src.at[k], dst.at[k], (px, py, c))
    return copies


def _reduce_chips(name, gs, gather):
    n, n_ag = len(gs), len(gather)

    def body(*refs):
        g_refs, ag_in, refs = refs[:n], refs[n:n + n_ag], refs[n + n_ag:]
        sib_refs, got_refs, ag_out, refs = refs[:n], refs[n:2 * n], refs[2 * n:2 * n + n_ag], refs[2 * n + n_ag:]
        own_s, oth_s, part_s, refs = refs[:n], refs[n:2 * n], refs[2 * n:3 * n], refs[3 * n:]
        sib_send, sib_recv, chip_send, chip_recv, load_sems = refs[:5]
        x, y, c = _my_place()
        chips = [(1 - x, y), (x, 1 - y), (1 - x, 1 - y), (x, y)]
        slot = lambda k: 2 * chips[k][0] + chips[k][1]

        def to_sibling(a, k):
            return pltpu.make_async_remote_copy(
                src_ref=g_refs[a].at[2 * slot(k) + 1 - c], dst_ref=sib_refs[a].at[slot(k)],
                send_sem=sib_send.at[4 * a + k], recv_sem=sib_recv.at[4 * a + k], device_id=(x, y, 1 - c),
                device_id_type=MESH_ID)

        def to_chip(a, k):
            return pltpu.make_async_remote_copy(
                src_ref=part_s[a].at[k], dst_ref=got_refs[a].at[k], send_sem=chip_send.at[3 * a + k],
                recv_sem=chip_recv.at[3 * a + k], device_id=(*chips[k], c), device_id_type=MESH_ID)

        def load_own(a, k):
            return pltpu.make_async_copy(g_refs[a].at[2 * slot(k) + c], own_s[a].at[k], load_sems.at[6 * a + k])

        def load_other(a, k):
            return pltpu.make_async_copy(sib_refs[a].at[slot(k)], oth_s[a].at[k], load_sems.at[6 * a + 3 + k])

        if n_ag:
            _ag_start(ag_in, ag_out, refs[5:])
        for k in range(4):
            for a in range(n):
                to_sibling(a, k).start()
        for k in range(3):
            for a in range(n):
                load_own(a, k).start()
        for k in range(3):
            for a in range(n):
                to_sibling(a, k).wait_recv()
                load_other(a, k).start()
            for a in range(n):
                load_own(a, k).wait()
                load_other(a, k).wait()
                part_s[a][k] = (own_s[a][k] + oth_s[a][k]).astype(MM)
                to_chip(a, k).start()
        if n_ag:
            _ag_finish(ag_in, ag_out, refs[5:])
        for a in range(n):
            to_sibling(a, 3).wait_recv()
        for k in range(3):
            for a in range(n):
                to_chip(a, k).wait()
        for k in range(4):
            for a in range(n):
                to_sibling(a, k).wait_send()

    any_spec = pl.BlockSpec(memory_space=pl.ANY)
    shapes = [g.shape[1:] for g in gs]
    dma_sems = lambda count: pltpu.SemaphoreType.DMA((count,))
    return pl.pallas_call(
        body, name=name,
        out_shape=([jax.ShapeDtypeStruct((4,) + s, F32) for s in shapes]
                   + [jax.ShapeDtypeStruct((3,) + s, MM) for s in shapes] + _ag_out_shapes(gather)),
        in_specs=[any_spec] * (n + n_ag), out_specs=[any_spec] * (2 * n + n_ag),
        scratch_shapes=([pltpu.VMEM((3,) + s, F32) for s in shapes] * 2 + [pltpu.VMEM((3,) + s, MM) for s in shapes]
                        + [dma_sems(4 * n), dma_sems(4 * n), dma_sems(3 * n), dma_sems(3 * n), dma_sems(6 * n)]
                        + (_ag_sems(n_ag) if n_ag else [])),
        compiler_params=_cparams((), VMEM_LIMIT),
    )(*gs, *gather)


def _adamw_math(w, g, m, v):
    m = ADAM_B1 * m + (1.0 - ADAM_B1) * g
    v = ADAM_B2 * v + (1.0 - ADAM_B2) * jnp.square(g)
    m_hat = m / (1.0 - ADAM_B1 ** ADAM_STEP)
    v_hat = v / (1.0 - ADAM_B2 ** ADAM_STEP)
    delta = -ADAM_LR * (m_hat / (jnp.sqrt(v_hat) + ADAM_EPS) + ADAM_WD * w)
    return delta, m, v


def _sum_adamws(name, gs, sibs, gots, ws, ms, vs, slot_idx, chip_idx):
    n = len(gs)

    def body(s_ref, j_ref, *refs):
        ins, outs = refs[:6 * n], refs[6 * n:]
        for a in range(n):
            g_ref, sib_ref, got_ref, w_ref, m_ref, v_ref = (ins[k * n + a] for k in range(6))
            go_ref, d_ref, m2_ref, v2_ref = outs[4 * a:4 * a + 4]
            grad = g_ref[0] + sib_ref[0]
            for k in range(3):
                grad = grad + got_ref[k].astype(F32)
            go_ref[...] = grad
            d_ref[...], m2_ref[...], v2_ref[...] = _adamw_math(w_ref[...], grad, m_ref[...], v_ref[...])

    shapes = [g.shape[1:] for g in gs]
    flat = [_part_spec(*s, 0) for s in shapes]
    in_specs = ([_part_spec(*s, 0, (1,), lambda t, s_ref, j_ref: (s_ref[0],)) for s in shapes]
                + [_part_spec(*s, 0, (1,), lambda t, s_ref, j_ref: (j_ref[0],)) for s in shapes]
                + [_part_spec(*s, 0, (3,), lambda t, s_ref, j_ref: (0,)) for s in shapes] + flat * 3)
    grid_spec = pltpu.PrefetchScalarGridSpec(
        num_scalar_prefetch=2, grid=(N_PARTS,), in_specs=in_specs, out_specs=[f for f in flat for _ in range(4)])
    res = pl.pallas_call(
        body, name=name, grid_spec=grid_spec,
        out_shape=[jax.ShapeDtypeStruct(s, F32) for s in shapes for _ in range(4)],
        compiler_params=_cparams(("arbitrary",), VMEM_LIMIT),
    )(slot_idx, chip_idx, *gs, *sibs, *gots, *ws, *ms, *vs)
    return [res[4 * a:4 * a + 4] for a in range(n)]


BIG = ("w_in", "w_uq", "w_ukv", "w_branch", "w_out", "w_ffn_gate", "w_ffn_up", "w_ffn_down", "w_ple_gate", "w_ple_proj")
SMALL = (
    ("mix_norm_g", 0, 1, 1024), ("q_a_norm_g", 1, 1, 384), ("kv_a_norm_g", 2, 1, 256), ("q_norm_g", 3, 1, 96),
    ("k_norm_g", 4, 1, 96), ("hg_lb_logits", 5, 2, 512), ("hg_out_norm_g", 7, 1, 128), ("ffn_norm_g", 8, 1, 1024),
    ("ple_gate_norm_g", 9, 1, 1024), ("ple_post_norm_g", 10, 1, 1024),
)
SLAB_ROWS, LOSS_ROW = 16, 15


def _pack_partials(small_g, loss_p):
    def body(*refs):
        val_refs, loss_ref, out_ref = refs[:len(SMALL)], refs[len(SMALL)], refs[len(SMALL) + 1]
        out_ref[...] = jnp.zeros_like(out_ref)
        for (_, r0, rows, cols), ref in zip(SMALL, val_refs):
            val = ref[...]
            if val.shape[0] != rows:
                val = jnp.sum(val, axis=0, keepdims=True)
            out_ref[r0:r0 + rows, :cols] = val[:, :cols]
        out_ref[LOSS_ROW:LOSS_ROW + 1, :HEAD_PAD] = jnp.full((1, HEAD_PAD), jnp.sum(loss_ref[...]), F32)

    return pl.pallas_call(
        body, name="pack_partials", out_shape=jax.ShapeDtypeStruct((SLAB_ROWS, D_MODEL), F32),
    )(*[small_g[n] for n, *_ in SMALL], loss_p)


def _adamw_small(parts, ws, ms, vs):
    n = len(SMALL)

    def body(p_ref, *refs):
        ins, loss_ref, outs = refs[:3 * n], refs[3 * n], refs[3 * n + 1:]
        total = p_ref[0]
        for d in range(1, N_DEV):
            total = total + p_ref[d]
        loss_ref[...] = total[LOSS_ROW:LOSS_ROW + 1, 0:1]
        for a, (_, r0, rows, cols) in enumerate(SMALL):
            g = total[r0:r0 + rows, :cols]
            outs[4 * a][...] = g
            outs[4 * a + 1][...], outs[4 * a + 2][...], outs[4 * a + 3][...] = _adamw_math(
                ins[a][...], g, ins[n + a][...], ins[2 * n + a][...])

    shapes = [jax.ShapeDtypeStruct((rows, cols), F32) for _, _, rows, cols in SMALL]
    res = pl.pallas_call(
        body, name="adamw_small", out_shape=[jax.ShapeDtypeStruct((1, 1), F32)] + [s for s in shapes for _ in range(4)],
    )(parts, *ws, *ms, *vs)
    return res[0], [res[1 + 4 * a:5 + 4 * a] for a in range(n)]


_WEIGHTS = ["mix_norm_g", "w_in", "q_a_norm_g", "w_uq", "kv_a_norm_g", "w_ukv", "q_norm_g", "k_norm_g", "hg_lb_logits",
            "hg_out_norm_g", "w_branch", "w_out", "ffn_norm_g", "w_ffn_gate", "w_ffn_up", "w_ffn_down",
            "ple_gate_norm_g", "w_ple_gate", "w_ple_proj", "ple_post_norm_g"]


def _step(x, p, positions, tgt, w, m, v):
    small_names = [n for n, *_ in SMALL]
    T = x.shape[1]
    px, py, pc = _my_place()
    as_idx = lambda t: jnp.reshape(t, (1,)).astype(jnp.int32)

    def two_d(n, t):
        t = t.reshape(-1, t.shape[-1])
        return t.T if n in TRANSPOSED else t

    def full_shape(n, t):
        return (t.T if n in TRANSPOSED else t).reshape(w[n].shape)

    blocks = {n: two_d(n, w[n]).astype(MM) for n in BIG}
    big = dict(zip(EARLY, _all_gather("ag_weights", [blocks[n] for n in EARLY])))
    small = {n: (w[n] if n == "hg_lb_logits" else w[n].reshape(1, -1)) for n in small_names}

    loss_p, grad_x, small_g, grads, sibs, gots = _local_step(
        x[0], p[0, 0], positions.reshape(T, 1), tgt[0], small, big, late_blocks=blocks,
        core=jnp.stack([pc, 2 * (1 - px) + py, 2 * px + 1 - py, 2 * (1 - px) + 1 - py]).astype(jnp.int32))

    *res_c, slabs = _reduce_chips("rs_chips", [grads[n] for n in GROUP_C], [_pack_partials(small_g, loss_p)])
    sibs.update(zip(GROUP_C, res_c[:len(GROUP_C)]))
    gots.update(zip(GROUP_C, res_c[len(GROUP_C):]))
    out_g, out_d, out_m, out_v = {}, {}, {}, {}
    for tag, names in (("ab", GROUP_A + GROUP_B), ("c", GROUP_C)):
        pick = lambda table: [table[n] for n in names]
        res = _sum_adamws("adamw_" + tag, pick(grads), pick(sibs), pick(gots), [two_d(n, w[n]) for n in names],
                          [two_d(n, m[n]) for n in names], [two_d(n, v[n]) for n in names],
                          as_idx(4 * px + 2 * py + pc), as_idx(2 * px + py))
        for n, r in zip(names, res):
            out_g[n], out_d[n], out_m[n], out_v[n] = [full_shape(n, t) for t in r]

    loss, res = _adamw_small(slabs, *([t[n] for n in small_names] for t in (w, m, v)))
    for n, r in zip(small_names, res):
        out_g[n], out_d[n], out_m[n], out_v[n] = r

    outs = [loss.reshape(()), grad_x[None]]
    for table in (out_g, out_d, out_m, out_v):
        outs += [table[n] for n in _WEIGHTS]
    return tuple(outs)


def kernel(x, p, positions, mix_norm_g, w_in, q_a_norm_g, w_uq, kv_a_norm_g, w_ukv, q_norm_g, k_norm_g, hg_lb_logits, hg_out_norm_g, w_branch, w_out, ffn_norm_g, w_ffn_gate, w_ffn_up, w_ffn_down, ple_gate_norm_g, w_ple_gate, w_ple_proj, ple_post_norm_g, loss_target, m_mix_norm_g, m_w_in, m_q_a_norm_g, m_w_uq, m_kv_a_norm_g, m_w_ukv, m_q_norm_g, m_k_norm_g, m_hg_lb_logits, m_hg_out_norm_g, m_w_branch, m_w_out, m_ffn_norm_g, m_w_ffn_gate, m_w_ffn_up, m_w_ffn_down, m_ple_gate_norm_g, m_w_ple_gate, m_w_ple_proj, m_ple_post_norm_g, v_mix_norm_g, v_w_in, v_q_a_norm_g, v_w_uq, v_kv_a_norm_g, v_w_ukv, v_q_norm_g, v_k_norm_g, v_hg_lb_logits, v_hg_out_norm_g, v_w_branch, v_w_out, v_ffn_norm_g, v_w_ffn_gate, v_w_ffn_up, v_w_ffn_down, v_ple_gate_norm_g, v_w_ple_gate, v_w_ple_proj, v_ple_post_norm_g):
    w = dict(mix_norm_g=mix_norm_g, w_in=w_in, q_a_norm_g=q_a_norm_g, w_uq=w_uq, kv_a_norm_g=kv_a_norm_g, w_ukv=w_ukv,
             q_norm_g=q_norm_g, k_norm_g=k_norm_g, hg_lb_logits=hg_lb_logits, hg_out_norm_g=hg_out_norm_g,
             w_branch=w_branch, w_out=w_out, ffn_norm_g=ffn_norm_g, w_ffn_gate=w_ffn_gate, w_ffn_up=w_ffn_up,
             w_ffn_down=w_ffn_down, ple_gate_norm_g=ple_gate_norm_g, w_ple_gate=w_ple_gate, w_ple_proj=w_ple_proj,
             ple_post_norm_g=ple_post_norm_g)
    m = dict(mix_norm_g=m_mix_norm_g, w_in=m_w_in, q_a_norm_g=m_q_a_norm_g, w_uq=m_w_uq, kv_a_norm_g=m_kv_a_norm_g,
             w_ukv=m_w_ukv, q_norm_g=m_q_norm_g, k_norm_g=m_k_norm_g, hg_lb_logits=m_hg_lb_logits,
             hg_out_norm_g=m_hg_out_norm_g, w_branch=m_w_branch, w_out=m_w_out, ffn_norm_g=m_ffn_norm_g,
             w_ffn_gate=m_w_ffn_gate, w_ffn_up=m_w_ffn_up, w_ffn_down=m_w_ffn_down,
             ple_gate_norm_g=m_ple_gate_norm_g, w_ple_gate=m_w_ple_gate, w_ple_proj=m_w_ple_proj,
             ple_post_norm_g=m_ple_post_norm_g)
    v = dict(mix_norm_g=v_mix_norm_g, w_in=v_w_in, q_a_norm_g=v_q_a_norm_g, w_uq=v_w_uq, kv_a_norm_g=v_kv_a_norm_g,
             w_ukv=v_w_ukv, q_norm_g=v_q_norm_g, k_norm_g=v_k_norm_g, hg_lb_logits=v_hg_lb_logits,
             hg_out_norm_g=v_hg_out_norm_g, w_branch=v_w_branch, w_out=v_w_out, ffn_norm_g=v_ffn_norm_g,
             w_ffn_gate=v_w_ffn_gate, w_ffn_up=v_w_ffn_up, w_ffn_down=v_w_ffn_down,
             ple_gate_norm_g=v_ple_gate_norm_g, w_ple_gate=v_w_ple_gate, w_ple_proj=v_w_ple_proj,
             ple_post_norm_g=v_ple_post_norm_g)
    return _step(x, p, positions, loss_target, w, m, v)
```

```python
import jax
import jax.numpy as jnp
import numpy as np
from jax import lax
from jax.experimental import pallas as pl
from jax.experimental.pallas import tpu as pltpu

F32 = jnp.float32
MM = jnp.bfloat16
MESH_ID = pl.DeviceIdType.MESH

D_MODEL = 1024
N_DEV = 8
MLA_HEADS = 8
QK_NOPE = 64
QK_ROPE = 32
QK_DIM = 96
V_DIM = 64
HEAD_PAD = 128
Q_RANK = 384
KV_RANK = 256
ROPE_BASE = 10000.0
HG_HEADS = 4
HG_DIM = 128
HG_W = 512
HG_CHUNK = 64
FFN = 2816
PLE = 256
EPS = 1e-6
ATT_SCALE = QK_DIM ** -0.5
NEG = -1e30

ADAM_LR = 0.001
ADAM_B1 = 0.9
ADAM_B2 = 0.999
ADAM_EPS = 1e-08
ADAM_WD = 0.01
ADAM_STEP = 10

COL_SECTIONS = ((0, 384), (384, 256), (640, 32), (672, 512), (1184, 512), (1696, 512), (2208, 512), (2720, 2048))
STORED_WIDTHS = tuple(HEAD_PAD if n == QK_ROPE else n for _, n in COL_SECTIONS)
IN_COLS = 4768
IN_BLOCK = IN_COLS // N_DEV

VMEM_LIMIT = 58 * 1024 * 1024
WIDE_TILE = 512
ROW_TILE = 256
DW_TOKENS = 1024
ATT_TILE = 1024
ATT_HEADS = 4
HG_BLOCK = 512
HG_UNROLL = 4


def _dot(a, b):
    return jnp.dot(a.astype(MM), b.astype(MM), preferred_element_type=F32)


def _dot_nt(a, b):
    return lax.dot_general(a.astype(MM), b.astype(MM), (((1,), (1,)), ((), ())), preferred_element_type=F32)


def _dot_tn(a, b):
    return lax.dot_general(a.astype(MM), b.astype(MM), (((0,), (0,)), ((), ())), preferred_element_type=F32)


def _sigmoid(x):
    return 1.0 / (1.0 + jnp.exp(-x))


def _rms(x, n=None):
    n = x.shape[-1] if n is None else n
    r = lax.rsqrt(jnp.sum(x * x, axis=-1, keepdims=True) * (1.0 / n) + EPS)
    return x * r, r


def _rms_bwd(dxh, xh, r, n=None):
    n = xh.shape[-1] if n is None else n
    return r * (dxh - xh * (jnp.sum(dxh * xh, axis=-1, keepdims=True) * (1.0 / n)))


def _rope_tables(pos, tm):
    lane = lax.broadcasted_iota(jnp.int32, (tm, HEAD_PAD), 1)
    idx = jnp.where(lane < QK_NOPE + QK_ROPE // 2, lane - QK_NOPE, lane - QK_NOPE - QK_ROPE // 2)
    inv = jnp.exp(idx.astype(F32) * (-np.log(ROPE_BASE) * 2.0 / QK_ROPE))
    ang = pos.astype(F32) * inv
    in_rope = (lane >= QK_NOPE) & (lane < QK_DIM)
    first = lane < QK_NOPE + QK_ROPE // 2
    cos_t = jnp.where(in_rope, jnp.cos(ang), 1.0)
    sin_t = jnp.where(in_rope, jnp.where(first, -jnp.sin(ang), jnp.sin(ang)), 0.0)
    return cos_t, sin_t, (first, in_rope)


def _rope_swap(x, halves):
    first, in_rope = halves
    half = QK_ROPE // 2
    return jnp.where(in_rope, jnp.where(first, pltpu.roll(x, HEAD_PAD - half, 1), pltpu.roll(x, half, 1)), 0.0)


def _cparams(sem, vmem=None):
    return pltpu.CompilerParams(dimension_semantics=sem, vmem_limit_bytes=vmem)


def _row_call(name, body, T, tm, row_ins, full_ins, row_outs, acc_outs, vmem=None, scratch=(), xchg=((), ())):
    n_in, n_out, n_x = len(row_ins) + len(full_ins), len(row_outs) + len(acc_outs), _x_count(xchg)
    steps = T // tm

    def kern(*refs):
        ins, x_in, refs = refs[:n_in], refs[n_in:n_in + n_x], refs[n_in + n_x:]
        outs, x_out, refs = refs[:n_out], refs[n_out:n_out + n_x], refs[n_out + n_x:]
        scr, x_sems = refs[:len(scratch)], refs[len(scratch):]
        i = pl.program_id(0)
        if n_x:
            @pl.when(i == 0)
            def _():
                for cp in _x_copies(len(xchg[0]), x_in, x_out, x_sems):
                    cp.start()

        body(i, *ins, *outs, *scr)
        if n_x:
            @pl.when(i == steps - 1)
            def _():
                for cp in _x_copies(len(xchg[0]), x_in, x_out, x_sems):
                    cp.wait()

    any_spec = pl.BlockSpec(memory_space=pl.ANY)
    in_specs = [pl.BlockSpec((tm, a.shape[1]), lambda i: (i, 0)) for a in row_ins]
    in_specs += [pl.BlockSpec(a.shape, lambda i, nd=a.ndim: (0,) * nd, pipeline_mode=pl.Buffered(1)) for a in full_ins]
    out_specs = [pl.BlockSpec((tm, n), lambda i: (i, 0)) for n, _ in row_outs]
    out_specs += [pl.BlockSpec(s, lambda i, nd=len(s): (0,) * nd) for s, _ in acc_outs]
    out_shape = [jax.ShapeDtypeStruct((T, n), dt) for n, dt in row_outs]
    out_shape += [jax.ShapeDtypeStruct(s, dt) for s, dt in acc_outs]
    return pl.pallas_call(
        kern, name=name, grid=(steps,), in_specs=in_specs + [any_spec] * n_x, out_specs=out_specs + [any_spec] * n_x,
        out_shape=out_shape + _x_out_shapes(xchg), scratch_shapes=list(scratch) + _x_sems(xchg),
        compiler_params=_cparams(("arbitrary",), vmem),
    )(*row_ins, *full_ins, *xchg[0], *xchg[1])


FFN_HALVES = (slice(0, FFN // 2), slice(FFN // 2, FFN))
ROW_CHUNK = 16
CHUNK_UNROLL = True


def _by_chunks(tm, fn):
    def step(c, carry):
        fn(pl.ds(pl.multiple_of(c * ROW_CHUNK, ROW_CHUNK), ROW_CHUNK))
        return carry

    lax.fori_loop(0, tm // ROW_CHUNK, step, 0, unroll=CHUNK_UNROLL)


def _fold8(x):
    return x[:8] + x[8:]


def _acc(ref, i, val):
    @pl.when(i == 0)
    def _():
        ref[...] = val

    @pl.when(i != 0)
    def _():
        ref[...] += val


def _in_proj_fwd(x, g_mix, w_in, T, tm):
    def body(i, x_ref, g_ref, w_ref, h_ref, *rest):
        outs, pj_s = rest[:-1], rest[-1]
        g = g_ref[...]

        def norm(rows):
            h_ref[rows, :] = (_rms(x_ref[rows, :])[0] * g).astype(MM)

        _by_chunks(tm, norm)
        for d in range(N_DEV):
            pj_s[d] = _dot_nt(h_ref[...], w_ref[d])

        def join_and_cut(rows):
            proj = jnp.concatenate([pj_s[d, rows, :] for d in range(N_DEV)], axis=1)
            for (s, n), o_ref in zip(COL_SECTIONS, outs):
                if n == QK_ROPE:
                    o_ref[rows, :] = jnp.concatenate(
                        [jnp.zeros((ROW_CHUNK, QK_NOPE), F32), proj[:, s:s + n],
                         jnp.zeros((ROW_CHUNK, HEAD_PAD - QK_DIM), F32)], axis=1)
                else:
                    o_ref[rows, :] = proj[:, s:s + n]

        _by_chunks(tm, join_and_cut)

    row_outs = [(D_MODEL, MM)] + [(n, F32) for n in STORED_WIDTHS]
    return _row_call("in_proj_fwd", body, T, tm, [x], [g_mix, w_in], row_outs, [], VMEM_LIMIT,
                     scratch=[pltpu.VMEM((N_DEV, tm, IN_BLOCK), F32)])


def _mla_heads_fwd(raw, g_pad, cos_t, sin_t, first):
    outs, saved = [], []
    for h in range(MLA_HEADS):
        xh, r = _rms(raw[:, h * HEAD_PAD:(h + 1) * HEAD_PAD], QK_DIM)
        y = xh * g_pad
        outs.append(y * cos_t + _rope_swap(y, first) * sin_t)
        saved.append((xh, r))
    return outs, saved


def _mla_raw_heads(cqn, ckvn, kr, wuq_ref, wukv_ref, tm):
    lane = lax.broadcasted_iota(jnp.int32, (tm, HEAD_PAD), 1)
    nope = lane < QK_NOPE
    one_lane = jnp.where(lane == V_DIM, 1.0, 0.0)
    qs, ks, vs = [], [], []
    for h in range(MLA_HEADS):
        qs.append(_dot_nt(cqn, wuq_ref[h]))
        kv = _dot(ckvn, wukv_ref[h])
        ks.append(jnp.where(nope, kv, kr))
        vs.append(jnp.where(nope, pltpu.roll(kv, V_DIM, 1), one_lane))
    return jnp.concatenate(qs, axis=1), jnp.concatenate(ks, axis=1), jnp.concatenate(vs, axis=1)


def _mla_prep_fwd(cq, ckv, kr, pos, g_qa, g_kva, g_qn, g_kn, w_uq, w_ukv, T, tm):
    def body(i, cq_ref, ckv_ref, kr_ref, pos_ref, gqa_ref, gkva_ref, gqn_ref, gkn_ref, wuq_ref, wukv_ref,
             q_ref, k_ref, v_ref):
        cos_t, sin_t, first = _rope_tables(pos_ref[...], tm)
        cqn = _rms(cq_ref[...])[0] * gqa_ref[...]
        ckvn = _rms(ckv_ref[...])[0] * gkva_ref[...]
        q_raw, k_raw, v = _mla_raw_heads(cqn, ckvn, kr_ref[...], wuq_ref, wukv_ref, tm)
        qs, _ = _mla_heads_fwd(q_raw, gqn_ref[...], cos_t, sin_t, first)
        ks, _ = _mla_heads_fwd(k_raw, gkn_ref[...], cos_t, sin_t, first)
        q_ref[...] = (jnp.concatenate(qs, axis=1) * ATT_SCALE).astype(MM)
        k_ref[...] = jnp.concatenate(ks, axis=1).astype(MM)
        v_ref[...] = v.astype(MM)

    w = MLA_HEADS * HEAD_PAD
    return _row_call("mla_prep_fwd", body, T, tm, [cq, ckv, kr, pos], [g_qa, g_kva, g_qn, g_kn, w_uq, w_ukv],
                     [(w, MM), (w, MM), (w, MM)], [])


def _causal_pairs(n, by_query):
    if by_query:
        pairs = [(q, k) for q in range(n) for k in range(q + 1)]
    else:
        pairs = [(q, k) for k in range(n) for q in range(k, n)]
    return np.array([p[0] for p in pairs], np.int32), np.array([p[1] for p in pairs], np.int32)


def _flash_fwd(qf, kf, vf, T, ag_blocks=()):
    tq = min(ATT_TILE, T)
    nq = T // tq

    qi_tab, ki_tab = _causal_pairs(nq, by_query=True)

    hp = ATT_HEADS

    n_ag = len(ag_blocks)
    n_heads, n_pairs = MLA_HEADS // hp, len(qi_tab)

    def body(qi_ref, ki_ref, q_ref, k_ref, v_ref, *rest):
        ag_in, (o_ref, lse_ref), rest = rest[:n_ag], rest[n_ag:n_ag + 2], rest[n_ag + 2:]
        ag_out, (m_s, acc_s), ag_sems = rest[:n_ag], rest[n_ag:n_ag + 2], rest[n_ag + 2:]
        t = pl.program_id(1)
        qi, ki = qi_ref[t], ki_ref[t]
        if n_ag:
            @pl.when((pl.program_id(0) == 0) & (t == 0))
            def _():
                _ag_start(ag_in, ag_out, ag_sems)

        @pl.when(ki == 0)
        def _():
            m_s[...] = jnp.full_like(m_s, NEG)
            acc_s[...] = jnp.zeros_like(acc_s)

        def step(masked):
            halves = 2 if masked and tq % (2 * HEAD_PAD) == 0 else 1
            w = tq // halves
            for hh in range(hp):
                hs = slice(hh * HEAD_PAD, (hh + 1) * HEAD_PAD)
                for part in range(halves):
                    cols, nk = slice(part * w, (part + 1) * w), (part + 1) * w
                    s_t = _dot_nt(k_ref[:nk, hs], q_ref[cols, hs])
                    if masked:
                        key = lax.broadcasted_iota(jnp.int32, (nk, w), 0)
                        qry = lax.broadcasted_iota(jnp.int32, (nk, w), 1) + part * w
                        s_t = jnp.where(key <= qry, s_t, NEG)
                    m_old = m_s[hh, :, cols]
                    m_new = jnp.maximum(m_old, jnp.max(s_t, axis=0, keepdims=True))
                    p_t = jnp.exp(s_t - m_new)
                    acc_s[hh, :, cols] = jnp.exp(m_old - m_new) * acc_s[hh, :, cols] + _dot_tn(v_ref[:nk, hs], p_t)
                    m_s[hh, :, cols] = m_new

        @pl.when(ki < qi)
        def _():
            step(False)

        @pl.when(ki == qi)
        def _():
            step(True)
            real = lax.broadcasted_iota(jnp.int32, (HEAD_PAD, tq), 0) < V_DIM
            for hh in range(hp):
                hs = slice(hh * HEAD_PAD, (hh + 1) * HEAD_PAD)
                acc = acc_s[hh]
                l = acc[V_DIM:V_DIM + 1]
                o_ref[:, hs] = jnp.where(real, acc / l, 0.0).T
                lse_ref[:, hs] = jnp.broadcast_to(m_s[hh] + jnp.log(l), (HEAD_PAD, tq)).T

        if n_ag:
            @pl.when((pl.program_id(0) == n_heads - 1) & (t == n_pairs - 1))
            def _():
                _ag_finish(ag_in, ag_out, ag_sems)

    q_spec = pl.BlockSpec((tq, hp * HEAD_PAD), lambda h, t, qi_ref, ki_ref: (qi_ref[t], h))
    kv_spec = pl.BlockSpec((tq, hp * HEAD_PAD), lambda h, t, qi_ref, ki_ref: (ki_ref[t], h))
    any_spec = pl.BlockSpec(memory_space=pl.ANY)
    grid_spec = pltpu.PrefetchScalarGridSpec(
        num_scalar_prefetch=2, grid=(n_heads, n_pairs),
        in_specs=[q_spec, kv_spec, kv_spec] + [any_spec] * n_ag, out_specs=[q_spec, q_spec] + [any_spec] * n_ag,
        scratch_shapes=[pltpu.VMEM((hp, 1, tq), F32), pltpu.VMEM((hp, HEAD_PAD, tq), F32)]
        + (_ag_sems(n_ag) if n_ag else []))
    return pl.pallas_call(
        body, name="flash_fwd", grid_spec=grid_spec,
        out_shape=[jax.ShapeDtypeStruct((T, MLA_HEADS * HEAD_PAD), F32)] * 2 + _ag_out_shapes(ag_blocks),
        compiler_params=_cparams(("arbitrary", "arbitrary")),
    )(jnp.asarray(qi_tab), jnp.asarray(ki_tab), qf, kf, vf, *ag_blocks)


def _hg_gates(hf, lb):
    sg = _sigmoid(hf)
    f = lb + (1.0 - lb) * sg
    return sg, f, jnp.log(f), 1.0 - f


def _prefix_sum(x, reverse=False):
    n = x.shape[0]
    row = lax.broadcasted_iota(jnp.int32, x.shape, 0)
    step = 1
    while step < n:
        if reverse:
            x = x + jnp.where(row < n - step, pltpu.roll(x, n - step, 0), 0.0)
        else:
            x = x + jnp.where(row >= step, pltpu.roll(x, step, 0), 0.0)
        step *= 2
    return x


def _hg_levels():
    C = HG_CHUNK
    t = lax.broadcasted_iota(jnp.int32, (C, C), 0)
    s = lax.broadcasted_iota(jnp.int32, (C, C), 1)
    levels = []
    for shift in range(C.bit_length() - 2, -1, -1):
        pair_t, pair_s = lax.shift_right_logical(t, shift + 1), lax.shift_right_logical(s, shift + 1)
        later_t = (lax.shift_right_logical(t, shift) & 1) == 1
        earlier_s = (lax.shift_right_logical(s, shift) & 1) == 0
        levels.append((1 << shift, (pair_t == pair_s) & later_t & earlier_s))
    return levels, t == s


def _hg_refs(b):
    C, n = b.shape
    row = lax.broadcasted_iota(jnp.int32, (C, n), 0)
    back1, back2, ahead1 = pltpu.roll(b, 1, 0), pltpu.roll(b, 2, 0), pltpu.roll(b, C - 1, 0)
    refs = []
    for half in (32, 16, 8, 4):
        refs.append(jnp.concatenate(
            [jnp.broadcast_to(b[lo + half - 1:lo + half], (2 * half, n)) for lo in range(0, C, 2 * half)], axis=0))
    in4 = row & 3
    refs.append(jnp.where(in4 == 0, ahead1, jnp.where(in4 == 1, b, jnp.where(in4 == 2, back1, back2))))
    refs.append(jnp.where((row & 1) == 1, back1, b))
    return refs


def _hg_intra(q, k, b, refs, levels, eye):
    a = jnp.where(eye, jnp.sum(q * k, axis=1, keepdims=True), 0.0)
    saved = []
    for r, (_, mask) in zip(refs, levels):
        e = jnp.exp(-jnp.abs(b - r))
        q_t, k_t = q * e, k * e
        a = a + jnp.where(mask, _dot_nt(q_t, k_t), 0.0)
        saved.append((q_t, k_t, e))
    return a, saved


def _hg_intra_bwd(d_a, q, k, saved, levels, eye):
    diag = jnp.sum(jnp.where(eye, d_a, 0.0), axis=1, keepdims=True)
    dq, dk = diag * k, diag * q
    for (q_t, k_t, e), (_, mask) in zip(saved, levels):
        da = jnp.where(mask, d_a, 0.0)
        dq = dq + _dot(da, k_t) * e
        dk = dk + _dot_tn(da, q_t) * e
    return dq, dk


def _hgrn_fwd(hq, hf, hi, lb, T):
    rb = min(HG_BLOCK, T)
    ncb = rb // HG_CHUNK

    def body(hq_ref, hf_ref, hi_ref, lb_ref, o_ref, s0_ref, st_ref):
        @pl.when(pl.program_id(0) == 0)
        def _():
            st_ref[...] = jnp.zeros_like(st_ref)

        levels, eye = _hg_levels()

        def chunk(c, carry):
            rows = pl.ds(pl.multiple_of(c * HG_CHUNK, HG_CHUNK), HG_CHUNK)
            _, _, logf, kk = _hg_gates(hf_ref[rows, :], lb_ref[...])
            b = _prefix_sum(logf)
            refs = _hg_refs(b)
            q_all, v_all = hq_ref[rows, :], hi_ref[rows, :]
            outs = []
            for h in range(HG_HEADS):
                ls = slice(h * HG_DIM, (h + 1) * HG_DIM)
                q, k, v, bh = q_all[:, ls], kk[:, ls], v_all[:, ls], b[:, ls]
                st = st_ref[h]
                s0_ref[c, h * HG_DIM:(h + 1) * HG_DIM, :] = st
                b_end = bh[HG_CHUNK - 1:HG_CHUNK]
                a, _ = _hg_intra(q, k, bh, [r[:, ls] for r in refs], levels, eye)
                outs.append(_dot_nt(q * jnp.exp(bh), st) + _dot(a, v))
                st_ref[h] = st * jnp.exp(b_end) + _dot_tn(v, k * jnp.exp(b_end - bh))
            o_ref[rows, :] = jnp.concatenate(outs, axis=1)
            return carry

        lax.fori_loop(0, ncb, chunk, 0, unroll=HG_UNROLL)

    row = pl.BlockSpec((rb, HG_W), lambda i: (i, 0))
    return pl.pallas_call(
        body, name="hgrn_fwd", grid=(T // rb,),
        in_specs=[row, row, row, pl.BlockSpec((1, HG_W), lambda i: (0, 0))],
        out_specs=[row, pl.BlockSpec((ncb, HG_W, HG_DIM), lambda i: (i, 0, 0))],
        out_shape=[jax.ShapeDtypeStruct((T, HG_W), F32), jax.ShapeDtypeStruct((T // HG_CHUNK, HG_W, HG_DIM), F32)],
        scratch_shapes=[pltpu.VMEM((HG_HEADS, HG_DIM, HG_DIM), F32)],
        compiler_params=_cparams(("arbitrary",)),
    )(hq, hf, hi, lb)


def _hgrn_bwd(hq, hf, hi, do, s0, lb, T, xchg=((), ())):
    rb = min(HG_BLOCK, T)
    ncb = rb // HG_CHUNK
    nb = T // rb
    C = HG_CHUNK
    n_x, n_sib = _x_count(xchg), len(xchg[0])

    def body(hq_ref, hf_ref, hi_ref, do_ref, s0_ref, lb_ref, *rest):
        x_in, (dq_ref, df_ref, dv_ref, dlb_ref), rest = rest[:n_x], rest[n_x:n_x + 4], rest[n_x + 4:]
        x_out, dst_ref, x_sems = rest[:n_x], rest[n_x], rest[n_x + 1:]

        @pl.when(pl.program_id(0) == 0)
        def _():
            dst_ref[...] = jnp.zeros_like(dst_ref)
            dlb_ref[...] = jnp.zeros_like(dlb_ref)
            for cp in _x_copies(n_sib, x_in, x_out, x_sems):
                cp.start()

        row_cc = lax.broadcasted_iota(jnp.int32, (C, C), 0)
        col_cc = lax.broadcasted_iota(jnp.int32, (C, C), 1)
        last_row = lax.broadcasted_iota(jnp.int32, (C, HG_DIM), 0) == C - 1
        lb_v = lb_ref[...]
        levels, eye = _hg_levels()

        def chunk(cc, carry):
            c = ncb - 1 - cc
            rows = pl.ds(pl.multiple_of(c * C, C), C)
            hf_c = hf_ref[rows, :]
            sg, f, logf, kk = _hg_gates(hf_c, lb_v)
            b = _prefix_sum(logf)
            refs = _hg_refs(b)
            q_all, v_all, do_all = hq_ref[rows, :], hi_ref[rows, :], do_ref[rows, :]
            dq_o, dk_o, dv_o, db_o = [], [], [], []
            for h in range(HG_HEADS):
                ls = slice(h * HG_DIM, (h + 1) * HG_DIM)
                q, k, v, bh, d_o = q_all[:, ls], kk[:, ls], v_all[:, ls], b[:, ls], do_all[:, ls]
                st0 = s0_ref[c, h * HG_DIM:(h + 1) * HG_DIM, :]
                dst = dst_ref[h]
                b_end = bh[C - 1:C]
                e_b, e_end = jnp.exp(bh), jnp.exp(b_end)
                e_rem = jnp.exp(b_end - bh)
                qe, kd = q * e_b, k * e_rem
                st_end = st0 * e_end + _dot_tn(v, kd)
                a, saved = _hg_intra(q, k, bh, [r[:, ls] for r in refs], levels, eye)
                d_a = jnp.where(col_cc <= row_cc, _dot_nt(d_o, v), 0.0)
                dq_i, dk_i = _hg_intra_bwd(d_a, q, k, saved, levels, eye)
                dv = _dot_tn(a, d_o) + _dot_nt(kd, dst)
                dq = e_b * _dot(d_o, st0) + dq_i
                dk = e_rem * _dot(v, dst) + dk_i
                extra = jnp.sum(dst * st_end, axis=0, keepdims=True)
                db_o.append(q * dq - k * dk + jnp.where(last_row, extra, 0.0))
                dst_ref[h] = dst * e_end + _dot_tn(d_o, qe)
                dq_o.append(dq)
                dk_o.append(dk)
                dv_o.append(dv)
            dlogf = _prefix_sum(jnp.concatenate(db_o, axis=1), reverse=True)
            d_f = dlogf / f - jnp.concatenate(dk_o, axis=1)
            dq_ref[rows, :] = jnp.concatenate(dq_o, axis=1).astype(MM)
            dv_ref[rows, :] = jnp.concatenate(dv_o, axis=1).astype(MM)
            df_ref[rows, :] = (d_f * (1.0 - lb_v) * sg * (1.0 - sg)).astype(MM)
            dlb_ref[...] += jnp.sum(d_f * (1.0 - sg), axis=0, keepdims=True)
            return carry

        lax.fori_loop(0, ncb, chunk, 0, unroll=HG_UNROLL)

        if n_x:
            @pl.when(pl.program_id(0) == nb - 1)
            def _():
                for cp in _x_copies(n_sib, x_in, x_out, x_sems):
                    cp.wait()

    row = pl.BlockSpec((rb, HG_W), lambda i: (nb - 1 - i, 0))
    one = pl.BlockSpec((1, HG_W), lambda i: (0, 0))
    any_spec = pl.BlockSpec(memory_space=pl.ANY)
    return pl.pallas_call(
        body, name="hgrn_bwd", grid=(nb,),
        in_specs=[row, row, row, row, pl.BlockSpec((ncb, HG_W, HG_DIM), lambda i: (nb - 1 - i, 0, 0)), one]
        + [any_spec] * n_x,
        out_specs=[row, row, row, one] + [any_spec] * n_x,
        out_shape=[jax.ShapeDtypeStruct((T, HG_W), MM)] * 3 + [jax.ShapeDtypeStruct((1, HG_W), F32)]
        + _x_out_shapes(xchg),
        scratch_shapes=[pltpu.VMEM((HG_HEADS, HG_DIM, HG_DIM), F32)] + _x_sems(xchg),
        compiler_params=_cparams(("arbitrary",)),
    )(hq, hf, hi, do, s0, lb, *xchg[0], *xchg[1])


def _silu_parts(x):
    sg = _sigmoid(x)
    return x * sg, sg * (1.0 + x * (1.0 - sg))


def _merge_fwd(attn, o, hg, bg, x, g_out, w_bra, w_brb, w_out, T, tm):
    def body(i, attn_ref, o_ref, hg_ref, bg_ref, x_ref, g_ref, wa_ref, wb_ref, wo_ref,
             x1_ref, ya_ref, yb_ref, m_ref, rec_ref):
        g = g_ref[...]

        def recurrent_out(rows):
            for h in range(HG_HEADS):
                ls = slice(h * HG_DIM, (h + 1) * HG_DIM)
                rec_ref[rows, ls] = (_rms(o_ref[rows, ls])[0] * g * _silu_parts(hg_ref[rows, ls])[0]).astype(MM)

        _by_chunks(tm, recurrent_out)
        ya_ref[...] = _dot(attn_ref[...], wa_ref[...])
        yb_ref[...] = jnp.dot(rec_ref[...], wb_ref[...], preferred_element_type=F32)

        def gate(rows):
            m_ref[rows, :] = (_sigmoid(bg_ref[rows, :D_MODEL]) * ya_ref[rows, :]
                              + _sigmoid(bg_ref[rows, D_MODEL:]) * yb_ref[rows, :]).astype(MM)

        _by_chunks(tm, gate)
        x1_ref[...] = x_ref[...] + jnp.dot(m_ref[...], wo_ref[...], preferred_element_type=F32)

    return _row_call("merge_fwd", body, T, tm, [attn, o, hg, bg, x], [g_out, w_bra, w_brb, w_out],
                     [(D_MODEL, F32), (D_MODEL, F32), (D_MODEL, F32), (D_MODEL, MM), (HG_W, MM)], [], VMEM_LIMIT)


def _ffn_fwd(x1, g_ffn, w_g, w_u, w_d, T, tm):
    def body(i, x1_ref, g_ref, wg_ref, wu_ref, wd_ref, x2_ref, gt_ref, up_ref, h2_ref, a_s):
        g = g_ref[...]

        def norm(rows):
            h2_ref[rows, :] = (_rms(x1_ref[rows, :])[0] * g).astype(MM)

        _by_chunks(tm, norm)
        gt_ref[...] = _dot_nt(h2_ref[...], wg_ref[...])
        up_ref[...] = _dot_nt(h2_ref[...], wu_ref[...])

        def act(rows):
            for cs in FFN_HALVES:
                a_s[rows, cs] = (_silu_parts(gt_ref[rows, cs])[0] * up_ref[rows, cs]).astype(MM)

        _by_chunks(tm, act)
        x2_ref[...] = x1_ref[...] + jnp.dot(a_s[...], wd_ref[...], preferred_element_type=F32)

    return _row_call("ffn_fwd", body, T, tm, [x1], [g_ffn, w_g, w_u, w_d],
                     [(D_MODEL, F32), (FFN, F32), (FFN, F32), (D_MODEL, MM)], [], VMEM_LIMIT,
                     scratch=[pltpu.VMEM((tm, FFN), MM)])


def _ple_loss(x2, p, tgt, g_pg, g_post, w_pg, w_pp, T, tm):
    def body(i, x2_ref, p_ref, t_ref, gpg_ref, gpo_ref, wpg_ref, wpp_ref,
             dx2_ref, loss_ref, dgpo_ref, dgpg_ref, dwpg_ref, dwpp_ref, u_s, n3_s, z_s, dz_s, du_s, dy_s, dn3_s):
        @pl.when(i == 0)
        def _():
            for ref in (loss_ref, dgpo_ref, dgpg_ref, dwpg_ref, dwpp_ref):
                ref[...] = jnp.zeros_like(ref)

        gpg, gpo = gpg_ref[...], gpo_ref[...]
        p_mm = p_ref[...].astype(MM)
        for d in range(N_DEV):
            u_s[:, d * HEAD_PAD:(d + 1) * HEAD_PAD] = jnp.dot(p_mm, wpp_ref[d], preferred_element_type=F32)

        def gate_input(rows):
            n3_s[rows, :] = (_rms(x2_ref[rows, :])[0] * gpg).astype(MM)

        _by_chunks(tm, gate_input)
        z_s[...] = jnp.dot(n3_s[...], wpg_ref[...], preferred_element_type=F32)

        def loss_and_back(rows):
            uh, ru = _rms(u_s[rows, :])
            e = uh * gpo
            gate = _sigmoid(z_s[rows, :])
            diff = x2_ref[rows, :] + gate * e - t_ref[rows, :]
            dy = diff * (1.0 / D_MODEL)
            de = dy * gate
            dz_s[rows, :] = (dy * e * gate * (1.0 - gate)).astype(MM)
            du_s[rows, :] = _rms_bwd(de * gpo, uh, ru).astype(MM)
            dy_s[rows, :] = dy
            loss_ref[...] += _fold8(diff * diff) * (0.5 / D_MODEL)
            dgpo_ref[...] += _fold8(de * uh)

        _by_chunks(tm, loss_and_back)
        dn3_s[...] = _dot_nt(dz_s[...], wpg_ref[...])

        def gate_norm_back(rows):
            x2h, r3 = _rms(x2_ref[rows, :])
            dn3 = dn3_s[rows, :]
            dx2_ref[rows, :] = dy_s[rows, :] + _rms_bwd(dn3 * gpg, x2h, r3)
            dgpg_ref[...] += _fold8(dn3 * x2h)

        _by_chunks(tm, gate_norm_back)
        dwpg_ref[...] += _dot_tn(n3_s[...], dz_s[...])
        for d in range(N_DEV):
            dwpp_ref[d] += _dot_tn(p_mm, du_s[:, d * HEAD_PAD:(d + 1) * HEAD_PAD])

    vec = ((8, D_MODEL), F32)
    wide = lambda dt: pltpu.VMEM((tm, D_MODEL), dt)
    return _row_call("ple_loss", body, T, tm, [x2, p, tgt], [g_pg, g_post, w_pg, w_pp], [(D_MODEL, F32)],
                     [vec, vec, vec, ((D_MODEL, D_MODEL), F32), ((N_DEV, PLE, HEAD_PAD), F32)], VMEM_LIMIT,
                     scratch=[wide(F32), wide(MM), wide(F32), wide(MM), wide(MM), wide(F32), wide(F32)])


def _ffn_bwd(dx2, x1, gt, up, g_ffn, w_g, w_u, w_d, T, tm):
    def body(i, dx2_ref, x1_ref, gt_ref, up_ref, g_ref, wg_ref, wu_ref, wd_ref,
             dx1_ref, a_ref, dgt_ref, dup_ref, dg_ref, da_s, dh2_s):
        @pl.when(i == 0)
        def _():
            dg_ref[...] = jnp.zeros_like(dg_ref)

        g = g_ref[...]
        da_s[...] = _dot_nt(dx2_ref[...], wd_ref[...])

        def act_back(rows):
            for cs in FFN_HALVES:
                up, da = up_ref[rows, cs], da_s[rows, cs]
                silu, dsilu = _silu_parts(gt_ref[rows, cs])
                dgt_ref[rows, cs] = (da * up * dsilu).astype(MM)
                dup_ref[rows, cs] = (da * silu).astype(MM)
                a_ref[rows, cs] = (silu * up).astype(MM)

        _by_chunks(tm, act_back)
        dh2_s[...] = (jnp.dot(dgt_ref[...], wg_ref[...], preferred_element_type=F32)
                      + jnp.dot(dup_ref[...], wu_ref[...], preferred_element_type=F32))

        def norm_back(rows):
            x1h, r = _rms(x1_ref[rows, :])
            dh2 = dh2_s[rows, :]
            dx1_ref[rows, :] = dx2_ref[rows, :] + _rms_bwd(dh2 * g, x1h, r)
            dg_ref[...] += _fold8(dh2 * x1h)

        _by_chunks(tm, norm_back)

    return _row_call("ffn_bwd", body, T, tm, [dx2, x1, gt, up], [g_ffn, w_g, w_u, w_d],
                     [(D_MODEL, F32), (FFN, MM), (FFN, MM), (FFN, MM)], [((8, D_MODEL), F32)], VMEM_LIMIT,
                     scratch=[pltpu.VMEM((tm, FFN), F32), pltpu.VMEM((tm, D_MODEL), F32)])


def _merge_bwd(dx1, ya, yb, bg, o, hg, attn, m, rec, g_out, w_bra, w_brb, w_out, T, tm, xchg=((), ())):
    def body(i, dx1_ref, ya_ref, yb_ref, bg_ref, o_ref, hg_ref, attn_ref, m_ref, rec_ref, g_ref, wa_ref, wb_ref, wo_ref,
             dattn_ref, do_ref, dhg_ref, dbg_ref, dg_ref, dwo_ref, dwa_ref, dwb_ref, dm_s, dya_s, dyb_s, drec_s):
        @pl.when(i == 0)
        def _():
            for ref in (dg_ref, dwo_ref, dwa_ref, dwb_ref):
                ref[...] = jnp.zeros_like(ref)

        g = g_ref[...]
        dx1 = dx1_ref[...].astype(MM)
        dm_s[...] = _dot_nt(dx1, wo_ref[...])

        def gate_back(rows):
            dm = dm_s[rows, :]
            ga, gb = _sigmoid(bg_ref[rows, :D_MODEL]), _sigmoid(bg_ref[rows, D_MODEL:])
            dya_s[rows, :] = (dm * ga).astype(MM)
            dyb_s[rows, :] = (dm * gb).astype(MM)
            dbg_ref[rows, :D_MODEL] = (dm * ya_ref[rows, :] * ga * (1.0 - ga)).astype(MM)
            dbg_ref[rows, D_MODEL:] = (dm * yb_ref[rows, :] * gb * (1.0 - gb)).astype(MM)

        _by_chunks(tm, gate_back)
        dwo_ref[...] += _dot_tn(m_ref[...], dx1)
        attn_mm = attn_ref[...].astype(MM)
        for d in range(N_DEV):
            ds = slice(d * HEAD_PAD, (d + 1) * HEAD_PAD)
            dwa_ref[d] += _dot_tn(attn_mm, dya_s[:, ds])
            dwb_ref[d] += _dot_tn(rec_ref[...], dyb_s[:, ds])
        dattn_ref[...] = _dot_nt(dya_s[...], wa_ref[...])
        drec_s[...] = _dot_nt(dyb_s[...], wb_ref[...])

        def recurrent_out_back(rows):
            for h in range(HG_HEADS):
                ls = slice(h * HG_DIM, (h + 1) * HG_DIM)
                oh, r = _rms(o_ref[rows, ls])
                silu, dsilu = _silu_parts(hg_ref[rows, ls])
                dr = drec_s[rows, ls]
                dhg_ref[rows, ls] = (dr * oh * g * dsilu).astype(MM)
                don = dr * silu
                dg_ref[...] += _fold8(don * oh)
                do_ref[rows, ls] = _rms_bwd(don * g, oh, r)

        _by_chunks(tm, recurrent_out_back)

    wide = lambda n, dt: pltpu.VMEM((tm, n), dt)
    return _row_call("merge_bwd", body, T, tm, [dx1, ya, yb, bg, o, hg, attn, m, rec], [g_out, w_bra, w_brb, w_out],
                     [(D_MODEL, F32), (HG_W, F32), (HG_W, MM), (2 * D_MODEL, MM)],
                     [((8, HG_DIM), F32), ((D_MODEL, D_MODEL), F32), ((N_DEV, MLA_HEADS * HEAD_PAD, HEAD_PAD), F32),
                      ((N_DEV, HG_W, HEAD_PAD), F32)], VMEM_LIMIT,
                     scratch=[wide(D_MODEL, F32), wide(D_MODEL, MM), wide(D_MODEL, MM), wide(HG_W, F32)], xchg=xchg)


def _flash_bwd(qf, kf, vf, o, do, lse, T, xchg=((), ())):
    tq = min(ATT_TILE, T)
    nq = T // tq

    qi_tab, ki_tab = _causal_pairs(nq, by_query=False)

    n_x, n_sib = _x_count(xchg), len(xchg[0])
    hp = ATT_HEADS
    n_heads, n_pairs = MLA_HEADS // hp, len(qi_tab)

    def body(qi_ref, ki_ref, q_ref, k_ref, v_ref, o_ref, do_ref, lse_ref, *rest):
        x_in, (dq_ref, dk_ref, dv_ref), rest = rest[:n_x], rest[n_x:n_x + 3], rest[n_x + 3:]
        x_out, x_sems = rest[:n_x], rest[n_x:]
        t = pl.program_id(1)
        qi, ki = qi_ref[t], ki_ref[t]
        if n_x:
            @pl.when((pl.program_id(0) == 0) & (t == 0))
            def _():
                for cp in _x_copies(n_sib, x_in, x_out, x_sems):
                    cp.start()

        @pl.when(t == 0)
        def _():
            dq_ref[...] = jnp.zeros_like(dq_ref)

        def step(first):
            halves = 2 if first and tq % (2 * HEAD_PAD) == 0 else 1
            w = tq // halves
            for hh in range(hp):
                hs = slice(hh * HEAD_PAD, (hh + 1) * HEAD_PAD)
                for part in range(halves):
                    keys, qs = slice(part * w, (part + 1) * w), slice(part * w, tq)
                    nq_ = tq - part * w
                    q, k, d_o = q_ref[qs, hs], k_ref[keys, hs], do_ref[qs, hs]
                    s = _dot_nt(q, k)
                    if first:
                        row = lax.broadcasted_iota(jnp.int32, (nq_, w), 0)
                        col = lax.broadcasted_iota(jnp.int32, (nq_, w), 1)
                        s = jnp.where(col <= row, s, NEG)
                    p = jnp.exp(s - lse_ref[qs, hh * HEAD_PAD:hh * HEAD_PAD + 1])
                    delta = jnp.sum(d_o * o_ref[qs, hs], axis=1, keepdims=True)
                    ds = p * (_dot_nt(d_o, v_ref[keys, hs]) - delta)
                    rows = pl.ds(pl.multiple_of(qi * tq + part * w, w), nq_)
                    dq_ref[rows, hs] += _dot(ds, k)
                    if first:
                        dv_ref[keys, hs] = _dot_tn(p, d_o)
                        dk_ref[keys, hs] = _dot_tn(ds, q)
                    else:
                        dv_ref[keys, hs] += _dot_tn(p, d_o)
                        dk_ref[keys, hs] += _dot_tn(ds, q)

        @pl.when(qi == ki)
        def _():
            step(True)

        @pl.when(qi > ki)
        def _():
            step(False)

        if n_x:
            @pl.when((pl.program_id(0) == n_heads - 1) & (t == n_pairs - 1))
            def _():
                for cp in _x_copies(n_sib, x_in, x_out, x_sems):
                    cp.wait()

    q_spec = pl.BlockSpec((tq, hp * HEAD_PAD), lambda h, t, qi_ref, ki_ref: (qi_ref[t], h))
    kv_spec = pl.BlockSpec((tq, hp * HEAD_PAD), lambda h, t, qi_ref, ki_ref: (ki_ref[t], h))
    any_spec = pl.BlockSpec(memory_space=pl.ANY)
    w = MLA_HEADS * HEAD_PAD
    grid_spec = pltpu.PrefetchScalarGridSpec(
        num_scalar_prefetch=2, grid=(n_heads, n_pairs),
        in_specs=[q_spec, kv_spec, kv_spec, q_spec, q_spec, q_spec] + [any_spec] * n_x,
        out_specs=[pl.BlockSpec((T, hp * HEAD_PAD), lambda h, t, qi_ref, ki_ref: (0, h)), kv_spec, kv_spec]
        + [any_spec] * n_x,
        scratch_shapes=_x_sems(xchg))
    return pl.pallas_call(
        body, name="flash_bwd", grid_spec=grid_spec,
        out_shape=[jax.ShapeDtypeStruct((T, w), F32)] * 3 + _x_out_shapes(xchg),
        compiler_params=_cparams(("arbitrary", "arbitrary")),
    )(jnp.asarray(qi_tab), jnp.asarray(ki_tab), qf, kf, vf, o, do, lse, *xchg[0], *xchg[1])


def _mla_heads_bwd(d_out, saved, g_pad, cos_t, sin_t, first):
    d_raw, dg = [], jnp.zeros((1, HEAD_PAD), F32)
    for h in range(MLA_HEADS):
        xh, r = saved[h]
        dy = d_out[:, h * HEAD_PAD:(h + 1) * HEAD_PAD]
        dn = dy * cos_t + _rope_swap(dy * sin_t, first)
        dg = dg + jnp.sum(dn * xh, axis=0, keepdims=True)
        d_raw.append(_rms_bwd(dn * g_pad, xh, r, QK_DIM))
    return d_raw, dg


def _mla_prep_bwd(cq, ckv, kr, pos, dqf, dkf, dvf, g_qa, g_kva, g_qn, g_kn, w_uq, w_ukv, T, tm):
    def body(i, cq_ref, ckv_ref, kr_ref, pos_ref, dq_ref, dk_ref, dv_ref,
             gqa_ref, gkva_ref, gqn_ref, gkn_ref, wuq_ref, wukv_ref,
             dcq_ref, dckv_ref, dkr_ref, dgqa_ref, dgkva_ref, dgqn_ref, dgkn_ref, dwuq_ref, dwukv_ref):
        cos_t, sin_t, first = _rope_tables(pos_ref[...], tm)
        cqh, rq = _rms(cq_ref[...])
        ckvh, rkv = _rms(ckv_ref[...])
        cqn, ckvn = cqh * gqa_ref[...], ckvh * gkva_ref[...]
        q_raw, k_raw, _ = _mla_raw_heads(cqn, ckvn, kr_ref[...], wuq_ref, wukv_ref, tm)
        _, q_saved = _mla_heads_fwd(q_raw, gqn_ref[...], cos_t, sin_t, first)
        _, k_saved = _mla_heads_fwd(k_raw, gkn_ref[...], cos_t, sin_t, first)
        dq_heads, dgqn = _mla_heads_bwd(dq_ref[...] * ATT_SCALE, q_saved, gqn_ref[...], cos_t, sin_t, first)
        dk_heads, dgkn = _mla_heads_bwd(dk_ref[...], k_saved, gkn_ref[...], cos_t, sin_t, first)
        lane = lax.broadcasted_iota(jnp.int32, (tm, HEAD_PAD), 1)
        nope = lane < QK_NOPE
        dcqn = jnp.zeros((tm, Q_RANK), F32)
        dckvn = jnp.zeros((tm, KV_RANK), F32)
        dkr = jnp.zeros((tm, HEAD_PAD), F32)
        cqn_mm, ckvn_mm = cqn.astype(MM), ckvn.astype(MM)
        for h in range(MLA_HEADS):
            hs = slice(h * HEAD_PAD, (h + 1) * HEAD_PAD)
            dq_h = dq_heads[h].astype(MM)
            dkv_h = jnp.where(nope, dk_heads[h], pltpu.roll(dv_ref[:, hs], V_DIM, 1)).astype(MM)
            _acc(dwuq_ref.at[h], i, _dot_tn(dq_h, cqn_mm))
            _acc(dwukv_ref.at[h], i, _dot_tn(ckvn_mm, dkv_h))
            dcqn = dcqn + jnp.dot(dq_h, wuq_ref[h], preferred_element_type=F32)
            dckvn = dckvn + lax.dot_general(dkv_h, wukv_ref[h], (((1,), (1,)), ((), ())), preferred_element_type=F32)
            dkr = dkr + dk_heads[h]
        dkr_ref[...] = jnp.where((lane >= QK_NOPE) & (lane < QK_DIM), dkr, 0.0).astype(MM)
        dcq_ref[...] = _rms_bwd(dcqn * gqa_ref[...], cqh, rq).astype(MM)
        dckv_ref[...] = _rms_bwd(dckvn * gkva_ref[...], ckvh, rkv).astype(MM)
        _acc(dgqa_ref, i, jnp.sum(dcqn * cqh, axis=0, keepdims=True))
        _acc(dgkva_ref, i, jnp.sum(dckvn * ckvh, axis=0, keepdims=True))
        _acc(dgqn_ref, i, dgqn)
        _acc(dgkn_ref, i, dgkn)

    return _row_call(
        "mla_prep_bwd", body, T, tm, [cq, ckv, kr, pos, dqf, dkf, dvf], [g_qa, g_kva, g_qn, g_kn, w_uq, w_ukv],
        [(Q_RANK, MM), (KV_RANK, MM), (HEAD_PAD, MM)],
        [((1, Q_RANK), F32), ((1, KV_RANK), F32), ((1, HEAD_PAD), F32), ((1, HEAD_PAD), F32),
         ((MLA_HEADS, HEAD_PAD, Q_RANK), F32), ((MLA_HEADS, KV_RANK, HEAD_PAD), F32)], VMEM_LIMIT)


def _in_proj_bwd(x, dx1, dsecs, g_mix, w_in, T, tm):
    def body(i, x_ref, dx1_ref, *rest):
        d_refs, (g_ref, w_ref, dx_ref, dp_ref, dg_ref, dh_s) = rest[:len(COL_SECTIONS)], rest[len(COL_SECTIONS):]

        @pl.when(i == 0)
        def _():
            dg_ref[...] = jnp.zeros_like(dg_ref)

        g = g_ref[...]

        def join_and_cut(rows):
            pieces = [(d_ref[rows, QK_NOPE:QK_DIM] if n == QK_ROPE else d_ref[rows, :]).astype(F32)
                      for (_, n), d_ref in zip(COL_SECTIONS, d_refs)]
            dproj = jnp.concatenate(pieces, axis=1)
            for d in range(N_DEV):
                dp_ref[d, rows, :] = dproj[:, d * IN_BLOCK:(d + 1) * IN_BLOCK].astype(MM)

        _by_chunks(tm, join_and_cut)
        dh = jnp.dot(dp_ref[0], w_ref[0], preferred_element_type=F32)
        for d in range(1, N_DEV):
            dh = dh + jnp.dot(dp_ref[d], w_ref[d], preferred_element_type=F32)
        dh_s[...] = dh

        def norm_back(rows):
            xh, r = _rms(x_ref[rows, :])
            dh_c = dh_s[rows, :]
            dx_ref[rows, :] = dx1_ref[rows, :] + _rms_bwd(dh_c * g, xh, r)
            dg_ref[...] += _fold8(dh_c * xh)

        _by_chunks(tm, norm_back)

    in_specs = [pl.BlockSpec((tm, a.shape[1]), lambda i: (i, 0)) for a in [x, dx1, *dsecs]]
    in_specs += [pl.BlockSpec(g_mix.shape, lambda i: (0, 0)),
                 pl.BlockSpec(w_in.shape, lambda i: (0, 0, 0), pipeline_mode=pl.Buffered(1))]

    def kern(*refs):
        body(pl.program_id(0), *refs)

    return pl.pallas_call(
        kern, name="in_proj_bwd", grid=(T // tm,), in_specs=in_specs,
        out_specs=[pl.BlockSpec((tm, D_MODEL), lambda i: (i, 0)),
                   pl.BlockSpec((N_DEV, tm, IN_BLOCK), lambda i: (0, i, 0)),
                   pl.BlockSpec((8, D_MODEL), lambda i: (0, 0))],
        out_shape=[jax.ShapeDtypeStruct((T, D_MODEL), F32), jax.ShapeDtypeStruct((N_DEV, T, IN_BLOCK), MM),
                   jax.ShapeDtypeStruct((8, D_MODEL), F32)],
        scratch_shapes=[pltpu.VMEM((tm, D_MODEL), F32)],
        compiler_params=_cparams(("arbitrary",), VMEM_LIMIT),
    )(x, dx1, *dsecs, g_mix, w_in)


def _pick_block(n, cap):
    best = None
    for cand in range(128, min(n, cap) + 1, 128):
        if n % cand == 0:
            best = cand
    return n if best is None else best


def _matmul_tn(name, a, b):
    T, M = a.shape
    N = b.shape[1]
    bm, bk = _pick_block(M, 1408), min(DW_TOKENS, T)
    bn = _pick_block(N, 2560)

    def body(a_ref, b_ref, c_ref):
        @pl.when(pl.program_id(2) == 0)
        def _():
            c_ref[...] = jnp.zeros_like(c_ref)

        c_ref[...] += _dot_tn(a_ref[...], b_ref[...])

    return pl.pallas_call(
        body, name=name, grid=(M // bm, N // bn, T // bk),
        in_specs=[pl.BlockSpec((bk, bm), lambda i, j, k: (k, i)), pl.BlockSpec((bk, bn), lambda i, j, k: (k, j))],
        out_specs=pl.BlockSpec((bm, bn), lambda i, j, k: (i, j)), out_shape=jax.ShapeDtypeStruct((M, N), F32),
        compiler_params=_cparams(("parallel", "parallel", "arbitrary"), VMEM_LIMIT),
    )(a, b)


def _matmul_tn_blocks(name, a, b):
    T, M = a.shape
    nd, _, c = b.shape
    bm, bk = _pick_block(M, 512), min(DW_TOKENS, T)

    def body(a_ref, b_ref, c_ref):
        @pl.when(pl.program_id(1) == 0)
        def _():
            c_ref[...] = jnp.zeros_like(c_ref)

        a_blk = a_ref[...].astype(MM)
        for d in range(nd):
            c_ref[d] += _dot_tn(b_ref[d], a_blk)

    return pl.pallas_call(
        body, name=name, grid=(M // bm, T // bk),
        in_specs=[pl.BlockSpec((bk, bm), lambda i, k: (k, i)), pl.BlockSpec((nd, bk, c), lambda i, k: (0, k, 0))],
        out_specs=pl.BlockSpec((nd, c, bm), lambda i, k: (0, 0, i)),
        out_shape=jax.ShapeDtypeStruct((nd, c, M), F32),
        compiler_params=_cparams(("parallel", "arbitrary"), VMEM_LIMIT),
    )(a, b)


def _pad_gain(g, n):
    return jnp.pad(g.reshape(1, -1), ((0, 0), (0, n - g.shape[-1])))


GROUP_A = ("w_ffn_gate", "w_ffn_up", "w_ffn_down", "w_ple_gate", "w_ple_proj")
GROUP_B = ("w_branch", "w_out")
GROUP_C = ("w_in", "w_uq", "w_ukv")
EARLY = GROUP_C
LATE = GROUP_B + GROUP_A
TRANSPOSED = ("w_in", "w_uq", "w_ffn_gate", "w_ffn_up")


def _local_step(x, p, pos, tgt, small, big, late_blocks=None, core=None):
    T = x.shape[0]
    tm = min(ROW_TILE, T)
    tw = min(WIDE_TILE, T)
    w_in = big["w_in"]
    w_uq = jnp.pad(big["w_uq"], ((0, 0), (0, HEAD_PAD - QK_DIM), (0, 0)))
    w_ukv = big["w_ukv"]

    g_mix, g_qa, g_kva = small["mix_norm_g"], small["q_a_norm_g"], small["kv_a_norm_g"]
    g_qn, g_kn = _pad_gain(small["q_norm_g"], HEAD_PAD), _pad_gain(small["k_norm_g"], HEAD_PAD)
    g_out, g_ffn = small["hg_out_norm_g"], small["ffn_norm_g"]
    g_pg, g_post = small["ple_gate_norm_g"], small["ple_post_norm_g"]
    logits = small["hg_lb_logits"]
    lb = _lower_bound(logits)

    h, cq, ckv, kr, hq, hf, hi, hg, bg = _in_proj_fwd(x, g_mix, w_in, T, tw)
    qf, kf, vf = _mla_prep_fwd(cq, ckv, kr, pos, g_qa, g_kva, g_qn, g_kn, w_uq, w_ukv, T, tw)
    if late_blocks is None:
        attn, lse = _flash_fwd(qf, kf, vf, T)
    else:
        attn, lse, *late = _flash_fwd(qf, kf, vf, T, ag_blocks=[late_blocks[n] for n in LATE])
        big = {**big, **dict(zip(LATE, late))}
    o, s0 = _hgrn_fwd(hq, hf, hi, lb, T)
    w_branch = jnp.moveaxis(big["w_branch"].reshape(N_DEV, 2, HG_W, HEAD_PAD), 0, 2).reshape(2, HG_W, D_MODEL)
    w_bra = jnp.pad(w_branch[0].reshape(MLA_HEADS, V_DIM, D_MODEL),
                    ((0, 0), (0, HEAD_PAD - V_DIM), (0, 0))).reshape(MLA_HEADS * HEAD_PAD, D_MODEL)
    w_brb = w_branch[1]
    w_out = big["w_out"].reshape(D_MODEL, D_MODEL)
    w_g, w_u = big["w_ffn_gate"].reshape(FFN, D_MODEL), big["w_ffn_up"].reshape(FFN, D_MODEL)
    w_d = big["w_ffn_down"].reshape(FFN, D_MODEL)
    w_pg, w_pp = big["w_ple_gate"].reshape(D_MODEL, D_MODEL), big["w_ple_proj"]
    x1, ya, yb, m, rec = _merge_fwd(attn, o, hg, bg, x, g_out, w_bra, w_brb, w_out, T, tw)
    x2, gt, up, h2 = _ffn_fwd(x1, g_ffn, w_g, w_u, w_d, T, tw)
    dx2, loss_p, dg_post, dg_pg, d_pg, d_pp = _ple_loss(x2, p, tgt, g_pg, g_post, w_pg, w_pp, T, tw)

    grads, sibs, gots = {}, {}, {}
    dist = core is not None
    pick = lambda names: [grads[n] for n in names] if dist else ()

    def partials(tag, names, got):
        if not dist:
            return ()
        sibs.update(zip(names, got))
        return _chip_partials("rs_partial_" + tag, pick(names), got, core)

    dx1, a, dgt, dup, dg_ffn = _ffn_bwd(dx2, x1, gt, up, g_ffn, w_g, w_u, w_d, T, tm)
    grads["w_ffn_gate"] = _matmul_tn("dw_gate", dgt, h2).reshape(N_DEV, -1, D_MODEL)
    grads["w_ffn_up"] = _matmul_tn("dw_up", dup, h2).reshape(N_DEV, -1, D_MODEL)
    grads["w_ffn_down"] = _matmul_tn("dw_down", a, dx2).reshape(N_DEV, -1, D_MODEL)
    grads["w_ple_gate"] = d_pg.reshape(N_DEV, -1, D_MODEL)
    grads["w_ple_proj"] = d_pp

    dattn, do, dhg, dbg, dg_out, d_out, d_bra, d_brb, *sib_a = _merge_bwd(
        dx1, ya, yb, bg, o, hg, attn, m, rec, g_out, w_bra, w_brb, w_out, T, tm, xchg=(pick(GROUP_A), ()))
    parts_a = partials("a", GROUP_A, sib_a)
    d_bra = d_bra.reshape(N_DEV, MLA_HEADS, HEAD_PAD, HEAD_PAD)[:, :, :V_DIM].reshape(N_DEV, HG_W, HEAD_PAD)
    grads["w_branch"] = jnp.concatenate([d_bra, d_brb], axis=1)
    grads["w_out"] = d_out.reshape(N_DEV, -1, D_MODEL)

    dhq, dhf, dhi, dlb, *got = _hgrn_bwd(hq, hf, hi, do, s0, lb, T, xchg=(pick(GROUP_B), parts_a))
    sib_b, got_a = got[:len(GROUP_B)], got[len(GROUP_B):]
    parts_b = partials("b", GROUP_B, sib_b)
    dqf, dkf, dvf, *got_b = _flash_bwd(qf, kf, vf, attn, dattn, lse, T, xchg=((), parts_b))
    (dcq, dckv, dkr, dg_qa, dg_kva, dg_qn, dg_kn, d_uq, d_ukv) = _mla_prep_bwd(
        cq, ckv, kr, pos, dqf, dkf, dvf, g_qa, g_kva, g_qn, g_kn, w_uq, w_ukv, T, tw)
    grad_x, dproj, dg_mix = _in_proj_bwd(x, dx1, [dcq, dckv, dkr, dhq, dhf, dhi, dhg, dbg], g_mix, w_in, T, tw)
    grads["w_in"] = _matmul_tn_blocks("dw_in", h, dproj)
    grads["w_uq"] = d_uq[:, :QK_DIM]
    grads["w_ukv"] = d_ukv
    if dist:
        gots.update(zip(GROUP_A, got_a))
        gots.update(zip(GROUP_B, got_b))

    dl0 = dlb * lb * (1.0 - lb)
    small_g = {
        "mix_norm_g": dg_mix, "q_a_norm_g": dg_qa, "kv_a_norm_g": dg_kva, "q_norm_g": dg_qn, "k_norm_g": dg_kn,
        "hg_lb_logits": jnp.concatenate([dl0, -dl0], axis=0), "hg_out_norm_g": dg_out,
        "ffn_norm_g": dg_ffn, "ple_gate_norm_g": dg_pg, "ple_post_norm_g": dg_post,
    }
    return loss_p, grad_x, small_g, grads, sibs, gots


def _lower_bound(logits):
    def body(l_ref, lb_ref):
        l = l_ref[...]
        mx = jnp.max(l, axis=0, keepdims=True)
        e = jnp.exp(l - mx)
        lb_ref[...] = e[0:1] / jnp.sum(e, axis=0, keepdims=True)

    return pl.pallas_call(body, name="lower_bound", out_shape=jax.ShapeDtypeStruct((1, HG_W), F32))(logits)


def _my_place():
    return lax.axis_index("x"), lax.axis_index("y"), lax.axis_index("c")


def _all_gather(name, blocks):
    n = len(blocks)

    def body(*refs):
        x_refs, out_refs, sems = refs[:n], refs[n:2 * n], refs[2 * n:]
        _ag_start(x_refs, out_refs, sems)
        _ag_finish(x_refs, out_refs, sems)

    any_spec = pl.BlockSpec(memory_space=pl.ANY)
    return pl.pallas_call(
        body, name=name, out_shape=_ag_out_shapes(blocks),
        in_specs=[any_spec] * n, out_specs=[any_spec] * n, scratch_shapes=_ag_sems(n),
    )(*blocks)


def _ag_out_shapes(blocks):
    return [jax.ShapeDtypeStruct((N_DEV,) + b.shape, b.dtype) for b in blocks]


def _ag_sems(n):
    return [pltpu.SemaphoreType.DMA((7 * n,)), pltpu.SemaphoreType.DMA((7 * n,)), pltpu.SemaphoreType.DMA((n,))]


def _ag_parts(x_refs, out_refs, sems):
    send_sems, recv_sems, local_sems = sems
    x, y, c = _my_place()
    me, sibling = (x, y, c), (x, y, 1 - c)
    chips = [(1 - x, y), (x, 1 - y), (1 - x, 1 - y)]
    n = len(x_refs)

    def copy(a, k, block, to, own=False):
        px, py, pc = block
        dst = out_refs[a].at[4 * px + 2 * py + pc]
        return pltpu.make_async_remote_copy(
            src_ref=x_refs[a] if own else dst, dst_ref=dst, send_sem=send_sems.at[7 * a + k],
            recv_sem=recv_sems.at[7 * a + k], device_id=to, device_id_type=MESH_ID)

    mine = [pltpu.make_async_copy(x_refs[a], out_refs[a].at[4 * x + 2 * y + c], local_sems.at[a]) for a in range(n)]
    first = []
    for a in range(n):
        first.append(copy(a, 0, me, sibling, own=True))
        first += [copy(a, 1 + j, me, (*chip, c), own=True) for j, chip in enumerate(chips)]
    return copy, mine, first, me, sibling, chips, c, n


def _ag_start(x_refs, out_refs, sems):
    _, mine, first, *_ = _ag_parts(x_refs, out_refs, sems)
    for cp in mine + first:
        cp.start()


def _ag_finish(x_refs, out_refs, sems):
    copy, mine, first, me, sibling, chips, c, n = _ag_parts(x_refs, out_refs, sems)
    passed = []
    for j, chip in enumerate(chips):
        for a in range(n):
            copy(a, 1 + j, (*chip, c), me).wait_recv()
            passed.append(copy(a, 4 + j, (*chip, c), sibling))
            passed[-1].start()
    for a in range(n):
        copy(a, 0, sibling, me).wait_recv()
    for j, chip in enumerate(chips):
        for a in range(n):
            copy(a, 4 + j, (*chip, 1 - c), me).wait_recv()
    for cp in first + passed:
        cp.wait_send()
    for cp in mine:
        cp.wait()


N_PARTS = 4


def _part_spec(rows, cols, t_pos, lead_block=(), lead_index=lambda *args: ()):
    if rows % (16 * N_PARTS) == 0:
        axis, shape, count = 0, (rows // N_PARTS, cols), N_PARTS
    elif cols % (128 * N_PARTS) == 0:
        axis, shape, count = 1, (rows, cols // N_PARTS), N_PARTS
    else:
        axis, shape, count = 0, (rows, cols), 1

    def index(*args):
        i = jnp.minimum(args[t_pos], count - 1)
        return (*lead_index(*args), *((i, 0) if axis == 0 else (0, i)))

    return pl.BlockSpec((*lead_block, *shape), index)


def _chip_partials(name, gs, sibs, place):
    n = len(gs)

    def body(place_ref, *refs):
        for g_ref, sib_ref, out_ref in zip(refs[:n], refs[n:2 * n], refs[2 * n:]):
            out_ref[...] = (g_ref[...] + sib_ref[...]).astype(MM)

    own = [_part_spec(*g.shape[1:], 1, (1,), lambda k, t, place_ref: (2 * place_ref[1 + k] + place_ref[0],))
           for g in gs]
    theirs = [_part_spec(*g.shape[1:], 1, (1,), lambda k, t, place_ref: (place_ref[1 + k],)) for g in gs]
    out = [_part_spec(*g.shape[1:], 1, (1,), lambda k, t, place_ref: (k,)) for g in gs]
    grid_spec = pltpu.PrefetchScalarGridSpec(
        num_scalar_prefetch=1, grid=(3, N_PARTS), in_specs=own + theirs, out_specs=out)
    return pl.pallas_call(
        body, name=name, grid_spec=grid_spec, out_shape=[jax.ShapeDtypeStruct((3,) + g.shape[1:], MM) for g in gs],
        compiler_params=_cparams(("arbitrary", "arbitrary"), VMEM_LIMIT),
    )(place, *gs, *sibs)


def _x_count(xchg):
    return len(xchg[0]) + len(xchg[1])


def _x_out_shapes(xchg):
    return ([jax.ShapeDtypeStruct((4,) + g.shape[1:], g.dtype) for g in xchg[0]]
            + [jax.ShapeDtypeStruct((3,) + p.shape[1:], p.dtype) for p in xchg[1]])


def _x_sems(xchg):
    n = 4 * len(xchg[0]) + 3 * len(xchg[1])
    return [pltpu.SemaphoreType.DMA((n,)), pltpu.SemaphoreType.DMA((n,))] if n else []


def _x_copies(n_sib, in_refs, out_refs, sems):
    if not in_refs:
        return []
    send_sems, recv_sems = sems
    x, y, c = _my_place()
    chips = [(1 - x, y), (x, 1 - y), (1 - x, 1 - y)]
    copies = []

    def add(src, dst, to):
        k = len(copies)
        copies.append(pltpu.make_async_remote_copy(
            src_ref=src, dst_ref=dst, send_sem=send_sems.at[k], recv_sem=recv_sems.at[k], device_id=to,
            device_id_type=MESH_ID))

    for a, (src, dst) in enumerate(zip(in_refs, out_refs)):
        if a < n_sib:
            for j in range(4):
                add(src.at[2 * j + 1 - c], dst.at[j], (x, y, 1 - c))
        else:
            for k, (px, py) in enumerate(chips):
                add(src.at[k], dst.at[k], (px, py, c))
    return copies


def _reduce_chips(name, gs, gather):
    n, n_ag = len(gs), len(gather)

    def body(*refs):
        g_refs, ag_in, refs = refs[:n], refs[n:n + n_ag], refs[n + n_ag:]
        sib_refs, got_refs, ag_out, refs = refs[:n], refs[n:2 * n], refs[2 * n:2 * n + n_ag], refs[2 * n + n_ag:]
        own_s, oth_s, part_s, refs = refs[:n], refs[n:2 * n], refs[2 * n:3 * n], refs[3 * n:]
        sib_send, sib_recv, chip_send, chip_recv, load_sems = refs[:5]
        x, y, c = _my_place()
        chips = [(1 - x, y), (x, 1 - y), (1 - x, 1 - y), (x, y)]
        slot = lambda k: 2 * chips[k][0] + chips[k][1]

        def to_sibling(a, k):
            return pltpu.make_async_remote_copy(
                src_ref=g_refs[a].at[2 * slot(k) + 1 - c], dst_ref=sib_refs[a].at[slot(k)],
                send_sem=sib_send.at[4 * a + k], recv_sem=sib_recv.at[4 * a + k], device_id=(x, y, 1 - c),
                device_id_type=MESH_ID)

        def to_chip(a, k, p):
            rows = pl.ds(*pieces[a][p])
            return pltpu.make_async_remote_copy(
                src_ref=part_s[a].at[k, rows], dst_ref=got_refs[a].at[k, rows],
                send_sem=chip_send.at[2 * (3 * a + k) + p], recv_sem=chip_recv.at[2 * (3 * a + k) + p],
                device_id=(*chips[k], c), device_id_type=MESH_ID)

        def load_own(a, k):
            return pltpu.make_async_copy(g_refs[a].at[2 * slot(k) + c], own_s[a].at[k], load_sems.at[6 * a + k])

        def load_other(a, k):
            return pltpu.make_async_copy(sib_refs[a].at[slot(k)], oth_s[a].at[k], load_sems.at[6 * a + 3 + k])

        if n_ag:
            _ag_start(ag_in, ag_out, refs[5:])
        for k in range(4):
            for a in range(n):
                to_sibling(a, k).start()
        for k in range(3):
            for a in range(n):
                load_own(a, k).start()
        for k in range(3):
            for a in range(n):
                to_sibling(a, k).wait_recv()
                load_other(a, k).start()
            for a in range(n):
                load_own(a, k).wait()
                load_other(a, k).wait()
                for p, (first, count) in enumerate(pieces[a]):
                    rows = pl.ds(first, count)
                    part_s[a][k, rows] = (own_s[a][k, rows] + oth_s[a][k, rows]).astype(MM)
                    to_chip(a, k, p).start()
        if n_ag:
            _ag_finish(ag_in, ag_out, refs[5:])
        for a in range(n):
            to_sibling(a, 3).wait_recv()
        for k in range(3):
            for a in range(n):
                for p in range(2):
                    to_chip(a, k, p).wait()
        for k in range(4):
            for a in range(n):
                to_sibling(a, k).wait_send()

    any_spec = pl.BlockSpec(memory_space=pl.ANY)
    shapes = [g.shape[1:] for g in gs]
    pieces = [((0, s[0] // 32 * 16), (s[0] // 32 * 16, s[0] - s[0] // 32 * 16)) for s in shapes]
    dma_sems = lambda count: pltpu.SemaphoreType.DMA((count,))
    return pl.pallas_call(
        body, name=name,
        out_shape=([jax.ShapeDtypeStruct((4,) + s, F32) for s in shapes]
                   + [jax.ShapeDtypeStruct((3,) + s, MM) for s in shapes] + _ag_out_shapes(gather)),
        in_specs=[any_spec] * (n + n_ag), out_specs=[any_spec] * (2 * n + n_ag),
        scratch_shapes=([pltpu.VMEM((3,) + s, F32) for s in shapes] * 2 + [pltpu.VMEM((3,) + s, MM) for s in shapes]
                        + [dma_sems(4 * n), dma_sems(4 * n), dma_sems(6 * n), dma_sems(6 * n), dma_sems(6 * n)]
                        + (_ag_sems(n_ag) if n_ag else [])),
        compiler_params=_cparams((), VMEM_LIMIT),
    )(*gs, *gather)


def _adamw_math(w, g, m, v):
    m = ADAM_B1 * m + (1.0 - ADAM_B1) * g
    v = ADAM_B2 * v + (1.0 - ADAM_B2) * jnp.square(g)
    m_hat = m / (1.0 - ADAM_B1 ** ADAM_STEP)
    v_hat = v / (1.0 - ADAM_B2 ** ADAM_STEP)
    delta = -ADAM_LR * (m_hat / (jnp.sqrt(v_hat) + ADAM_EPS) + ADAM_WD * w)
    return delta, m, v


def _sum_adamws(name, gs, sibs, gots, ws, ms, vs, slot_idx, chip_idx):
    n = len(gs)

    def body(s_ref, j_ref, *refs):
        ins, outs = refs[:6 * n], refs[6 * n:]
        for a in range(n):
            g_ref, sib_ref, got_ref, w_ref, m_ref, v_ref = (ins[k * n + a] for k in range(6))
            go_ref, d_ref, m2_ref, v2_ref = outs[4 * a:4 * a + 4]
            grad = g_ref[0] + sib_ref[0]
            for k in range(3):
                grad = grad + got_ref[k].astype(F32)
            go_ref[...] = grad
            d_ref[...], m2_ref[...], v2_ref[...] = _adamw_math(w_ref[...], grad, m_ref[...], v_ref[...])

    shapes = [g.shape[1:] for g in gs]
    flat = [_part_spec(*s, 0) for s in shapes]
    in_specs = ([_part_spec(*s, 0, (1,), lambda t, s_ref, j_ref: (s_ref[0],)) for s in shapes]
                + [_part_spec(*s, 0, (1,), lambda t, s_ref, j_ref: (j_ref[0],)) for s in shapes]
                + [_part_spec(*s, 0, (3,), lambda t, s_ref, j_ref: (0,)) for s in shapes] + flat * 3)
    grid_spec = pltpu.PrefetchScalarGridSpec(
        num_scalar_prefetch=2, grid=(N_PARTS,), in_specs=in_specs, out_specs=[f for f in flat for _ in range(4)])
    res = pl.pallas_call(
        body, name=name, grid_spec=grid_spec,
        out_shape=[jax.ShapeDtypeStruct(s, F32) for s in shapes for _ in range(4)],
        compiler_params=_cparams(("arbitrary",), VMEM_LIMIT),
    )(slot_idx, chip_idx, *gs, *sibs, *gots, *ws, *ms, *vs)
    return [res[4 * a:4 * a + 4] for a in range(n)]


BIG = ("w_in", "w_uq", "w_ukv", "w_branch", "w_out", "w_ffn_gate", "w_ffn_up", "w_ffn_down", "w_ple_gate", "w_ple_proj")
SMALL = (
    ("mix_norm_g", 0, 1, 1024), ("q_a_norm_g", 1, 1, 384), ("kv_a_norm_g", 2, 1, 256), ("q_norm_g", 3, 1, 96),
    ("k_norm_g", 4, 1, 96), ("hg_lb_logits", 5, 2, 512), ("hg_out_norm_g", 7, 1, 128), ("ffn_norm_g", 8, 1, 1024),
    ("ple_gate_norm_g", 9, 1, 1024), ("ple_post_norm_g", 10, 1, 1024),
)
SLAB_ROWS, LOSS_ROW = 16, 15


def _pack_partials(small_g, loss_p):
    def body(*refs):
        val_refs, loss_ref, out_ref = refs[:len(SMALL)], refs[len(SMALL)], refs[len(SMALL) + 1]
        out_ref[...] = jnp.zeros_like(out_ref)
        for (_, r0, rows, cols), ref in zip(SMALL, val_refs):
            val = ref[...]
            if val.shape[0] != rows:
                val = jnp.sum(val, axis=0, keepdims=True)
            out_ref[r0:r0 + rows, :cols] = val[:, :cols]
        out_ref[LOSS_ROW:LOSS_ROW + 1, :HEAD_PAD] = jnp.full((1, HEAD_PAD), jnp.sum(loss_ref[...]), F32)

    return pl.pallas_call(
        body, name="pack_partials", out_shape=jax.ShapeDtypeStruct((SLAB_ROWS, D_MODEL), F32),
    )(*[small_g[n] for n, *_ in SMALL], loss_p)


def _adamw_small(parts, ws, ms, vs):
    n = len(SMALL)

    def body(p_ref, *refs):
        ins, loss_ref, outs = refs[:3 * n], refs[3 * n], refs[3 * n + 1:]
        total = p_ref[0]
        for d in range(1, N_DEV):
            total = total + p_ref[d]
        loss_ref[...] = total[LOSS_ROW:LOSS_ROW + 1, 0:1]
        for a, (_, r0, rows, cols) in enumerate(SMALL):
            g = total[r0:r0 + rows, :cols]
            outs[4 * a][...] = g
            outs[4 * a + 1][...], outs[4 * a + 2][...], outs[4 * a + 3][...] = _adamw_math(
                ins[a][...], g, ins[n + a][...], ins[2 * n + a][...])

    shapes = [jax.ShapeDtypeStruct((rows, cols), F32) for _, _, rows, cols in SMALL]
    res = pl.pallas_call(
        body, name="adamw_small", out_shape=[jax.ShapeDtypeStruct((1, 1), F32)] + [s for s in shapes for _ in range(4)],
    )(parts, *ws, *ms, *vs)
    return res[0], [res[1 + 4 * a:5 + 4 * a] for a in range(n)]


_WEIGHTS = ["mix_norm_g", "w_in", "q_a_norm_g", "w_uq", "kv_a_norm_g", "w_ukv", "q_norm_g", "k_norm_g", "hg_lb_logits",
            "hg_out_norm_g", "w_branch", "w_out", "ffn_norm_g", "w_ffn_gate", "w_ffn_up", "w_ffn_down",
            "ple_gate_norm_g", "w_ple_gate", "w_ple_proj", "ple_post_norm_g"]


def _step(x, p, positions, tgt, w, m, v):
    small_names = [n for n, *_ in SMALL]
    T = x.shape[1]
    px, py, pc = _my_place()
    as_idx = lambda t: jnp.reshape(t, (1,)).astype(jnp.int32)

    def two_d(n, t):
        t = t.reshape(-1, t.shape[-1])
        return t.T if n in TRANSPOSED else t

    def full_shape(n, t):
        return (t.T if n in TRANSPOSED else t).reshape(w[n].shape)

    blocks = {n: two_d(n, w[n]).astype(MM) for n in BIG}
    big = dict(zip(EARLY, _all_gather("ag_weights", [blocks[n] for n in EARLY])))
    small = {n: (w[n] if n == "hg_lb_logits" else w[n].reshape(1, -1)) for n in small_names}

    loss_p, grad_x, small_g, grads, sibs, gots = _local_step(
        x[0], p[0, 0], positions.reshape(T, 1), tgt[0], small, big, late_blocks=blocks,
        core=jnp.stack([pc, 2 * (1 - px) + py, 2 * px + 1 - py, 2 * (1 - px) + 1 - py]).astype(jnp.int32))

    *res_c, slabs = _reduce_chips("rs_chips", [grads[n] for n in GROUP_C], [_pack_partials(small_g, loss_p)])
    sibs.update(zip(GROUP_C, res_c[:len(GROUP_C)]))
    gots.update(zip(GROUP_C, res_c[len(GROUP_C):]))
    out_g, out_d, out_m, out_v = {}, {}, {}, {}
    for tag, names in (("ab", GROUP_A + GROUP_B), ("c", GROUP_C)):
        pick = lambda table: [table[n] for n in names]
        res = _sum_adamws("adamw_" + tag, pick(grads), pick(sibs), pick(gots), [two_d(n, w[n]) for n in names],
                          [two_d(n, m[n]) for n in names], [two_d(n, v[n]) for n in names],
                          as_idx(4 * px + 2 * py + pc), as_idx(2 * px + py))
        for n, r in zip(names, res):
            out_g[n], out_d[n], out_m[n], out_v[n] = [full_shape(n, t) for t in r]

    loss, res = _adamw_small(slabs, *([t[n] for n in small_names] for t in (w, m, v)))
    for n, r in zip(small_names, res):
        out_g[n], out_d[n], out_m[n], out_v[n] = r

    outs = [loss.reshape(()), grad_x[None]]
    for table in (out_g, out_d, out_m, out_v):
        outs += [table[n] for n in _WEIGHTS]
    return tuple(outs)


def kernel(x, p, positions, mix_norm_g, w_in, q_a_norm_g, w_uq, kv_a_norm_g, w_ukv, q_norm_g, k_norm_g, hg_lb_logits, hg_out_norm_g, w_branch, w_out, ffn_norm_g, w_ffn_gate, w_ffn_up, w_ffn_down, ple_gate_norm_g, w_ple_gate, w_ple_proj, ple_post_norm_g, loss_target, m_mix_norm_g, m_w_in, m_q_a_norm_g, m_w_uq, m_kv_a_norm_g, m_w_ukv, m_q_norm_g, m_k_norm_g, m_hg_lb_logits, m_hg_out_norm_g, m_w_branch, m_w_out, m_ffn_norm_g, m_w_ffn_gate, m_w_ffn_up, m_w_ffn_down, m_ple_gate_norm_g, m_w_ple_gate, m_w_ple_proj, m_ple_post_norm_g, v_mix_norm_g, v_w_in, v_q_a_norm_g, v_w_uq, v_kv_a_norm_g, v_w_ukv, v_q_norm_g, v_k_norm_g, v_hg_lb_logits, v_hg_out_norm_g, v_w_branch, v_w_out, v_ffn_norm_g, v_w_ffn_gate, v_w_ffn_up, v_w_ffn_down, v_ple_gate_norm_g, v_w_ple_gate, v_w_ple_proj, v_ple_post_norm_g):
    w = dict(mix_norm_g=mix_norm_g, w_in=w_in, q_a_norm_g=q_a_norm_g, w_uq=w_uq, kv_a_norm_g=kv_a_norm_g, w_ukv=w_ukv,
             q_norm_g=q_norm_g, k_norm_g=k_norm_g, hg_lb_logits=hg_lb_logits, hg_out_norm_g=hg_out_norm_g,
             w_branch=w_branch, w_out=w_out, ffn_norm_g=ffn_norm_g, w_ffn_gate=w_ffn_gate, w_ffn_up=w_ffn_up,
             w_ffn_down=w_ffn_down, ple_gate_norm_g=ple_gate_norm_g, w_ple_gate=w_ple_gate, w_ple_proj=w_ple_proj,
             ple_post_norm_g=ple_post_norm_g)
    m = dict(mix_norm_g=m_mix_norm_g, w_in=m_w_in, q_a_norm_g=m_q_a_norm_g, w_uq=m_w_uq, kv_a_norm_g=m_kv_a_norm_g,
             w_ukv=m_w_ukv, q_norm_g=m_q_norm_g, k_norm_g=m_k_norm_g, hg_lb_logits=m_hg_lb_logits,
             hg_out_norm_g=m_hg_out_norm_g, w_branch=m_w_branch, w_out=m_w_out, ffn_norm_g=m_ffn_norm_g,
             w_ffn_gate=m_w_ffn_gate, w_ffn_up=m_w_ffn_up, w_ffn_down=m_w_ffn_down,
             ple_gate_norm_g=m_ple_gate_norm_g, w_ple_gate=m_w_ple_gate, w_ple_proj=m_w_ple_proj,
             ple_post_norm_g=m_ple_post_norm_g)
    v = dict(mix_norm_g=v_mix_norm_g, w_in=v_w_in, q_a_norm_g=v_q_a_norm_g, w_uq=v_w_uq, kv_a_norm_g=v_kv_a_norm_g,
             w_ukv=v_w_ukv, q_norm_g=v_q_norm_g, k_norm_g=v_k_norm_g, hg_lb_logits=v_hg_lb_logits,
             hg_out_norm_g=v_hg_out_norm_g, w_branch=v_w_branch, w_out=v_w_out, ffn_norm_g=v_ffn_norm_g,
             w_ffn_gate=v_w_ffn_gate, w_ffn_up=v_w_ffn_up, w_ffn_down=v_w_ffn_down,
             ple_gate_norm_g=v_ple_gate_norm_g, w_ple_gate=v_w_ple_gate, w_ple_proj=v_w_ple_proj,
             ple_post_norm_g=v_ple_post_norm_g)
    return _step(x, p, positions, loss_target, w, m, v)
```
